```python
import math
import jax, jax.numpy as jnp
from jax import lax
import numpy as np

D_MODEL = 1024
BATCH = 8
SEQ = 4096
DEPTH = 2

MLA_HEADS = 8
MLA_Q_LORA = 256
MLA_KV_LORA = 256
MLA_NOPE = 64
MLA_ROPE = 32
MLA_V = 64
MLA_SCALE = (MLA_NOPE + MLA_ROPE) ** -0.5
ROPE_BASE = 10000.0
Q_BLOCK = 128
MAX_POS_OFFSET = 1024

SGU_GROUPS = 4
SGU_GROUP_DIM = 128
SGU_DIM = SGU_GROUPS * SGU_GROUP_DIM
SGU_CHUNK = 128

EVEN_IN = MLA_Q_LORA + MLA_KV_LORA + MLA_ROPE + 2 * SGU_DIM
EVEN_SPLITS = (MLA_Q_LORA,
               MLA_Q_LORA + MLA_KV_LORA,
               MLA_Q_LORA + MLA_KV_LORA + MLA_ROPE,
               MLA_Q_LORA + MLA_KV_LORA + MLA_ROPE + SGU_DIM)
EVEN_MIX = MLA_HEADS * MLA_V + SGU_DIM

HG_HEADS = 8
HG_DK = 128
HG_DV = D_MODEL // HG_HEADS
HG_KEY_DIM = HG_HEADS * HG_DK
HG_VAL_DIM = HG_HEADS * HG_DV
HG_CHUNK = 64
ODD_IN = 2 * HG_KEY_DIM + 2 * HG_VAL_DIM
ODD_SPLITS = (HG_KEY_DIM, 2 * HG_KEY_DIM, 2 * HG_KEY_DIM + HG_VAL_DIM)

D_FF = 4 * D_MODEL

N_EVEN = (DEPTH + 1) // 2
N_ODD = DEPTH // 2
DN_ALPHA = (2 * DEPTH) ** 0.25
DN_BETA = (8 * DEPTH) ** -0.25
NORM_EPS = 1e-5

kernel_name = 'hybrid_mla_sgu_hgrn2_deepnorm'


def layer_norm(x, g, b):
    xf = x.astype(jnp.float32)
    mu = jnp.mean(xf, -1, keepdims=True)
    var = jnp.mean(jnp.square(xf - mu), -1, keepdims=True)
    y = (xf - mu) * lax.rsqrt(var + NORM_EPS)
    return (y * g.astype(jnp.float32) + b.astype(jnp.float32)).astype(x.dtype)


def rms_norm(x, g):
    xf = x.astype(jnp.float32)
    y = xf * lax.rsqrt(jnp.mean(jnp.square(xf), -1, keepdims=True) + NORM_EPS)
    return (y * g.astype(jnp.float32)).astype(x.dtype)


def apply_rope(x, cos, sin):
    half = MLA_ROPE // 2
    xf = x.astype(jnp.float32)
    x1, x2 = xf[..., :half], xf[..., half:]
    out = jnp.concatenate([x1 * cos - x2 * sin, x2 * cos + x1 * sin], axis=-1)
    return out.astype(x.dtype)


def mla(c_q, c_kv, k_rope, positions, g_q, g_kv, w_qb, w_kvb):
    B, S, _ = c_q.shape
    q = (rms_norm(c_q, g_q) @ w_qb).reshape(B, S, MLA_HEADS, MLA_NOPE + MLA_ROPE)
    q_nope, q_rope = q[..., :MLA_NOPE], q[..., MLA_NOPE:]
    kv = (rms_norm(c_kv, g_kv) @ w_kvb).reshape(B, S, MLA_HEADS, MLA_NOPE + MLA_V)
    k_nope, v = kv[..., :MLA_NOPE], kv[..., MLA_NOPE:]
    half = MLA_ROPE // 2
    inv_freq = ROPE_BASE ** (-jnp.arange(half, dtype=jnp.float32) / half)
    ang = positions.astype(jnp.float32)[..., None] * inv_freq
    cos, sin = jnp.cos(ang), jnp.sin(ang)
    q_rope = apply_rope(q_rope, cos[:, :, None, :], sin[:, :, None, :])
    k_rope = apply_rope(k_rope, cos, sin)
    nb = S // Q_BLOCK
    qn_b = q_nope.reshape(B, nb, Q_BLOCK, MLA_HEADS, MLA_NOPE).transpose(1, 0, 2, 3, 4)
    qr_b = q_rope.reshape(B, nb, Q_BLOCK, MLA_HEADS, MLA_ROPE).transpose(1, 0, 2, 3, 4)
    key_idx = jnp.arange(S)

    def block(args):
        qn, qr, bi = args
        s = (jnp.einsum('bqhd,bkhd->bhqk', qn, k_nope)
             + jnp.einsum('bqhr,bkr->bhqk', qr, k_rope)).astype(jnp.float32) * MLA_SCALE
        q_idx = bi * Q_BLOCK + jnp.arange(Q_BLOCK)
        mask = key_idx[None, :] <= q_idx[:, None]
        s = jnp.where(mask[None, None], s, -jnp.inf)
        p = jax.nn.softmax(s, axis=-1).astype(v.dtype)
        return jnp.einsum('bhqk,bkhd->bqhd', p, v)

    out = lax.map(block, (qn_b, qr_b, jnp.arange(nb)))
    return out.transpose(1, 0, 2, 3, 4).reshape(B, S, MLA_HEADS * MLA_V)


def sgu(u, v, ln_g, ln_b, w_s, b_s):
    B, S, _ = u.shape
    u = jax.nn.gelu(u)
    v = layer_norm(jax.nn.gelu(v), ln_g, ln_b)
    nc = S // SGU_CHUNK
    vc = v.reshape(B, nc, SGU_CHUNK, SGU_GROUPS, SGU_GROUP_DIM)
    causal = jnp.tril(jnp.ones((SGU_CHUNK, SGU_CHUNK), dtype=bool))
    w = jnp.where(causal[None], w_s, jnp.zeros_like(w_s))
    mixed = jnp.einsum('gts,bnsgc->bntgc', w, vc) + b_s.T[:, :, None]
    return u * mixed.reshape(B, S, SGU_DIM)


def hgrn2(q, f, i, g, lb, g_norm):
    B, S, _ = q.shape
    nc = S // HG_CHUNK
    qf = jax.nn.silu(q.astype(jnp.float32))
    gate = lb + (1.0 - lb) * jax.nn.sigmoid(f.astype(jnp.float32))
    k = 1.0 - gate
    log_g = jnp.log(gate)
    vf = i.astype(jnp.float32)

    def chunks(t, d):
        return t.reshape(B, nc, HG_CHUNK, HG_HEADS, d).transpose(1, 0, 3, 2, 4)

    xs = (chunks(qf, HG_DK), chunks(k, HG_DK), chunks(vf, HG_DV), chunks(log_g, HG_DK))
    tri = jnp.tril(jnp.ones((HG_CHUNK, HG_CHUNK), dtype=bool))[:, :, None]

    def step(state, inp):
        qc, kc, vc, lg = inp
        bcum = jnp.cumsum(lg, axis=2)
        diff = bcum[:, :, :, None, :] - bcum[:, :, None, :, :]
        decay = jnp.exp(jnp.where(tri, diff, -jnp.inf))
        attn = jnp.einsum('bhtd,bhsd,bhtsd->bhts', qc, kc, decay)
        o = (jnp.einsum('bhts,bhsv->bhtv', attn, vc)
             + jnp.einsum('bhtd,bhdv->bhtv', qc * jnp.exp(bcum), state))
        b_last = bcum[:, :, -1:, :]
        k_dec = kc * jnp.exp(b_last - bcum)
        new_state = (jnp.exp(b_last[:, :, 0, :])[..., None] * state
                     + jnp.einsum('bhsd,bhsv->bhdv', k_dec, vc))
        return new_state, o

    state0 = jnp.zeros((B, HG_HEADS, HG_DK, HG_DV), jnp.float32)
    _, o = lax.scan(step, state0, xs)
    o = o.transpose(1, 0, 3, 2, 4).reshape(B, S, HG_HEADS, HG_DV)
    o = o * lax.rsqrt(jnp.mean(jnp.square(o), -1, keepdims=True) + NORM_EPS)
    o = o * g_norm.astype(jnp.float32).reshape(HG_HEADS, HG_DV)
    o = o * jax.nn.silu(g.astype(jnp.float32).reshape(B, S, HG_HEADS, HG_DV))
    return o.reshape(B, S, HG_VAL_DIM).astype(i.dtype)


def _fwd_setup_inputs(seed: int = 0) -> dict:
    key = jax.random.key(seed)
    ks = jax.random.split(key, 24)

    def nrm(k, shape, scale):
        return jax.random.normal(k, shape, jnp.float32) * scale

    def gain(k, shape):
        return 1.0 + 0.01 * jax.random.normal(k, shape, jnp.float32)

    x = jax.random.normal(ks[0], (BATCH, SEQ, D_MODEL), jnp.float32)
    offs = jax.random.randint(ks[1], (BATCH, 1), 0, MAX_POS_OFFSET, dtype=jnp.int32)
    positions = (offs + jnp.arange(SEQ, dtype=jnp.int32)[None, :]).astype(jnp.int32)
    return {
        'x': x,
        'positions': positions,
        'w_in_e': nrm(ks[2], (N_EVEN, D_MODEL, EVEN_IN), D_MODEL ** -0.5),
        'mla_gq': gain(ks[3], (N_EVEN, MLA_Q_LORA)),
        'mla_gkv': gain(ks[4], (N_EVEN, MLA_KV_LORA)),
        'w_qb': nrm(ks[5], (N_EVEN, MLA_Q_LORA, MLA_HEADS * (MLA_NOPE + MLA_ROPE)), MLA_Q_LORA ** -0.5),
        'w_kvb': nrm(ks[6], (N_EVEN, MLA_KV_LORA, MLA_HEADS * (MLA_NOPE + MLA_V)), MLA_KV_LORA ** -0.5),
        'sgu_ln_g': gain(ks[7], (N_EVEN, SGU_DIM)),
        'sgu_ln_b': nrm(ks[8], (N_EVEN, SGU_DIM), 0.01),
        'sgu_w': nrm(ks[9], (N_EVEN, SGU_GROUPS, SGU_CHUNK, SGU_CHUNK), SGU_CHUNK ** -0.5),
        'sgu_b': gain(ks[10], (N_EVEN, SGU_GROUPS, SGU_CHUNK)),
        'w_out_e': nrm(ks[11], (N_EVEN, EVEN_MIX, D_MODEL), DN_BETA * EVEN_MIX ** -0.5),
        'w_in_o': nrm(ks[12], (N_ODD, D_MODEL, ODD_IN), D_MODEL ** -0.5),
        'hg_lb': nrm(ks[13], (DEPTH, HG_KEY_DIM), 0.1),
        'hg_gnorm': gain(ks[14], (N_ODD, HG_VAL_DIM)),
        'w_out_o': nrm(ks[15], (N_ODD, HG_VAL_DIM, D_MODEL), DN_BETA * HG_VAL_DIM ** -0.5),
        'ln1_g': gain(ks[16], (DEPTH, D_MODEL)),
        'ln1_b': nrm(ks[17], (DEPTH, D_MODEL), 0.01),
        'w_ff1': nrm(ks[18], (DEPTH, D_MODEL, D_FF), DN_BETA * D_MODEL ** -0.5),
        'w_ff2': nrm(ks[19], (DEPTH, D_FF, D_MODEL), DN_BETA * D_FF ** -0.5),
        'ln2_g': gain(ks[20], (DEPTH, D_MODEL)),
        'ln2_b': nrm(ks[21], (DEPTH, D_MODEL), 0.01),
    }


def _fwd_reference(x, positions, w_in_e, mla_gq, mla_gkv, w_qb, w_kvb, sgu_ln_g, sgu_ln_b,
              sgu_w, sgu_b, w_out_e, w_in_o, hg_lb, hg_gnorm, w_out_o,
              ln1_g, ln1_b, w_ff1, w_ff2, ln2_g, ln2_b):
    lb_sm = jax.nn.softmax(hg_lb.astype(jnp.float32), axis=0)
    lb_all = jnp.cumsum(lb_sm, axis=0) - lb_sm[0:1]
    h = x
    for l in range(DEPTH):
        if l % 2 == 0:
            e = l // 2
            z = h @ w_in_e[e]
            c_q, c_kv, k_r, u, v = jnp.split(z, EVEN_SPLITS, axis=-1)
            a_out = mla(c_q, c_kv, k_r, positions, mla_gq[e], mla_gkv[e], w_qb[e], w_kvb[e])
            b_out = sgu(u, v, sgu_ln_g[e], sgu_ln_b[e], sgu_w[e], sgu_b[e])
            mix = jnp.concatenate([a_out, b_out], axis=-1) @ w_out_e[e]
        else:
            o = l // 2
            z = h @ w_in_o[o]
            q, f, i, g = jnp.split(z, ODD_SPLITS, axis=-1)
            mix = hgrn2(q, f, i, g, lb_all[l], hg_gnorm[o]) @ w_out_o[o]
        h = layer_norm(DN_ALPHA * h + mix, ln1_g[l], ln1_b[l])
        ff = jnp.square(jax.nn.relu(h @ w_ff1[l])) @ w_ff2[l]
        h = layer_norm(DN_ALPHA * h + ff, ln2_g[l], ln2_b[l])
    return h


import jax as _jax
import jax.numpy as _jnp

TWIN_FORMAT = 'train_step'
FWD_PARAMS = ['x', 'positions', 'w_in_e', 'mla_gq', 'mla_gkv', 'w_qb', 'w_kvb', 'sgu_ln_g', 'sgu_ln_b', 'sgu_w', 'sgu_b', 'w_out_e', 'w_in_o', 'hg_lb', 'hg_gnorm', 'w_out_o', 'ln1_g', 'ln1_b', 'w_ff1', 'w_ff2', 'ln2_g', 'ln2_b']
TWIN_WEIGHTS = ['w_in_e', 'mla_gq', 'mla_gkv', 'w_qb', 'w_kvb', 'sgu_ln_g', 'sgu_ln_b', 'sgu_w', 'sgu_b', 'w_out_e', 'w_in_o', 'hg_lb', 'hg_gnorm', 'w_out_o', 'ln1_g', 'ln1_b', 'w_ff1', 'w_ff2', 'ln2_g', 'ln2_b']
TWIN_DIFF_INPUT = 'x'
TWIN_INPUTS = ['x', 'positions', 'w_in_e', 'mla_gq', 'mla_gkv', 'w_qb', 'w_kvb', 'sgu_ln_g', 'sgu_ln_b', 'sgu_w', 'sgu_b', 'w_out_e', 'w_in_o', 'hg_lb', 'hg_gnorm', 'w_out_o', 'ln1_g', 'ln1_b', 'w_ff1', 'w_ff2', 'ln2_g', 'ln2_b', 'loss_target', 'm_w_in_e', 'm_mla_gq', 'm_mla_gkv', 'm_w_qb', 'm_w_kvb', 'm_sgu_ln_g', 'm_sgu_ln_b', 'm_sgu_w', 'm_sgu_b', 'm_w_out_e', 'm_w_in_o', 'm_hg_lb', 'm_hg_gnorm', 'm_w_out_o', 'm_ln1_g', 'm_ln1_b', 'm_w_ff1', 'm_w_ff2', 'm_ln2_g', 'm_ln2_b', 'v_w_in_e', 'v_mla_gq', 'v_mla_gkv', 'v_w_qb', 'v_w_kvb', 'v_sgu_ln_g', 'v_sgu_ln_b', 'v_sgu_w', 'v_sgu_b', 'v_w_out_e', 'v_w_in_o', 'v_hg_lb', 'v_hg_gnorm', 'v_w_out_o', 'v_ln1_g', 'v_ln1_b', 'v_w_ff1', 'v_w_ff2', 'v_ln2_g', 'v_ln2_b']
TWIN_OUTPUTS = ['loss', 'grad_x', 'grad_w_in_e', 'grad_mla_gq', 'grad_mla_gkv', 'grad_w_qb', 'grad_w_kvb', 'grad_sgu_ln_g', 'grad_sgu_ln_b', 'grad_sgu_w', 'grad_sgu_b', 'grad_w_out_e', 'grad_w_in_o', 'grad_hg_lb', 'grad_hg_gnorm', 'grad_w_out_o', 'grad_ln1_g', 'grad_ln1_b', 'grad_w_ff1', 'grad_w_ff2', 'grad_ln2_g', 'grad_ln2_b', 'delta_w_in_e', 'delta_mla_gq', 'delta_mla_gkv', 'delta_w_qb', 'delta_w_kvb', 'delta_sgu_ln_g', 'delta_sgu_ln_b', 'delta_sgu_w', 'delta_sgu_b', 'delta_w_out_e', 'delta_w_in_o', 'delta_hg_lb', 'delta_hg_gnorm', 'delta_w_out_o', 'delta_ln1_g', 'delta_ln1_b', 'delta_w_ff1', 'delta_w_ff2', 'delta_ln2_g', 'delta_ln2_b', 'new_m_w_in_e', 'new_m_mla_gq', 'new_m_mla_gkv', 'new_m_w_qb', 'new_m_w_kvb', 'new_m_sgu_ln_g', 'new_m_sgu_ln_b', 'new_m_sgu_w', 'new_m_sgu_b', 'new_m_w_out_e', 'new_m_w_in_o', 'new_m_hg_lb', 'new_m_hg_gnorm', 'new_m_w_out_o', 'new_m_ln1_g', 'new_m_ln1_b', 'new_m_w_ff1', 'new_m_w_ff2', 'new_m_ln2_g', 'new_m_ln2_b', 'new_v_w_in_e', 'new_v_mla_gq', 'new_v_mla_gkv', 'new_v_w_qb', 'new_v_w_kvb', 'new_v_sgu_ln_g', 'new_v_sgu_ln_b', 'new_v_sgu_w', 'new_v_sgu_b', 'new_v_w_out_e', 'new_v_w_in_o', 'new_v_hg_lb', 'new_v_hg_gnorm', 'new_v_w_out_o', 'new_v_ln1_g', 'new_v_ln1_b', 'new_v_w_ff1', 'new_v_w_ff2', 'new_v_ln2_g', 'new_v_ln2_b']
TWIN_LEAF_KINDS = {'loss': 'loss', 'grad_x': 'grad_x', 'grad_w_in_e': 'grad_w', 'grad_mla_gq': 'grad_w', 'grad_mla_gkv': 'grad_w', 'grad_w_qb': 'grad_w', 'grad_w_kvb': 'grad_w', 'grad_sgu_ln_g': 'grad_w', 'grad_sgu_ln_b': 'grad_w', 'grad_sgu_w': 'grad_w', 'grad_sgu_b': 'grad_w', 'grad_w_out_e': 'grad_w', 'grad_w_in_o': 'grad_w', 'grad_hg_lb': 'grad_w', 'grad_hg_gnorm': 'grad_w', 'grad_w_out_o': 'grad_w', 'grad_ln1_g': 'grad_w', 'grad_ln1_b': 'grad_w', 'grad_w_ff1': 'grad_w', 'grad_w_ff2': 'grad_w', 'grad_ln2_g': 'grad_w', 'grad_ln2_b': 'grad_w', 'delta_w_in_e': 'delta_w', 'delta_mla_gq': 'delta_w', 'delta_mla_gkv': 'delta_w', 'delta_w_qb': 'delta_w', 'delta_w_kvb': 'delta_w', 'delta_sgu_ln_g': 'delta_w', 'delta_sgu_ln_b': 'delta_w', 'delta_sgu_w': 'delta_w', 'delta_sgu_b': 'delta_w', 'delta_w_out_e': 'delta_w', 'delta_w_in_o': 'delta_w', 'delta_hg_lb': 'delta_w', 'delta_hg_gnorm': 'delta_w', 'delta_w_out_o': 'delta_w', 'delta_ln1_g': 'delta_w', 'delta_ln1_b': 'delta_w', 'delta_w_ff1': 'delta_w', 'delta_w_ff2': 'delta_w', 'delta_ln2_g': 'delta_w', 'delta_ln2_b': 'delta_w', 'new_m_w_in_e': 'new_m', 'new_m_mla_gq': 'new_m', 'new_m_mla_gkv': 'new_m', 'new_m_w_qb': 'new_m', 'new_m_w_kvb': 'new_m', 'new_m_sgu_ln_g': 'new_m', 'new_m_sgu_ln_b': 'new_m', 'new_m_sgu_w': 'new_m', 'new_m_sgu_b': 'new_m', 'new_m_w_out_e': 'new_m', 'new_m_w_in_o': 'new_m', 'new_m_hg_lb': 'new_m', 'new_m_hg_gnorm': 'new_m', 'new_m_w_out_o': 'new_m', 'new_m_ln1_g': 'new_m', 'new_m_ln1_b': 'new_m', 'new_m_w_ff1': 'new_m', 'new_m_w_ff2': 'new_m', 'new_m_ln2_g': 'new_m', 'new_m_ln2_b': 'new_m', 'new_v_w_in_e': 'new_v', 'new_v_mla_gq': 'new_v', 'new_v_mla_gkv': 'new_v', 'new_v_w_qb': 'new_v', 'new_v_w_kvb': 'new_v', 'new_v_sgu_ln_g': 'new_v', 'new_v_sgu_ln_b': 'new_v', 'new_v_sgu_w': 'new_v', 'new_v_sgu_b': 'new_v', 'new_v_w_out_e': 'new_v', 'new_v_w_in_o': 'new_v', 'new_v_hg_lb': 'new_v', 'new_v_hg_gnorm': 'new_v', 'new_v_w_out_o': 'new_v', 'new_v_ln1_g': 'new_v', 'new_v_ln1_b': 'new_v', 'new_v_w_ff1': 'new_v', 'new_v_w_ff2': 'new_v', 'new_v_ln2_g': 'new_v', 'new_v_ln2_b': 'new_v'}


def _forward(args):
    return _fwd_reference(*[args[k] for k in FWD_PARAMS])


def _output_shape():
    out = _jax.eval_shape(lambda: _forward(_fwd_setup_inputs(0)))
    return out.shape, out.dtype

N_MICROBATCH = 1
ADAM_LR = 0.001
ADAM_B1 = 0.9
ADAM_B2 = 0.999
ADAM_EPS = 1e-08
ADAM_WD = 0.01
ADAM_STEP = 10
PER_EXAMPLE_BATCH_AXIS = {'x': 0, 'positions': 0, 'loss_target': 0}
SHARED_INPUTS = []
_WEIGHT_DTYPES = {'w_in_e': _jnp.float32, 'mla_gq': _jnp.float32, 'mla_gkv': _jnp.float32, 'w_qb': _jnp.float32, 'w_kvb': _jnp.float32, 'sgu_ln_g': _jnp.float32, 'sgu_ln_b': _jnp.float32, 'sgu_w': _jnp.float32, 'sgu_b': _jnp.float32, 'w_out_e': _jnp.float32, 'w_in_o': _jnp.float32, 'hg_lb': _jnp.float32, 'hg_gnorm': _jnp.float32, 'w_out_o': _jnp.float32, 'ln1_g': _jnp.float32, 'ln1_b': _jnp.float32, 'w_ff1': _jnp.float32, 'w_ff2': _jnp.float32, 'ln2_g': _jnp.float32, 'ln2_b': _jnp.float32}
MOMENT_SCALE = {'w_in_e': 3.732855e-02, 'mla_gq': 1.653564e-02, 'mla_gkv': 2.665319e-02, 'w_qb': 9.224245e-03, 'w_kvb': 1.227680e-02, 'sgu_ln_g': 3.331855e-02, 'sgu_ln_b': 2.922950e-02, 'sgu_w': 2.917732e-02, 'sgu_b': 4.021184e-02, 'w_out_e': 7.495156e-02, 'w_in_o': 2.666164e-02, 'hg_lb': 3.351487e-03, 'hg_gnorm': 3.780272e-02, 'w_out_o': 7.344965e-02, 'ln1_g': 4.420005e-01, 'ln1_b': 2.607922e-01, 'w_ff1': 2.224955e-02, 'w_ff2': 4.163974e-02, 'ln2_g': 2.261426e+01, 'ln2_b': 1.934939e+00}


def _to_microbatches(a, axis):
    t = _jnp.moveaxis(a, axis, 0)
    t = t.reshape((N_MICROBATCH, t.shape[0] // N_MICROBATCH) + t.shape[1:])
    return _jnp.moveaxis(t, 1, axis + 1)


def setup_inputs(seed: int = 0) -> dict:
    inp = _fwd_setup_inputs(seed)
    key = _jax.random.fold_in(_jax.random.key(seed), 7919)
    shape, _ = _output_shape()
    out = dict(inp)
    out["loss_target"] = _jax.random.normal(_jax.random.fold_in(key, 0), shape, _jnp.float32)
    for i, name in enumerate(TWIN_WEIGHTS):
        w = inp[name].astype(_jnp.float32)
        if MOMENT_SCALE is None:
            s = _jnp.sqrt(_jnp.mean(_jnp.square(w)) + 1e-30)
        else:
            s = MOMENT_SCALE[name]
        km, kv = _jax.random.split(_jax.random.fold_in(key, i + 1))
        out[name] = w
        out["m_" + name] = s * _jax.random.normal(km, w.shape, _jnp.float32)
        out["v_" + name] = (s * s) * _jax.random.uniform(kv, w.shape, _jnp.float32, 0.5, 1.5)
    if N_MICROBATCH > 1:
        for name, axis in PER_EXAMPLE_BATCH_AXIS.items():
            out[name] = _to_microbatches(out[name], axis)
    return {'x': out['x'], 'positions': out['positions'], 'w_in_e': out['w_in_e'], 'mla_gq': out['mla_gq'], 'mla_gkv': out['mla_gkv'], 'w_qb': out['w_qb'], 'w_kvb': out['w_kvb'], 'sgu_ln_g': out['sgu_ln_g'], 'sgu_ln_b': out['sgu_ln_b'], 'sgu_w': out['sgu_w'], 'sgu_b': out['sgu_b'], 'w_out_e': out['w_out_e'], 'w_in_o': out['w_in_o'], 'hg_lb': out['hg_lb'], 'hg_gnorm': out['hg_gnorm'], 'w_out_o': out['w_out_o'], 'ln1_g': out['ln1_g'], 'ln1_b': out['ln1_b'], 'w_ff1': out['w_ff1'], 'w_ff2': out['w_ff2'], 'ln2_g': out['ln2_g'], 'ln2_b': out['ln2_b'], 'loss_target': out['loss_target'], 'm_w_in_e': out['m_w_in_e'], 'm_mla_gq': out['m_mla_gq'], 'm_mla_gkv': out['m_mla_gkv'], 'm_w_qb': out['m_w_qb'], 'm_w_kvb': out['m_w_kvb'], 'm_sgu_ln_g': out['m_sgu_ln_g'], 'm_sgu_ln_b': out['m_sgu_ln_b'], 'm_sgu_w': out['m_sgu_w'], 'm_sgu_b': out['m_sgu_b'], 'm_w_out_e': out['m_w_out_e'], 'm_w_in_o': out['m_w_in_o'], 'm_hg_lb': out['m_hg_lb'], 'm_hg_gnorm': out['m_hg_gnorm'], 'm_w_out_o': out['m_w_out_o'], 'm_ln1_g': out['m_ln1_g'], 'm_ln1_b': out['m_ln1_b'], 'm_w_ff1': out['m_w_ff1'], 'm_w_ff2': out['m_w_ff2'], 'm_ln2_g': out['m_ln2_g'], 'm_ln2_b': out['m_ln2_b'], 'v_w_in_e': out['v_w_in_e'], 'v_mla_gq': out['v_mla_gq'], 'v_mla_gkv': out['v_mla_gkv'], 'v_w_qb': out['v_w_qb'], 'v_w_kvb': out['v_w_kvb'], 'v_sgu_ln_g': out['v_sgu_ln_g'], 'v_sgu_ln_b': out['v_sgu_ln_b'], 'v_sgu_w': out['v_sgu_w'], 'v_sgu_b': out['v_sgu_b'], 'v_w_out_e': out['v_w_out_e'], 'v_w_in_o': out['v_w_in_o'], 'v_hg_lb': out['v_hg_lb'], 'v_hg_gnorm': out['v_hg_gnorm'], 'v_w_out_o': out['v_w_out_o'], 'v_ln1_g': out['v_ln1_g'], 'v_ln1_b': out['v_ln1_b'], 'v_w_ff1': out['v_w_ff1'], 'v_w_ff2': out['v_w_ff2'], 'v_ln2_g': out['v_ln2_g'], 'v_ln2_b': out['v_ln2_b']}


def _loss(weights, diff, rest, loss_target):
    with _jax.named_scope("forward"):
        args = {**rest, TWIN_DIFF_INPUT: diff, **{k: w.astype(_WEIGHT_DTYPES[k]) for k, w in weights.items()}}
        y = _forward(args)
    with _jax.named_scope("loss_head"):
        err = _jnp.square(y.astype(_jnp.float32) - loss_target)
        return 0.5 * _jnp.sum(_jnp.mean(err, axis=-1)) if err.ndim else 0.5 * err


def _adamw(w, g, m, v):
    m = ADAM_B1 * m + (1.0 - ADAM_B1) * g
    v = ADAM_B2 * v + (1.0 - ADAM_B2) * _jnp.square(g)
    m_hat = m / (1.0 - ADAM_B1 ** ADAM_STEP)
    v_hat = v / (1.0 - ADAM_B2 ** ADAM_STEP)
    delta = -ADAM_LR * (m_hat / (_jnp.sqrt(v_hat) + ADAM_EPS) + ADAM_WD * w)
    return delta, m, v


def reference(x, positions, w_in_e, mla_gq, mla_gkv, w_qb, w_kvb, sgu_ln_g, sgu_ln_b, sgu_w, sgu_b, w_out_e, w_in_o, hg_lb, hg_gnorm, w_out_o, ln1_g, ln1_b, w_ff1, w_ff2, ln2_g, ln2_b, loss_target, m_w_in_e, m_mla_gq, m_mla_gkv, m_w_qb, m_w_kvb, m_sgu_ln_g, m_sgu_ln_b, m_sgu_w, m_sgu_b, m_w_out_e, m_w_in_o, m_hg_lb, m_hg_gnorm, m_w_out_o, m_ln1_g, m_ln1_b, m_w_ff1, m_w_ff2, m_ln2_g, m_ln2_b, v_w_in_e, v_mla_gq, v_mla_gkv, v_w_qb, v_w_kvb, v_sgu_ln_g, v_sgu_ln_b, v_sgu_w, v_sgu_b, v_w_out_e, v_w_in_o, v_hg_lb, v_hg_gnorm, v_w_out_o, v_ln1_g, v_ln1_b, v_w_ff1, v_w_ff2, v_ln2_g, v_ln2_b):
    given = dict(x=x, positions=positions, w_in_e=w_in_e, mla_gq=mla_gq, mla_gkv=mla_gkv, w_qb=w_qb, w_kvb=w_kvb, sgu_ln_g=sgu_ln_g, sgu_ln_b=sgu_ln_b, sgu_w=sgu_w, sgu_b=sgu_b, w_out_e=w_out_e, w_in_o=w_in_o, hg_lb=hg_lb, hg_gnorm=hg_gnorm, w_out_o=w_out_o, ln1_g=ln1_g, ln1_b=ln1_b, w_ff1=w_ff1, w_ff2=w_ff2, ln2_g=ln2_g, ln2_b=ln2_b, loss_target=loss_target, m_w_in_e=m_w_in_e, m_mla_gq=m_mla_gq, m_mla_gkv=m_mla_gkv, m_w_qb=m_w_qb, m_w_kvb=m_w_kvb, m_sgu_ln_g=m_sgu_ln_g, m_sgu_ln_b=m_sgu_ln_b, m_sgu_w=m_sgu_w, m_sgu_b=m_sgu_b, m_w_out_e=m_w_out_e, m_w_in_o=m_w_in_o, m_hg_lb=m_hg_lb, m_hg_gnorm=m_hg_gnorm, m_w_out_o=m_w_out_o, m_ln1_g=m_ln1_g, m_ln1_b=m_ln1_b, m_w_ff1=m_w_ff1, m_w_ff2=m_w_ff2, m_ln2_g=m_ln2_g, m_ln2_b=m_ln2_b, v_w_in_e=v_w_in_e, v_mla_gq=v_mla_gq, v_mla_gkv=v_mla_gkv, v_w_qb=v_w_qb, v_w_kvb=v_w_kvb, v_sgu_ln_g=v_sgu_ln_g, v_sgu_ln_b=v_sgu_ln_b, v_sgu_w=v_sgu_w, v_sgu_b=v_sgu_b, v_w_out_e=v_w_out_e, v_w_in_o=v_w_in_o, v_hg_lb=v_hg_lb, v_hg_gnorm=v_hg_gnorm, v_w_out_o=v_w_out_o, v_ln1_g=v_ln1_g, v_ln1_b=v_ln1_b, v_w_ff1=v_w_ff1, v_w_ff2=v_w_ff2, v_ln2_g=v_ln2_g, v_ln2_b=v_ln2_b)
    weights = {n: given[n] for n in TWIN_WEIGHTS}
    shared = {n: given[n] for n in SHARED_INPUTS}
    per_example = {n: given[n] for n in ['x', 'positions']}
    grad_fn = _jax.value_and_grad(_loss, argnums=(0, 1))

    def one_microbatch(ex, loss_target):
        ex = dict(ex)
        diff = ex.pop(TWIN_DIFF_INPUT)
        return grad_fn(weights, diff, {**shared, **ex}, loss_target)

    if N_MICROBATCH == 1:
        loss, (grad_w, grad_x) = one_microbatch(per_example, given["loss_target"])
    else:
        def body(carry, xs):
            loss_sum, grad_sum = carry
            l_k, (gw_k, gx_k) = one_microbatch(xs[0], xs[1])
            with _jax.named_scope("update"):
                return (loss_sum + l_k, _jax.tree.map(_jnp.add, grad_sum, gw_k)), gx_k

        init = (_jnp.zeros((), _jnp.float32), _jax.tree.map(_jnp.zeros_like, weights))
        (loss, grad_w), grad_x = _jax.lax.scan(body, init, (per_example, given["loss_target"]))
    with _jax.named_scope("update"):
        delta_w, new_m, new_v = {}, {}, {}
        for n in TWIN_WEIGHTS:
            delta_w[n], new_m[n], new_v[n] = _adamw(weights[n], grad_w[n], given["m_" + n], given["v_" + n])
    return (loss, grad_x, *[grad_w[n] for n in TWIN_WEIGHTS], *[delta_w[n] for n in TWIN_WEIGHTS],
            *[new_m[n] for n in TWIN_WEIGHTS], *[new_v[n] for n in TWIN_WEIGHTS])
```

```python
import functools
import math

import jax
import jax.numpy as jnp
import numpy as np
from jax import lax
from jax.experimental import pallas as pl
from jax.experimental.pallas import tpu as pltpu

F32 = jnp.float32
BF16 = jnp.bfloat16
MESH = pl.DeviceIdType.MESH
HIGHEST = lax.Precision.HIGHEST

D_MODEL = 1024
D_FF = 4096
N_DEV = 8
HEADS = 8
HEAD_W = 128
MLA_NOPE = 64
MLA_ROPE = 32
MLA_V = 64
MLA_LORA = 256
MLA_SCALE = (MLA_NOPE + MLA_ROPE) ** -0.5
ROPE_BASE = 10000.0
SGU_DIM = 512
SGU_G = 4
SGU_CHUNK = 128
HG_CHUNK = 64
ALPHA = (2 * 2) ** 0.25
EPS = 1e-5
ADAM_LR, ADAM_B1, ADAM_B2, ADAM_EPS, ADAM_WD, ADAM_STEP = 0.001, 0.9, 0.999, 1e-08, 0.01, 10

LANES = 1024
VMEM_CAP_V7X = 56 * 2**20
VMEM_SLACK = 12 * 2**20

SEGS = (("w_in_e", (1024, 196)), ("w_qb", (256, 96)), ("w_kvb", (256, 128)), ("w_out_e", (128, 1024)),
        ("w_in_o", (1024, 512)), ("w_out_o", (128, 1024)), ("w_ff1", (2, 1024, 512)), ("w_ff2", (2, 512, 1024)))
PACK_ROWS = 3072
SMALL = (("mla_gq", (1, 256)), ("mla_gkv", (1, 256)), ("sgu_ln_g", (1, 512)), ("sgu_ln_b", (1, 512)), ("sgu_b", (1, 4, 128)),
         ("sgu_w", (1, 4, 128, 128)), ("hg_lb", (2, 1024)), ("ln1_g", (2, 1024)), ("ln1_b", (2, 1024)),
         ("ln2_g", (2, 1024)), ("ln2_b", (2, 1024)), ("hg_gnorm", (1, 1024)))
SMALL_ROWS = 80


def _vmem(block_bytes):
    return int(min(VMEM_CAP_V7X, 2 * block_bytes + VMEM_SLACK))


def _nbytes(shape, dtype):
    return int(np.prod(shape)) * jnp.dtype(dtype).itemsize


def _sig(x):
    return 1.0 / (1.0 + jnp.exp(-x))


def _gelu(x):
    c = math.sqrt(2.0 / math.pi)
    t = jnp.tanh(c * (x + 0.044715 * x * x * x))
    return 0.5 * x * (1.0 + t), t


def _gelu_grad(x, t):
    c = math.sqrt(2.0 / math.pi)
    return 0.5 * (1.0 + t) + 0.5 * x * (1.0 - t * t) * c * (1.0 + 3 * 0.044715 * x * x)


def _dot(a, b, dims, precision=None):
    return lax.dot_general(a, b, (dims, ((), ())), preferred_element_type=F32, precision=precision)


NN = ((1,), (0,))
NT = ((1,), (1,))
TN = ((0,), (0,))


def _mm(name, pairs, mode, outs, tm=512, tn=512, epilogue=None, extras=()):
    a0, b0 = pairs[0]
    M = a0.shape[1] if mode == "tn" else a0.shape[0]
    N = b0.shape[0] if mode == "nt" else b0.shape[1]
    tm, tn = min(tm, M), min(tn, N)
    dims = {"nn": NN, "nt": NT, "tn": TN}[mode]
    n_pairs, n_ext, n_out = len(pairs), len(extras), len(outs)

    def kern(*refs):
        acc = None
        for k in range(n_pairs):
            a = refs[2 * k][...].astype(BF16)
            b = refs[2 * k + 1][...].astype(BF16)
            d = _dot(a, b, dims)
            acc = d if acc is None else acc + d
        ext = [r[...] for r in refs[2 * n_pairs:2 * n_pairs + n_ext]]
        res = epilogue(acc, *ext) if epilogue is not None else (acc,)
        for o_ref, r in zip(refs[2 * n_pairs + n_ext:], res):
            o_ref[...] = r.astype(o_ref.dtype)

    in_specs, args, nbytes = [], [], 0
    for a, b in pairs:
        if mode == "tn":
            a_blk, a_map = (a.shape[0], tm), (lambda j, i: (0, i))
        else:
            a_blk, a_map = (tm, a.shape[1]), (lambda j, i: (i, 0))
        if mode == "nt":
            b_blk, b_map = (tn, b.shape[1]), (lambda j, i: (j, 0))
        else:
            b_blk, b_map = (b.shape[0], tn), (lambda j, i: (0, j))
        in_specs += [pl.BlockSpec(a_blk, a_map), pl.BlockSpec(b_blk, b_map)]
        args += [a, b]
        nbytes += _nbytes(a_blk, a.dtype) + _nbytes(b_blk, b.dtype)
    for arr, width, imap in extras:
        blk = (tm, width)
        in_specs.append(pl.BlockSpec(blk, functools.partial(lambda j, i, f: f(i, j), f=imap)))
        args.append(arr)
        nbytes += _nbytes(blk, arr.dtype)
    out_specs = [pl.BlockSpec((tm, w), lambda j, i: (i, j)) for w, _ in outs]
    out_shape = [jax.ShapeDtypeStruct((M, N // tn * w), dt) for w, dt in outs]
    nbytes += sum(_nbytes((tm, w), dt) for w, dt in outs) + 2 * tm * tn * 4
    res = pl.pallas_call(
        kern, name=name, grid=(N // tn, M // tm), in_specs=in_specs, out_specs=out_specs, out_shape=out_shape,
        compiler_params=pltpu.CompilerParams(dimension_semantics=("parallel", "parallel"), vmem_limit_bytes=_vmem(nbytes)),
    )(*args)
    return res if n_out > 1 else res[0]


def _rowwise(name, body, rows, consts, out_rows, out_accs=(), tr=512):
    T = rows[0][0].shape[0]
    tr = min(tr, T)
    nr, ncn, no = len(rows), len(consts), len(out_rows)

    def kern(*refs):
        accs = refs[nr + ncn + no:]
        if accs:
            @pl.when(pl.program_id(0) == 0)
            def _():
                for a in accs:
                    a[...] = jnp.zeros(a.shape, a.dtype)
        body(refs[:nr], refs[nr:nr + ncn], refs[nr + ncn:nr + ncn + no], accs)

    in_specs = [pl.BlockSpec((tr, w), functools.partial(lambda i, cb: (i, cb), cb=cb)) for _, w, cb in rows]
    in_specs += [pl.BlockSpec(c.shape, functools.partial(lambda i, nd: (0,) * nd, nd=c.ndim)) for c in consts]
    out_specs = [pl.BlockSpec((tr, w), lambda i: (i, 0)) for w, _ in out_rows]
    out_specs += [pl.BlockSpec(s, functools.partial(lambda i, nd: (0,) * nd, nd=len(s))) for s, _ in out_accs]
    out_shape = [jax.ShapeDtypeStruct((T, w), dt) for w, dt in out_rows]
    out_shape += [jax.ShapeDtypeStruct(s, dt) for s, dt in out_accs]
    nbytes = sum(_nbytes((tr, w), a.dtype) for a, w, _ in rows) + sum(_nbytes(c.shape, c.dtype) for c in consts)
    nbytes += sum(_nbytes((tr, w), dt) for w, dt in out_rows) + sum(_nbytes(s, dt) for s, dt in out_accs)
    res = pl.pallas_call(
        kern, name=name, grid=(T // tr,), in_specs=in_specs, out_specs=out_specs, out_shape=out_shape,
        compiler_params=pltpu.CompilerParams(dimension_semantics=("arbitrary",), vmem_limit_bytes=_vmem(nbytes)),
    )(*[a for a, _, _ in rows], *consts)
    return res if len(res) > 1 else res[0]


def _full(a):
    return (a, a.shape[1], 0)


def _ln_stats(y):
    mu = jnp.mean(y, axis=-1, keepdims=True)
    yc = y - mu
    r = lax.rsqrt(jnp.mean(yc * yc, axis=-1, keepdims=True) + EPS)
    return yc * r, r


def _ln_fwd(name, h_in, mix, g, b):
    def body(rows, consts, outs, accs):
        y = ALPHA * rows[0][...] + rows[1][...]
        xh, _ = _ln_stats(y)
        h = xh * consts[0][...] + consts[1][...]
        outs[0][...] = y
        outs[1][...] = h
        outs[2][...] = h.astype(BF16)

    return _rowwise(name, body, [_full(h_in), _full(mix)], [g, b], [(D_MODEL, F32), (D_MODEL, F32), (D_MODEL, BF16)], tr=256)


def _ln_loss(name, h_in, mix, g, b, target):
    def body(rows, consts, outs, accs):
        y = ALPHA * rows[0][...] + rows[1][...]
        xh, _ = _ln_stats(y)
        err = xh * consts[0][...] + consts[1][...] - rows[2][...]
        outs[0][...] = y
        outs[1][...] = err * (1.0 / D_MODEL)
        accs[0][...] += jnp.sum(err * err, axis=0, keepdims=True)

    return _rowwise(name, body, [_full(h_in), _full(mix), _full(target)], [g, b], [(D_MODEL, F32), (D_MODEL, F32)],
                    [((1, D_MODEL), F32)], tr=256)


def _ln_bwd(name, y, dh, g):
    def body(rows, consts, outs, accs):
        xh, r = _ln_stats(rows[0][...])
        d = rows[1][...]
        accs[0][...] += jnp.sum(d * xh, axis=0, keepdims=True)
        accs[1][...] += jnp.sum(d, axis=0, keepdims=True)
        dx = d * consts[0][...]
        dy = r * (dx - jnp.mean(dx, axis=-1, keepdims=True) - xh * jnp.mean(dx * xh, axis=-1, keepdims=True))
        outs[0][...] = dy
        outs[1][...] = dy.astype(BF16)

    return _rowwise(name, body, [_full(y), _full(dh)], [g], [(D_MODEL, F32), (D_MODEL, BF16)],
                    [((1, D_MODEL), F32), ((1, D_MODEL), F32)], tr=256)


def _relu2_epilogue(acc):
    a = jnp.maximum(acc, 0.0)
    return acc, a * a


def _mlp_fwd(tag, h_bf, w1, w2):
    a, act = _mm(f"{tag}_ff1", [(h_bf, w1)], "nn", [(512, BF16), (512, BF16)], epilogue=_relu2_epilogue)
    ff = _mm(f"{tag}_ff2", [(act, w2)], "nn", [(512, F32)])
    return a, act, ff


def _mlp_bwd(tag, h_bf, a, act, dff_bf, dy, w1, w2):
    da = _mm(f"{tag}_dact", [(dff_bf, w2)], "nt", [(512, BF16)],
             extras=[(a, 512, lambda i, j: (i, j))],
             epilogue=lambda acc, a_t: (acc * 2.0 * jnp.maximum(a_t.astype(F32), 0.0),))
    dw2 = _mm(f"{tag}_dw2", [(act, dff_bf)], "tn", [(512, F32)])
    dw1 = _mm(f"{tag}_dw1", [(h_bf, da)], "tn", [(512, F32)])
    dh, dh_bf = _mm(f"{tag}_dh", [(da, w1)], "nt", [(512, F32), (512, BF16)],
                    extras=[(dy, 512, lambda i, j: (i, j))],
                    epilogue=lambda acc, dy_t: (acc + ALPHA * dy_t,) * 2)
    return dh, dh_bf, dw1, dw2


def _rope_tables(positions_col, invf_lane):
    def body(rows, consts, outs, accs):
        ang = rows[0][...].astype(F32) * consts[0][...]
        c, s = jnp.cos(ang), jnp.sin(ang)
        lane = lax.broadcasted_iota(jnp.int32, ang.shape, 1)
        outs[0][...] = jnp.where(lane < 64, 1.0, jnp.where(lane < 96, c, 0.0))
        outs[1][...] = jnp.where((lane >= 64) & (lane < 80), -s, 0.0)
        outs[2][...] = jnp.where((lane >= 80) & (lane < 96), s, 0.0)

    return _rowwise("rope_tables", body, [_full(positions_col)], [invf_lane], [(HEAD_W, F32)] * 3)


def _rope(x, c, s1, s2):
    return x * c + pltpu.roll(x, 112, 1) * s1 + pltpu.roll(x, 16, 1) * s2


def _rope_t(dx, c, s1, s2):
    return dx * c + pltpu.roll(dx * s1, 16, 1) + pltpu.roll(dx * s2, 112, 1)


def _rms(c, g):
    r = lax.rsqrt(jnp.mean(c * c, axis=-1, keepdims=True) + EPS)
    return c * r, r


def _mla_pre(zm, tabs, gq, gkv):
    def body(rows, consts, outs, accs):
        cq, _ = _rms(rows[0][...], None)
        ckv, _ = _rms(rows[1][...], None)
        outs[0][...] = (cq * consts[0][...]).astype(BF16)
        outs[1][...] = (ckv * consts[1][...]).astype(BF16)
        outs[2][...] = _rope(rows[2][...], rows[3][...], rows[4][...], rows[5][...])

    rows = [(zm, 256, 0), (zm, 256, 1), (zm, 128, 4)] + [_full(t) for t in tabs]
    return _rowwise("mla_pre", body, rows, [gq, gkv], [(256, BF16), (256, BF16), (HEAD_W, F32)])


def _mla_pre_bwd(zm, tabs, gq, gkv, dcqn, dckvn, dk):
    def body(rows, consts, outs, accs):
        res = []
        for k in range(2):
            ch, r = _rms(rows[k][...], None)
            d = rows[6 + k][...]
            accs[k][...] += jnp.sum(d * ch, axis=0, keepdims=True)
            dc = d * consts[k][...]
            res.append(r * (dc - ch * jnp.mean(dc * ch, axis=-1, keepdims=True)))
        dks = rows[8][:, 0:HEAD_W]
        for h in range(1, HEADS):
            dks = dks + rows[8][:, h * HEAD_W:(h + 1) * HEAD_W]
        lane = lax.broadcasted_iota(jnp.int32, dks.shape, 1)
        dks = jnp.where((lane >= 64) & (lane < 96), dks, 0.0)
        dkr = _rope_t(dks, rows[3][...], rows[4][...], rows[5][...])
        outs[0][:, 0:256] = res[0].astype(BF16)
        outs[0][:, 256:512] = res[1].astype(BF16)
        outs[0][:, 512:640] = dkr.astype(BF16)

    rows = [(zm, 256, 0), (zm, 256, 1), (zm, 128, 4)] + [_full(t) for t in tabs] + [_full(dcqn), _full(dckvn), _full(dk)]
    return _rowwise("mla_pre_bwd", body, rows, [gq, gkv], [(640, BF16)], [((1, 256), F32), ((1, 256), F32)])


def _rope_heads(x, c, s1, s2, fn):
    return jnp.concatenate([fn(x[:, h * HEAD_W:(h + 1) * HEAD_W], c, s1, s2) for h in range(HEADS)], axis=1)


def _unrope_heads(dq, tabs):
    def body(rows, consts, outs, accs):
        outs[0][...] = _rope_heads(rows[0][...], rows[1][...], rows[2][...], rows[3][...], _rope_t).astype(BF16)

    return _rowwise("l0_dq_rope", body, [_full(dq)] + [_full(t) for t in tabs], [], [(HEADS * HEAD_W, BF16)])


def _attn_block(T):
    return min(256, T)


def _attn_fwd(q, k, v):
    T = q.shape[0]
    BQ = _attn_block(T)
    nq = T // BQ

    def kern(q_ref, k_ref, v_ref, o_ref, lse_ref):
        def step(i, j, carry, masked):
            m, l, acc = carry
            qb = q_ref[pl.ds(pl.multiple_of(i * BQ, BQ), BQ), :]
            kb = k_ref[pl.ds(pl.multiple_of(j * BQ, BQ), BQ), :]
            vb = v_ref[pl.ds(pl.multiple_of(j * BQ, BQ), BQ), :]
            s = _dot(qb, kb, NT) * MLA_SCALE
            if masked:
                row = lax.broadcasted_iota(jnp.int32, s.shape, 0)
                col = lax.broadcasted_iota(jnp.int32, s.shape, 1)
                s = jnp.where(col <= row, s, -1e30)
            m_new = jnp.maximum(m, jnp.max(s, axis=-1, keepdims=True))
            p = jnp.exp(s - m_new)
            a = jnp.exp(m - m_new)
            l = a * l + jnp.sum(p, axis=-1, keepdims=True)
            acc = a * acc + _dot(p.astype(BF16), vb, NN)
            return m_new, l, acc

        def qloop(i, _):
            init = (jnp.full((BQ, 1), -1e30, F32), jnp.zeros((BQ, 1), F32), jnp.zeros((BQ, HEAD_W), F32))
            carry = lax.fori_loop(0, i, lambda j, c: step(i, j, c, False), init)
            m, l, acc = step(i, i, carry, True)
            rows = pl.ds(pl.multiple_of(i * BQ, BQ), BQ)
            o_ref[rows, :] = acc / l
            lse_ref[0, rows, :] = m + jnp.log(l)
            return 0

        lax.fori_loop(0, nq, qloop, 0)

    head = pl.BlockSpec((T, HEAD_W), lambda h: (0, h))
    nbytes = 3 * _nbytes((T, HEAD_W), BF16) + _nbytes((T, HEAD_W), F32) + _nbytes((T, 128), F32)
    return pl.pallas_call(
        kern, name="attn_fwd", grid=(HEADS,), in_specs=[head, head, head],
        out_specs=[head, pl.BlockSpec((1, T, 1), lambda h: (h, 0, 0))],
        out_shape=[jax.ShapeDtypeStruct((T, HEADS * HEAD_W), F32), jax.ShapeDtypeStruct((HEADS, T, 1), F32)],
        compiler_params=pltpu.CompilerParams(dimension_semantics=("parallel",), vmem_limit_bytes=_vmem(nbytes)),
    )(q, k, v)


def _attn_bwd(q, k, v, o, lse, dcat):
    T = q.shape[0]
    BQ = _attn_block(T)
    nq = T // BQ

    def kern(q_ref, k_ref, v_ref, o_ref, lse_ref, do_ref, dq_ref, dk_ref, dv_ref, dd_ref):
        dq_ref[...] = jnp.zeros(dq_ref.shape, F32)

        def dloop(i, _):
            rows = pl.ds(pl.multiple_of(i * BQ, BQ), BQ)
            dd_ref[rows, :] = jnp.sum(do_ref[rows, :].astype(F32) * o_ref[rows, :], axis=-1, keepdims=True)
            return 0

        lax.fori_loop(0, nq, dloop, 0)

        def step(j, i, carry, masked):
            dk_acc, dv_acc = carry
            rq = pl.ds(pl.multiple_of(i * BQ, BQ), BQ)
            rk = pl.ds(pl.multiple_of(j * BQ, BQ), BQ)
            qb, kb, vb, dob = q_ref[rq, :], k_ref[rk, :], v_ref[rk, :], do_ref[rq, :]
            s = _dot(qb, kb, NT) * MLA_SCALE
            p = jnp.exp(s - lse_ref[0, rq, :])
            if masked:
                row = lax.broadcasted_iota(jnp.int32, s.shape, 0)
                col = lax.broadcasted_iota(jnp.int32, s.shape, 1)
                p = jnp.where(col <= row, p, 0.0)
            dp = _dot(dob, vb, NT)
            ds = (p * (dp - dd_ref[rq, :]) * MLA_SCALE).astype(BF16)
            dv_acc = dv_acc + _dot(p.astype(BF16), dob, TN)
            dk_acc = dk_acc + _dot(ds, qb, TN)
            dq_ref[rq, :] += _dot(ds, kb, NN)
            return dk_acc, dv_acc

        def kloop(j, _):
            init = (jnp.zeros((BQ, HEAD_W), F32), jnp.zeros((BQ, HEAD_W), F32))
            carry = step(j, j, init, True)
            dk_acc, dv_acc = lax.fori_loop(j + 1, nq, lambda i, c: step(j, i, c, False), carry)
            rk = pl.ds(pl.multiple_of(j * BQ, BQ), BQ)
            dk_ref[rk, :] = dk_acc
            dv_ref[rk, :] = dv_acc
            return 0

        lax.fori_loop(0, nq, kloop, 0)

    head = pl.BlockSpec((T, HEAD_W), lambda h: (0, h))
    nbytes = 4 * _nbytes((T, HEAD_W), BF16) + 5 * _nbytes((T, HEAD_W), F32) + 2 * _nbytes((T, 128), F32)
    return pl.pallas_call(
        kern, name="attn_bwd", grid=(HEADS,),
        in_specs=[head, head, head, head, pl.BlockSpec((1, T, 1), lambda h: (h, 0, 0)), head],
        out_specs=[head, head, head],
        out_shape=[jax.ShapeDtypeStruct((T, HEADS * HEAD_W), F32)] * 3,
        scratch_shapes=[pltpu.VMEM((T, 1), F32)],
        compiler_params=pltpu.CompilerParams(dimension_semantics=("parallel",), vmem_limit_bytes=_vmem(nbytes)),
    )(q, k, v, o, lse, dcat)


def _sgu_common(u, v, ln_g, ln_b):
    ua, tu = _gelu(u)
    va, tv = _gelu(v)
    vh, r = _ln_stats(va)
    return ua, tu, tv, vh, r, vh * ln_g + ln_b


def _tril_mask(n):
    return lax.broadcasted_iota(jnp.int32, (n, n), 1) <= lax.broadcasted_iota(jnp.int32, (n, n), 0)


def _sgu_fwd(zs, ln_g, ln_b, w, bias_full):
    def body(rows, consts, outs, accs):
        ua, _, _, _, _, vn = _sgu_common(rows[0][...], rows[1][...], consts[0][...], consts[1][...])
        vn = vn.astype(BF16)
        tri = _tril_mask(SGU_CHUNK)
        for g in range(SGU_G):
            wg = jnp.where(tri, consts[2][g], 0.0).astype(BF16)
            cols = slice(g * 128, (g + 1) * 128)
            for c in range(ua.shape[0] // SGU_CHUNK):
                rws = slice(c * SGU_CHUNK, (c + 1) * SGU_CHUNK)
                mixed = _dot(wg, vn[rws, cols], NN) + consts[3][:, cols]
                outs[0][rws, cols] = (ua[rws, cols] * mixed).astype(BF16)

    return _rowwise("sgu_fwd", body, [(zs, 512, 0), (zs, 512, 1)], [ln_g, ln_b, w, bias_full], [(SGU_DIM, BF16)])


def _sgu_bwd(zs, dcat, ln_g, ln_b, w, bias_full):
    def body(rows, consts, outs, accs):
        u, v = rows[0][...], rows[1][...]
        ua, tu, tv, vh, r, vn = _sgu_common(u, v, consts[0][...], consts[1][...])
        dout = rows[2][...].astype(F32)
        vn_bf = vn.astype(BF16)
        tri = _tril_mask(SGU_CHUNK)
        dmixed = (dout * ua)
        dmixed_bf = dmixed.astype(BF16)
        dvn_cols, mixed_cols = [], []
        for g in range(SGU_G):
            wg = jnp.where(tri, consts[2][g], 0.0).astype(BF16)
            cols = slice(g * 128, (g + 1) * 128)
            dvn_rows, mixed_rows = [], []
            dw = jnp.zeros((SGU_CHUNK, SGU_CHUNK), F32)
            dbs = jnp.zeros((SGU_CHUNK, 1), F32)
            for c in range(u.shape[0] // SGU_CHUNK):
                rws = slice(c * SGU_CHUNK, (c + 1) * SGU_CHUNK)
                mixed_rows.append(_dot(wg, vn_bf[rws, cols], NN) + consts[3][:, cols])
                dvn_rows.append(_dot(wg, dmixed_bf[rws, cols], TN))
                dw = dw + _dot(dmixed_bf[rws, cols], vn_bf[rws, cols], NT)
                dbs = dbs + jnp.sum(dmixed[rws, cols], axis=1, keepdims=True)
            accs[0][g] += jnp.where(tri, dw, 0.0)
            accs[3][g] += dbs
            dvn_cols.append(jnp.concatenate(dvn_rows, axis=0))
            mixed_cols.append(jnp.concatenate(mixed_rows, axis=0))
        dvn = jnp.concatenate(dvn_cols, axis=1)
        mixed = jnp.concatenate(mixed_cols, axis=1)
        accs[1][...] += jnp.sum(dvn * vh, axis=0, keepdims=True)
        accs[2][...] += jnp.sum(dvn, axis=0, keepdims=True)
        dvh = dvn * consts[0][...]
        dva = r * (dvh - jnp.mean(dvh, axis=-1, keepdims=True) - vh * jnp.mean(dvh * vh, axis=-1, keepdims=True))
        outs[0][:, 0:512] = (dout * mixed * _gelu_grad(u, tu)).astype(BF16)
        outs[0][:, 512:1024] = (dva * _gelu_grad(v, tv)).astype(BF16)

    return _rowwise("sgu_bwd", body, [(zs, 512, 0), (zs, 512, 1), (dcat, 512, 2)], [ln_g, ln_b, w, bias_full], [(1024, BF16)],
                    [((SGU_G, 128, 128), F32), ((1, SGU_DIM), F32), ((1, SGU_DIM), F32), ((SGU_G, 128, 1), F32)], tr=256)


def _lower_bound(hg_lb):
    a0, a1 = hg_lb[0:1, :], hg_lb[1:2, :]
    m = jnp.maximum(a0, a1)
    e0, e1 = jnp.exp(a0 - m), jnp.exp(a1 - m)
    s0, s1 = e0 / (e0 + e1), e1 / (e0 + e1)
    return (s0 + s1) - s0, s0, s1


def _hg_gates(qr, fr, lb):
    C = qr.shape[0]
    sq = _sig(qr)
    qf = qr * sq
    sf = _sig(fr)
    gate = lb + (1.0 - lb) * sf
    kk = 1.0 - gate
    tri = _tril_mask(C)
    b = _dot(jnp.where(tri, 1.0, 0.0), jnp.log(gate), NN, precision=HIGHEST)
    bref = b[C // 2 - 1:C // 2, :]
    bl = b[C - 1:C, :]
    e_b = jnp.exp(b)
    e_q = jnp.exp(b - bref)
    e_k = jnp.exp(bref - b)
    e_lb = jnp.exp(bl - b)
    return dict(sq=sq, qf=qf, sf=sf, gate=gate, kk=kk, tri=tri, bl=bl, e_b=e_b, e_q=e_q, e_k=e_k, e_lb=e_lb)


def _hgrn_fwd(z1, hg_lb, gnorm):
    T = z1.shape[0]
    C = min(HG_CHUNK, T)
    nc = T // C

    def kern(q_ref, f_ref, i_ref, g_ref, lb_ref, gn_ref, o_ref, hg_ref, st_ref, s_scr):
        @pl.when(pl.program_id(0) == 0)
        def _():
            s_scr[...] = jnp.zeros(s_scr.shape, F32)

        lb_all, _, _ = _lower_bound(lb_ref[...])
        st_ref[0] = s_scr[...]
        for h in range(HEADS):
            cols = slice(h * HEAD_W, (h + 1) * HEAD_W)
            t = _hg_gates(q_ref[:, cols], f_ref[:, cols], lb_all[:, cols])
            v = i_ref[:, cols]
            v_bf = v.astype(BF16)
            st = s_scr[h]
            a = jnp.where(t["tri"], _dot((t["qf"] * t["e_q"]).astype(BF16), (t["kk"] * t["e_k"]).astype(BF16), NT), 0.0)
            o = _dot(a.astype(BF16), v_bf, NN) + _dot((t["qf"] * t["e_b"]).astype(BF16), st.astype(BF16), NT)
            s_scr[h] = st * jnp.exp(t["bl"]) + _dot(v_bf, (t["kk"] * t["e_lb"]).astype(BF16), TN)
            o_ref[:, cols] = o
            gr = g_ref[:, cols]
            r = lax.rsqrt(jnp.mean(o * o, axis=-1, keepdims=True) + EPS)
            hg_ref[:, cols] = (o * r * gn_ref[:, cols] * (gr * _sig(gr))).astype(BF16)

    seg = lambda k: pl.BlockSpec((C, D_MODEL), functools.partial(lambda n, k: (n, k), k=k))
    row = pl.BlockSpec((C, D_MODEL), lambda n: (n, 0))
    nbytes = 6 * _nbytes((C, D_MODEL), F32) + 3 * _nbytes((HEADS, 128, 128), F32)
    return pl.pallas_call(
        kern, name="hgrn_fwd", grid=(nc,),
        in_specs=[seg(0), seg(1), seg(2), seg(3), pl.BlockSpec((2, D_MODEL), lambda n: (0, 0)),
                  pl.BlockSpec((1, D_MODEL), lambda n: (0, 0))],
        out_specs=[row, row, pl.BlockSpec((1, HEADS, 128, 128), lambda n: (n, 0, 0, 0))],
        out_shape=[jax.ShapeDtypeStruct((T, D_MODEL), F32), jax.ShapeDtypeStruct((T, D_MODEL), BF16),
                   jax.ShapeDtypeStruct((nc, HEADS, 128, 128), F32)],
        scratch_shapes=[pltpu.VMEM((HEADS, 128, 128), F32)],
        compiler_params=pltpu.CompilerParams(dimension_semantics=("arbitrary",), vmem_limit_bytes=_vmem(nbytes)),
    )(z1, z1, z1, z1, hg_lb, gnorm)


def _hgrn_bwd(z1, o_pre, dhg, states, hg_lb, gnorm):
    T = z1.shape[0]
    C = min(HG_CHUNK, T)
    nc = T // C

    def kern(q_ref, f_ref, i_ref, g_ref, o_ref, dhg_ref, st_ref, lb_ref, gn_ref, dz_ref, dlb_ref, dgn_ref, ds_scr, dlb_scr):
        n = pl.program_id(0)

        @pl.when(n == 0)
        def _():
            ds_scr[...] = jnp.zeros(ds_scr.shape, F32)
            dlb_scr[...] = jnp.zeros(dlb_scr.shape, F32)
            dgn_ref[...] = jnp.zeros(dgn_ref.shape, F32)

        lb_all, s0, s1 = _lower_bound(lb_ref[...])
        for h in range(HEADS):
            cols = slice(h * HEAD_W, (h + 1) * HEAD_W)
            lb = lb_all[:, cols]
            qr, fr = q_ref[:, cols], f_ref[:, cols]
            t = _hg_gates(qr, fr, lb)
            tri = t["tri"]
            v_bf = i_ref[:, cols].astype(BF16)
            st_bf = st_ref[0, h].astype(BF16)
            dst = ds_scr[h]
            dst_bf = dst.astype(BF16)
            o = o_ref[:, cols]
            gr = g_ref[:, cols]
            sg = _sig(gr)
            sil = gr * sg
            gn = gn_ref[:, cols]
            r = lax.rsqrt(jnp.mean(o * o, axis=-1, keepdims=True) + EPS)
            on = o * r
            dh = dhg_ref[:, cols].astype(F32)
            dgn_ref[:, cols] += jnp.sum(dh * on * sil, axis=0, keepdims=True)
            dg = dh * on * gn * (sg * (1.0 + gr * (1.0 - sg)))
            don = dh * gn * sil
            do_bf = (r * (don - on * jnp.mean(don * on, axis=-1, keepdims=True))).astype(BF16)
            qe = (t["qf"] * t["e_q"]).astype(BF16)
            ke = (t["kk"] * t["e_k"]).astype(BF16)
            qb = (t["qf"] * t["e_b"]).astype(BF16)
            kh = t["kk"] * t["e_lb"]
            kh_bf = kh.astype(BF16)
            a_bf = jnp.where(tri, _dot(qe, ke, NT), 0.0).astype(BF16)
            da_bf = jnp.where(tri, _dot(do_bf, v_bf, NT), 0.0).astype(BF16)
            dv = _dot(a_bf, do_bf, TN) + _dot(kh_bf, dst_bf, NT)
            dqe = _dot(da_bf, ke, NN)
            dqb = _dot(do_bf, st_bf, NN)
            dke = _dot(da_bf, qe, TN)
            dkh = _dot(v_bf, dst_bf, NN)
            dqf = dqe * t["e_q"] + dqb * t["e_b"]
            dkk = dke * t["e_k"] + dkh * t["e_lb"]
            kh_r = kh_bf.astype(F32)
            db = qe.astype(F32) * dqe - ke.astype(F32) * dke + qb.astype(F32) * dqb - kh_r * dkh
            e_bl = jnp.exp(t["bl"])
            dbl = jnp.sum(dkh * kh_r, axis=0, keepdims=True) + e_bl * jnp.sum(st_ref[0, h] * dst, axis=0, keepdims=True)
            dlg = _dot(jnp.where(tri, 1.0, 0.0), db, TN, precision=HIGHEST) + dbl
            ds_scr[h] = dst * e_bl + _dot(do_bf, qb, TN)
            dgate = dlg / t["gate"] - dkk
            sf = t["sf"]
            dlb_scr[:, cols] += jnp.sum(dgate * (1.0 - sf), axis=0, keepdims=True)
            df = dgate * (1.0 - lb) * sf * (1.0 - sf)
            dq = dqf * (t["sq"] * (1.0 + qr * (1.0 - t["sq"])))
            dz_ref[:, cols] = dq.astype(BF16)
            dz_ref[:, D_MODEL + h * HEAD_W:D_MODEL + (h + 1) * HEAD_W] = df.astype(BF16)
            dz_ref[:, 2 * D_MODEL + h * HEAD_W:2 * D_MODEL + (h + 1) * HEAD_W] = dv.astype(BF16)
            dz_ref[:, 3 * D_MODEL + h * HEAD_W:3 * D_MODEL + (h + 1) * HEAD_W] = dg.astype(BF16)

        @pl.when(n == nc - 1)
        def _():
            d = s0 * s1 * dlb_scr[...]
            dlb_ref[0:1, :] = -d
            dlb_ref[1:2, :] = d

    seg = lambda k: pl.BlockSpec((C, D_MODEL), functools.partial(lambda n, k: (nc - 1 - n, k), k=k))
    nbytes = 6 * _nbytes((C, D_MODEL), F32) + _nbytes((C, 4 * D_MODEL), BF16) + 3 * _nbytes((HEADS, 128, 128), F32)
    return pl.pallas_call(
        kern, name="hgrn_bwd", grid=(nc,),
        in_specs=[seg(0), seg(1), seg(2), seg(3), seg(0), seg(0),
                  pl.BlockSpec((1, HEADS, 128, 128), lambda n: (nc - 1 - n, 0, 0, 0)),
                  pl.BlockSpec((2, D_MODEL), lambda n: (0, 0)), pl.BlockSpec((1, D_MODEL), lambda n: (0, 0))],
        out_specs=[pl.BlockSpec((C, 4 * D_MODEL), lambda n: (nc - 1 - n, 0)),
                   pl.BlockSpec((2, D_MODEL), lambda n: (0, 0)), pl.BlockSpec((1, D_MODEL), lambda n: (0, 0))],
        out_shape=[jax.ShapeDtypeStruct((T, 4 * D_MODEL), BF16), jax.ShapeDtypeStruct((2, D_MODEL), F32),
                   jax.ShapeDtypeStruct((1, D_MODEL), F32)],
        scratch_shapes=[pltpu.VMEM((HEADS, 128, 128), F32), pltpu.VMEM((1, D_MODEL), F32)],
        compiler_params=pltpu.CompilerParams(dimension_semantics=("arbitrary",), vmem_limit_bytes=_vmem(nbytes)),
    )(z1, z1, z1, z1, o_pre, dhg, states, hg_lb, gnorm)


def _prep_weights(wf):
    w_in_e = wf["w_in_e"]
    kr = jnp.pad(w_in_e[:, 512:544], ((0, 0), (64, 32)))
    wm = jnp.concatenate([w_in_e[:, 0:512], kr], axis=1)
    ws = w_in_e[:, 544:1568]
    wq = jnp.pad(wf["w_qb"].reshape(MLA_LORA, HEADS, 96), ((0, 0), (0, 0), (0, 32))).reshape(MLA_LORA, HEADS * HEAD_W)
    kvb = wf["w_kvb"].reshape(MLA_LORA, HEADS, 128)
    wk = jnp.pad(kvb[:, :, :64], ((0, 0), (0, 0), (0, 64))).reshape(MLA_LORA, HEADS * HEAD_W)
    wv = jnp.pad(kvb[:, :, 64:], ((0, 0), (0, 0), (0, 64))).reshape(MLA_LORA, HEADS * HEAD_W)
    wkv = jnp.concatenate([wk, wv], axis=1)
    woa = jnp.pad(wf["w_out_e"][:512].reshape(HEADS, 64, D_MODEL), ((0, 0), (0, 64), (0, 0))).reshape(HEADS * HEAD_W, D_MODEL)
    wob = wf["w_out_e"][512:]
    return dict(wm=wm, ws=ws, wq=wq, wkv=wkv, woa=woa, wob=wob, w_in_o=wf["w_in_o"], w_out_o=wf["w_out_o"],
                w_ff1=wf["w_ff1"], w_ff2=wf["w_ff2"])


def _unprep_grads(g):
    dwm, dws = g["wm"], g["ws"]
    d_in_e = jnp.concatenate([dwm[:, 0:512], dwm[:, 512 + 64:512 + 96], dws], axis=1)
    d_qb = g["wq"].reshape(MLA_LORA, HEADS, HEAD_W)[:, :, :96].reshape(MLA_LORA, HEADS * 96)
    dk = g["wkv"][:, :1024].reshape(MLA_LORA, HEADS, HEAD_W)[:, :, :64]
    dv = g["wkv"][:, 1024:].reshape(MLA_LORA, HEADS, HEAD_W)[:, :, :64]
    d_kvb = jnp.concatenate([dk, dv], axis=2).reshape(MLA_LORA, HEADS * 128)
    d_oa = g["woa"].reshape(HEADS, HEAD_W, D_MODEL)[:, :64].reshape(HEADS * 64, D_MODEL)
    d_out_e = jnp.concatenate([d_oa, g["wob"]], axis=0)
    return dict(w_in_e=d_in_e, w_qb=d_qb, w_kvb=d_kvb, w_out_e=d_out_e, w_in_o=g["w_in_o"], w_out_o=g["w_out_o"],
                w_ff1=g["w_ff1"], w_ff2=g["w_ff2"])


def _local_step(x, positions, target, wf, sp):
    w = _prep_weights(wf)
    T = x.shape[0]
    half = MLA_ROPE // 2
    inv_freq = ROPE_BASE ** (-jnp.arange(half, dtype=F32) / half)
    invf_lane = jnp.concatenate([jnp.zeros((64,), F32), inv_freq, inv_freq, jnp.zeros((32,), F32)]).reshape(1, HEAD_W)
    tabs = _rope_tables(positions.reshape(T, 1), invf_lane)
    bias_full = jnp.repeat(sp["sgu_b"][0].T, 128, axis=1)
    sgu_w = sp["sgu_w"][0]
    gq, gkv = sp["mla_gq"], sp["mla_gkv"]

    zm = _mm("l0_in_mla", [(x, w["wm"])], "nn", [(640, F32)], tn=640)
    zs = _mm("l0_in_sgu", [(x, w["ws"])], "nn", [(512, F32)])
    cqn, ckvn, kr_rot = _mla_pre(zm, tabs, gq, gkv)
    q = _mm("l0_q", [(cqn, w["wq"])], "nn", [(1024, BF16)], tn=1024,
            extras=[(t, HEAD_W, lambda i, j: (i, 0)) for t in tabs],
            epilogue=lambda acc, c, s1, s2: (_rope_heads(acc, c, s1, s2, _rope),))
    k, v = _mm("l0_kv", [(ckvn, w["wkv"])], "nn", [(1024, BF16), (1024, BF16)], tn=2048,
               extras=[(kr_rot, HEAD_W, lambda i, j: (i, 0))],
               epilogue=lambda acc, kr: (acc[:, :1024] + jnp.concatenate([kr] * HEADS, axis=1), acc[:, 1024:]))
    o_att, lse = _attn_fwd(q, k, v)
    b_out = _sgu_fwd(zs, sp["sgu_ln_g"], sp["sgu_ln_b"], sgu_w, bias_full)
    mix0 = _mm("l0_out", [(o_att, w["woa"]), (b_out, w["wob"])], "nn", [(512, F32)])
    y1, h1, h1_bf = _ln_fwd("l0_ln1", x, mix0, sp["ln1_g"][0:1], sp["ln1_b"][0:1])
    a0, act0, ff0 = _mlp_fwd("l0", h1_bf, w["w_ff1"][0], w["w_ff2"][0])
    y2, h2, h2_bf = _ln_fwd("l0_ln2", h1, ff0, sp["ln2_g"][0:1], sp["ln2_b"][0:1])

    z1 = _mm("l1_in", [(h2_bf, w["w_in_o"])], "nn", [(512, F32)])
    o_pre, hg, states = _hgrn_fwd(z1, sp["hg_lb"], sp["hg_gnorm"])
    mix1 = _mm("l1_out", [(hg, w["w_out_o"])], "nn", [(512, F32)])
    y3, h3, h3_bf = _ln_fwd("l1_ln1", h2, mix1, sp["ln1_g"][1:2], sp["ln1_b"][1:2])
    a1, act1, ff1 = _mlp_fwd("l1", h3_bf, w["w_ff1"][1], w["w_ff2"][1])
    y4, dh4, sq_err = _ln_loss("l1_ln2", h3, ff1, sp["ln2_g"][1:2], sp["ln2_b"][1:2], target)

    gs, gw = {}, {}
    dy4, dy4_bf, dg, db = _ln_bwd("l1_ln2_bwd", y4, dh4, sp["ln2_g"][1:2])
    gs["ln2_g1"], gs["ln2_b1"] = dg, db
    dh3, dh3_bf, dw1_1, dw2_1 = _mlp_bwd("l1", h3_bf, a1, act1, dy4_bf, dy4, w["w_ff1"][1], w["w_ff2"][1])
    dy3, dy3_bf, dg, db = _ln_bwd("l1_ln1_bwd", y3, dh3, sp["ln1_g"][1:2])
    gs["ln1_g1"], gs["ln1_b1"] = dg, db
    gw["w_out_o"] = _mm("l1_dwout", [(hg, dy3_bf)], "tn", [(512, F32)])
    dhg = _mm("l1_dhg", [(dy3_bf, w["w_out_o"])], "nt", [(512, BF16)])
    dz1, gs["hg_lb"], gs["hg_gnorm"] = _hgrn_bwd(z1, o_pre, dhg, states, sp["hg_lb"], sp["hg_gnorm"])
    gw["w_in_o"] = _mm("l1_dwin", [(h2_bf, dz1)], "tn", [(512, F32)])
    dh2 = _mm("l1_dh2", [(dz1, w["w_in_o"])], "nt", [(512, F32)],
              extras=[(dy3, 512, lambda i, j: (i, j))], epilogue=lambda acc, d: (acc + ALPHA * d,))

    dy2, dy2_bf, dg, db = _ln_bwd("l0_ln2_bwd", y2, dh2, sp["ln2_g"][0:1])
    gs["ln2_g0"], gs["ln2_b0"] = dg, db
    dh1, _, dw1_0, dw2_0 = _mlp_bwd("l0", h1_bf, a0, act0, dy2_bf, dy2, w["w_ff1"][0], w["w_ff2"][0])
    gw["w_ff1"] = jnp.stack([dw1_0, dw1_1])
    gw["w_ff2"] = jnp.stack([dw2_0, dw2_1])
    dy1, dy1_bf, dg, db = _ln_bwd("l0_ln1_bwd", y1, dh1, sp["ln1_g"][0:1])
    gs["ln1_g0"], gs["ln1_b0"] = dg, db
    gw["woa"] = _mm("l0_dwoa", [(o_att, dy1_bf)], "tn", [(512, F32)])
    gw["wob"] = _mm("l0_dwob", [(b_out, dy1_bf)], "tn", [(512, F32)])
    wo_cat = jnp.concatenate([w["woa"], w["wob"]], axis=0)
    dcat = _mm("l0_dcat", [(dy1_bf, wo_cat)], "nt", [(512, BF16)])
    dzs, gs["sgu_w"], gs["sgu_ln_g"], gs["sgu_ln_b"], dbs = _sgu_bwd(zs, dcat, sp["sgu_ln_g"], sp["sgu_ln_b"], sgu_w, bias_full)
    gs["sgu_b"] = dbs.reshape(1, SGU_G, 128)
    dq, dk, dv = _attn_bwd(q, k, v, o_att, lse, dcat)
    dq_pre = _unrope_heads(dq, tabs)
    gw["wq"] = _mm("l0_dwq", [(cqn, dq_pre)], "tn", [(1024, F32)], tn=1024)
    dcqn = _mm("l0_dcqn", [(dq_pre, w["wq"])], "nt", [(256, F32)], tn=256)
    gw["wkv"] = jnp.concatenate([_mm("l0_dwk", [(ckvn, dk)], "tn", [(1024, F32)], tn=1024),
                                 _mm("l0_dwv", [(ckvn, dv)], "tn", [(1024, F32)], tn=1024)], axis=1)
    dckvn = _mm("l0_dckvn", [(dk, w["wkv"][:, :1024]), (dv, w["wkv"][:, 1024:])], "nt", [(256, F32)], tn=256)
    dzm, gs["mla_gq"], gs["mla_gkv"] = _mla_pre_bwd(zm, tabs, gq, gkv, dcqn, dckvn, dk)
    gw["wm"] = _mm("l0_dwm", [(x, dzm)], "tn", [(640, F32)], tn=640)
    gw["ws"] = _mm("l0_dws", [(x, dzs)], "tn", [(512, F32)])
    dx = _mm("l0_dx", [(dzm, w["wm"]), (dzs, w["ws"])], "nt", [(512, F32)],
             extras=[(dy1, 512, lambda i, j: (i, j))], epilogue=lambda acc, d: (acc + ALPHA * d,))

    small = dict(mla_gq=gs["mla_gq"], mla_gkv=gs["mla_gkv"], sgu_ln_g=gs["sgu_ln_g"], sgu_ln_b=gs["sgu_ln_b"],
                 sgu_b=gs["sgu_b"], sgu_w=gs["sgu_w"][None], hg_lb=gs["hg_lb"],
                 ln1_g=jnp.concatenate([gs["ln1_g0"], gs["ln1_g1"]]), ln1_b=jnp.concatenate([gs["ln1_b0"], gs["ln1_b1"]]),
                 ln2_g=jnp.concatenate([gs["ln2_g0"], gs["ln2_g1"]]), ln2_b=jnp.concatenate([gs["ln2_b0"], gs["ln2_b1"]]),
                 hg_gnorm=gs["hg_gnorm"])
    return sq_err, dx, _unprep_grads(gw), small


def _me():
    return lax.axis_index("x"), lax.axis_index("y"), lax.axis_index("c")


def _all_gather(shard):
    R, W = shard.shape

    def kern(x_ref, out_ref, send_sems, recv_sems, local_sem):
        x, y, c = _me()
        me, sibling = (x, y, c), (x, y, 1 - c)
        chips = [(1 - x, y), (x, 1 - y), (1 - x, 1 - y)]

        def slot(px, py, pc):
            return out_ref.at[4 * px + 2 * py + pc]

        def copy(k, block, to, src=None):
            return pltpu.make_async_remote_copy(
                src_ref=slot(*block) if src is None else src, dst_ref=slot(*block),
                send_sem=send_sems.at[k], recv_sem=recv_sems.at[k], device_id=to, device_id_type=MESH)

        mine = pltpu.make_async_copy(x_ref, slot(*me), local_sem)
        mine.start()
        first = [copy(0, me, sibling, src=x_ref)]
        first += [copy(1 + j, me, (*chip, c), src=x_ref) for j, chip in enumerate(chips)]
        for cp in first:
            cp.start()
        passed = [copy(4 + j, (*chip, c), sibling) for j, chip in enumerate(chips)]
        for j, chip in enumerate(chips):
            copy(1 + j, (*chip, c), me).wait_recv()
            passed[j].start()
        copy(0, sibling, me).wait_recv()
        for j, chip in enumerate(chips):
            copy(4 + j, (*chip, 1 - c), me).wait_recv()
        for cp in first + passed:
            cp.wait_send()
        mine.wait()

    return pl.pallas_call(
        kern, name="weights_all_gather", out_shape=jax.ShapeDtypeStruct((N_DEV, R, W), shard.dtype),
        in_specs=[pl.BlockSpec(memory_space=pl.ANY)], out_specs=pl.BlockSpec(memory_space=pl.ANY),
        scratch_shapes=[pltpu.SemaphoreType.DMA((7,)), pltpu.SemaphoreType.DMA((7,)), pltpu.SemaphoreType.DMA],
    )(shard)


def _rs_sibling(g):
    _, _, R, W = g.shape

    def kern(g_ref, out_ref, send_sems, recv_sems):
        x, y, c = _me()
        copies = [pltpu.make_async_remote_copy(
            src_ref=g_ref.at[k, 1 - c], dst_ref=out_ref.at[k], send_sem=send_sems.at[k], recv_sem=recv_sems.at[k],
            device_id=(x, y, 1 - c), device_id_type=MESH) for k in range(4)]
        for cp in copies:
            cp.start()
        for cp in copies:
            cp.wait()

    return pl.pallas_call(
        kern, name="grads_to_sibling", out_shape=jax.ShapeDtypeStruct((4, R, W), g.dtype),
        in_specs=[pl.BlockSpec(memory_space=pl.ANY)], out_specs=pl.BlockSpec(memory_space=pl.ANY),
        scratch_shapes=[pltpu.SemaphoreType.DMA((4,)), pltpu.SemaphoreType.DMA((4,))],
    )(g)


def _chip_sum(g, from_sibling, core):
    _, _, R, W = g.shape
    tr = 512

    def kern(core_ref, g_ref, s_ref, o_ref):
        o_ref[...] = (g_ref[0, 0] + s_ref[0]).astype(BF16)[None]

    return pl.pallas_call(
        kern, name="grads_chip_sum", out_shape=jax.ShapeDtypeStruct((4, R, W), BF16),
        grid_spec=pltpu.PrefetchScalarGridSpec(
            num_scalar_prefetch=1, grid=(4, R // tr),
            in_specs=[pl.BlockSpec((1, 1, tr, W), lambda k, i, core: (k, core[0], i, 0)),
                      pl.BlockSpec((1, tr, W), lambda k, i, core: (k, i, 0))],
            out_specs=pl.BlockSpec((1, tr, W), lambda k, i, core: (k, i, 0))),
        compiler_params=pltpu.CompilerParams(dimension_semantics=("parallel", "parallel"),
                                             vmem_limit_bytes=_vmem(3 * tr * W * 4)),
    )(core, g, from_sibling)


def _rs_chips(p):
    _, R, W = p.shape

    def kern(p_ref, out_ref, send_sems, recv_sems):
        x, y, c = _me()
        chips = [(1 - x, y), (x, 1 - y), (1 - x, 1 - y)]
        copies = [pltpu.make_async_remote_copy(
            src_ref=p_ref.at[2 * cx + cy], dst_ref=out_ref.at[j], send_sem=send_sems.at[j], recv_sem=recv_sems.at[j],
            device_id=(cx, cy, c), device_id_type=MESH) for j, (cx, cy) in enumerate(chips)]
        for cp in copies:
            cp.start()
        for cp in copies:
            cp.wait()

    return pl.pallas_call(
        kern, name="grads_between_chips", out_shape=jax.ShapeDtypeStruct((3, R, W), p.dtype),
        in_specs=[pl.BlockSpec(memory_space=pl.ANY)], out_specs=pl.BlockSpec(memory_space=pl.ANY),
        scratch_shapes=[pltpu.SemaphoreType.DMA((3,)), pltpu.SemaphoreType.DMA((3,))],
    )(p)


def _adamw(w, g, m, v):
    m = ADAM_B1 * m + (1.0 - ADAM_B1) * g
    v = ADAM_B2 * v + (1.0 - ADAM_B2) * (g * g)
    m_hat = m / (1.0 - ADAM_B1 ** ADAM_STEP)
    v_hat = v / (1.0 - ADAM_B2 ** ADAM_STEP)
    return -ADAM_LR * (m_hat / (jnp.sqrt(v_hat) + ADAM_EPS) + ADAM_WD * w), m, v


def _finish_sharded(g, from_sibling, from_chips, w, m, v, where):
    _, _, R, W = g.shape
    tr = 512

    def kern(where_ref, g_ref, s_ref, c_ref, w_ref, m_ref, v_ref, go_ref, d_ref, mo_ref, vo_ref):
        grad = g_ref[0, 0] + s_ref[0]
        for j in range(3):
            grad = grad + c_ref[j].astype(F32)
        go_ref[...] = grad
        d_ref[...], mo_ref[...], vo_ref[...] = _adamw(w_ref[...], grad, m_ref[...], v_ref[...])

    row = pl.BlockSpec((tr, W), lambda i, wh: (i, 0))
    return pl.pallas_call(
        kern, name="grads_finish_adamw", out_shape=[jax.ShapeDtypeStruct((R, W), F32)] * 4,
        grid_spec=pltpu.PrefetchScalarGridSpec(
            num_scalar_prefetch=1, grid=(R // tr,),
            in_specs=[pl.BlockSpec((1, 1, tr, W), lambda i, wh: (wh[0], wh[1], i, 0)),
                      pl.BlockSpec((1, tr, W), lambda i, wh: (wh[0], i, 0)),
                      pl.BlockSpec((3, tr, W), lambda i, wh: (0, i, 0)), row, row, row],
            out_specs=[row, row, row, row]),
        compiler_params=pltpu.CompilerParams(dimension_semantics=("parallel",), vmem_limit_bytes=_vmem(12 * tr * W * 4)),
    )(where, g, from_sibling, from_chips, w, m, v)


def _all_reduce_small(name, x):
    R, W = x.shape

    def kern(x_ref, out_ref, buf, send_sems, recv_sems):
        px, py, pc = _me()
        me = 4 * px + 2 * py + pc
        buf[me] = x_ref[...]
        copies = []
        for r in range(1, N_DEV):
            peer = (px ^ (r >> 2), py ^ ((r >> 1) & 1), pc ^ (r & 1))
            copies.append(pltpu.make_async_remote_copy(
                src_ref=x_ref, dst_ref=buf.at[me], send_sem=send_sems.at[r - 1], recv_sem=recv_sems.at[r - 1],
                device_id=peer, device_id_type=MESH))
        for cp in copies:
            cp.start()
        for r in range(1, N_DEV):
            peer = me ^ r
            pltpu.make_async_remote_copy(
                src_ref=x_ref, dst_ref=buf.at[peer], send_sem=send_sems.at[r - 1], recv_sem=recv_sems.at[r - 1],
                device_id=(px, py, pc), device_id_type=MESH).wait_recv()
        for cp in copies:
            cp.wait_send()
        acc = buf[0]
        for d in range(1, N_DEV):
            acc = acc + buf[d]
        out_ref[...] = acc

    return pl.pallas_call(
        kern, name=name, out_shape=jax.ShapeDtypeStruct((R, W), F32),
        in_specs=[pl.BlockSpec(memory_space=pltpu.VMEM)], out_specs=pl.BlockSpec(memory_space=pltpu.VMEM),
        scratch_shapes=[pltpu.VMEM((N_DEV, R, W), F32), pltpu.SemaphoreType.DMA((7,)), pltpu.SemaphoreType.DMA((7,))],
    )(x)


def _adamw_small(w, g, m, v):
    def body(rows, consts, outs, accs):
        outs[0][...], outs[1][...], outs[2][...] = _adamw(rows[0][...], rows[1][...], rows[2][...], rows[3][...])

    return _rowwise("small_adamw", body, [_full(w), _full(g), _full(m), _full(v)], [], [(LANES, F32)] * 3, tr=SMALL_ROWS)


def _pack(parts, rows, dtype):
    flat = jnp.concatenate([p.astype(dtype).reshape(-1) for p in parts])
    return jnp.pad(flat, (0, rows * LANES - flat.shape[0])).reshape(rows, LANES)


def _unpack(buf, table):
    flat, out, off = buf.reshape(-1), {}, 0
    for name, shape in table:
        n = int(np.prod(shape))
        out[name] = flat[off:off + n].reshape(shape)
        off += n
    return out


def _full_weights(gathered):
    per_dev = [_unpack(gathered[d], SEGS) for d in range(N_DEV)]
    cat = lambda name, axis: jnp.concatenate([p[name] for p in per_dev], axis=axis)
    return dict(w_in_e=cat("w_in_e", 1), w_qb=cat("w_qb", 1), w_kvb=cat("w_kvb", 1), w_out_e=cat("w_out_e", 0),
                w_in_o=cat("w_in_o", 1), w_out_o=cat("w_out_o", 0), w_ff1=cat("w_ff1", 2), w_ff2=cat("w_ff2", 1))


def _grad_blocks(gw):
    axis = dict(w_in_e=1, w_qb=1, w_kvb=1, w_out_e=0, w_in_o=1, w_out_o=0, w_ff1=2, w_ff2=1)
    blocks = []
    for d in range(N_DEV):
        parts = []
        for name, shape in SEGS:
            n = shape[axis[name]]
            parts.append(lax.slice_in_dim(gw[name], d * n, (d + 1) * n, axis=axis[name]))
        blocks.append(_pack(parts, PACK_ROWS, F32))
    return jnp.stack(blocks).reshape(4, 2, PACK_ROWS, LANES)


def kernel(x, positions, w_in_e, mla_gq, mla_gkv, w_qb, w_kvb, sgu_ln_g, sgu_ln_b, sgu_w, sgu_b, w_out_e, w_in_o, hg_lb, hg_gnorm, w_out_o, ln1_g, ln1_b, w_ff1, w_ff2, ln2_g, ln2_b, loss_target, m_w_in_e, m_mla_gq, m_mla_gkv, m_w_qb, m_w_kvb, m_sgu_ln_g, m_sgu_ln_b, m_sgu_w, m_sgu_b, m_w_out_e, m_w_in_o, m_hg_lb, m_hg_gnorm, m_w_out_o, m_ln1_g, m_ln1_b, m_w_ff1, m_w_ff2, m_ln2_g, m_ln2_b, v_w_in_e, v_mla_gq, v_mla_gkv, v_w_qb, v_w_kvb, v_sgu_ln_g, v_sgu_ln_b, v_sgu_w, v_sgu_b, v_w_out_e, v_w_in_o, v_hg_lb, v_hg_gnorm, v_w_out_o, v_ln1_g, v_ln1_b, v_w_ff1, v_w_ff2, v_ln2_g, v_ln2_b):
    given = dict(locals())
    px, py, pc = _me()
    dev = 4 * px + 2 * py + pc
    own = dev * 128

    def widen(a):
        return lax.dynamic_update_slice(jnp.zeros((1, D_MODEL), F32), a, (0, own))

    shards = {name: given[name][0] if name not in ("w_ff1", "w_ff2") else given[name] for name, _ in SEGS}
    gathered = _all_gather(_pack([shards[name] for name, _ in SEGS], PACK_ROWS, BF16))
    wf = _full_weights(gathered)
    gnorm_full = _all_reduce_small("gnorm_all_gather", jnp.pad(widen(hg_gnorm), ((0, 7), (0, 0))))[0:1]
    sp = {name: given[name] for name, _ in SMALL}
    sp["hg_gnorm"] = gnorm_full

    sq_err, dx, gw, gs = _local_step(x[0], positions[0], loss_target[0], wf, sp)
    loss = lax.psum(0.5 * jnp.sum(sq_err) / D_MODEL, ("x", "y", "c"))

    g_blocks = _grad_blocks(gw)
    from_sibling = _rs_sibling(g_blocks)
    chip_sums = _chip_sum(g_blocks, from_sibling, pc.reshape(1).astype(jnp.int32))
    from_chips = _rs_chips(chip_sums)
    where = jnp.stack([2 * px + py, pc]).astype(jnp.int32)
    packs = [_pack([shards_of[name] for name, _ in SEGS], PACK_ROWS, F32)
             for shards_of in (shards,
                               {n: given["m_" + n][0] if n not in ("w_ff1", "w_ff2") else given["m_" + n] for n, _ in SEGS},
                               {n: given["v_" + n][0] if n not in ("w_ff1", "w_ff2") else given["v_" + n] for n, _ in SEGS})]
    big = [_unpack(r, SEGS) for r in _finish_sharded(g_blocks, from_sibling, from_chips, *packs, where)]

    g_small = _all_reduce_small("small_all_reduce", _pack([gs[name] for name, _ in SMALL], SMALL_ROWS, F32))
    small_packs = [_pack([given[pre + name] if name != "hg_gnorm" else widen(given[pre + name]) for name, _ in SMALL], SMALL_ROWS, F32)
                   for pre in ("", "m_", "v_")]
    small = [_unpack(r, SMALL) for r in (g_small, *_adamw_small(small_packs[0], g_small, small_packs[1], small_packs[2]))]

    def leaf(kind, name):
        if name == "hg_gnorm":
            return lax.dynamic_slice(small[kind][name], (0, own), (1, 128))
        if name in small[kind]:
            return small[kind][name]
        a = big[kind][name]
        return a if name in ("w_ff1", "w_ff2") else a[None]

    order = ["w_in_e", "mla_gq", "mla_gkv", "w_qb", "w_kvb", "sgu_ln_g", "sgu_ln_b", "sgu_w", "sgu_b", "w_out_e", "w_in_o",
             "hg_lb", "hg_gnorm", "w_out_o", "ln1_g", "ln1_b", "w_ff1", "w_ff2", "ln2_g", "ln2_b"]
    return (loss, dx[None], *[leaf(kind, name) for kind in range(4) for name in order])
```

```python
import functools
import math

import jax
import jax.numpy as jnp
import numpy as np
from jax import lax
from jax.experimental import pallas as pl
from jax.experimental.pallas import tpu as pltpu

F32 = jnp.float32
BF16 = jnp.bfloat16
MESH = pl.DeviceIdType.MESH
HIGHEST = lax.Precision.HIGHEST

D_MODEL = 1024
D_FF = 4096
N_DEV = 8
HEADS = 8
HEAD_W = 128
MLA_NOPE = 64
MLA_ROPE = 32
MLA_V = 64
MLA_LORA = 256
MLA_SCALE = (MLA_NOPE + MLA_ROPE) ** -0.5
ROPE_BASE = 10000.0
SGU_DIM = 512
SGU_G = 4
SGU_CHUNK = 128
HG_CHUNK = 64
ALPHA = (2 * 2) ** 0.25
EPS = 1e-5
ADAM_LR, ADAM_B1, ADAM_B2, ADAM_EPS, ADAM_WD, ADAM_STEP = 0.001, 0.9, 0.999, 1e-08, 0.01, 10

VMEM_CAP_V7X = 56 * 2**20
VMEM_SLACK = 12 * 2**20
TM = 512
TN = 512


def _vmem(block_bytes):
    return int(min(VMEM_CAP_V7X, 2 * block_bytes + VMEM_SLACK))


def _nbytes(shape, dtype):
    return int(np.prod([d for d in shape if d is not None])) * jnp.dtype(dtype).itemsize


def _sig(x):
    return 1.0 / (1.0 + jnp.exp(-x))


def _gelu(x):
    c = math.sqrt(2.0 / math.pi)
    t = jnp.tanh(c * (x + 0.044715 * x * x * x))
    return 0.5 * x * (1.0 + t), t


def _gelu_grad(x, t):
    c = math.sqrt(2.0 / math.pi)
    return 0.5 * (1.0 + t) + 0.5 * x * (1.0 - t * t) * c * (1.0 + 3 * 0.044715 * x * x)


def _dot(a, b, dims, precision=None):
    return lax.dot_general(a, b, (dims, ((), ())), preferred_element_type=F32, precision=precision)


NN = ((1,), (0,))
NT = ((1,), (1,))
TN_ = ((0,), (0,))


def _tiled(name, grid, ins, outs, compute):
    n_in = len(ins)

    def kern(*refs):
        for o_ref, r in zip(refs[n_in:], compute(*refs[:n_in])):
            o_ref[...] = r.astype(o_ref.dtype).reshape(o_ref.shape)

    swap = lambda f: (lambda j, i: f(i, j))
    nbytes = sum(_nbytes(blk, a.dtype) for a, blk, _ in ins) + sum(_nbytes(blk, dt) + _nbytes(blk, F32) for _, dt, blk, _ in outs)
    res = pl.pallas_call(
        kern, name=name, grid=grid,
        in_specs=[pl.BlockSpec(blk, swap(f)) for _, blk, f in ins],
        out_specs=[pl.BlockSpec(blk, swap(f)) for _, _, blk, f in outs],
        out_shape=[jax.ShapeDtypeStruct(shape, dt) for shape, dt, _, _ in outs],
        compiler_params=pltpu.CompilerParams(dimension_semantics=("parallel", "parallel"), vmem_limit_bytes=_vmem(nbytes)),
    )(*[a for a, _, _ in ins])
    return res if len(res) > 1 else res[0]


def _rb(a, tm, w=None, cb=0):
    return (a, (tm, a.shape[1] if w is None else w), lambda i, j: (i, cb))


def _rbj(a, tm, tn):
    return (a, (tm, tn), lambda i, j: (i, j))


def _cw(b, tn):
    return (b, (b.shape[0], tn), lambda i, j: (0, j))


def _rw(b, tn):
    return (b, (tn, b.shape[1]), lambda i, j: (j, 0))


def _tl(a, tm):
    return (a, (a.shape[0], tm), lambda i, j: (0, i))


def _gcw(g):
    return (g, (None, g.shape[1], g.shape[2]), lambda i, j: (j, 0, 0))


def _grw(g, tn):
    return (g, (N_DEV, tn, g.shape[2]), lambda i, j: (0, j, 0))


def _out(m, n, dtype, tm, tn):
    return ((m, n), dtype, (tm, tn), lambda i, j: (i, j))


def _out_dev(k, n, tm):
    return ((N_DEV, k, n), F32, (None, tm, n), lambda i, j: (j, i, 0))


def _mmc(dims, n_pairs=1, epilogue=None):
    def compute(*refs):
        acc = None
        for k in range(n_pairs):
            d = _dot(refs[2 * k][...].astype(BF16), refs[2 * k + 1][...].astype(BF16), dims)
            acc = d if acc is None else acc + d
        ext = [r[...] for r in refs[2 * n_pairs:]]
        return epilogue(acc, *ext) if epilogue is not None else (acc,)

    return compute


def _mmc_dev(epilogue=None):
    def compute(a_ref, b_ref, *ext_refs):
        n = b_ref.shape[2]
        acc = None
        for d in range(N_DEV):
            t = _dot(a_ref[:, d * n:(d + 1) * n].astype(BF16), b_ref[d].astype(BF16), NT)
            acc = t if acc is None else acc + t
        ext = [r[...] for r in ext_refs]
        return epilogue(acc, *ext) if epilogue is not None else (acc,)

    return compute


def _rowwise(name, body, rows, consts, out_rows, out_accs=(), tr=512):
    T = rows[0][0].shape[0]
    tr = min(tr, T)
    nr, ncn, no = len(rows), len(consts), len(out_rows)

    def kern(*refs):
        accs = refs[nr + ncn + no:]
        if accs:
            @pl.when(pl.program_id(0) == 0)
            def _():
                for a in accs:
                    a[...] = jnp.zeros(a.shape, a.dtype)
        body(refs[:nr], refs[nr:nr + ncn], refs[nr + ncn:nr + ncn + no], accs)

    in_specs = [pl.BlockSpec((tr, w), functools.partial(lambda i, cb: (i, cb), cb=cb)) for _, w, cb in rows]
    in_specs += [pl.BlockSpec(c.shape, functools.partial(lambda i, nd: (0,) * nd, nd=c.ndim)) for c in consts]
    out_specs = [pl.BlockSpec((tr, w), lambda i: (i, 0)) for w, _ in out_rows]
    out_specs += [pl.BlockSpec(s, functools.partial(lambda i, nd: (0,) * nd, nd=len(s))) for s, _ in out_accs]
    out_shape = [jax.ShapeDtypeStruct((T, w), dt) for w, dt in out_rows]
    out_shape += [jax.ShapeDtypeStruct(s, dt) for s, dt in out_accs]
    nbytes = sum(_nbytes((tr, w), a.dtype) for a, w, _ in rows) + sum(_nbytes(c.shape, c.dtype) for c in consts)
    nbytes += sum(_nbytes((tr, w), dt) for w, dt in out_rows) + sum(_nbytes(s, dt) for s, dt in out_accs)
    res = pl.pallas_call(
        kern, name=name, grid=(T // tr,), in_specs=in_specs, out_specs=out_specs, out_shape=out_shape,
        compiler_params=pltpu.CompilerParams(dimension_semantics=("arbitrary",), vmem_limit_bytes=_vmem(nbytes)),
    )(*[a for a, _, _ in rows], *consts)
    return res if len(res) > 1 else res[0]


def _full(a):
    return (a, a.shape[1], 0)


def _ln_stats(y):
    mu = jnp.mean(y, axis=-1, keepdims=True)
    yc = y - mu
    r = lax.rsqrt(jnp.mean(yc * yc, axis=-1, keepdims=True) + EPS)
    return yc * r, r


def _ln_fwd(name, h_in, mix, g, b, layer):
    def body(rows, consts, outs, accs):
        y = ALPHA * rows[0][...] + rows[1][...]
        xh, _ = _ln_stats(y)
        h = xh * consts[0][layer:layer + 1, :] + consts[1][layer:layer + 1, :]
        outs[0][...] = y
        outs[1][...] = h
        outs[2][...] = h.astype(BF16)

    return _rowwise(name, body, [_full(h_in), _full(mix)], [g, b], [(D_MODEL, F32), (D_MODEL, F32), (D_MODEL, BF16)], tr=256)


def _ln_loss(name, h_in, mix, g, b, layer, target):
    def body(rows, consts, outs, accs):
        y = ALPHA * rows[0][...] + rows[1][...]
        xh, _ = _ln_stats(y)
        err = xh * consts[0][layer:layer + 1, :] + consts[1][layer:layer + 1, :] - rows[2][...]
        outs[0][...] = y
        outs[1][...] = err * (1.0 / D_MODEL)
        accs[0][...] += jnp.sum(err * err, axis=0, keepdims=True)

    return _rowwise(name, body, [_full(h_in), _full(mix), _full(target)], [g, b], [(D_MODEL, F32), (D_MODEL, F32)],
                    [((1, D_MODEL), F32)], tr=256)


def _ln_bwd(name, y, dh, g, layer):
    def body(rows, consts, outs, accs):
        xh, r = _ln_stats(rows[0][...])
        d = rows[1][...]
        accs[0][...] += jnp.sum(d * xh, axis=0, keepdims=True)
        accs[1][...] += jnp.sum(d, axis=0, keepdims=True)
        dx = d * consts[0][layer:layer + 1, :]
        dy = r * (dx - jnp.mean(dx, axis=-1, keepdims=True) - xh * jnp.mean(dx * xh, axis=-1, keepdims=True))
        outs[0][...] = dy
        outs[1][...] = dy.astype(BF16)

    return _rowwise(name, body, [_full(y), _full(dh)], [g], [(D_MODEL, F32), (D_MODEL, BF16)],
                    [((1, D_MODEL), F32), ((1, D_MODEL), F32)], tr=256)


def _relu2_epilogue(acc):
    a = jnp.maximum(acc, 0.0)
    return acc, a * a


def _mlp_fwd(tag, h_bf, w1, w2):
    T = h_bf.shape[0]
    tm = min(TM, T)
    a, act = _tiled(f"{tag}_ff1", (N_DEV, T // tm), [_rb(h_bf, tm), _gcw(w1)],
                    [_out(T, D_FF, BF16, tm, TN), _out(T, D_FF, BF16, tm, TN)], _mmc(NN, epilogue=_relu2_epilogue))
    ff = _tiled(f"{tag}_ff2", (D_MODEL // TN, T // tm), [_rb(act, tm), _cw(w2.reshape(D_FF, D_MODEL), TN)],
                [_out(T, D_MODEL, F32, tm, TN)], _mmc(NN))
    return a, act, ff


def _mlp_bwd(tag, h_bf, a, act, dff_bf, dy, w1, w2):
    T = h_bf.shape[0]
    tm = min(TM, T)
    da = _tiled(f"{tag}_dact", (N_DEV, T // tm), [_rb(dff_bf, tm), _rw(w2.reshape(D_FF, D_MODEL), TN), _rbj(a, tm, TN)],
                [_out(T, D_FF, BF16, tm, TN)],
                _mmc(NT, epilogue=lambda acc, a_t: (acc * 2.0 * jnp.maximum(a_t.astype(F32), 0.0),)))
    dw2 = _tiled(f"{tag}_dw2", (D_MODEL // TN, D_FF // TM), [_tl(act, TM), _cw(dff_bf, TN)],
                 [_out(D_FF, D_MODEL, F32, TM, TN)], _mmc(TN_)).reshape(N_DEV, D_FF // N_DEV, D_MODEL)
    dw1 = _tiled(f"{tag}_dw1", (N_DEV, D_MODEL // TM), [_tl(h_bf, TM), _cw(da, TN)], [_out_dev(D_MODEL, TN, TM)], _mmc(TN_))
    dh, dh_bf = _tiled(f"{tag}_dh", (D_MODEL // TN, T // tm), [_rb(da, tm), _grw(w1, TN), _rbj(dy, tm, TN)],
                       [_out(T, D_MODEL, F32, tm, TN), _out(T, D_MODEL, BF16, tm, TN)],
                       _mmc_dev(epilogue=lambda acc, dy_t: (acc + ALPHA * dy_t,) * 2))
    return dh, dh_bf, dw1, dw2


def _rope_tables(positions_col, invf_lane):
    def body(rows, consts, outs, accs):
        ang = rows[0][...].astype(F32) * consts[0][...]
        c, s = jnp.cos(ang), jnp.sin(ang)
        lane = lax.broadcasted_iota(jnp.int32, ang.shape, 1)
        outs[0][...] = jnp.where(lane < 64, 1.0, jnp.where(lane < 96, c, 0.0))
        outs[1][...] = jnp.where((lane >= 64) & (lane < 80), -s, 0.0)
        outs[2][...] = jnp.where((lane >= 80) & (lane < 96), s, 0.0)

    return _rowwise("rope_tables", body, [_full(positions_col)], [invf_lane], [(HEAD_W, F32)] * 3)


def _rope(x, c, s1, s2):
    return x * c + pltpu.roll(x, 112, 1) * s1 + pltpu.roll(x, 16, 1) * s2


def _rope_t(dx, c, s1, s2):
    return dx * c + pltpu.roll(dx * s1, 16, 1) + pltpu.roll(dx * s2, 112, 1)


def _rms(c):
    r = lax.rsqrt(jnp.mean(c * c, axis=-1, keepdims=True) + EPS)
    return c * r, r


def _mla_pre(zm, tabs, gq, gkv):
    def body(rows, consts, outs, accs):
        cq, _ = _rms(rows[0][...])
        ckv, _ = _rms(rows[1][...])
        outs[0][...] = (cq * consts[0][...]).astype(BF16)
        outs[1][...] = (ckv * consts[1][...]).astype(BF16)
        outs[2][...] = _rope(rows[2][...], rows[3][...], rows[4][...], rows[5][...])

    rows = [(zm, 256, 0), (zm, 256, 1), (zm, 128, 4)] + [_full(t) for t in tabs]
    return _rowwise("mla_pre", body, rows, [gq, gkv], [(256, BF16), (256, BF16), (HEAD_W, F32)])


def _mla_pre_bwd(zm, tabs, gq, gkv, dcqn, dckvn, dk):
    def body(rows, consts, outs, accs):
        res = []
        for k in range(2):
            ch, r = _rms(rows[k][...])
            d = rows[5 + k][...]
            accs[k][...] += jnp.sum(d * ch, axis=0, keepdims=True)
            dc = d * consts[k][...]
            res.append(r * (dc - ch * jnp.mean(dc * ch, axis=-1, keepdims=True)))
        dks = rows[7][:, 0:HEAD_W]
        for h in range(1, HEADS):
            dks = dks + rows[7][:, h * HEAD_W:(h + 1) * HEAD_W]
        lane = lax.broadcasted_iota(jnp.int32, dks.shape, 1)
        dks = jnp.where((lane >= 64) & (lane < 96), dks, 0.0)
        dkr = _rope_t(dks, rows[2][...], rows[3][...], rows[4][...])
        outs[0][:, 0:256] = res[0].astype(BF16)
        outs[0][:, 256:512] = res[1].astype(BF16)
        outs[0][:, 512:640] = dkr.astype(BF16)

    rows = [(zm, 256, 0), (zm, 256, 1)] + [_full(t) for t in tabs] + [_full(dcqn), _full(dckvn), _full(dk)]
    return _rowwise("mla_pre_bwd", body, rows, [gq, gkv], [(640, BF16)], [((1, 256), F32), ((1, 256), F32)])


def _rope_heads(x, c, s1, s2, fn):
    return jnp.concatenate([fn(x[:, h * HEAD_W:(h + 1) * HEAD_W], c, s1, s2) for h in range(HEADS)], axis=1)


def _unrope_heads(dq, tabs):
    def body(rows, consts, outs, accs):
        outs[0][...] = _rope_heads(rows[0][...], rows[1][...], rows[2][...], rows[3][...], _rope_t).astype(BF16)

    return _rowwise("l0_dq_rope", body, [_full(dq)] + [_full(t) for t in tabs], [], [(HEADS * HEAD_W, BF16)])


def _attn_block(T):
    return min(256, T)


def _attn_fwd(q, k, v):
    T = q.shape[0]
    BQ = _attn_block(T)
    nq = T // BQ

    def kern(q_ref, k_ref, v_ref, o_ref, lse_ref):
        def step(i, j, carry, masked):
            m, l, acc = carry
            qb = q_ref[pl.ds(pl.multiple_of(i * BQ, BQ), BQ), :]
            kb = k_ref[pl.ds(pl.multiple_of(j * BQ, BQ), BQ), :]
            vb = v_ref[pl.ds(pl.multiple_of(j * BQ, BQ), BQ), :]
            s = _dot(qb, kb, NT) * MLA_SCALE
            if masked:
                row = lax.broadcasted_iota(jnp.int32, s.shape, 0)
                col = lax.broadcasted_iota(jnp.int32, s.shape, 1)
                s = jnp.where(col <= row, s, -1e30)
            m_new = jnp.maximum(m, jnp.max(s, axis=-1, keepdims=True))
            p = jnp.exp(s - m_new)
            a = jnp.exp(m - m_new)
            l = a * l + jnp.sum(p, axis=-1, keepdims=True)
            acc = a * acc + _dot(p.astype(BF16), vb, NN)
            return m_new, l, acc

        def qloop(i, _):
            init = (jnp.full((BQ, 1), -1e30, F32), jnp.zeros((BQ, 1), F32), jnp.zeros((BQ, HEAD_W), F32))
            carry = lax.fori_loop(0, i, lambda j, c: step(i, j, c, False), init)
            m, l, acc = step(i, i, carry, True)
            rows = pl.ds(pl.multiple_of(i * BQ, BQ), BQ)
            o_ref[rows, :] = acc / l
            lse_ref[0, rows, :] = m + jnp.log(l)
            return 0

        lax.fori_loop(0, nq, qloop, 0)

    head = pl.BlockSpec((T, HEAD_W), lambda h: (0, h))
    nbytes = 3 * _nbytes((T, HEAD_W), BF16) + _nbytes((T, HEAD_W), F32) + _nbytes((T, 128), F32)
    return pl.pallas_call(
        kern, name="attn_fwd", grid=(HEADS,), in_specs=[head, head, head],
        out_specs=[head, pl.BlockSpec((1, T, 1), lambda h: (h, 0, 0))],
        out_shape=[jax.ShapeDtypeStruct((T, HEADS * HEAD_W), F32), jax.ShapeDtypeStruct((HEADS, T, 1), F32)],
        compiler_params=pltpu.CompilerParams(dimension_semantics=("parallel",), vmem_limit_bytes=_vmem(nbytes)),
    )(q, k, v)


def _attn_bwd(q, k, v, o, lse, dcat):
    T = q.shape[0]
    BQ = _attn_block(T)
    nq = T // BQ

    def kern(q_ref, k_ref, v_ref, o_ref, lse_ref, do_ref, dq_ref, dk_ref, dv_ref, dd_ref):
        dq_ref[...] = jnp.zeros(dq_ref.shape, F32)

        def dloop(i, _):
            rows = pl.ds(pl.multiple_of(i * BQ, BQ), BQ)
            dd_ref[rows, :] = jnp.sum(do_ref[rows, :].astype(F32) * o_ref[rows, :], axis=-1, keepdims=True)
            return 0

        lax.fori_loop(0, nq, dloop, 0)

        def step(j, i, carry, masked):
            dk_acc, dv_acc = carry
            rq = pl.ds(pl.multiple_of(i * BQ, BQ), BQ)
            rk = pl.ds(pl.multiple_of(j * BQ, BQ), BQ)
            qb, kb, vb, dob = q_ref[rq, :], k_ref[rk, :], v_ref[rk, :], do_ref[rq, :]
            s = _dot(qb, kb, NT) * MLA_SCALE
            p = jnp.exp(s - lse_ref[0, rq, :])
            if masked:
                row = lax.broadcasted_iota(jnp.int32, s.shape, 0)
                col = lax.broadcasted_iota(jnp.int32, s.shape, 1)
                p = jnp.where(col <= row, p, 0.0)
            dp = _dot(dob, vb, NT)
            ds = (p * (dp - dd_ref[rq, :]) * MLA_SCALE).astype(BF16)
            dv_acc = dv_acc + _dot(p.astype(BF16), dob, TN_)
            dk_acc = dk_acc + _dot(ds, qb, TN_)
            dq_ref[rq, :] += _dot(ds, kb, NN)
            return dk_acc, dv_acc

        def kloop(j, _):
            init = (jnp.zeros((BQ, HEAD_W), F32), jnp.zeros((BQ, HEAD_W), F32))
            carry = step(j, j, init, True)
            dk_acc, dv_acc = lax.fori_loop(j + 1, nq, lambda i, c: step(j, i, c, False), carry)
            rk = pl.ds(pl.multiple_of(j * BQ, BQ), BQ)
            dk_ref[rk, :] = dk_acc
            dv_ref[rk, :] = dv_acc
            return 0

        lax.fori_loop(0, nq, kloop, 0)

    head = pl.BlockSpec((T, HEAD_W), lambda h: (0, h))
    nbytes = 4 * _nbytes((T, HEAD_W), BF16) + 5 * _nbytes((T, HEAD_W), F32) + 2 * _nbytes((T, 128), F32)
    return pl.pallas_call(
        kern, name="attn_bwd", grid=(HEADS,),
        in_specs=[head, head, head, head, pl.BlockSpec((1, T, 1), lambda h: (h, 0, 0)), head],
        out_specs=[head, head, head],
        out_shape=[jax.ShapeDtypeStruct((T, HEADS * HEAD_W), F32)] * 3,
        scratch_shapes=[pltpu.VMEM((T, 1), F32)],
        compiler_params=pltpu.CompilerParams(dimension_semantics=("parallel",), vmem_limit_bytes=_vmem(nbytes)),
    )(q, k, v, o, lse, dcat)


def _sgu_common(u, v, ln_g, ln_b):
    ua, tu = _gelu(u)
    va, tv = _gelu(v)
    vh, r = _ln_stats(va)
    return ua, tu, tv, vh, r, vh * ln_g + ln_b


def _tril_mask(n):
    return lax.broadcasted_iota(jnp.int32, (n, n), 1) <= lax.broadcasted_iota(jnp.int32, (n, n), 0)


def _sgu_fwd(zs, ln_g, ln_b, w, bias_full):
    def body(rows, consts, outs, accs):
        ua, _, _, _, _, vn = _sgu_common(rows[0][...], rows[1][...], consts[0][...], consts[1][...])
        vn = vn.astype(BF16)
        tri = _tril_mask(SGU_CHUNK)
        for g in range(SGU_G):
            wg = jnp.where(tri, consts[2][0, g], 0.0).astype(BF16)
            cols = slice(g * 128, (g + 1) * 128)
            for c in range(ua.shape[0] // SGU_CHUNK):
                rws = slice(c * SGU_CHUNK, (c + 1) * SGU_CHUNK)
                mixed = _dot(wg, vn[rws, cols], NN) + consts[3][:, cols]
                outs[0][rws, cols] = (ua[rws, cols] * mixed).astype(BF16)

    return _rowwise("sgu_fwd", body, [(zs, 512, 0), (zs, 512, 1)], [ln_g, ln_b, w, bias_full], [(SGU_DIM, BF16)])


def _sgu_bwd(zs, dcat, ln_g, ln_b, w, bias_full):
    def body(rows, consts, outs, accs):
        u, v = rows[0][...], rows[1][...]
        ua, tu, tv, vh, r, vn = _sgu_common(u, v, consts[0][...], consts[1][...])
        dout = rows[2][...].astype(F32)
        vn_bf = vn.astype(BF16)
        tri = _tril_mask(SGU_CHUNK)
        dmixed = (dout * ua)
        dmixed_bf = dmixed.astype(BF16)
        ones = jnp.ones((8, SGU_CHUNK), F32)
        dvn_cols, mixed_cols = [], []
        for g in range(SGU_G):
            wg = jnp.where(tri, consts[2][0, g], 0.0).astype(BF16)
            cols = slice(g * 128, (g + 1) * 128)
            dvn_rows, mixed_rows = [], []
            dw = jnp.zeros((SGU_CHUNK, SGU_CHUNK), F32)
            dmix_sum = jnp.zeros((SGU_CHUNK, 128), F32)
            for c in range(u.shape[0] // SGU_CHUNK):
                rws = slice(c * SGU_CHUNK, (c + 1) * SGU_CHUNK)
                mixed_rows.append(_dot(wg, vn_bf[rws, cols], NN) + consts[3][:, cols])
                dvn_rows.append(_dot(wg, dmixed_bf[rws, cols], TN_))
                dw = dw + _dot(dmixed_bf[rws, cols], vn_bf[rws, cols], NT)
                dmix_sum = dmix_sum + dmixed[rws, cols]
            accs[0][g] += jnp.where(tri, dw, 0.0)
            accs[3][g:g + 1, :] += _dot(ones, dmix_sum, NT, precision=HIGHEST)[0:1, :]
            dvn_cols.append(jnp.concatenate(dvn_rows, axis=0))
            mixed_cols.append(jnp.concatenate(mixed_rows, axis=0))
        dvn = jnp.concatenate(dvn_cols, axis=1)
        mixed = jnp.concatenate(mixed_cols, axis=1)
        accs[1][...] += jnp.sum(dvn * vh, axis=0, keepdims=True)
        accs[2][...] += jnp.sum(dvn, axis=0, keepdims=True)
        dvh = dvn * consts[0][...]
        dva = r * (dvh - jnp.mean(dvh, axis=-1, keepdims=True) - vh * jnp.mean(dvh * vh, axis=-1, keepdims=True))
        outs[0][:, 0:512] = (dout * mixed * _gelu_grad(u, tu)).astype(BF16)
        outs[0][:, 512:1024] = (dva * _gelu_grad(v, tv)).astype(BF16)

    return _rowwise("sgu_bwd", body, [(zs, 512, 0), (zs, 512, 1), (dcat, 512, 2)], [ln_g, ln_b, w, bias_full], [(1024, BF16)],
                    [((SGU_G, 128, 128), F32), ((1, SGU_DIM), F32), ((1, SGU_DIM), F32), ((SGU_G, 128), F32)], tr=256)


def _lower_bound(hg_lb):
    a0, a1 = hg_lb[0:1, :], hg_lb[1:2, :]
    m = jnp.maximum(a0, a1)
    e0, e1 = jnp.exp(a0 - m), jnp.exp(a1 - m)
    s0, s1 = e0 / (e0 + e1), e1 / (e0 + e1)
    return (s0 + s1) - s0, s0, s1


def _hg_gates(qr, fr, lb):
    C = qr.shape[0]
    sq = _sig(qr)
    qf = qr * sq
    sf = _sig(fr)
    gate = lb + (1.0 - lb) * sf
    kk = 1.0 - gate
    tri = _tril_mask(C)
    b = _dot(jnp.where(tri, 1.0, 0.0), jnp.log(gate), NN, precision=HIGHEST)
    bref = b[C // 2 - 1:C // 2, :]
    bl = b[C - 1:C, :]
    e_b = jnp.exp(b)
    e_q = jnp.exp(b - bref)
    e_k = jnp.exp(bref - b)
    e_lb = jnp.exp(bl - b)
    return dict(sq=sq, qf=qf, sf=sf, gate=gate, kk=kk, tri=tri, bl=bl, e_b=e_b, e_q=e_q, e_k=e_k, e_lb=e_lb)


def _hgrn_fwd(z1, hg_lb, gnorm):
    T = z1.shape[0]
    C = min(HG_CHUNK, T)
    nc = T // C

    def kern(q_ref, f_ref, i_ref, g_ref, lb_ref, gn_ref, o_ref, hg_ref, st_ref, s_scr):
        @pl.when(pl.program_id(0) == 0)
        def _():
            s_scr[...] = jnp.zeros(s_scr.shape, F32)

        lb_all, _, _ = _lower_bound(lb_ref[...])
        st_ref[0] = s_scr[...]
        for h in range(HEADS):
            cols = slice(h * HEAD_W, (h + 1) * HEAD_W)
            t = _hg_gates(q_ref[:, cols], f_ref[:, cols], lb_all[:, cols])
            v = i_ref[:, cols]
            v_bf = v.astype(BF16)
            st = s_scr[h]
            a = jnp.where(t["tri"], _dot((t["qf"] * t["e_q"]).astype(BF16), (t["kk"] * t["e_k"]).astype(BF16), NT), 0.0)
            o = _dot(a.astype(BF16), v_bf, NN) + _dot((t["qf"] * t["e_b"]).astype(BF16), st.astype(BF16), NT)
            s_scr[h] = st * jnp.exp(t["bl"]) + _dot(v_bf, (t["kk"] * t["e_lb"]).astype(BF16), TN_)
            o_ref[:, cols] = o
            gr = g_ref[:, cols]
            r = lax.rsqrt(jnp.mean(o * o, axis=-1, keepdims=True) + EPS)
            hg_ref[:, cols] = (o * r * gn_ref[:, cols] * (gr * _sig(gr))).astype(BF16)

    seg = lambda k: pl.BlockSpec((C, D_MODEL), functools.partial(lambda n, k: (n, k), k=k))
    row = pl.BlockSpec((C, D_MODEL), lambda n: (n, 0))
    nbytes = 6 * _nbytes((C, D_MODEL), F32) + 3 * _nbytes((HEADS, 128, 128), F32)
    return pl.pallas_call(
        kern, name="hgrn_fwd", grid=(nc,),
        in_specs=[seg(0), seg(1), seg(2), seg(3), pl.BlockSpec((2, D_MODEL), lambda n: (0, 0)),
                  pl.BlockSpec((1, D_MODEL), lambda n: (0, 0))],
        out_specs=[row, row, pl.BlockSpec((1, HEADS, 128, 128), lambda n: (n, 0, 0, 0))],
        out_shape=[jax.ShapeDtypeStruct((T, D_MODEL), F32), jax.ShapeDtypeStruct((T, D_MODEL), BF16),
                   jax.ShapeDtypeStruct((nc, HEADS, 128, 128), F32)],
        scratch_shapes=[pltpu.VMEM((HEADS, 128, 128), F32)],
        compiler_params=pltpu.CompilerParams(dimension_semantics=("arbitrary",), vmem_limit_bytes=_vmem(nbytes)),
    )(z1, z1, z1, z1, hg_lb, gnorm)


def _hgrn_bwd(z1, o_pre, dhg, states, hg_lb, gnorm):
    T = z1.shape[0]
    C = min(HG_CHUNK, T)
    nc = T // C

    def kern(q_ref, f_ref, i_ref, g_ref, o_ref, dhg_ref, st_ref, lb_ref, gn_ref, dz_ref, dlb_ref, dgn_ref, ds_scr, dlb_scr):
        n = pl.program_id(0)

        @pl.when(n == 0)
        def _():
            ds_scr[...] = jnp.zeros(ds_scr.shape, F32)
            dlb_scr[...] = jnp.zeros(dlb_scr.shape, F32)
            dgn_ref[...] = jnp.zeros(dgn_ref.shape, F32)

        lb_all, s0, s1 = _lower_bound(lb_ref[...])
        for h in range(HEADS):
            cols = slice(h * HEAD_W, (h + 1) * HEAD_W)
            lb = lb_all[:, cols]
            qr, fr = q_ref[:, cols], f_ref[:, cols]
            t = _hg_gates(qr, fr, lb)
            tri = t["tri"]
            v_bf = i_ref[:, cols].astype(BF16)
            st_bf = st_ref[0, h].astype(BF16)
            dst = ds_scr[h]
            dst_bf = dst.astype(BF16)
            o = o_ref[:, cols]
            gr = g_ref[:, cols]
            sg = _sig(gr)
            sil = gr * sg
            gn = gn_ref[:, cols]
            r = lax.rsqrt(jnp.mean(o * o, axis=-1, keepdims=True) + EPS)
            on = o * r
            dh = dhg_ref[:, cols].astype(F32)
            dgn_ref[:, cols] += jnp.sum(dh * on * sil, axis=0, keepdims=True)
            dg = dh * on * gn * (sg * (1.0 + gr * (1.0 - sg)))
            don = dh * gn * sil
            do_bf = (r * (don - on * jnp.mean(don * on, axis=-1, keepdims=True))).astype(BF16)
            qe = (t["qf"] * t["e_q"]).astype(BF16)
            ke = (t["kk"] * t["e_k"]).astype(BF16)
            qb = (t["qf"] * t["e_b"]).astype(BF16)
            kh_bf = (t["kk"] * t["e_lb"]).astype(BF16)
            a_bf = jnp.where(tri, _dot(qe, ke, NT), 0.0).astype(BF16)
            da_bf = jnp.where(tri, _dot(do_bf, v_bf, NT), 0.0).astype(BF16)
            dv = _dot(a_bf, do_bf, TN_) + _dot(kh_bf, dst_bf, NT)
            dqe = _dot(da_bf, ke, NN)
            dqb = _dot(do_bf, st_bf, NN)
            dke = _dot(da_bf, qe, TN_)
            dkh = _dot(v_bf, dst_bf, NN)
            dqf = dqe * t["e_q"] + dqb * t["e_b"]
            dkk = dke * t["e_k"] + dkh * t["e_lb"]
            kh_r = kh_bf.astype(F32)
            db = qe.astype(F32) * dqe - ke.astype(F32) * dke + qb.astype(F32) * dqb - kh_r * dkh
            e_bl = jnp.exp(t["bl"])
            dbl = jnp.sum(dkh * kh_r, axis=0, keepdims=True) + e_bl * jnp.sum(st_ref[0, h] * dst, axis=0, keepdims=True)
            dlg = _dot(jnp.where(tri, 1.0, 0.0), db, TN_, precision=HIGHEST) + dbl
            ds_scr[h] = dst * e_bl + _dot(do_bf, qb, TN_)
            dgate = dlg / t["gate"] - dkk
            sf = t["sf"]
            dlb_scr[:, cols] += jnp.sum(dgate * (1.0 - sf), axis=0, keepdims=True)
            df = dgate * (1.0 - lb) * sf * (1.0 - sf)
            dq = dqf * (t["sq"] * (1.0 + qr * (1.0 - t["sq"])))
            dz_ref[:, cols] = dq.astype(BF16)
            dz_ref[:, D_MODEL + h * HEAD_W:D_MODEL + (h + 1) * HEAD_W] = df.astype(BF16)
            dz_ref[:, 2 * D_MODEL + h * HEAD_W:2 * D_MODEL + (h + 1) * HEAD_W] = dv.astype(BF16)
            dz_ref[:, 3 * D_MODEL + h * HEAD_W:3 * D_MODEL + (h + 1) * HEAD_W] = dg.astype(BF16)

        @pl.when(n == nc - 1)
        def _():
            d = s0 * s1 * dlb_scr[...]
            dlb_ref[0:1, :] = -d
            dlb_ref[1:2, :] = d

    seg = lambda k: pl.BlockSpec((C, D_MODEL), functools.partial(lambda n, k: (nc - 1 - n, k), k=k))
    nbytes = 6 * _nbytes((C, D_MODEL), F32) + _nbytes((C, 4 * D_MODEL), BF16) + 3 * _nbytes((HEADS, 128, 128), F32)
    return pl.pallas_call(
        kern, name="hgrn_bwd", grid=(nc,),
        in_specs=[seg(0), seg(1), seg(2), seg(3), seg(0), seg(0),
                  pl.BlockSpec((1, HEADS, 128, 128), lambda n: (nc - 1 - n, 0, 0, 0)),
                  pl.BlockSpec((2, D_MODEL), lambda n: (0, 0)), pl.BlockSpec((1, D_MODEL), lambda n: (0, 0))],
        out_specs=[pl.BlockSpec((C, 4 * D_MODEL), lambda n: (nc - 1 - n, 0)),
                   pl.BlockSpec((2, D_MODEL), lambda n: (0, 0)), pl.BlockSpec((1, D_MODEL), lambda n: (0, 0))],
        out_shape=[jax.ShapeDtypeStruct((T, 4 * D_MODEL), BF16), jax.ShapeDtypeStruct((2, D_MODEL), F32),
                   jax.ShapeDtypeStruct((1, D_MODEL), F32)],
        scratch_shapes=[pltpu.VMEM((HEADS, 128, 128), F32), pltpu.VMEM((1, D_MODEL), F32)],
        compiler_params=pltpu.CompilerParams(dimension_semantics=("arbitrary",), vmem_limit_bytes=_vmem(nbytes)),
    )(z1, z1, z1, z1, o_pre, dhg, states, hg_lb, gnorm)


def _prep_weights(gw):
    w_in_e = gw["w_in_e"].transpose(1, 0, 2).reshape(D_MODEL, 1568)
    kr = jnp.pad(w_in_e[:, 512:544], ((0, 0), (64, 32)))
    wm = jnp.concatenate([w_in_e[:, 0:512], kr], axis=1)
    ws = w_in_e[:, 544:1568]
    w_qb = gw["w_qb"].transpose(1, 0, 2).reshape(MLA_LORA, HEADS, 96)
    wq = jnp.pad(w_qb, ((0, 0), (0, 0), (0, 32))).reshape(MLA_LORA, HEADS * HEAD_W)
    kvb = gw["w_kvb"].transpose(1, 0, 2).reshape(MLA_LORA, HEADS, 128)
    wk = jnp.pad(kvb[:, :, :64], ((0, 0), (0, 0), (0, 64))).reshape(MLA_LORA, HEADS * HEAD_W)
    wv = jnp.pad(kvb[:, :, 64:], ((0, 0), (0, 0), (0, 64))).reshape(MLA_LORA, HEADS * HEAD_W)
    w_out_e = gw["w_out_e"].reshape(D_MODEL, D_MODEL)
    woa = jnp.pad(w_out_e[:512].reshape(HEADS, 64, D_MODEL), ((0, 0), (0, 64), (0, 0))).reshape(HEADS * HEAD_W, D_MODEL)
    return dict(wm=wm, ws=ws, wq=wq, wk=wk, wv=wv, woa=woa, wob=w_out_e[512:])


def _unprep_grads(g):
    dwm, dws = g["wm"], g["ws"]
    d_in_e = jnp.concatenate([dwm[:, 0:512], dwm[:, 512 + 64:512 + 96], dws], axis=1)
    d_qb = g["wq"].reshape(MLA_LORA, HEADS, HEAD_W)[:, :, :96].reshape(MLA_LORA, HEADS * 96)
    dk = g["wk"].reshape(MLA_LORA, HEADS, HEAD_W)[:, :, :64]
    dv = g["wv"].reshape(MLA_LORA, HEADS, HEAD_W)[:, :, :64]
    d_kvb = jnp.concatenate([dk, dv], axis=2).reshape(MLA_LORA, HEADS * 128)
    d_oa = g["woa"].reshape(HEADS, HEAD_W, D_MODEL)[:, :64].reshape(HEADS * 64, D_MODEL)
    dev_major = lambda a: a.reshape(a.shape[0], N_DEV, a.shape[1] // N_DEV).transpose(1, 0, 2)
    return dict(w_in_e=dev_major(d_in_e), w_qb=dev_major(d_qb), w_kvb=dev_major(d_kvb),
                w_out_e=jnp.concatenate([d_oa, g["wob"]], axis=0).reshape(N_DEV, D_MODEL // N_DEV, D_MODEL))


def _local_step(x, positions, target, gw, sp):
    w = _prep_weights(gw)
    T = x.shape[0]
    tm = min(TM, T)
    nt = T // tm
    half = MLA_ROPE // 2
    inv_freq = ROPE_BASE ** (-jnp.arange(half, dtype=F32) / half)
    invf_lane = jnp.concatenate([jnp.zeros((64,), F32), inv_freq, inv_freq, jnp.zeros((32,), F32)]).reshape(1, HEAD_W)
    tabs = _rope_tables(positions.reshape(T, 1), invf_lane)
    bias_full = jnp.repeat(sp["sgu_b"][0].T, 128, axis=1)
    sgu_w = sp["sgu_w"]
    gq, gkv = sp["mla_gq"], sp["mla_gkv"]
    ln1_g, ln1_b, ln2_g, ln2_b = sp["ln1_g"], sp["ln1_b"], sp["ln2_g"], sp["ln2_b"]
    w_in_o, w_out_o = gw["w_in_o"], gw["w_out_o"].reshape(D_MODEL, D_MODEL)
    wide = HEADS * HEAD_W
    tab_rows = [_rb(t, tm) for t in tabs]
    resid = lambda acc, d: (acc + ALPHA * d,)

    zm = _tiled("l0_in_mla", (1, nt), [_rb(x, tm), _cw(w["wm"], 640)], [_out(T, 640, F32, tm, 640)], _mmc(NN))
    zs = _tiled("l0_in_sgu", (2, nt), [_rb(x, tm), _cw(w["ws"], TN)], [_out(T, 1024, F32, tm, TN)], _mmc(NN))
    cqn, ckvn, kr_rot = _mla_pre(zm, tabs, gq, gkv)
    q = _tiled("l0_q", (1, nt), [_rb(cqn, tm), _cw(w["wq"], wide)] + tab_rows, [_out(T, wide, BF16, tm, wide)],
               _mmc(NN, epilogue=lambda acc, c, s1, s2: (_rope_heads(acc, c, s1, s2, _rope),)))
    k = _tiled("l0_k", (1, nt), [_rb(ckvn, tm), _cw(w["wk"], wide), _rb(kr_rot, tm)], [_out(T, wide, BF16, tm, wide)],
               _mmc(NN, epilogue=lambda acc, kr: (acc + jnp.concatenate([kr] * HEADS, axis=1),)))
    v = _tiled("l0_v", (1, nt), [_rb(ckvn, tm), _cw(w["wv"], wide)], [_out(T, wide, BF16, tm, wide)], _mmc(NN))
    o_att, lse = _attn_fwd(q, k, v)
    b_out = _sgu_fwd(zs, sp["sgu_ln_g"], sp["sgu_ln_b"], sgu_w, bias_full)
    mix0 = _tiled("l0_out", (2, nt), [_rb(o_att, tm), _cw(w["woa"], TN), _rb(b_out, tm), _cw(w["wob"], TN)],
                  [_out(T, D_MODEL, F32, tm, TN)], _mmc(NN, n_pairs=2))
    y1, h1, h1_bf = _ln_fwd("l0_ln1", x, mix0, ln1_g, ln1_b, 0)
    a0, act0, ff0 = _mlp_fwd("l0", h1_bf, gw["w_ff1"][0], gw["w_ff2"][0])
    y2, h2, h2_bf = _ln_fwd("l0_ln2", h1, ff0, ln2_g, ln2_b, 0)

    z1 = _tiled("l1_in", (N_DEV, nt), [_rb(h2_bf, tm), _gcw(w_in_o)], [_out(T, 4 * D_MODEL, F32, tm, TN)], _mmc(NN))
    o_pre, hg, states = _hgrn_fwd(z1, sp["hg_lb"], sp["hg_gnorm"])
    mix1 = _tiled("l1_out", (2, nt), [_rb(hg, tm), _cw(w_out_o, TN)], [_out(T, D_MODEL, F32, tm, TN)], _mmc(NN))
    y3, h3, h3_bf = _ln_fwd("l1_ln1", h2, mix1, ln1_g, ln1_b, 1)
    a1, act1, ff1 = _mlp_fwd("l1", h3_bf, gw["w_ff1"][1], gw["w_ff2"][1])
    y4, dh4, sq_err = _ln_loss("l1_ln2", h3, ff1, ln2_g, ln2_b, 1, target)

    gs, g0 = {}, {}
    dy4, dy4_bf, gs["ln2_g1"], gs["ln2_b1"] = _ln_bwd("l1_ln2_bwd", y4, dh4, ln2_g, 1)
    dh3, _, dw1_1, dw2_1 = _mlp_bwd("l1", h3_bf, a1, act1, dy4_bf, dy4, gw["w_ff1"][1], gw["w_ff2"][1])
    dy3, dy3_bf, gs["ln1_g1"], gs["ln1_b1"] = _ln_bwd("l1_ln1_bwd", y3, dh3, ln1_g, 1)
    d_out_o = _tiled("l1_dwout", (2, D_MODEL // TM), [_tl(hg, TM), _cw(dy3_bf, TN)], [_out(D_MODEL, D_MODEL, F32, TM, TN)],
                     _mmc(TN_)).reshape(N_DEV, D_MODEL // N_DEV, D_MODEL)
    dhg = _tiled("l1_dhg", (2, nt), [_rb(dy3_bf, tm), _rw(w_out_o, TN)], [_out(T, D_MODEL, BF16, tm, TN)], _mmc(NT))
    dz1, gs["hg_lb"], gs["hg_gnorm"] = _hgrn_bwd(z1, o_pre, dhg, states, sp["hg_lb"], sp["hg_gnorm"])
    d_in_o = _tiled("l1_dwin", (N_DEV, D_MODEL // TM), [_tl(h2_bf, TM), _cw(dz1, TN)], [_out_dev(D_MODEL, TN, TM)], _mmc(TN_))
    dh2 = _tiled("l1_dh2", (2, nt), [_rb(dz1, tm), _grw(w_in_o, TN), _rbj(dy3, tm, TN)], [_out(T, D_MODEL, F32, tm, TN)],
                 _mmc_dev(epilogue=resid))

    dy2, dy2_bf, gs["ln2_g0"], gs["ln2_b0"] = _ln_bwd("l0_ln2_bwd", y2, dh2, ln2_g, 0)
    dh1, _, dw1_0, dw2_0 = _mlp_bwd("l0", h1_bf, a0, act0, dy2_bf, dy2, gw["w_ff1"][0], gw["w_ff2"][0])
    dy1, dy1_bf, gs["ln1_g0"], gs["ln1_b0"] = _ln_bwd("l0_ln1_bwd", y1, dh1, ln1_g, 0)
    g0["woa"] = _tiled("l0_dwoa", (2, wide // TM), [_tl(o_att, TM), _cw(dy1_bf, TN)], [_out(wide, D_MODEL, F32, TM, TN)], _mmc(TN_))
    g0["wob"] = _tiled("l0_dwob", (2, 1), [_tl(b_out, SGU_DIM), _cw(dy1_bf, TN)], [_out(SGU_DIM, D_MODEL, F32, SGU_DIM, TN)], _mmc(TN_))
    wo_cat = jnp.concatenate([w["woa"], w["wob"]], axis=0)
    dcat = _tiled("l0_dcat", (3, nt), [_rb(dy1_bf, tm), _rw(wo_cat, TN)], [_out(T, wide + SGU_DIM, BF16, tm, TN)], _mmc(NT))
    dzs, gs["sgu_w"], gs["sgu_ln_g"], gs["sgu_ln_b"], gs["sgu_b"] = _sgu_bwd(zs, dcat, sp["sgu_ln_g"], sp["sgu_ln_b"], sgu_w, bias_full)
    dq, dk, dv = _attn_bwd(q, k, v, o_att, lse, dcat)
    dq_pre = _unrope_heads(dq, tabs)
    lora_w = lambda name, a, d: _tiled(name, (wide // TN, 1), [_tl(a, MLA_LORA), _cw(d, TN)], [_out(MLA_LORA, wide, F32, MLA_LORA, TN)], _mmc(TN_))
    g0["wq"] = lora_w("l0_dwq", cqn, dq_pre)
    g0["wk"] = lora_w("l0_dwk", ckvn, dk)
    g0["wv"] = lora_w("l0_dwv", ckvn, dv)
    dcqn = _tiled("l0_dcqn", (1, nt), [_rb(dq_pre, tm), _rw(w["wq"], MLA_LORA)], [_out(T, MLA_LORA, F32, tm, MLA_LORA)], _mmc(NT))
    dckvn = _tiled("l0_dckvn", (1, nt), [_rb(dk, tm), _rw(w["wk"], MLA_LORA), _rb(dv, tm), _rw(w["wv"], MLA_LORA)],
                   [_out(T, MLA_LORA, F32, tm, MLA_LORA)], _mmc(NT, n_pairs=2))
    dzm, gs["mla_gq"], gs["mla_gkv"] = _mla_pre_bwd(zm, tabs, gq, gkv, dcqn, dckvn, dk)
    g0["wm"] = _tiled("l0_dwm", (1, D_MODEL // TM), [_tl(x, TM), _cw(dzm, 640)], [_out(D_MODEL, 640, F32, TM, 640)], _mmc(TN_))
    g0["ws"] = _tiled("l0_dws", (2, D_MODEL // TM), [_tl(x, TM), _cw(dzs, TN)], [_out(D_MODEL, 1024, F32, TM, TN)], _mmc(TN_))
    dx = _tiled("l0_dx", (2, nt), [_rb(dzm, tm), _rw(w["wm"], TN), _rb(dzs, tm), _rw(w["ws"], TN), _rbj(dy1, tm, TN)],
                [_out(T, D_MODEL, F32, tm, TN)], _mmc(NT, n_pairs=2, epilogue=resid))

    grads = _unprep_grads(g0)
    grads.update(w_in_o=d_in_o, w_out_o=d_out_o, w_ff1=[dw1_0, dw1_1], w_ff2=[dw2_0, dw2_1])
    return sq_err, dx, grads, gs


def _me():
    return lax.axis_index("x"), lax.axis_index("y"), lax.axis_index("c")


def _hbm_call(name, kern, operands, out_shape, n_sems, extra_scratch=()):
    any_spec = pl.BlockSpec(memory_space=pl.ANY)
    return pl.pallas_call(
        kern, name=name, out_shape=out_shape, in_specs=[any_spec] * len(operands), out_specs=[any_spec] * len(out_shape),
        scratch_shapes=[pltpu.SemaphoreType.DMA((n_sems,)), pltpu.SemaphoreType.DMA((n_sems,)), *extra_scratch],
    )(*operands)


def _all_gather(shards):
    n = len(shards)

    def kern(*refs):
        x_refs, out_refs, (send_sems, recv_sems, local_sems) = refs[:n], refs[n:2 * n], refs[2 * n:]
        x, y, c = _me()
        me, sibling = (x, y, c), (x, y, 1 - c)
        chips = [(1 - x, y), (x, 1 - y), (1 - x, 1 - y)]

        def copy(op, k, block, to, own=False):
            slot = out_refs[op].at[4 * block[0] + 2 * block[1] + block[2]]
            return pltpu.make_async_remote_copy(
                src_ref=x_refs[op] if own else slot, dst_ref=slot, send_sem=send_sems.at[7 * op + k],
                recv_sem=recv_sems.at[7 * op + k], device_id=to, device_id_type=MESH)

        mine = [pltpu.make_async_copy(x_refs[op], out_refs[op].at[4 * x + 2 * y + c], local_sems.at[op]) for op in range(n)]
        for cp in mine:
            cp.start()
        first = []
        for op in range(n):
            first.append(copy(op, 0, me, sibling, own=True))
            first += [copy(op, 1 + j, me, (*chip, c), own=True) for j, chip in enumerate(chips)]
        for cp in first:
            cp.start()
        passed = []
        for j, chip in enumerate(chips):
            for op in range(n):
                copy(op, 1 + j, (*chip, c), me).wait_recv()
                passed.append(copy(op, 4 + j, (*chip, c), sibling))
                passed[-1].start()
        for op in range(n):
            copy(op, 0, sibling, me).wait_recv()
            for j, chip in enumerate(chips):
                copy(op, 4 + j, (*chip, 1 - c), me).wait_recv()
        for cp in first + passed:
            cp.wait_send()
        for cp in mine:
            cp.wait()

    out_shape = [jax.ShapeDtypeStruct((N_DEV, *s.shape), s.dtype) for s in shards]
    return _hbm_call("weights_all_gather", kern, shards, out_shape, 7 * n, [pltpu.SemaphoreType.DMA((n,))])


def _rs_sibling(grads):
    n = len(grads)

    def kern(*refs):
        g_refs, out_refs, (send_sems, recv_sems) = refs[:n], refs[n:2 * n], refs[2 * n:]
        x, y, c = _me()
        copies = [pltpu.make_async_remote_copy(
            src_ref=g_refs[op].at[k, 1 - c], dst_ref=out_refs[op].at[k], send_sem=send_sems.at[4 * op + k],
            recv_sem=recv_sems.at[4 * op + k], device_id=(x, y, 1 - c), device_id_type=MESH) for op in range(n) for k in range(4)]
        for cp in copies:
            cp.start()
        for cp in copies:
            cp.wait()

    out_shape = [jax.ShapeDtypeStruct((4, *g.shape[2:]), g.dtype) for g in grads]
    return _hbm_call("grads_to_sibling", kern, grads, out_shape, 4 * n)


def _rs_chips(sums):
    n = len(sums)

    def kern(*refs):
        p_refs, out_refs, (send_sems, recv_sems) = refs[:n], refs[n:2 * n], refs[2 * n:]
        x, y, c = _me()
        chips = [(1 - x, y), (x, 1 - y), (1 - x, 1 - y)]
        copies = [pltpu.make_async_remote_copy(
            src_ref=p_refs[op].at[2 * cx + cy], dst_ref=out_refs[op].at[j], send_sem=send_sems.at[3 * op + j],
            recv_sem=recv_sems.at[3 * op + j], device_id=(cx, cy, c), device_id_type=MESH)
            for op in range(n) for j, (cx, cy) in enumerate(chips)]
        for cp in copies:
            cp.start()
        for cp in copies:
            cp.wait()

    out_shape = [jax.ShapeDtypeStruct((3, *p.shape[1:]), p.dtype) for p in sums]
    return _hbm_call("grads_between_chips", kern, sums, out_shape, 3 * n)


def _row_tile(r):
    return r if r <= 256 else 256


def _chip_sum(name, g, from_sibling, core):
    _, _, R, W = g.shape
    tr = _row_tile(R)

    def kern(core_ref, g_ref, s_ref, o_ref):
        o_ref[...] = (g_ref[...] + s_ref[...]).astype(BF16)

    return pl.pallas_call(
        kern, name=name, out_shape=jax.ShapeDtypeStruct((4, R, W), BF16),
        grid_spec=pltpu.PrefetchScalarGridSpec(
            num_scalar_prefetch=1, grid=(4, R // tr),
            in_specs=[pl.BlockSpec((None, None, tr, W), lambda k, i, core: (k, core[0], i, 0)),
                      pl.BlockSpec((None, tr, W), lambda k, i, core: (k, i, 0))],
            out_specs=pl.BlockSpec((None, tr, W), lambda k, i, core: (k, i, 0))),
        compiler_params=pltpu.CompilerParams(dimension_semantics=("parallel", "parallel"), vmem_limit_bytes=_vmem(3 * tr * W * 4)),
    )(core, g, from_sibling)


def _adamw(w, g, m, v):
    m = ADAM_B1 * m + (1.0 - ADAM_B1) * g
    v = ADAM_B2 * v + (1.0 - ADAM_B2) * (g * g)
    m_hat = m / (1.0 - ADAM_B1 ** ADAM_STEP)
    v_hat = v / (1.0 - ADAM_B2 ** ADAM_STEP)
    return -ADAM_LR * (m_hat / (jnp.sqrt(v_hat) + ADAM_EPS) + ADAM_WD * w), m, v


def _finish_sharded(name, layers, w, m, v, where):
    nl, R, W = w.shape
    tr = _row_tile(R)

    def kern(where_ref, *refs):
        w_ref, m_ref, v_ref, go_ref, d_ref, mo_ref, vo_ref = refs[3 * nl:]
        for l in range(nl):
            g_ref, s_ref, c_ref = refs[3 * l:3 * l + 3]
            grad = g_ref[...] + s_ref[...]
            for j in range(3):
                grad = grad + c_ref[j].astype(F32)
            go_ref[l] = grad
            d_ref[l], mo_ref[l], vo_ref[l] = _adamw(w_ref[l], grad, m_ref[l], v_ref[l])

    row = pl.BlockSpec((nl, tr, W), lambda i, wh: (0, i, 0))
    in_specs, args = [], []
    for g, s, c in layers:
        in_specs += [pl.BlockSpec((None, None, tr, W), lambda i, wh: (wh[0], wh[1], i, 0)),
                     pl.BlockSpec((None, tr, W), lambda i, wh: (wh[0], i, 0)),
                     pl.BlockSpec((3, tr, W), lambda i, wh: (0, i, 0))]
        args += [g, s, c]
    return pl.pallas_call(
        kern, name=name, out_shape=[jax.ShapeDtypeStruct((nl, R, W), F32)] * 4,
        grid_spec=pltpu.PrefetchScalarGridSpec(num_scalar_prefetch=1, grid=(R // tr,), in_specs=in_specs + [row, row, row],
                                               out_specs=[row, row, row, row]),
        compiler_params=pltpu.CompilerParams(dimension_semantics=("parallel",), vmem_limit_bytes=_vmem(nl * 11 * tr * W * 4)),
    )(where, *args, w, m, v)


SMALL_PLACE = (("mla_gq", 0, 0, 1, 256), ("mla_gkv", 0, 256, 1, 256), ("sgu_ln_g", 0, 512, 1, 512), ("sgu_ln_b", 1, 0, 1, 512),
               ("hg_lb", 2, 0, 2, 1024), ("ln1_g", 4, 0, 2, 1024), ("ln1_b", 6, 0, 2, 1024), ("sgu_b", 8, 0, 4, 128),
               ("ln2_g", 12, 0, 2, 1024), ("ln2_b", 14, 0, 2, 1024), ("hg_gnorm", 16, 0, 1, 1024))
SMALL_BUF_ROWS = 24


def _small_reduce_adamw(gs, given):
    pieces = [(gs["mla_gq"], 0, 0), (gs["mla_gkv"], 0, 256), (gs["sgu_ln_g"], 0, 512), (gs["sgu_ln_b"], 1, 0), (gs["hg_lb"], 2, 0),
              (gs["ln1_g0"], 4, 0), (gs["ln1_g1"], 5, 0), (gs["ln1_b0"], 6, 0), (gs["ln1_b1"], 7, 0), (gs["sgu_b"], 8, 0),
              (gs["ln2_g0"], 12, 0), (gs["ln2_g1"], 13, 0), (gs["ln2_b0"], 14, 0), (gs["ln2_b1"], 15, 0), (gs["hg_gnorm"], 16, 0)]
    names = [p[0] for p in SMALL_PLACE] + ["sgu_w"]
    n_p, n_names = len(pieces), len(names)
    wmv = [given[pre + name] for name in names for pre in ("", "m_", "v_")]

    def kern(*refs):
        piece_refs, gw_ref = refs[:n_p], refs[n_p]
        wmv_refs = refs[n_p + 1:n_p + 1 + 3 * n_names]
        out_refs = refs[n_p + 1 + 3 * n_names:n_p + 1 + 7 * n_names]
        buf_a, buf_b, send_sems, recv_sems = refs[n_p + 1 + 7 * n_names:]
        px, py, pc = _me()
        me = 4 * px + 2 * py + pc
        mine_a, mine_b = buf_a.at[me], buf_b.at[me]
        mine_a[...] = jnp.zeros(mine_a.shape, F32)
        for ref, (_, r, l0) in zip(piece_refs, pieces):
            mine_a[r:r + ref.shape[0], l0:l0 + ref.shape[1]] = ref[...]
        mine_b[...] = gw_ref[...]
        copies = []
        for r in range(1, N_DEV):
            peer = (px ^ (r >> 2), py ^ ((r >> 1) & 1), pc ^ (r & 1))
            for k, mine in enumerate((mine_a, mine_b)):
                copies.append(pltpu.make_async_remote_copy(
                    src_ref=mine, dst_ref=mine, send_sem=send_sems.at[2 * (r - 1) + k], recv_sem=recv_sems.at[2 * (r - 1) + k],
                    device_id=peer, device_id_type=MESH))
        for cp in copies:
            cp.start()
        for r in range(1, N_DEV):
            for k, buf in enumerate((buf_a, buf_b)):
                theirs = buf.at[me ^ r]
                pltpu.make_async_remote_copy(
                    src_ref=theirs, dst_ref=theirs, send_sem=send_sems.at[2 * (r - 1) + k], recv_sem=recv_sems.at[2 * (r - 1) + k],
                    device_id=(px, py, pc), device_id_type=MESH).wait_recv()
        for cp in copies:
            cp.wait_send()
        sum_a, sum_b = buf_a[0], buf_b[0]
        for d in range(1, N_DEV):
            sum_a, sum_b = sum_a + buf_a[d], sum_b + buf_b[d]

        def own_block(full):
            acc = full[:, 0:128]
            for b in range(1, N_DEV):
                acc = jnp.where(me == b, full[:, b * 128:(b + 1) * 128], acc)
            return acc

        for idx, name in enumerate(names):
            w_ref, m_ref, v_ref = wmv_refs[3 * idx:3 * idx + 3]
            if name == "sgu_w":
                grad = sum_b[None]
            else:
                _, r, l0, nr, nl = SMALL_PLACE[idx]
                grad = sum_a[r:r + nr, l0:l0 + nl]
                if name == "hg_gnorm":
                    grad = own_block(grad)
                if name == "sgu_b":
                    grad = grad[None]
            res = (grad, *_adamw(w_ref[...], grad, m_ref[...], v_ref[...]))
            for o_ref, val in zip(out_refs[4 * idx:4 * idx + 4], res):
                o_ref[...] = val

    vmem = pl.BlockSpec(memory_space=pltpu.VMEM)
    operands = [p[0] for p in pieces] + [gs["sgu_w"]] + wmv
    out_shape = [jax.ShapeDtypeStruct(given[name].shape, F32) for name in names for _ in range(4)]
    res = pl.pallas_call(
        kern, name="small_all_reduce_adamw", out_shape=out_shape, in_specs=[vmem] * len(operands), out_specs=[vmem] * len(out_shape),
        scratch_shapes=[pltpu.VMEM((N_DEV, SMALL_BUF_ROWS, D_MODEL), F32), pltpu.VMEM((N_DEV, SGU_G, 128, 128), F32),
                        pltpu.SemaphoreType.DMA((14,)), pltpu.SemaphoreType.DMA((14,))],
    )(*operands)
    return {name: res[4 * idx:4 * idx + 4] for idx, name in enumerate(names)}


SHARDED = ("w_in_e", "w_qb", "w_kvb", "w_out_e", "w_in_o", "w_out_o", "w_ff1", "w_ff2")


def kernel(x, positions, w_in_e, mla_gq, mla_gkv, w_qb, w_kvb, sgu_ln_g, sgu_ln_b, sgu_w, sgu_b, w_out_e, w_in_o, hg_lb, hg_gnorm, w_out_o, ln1_g, ln1_b, w_ff1, w_ff2, ln2_g, ln2_b, loss_target, m_w_in_e, m_mla_gq, m_mla_gkv, m_w_qb, m_w_kvb, m_sgu_ln_g, m_sgu_ln_b, m_sgu_w, m_sgu_b, m_w_out_e, m_w_in_o, m_hg_lb, m_hg_gnorm, m_w_out_o, m_ln1_g, m_ln1_b, m_w_ff1, m_w_ff2, m_ln2_g, m_ln2_b, v_w_in_e, v_mla_gq, v_mla_gkv, v_w_qb, v_w_kvb, v_sgu_ln_g, v_sgu_ln_b, v_sgu_w, v_sgu_b, v_w_out_e, v_w_in_o, v_hg_lb, v_hg_gnorm, v_w_out_o, v_ln1_g, v_ln1_b, v_w_ff1, v_w_ff2, v_ln2_g, v_ln2_b):
    given = dict(locals())
    px, py, pc = _me()

    names = ["w_in_e", "w_qb", "w_kvb", "w_out_e", "w_in_o", "w_out_o"]
    shards = [given[n][0].astype(BF16) for n in names]
    shards += [w_ff1[0].astype(BF16), w_ff1[1].astype(BF16), w_ff2[0].astype(BF16), w_ff2[1].astype(BF16), hg_gnorm]
    got = _all_gather(shards)
    gw = dict(zip(names, got[:6]))
    gw["w_ff1"], gw["w_ff2"] = [got[6], got[7]], [got[8], got[9]]
    small_names = ["mla_gq", "mla_gkv", "sgu_ln_g", "sgu_ln_b", "sgu_w", "sgu_b", "hg_lb", "ln1_g", "ln1_b", "ln2_g", "ln2_b"]
    sp = {n: given[n] for n in small_names}
    sp["hg_gnorm"] = got[10].reshape(1, D_MODEL)

    sq_err, dx, grads, gs = _local_step(x[0], positions[0], loss_target[0], gw, sp)
    loss = lax.psum(0.5 * jnp.sum(sq_err) / D_MODEL, ("x", "y", "c"))

    flat = [grads[n] for n in names] + grads["w_ff1"] + grads["w_ff2"]
    blocks = [g.reshape(4, 2, *g.shape[1:]) for g in flat]
    from_sibling = _rs_sibling(blocks)
    core = pc.reshape(1).astype(jnp.int32)
    chip_sums = [_chip_sum(f"grads_chip_sum_{k}", b, s, core) for k, (b, s) in enumerate(zip(blocks, from_sibling))]
    from_chips = _rs_chips(chip_sums)
    where = jnp.stack([2 * px + py, pc]).astype(jnp.int32)
    layers = list(zip(blocks, from_sibling, from_chips))
    per_weight = dict(zip(names, [[l] for l in layers[:6]]))
    per_weight["w_ff1"], per_weight["w_ff2"] = layers[6:8], layers[8:10]
    results = {n: _finish_sharded(f"finish_{n}", per_weight[n], given[n], given["m_" + n], given["v_" + n], where) for n in SHARDED}

    results.update(_small_reduce_adamw(gs, given))

    order = ["w_in_e", "mla_gq", "mla_gkv", "w_qb", "w_kvb", "sgu_ln_g", "sgu_ln_b", "sgu_w", "sgu_b", "w_out_e", "w_in_o",
             "hg_lb", "hg_gnorm", "w_out_o", "ln1_g", "ln1_b", "w_ff1", "w_ff2", "ln2_g", "ln2_b"]
    return (loss, dx[None], *[results[name][kind] for kind in range(4) for name in order])
```

```python
import functools
import math

import jax
import jax.numpy as jnp
import numpy as np
from jax import lax
from jax.experimental import pallas as pl
from jax.experimental.pallas import tpu as pltpu

F32 = jnp.float32
BF16 = jnp.bfloat16
MESH = pl.DeviceIdType.MESH
HIGHEST = lax.Precision.HIGHEST

D_MODEL = 1024
D_FF = 4096
N_DEV = 8
HEADS = 8
HEAD_W = 128
MLA_NOPE = 64
MLA_ROPE = 32
MLA_V = 64
MLA_LORA = 256
MLA_SCALE = (MLA_NOPE + MLA_ROPE) ** -0.5
ROPE_BASE = 10000.0
SGU_DIM = 512
SGU_G = 4
SGU_CHUNK = 128
HG_CHUNK = 64
ALPHA = (2 * 2) ** 0.25
EPS = 1e-5
ADAM_LR, ADAM_B1, ADAM_B2, ADAM_EPS, ADAM_WD, ADAM_STEP = 0.001, 0.9, 0.999, 1e-08, 0.01, 10

VMEM_CAP_V7X = 56 * 2**20
VMEM_SLACK = 12 * 2**20
TM = 512
TN = 512


def _vmem(block_bytes):
    return int(min(VMEM_CAP_V7X, 2 * block_bytes + VMEM_SLACK))


def _hbm(a):
    return pltpu.with_memory_space_constraint(a, pltpu.HBM)


def _nbytes(shape, dtype):
    return int(np.prod([d for d in shape if d is not None])) * jnp.dtype(dtype).itemsize


def _sig(x):
    return 1.0 / (1.0 + jnp.exp(-x))


def _gelu(x):
    c = math.sqrt(2.0 / math.pi)
    t = jnp.tanh(c * (x + 0.044715 * x * x * x))
    return 0.5 * x * (1.0 + t), t


def _gelu_grad(x, t):
    c = math.sqrt(2.0 / math.pi)
    return 0.5 * (1.0 + t) + 0.5 * x * (1.0 - t * t) * c * (1.0 + 3 * 0.044715 * x * x)


def _dot(a, b, dims, precision=None):
    return lax.dot_general(a, b, (dims, ((), ())), preferred_element_type=F32, precision=precision)


NN = ((1,), (0,))
NT = ((1,), (1,))
TN_ = ((0,), (0,))


def _tiled(name, grid, ins, outs, compute):
    n_in = len(ins)

    def kern(*refs):
        for o_ref, r in zip(refs[n_in:], compute(*refs[:n_in])):
            o_ref[...] = r.astype(o_ref.dtype).reshape(o_ref.shape)

    swap = lambda f: (lambda j, i: f(i, j))
    nbytes = sum(_nbytes(blk, a.dtype) for a, blk, _ in ins) + sum(_nbytes(blk, dt) + _nbytes(blk, F32) for _, dt, blk, _ in outs)
    res = pl.pallas_call(
        kern, name=name, grid=grid,
        in_specs=[pl.BlockSpec(blk, swap(f)) for _, blk, f in ins],
        out_specs=[pl.BlockSpec(blk, swap(f)) for _, _, blk, f in outs],
        out_shape=[pltpu.HBM(shape, dt) for shape, dt, _, _ in outs],
        compiler_params=pltpu.CompilerParams(dimension_semantics=("parallel", "parallel"), vmem_limit_bytes=_vmem(nbytes)),
    )(*[_hbm(a) for a, _, _ in ins])
    return res if len(res) > 1 else res[0]


def _rb(a, tm, w=None, cb=0):
    return (a, (tm, a.shape[1] if w is None else w), lambda i, j: (i, cb))


def _rbj(a, tm, tn):
    return (a, (tm, tn), lambda i, j: (i, j))


def _cw(b, tn):
    return (b, (b.shape[0], tn), lambda i, j: (0, j))


def _rw(b, tn):
    return (b, (tn, b.shape[1]), lambda i, j: (j, 0))


def _tl(a, tm):
    return (a, (a.shape[0], tm), lambda i, j: (0, i))


def _gcw(g):
    return (g, (None, g.shape[1], g.shape[2]), lambda i, j: (j, 0, 0))


def _grw(g, tn):
    return (g, (N_DEV, tn, g.shape[2]), lambda i, j: (0, j, 0))


def _out(m, n, dtype, tm, tn):
    return ((m, n), dtype, (tm, tn), lambda i, j: (i, j))


def _out_dev(k, n, tm):
    return ((N_DEV, k, n), F32, (None, tm, n), lambda i, j: (j, i, 0))


def _mmc(dims, n_pairs=1, epilogue=None):
    def compute(*refs):
        acc = None
        for k in range(n_pairs):
            d = _dot(refs[2 * k][...].astype(BF16), refs[2 * k + 1][...].astype(BF16), dims)
            acc = d if acc is None else acc + d
        ext = [r[...] for r in refs[2 * n_pairs:]]
        return epilogue(acc, *ext) if epilogue is not None else (acc,)

    return compute


def _mmc_dev(epilogue=None):
    def compute(a_ref, b_ref, *ext_refs):
        n = b_ref.shape[2]
        acc = None
        for d in range(N_DEV):
            t = _dot(a_ref[:, d * n:(d + 1) * n].astype(BF16), b_ref[d].astype(BF16), NT)
            acc = t if acc is None else acc + t
        ext = [r[...] for r in ext_refs]
        return epilogue(acc, *ext) if epilogue is not None else (acc,)

    return compute


def _rowwise(name, body, rows, consts, out_rows, out_accs=(), tr=512):
    T = rows[0][0].shape[0]
    tr = min(tr, T)
    nr, ncn, no = len(rows), len(consts), len(out_rows)

    def kern(*refs):
        accs = refs[nr + ncn + no:]
        if accs:
            @pl.when(pl.program_id(0) == 0)
            def _():
                for a in accs:
                    a[...] = jnp.zeros(a.shape, a.dtype)
        body(refs[:nr], refs[nr:nr + ncn], refs[nr + ncn:nr + ncn + no], accs)

    in_specs = [pl.BlockSpec((tr, w), functools.partial(lambda i, cb: (i, cb), cb=cb)) for _, w, cb in rows]
    in_specs += [pl.BlockSpec(c.shape, functools.partial(lambda i, nd: (0,) * nd, nd=c.ndim)) for c in consts]
    out_specs = [pl.BlockSpec((tr, w), lambda i: (i, 0)) for w, _ in out_rows]
    out_specs += [pl.BlockSpec(s, functools.partial(lambda i, nd: (0,) * nd, nd=len(s))) for s, _ in out_accs]
    out_shape = [pltpu.HBM((T, w), dt) for w, dt in out_rows]
    out_shape += [pltpu.HBM(s, dt) for s, dt in out_accs]
    nbytes = sum(_nbytes((tr, w), a.dtype) for a, w, _ in rows) + sum(_nbytes(c.shape, c.dtype) for c in consts)
    nbytes += sum(_nbytes((tr, w), dt) for w, dt in out_rows) + sum(_nbytes(s, dt) for s, dt in out_accs)
    res = pl.pallas_call(
        kern, name=name, grid=(T // tr,), in_specs=in_specs, out_specs=out_specs, out_shape=out_shape,
        compiler_params=pltpu.CompilerParams(dimension_semantics=("arbitrary",), vmem_limit_bytes=_vmem(nbytes)),
    )(*[_hbm(a) for a, _, _ in rows], *[_hbm(c) for c in consts])
    return res if len(res) > 1 else res[0]


def _full(a):
    return (a, a.shape[1], 0)


def _ln_stats(y):
    mu = jnp.mean(y, axis=-1, keepdims=True)
    yc = y - mu
    r = lax.rsqrt(jnp.mean(yc * yc, axis=-1, keepdims=True) + EPS)
    return yc * r, r


def _ln_fwd(name, h_in, mix, g, b, layer):
    def body(rows, consts, outs, accs):
        y = ALPHA * rows[0][...] + rows[1][...]
        xh, _ = _ln_stats(y)
        h = xh * consts[0][layer:layer + 1, :] + consts[1][layer:layer + 1, :]
        outs[0][...] = y
        outs[1][...] = h
        outs[2][...] = h.astype(BF16)

    return _rowwise(name, body, [_full(h_in), _full(mix)], [g, b], [(D_MODEL, F32), (D_MODEL, F32), (D_MODEL, BF16)], tr=256)


def _ln_loss(name, h_in, mix, g, b, layer, target):
    def body(rows, consts, outs, accs):
        y = ALPHA * rows[0][...] + rows[1][...]
        xh, _ = _ln_stats(y)
        err = xh * consts[0][layer:layer + 1, :] + consts[1][layer:layer + 1, :] - rows[2][...]
        outs[0][...] = y
        outs[1][...] = err * (1.0 / D_MODEL)
        accs[0][...] += jnp.sum(err * err, axis=0, keepdims=True)

    return _rowwise(name, body, [_full(h_in), _full(mix), _full(target)], [g, b], [(D_MODEL, F32), (D_MODEL, F32)],
                    [((1, D_MODEL), F32)], tr=256)


def _ln_bwd(name, y, dh, g, layer):
    def body(rows, consts, outs, accs):
        xh, r = _ln_stats(rows[0][...])
        d = rows[1][...]
        accs[0][...] += jnp.sum(d * xh, axis=0, keepdims=True)
        accs[1][...] += jnp.sum(d, axis=0, keepdims=True)
        dx = d * consts[0][layer:layer + 1, :]
        dy = r * (dx - jnp.mean(dx, axis=-1, keepdims=True) - xh * jnp.mean(dx * xh, axis=-1, keepdims=True))
        outs[0][...] = dy
        outs[1][...] = dy.astype(BF16)

    return _rowwise(name, body, [_full(y), _full(dh)], [g], [(D_MODEL, F32), (D_MODEL, BF16)],
                    [((1, D_MODEL), F32), ((1, D_MODEL), F32)], tr=256)


def _relu2_epilogue(acc):
    a = jnp.maximum(acc, 0.0)
    return acc, a * a


def _mlp_fwd(tag, h_bf, w1, w2):
    T = h_bf.shape[0]
    tm = min(TM, T)
    a, act = _tiled(f"{tag}_ff1", (N_DEV, T // tm), [_rb(h_bf, tm), _gcw(w1)],
                    [_out(T, D_FF, BF16, tm, TN), _out(T, D_FF, BF16, tm, TN)], _mmc(NN, epilogue=_relu2_epilogue))
    ff = _tiled(f"{tag}_ff2", (D_MODEL // TN, T // tm), [_rb(act, tm), _cw(w2.reshape(D_FF, D_MODEL), TN)],
                [_out(T, D_MODEL, F32, tm, TN)], _mmc(NN))
    return a, act, ff


def _mlp_bwd(tag, h_bf, a, act, dff_bf, dy, w1, w2):
    T = h_bf.shape[0]
    tm = min(TM, T)
    da = _tiled(f"{tag}_dact", (N_DEV, T // tm), [_rb(dff_bf, tm), _rw(w2.reshape(D_FF, D_MODEL), TN), _rbj(a, tm, TN)],
                [_out(T, D_FF, BF16, tm, TN)],
                _mmc(NT, epilogue=lambda acc, a_t: (acc * 2.0 * jnp.maximum(a_t.astype(F32), 0.0),)))
    dw2 = _tiled(f"{tag}_dw2", (D_MODEL // TN, D_FF // TM), [_tl(act, TM), _cw(dff_bf, TN)],
                 [_out(D_FF, D_MODEL, F32, TM, TN)], _mmc(TN_)).reshape(N_DEV, D_FF // N_DEV, D_MODEL)
    dw1 = _tiled(f"{tag}_dw1", (N_DEV, D_MODEL // TM), [_tl(h_bf, TM), _cw(da, TN)], [_out_dev(D_MODEL, TN, TM)], _mmc(TN_))
    dh, dh_bf = _tiled(f"{tag}_dh", (D_MODEL // TN, T // tm), [_rb(da, tm), _grw(w1, TN), _rbj(dy, tm, TN)],
                       [_out(T, D_MODEL, F32, tm, TN), _out(T, D_MODEL, BF16, tm, TN)],
                       _mmc_dev(epilogue=lambda acc, dy_t: (acc + ALPHA * dy_t,) * 2))
    return dh, dh_bf, dw1, dw2


def _rope_tables(positions_col, invf_lane):
    def body(rows, consts, outs, accs):
        ang = rows[0][...].astype(F32) * consts[0][...]
        c, s = jnp.cos(ang), jnp.sin(ang)
        lane = lax.broadcasted_iota(jnp.int32, ang.shape, 1)
        outs[0][...] = jnp.where(lane < 64, 1.0, jnp.where(lane < 96, c, 0.0))
        outs[1][...] = jnp.where((lane >= 64) & (lane < 80), -s, 0.0)
        outs[2][...] = jnp.where((lane >= 80) & (lane < 96), s, 0.0)

    return _rowwise("rope_tables", body, [_full(positions_col)], [invf_lane], [(HEAD_W, F32)] * 3)


def _rope(x, c, s1, s2):
    return x * c + pltpu.roll(x, 112, 1) * s1 + pltpu.roll(x, 16, 1) * s2


def _rope_t(dx, c, s1, s2):
    return dx * c + pltpu.roll(dx * s1, 16, 1) + pltpu.roll(dx * s2, 112, 1)


def _rms(c):
    r = lax.rsqrt(jnp.mean(c * c, axis=-1, keepdims=True) + EPS)
    return c * r, r


def _mla_pre(zm, tabs, gq, gkv):
    def body(rows, consts, outs, accs):
        cq, _ = _rms(rows[0][...])
        ckv, _ = _rms(rows[1][...])
        outs[0][...] = (cq * consts[0][...]).astype(BF16)
        outs[1][...] = (ckv * consts[1][...]).astype(BF16)
        outs[2][...] = _rope(rows[2][...], rows[3][...], rows[4][...], rows[5][...])

    rows = [(zm, 256, 0), (zm, 256, 1), (zm, 128, 4)] + [_full(t) for t in tabs]
    return _rowwise("mla_pre", body, rows, [gq, gkv], [(256, BF16), (256, BF16), (HEAD_W, F32)])


def _mla_pre_bwd(zm, tabs, gq, gkv, dcqn, dckvn, dk):
    def body(rows, consts, outs, accs):
        res = []
        for k in range(2):
            ch, r = _rms(rows[k][...])
            d = rows[5 + k][...]
            accs[k][...] += jnp.sum(d * ch, axis=0, keepdims=True)
            dc = d * consts[k][...]
            res.append(r * (dc - ch * jnp.mean(dc * ch, axis=-1, keepdims=True)))
        dks = rows[7][:, 0:HEAD_W]
        for h in range(1, HEADS):
            dks = dks + rows[7][:, h * HEAD_W:(h + 1) * HEAD_W]
        lane = lax.broadcasted_iota(jnp.int32, dks.shape, 1)
        dks = jnp.where((lane >= 64) & (lane < 96), dks, 0.0)
        dkr = _rope_t(dks, rows[2][...], rows[3][...], rows[4][...])
        outs[0][:, 0:256] = res[0].astype(BF16)
        outs[0][:, 256:512] = res[1].astype(BF16)
        outs[0][:, 512:640] = dkr.astype(BF16)

    rows = [(zm, 256, 0), (zm, 256, 1)] + [_full(t) for t in tabs] + [_full(dcqn), _full(dckvn), _full(dk)]
    return _rowwise("mla_pre_bwd", body, rows, [gq, gkv], [(640, BF16)], [((1, 256), F32), ((1, 256), F32)])


def _rope_heads(x, c, s1, s2, fn):
    return jnp.concatenate([fn(x[:, h * HEAD_W:(h + 1) * HEAD_W], c, s1, s2) for h in range(HEADS)], axis=1)


def _unrope_heads(dq, tabs):
    def body(rows, consts, outs, accs):
        outs[0][...] = _rope_heads(rows[0][...], rows[1][...], rows[2][...], rows[3][...], _rope_t).astype(BF16)

    return _rowwise("l0_dq_rope", body, [_full(dq)] + [_full(t) for t in tabs], [], [(HEADS * HEAD_W, BF16)])


def _attn_block(T):
    return min(256, T)


def _attn_fwd(q, k, v):
    T = q.shape[0]
    BQ = _attn_block(T)
    nq = T // BQ

    def kern(q_ref, k_ref, v_ref, o_ref, lse_ref):
        def step(i, j, carry, masked):
            m, l, acc = carry
            qb = q_ref[pl.ds(pl.multiple_of(i * BQ, BQ), BQ), :]
            kb = k_ref[pl.ds(pl.multiple_of(j * BQ, BQ), BQ), :]
            vb = v_ref[pl.ds(pl.multiple_of(j * BQ, BQ), BQ), :]
            s = _dot(qb, kb, NT) * MLA_SCALE
            if masked:
                row = lax.broadcasted_iota(jnp.int32, s.shape, 0)
                col = lax.broadcasted_iota(jnp.int32, s.shape, 1)
                s = jnp.where(col <= row, s, -1e30)
            m_new = jnp.maximum(m, jnp.max(s, axis=-1, keepdims=True))
            p = jnp.exp(s - m_new)
            a = jnp.exp(m - m_new)
            l = a * l + jnp.sum(p, axis=-1, keepdims=True)
            acc = a * acc + _dot(p.astype(BF16), vb, NN)
            return m_new, l, acc

        def qloop(i, _):
            init = (jnp.full((BQ, 1), -1e30, F32), jnp.zeros((BQ, 1), F32), jnp.zeros((BQ, HEAD_W), F32))
            carry = lax.fori_loop(0, i, lambda j, c: step(i, j, c, False), init)
            m, l, acc = step(i, i, carry, True)
            rows = pl.ds(pl.multiple_of(i * BQ, BQ), BQ)
            o_ref[rows, :] = acc / l
            lse_ref[0, rows, :] = m + jnp.log(l)
            return 0

        lax.fori_loop(0, nq, qloop, 0)

    head = pl.BlockSpec((T, HEAD_W), lambda h: (0, h))
    nbytes = 3 * _nbytes((T, HEAD_W), BF16) + _nbytes((T, HEAD_W), F32) + _nbytes((T, 128), F32)
    return pl.pallas_call(
        kern, name="attn_fwd", grid=(HEADS,), in_specs=[head, head, head],
        out_specs=[head, pl.BlockSpec((1, T, 1), lambda h: (h, 0, 0))],
        out_shape=[pltpu.HBM((T, HEADS * HEAD_W), F32), pltpu.HBM((HEADS, T, 1), F32)],
        compiler_params=pltpu.CompilerParams(dimension_semantics=("parallel",), vmem_limit_bytes=_vmem(nbytes)),
    )(_hbm(q), _hbm(k), _hbm(v))


def _attn_bwd(q, k, v, o, lse, dcat):
    T = q.shape[0]
    BQ = _attn_block(T)
    nq = T // BQ

    def kern(q_ref, k_ref, v_ref, o_ref, lse_ref, do_ref, dq_ref, dk_ref, dv_ref, dd_ref):
        dq_ref[...] = jnp.zeros(dq_ref.shape, F32)

        def dloop(i, _):
            rows = pl.ds(pl.multiple_of(i * BQ, BQ), BQ)
            dd_ref[rows, :] = jnp.sum(do_ref[rows, :].astype(F32) * o_ref[rows, :], axis=-1, keepdims=True)
            return 0

        lax.fori_loop(0, nq, dloop, 0)

        def step(j, i, carry, masked):
            dk_acc, dv_acc = carry
            rq = pl.ds(pl.multiple_of(i * BQ, BQ), BQ)
            rk = pl.ds(pl.multiple_of(j * BQ, BQ), BQ)
            qb, kb, vb, dob = q_ref[rq, :], k_ref[rk, :], v_ref[rk, :], do_ref[rq, :]
            s = _dot(qb, kb, NT) * MLA_SCALE
            p = jnp.exp(s - lse_ref[0, rq, :])
            if masked:
                row = lax.broadcasted_iota(jnp.int32, s.shape, 0)
                col = lax.broadcasted_iota(jnp.int32, s.shape, 1)
                p = jnp.where(col <= row, p, 0.0)
            dp = _dot(dob, vb, NT)
            ds = (p * (dp - dd_ref[rq, :]) * MLA_SCALE).astype(BF16)
            dv_acc = dv_acc + _dot(p.astype(BF16), dob, TN_)
            dk_acc = dk_acc + _dot(ds, qb, TN_)
            dq_ref[rq, :] += _dot(ds, kb, NN)
            return dk_acc, dv_acc

        def kloop(j, _):
            init = (jnp.zeros((BQ, HEAD_W), F32), jnp.zeros((BQ, HEAD_W), F32))
            carry = step(j, j, init, True)
            dk_acc, dv_acc = lax.fori_loop(j + 1, nq, lambda i, c: step(j, i, c, False), carry)
            rk = pl.ds(pl.multiple_of(j * BQ, BQ), BQ)
            dk_ref[rk, :] = dk_acc
            dv_ref[rk, :] = dv_acc
            return 0

        lax.fori_loop(0, nq, kloop, 0)

    head = pl.BlockSpec((T, HEAD_W), lambda h: (0, h))
    nbytes = 4 * _nbytes((T, HEAD_W), BF16) + 5 * _nbytes((T, HEAD_W), F32) + 2 * _nbytes((T, 128), F32)
    return pl.pallas_call(
        kern, name="attn_bwd", grid=(HEADS,),
        in_specs=[head, head, head, head, pl.BlockSpec((1, T, 1), lambda h: (h, 0, 0)), head],
        out_specs=[head, head, head],
        out_shape=[pltpu.HBM((T, HEADS * HEAD_W), F32)] * 3,
        scratch_shapes=[pltpu.VMEM((T, 1), F32)],
        compiler_params=pltpu.CompilerParams(dimension_semantics=("parallel",), vmem_limit_bytes=_vmem(nbytes)),
    )(*[_hbm(a) for a in (q, k, v, o, lse, dcat)])


def _sgu_common(u, v, ln_g, ln_b):
    ua, tu = _gelu(u)
    va, tv = _gelu(v)
    vh, r = _ln_stats(va)
    return ua, tu, tv, vh, r, vh * ln_g + ln_b


def _tril_mask(n):
    return lax.broadcasted_iota(jnp.int32, (n, n), 1) <= lax.broadcasted_iota(jnp.int32, (n, n), 0)


def _sgu_fwd(zs, ln_g, ln_b, w, bias_full):
    def body(rows, consts, outs, accs):
        ua, _, _, _, _, vn = _sgu_common(rows[0][...], rows[1][...], consts[0][...], consts[1][...])
        vn = vn.astype(BF16)
        tri = _tril_mask(SGU_CHUNK)
        for g in range(SGU_G):
            wg = jnp.where(tri, consts[2][0, g], 0.0).astype(BF16)
            cols = slice(g * 128, (g + 1) * 128)
            for c in range(ua.shape[0] // SGU_CHUNK):
                rws = slice(c * SGU_CHUNK, (c + 1) * SGU_CHUNK)
                mixed = _dot(wg, vn[rws, cols], NN) + consts[3][:, cols]
                outs[0][rws, cols] = (ua[rws, cols] * mixed).astype(BF16)

    return _rowwise("sgu_fwd", body, [(zs, 512, 0), (zs, 512, 1)], [ln_g, ln_b, w, bias_full], [(SGU_DIM, BF16)])


def _sgu_bwd(zs, dcat, ln_g, ln_b, w, bias_full):
    def body(rows, consts, outs, accs):
        u, v = rows[0][...], rows[1][...]
        ua, tu, tv, vh, r, vn = _sgu_common(u, v, consts[0][...], consts[1][...])
        dout = rows[2][...].astype(F32)
        vn_bf = vn.astype(BF16)
        tri = _tril_mask(SGU_CHUNK)
        dmixed = (dout * ua)
        dmixed_bf = dmixed.astype(BF16)
        ones = jnp.ones((8, SGU_CHUNK), F32)
        dvn_cols, mixed_cols = [], []
        for g in range(SGU_G):
            wg = jnp.where(tri, consts[2][0, g], 0.0).astype(BF16)
            cols = slice(g * 128, (g + 1) * 128)
            dvn_rows, mixed_rows = [], []
            dw = jnp.zeros((SGU_CHUNK, SGU_CHUNK), F32)
            dmix_sum = jnp.zeros((SGU_CHUNK, 128), F32)
            for c in range(u.shape[0] // SGU_CHUNK):
                rws = slice(c * SGU_CHUNK, (c + 1) * SGU_CHUNK)
                mixed_rows.append(_dot(wg, vn_bf[rws, cols], NN) + consts[3][:, cols])
                dvn_rows.append(_dot(wg, dmixed_bf[rws, cols], TN_))
                dw = dw + _dot(dmixed_bf[rws, cols], vn_bf[rws, cols], NT)
                dmix_sum = dmix_sum + dmixed[rws, cols]
            accs[0][g] += jnp.where(tri, dw, 0.0)
            accs[3][g:g + 1, :] += _dot(ones, dmix_sum, NT, precision=HIGHEST)[0:1, :]
            dvn_cols.append(jnp.concatenate(dvn_rows, axis=0))
            mixed_cols.append(jnp.concatenate(mixed_rows, axis=0))
        dvn = jnp.concatenate(dvn_cols, axis=1)
        mixed = jnp.concatenate(mixed_cols, axis=1)
        accs[1][...] += jnp.sum(dvn * vh, axis=0, keepdims=True)
        accs[2][...] += jnp.sum(dvn, axis=0, keepdims=True)
        dvh = dvn * consts[0][...]
        dva = r * (dvh - jnp.mean(dvh, axis=-1, keepdims=True) - vh * jnp.mean(dvh * vh, axis=-1, keepdims=True))
        outs[0][:, 0:512] = (dout * mixed * _gelu_grad(u, tu)).astype(BF16)
        outs[0][:, 512:1024] = (dva * _gelu_grad(v, tv)).astype(BF16)

    return _rowwise("sgu_bwd", body, [(zs, 512, 0), (zs, 512, 1), (dcat, 512, 2)], [ln_g, ln_b, w, bias_full], [(1024, BF16)],
                    [((SGU_G, 128, 128), F32), ((1, SGU_DIM), F32), ((1, SGU_DIM), F32), ((SGU_G, 128), F32)], tr=256)


def _lower_bound(hg_lb):
    a0, a1 = hg_lb[0:1, :], hg_lb[1:2, :]
    m = jnp.maximum(a0, a1)
    e0, e1 = jnp.exp(a0 - m), jnp.exp(a1 - m)
    s0, s1 = e0 / (e0 + e1), e1 / (e0 + e1)
    return (s0 + s1) - s0, s0, s1


def _hg_gates(qr, fr, lb):
    C = qr.shape[0]
    sq = _sig(qr)
    qf = qr * sq
    sf = _sig(fr)
    gate = lb + (1.0 - lb) * sf
    kk = 1.0 - gate
    tri = _tril_mask(C)
    b = _dot(jnp.where(tri, 1.0, 0.0), jnp.log(gate), NN, precision=HIGHEST)
    bref = b[C // 2 - 1:C // 2, :]
    bl = b[C - 1:C, :]
    e_b = jnp.exp(b)
    e_q = jnp.exp(b - bref)
    e_k = jnp.exp(bref - b)
    e_lb = jnp.exp(bl - b)
    return dict(sq=sq, qf=qf, sf=sf, gate=gate, kk=kk, tri=tri, bl=bl, e_b=e_b, e_q=e_q, e_k=e_k, e_lb=e_lb)


def _hgrn_fwd(z1, hg_lb, gnorm):
    T = z1.shape[0]
    C = min(HG_CHUNK, T)
    nc = T // C

    def kern(q_ref, f_ref, i_ref, g_ref, lb_ref, gn_ref, o_ref, hg_ref, st_ref, s_scr):
        @pl.when(pl.program_id(0) == 0)
        def _():
            s_scr[...] = jnp.zeros(s_scr.shape, F32)

        lb_all, _, _ = _lower_bound(lb_ref[...])
        st_ref[0] = s_scr[...]
        for h in range(HEADS):
            cols = slice(h * HEAD_W, (h + 1) * HEAD_W)
            t = _hg_gates(q_ref[:, cols], f_ref[:, cols], lb_all[:, cols])
            v = i_ref[:, cols]
            v_bf = v.astype(BF16)
            st = s_scr[h]
            a = jnp.where(t["tri"], _dot((t["qf"] * t["e_q"]).astype(BF16), (t["kk"] * t["e_k"]).astype(BF16), NT), 0.0)
            o = _dot(a.astype(BF16), v_bf, NN) + _dot((t["qf"] * t["e_b"]).astype(BF16), st.astype(BF16), NT)
            s_scr[h] = st * jnp.exp(t["bl"]) + _dot(v_bf, (t["kk"] * t["e_lb"]).astype(BF16), TN_)
            o_ref[:, cols] = o
            gr = g_ref[:, cols]
            r = lax.rsqrt(jnp.mean(o * o, axis=-1, keepdims=True) + EPS)
            hg_ref[:, cols] = (o * r * gn_ref[:, cols] * (gr * _sig(gr))).astype(BF16)

    seg = lambda k: pl.BlockSpec((C, D_MODEL), functools.partial(lambda n, k: (n, k), k=k))
    row = pl.BlockSpec((C, D_MODEL), lambda n: (n, 0))
    nbytes = 6 * _nbytes((C, D_MODEL), F32) + 3 * _nbytes((HEADS, 128, 128), F32)
    return pl.pallas_call(
        kern, name="hgrn_fwd", grid=(nc,),
        in_specs=[seg(0), seg(1), seg(2), seg(3), pl.BlockSpec((2, D_MODEL), lambda n: (0, 0)),
                  pl.BlockSpec((1, D_MODEL), lambda n: (0, 0))],
        out_specs=[row, row, pl.BlockSpec((1, HEADS, 128, 128), lambda n: (n, 0, 0, 0))],
        out_shape=[pltpu.HBM((T, D_MODEL), F32), pltpu.HBM((T, D_MODEL), BF16),
                   pltpu.HBM((nc, HEADS, 128, 128), F32)],
        scratch_shapes=[pltpu.VMEM((HEADS, 128, 128), F32)],
        compiler_params=pltpu.CompilerParams(dimension_semantics=("arbitrary",), vmem_limit_bytes=_vmem(nbytes)),
    )(*[_hbm(a) for a in (z1, z1, z1, z1, hg_lb, gnorm)])


def _hgrn_bwd(z1, o_pre, dhg, states, hg_lb, gnorm):
    T = z1.shape[0]
    C = min(HG_CHUNK, T)
    nc = T // C

    def kern(q_ref, f_ref, i_ref, g_ref, o_ref, dhg_ref, st_ref, lb_ref, gn_ref, dz_ref, dlb_ref, dgn_ref, ds_scr, dlb_scr):
        n = pl.program_id(0)

        @pl.when(n == 0)
        def _():
            ds_scr[...] = jnp.zeros(ds_scr.shape, F32)
            dlb_scr[...] = jnp.zeros(dlb_scr.shape, F32)
            dgn_ref[...] = jnp.zeros(dgn_ref.shape, F32)

        lb_all, s0, s1 = _lower_bound(lb_ref[...])
        for h in range(HEADS):
            cols = slice(h * HEAD_W, (h + 1) * HEAD_W)
            lb = lb_all[:, cols]
            qr, fr = q_ref[:, cols], f_ref[:, cols]
            t = _hg_gates(qr, fr, lb)
            tri = t["tri"]
            v_bf = i_ref[:, cols].astype(BF16)
            st_bf = st_ref[0, h].astype(BF16)
            dst = ds_scr[h]
            dst_bf = dst.astype(BF16)
            o = o_ref[:, cols]
            gr = g_ref[:, cols]
            sg = _sig(gr)
            sil = gr * sg
            gn = gn_ref[:, cols]
            r = lax.rsqrt(jnp.mean(o * o, axis=-1, keepdims=True) + EPS)
            on = o * r
            dh = dhg_ref[:, cols].astype(F32)
            dgn_ref[:, cols] += jnp.sum(dh * on * sil, axis=0, keepdims=True)
            dg = dh * on * gn * (sg * (1.0 + gr * (1.0 - sg)))
            don = dh * gn * sil
            do_bf = (r * (don - on * jnp.mean(don * on, axis=-1, keepdims=True))).astype(BF16)
            qe = (t["qf"] * t["e_q"]).astype(BF16)
            ke = (t["kk"] * t["e_k"]).astype(BF16)
            qb = (t["qf"] * t["e_b"]).astype(BF16)
            kh_bf = (t["kk"] * t["e_lb"]).astype(BF16)
            a_bf = jnp.where(tri, _dot(qe, ke, NT), 0.0).astype(BF16)
            da_bf = jnp.where(tri, _dot(do_bf, v_bf, NT), 0.0).astype(BF16)
            dv = _dot(a_bf, do_bf, TN_) + _dot(kh_bf, dst_bf, NT)
            dqe = _dot(da_bf, ke, NN)
            dqb = _dot(do_bf, st_bf, NN)
            dke = _dot(da_bf, qe, TN_)
            dkh = _dot(v_bf, dst_bf, NN)
            dqf = dqe * t["e_q"] + dqb * t["e_b"]
            dkk = dke * t["e_k"] + dkh * t["e_lb"]
            kh_r = kh_bf.astype(F32)
            db = qe.astype(F32) * dqe - ke.astype(F32) * dke + qb.astype(F32) * dqb - kh_r * dkh
            e_bl = jnp.exp(t["bl"])
            dbl = jnp.sum(dkh * kh_r, axis=0, keepdims=True) + e_bl * jnp.sum(st_ref[0, h] * dst, axis=0, keepdims=True)
            dlg = _dot(jnp.where(tri, 1.0, 0.0), db, TN_, precision=HIGHEST) + dbl
            ds_scr[h] = dst * e_bl + _dot(do_bf, qb, TN_)
            dgate = dlg / t["gate"] - dkk
            sf = t["sf"]
            dlb_scr[:, cols] += jnp.sum(dgate * (1.0 - sf), axis=0, keepdims=True)
            df = dgate * (1.0 - lb) * sf * (1.0 - sf)
            dq = dqf * (t["sq"] * (1.0 + qr * (1.0 - t["sq"])))
            dz_ref[:, cols] = dq.astype(BF16)
            dz_ref[:, D_MODEL + h * HEAD_W:D_MODEL + (h + 1) * HEAD_W] = df.astype(BF16)
            dz_ref[:, 2 * D_MODEL + h * HEAD_W:2 * D_MODEL + (h + 1) * HEAD_W] = dv.astype(BF16)
            dz_ref[:, 3 * D_MODEL + h * HEAD_W:3 * D_MODEL + (h + 1) * HEAD_W] = dg.astype(BF16)

        @pl.when(n == nc - 1)
        def _():
            d = s0 * s1 * dlb_scr[...]
            dlb_ref[0:1, :] = -d
            dlb_ref[1:2, :] = d

    seg = lambda k: pl.BlockSpec((C, D_MODEL), functools.partial(lambda n, k: (nc - 1 - n, k), k=k))
    nbytes = 6 * _nbytes((C, D_MODEL), F32) + _nbytes((C, 4 * D_MODEL), BF16) + 3 * _nbytes((HEADS, 128, 128), F32)
    return pl.pallas_call(
        kern, name="hgrn_bwd", grid=(nc,),
        in_specs=[seg(0), seg(1), seg(2), seg(3), seg(0), seg(0),
                  pl.BlockSpec((1, HEADS, 128, 128), lambda n: (nc - 1 - n, 0, 0, 0)),
                  pl.BlockSpec((2, D_MODEL), lambda n: (0, 0)), pl.BlockSpec((1, D_MODEL), lambda n: (0, 0))],
        out_specs=[pl.BlockSpec((C, 4 * D_MODEL), lambda n: (nc - 1 - n, 0)),
                   pl.BlockSpec((2, D_MODEL), lambda n: (0, 0)), pl.BlockSpec((1, D_MODEL), lambda n: (0, 0))],
        out_shape=[pltpu.HBM((T, 4 * D_MODEL), BF16), pltpu.HBM((2, D_MODEL), F32),
                   pltpu.HBM((1, D_MODEL), F32)],
        scratch_shapes=[pltpu.VMEM((HEADS, 128, 128), F32), pltpu.VMEM((1, D_MODEL), F32)],
        compiler_params=pltpu.CompilerParams(dimension_semantics=("arbitrary",), vmem_limit_bytes=_vmem(nbytes)),
    )(*[_hbm(a) for a in (z1, z1, z1, z1, o_pre, dhg, states, hg_lb, gnorm)])


def _prep_weights(gw):
    w_in_e = gw["w_in_e"].transpose(1, 0, 2).reshape(D_MODEL, 1568)
    kr = jnp.pad(w_in_e[:, 512:544], ((0, 0), (64, 32)))
    wm = jnp.concatenate([w_in_e[:, 0:512], kr], axis=1)
    ws = w_in_e[:, 544:1568]
    w_qb = gw["w_qb"].transpose(1, 0, 2).reshape(MLA_LORA, HEADS, 96)
    wq = jnp.pad(w_qb, ((0, 0), (0, 0), (0, 32))).reshape(MLA_LORA, HEADS * HEAD_W)
    kvb = gw["w_kvb"].transpose(1, 0, 2).reshape(MLA_LORA, HEADS, 128)
    wk = jnp.pad(kvb[:, :, :64], ((0, 0), (0, 0), (0, 64))).reshape(MLA_LORA, HEADS * HEAD_W)
    wv = jnp.pad(kvb[:, :, 64:], ((0, 0), (0, 0), (0, 64))).reshape(MLA_LORA, HEADS * HEAD_W)
    w_out_e = gw["w_out_e"].reshape(D_MODEL, D_MODEL)
    woa = jnp.pad(w_out_e[:512].reshape(HEADS, 64, D_MODEL), ((0, 0), (0, 64), (0, 0))).reshape(HEADS * HEAD_W, D_MODEL)
    return dict(wm=wm, ws=ws, wq=wq, wk=wk, wv=wv, woa=woa, wob=w_out_e[512:])


def _unprep_grads(g):
    dwm, dws = g["wm"], g["ws"]
    d_in_e = jnp.concatenate([dwm[:, 0:512], dwm[:, 512 + 64:512 + 96], dws], axis=1)
    d_qb = g["wq"].reshape(MLA_LORA, HEADS, HEAD_W)[:, :, :96].reshape(MLA_LORA, HEADS * 96)
    dk = g["wk"].reshape(MLA_LORA, HEADS, HEAD_W)[:, :, :64]
    dv = g["wv"].reshape(MLA_LORA, HEADS, HEAD_W)[:, :, :64]
    d_kvb = jnp.concatenate([dk, dv], axis=2).reshape(MLA_LORA, HEADS * 128)
    d_oa = g["woa"].reshape(HEADS, HEAD_W, D_MODEL)[:, :64].reshape(HEADS * 64, D_MODEL)
    dev_major = lambda a: a.reshape(a.shape[0], N_DEV, a.shape[1] // N_DEV).transpose(1, 0, 2)
    return dict(w_in_e=dev_major(d_in_e), w_qb=dev_major(d_qb), w_kvb=dev_major(d_kvb),
                w_out_e=jnp.concatenate([d_oa, g["wob"]], axis=0).reshape(N_DEV, D_MODEL // N_DEV, D_MODEL))


def _local_step(x, positions, target, gw, sp):
    w = _prep_weights(gw)
    T = x.shape[0]
    tm = min(TM, T)
    nt = T // tm
    half = MLA_ROPE // 2
    inv_freq = ROPE_BASE ** (-jnp.arange(half, dtype=F32) / half)
    invf_lane = jnp.concatenate([jnp.zeros((64,), F32), inv_freq, inv_freq, jnp.zeros((32,), F32)]).reshape(1, HEAD_W)
    tabs = _rope_tables(positions.reshape(T, 1), invf_lane)
    bias_full = jnp.repeat(sp["sgu_b"][0].T, 128, axis=1)
    sgu_w = sp["sgu_w"]
    gq, gkv = sp["mla_gq"], sp["mla_gkv"]
    ln1_g, ln1_b, ln2_g, ln2_b = sp["ln1_g"], sp["ln1_b"], sp["ln2_g"], sp["ln2_b"]
    w_in_o, w_out_o = gw["w_in_o"], gw["w_out_o"].reshape(D_MODEL, D_MODEL)
    wide = HEADS * HEAD_W
    tab_rows = [_rb(t, tm) for t in tabs]
    resid = lambda acc, d: (acc + ALPHA * d,)

    zm = _tiled("l0_in_mla", (1, nt), [_rb(x, tm), _cw(w["wm"], 640)], [_out(T, 640, F32, tm, 640)], _mmc(NN))
    zs = _tiled("l0_in_sgu", (2, nt), [_rb(x, tm), _cw(w["ws"], TN)], [_out(T, 1024, F32, tm, TN)], _mmc(NN))
    cqn, ckvn, kr_rot = _mla_pre(zm, tabs, gq, gkv)
    q = _tiled("l0_q", (1, nt), [_rb(cqn, tm), _cw(w["wq"], wide)] + tab_rows, [_out(T, wide, BF16, tm, wide)],
               _mmc(NN, epilogue=lambda acc, c, s1, s2: (_rope_heads(acc, c, s1, s2, _rope),)))
    k = _tiled("l0_k", (1, nt), [_rb(ckvn, tm), _cw(w["wk"], wide), _rb(kr_rot, tm)], [_out(T, wide, BF16, tm, wide)],
               _mmc(NN, epilogue=lambda acc, kr: (acc + jnp.concatenate([kr] * HEADS, axis=1),)))
    v = _tiled("l0_v", (1, nt), [_rb(ckvn, tm), _cw(w["wv"], wide)], [_out(T, wide, BF16, tm, wide)], _mmc(NN))
    o_att, lse = _attn_fwd(q, k, v)
    b_out = _sgu_fwd(zs, sp["sgu_ln_g"], sp["sgu_ln_b"], sgu_w, bias_full)
    mix0 = _tiled("l0_out", (2, nt), [_rb(o_att, tm), _cw(w["woa"], TN), _rb(b_out, tm), _cw(w["wob"], TN)],
                  [_out(T, D_MODEL, F32, tm, TN)], _mmc(NN, n_pairs=2))
    y1, h1, h1_bf = _ln_fwd("l0_ln1", x, mix0, ln1_g, ln1_b, 0)
    a0, act0, ff0 = _mlp_fwd("l0", h1_bf, gw["w_ff1"][0], gw["w_ff2"][0])
    y2, h2, h2_bf = _ln_fwd("l0_ln2", h1, ff0, ln2_g, ln2_b, 0)

    z1 = _tiled("l1_in", (N_DEV, nt), [_rb(h2_bf, tm), _gcw(w_in_o)], [_out(T, 4 * D_MODEL, F32, tm, TN)], _mmc(NN))
    o_pre, hg, states = _hgrn_fwd(z1, sp["hg_lb"], sp["hg_gnorm"])
    mix1 = _tiled("l1_out", (2, nt), [_rb(hg, tm), _cw(w_out_o, TN)], [_out(T, D_MODEL, F32, tm, TN)], _mmc(NN))
    y3, h3, h3_bf = _ln_fwd("l1_ln1", h2, mix1, ln1_g, ln1_b, 1)
    a1, act1, ff1 = _mlp_fwd("l1", h3_bf, gw["w_ff1"][1], gw["w_ff2"][1])
    y4, dh4, sq_err = _ln_loss("l1_ln2", h3, ff1, ln2_g, ln2_b, 1, target)

    gs, g0 = {}, {}
    dy4, dy4_bf, gs["ln2_g1"], gs["ln2_b1"] = _ln_bwd("l1_ln2_bwd", y4, dh4, ln2_g, 1)
    dh3, _, dw1_1, dw2_1 = _mlp_bwd("l1", h3_bf, a1, act1, dy4_bf, dy4, gw["w_ff1"][1], gw["w_ff2"][1])
    dy3, dy3_bf, gs["ln1_g1"], gs["ln1_b1"] = _ln_bwd("l1_ln1_bwd", y3, dh3, ln1_g, 1)
    d_out_o = _tiled("l1_dwout", (2, D_MODEL // TM), [_tl(hg, TM), _cw(dy3_bf, TN)], [_out(D_MODEL, D_MODEL, F32, TM, TN)],
                     _mmc(TN_)).reshape(N_DEV, D_MODEL // N_DEV, D_MODEL)
    dhg = _tiled("l1_dhg", (2, nt), [_rb(dy3_bf, tm), _rw(w_out_o, TN)], [_out(T, D_MODEL, BF16, tm, TN)], _mmc(NT))
    dz1, gs["hg_lb"], gs["hg_gnorm"] = _hgrn_bwd(z1, o_pre, dhg, states, sp["hg_lb"], sp["hg_gnorm"])
    d_in_o = _tiled("l1_dwin", (N_DEV, D_MODEL // TM), [_tl(h2_bf, TM), _cw(dz1, TN)], [_out_dev(D_MODEL, TN, TM)], _mmc(TN_))
    dh2 = _tiled("l1_dh2", (2, nt), [_rb(dz1, tm), _grw(w_in_o, TN), _rbj(dy3, tm, TN)], [_out(T, D_MODEL, F32, tm, TN)],
                 _mmc_dev(epilogue=resid))

    dy2, dy2_bf, gs["ln2_g0"], gs["ln2_b0"] = _ln_bwd("l0_ln2_bwd", y2, dh2, ln2_g, 0)
    dh1, _, dw1_0, dw2_0 = _mlp_bwd("l0", h1_bf, a0, act0, dy2_bf, dy2, gw["w_ff1"][0], gw["w_ff2"][0])
    dy1, dy1_bf, gs["ln1_g0"], gs["ln1_b0"] = _ln_bwd("l0_ln1_bwd", y1, dh1, ln1_g, 0)
    g0["woa"] = _tiled("l0_dwoa", (2, wide // TM), [_tl(o_att, TM), _cw(dy1_bf, TN)], [_out(wide, D_MODEL, F32, TM, TN)], _mmc(TN_))
    g0["wob"] = _tiled("l0_dwob", (2, 1), [_tl(b_out, SGU_DIM), _cw(dy1_bf, TN)], [_out(SGU_DIM, D_MODEL, F32, SGU_DIM, TN)], _mmc(TN_))
    wo_cat = jnp.concatenate([w["woa"], w["wob"]], axis=0)
    dcat = _tiled("l0_dcat", (3, nt), [_rb(dy1_bf, tm), _rw(wo_cat, TN)], [_out(T, wide + SGU_DIM, BF16, tm, TN)], _mmc(NT))
    dzs, gs["sgu_w"], gs["sgu_ln_g"], gs["sgu_ln_b"], gs["sgu_b"] = _sgu_bwd(zs, dcat, sp["sgu_ln_g"], sp["sgu_ln_b"], sgu_w, bias_full)
    dq, dk, dv = _attn_bwd(q, k, v, o_att, lse, dcat)
    dq_pre = _unrope_heads(dq, tabs)
    lora_w = lambda name, a, d: _tiled(name, (wide // TN, 1), [_tl(a, MLA_LORA), _cw(d, TN)], [_out(MLA_LORA, wide, F32, MLA_LORA, TN)], _mmc(TN_))
    g0["wq"] = lora_w("l0_dwq", cqn, dq_pre)
    g0["wk"] = lora_w("l0_dwk", ckvn, dk)
    g0["wv"] = lora_w("l0_dwv", ckvn, dv)
    dcqn = _tiled("l0_dcqn", (1, nt), [_rb(dq_pre, tm), _rw(w["wq"], MLA_LORA)], [_out(T, MLA_LORA, F32, tm, MLA_LORA)], _mmc(NT))
    dckvn = _tiled("l0_dckvn", (1, nt), [_rb(dk, tm), _rw(w["wk"], MLA_LORA), _rb(dv, tm), _rw(w["wv"], MLA_LORA)],
                   [_out(T, MLA_LORA, F32, tm, MLA_LORA)], _mmc(NT, n_pairs=2))
    dzm, gs["mla_gq"], gs["mla_gkv"] = _mla_pre_bwd(zm, tabs, gq, gkv, dcqn, dckvn, dk)
    g0["wm"] = _tiled("l0_dwm", (1, D_MODEL // TM), [_tl(x, TM), _cw(dzm, 640)], [_out(D_MODEL, 640, F32, TM, 640)], _mmc(TN_))
    g0["ws"] = _tiled("l0_dws", (2, D_MODEL // TM), [_tl(x, TM), _cw(dzs, TN)], [_out(D_MODEL, 1024, F32, TM, TN)], _mmc(TN_))
    dx = _tiled("l0_dx", (2, nt), [_rb(dzm, tm), _rw(w["wm"], TN), _rb(dzs, tm), _rw(w["ws"], TN), _rbj(dy1, tm, TN)],
                [_out(T, D_MODEL, F32, tm, TN)], _mmc(NT, n_pairs=2, epilogue=resid))

    grads = _unprep_grads(g0)
    grads.update(w_in_o=d_in_o, w_out_o=d_out_o, w_ff1=[dw1_0, dw1_1], w_ff2=[dw2_0, dw2_1])
    return sq_err, dx, grads, gs


def _me():
    return lax.axis_index("x"), lax.axis_index("y"), lax.axis_index("c")


def _hbm_call(name, kern, operands, out_shape, n_sems, extra_scratch=()):
    any_spec = pl.BlockSpec(memory_space=pl.ANY)
    return pl.pallas_call(
        kern, name=name, out_shape=out_shape, in_specs=[any_spec] * len(operands), out_specs=[any_spec] * len(out_shape),
        scratch_shapes=[pltpu.SemaphoreType.DMA((n_sems,)), pltpu.SemaphoreType.DMA((n_sems,)), *extra_scratch],
    )(*[_hbm(a) for a in operands])


def _all_gather(shards):
    n = len(shards)

    def kern(*refs):
        x_refs, out_refs, (send_sems, recv_sems, local_sems) = refs[:n], refs[n:2 * n], refs[2 * n:]
        x, y, c = _me()
        me, sibling = (x, y, c), (x, y, 1 - c)
        chips = [(1 - x, y), (x, 1 - y), (1 - x, 1 - y)]

        def copy(op, k, block, to, own=False):
            slot = out_refs[op].at[4 * block[0] + 2 * block[1] + block[2]]
            return pltpu.make_async_remote_copy(
                src_ref=x_refs[op] if own else slot, dst_ref=slot, send_sem=send_sems.at[7 * op + k],
                recv_sem=recv_sems.at[7 * op + k], device_id=to, device_id_type=MESH)

        mine = [pltpu.make_async_copy(x_refs[op], out_refs[op].at[4 * x + 2 * y + c], local_sems.at[op]) for op in range(n)]
        for cp in mine:
            cp.start()
        first = []
        for op in range(n):
            first.append(copy(op, 0, me, sibling, own=True))
            first += [copy(op, 1 + j, me, (*chip, c), own=True) for j, chip in enumerate(chips)]
        for cp in first:
            cp.start()
        passed = []
        for j, chip in enumerate(chips):
            for op in range(n):
                copy(op, 1 + j, (*chip, c), me).wait_recv()
                passed.append(copy(op, 4 + j, (*chip, c), sibling))
                passed[-1].start()
        for op in range(n):
            copy(op, 0, sibling, me).wait_recv()
            for j, chip in enumerate(chips):
                copy(op, 4 + j, (*chip, 1 - c), me).wait_recv()
        for cp in first + passed:
            cp.wait_send()
        for cp in mine:
            cp.wait()

    out_shape = [pltpu.HBM((N_DEV, *s.shape), s.dtype) for s in shards]
    return _hbm_call("weights_all_gather", kern, shards, out_shape, 7 * n, [pltpu.SemaphoreType.DMA((n,))])


def _rs_sibling(grads):
    n = len(grads)

    def kern(*refs):
        g_refs, out_refs, (send_sems, recv_sems) = refs[:n], refs[n:2 * n], refs[2 * n:]
        x, y, c = _me()
        copies = [pltpu.make_async_remote_copy(
            src_ref=g_refs[op].at[k, 1 - c], dst_ref=out_refs[op].at[k], send_sem=send_sems.at[4 * op + k],
            recv_sem=recv_sems.at[4 * op + k], device_id=(x, y, 1 - c), device_id_type=MESH) for op in range(n) for k in range(4)]
        for cp in copies:
            cp.start()
        for cp in copies:
            cp.wait()

    out_shape = [pltpu.HBM((4, *g.shape[2:]), g.dtype) for g in grads]
    return _hbm_call("grads_to_sibling", kern, grads, out_shape, 4 * n)


def _rs_chips(sums):
    n = len(sums)

    def kern(*refs):
        p_refs, out_refs, (send_sems, recv_sems) = refs[:n], refs[n:2 * n], refs[2 * n:]
        x, y, c = _me()
        chips = [(1 - x, y), (x, 1 - y), (1 - x, 1 - y)]
        copies = [pltpu.make_async_remote_copy(
            src_ref=p_refs[op].at[2 * cx + cy], dst_ref=out_refs[op].at[j], send_sem=send_sems.at[3 * op + j],
            recv_sem=recv_sems.at[3 * op + j], device_id=(cx, cy, c), device_id_type=MESH)
            for op in range(n) for j, (cx, cy) in enumerate(chips)]
        for cp in copies:
            cp.start()
        for cp in copies:
            cp.wait()

    out_shape = [pltpu.HBM((3, *p.shape[1:]), p.dtype) for p in sums]
    return _hbm_call("grads_between_chips", kern, sums, out_shape, 3 * n)


def _row_tile(r):
    return r if r <= 256 else 256


def _chip_sum(name, g, from_sibling, core):
    _, _, R, W = g.shape
    tr = _row_tile(R)

    def kern(core_ref, g_ref, s_ref, o_ref):
        o_ref[...] = (g_ref[...] + s_ref[...]).astype(BF16)

    return pl.pallas_call(
        kern, name=name, out_shape=pltpu.HBM((4, R, W), BF16),
        grid_spec=pltpu.PrefetchScalarGridSpec(
            num_scalar_prefetch=1, grid=(4, R // tr),
            in_specs=[pl.BlockSpec((None, None, tr, W), lambda k, i, core: (k, core[0], i, 0)),
                      pl.BlockSpec((None, tr, W), lambda k, i, core: (k, i, 0))],
            out_specs=pl.BlockSpec((None, tr, W), lambda k, i, core: (k, i, 0))),
        compiler_params=pltpu.CompilerParams(dimension_semantics=("parallel", "parallel"), vmem_limit_bytes=_vmem(3 * tr * W * 4)),
    )(core, _hbm(g), _hbm(from_sibling))


def _adamw(w, g, m, v):
    m = ADAM_B1 * m + (1.0 - ADAM_B1) * g
    v = ADAM_B2 * v + (1.0 - ADAM_B2) * (g * g)
    m_hat = m / (1.0 - ADAM_B1 ** ADAM_STEP)
    v_hat = v / (1.0 - ADAM_B2 ** ADAM_STEP)
    return -ADAM_LR * (m_hat / (jnp.sqrt(v_hat) + ADAM_EPS) + ADAM_WD * w), m, v


def _finish_sharded(name, layers, w, m, v, where):
    nl, R, W = w.shape
    tr = _row_tile(R)

    def kern(where_ref, *refs):
        w_ref, m_ref, v_ref, go_ref, d_ref, mo_ref, vo_ref = refs[3 * nl:]
        for l in range(nl):
            g_ref, s_ref, c_ref = refs[3 * l:3 * l + 3]
            grad = g_ref[...] + s_ref[...]
            for j in range(3):
                grad = grad + c_ref[j].astype(F32)
            go_ref[l] = grad
            d_ref[l], mo_ref[l], vo_ref[l] = _adamw(w_ref[l], grad, m_ref[l], v_ref[l])

    row = pl.BlockSpec((nl, tr, W), lambda i, wh: (0, i, 0))
    in_specs, args = [], []
    for g, s, c in layers:
        in_specs += [pl.BlockSpec((None, None, tr, W), lambda i, wh: (wh[0], wh[1], i, 0)),
                     pl.BlockSpec((None, tr, W), lambda i, wh: (wh[0], i, 0)),
                     pl.BlockSpec((3, tr, W), lambda i, wh: (0, i, 0))]
        args += [g, s, c]
    return pl.pallas_call(
        kern, name=name, out_shape=[pltpu.HBM((nl, R, W), F32)] * 4,
        grid_spec=pltpu.PrefetchScalarGridSpec(num_scalar_prefetch=1, grid=(R // tr,), in_specs=in_specs + [row, row, row],
                                               out_specs=[row, row, row, row]),
        compiler_params=pltpu.CompilerParams(dimension_semantics=("parallel",), vmem_limit_bytes=_vmem(nl * 11 * tr * W * 4)),
    )(where, *[_hbm(a) for a in (*args, w, m, v)])


SMALL_PLACE = (("mla_gq", 0, 0, 1, 256), ("mla_gkv", 0, 256, 1, 256), ("sgu_ln_g", 0, 512, 1, 512), ("sgu_ln_b", 1, 0, 1, 512),
               ("hg_lb", 2, 0, 2, 1024), ("ln1_g", 4, 0, 2, 1024), ("ln1_b", 6, 0, 2, 1024), ("sgu_b", 8, 0, 4, 128),
               ("ln2_g", 12, 0, 2, 1024), ("ln2_b", 14, 0, 2, 1024), ("hg_gnorm", 16, 0, 1, 1024))
SMALL_BUF_ROWS = 24


def _small_reduce_adamw(gs, given):
    pieces = [(gs["mla_gq"], 0, 0), (gs["mla_gkv"], 0, 256), (gs["sgu_ln_g"], 0, 512), (gs["sgu_ln_b"], 1, 0), (gs["hg_lb"], 2, 0),
              (gs["ln1_g0"], 4, 0), (gs["ln1_g1"], 5, 0), (gs["ln1_b0"], 6, 0), (gs["ln1_b1"], 7, 0), (gs["sgu_b"], 8, 0),
              (gs["ln2_g0"], 12, 0), (gs["ln2_g1"], 13, 0), (gs["ln2_b0"], 14, 0), (gs["ln2_b1"], 15, 0), (gs["hg_gnorm"], 16, 0)]
    names = [p[0] for p in SMALL_PLACE] + ["sgu_w"]
    n_p, n_names = len(pieces), len(names)
    wmv = [given[pre + name] for name in names for pre in ("", "m_", "v_")]

    def kern(*refs):
        piece_refs, gw_ref = refs[:n_p], refs[n_p]
        wmv_refs = refs[n_p + 1:n_p + 1 + 3 * n_names]
        out_refs = refs[n_p + 1 + 3 * n_names:n_p + 1 + 7 * n_names]
        buf_a, buf_b, send_sems, recv_sems = refs[n_p + 1 + 7 * n_names:]
        px, py, pc = _me()
        me = 4 * px + 2 * py + pc
        mine_a, mine_b = buf_a.at[me], buf_b.at[me]
        mine_a[...] = jnp.zeros(mine_a.shape, F32)
        for ref, (_, r, l0) in zip(piece_refs, pieces):
            mine_a[r:r + ref.shape[0], l0:l0 + ref.shape[1]] = ref[...]
        mine_b[...] = gw_ref[...]
        copies = []
        for r in range(1, N_DEV):
            peer = (px ^ (r >> 2), py ^ ((r >> 1) & 1), pc ^ (r & 1))
            for k, mine in enumerate((mine_a, mine_b)):
                copies.append(pltpu.make_async_remote_copy(
                    src_ref=mine, dst_ref=mine, send_sem=send_sems.at[2 * (r - 1) + k], recv_sem=recv_sems.at[2 * (r - 1) + k],
                    device_id=peer, device_id_type=MESH))
        for cp in copies:
            cp.start()
        for r in range(1, N_DEV):
            for k, buf in enumerate((buf_a, buf_b)):
                theirs = buf.at[me ^ r]
                pltpu.make_async_remote_copy(
                    src_ref=theirs, dst_ref=theirs, send_sem=send_sems.at[2 * (r - 1) + k], recv_sem=recv_sems.at[2 * (r - 1) + k],
                    device_id=(px, py, pc), device_id_type=MESH).wait_recv()
        for cp in copies:
            cp.wait_send()
        sum_a, sum_b = buf_a[0], buf_b[0]
        for d in range(1, N_DEV):
            sum_a, sum_b = sum_a + buf_a[d], sum_b + buf_b[d]

        def own_block(full):
            acc = full[:, 0:128]
            for b in range(1, N_DEV):
                acc = jnp.where(me == b, full[:, b * 128:(b + 1) * 128], acc)
            return acc

        for idx, name in enumerate(names):
            w_ref, m_ref, v_ref = wmv_refs[3 * idx:3 * idx + 3]
            if name == "sgu_w":
                grad = sum_b[None]
            else:
                _, r, l0, nr, nl = SMALL_PLACE[idx]
                grad = sum_a[r:r + nr, l0:l0 + nl]
                if name == "hg_gnorm":
                    grad = own_block(grad)
                if name == "sgu_b":
                    grad = grad[None]
            res = (grad, *_adamw(w_ref[...], grad, m_ref[...], v_ref[...]))
            for o_ref, val in zip(out_refs[4 * idx:4 * idx + 4], res):
                o_ref[...] = val

    vmem = pl.BlockSpec(memory_space=pltpu.VMEM)
    operands = [p[0] for p in pieces] + [gs["sgu_w"]] + wmv
    out_shape = [jax.ShapeDtypeStruct(given[name].shape, F32) for name in names for _ in range(4)]
    res = pl.pallas_call(
        kern, name="small_all_reduce_adamw", out_shape=out_shape, in_specs=[vmem] * len(operands), out_specs=[vmem] * len(out_shape),
        scratch_shapes=[pltpu.VMEM((N_DEV, SMALL_BUF_ROWS, D_MODEL), F32), pltpu.VMEM((N_DEV, SGU_G, 128, 128), F32),
                        pltpu.SemaphoreType.DMA((14,)), pltpu.SemaphoreType.DMA((14,))],
    )(*operands)
    return {name: res[4 * idx:4 * idx + 4] for idx, name in enumerate(names)}


SHARDED = ("w_in_e", "w_qb", "w_kvb", "w_out_e", "w_in_o", "w_out_o", "w_ff1", "w_ff2")


def kernel(x, positions, w_in_e, mla_gq, mla_gkv, w_qb, w_kvb, sgu_ln_g, sgu_ln_b, sgu_w, sgu_b, w_out_e, w_in_o, hg_lb, hg_gnorm, w_out_o, ln1_g, ln1_b, w_ff1, w_ff2, ln2_g, ln2_b, loss_target, m_w_in_e, m_mla_gq, m_mla_gkv, m_w_qb, m_w_kvb, m_sgu_ln_g, m_sgu_ln_b, m_sgu_w, m_sgu_b, m_w_out_e, m_w_in_o, m_hg_lb, m_hg_gnorm, m_w_out_o, m_ln1_g, m_ln1_b, m_w_ff1, m_w_ff2, m_ln2_g, m_ln2_b, v_w_in_e, v_mla_gq, v_mla_gkv, v_w_qb, v_w_kvb, v_sgu_ln_g, v_sgu_ln_b, v_sgu_w, v_sgu_b, v_w_out_e, v_w_in_o, v_hg_lb, v_hg_gnorm, v_w_out_o, v_ln1_g, v_ln1_b, v_w_ff1, v_w_ff2, v_ln2_g, v_ln2_b):
    given = dict(locals())
    px, py, pc = _me()

    names = ["w_in_e", "w_qb", "w_kvb", "w_out_e", "w_in_o", "w_out_o"]
    shards = [given[n][0].astype(BF16) for n in names]
    shards += [w_ff1[0].astype(BF16), w_ff1[1].astype(BF16), w_ff2[0].astype(BF16), w_ff2[1].astype(BF16), hg_gnorm]
    got = _all_gather(shards)
    gw = dict(zip(names, got[:6]))
    gw["w_ff1"], gw["w_ff2"] = [got[6], got[7]], [got[8], got[9]]
    small_names = ["mla_gq", "mla_gkv", "sgu_ln_g", "sgu_ln_b", "sgu_w", "sgu_b", "hg_lb", "ln1_g", "ln1_b", "ln2_g", "ln2_b"]
    sp = {n: given[n] for n in small_names}
    sp["hg_gnorm"] = got[10].reshape(1, D_MODEL)

    sq_err, dx, grads, gs = _local_step(x[0], positions[0], loss_target[0], gw, sp)
    loss = lax.psum(0.5 * jnp.sum(sq_err) / D_MODEL, ("x", "y", "c"))

    flat = [grads[n] for n in names] + grads["w_ff1"] + grads["w_ff2"]
    blocks = [g.reshape(4, 2, *g.shape[1:]) for g in flat]
    from_sibling = _rs_sibling(blocks)
    core = pc.reshape(1).astype(jnp.int32)
    chip_sums = [_chip_sum(f"grads_chip_sum_{k}", b, s, core) for k, (b, s) in enumerate(zip(blocks, from_sibling))]
    from_chips = _rs_chips(chip_sums)
    where = jnp.stack([2 * px + py, pc]).astype(jnp.int32)
    layers = list(zip(blocks, from_sibling, from_chips))
    per_weight = dict(zip(names, [[l] for l in layers[:6]]))
    per_weight["w_ff1"], per_weight["w_ff2"] = layers[6:8], layers[8:10]
    results = {n: _finish_sharded(f"finish_{n}", per_weight[n], given[n], given["m_" + n], given["v_" + n], where) for n in SHARDED}

    results.update(_small_reduce_adamw(gs, given))

    order = ["w_in_e", "mla_gq", "mla_gkv", "w_qb", "w_kvb", "sgu_ln_g", "sgu_ln_b", "sgu_w", "sgu_b", "w_out_e", "w_in_o",
             "hg_lb", "hg_gnorm", "w_out_o", "ln1_g", "ln1_b", "w_ff1", "w_ff2", "ln2_g", "ln2_b"]
    return (loss, dx[None], *[results[name][kind] for kind in range(4) for name in order])
```

```python
import functools
import math

import jax
import jax.numpy as jnp
import numpy as np
from jax import lax
from jax.experimental import pallas as pl
from jax.experimental.pallas import tpu as pltpu

F32 = jnp.float32
BF16 = jnp.bfloat16
MESH = pl.DeviceIdType.MESH
HIGHEST = lax.Precision.HIGHEST

D_MODEL = 1024
D_FF = 4096
N_DEV = 8
HEADS = 8
HEAD_W = 128
MLA_NOPE = 64
MLA_ROPE = 32
MLA_V = 64
MLA_LORA = 256
MLA_SCALE = (MLA_NOPE + MLA_ROPE) ** -0.5
ROPE_BASE = 10000.0
SGU_DIM = 512
SGU_G = 4
SGU_CHUNK = 128
HG_CHUNK = 64
ALPHA = (2 * 2) ** 0.25
EPS = 1e-5
ADAM_LR, ADAM_B1, ADAM_B2, ADAM_EPS, ADAM_WD, ADAM_STEP = 0.001, 0.9, 0.999, 1e-08, 0.01, 10

VMEM_CAP_V7X = 56 * 2**20
VMEM_SLACK = 12 * 2**20
TM = 512
TN = 512


def _vmem(block_bytes):
    return int(min(VMEM_CAP_V7X, 2 * block_bytes + VMEM_SLACK))


def _hbm(a):
    return pltpu.with_memory_space_constraint(a, pltpu.HBM)


def _nbytes(shape, dtype):
    return int(np.prod([d for d in shape if d is not None])) * jnp.dtype(dtype).itemsize


def _sig(x):
    return 1.0 / (1.0 + jnp.exp(-x))


def _gelu(x):
    c = math.sqrt(2.0 / math.pi)
    t = jnp.tanh(c * (x + 0.044715 * x * x * x))
    return 0.5 * x * (1.0 + t), t


def _gelu_grad(x, t):
    c = math.sqrt(2.0 / math.pi)
    return 0.5 * (1.0 + t) + 0.5 * x * (1.0 - t * t) * c * (1.0 + 3 * 0.044715 * x * x)


def _dot(a, b, dims, precision=None):
    return lax.dot_general(a, b, (dims, ((), ())), preferred_element_type=F32, precision=precision)


NN = ((1,), (0,))
NT = ((1,), (1,))
TN_ = ((0,), (0,))


def _tiled(name, grid, ins, outs, compute):
    n_in = len(ins)

    def kern(*refs):
        for o_ref, r in zip(refs[n_in:], compute(*refs[:n_in])):
            o_ref[...] = r.astype(o_ref.dtype).reshape(o_ref.shape)

    swap = lambda f: (lambda j, i: f(i, j))
    nbytes = sum(_nbytes(blk, a.dtype) for a, blk, _ in ins) + sum(_nbytes(blk, dt) + _nbytes(blk, F32) for _, dt, blk, _ in outs)
    res = pl.pallas_call(
        kern, name=name, grid=grid,
        in_specs=[pl.BlockSpec(blk, swap(f)) for _, blk, f in ins],
        out_specs=[pl.BlockSpec(blk, swap(f)) for _, _, blk, f in outs],
        out_shape=[pltpu.HBM(shape, dt) for shape, dt, _, _ in outs],
        compiler_params=pltpu.CompilerParams(dimension_semantics=("parallel", "parallel"), vmem_limit_bytes=_vmem(nbytes)),
    )(*[_hbm(a) for a, _, _ in ins])
    return res if len(res) > 1 else res[0]


def _rb(a, tm, w=None, cb=0):
    return (a, (tm, a.shape[1] if w is None else w), lambda i, j: (i, cb))


def _rbj(a, tm, tn):
    return (a, (tm, tn), lambda i, j: (i, j))


def _cw(b, tn):
    return (b, (b.shape[0], tn), lambda i, j: (0, j))


def _rw(b, tn):
    return (b, (tn, b.shape[1]), lambda i, j: (j, 0))


def _tl(a, tm):
    return (a, (a.shape[0], tm), lambda i, j: (0, i))


def _gcw(g):
    return (g, (None, g.shape[1], g.shape[2]), lambda i, j: (j, 0, 0))


def _grw(g, tn):
    return (g, (N_DEV, tn, g.shape[2]), lambda i, j: (0, j, 0))


def _out(m, n, dtype, tm, tn):
    return ((m, n), dtype, (tm, tn), lambda i, j: (i, j))


def _out_dev(k, n, tm):
    return ((N_DEV, k, n), F32, (None, tm, n), lambda i, j: (j, i, 0))


def _mmc(dims, n_pairs=1, epilogue=None):
    def compute(*refs):
        acc = None
        for k in range(n_pairs):
            d = _dot(refs[2 * k][...].astype(BF16), refs[2 * k + 1][...].astype(BF16), dims)
            acc = d if acc is None else acc + d
        ext = [r[...] for r in refs[2 * n_pairs:]]
        return epilogue(acc, *ext) if epilogue is not None else (acc,)

    return compute


def _mmc_dev(epilogue=None):
    def compute(a_ref, b_ref, *ext_refs):
        n = b_ref.shape[2]
        acc = None
        for d in range(N_DEV):
            t = _dot(a_ref[:, d * n:(d + 1) * n].astype(BF16), b_ref[d].astype(BF16), NT)
            acc = t if acc is None else acc + t
        ext = [r[...] for r in ext_refs]
        return epilogue(acc, *ext) if epilogue is not None else (acc,)

    return compute


def _rowwise(name, body, rows, consts, out_rows, out_accs=(), tr=512):
    T = rows[0][0].shape[0]
    tr = min(tr, T)
    nr, ncn, no = len(rows), len(consts), len(out_rows)

    def kern(*refs):
        accs = refs[nr + ncn + no:]
        if accs:
            @pl.when(pl.program_id(0) == 0)
            def _():
                for a in accs:
                    a[...] = jnp.zeros(a.shape, a.dtype)
        body(refs[:nr], refs[nr:nr + ncn], refs[nr + ncn:nr + ncn + no], accs)

    in_specs = [pl.BlockSpec((tr, w), functools.partial(lambda i, cb: (i, cb), cb=cb)) for _, w, cb in rows]
    in_specs += [pl.BlockSpec(c.shape, functools.partial(lambda i, nd: (0,) * nd, nd=c.ndim)) for c in consts]
    out_specs = [pl.BlockSpec((tr, w), lambda i: (i, 0)) for w, _ in out_rows]
    out_specs += [pl.BlockSpec(s, functools.partial(lambda i, nd: (0,) * nd, nd=len(s))) for s, _ in out_accs]
    out_shape = [pltpu.HBM((T, w), dt) for w, dt in out_rows]
    out_shape += [pltpu.HBM(s, dt) for s, dt in out_accs]
    nbytes = sum(_nbytes((tr, w), a.dtype) for a, w, _ in rows) + sum(_nbytes(c.shape, c.dtype) for c in consts)
    nbytes += sum(_nbytes((tr, w), dt) for w, dt in out_rows) + sum(_nbytes(s, dt) for s, dt in out_accs)
    res = pl.pallas_call(
        kern, name=name, grid=(T // tr,), in_specs=in_specs, out_specs=out_specs, out_shape=out_shape,
        compiler_params=pltpu.CompilerParams(dimension_semantics=("arbitrary",), vmem_limit_bytes=_vmem(nbytes)),
    )(*[_hbm(a) for a, _, _ in rows], *[_hbm(c) for c in consts])
    return res if len(res) > 1 else res[0]


def _full(a):
    return (a, a.shape[1], 0)


def _ln_stats(y):
    mu = jnp.mean(y, axis=-1, keepdims=True)
    yc = y - mu
    r = lax.rsqrt(jnp.mean(yc * yc, axis=-1, keepdims=True) + EPS)
    return yc * r, r


def _ln_fwd(name, h_in, mix, g, b, layer):
    def body(rows, consts, outs, accs):
        y = ALPHA * rows[0][...] + rows[1][...]
        xh, _ = _ln_stats(y)
        h = xh * consts[0][layer:layer + 1, :] + consts[1][layer:layer + 1, :]
        outs[0][...] = y
        outs[1][...] = h
        outs[2][...] = h.astype(BF16)

    return _rowwise(name, body, [_full(h_in), _full(mix)], [g, b], [(D_MODEL, F32), (D_MODEL, F32), (D_MODEL, BF16)], tr=256)


def _ln_loss(name, h_in, mix, g, b, layer, target):
    def body(rows, consts, outs, accs):
        y = ALPHA * rows[0][...] + rows[1][...]
        xh, _ = _ln_stats(y)
        err = xh * consts[0][layer:layer + 1, :] + consts[1][layer:layer + 1, :] - rows[2][...]
        outs[0][...] = y
        outs[1][...] = err * (1.0 / D_MODEL)
        accs[0][...] += jnp.sum(err * err, axis=0, keepdims=True)

    return _rowwise(name, body, [_full(h_in), _full(mix), _full(target)], [g, b], [(D_MODEL, F32), (D_MODEL, F32)],
                    [((1, D_MODEL), F32)], tr=256)


def _ln_bwd(name, y, dh, g, layer):
    def body(rows, consts, outs, accs):
        xh, r = _ln_stats(rows[0][...])
        d = rows[1][...]
        accs[0][...] += jnp.sum(d * xh, axis=0, keepdims=True)
        accs[1][...] += jnp.sum(d, axis=0, keepdims=True)
        dx = d * consts[0][layer:layer + 1, :]
        dy = r * (dx - jnp.mean(dx, axis=-1, keepdims=True) - xh * jnp.mean(dx * xh, axis=-1, keepdims=True))
        outs[0][...] = dy
        outs[1][...] = dy.astype(BF16)

    return _rowwise(name, body, [_full(y), _full(dh)], [g], [(D_MODEL, F32), (D_MODEL, BF16)],
                    [((1, D_MODEL), F32), ((1, D_MODEL), F32)], tr=256)


def _relu2_epilogue(acc):
    a = jnp.maximum(acc, 0.0)
    return acc, a * a


def _mlp_fwd(tag, h_bf, w1, w2):
    T = h_bf.shape[0]
    tm = min(TM, T)
    a, act = _tiled(f"{tag}_ff1", (N_DEV, T // tm), [_rb(h_bf, tm), _gcw(w1)],
                    [_out(T, D_FF, BF16, tm, TN), _out(T, D_FF, BF16, tm, TN)], _mmc(NN, epilogue=_relu2_epilogue))
    ff = _tiled(f"{tag}_ff2", (D_MODEL // TN, T // tm), [_rb(act, tm), _cw(w2.reshape(D_FF, D_MODEL), TN)],
                [_out(T, D_MODEL, F32, tm, TN)], _mmc(NN))
    return a, act, ff


def _mlp_bwd(tag, h_bf, a, act, dff_bf, dy, w1, w2):
    T = h_bf.shape[0]
    tm = min(TM, T)
    da = _tiled(f"{tag}_dact", (N_DEV, T // tm), [_rb(dff_bf, tm), _rw(w2.reshape(D_FF, D_MODEL), TN), _rbj(a, tm, TN)],
                [_out(T, D_FF, BF16, tm, TN)],
                _mmc(NT, epilogue=lambda acc, a_t: (acc * 2.0 * jnp.maximum(a_t.astype(F32), 0.0),)))
    dw2 = _tiled(f"{tag}_dw2", (D_MODEL // TN, D_FF // TM), [_tl(act, TM), _cw(dff_bf, TN)],
                 [_out(D_FF, D_MODEL, F32, TM, TN)], _mmc(TN_)).reshape(N_DEV, D_FF // N_DEV, D_MODEL)
    dw1 = _tiled(f"{tag}_dw1", (N_DEV, D_MODEL // TM), [_tl(h_bf, TM), _cw(da, TN)], [_out_dev(D_MODEL, TN, TM)], _mmc(TN_))
    dh, dh_bf = _tiled(f"{tag}_dh", (D_MODEL // TN, T // tm), [_rb(da, tm), _grw(w1, TN), _rbj(dy, tm, TN)],
                       [_out(T, D_MODEL, F32, tm, TN), _out(T, D_MODEL, BF16, tm, TN)],
                       _mmc_dev(epilogue=lambda acc, dy_t: (acc + ALPHA * dy_t,) * 2))
    return dh, dh_bf, dw1, dw2


def _rope_tables(positions_col, invf_lane):
    def body(rows, consts, outs, accs):
        ang = rows[0][...].astype(F32) * consts[0][...]
        c, s = jnp.cos(ang), jnp.sin(ang)
        lane = lax.broadcasted_iota(jnp.int32, ang.shape, 1)
        outs[0][...] = jnp.where(lane < 64, 1.0, jnp.where(lane < 96, c, 0.0))
        outs[1][...] = jnp.where((lane >= 64) & (lane < 80), -s, 0.0)
        outs[2][...] = jnp.where((lane >= 80) & (lane < 96), s, 0.0)

    return _rowwise("rope_tables", body, [_full(positions_col)], [invf_lane], [(HEAD_W, F32)] * 3)


def _rope(x, c, s1, s2):
    return x * c + pltpu.roll(x, 112, 1) * s1 + pltpu.roll(x, 16, 1) * s2


def _rope_t(dx, c, s1, s2):
    return dx * c + pltpu.roll(dx * s1, 16, 1) + pltpu.roll(dx * s2, 112, 1)


def _rms(c):
    r = lax.rsqrt(jnp.mean(c * c, axis=-1, keepdims=True) + EPS)
    return c * r, r


def _mla_pre(zm, tabs, gq, gkv):
    def body(rows, consts, outs, accs):
        cq, _ = _rms(rows[0][...])
        ckv, _ = _rms(rows[1][...])
        outs[0][...] = (cq * consts[0][...]).astype(BF16)
        outs[1][...] = (ckv * consts[1][...]).astype(BF16)
        outs[2][...] = _rope(rows[2][...], rows[3][...], rows[4][...], rows[5][...])

    rows = [(zm, 256, 0), (zm, 256, 1), (zm, 128, 4)] + [_full(t) for t in tabs]
    return _rowwise("mla_pre", body, rows, [gq, gkv], [(256, BF16), (256, BF16), (HEAD_W, F32)])


def _mla_pre_bwd(zm, tabs, gq, gkv, dcqn, dckvn, dk):
    def body(rows, consts, outs, accs):
        res = []
        for k in range(2):
            ch, r = _rms(rows[k][...])
            d = rows[5 + k][...]
            accs[k][...] += jnp.sum(d * ch, axis=0, keepdims=True)
            dc = d * consts[k][...]
            res.append(r * (dc - ch * jnp.mean(dc * ch, axis=-1, keepdims=True)))
        dks = rows[7][:, 0:HEAD_W]
        for h in range(1, HEADS):
            dks = dks + rows[7][:, h * HEAD_W:(h + 1) * HEAD_W]
        lane = lax.broadcasted_iota(jnp.int32, dks.shape, 1)
        dks = jnp.where((lane >= 64) & (lane < 96), dks, 0.0)
        dkr = _rope_t(dks, rows[2][...], rows[3][...], rows[4][...])
        outs[0][:, 0:256] = res[0].astype(BF16)
        outs[0][:, 256:512] = res[1].astype(BF16)
        outs[0][:, 512:640] = dkr.astype(BF16)

    rows = [(zm, 256, 0), (zm, 256, 1)] + [_full(t) for t in tabs] + [_full(dcqn), _full(dckvn), _full(dk)]
    return _rowwise("mla_pre_bwd", body, rows, [gq, gkv], [(640, BF16)], [((1, 256), F32), ((1, 256), F32)])


def _rope_heads(x, c, s1, s2, fn):
    return jnp.concatenate([fn(x[:, h * HEAD_W:(h + 1) * HEAD_W], c, s1, s2) for h in range(HEADS)], axis=1)


def _unrope_heads(dq, tabs):
    def body(rows, consts, outs, accs):
        outs[0][...] = _rope_heads(rows[0][...], rows[1][...], rows[2][...], rows[3][...], _rope_t).astype(BF16)

    return _rowwise("l0_dq_rope", body, [_full(dq)] + [_full(t) for t in tabs], [], [(HEADS * HEAD_W, BF16)])


def _attn_block(T):
    return min(1024, T)


def _attn_fwd(q, k, v):
    T = q.shape[0]
    BQ = _attn_block(T)
    nq = T // BQ

    def kern(q_ref, k_ref, v_ref, o_ref, lse_ref):
        def step(i, j, carry, masked):
            m, l, acc = carry
            qb = q_ref[pl.ds(pl.multiple_of(i * BQ, BQ), BQ), :]
            kb = k_ref[pl.ds(pl.multiple_of(j * BQ, BQ), BQ), :]
            vb = v_ref[pl.ds(pl.multiple_of(j * BQ, BQ), BQ), :]
            s = _dot(qb, kb, NT) * MLA_SCALE
            if masked:
                row = lax.broadcasted_iota(jnp.int32, s.shape, 0)
                col = lax.broadcasted_iota(jnp.int32, s.shape, 1)
                s = jnp.where(col <= row, s, -1e30)
            m_new = jnp.maximum(m, jnp.max(s, axis=-1, keepdims=True))
            p = jnp.exp(s - m_new)
            a = jnp.exp(m - m_new)
            l = a * l + jnp.sum(p, axis=-1, keepdims=True)
            acc = a * acc + _dot(p.astype(BF16), vb, NN)
            return m_new, l, acc

        def qloop(i, _):
            init = (jnp.full((BQ, 1), -1e30, F32), jnp.zeros((BQ, 1), F32), jnp.zeros((BQ, HEAD_W), F32))
            carry = lax.fori_loop(0, i, lambda j, c: step(i, j, c, False), init)
            m, l, acc = step(i, i, carry, True)
            rows = pl.ds(pl.multiple_of(i * BQ, BQ), BQ)
            o_ref[rows, :] = acc / l
            lse_ref[0, rows, :] = m + jnp.log(l)
            return 0

        lax.fori_loop(0, nq, qloop, 0)

    head = pl.BlockSpec((T, HEAD_W), lambda h: (0, h))
    nbytes = 3 * _nbytes((T, HEAD_W), BF16) + _nbytes((T, HEAD_W), F32) + _nbytes((T, 128), F32)
    return pl.pallas_call(
        kern, name="attn_fwd", grid=(HEADS,), in_specs=[head, head, head],
        out_specs=[head, pl.BlockSpec((1, T, 1), lambda h: (h, 0, 0))],
        out_shape=[pltpu.HBM((T, HEADS * HEAD_W), F32), pltpu.HBM((HEADS, T, 1), F32)],
        compiler_params=pltpu.CompilerParams(dimension_semantics=("parallel",), vmem_limit_bytes=_vmem(nbytes)),
    )(_hbm(q), _hbm(k), _hbm(v))


def _attn_bwd(q, k, v, o, lse, dcat):
    T = q.shape[0]
    BQ = _attn_block(T)
    nq = T // BQ

    def kern(q_ref, k_ref, v_ref, o_ref, lse_ref, do_ref, dq_ref, dk_ref, dv_ref, dd_ref):
        dq_ref[...] = jnp.zeros(dq_ref.shape, F32)

        def dloop(i, _):
            rows = pl.ds(pl.multiple_of(i * BQ, BQ), BQ)
            dd_ref[rows, :] = jnp.sum(do_ref[rows, :].astype(F32) * o_ref[rows, :], axis=-1, keepdims=True)
            return 0

        lax.fori_loop(0, nq, dloop, 0)

        def step(j, i, carry, masked):
            dk_acc, dv_acc = carry
            rq = pl.ds(pl.multiple_of(i * BQ, BQ), BQ)
            rk = pl.ds(pl.multiple_of(j * BQ, BQ), BQ)
            qb, kb, vb, dob = q_ref[rq, :], k_ref[rk, :], v_ref[rk, :], do_ref[rq, :]
            s = _dot(qb, kb, NT) * MLA_SCALE
            p = jnp.exp(s - lse_ref[0, rq, :])
            if masked:
                row = lax.broadcasted_iota(jnp.int32, s.shape, 0)
                col = lax.broadcasted_iota(jnp.int32, s.shape, 1)
                p = jnp.where(col <= row, p, 0.0)
            dp = _dot(dob, vb, NT)
            ds = (p * (dp - dd_ref[rq, :]) * MLA_SCALE).astype(BF16)
            dv_acc = dv_acc + _dot(p.astype(BF16), dob, TN_)
            dk_acc = dk_acc + _dot(ds, qb, TN_)
            dq_ref[rq, :] += _dot(ds, kb, NN)
            return dk_acc, dv_acc

        def kloop(j, _):
            init = (jnp.zeros((BQ, HEAD_W), F32), jnp.zeros((BQ, HEAD_W), F32))
            carry = step(j, j, init, True)
            dk_acc, dv_acc = lax.fori_loop(j + 1, nq, lambda i, c: step(j, i, c, False), carry)
            rk = pl.ds(pl.multiple_of(j * BQ, BQ), BQ)
            dk_ref[rk, :] = dk_acc
            dv_ref[rk, :] = dv_acc
            return 0

        lax.fori_loop(0, nq, kloop, 0)

    head = pl.BlockSpec((T, HEAD_W), lambda h: (0, h))
    nbytes = 4 * _nbytes((T, HEAD_W), BF16) + 5 * _nbytes((T, HEAD_W), F32) + 2 * _nbytes((T, 128), F32)
    return pl.pallas_call(
        kern, name="attn_bwd", grid=(HEADS,),
        in_specs=[head, head, head, head, pl.BlockSpec((1, T, 1), lambda h: (h, 0, 0)), head],
        out_specs=[head, head, head],
        out_shape=[pltpu.HBM((T, HEADS * HEAD_W), F32)] * 3,
        scratch_shapes=[pltpu.VMEM((T, 1), F32)],
        compiler_params=pltpu.CompilerParams(dimension_semantics=("parallel",), vmem_limit_bytes=_vmem(nbytes)),
    )(*[_hbm(a) for a in (q, k, v, o, lse, dcat)])


def _sgu_common(u, v, ln_g, ln_b):
    ua, tu = _gelu(u)
    va, tv = _gelu(v)
    vh, r = _ln_stats(va)
    return ua, tu, tv, vh, r, vh * ln_g + ln_b


def _tril_mask(n):
    return lax.broadcasted_iota(jnp.int32, (n, n), 1) <= lax.broadcasted_iota(jnp.int32, (n, n), 0)


def _sgu_fwd(zs, ln_g, ln_b, w, bias_full):
    def body(rows, consts, outs, accs):
        ua, _, _, _, _, vn = _sgu_common(rows[0][...], rows[1][...], consts[0][...], consts[1][...])
        vn = vn.astype(BF16)
        tri = _tril_mask(SGU_CHUNK)
        for g in range(SGU_G):
            wg = jnp.where(tri, consts[2][0, g], 0.0).astype(BF16)
            cols = slice(g * 128, (g + 1) * 128)
            for c in range(ua.shape[0] // SGU_CHUNK):
                rws = slice(c * SGU_CHUNK, (c + 1) * SGU_CHUNK)
                mixed = _dot(wg, vn[rws, cols], NN) + consts[3][:, cols]
                outs[0][rws, cols] = (ua[rws, cols] * mixed).astype(BF16)

    return _rowwise("sgu_fwd", body, [(zs, 512, 0), (zs, 512, 1)], [ln_g, ln_b, w, bias_full], [(SGU_DIM, BF16)])


def _sgu_bwd(zs, dcat, ln_g, ln_b, w, bias_full):
    def body(rows, consts, outs, accs):
        u, v = rows[0][...], rows[1][...]
        ua, tu, tv, vh, r, vn = _sgu_common(u, v, consts[0][...], consts[1][...])
        dout = rows[2][...].astype(F32)
        vn_bf = vn.astype(BF16)
        tri = _tril_mask(SGU_CHUNK)
        dmixed = (dout * ua)
        dmixed_bf = dmixed.astype(BF16)
        ones = jnp.ones((8, SGU_CHUNK), F32)
        dvn_cols, mixed_cols = [], []
        for g in range(SGU_G):
            wg = jnp.where(tri, consts[2][0, g], 0.0).astype(BF16)
            cols = slice(g * 128, (g + 1) * 128)
            dvn_rows, mixed_rows = [], []
            dw = jnp.zeros((SGU_CHUNK, SGU_CHUNK), F32)
            dmix_sum = jnp.zeros((SGU_CHUNK, 128), F32)
            for c in range(u.shape[0] // SGU_CHUNK):
                rws = slice(c * SGU_CHUNK, (c + 1) * SGU_CHUNK)
                mixed_rows.append(_dot(wg, vn_bf[rws, cols], NN) + consts[3][:, cols])
                dvn_rows.append(_dot(wg, dmixed_bf[rws, cols], TN_))
                dw = dw + _dot(dmixed_bf[rws, cols], vn_bf[rws, cols], NT)
                dmix_sum = dmix_sum + dmixed[rws, cols]
            accs[0][g] += jnp.where(tri, dw, 0.0)
            accs[3][g:g + 1, :] += _dot(ones, dmix_sum, NT, precision=HIGHEST)[0:1, :]
            dvn_cols.append(jnp.concatenate(dvn_rows, axis=0))
            mixed_cols.append(jnp.concatenate(mixed_rows, axis=0))
        dvn = jnp.concatenate(dvn_cols, axis=1)
        mixed = jnp.concatenate(mixed_cols, axis=1)
        accs[1][...] += jnp.sum(dvn * vh, axis=0, keepdims=True)
        accs[2][...] += jnp.sum(dvn, axis=0, keepdims=True)
        dvh = dvn * consts[0][...]
        dva = r * (dvh - jnp.mean(dvh, axis=-1, keepdims=True) - vh * jnp.mean(dvh * vh, axis=-1, keepdims=True))
        outs[0][:, 0:512] = (dout * mixed * _gelu_grad(u, tu)).astype(BF16)
        outs[0][:, 512:1024] = (dva * _gelu_grad(v, tv)).astype(BF16)

    return _rowwise("sgu_bwd", body, [(zs, 512, 0), (zs, 512, 1), (dcat, 512, 2)], [ln_g, ln_b, w, bias_full], [(1024, BF16)],
                    [((SGU_G, 128, 128), F32), ((1, SGU_DIM), F32), ((1, SGU_DIM), F32), ((SGU_G, 128), F32)], tr=256)


def _lower_bound(hg_lb):
    a0, a1 = hg_lb[0:1, :], hg_lb[1:2, :]
    m = jnp.maximum(a0, a1)
    e0, e1 = jnp.exp(a0 - m), jnp.exp(a1 - m)
    s0, s1 = e0 / (e0 + e1), e1 / (e0 + e1)
    return (s0 + s1) - s0, s0, s1


def _hg_gates(qr, fr, lb):
    C = qr.shape[0]
    sq = _sig(qr)
    qf = qr * sq
    sf = _sig(fr)
    gate = lb + (1.0 - lb) * sf
    kk = 1.0 - gate
    tri = _tril_mask(C)
    b = _dot(jnp.where(tri, 1.0, 0.0), jnp.log(gate), NN, precision=HIGHEST)
    bref = b[C // 2 - 1:C // 2, :]
    bl = b[C - 1:C, :]
    e_b = jnp.exp(b)
    e_q = jnp.exp(b - bref)
    e_k = jnp.exp(bref - b)
    e_lb = jnp.exp(bl - b)
    return dict(sq=sq, qf=qf, sf=sf, gate=gate, kk=kk, tri=tri, bl=bl, e_b=e_b, e_q=e_q, e_k=e_k, e_lb=e_lb)


def _hgrn_fwd(z1, hg_lb, gnorm):
    T = z1.shape[0]
    C = min(HG_CHUNK, T)
    nc = T // C

    def kern(q_ref, f_ref, i_ref, g_ref, lb_ref, gn_ref, o_ref, hg_ref, st_ref, s_scr):
        @pl.when(pl.program_id(0) == 0)
        def _():
            s_scr[...] = jnp.zeros(s_scr.shape, F32)

        lb_all, _, _ = _lower_bound(lb_ref[...])
        st_ref[0] = s_scr[...]
        for h in range(HEADS):
            cols = slice(h * HEAD_W, (h + 1) * HEAD_W)
            t = _hg_gates(q_ref[:, cols], f_ref[:, cols], lb_all[:, cols])
            v = i_ref[:, cols]
            v_bf = v.astype(BF16)
            st = s_scr[h]
            a = jnp.where(t["tri"], _dot((t["qf"] * t["e_q"]).astype(BF16), (t["kk"] * t["e_k"]).astype(BF16), NT), 0.0)
            o = _dot(a.astype(BF16), v_bf, NN) + _dot((t["qf"] * t["e_b"]).astype(BF16), st.astype(BF16), NT)
            s_scr[h] = st * jnp.exp(t["bl"]) + _dot(v_bf, (t["kk"] * t["e_lb"]).astype(BF16), TN_)
            o_ref[:, cols] = o
            gr = g_ref[:, cols]
            r = lax.rsqrt(jnp.mean(o * o, axis=-1, keepdims=True) + EPS)
            hg_ref[:, cols] = (o * r * gn_ref[:, cols] * (gr * _sig(gr))).astype(BF16)

    seg = lambda k: pl.BlockSpec((C, D_MODEL), functools.partial(lambda n, k: (n, k), k=k))
    row = pl.BlockSpec((C, D_MODEL), lambda n: (n, 0))
    nbytes = 6 * _nbytes((C, D_MODEL), F32) + 3 * _nbytes((HEADS, 128, 128), F32)
    return pl.pallas_call(
        kern, name="hgrn_fwd", grid=(nc,),
        in_specs=[seg(0), seg(1), seg(2), seg(3), pl.BlockSpec((2, D_MODEL), lambda n: (0, 0)),
                  pl.BlockSpec((1, D_MODEL), lambda n: (0, 0))],
        out_specs=[row, row, pl.BlockSpec((1, HEADS, 128, 128), lambda n: (n, 0, 0, 0))],
        out_shape=[pltpu.HBM((T, D_MODEL), F32), pltpu.HBM((T, D_MODEL), BF16),
                   pltpu.HBM((nc, HEADS, 128, 128), F32)],
        scratch_shapes=[pltpu.VMEM((HEADS, 128, 128), F32)],
        compiler_params=pltpu.CompilerParams(dimension_semantics=("arbitrary",), vmem_limit_bytes=_vmem(nbytes)),
    )(*[_hbm(a) for a in (z1, z1, z1, z1, hg_lb, gnorm)])


def _hgrn_bwd(z1, o_pre, dhg, states, hg_lb, gnorm):
    T = z1.shape[0]
    C = min(HG_CHUNK, T)
    nc = T // C

    def kern(q_ref, f_ref, i_ref, g_ref, o_ref, dhg_ref, st_ref, lb_ref, gn_ref, dz_ref, dlb_ref, dgn_ref, ds_scr, dlb_scr):
        n = pl.program_id(0)

        @pl.when(n == 0)
        def _():
            ds_scr[...] = jnp.zeros(ds_scr.shape, F32)
            dlb_scr[...] = jnp.zeros(dlb_scr.shape, F32)
            dgn_ref[...] = jnp.zeros(dgn_ref.shape, F32)

        lb_all, s0, s1 = _lower_bound(lb_ref[...])
        for h in range(HEADS):
            cols = slice(h * HEAD_W, (h + 1) * HEAD_W)
            lb = lb_all[:, cols]
            qr, fr = q_ref[:, cols], f_ref[:, cols]
            t = _hg_gates(qr, fr, lb)
            tri = t["tri"]
            v_bf = i_ref[:, cols].astype(BF16)
            st_bf = st_ref[0, h].astype(BF16)
            dst = ds_scr[h]
            dst_bf = dst.astype(BF16)
            o = o_ref[:, cols]
            gr = g_ref[:, cols]
            sg = _sig(gr)
            sil = gr * sg
            gn = gn_ref[:, cols]
            r = lax.rsqrt(jnp.mean(o * o, axis=-1, keepdims=True) + EPS)
            on = o * r
            dh = dhg_ref[:, cols].astype(F32)
            dgn_ref[:, cols] += jnp.sum(dh * on * sil, axis=0, keepdims=True)
            dg = dh * on * gn * (sg * (1.0 + gr * (1.0 - sg)))
            don = dh * gn * sil
            do_bf = (r * (don - on * jnp.mean(don * on, axis=-1, keepdims=True))).astype(BF16)
            qe = (t["qf"] * t["e_q"]).astype(BF16)
            ke = (t["kk"] * t["e_k"]).astype(BF16)
            qb = (t["qf"] * t["e_b"]).astype(BF16)
            kh_bf = (t["kk"] * t["e_lb"]).astype(BF16)
            a_bf = jnp.where(tri, _dot(qe, ke, NT), 0.0).astype(BF16)
            da_bf = jnp.where(tri, _dot(do_bf, v_bf, NT), 0.0).astype(BF16)
            dv = _dot(a_bf, do_bf, TN_) + _dot(kh_bf, dst_bf, NT)
            dqe = _dot(da_bf, ke, NN)
            dqb = _dot(do_bf, st_bf, NN)
            dke = _dot(da_bf, qe, TN_)
            dkh = _dot(v_bf, dst_bf, NN)
            dqf = dqe * t["e_q"] + dqb * t["e_b"]
            dkk = dke * t["e_k"] + dkh * t["e_lb"]
            kh_r = kh_bf.astype(F32)
            db = qe.astype(F32) * dqe - ke.astype(F32) * dke + qb.astype(F32) * dqb - kh_r * dkh
            e_bl = jnp.exp(t["bl"])
            dbl = jnp.sum(dkh * kh_r, axis=0, keepdims=True) + e_bl * jnp.sum(st_ref[0, h] * dst, axis=0, keepdims=True)
            dlg = _dot(jnp.where(tri, 1.0, 0.0), db, TN_, precision=HIGHEST) + dbl
            ds_scr[h] = dst * e_bl + _dot(do_bf, qb, TN_)
            dgate = dlg / t["gate"] - dkk
            sf = t["sf"]
            dlb_scr[:, cols] += jnp.sum(dgate * (1.0 - sf), axis=0, keepdims=True)
            df = dgate * (1.0 - lb) * sf * (1.0 - sf)
            dq = dqf * (t["sq"] * (1.0 + qr * (1.0 - t["sq"])))
            dz_ref[:, cols] = dq.astype(BF16)
            dz_ref[:, D_MODEL + h * HEAD_W:D_MODEL + (h + 1) * HEAD_W] = df.astype(BF16)
            dz_ref[:, 2 * D_MODEL + h * HEAD_W:2 * D_MODEL + (h + 1) * HEAD_W] = dv.astype(BF16)
            dz_ref[:, 3 * D_MODEL + h * HEAD_W:3 * D_MODEL + (h + 1) * HEAD_W] = dg.astype(BF16)

        @pl.when(n == nc - 1)
        def _():
            d = s0 * s1 * dlb_scr[...]
            dlb_ref[0:1, :] = -d
            dlb_ref[1:2, :] = d

    seg = lambda k: pl.BlockSpec((C, D_MODEL), functools.partial(lambda n, k: (nc - 1 - n, k), k=k))
    nbytes = 6 * _nbytes((C, D_MODEL), F32) + _nbytes((C, 4 * D_MODEL), BF16) + 3 * _nbytes((HEADS, 128, 128), F32)
    return pl.pallas_call(
        kern, name="hgrn_bwd", grid=(nc,),
        in_specs=[seg(0), seg(1), seg(2), seg(3), seg(0), seg(0),
                  pl.BlockSpec((1, HEADS, 128, 128), lambda n: (nc - 1 - n, 0, 0, 0)),
                  pl.BlockSpec((2, D_MODEL), lambda n: (0, 0)), pl.BlockSpec((1, D_MODEL), lambda n: (0, 0))],
        out_specs=[pl.BlockSpec((C, 4 * D_MODEL), lambda n: (nc - 1 - n, 0)),
                   pl.BlockSpec((2, D_MODEL), lambda n: (0, 0)), pl.BlockSpec((1, D_MODEL), lambda n: (0, 0))],
        out_shape=[pltpu.HBM((T, 4 * D_MODEL), BF16), pltpu.HBM((2, D_MODEL), F32),
                   pltpu.HBM((1, D_MODEL), F32)],
        scratch_shapes=[pltpu.VMEM((HEADS, 128, 128), F32), pltpu.VMEM((1, D_MODEL), F32)],
        compiler_params=pltpu.CompilerParams(dimension_semantics=("arbitrary",), vmem_limit_bytes=_vmem(nbytes)),
    )(*[_hbm(a) for a in (z1, z1, z1, z1, o_pre, dhg, states, hg_lb, gnorm)])


def _prep_weights(gw):
    w_in_e = gw["w_in_e"].transpose(1, 0, 2).reshape(D_MODEL, 1568)
    kr = jnp.pad(w_in_e[:, 512:544], ((0, 0), (64, 32)))
    wm = jnp.concatenate([w_in_e[:, 0:512], kr], axis=1)
    ws = w_in_e[:, 544:1568]
    w_qb = gw["w_qb"].transpose(1, 0, 2).reshape(MLA_LORA, HEADS, 96)
    wq = jnp.pad(w_qb, ((0, 0), (0, 0), (0, 32))).reshape(MLA_LORA, HEADS * HEAD_W)
    kvb = gw["w_kvb"].transpose(1, 0, 2).reshape(MLA_LORA, HEADS, 128)
    wk = jnp.pad(kvb[:, :, :64], ((0, 0), (0, 0), (0, 64))).reshape(MLA_LORA, HEADS * HEAD_W)
    wv = jnp.pad(kvb[:, :, 64:], ((0, 0), (0, 0), (0, 64))).reshape(MLA_LORA, HEADS * HEAD_W)
    w_out_e = gw["w_out_e"].reshape(D_MODEL, D_MODEL)
    woa = jnp.pad(w_out_e[:512].reshape(HEADS, 64, D_MODEL), ((0, 0), (0, 64), (0, 0))).reshape(HEADS * HEAD_W, D_MODEL)
    return dict(wm=wm, ws=ws, wq=wq, wk=wk, wv=wv, woa=woa, wob=w_out_e[512:])


def _unprep_grads(g):
    dwm, dws = g["wm"], g["ws"]
    d_in_e = jnp.concatenate([dwm[:, 0:512], dwm[:, 512 + 64:512 + 96], dws], axis=1)
    d_qb = g["wq"].reshape(MLA_LORA, HEADS, HEAD_W)[:, :, :96].reshape(MLA_LORA, HEADS * 96)
    dk = g["wk"].reshape(MLA_LORA, HEADS, HEAD_W)[:, :, :64]
    dv = g["wv"].reshape(MLA_LORA, HEADS, HEAD_W)[:, :, :64]
    d_kvb = jnp.concatenate([dk, dv], axis=2).reshape(MLA_LORA, HEADS * 128)
    d_oa = g["woa"].reshape(HEADS, HEAD_W, D_MODEL)[:, :64].reshape(HEADS * 64, D_MODEL)
    dev_major = lambda a: a.reshape(a.shape[0], N_DEV, a.shape[1] // N_DEV).transpose(1, 0, 2)
    return dict(w_in_e=dev_major(d_in_e), w_qb=dev_major(d_qb), w_kvb=dev_major(d_kvb),
                w_out_e=jnp.concatenate([d_oa, g["wob"]], axis=0).reshape(N_DEV, D_MODEL // N_DEV, D_MODEL))


def _local_step(x, positions, target, gw, sp):
    w = _prep_weights(gw)
    T = x.shape[0]
    tm = min(TM, T)
    nt = T // tm
    half = MLA_ROPE // 2
    inv_freq = ROPE_BASE ** (-jnp.arange(half, dtype=F32) / half)
    invf_lane = jnp.concatenate([jnp.zeros((64,), F32), inv_freq, inv_freq, jnp.zeros((32,), F32)]).reshape(1, HEAD_W)
    tabs = _rope_tables(positions.reshape(T, 1), invf_lane)
    bias_full = jnp.repeat(sp["sgu_b"][0].T, 128, axis=1)
    sgu_w = sp["sgu_w"]
    gq, gkv = sp["mla_gq"], sp["mla_gkv"]
    ln1_g, ln1_b, ln2_g, ln2_b = sp["ln1_g"], sp["ln1_b"], sp["ln2_g"], sp["ln2_b"]
    w_in_o, w_out_o = gw["w_in_o"], gw["w_out_o"].reshape(D_MODEL, D_MODEL)
    wide = HEADS * HEAD_W
    tab_rows = [_rb(t, tm) for t in tabs]
    resid = lambda acc, d: (acc + ALPHA * d,)

    zm = _tiled("l0_in_mla", (1, nt), [_rb(x, tm), _cw(w["wm"], 640)], [_out(T, 640, F32, tm, 640)], _mmc(NN))
    zs = _tiled("l0_in_sgu", (2, nt), [_rb(x, tm), _cw(w["ws"], TN)], [_out(T, 1024, F32, tm, TN)], _mmc(NN))
    cqn, ckvn, kr_rot = _mla_pre(zm, tabs, gq, gkv)
    q = _tiled("l0_q", (1, nt), [_rb(cqn, tm), _cw(w["wq"], wide)] + tab_rows, [_out(T, wide, BF16, tm, wide)],
               _mmc(NN, epilogue=lambda acc, c, s1, s2: (_rope_heads(acc, c, s1, s2, _rope),)))
    k = _tiled("l0_k", (1, nt), [_rb(ckvn, tm), _cw(w["wk"], wide), _rb(kr_rot, tm)], [_out(T, wide, BF16, tm, wide)],
               _mmc(NN, epilogue=lambda acc, kr: (acc + jnp.concatenate([kr] * HEADS, axis=1),)))
    v = _tiled("l0_v", (1, nt), [_rb(ckvn, tm), _cw(w["wv"], wide)], [_out(T, wide, BF16, tm, wide)], _mmc(NN))
    o_att, lse = _attn_fwd(q, k, v)
    b_out = _sgu_fwd(zs, sp["sgu_ln_g"], sp["sgu_ln_b"], sgu_w, bias_full)
    mix0 = _tiled("l0_out", (2, nt), [_rb(o_att, tm), _cw(w["woa"], TN), _rb(b_out, tm), _cw(w["wob"], TN)],
                  [_out(T, D_MODEL, F32, tm, TN)], _mmc(NN, n_pairs=2))
    y1, h1, h1_bf = _ln_fwd("l0_ln1", x, mix0, ln1_g, ln1_b, 0)
    a0, act0, ff0 = _mlp_fwd("l0", h1_bf, gw["w_ff1"][0], gw["w_ff2"][0])
    y2, h2, h2_bf = _ln_fwd("l0_ln2", h1, ff0, ln2_g, ln2_b, 0)

    z1 = _tiled("l1_in", (N_DEV, nt), [_rb(h2_bf, tm), _gcw(w_in_o)], [_out(T, 4 * D_MODEL, F32, tm, TN)], _mmc(NN))
    o_pre, hg, states = _hgrn_fwd(z1, sp["hg_lb"], sp["hg_gnorm"])
    mix1 = _tiled("l1_out", (2, nt), [_rb(hg, tm), _cw(w_out_o, TN)], [_out(T, D_MODEL, F32, tm, TN)], _mmc(NN))
    y3, h3, h3_bf = _ln_fwd("l1_ln1", h2, mix1, ln1_g, ln1_b, 1)
    a1, act1, ff1 = _mlp_fwd("l1", h3_bf, gw["w_ff1"][1], gw["w_ff2"][1])
    y4, dh4, sq_err = _ln_loss("l1_ln2", h3, ff1, ln2_g, ln2_b, 1, target)

    gs, g0 = {}, {}
    dy4, dy4_bf, gs["ln2_g1"], gs["ln2_b1"] = _ln_bwd("l1_ln2_bwd", y4, dh4, ln2_g, 1)
    dh3, _, dw1_1, dw2_1 = _mlp_bwd("l1", h3_bf, a1, act1, dy4_bf, dy4, gw["w_ff1"][1], gw["w_ff2"][1])
    dy3, dy3_bf, gs["ln1_g1"], gs["ln1_b1"] = _ln_bwd("l1_ln1_bwd", y3, dh3, ln1_g, 1)
    d_out_o = _tiled("l1_dwout", (2, D_MODEL // TM), [_tl(hg, TM), _cw(dy3_bf, TN)], [_out(D_MODEL, D_MODEL, F32, TM, TN)],
                     _mmc(TN_)).reshape(N_DEV, D_MODEL // N_DEV, D_MODEL)
    dhg = _tiled("l1_dhg", (2, nt), [_rb(dy3_bf, tm), _rw(w_out_o, TN)], [_out(T, D_MODEL, BF16, tm, TN)], _mmc(NT))
    dz1, gs["hg_lb"], gs["hg_gnorm"] = _hgrn_bwd(z1, o_pre, dhg, states, sp["hg_lb"], sp["hg_gnorm"])
    d_in_o = _tiled("l1_dwin", (N_DEV, D_MODEL // TM), [_tl(h2_bf, TM), _cw(dz1, TN)], [_out_dev(D_MODEL, TN, TM)], _mmc(TN_))
    dh2 = _tiled("l1_dh2", (2, nt), [_rb(dz1, tm), _grw(w_in_o, TN), _rbj(dy3, tm, TN)], [_out(T, D_MODEL, F32, tm, TN)],
                 _mmc_dev(epilogue=resid))

    dy2, dy2_bf, gs["ln2_g0"], gs["ln2_b0"] = _ln_bwd("l0_ln2_bwd", y2, dh2, ln2_g, 0)
    dh1, _, dw1_0, dw2_0 = _mlp_bwd("l0", h1_bf, a0, act0, dy2_bf, dy2, gw["w_ff1"][0], gw["w_ff2"][0])
    dy1, dy1_bf, gs["ln1_g0"], gs["ln1_b0"] = _ln_bwd("l0_ln1_bwd", y1, dh1, ln1_g, 0)
    g0["woa"] = _tiled("l0_dwoa", (2, wide // TM), [_tl(o_att, TM), _cw(dy1_bf, TN)], [_out(wide, D_MODEL, F32, TM, TN)], _mmc(TN_))
    g0["wob"] = _tiled("l0_dwob", (2, 1), [_tl(b_out, SGU_DIM), _cw(dy1_bf, TN)], [_out(SGU_DIM, D_MODEL, F32, SGU_DIM, TN)], _mmc(TN_))
    wo_cat = jnp.concatenate([w["woa"], w["wob"]], axis=0)
    dcat = _tiled("l0_dcat", (3, nt), [_rb(dy1_bf, tm), _rw(wo_cat, TN)], [_out(T, wide + SGU_DIM, BF16, tm, TN)], _mmc(NT))
    dzs, gs["sgu_w"], gs["sgu_ln_g"], gs["sgu_ln_b"], gs["sgu_b"] = _sgu_bwd(zs, dcat, sp["sgu_ln_g"], sp["sgu_ln_b"], sgu_w, bias_full)
    dq, dk, dv = _attn_bwd(q, k, v, o_att, lse, dcat)
    dq_pre = _unrope_heads(dq, tabs)
    lora_w = lambda name, a, d: _tiled(name, (wide // TN, 1), [_tl(a, MLA_LORA), _cw(d, TN)], [_out(MLA_LORA, wide, F32, MLA_LORA, TN)], _mmc(TN_))
    g0["wq"] = lora_w("l0_dwq", cqn, dq_pre)
    g0["wk"] = lora_w("l0_dwk", ckvn, dk)
    g0["wv"] = lora_w("l0_dwv", ckvn, dv)
    dcqn = _tiled("l0_dcqn", (1, nt), [_rb(dq_pre, tm), _rw(w["wq"], MLA_LORA)], [_out(T, MLA_LORA, F32, tm, MLA_LORA)], _mmc(NT))
    dckvn = _tiled("l0_dckvn", (1, nt), [_rb(dk, tm), _rw(w["wk"], MLA_LORA), _rb(dv, tm), _rw(w["wv"], MLA_LORA)],
                   [_out(T, MLA_LORA, F32, tm, MLA_LORA)], _mmc(NT, n_pairs=2))
    dzm, gs["mla_gq"], gs["mla_gkv"] = _mla_pre_bwd(zm, tabs, gq, gkv, dcqn, dckvn, dk)
    g0["wm"] = _tiled("l0_dwm", (1, D_MODEL // TM), [_tl(x, TM), _cw(dzm, 640)], [_out(D_MODEL, 640, F32, TM, 640)], _mmc(TN_))
    g0["ws"] = _tiled("l0_dws", (2, D_MODEL // TM), [_tl(x, TM), _cw(dzs, TN)], [_out(D_MODEL, 1024, F32, TM, TN)], _mmc(TN_))
    dx = _tiled("l0_dx", (2, nt), [_rb(dzm, tm), _rw(w["wm"], TN), _rb(dzs, tm), _rw(w["ws"], TN), _rbj(dy1, tm, TN)],
                [_out(T, D_MODEL, F32, tm, TN)], _mmc(NT, n_pairs=2, epilogue=resid))

    grads = _unprep_grads(g0)
    grads.update(w_in_o=d_in_o, w_out_o=d_out_o, w_ff1=[dw1_0, dw1_1], w_ff2=[dw2_0, dw2_1])
    return sq_err, dx, grads, gs


def _me():
    return lax.axis_index("x"), lax.axis_index("y"), lax.axis_index("c")


def _hbm_call(name, kern, operands, out_shape, n_sems, extra_scratch=()):
    any_spec = pl.BlockSpec(memory_space=pl.ANY)
    return pl.pallas_call(
        kern, name=name, out_shape=out_shape, in_specs=[any_spec] * len(operands), out_specs=[any_spec] * len(out_shape),
        scratch_shapes=[pltpu.SemaphoreType.DMA((n_sems,)), pltpu.SemaphoreType.DMA((n_sems,)), *extra_scratch],
    )(*[_hbm(a) for a in operands])


def _all_gather(shards):
    n = len(shards)

    def kern(*refs):
        x_refs, out_refs, (send_sems, recv_sems, local_sems) = refs[:n], refs[n:2 * n], refs[2 * n:]
        x, y, c = _me()
        me, sibling = (x, y, c), (x, y, 1 - c)
        chips = [(1 - x, y), (x, 1 - y), (1 - x, 1 - y)]

        def copy(op, k, block, to, own=False):
            slot = out_refs[op].at[4 * block[0] + 2 * block[1] + block[2]]
            return pltpu.make_async_remote_copy(
                src_ref=x_refs[op] if own else slot, dst_ref=slot, send_sem=send_sems.at[7 * op + k],
                recv_sem=recv_sems.at[7 * op + k], device_id=to, device_id_type=MESH)

        mine = [pltpu.make_async_copy(x_refs[op], out_refs[op].at[4 * x + 2 * y + c], local_sems.at[op]) for op in range(n)]
        for cp in mine:
            cp.start()
        first = []
        for op in range(n):
            first.append(copy(op, 0, me, sibling, own=True))
            first += [copy(op, 1 + j, me, (*chip, c), own=True) for j, chip in enumerate(chips)]
        for cp in first:
            cp.start()
        passed = []
        for j, chip in enumerate(chips):
            for op in range(n):
                copy(op, 1 + j, (*chip, c), me).wait_recv()
                passed.append(copy(op, 4 + j, (*chip, c), sibling))
                passed[-1].start()
        for op in range(n):
            copy(op, 0, sibling, me).wait_recv()
            for j, chip in enumerate(chips):
                copy(op, 4 + j, (*chip, 1 - c), me).wait_recv()
        for cp in first + passed:
            cp.wait_send()
        for cp in mine:
            cp.wait()

    out_shape = [pltpu.HBM((N_DEV, *s.shape), s.dtype) for s in shards]
    return _hbm_call("weights_all_gather", kern, shards, out_shape, 7 * n, [pltpu.SemaphoreType.DMA((n,))])


def _rs_sibling(grads):
    n = len(grads)

    def kern(*refs):
        g_refs, out_refs, (send_sems, recv_sems) = refs[:n], refs[n:2 * n], refs[2 * n:]
        x, y, c = _me()
        copies = [pltpu.make_async_remote_copy(
            src_ref=g_refs[op].at[k, 1 - c], dst_ref=out_refs[op].at[k], send_sem=send_sems.at[4 * op + k],
            recv_sem=recv_sems.at[4 * op + k], device_id=(x, y, 1 - c), device_id_type=MESH) for op in range(n) for k in range(4)]
        for cp in copies:
            cp.start()
        for cp in copies:
            cp.wait()

    out_shape = [pltpu.HBM((4, *g.shape[2:]), g.dtype) for g in grads]
    return _hbm_call("grads_to_sibling", kern, grads, out_shape, 4 * n)


def _rs_chips(sums):
    n = len(sums)

    def kern(*refs):
        p_refs, out_refs, (send_sems, recv_sems) = refs[:n], refs[n:2 * n], refs[2 * n:]
        x, y, c = _me()
        chips = [(1 - x, y), (x, 1 - y), (1 - x, 1 - y)]
        copies = [pltpu.make_async_remote_copy(
            src_ref=p_refs[op].at[2 * cx + cy], dst_ref=out_refs[op].at[j], send_sem=send_sems.at[3 * op + j],
            recv_sem=recv_sems.at[3 * op + j], device_id=(cx, cy, c), device_id_type=MESH)
            for op in range(n) for j, (cx, cy) in enumerate(chips)]
        for cp in copies:
            cp.start()
        for cp in copies:
            cp.wait()

    out_shape = [pltpu.HBM((3, *p.shape[1:]), p.dtype) for p in sums]
    return _hbm_call("grads_between_chips", kern, sums, out_shape, 3 * n)


def _row_tile(r):
    return r if r <= 256 else 256


def _chip_sum(name, g, from_sibling, core):
    _, _, R, W = g.shape
    tr = _row_tile(R)

    def kern(core_ref, g_ref, s_ref, o_ref):
        o_ref[...] = (g_ref[...] + s_ref[...]).astype(BF16)

    return pl.pallas_call(
        kern, name=name, out_shape=pltpu.HBM((4, R, W), BF16),
        grid_spec=pltpu.PrefetchScalarGridSpec(
            num_scalar_prefetch=1, grid=(4, R // tr),
            in_specs=[pl.BlockSpec((None, None, tr, W), lambda k, i, core: (k, core[0], i, 0)),
                      pl.BlockSpec((None, tr, W), lambda k, i, core: (k, i, 0))],
            out_specs=pl.BlockSpec((None, tr, W), lambda k, i, core: (k, i, 0))),
        compiler_params=pltpu.CompilerParams(dimension_semantics=("parallel", "parallel"), vmem_limit_bytes=_vmem(3 * tr * W * 4)),
    )(core, _hbm(g), _hbm(from_sibling))


def _adamw(w, g, m, v):
    m = ADAM_B1 * m + (1.0 - ADAM_B1) * g
    v = ADAM_B2 * v + (1.0 - ADAM_B2) * (g * g)
    m_hat = m / (1.0 - ADAM_B1 ** ADAM_STEP)
    v_hat = v / (1.0 - ADAM_B2 ** ADAM_STEP)
    return -ADAM_LR * (m_hat / (jnp.sqrt(v_hat) + ADAM_EPS) + ADAM_WD * w), m, v


def _finish_sharded(name, layers, w, m, v, where):
    nl, R, W = w.shape
    tr = _row_tile(R)

    def kern(where_ref, *refs):
        w_ref, m_ref, v_ref, go_ref, d_ref, mo_ref, vo_ref = refs[3 * nl:]
        for l in range(nl):
            g_ref, s_ref, c_ref = refs[3 * l:3 * l + 3]
            grad = g_ref[...] + s_ref[...]
            for j in range(3):
                grad = grad + c_ref[j].astype(F32)
            go_ref[l] = grad
            d_ref[l], mo_ref[l], vo_ref[l] = _adamw(w_ref[l], grad, m_ref[l], v_ref[l])

    row = pl.BlockSpec((nl, tr, W), lambda i, wh: (0, i, 0))
    in_specs, args = [], []
    for g, s, c in layers:
        in_specs += [pl.BlockSpec((None, None, tr, W), lambda i, wh: (wh[0], wh[1], i, 0)),
                     pl.BlockSpec((None, tr, W), lambda i, wh: (wh[0], i, 0)),
                     pl.BlockSpec((3, tr, W), lambda i, wh: (0, i, 0))]
        args += [g, s, c]
    return pl.pallas_call(
        kern, name=name, out_shape=[pltpu.HBM((nl, R, W), F32)] * 4,
        grid_spec=pltpu.PrefetchScalarGridSpec(num_scalar_prefetch=1, grid=(R // tr,), in_specs=in_specs + [row, row, row],
                                               out_specs=[row, row, row, row]),
        compiler_params=pltpu.CompilerParams(dimension_semantics=("parallel",), vmem_limit_bytes=_vmem(nl * 11 * tr * W * 4)),
    )(where, *[_hbm(a) for a in (*args, w, m, v)])


SMALL_PLACE = (("mla_gq", 0, 0, 1, 256), ("mla_gkv", 0, 256, 1, 256), ("sgu_ln_g", 0, 512, 1, 512), ("sgu_ln_b", 1, 0, 1, 512),
               ("hg_lb", 2, 0, 2, 1024), ("ln1_g", 4, 0, 2, 1024), ("ln1_b", 6, 0, 2, 1024), ("sgu_b", 8, 0, 4, 128),
               ("ln2_g", 12, 0, 2, 1024), ("ln2_b", 14, 0, 2, 1024), ("hg_gnorm", 16, 0, 1, 1024))
SMALL_BUF_ROWS = 24


def _small_reduce_adamw(gs, given):
    pieces = [(gs["mla_gq"], 0, 0), (gs["mla_gkv"], 0, 256), (gs["sgu_ln_g"], 0, 512), (gs["sgu_ln_b"], 1, 0), (gs["hg_lb"], 2, 0),
              (gs["ln1_g0"], 4, 0), (gs["ln1_g1"], 5, 0), (gs["ln1_b0"], 6, 0), (gs["ln1_b1"], 7, 0), (gs["sgu_b"], 8, 0),
              (gs["ln2_g0"], 12, 0), (gs["ln2_g1"], 13, 0), (gs["ln2_b0"], 14, 0), (gs["ln2_b1"], 15, 0), (gs["hg_gnorm"], 16, 0)]
    names = [p[0] for p in SMALL_PLACE] + ["sgu_w"]
    n_p, n_names = len(pieces), len(names)
    wmv = [given[pre + name] for name in names for pre in ("", "m_", "v_")]

    def kern(*refs):
        piece_refs, gw_ref = refs[:n_p], refs[n_p]
        wmv_refs = refs[n_p + 1:n_p + 1 + 3 * n_names]
        out_refs = refs[n_p + 1 + 3 * n_names:n_p + 1 + 7 * n_names]
        buf_a, buf_b, send_sems, recv_sems = refs[n_p + 1 + 7 * n_names:]
        px, py, pc = _me()
        me = 4 * px + 2 * py + pc
        mine_a, mine_b = buf_a.at[me], buf_b.at[me]
        mine_a[...] = jnp.zeros(mine_a.shape, F32)
        for ref, (_, r, l0) in zip(piece_refs, pieces):
            mine_a[r:r + ref.shape[0], l0:l0 + ref.shape[1]] = ref[...]
        mine_b[...] = gw_ref[...]
        copies = []
        for r in range(1, N_DEV):
            peer = (px ^ (r >> 2), py ^ ((r >> 1) & 1), pc ^ (r & 1))
            for k, mine in enumerate((mine_a, mine_b)):
                copies.append(pltpu.make_async_remote_copy(
                    src_ref=mine, dst_ref=mine, send_sem=send_sems.at[2 * (r - 1) + k], recv_sem=recv_sems.at[2 * (r - 1) + k],
                    device_id=peer, device_id_type=MESH))
        for cp in copies:
            cp.start()
        for r in range(1, N_DEV):
            for k, buf in enumerate((buf_a, buf_b)):
                theirs = buf.at[me ^ r]
                pltpu.make_async_remote_copy(
                    src_ref=theirs, dst_ref=theirs, send_sem=send_sems.at[2 * (r - 1) + k], recv_sem=recv_sems.at[2 * (r - 1) + k],
                    device_id=(px, py, pc), device_id_type=MESH).wait_recv()
        for cp in copies:
            cp.wait_send()
        sum_a, sum_b = buf_a[0], buf_b[0]
        for d in range(1, N_DEV):
            sum_a, sum_b = sum_a + buf_a[d], sum_b + buf_b[d]

        def own_block(full):
            acc = full[:, 0:128]
            for b in range(1, N_DEV):
                acc = jnp.where(me == b, full[:, b * 128:(b + 1) * 128], acc)
            return acc

        for idx, name in enumerate(names):
            w_ref, m_ref, v_ref = wmv_refs[3 * idx:3 * idx + 3]
            if name == "sgu_w":
                grad = sum_b[None]
            else:
                _, r, l0, nr, nl = SMALL_PLACE[idx]
                grad = sum_a[r:r + nr, l0:l0 + nl]
                if name == "hg_gnorm":
                    grad = own_block(grad)
                if name == "sgu_b":
                    grad = grad[None]
            res = (grad, *_adamw(w_ref[...], grad, m_ref[...], v_ref[...]))
            for o_ref, val in zip(out_refs[4 * idx:4 * idx + 4], res):
                o_ref[...] = val

    vmem = pl.BlockSpec(memory_space=pltpu.VMEM)
    operands = [p[0] for p in pieces] + [gs["sgu_w"]] + wmv
    out_shape = [jax.ShapeDtypeStruct(given[name].shape, F32) for name in names for _ in range(4)]
    res = pl.pallas_call(
        kern, name="small_all_reduce_adamw", out_shape=out_shape, in_specs=[vmem] * len(operands), out_specs=[vmem] * len(out_shape),
        scratch_shapes=[pltpu.VMEM((N_DEV, SMALL_BUF_ROWS, D_MODEL), F32), pltpu.VMEM((N_DEV, SGU_G, 128, 128), F32),
                        pltpu.SemaphoreType.DMA((14,)), pltpu.SemaphoreType.DMA((14,))],
    )(*operands)
    return {name: res[4 * idx:4 * idx + 4] for idx, name in enumerate(names)}


SHARDED = ("w_in_e", "w_qb", "w_kvb", "w_out_e", "w_in_o", "w_out_o", "w_ff1", "w_ff2")


def kernel(x, positions, w_in_e, mla_gq, mla_gkv, w_qb, w_kvb, sgu_ln_g, sgu_ln_b, sgu_w, sgu_b, w_out_e, w_in_o, hg_lb, hg_gnorm, w_out_o, ln1_g, ln1_b, w_ff1, w_ff2, ln2_g, ln2_b, loss_target, m_w_in_e, m_mla_gq, m_mla_gkv, m_w_qb, m_w_kvb, m_sgu_ln_g, m_sgu_ln_b, m_sgu_w, m_sgu_b, m_w_out_e, m_w_in_o, m_hg_lb, m_hg_gnorm, m_w_out_o, m_ln1_g, m_ln1_b, m_w_ff1, m_w_ff2, m_ln2_g, m_ln2_b, v_w_in_e, v_mla_gq, v_mla_gkv, v_w_qb, v_w_kvb, v_sgu_ln_g, v_sgu_ln_b, v_sgu_w, v_sgu_b, v_w_out_e, v_w_in_o, v_hg_lb, v_hg_gnorm, v_w_out_o, v_ln1_g, v_ln1_b, v_w_ff1, v_w_ff2, v_ln2_g, v_ln2_b):
    given = dict(locals())
    px, py, pc = _me()

    names = ["w_in_e", "w_qb", "w_kvb", "w_out_e", "w_in_o", "w_out_o"]
    shards = [given[n][0].astype(BF16) for n in names]
    shards += [w_ff1[0].astype(BF16), w_ff1[1].astype(BF16), w_ff2[0].astype(BF16), w_ff2[1].astype(BF16), hg_gnorm]
    got = _all_gather(shards)
    gw = dict(zip(names, got[:6]))
    gw["w_ff1"], gw["w_ff2"] = [got[6], got[7]], [got[8], got[9]]
    small_names = ["mla_gq", "mla_gkv", "sgu_ln_g", "sgu_ln_b", "sgu_w", "sgu_b", "hg_lb", "ln1_g", "ln1_b", "ln2_g", "ln2_b"]
    sp = {n: given[n] for n in small_names}
    sp["hg_gnorm"] = got[10].reshape(1, D_MODEL)

    sq_err, dx, grads, gs = _local_step(x[0], positions[0], loss_target[0], gw, sp)
    loss = lax.psum(0.5 * jnp.sum(sq_err) / D_MODEL, ("x", "y", "c"))

    flat = [grads[n] for n in names] + grads["w_ff1"] + grads["w_ff2"]
    blocks = [g.reshape(4, 2, *g.shape[1:]) for g in flat]
    from_sibling = _rs_sibling(blocks)
    core = pc.reshape(1).astype(jnp.int32)
    chip_sums = [_chip_sum(f"grads_chip_sum_{k}", b, s, core) for k, (b, s) in enumerate(zip(blocks, from_sibling))]
    from_chips = _rs_chips(chip_sums)
    where = jnp.stack([2 * px + py, pc]).astype(jnp.int32)
    layers = list(zip(blocks, from_sibling, from_chips))
    per_weight = dict(zip(names, [[l] for l in layers[:6]]))
    per_weight["w_ff1"], per_weight["w_ff2"] = layers[6:8], layers[8:10]
    results = {n: _finish_sharded(f"finish_{n}", per_weight[n], given[n], given["m_" + n], given["v_" + n], where) for n in SHARDED}

    results.update(_small_reduce_adamw(gs, given))

    order = ["w_in_e", "mla_gq", "mla_gkv", "w_qb", "w_kvb", "sgu_ln_g", "sgu_ln_b", "sgu_w", "sgu_b", "w_out_e", "w_in_o",
             "hg_lb", "hg_gnorm", "w_out_o", "ln1_g", "ln1_b", "w_ff1", "w_ff2", "ln2_g", "ln2_b"]
    return (loss, dx[None], *[results[name][kind] for kind in range(4) for name in order])
```

```python
import functools
import math

import jax
import jax.numpy as jnp
import numpy as np
from jax import lax
from jax.experimental import pallas as pl
from jax.experimental.pallas import tpu as pltpu

F32 = jnp.float32
BF16 = jnp.bfloat16
MESH = pl.DeviceIdType.MESH
HIGHEST = lax.Precision.HIGHEST

D_MODEL = 1024
D_FF = 4096
N_DEV = 8
HEADS = 8
HEAD_W = 128
MLA_NOPE = 64
MLA_ROPE = 32
MLA_V = 64
MLA_LORA = 256
MLA_SCALE = (MLA_NOPE + MLA_ROPE) ** -0.5
ROPE_BASE = 10000.0
SGU_DIM = 512
SGU_G = 4
SGU_CHUNK = 128
HG_CHUNK = 64
ALPHA = (2 * 2) ** 0.25
EPS = 1e-5
ADAM_LR, ADAM_B1, ADAM_B2, ADAM_EPS, ADAM_WD, ADAM_STEP = 0.001, 0.9, 0.999, 1e-08, 0.01, 10

VMEM_CAP_V7X = 56 * 2**20
VMEM_SLACK = 12 * 2**20
TM = 512
TN = 512


def _vmem(block_bytes):
    return int(min(VMEM_CAP_V7X, 2 * block_bytes + VMEM_SLACK))


def _hbm(a):
    return pltpu.with_memory_space_constraint(a, pltpu.HBM)


def _nbytes(shape, dtype):
    return int(np.prod([d for d in shape if d is not None])) * jnp.dtype(dtype).itemsize


def _sig(x):
    return 1.0 / (1.0 + jnp.exp(-x))


def _gelu(x):
    c = math.sqrt(2.0 / math.pi)
    t = jnp.tanh(c * (x + 0.044715 * x * x * x))
    return 0.5 * x * (1.0 + t), t


def _gelu_grad(x, t):
    c = math.sqrt(2.0 / math.pi)
    return 0.5 * (1.0 + t) + 0.5 * x * (1.0 - t * t) * c * (1.0 + 3 * 0.044715 * x * x)


def _dot(a, b, dims, precision=None):
    return lax.dot_general(a, b, (dims, ((), ())), preferred_element_type=F32, precision=precision)


NN = ((1,), (0,))
NT = ((1,), (1,))
TN_ = ((0,), (0,))


def _tiled(name, grid, ins, outs, compute, direct=False):
    n_in = len(ins)

    def kern(*refs):
        if direct:
            compute(refs[:n_in], refs[n_in:])
            return
        for o_ref, r in zip(refs[n_in:], compute(*refs[:n_in])):
            o_ref[...] = r.astype(o_ref.dtype).reshape(o_ref.shape)

    swap = lambda f: (lambda j, i: f(i, j))
    nbytes = sum(_nbytes(blk, a.dtype) for a, blk, _ in ins) + sum(_nbytes(blk, dt) + _nbytes(blk, F32) for _, dt, blk, _ in outs)
    res = pl.pallas_call(
        kern, name=name, grid=grid,
        in_specs=[pl.BlockSpec(blk, swap(f)) for _, blk, f in ins],
        out_specs=[pl.BlockSpec(blk, swap(f)) for _, _, blk, f in outs],
        out_shape=[pltpu.HBM(shape, dt) for shape, dt, _, _ in outs],
        compiler_params=pltpu.CompilerParams(dimension_semantics=("parallel", "parallel"), vmem_limit_bytes=_vmem(nbytes)),
    )(*[_hbm(a) for a, _, _ in ins])
    return res if len(res) > 1 else res[0]


def _rb(a, tm, w=None, cb=0):
    return (a, (tm, a.shape[1] if w is None else w), lambda i, j: (i, cb))


def _rbj(a, tm, tn):
    return (a, (tm, tn), lambda i, j: (i, j))


def _cw(b, tn):
    return (b, (b.shape[0], tn), lambda i, j: (0, j))


def _rw(b, tn):
    return (b, (tn, b.shape[1]), lambda i, j: (j, 0))


def _tl(a, tm):
    return (a, (a.shape[0], tm), lambda i, j: (0, i))


def _gcw(g):
    return (g, (None, g.shape[1], g.shape[2]), lambda i, j: (j, 0, 0))


def _grw(g, tn):
    return (g, (N_DEV, tn, g.shape[2]), lambda i, j: (0, j, 0))


def _out(m, n, dtype, tm, tn):
    return ((m, n), dtype, (tm, tn), lambda i, j: (i, j))


def _out_dev(k, n, tm):
    return ((N_DEV, k, n), F32, (None, tm, n), lambda i, j: (j, i, 0))


def _mmc(dims, n_pairs=1, epilogue=None):
    def compute(*refs):
        acc = None
        for k in range(n_pairs):
            d = _dot(refs[2 * k][...].astype(BF16), refs[2 * k + 1][...].astype(BF16), dims)
            acc = d if acc is None else acc + d
        ext = [r[...] for r in refs[2 * n_pairs:]]
        return epilogue(acc, *ext) if epilogue is not None else (acc,)

    return compute


def _res(w):
    return (w, w.shape, functools.partial(lambda i, j, nd: (0,) * nd, nd=w.ndim))


def _mmc_blocks(nblk, dims, rhs_block, epilogue=None):
    def compute(in_refs, out_refs):
        a = in_refs[0][...].astype(BF16)
        for d in range(nblk):
            acc = _dot(a, rhs_block(in_refs[1], d).astype(BF16), dims)
            n = acc.shape[1]
            ext = [r[:, d * n:(d + 1) * n] for r in in_refs[2:]]
            res = epilogue(acc, *ext) if epilogue is not None else (acc,)
            for o_ref, r in zip(out_refs, res):
                o_ref[:, d * n:(d + 1) * n] = r.astype(o_ref.dtype)

    return compute


def _mmc_dev(epilogue=None):
    def compute(a_ref, b_ref, *ext_refs):
        n = b_ref.shape[2]
        acc = None
        for d in range(N_DEV):
            t = _dot(a_ref[:, d * n:(d + 1) * n].astype(BF16), b_ref[d].astype(BF16), NT)
            acc = t if acc is None else acc + t
        ext = [r[...] for r in ext_refs]
        return epilogue(acc, *ext) if epilogue is not None else (acc,)

    return compute


def _rowwise(name, body, rows, consts, out_rows, out_accs=(), tr=512):
    T = rows[0][0].shape[0]
    tr = min(tr, T)
    nr, ncn, no = len(rows), len(consts), len(out_rows)

    def kern(*refs):
        accs = refs[nr + ncn + no:]
        if accs:
            @pl.when(pl.program_id(0) == 0)
            def _():
                for a in accs:
                    a[...] = jnp.zeros(a.shape, a.dtype)
        body(refs[:nr], refs[nr:nr + ncn], refs[nr + ncn:nr + ncn + no], accs)

    in_specs = [pl.BlockSpec((tr, w), functools.partial(lambda i, cb: (i, cb), cb=cb)) for _, w, cb in rows]
    in_specs += [pl.BlockSpec(c.shape, functools.partial(lambda i, nd: (0,) * nd, nd=c.ndim)) for c in consts]
    out_specs = [pl.BlockSpec((tr, w), lambda i: (i, 0)) for w, _ in out_rows]
    out_specs += [pl.BlockSpec(s, functools.partial(lambda i, nd: (0,) * nd, nd=len(s))) for s, _ in out_accs]
    out_shape = [pltpu.HBM((T, w), dt) for w, dt in out_rows]
    out_shape += [pltpu.HBM(s, dt) for s, dt in out_accs]
    nbytes = sum(_nbytes((tr, w), a.dtype) for a, w, _ in rows) + sum(_nbytes(c.shape, c.dtype) for c in consts)
    nbytes += sum(_nbytes((tr, w), dt) for w, dt in out_rows) + sum(_nbytes(s, dt) for s, dt in out_accs)
    res = pl.pallas_call(
        kern, name=name, grid=(T // tr,), in_specs=in_specs, out_specs=out_specs, out_shape=out_shape,
        compiler_params=pltpu.CompilerParams(dimension_semantics=("arbitrary",), vmem_limit_bytes=_vmem(nbytes)),
    )(*[_hbm(a) for a, _, _ in rows], *[_hbm(c) for c in consts])
    return res if len(res) > 1 else res[0]


def _full(a):
    return (a, a.shape[1], 0)


def _ln_stats(y):
    mu = jnp.mean(y, axis=-1, keepdims=True)
    yc = y - mu
    r = lax.rsqrt(jnp.mean(yc * yc, axis=-1, keepdims=True) + EPS)
    return yc * r, r


def _ln_fwd(name, h_in, mix, g, b, layer):
    def body(rows, consts, outs, accs):
        y = ALPHA * rows[0][...] + rows[1][...]
        xh, _ = _ln_stats(y)
        h = xh * consts[0][layer:layer + 1, :] + consts[1][layer:layer + 1, :]
        outs[0][...] = y
        outs[1][...] = h
        outs[2][...] = h.astype(BF16)

    return _rowwise(name, body, [_full(h_in), _full(mix)], [g, b], [(D_MODEL, F32), (D_MODEL, F32), (D_MODEL, BF16)], tr=256)


def _ln_loss(name, h_in, mix, g, b, layer, target):
    def body(rows, consts, outs, accs):
        y = ALPHA * rows[0][...] + rows[1][...]
        xh, _ = _ln_stats(y)
        err = xh * consts[0][layer:layer + 1, :] + consts[1][layer:layer + 1, :] - rows[2][...]
        outs[0][...] = y
        outs[1][...] = err * (1.0 / D_MODEL)
        accs[0][...] += jnp.sum(err * err, axis=0, keepdims=True)

    return _rowwise(name, body, [_full(h_in), _full(mix), _full(target)], [g, b], [(D_MODEL, F32), (D_MODEL, F32)],
                    [((1, D_MODEL), F32)], tr=256)


def _ln_bwd(name, y, dh, g, layer):
    def body(rows, consts, outs, accs):
        xh, r = _ln_stats(rows[0][...])
        d = rows[1][...]
        accs[0][...] += jnp.sum(d * xh, axis=0, keepdims=True)
        accs[1][...] += jnp.sum(d, axis=0, keepdims=True)
        dx = d * consts[0][layer:layer + 1, :]
        dy = r * (dx - jnp.mean(dx, axis=-1, keepdims=True) - xh * jnp.mean(dx * xh, axis=-1, keepdims=True))
        outs[0][...] = dy
        outs[1][...] = dy.astype(BF16)

    return _rowwise(name, body, [_full(y), _full(dh)], [g], [(D_MODEL, F32), (D_MODEL, BF16)],
                    [((1, D_MODEL), F32), ((1, D_MODEL), F32)], tr=256)


def _relu2_epilogue(acc):
    a = jnp.maximum(acc, 0.0)
    return acc, a * a


def _mlp_fwd(tag, h_bf, w1, w2):
    T = h_bf.shape[0]
    tm = min(TM, T)
    a, act = _tiled(f"{tag}_ff1", (1, T // tm), [_rb(h_bf, tm), _res(w1)],
                    [_out(T, D_FF, BF16, tm, D_FF), _out(T, D_FF, BF16, tm, D_FF)],
                    _mmc_blocks(N_DEV, NN, lambda w, d: w[d], epilogue=_relu2_epilogue), direct=True)
    ff = _tiled(f"{tag}_ff2", (1, T // tm), [_rb(act, tm), _res(w2.reshape(D_FF, D_MODEL))],
                [_out(T, D_MODEL, F32, tm, D_MODEL)], _mmc(NN))
    return a, act, ff


def _mlp_bwd(tag, h_bf, a, act, dff_bf, dy, w1, w2):
    T = h_bf.shape[0]
    tm = min(TM, T)
    da = _tiled(f"{tag}_dact", (1, T // tm), [_rb(dff_bf, tm), _res(w2), _rb(a, tm)], [_out(T, D_FF, BF16, tm, D_FF)],
                _mmc_blocks(N_DEV, NT, lambda w, d: w[d], epilogue=lambda acc, a_t: (acc * 2.0 * jnp.maximum(a_t.astype(F32), 0.0),)),
                direct=True)
    dw2 = _tiled(f"{tag}_dw2", (1, D_FF // TM), [_tl(act, TM), _res(dff_bf)],
                 [_out(D_FF, D_MODEL, F32, TM, D_MODEL)], _mmc(TN_)).reshape(N_DEV, D_FF // N_DEV, D_MODEL)
    dw1 = _tiled(f"{tag}_dw1", (N_DEV, 1), [_res(h_bf), _cw(da, TN)], [_out_dev(D_MODEL, TN, D_MODEL)], _mmc(TN_))
    dh, dh_bf = _tiled(f"{tag}_dh", (1, T // tm), [_rb(da, tm), _res(w1), _rb(dy, tm)],
                       [_out(T, D_MODEL, F32, tm, D_MODEL), _out(T, D_MODEL, BF16, tm, D_MODEL)],
                       _mmc_dev(epilogue=lambda acc, dy_t: (acc + ALPHA * dy_t,) * 2))
    return dh, dh_bf, dw1, dw2


def _rope_tables(positions_col, invf_lane):
    def body(rows, consts, outs, accs):
        ang = rows[0][...].astype(F32) * consts[0][...]
        c, s = jnp.cos(ang), jnp.sin(ang)
        lane = lax.broadcasted_iota(jnp.int32, ang.shape, 1)
        outs[0][...] = jnp.where(lane < 64, 1.0, jnp.where(lane < 96, c, 0.0))
        outs[1][...] = jnp.where((lane >= 64) & (lane < 80), -s, 0.0)
        outs[2][...] = jnp.where((lane >= 80) & (lane < 96), s, 0.0)

    return _rowwise("rope_tables", body, [_full(positions_col)], [invf_lane], [(HEAD_W, F32)] * 3)


def _rope(x, c, s1, s2):
    return x * c + pltpu.roll(x, 112, 1) * s1 + pltpu.roll(x, 16, 1) * s2


def _rope_t(dx, c, s1, s2):
    return dx * c + pltpu.roll(dx * s1, 16, 1) + pltpu.roll(dx * s2, 112, 1)


def _rms(c):
    r = lax.rsqrt(jnp.mean(c * c, axis=-1, keepdims=True) + EPS)
    return c * r, r


def _mla_pre(zm, tabs, gq, gkv):
    def body(rows, consts, outs, accs):
        cq, _ = _rms(rows[0][...])
        ckv, _ = _rms(rows[1][...])
        outs[0][...] = (cq * consts[0][...]).astype(BF16)
        outs[1][...] = (ckv * consts[1][...]).astype(BF16)
        outs[2][...] = _rope(rows[2][...], rows[3][...], rows[4][...], rows[5][...])

    rows = [(zm, 256, 0), (zm, 256, 1), (zm, 128, 4)] + [_full(t) for t in tabs]
    return _rowwise("mla_pre", body, rows, [gq, gkv], [(256, BF16), (256, BF16), (HEAD_W, F32)])


def _mla_pre_bwd(zm, tabs, gq, gkv, dcqn, dckvn, dk):
    def body(rows, consts, outs, accs):
        res = []
        for k in range(2):
            ch, r = _rms(rows[k][...])
            d = rows[5 + k][...]
            accs[k][...] += jnp.sum(d * ch, axis=0, keepdims=True)
            dc = d * consts[k][...]
            res.append(r * (dc - ch * jnp.mean(dc * ch, axis=-1, keepdims=True)))
        dks = rows[7][:, 0:HEAD_W]
        for h in range(1, HEADS):
            dks = dks + rows[7][:, h * HEAD_W:(h + 1) * HEAD_W]
        lane = lax.broadcasted_iota(jnp.int32, dks.shape, 1)
        dks = jnp.where((lane >= 64) & (lane < 96), dks, 0.0)
        dkr = _rope_t(dks, rows[2][...], rows[3][...], rows[4][...])
        outs[0][:, 0:256] = res[0].astype(BF16)
        outs[0][:, 256:512] = res[1].astype(BF16)
        outs[0][:, 512:640] = dkr.astype(BF16)

    rows = [(zm, 256, 0), (zm, 256, 1)] + [_full(t) for t in tabs] + [_full(dcqn), _full(dckvn), _full(dk)]
    return _rowwise("mla_pre_bwd", body, rows, [gq, gkv], [(640, BF16)], [((1, 256), F32), ((1, 256), F32)])


def _rope_heads(x, c, s1, s2, fn):
    return jnp.concatenate([fn(x[:, h * HEAD_W:(h + 1) * HEAD_W], c, s1, s2) for h in range(HEADS)], axis=1)


def _unrope_heads(dq, tabs):
    def body(rows, consts, outs, accs):
        outs[0][...] = _rope_heads(rows[0][...], rows[1][...], rows[2][...], rows[3][...], _rope_t).astype(BF16)

    return _rowwise("l0_dq_rope", body, [_full(dq)] + [_full(t) for t in tabs], [], [(HEADS * HEAD_W, BF16)])


def _attn_block(T):
    return min(1024, T)


def _attn_fwd(q, k, v):
    T = q.shape[0]
    BQ = _attn_block(T)
    nq = T // BQ

    def kern(q_ref, k_ref, v_ref, o_ref, lse_ref):
        def step(i, j, carry, masked):
            m, l, acc = carry
            qb = q_ref[pl.ds(pl.multiple_of(i * BQ, BQ), BQ), :]
            kb = k_ref[pl.ds(pl.multiple_of(j * BQ, BQ), BQ), :]
            vb = v_ref[pl.ds(pl.multiple_of(j * BQ, BQ), BQ), :]
            s = _dot(qb, kb, NT) * MLA_SCALE
            if masked:
                row = lax.broadcasted_iota(jnp.int32, s.shape, 0)
                col = lax.broadcasted_iota(jnp.int32, s.shape, 1)
                s = jnp.where(col <= row, s, -1e30)
            m_new = jnp.maximum(m, jnp.max(s, axis=-1, keepdims=True))
            p = jnp.exp(s - m_new)
            a = jnp.exp(m - m_new)
            l = a * l + jnp.sum(p, axis=-1, keepdims=True)
            acc = a * acc + _dot(p.astype(BF16), vb, NN)
            return m_new, l, acc

        def qloop(i, _):
            init = (jnp.full((BQ, 1), -1e30, F32), jnp.zeros((BQ, 1), F32), jnp.zeros((BQ, HEAD_W), F32))
            carry = lax.fori_loop(0, i, lambda j, c: step(i, j, c, False), init)
            m, l, acc = step(i, i, carry, True)
            rows = pl.ds(pl.multiple_of(i * BQ, BQ), BQ)
            o_ref[rows, :] = acc / l
            lse_ref[0, rows, :] = m + jnp.log(l)
            return 0

        lax.fori_loop(0, nq, qloop, 0)

    head = pl.BlockSpec((T, HEAD_W), lambda h: (0, h))
    nbytes = 3 * _nbytes((T, HEAD_W), BF16) + _nbytes((T, HEAD_W), F32) + _nbytes((T, 128), F32)
    return pl.pallas_call(
        kern, name="attn_fwd", grid=(HEADS,), in_specs=[head, head, head],
        out_specs=[head, pl.BlockSpec((1, T, 1), lambda h: (h, 0, 0))],
        out_shape=[pltpu.HBM((T, HEADS * HEAD_W), F32), pltpu.HBM((HEADS, T, 1), F32)],
        compiler_params=pltpu.CompilerParams(dimension_semantics=("parallel",), vmem_limit_bytes=_vmem(nbytes)),
    )(_hbm(q), _hbm(k), _hbm(v))


def _attn_bwd(q, k, v, o, lse, dcat):
    T = q.shape[0]
    BQ = _attn_block(T)
    nq = T // BQ

    def kern(q_ref, k_ref, v_ref, o_ref, lse_ref, do_ref, dq_ref, dk_ref, dv_ref, dd_ref):
        dq_ref[...] = jnp.zeros(dq_ref.shape, F32)

        def dloop(i, _):
            rows = pl.ds(pl.multiple_of(i * BQ, BQ), BQ)
            dd_ref[rows, :] = jnp.sum(do_ref[rows, :].astype(F32) * o_ref[rows, :], axis=-1, keepdims=True)
            return 0

        lax.fori_loop(0, nq, dloop, 0)

        def step(j, i, carry, masked):
            dk_acc, dv_acc = carry
            rq = pl.ds(pl.multiple_of(i * BQ, BQ), BQ)
            rk = pl.ds(pl.multiple_of(j * BQ, BQ), BQ)
            qb, kb, vb, dob = q_ref[rq, :], k_ref[rk, :], v_ref[rk, :], do_ref[rq, :]
            s = _dot(qb, kb, NT) * MLA_SCALE
            p = jnp.exp(s - lse_ref[0, rq, :])
            if masked:
                row = lax.broadcasted_iota(jnp.int32, s.shape, 0)
                col = lax.broadcasted_iota(jnp.int32, s.shape, 1)
                p = jnp.where(col <= row, p, 0.0)
            dp = _dot(dob, vb, NT)
            ds = (p * (dp - dd_ref[rq, :]) * MLA_SCALE).astype(BF16)
            dv_acc = dv_acc + _dot(p.astype(BF16), dob, TN_)
            dk_acc = dk_acc + _dot(ds, qb, TN_)
            dq_ref[rq, :] += _dot(ds, kb, NN)
            return dk_acc, dv_acc

        def kloop(j, _):
            init = (jnp.zeros((BQ, HEAD_W), F32), jnp.zeros((BQ, HEAD_W), F32))
            carry = step(j, j, init, True)
            dk_acc, dv_acc = lax.fori_loop(j + 1, nq, lambda i, c: step(j, i, c, False), carry)
            rk = pl.ds(pl.multiple_of(j * BQ, BQ), BQ)
            dk_ref[rk, :] = dk_acc
            dv_ref[rk, :] = dv_acc
            return 0

        lax.fori_loop(0, nq, kloop, 0)

    head = pl.BlockSpec((T, HEAD_W), lambda h: (0, h))
    nbytes = 4 * _nbytes((T, HEAD_W), BF16) + 5 * _nbytes((T, HEAD_W), F32) + 2 * _nbytes((T, 128), F32)
    return pl.pallas_call(
        kern, name="attn_bwd", grid=(HEADS,),
        in_specs=[head, head, head, head, pl.BlockSpec((1, T, 1), lambda h: (h, 0, 0)), head],
        out_specs=[head, head, head],
        out_shape=[pltpu.HBM((T, HEADS * HEAD_W), F32)] * 3,
        scratch_shapes=[pltpu.VMEM((T, 1), F32)],
        compiler_params=pltpu.CompilerParams(dimension_semantics=("parallel",), vmem_limit_bytes=_vmem(nbytes)),
    )(*[_hbm(a) for a in (q, k, v, o, lse, dcat)])


def _sgu_common(u, v, ln_g, ln_b):
    ua, tu = _gelu(u)
    va, tv = _gelu(v)
    vh, r = _ln_stats(va)
    return ua, tu, tv, vh, r, vh * ln_g + ln_b


def _tril_mask(n):
    return lax.broadcasted_iota(jnp.int32, (n, n), 1) <= lax.broadcasted_iota(jnp.int32, (n, n), 0)


def _sgu_fwd(zs, ln_g, ln_b, w, bias_full):
    def body(rows, consts, outs, accs):
        ua, _, _, _, _, vn = _sgu_common(rows[0][...], rows[1][...], consts[0][...], consts[1][...])
        vn = vn.astype(BF16)
        tri = _tril_mask(SGU_CHUNK)
        for g in range(SGU_G):
            wg = jnp.where(tri, consts[2][0, g], 0.0).astype(BF16)
            cols = slice(g * 128, (g + 1) * 128)
            for c in range(ua.shape[0] // SGU_CHUNK):
                rws = slice(c * SGU_CHUNK, (c + 1) * SGU_CHUNK)
                mixed = _dot(wg, vn[rws, cols], NN) + consts[3][:, cols]
                outs[0][rws, cols] = (ua[rws, cols] * mixed).astype(BF16)

    return _rowwise("sgu_fwd", body, [(zs, 512, 0), (zs, 512, 1)], [ln_g, ln_b, w, bias_full], [(SGU_DIM, BF16)])


def _sgu_bwd(zs, dcat, ln_g, ln_b, w, bias_full):
    def body(rows, consts, outs, accs):
        u, v = rows[0][...], rows[1][...]
        ua, tu, tv, vh, r, vn = _sgu_common(u, v, consts[0][...], consts[1][...])
        dout = rows[2][...].astype(F32)
        vn_bf = vn.astype(BF16)
        tri = _tril_mask(SGU_CHUNK)
        dmixed = (dout * ua)
        dmixed_bf = dmixed.astype(BF16)
        ones = jnp.ones((8, SGU_CHUNK), F32)
        dvn_cols, mixed_cols = [], []
        for g in range(SGU_G):
            wg = jnp.where(tri, consts[2][0, g], 0.0).astype(BF16)
            cols = slice(g * 128, (g + 1) * 128)
            dvn_rows, mixed_rows = [], []
            dw = jnp.zeros((SGU_CHUNK, SGU_CHUNK), F32)
            dmix_sum = jnp.zeros((SGU_CHUNK, 128), F32)
            for c in range(u.shape[0] // SGU_CHUNK):
                rws = slice(c * SGU_CHUNK, (c + 1) * SGU_CHUNK)
                mixed_rows.append(_dot(wg, vn_bf[rws, cols], NN) + consts[3][:, cols])
                dvn_rows.append(_dot(wg, dmixed_bf[rws, cols], TN_))
                dw = dw + _dot(dmixed_bf[rws, cols], vn_bf[rws, cols], NT)
                dmix_sum = dmix_sum + dmixed[rws, cols]
            accs[0][g] += jnp.where(tri, dw, 0.0)
            accs[3][g:g + 1, :] += _dot(ones, dmix_sum, NT, precision=HIGHEST)[0:1, :]
            dvn_cols.append(jnp.concatenate(dvn_rows, axis=0))
            mixed_cols.append(jnp.concatenate(mixed_rows, axis=0))
        dvn = jnp.concatenate(dvn_cols, axis=1)
        mixed = jnp.concatenate(mixed_cols, axis=1)
        accs[1][...] += jnp.sum(dvn * vh, axis=0, keepdims=True)
        accs[2][...] += jnp.sum(dvn, axis=0, keepdims=True)
        dvh = dvn * consts[0][...]
        dva = r * (dvh - jnp.mean(dvh, axis=-1, keepdims=True) - vh * jnp.mean(dvh * vh, axis=-1, keepdims=True))
        outs[0][:, 0:512] = (dout * mixed * _gelu_grad(u, tu)).astype(BF16)
        outs[0][:, 512:1024] = (dva * _gelu_grad(v, tv)).astype(BF16)

    return _rowwise("sgu_bwd", body, [(zs, 512, 0), (zs, 512, 1), (dcat, 512, 2)], [ln_g, ln_b, w, bias_full], [(1024, BF16)],
                    [((SGU_G, 128, 128), F32), ((1, SGU_DIM), F32), ((1, SGU_DIM), F32), ((SGU_G, 128), F32)], tr=256)


def _lower_bound(hg_lb):
    a0, a1 = hg_lb[0:1, :], hg_lb[1:2, :]
    m = jnp.maximum(a0, a1)
    e0, e1 = jnp.exp(a0 - m), jnp.exp(a1 - m)
    s0, s1 = e0 / (e0 + e1), e1 / (e0 + e1)
    return (s0 + s1) - s0, s0, s1


def _hg_gates(qr, fr, lb):
    C = qr.shape[0]
    sq = _sig(qr)
    qf = qr * sq
    sf = _sig(fr)
    gate = lb + (1.0 - lb) * sf
    kk = 1.0 - gate
    tri = _tril_mask(C)
    b = _dot(jnp.where(tri, 1.0, 0.0), jnp.log(gate), NN, precision=HIGHEST)
    bref = b[C // 2 - 1:C // 2, :]
    bl = b[C - 1:C, :]
    e_b = jnp.exp(b)
    e_q = jnp.exp(b - bref)
    e_k = jnp.exp(bref - b)
    e_lb = jnp.exp(bl - b)
    return dict(sq=sq, qf=qf, sf=sf, gate=gate, kk=kk, tri=tri, bl=bl, e_b=e_b, e_q=e_q, e_k=e_k, e_lb=e_lb)


def _hgrn_fwd(z1, hg_lb, gnorm):
    T = z1.shape[0]
    C = min(HG_CHUNK, T)
    nc = T // C

    def kern(q_ref, f_ref, i_ref, g_ref, lb_ref, gn_ref, o_ref, hg_ref, st_ref, s_scr):
        @pl.when(pl.program_id(0) == 0)
        def _():
            s_scr[...] = jnp.zeros(s_scr.shape, F32)

        lb_all, _, _ = _lower_bound(lb_ref[...])
        st_ref[0] = s_scr[...]
        for h in range(HEADS):
            cols = slice(h * HEAD_W, (h + 1) * HEAD_W)
            t = _hg_gates(q_ref[:, cols], f_ref[:, cols], lb_all[:, cols])
            v = i_ref[:, cols]
            v_bf = v.astype(BF16)
            st = s_scr[h]
            a = jnp.where(t["tri"], _dot((t["qf"] * t["e_q"]).astype(BF16), (t["kk"] * t["e_k"]).astype(BF16), NT), 0.0)
            o = _dot(a.astype(BF16), v_bf, NN) + _dot((t["qf"] * t["e_b"]).astype(BF16), st.astype(BF16), NT)
            s_scr[h] = st * jnp.exp(t["bl"]) + _dot(v_bf, (t["kk"] * t["e_lb"]).astype(BF16), TN_)
            o_ref[:, cols] = o
            gr = g_ref[:, cols]
            r = lax.rsqrt(jnp.mean(o * o, axis=-1, keepdims=True) + EPS)
            hg_ref[:, cols] = (o * r * gn_ref[:, cols] * (gr * _sig(gr))).astype(BF16)

    seg = lambda k: pl.BlockSpec((C, D_MODEL), functools.partial(lambda n, k: (n, k), k=k))
    row = pl.BlockSpec((C, D_MODEL), lambda n: (n, 0))
    nbytes = 6 * _nbytes((C, D_MODEL), F32) + 3 * _nbytes((HEADS, 128, 128), F32)
    return pl.pallas_call(
        kern, name="hgrn_fwd", grid=(nc,),
        in_specs=[seg(0), seg(1), seg(2), seg(3), pl.BlockSpec((2, D_MODEL), lambda n: (0, 0)),
                  pl.BlockSpec((1, D_MODEL), lambda n: (0, 0))],
        out_specs=[row, row, pl.BlockSpec((1, HEADS, 128, 128), lambda n: (n, 0, 0, 0))],
        out_shape=[pltpu.HBM((T, D_MODEL), F32), pltpu.HBM((T, D_MODEL), BF16),
                   pltpu.HBM((nc, HEADS, 128, 128), F32)],
        scratch_shapes=[pltpu.VMEM((HEADS, 128, 128), F32)],
        compiler_params=pltpu.CompilerParams(dimension_semantics=("arbitrary",), vmem_limit_bytes=_vmem(nbytes)),
    )(*[_hbm(a) for a in (z1, z1, z1, z1, hg_lb, gnorm)])


def _hgrn_bwd(z1, o_pre, dhg, states, hg_lb, gnorm):
    T = z1.shape[0]
    C = min(HG_CHUNK, T)
    nc = T // C

    def kern(q_ref, f_ref, i_ref, g_ref, o_ref, dhg_ref, st_ref, lb_ref, gn_ref, dz_ref, dlb_ref, dgn_ref, ds_scr, dlb_scr):
        n = pl.program_id(0)

        @pl.when(n == 0)
        def _():
            ds_scr[...] = jnp.zeros(ds_scr.shape, F32)
            dlb_scr[...] = jnp.zeros(dlb_scr.shape, F32)
            dgn_ref[...] = jnp.zeros(dgn_ref.shape, F32)

        lb_all, s0, s1 = _lower_bound(lb_ref[...])
        for h in range(HEADS):
            cols = slice(h * HEAD_W, (h + 1) * HEAD_W)
            lb = lb_all[:, cols]
            qr, fr = q_ref[:, cols], f_ref[:, cols]
            t = _hg_gates(qr, fr, lb)
            tri = t["tri"]
            v_bf = i_ref[:, cols].astype(BF16)
            st_bf = st_ref[0, h].astype(BF16)
            dst = ds_scr[h]
            dst_bf = dst.astype(BF16)
            o = o_ref[:, cols]
            gr = g_ref[:, cols]
            sg = _sig(gr)
            sil = gr * sg
            gn = gn_ref[:, cols]
            r = lax.rsqrt(jnp.mean(o * o, axis=-1, keepdims=True) + EPS)
            on = o * r
            dh = dhg_ref[:, cols].astype(F32)
            dgn_ref[:, cols] += jnp.sum(dh * on * sil, axis=0, keepdims=True)
            dg = dh * on * gn * (sg * (1.0 + gr * (1.0 - sg)))
            don = dh * gn * sil
            do_bf = (r * (don - on * jnp.mean(don * on, axis=-1, keepdims=True))).astype(BF16)
            qe = (t["qf"] * t["e_q"]).astype(BF16)
            ke = (t["kk"] * t["e_k"]).astype(BF16)
            qb = (t["qf"] * t["e_b"]).astype(BF16)
            kh_bf = (t["kk"] * t["e_lb"]).astype(BF16)
            a_bf = jnp.where(tri, _dot(qe, ke, NT), 0.0).astype(BF16)
            da_bf = jnp.where(tri, _dot(do_bf, v_bf, NT), 0.0).astype(BF16)
            dv = _dot(a_bf, do_bf, TN_) + _dot(kh_bf, dst_bf, NT)
            dqe = _dot(da_bf, ke, NN)
            dqb = _dot(do_bf, st_bf, NN)
            dke = _dot(da_bf, qe, TN_)
            dkh = _dot(v_bf, dst_bf, NN)
            dqf = dqe * t["e_q"] + dqb * t["e_b"]
            dkk = dke * t["e_k"] + dkh * t["e_lb"]
            kh_r = kh_bf.astype(F32)
            db = qe.astype(F32) * dqe - ke.astype(F32) * dke + qb.astype(F32) * dqb - kh_r * dkh
            e_bl = jnp.exp(t["bl"])
            dbl = jnp.sum(dkh * kh_r, axis=0, keepdims=True) + e_bl * jnp.sum(st_ref[0, h] * dst, axis=0, keepdims=True)
            dlg = _dot(jnp.where(tri, 1.0, 0.0), db, TN_, precision=HIGHEST) + dbl
            ds_scr[h] = dst * e_bl + _dot(do_bf, qb, TN_)
            dgate = dlg / t["gate"] - dkk
            sf = t["sf"]
            dlb_scr[:, cols] += jnp.sum(dgate * (1.0 - sf), axis=0, keepdims=True)
            df = dgate * (1.0 - lb) * sf * (1.0 - sf)
            dq = dqf * (t["sq"] * (1.0 + qr * (1.0 - t["sq"])))
            dz_ref[:, cols] = dq.astype(BF16)
            dz_ref[:, D_MODEL + h * HEAD_W:D_MODEL + (h + 1) * HEAD_W] = df.astype(BF16)
            dz_ref[:, 2 * D_MODEL + h * HEAD_W:2 * D_MODEL + (h + 1) * HEAD_W] = dv.astype(BF16)
            dz_ref[:, 3 * D_MODEL + h * HEAD_W:3 * D_MODEL + (h + 1) * HEAD_W] = dg.astype(BF16)

        @pl.when(n == nc - 1)
        def _():
            d = s0 * s1 * dlb_scr[...]
            dlb_ref[0:1, :] = -d
            dlb_ref[1:2, :] = d

    seg = lambda k: pl.BlockSpec((C, D_MODEL), functools.partial(lambda n, k: (nc - 1 - n, k), k=k))
    nbytes = 6 * _nbytes((C, D_MODEL), F32) + _nbytes((C, 4 * D_MODEL), BF16) + 3 * _nbytes((HEADS, 128, 128), F32)
    return pl.pallas_call(
        kern, name="hgrn_bwd", grid=(nc,),
        in_specs=[seg(0), seg(1), seg(2), seg(3), seg(0), seg(0),
                  pl.BlockSpec((1, HEADS, 128, 128), lambda n: (nc - 1 - n, 0, 0, 0)),
                  pl.BlockSpec((2, D_MODEL), lambda n: (0, 0)), pl.BlockSpec((1, D_MODEL), lambda n: (0, 0))],
        out_specs=[pl.BlockSpec((C, 4 * D_MODEL), lambda n: (nc - 1 - n, 0)),
                   pl.BlockSpec((2, D_MODEL), lambda n: (0, 0)), pl.BlockSpec((1, D_MODEL), lambda n: (0, 0))],
        out_shape=[pltpu.HBM((T, 4 * D_MODEL), BF16), pltpu.HBM((2, D_MODEL), F32),
                   pltpu.HBM((1, D_MODEL), F32)],
        scratch_shapes=[pltpu.VMEM((HEADS, 128, 128), F32), pltpu.VMEM((1, D_MODEL), F32)],
        compiler_params=pltpu.CompilerParams(dimension_semantics=("arbitrary",), vmem_limit_bytes=_vmem(nbytes)),
    )(*[_hbm(a) for a in (z1, z1, z1, z1, o_pre, dhg, states, hg_lb, gnorm)])


def _prep_weights(gw):
    w_in_e = gw["w_in_e"].transpose(1, 0, 2).reshape(D_MODEL, 1568)
    kr = jnp.pad(w_in_e[:, 512:544], ((0, 0), (64, 32)))
    wm = jnp.concatenate([w_in_e[:, 0:512], kr], axis=1)
    ws = w_in_e[:, 544:1568]
    w_qb = gw["w_qb"].transpose(1, 0, 2).reshape(MLA_LORA, HEADS, 96)
    wq = jnp.pad(w_qb, ((0, 0), (0, 0), (0, 32))).reshape(MLA_LORA, HEADS * HEAD_W)
    kvb = gw["w_kvb"].transpose(1, 0, 2).reshape(MLA_LORA, HEADS, 128)
    wk = jnp.pad(kvb[:, :, :64], ((0, 0), (0, 0), (0, 64))).reshape(MLA_LORA, HEADS * HEAD_W)
    wv = jnp.pad(kvb[:, :, 64:], ((0, 0), (0, 0), (0, 64))).reshape(MLA_LORA, HEADS * HEAD_W)
    w_out_e = gw["w_out_e"].reshape(D_MODEL, D_MODEL)
    woa = jnp.pad(w_out_e[:512].reshape(HEADS, 64, D_MODEL), ((0, 0), (0, 64), (0, 0))).reshape(HEADS * HEAD_W, D_MODEL)
    return dict(wm=wm, ws=ws, wq=wq, wk=wk, wv=wv, woa=woa, wob=w_out_e[512:])


def _unprep_grads(g):
    dwm, dws = g["wm"], g["ws"]
    d_in_e = jnp.concatenate([dwm[:, 0:512], dwm[:, 512 + 64:512 + 96], dws], axis=1)
    d_qb = g["wq"].reshape(MLA_LORA, HEADS, HEAD_W)[:, :, :96].reshape(MLA_LORA, HEADS * 96)
    dk = g["wk"].reshape(MLA_LORA, HEADS, HEAD_W)[:, :, :64]
    dv = g["wv"].reshape(MLA_LORA, HEADS, HEAD_W)[:, :, :64]
    d_kvb = jnp.concatenate([dk, dv], axis=2).reshape(MLA_LORA, HEADS * 128)
    d_oa = g["woa"].reshape(HEADS, HEAD_W, D_MODEL)[:, :64].reshape(HEADS * 64, D_MODEL)
    dev_major = lambda a: a.reshape(a.shape[0], N_DEV, a.shape[1] // N_DEV).transpose(1, 0, 2)
    return dict(w_in_e=dev_major(d_in_e), w_qb=dev_major(d_qb), w_kvb=dev_major(d_kvb),
                w_out_e=jnp.concatenate([d_oa, g["wob"]], axis=0).reshape(N_DEV, D_MODEL // N_DEV, D_MODEL))


def _local_step(x, positions, target, gw, sp):
    w = _prep_weights(gw)
    T = x.shape[0]
    tm = min(TM, T)
    nt = T // tm
    half = MLA_ROPE // 2
    inv_freq = ROPE_BASE ** (-jnp.arange(half, dtype=F32) / half)
    invf_lane = jnp.concatenate([jnp.zeros((64,), F32), inv_freq, inv_freq, jnp.zeros((32,), F32)]).reshape(1, HEAD_W)
    tabs = _rope_tables(positions.reshape(T, 1), invf_lane)
    bias_full = jnp.repeat(sp["sgu_b"][0].T, 128, axis=1)
    sgu_w = sp["sgu_w"]
    gq, gkv = sp["mla_gq"], sp["mla_gkv"]
    ln1_g, ln1_b, ln2_g, ln2_b = sp["ln1_g"], sp["ln1_b"], sp["ln2_g"], sp["ln2_b"]
    w_in_o, w_out_o = gw["w_in_o"], gw["w_out_o"].reshape(D_MODEL, D_MODEL)
    wide = HEADS * HEAD_W
    tab_rows = [_rb(t, tm) for t in tabs]
    resid = lambda acc, d: (acc + ALPHA * d,)

    zm = _tiled("l0_in_mla", (1, nt), [_rb(x, tm), _cw(w["wm"], 640)], [_out(T, 640, F32, tm, 640)], _mmc(NN))
    zs = _tiled("l0_in_sgu", (2, nt), [_rb(x, tm), _cw(w["ws"], TN)], [_out(T, 1024, F32, tm, TN)], _mmc(NN))
    cqn, ckvn, kr_rot = _mla_pre(zm, tabs, gq, gkv)
    q = _tiled("l0_q", (1, nt), [_rb(cqn, tm), _cw(w["wq"], wide)] + tab_rows, [_out(T, wide, BF16, tm, wide)],
               _mmc(NN, epilogue=lambda acc, c, s1, s2: (_rope_heads(acc, c, s1, s2, _rope),)))
    k = _tiled("l0_k", (1, nt), [_rb(ckvn, tm), _cw(w["wk"], wide), _rb(kr_rot, tm)], [_out(T, wide, BF16, tm, wide)],
               _mmc(NN, epilogue=lambda acc, kr: (acc + jnp.concatenate([kr] * HEADS, axis=1),)))
    v = _tiled("l0_v", (1, nt), [_rb(ckvn, tm), _cw(w["wv"], wide)], [_out(T, wide, BF16, tm, wide)], _mmc(NN))
    o_att, lse = _attn_fwd(q, k, v)
    b_out = _sgu_fwd(zs, sp["sgu_ln_g"], sp["sgu_ln_b"], sgu_w, bias_full)
    mix0 = _tiled("l0_out", (2, nt), [_rb(o_att, tm), _cw(w["woa"], TN), _rb(b_out, tm), _cw(w["wob"], TN)],
                  [_out(T, D_MODEL, F32, tm, TN)], _mmc(NN, n_pairs=2))
    y1, h1, h1_bf = _ln_fwd("l0_ln1", x, mix0, ln1_g, ln1_b, 0)
    a0, act0, ff0 = _mlp_fwd("l0", h1_bf, gw["w_ff1"][0], gw["w_ff2"][0])
    y2, h2, h2_bf = _ln_fwd("l0_ln2", h1, ff0, ln2_g, ln2_b, 0)

    z1 = _tiled("l1_in", (1, nt), [_rb(h2_bf, tm), _res(w_in_o)], [_out(T, 4 * D_MODEL, F32, tm, 4 * D_MODEL)],
                _mmc_blocks(N_DEV, NN, lambda w, d: w[d]), direct=True)
    o_pre, hg, states = _hgrn_fwd(z1, sp["hg_lb"], sp["hg_gnorm"])
    mix1 = _tiled("l1_out", (2, nt), [_rb(hg, tm), _cw(w_out_o, TN)], [_out(T, D_MODEL, F32, tm, TN)], _mmc(NN))
    y3, h3, h3_bf = _ln_fwd("l1_ln1", h2, mix1, ln1_g, ln1_b, 1)
    a1, act1, ff1 = _mlp_fwd("l1", h3_bf, gw["w_ff1"][1], gw["w_ff2"][1])
    y4, dh4, sq_err = _ln_loss("l1_ln2", h3, ff1, ln2_g, ln2_b, 1, target)

    gs, g0 = {}, {}
    dy4, dy4_bf, gs["ln2_g1"], gs["ln2_b1"] = _ln_bwd("l1_ln2_bwd", y4, dh4, ln2_g, 1)
    dh3, _, dw1_1, dw2_1 = _mlp_bwd("l1", h3_bf, a1, act1, dy4_bf, dy4, gw["w_ff1"][1], gw["w_ff2"][1])
    dy3, dy3_bf, gs["ln1_g1"], gs["ln1_b1"] = _ln_bwd("l1_ln1_bwd", y3, dh3, ln1_g, 1)
    d_out_o = _tiled("l1_dwout", (2, D_MODEL // TM), [_tl(hg, TM), _cw(dy3_bf, TN)], [_out(D_MODEL, D_MODEL, F32, TM, TN)],
                     _mmc(TN_)).reshape(N_DEV, D_MODEL // N_DEV, D_MODEL)
    dhg = _tiled("l1_dhg", (2, nt), [_rb(dy3_bf, tm), _rw(w_out_o, TN)], [_out(T, D_MODEL, BF16, tm, TN)], _mmc(NT))
    dz1, gs["hg_lb"], gs["hg_gnorm"] = _hgrn_bwd(z1, o_pre, dhg, states, sp["hg_lb"], sp["hg_gnorm"])
    d_in_o = _tiled("l1_dwin", (N_DEV, 1), [_res(h2_bf), _cw(dz1, TN)], [_out_dev(D_MODEL, TN, D_MODEL)], _mmc(TN_))
    dh2 = _tiled("l1_dh2", (1, nt), [_rb(dz1, tm), _res(w_in_o), _rb(dy3, tm)], [_out(T, D_MODEL, F32, tm, D_MODEL)],
                 _mmc_dev(epilogue=resid))

    dy2, dy2_bf, gs["ln2_g0"], gs["ln2_b0"] = _ln_bwd("l0_ln2_bwd", y2, dh2, ln2_g, 0)
    dh1, _, dw1_0, dw2_0 = _mlp_bwd("l0", h1_bf, a0, act0, dy2_bf, dy2, gw["w_ff1"][0], gw["w_ff2"][0])
    dy1, dy1_bf, gs["ln1_g0"], gs["ln1_b0"] = _ln_bwd("l0_ln1_bwd", y1, dh1, ln1_g, 0)
    g0["woa"] = _tiled("l0_dwoa", (2, wide // TM), [_tl(o_att, TM), _cw(dy1_bf, TN)], [_out(wide, D_MODEL, F32, TM, TN)], _mmc(TN_))
    g0["wob"] = _tiled("l0_dwob", (2, 1), [_tl(b_out, SGU_DIM), _cw(dy1_bf, TN)], [_out(SGU_DIM, D_MODEL, F32, SGU_DIM, TN)], _mmc(TN_))
    wo_cat = jnp.concatenate([w["woa"], w["wob"]], axis=0)
    dcat = _tiled("l0_dcat", (3, nt), [_rb(dy1_bf, tm), _rw(wo_cat, TN)], [_out(T, wide + SGU_DIM, BF16, tm, TN)], _mmc(NT))
    dzs, gs["sgu_w"], gs["sgu_ln_g"], gs["sgu_ln_b"], gs["sgu_b"] = _sgu_bwd(zs, dcat, sp["sgu_ln_g"], sp["sgu_ln_b"], sgu_w, bias_full)
    dq, dk, dv = _attn_bwd(q, k, v, o_att, lse, dcat)
    dq_pre = _unrope_heads(dq, tabs)
    lora_w = lambda name, a, d: _tiled(name, (wide // TN, 1), [_tl(a, MLA_LORA), _cw(d, TN)], [_out(MLA_LORA, wide, F32, MLA_LORA, TN)], _mmc(TN_))
    g0["wq"] = lora_w("l0_dwq", cqn, dq_pre)
    g0["wk"] = lora_w("l0_dwk", ckvn, dk)
    g0["wv"] = lora_w("l0_dwv", ckvn, dv)
    dcqn = _tiled("l0_dcqn", (1, nt), [_rb(dq_pre, tm), _rw(w["wq"], MLA_LORA)], [_out(T, MLA_LORA, F32, tm, MLA_LORA)], _mmc(NT))
    dckvn = _tiled("l0_dckvn", (1, nt), [_rb(dk, tm), _rw(w["wk"], MLA_LORA), _rb(dv, tm), _rw(w["wv"], MLA_LORA)],
                   [_out(T, MLA_LORA, F32, tm, MLA_LORA)], _mmc(NT, n_pairs=2))
    dzm, gs["mla_gq"], gs["mla_gkv"] = _mla_pre_bwd(zm, tabs, gq, gkv, dcqn, dckvn, dk)
    g0["wm"] = _tiled("l0_dwm", (1, D_MODEL // TM), [_tl(x, TM), _cw(dzm, 640)], [_out(D_MODEL, 640, F32, TM, 640)], _mmc(TN_))
    g0["ws"] = _tiled("l0_dws", (2, D_MODEL // TM), [_tl(x, TM), _cw(dzs, TN)], [_out(D_MODEL, 1024, F32, TM, TN)], _mmc(TN_))
    dx = _tiled("l0_dx", (2, nt), [_rb(dzm, tm), _rw(w["wm"], TN), _rb(dzs, tm), _rw(w["ws"], TN), _rbj(dy1, tm, TN)],
                [_out(T, D_MODEL, F32, tm, TN)], _mmc(NT, n_pairs=2, epilogue=resid))

    grads = _unprep_grads(g0)
    grads.update(w_in_o=d_in_o, w_out_o=d_out_o, w_ff1=[dw1_0, dw1_1], w_ff2=[dw2_0, dw2_1])
    return sq_err, dx, grads, gs


def _me():
    return lax.axis_index("x"), lax.axis_index("y"), lax.axis_index("c")


def _hbm_call(name, kern, operands, out_shape, n_sems, extra_scratch=()):
    any_spec = pl.BlockSpec(memory_space=pl.ANY)
    return pl.pallas_call(
        kern, name=name, out_shape=out_shape, in_specs=[any_spec] * len(operands), out_specs=[any_spec] * len(out_shape),
        scratch_shapes=[pltpu.SemaphoreType.DMA((n_sems,)), pltpu.SemaphoreType.DMA((n_sems,)), *extra_scratch],
    )(*[_hbm(a) for a in operands])


def _all_gather(shards):
    n = len(shards)

    def kern(*refs):
        x_refs, out_refs, (send_sems, recv_sems, local_sems) = refs[:n], refs[n:2 * n], refs[2 * n:]
        x, y, c = _me()
        me, sibling = (x, y, c), (x, y, 1 - c)
        chips = [(1 - x, y), (x, 1 - y), (1 - x, 1 - y)]

        def copy(op, k, block, to, own=False):
            slot = out_refs[op].at[4 * block[0] + 2 * block[1] + block[2]]
            return pltpu.make_async_remote_copy(
                src_ref=x_refs[op] if own else slot, dst_ref=slot, send_sem=send_sems.at[7 * op + k],
                recv_sem=recv_sems.at[7 * op + k], device_id=to, device_id_type=MESH)

        mine = [pltpu.make_async_copy(x_refs[op], out_refs[op].at[4 * x + 2 * y + c], local_sems.at[op]) for op in range(n)]
        for cp in mine:
            cp.start()
        first = []
        for op in range(n):
            first.append(copy(op, 0, me, sibling, own=True))
            first += [copy(op, 1 + j, me, (*chip, c), own=True) for j, chip in enumerate(chips)]
        for cp in first:
            cp.start()
        passed = []
        for j, chip in enumerate(chips):
            for op in range(n):
                copy(op, 1 + j, (*chip, c), me).wait_recv()
                passed.append(copy(op, 4 + j, (*chip, c), sibling))
                passed[-1].start()
        for op in range(n):
            copy(op, 0, sibling, me).wait_recv()
            for j, chip in enumerate(chips):
                copy(op, 4 + j, (*chip, 1 - c), me).wait_recv()
        for cp in first + passed:
            cp.wait_send()
        for cp in mine:
            cp.wait()

    out_shape = [pltpu.HBM((N_DEV, *s.shape), s.dtype) for s in shards]
    return _hbm_call("weights_all_gather", kern, shards, out_shape, 7 * n, [pltpu.SemaphoreType.DMA((n,))])


def _rs_sibling(grads):
    n = len(grads)

    def kern(*refs):
        g_refs, out_refs, (send_sems, recv_sems) = refs[:n], refs[n:2 * n], refs[2 * n:]
        x, y, c = _me()
        copies = [pltpu.make_async_remote_copy(
            src_ref=g_refs[op].at[k, 1 - c], dst_ref=out_refs[op].at[k], send_sem=send_sems.at[4 * op + k],
            recv_sem=recv_sems.at[4 * op + k], device_id=(x, y, 1 - c), device_id_type=MESH) for op in range(n) for k in range(4)]
        for cp in copies:
            cp.start()
        for cp in copies:
            cp.wait()

    out_shape = [pltpu.HBM((4, *g.shape[2:]), g.dtype) for g in grads]
    return _hbm_call("grads_to_sibling", kern, grads, out_shape, 4 * n)


def _rs_chips(sums):
    n = len(sums)

    def kern(*refs):
        p_refs, out_refs, (send_sems, recv_sems) = refs[:n], refs[n:2 * n], refs[2 * n:]
        x, y, c = _me()
        chips = [(1 - x, y), (x, 1 - y), (1 - x, 1 - y)]
        copies = [pltpu.make_async_remote_copy(
            src_ref=p_refs[op].at[2 * cx + cy], dst_ref=out_refs[op].at[j], send_sem=send_sems.at[3 * op + j],
            recv_sem=recv_sems.at[3 * op + j], device_id=(cx, cy, c), device_id_type=MESH)
            for op in range(n) for j, (cx, cy) in enumerate(chips)]
        for cp in copies:
            cp.start()
        for cp in copies:
            cp.wait()

    out_shape = [pltpu.HBM((3, *p.shape[1:]), p.dtype) for p in sums]
    return _hbm_call("grads_between_chips", kern, sums, out_shape, 3 * n)


def _row_tile(r):
    return r if r <= 256 else 256


def _chip_sum(name, g, from_sibling, core):
    _, _, R, W = g.shape
    tr = _row_tile(R)

    def kern(core_ref, g_ref, s_ref, o_ref):
        o_ref[...] = (g_ref[...] + s_ref[...]).astype(BF16)

    return pl.pallas_call(
        kern, name=name, out_shape=pltpu.HBM((4, R, W), BF16),
        grid_spec=pltpu.PrefetchScalarGridSpec(
            num_scalar_prefetch=1, grid=(4, R // tr),
            in_specs=[pl.BlockSpec((None, None, tr, W), lambda k, i, core: (k, core[0], i, 0)),
                      pl.BlockSpec((None, tr, W), lambda k, i, core: (k, i, 0))],
            out_specs=pl.BlockSpec((None, tr, W), lambda k, i, core: (k, i, 0))),
        compiler_params=pltpu.CompilerParams(dimension_semantics=("parallel", "parallel"), vmem_limit_bytes=_vmem(3 * tr * W * 4)),
    )(core, _hbm(g), _hbm(from_sibling))


def _adamw(w, g, m, v):
    m = ADAM_B1 * m + (1.0 - ADAM_B1) * g
    v = ADAM_B2 * v + (1.0 - ADAM_B2) * (g * g)
    m_hat = m / (1.0 - ADAM_B1 ** ADAM_STEP)
    v_hat = v / (1.0 - ADAM_B2 ** ADAM_STEP)
    return -ADAM_LR * (m_hat / (jnp.sqrt(v_hat) + ADAM_EPS) + ADAM_WD * w), m, v


def _finish_sharded(name, layers, w, m, v, where):
    nl, R, W = w.shape
    tr = _row_tile(R)

    def kern(where_ref, *refs):
        w_ref, m_ref, v_ref, go_ref, d_ref, mo_ref, vo_ref = refs[3 * nl:]
        for l in range(nl):
            g_ref, s_ref, c_ref = refs[3 * l:3 * l + 3]
            grad = g_ref[...] + s_ref[...]
            for j in range(3):
                grad = grad + c_ref[j].astype(F32)
            go_ref[l] = grad
            d_ref[l], mo_ref[l], vo_ref[l] = _adamw(w_ref[l], grad, m_ref[l], v_ref[l])

    row = pl.BlockSpec((nl, tr, W), lambda i, wh: (0, i, 0))
    in_specs, args = [], []
    for g, s, c in layers:
        in_specs += [pl.BlockSpec((None, None, tr, W), lambda i, wh: (wh[0], wh[1], i, 0)),
                     pl.BlockSpec((None, tr, W), lambda i, wh: (wh[0], i, 0)),
                     pl.BlockSpec((3, tr, W), lambda i, wh: (0, i, 0))]
        args += [g, s, c]
    return pl.pallas_call(
        kern, name=name, out_shape=[pltpu.HBM((nl, R, W), F32)] * 4,
        grid_spec=pltpu.PrefetchScalarGridSpec(num_scalar_prefetch=1, grid=(R // tr,), in_specs=in_specs + [row, row, row],
                                               out_specs=[row, row, row, row]),
        compiler_params=pltpu.CompilerParams(dimension_semantics=("parallel",), vmem_limit_bytes=_vmem(nl * 11 * tr * W * 4)),
    )(where, *[_hbm(a) for a in (*args, w, m, v)])


SMALL_PLACE = (("mla_gq", 0, 0, 1, 256), ("mla_gkv", 0, 256, 1, 256), ("sgu_ln_g", 0, 512, 1, 512), ("sgu_ln_b", 1, 0, 1, 512),
               ("hg_lb", 2, 0, 2, 1024), ("ln1_g", 4, 0, 2, 1024), ("ln1_b", 6, 0, 2, 1024), ("sgu_b", 8, 0, 4, 128),
               ("ln2_g", 12, 0, 2, 1024), ("ln2_b", 14, 0, 2, 1024), ("hg_gnorm", 16, 0, 1, 1024))
SMALL_BUF_ROWS = 24


def _small_reduce_adamw(gs, given):
    pieces = [(gs["mla_gq"], 0, 0), (gs["mla_gkv"], 0, 256), (gs["sgu_ln_g"], 0, 512), (gs["sgu_ln_b"], 1, 0), (gs["hg_lb"], 2, 0),
              (gs["ln1_g0"], 4, 0), (gs["ln1_g1"], 5, 0), (gs["ln1_b0"], 6, 0), (gs["ln1_b1"], 7, 0), (gs["sgu_b"], 8, 0),
              (gs["ln2_g0"], 12, 0), (gs["ln2_g1"], 13, 0), (gs["ln2_b0"], 14, 0), (gs["ln2_b1"], 15, 0), (gs["hg_gnorm"], 16, 0)]
    names = [p[0] for p in SMALL_PLACE] + ["sgu_w"]
    n_p, n_names = len(pieces), len(names)
    wmv = [given[pre + name] for name in names for pre in ("", "m_", "v_")]

    def kern(*refs):
        piece_refs, gw_ref = refs[:n_p], refs[n_p]
        wmv_refs = refs[n_p + 1:n_p + 1 + 3 * n_names]
        out_refs = refs[n_p + 1 + 3 * n_names:n_p + 1 + 7 * n_names]
        buf_a, buf_b, send_sems, recv_sems = refs[n_p + 1 + 7 * n_names:]
        px, py, pc = _me()
        me = 4 * px + 2 * py + pc
        mine_a, mine_b = buf_a.at[me], buf_b.at[me]
        mine_a[...] = jnp.zeros(mine_a.shape, F32)
        for ref, (_, r, l0) in zip(piece_refs, pieces):
            mine_a[r:r + ref.shape[0], l0:l0 + ref.shape[1]] = ref[...]
        mine_b[...] = gw_ref[...]
        copies = []
        for r in range(1, N_DEV):
            peer = (px ^ (r >> 2), py ^ ((r >> 1) & 1), pc ^ (r & 1))
            for k, mine in enumerate((mine_a, mine_b)):
                copies.append(pltpu.make_async_remote_copy(
                    src_ref=mine, dst_ref=mine, send_sem=send_sems.at[2 * (r - 1) + k], recv_sem=recv_sems.at[2 * (r - 1) + k],
                    device_id=peer, device_id_type=MESH))
        for cp in copies:
            cp.start()
        for r in range(1, N_DEV):
            for k, buf in enumerate((buf_a, buf_b)):
                theirs = buf.at[me ^ r]
                pltpu.make_async_remote_copy(
                    src_ref=theirs, dst_ref=theirs, send_sem=send_sems.at[2 * (r - 1) + k], recv_sem=recv_sems.at[2 * (r - 1) + k],
                    device_id=(px, py, pc), device_id_type=MESH).wait_recv()
        for cp in copies:
            cp.wait_send()
        sum_a, sum_b = buf_a[0], buf_b[0]
        for d in range(1, N_DEV):
            sum_a, sum_b = sum_a + buf_a[d], sum_b + buf_b[d]

        def own_block(full):
            acc = full[:, 0:128]
            for b in range(1, N_DEV):
                acc = jnp.where(me == b, full[:, b * 128:(b + 1) * 128], acc)
            return acc

        for idx, name in enumerate(names):
            w_ref, m_ref, v_ref = wmv_refs[3 * idx:3 * idx + 3]
            if name == "sgu_w":
                grad = sum_b[None]
            else:
                _, r, l0, nr, nl = SMALL_PLACE[idx]
                grad = sum_a[r:r + nr, l0:l0 + nl]
                if name == "hg_gnorm":
                    grad = own_block(grad)
                if name == "sgu_b":
                    grad = grad[None]
            res = (grad, *_adamw(w_ref[...], grad, m_ref[...], v_ref[...]))
            for o_ref, val in zip(out_refs[4 * idx:4 * idx + 4], res):
                o_ref[...] = val

    vmem = pl.BlockSpec(memory_space=pltpu.VMEM)
    operands = [p[0] for p in pieces] + [gs["sgu_w"]] + wmv
    out_shape = [jax.ShapeDtypeStruct(given[name].shape, F32) for name in names for _ in range(4)]
    res = pl.pallas_call(
        kern, name="small_all_reduce_adamw", out_shape=out_shape, in_specs=[vmem] * len(operands), out_specs=[vmem] * len(out_shape),
        scratch_shapes=[pltpu.VMEM((N_DEV, SMALL_BUF_ROWS, D_MODEL), F32), pltpu.VMEM((N_DEV, SGU_G, 128, 128), F32),
                        pltpu.SemaphoreType.DMA((14,)), pltpu.SemaphoreType.DMA((14,))],
    )(*operands)
    return {name: res[4 * idx:4 * idx + 4] for idx, name in enumerate(names)}


SHARDED = ("w_in_e", "w_qb", "w_kvb", "w_out_e", "w_in_o", "w_out_o", "w_ff1", "w_ff2")


def kernel(x, positions, w_in_e, mla_gq, mla_gkv, w_qb, w_kvb, sgu_ln_g, sgu_ln_b, sgu_w, sgu_b, w_out_e, w_in_o, hg_lb, hg_gnorm, w_out_o, ln1_g, ln1_b, w_ff1, w_ff2, ln2_g, ln2_b, loss_target, m_w_in_e, m_mla_gq, m_mla_gkv, m_w_qb, m_w_kvb, m_sgu_ln_g, m_sgu_ln_b, m_sgu_w, m_sgu_b, m_w_out_e, m_w_in_o, m_hg_lb, m_hg_gnorm, m_w_out_o, m_ln1_g, m_ln1_b, m_w_ff1, m_w_ff2, m_ln2_g, m_ln2_b, v_w_in_e, v_mla_gq, v_mla_gkv, v_w_qb, v_w_kvb, v_sgu_ln_g, v_sgu_ln_b, v_sgu_w, v_sgu_b, v_w_out_e, v_w_in_o, v_hg_lb, v_hg_gnorm, v_w_out_o, v_ln1_g, v_ln1_b, v_w_ff1, v_w_ff2, v_ln2_g, v_ln2_b):
    given = dict(locals())
    px, py, pc = _me()

    names = ["w_in_e", "w_qb", "w_kvb", "w_out_e", "w_in_o", "w_out_o"]
    shards = [given[n][0].astype(BF16) for n in names]
    shards += [w_ff1[0].astype(BF16), w_ff1[1].astype(BF16), w_ff2[0].astype(BF16), w_ff2[1].astype(BF16), hg_gnorm]
    got = _all_gather(shards)
    gw = dict(zip(names, got[:6]))
    gw["w_ff1"], gw["w_ff2"] = [got[6], got[7]], [got[8], got[9]]
    small_names = ["mla_gq", "mla_gkv", "sgu_ln_g", "sgu_ln_b", "sgu_w", "sgu_b", "hg_lb", "ln1_g", "ln1_b", "ln2_g", "ln2_b"]
    sp = {n: given[n] for n in small_names}
    sp["hg_gnorm"] = got[10].reshape(1, D_MODEL)

    sq_err, dx, grads, gs = _local_step(x[0], positions[0], loss_target[0], gw, sp)
    loss = lax.psum(0.5 * jnp.sum(sq_err) / D_MODEL, ("x", "y", "c"))

    flat = [grads[n] for n in names] + grads["w_ff1"] + grads["w_ff2"]
    blocks = [g.reshape(4, 2, *g.shape[1:]) for g in flat]
    from_sibling = _rs_sibling(blocks)
    core = pc.reshape(1).astype(jnp.int32)
    chip_sums = [_chip_sum(f"grads_chip_sum_{k}", b, s, core) for k, (b, s) in enumerate(zip(blocks, from_sibling))]
    from_chips = _rs_chips(chip_sums)
    where = jnp.stack([2 * px + py, pc]).astype(jnp.int32)
    layers = list(zip(blocks, from_sibling, from_chips))
    per_weight = dict(zip(names, [[l] for l in layers[:6]]))
    per_weight["w_ff1"], per_weight["w_ff2"] = layers[6:8], layers[8:10]
    results = {n: _finish_sharded(f"finish_{n}", per_weight[n], given[n], given["m_" + n], given["v_" + n], where) for n in SHARDED}

    results.update(_small_reduce_adamw(gs, given))

    order = ["w_in_e", "mla_gq", "mla_gkv", "w_qb", "w_kvb", "sgu_ln_g", "sgu_ln_b", "sgu_w", "sgu_b", "w_out_e", "w_in_o",
             "hg_lb", "hg_gnorm", "w_out_o", "ln1_g", "ln1_b", "w_ff1", "w_ff2", "ln2_g", "ln2_b"]
    return (loss, dx[None], *[results[name][kind] for kind in range(4) for name in order])
```

```python
import functools
import math

import jax
import jax.numpy as jnp
import numpy as np
from jax import lax
from jax.experimental import pallas as pl
from jax.experimental.pallas import tpu as pltpu

F32 = jnp.float32
BF16 = jnp.bfloat16
MESH = pl.DeviceIdType.MESH
HIGHEST = lax.Precision.HIGHEST

D_MODEL = 1024
D_FF = 4096
N_DEV = 8
HEADS = 8
HEAD_W = 128
MLA_NOPE = 64
MLA_ROPE = 32
MLA_V = 64
MLA_LORA = 256
MLA_SCALE = (MLA_NOPE + MLA_ROPE) ** -0.5
ROPE_BASE = 10000.0
SGU_DIM = 512
SGU_G = 4
SGU_CHUNK = 128
HG_CHUNK = 64
ALPHA = (2 * 2) ** 0.25
EPS = 1e-5
ADAM_LR, ADAM_B1, ADAM_B2, ADAM_EPS, ADAM_WD, ADAM_STEP = 0.001, 0.9, 0.999, 1e-08, 0.01, 10

VMEM_CAP_V7X = 56 * 2**20
VMEM_SLACK = 12 * 2**20
TM = 512
TN = 512


def _vmem(block_bytes):
    return int(min(VMEM_CAP_V7X, 2 * block_bytes + VMEM_SLACK))


def _hbm(a):
    return pltpu.with_memory_space_constraint(a, pltpu.HBM)


def _nbytes(shape, dtype):
    return int(np.prod([d for d in shape if d is not None])) * jnp.dtype(dtype).itemsize


def _sig(x):
    return 1.0 / (1.0 + jnp.exp(-x))


def _gelu(x):
    c = math.sqrt(2.0 / math.pi)
    t = jnp.tanh(c * (x + 0.044715 * x * x * x))
    return 0.5 * x * (1.0 + t), t


def _gelu_grad(x, t):
    c = math.sqrt(2.0 / math.pi)
    return 0.5 * (1.0 + t) + 0.5 * x * (1.0 - t * t) * c * (1.0 + 3 * 0.044715 * x * x)


def _dot(a, b, dims, precision=None):
    return lax.dot_general(a, b, (dims, ((), ())), preferred_element_type=F32, precision=precision)


NN = ((1,), (0,))
NT = ((1,), (1,))
TN_ = ((0,), (0,))


def _deps(deps):
    return [d for d in deps if d is not None]


def _tiled(name, grid, ins, outs, compute, direct=False, deps=()):
    n_in, deps = len(ins), _deps(deps)
    n_skip = n_in + len(deps)

    def kern(*refs):
        if direct:
            compute(refs[:n_in], refs[n_skip:])
            return
        for o_ref, r in zip(refs[n_skip:], compute(*refs[:n_in])):
            o_ref[...] = r.astype(o_ref.dtype).reshape(o_ref.shape)

    swap = lambda f: (lambda j, i: f(i, j))
    nbytes = sum(_nbytes(blk, a.dtype) for a, blk, _ in ins) + sum(_nbytes(blk, dt) + _nbytes(blk, F32) for _, dt, blk, _ in outs)
    res = pl.pallas_call(
        kern, name=name, grid=grid,
        in_specs=[pl.BlockSpec(blk, swap(f)) for _, blk, f in ins] + [ANY_SPEC] * len(deps),
        out_specs=[pl.BlockSpec(blk, swap(f)) for _, _, blk, f in outs],
        out_shape=[pltpu.HBM(shape, dt) for shape, dt, _, _ in outs],
        compiler_params=pltpu.CompilerParams(dimension_semantics=("parallel", "parallel"), vmem_limit_bytes=_vmem(nbytes)),
    )(*[_hbm(a) for a, _, _ in ins], *deps)
    return res if len(res) > 1 else res[0]


def _rb(a, tm, w=None, cb=0):
    return (a, (tm, a.shape[1] if w is None else w), lambda i, j: (i, cb))


def _rbj(a, tm, tn):
    return (a, (tm, tn), lambda i, j: (i, j))


def _cw(b, tn):
    return (b, (b.shape[0], tn), lambda i, j: (0, j))


def _rw(b, tn):
    return (b, (tn, b.shape[1]), lambda i, j: (j, 0))


def _tl(a, tm):
    return (a, (a.shape[0], tm), lambda i, j: (0, i))


def _gcw(g):
    return (g, (None, g.shape[1], g.shape[2]), lambda i, j: (j, 0, 0))


def _grw(g, tn):
    return (g, (N_DEV, tn, g.shape[2]), lambda i, j: (0, j, 0))


def _out(m, n, dtype, tm, tn):
    return ((m, n), dtype, (tm, tn), lambda i, j: (i, j))


def _out_dev(k, n, tm):
    return ((N_DEV, k, n), F32, (None, tm, n), lambda i, j: (j, i, 0))


def _mmc(dims, n_pairs=1, epilogue=None):
    def compute(*refs):
        acc = None
        for k in range(n_pairs):
            d = _dot(refs[2 * k][...].astype(BF16), refs[2 * k + 1][...].astype(BF16), dims)
            acc = d if acc is None else acc + d
        ext = [r[...] for r in refs[2 * n_pairs:]]
        return epilogue(acc, *ext) if epilogue is not None else (acc,)

    return compute


def _res(w):
    return (w, w.shape, functools.partial(lambda i, j, nd: (0,) * nd, nd=w.ndim))


def _mmc_blocks(nblk, dims, rhs_block, epilogue=None):
    def compute(in_refs, out_refs):
        a = in_refs[0][...].astype(BF16)
        for d in range(nblk):
            acc = _dot(a, rhs_block(in_refs[1], d).astype(BF16), dims)
            n = acc.shape[1]
            ext = [r[:, d * n:(d + 1) * n] for r in in_refs[2:]]
            res = epilogue(acc, *ext) if epilogue is not None else (acc,)
            for o_ref, r in zip(out_refs, res):
                o_ref[:, d * n:(d + 1) * n] = r.astype(o_ref.dtype)

    return compute


def _mmc_dev(epilogue=None):
    def compute(a_ref, b_ref, *ext_refs):
        n = b_ref.shape[2]
        acc = None
        for d in range(N_DEV):
            t = _dot(a_ref[:, d * n:(d + 1) * n].astype(BF16), b_ref[d].astype(BF16), NT)
            acc = t if acc is None else acc + t
        ext = [r[...] for r in ext_refs]
        return epilogue(acc, *ext) if epilogue is not None else (acc,)

    return compute


def _rowwise(name, body, rows, consts, out_rows, out_accs=(), tr=512, deps=()):
    T = rows[0][0].shape[0]
    tr = min(tr, T)
    deps = _deps(deps)
    nr, ncn, no, nd = len(rows), len(consts), len(out_rows), len(deps)

    def kern(*refs):
        accs = refs[nr + ncn + nd + no:]
        if accs:
            @pl.when(pl.program_id(0) == 0)
            def _():
                for a in accs:
                    a[...] = jnp.zeros(a.shape, a.dtype)
        body(refs[:nr], refs[nr:nr + ncn], refs[nr + ncn + nd:nr + ncn + nd + no], accs)

    in_specs = [pl.BlockSpec((tr, w), functools.partial(lambda i, cb: (i, cb), cb=cb)) for _, w, cb in rows]
    in_specs += [pl.BlockSpec(c.shape, functools.partial(lambda i, nd: (0,) * nd, nd=c.ndim)) for c in consts]
    in_specs += [ANY_SPEC] * nd
    out_specs = [pl.BlockSpec((tr, w), lambda i: (i, 0)) for w, _ in out_rows]
    out_specs += [pl.BlockSpec(s, functools.partial(lambda i, nd: (0,) * nd, nd=len(s))) for s, _ in out_accs]
    out_shape = [pltpu.HBM((T, w), dt) for w, dt in out_rows]
    out_shape += [pltpu.HBM(s, dt) for s, dt in out_accs]
    nbytes = sum(_nbytes((tr, w), a.dtype) for a, w, _ in rows) + sum(_nbytes(c.shape, c.dtype) for c in consts)
    nbytes += sum(_nbytes((tr, w), dt) for w, dt in out_rows) + sum(_nbytes(s, dt) for s, dt in out_accs)
    res = pl.pallas_call(
        kern, name=name, grid=(T // tr,), in_specs=in_specs, out_specs=out_specs, out_shape=out_shape,
        compiler_params=pltpu.CompilerParams(dimension_semantics=("arbitrary",), vmem_limit_bytes=_vmem(nbytes)),
    )(*[_hbm(a) for a, _, _ in rows], *[_hbm(c) for c in consts], *deps)
    return res if len(res) > 1 else res[0]


def _full(a):
    return (a, a.shape[1], 0)


def _ln_stats(y):
    mu = jnp.mean(y, axis=-1, keepdims=True)
    yc = y - mu
    r = lax.rsqrt(jnp.mean(yc * yc, axis=-1, keepdims=True) + EPS)
    return yc * r, r


def _ln_fwd(name, h_in, mix, g, b, layer):
    def body(rows, consts, outs, accs):
        y = ALPHA * rows[0][...] + rows[1][...]
        xh, _ = _ln_stats(y)
        h = xh * consts[0][layer:layer + 1, :] + consts[1][layer:layer + 1, :]
        outs[0][...] = y
        outs[1][...] = h
        outs[2][...] = h.astype(BF16)

    return _rowwise(name, body, [_full(h_in), _full(mix)], [g, b], [(D_MODEL, F32), (D_MODEL, F32), (D_MODEL, BF16)], tr=256)


def _ln_loss(name, h_in, mix, g, b, layer, target):
    def body(rows, consts, outs, accs):
        y = ALPHA * rows[0][...] + rows[1][...]
        xh, _ = _ln_stats(y)
        err = xh * consts[0][layer:layer + 1, :] + consts[1][layer:layer + 1, :] - rows[2][...]
        outs[0][...] = y
        outs[1][...] = err * (1.0 / D_MODEL)
        accs[0][...] += jnp.sum(err * err, axis=0, keepdims=True)

    return _rowwise(name, body, [_full(h_in), _full(mix), _full(target)], [g, b], [(D_MODEL, F32), (D_MODEL, F32)],
                    [((1, D_MODEL), F32)], tr=256)


def _ln_bwd(name, y, dh, g, layer):
    def body(rows, consts, outs, accs):
        xh, r = _ln_stats(rows[0][...])
        d = rows[1][...]
        accs[0][...] += jnp.sum(d * xh, axis=0, keepdims=True)
        accs[1][...] += jnp.sum(d, axis=0, keepdims=True)
        dx = d * consts[0][layer:layer + 1, :]
        dy = r * (dx - jnp.mean(dx, axis=-1, keepdims=True) - xh * jnp.mean(dx * xh, axis=-1, keepdims=True))
        outs[0][...] = dy
        outs[1][...] = dy.astype(BF16)

    return _rowwise(name, body, [_full(y), _full(dh)], [g], [(D_MODEL, F32), (D_MODEL, BF16)],
                    [((1, D_MODEL), F32), ((1, D_MODEL), F32)], tr=256)


def _relu2_epilogue(acc):
    a = jnp.maximum(acc, 0.0)
    return acc, a * a


def _mlp_fwd(tag, h_bf, w1, w2):
    T = h_bf.shape[0]
    tm = min(TM, T)
    a, act = _tiled(f"{tag}_ff1", (1, T // tm), [_rb(h_bf, tm), _res(w1)],
                    [_out(T, D_FF, BF16, tm, D_FF), _out(T, D_FF, BF16, tm, D_FF)],
                    _mmc_blocks(N_DEV, NN, lambda w, d: w[d], epilogue=_relu2_epilogue), direct=True)
    ff = _tiled(f"{tag}_ff2", (1, T // tm), [_rb(act, tm), _res(w2.reshape(D_FF, D_MODEL))],
                [_out(T, D_MODEL, F32, tm, D_MODEL)], _mmc(NN))
    return a, act, ff


def _mlp_bwd_w(tag, h_bf, a, act, dff_bf, w2, deps=()):
    T = h_bf.shape[0]
    tm = min(TM, T)
    da = _tiled(f"{tag}_dact", (1, T // tm), [_rb(dff_bf, tm), _res(w2), _rb(a, tm)], [_out(T, D_FF, BF16, tm, D_FF)],
                _mmc_blocks(N_DEV, NT, lambda w, d: w[d], epilogue=lambda acc, a_t: (acc * 2.0 * jnp.maximum(a_t.astype(F32), 0.0),)),
                direct=True, deps=deps)
    dw2 = _tiled(f"{tag}_dw2", (1, D_FF // TM), [_tl(act, TM), _res(dff_bf)],
                 [_out(D_FF, D_MODEL, F32, TM, D_MODEL)], _mmc(TN_)).reshape(N_DEV, D_FF // N_DEV, D_MODEL)
    dw1 = _tiled(f"{tag}_dw1", (N_DEV, 1), [_res(h_bf), _cw(da, TN)], [_out_dev(D_MODEL, TN, D_MODEL)], _mmc(TN_))
    return da, dw1, dw2


def _mlp_bwd_h(tag, da, w1, dy, deps=()):
    T = da.shape[0]
    tm = min(TM, T)
    return _tiled(f"{tag}_dh", (1, T // tm), [_rb(da, tm), _res(w1), _rb(dy, tm)],
                  [_out(T, D_MODEL, F32, tm, D_MODEL), _out(T, D_MODEL, BF16, tm, D_MODEL)],
                  _mmc_dev(epilogue=lambda acc, dy_t: (acc + ALPHA * dy_t,) * 2), deps=deps)


def _rope_tables(positions_col, invf_lane):
    def body(rows, consts, outs, accs):
        ang = rows[0][...].astype(F32) * consts[0][...]
        c, s = jnp.cos(ang), jnp.sin(ang)
        lane = lax.broadcasted_iota(jnp.int32, ang.shape, 1)
        outs[0][...] = jnp.where(lane < 64, 1.0, jnp.where(lane < 96, c, 0.0))
        outs[1][...] = jnp.where((lane >= 64) & (lane < 80), -s, 0.0)
        outs[2][...] = jnp.where((lane >= 80) & (lane < 96), s, 0.0)

    return _rowwise("rope_tables", body, [_full(positions_col)], [invf_lane], [(HEAD_W, F32)] * 3)


def _rope(x, c, s1, s2):
    return x * c + pltpu.roll(x, 112, 1) * s1 + pltpu.roll(x, 16, 1) * s2


def _rope_t(dx, c, s1, s2):
    return dx * c + pltpu.roll(dx * s1, 16, 1) + pltpu.roll(dx * s2, 112, 1)


def _rms(c):
    r = lax.rsqrt(jnp.mean(c * c, axis=-1, keepdims=True) + EPS)
    return c * r, r


def _mla_pre(zm, tabs, gq, gkv):
    def body(rows, consts, outs, accs):
        cq, _ = _rms(rows[0][...])
        ckv, _ = _rms(rows[1][...])
        outs[0][...] = (cq * consts[0][...]).astype(BF16)
        outs[1][...] = (ckv * consts[1][...]).astype(BF16)
        outs[2][...] = _rope(rows[2][...], rows[3][...], rows[4][...], rows[5][...])

    rows = [(zm, 256, 0), (zm, 256, 1), (zm, 128, 4)] + [_full(t) for t in tabs]
    return _rowwise("mla_pre", body, rows, [gq, gkv], [(256, BF16), (256, BF16), (HEAD_W, F32)])


def _mla_pre_bwd(zm, tabs, gq, gkv, dcqn, dckvn, dk):
    def body(rows, consts, outs, accs):
        res = []
        for k in range(2):
            ch, r = _rms(rows[k][...])
            d = rows[5 + k][...]
            accs[k][...] += jnp.sum(d * ch, axis=0, keepdims=True)
            dc = d * consts[k][...]
            res.append(r * (dc - ch * jnp.mean(dc * ch, axis=-1, keepdims=True)))
        dks = rows[7][:, 0:HEAD_W]
        for h in range(1, HEADS):
            dks = dks + rows[7][:, h * HEAD_W:(h + 1) * HEAD_W]
        lane = lax.broadcasted_iota(jnp.int32, dks.shape, 1)
        dks = jnp.where((lane >= 64) & (lane < 96), dks, 0.0)
        dkr = _rope_t(dks, rows[2][...], rows[3][...], rows[4][...])
        outs[0][:, 0:256] = res[0].astype(BF16)
        outs[0][:, 256:512] = res[1].astype(BF16)
        outs[0][:, 512:640] = dkr.astype(BF16)

    rows = [(zm, 256, 0), (zm, 256, 1)] + [_full(t) for t in tabs] + [_full(dcqn), _full(dckvn), _full(dk)]
    return _rowwise("mla_pre_bwd", body, rows, [gq, gkv], [(640, BF16)], [((1, 256), F32), ((1, 256), F32)])


def _rope_heads(x, c, s1, s2, fn):
    return jnp.concatenate([fn(x[:, h * HEAD_W:(h + 1) * HEAD_W], c, s1, s2) for h in range(HEADS)], axis=1)


def _unrope_heads(dq, tabs):
    def body(rows, consts, outs, accs):
        outs[0][...] = _rope_heads(rows[0][...], rows[1][...], rows[2][...], rows[3][...], _rope_t).astype(BF16)

    return _rowwise("l0_dq_rope", body, [_full(dq)] + [_full(t) for t in tabs], [], [(HEADS * HEAD_W, BF16)])


def _attn_block(T):
    return min(1024, T)


def _attn_fwd(q, k, v):
    T = q.shape[0]
    BQ = _attn_block(T)
    nq = T // BQ

    def kern(q_ref, k_ref, v_ref, o_ref, lse_ref):
        def step(i, j, carry, masked):
            m, l, acc = carry
            qb = q_ref[pl.ds(pl.multiple_of(i * BQ, BQ), BQ), :]
            kb = k_ref[pl.ds(pl.multiple_of(j * BQ, BQ), BQ), :]
            vb = v_ref[pl.ds(pl.multiple_of(j * BQ, BQ), BQ), :]
            s = _dot(qb, kb, NT) * MLA_SCALE
            if masked:
                row = lax.broadcasted_iota(jnp.int32, s.shape, 0)
                col = lax.broadcasted_iota(jnp.int32, s.shape, 1)
                s = jnp.where(col <= row, s, -1e30)
            m_new = jnp.maximum(m, jnp.max(s, axis=-1, keepdims=True))
            p = jnp.exp(s - m_new)
            a = jnp.exp(m - m_new)
            l = a * l + jnp.sum(p, axis=-1, keepdims=True)
            acc = a * acc + _dot(p.astype(BF16), vb, NN)
            return m_new, l, acc

        def qloop(i, _):
            init = (jnp.full((BQ, 1), -1e30, F32), jnp.zeros((BQ, 1), F32), jnp.zeros((BQ, HEAD_W), F32))
            carry = lax.fori_loop(0, i, lambda j, c: step(i, j, c, False), init)
            m, l, acc = step(i, i, carry, True)
            rows = pl.ds(pl.multiple_of(i * BQ, BQ), BQ)
            o_ref[rows, :] = acc / l
            lse_ref[0, rows, :] = m + jnp.log(l)
            return 0

        lax.fori_loop(0, nq, qloop, 0)

    head = pl.BlockSpec((T, HEAD_W), lambda h: (0, h))
    nbytes = 3 * _nbytes((T, HEAD_W), BF16) + _nbytes((T, HEAD_W), F32) + _nbytes((T, 128), F32)
    return pl.pallas_call(
        kern, name="attn_fwd", grid=(HEADS,), in_specs=[head, head, head],
        out_specs=[head, pl.BlockSpec((1, T, 1), lambda h: (h, 0, 0))],
        out_shape=[pltpu.HBM((T, HEADS * HEAD_W), F32), pltpu.HBM((HEADS, T, 1), F32)],
        compiler_params=pltpu.CompilerParams(dimension_semantics=("parallel",), vmem_limit_bytes=_vmem(nbytes)),
    )(_hbm(q), _hbm(k), _hbm(v))


def _attn_bwd(q, k, v, o, lse, dcat):
    T = q.shape[0]
    BQ = _attn_block(T)
    nq = T // BQ

    def kern(q_ref, k_ref, v_ref, o_ref, lse_ref, do_ref, dq_ref, dk_ref, dv_ref, dd_ref):
        dq_ref[...] = jnp.zeros(dq_ref.shape, F32)

        def dloop(i, _):
            rows = pl.ds(pl.multiple_of(i * BQ, BQ), BQ)
            dd_ref[rows, :] = jnp.sum(do_ref[rows, :].astype(F32) * o_ref[rows, :], axis=-1, keepdims=True)
            return 0

        lax.fori_loop(0, nq, dloop, 0)

        def step(j, i, carry, masked):
            dk_acc, dv_acc = carry
            rq = pl.ds(pl.multiple_of(i * BQ, BQ), BQ)
            rk = pl.ds(pl.multiple_of(j * BQ, BQ), BQ)
            qb, kb, vb, dob = q_ref[rq, :], k_ref[rk, :], v_ref[rk, :], do_ref[rq, :]
            s = _dot(qb, kb, NT) * MLA_SCALE
            p = jnp.exp(s - lse_ref[0, rq, :])
            if masked:
                row = lax.broadcasted_iota(jnp.int32, s.shape, 0)
                col = lax.broadcasted_iota(jnp.int32, s.shape, 1)
                p = jnp.where(col <= row, p, 0.0)
            dp = _dot(dob, vb, NT)
            ds = (p * (dp - dd_ref[rq, :]) * MLA_SCALE).astype(BF16)
            dv_acc = dv_acc + _dot(p.astype(BF16), dob, TN_)
            dk_acc = dk_acc + _dot(ds, qb, TN_)
            dq_ref[rq, :] += _dot(ds, kb, NN)
            return dk_acc, dv_acc

        def kloop(j, _):
            init = (jnp.zeros((BQ, HEAD_W), F32), jnp.zeros((BQ, HEAD_W), F32))
            carry = step(j, j, init, True)
            dk_acc, dv_acc = lax.fori_loop(j + 1, nq, lambda i, c: step(j, i, c, False), carry)
            rk = pl.ds(pl.multiple_of(j * BQ, BQ), BQ)
            dk_ref[rk, :] = dk_acc
            dv_ref[rk, :] = dv_acc
            return 0

        lax.fori_loop(0, nq, kloop, 0)

    head = pl.BlockSpec((T, HEAD_W), lambda h: (0, h))
    nbytes = 4 * _nbytes((T, HEAD_W), BF16) + 5 * _nbytes((T, HEAD_W), F32) + 2 * _nbytes((T, 128), F32)
    return pl.pallas_call(
        kern, name="attn_bwd", grid=(HEADS,),
        in_specs=[head, head, head, head, pl.BlockSpec((1, T, 1), lambda h: (h, 0, 0)), head],
        out_specs=[head, head, head],
        out_shape=[pltpu.HBM((T, HEADS * HEAD_W), F32)] * 3,
        scratch_shapes=[pltpu.VMEM((T, 1), F32)],
        compiler_params=pltpu.CompilerParams(dimension_semantics=("parallel",), vmem_limit_bytes=_vmem(nbytes)),
    )(*[_hbm(a) for a in (q, k, v, o, lse, dcat)])


def _sgu_common(u, v, ln_g, ln_b):
    ua, tu = _gelu(u)
    va, tv = _gelu(v)
    vh, r = _ln_stats(va)
    return ua, tu, tv, vh, r, vh * ln_g + ln_b


def _tril_mask(n):
    return lax.broadcasted_iota(jnp.int32, (n, n), 1) <= lax.broadcasted_iota(jnp.int32, (n, n), 0)


def _sgu_fwd(zs, ln_g, ln_b, w, bias_full):
    def body(rows, consts, outs, accs):
        ua, _, _, _, _, vn = _sgu_common(rows[0][...], rows[1][...], consts[0][...], consts[1][...])
        vn = vn.astype(BF16)
        tri = _tril_mask(SGU_CHUNK)
        for g in range(SGU_G):
            wg = jnp.where(tri, consts[2][0, g], 0.0).astype(BF16)
            cols = slice(g * 128, (g + 1) * 128)
            for c in range(ua.shape[0] // SGU_CHUNK):
                rws = slice(c * SGU_CHUNK, (c + 1) * SGU_CHUNK)
                mixed = _dot(wg, vn[rws, cols], NN) + consts[3][:, cols]
                outs[0][rws, cols] = (ua[rws, cols] * mixed).astype(BF16)

    return _rowwise("sgu_fwd", body, [(zs, 512, 0), (zs, 512, 1)], [ln_g, ln_b, w, bias_full], [(SGU_DIM, BF16)])


def _sgu_bwd(zs, dcat, ln_g, ln_b, w, bias_full):
    def body(rows, consts, outs, accs):
        u, v = rows[0][...], rows[1][...]
        ua, tu, tv, vh, r, vn = _sgu_common(u, v, consts[0][...], consts[1][...])
        dout = rows[2][...].astype(F32)
        vn_bf = vn.astype(BF16)
        tri = _tril_mask(SGU_CHUNK)
        dmixed = (dout * ua)
        dmixed_bf = dmixed.astype(BF16)
        ones = jnp.ones((8, SGU_CHUNK), F32)
        dvn_cols, mixed_cols = [], []
        for g in range(SGU_G):
            wg = jnp.where(tri, consts[2][0, g], 0.0).astype(BF16)
            cols = slice(g * 128, (g + 1) * 128)
            dvn_rows, mixed_rows = [], []
            dw = jnp.zeros((SGU_CHUNK, SGU_CHUNK), F32)
            dmix_sum = jnp.zeros((SGU_CHUNK, 128), F32)
            for c in range(u.shape[0] // SGU_CHUNK):
                rws = slice(c * SGU_CHUNK, (c + 1) * SGU_CHUNK)
                mixed_rows.append(_dot(wg, vn_bf[rws, cols], NN) + consts[3][:, cols])
                dvn_rows.append(_dot(wg, dmixed_bf[rws, cols], TN_))
                dw = dw + _dot(dmixed_bf[rws, cols], vn_bf[rws, cols], NT)
                dmix_sum = dmix_sum + dmixed[rws, cols]
            accs[0][g] += jnp.where(tri, dw, 0.0)
            accs[3][g:g + 1, :] += _dot(ones, dmix_sum, NT, precision=HIGHEST)[0:1, :]
            dvn_cols.append(jnp.concatenate(dvn_rows, axis=0))
            mixed_cols.append(jnp.concatenate(mixed_rows, axis=0))
        dvn = jnp.concatenate(dvn_cols, axis=1)
        mixed = jnp.concatenate(mixed_cols, axis=1)
        accs[1][...] += jnp.sum(dvn * vh, axis=0, keepdims=True)
        accs[2][...] += jnp.sum(dvn, axis=0, keepdims=True)
        dvh = dvn * consts[0][...]
        dva = r * (dvh - jnp.mean(dvh, axis=-1, keepdims=True) - vh * jnp.mean(dvh * vh, axis=-1, keepdims=True))
        outs[0][:, 0:512] = (dout * mixed * _gelu_grad(u, tu)).astype(BF16)
        outs[0][:, 512:1024] = (dva * _gelu_grad(v, tv)).astype(BF16)

    return _rowwise("sgu_bwd", body, [(zs, 512, 0), (zs, 512, 1), (dcat, 512, 2)], [ln_g, ln_b, w, bias_full], [(1024, BF16)],
                    [((SGU_G, 128, 128), F32), ((1, SGU_DIM), F32), ((1, SGU_DIM), F32), ((SGU_G, 128), F32)], tr=256)


def _lower_bound(hg_lb):
    a0, a1 = hg_lb[0:1, :], hg_lb[1:2, :]
    m = jnp.maximum(a0, a1)
    e0, e1 = jnp.exp(a0 - m), jnp.exp(a1 - m)
    s0, s1 = e0 / (e0 + e1), e1 / (e0 + e1)
    return (s0 + s1) - s0, s0, s1


def _hg_gates(qr, fr, lb):
    C = qr.shape[0]
    sq = _sig(qr)
    qf = qr * sq
    sf = _sig(fr)
    gate = lb + (1.0 - lb) * sf
    kk = 1.0 - gate
    tri = _tril_mask(C)
    b = _dot(jnp.where(tri, 1.0, 0.0), jnp.log(gate), NN, precision=HIGHEST)
    bref = b[C // 2 - 1:C // 2, :]
    bl = b[C - 1:C, :]
    e_b = jnp.exp(b)
    e_q = jnp.exp(b - bref)
    e_k = jnp.exp(bref - b)
    e_lb = jnp.exp(bl - b)
    return dict(sq=sq, qf=qf, sf=sf, gate=gate, kk=kk, tri=tri, bl=bl, e_b=e_b, e_q=e_q, e_k=e_k, e_lb=e_lb)


def _hgrn_fwd(z1, hg_lb, gnorm):
    T = z1.shape[0]
    C = min(HG_CHUNK, T)
    nc = T // C

    def kern(q_ref, f_ref, i_ref, g_ref, lb_ref, gn_ref, o_ref, hg_ref, st_ref, s_scr):
        @pl.when(pl.program_id(0) == 0)
        def _():
            s_scr[...] = jnp.zeros(s_scr.shape, F32)

        lb_all, _, _ = _lower_bound(lb_ref[...])
        st_ref[0] = s_scr[...]
        for h in range(HEADS):
            cols = slice(h * HEAD_W, (h + 1) * HEAD_W)
            t = _hg_gates(q_ref[:, cols], f_ref[:, cols], lb_all[:, cols])
            v = i_ref[:, cols]
            v_bf = v.astype(BF16)
            st = s_scr[h]
            a = jnp.where(t["tri"], _dot((t["qf"] * t["e_q"]).astype(BF16), (t["kk"] * t["e_k"]).astype(BF16), NT), 0.0)
            o = _dot(a.astype(BF16), v_bf, NN) + _dot((t["qf"] * t["e_b"]).astype(BF16), st.astype(BF16), NT)
            s_scr[h] = st * jnp.exp(t["bl"]) + _dot(v_bf, (t["kk"] * t["e_lb"]).astype(BF16), TN_)
            o_ref[:, cols] = o
            gr = g_ref[:, cols]
            r = lax.rsqrt(jnp.mean(o * o, axis=-1, keepdims=True) + EPS)
            hg_ref[:, cols] = (o * r * gn_ref[:, cols] * (gr * _sig(gr))).astype(BF16)

    seg = lambda k: pl.BlockSpec((C, D_MODEL), functools.partial(lambda n, k: (n, k), k=k))
    row = pl.BlockSpec((C, D_MODEL), lambda n: (n, 0))
    nbytes = 6 * _nbytes((C, D_MODEL), F32) + 3 * _nbytes((HEADS, 128, 128), F32)
    return pl.pallas_call(
        kern, name="hgrn_fwd", grid=(nc,),
        in_specs=[seg(0), seg(1), seg(2), seg(3), pl.BlockSpec((2, D_MODEL), lambda n: (0, 0)),
                  pl.BlockSpec((1, D_MODEL), lambda n: (0, 0))],
        out_specs=[row, row, pl.BlockSpec((1, HEADS, 128, 128), lambda n: (n, 0, 0, 0))],
        out_shape=[pltpu.HBM((T, D_MODEL), F32), pltpu.HBM((T, D_MODEL), BF16),
                   pltpu.HBM((nc, HEADS, 128, 128), F32)],
        scratch_shapes=[pltpu.VMEM((HEADS, 128, 128), F32)],
        compiler_params=pltpu.CompilerParams(dimension_semantics=("arbitrary",), vmem_limit_bytes=_vmem(nbytes)),
    )(*[_hbm(a) for a in (z1, z1, z1, z1, hg_lb, gnorm)])


def _hgrn_bwd(z1, o_pre, dhg, states, hg_lb, gnorm):
    T = z1.shape[0]
    C = min(HG_CHUNK, T)
    nc = T // C

    def kern(q_ref, f_ref, i_ref, g_ref, o_ref, dhg_ref, st_ref, lb_ref, gn_ref, dz_ref, dlb_ref, dgn_ref, ds_scr, dlb_scr):
        n = pl.program_id(0)

        @pl.when(n == 0)
        def _():
            ds_scr[...] = jnp.zeros(ds_scr.shape, F32)
            dlb_scr[...] = jnp.zeros(dlb_scr.shape, F32)
            dgn_ref[...] = jnp.zeros(dgn_ref.shape, F32)

        lb_all, s0, s1 = _lower_bound(lb_ref[...])
        for h in range(HEADS):
            cols = slice(h * HEAD_W, (h + 1) * HEAD_W)
            lb = lb_all[:, cols]
            qr, fr = q_ref[:, cols], f_ref[:, cols]
            t = _hg_gates(qr, fr, lb)
            tri = t["tri"]
            v_bf = i_ref[:, cols].astype(BF16)
            st_bf = st_ref[0, h].astype(BF16)
            dst = ds_scr[h]
            dst_bf = dst.astype(BF16)
            o = o_ref[:, cols]
            gr = g_ref[:, cols]
            sg = _sig(gr)
            sil = gr * sg
            gn = gn_ref[:, cols]
            r = lax.rsqrt(jnp.mean(o * o, axis=-1, keepdims=True) + EPS)
            on = o * r
            dh = dhg_ref[:, cols].astype(F32)
            dgn_ref[:, cols] += jnp.sum(dh * on * sil, axis=0, keepdims=True)
            dg = dh * on * gn * (sg * (1.0 + gr * (1.0 - sg)))
            don = dh * gn * sil
            do_bf = (r * (don - on * jnp.mean(don * on, axis=-1, keepdims=True))).astype(BF16)
            qe = (t["qf"] * t["e_q"]).astype(BF16)
            ke = (t["kk"] * t["e_k"]).astype(BF16)
            qb = (t["qf"] * t["e_b"]).astype(BF16)
            kh_bf = (t["kk"] * t["e_lb"]).astype(BF16)
            a_bf = jnp.where(tri, _dot(qe, ke, NT), 0.0).astype(BF16)
            da_bf = jnp.where(tri, _dot(do_bf, v_bf, NT), 0.0).astype(BF16)
            dv = _dot(a_bf, do_bf, TN_) + _dot(kh_bf, dst_bf, NT)
            dqe = _dot(da_bf, ke, NN)
            dqb = _dot(do_bf, st_bf, NN)
            dke = _dot(da_bf, qe, TN_)
            dkh = _dot(v_bf, dst_bf, NN)
            dqf = dqe * t["e_q"] + dqb * t["e_b"]
            dkk = dke * t["e_k"] + dkh * t["e_lb"]
            kh_r = kh_bf.astype(F32)
            db = qe.astype(F32) * dqe - ke.astype(F32) * dke + qb.astype(F32) * dqb - kh_r * dkh
            e_bl = jnp.exp(t["bl"])
            dbl = jnp.sum(dkh * kh_r, axis=0, keepdims=True) + e_bl * jnp.sum(st_ref[0, h] * dst, axis=0, keepdims=True)
            dlg = _dot(jnp.where(tri, 1.0, 0.0), db, TN_, precision=HIGHEST) + dbl
            ds_scr[h] = dst * e_bl + _dot(do_bf, qb, TN_)
            dgate = dlg / t["gate"] - dkk
            sf = t["sf"]
            dlb_scr[:, cols] += jnp.sum(dgate * (1.0 - sf), axis=0, keepdims=True)
            df = dgate * (1.0 - lb) * sf * (1.0 - sf)
            dq = dqf * (t["sq"] * (1.0 + qr * (1.0 - t["sq"])))
            dz_ref[:, cols] = dq.astype(BF16)
            dz_ref[:, D_MODEL + h * HEAD_W:D_MODEL + (h + 1) * HEAD_W] = df.astype(BF16)
            dz_ref[:, 2 * D_MODEL + h * HEAD_W:2 * D_MODEL + (h + 1) * HEAD_W] = dv.astype(BF16)
            dz_ref[:, 3 * D_MODEL + h * HEAD_W:3 * D_MODEL + (h + 1) * HEAD_W] = dg.astype(BF16)

        @pl.when(n == nc - 1)
        def _():
            d = s0 * s1 * dlb_scr[...]
            dlb_ref[0:1, :] = -d
            dlb_ref[1:2, :] = d

    seg = lambda k: pl.BlockSpec((C, D_MODEL), functools.partial(lambda n, k: (nc - 1 - n, k), k=k))
    nbytes = 6 * _nbytes((C, D_MODEL), F32) + _nbytes((C, 4 * D_MODEL), BF16) + 3 * _nbytes((HEADS, 128, 128), F32)
    return pl.pallas_call(
        kern, name="hgrn_bwd", grid=(nc,),
        in_specs=[seg(0), seg(1), seg(2), seg(3), seg(0), seg(0),
                  pl.BlockSpec((1, HEADS, 128, 128), lambda n: (nc - 1 - n, 0, 0, 0)),
                  pl.BlockSpec((2, D_MODEL), lambda n: (0, 0)), pl.BlockSpec((1, D_MODEL), lambda n: (0, 0))],
        out_specs=[pl.BlockSpec((C, 4 * D_MODEL), lambda n: (nc - 1 - n, 0)),
                   pl.BlockSpec((2, D_MODEL), lambda n: (0, 0)), pl.BlockSpec((1, D_MODEL), lambda n: (0, 0))],
        out_shape=[pltpu.HBM((T, 4 * D_MODEL), BF16), pltpu.HBM((2, D_MODEL), F32),
                   pltpu.HBM((1, D_MODEL), F32)],
        scratch_shapes=[pltpu.VMEM((HEADS, 128, 128), F32), pltpu.VMEM((1, D_MODEL), F32)],
        compiler_params=pltpu.CompilerParams(dimension_semantics=("arbitrary",), vmem_limit_bytes=_vmem(nbytes)),
    )(*[_hbm(a) for a in (z1, z1, z1, z1, o_pre, dhg, states, hg_lb, gnorm)])


def _prep_weights(gw):
    w_in_e = gw["w_in_e"].transpose(1, 0, 2).reshape(D_MODEL, 1568)
    kr = jnp.pad(w_in_e[:, 512:544], ((0, 0), (64, 32)))
    wm = jnp.concatenate([w_in_e[:, 0:512], kr], axis=1)
    ws = w_in_e[:, 544:1568]
    w_qb = gw["w_qb"].transpose(1, 0, 2).reshape(MLA_LORA, HEADS, 96)
    wq = jnp.pad(w_qb, ((0, 0), (0, 0), (0, 32))).reshape(MLA_LORA, HEADS * HEAD_W)
    kvb = gw["w_kvb"].transpose(1, 0, 2).reshape(MLA_LORA, HEADS, 128)
    wk = jnp.pad(kvb[:, :, :64], ((0, 0), (0, 0), (0, 64))).reshape(MLA_LORA, HEADS * HEAD_W)
    wv = jnp.pad(kvb[:, :, 64:], ((0, 0), (0, 0), (0, 64))).reshape(MLA_LORA, HEADS * HEAD_W)
    w_out_e = gw["w_out_e"].reshape(D_MODEL, D_MODEL)
    woa = jnp.pad(w_out_e[:512].reshape(HEADS, 64, D_MODEL), ((0, 0), (0, 64), (0, 0))).reshape(HEADS * HEAD_W, D_MODEL)
    return dict(wm=wm, ws=ws, wq=wq, wk=wk, wv=wv, woa=woa, wob=w_out_e[512:])


def _unprep_grads(g):
    dwm, dws = g["wm"], g["ws"]
    d_in_e = jnp.concatenate([dwm[:, 0:512], dwm[:, 512 + 64:512 + 96], dws], axis=1)
    d_qb = g["wq"].reshape(MLA_LORA, HEADS, HEAD_W)[:, :, :96].reshape(MLA_LORA, HEADS * 96)
    dk = g["wk"].reshape(MLA_LORA, HEADS, HEAD_W)[:, :, :64]
    dv = g["wv"].reshape(MLA_LORA, HEADS, HEAD_W)[:, :, :64]
    d_kvb = jnp.concatenate([dk, dv], axis=2).reshape(MLA_LORA, HEADS * 128)
    d_oa = g["woa"].reshape(HEADS, HEAD_W, D_MODEL)[:, :64].reshape(HEADS * 64, D_MODEL)
    dev_major = lambda a: a.reshape(a.shape[0], N_DEV, a.shape[1] // N_DEV).transpose(1, 0, 2)
    return dict(w_in_e=dev_major(d_in_e), w_qb=dev_major(d_qb), w_kvb=dev_major(d_kvb),
                w_out_e=jnp.concatenate([d_oa, g["wob"]], axis=0).reshape(N_DEV, D_MODEL // N_DEV, D_MODEL))


def _local_step(x, positions, target, gw, sp, ex):
    w = _prep_weights(gw)
    T = x.shape[0]
    tm = min(TM, T)
    nt = T // tm
    half = MLA_ROPE // 2
    inv_freq = ROPE_BASE ** (-jnp.arange(half, dtype=F32) / half)
    invf_lane = jnp.concatenate([jnp.zeros((64,), F32), inv_freq, inv_freq, jnp.zeros((32,), F32)]).reshape(1, HEAD_W)
    tabs = _rope_tables(positions.reshape(T, 1), invf_lane)
    bias_full = jnp.repeat(sp["sgu_b"][0].T, 128, axis=1)
    sgu_w = sp["sgu_w"]
    gq, gkv = sp["mla_gq"], sp["mla_gkv"]
    ln1_g, ln1_b, ln2_g, ln2_b = sp["ln1_g"], sp["ln1_b"], sp["ln2_g"], sp["ln2_b"]
    wide = HEADS * HEAD_W
    tab_rows = [_rb(t, tm) for t in tabs]
    resid = lambda acc, d: (acc + ALPHA * d,)

    zm = _tiled("l0_in_mla", (1, nt), [_rb(x, tm), _cw(w["wm"], 640)], [_out(T, 640, F32, tm, 640)], _mmc(NN))
    zs = _tiled("l0_in_sgu", (2, nt), [_rb(x, tm), _cw(w["ws"], TN)], [_out(T, 1024, F32, tm, TN)], _mmc(NN))
    cqn, ckvn, kr_rot = _mla_pre(zm, tabs, gq, gkv)
    q = _tiled("l0_q", (1, nt), [_rb(cqn, tm), _cw(w["wq"], wide)] + tab_rows, [_out(T, wide, BF16, tm, wide)],
               _mmc(NN, epilogue=lambda acc, c, s1, s2: (_rope_heads(acc, c, s1, s2, _rope),)))
    k = _tiled("l0_k", (1, nt), [_rb(ckvn, tm), _cw(w["wk"], wide), _rb(kr_rot, tm)], [_out(T, wide, BF16, tm, wide)],
               _mmc(NN, epilogue=lambda acc, kr: (acc + jnp.concatenate([kr] * HEADS, axis=1),)))
    v = _tiled("l0_v", (1, nt), [_rb(ckvn, tm), _cw(w["wv"], wide)], [_out(T, wide, BF16, tm, wide)], _mmc(NN))
    o_att, lse = _attn_fwd(q, k, v)
    b_out = _sgu_fwd(zs, sp["sgu_ln_g"], sp["sgu_ln_b"], sgu_w, bias_full)
    mix0 = _tiled("l0_out", (2, nt), [_rb(o_att, tm), _cw(w["woa"], TN), _rb(b_out, tm), _cw(w["wob"], TN)],
                  [_out(T, D_MODEL, F32, tm, TN)], _mmc(NN, n_pairs=2))
    y1, h1, h1_bf = _ln_fwd("l0_ln1", x, mix0, ln1_g, ln1_b, 0)
    big = ex.weights_ready(after=y1)
    w_ff1, w_ff2, w_in_o, w_out_o = big["w_ff1"], big["w_ff2"], big["w_in_o"], big["w_out_o"].reshape(D_MODEL, D_MODEL)
    a0, act0, ff0 = _mlp_fwd("l0", h1_bf, w_ff1[0], w_ff2[0])
    y2, h2, h2_bf = _ln_fwd("l0_ln2", h1, ff0, ln2_g, ln2_b, 0)

    z1 = _tiled("l1_in", (1, nt), [_rb(h2_bf, tm), _res(w_in_o)], [_out(T, 4 * D_MODEL, F32, tm, 4 * D_MODEL)],
                _mmc_blocks(N_DEV, NN, lambda w, d: w[d]), direct=True)
    o_pre, hg, states = _hgrn_fwd(z1, sp["hg_lb"], sp["hg_gnorm"])
    mix1 = _tiled("l1_out", (2, nt), [_rb(hg, tm), _cw(w_out_o, TN)], [_out(T, D_MODEL, F32, tm, TN)], _mmc(NN))
    y3, h3, h3_bf = _ln_fwd("l1_ln1", h2, mix1, ln1_g, ln1_b, 1)
    a1, act1, ff1 = _mlp_fwd("l1", h3_bf, w_ff1[1], w_ff2[1])
    y4, dh4, sq_err = _ln_loss("l1_ln2", h3, ff1, ln2_g, ln2_b, 1, target)

    gs, g0 = {}, {}
    dy4, dy4_bf, gs["ln2_g1"], gs["ln2_b1"] = _ln_bwd("l1_ln2_bwd", y4, dh4, ln2_g, 1)
    da1, dw1_1, dw2_1 = _mlp_bwd_w("l1", h3_bf, a1, act1, dy4_bf, w_ff2[1])
    dh3, _ = _mlp_bwd_h("l1", da1, w_ff1[1], dy4)
    dy3, dy3_bf, gs["ln1_g1"], gs["ln1_b1"] = _ln_bwd("l1_ln1_bwd", y3, dh3, ln1_g, 1)
    d_out_o = _tiled("l1_dwout", (2, D_MODEL // TM), [_tl(hg, TM), _cw(dy3_bf, TN)], [_out(D_MODEL, D_MODEL, F32, TM, TN)],
                     _mmc(TN_)).reshape(N_DEV, D_MODEL // N_DEV, D_MODEL)
    dhg = _tiled("l1_dhg", (2, nt), [_rb(dy3_bf, tm), _rw(w_out_o, TN)], [_out(T, D_MODEL, BF16, tm, TN)], _mmc(NT))
    dz1, gs["hg_lb"], gs["hg_gnorm"] = _hgrn_bwd(z1, o_pre, dhg, states, sp["hg_lb"], sp["hg_gnorm"])
    d_in_o = _tiled("l1_dwin", (N_DEV, 1), [_res(h2_bf), _cw(dz1, TN)], [_out_dev(D_MODEL, TN, D_MODEL)], _mmc(TN_))
    token = ex.grads_start("l1", [dw1_1, dw2_1, d_in_o, d_out_o])
    dh2 = _tiled("l1_dh2", (1, nt), [_rb(dz1, tm), _res(w_in_o), _rb(dy3, tm)], [_out(T, D_MODEL, F32, tm, D_MODEL)],
                 _mmc_dev(epilogue=resid), deps=[token])

    dy2, dy2_bf, gs["ln2_g0"], gs["ln2_b0"] = _ln_bwd("l0_ln2_bwd", y2, dh2, ln2_g, 0)
    token = ex.grads_middle("l1", after=dy2)
    da0, dw1_0, dw2_0 = _mlp_bwd_w("l0", h1_bf, a0, act0, dy2_bf, w_ff2[0], deps=[token])
    token = ex.grads_start("l0m", [dw1_0, dw2_0])
    dh1, _ = _mlp_bwd_h("l0", da0, w_ff1[0], dy2, deps=[token])
    ex.grads_end("l1", after=dh1)
    dy1, dy1_bf, gs["ln1_g0"], gs["ln1_b0"] = _ln_bwd("l0_ln1_bwd", y1, dh1, ln1_g, 0)
    token = ex.grads_middle("l0m", after=dy1)
    g0["woa"] = _tiled("l0_dwoa", (2, wide // TM), [_tl(o_att, TM), _cw(dy1_bf, TN)], [_out(wide, D_MODEL, F32, TM, TN)], _mmc(TN_))
    g0["wob"] = _tiled("l0_dwob", (2, 1), [_tl(b_out, SGU_DIM), _cw(dy1_bf, TN)], [_out(SGU_DIM, D_MODEL, F32, SGU_DIM, TN)], _mmc(TN_))
    wo_cat = jnp.concatenate([w["woa"], w["wob"]], axis=0)
    dcat = _tiled("l0_dcat", (3, nt), [_rb(dy1_bf, tm), _rw(wo_cat, TN)], [_out(T, wide + SGU_DIM, BF16, tm, TN)], _mmc(NT), deps=[token])
    dzs, gs["sgu_w"], gs["sgu_ln_g"], gs["sgu_ln_b"], gs["sgu_b"] = _sgu_bwd(zs, dcat, sp["sgu_ln_g"], sp["sgu_ln_b"], sgu_w, bias_full)
    dq, dk, dv = _attn_bwd(q, k, v, o_att, lse, dcat)
    ex.grads_end("l0m", after=dq)
    dq_pre = _unrope_heads(dq, tabs)
    lora_w = lambda name, a, d: _tiled(name, (wide // TN, 1), [_tl(a, MLA_LORA), _cw(d, TN)], [_out(MLA_LORA, wide, F32, MLA_LORA, TN)], _mmc(TN_))
    g0["wq"] = lora_w("l0_dwq", cqn, dq_pre)
    g0["wk"] = lora_w("l0_dwk", ckvn, dk)
    g0["wv"] = lora_w("l0_dwv", ckvn, dv)
    dcqn = _tiled("l0_dcqn", (1, nt), [_rb(dq_pre, tm), _rw(w["wq"], MLA_LORA)], [_out(T, MLA_LORA, F32, tm, MLA_LORA)], _mmc(NT))
    dckvn = _tiled("l0_dckvn", (1, nt), [_rb(dk, tm), _rw(w["wk"], MLA_LORA), _rb(dv, tm), _rw(w["wv"], MLA_LORA)],
                   [_out(T, MLA_LORA, F32, tm, MLA_LORA)], _mmc(NT, n_pairs=2))
    dzm, gs["mla_gq"], gs["mla_gkv"] = _mla_pre_bwd(zm, tabs, gq, gkv, dcqn, dckvn, dk)
    g0["wm"] = _tiled("l0_dwm", (1, D_MODEL // TM), [_tl(x, TM), _cw(dzm, 640)], [_out(D_MODEL, 640, F32, TM, 640)], _mmc(TN_))
    g0["ws"] = _tiled("l0_dws", (2, D_MODEL // TM), [_tl(x, TM), _cw(dzs, TN)], [_out(D_MODEL, 1024, F32, TM, TN)], _mmc(TN_))
    dx = _tiled("l0_dx", (2, nt), [_rb(dzm, tm), _rw(w["wm"], TN), _rb(dzs, tm), _rw(w["ws"], TN), _rbj(dy1, tm, TN)],
                [_out(T, D_MODEL, F32, tm, TN)], _mmc(NT, n_pairs=2, epilogue=resid))

    return sq_err, dx, _unprep_grads(g0), gs


def _me():
    return lax.axis_index("x"), lax.axis_index("y"), lax.axis_index("c")


def _hbm_call(name, kern, operands, out_shape, n_sems, extra_scratch=()):
    any_spec = pl.BlockSpec(memory_space=pl.ANY)
    return pl.pallas_call(
        kern, name=name, out_shape=out_shape, in_specs=[any_spec] * len(operands), out_specs=[any_spec] * len(out_shape),
        scratch_shapes=[pltpu.SemaphoreType.DMA((n_sems,)), pltpu.SemaphoreType.DMA((n_sems,)), *extra_scratch],
    )(*[_hbm(a) for a in operands])


ANY_SPEC = pl.BlockSpec(memory_space=pl.ANY)
HBM_SPEC = pl.BlockSpec(memory_space=pltpu.HBM)
SEM_SPEC = pl.BlockSpec(memory_space=pltpu.SEMAPHORE)
EFFECT = pltpu.SideEffectType.DATAFLOW_SIDE_EFFECTING


def _split_start(name, srcs, land_shapes, n_sems, make_copies):
    n, m = len(srcs), len(land_shapes)

    def body(*refs):
        for cp in make_copies(refs[:n], refs[n:n + m], refs[n + m], refs[n + m + 1]):
            cp.start()
        refs[-1][...] = jnp.zeros(refs[-1].shape, F32)

    lands = [lax.empty(shape, dtype) for shape, dtype in land_shapes]
    out_shape = (pltpu.SemaphoreType.DMA((n_sems,)), pltpu.SemaphoreType.DMA((n_sems,)),
                 *[pltpu.HBM(a.shape, a.dtype) for a in srcs], *[pltpu.HBM(shape, dtype) for shape, dtype in land_shapes],
                 jax.ShapeDtypeStruct((8, 128), F32))
    res = pl.pallas_call(
        body, name=name, out_shape=out_shape, in_specs=[HBM_SPEC] * (n + m),
        out_specs=(SEM_SPEC, SEM_SPEC, *[HBM_SPEC] * (n + m), pl.BlockSpec(memory_space=pltpu.VMEM)),
        input_output_aliases={i: 2 + i for i in range(n + m)},
        compiler_params=pltpu.CompilerParams(has_side_effects=EFFECT),
    )(*[_hbm(a) for a in (*srcs, *lands)])
    return res[0], res[1], list(res[2:2 + n]), list(res[2 + n:2 + n + m]), res[-1]


def _split_wait(name, send_sems, recv_sems, srcs, lands, after, make_copies):
    n, m = len(srcs), len(lands)

    def body(*refs):
        for cp in make_copies(refs[:n], refs[n:n + m], refs[n + m], refs[n + m + 1]):
            cp.wait_send()
            cp.wait_recv()

    res = pl.pallas_call(
        body, name=name, out_shape=tuple(pltpu.HBM(a.shape, a.dtype) for a in (*srcs, *lands)),
        in_specs=[HBM_SPEC] * (n + m) + [SEM_SPEC, SEM_SPEC] + [ANY_SPEC] * len(after), out_specs=tuple([HBM_SPEC] * (n + m)),
        input_output_aliases={i: i for i in range(n + m)},
        compiler_params=pltpu.CompilerParams(has_side_effects=EFFECT),
    )(*srcs, *lands, send_sems, recv_sems, *after)
    return list(res[:n]), list(res[n:])


def _ag_first_copies(x_refs, out_refs, send_sems, recv_sems):
    x, y, c = _me()
    targets = [(x, y, 1 - c), (1 - x, y, c), (x, 1 - y, c), (1 - x, 1 - y, c)]
    return [pltpu.make_async_remote_copy(
        src_ref=x_refs[op], dst_ref=out_refs[op].at[4 * x + 2 * y + c], send_sem=send_sems.at[4 * op + k],
        recv_sem=recv_sems.at[4 * op + k], device_id=to, device_id_type=MESH)
        for op in range(len(x_refs)) for k, to in enumerate(targets)]


def _ag_second(shards, gathered):
    n = len(shards)

    def kern(*refs):
        x_refs, in_refs, out_refs, (send_sems, recv_sems, local_sems) = refs[:n], refs[n:2 * n], refs[2 * n:3 * n], refs[3 * n:]
        x, y, c = _me()
        chips = [(1 - x, y), (x, 1 - y), (1 - x, 1 - y)]
        mine = [pltpu.make_async_copy(x_refs[op], out_refs[op].at[4 * x + 2 * y + c], local_sems.at[op]) for op in range(n)]
        passed = [pltpu.make_async_remote_copy(
            src_ref=in_refs[op].at[4 * cx + 2 * cy + c], dst_ref=out_refs[op].at[4 * cx + 2 * cy + c],
            send_sem=send_sems.at[3 * op + j], recv_sem=recv_sems.at[3 * op + j], device_id=(x, y, 1 - c), device_id_type=MESH)
            for op in range(n) for j, (cx, cy) in enumerate(chips)]
        for cp in mine + passed:
            cp.start()
        for cp in passed:
            cp.wait_send()
        for op in range(n):
            for j, (cx, cy) in enumerate(chips):
                slot = out_refs[op].at[4 * cx + 2 * cy + 1 - c]
                pltpu.make_async_remote_copy(src_ref=slot, dst_ref=slot, send_sem=send_sems.at[3 * op + j],
                                             recv_sem=recv_sems.at[3 * op + j], device_id=(x, y, c), device_id_type=MESH).wait_recv()
        for cp in mine:
            cp.wait()

    return pl.pallas_call(
        kern, name="weights_all_gather_second", out_shape=[pltpu.HBM(g.shape, g.dtype) for g in gathered],
        in_specs=[ANY_SPEC] * (2 * n), out_specs=[ANY_SPEC] * n, input_output_aliases={n + i: i for i in range(n)},
        scratch_shapes=[pltpu.SemaphoreType.DMA((3 * n,)), pltpu.SemaphoreType.DMA((3 * n,)), pltpu.SemaphoreType.DMA((n,))],
    )(*[_hbm(a) for a in (*shards, *gathered)])


def _rs_sibling_copies(g_refs, out_refs, send_sems, recv_sems):
    x, y, c = _me()
    return [pltpu.make_async_remote_copy(
        src_ref=g_refs[op].at[k, 1 - c], dst_ref=out_refs[op].at[k], send_sem=send_sems.at[4 * op + k],
        recv_sem=recv_sems.at[4 * op + k], device_id=(x, y, 1 - c), device_id_type=MESH)
        for op in range(len(g_refs)) for k in range(4)]


def _rs_chip_copies(p_refs, out_refs, send_sems, recv_sems):
    x, y, c = _me()
    chips = [(1 - x, y), (x, 1 - y), (1 - x, 1 - y)]
    return [pltpu.make_async_remote_copy(
        src_ref=p_refs[op].at[2 * cx + cy], dst_ref=out_refs[op].at[j], send_sem=send_sems.at[3 * op + j],
        recv_sem=recv_sems.at[3 * op + j], device_id=(cx, cy, c), device_id_type=MESH)
        for op in range(len(p_refs)) for j, (cx, cy) in enumerate(chips)]


def _all_gather(shards, deps=()):
    n, deps = len(shards), _deps(deps)
    first_out = n + len(deps)

    def kern(*refs):
        x_refs, out_refs, (send_sems, recv_sems, local_sems) = refs[:n], refs[first_out:first_out + n], refs[first_out + n:]
        x, y, c = _me()
        me, sibling = (x, y, c), (x, y, 1 - c)
        chips = [(1 - x, y), (x, 1 - y), (1 - x, 1 - y)]

        def copy(op, k, block, to, own=False):
            slot = out_refs[op].at[4 * block[0] + 2 * block[1] + block[2]]
            return pltpu.make_async_remote_copy(
                src_ref=x_refs[op] if own else slot, dst_ref=slot, send_sem=send_sems.at[7 * op + k],
                recv_sem=recv_sems.at[7 * op + k], device_id=to, device_id_type=MESH)

        mine = [pltpu.make_async_copy(x_refs[op], out_refs[op].at[4 * x + 2 * y + c], local_sems.at[op]) for op in range(n)]
        for cp in mine:
            cp.start()
        first = []
        for op in range(n):
            first.append(copy(op, 0, me, sibling, own=True))
            first += [copy(op, 1 + j, me, (*chip, c), own=True) for j, chip in enumerate(chips)]
        for cp in first:
            cp.start()
        passed = []
        for j, chip in enumerate(chips):
            for op in range(n):
                copy(op, 1 + j, (*chip, c), me).wait_recv()
                passed.append(copy(op, 4 + j, (*chip, c), sibling))
                passed[-1].start()
        for op in range(n):
            copy(op, 0, sibling, me).wait_recv()
            for j, chip in enumerate(chips):
                copy(op, 4 + j, (*chip, 1 - c), me).wait_recv()
        for cp in first + passed:
            cp.wait_send()
        for cp in mine:
            cp.wait()

    out_shape = [pltpu.HBM((N_DEV, *s.shape), s.dtype) for s in shards]
    return _hbm_call("weights_all_gather", kern, [*shards, *deps], out_shape, 7 * n, [pltpu.SemaphoreType.DMA((n,))])


def _rs_sibling(grads):
    n = len(grads)

    def kern(*refs):
        g_refs, out_refs, (send_sems, recv_sems) = refs[:n], refs[n:2 * n], refs[2 * n:]
        x, y, c = _me()
        copies = [pltpu.make_async_remote_copy(
            src_ref=g_refs[op].at[k, 1 - c], dst_ref=out_refs[op].at[k], send_sem=send_sems.at[4 * op + k],
            recv_sem=recv_sems.at[4 * op + k], device_id=(x, y, 1 - c), device_id_type=MESH) for op in range(n) for k in range(4)]
        for cp in copies:
            cp.start()
        for cp in copies:
            cp.wait()

    out_shape = [pltpu.HBM((4, *g.shape[2:]), g.dtype) for g in grads]
    return _hbm_call("grads_to_sibling", kern, grads, out_shape, 4 * n)


def _rs_chips(sums):
    n = len(sums)

    def kern(*refs):
        p_refs, out_refs, (send_sems, recv_sems) = refs[:n], refs[n:2 * n], refs[2 * n:]
        x, y, c = _me()
        chips = [(1 - x, y), (x, 1 - y), (1 - x, 1 - y)]
        copies = [pltpu.make_async_remote_copy(
            src_ref=p_refs[op].at[2 * cx + cy], dst_ref=out_refs[op].at[j], send_sem=send_sems.at[3 * op + j],
            recv_sem=recv_sems.at[3 * op + j], device_id=(cx, cy, c), device_id_type=MESH)
            for op in range(n) for j, (cx, cy) in enumerate(chips)]
        for cp in copies:
            cp.start()
        for cp in copies:
            cp.wait()

    out_shape = [pltpu.HBM((3, *p.shape[1:]), p.dtype) for p in sums]
    return _hbm_call("grads_between_chips", kern, sums, out_shape, 3 * n)


def _row_tile(r):
    return r if r <= 256 else 256


def _chip_sum(name, g, from_sibling, core):
    _, _, R, W = g.shape
    tr = _row_tile(R)

    def kern(core_ref, g_ref, s_ref, o_ref):
        o_ref[...] = (g_ref[...] + s_ref[...]).astype(BF16)

    return pl.pallas_call(
        kern, name=name, out_shape=pltpu.HBM((4, R, W), BF16),
        grid_spec=pltpu.PrefetchScalarGridSpec(
            num_scalar_prefetch=1, grid=(4, R // tr),
            in_specs=[pl.BlockSpec((None, None, tr, W), lambda k, i, core: (k, core[0], i, 0)),
                      pl.BlockSpec((None, tr, W), lambda k, i, core: (k, i, 0))],
            out_specs=pl.BlockSpec((None, tr, W), lambda k, i, core: (k, i, 0))),
        compiler_params=pltpu.CompilerParams(dimension_semantics=("parallel", "parallel"), vmem_limit_bytes=_vmem(3 * tr * W * 4)),
    )(core, _hbm(g), _hbm(from_sibling))


def _adamw(w, g, m, v):
    m = ADAM_B1 * m + (1.0 - ADAM_B1) * g
    v = ADAM_B2 * v + (1.0 - ADAM_B2) * (g * g)
    m_hat = m / (1.0 - ADAM_B1 ** ADAM_STEP)
    v_hat = v / (1.0 - ADAM_B2 ** ADAM_STEP)
    return -ADAM_LR * (m_hat / (jnp.sqrt(v_hat) + ADAM_EPS) + ADAM_WD * w), m, v


def _finish_sharded(name, layers, w, m, v, where):
    nl, R, W = w.shape
    tr = _row_tile(R)

    def kern(where_ref, *refs):
        w_ref, m_ref, v_ref, go_ref, d_ref, mo_ref, vo_ref = refs[3 * nl:]
        for l in range(nl):
            g_ref, s_ref, c_ref = refs[3 * l:3 * l + 3]
            grad = g_ref[...] + s_ref[...]
            for j in range(3):
                grad = grad + c_ref[j].astype(F32)
            go_ref[l] = grad
            d_ref[l], mo_ref[l], vo_ref[l] = _adamw(w_ref[l], grad, m_ref[l], v_ref[l])

    row = pl.BlockSpec((nl, tr, W), lambda i, wh: (0, i, 0))
    in_specs, args = [], []
    for g, s, c in layers:
        in_specs += [pl.BlockSpec((None, None, tr, W), lambda i, wh: (wh[0], wh[1], i, 0)),
                     pl.BlockSpec((None, tr, W), lambda i, wh: (wh[0], i, 0)),
                     pl.BlockSpec((3, tr, W), lambda i, wh: (0, i, 0))]
        args += [g, s, c]
    return pl.pallas_call(
        kern, name=name, out_shape=[pltpu.HBM((nl, R, W), F32)] * 4,
        grid_spec=pltpu.PrefetchScalarGridSpec(num_scalar_prefetch=1, grid=(R // tr,), in_specs=in_specs + [row, row, row],
                                               out_specs=[row, row, row, row]),
        compiler_params=pltpu.CompilerParams(dimension_semantics=("parallel",), vmem_limit_bytes=_vmem(nl * 11 * tr * W * 4)),
    )(where, *[_hbm(a) for a in (*args, w, m, v)])


SMALL_PLACE = (("mla_gq", 0, 0, 1, 256), ("mla_gkv", 0, 256, 1, 256), ("sgu_ln_g", 0, 512, 1, 512), ("sgu_ln_b", 1, 0, 1, 512),
               ("hg_lb", 2, 0, 2, 1024), ("ln1_g", 4, 0, 2, 1024), ("ln1_b", 6, 0, 2, 1024), ("sgu_b", 8, 0, 4, 128),
               ("ln2_g", 12, 0, 2, 1024), ("ln2_b", 14, 0, 2, 1024), ("hg_gnorm", 16, 0, 1, 1024))
SMALL_BUF_ROWS = 24


def _small_reduce_adamw(gs, given):
    pieces = [(gs["mla_gq"], 0, 0), (gs["mla_gkv"], 0, 256), (gs["sgu_ln_g"], 0, 512), (gs["sgu_ln_b"], 1, 0), (gs["hg_lb"], 2, 0),
              (gs["ln1_g0"], 4, 0), (gs["ln1_g1"], 5, 0), (gs["ln1_b0"], 6, 0), (gs["ln1_b1"], 7, 0), (gs["sgu_b"], 8, 0),
              (gs["ln2_g0"], 12, 0), (gs["ln2_g1"], 13, 0), (gs["ln2_b0"], 14, 0), (gs["ln2_b1"], 15, 0), (gs["hg_gnorm"], 16, 0)]
    names = [p[0] for p in SMALL_PLACE] + ["sgu_w"]
    n_p, n_names = len(pieces), len(names)
    wmv = [given[pre + name] for name in names for pre in ("", "m_", "v_")]

    def kern(*refs):
        piece_refs, gw_ref = refs[:n_p], refs[n_p]
        wmv_refs = refs[n_p + 1:n_p + 1 + 3 * n_names]
        out_refs = refs[n_p + 1 + 3 * n_names:n_p + 1 + 7 * n_names]
        buf_a, buf_b, send_sems, recv_sems = refs[n_p + 1 + 7 * n_names:]
        px, py, pc = _me()
        me = 4 * px + 2 * py + pc
        mine_a, mine_b = buf_a.at[me], buf_b.at[me]
        mine_a[...] = jnp.zeros(mine_a.shape, F32)
        for ref, (_, r, l0) in zip(piece_refs, pieces):
            mine_a[r:r + ref.shape[0], l0:l0 + ref.shape[1]] = ref[...]
        mine_b[...] = gw_ref[...]
        copies = []
        for r in range(1, N_DEV):
            peer = (px ^ (r >> 2), py ^ ((r >> 1) & 1), pc ^ (r & 1))
            for k, mine in enumerate((mine_a, mine_b)):
                copies.append(pltpu.make_async_remote_copy(
                    src_ref=mine, dst_ref=mine, send_sem=send_sems.at[2 * (r - 1) + k], recv_sem=recv_sems.at[2 * (r - 1) + k],
                    device_id=peer, device_id_type=MESH))
        for cp in copies:
            cp.start()
        for r in range(1, N_DEV):
            for k, buf in enumerate((buf_a, buf_b)):
                theirs = buf.at[me ^ r]
                pltpu.make_async_remote_copy(
                    src_ref=theirs, dst_ref=theirs, send_sem=send_sems.at[2 * (r - 1) + k], recv_sem=recv_sems.at[2 * (r - 1) + k],
                    device_id=(px, py, pc), device_id_type=MESH).wait_recv()
        for cp in copies:
            cp.wait_send()
        sum_a, sum_b = buf_a[0], buf_b[0]
        for d in range(1, N_DEV):
            sum_a, sum_b = sum_a + buf_a[d], sum_b + buf_b[d]

        def own_block(full):
            acc = full[:, 0:128]
            for b in range(1, N_DEV):
                acc = jnp.where(me == b, full[:, b * 128:(b + 1) * 128], acc)
            return acc

        for idx, name in enumerate(names):
            w_ref, m_ref, v_ref = wmv_refs[3 * idx:3 * idx + 3]
            if name == "sgu_w":
                grad = sum_b[None]
            else:
                _, r, l0, nr, nl = SMALL_PLACE[idx]
                grad = sum_a[r:r + nr, l0:l0 + nl]
                if name == "hg_gnorm":
                    grad = own_block(grad)
                if name == "sgu_b":
                    grad = grad[None]
            res = (grad, *_adamw(w_ref[...], grad, m_ref[...], v_ref[...]))
            for o_ref, val in zip(out_refs[4 * idx:4 * idx + 4], res):
                o_ref[...] = val

    vmem = pl.BlockSpec(memory_space=pltpu.VMEM)
    operands = [p[0] for p in pieces] + [gs["sgu_w"]] + wmv
    out_shape = [jax.ShapeDtypeStruct(given[name].shape, F32) for name in names for _ in range(4)]
    res = pl.pallas_call(
        kern, name="small_all_reduce_adamw", out_shape=out_shape, in_specs=[vmem] * len(operands), out_specs=[vmem] * len(out_shape),
        scratch_shapes=[pltpu.VMEM((N_DEV, SMALL_BUF_ROWS, D_MODEL), F32), pltpu.VMEM((N_DEV, SGU_G, 128, 128), F32),
                        pltpu.SemaphoreType.DMA((14,)), pltpu.SemaphoreType.DMA((14,))],
    )(*operands)
    return {name: res[4 * idx:4 * idx + 4] for idx, name in enumerate(names)}


class _Exchange:
    def __init__(self, given):
        self.given = given
        px, py, pc = _me()
        self.core = pc.reshape(1).astype(jnp.int32)
        self.where = jnp.stack([2 * px + py, pc]).astype(jnp.int32)
        self.state, self.layers = {}, {}

    def start_weights(self):
        g = self.given
        shards = [a.astype(BF16) for a in (g["w_in_o"][0], g["w_out_o"][0], g["w_ff1"][0], g["w_ff1"][1], g["w_ff2"][0], g["w_ff2"][1])]
        lands = [((N_DEV, *s.shape), BF16) for s in shards]
        self.weights = _split_start("weights_first_start", shards, lands, 4 * len(shards), _ag_first_copies)
        return self.weights[4]

    def weights_ready(self, after):
        send_sems, recv_sems, shards, lands, _ = self.weights
        shards, lands = _split_wait("weights_first_wait", send_sems, recv_sems, shards, lands, [after], _ag_first_copies)
        got = _ag_second(shards, lands)
        return dict(w_in_o=got[0], w_out_o=got[1], w_ff1=[got[2], got[3]], w_ff2=[got[4], got[5]])

    def grads_start(self, tag, grads):
        blocks = [g.reshape(4, 2, *g.shape[1:]) for g in grads]
        lands = [((4, *b.shape[2:]), F32) for b in blocks]
        self.state[tag] = _split_start(f"grads_{tag}_sibling_start", blocks, lands, 4 * len(blocks), _rs_sibling_copies)
        return self.state[tag][4]

    def grads_middle(self, tag, after):
        send_sems, recv_sems, blocks, lands, _ = self.state[tag]
        blocks, from_sibling = _split_wait(f"grads_{tag}_sibling_wait", send_sems, recv_sems, blocks, lands, [after], _rs_sibling_copies)
        sums = [_chip_sum(f"grads_{tag}_chip_sum_{k}", b, s, self.core) for k, (b, s) in enumerate(zip(blocks, from_sibling))]
        lands = [((3, *p.shape[1:]), BF16) for p in sums]
        self.state[tag] = (blocks, from_sibling, _split_start(f"grads_{tag}_chips_start", sums, lands, 3 * len(sums), _rs_chip_copies))
        return self.state[tag][2][4]

    def grads_end(self, tag, after):
        blocks, from_sibling, (send_sems, recv_sems, sums, lands, _) = self.state[tag]
        _, from_chips = _split_wait(f"grads_{tag}_chips_wait", send_sems, recv_sems, sums, lands, [after], _rs_chip_copies)
        self.layers[tag] = list(zip(blocks, from_sibling, from_chips))


SHARDED = ("w_in_e", "w_qb", "w_kvb", "w_out_e", "w_in_o", "w_out_o", "w_ff1", "w_ff2")


def kernel(x, positions, w_in_e, mla_gq, mla_gkv, w_qb, w_kvb, sgu_ln_g, sgu_ln_b, sgu_w, sgu_b, w_out_e, w_in_o, hg_lb, hg_gnorm, w_out_o, ln1_g, ln1_b, w_ff1, w_ff2, ln2_g, ln2_b, loss_target, m_w_in_e, m_mla_gq, m_mla_gkv, m_w_qb, m_w_kvb, m_sgu_ln_g, m_sgu_ln_b, m_sgu_w, m_sgu_b, m_w_out_e, m_w_in_o, m_hg_lb, m_hg_gnorm, m_w_out_o, m_ln1_g, m_ln1_b, m_w_ff1, m_w_ff2, m_ln2_g, m_ln2_b, v_w_in_e, v_mla_gq, v_mla_gkv, v_w_qb, v_w_kvb, v_sgu_ln_g, v_sgu_ln_b, v_sgu_w, v_sgu_b, v_w_out_e, v_w_in_o, v_hg_lb, v_hg_gnorm, v_w_out_o, v_ln1_g, v_ln1_b, v_w_ff1, v_w_ff2, v_ln2_g, v_ln2_b):
    given = dict(locals())
    ex = _Exchange(given)

    first_token = ex.start_weights()
    names = ["w_in_e", "w_qb", "w_kvb", "w_out_e"]
    got = _all_gather([given[n][0].astype(BF16) for n in names] + [hg_gnorm], deps=[first_token])
    gw = dict(zip(names, got[:4]))
    small_names = ["mla_gq", "mla_gkv", "sgu_ln_g", "sgu_ln_b", "sgu_w", "sgu_b", "hg_lb", "ln1_g", "ln1_b", "ln2_g", "ln2_b"]
    sp = {n: given[n] for n in small_names}
    sp["hg_gnorm"] = got[4].reshape(1, D_MODEL)

    sq_err, dx, grads, gs = _local_step(x[0], positions[0], loss_target[0], gw, sp, ex)
    loss = lax.psum(0.5 * jnp.sum(sq_err) / D_MODEL, ("x", "y", "c"))

    blocks = [grads[n].reshape(4, 2, *grads[n].shape[1:]) for n in names]
    from_sibling = _rs_sibling(blocks)
    chip_sums = [_chip_sum(f"grads_l0_chip_sum_{k}", b, s, ex.core) for k, (b, s) in enumerate(zip(blocks, from_sibling))]
    from_chips = _rs_chips(chip_sums)
    per_weight = dict(zip(names, [[l] for l in zip(blocks, from_sibling, from_chips)]))
    l1, l0m = ex.layers["l1"], ex.layers["l0m"]
    per_weight.update(w_ff1=[l0m[0], l1[0]], w_ff2=[l0m[1], l1[1]], w_in_o=[l1[2]], w_out_o=[l1[3]])
    results = {n: _finish_sharded(f"finish_{n}", per_weight[n], given[n], given["m_" + n], given["v_" + n], ex.where) for n in SHARDED}

    results.update(_small_reduce_adamw(gs, given))

    order = ["w_in_e", "mla_gq", "mla_gkv", "w_qb", "w_kvb", "sgu_ln_g", "sgu_ln_b", "sgu_w", "sgu_b", "w_out_e", "w_in_o",
             "hg_lb", "hg_gnorm", "w_out_o", "ln1_g", "ln1_b", "w_ff1", "w_ff2", "ln2_g", "ln2_b"]
    return (loss, dx[None], *[results[name][kind] for kind in range(4) for name in order])
```

```python
import functools
import math

import jax
import jax.numpy as jnp
import numpy as np
from jax import lax
from jax.experimental import pallas as pl
from jax.experimental.pallas import tpu as pltpu

F32 = jnp.float32
BF16 = jnp.bfloat16
MESH = pl.DeviceIdType.MESH
HIGHEST = lax.Precision.HIGHEST

D_MODEL = 1024
D_FF = 4096
N_DEV = 8
HEADS = 8
HEAD_W = 128
MLA_NOPE = 64
MLA_ROPE = 32
MLA_V = 64
MLA_LORA = 256
MLA_SCALE = (MLA_NOPE + MLA_ROPE) ** -0.5
ROPE_BASE = 10000.0
SGU_DIM = 512
SGU_G = 4
SGU_CHUNK = 128
HG_CHUNK = 64
ALPHA = (2 * 2) ** 0.25
EPS = 1e-5
ADAM_LR, ADAM_B1, ADAM_B2, ADAM_EPS, ADAM_WD, ADAM_STEP = 0.001, 0.9, 0.999, 1e-08, 0.01, 10

VMEM_CAP_V7X = 56 * 2**20
VMEM_SLACK = 12 * 2**20
TM = 512
TN = 512


def _vmem(block_bytes):
    return int(min(VMEM_CAP_V7X, 2 * block_bytes + VMEM_SLACK))


def _hbm(a):
    return pltpu.with_memory_space_constraint(a, pltpu.HBM)


def _nbytes(shape, dtype):
    return int(np.prod([d for d in shape if d is not None])) * jnp.dtype(dtype).itemsize


def _sig(x):
    return 1.0 / (1.0 + jnp.exp(-x))


def _gelu(x):
    c = math.sqrt(2.0 / math.pi)
    t = jnp.tanh(c * (x + 0.044715 * x * x * x))
    return 0.5 * x * (1.0 + t), t


def _gelu_grad(x, t):
    c = math.sqrt(2.0 / math.pi)
    return 0.5 * (1.0 + t) + 0.5 * x * (1.0 - t * t) * c * (1.0 + 3 * 0.044715 * x * x)


def _dot(a, b, dims, precision=None):
    return lax.dot_general(a, b, (dims, ((), ())), preferred_element_type=F32, precision=precision)


NN = ((1,), (0,))
NT = ((1,), (1,))
TN_ = ((0,), (0,))


def _deps(deps):
    return [d for d in deps if d is not None]


def _tiled(name, grid, ins, outs, compute, direct=False, deps=()):
    n_in, deps = len(ins), _deps(deps)
    n_skip = n_in + len(deps)

    def kern(*refs):
        if direct:
            compute(refs[:n_in], refs[n_skip:])
            return
        for o_ref, r in zip(refs[n_skip:], compute(*refs[:n_in])):
            o_ref[...] = r.astype(o_ref.dtype).reshape(o_ref.shape)

    swap = lambda f: (lambda j, i: f(i, j))
    nbytes = sum(_nbytes(blk, a.dtype) for a, blk, _ in ins) + sum(_nbytes(blk, dt) + _nbytes(blk, F32) for _, dt, blk, _ in outs)
    res = pl.pallas_call(
        kern, name=name, grid=grid,
        in_specs=[pl.BlockSpec(blk, swap(f)) for _, blk, f in ins] + [ANY_SPEC] * len(deps),
        out_specs=[pl.BlockSpec(blk, swap(f)) for _, _, blk, f in outs],
        out_shape=[pltpu.HBM(shape, dt) for shape, dt, _, _ in outs],
        compiler_params=pltpu.CompilerParams(dimension_semantics=("parallel", "parallel"), vmem_limit_bytes=_vmem(nbytes)),
    )(*[_hbm(a) for a, _, _ in ins], *deps)
    return res if len(res) > 1 else res[0]


def _rb(a, tm, w=None, cb=0):
    return (a, (tm, a.shape[1] if w is None else w), lambda i, j: (i, cb))


def _rbj(a, tm, tn):
    return (a, (tm, tn), lambda i, j: (i, j))


def _cw(b, tn):
    return (b, (b.shape[0], tn), lambda i, j: (0, j))


def _rw(b, tn):
    return (b, (tn, b.shape[1]), lambda i, j: (j, 0))


def _tl(a, tm):
    return (a, (a.shape[0], tm), lambda i, j: (0, i))


def _gcw(g):
    return (g, (None, g.shape[1], g.shape[2]), lambda i, j: (j, 0, 0))


def _grw(g, tn):
    return (g, (N_DEV, tn, g.shape[2]), lambda i, j: (0, j, 0))


def _out(m, n, dtype, tm, tn):
    return ((m, n), dtype, (tm, tn), lambda i, j: (i, j))


def _out_dev(k, n, tm):
    return ((N_DEV, k, n), F32, (None, tm, n), lambda i, j: (j, i, 0))


def _mmc(dims, n_pairs=1, epilogue=None):
    def compute(*refs):
        acc = None
        for k in range(n_pairs):
            d = _dot(refs[2 * k][...].astype(BF16), refs[2 * k + 1][...].astype(BF16), dims)
            acc = d if acc is None else acc + d
        ext = [r[...] for r in refs[2 * n_pairs:]]
        return epilogue(acc, *ext) if epilogue is not None else (acc,)

    return compute


def _res(w):
    return (w, w.shape, functools.partial(lambda i, j, nd: (0,) * nd, nd=w.ndim))


def _mmc_blocks(nblk, dims, rhs_block, epilogue=None):
    def compute(in_refs, out_refs):
        a = in_refs[0][...].astype(BF16)
        for d in range(nblk):
            acc = _dot(a, rhs_block(in_refs[1], d).astype(BF16), dims)
            n = acc.shape[1]
            ext = [r[:, d * n:(d + 1) * n] for r in in_refs[2:]]
            res = epilogue(acc, *ext) if epilogue is not None else (acc,)
            for o_ref, r in zip(out_refs, res):
                o_ref[:, d * n:(d + 1) * n] = r.astype(o_ref.dtype)

    return compute


def _mmc_dev(epilogue=None):
    def compute(a_ref, b_ref, *ext_refs):
        n = b_ref.shape[2]
        acc = None
        for d in range(N_DEV):
            t = _dot(a_ref[:, d * n:(d + 1) * n].astype(BF16), b_ref[d].astype(BF16), NT)
            acc = t if acc is None else acc + t
        ext = [r[...] for r in ext_refs]
        return epilogue(acc, *ext) if epilogue is not None else (acc,)

    return compute


def _rowwise(name, body, rows, consts, out_rows, out_accs=(), tr=512, deps=()):
    T = rows[0][0].shape[0]
    tr = min(tr, T)
    deps = _deps(deps)
    nr, ncn, no, nd = len(rows), len(consts), len(out_rows), len(deps)

    def kern(*refs):
        accs = refs[nr + ncn + nd + no:]
        if accs:
            @pl.when(pl.program_id(0) == 0)
            def _():
                for a in accs:
                    a[...] = jnp.zeros(a.shape, a.dtype)
        body(refs[:nr], refs[nr:nr + ncn], refs[nr + ncn + nd:nr + ncn + nd + no], accs)

    in_specs = [pl.BlockSpec((tr, w), functools.partial(lambda i, cb: (i, cb), cb=cb)) for _, w, cb in rows]
    in_specs += [pl.BlockSpec(c.shape, functools.partial(lambda i, nd: (0,) * nd, nd=c.ndim)) for c in consts]
    in_specs += [ANY_SPEC] * nd
    out_specs = [pl.BlockSpec((tr, w), lambda i: (i, 0)) for w, _ in out_rows]
    out_specs += [pl.BlockSpec(s, functools.partial(lambda i, nd: (0,) * nd, nd=len(s))) for s, _ in out_accs]
    out_shape = [pltpu.HBM((T, w), dt) for w, dt in out_rows]
    out_shape += [pltpu.HBM(s, dt) for s, dt in out_accs]
    nbytes = sum(_nbytes((tr, w), a.dtype) for a, w, _ in rows) + sum(_nbytes(c.shape, c.dtype) for c in consts)
    nbytes += sum(_nbytes((tr, w), dt) for w, dt in out_rows) + sum(_nbytes(s, dt) for s, dt in out_accs)
    res = pl.pallas_call(
        kern, name=name, grid=(T // tr,), in_specs=in_specs, out_specs=out_specs, out_shape=out_shape,
        compiler_params=pltpu.CompilerParams(dimension_semantics=("arbitrary",), vmem_limit_bytes=_vmem(nbytes)),
    )(*[_hbm(a) for a, _, _ in rows], *[_hbm(c) for c in consts], *deps)
    return res if len(res) > 1 else res[0]


def _full(a):
    return (a, a.shape[1], 0)


def _ln_stats(y):
    mu = jnp.mean(y, axis=-1, keepdims=True)
    yc = y - mu
    r = lax.rsqrt(jnp.mean(yc * yc, axis=-1, keepdims=True) + EPS)
    return yc * r, r


def _ln_fwd(name, h_in, mix, g, b, layer):
    def body(rows, consts, outs, accs):
        y = ALPHA * rows[0][...] + rows[1][...]
        xh, _ = _ln_stats(y)
        h = xh * consts[0][layer:layer + 1, :] + consts[1][layer:layer + 1, :]
        outs[0][...] = y
        outs[1][...] = h
        outs[2][...] = h.astype(BF16)

    return _rowwise(name, body, [_full(h_in), _full(mix)], [g, b], [(D_MODEL, F32), (D_MODEL, F32), (D_MODEL, BF16)], tr=256)


def _ln_loss(name, h_in, mix, g, b, layer, target):
    def body(rows, consts, outs, accs):
        y = ALPHA * rows[0][...] + rows[1][...]
        xh, _ = _ln_stats(y)
        err = xh * consts[0][layer:layer + 1, :] + consts[1][layer:layer + 1, :] - rows[2][...]
        outs[0][...] = y
        outs[1][...] = err * (1.0 / D_MODEL)
        accs[0][...] += jnp.sum(err * err, axis=0, keepdims=True)

    return _rowwise(name, body, [_full(h_in), _full(mix), _full(target)], [g, b], [(D_MODEL, F32), (D_MODEL, F32)],
                    [((1, D_MODEL), F32)], tr=256)


def _ln_bwd(name, y, dh, g, layer):
    def body(rows, consts, outs, accs):
        xh, r = _ln_stats(rows[0][...])
        d = rows[1][...]
        accs[0][...] += jnp.sum(d * xh, axis=0, keepdims=True)
        accs[1][...] += jnp.sum(d, axis=0, keepdims=True)
        dx = d * consts[0][layer:layer + 1, :]
        dy = r * (dx - jnp.mean(dx, axis=-1, keepdims=True) - xh * jnp.mean(dx * xh, axis=-1, keepdims=True))
        outs[0][...] = dy
        outs[1][...] = dy.astype(BF16)

    return _rowwise(name, body, [_full(y), _full(dh)], [g], [(D_MODEL, F32), (D_MODEL, BF16)],
                    [((1, D_MODEL), F32), ((1, D_MODEL), F32)], tr=256)


def _relu2_epilogue(acc):
    a = jnp.maximum(acc, 0.0)
    return acc, a * a


def _mlp_fwd(tag, h_bf, w1, w2):
    T = h_bf.shape[0]
    tm = min(TM, T)
    a, act = _tiled(f"{tag}_ff1", (1, T // tm), [_rb(h_bf, tm), _res(w1)],
                    [_out(T, D_FF, BF16, tm, D_FF), _out(T, D_FF, BF16, tm, D_FF)],
                    _mmc_blocks(N_DEV, NN, lambda w, d: w[d], epilogue=_relu2_epilogue), direct=True)
    ff = _tiled(f"{tag}_ff2", (1, T // tm), [_rb(act, tm), _res(w2.reshape(D_FF, D_MODEL))],
                [_out(T, D_MODEL, F32, tm, D_MODEL)], _mmc(NN))
    return a, act, ff


def _mlp_bwd_w(tag, h_bf, a, act, dff_bf, w2, deps=()):
    T = h_bf.shape[0]
    tm = min(TM, T)
    da = _tiled(f"{tag}_dact", (1, T // tm), [_rb(dff_bf, tm), _res(w2), _rb(a, tm)], [_out(T, D_FF, BF16, tm, D_FF)],
                _mmc_blocks(N_DEV, NT, lambda w, d: w[d], epilogue=lambda acc, a_t: (acc * 2.0 * jnp.maximum(a_t.astype(F32), 0.0),)),
                direct=True, deps=deps)
    dw2 = _tiled(f"{tag}_dw2", (1, D_FF // TM), [_tl(act, TM), _res(dff_bf)],
                 [_out(D_FF, D_MODEL, F32, TM, D_MODEL)], _mmc(TN_)).reshape(N_DEV, D_FF // N_DEV, D_MODEL)
    dw1 = _tiled(f"{tag}_dw1", (N_DEV, 1), [_res(h_bf), _cw(da, TN)], [_out_dev(D_MODEL, TN, D_MODEL)], _mmc(TN_))
    return da, dw1, dw2


def _mlp_bwd_h(tag, da, w1, dy, deps=()):
    T = da.shape[0]
    tm = min(TM, T)
    return _tiled(f"{tag}_dh", (1, T // tm), [_rb(da, tm), _res(w1), _rb(dy, tm)],
                  [_out(T, D_MODEL, F32, tm, D_MODEL), _out(T, D_MODEL, BF16, tm, D_MODEL)],
                  _mmc_dev(epilogue=lambda acc, dy_t: (acc + ALPHA * dy_t,) * 2), deps=deps)


def _rope_tables(positions_col, invf_lane):
    def body(rows, consts, outs, accs):
        ang = rows[0][...].astype(F32) * consts[0][...]
        c, s = jnp.cos(ang), jnp.sin(ang)
        lane = lax.broadcasted_iota(jnp.int32, ang.shape, 1)
        outs[0][...] = jnp.where(lane < 64, 1.0, jnp.where(lane < 96, c, 0.0))
        outs[1][...] = jnp.where((lane >= 64) & (lane < 80), -s, 0.0)
        outs[2][...] = jnp.where((lane >= 80) & (lane < 96), s, 0.0)

    return _rowwise("rope_tables", body, [_full(positions_col)], [invf_lane], [(HEAD_W, F32)] * 3)


def _rope(x, c, s1, s2):
    return x * c + pltpu.roll(x, 112, 1) * s1 + pltpu.roll(x, 16, 1) * s2


def _rope_t(dx, c, s1, s2):
    return dx * c + pltpu.roll(dx * s1, 16, 1) + pltpu.roll(dx * s2, 112, 1)


def _rms(c):
    r = lax.rsqrt(jnp.mean(c * c, axis=-1, keepdims=True) + EPS)
    return c * r, r


def _mla_pre(zm, tabs, gq, gkv):
    def body(rows, consts, outs, accs):
        cq, _ = _rms(rows[0][...])
        ckv, _ = _rms(rows[1][...])
        outs[0][...] = (cq * consts[0][...]).astype(BF16)
        outs[1][...] = (ckv * consts[1][...]).astype(BF16)
        outs[2][...] = _rope(rows[2][...], rows[3][...], rows[4][...], rows[5][...])

    rows = [(zm, 256, 0), (zm, 256, 1), (zm, 128, 4)] + [_full(t) for t in tabs]
    return _rowwise("mla_pre", body, rows, [gq, gkv], [(256, BF16), (256, BF16), (HEAD_W, F32)])


def _mla_pre_bwd(zm, tabs, gq, gkv, dcqn, dckvn, dk):
    def body(rows, consts, outs, accs):
        res = []
        for k in range(2):
            ch, r = _rms(rows[k][...])
            d = rows[5 + k][...]
            accs[k][...] += jnp.sum(d * ch, axis=0, keepdims=True)
            dc = d * consts[k][...]
            res.append(r * (dc - ch * jnp.mean(dc * ch, axis=-1, keepdims=True)))
        dks = rows[7][:, 0:HEAD_W]
        for h in range(1, HEADS):
            dks = dks + rows[7][:, h * HEAD_W:(h + 1) * HEAD_W]
        lane = lax.broadcasted_iota(jnp.int32, dks.shape, 1)
        dks = jnp.where((lane >= 64) & (lane < 96), dks, 0.0)
        dkr = _rope_t(dks, rows[2][...], rows[3][...], rows[4][...])
        outs[0][:, 0:256] = res[0].astype(BF16)
        outs[0][:, 256:512] = res[1].astype(BF16)
        outs[0][:, 512:640] = dkr.astype(BF16)

    rows = [(zm, 256, 0), (zm, 256, 1)] + [_full(t) for t in tabs] + [_full(dcqn), _full(dckvn), _full(dk)]
    return _rowwise("mla_pre_bwd", body, rows, [gq, gkv], [(640, BF16)], [((1, 256), F32), ((1, 256), F32)])


def _rope_heads(x, c, s1, s2, fn):
    return jnp.concatenate([fn(x[:, h * HEAD_W:(h + 1) * HEAD_W], c, s1, s2) for h in range(HEADS)], axis=1)


def _unrope_heads(dq, tabs):
    def body(rows, consts, outs, accs):
        outs[0][...] = _rope_heads(rows[0][...], rows[1][...], rows[2][...], rows[3][...], _rope_t).astype(BF16)

    return _rowwise("l0_dq_rope", body, [_full(dq)] + [_full(t) for t in tabs], [], [(HEADS * HEAD_W, BF16)])


def _attn_block(T):
    return min(1024, T)


def _attn_fwd(q, k, v):
    T = q.shape[0]
    BQ = _attn_block(T)
    nq = T // BQ

    def kern(q_ref, k_ref, v_ref, o_ref, lse_ref):
        def step(i, j, carry, masked):
            m, l, acc = carry
            qb = q_ref[pl.ds(pl.multiple_of(i * BQ, BQ), BQ), :]
            kb = k_ref[pl.ds(pl.multiple_of(j * BQ, BQ), BQ), :]
            vb = v_ref[pl.ds(pl.multiple_of(j * BQ, BQ), BQ), :]
            s = _dot(qb, kb, NT) * MLA_SCALE
            if masked:
                row = lax.broadcasted_iota(jnp.int32, s.shape, 0)
                col = lax.broadcasted_iota(jnp.int32, s.shape, 1)
                s = jnp.where(col <= row, s, -1e30)
            m_new = jnp.maximum(m, jnp.max(s, axis=-1, keepdims=True))
            p = jnp.exp(s - m_new)
            a = jnp.exp(m - m_new)
            l = a * l + jnp.sum(p, axis=-1, keepdims=True)
            acc = a * acc + _dot(p.astype(BF16), vb, NN)
            return m_new, l, acc

        def qloop(i, _):
            init = (jnp.full((BQ, 1), -1e30, F32), jnp.zeros((BQ, 1), F32), jnp.zeros((BQ, HEAD_W), F32))
            carry = lax.fori_loop(0, i, lambda j, c: step(i, j, c, False), init)
            m, l, acc = step(i, i, carry, True)
            rows = pl.ds(pl.multiple_of(i * BQ, BQ), BQ)
            o_ref[rows, :] = acc / l
            lse_ref[0, rows, :] = m + jnp.log(l)
            return 0

        lax.fori_loop(0, nq, qloop, 0)

    head = pl.BlockSpec((T, HEAD_W), lambda h: (0, h))
    nbytes = 3 * _nbytes((T, HEAD_W), BF16) + _nbytes((T, HEAD_W), F32) + _nbytes((T, 128), F32)
    return pl.pallas_call(
        kern, name="attn_fwd", grid=(HEADS,), in_specs=[head, head, head],
        out_specs=[head, pl.BlockSpec((1, T, 1), lambda h: (h, 0, 0))],
        out_shape=[pltpu.HBM((T, HEADS * HEAD_W), F32), pltpu.HBM((HEADS, T, 1), F32)],
        compiler_params=pltpu.CompilerParams(dimension_semantics=("parallel",), vmem_limit_bytes=_vmem(nbytes)),
    )(_hbm(q), _hbm(k), _hbm(v))


def _attn_bwd(q, k, v, o, lse, dcat):
    T = q.shape[0]
    BQ = _attn_block(T)
    nq = T // BQ

    def kern(q_ref, k_ref, v_ref, o_ref, lse_ref, do_ref, dq_ref, dk_ref, dv_ref, dd_ref):
        dq_ref[...] = jnp.zeros(dq_ref.shape, F32)

        def dloop(i, _):
            rows = pl.ds(pl.multiple_of(i * BQ, BQ), BQ)
            dd_ref[rows, :] = jnp.sum(do_ref[rows, :].astype(F32) * o_ref[rows, :], axis=-1, keepdims=True)
            return 0

        lax.fori_loop(0, nq, dloop, 0)

        def step(j, i, carry, masked):
            dk_acc, dv_acc = carry
            rq = pl.ds(pl.multiple_of(i * BQ, BQ), BQ)
            rk = pl.ds(pl.multiple_of(j * BQ, BQ), BQ)
            qb, kb, vb, dob = q_ref[rq, :], k_ref[rk, :], v_ref[rk, :], do_ref[rq, :]
            s = _dot(qb, kb, NT) * MLA_SCALE
            p = jnp.exp(s - lse_ref[0, rq, :])
            if masked:
                row = lax.broadcasted_iota(jnp.int32, s.shape, 0)
                col = lax.broadcasted_iota(jnp.int32, s.shape, 1)
                p = jnp.where(col <= row, p, 0.0)
            dp = _dot(dob, vb, NT)
            ds = (p * (dp - dd_ref[rq, :]) * MLA_SCALE).astype(BF16)
            dv_acc = dv_acc + _dot(p.astype(BF16), dob, TN_)
            dk_acc = dk_acc + _dot(ds, qb, TN_)
            dq_ref[rq, :] += _dot(ds, kb, NN)
            return dk_acc, dv_acc

        def kloop(j, _):
            init = (jnp.zeros((BQ, HEAD_W), F32), jnp.zeros((BQ, HEAD_W), F32))
            carry = step(j, j, init, True)
            dk_acc, dv_acc = lax.fori_loop(j + 1, nq, lambda i, c: step(j, i, c, False), carry)
            rk = pl.ds(pl.multiple_of(j * BQ, BQ), BQ)
            dk_ref[rk, :] = dk_acc
            dv_ref[rk, :] = dv_acc
            return 0

        lax.fori_loop(0, nq, kloop, 0)

    head = pl.BlockSpec((T, HEAD_W), lambda h: (0, h))
    nbytes = 4 * _nbytes((T, HEAD_W), BF16) + 5 * _nbytes((T, HEAD_W), F32) + 2 * _nbytes((T, 128), F32)
    return pl.pallas_call(
        kern, name="attn_bwd", grid=(HEADS,),
        in_specs=[head, head, head, head, pl.BlockSpec((1, T, 1), lambda h: (h, 0, 0)), head],
        out_specs=[head, head, head],
        out_shape=[pltpu.HBM((T, HEADS * HEAD_W), F32)] * 3,
        scratch_shapes=[pltpu.VMEM((T, 1), F32)],
        compiler_params=pltpu.CompilerParams(dimension_semantics=("parallel",), vmem_limit_bytes=_vmem(nbytes)),
    )(*[_hbm(a) for a in (q, k, v, o, lse, dcat)])


def _sgu_common(u, v, ln_g, ln_b):
    ua, tu = _gelu(u)
    va, tv = _gelu(v)
    vh, r = _ln_stats(va)
    return ua, tu, tv, vh, r, vh * ln_g + ln_b


def _tril_mask(n):
    return lax.broadcasted_iota(jnp.int32, (n, n), 1) <= lax.broadcasted_iota(jnp.int32, (n, n), 0)


def _sgu_fwd(zs, ln_g, ln_b, w, bias_full):
    def body(rows, consts, outs, accs):
        ua, _, _, _, _, vn = _sgu_common(rows[0][...], rows[1][...], consts[0][...], consts[1][...])
        vn = vn.astype(BF16)
        tri = _tril_mask(SGU_CHUNK)
        for g in range(SGU_G):
            wg = jnp.where(tri, consts[2][0, g], 0.0).astype(BF16)
            cols = slice(g * 128, (g + 1) * 128)
            for c in range(ua.shape[0] // SGU_CHUNK):
                rws = slice(c * SGU_CHUNK, (c + 1) * SGU_CHUNK)
                mixed = _dot(wg, vn[rws, cols], NN) + consts[3][:, cols]
                outs[0][rws, cols] = (ua[rws, cols] * mixed).astype(BF16)

    return _rowwise("sgu_fwd", body, [(zs, 512, 0), (zs, 512, 1)], [ln_g, ln_b, w, bias_full], [(SGU_DIM, BF16)])


def _sgu_bwd(zs, dcat, ln_g, ln_b, w, bias_full):
    def body(rows, consts, outs, accs):
        u, v = rows[0][...], rows[1][...]
        ua, tu, tv, vh, r, vn = _sgu_common(u, v, consts[0][...], consts[1][...])
        dout = rows[2][...].astype(F32)
        vn_bf = vn.astype(BF16)
        tri = _tril_mask(SGU_CHUNK)
        dmixed = (dout * ua)
        dmixed_bf = dmixed.astype(BF16)
        ones = jnp.ones((8, SGU_CHUNK), F32)
        dvn_cols, mixed_cols = [], []
        for g in range(SGU_G):
            wg = jnp.where(tri, consts[2][0, g], 0.0).astype(BF16)
            cols = slice(g * 128, (g + 1) * 128)
            dvn_rows, mixed_rows = [], []
            dw = jnp.zeros((SGU_CHUNK, SGU_CHUNK), F32)
            dmix_sum = jnp.zeros((SGU_CHUNK, 128), F32)
            for c in range(u.shape[0] // SGU_CHUNK):
                rws = slice(c * SGU_CHUNK, (c + 1) * SGU_CHUNK)
                mixed_rows.append(_dot(wg, vn_bf[rws, cols], NN) + consts[3][:, cols])
                dvn_rows.append(_dot(wg, dmixed_bf[rws, cols], TN_))
                dw = dw + _dot(dmixed_bf[rws, cols], vn_bf[rws, cols], NT)
                dmix_sum = dmix_sum + dmixed[rws, cols]
            accs[0][g] += jnp.where(tri, dw, 0.0)
            accs[3][g:g + 1, :] += _dot(ones, dmix_sum, NT, precision=HIGHEST)[0:1, :]
            dvn_cols.append(jnp.concatenate(dvn_rows, axis=0))
            mixed_cols.append(jnp.concatenate(mixed_rows, axis=0))
        dvn = jnp.concatenate(dvn_cols, axis=1)
        mixed = jnp.concatenate(mixed_cols, axis=1)
        accs[1][...] += jnp.sum(dvn * vh, axis=0, keepdims=True)
        accs[2][...] += jnp.sum(dvn, axis=0, keepdims=True)
        dvh = dvn * consts[0][...]
        dva = r * (dvh - jnp.mean(dvh, axis=-1, keepdims=True) - vh * jnp.mean(dvh * vh, axis=-1, keepdims=True))
        outs[0][:, 0:512] = (dout * mixed * _gelu_grad(u, tu)).astype(BF16)
        outs[0][:, 512:1024] = (dva * _gelu_grad(v, tv)).astype(BF16)

    return _rowwise("sgu_bwd", body, [(zs, 512, 0), (zs, 512, 1), (dcat, 512, 2)], [ln_g, ln_b, w, bias_full], [(1024, BF16)],
                    [((SGU_G, 128, 128), F32), ((1, SGU_DIM), F32), ((1, SGU_DIM), F32), ((SGU_G, 128), F32)], tr=256)


def _lower_bound(hg_lb):
    a0, a1 = hg_lb[0:1, :], hg_lb[1:2, :]
    m = jnp.maximum(a0, a1)
    e0, e1 = jnp.exp(a0 - m), jnp.exp(a1 - m)
    s0, s1 = e0 / (e0 + e1), e1 / (e0 + e1)
    return (s0 + s1) - s0, s0, s1


def _hg_gates(qr, fr, lb):
    C = qr.shape[0]
    sq = _sig(qr)
    qf = qr * sq
    sf = _sig(fr)
    gate = lb + (1.0 - lb) * sf
    kk = 1.0 - gate
    tri = _tril_mask(C)
    b = _dot(jnp.where(tri, 1.0, 0.0), jnp.log(gate), NN, precision=HIGHEST)
    bref = b[C // 2 - 1:C // 2, :]
    bl = b[C - 1:C, :]
    e_b = jnp.exp(b)
    e_q = jnp.exp(b - bref)
    e_k = jnp.exp(bref - b)
    e_lb = jnp.exp(bl - b)
    return dict(sq=sq, qf=qf, sf=sf, gate=gate, kk=kk, tri=tri, bl=bl, e_b=e_b, e_q=e_q, e_k=e_k, e_lb=e_lb)


def _hgrn_fwd(z1, hg_lb, gnorm):
    T = z1.shape[0]
    C = min(HG_CHUNK, T)
    nc = T // C

    def kern(q_ref, f_ref, i_ref, g_ref, lb_ref, gn_ref, o_ref, hg_ref, st_ref, s_scr):
        @pl.when(pl.program_id(0) == 0)
        def _():
            s_scr[...] = jnp.zeros(s_scr.shape, F32)

        lb_all, _, _ = _lower_bound(lb_ref[...])
        st_ref[0] = s_scr[...]
        for h in range(HEADS):
            cols = slice(h * HEAD_W, (h + 1) * HEAD_W)
            t = _hg_gates(q_ref[:, cols], f_ref[:, cols], lb_all[:, cols])
            v = i_ref[:, cols]
            v_bf = v.astype(BF16)
            st = s_scr[h]
            a = jnp.where(t["tri"], _dot((t["qf"] * t["e_q"]).astype(BF16), (t["kk"] * t["e_k"]).astype(BF16), NT), 0.0)
            o = _dot(a.astype(BF16), v_bf, NN) + _dot((t["qf"] * t["e_b"]).astype(BF16), st.astype(BF16), NT)
            s_scr[h] = st * jnp.exp(t["bl"]) + _dot(v_bf, (t["kk"] * t["e_lb"]).astype(BF16), TN_)
            o_ref[:, cols] = o
            gr = g_ref[:, cols]
            r = lax.rsqrt(jnp.mean(o * o, axis=-1, keepdims=True) + EPS)
            hg_ref[:, cols] = (o * r * gn_ref[:, cols] * (gr * _sig(gr))).astype(BF16)

    seg = lambda k: pl.BlockSpec((C, D_MODEL), functools.partial(lambda n, k: (n, k), k=k))
    row = pl.BlockSpec((C, D_MODEL), lambda n: (n, 0))
    nbytes = 6 * _nbytes((C, D_MODEL), F32) + 3 * _nbytes((HEADS, 128, 128), F32)
    return pl.pallas_call(
        kern, name="hgrn_fwd", grid=(nc,),
        in_specs=[seg(0), seg(1), seg(2), seg(3), pl.BlockSpec((2, D_MODEL), lambda n: (0, 0)),
                  pl.BlockSpec((1, D_MODEL), lambda n: (0, 0))],
        out_specs=[row, row, pl.BlockSpec((1, HEADS, 128, 128), lambda n: (n, 0, 0, 0))],
        out_shape=[pltpu.HBM((T, D_MODEL), F32), pltpu.HBM((T, D_MODEL), BF16),
                   pltpu.HBM((nc, HEADS, 128, 128), F32)],
        scratch_shapes=[pltpu.VMEM((HEADS, 128, 128), F32)],
        compiler_params=pltpu.CompilerParams(dimension_semantics=("arbitrary",), vmem_limit_bytes=_vmem(nbytes)),
    )(*[_hbm(a) for a in (z1, z1, z1, z1, hg_lb, gnorm)])


def _hgrn_bwd(z1, o_pre, dhg, states, hg_lb, gnorm):
    T = z1.shape[0]
    C = min(HG_CHUNK, T)
    nc = T // C

    def kern(q_ref, f_ref, i_ref, g_ref, o_ref, dhg_ref, st_ref, lb_ref, gn_ref, dz_ref, dlb_ref, dgn_ref, ds_scr, dlb_scr):
        n = pl.program_id(0)

        @pl.when(n == 0)
        def _():
            ds_scr[...] = jnp.zeros(ds_scr.shape, F32)
            dlb_scr[...] = jnp.zeros(dlb_scr.shape, F32)
            dgn_ref[...] = jnp.zeros(dgn_ref.shape, F32)

        lb_all, s0, s1 = _lower_bound(lb_ref[...])
        for h in range(HEADS):
            cols = slice(h * HEAD_W, (h + 1) * HEAD_W)
            lb = lb_all[:, cols]
            qr, fr = q_ref[:, cols], f_ref[:, cols]
            t = _hg_gates(qr, fr, lb)
            tri = t["tri"]
            v_bf = i_ref[:, cols].astype(BF16)
            st_bf = st_ref[0, h].astype(BF16)
            dst = ds_scr[h]
            dst_bf = dst.astype(BF16)
            o = o_ref[:, cols]
            gr = g_ref[:, cols]
            sg = _sig(gr)
            sil = gr * sg
            gn = gn_ref[:, cols]
            r = lax.rsqrt(jnp.mean(o * o, axis=-1, keepdims=True) + EPS)
            on = o * r
            dh = dhg_ref[:, cols].astype(F32)
            dgn_ref[:, cols] += jnp.sum(dh * on * sil, axis=0, keepdims=True)
            dg = dh * on * gn * (sg * (1.0 + gr * (1.0 - sg)))
            don = dh * gn * sil
            do_bf = (r * (don - on * jnp.mean(don * on, axis=-1, keepdims=True))).astype(BF16)
            qe = (t["qf"] * t["e_q"]).astype(BF16)
            ke = (t["kk"] * t["e_k"]).astype(BF16)
            qb = (t["qf"] * t["e_b"]).astype(BF16)
            kh_bf = (t["kk"] * t["e_lb"]).astype(BF16)
            a_bf = jnp.where(tri, _dot(qe, ke, NT), 0.0).astype(BF16)
            da_bf = jnp.where(tri, _dot(do_bf, v_bf, NT), 0.0).astype(BF16)
            dv = _dot(a_bf, do_bf, TN_) + _dot(kh_bf, dst_bf, NT)
            dqe = _dot(da_bf, ke, NN)
            dqb = _dot(do_bf, st_bf, NN)
            dke = _dot(da_bf, qe, TN_)
            dkh = _dot(v_bf, dst_bf, NN)
            dqf = dqe * t["e_q"] + dqb * t["e_b"]
            dkk = dke * t["e_k"] + dkh * t["e_lb"]
            kh_r = kh_bf.astype(F32)
            db = qe.astype(F32) * dqe - ke.astype(F32) * dke + qb.astype(F32) * dqb - kh_r * dkh
            e_bl = jnp.exp(t["bl"])
            dbl = jnp.sum(dkh * kh_r, axis=0, keepdims=True) + e_bl * jnp.sum(st_ref[0, h] * dst, axis=0, keepdims=True)
            dlg = _dot(jnp.where(tri, 1.0, 0.0), db, TN_, precision=HIGHEST) + dbl
            ds_scr[h] = dst * e_bl + _dot(do_bf, qb, TN_)
            dgate = dlg / t["gate"] - dkk
            sf = t["sf"]
            dlb_scr[:, cols] += jnp.sum(dgate * (1.0 - sf), axis=0, keepdims=True)
            df = dgate * (1.0 - lb) * sf * (1.0 - sf)
            dq = dqf * (t["sq"] * (1.0 + qr * (1.0 - t["sq"])))
            dz_ref[:, cols] = dq.astype(BF16)
            dz_ref[:, D_MODEL + h * HEAD_W:D_MODEL + (h + 1) * HEAD_W] = df.astype(BF16)
            dz_ref[:, 2 * D_MODEL + h * HEAD_W:2 * D_MODEL + (h + 1) * HEAD_W] = dv.astype(BF16)
            dz_ref[:, 3 * D_MODEL + h * HEAD_W:3 * D_MODEL + (h + 1) * HEAD_W] = dg.astype(BF16)

        @pl.when(n == nc - 1)
        def _():
            d = s0 * s1 * dlb_scr[...]
            dlb_ref[0:1, :] = -d
            dlb_ref[1:2, :] = d

    seg = lambda k: pl.BlockSpec((C, D_MODEL), functools.partial(lambda n, k: (nc - 1 - n, k), k=k))
    nbytes = 6 * _nbytes((C, D_MODEL), F32) + _nbytes((C, 4 * D_MODEL), BF16) + 3 * _nbytes((HEADS, 128, 128), F32)
    return pl.pallas_call(
        kern, name="hgrn_bwd", grid=(nc,),
        in_specs=[seg(0), seg(1), seg(2), seg(3), seg(0), seg(0),
                  pl.BlockSpec((1, HEADS, 128, 128), lambda n: (nc - 1 - n, 0, 0, 0)),
                  pl.BlockSpec((2, D_MODEL), lambda n: (0, 0)), pl.BlockSpec((1, D_MODEL), lambda n: (0, 0))],
        out_specs=[pl.BlockSpec((C, 4 * D_MODEL), lambda n: (nc - 1 - n, 0)),
                   pl.BlockSpec((2, D_MODEL), lambda n: (0, 0)), pl.BlockSpec((1, D_MODEL), lambda n: (0, 0))],
        out_shape=[pltpu.HBM((T, 4 * D_MODEL), BF16), pltpu.HBM((2, D_MODEL), F32),
                   pltpu.HBM((1, D_MODEL), F32)],
        scratch_shapes=[pltpu.VMEM((HEADS, 128, 128), F32), pltpu.VMEM((1, D_MODEL), F32)],
        compiler_params=pltpu.CompilerParams(dimension_semantics=("arbitrary",), vmem_limit_bytes=_vmem(nbytes)),
    )(*[_hbm(a) for a in (z1, z1, z1, z1, o_pre, dhg, states, hg_lb, gnorm)])


def _prep_weights(gw):
    w_in_e = gw["w_in_e"].transpose(1, 0, 2).reshape(D_MODEL, 1568)
    kr = jnp.pad(w_in_e[:, 512:544], ((0, 0), (64, 32)))
    wm = jnp.concatenate([w_in_e[:, 0:512], kr], axis=1)
    ws = w_in_e[:, 544:1568]
    w_qb = gw["w_qb"].transpose(1, 0, 2).reshape(MLA_LORA, HEADS, 96)
    wq = jnp.pad(w_qb, ((0, 0), (0, 0), (0, 32))).reshape(MLA_LORA, HEADS * HEAD_W)
    kvb = gw["w_kvb"].transpose(1, 0, 2).reshape(MLA_LORA, HEADS, 128)
    wk = jnp.pad(kvb[:, :, :64], ((0, 0), (0, 0), (0, 64))).reshape(MLA_LORA, HEADS * HEAD_W)
    wv = jnp.pad(kvb[:, :, 64:], ((0, 0), (0, 0), (0, 64))).reshape(MLA_LORA, HEADS * HEAD_W)
    w_out_e = gw["w_out_e"].reshape(D_MODEL, D_MODEL)
    woa = jnp.pad(w_out_e[:512].reshape(HEADS, 64, D_MODEL), ((0, 0), (0, 64), (0, 0))).reshape(HEADS * HEAD_W, D_MODEL)
    return dict(wm=wm, ws=ws, wq=wq, wk=wk, wv=wv, woa=woa, wob=w_out_e[512:])


def _unprep_grads(g):
    dwm, dws = g["wm"], g["ws"]
    d_in_e = jnp.concatenate([dwm[:, 0:512], dwm[:, 512 + 64:512 + 96], dws], axis=1)
    d_qb = g["wq"].reshape(MLA_LORA, HEADS, HEAD_W)[:, :, :96].reshape(MLA_LORA, HEADS * 96)
    dk = g["wk"].reshape(MLA_LORA, HEADS, HEAD_W)[:, :, :64]
    dv = g["wv"].reshape(MLA_LORA, HEADS, HEAD_W)[:, :, :64]
    d_kvb = jnp.concatenate([dk, dv], axis=2).reshape(MLA_LORA, HEADS * 128)
    d_oa = g["woa"].reshape(HEADS, HEAD_W, D_MODEL)[:, :64].reshape(HEADS * 64, D_MODEL)
    dev_major = lambda a: a.reshape(a.shape[0], N_DEV, a.shape[1] // N_DEV).transpose(1, 0, 2)
    return dict(w_in_e=dev_major(d_in_e), w_qb=dev_major(d_qb), w_kvb=dev_major(d_kvb),
                w_out_e=jnp.concatenate([d_oa, g["wob"]], axis=0).reshape(N_DEV, D_MODEL // N_DEV, D_MODEL))


def _local_step(x, positions, target, gw, sp, ex):
    w = _prep_weights(gw)
    T = x.shape[0]
    tm = min(TM, T)
    nt = T // tm
    half = MLA_ROPE // 2
    inv_freq = ROPE_BASE ** (-jnp.arange(half, dtype=F32) / half)
    invf_lane = jnp.concatenate([jnp.zeros((64,), F32), inv_freq, inv_freq, jnp.zeros((32,), F32)]).reshape(1, HEAD_W)
    tabs = _rope_tables(positions.reshape(T, 1), invf_lane)
    bias_full = jnp.repeat(sp["sgu_b"][0].T, 128, axis=1)
    sgu_w = sp["sgu_w"]
    gq, gkv = sp["mla_gq"], sp["mla_gkv"]
    ln1_g, ln1_b, ln2_g, ln2_b = sp["ln1_g"], sp["ln1_b"], sp["ln2_g"], sp["ln2_b"]
    wide = HEADS * HEAD_W
    tab_rows = [_rb(t, tm) for t in tabs]
    resid = lambda acc, d: (acc + ALPHA * d,)

    zm = _tiled("l0_in_mla", (1, nt), [_rb(x, tm), _cw(w["wm"], 640)], [_out(T, 640, F32, tm, 640)], _mmc(NN), deps=[ex.first_token])
    zs = _tiled("l0_in_sgu", (2, nt), [_rb(x, tm), _cw(w["ws"], TN)], [_out(T, 1024, F32, tm, TN)], _mmc(NN), deps=[ex.first_token])
    cqn, ckvn, kr_rot = _mla_pre(zm, tabs, gq, gkv)
    q = _tiled("l0_q", (1, nt), [_rb(cqn, tm), _cw(w["wq"], wide)] + tab_rows, [_out(T, wide, BF16, tm, wide)],
               _mmc(NN, epilogue=lambda acc, c, s1, s2: (_rope_heads(acc, c, s1, s2, _rope),)))
    k = _tiled("l0_k", (1, nt), [_rb(ckvn, tm), _cw(w["wk"], wide), _rb(kr_rot, tm)], [_out(T, wide, BF16, tm, wide)],
               _mmc(NN, epilogue=lambda acc, kr: (acc + jnp.concatenate([kr] * HEADS, axis=1),)))
    v = _tiled("l0_v", (1, nt), [_rb(ckvn, tm), _cw(w["wv"], wide)], [_out(T, wide, BF16, tm, wide)], _mmc(NN))
    o_att, lse = _attn_fwd(q, k, v)
    b_out = _sgu_fwd(zs, sp["sgu_ln_g"], sp["sgu_ln_b"], sgu_w, bias_full)
    mix0 = _tiled("l0_out", (2, nt), [_rb(o_att, tm), _cw(w["woa"], TN), _rb(b_out, tm), _cw(w["wob"], TN)],
                  [_out(T, D_MODEL, F32, tm, TN)], _mmc(NN, n_pairs=2))
    y1, h1, h1_bf = _ln_fwd("l0_ln1", x, mix0, ln1_g, ln1_b, 0)
    big = ex.weights_ready(after=y1)
    w_ff1, w_ff2, w_in_o, w_out_o = big["w_ff1"], big["w_ff2"], big["w_in_o"], big["w_out_o"].reshape(D_MODEL, D_MODEL)
    a0, act0, ff0 = _mlp_fwd("l0", h1_bf, w_ff1[0], w_ff2[0])
    y2, h2, h2_bf = _ln_fwd("l0_ln2", h1, ff0, ln2_g, ln2_b, 0)

    z1 = _tiled("l1_in", (1, nt), [_rb(h2_bf, tm), _res(w_in_o)], [_out(T, 4 * D_MODEL, F32, tm, 4 * D_MODEL)],
                _mmc_blocks(N_DEV, NN, lambda w, d: w[d]), direct=True)
    o_pre, hg, states = _hgrn_fwd(z1, sp["hg_lb"], sp["hg_gnorm"])
    mix1 = _tiled("l1_out", (2, nt), [_rb(hg, tm), _cw(w_out_o, TN)], [_out(T, D_MODEL, F32, tm, TN)], _mmc(NN))
    y3, h3, h3_bf = _ln_fwd("l1_ln1", h2, mix1, ln1_g, ln1_b, 1)
    a1, act1, ff1 = _mlp_fwd("l1", h3_bf, w_ff1[1], w_ff2[1])
    y4, dh4, sq_err = _ln_loss("l1_ln2", h3, ff1, ln2_g, ln2_b, 1, target)

    gs, g0 = {}, {}
    dy4, dy4_bf, gs["ln2_g1"], gs["ln2_b1"] = _ln_bwd("l1_ln2_bwd", y4, dh4, ln2_g, 1)
    da1, dw1_1, dw2_1 = _mlp_bwd_w("l1", h3_bf, a1, act1, dy4_bf, w_ff2[1])
    dh3, _ = _mlp_bwd_h("l1", da1, w_ff1[1], dy4)
    dy3, dy3_bf, gs["ln1_g1"], gs["ln1_b1"] = _ln_bwd("l1_ln1_bwd", y3, dh3, ln1_g, 1)
    d_out_o = _tiled("l1_dwout", (2, D_MODEL // TM), [_tl(hg, TM), _cw(dy3_bf, TN)], [_out(D_MODEL, D_MODEL, F32, TM, TN)],
                     _mmc(TN_)).reshape(N_DEV, D_MODEL // N_DEV, D_MODEL)
    dhg = _tiled("l1_dhg", (2, nt), [_rb(dy3_bf, tm), _rw(w_out_o, TN)], [_out(T, D_MODEL, BF16, tm, TN)], _mmc(NT))
    dz1, gs["hg_lb"], gs["hg_gnorm"] = _hgrn_bwd(z1, o_pre, dhg, states, sp["hg_lb"], sp["hg_gnorm"])
    d_in_o = _tiled("l1_dwin", (N_DEV, 1), [_res(h2_bf), _cw(dz1, TN)], [_out_dev(D_MODEL, TN, D_MODEL)], _mmc(TN_))
    token = ex.grads_start("l1", [dw1_1, dw2_1, d_in_o, d_out_o])
    dh2 = _tiled("l1_dh2", (1, nt), [_rb(dz1, tm), _res(w_in_o), _rb(dy3, tm)], [_out(T, D_MODEL, F32, tm, D_MODEL)],
                 _mmc_dev(epilogue=resid), deps=[token])

    dy2, dy2_bf, gs["ln2_g0"], gs["ln2_b0"] = _ln_bwd("l0_ln2_bwd", y2, dh2, ln2_g, 0)
    token = ex.grads_middle("l1", after=dy2)
    da0, dw1_0, dw2_0 = _mlp_bwd_w("l0", h1_bf, a0, act0, dy2_bf, w_ff2[0], deps=[token])
    token = ex.grads_start("l0m", [dw1_0, dw2_0])
    dh1, _ = _mlp_bwd_h("l0", da0, w_ff1[0], dy2, deps=[token])
    ex.grads_end("l1", after=dh1)
    dy1, dy1_bf, gs["ln1_g0"], gs["ln1_b0"] = _ln_bwd("l0_ln1_bwd", y1, dh1, ln1_g, 0)
    token = ex.grads_middle("l0m", after=dy1)
    g0["woa"] = _tiled("l0_dwoa", (2, wide // TM), [_tl(o_att, TM), _cw(dy1_bf, TN)], [_out(wide, D_MODEL, F32, TM, TN)], _mmc(TN_))
    g0["wob"] = _tiled("l0_dwob", (2, 1), [_tl(b_out, SGU_DIM), _cw(dy1_bf, TN)], [_out(SGU_DIM, D_MODEL, F32, SGU_DIM, TN)], _mmc(TN_))
    wo_cat = jnp.concatenate([w["woa"], w["wob"]], axis=0)
    dcat = _tiled("l0_dcat", (3, nt), [_rb(dy1_bf, tm), _rw(wo_cat, TN)], [_out(T, wide + SGU_DIM, BF16, tm, TN)], _mmc(NT), deps=[token])
    dzs, gs["sgu_w"], gs["sgu_ln_g"], gs["sgu_ln_b"], gs["sgu_b"] = _sgu_bwd(zs, dcat, sp["sgu_ln_g"], sp["sgu_ln_b"], sgu_w, bias_full)
    dq, dk, dv = _attn_bwd(q, k, v, o_att, lse, dcat)
    ex.grads_end("l0m", after=dq)
    dq_pre = _unrope_heads(dq, tabs)
    lora_w = lambda name, a, d: _tiled(name, (wide // TN, 1), [_tl(a, MLA_LORA), _cw(d, TN)], [_out(MLA_LORA, wide, F32, MLA_LORA, TN)], _mmc(TN_))
    g0["wq"] = lora_w("l0_dwq", cqn, dq_pre)
    g0["wk"] = lora_w("l0_dwk", ckvn, dk)
    g0["wv"] = lora_w("l0_dwv", ckvn, dv)
    dcqn = _tiled("l0_dcqn", (1, nt), [_rb(dq_pre, tm), _rw(w["wq"], MLA_LORA)], [_out(T, MLA_LORA, F32, tm, MLA_LORA)], _mmc(NT))
    dckvn = _tiled("l0_dckvn", (1, nt), [_rb(dk, tm), _rw(w["wk"], MLA_LORA), _rb(dv, tm), _rw(w["wv"], MLA_LORA)],
                   [_out(T, MLA_LORA, F32, tm, MLA_LORA)], _mmc(NT, n_pairs=2))
    dzm, gs["mla_gq"], gs["mla_gkv"] = _mla_pre_bwd(zm, tabs, gq, gkv, dcqn, dckvn, dk)
    g0["wm"] = _tiled("l0_dwm", (1, D_MODEL // TM), [_tl(x, TM), _cw(dzm, 640)], [_out(D_MODEL, 640, F32, TM, 640)], _mmc(TN_))
    g0["ws"] = _tiled("l0_dws", (2, D_MODEL // TM), [_tl(x, TM), _cw(dzs, TN)], [_out(D_MODEL, 1024, F32, TM, TN)], _mmc(TN_))
    dx = _tiled("l0_dx", (2, nt), [_rb(dzm, tm), _rw(w["wm"], TN), _rb(dzs, tm), _rw(w["ws"], TN), _rbj(dy1, tm, TN)],
                [_out(T, D_MODEL, F32, tm, TN)], _mmc(NT, n_pairs=2, epilogue=resid))

    return sq_err, dx, _unprep_grads(g0), gs


def _me():
    return lax.axis_index("x"), lax.axis_index("y"), lax.axis_index("c")


def _hbm_call(name, kern, operands, out_shape, n_sems, extra_scratch=()):
    any_spec = pl.BlockSpec(memory_space=pl.ANY)
    return pl.pallas_call(
        kern, name=name, out_shape=out_shape, in_specs=[any_spec] * len(operands), out_specs=[any_spec] * len(out_shape),
        scratch_shapes=[pltpu.SemaphoreType.DMA((n_sems,)), pltpu.SemaphoreType.DMA((n_sems,)), *extra_scratch],
    )(*[_hbm(a) for a in operands])


ANY_SPEC = pl.BlockSpec(memory_space=pl.ANY)
HBM_SPEC = pl.BlockSpec(memory_space=pltpu.HBM)
SEM_SPEC = pl.BlockSpec(memory_space=pltpu.SEMAPHORE)
EFFECT = pltpu.SideEffectType.DATAFLOW_SIDE_EFFECTING


def _split_start(name, srcs, lands, n_sems, make_copies):
    n, m = len(srcs), len(lands)

    def body(*refs):
        for cp in make_copies(refs[:n], refs[n:n + m], refs[n + m], refs[n + m + 1]):
            cp.start()
        refs[-1][...] = jnp.zeros(refs[-1].shape, F32)

    out_shape = (pltpu.SemaphoreType.DMA((n_sems,)), pltpu.SemaphoreType.DMA((n_sems,)),
                 *[pltpu.HBM(a.shape, a.dtype) for a in (*srcs, *lands)], jax.ShapeDtypeStruct((8, 128), F32))
    res = pl.pallas_call(
        body, name=name, out_shape=out_shape, in_specs=[HBM_SPEC] * (n + m),
        out_specs=(SEM_SPEC, SEM_SPEC, *[HBM_SPEC] * (n + m), pl.BlockSpec(memory_space=pltpu.VMEM)),
        input_output_aliases={i: 2 + i for i in range(n + m)},
        compiler_params=pltpu.CompilerParams(has_side_effects=EFFECT),
    )(*[_hbm(a) for a in (*srcs, *lands)])
    return res[0], res[1], list(res[2:2 + n]), list(res[2 + n:2 + n + m]), res[-1]


def _split_wait(name, send_sems, recv_sems, srcs, lands, after, make_copies):
    n, m = len(srcs), len(lands)

    def body(*refs):
        for cp in make_copies(refs[:n], refs[n:n + m], refs[n + m], refs[n + m + 1]):
            cp.wait_send()
            cp.wait_recv()

    res = pl.pallas_call(
        body, name=name, out_shape=tuple(pltpu.HBM(a.shape, a.dtype) for a in (*srcs, *lands)),
        in_specs=[HBM_SPEC] * (n + m) + [SEM_SPEC, SEM_SPEC] + [ANY_SPEC] * len(after), out_specs=tuple([HBM_SPEC] * (n + m)),
        input_output_aliases={i: i for i in range(n + m)},
        compiler_params=pltpu.CompilerParams(has_side_effects=EFFECT),
    )(*srcs, *lands, send_sems, recv_sems, *after)
    return list(res[:n]), list(res[n:])


def _ag_first_copies(x_refs, out_refs, send_sems, recv_sems):
    x, y, c = _me()
    targets = [(x, y, 1 - c), (1 - x, y, c), (x, 1 - y, c), (1 - x, 1 - y, c)]
    return [pltpu.make_async_remote_copy(
        src_ref=x_refs[op], dst_ref=out_refs[op].at[4 * x + 2 * y + c], send_sem=send_sems.at[4 * op + k],
        recv_sem=recv_sems.at[4 * op + k], device_id=to, device_id_type=MESH)
        for op in range(len(x_refs)) for k, to in enumerate(targets)]


def _ag_second(gathered):
    n = len(gathered)

    def kern(*refs):
        in_refs, out_refs, (send_sems, recv_sems) = refs[:n], refs[n:2 * n], refs[2 * n:]
        x, y, c = _me()
        chips = [(1 - x, y), (x, 1 - y), (1 - x, 1 - y)]
        passed = [pltpu.make_async_remote_copy(
            src_ref=in_refs[op].at[4 * cx + 2 * cy + c], dst_ref=out_refs[op].at[4 * cx + 2 * cy + c],
            send_sem=send_sems.at[3 * op + j], recv_sem=recv_sems.at[3 * op + j], device_id=(x, y, 1 - c), device_id_type=MESH)
            for op in range(n) for j, (cx, cy) in enumerate(chips)]
        for cp in passed:
            cp.start()
        for cp in passed:
            cp.wait_send()
        for op in range(n):
            for j, (cx, cy) in enumerate(chips):
                slot = out_refs[op].at[4 * cx + 2 * cy + 1 - c]
                pltpu.make_async_remote_copy(src_ref=slot, dst_ref=slot, send_sem=send_sems.at[3 * op + j],
                                             recv_sem=recv_sems.at[3 * op + j], device_id=(x, y, c), device_id_type=MESH).wait_recv()

    return pl.pallas_call(
        kern, name="weights_all_gather_second", out_shape=[pltpu.HBM(g.shape, g.dtype) for g in gathered],
        in_specs=[ANY_SPEC] * n, out_specs=[ANY_SPEC] * n, input_output_aliases={i: i for i in range(n)},
        scratch_shapes=[pltpu.SemaphoreType.DMA((3 * n,)), pltpu.SemaphoreType.DMA((3 * n,))],
    )(*[_hbm(a) for a in gathered])


def _rs_sibling_copies(g_refs, out_refs, send_sems, recv_sems):
    x, y, c = _me()
    return [pltpu.make_async_remote_copy(
        src_ref=g_refs[op].at[k, 1 - c], dst_ref=out_refs[op].at[k], send_sem=send_sems.at[4 * op + k],
        recv_sem=recv_sems.at[4 * op + k], device_id=(x, y, 1 - c), device_id_type=MESH)
        for op in range(len(g_refs)) for k in range(4)]


def _rs_chip_copies(p_refs, out_refs, send_sems, recv_sems):
    x, y, c = _me()
    chips = [(1 - x, y), (x, 1 - y), (1 - x, 1 - y)]
    return [pltpu.make_async_remote_copy(
        src_ref=p_refs[op].at[2 * cx + cy], dst_ref=out_refs[op].at[j], send_sem=send_sems.at[3 * op + j],
        recv_sem=recv_sems.at[3 * op + j], device_id=(cx, cy, c), device_id_type=MESH)
        for op in range(len(p_refs)) for j, (cx, cy) in enumerate(chips)]


def _all_gather(shards, place=()):
    n, p = len(shards), len(place)

    def kern(*refs):
        x_refs, place_refs, out_refs, placed_refs = refs[:n], refs[n:n + p], refs[n + p:2 * n + p], refs[2 * n + p:2 * (n + p)]
        send_sems, recv_sems, local_sems = refs[2 * (n + p):]
        x, y, c = _me()
        me, sibling = (x, y, c), (x, y, 1 - c)
        chips = [(1 - x, y), (x, 1 - y), (1 - x, 1 - y)]

        def copy(op, k, block, to, own=False):
            slot = out_refs[op].at[4 * block[0] + 2 * block[1] + block[2]]
            return pltpu.make_async_remote_copy(
                src_ref=x_refs[op] if own else slot, dst_ref=slot, send_sem=send_sems.at[7 * op + k],
                recv_sem=recv_sems.at[7 * op + k], device_id=to, device_id_type=MESH)

        mine = [pltpu.make_async_copy(src, dst.at[4 * x + 2 * y + c], local_sems.at[k])
                for k, (src, dst) in enumerate(zip((*x_refs, *place_refs), (*out_refs, *placed_refs)))]
        for cp in mine:
            cp.start()
        first = []
        for op in range(n):
            first.append(copy(op, 0, me, sibling, own=True))
            first += [copy(op, 1 + j, me, (*chip, c), own=True) for j, chip in enumerate(chips)]
        for cp in first:
            cp.start()
        passed = []
        for j, chip in enumerate(chips):
            for op in range(n):
                copy(op, 1 + j, (*chip, c), me).wait_recv()
                passed.append(copy(op, 4 + j, (*chip, c), sibling))
                passed[-1].start()
        for op in range(n):
            copy(op, 0, sibling, me).wait_recv()
            for j, chip in enumerate(chips):
                copy(op, 4 + j, (*chip, 1 - c), me).wait_recv()
        for cp in first + passed:
            cp.wait_send()
        for cp in mine:
            cp.wait()

    out_shape = [pltpu.HBM((N_DEV, *s.shape), s.dtype) for s in (*shards, *place)]
    return _hbm_call("weights_all_gather", kern, [*shards, *place], out_shape, 7 * n, [pltpu.SemaphoreType.DMA((n + p,))])


def _rs_sibling(grads):
    n = len(grads)

    def kern(*refs):
        g_refs, out_refs, (send_sems, recv_sems) = refs[:n], refs[n:2 * n], refs[2 * n:]
        x, y, c = _me()
        copies = [pltpu.make_async_remote_copy(
            src_ref=g_refs[op].at[k, 1 - c], dst_ref=out_refs[op].at[k], send_sem=send_sems.at[4 * op + k],
            recv_sem=recv_sems.at[4 * op + k], device_id=(x, y, 1 - c), device_id_type=MESH) for op in range(n) for k in range(4)]
        for cp in copies:
            cp.start()
        for cp in copies:
            cp.wait()

    out_shape = [pltpu.HBM((4, *g.shape[2:]), g.dtype) for g in grads]
    return _hbm_call("grads_to_sibling", kern, grads, out_shape, 4 * n)


def _rs_chips(sums):
    n = len(sums)

    def kern(*refs):
        p_refs, out_refs, (send_sems, recv_sems) = refs[:n], refs[n:2 * n], refs[2 * n:]
        x, y, c = _me()
        chips = [(1 - x, y), (x, 1 - y), (1 - x, 1 - y)]
        copies = [pltpu.make_async_remote_copy(
            src_ref=p_refs[op].at[2 * cx + cy], dst_ref=out_refs[op].at[j], send_sem=send_sems.at[3 * op + j],
            recv_sem=recv_sems.at[3 * op + j], device_id=(cx, cy, c), device_id_type=MESH)
            for op in range(n) for j, (cx, cy) in enumerate(chips)]
        for cp in copies:
            cp.start()
        for cp in copies:
            cp.wait()

    out_shape = [pltpu.HBM((3, *p.shape[1:]), p.dtype) for p in sums]
    return _hbm_call("grads_between_chips", kern, sums, out_shape, 3 * n)


def _row_tile(r):
    return r if r <= 256 else 256


def _chip_sum(name, g, from_sibling, core):
    _, _, R, W = g.shape
    tr = _row_tile(R)

    def kern(core_ref, g_ref, s_ref, o_ref):
        o_ref[...] = (g_ref[...] + s_ref[...]).astype(BF16)

    return pl.pallas_call(
        kern, name=name, out_shape=pltpu.HBM((4, R, W), BF16),
        grid_spec=pltpu.PrefetchScalarGridSpec(
            num_scalar_prefetch=1, grid=(4, R // tr),
            in_specs=[pl.BlockSpec((None, None, tr, W), lambda k, i, core: (k, core[0], i, 0)),
                      pl.BlockSpec((None, tr, W), lambda k, i, core: (k, i, 0))],
            out_specs=pl.BlockSpec((None, tr, W), lambda k, i, core: (k, i, 0))),
        compiler_params=pltpu.CompilerParams(dimension_semantics=("parallel", "parallel"), vmem_limit_bytes=_vmem(3 * tr * W * 4)),
    )(core, _hbm(g), _hbm(from_sibling))


def _adamw(w, g, m, v):
    m = ADAM_B1 * m + (1.0 - ADAM_B1) * g
    v = ADAM_B2 * v + (1.0 - ADAM_B2) * (g * g)
    m_hat = m / (1.0 - ADAM_B1 ** ADAM_STEP)
    v_hat = v / (1.0 - ADAM_B2 ** ADAM_STEP)
    return -ADAM_LR * (m_hat / (jnp.sqrt(v_hat) + ADAM_EPS) + ADAM_WD * w), m, v


def _finish_sharded(name, layers, w, m, v, where):
    nl, R, W = w.shape
    tr = _row_tile(R)

    def kern(where_ref, *refs):
        w_ref, m_ref, v_ref, go_ref, d_ref, mo_ref, vo_ref = refs[3 * nl:]
        for l in range(nl):
            g_ref, s_ref, c_ref = refs[3 * l:3 * l + 3]
            grad = g_ref[...] + s_ref[...]
            for j in range(3):
                grad = grad + c_ref[j].astype(F32)
            go_ref[l] = grad
            d_ref[l], mo_ref[l], vo_ref[l] = _adamw(w_ref[l], grad, m_ref[l], v_ref[l])

    row = pl.BlockSpec((nl, tr, W), lambda i, wh: (0, i, 0))
    in_specs, args = [], []
    for g, s, c in layers:
        in_specs += [pl.BlockSpec((None, None, tr, W), lambda i, wh: (wh[0], wh[1], i, 0)),
                     pl.BlockSpec((None, tr, W), lambda i, wh: (wh[0], i, 0)),
                     pl.BlockSpec((3, tr, W), lambda i, wh: (0, i, 0))]
        args += [g, s, c]
    return pl.pallas_call(
        kern, name=name, out_shape=[pltpu.HBM((nl, R, W), F32)] * 4,
        grid_spec=pltpu.PrefetchScalarGridSpec(num_scalar_prefetch=1, grid=(R // tr,), in_specs=in_specs + [row, row, row],
                                               out_specs=[row, row, row, row]),
        compiler_params=pltpu.CompilerParams(dimension_semantics=("parallel",), vmem_limit_bytes=_vmem(nl * 11 * tr * W * 4)),
    )(where, *[_hbm(a) for a in (*args, w, m, v)])


SMALL_PLACE = (("mla_gq", 0, 0, 1, 256), ("mla_gkv", 0, 256, 1, 256), ("sgu_ln_g", 0, 512, 1, 512), ("sgu_ln_b", 1, 0, 1, 512),
               ("hg_lb", 2, 0, 2, 1024), ("ln1_g", 4, 0, 2, 1024), ("ln1_b", 6, 0, 2, 1024), ("sgu_b", 8, 0, 4, 128),
               ("ln2_g", 12, 0, 2, 1024), ("ln2_b", 14, 0, 2, 1024), ("hg_gnorm", 16, 0, 1, 1024))
SMALL_BUF_ROWS = 24


def _small_reduce_adamw(gs, given):
    pieces = [(gs["mla_gq"], 0, 0), (gs["mla_gkv"], 0, 256), (gs["sgu_ln_g"], 0, 512), (gs["sgu_ln_b"], 1, 0), (gs["hg_lb"], 2, 0),
              (gs["ln1_g0"], 4, 0), (gs["ln1_g1"], 5, 0), (gs["ln1_b0"], 6, 0), (gs["ln1_b1"], 7, 0), (gs["sgu_b"], 8, 0),
              (gs["ln2_g0"], 12, 0), (gs["ln2_g1"], 13, 0), (gs["ln2_b0"], 14, 0), (gs["ln2_b1"], 15, 0), (gs["hg_gnorm"], 16, 0)]
    names = [p[0] for p in SMALL_PLACE] + ["sgu_w"]
    n_p, n_names = len(pieces), len(names)
    wmv = [given[pre + name] for name in names for pre in ("", "m_", "v_")]

    def kern(*refs):
        piece_refs, gw_ref = refs[:n_p], refs[n_p]
        wmv_refs = refs[n_p + 1:n_p + 1 + 3 * n_names]
        out_refs = refs[n_p + 1 + 3 * n_names:n_p + 1 + 7 * n_names]
        buf_a, buf_b, send_sems, recv_sems = refs[n_p + 1 + 7 * n_names:]
        px, py, pc = _me()
        me = 4 * px + 2 * py + pc
        mine_a, mine_b = buf_a.at[me], buf_b.at[me]
        mine_a[...] = jnp.zeros(mine_a.shape, F32)
        for ref, (_, r, l0) in zip(piece_refs, pieces):
            mine_a[r:r + ref.shape[0], l0:l0 + ref.shape[1]] = ref[...]
        mine_b[...] = gw_ref[...]
        copies = []
        for r in range(1, N_DEV):
            peer = (px ^ (r >> 2), py ^ ((r >> 1) & 1), pc ^ (r & 1))
            for k, mine in enumerate((mine_a, mine_b)):
                copies.append(pltpu.make_async_remote_copy(
                    src_ref=mine, dst_ref=mine, send_sem=send_sems.at[2 * (r - 1) + k], recv_sem=recv_sems.at[2 * (r - 1) + k],
                    device_id=peer, device_id_type=MESH))
        for cp in copies:
            cp.start()
        for r in range(1, N_DEV):
            for k, buf in enumerate((buf_a, buf_b)):
                theirs = buf.at[me ^ r]
                pltpu.make_async_remote_copy(
                    src_ref=theirs, dst_ref=theirs, send_sem=send_sems.at[2 * (r - 1) + k], recv_sem=recv_sems.at[2 * (r - 1) + k],
                    device_id=(px, py, pc), device_id_type=MESH).wait_recv()
        for cp in copies:
            cp.wait_send()
        sum_a, sum_b = buf_a[0], buf_b[0]
        for d in range(1, N_DEV):
            sum_a, sum_b = sum_a + buf_a[d], sum_b + buf_b[d]

        def own_block(full):
            acc = full[:, 0:128]
            for b in range(1, N_DEV):
                acc = jnp.where(me == b, full[:, b * 128:(b + 1) * 128], acc)
            return acc

        for idx, name in enumerate(names):
            w_ref, m_ref, v_ref = wmv_refs[3 * idx:3 * idx + 3]
            if name == "sgu_w":
                grad = sum_b[None]
            else:
                _, r, l0, nr, nl = SMALL_PLACE[idx]
                grad = sum_a[r:r + nr, l0:l0 + nl]
                if name == "hg_gnorm":
                    grad = own_block(grad)
                if name == "sgu_b":
                    grad = grad[None]
            res = (grad, *_adamw(w_ref[...], grad, m_ref[...], v_ref[...]))
            for o_ref, val in zip(out_refs[4 * idx:4 * idx + 4], res):
                o_ref[...] = val

    vmem = pl.BlockSpec(memory_space=pltpu.VMEM)
    operands = [p[0] for p in pieces] + [gs["sgu_w"]] + wmv
    out_shape = [jax.ShapeDtypeStruct(given[name].shape, F32) for name in names for _ in range(4)]
    res = pl.pallas_call(
        kern, name="small_all_reduce_adamw", out_shape=out_shape, in_specs=[vmem] * len(operands), out_specs=[vmem] * len(out_shape),
        scratch_shapes=[pltpu.VMEM((N_DEV, SMALL_BUF_ROWS, D_MODEL), F32), pltpu.VMEM((N_DEV, SGU_G, 128, 128), F32),
                        pltpu.SemaphoreType.DMA((14,)), pltpu.SemaphoreType.DMA((14,))],
    )(*operands)
    return {name: res[4 * idx:4 * idx + 4] for idx, name in enumerate(names)}


class _Exchange:
    def __init__(self, given):
        self.given = given
        px, py, pc = _me()
        self.core = pc.reshape(1).astype(jnp.int32)
        self.where = jnp.stack([2 * px + py, pc]).astype(jnp.int32)
        self.state, self.layers = {}, {}

    def late_shards(self):
        g = self.given
        return [a.astype(BF16) for a in (g["w_in_o"][0], g["w_out_o"][0], g["w_ff1"][0], g["w_ff1"][1], g["w_ff2"][0], g["w_ff2"][1])]

    def start_weights(self, shards, lands):
        self.weights = _split_start("weights_first_start", shards, lands, 4 * len(shards), _ag_first_copies)
        self.first_token = self.weights[4]

    def weights_ready(self, after):
        send_sems, recv_sems, shards, lands, _ = self.weights
        _, lands = _split_wait("weights_first_wait", send_sems, recv_sems, shards, lands, [after], _ag_first_copies)
        got = _ag_second(lands)
        return dict(w_in_o=got[0], w_out_o=got[1], w_ff1=[got[2], got[3]], w_ff2=[got[4], got[5]])

    def grads_start(self, tag, grads):
        blocks = [g.reshape(4, 2, *g.shape[1:]) for g in grads]
        lands = [lax.empty((4, *b.shape[2:]), F32) for b in blocks]
        self.state[tag] = _split_start(f"grads_{tag}_sibling_start", blocks, lands, 4 * len(blocks), _rs_sibling_copies)
        return self.state[tag][4]

    def grads_middle(self, tag, after):
        send_sems, recv_sems, blocks, lands, _ = self.state[tag]
        blocks, from_sibling = _split_wait(f"grads_{tag}_sibling_wait", send_sems, recv_sems, blocks, lands, [after], _rs_sibling_copies)
        sums = [_chip_sum(f"grads_{tag}_chip_sum_{k}", b, s, self.core) for k, (b, s) in enumerate(zip(blocks, from_sibling))]
        lands = [lax.empty((3, *p.shape[1:]), BF16) for p in sums]
        self.state[tag] = (blocks, from_sibling, _split_start(f"grads_{tag}_chips_start", sums, lands, 3 * len(sums), _rs_chip_copies))
        return self.state[tag][2][4]

    def grads_end(self, tag, after):
        blocks, from_sibling, (send_sems, recv_sems, sums, lands, _) = self.state[tag]
        _, from_chips = _split_wait(f"grads_{tag}_chips_wait", send_sems, recv_sems, sums, lands, [after], _rs_chip_copies)
        self.layers[tag] = list(zip(blocks, from_sibling, from_chips))


SHARDED = ("w_in_e", "w_qb", "w_kvb", "w_out_e", "w_in_o", "w_out_o", "w_ff1", "w_ff2")


def kernel(x, positions, w_in_e, mla_gq, mla_gkv, w_qb, w_kvb, sgu_ln_g, sgu_ln_b, sgu_w, sgu_b, w_out_e, w_in_o, hg_lb, hg_gnorm, w_out_o, ln1_g, ln1_b, w_ff1, w_ff2, ln2_g, ln2_b, loss_target, m_w_in_e, m_mla_gq, m_mla_gkv, m_w_qb, m_w_kvb, m_sgu_ln_g, m_sgu_ln_b, m_sgu_w, m_sgu_b, m_w_out_e, m_w_in_o, m_hg_lb, m_hg_gnorm, m_w_out_o, m_ln1_g, m_ln1_b, m_w_ff1, m_w_ff2, m_ln2_g, m_ln2_b, v_w_in_e, v_mla_gq, v_mla_gkv, v_w_qb, v_w_kvb, v_sgu_ln_g, v_sgu_ln_b, v_sgu_w, v_sgu_b, v_w_out_e, v_w_in_o, v_hg_lb, v_hg_gnorm, v_w_out_o, v_ln1_g, v_ln1_b, v_w_ff1, v_w_ff2, v_ln2_g, v_ln2_b):
    given = dict(locals())
    ex = _Exchange(given)

    names = ["w_in_e", "w_qb", "w_kvb", "w_out_e"]
    late = ex.late_shards()
    got = _all_gather([given[n][0].astype(BF16) for n in names] + [hg_gnorm], place=late)
    ex.start_weights(late, got[5:])
    gw = dict(zip(names, got[:4]))
    small_names = ["mla_gq", "mla_gkv", "sgu_ln_g", "sgu_ln_b", "sgu_w", "sgu_b", "hg_lb", "ln1_g", "ln1_b", "ln2_g", "ln2_b"]
    sp = {n: given[n] for n in small_names}
    sp["hg_gnorm"] = got[4].reshape(1, D_MODEL)

    sq_err, dx, grads, gs = _local_step(x[0], positions[0], loss_target[0], gw, sp, ex)
    loss = lax.psum(0.5 * jnp.sum(sq_err) / D_MODEL, ("x", "y", "c"))

    blocks = [grads[n].reshape(4, 2, *grads[n].shape[1:]) for n in names]
    from_sibling = _rs_sibling(blocks)
    chip_sums = [_chip_sum(f"grads_l0_chip_sum_{k}", b, s, ex.core) for k, (b, s) in enumerate(zip(blocks, from_sibling))]
    from_chips = _rs_chips(chip_sums)
    per_weight = dict(zip(names, [[l] for l in zip(blocks, from_sibling, from_chips)]))
    l1, l0m = ex.layers["l1"], ex.layers["l0m"]
    per_weight.update(w_ff1=[l0m[0], l1[0]], w_ff2=[l0m[1], l1[1]], w_in_o=[l1[2]], w_out_o=[l1[3]])
    results = {n: _finish_sharded(f"finish_{n}", per_weight[n], given[n], given["m_" + n], given["v_" + n], ex.where) for n in SHARDED}

    results.update(_small_reduce_adamw(gs, given))

    order = ["w_in_e", "mla_gq", "mla_gkv", "w_qb", "w_kvb", "sgu_ln_g", "sgu_ln_b", "sgu_w", "sgu_b", "w_out_e", "w_in_o",
             "hg_lb", "hg_gnorm", "w_out_o", "ln1_g", "ln1_b", "w_ff1", "w_ff2", "ln2_g", "ln2_b"]
    return (loss, dx[None], *[results[name][kind] for kind in range(4) for name in order])
```

```python
import functools
import math

import jax
import jax.numpy as jnp
import numpy as np
from jax import lax
from jax.experimental import pallas as pl
from jax.experimental.pallas import tpu as pltpu

F32 = jnp.float32
BF16 = jnp.bfloat16
MESH = pl.DeviceIdType.MESH
HIGHEST = lax.Precision.HIGHEST

D_MODEL = 1024
D_FF = 4096
N_DEV = 8
HEADS = 8
HEAD_W = 128
MLA_NOPE = 64
MLA_ROPE = 32
MLA_V = 64
MLA_LORA = 256
MLA_SCALE = (MLA_NOPE + MLA_ROPE) ** -0.5
ROPE_BASE = 10000.0
SGU_DIM = 512
SGU_G = 4
SGU_CHUNK = 128
HG_CHUNK = 64
ALPHA = (2 * 2) ** 0.25
EPS = 1e-5
ADAM_LR, ADAM_B1, ADAM_B2, ADAM_EPS, ADAM_WD, ADAM_STEP = 0.001, 0.9, 0.999, 1e-08, 0.01, 10

VMEM_CAP_V7X = 56 * 2**20
VMEM_SLACK = 12 * 2**20
TM = 512
TN = 512


def _vmem(block_bytes):
    return int(min(VMEM_CAP_V7X, 2 * block_bytes + VMEM_SLACK))


def _hbm(a):
    return pltpu.with_memory_space_constraint(a, pltpu.HBM)


def _nbytes(shape, dtype):
    return int(np.prod([d for d in shape if d is not None])) * jnp.dtype(dtype).itemsize


def _sig(x):
    return 1.0 / (1.0 + jnp.exp(-x))


def _gelu(x):
    c = math.sqrt(2.0 / math.pi)
    t = jnp.tanh(c * (x + 0.044715 * x * x * x))
    return 0.5 * x * (1.0 + t), t


def _gelu_grad(x, t):
    c = math.sqrt(2.0 / math.pi)
    return 0.5 * (1.0 + t) + 0.5 * x * (1.0 - t * t) * c * (1.0 + 3 * 0.044715 * x * x)


def _dot(a, b, dims, precision=None):
    return lax.dot_general(a, b, (dims, ((), ())), preferred_element_type=F32, precision=precision)


NN = ((1,), (0,))
NT = ((1,), (1,))
TN_ = ((0,), (0,))


def _deps(deps):
    return [d for d in deps if d is not None]


def _tiled(name, grid, ins, outs, compute, direct=False, deps=()):
    n_in, deps = len(ins), _deps(deps)
    n_skip = n_in + len(deps)

    def kern(*refs):
        if direct:
            compute(refs[:n_in], refs[n_skip:])
            return
        for o_ref, r in zip(refs[n_skip:], compute(*refs[:n_in])):
            o_ref[...] = r.astype(o_ref.dtype).reshape(o_ref.shape)

    swap = lambda f: (lambda j, i: f(i, j))
    nbytes = sum(_nbytes(blk, a.dtype) for a, blk, _ in ins) + sum(_nbytes(blk, dt) + _nbytes(blk, F32) for _, dt, blk, _ in outs)
    res = pl.pallas_call(
        kern, name=name, grid=grid,
        in_specs=[pl.BlockSpec(blk, swap(f)) for _, blk, f in ins] + [ANY_SPEC] * len(deps),
        out_specs=[pl.BlockSpec(blk, swap(f)) for _, _, blk, f in outs],
        out_shape=[pltpu.HBM(shape, dt) for shape, dt, _, _ in outs],
        compiler_params=pltpu.CompilerParams(dimension_semantics=("parallel", "parallel"), vmem_limit_bytes=_vmem(nbytes)),
    )(*[_hbm(a) for a, _, _ in ins], *deps)
    return res if len(res) > 1 else res[0]


def _rb(a, tm, w=None, cb=0):
    return (a, (tm, a.shape[1] if w is None else w), lambda i, j: (i, cb))


def _rbj(a, tm, tn):
    return (a, (tm, tn), lambda i, j: (i, j))


def _cw(b, tn):
    return (b, (b.shape[0], tn), lambda i, j: (0, j))


def _rw(b, tn):
    return (b, (tn, b.shape[1]), lambda i, j: (j, 0))


def _tl(a, tm):
    return (a, (a.shape[0], tm), lambda i, j: (0, i))


def _gcw(g):
    return (g, (None, g.shape[1], g.shape[2]), lambda i, j: (j, 0, 0))


def _grw(g, tn):
    return (g, (N_DEV, tn, g.shape[2]), lambda i, j: (0, j, 0))


def _out(m, n, dtype, tm, tn):
    return ((m, n), dtype, (tm, tn), lambda i, j: (i, j))


def _out_dev(k, n, tm):
    return ((N_DEV, k, n), F32, (None, tm, n), lambda i, j: (j, i, 0))


def _mmc(dims, n_pairs=1, epilogue=None):
    def compute(*refs):
        acc = None
        for k in range(n_pairs):
            d = _dot(refs[2 * k][...].astype(BF16), refs[2 * k + 1][...].astype(BF16), dims)
            acc = d if acc is None else acc + d
        ext = [r[...] for r in refs[2 * n_pairs:]]
        return epilogue(acc, *ext) if epilogue is not None else (acc,)

    return compute


def _res(w):
    return (w, w.shape, functools.partial(lambda i, j, nd: (0,) * nd, nd=w.ndim))


def _mmc_blocks(nblk, dims, rhs_block, epilogue=None):
    def compute(in_refs, out_refs):
        a = in_refs[0][...].astype(BF16)
        for d in range(nblk):
            acc = _dot(a, rhs_block(in_refs[1], d).astype(BF16), dims)
            n = acc.shape[1]
            ext = [r[:, d * n:(d + 1) * n] for r in in_refs[2:]]
            res = epilogue(acc, *ext) if epilogue is not None else (acc,)
            for o_ref, r in zip(out_refs, res):
                o_ref[:, d * n:(d + 1) * n] = r.astype(o_ref.dtype)

    return compute


def _mmc_dev(epilogue=None):
    def compute(a_ref, b_ref, *ext_refs):
        n = b_ref.shape[2]
        acc = None
        for d in range(N_DEV):
            t = _dot(a_ref[:, d * n:(d + 1) * n].astype(BF16), b_ref[d].astype(BF16), NT)
            acc = t if acc is None else acc + t
        ext = [r[...] for r in ext_refs]
        return epilogue(acc, *ext) if epilogue is not None else (acc,)

    return compute


def _rowwise(name, body, rows, consts, out_rows, out_accs=(), tr=512, deps=()):
    T = rows[0][0].shape[0]
    tr = min(tr, T)
    deps = _deps(deps)
    nr, ncn, no, nd = len(rows), len(consts), len(out_rows), len(deps)

    def kern(*refs):
        accs = refs[nr + ncn + nd + no:]
        if accs:
            @pl.when(pl.program_id(0) == 0)
            def _():
                for a in accs:
                    a[...] = jnp.zeros(a.shape, a.dtype)
        body(refs[:nr], refs[nr:nr + ncn], refs[nr + ncn + nd:nr + ncn + nd + no], accs)

    in_specs = [pl.BlockSpec((tr, w), functools.partial(lambda i, cb: (i, cb), cb=cb)) for _, w, cb in rows]
    in_specs += [pl.BlockSpec(c.shape, functools.partial(lambda i, nd: (0,) * nd, nd=c.ndim)) for c in consts]
    in_specs += [ANY_SPEC] * nd
    out_specs = [pl.BlockSpec((tr, w), lambda i: (i, 0)) for w, _ in out_rows]
    out_specs += [pl.BlockSpec(s, functools.partial(lambda i, nd: (0,) * nd, nd=len(s))) for s, _ in out_accs]
    out_shape = [pltpu.HBM((T, w), dt) for w, dt in out_rows]
    out_shape += [pltpu.HBM(s, dt) for s, dt in out_accs]
    nbytes = sum(_nbytes((tr, w), a.dtype) for a, w, _ in rows) + sum(_nbytes(c.shape, c.dtype) for c in consts)
    nbytes += sum(_nbytes((tr, w), dt) for w, dt in out_rows) + sum(_nbytes(s, dt) for s, dt in out_accs)
    res = pl.pallas_call(
        kern, name=name, grid=(T // tr,), in_specs=in_specs, out_specs=out_specs, out_shape=out_shape,
        compiler_params=pltpu.CompilerParams(dimension_semantics=("arbitrary",), vmem_limit_bytes=_vmem(nbytes)),
    )(*[_hbm(a) for a, _, _ in rows], *[_hbm(c) for c in consts], *deps)
    return res if len(res) > 1 else res[0]


def _full(a):
    return (a, a.shape[1], 0)


def _ln_stats(y):
    mu = jnp.mean(y, axis=-1, keepdims=True)
    yc = y - mu
    r = lax.rsqrt(jnp.mean(yc * yc, axis=-1, keepdims=True) + EPS)
    return yc * r, r


def _ln_fwd(name, h_in, mix, g, b, layer):
    def body(rows, consts, outs, accs):
        y = ALPHA * rows[0][...] + rows[1][...]
        xh, _ = _ln_stats(y)
        h = xh * consts[0][layer:layer + 1, :] + consts[1][layer:layer + 1, :]
        outs[0][...] = y
        outs[1][...] = h
        outs[2][...] = h.astype(BF16)

    return _rowwise(name, body, [_full(h_in), _full(mix)], [g, b], [(D_MODEL, F32), (D_MODEL, F32), (D_MODEL, BF16)], tr=256)


def _ln_loss(name, h_in, mix, g, b, layer, target):
    def body(rows, consts, outs, accs):
        y = ALPHA * rows[0][...] + rows[1][...]
        xh, _ = _ln_stats(y)
        err = xh * consts[0][layer:layer + 1, :] + consts[1][layer:layer + 1, :] - rows[2][...]
        outs[0][...] = y
        outs[1][...] = err * (1.0 / D_MODEL)
        accs[0][...] += jnp.sum(err * err, axis=0, keepdims=True)

    return _rowwise(name, body, [_full(h_in), _full(mix), _full(target)], [g, b], [(D_MODEL, F32), (D_MODEL, F32)],
                    [((1, D_MODEL), F32)], tr=256)


def _ln_bwd(name, y, dh, g, layer):
    def body(rows, consts, outs, accs):
        xh, r = _ln_stats(rows[0][...])
        d = rows[1][...]
        accs[0][...] += jnp.sum(d * xh, axis=0, keepdims=True)
        accs[1][...] += jnp.sum(d, axis=0, keepdims=True)
        dx = d * consts[0][layer:layer + 1, :]
        dy = r * (dx - jnp.mean(dx, axis=-1, keepdims=True) - xh * jnp.mean(dx * xh, axis=-1, keepdims=True))
        outs[0][...] = dy
        outs[1][...] = dy.astype(BF16)

    return _rowwise(name, body, [_full(y), _full(dh)], [g], [(D_MODEL, F32), (D_MODEL, BF16)],
                    [((1, D_MODEL), F32), ((1, D_MODEL), F32)], tr=256)


def _relu2_epilogue(acc):
    a = jnp.maximum(acc, 0.0)
    return acc, a * a


def _mlp_fwd(tag, h_bf, w1, w2):
    T = h_bf.shape[0]
    tm = min(TM, T)
    a, act = _tiled(f"{tag}_ff1", (1, T // tm), [_rb(h_bf, tm), _res(w1)],
                    [_out(T, D_FF, BF16, tm, D_FF), _out(T, D_FF, BF16, tm, D_FF)],
                    _mmc_blocks(N_DEV, NN, lambda w, d: w[d], epilogue=_relu2_epilogue), direct=True)
    ff = _tiled(f"{tag}_ff2", (1, T // tm), [_rb(act, tm), _res(w2.reshape(D_FF, D_MODEL))],
                [_out(T, D_MODEL, F32, tm, D_MODEL)], _mmc(NN))
    return a, act, ff


def _mlp_bwd_w(tag, h_bf, a, act, dff_bf, w2, deps=()):
    T = h_bf.shape[0]
    tm = min(TM, T)
    da = _tiled(f"{tag}_dact", (1, T // tm), [_rb(dff_bf, tm), _res(w2), _rb(a, tm)], [_out(T, D_FF, BF16, tm, D_FF)],
                _mmc_blocks(N_DEV, NT, lambda w, d: w[d], epilogue=lambda acc, a_t: (acc * 2.0 * jnp.maximum(a_t.astype(F32), 0.0),)),
                direct=True, deps=deps)
    dw2 = _tiled(f"{tag}_dw2", (1, D_FF // TM), [_tl(act, TM), _res(dff_bf)],
                 [_out(D_FF, D_MODEL, F32, TM, D_MODEL)], _mmc(TN_)).reshape(N_DEV, D_FF // N_DEV, D_MODEL)
    dw1 = _tiled(f"{tag}_dw1", (N_DEV, 1), [_res(h_bf), _cw(da, TN)], [_out_dev(D_MODEL, TN, D_MODEL)], _mmc(TN_))
    return da, dw1, dw2


def _mlp_bwd_h(tag, da, w1, dy, deps=()):
    T = da.shape[0]
    tm = min(TM, T)
    return _tiled(f"{tag}_dh", (1, T // tm), [_rb(da, tm), _res(w1), _rb(dy, tm)],
                  [_out(T, D_MODEL, F32, tm, D_MODEL), _out(T, D_MODEL, BF16, tm, D_MODEL)],
                  _mmc_dev(epilogue=lambda acc, dy_t: (acc + ALPHA * dy_t,) * 2), deps=deps)


def _rope_tables(positions_col, invf_lane):
    def body(rows, consts, outs, accs):
        ang = rows[0][...].astype(F32) * consts[0][...]
        c, s = jnp.cos(ang), jnp.sin(ang)
        lane = lax.broadcasted_iota(jnp.int32, ang.shape, 1)
        outs[0][...] = jnp.where(lane < 64, 1.0, jnp.where(lane < 96, c, 0.0))
        outs[1][...] = jnp.where((lane >= 64) & (lane < 80), -s, 0.0)
        outs[2][...] = jnp.where((lane >= 80) & (lane < 96), s, 0.0)

    return _rowwise("rope_tables", body, [_full(positions_col)], [invf_lane], [(HEAD_W, F32)] * 3)


def _rope(x, c, s1, s2):
    return x * c + pltpu.roll(x, 112, 1) * s1 + pltpu.roll(x, 16, 1) * s2


def _rope_t(dx, c, s1, s2):
    return dx * c + pltpu.roll(dx * s1, 16, 1) + pltpu.roll(dx * s2, 112, 1)


def _rms(c):
    r = lax.rsqrt(jnp.mean(c * c, axis=-1, keepdims=True) + EPS)
    return c * r, r


def _mla_pre(zm, tabs, gq, gkv):
    def body(rows, consts, outs, accs):
        cq, _ = _rms(rows[0][...])
        ckv, _ = _rms(rows[1][...])
        outs[0][...] = (cq * consts[0][...]).astype(BF16)
        outs[1][...] = (ckv * consts[1][...]).astype(BF16)
        outs[2][...] = _rope(rows[2][...], rows[3][...], rows[4][...], rows[5][...])

    rows = [(zm, 256, 0), (zm, 256, 1), (zm, 128, 4)] + [_full(t) for t in tabs]
    return _rowwise("mla_pre", body, rows, [gq, gkv], [(256, BF16), (256, BF16), (HEAD_W, F32)])


def _mla_pre_bwd(zm, tabs, gq, gkv, dcqn, dckvn, dk):
    def body(rows, consts, outs, accs):
        res = []
        for k in range(2):
            ch, r = _rms(rows[k][...])
            d = rows[5 + k][...]
            accs[k][...] += jnp.sum(d * ch, axis=0, keepdims=True)
            dc = d * consts[k][...]
            res.append(r * (dc - ch * jnp.mean(dc * ch, axis=-1, keepdims=True)))
        dks = rows[7][:, 0:HEAD_W]
        for h in range(1, HEADS):
            dks = dks + rows[7][:, h * HEAD_W:(h + 1) * HEAD_W]
        lane = lax.broadcasted_iota(jnp.int32, dks.shape, 1)
        dks = jnp.where((lane >= 64) & (lane < 96), dks, 0.0)
        dkr = _rope_t(dks, rows[2][...], rows[3][...], rows[4][...])
        outs[0][:, 0:256] = res[0].astype(BF16)
        outs[0][:, 256:512] = res[1].astype(BF16)
        outs[0][:, 512:640] = dkr.astype(BF16)

    rows = [(zm, 256, 0), (zm, 256, 1)] + [_full(t) for t in tabs] + [_full(dcqn), _full(dckvn), _full(dk)]
    return _rowwise("mla_pre_bwd", body, rows, [gq, gkv], [(640, BF16)], [((1, 256), F32), ((1, 256), F32)])


def _rope_heads(x, c, s1, s2, fn):
    return jnp.concatenate([fn(x[:, h * HEAD_W:(h + 1) * HEAD_W], c, s1, s2) for h in range(HEADS)], axis=1)


def _unrope_heads(dq, tabs):
    def body(rows, consts, outs, accs):
        outs[0][...] = _rope_heads(rows[0][...], rows[1][...], rows[2][...], rows[3][...], _rope_t).astype(BF16)

    return _rowwise("l0_dq_rope", body, [_full(dq)] + [_full(t) for t in tabs], [], [(HEADS * HEAD_W, BF16)])


def _attn_block(T):
    return min(1024, T)


def _attn_fwd(q, k, v):
    T = q.shape[0]
    BQ = _attn_block(T)
    nq = T // BQ

    def kern(q_ref, k_ref, v_ref, o_ref, lse_ref):
        def step(i, j, carry, masked):
            m, l, acc = carry
            qb = q_ref[pl.ds(pl.multiple_of(i * BQ, BQ), BQ), :]
            kb = k_ref[pl.ds(pl.multiple_of(j * BQ, BQ), BQ), :]
            vb = v_ref[pl.ds(pl.multiple_of(j * BQ, BQ), BQ), :]
            s = _dot(qb, kb, NT) * MLA_SCALE
            if masked:
                row = lax.broadcasted_iota(jnp.int32, s.shape, 0)
                col = lax.broadcasted_iota(jnp.int32, s.shape, 1)
                s = jnp.where(col <= row, s, -1e30)
            m_new = jnp.maximum(m, jnp.max(s, axis=-1, keepdims=True))
            p = jnp.exp(s - m_new)
            a = jnp.exp(m - m_new)
            l = a * l + jnp.sum(p, axis=-1, keepdims=True)
            acc = a * acc + _dot(p.astype(BF16), vb, NN)
            return m_new, l, acc

        def qloop(i, _):
            init = (jnp.full((BQ, 1), -1e30, F32), jnp.zeros((BQ, 1), F32), jnp.zeros((BQ, HEAD_W), F32))
            carry = lax.fori_loop(0, i, lambda j, c: step(i, j, c, False), init)
            m, l, acc = step(i, i, carry, True)
            rows = pl.ds(pl.multiple_of(i * BQ, BQ), BQ)
            o_ref[rows, :] = acc / l
            lse_ref[0, rows, :] = m + jnp.log(l)
            return 0

        lax.fori_loop(0, nq, qloop, 0)

    head = pl.BlockSpec((T, HEAD_W), lambda h: (0, h))
    nbytes = 3 * _nbytes((T, HEAD_W), BF16) + _nbytes((T, HEAD_W), F32) + _nbytes((T, 128), F32)
    return pl.pallas_call(
        kern, name="attn_fwd", grid=(HEADS,), in_specs=[head, head, head],
        out_specs=[head, pl.BlockSpec((1, T, 1), lambda h: (h, 0, 0))],
        out_shape=[pltpu.HBM((T, HEADS * HEAD_W), F32), pltpu.HBM((HEADS, T, 1), F32)],
        compiler_params=pltpu.CompilerParams(dimension_semantics=("parallel",), vmem_limit_bytes=_vmem(nbytes)),
    )(_hbm(q), _hbm(k), _hbm(v))


def _attn_bwd(q, k, v, o, lse, dcat):
    T = q.shape[0]
    BQ = _attn_block(T)
    nq = T // BQ

    def kern(q_ref, k_ref, v_ref, o_ref, lse_ref, do_ref, dq_ref, dk_ref, dv_ref, dd_ref):
        dq_ref[...] = jnp.zeros(dq_ref.shape, F32)

        def dloop(i, _):
            rows = pl.ds(pl.multiple_of(i * BQ, BQ), BQ)
            dd_ref[rows, :] = jnp.sum(do_ref[rows, :].astype(F32) * o_ref[rows, :], axis=-1, keepdims=True)
            return 0

        lax.fori_loop(0, nq, dloop, 0)

        def step(j, i, carry, masked):
            dk_acc, dv_acc = carry
            rq = pl.ds(pl.multiple_of(i * BQ, BQ), BQ)
            rk = pl.ds(pl.multiple_of(j * BQ, BQ), BQ)
            qb, kb, vb, dob = q_ref[rq, :], k_ref[rk, :], v_ref[rk, :], do_ref[rq, :]
            s = _dot(qb, kb, NT) * MLA_SCALE
            p = jnp.exp(s - lse_ref[0, rq, :])
            if masked:
                row = lax.broadcasted_iota(jnp.int32, s.shape, 0)
                col = lax.broadcasted_iota(jnp.int32, s.shape, 1)
                p = jnp.where(col <= row, p, 0.0)
            dp = _dot(dob, vb, NT)
            ds = (p * (dp - dd_ref[rq, :]) * MLA_SCALE).astype(BF16)
            dv_acc = dv_acc + _dot(p.astype(BF16), dob, TN_)
            dk_acc = dk_acc + _dot(ds, qb, TN_)
            dq_ref[rq, :] += _dot(ds, kb, NN)
            return dk_acc, dv_acc

        def kloop(j, _):
            init = (jnp.zeros((BQ, HEAD_W), F32), jnp.zeros((BQ, HEAD_W), F32))
            carry = step(j, j, init, True)
            dk_acc, dv_acc = lax.fori_loop(j + 1, nq, lambda i, c: step(j, i, c, False), carry)
            rk = pl.ds(pl.multiple_of(j * BQ, BQ), BQ)
            dk_ref[rk, :] = dk_acc
            dv_ref[rk, :] = dv_acc
            return 0

        lax.fori_loop(0, nq, kloop, 0)

    head = pl.BlockSpec((T, HEAD_W), lambda h: (0, h))
    nbytes = 4 * _nbytes((T, HEAD_W), BF16) + 5 * _nbytes((T, HEAD_W), F32) + 2 * _nbytes((T, 128), F32)
    return pl.pallas_call(
        kern, name="attn_bwd", grid=(HEADS,),
        in_specs=[head, head, head, head, pl.BlockSpec((1, T, 1), lambda h: (h, 0, 0)), head],
        out_specs=[head, head, head],
        out_shape=[pltpu.HBM((T, HEADS * HEAD_W), F32)] * 3,
        scratch_shapes=[pltpu.VMEM((T, 1), F32)],
        compiler_params=pltpu.CompilerParams(dimension_semantics=("parallel",), vmem_limit_bytes=_vmem(nbytes)),
    )(*[_hbm(a) for a in (q, k, v, o, lse, dcat)])


def _sgu_common(u, v, ln_g, ln_b):
    ua, tu = _gelu(u)
    va, tv = _gelu(v)
    vh, r = _ln_stats(va)
    return ua, tu, tv, vh, r, vh * ln_g + ln_b


def _tril_mask(n):
    return lax.broadcasted_iota(jnp.int32, (n, n), 1) <= lax.broadcasted_iota(jnp.int32, (n, n), 0)


def _sgu_fwd(zs, ln_g, ln_b, w, bias_full):
    def body(rows, consts, outs, accs):
        ua, _, _, _, _, vn = _sgu_common(rows[0][...], rows[1][...], consts[0][...], consts[1][...])
        vn = vn.astype(BF16)
        tri = _tril_mask(SGU_CHUNK)
        for g in range(SGU_G):
            wg = jnp.where(tri, consts[2][0, g], 0.0).astype(BF16)
            cols = slice(g * 128, (g + 1) * 128)
            for c in range(ua.shape[0] // SGU_CHUNK):
                rws = slice(c * SGU_CHUNK, (c + 1) * SGU_CHUNK)
                mixed = _dot(wg, vn[rws, cols], NN) + consts[3][:, cols]
                outs[0][rws, cols] = (ua[rws, cols] * mixed).astype(BF16)

    return _rowwise("sgu_fwd", body, [(zs, 512, 0), (zs, 512, 1)], [ln_g, ln_b, w, bias_full], [(SGU_DIM, BF16)])


def _sgu_bwd(zs, dcat, ln_g, ln_b, w, bias_full):
    def body(rows, consts, outs, accs):
        u, v = rows[0][...], rows[1][...]
        ua, tu, tv, vh, r, vn = _sgu_common(u, v, consts[0][...], consts[1][...])
        dout = rows[2][...].astype(F32)
        vn_bf = vn.astype(BF16)
        tri = _tril_mask(SGU_CHUNK)
        dmixed = (dout * ua)
        dmixed_bf = dmixed.astype(BF16)
        ones = jnp.ones((8, SGU_CHUNK), F32)
        dvn_cols, mixed_cols = [], []
        for g in range(SGU_G):
            wg = jnp.where(tri, consts[2][0, g], 0.0).astype(BF16)
            cols = slice(g * 128, (g + 1) * 128)
            dvn_rows, mixed_rows = [], []
            dw = jnp.zeros((SGU_CHUNK, SGU_CHUNK), F32)
            dmix_sum = jnp.zeros((SGU_CHUNK, 128), F32)
            for c in range(u.shape[0] // SGU_CHUNK):
                rws = slice(c * SGU_CHUNK, (c + 1) * SGU_CHUNK)
                mixed_rows.append(_dot(wg, vn_bf[rws, cols], NN) + consts[3][:, cols])
                dvn_rows.append(_dot(wg, dmixed_bf[rws, cols], TN_))
                dw = dw + _dot(dmixed_bf[rws, cols], vn_bf[rws, cols], NT)
                dmix_sum = dmix_sum + dmixed[rws, cols]
            accs[0][g] += jnp.where(tri, dw, 0.0)
            accs[3][g:g + 1, :] += _dot(ones, dmix_sum, NT, precision=HIGHEST)[0:1, :]
            dvn_cols.append(jnp.concatenate(dvn_rows, axis=0))
            mixed_cols.append(jnp.concatenate(mixed_rows, axis=0))
        dvn = jnp.concatenate(dvn_cols, axis=1)
        mixed = jnp.concatenate(mixed_cols, axis=1)
        accs[1][...] += jnp.sum(dvn * vh, axis=0, keepdims=True)
        accs[2][...] += jnp.sum(dvn, axis=0, keepdims=True)
        dvh = dvn * consts[0][...]
        dva = r * (dvh - jnp.mean(dvh, axis=-1, keepdims=True) - vh * jnp.mean(dvh * vh, axis=-1, keepdims=True))
        outs[0][:, 0:512] = (dout * mixed * _gelu_grad(u, tu)).astype(BF16)
        outs[0][:, 512:1024] = (dva * _gelu_grad(v, tv)).astype(BF16)

    return _rowwise("sgu_bwd", body, [(zs, 512, 0), (zs, 512, 1), (dcat, 512, 2)], [ln_g, ln_b, w, bias_full], [(1024, BF16)],
                    [((SGU_G, 128, 128), F32), ((1, SGU_DIM), F32), ((1, SGU_DIM), F32), ((SGU_G, 128), F32)], tr=256)


def _lower_bound(hg_lb):
    a0, a1 = hg_lb[0:1, :], hg_lb[1:2, :]
    m = jnp.maximum(a0, a1)
    e0, e1 = jnp.exp(a0 - m), jnp.exp(a1 - m)
    s0, s1 = e0 / (e0 + e1), e1 / (e0 + e1)
    return (s0 + s1) - s0, s0, s1


def _hg_gates(qr, fr, lb):
    C = qr.shape[0]
    sq = _sig(qr)
    qf = qr * sq
    sf = _sig(fr)
    gate = lb + (1.0 - lb) * sf
    kk = 1.0 - gate
    tri = _tril_mask(C)
    b = _dot(jnp.where(tri, 1.0, 0.0), jnp.log(gate), NN, precision=HIGHEST)
    bref = b[C // 2 - 1:C // 2, :]
    bl = b[C - 1:C, :]
    e_b = jnp.exp(b)
    e_q = jnp.exp(b - bref)
    e_k = jnp.exp(bref - b)
    e_lb = jnp.exp(bl - b)
    return dict(sq=sq, qf=qf, sf=sf, gate=gate, kk=kk, tri=tri, bl=bl, e_b=e_b, e_q=e_q, e_k=e_k, e_lb=e_lb)


def _hgrn_fwd(z1, hg_lb, gnorm):
    T = z1.shape[0]
    C = min(HG_CHUNK, T)
    nc = T // C

    def kern(q_ref, f_ref, i_ref, g_ref, lb_ref, gn_ref, o_ref, hg_ref, st_ref, s_scr):
        @pl.when(pl.program_id(0) == 0)
        def _():
            s_scr[...] = jnp.zeros(s_scr.shape, F32)

        lb_all, _, _ = _lower_bound(lb_ref[...])
        st_ref[0] = s_scr[...]
        for h in range(HEADS):
            cols = slice(h * HEAD_W, (h + 1) * HEAD_W)
            t = _hg_gates(q_ref[:, cols], f_ref[:, cols], lb_all[:, cols])
            v = i_ref[:, cols]
            v_bf = v.astype(BF16)
            st = s_scr[h]
            a = jnp.where(t["tri"], _dot((t["qf"] * t["e_q"]).astype(BF16), (t["kk"] * t["e_k"]).astype(BF16), NT), 0.0)
            o = _dot(a.astype(BF16), v_bf, NN) + _dot((t["qf"] * t["e_b"]).astype(BF16), st.astype(BF16), NT)
            s_scr[h] = st * jnp.exp(t["bl"]) + _dot(v_bf, (t["kk"] * t["e_lb"]).astype(BF16), TN_)
            o_ref[:, cols] = o
            gr = g_ref[:, cols]
            r = lax.rsqrt(jnp.mean(o * o, axis=-1, keepdims=True) + EPS)
            hg_ref[:, cols] = (o * r * gn_ref[:, cols] * (gr * _sig(gr))).astype(BF16)

    seg = lambda k: pl.BlockSpec((C, D_MODEL), functools.partial(lambda n, k: (n, k), k=k))
    row = pl.BlockSpec((C, D_MODEL), lambda n: (n, 0))
    nbytes = 6 * _nbytes((C, D_MODEL), F32) + 3 * _nbytes((HEADS, 128, 128), F32)
    return pl.pallas_call(
        kern, name="hgrn_fwd", grid=(nc,),
        in_specs=[seg(0), seg(1), seg(2), seg(3), pl.BlockSpec((2, D_MODEL), lambda n: (0, 0)),
                  pl.BlockSpec((1, D_MODEL), lambda n: (0, 0))],
        out_specs=[row, row, pl.BlockSpec((1, HEADS, 128, 128), lambda n: (n, 0, 0, 0))],
        out_shape=[pltpu.HBM((T, D_MODEL), F32), pltpu.HBM((T, D_MODEL), BF16),
                   pltpu.HBM((nc, HEADS, 128, 128), F32)],
        scratch_shapes=[pltpu.VMEM((HEADS, 128, 128), F32)],
        compiler_params=pltpu.CompilerParams(dimension_semantics=("arbitrary",), vmem_limit_bytes=_vmem(nbytes)),
    )(*[_hbm(a) for a in (z1, z1, z1, z1, hg_lb, gnorm)])


def _hgrn_bwd(z1, o_pre, dhg, states, hg_lb, gnorm):
    T = z1.shape[0]
    C = min(HG_CHUNK, T)
    nc = T // C

    def kern(q_ref, f_ref, i_ref, g_ref, o_ref, dhg_ref, st_ref, lb_ref, gn_ref, dz_ref, dlb_ref, dgn_ref, ds_scr, dlb_scr):
        n = pl.program_id(0)

        @pl.when(n == 0)
        def _():
            ds_scr[...] = jnp.zeros(ds_scr.shape, F32)
            dlb_scr[...] = jnp.zeros(dlb_scr.shape, F32)
            dgn_ref[...] = jnp.zeros(dgn_ref.shape, F32)

        lb_all, s0, s1 = _lower_bound(lb_ref[...])
        for h in range(HEADS):
            cols = slice(h * HEAD_W, (h + 1) * HEAD_W)
            lb = lb_all[:, cols]
            qr, fr = q_ref[:, cols], f_ref[:, cols]
            t = _hg_gates(qr, fr, lb)
            tri = t["tri"]
            v_bf = i_ref[:, cols].astype(BF16)
            st_bf = st_ref[0, h].astype(BF16)
            dst = ds_scr[h]
            dst_bf = dst.astype(BF16)
            o = o_ref[:, cols]
            gr = g_ref[:, cols]
            sg = _sig(gr)
            sil = gr * sg
            gn = gn_ref[:, cols]
            r = lax.rsqrt(jnp.mean(o * o, axis=-1, keepdims=True) + EPS)
            on = o * r
            dh = dhg_ref[:, cols].astype(F32)
            dgn_ref[:, cols] += jnp.sum(dh * on * sil, axis=0, keepdims=True)
            dg = dh * on * gn * (sg * (1.0 + gr * (1.0 - sg)))
            don = dh * gn * sil
            do_bf = (r * (don - on * jnp.mean(don * on, axis=-1, keepdims=True))).astype(BF16)
            qe = (t["qf"] * t["e_q"]).astype(BF16)
            ke = (t["kk"] * t["e_k"]).astype(BF16)
            qb = (t["qf"] * t["e_b"]).astype(BF16)
            kh_bf = (t["kk"] * t["e_lb"]).astype(BF16)
            a_bf = jnp.where(tri, _dot(qe, ke, NT), 0.0).astype(BF16)
            da_bf = jnp.where(tri, _dot(do_bf, v_bf, NT), 0.0).astype(BF16)
            dv = _dot(a_bf, do_bf, TN_) + _dot(kh_bf, dst_bf, NT)
            dqe = _dot(da_bf, ke, NN)
            dqb = _dot(do_bf, st_bf, NN)
            dke = _dot(da_bf, qe, TN_)
            dkh = _dot(v_bf, dst_bf, NN)
            dqf = dqe * t["e_q"] + dqb * t["e_b"]
            dkk = dke * t["e_k"] + dkh * t["e_lb"]
            kh_r = kh_bf.astype(F32)
            db = qe.astype(F32) * dqe - ke.astype(F32) * dke + qb.astype(F32) * dqb - kh_r * dkh
            e_bl = jnp.exp(t["bl"])
            dbl = jnp.sum(dkh * kh_r, axis=0, keepdims=True) + e_bl * jnp.sum(st_ref[0, h] * dst, axis=0, keepdims=True)
            dlg = _dot(jnp.where(tri, 1.0, 0.0), db, TN_, precision=HIGHEST) + dbl
            ds_scr[h] = dst * e_bl + _dot(do_bf, qb, TN_)
            dgate = dlg / t["gate"] - dkk
            sf = t["sf"]
            dlb_scr[:, cols] += jnp.sum(dgate * (1.0 - sf), axis=0, keepdims=True)
            df = dgate * (1.0 - lb) * sf * (1.0 - sf)
            dq = dqf * (t["sq"] * (1.0 + qr * (1.0 - t["sq"])))
            dz_ref[:, cols] = dq.astype(BF16)
            dz_ref[:, D_MODEL + h * HEAD_W:D_MODEL + (h + 1) * HEAD_W] = df.astype(BF16)
            dz_ref[:, 2 * D_MODEL + h * HEAD_W:2 * D_MODEL + (h + 1) * HEAD_W] = dv.astype(BF16)
            dz_ref[:, 3 * D_MODEL + h * HEAD_W:3 * D_MODEL + (h + 1) * HEAD_W] = dg.astype(BF16)

        @pl.when(n == nc - 1)
        def _():
            d = s0 * s1 * dlb_scr[...]
            dlb_ref[0:1, :] = -d
            dlb_ref[1:2, :] = d

    seg = lambda k: pl.BlockSpec((C, D_MODEL), functools.partial(lambda n, k: (nc - 1 - n, k), k=k))
    nbytes = 6 * _nbytes((C, D_MODEL), F32) + _nbytes((C, 4 * D_MODEL), BF16) + 3 * _nbytes((HEADS, 128, 128), F32)
    return pl.pallas_call(
        kern, name="hgrn_bwd", grid=(nc,),
        in_specs=[seg(0), seg(1), seg(2), seg(3), seg(0), seg(0),
                  pl.BlockSpec((1, HEADS, 128, 128), lambda n: (nc - 1 - n, 0, 0, 0)),
                  pl.BlockSpec((2, D_MODEL), lambda n: (0, 0)), pl.BlockSpec((1, D_MODEL), lambda n: (0, 0))],
        out_specs=[pl.BlockSpec((C, 4 * D_MODEL), lambda n: (nc - 1 - n, 0)),
                   pl.BlockSpec((2, D_MODEL), lambda n: (0, 0)), pl.BlockSpec((1, D_MODEL), lambda n: (0, 0))],
        out_shape=[pltpu.HBM((T, 4 * D_MODEL), BF16), pltpu.HBM((2, D_MODEL), F32),
                   pltpu.HBM((1, D_MODEL), F32)],
        scratch_shapes=[pltpu.VMEM((HEADS, 128, 128), F32), pltpu.VMEM((1, D_MODEL), F32)],
        compiler_params=pltpu.CompilerParams(dimension_semantics=("arbitrary",), vmem_limit_bytes=_vmem(nbytes)),
    )(*[_hbm(a) for a in (z1, z1, z1, z1, o_pre, dhg, states, hg_lb, gnorm)])


def _prep_weights(gw):
    w_in_e = gw["w_in_e"].transpose(1, 0, 2).reshape(D_MODEL, 1568)
    kr = jnp.pad(w_in_e[:, 512:544], ((0, 0), (64, 32)))
    wm = jnp.concatenate([w_in_e[:, 0:512], kr], axis=1)
    ws = w_in_e[:, 544:1568]
    w_qb = gw["w_qb"].transpose(1, 0, 2).reshape(MLA_LORA, HEADS, 96)
    wq = jnp.pad(w_qb, ((0, 0), (0, 0), (0, 32))).reshape(MLA_LORA, HEADS * HEAD_W)
    kvb = gw["w_kvb"].transpose(1, 0, 2).reshape(MLA_LORA, HEADS, 128)
    wk = jnp.pad(kvb[:, :, :64], ((0, 0), (0, 0), (0, 64))).reshape(MLA_LORA, HEADS * HEAD_W)
    wv = jnp.pad(kvb[:, :, 64:], ((0, 0), (0, 0), (0, 64))).reshape(MLA_LORA, HEADS * HEAD_W)
    w_out_e = gw["w_out_e"].reshape(D_MODEL, D_MODEL)
    woa = jnp.pad(w_out_e[:512].reshape(HEADS, 64, D_MODEL), ((0, 0), (0, 64), (0, 0))).reshape(HEADS * HEAD_W, D_MODEL)
    return dict(wm=wm, ws=ws, wq=wq, wk=wk, wv=wv, woa=woa, wob=w_out_e[512:])


def _unprep_grads(g):
    dwm, dws = g["wm"], g["ws"]
    d_in_e = jnp.concatenate([dwm[:, 0:512], dwm[:, 512 + 64:512 + 96], dws], axis=1)
    d_qb = g["wq"].reshape(MLA_LORA, HEADS, HEAD_W)[:, :, :96].reshape(MLA_LORA, HEADS * 96)
    dk = g["wk"].reshape(MLA_LORA, HEADS, HEAD_W)[:, :, :64]
    dv = g["wv"].reshape(MLA_LORA, HEADS, HEAD_W)[:, :, :64]
    d_kvb = jnp.concatenate([dk, dv], axis=2).reshape(MLA_LORA, HEADS * 128)
    d_oa = g["woa"].reshape(HEADS, HEAD_W, D_MODEL)[:, :64].reshape(HEADS * 64, D_MODEL)
    dev_major = lambda a: a.reshape(a.shape[0], N_DEV, a.shape[1] // N_DEV).transpose(1, 0, 2)
    return dict(w_in_e=dev_major(d_in_e), w_qb=dev_major(d_qb), w_kvb=dev_major(d_kvb),
                w_out_e=jnp.concatenate([d_oa, g["wob"]], axis=0).reshape(N_DEV, D_MODEL // N_DEV, D_MODEL))


def _local_step(x, positions, target, gw, sp, ex):
    w = _prep_weights(gw)
    T = x.shape[0]
    tm = min(TM, T)
    nt = T // tm
    half = MLA_ROPE // 2
    inv_freq = ROPE_BASE ** (-jnp.arange(half, dtype=F32) / half)
    invf_lane = jnp.concatenate([jnp.zeros((64,), F32), inv_freq, inv_freq, jnp.zeros((32,), F32)]).reshape(1, HEAD_W)
    tabs = _rope_tables(positions.reshape(T, 1), invf_lane)
    bias_full = jnp.repeat(sp["sgu_b"][0].T, 128, axis=1)
    sgu_w = sp["sgu_w"]
    gq, gkv = sp["mla_gq"], sp["mla_gkv"]
    ln1_g, ln1_b, ln2_g, ln2_b = sp["ln1_g"], sp["ln1_b"], sp["ln2_g"], sp["ln2_b"]
    wide = HEADS * HEAD_W
    tab_rows = [_rb(t, tm) for t in tabs]
    resid = lambda acc, d: (acc + ALPHA * d,)

    zm = _tiled("l0_in_mla", (1, nt), [_rb(x, tm), _cw(w["wm"], 640)], [_out(T, 640, F32, tm, 640)], _mmc(NN), deps=[ex.first_token])
    zs = _tiled("l0_in_sgu", (2, nt), [_rb(x, tm), _cw(w["ws"], TN)], [_out(T, 1024, F32, tm, TN)], _mmc(NN), deps=[ex.first_token])
    cqn, ckvn, kr_rot = _mla_pre(zm, tabs, gq, gkv)
    q = _tiled("l0_q", (1, nt), [_rb(cqn, tm), _cw(w["wq"], wide)] + tab_rows, [_out(T, wide, BF16, tm, wide)],
               _mmc(NN, epilogue=lambda acc, c, s1, s2: (_rope_heads(acc, c, s1, s2, _rope),)))
    k = _tiled("l0_k", (1, nt), [_rb(ckvn, tm), _cw(w["wk"], wide), _rb(kr_rot, tm)], [_out(T, wide, BF16, tm, wide)],
               _mmc(NN, epilogue=lambda acc, kr: (acc + jnp.concatenate([kr] * HEADS, axis=1),)))
    v = _tiled("l0_v", (1, nt), [_rb(ckvn, tm), _cw(w["wv"], wide)], [_out(T, wide, BF16, tm, wide)], _mmc(NN))
    o_att, lse = _attn_fwd(q, k, v)
    b_out = _sgu_fwd(zs, sp["sgu_ln_g"], sp["sgu_ln_b"], sgu_w, bias_full)
    mix0 = _tiled("l0_out", (2, nt), [_rb(o_att, tm), _cw(w["woa"], TN), _rb(b_out, tm), _cw(w["wob"], TN)],
                  [_out(T, D_MODEL, F32, tm, TN)], _mmc(NN, n_pairs=2))
    y1, h1, h1_bf = _ln_fwd("l0_ln1", x, mix0, ln1_g, ln1_b, 0)
    big = ex.weights_ready(after=y1)
    w_ff1, w_ff2, w_in_o, w_out_o = big["w_ff1"], big["w_ff2"], big["w_in_o"], big["w_out_o"].reshape(D_MODEL, D_MODEL)
    a0, act0, ff0 = _mlp_fwd("l0", h1_bf, w_ff1[0], w_ff2[0])
    y2, h2, h2_bf = _ln_fwd("l0_ln2", h1, ff0, ln2_g, ln2_b, 0)

    z1 = _tiled("l1_in", (1, nt), [_rb(h2_bf, tm), _res(w_in_o)], [_out(T, 4 * D_MODEL, F32, tm, 4 * D_MODEL)],
                _mmc_blocks(N_DEV, NN, lambda w, d: w[d]), direct=True)
    o_pre, hg, states = _hgrn_fwd(z1, sp["hg_lb"], sp["hg_gnorm"])
    mix1 = _tiled("l1_out", (2, nt), [_rb(hg, tm), _cw(w_out_o, TN)], [_out(T, D_MODEL, F32, tm, TN)], _mmc(NN))
    y3, h3, h3_bf = _ln_fwd("l1_ln1", h2, mix1, ln1_g, ln1_b, 1)
    a1, act1, ff1 = _mlp_fwd("l1", h3_bf, w_ff1[1], w_ff2[1])
    y4, dh4, sq_err = _ln_loss("l1_ln2", h3, ff1, ln2_g, ln2_b, 1, target)

    gs, g0 = {}, {}
    dy4, dy4_bf, gs["ln2_g1"], gs["ln2_b1"] = _ln_bwd("l1_ln2_bwd", y4, dh4, ln2_g, 1)
    da1, dw1_1, dw2_1 = _mlp_bwd_w("l1", h3_bf, a1, act1, dy4_bf, w_ff2[1])
    dh3, _ = _mlp_bwd_h("l1", da1, w_ff1[1], dy4)
    dy3, dy3_bf, gs["ln1_g1"], gs["ln1_b1"] = _ln_bwd("l1_ln1_bwd", y3, dh3, ln1_g, 1)
    d_out_o = _tiled("l1_dwout", (2, D_MODEL // TM), [_tl(hg, TM), _cw(dy3_bf, TN)], [_out(D_MODEL, D_MODEL, F32, TM, TN)],
                     _mmc(TN_)).reshape(N_DEV, D_MODEL // N_DEV, D_MODEL)
    dhg = _tiled("l1_dhg", (2, nt), [_rb(dy3_bf, tm), _rw(w_out_o, TN)], [_out(T, D_MODEL, BF16, tm, TN)], _mmc(NT))
    dz1, gs["hg_lb"], gs["hg_gnorm"] = _hgrn_bwd(z1, o_pre, dhg, states, sp["hg_lb"], sp["hg_gnorm"])
    d_in_o = _tiled("l1_dwin", (N_DEV, 1), [_res(h2_bf), _cw(dz1, TN)], [_out_dev(D_MODEL, TN, D_MODEL)], _mmc(TN_))
    token = ex.grads_start("l1", [dw1_1, dw2_1, d_in_o, d_out_o])
    dh2 = _tiled("l1_dh2", (1, nt), [_rb(dz1, tm), _res(w_in_o), _rb(dy3, tm)], [_out(T, D_MODEL, F32, tm, D_MODEL)],
                 _mmc_dev(epilogue=resid), deps=[token])

    dy2, dy2_bf, gs["ln2_g0"], gs["ln2_b0"] = _ln_bwd("l0_ln2_bwd", y2, dh2, ln2_g, 0)
    token = ex.grads_middle("l1", after=dy2)
    da0, dw1_0, dw2_0 = _mlp_bwd_w("l0", h1_bf, a0, act0, dy2_bf, w_ff2[0], deps=[token])
    token = ex.grads_start("l0m", [dw1_0, dw2_0])
    dh1, _ = _mlp_bwd_h("l0", da0, w_ff1[0], dy2, deps=[token])
    ex.grads_end("l1", after=dh1)
    dy1, dy1_bf, gs["ln1_g0"], gs["ln1_b0"] = _ln_bwd("l0_ln1_bwd", y1, dh1, ln1_g, 0)
    token = ex.grads_middle("l0m", after=dy1)
    g0["woa"] = _tiled("l0_dwoa", (2, wide // TM), [_tl(o_att, TM), _cw(dy1_bf, TN)], [_out(wide, D_MODEL, F32, TM, TN)], _mmc(TN_))
    g0["wob"] = _tiled("l0_dwob", (2, 1), [_tl(b_out, SGU_DIM), _cw(dy1_bf, TN)], [_out(SGU_DIM, D_MODEL, F32, SGU_DIM, TN)], _mmc(TN_))
    wo_cat = jnp.concatenate([w["woa"], w["wob"]], axis=0)
    dcat = _tiled("l0_dcat", (3, nt), [_rb(dy1_bf, tm), _rw(wo_cat, TN)], [_out(T, wide + SGU_DIM, BF16, tm, TN)], _mmc(NT), deps=[token])
    dzs, gs["sgu_w"], gs["sgu_ln_g"], gs["sgu_ln_b"], gs["sgu_b"] = _sgu_bwd(zs, dcat, sp["sgu_ln_g"], sp["sgu_ln_b"], sgu_w, bias_full)
    dq, dk, dv = _attn_bwd(q, k, v, o_att, lse, dcat)
    ex.grads_end("l0m", after=dq)
    dq_pre = _unrope_heads(dq, tabs)
    lora_w = lambda name, a, d: _tiled(name, (wide // TN, 1), [_tl(a, MLA_LORA), _cw(d, TN)], [_out(MLA_LORA, wide, F32, MLA_LORA, TN)], _mmc(TN_))
    g0["wq"] = lora_w("l0_dwq", cqn, dq_pre)
    g0["wk"] = lora_w("l0_dwk", ckvn, dk)
    g0["wv"] = lora_w("l0_dwv", ckvn, dv)
    dcqn = _tiled("l0_dcqn", (1, nt), [_rb(dq_pre, tm), _rw(w["wq"], MLA_LORA)], [_out(T, MLA_LORA, F32, tm, MLA_LORA)], _mmc(NT))
    dckvn = _tiled("l0_dckvn", (1, nt), [_rb(dk, tm), _rw(w["wk"], MLA_LORA), _rb(dv, tm), _rw(w["wv"], MLA_LORA)],
                   [_out(T, MLA_LORA, F32, tm, MLA_LORA)], _mmc(NT, n_pairs=2))
    dzm, gs["mla_gq"], gs["mla_gkv"] = _mla_pre_bwd(zm, tabs, gq, gkv, dcqn, dckvn, dk)
    g0["wm"] = _tiled("l0_dwm", (1, D_MODEL // TM), [_tl(x, TM), _cw(dzm, 640)], [_out(D_MODEL, 640, F32, TM, 640)], _mmc(TN_))
    g0["ws"] = _tiled("l0_dws", (2, D_MODEL // TM), [_tl(x, TM), _cw(dzs, TN)], [_out(D_MODEL, 1024, F32, TM, TN)], _mmc(TN_))
    dx = _tiled("l0_dx", (2, nt), [_rb(dzm, tm), _rw(w["wm"], TN), _rb(dzs, tm), _rw(w["ws"], TN), _rbj(dy1, tm, TN)],
                [_out(T, D_MODEL, F32, tm, TN)], _mmc(NT, n_pairs=2, epilogue=resid))

    return sq_err, dx, _unprep_grads(g0), gs


def _me():
    return lax.axis_index("x"), lax.axis_index("y"), lax.axis_index("c")


def _hbm_call(name, kern, operands, out_shape, n_sems, extra_scratch=()):
    any_spec = pl.BlockSpec(memory_space=pl.ANY)
    return pl.pallas_call(
        kern, name=name, out_shape=out_shape, in_specs=[any_spec] * len(operands), out_specs=[any_spec] * len(out_shape),
        scratch_shapes=[pltpu.SemaphoreType.DMA((n_sems,)), pltpu.SemaphoreType.DMA((n_sems,)), *extra_scratch],
    )(*[_hbm(a) for a in operands])


ANY_SPEC = pl.BlockSpec(memory_space=pl.ANY)
HBM_SPEC = pl.BlockSpec(memory_space=pltpu.HBM)
SEM_SPEC = pl.BlockSpec(memory_space=pltpu.SEMAPHORE)
EFFECT = pltpu.SideEffectType.DATAFLOW_SIDE_EFFECTING


def _split_start(name, srcs, lands, n_sems, make_copies, after=()):
    n, m, k = len(srcs), len(lands), len(after)

    def body(*refs):
        for cp in make_copies(refs[:n], refs[n:n + m], refs[n + m + k], refs[n + m + k + 1]):
            cp.start()
        refs[-1][...] = jnp.zeros(refs[-1].shape, F32)

    out_shape = (pltpu.SemaphoreType.DMA((n_sems,)), pltpu.SemaphoreType.DMA((n_sems,)),
                 *[pltpu.HBM(a.shape, a.dtype) for a in (*srcs, *lands)], jax.ShapeDtypeStruct((8, 128), F32))
    res = pl.pallas_call(
        body, name=name, out_shape=out_shape, in_specs=[HBM_SPEC] * (n + m) + [ANY_SPEC] * k,
        out_specs=(SEM_SPEC, SEM_SPEC, *[HBM_SPEC] * (n + m), pl.BlockSpec(memory_space=pltpu.VMEM)),
        input_output_aliases={i: 2 + i for i in range(n + m)},
        compiler_params=pltpu.CompilerParams(has_side_effects=EFFECT),
    )(*[_hbm(a) for a in (*srcs, *lands)], *after)
    return res[0], res[1], list(res[2:2 + n]), list(res[2 + n:2 + n + m]), res[-1]


def _split_wait(name, send_sems, recv_sems, srcs, lands, after, make_copies):
    n, m = len(srcs), len(lands)

    def body(*refs):
        for cp in make_copies(refs[:n], refs[n:n + m], refs[n + m], refs[n + m + 1]):
            cp.wait_send()
            cp.wait_recv()

    res = pl.pallas_call(
        body, name=name, out_shape=tuple(pltpu.HBM(a.shape, a.dtype) for a in (*srcs, *lands)),
        in_specs=[HBM_SPEC] * (n + m) + [SEM_SPEC, SEM_SPEC] + [ANY_SPEC] * len(after), out_specs=tuple([HBM_SPEC] * (n + m)),
        input_output_aliases={i: i for i in range(n + m)},
        compiler_params=pltpu.CompilerParams(has_side_effects=EFFECT),
    )(*srcs, *lands, send_sems, recv_sems, *after)
    return list(res[:n]), list(res[n:])


def _place_own(shards, dev):
    n = len(shards)

    def kern(dev_ref, *refs):
        for x_ref, o_ref in zip(refs[:n], refs[n:]):
            o_ref[...] = x_ref[...].astype(o_ref.dtype)

    blocks = [(None, *a.shape[1:]) for a, _, _ in shards]
    nbytes = sum(_nbytes(b, a.dtype) + _nbytes(b, dt) for b, (a, _, dt) in zip(blocks, shards))
    return pl.pallas_call(
        kern, name="weights_place_own", out_shape=[pltpu.HBM((N_DEV, *a.shape[1:]), dt) for a, _, dt in shards],
        grid_spec=pltpu.PrefetchScalarGridSpec(
            num_scalar_prefetch=1, grid=(1,),
            in_specs=[pl.BlockSpec(b, functools.partial(lambda i, dev, l: (l, 0, 0), l=l)) for b, (_, l, _) in zip(blocks, shards)],
            out_specs=[pl.BlockSpec(b, lambda i, dev: (dev[0], 0, 0)) for b in blocks]),
        compiler_params=pltpu.CompilerParams(dimension_semantics=("arbitrary",), vmem_limit_bytes=_vmem(nbytes)),
    )(dev, *[_hbm(a) for a, _, _ in shards])


def _ag_first_copies(src_refs, out_refs, send_sems, recv_sems):
    x, y, c = _me()
    targets = [(x, y, 1 - c), (1 - x, y, c), (x, 1 - y, c), (1 - x, 1 - y, c)]
    return [pltpu.make_async_remote_copy(
        src_ref=out_refs[op].at[4 * x + 2 * y + c], dst_ref=out_refs[op].at[4 * x + 2 * y + c], send_sem=send_sems.at[4 * op + k],
        recv_sem=recv_sems.at[4 * op + k], device_id=to, device_id_type=MESH)
        for op in range(len(out_refs)) for k, to in enumerate(targets)]


def _ag_second(gathered):
    n = len(gathered)

    def kern(*refs):
        in_refs, out_refs, (send_sems, recv_sems) = refs[:n], refs[n:2 * n], refs[2 * n:]
        x, y, c = _me()
        chips = [(1 - x, y), (x, 1 - y), (1 - x, 1 - y)]
        passed = [pltpu.make_async_remote_copy(
            src_ref=in_refs[op].at[4 * cx + 2 * cy + c], dst_ref=out_refs[op].at[4 * cx + 2 * cy + c],
            send_sem=send_sems.at[3 * op + j], recv_sem=recv_sems.at[3 * op + j], device_id=(x, y, 1 - c), device_id_type=MESH)
            for op in range(n) for j, (cx, cy) in enumerate(chips)]
        for cp in passed:
            cp.start()
        for cp in passed:
            cp.wait_send()
        for op in range(n):
            for j, (cx, cy) in enumerate(chips):
                slot = out_refs[op].at[4 * cx + 2 * cy + 1 - c]
                pltpu.make_async_remote_copy(src_ref=slot, dst_ref=slot, send_sem=send_sems.at[3 * op + j],
                                             recv_sem=recv_sems.at[3 * op + j], device_id=(x, y, c), device_id_type=MESH).wait_recv()

    return pl.pallas_call(
        kern, name="weights_all_gather_second", out_shape=[pltpu.HBM(g.shape, g.dtype) for g in gathered],
        in_specs=[ANY_SPEC] * n, out_specs=[ANY_SPEC] * n, input_output_aliases={i: i for i in range(n)},
        scratch_shapes=[pltpu.SemaphoreType.DMA((3 * n,)), pltpu.SemaphoreType.DMA((3 * n,))],
    )(*[_hbm(a) for a in gathered])


def _rs_sibling_copies(g_refs, out_refs, send_sems, recv_sems):
    x, y, c = _me()
    return [pltpu.make_async_remote_copy(
        src_ref=g_refs[op].at[k, 1 - c], dst_ref=out_refs[op].at[k], send_sem=send_sems.at[4 * op + k],
        recv_sem=recv_sems.at[4 * op + k], device_id=(x, y, 1 - c), device_id_type=MESH)
        for op in range(len(g_refs)) for k in range(4)]


def _rs_chip_copies(p_refs, out_refs, send_sems, recv_sems):
    x, y, c = _me()
    chips = [(1 - x, y), (x, 1 - y), (1 - x, 1 - y)]
    return [pltpu.make_async_remote_copy(
        src_ref=p_refs[op].at[2 * cx + cy], dst_ref=out_refs[op].at[j], send_sem=send_sems.at[3 * op + j],
        recv_sem=recv_sems.at[3 * op + j], device_id=(cx, cy, c), device_id_type=MESH)
        for op in range(len(p_refs)) for j, (cx, cy) in enumerate(chips)]


def _all_gather(placed):
    n = len(placed)

    def kern(*refs):
        in_refs, out_refs, (send_sems, recv_sems) = refs[:n], refs[n:2 * n], refs[2 * n:]
        x, y, c = _me()
        me, sibling = (x, y, c), (x, y, 1 - c)
        chips = [(1 - x, y), (x, 1 - y), (1 - x, 1 - y)]

        def copy(op, k, block, to, own=False):
            idx = 4 * block[0] + 2 * block[1] + block[2]
            return pltpu.make_async_remote_copy(
                src_ref=(in_refs if own else out_refs)[op].at[idx], dst_ref=out_refs[op].at[idx], send_sem=send_sems.at[7 * op + k],
                recv_sem=recv_sems.at[7 * op + k], device_id=to, device_id_type=MESH)

        first = []
        for op in range(n):
            first.append(copy(op, 0, me, sibling, own=True))
            first += [copy(op, 1 + j, me, (*chip, c), own=True) for j, chip in enumerate(chips)]
        for cp in first:
            cp.start()
        passed = []
        for j, chip in enumerate(chips):
            for op in range(n):
                copy(op, 1 + j, (*chip, c), me).wait_recv()
                passed.append(copy(op, 4 + j, (*chip, c), sibling))
                passed[-1].start()
        for op in range(n):
            copy(op, 0, sibling, me).wait_recv()
            for j, chip in enumerate(chips):
                copy(op, 4 + j, (*chip, 1 - c), me).wait_recv()
        for cp in first + passed:
            cp.wait_send()

    return pl.pallas_call(
        kern, name="weights_all_gather", out_shape=[pltpu.HBM(g.shape, g.dtype) for g in placed],
        in_specs=[ANY_SPEC] * n, out_specs=[ANY_SPEC] * n, input_output_aliases={i: i for i in range(n)},
        scratch_shapes=[pltpu.SemaphoreType.DMA((7 * n,)), pltpu.SemaphoreType.DMA((7 * n,))],
    )(*[_hbm(a) for a in placed])


def _rs_sibling(grads):
    n = len(grads)

    def kern(*refs):
        g_refs, out_refs, (send_sems, recv_sems) = refs[:n], refs[n:2 * n], refs[2 * n:]
        x, y, c = _me()
        copies = [pltpu.make_async_remote_copy(
            src_ref=g_refs[op].at[k, 1 - c], dst_ref=out_refs[op].at[k], send_sem=send_sems.at[4 * op + k],
            recv_sem=recv_sems.at[4 * op + k], device_id=(x, y, 1 - c), device_id_type=MESH) for op in range(n) for k in range(4)]
        for cp in copies:
            cp.start()
        for cp in copies:
            cp.wait()

    out_shape = [pltpu.HBM((4, *g.shape[2:]), g.dtype) for g in grads]
    return _hbm_call("grads_to_sibling", kern, grads, out_shape, 4 * n)


def _rs_chips(sums):
    n = len(sums)

    def kern(*refs):
        p_refs, out_refs, (send_sems, recv_sems) = refs[:n], refs[n:2 * n], refs[2 * n:]
        x, y, c = _me()
        chips = [(1 - x, y), (x, 1 - y), (1 - x, 1 - y)]
        copies = [pltpu.make_async_remote_copy(
            src_ref=p_refs[op].at[2 * cx + cy], dst_ref=out_refs[op].at[j], send_sem=send_sems.at[3 * op + j],
            recv_sem=recv_sems.at[3 * op + j], device_id=(cx, cy, c), device_id_type=MESH)
            for op in range(n) for j, (cx, cy) in enumerate(chips)]
        for cp in copies:
            cp.start()
        for cp in copies:
            cp.wait()

    out_shape = [pltpu.HBM((3, *p.shape[1:]), p.dtype) for p in sums]
    return _hbm_call("grads_between_chips", kern, sums, out_shape, 3 * n)


def _row_tile(r):
    return r if r <= 256 else 256


def _chip_sum(name, g, from_sibling, core):
    _, _, R, W = g.shape
    tr = _row_tile(R)

    def kern(core_ref, g_ref, s_ref, o_ref):
        o_ref[...] = (g_ref[...] + s_ref[...]).astype(BF16)

    return pl.pallas_call(
        kern, name=name, out_shape=pltpu.HBM((4, R, W), BF16),
        grid_spec=pltpu.PrefetchScalarGridSpec(
            num_scalar_prefetch=1, grid=(4, R // tr),
            in_specs=[pl.BlockSpec((None, None, tr, W), lambda k, i, core: (k, core[0], i, 0)),
                      pl.BlockSpec((None, tr, W), lambda k, i, core: (k, i, 0))],
            out_specs=pl.BlockSpec((None, tr, W), lambda k, i, core: (k, i, 0))),
        compiler_params=pltpu.CompilerParams(dimension_semantics=("parallel", "parallel"), vmem_limit_bytes=_vmem(3 * tr * W * 4)),
    )(core, _hbm(g), _hbm(from_sibling))


def _adamw(w, g, m, v):
    m = ADAM_B1 * m + (1.0 - ADAM_B1) * g
    v = ADAM_B2 * v + (1.0 - ADAM_B2) * (g * g)
    m_hat = m / (1.0 - ADAM_B1 ** ADAM_STEP)
    v_hat = v / (1.0 - ADAM_B2 ** ADAM_STEP)
    return -ADAM_LR * (m_hat / (jnp.sqrt(v_hat) + ADAM_EPS) + ADAM_WD * w), m, v


def _finish_sharded(name, layers, w, m, v, where):
    nl, R, W = w.shape
    tr = _row_tile(R)

    def kern(where_ref, *refs):
        w_ref, m_ref, v_ref, go_ref, d_ref, mo_ref, vo_ref = refs[3 * nl:]
        for l in range(nl):
            g_ref, s_ref, c_ref = refs[3 * l:3 * l + 3]
            grad = g_ref[...] + s_ref[...]
            for j in range(3):
                grad = grad + c_ref[j].astype(F32)
            go_ref[l] = grad
            d_ref[l], mo_ref[l], vo_ref[l] = _adamw(w_ref[l], grad, m_ref[l], v_ref[l])

    row = pl.BlockSpec((nl, tr, W), lambda i, wh: (0, i, 0))
    in_specs, args = [], []
    for g, s, c in layers:
        in_specs += [pl.BlockSpec((None, None, tr, W), lambda i, wh: (wh[0], wh[1], i, 0)),
                     pl.BlockSpec((None, tr, W), lambda i, wh: (wh[0], i, 0)),
                     pl.BlockSpec((3, tr, W), lambda i, wh: (0, i, 0))]
        args += [g, s, c]
    return pl.pallas_call(
        kern, name=name, out_shape=[pltpu.HBM((nl, R, W), F32)] * 4,
        grid_spec=pltpu.PrefetchScalarGridSpec(num_scalar_prefetch=1, grid=(R // tr,), in_specs=in_specs + [row, row, row],
                                               out_specs=[row, row, row, row]),
        compiler_params=pltpu.CompilerParams(dimension_semantics=("parallel",), vmem_limit_bytes=_vmem(nl * 11 * tr * W * 4)),
    )(where, *[_hbm(a) for a in (*args, w, m, v)])


SMALL_PLACE = (("mla_gq", 0, 0, 1, 256), ("mla_gkv", 0, 256, 1, 256), ("sgu_ln_g", 0, 512, 1, 512), ("sgu_ln_b", 1, 0, 1, 512),
               ("hg_lb", 2, 0, 2, 1024), ("ln1_g", 4, 0, 2, 1024), ("ln1_b", 6, 0, 2, 1024), ("sgu_b", 8, 0, 4, 128),
               ("ln2_g", 12, 0, 2, 1024), ("ln2_b", 14, 0, 2, 1024), ("hg_gnorm", 16, 0, 1, 1024))
SMALL_BUF_ROWS = 24


def _small_reduce_adamw(gs, given):
    pieces = [(gs["mla_gq"], 0, 0), (gs["mla_gkv"], 0, 256), (gs["sgu_ln_g"], 0, 512), (gs["sgu_ln_b"], 1, 0), (gs["hg_lb"], 2, 0),
              (gs["ln1_g0"], 4, 0), (gs["ln1_g1"], 5, 0), (gs["ln1_b0"], 6, 0), (gs["ln1_b1"], 7, 0), (gs["sgu_b"], 8, 0),
              (gs["ln2_g0"], 12, 0), (gs["ln2_g1"], 13, 0), (gs["ln2_b0"], 14, 0), (gs["ln2_b1"], 15, 0), (gs["hg_gnorm"], 16, 0)]
    names = [p[0] for p in SMALL_PLACE] + ["sgu_w"]
    n_p, n_names = len(pieces), len(names)
    wmv = [given[pre + name] for name in names for pre in ("", "m_", "v_")]

    def kern(*refs):
        piece_refs, gw_ref = refs[:n_p], refs[n_p]
        wmv_refs = refs[n_p + 1:n_p + 1 + 3 * n_names]
        out_refs = refs[n_p + 1 + 3 * n_names:n_p + 1 + 7 * n_names]
        buf_a, buf_b, send_sems, recv_sems = refs[n_p + 1 + 7 * n_names:]
        px, py, pc = _me()
        me = 4 * px + 2 * py + pc
        mine_a, mine_b = buf_a.at[me], buf_b.at[me]
        mine_a[...] = jnp.zeros(mine_a.shape, F32)
        for ref, (_, r, l0) in zip(piece_refs, pieces):
            mine_a[r:r + ref.shape[0], l0:l0 + ref.shape[1]] = ref[...]
        mine_b[...] = gw_ref[...]
        copies = []
        for r in range(1, N_DEV):
            peer = (px ^ (r >> 2), py ^ ((r >> 1) & 1), pc ^ (r & 1))
            for k, mine in enumerate((mine_a, mine_b)):
                copies.append(pltpu.make_async_remote_copy(
                    src_ref=mine, dst_ref=mine, send_sem=send_sems.at[2 * (r - 1) + k], recv_sem=recv_sems.at[2 * (r - 1) + k],
                    device_id=peer, device_id_type=MESH))
        for cp in copies:
            cp.start()
        for r in range(1, N_DEV):
            for k, buf in enumerate((buf_a, buf_b)):
                theirs = buf.at[me ^ r]
                pltpu.make_async_remote_copy(
                    src_ref=theirs, dst_ref=theirs, send_sem=send_sems.at[2 * (r - 1) + k], recv_sem=recv_sems.at[2 * (r - 1) + k],
                    device_id=(px, py, pc), device_id_type=MESH).wait_recv()
        for cp in copies:
            cp.wait_send()
        sum_a, sum_b = buf_a[0], buf_b[0]
        for d in range(1, N_DEV):
            sum_a, sum_b = sum_a + buf_a[d], sum_b + buf_b[d]

        def own_block(full):
            acc = full[:, 0:128]
            for b in range(1, N_DEV):
                acc = jnp.where(me == b, full[:, b * 128:(b + 1) * 128], acc)
            return acc

        for idx, name in enumerate(names):
            w_ref, m_ref, v_ref = wmv_refs[3 * idx:3 * idx + 3]
            if name == "sgu_w":
                grad = sum_b[None]
            else:
                _, r, l0, nr, nl = SMALL_PLACE[idx]
                grad = sum_a[r:r + nr, l0:l0 + nl]
                if name == "hg_gnorm":
                    grad = own_block(grad)
                if name == "sgu_b":
                    grad = grad[None]
            res = (grad, *_adamw(w_ref[...], grad, m_ref[...], v_ref[...]))
            for o_ref, val in zip(out_refs[4 * idx:4 * idx + 4], res):
                o_ref[...] = val

    vmem = pl.BlockSpec(memory_space=pltpu.VMEM)
    operands = [p[0] for p in pieces] + [gs["sgu_w"]] + wmv
    out_shape = [jax.ShapeDtypeStruct(given[name].shape, F32) for name in names for _ in range(4)]
    res = pl.pallas_call(
        kern, name="small_all_reduce_adamw", out_shape=out_shape, in_specs=[vmem] * len(operands), out_specs=[vmem] * len(out_shape),
        scratch_shapes=[pltpu.VMEM((N_DEV, SMALL_BUF_ROWS, D_MODEL), F32), pltpu.VMEM((N_DEV, SGU_G, 128, 128), F32),
                        pltpu.SemaphoreType.DMA((14,)), pltpu.SemaphoreType.DMA((14,))],
    )(*operands)
    return {name: res[4 * idx:4 * idx + 4] for idx, name in enumerate(names)}


class _Exchange:
    def __init__(self, given):
        self.given = given
        px, py, pc = _me()
        self.core = pc.reshape(1).astype(jnp.int32)
        self.dev = (4 * px + 2 * py + pc).reshape(1).astype(jnp.int32)
        self.where = jnp.stack([2 * px + py, pc]).astype(jnp.int32)
        self.state, self.layers = {}, {}

    def start_weights(self, lands, after):
        self.weights = _split_start("weights_first_start", [], lands, 4 * len(lands), _ag_first_copies, after=after)
        self.first_token = self.weights[4]

    def weights_ready(self, after):
        send_sems, recv_sems, shards, lands, _ = self.weights
        _, lands = _split_wait("weights_first_wait", send_sems, recv_sems, shards, lands, [after], _ag_first_copies)
        got = _ag_second(lands)
        return dict(w_in_o=got[0], w_out_o=got[1], w_ff1=[got[2], got[3]], w_ff2=[got[4], got[5]])

    def grads_start(self, tag, grads):
        blocks = [g.reshape(4, 2, *g.shape[1:]) for g in grads]
        lands = [lax.empty((4, *b.shape[2:]), F32) for b in blocks]
        self.state[tag] = _split_start(f"grads_{tag}_sibling_start", blocks, lands, 4 * len(blocks), _rs_sibling_copies)
        return self.state[tag][4]

    def grads_middle(self, tag, after):
        send_sems, recv_sems, blocks, lands, _ = self.state[tag]
        blocks, from_sibling = _split_wait(f"grads_{tag}_sibling_wait", send_sems, recv_sems, blocks, lands, [after], _rs_sibling_copies)
        sums = [_chip_sum(f"grads_{tag}_chip_sum_{k}", b, s, self.core) for k, (b, s) in enumerate(zip(blocks, from_sibling))]
        lands = [lax.empty((3, *p.shape[1:]), BF16) for p in sums]
        self.state[tag] = (blocks, from_sibling, _split_start(f"grads_{tag}_chips_start", sums, lands, 3 * len(sums), _rs_chip_copies))
        return self.state[tag][2][4]

    def grads_end(self, tag, after):
        blocks, from_sibling, (send_sems, recv_sems, sums, lands, _) = self.state[tag]
        _, from_chips = _split_wait(f"grads_{tag}_chips_wait", send_sems, recv_sems, sums, lands, [after], _rs_chip_copies)
        self.layers[tag] = list(zip(blocks, from_sibling, from_chips))


SHARDED = ("w_in_e", "w_qb", "w_kvb", "w_out_e", "w_in_o", "w_out_o", "w_ff1", "w_ff2")


def kernel(x, positions, w_in_e, mla_gq, mla_gkv, w_qb, w_kvb, sgu_ln_g, sgu_ln_b, sgu_w, sgu_b, w_out_e, w_in_o, hg_lb, hg_gnorm, w_out_o, ln1_g, ln1_b, w_ff1, w_ff2, ln2_g, ln2_b, loss_target, m_w_in_e, m_mla_gq, m_mla_gkv, m_w_qb, m_w_kvb, m_sgu_ln_g, m_sgu_ln_b, m_sgu_w, m_sgu_b, m_w_out_e, m_w_in_o, m_hg_lb, m_hg_gnorm, m_w_out_o, m_ln1_g, m_ln1_b, m_w_ff1, m_w_ff2, m_ln2_g, m_ln2_b, v_w_in_e, v_mla_gq, v_mla_gkv, v_w_qb, v_w_kvb, v_sgu_ln_g, v_sgu_ln_b, v_sgu_w, v_sgu_b, v_w_out_e, v_w_in_o, v_hg_lb, v_hg_gnorm, v_w_out_o, v_ln1_g, v_ln1_b, v_w_ff1, v_w_ff2, v_ln2_g, v_ln2_b):
    given = dict(locals())
    ex = _Exchange(given)

    names = ["w_in_e", "w_qb", "w_kvb", "w_out_e"]
    placed = _place_own([(given[n], 0, BF16) for n in names] + [(hg_gnorm.reshape(1, 1, D_MODEL // N_DEV), 0, F32)]
                        + [(w_in_o, 0, BF16), (w_out_o, 0, BF16), (w_ff1, 0, BF16), (w_ff1, 1, BF16), (w_ff2, 0, BF16), (w_ff2, 1, BF16)],
                        ex.dev)
    got = _all_gather(placed[:5])
    ex.start_weights(placed[5:], after=[got[0]])
    gw = dict(zip(names, got[:4]))
    small_names = ["mla_gq", "mla_gkv", "sgu_ln_g", "sgu_ln_b", "sgu_w", "sgu_b", "hg_lb", "ln1_g", "ln1_b", "ln2_g", "ln2_b"]
    sp = {n: given[n] for n in small_names}
    sp["hg_gnorm"] = got[4].reshape(1, D_MODEL)

    sq_err, dx, grads, gs = _local_step(x[0], positions[0], loss_target[0], gw, sp, ex)
    loss = lax.psum(0.5 * jnp.sum(sq_err) / D_MODEL, ("x", "y", "c"))

    blocks = [grads[n].reshape(4, 2, *grads[n].shape[1:]) for n in names]
    from_sibling = _rs_sibling(blocks)
    chip_sums = [_chip_sum(f"grads_l0_chip_sum_{k}", b, s, ex.core) for k, (b, s) in enumerate(zip(blocks, from_sibling))]
    from_chips = _rs_chips(chip_sums)
    per_weight = dict(zip(names, [[l] for l in zip(blocks, from_sibling, from_chips)]))
    l1, l0m = ex.layers["l1"], ex.layers["l0m"]
    per_weight.update(w_ff1=[l0m[0], l1[0]], w_ff2=[l0m[1], l1[1]], w_in_o=[l1[2]], w_out_o=[l1[3]])
    results = {n: _finish_sharded(f"finish_{n}", per_weight[n], given[n], given["m_" + n], given["v_" + n], ex.where) for n in SHARDED}

    results.update(_small_reduce_adamw(gs, given))

    order = ["w_in_e", "mla_gq", "mla_gkv", "w_qb", "w_kvb", "sgu_ln_g", "sgu_ln_b", "sgu_w", "sgu_b", "w_out_e", "w_in_o",
             "hg_lb", "hg_gnorm", "w_out_o", "ln1_g", "ln1_b", "w_ff1", "w_ff2", "ln2_g", "ln2_b"]
    return (loss, dx[None], *[results[name][kind] for kind in range(4) for name in order])
```

```python
import functools
import math

import jax
import jax.numpy as jnp
import numpy as np
from jax import lax
from jax.experimental import pallas as pl
from jax.experimental.pallas import tpu as pltpu

F32 = jnp.float32
BF16 = jnp.bfloat16
MESH = pl.DeviceIdType.MESH
HIGHEST = lax.Precision.HIGHEST

D_MODEL = 1024
D_FF = 4096
N_DEV = 8
HEADS = 8
HEAD_W = 128
MLA_NOPE = 64
MLA_ROPE = 32
MLA_V = 64
MLA_LORA = 256
MLA_SCALE = (MLA_NOPE + MLA_ROPE) ** -0.5
ROPE_BASE = 10000.0
SGU_DIM = 512
SGU_G = 4
SGU_CHUNK = 128
HG_CHUNK = 64
ALPHA = (2 * 2) ** 0.25
EPS = 1e-5
ADAM_LR, ADAM_B1, ADAM_B2, ADAM_EPS, ADAM_WD, ADAM_STEP = 0.001, 0.9, 0.999, 1e-08, 0.01, 10

VMEM_CAP_V7X = 56 * 2**20
VMEM_SLACK = 12 * 2**20
TM = 512
TN = 512


def _vmem(block_bytes):
    return int(min(VMEM_CAP_V7X, 2 * block_bytes + VMEM_SLACK))


def _hbm(a):
    return pltpu.with_memory_space_constraint(a, pltpu.HBM)


def _nbytes(shape, dtype):
    return int(np.prod([d for d in shape if d is not None])) * jnp.dtype(dtype).itemsize


def _sig(x):
    return 1.0 / (1.0 + jnp.exp(-x))


def _gelu(x):
    c = math.sqrt(2.0 / math.pi)
    t = jnp.tanh(c * (x + 0.044715 * x * x * x))
    return 0.5 * x * (1.0 + t), t


def _gelu_grad(x, t):
    c = math.sqrt(2.0 / math.pi)
    return 0.5 * (1.0 + t) + 0.5 * x * (1.0 - t * t) * c * (1.0 + 3 * 0.044715 * x * x)


def _dot(a, b, dims, precision=None):
    return lax.dot_general(a, b, (dims, ((), ())), preferred_element_type=F32, precision=precision)


NN = ((1,), (0,))
NT = ((1,), (1,))
TN_ = ((0,), (0,))


def _deps(deps):
    return [d for d in deps if d is not None]


def _tiled(name, grid, ins, outs, compute, direct=False, deps=()):
    n_in, deps = len(ins), _deps(deps)
    n_skip = n_in + len(deps)

    def kern(*refs):
        if direct:
            compute(refs[:n_in], refs[n_skip:])
            return
        for o_ref, r in zip(refs[n_skip:], compute(*refs[:n_in])):
            o_ref[...] = r.astype(o_ref.dtype).reshape(o_ref.shape)

    swap = lambda f: (lambda j, i: f(i, j))
    nbytes = sum(_nbytes(blk, a.dtype) for a, blk, _ in ins) + sum(_nbytes(blk, dt) + _nbytes(blk, F32) for _, dt, blk, _ in outs)
    res = pl.pallas_call(
        kern, name=name, grid=grid,
        in_specs=[pl.BlockSpec(blk, swap(f)) for _, blk, f in ins] + [ANY_SPEC] * len(deps),
        out_specs=[pl.BlockSpec(blk, swap(f)) for _, _, blk, f in outs],
        out_shape=[pltpu.HBM(shape, dt) for shape, dt, _, _ in outs],
        compiler_params=pltpu.CompilerParams(dimension_semantics=("parallel", "parallel"), vmem_limit_bytes=_vmem(nbytes)),
    )(*[_hbm(a) for a, _, _ in ins], *deps)
    return res if len(res) > 1 else res[0]


def _rb(a, tm, w=None, cb=0):
    return (a, (tm, a.shape[1] if w is None else w), lambda i, j: (i, cb))


def _rbj(a, tm, tn):
    return (a, (tm, tn), lambda i, j: (i, j))


def _cw(b, tn):
    return (b, (b.shape[0], tn), lambda i, j: (0, j))


def _rw(b, tn):
    return (b, (tn, b.shape[1]), lambda i, j: (j, 0))


def _tl(a, tm):
    return (a, (a.shape[0], tm), lambda i, j: (0, i))


def _gcw(g):
    return (g, (None, g.shape[1], g.shape[2]), lambda i, j: (j, 0, 0))


def _grw(g, tn):
    return (g, (N_DEV, tn, g.shape[2]), lambda i, j: (0, j, 0))


def _out(m, n, dtype, tm, tn):
    return ((m, n), dtype, (tm, tn), lambda i, j: (i, j))


def _out_dev(k, n, tm):
    return ((N_DEV, k, n), F32, (None, tm, n), lambda i, j: (j, i, 0))


def _mmc(dims, n_pairs=1, epilogue=None):
    def compute(*refs):
        acc = None
        for k in range(n_pairs):
            d = _dot(refs[2 * k][...].astype(BF16), refs[2 * k + 1][...].astype(BF16), dims)
            acc = d if acc is None else acc + d
        ext = [r[...] for r in refs[2 * n_pairs:]]
        return epilogue(acc, *ext) if epilogue is not None else (acc,)

    return compute


def _res(w):
    return (w, w.shape, functools.partial(lambda i, j, nd: (0,) * nd, nd=w.ndim))


def _mmc_blocks(nblk, dims, rhs_block, epilogue=None):
    def compute(in_refs, out_refs):
        a = in_refs[0][...].astype(BF16)
        for d in range(nblk):
            acc = _dot(a, rhs_block(in_refs[1], d).astype(BF16), dims)
            n = acc.shape[1]
            ext = [r[:, d * n:(d + 1) * n] for r in in_refs[2:]]
            res = epilogue(acc, *ext) if epilogue is not None else (acc,)
            for o_ref, r in zip(out_refs, res):
                o_ref[:, d * n:(d + 1) * n] = r.astype(o_ref.dtype)

    return compute


def _mmc_dev(epilogue=None):
    def compute(a_ref, b_ref, *ext_refs):
        n = b_ref.shape[2]
        acc = None
        for d in range(N_DEV):
            t = _dot(a_ref[:, d * n:(d + 1) * n].astype(BF16), b_ref[d].astype(BF16), NT)
            acc = t if acc is None else acc + t
        ext = [r[...] for r in ext_refs]
        return epilogue(acc, *ext) if epilogue is not None else (acc,)

    return compute


def _rowwise(name, body, rows, consts, out_rows, out_accs=(), tr=512, deps=()):
    T = rows[0][0].shape[0]
    tr = min(tr, T)
    deps = _deps(deps)
    nr, ncn, no, nd = len(rows), len(consts), len(out_rows), len(deps)

    def kern(*refs):
        accs = refs[nr + ncn + nd + no:]
        if accs:
            @pl.when(pl.program_id(0) == 0)
            def _():
                for a in accs:
                    a[...] = jnp.zeros(a.shape, a.dtype)
        body(refs[:nr], refs[nr:nr + ncn], refs[nr + ncn + nd:nr + ncn + nd + no], accs)

    in_specs = [pl.BlockSpec((tr, w), functools.partial(lambda i, cb: (i, cb), cb=cb)) for _, w, cb in rows]
    in_specs += [pl.BlockSpec(c.shape, functools.partial(lambda i, nd: (0,) * nd, nd=c.ndim)) for c in consts]
    in_specs += [ANY_SPEC] * nd
    out_specs = [pl.BlockSpec((tr, w), lambda i: (i, 0)) for w, _ in out_rows]
    out_specs += [pl.BlockSpec(s, functools.partial(lambda i, nd: (0,) * nd, nd=len(s))) for s, _ in out_accs]
    out_shape = [pltpu.HBM((T, w), dt) for w, dt in out_rows]
    out_shape += [pltpu.HBM(s, dt) for s, dt in out_accs]
    nbytes = sum(_nbytes((tr, w), a.dtype) for a, w, _ in rows) + sum(_nbytes(c.shape, c.dtype) for c in consts)
    nbytes += sum(_nbytes((tr, w), dt) for w, dt in out_rows) + sum(_nbytes(s, dt) for s, dt in out_accs)
    res = pl.pallas_call(
        kern, name=name, grid=(T // tr,), in_specs=in_specs, out_specs=out_specs, out_shape=out_shape,
        compiler_params=pltpu.CompilerParams(dimension_semantics=("arbitrary",), vmem_limit_bytes=_vmem(nbytes)),
    )(*[_hbm(a) for a, _, _ in rows], *[_hbm(c) for c in consts], *deps)
    return res if len(res) > 1 else res[0]


def _full(a):
    return (a, a.shape[1], 0)


def _ln_stats(y):
    mu = jnp.mean(y, axis=-1, keepdims=True)
    yc = y - mu
    r = lax.rsqrt(jnp.mean(yc * yc, axis=-1, keepdims=True) + EPS)
    return yc * r, r


def _ln_back(dh, xh, r, gain, dg_ref, db_ref):
    dg_ref[...] += jnp.sum(dh * xh, axis=0, keepdims=True)
    db_ref[...] += jnp.sum(dh, axis=0, keepdims=True)
    dx = dh * gain
    return r * (dx - jnp.mean(dx, axis=-1, keepdims=True) - xh * jnp.mean(dx * xh, axis=-1, keepdims=True))


def _proj_ln(name, acts, weights, h_in, g, b, layer):
    n = len(acts)

    def body(rows, consts, outs, accs):
        acc = None
        for k in range(n):
            d = _dot(rows[k][...].astype(BF16), consts[k][...], NN)
            acc = d if acc is None else acc + d
        y = ALPHA * rows[n][...] + acc
        xh, _ = _ln_stats(y)
        h = xh * consts[n][layer:layer + 1, :] + consts[n + 1][layer:layer + 1, :]
        outs[0][...] = y
        outs[1][...] = h
        outs[2][...] = h.astype(BF16)

    return _rowwise(name, body, [_full(a) for a in acts] + [_full(h_in)], [*weights, g, b],
                    [(D_MODEL, F32), (D_MODEL, F32), (D_MODEL, BF16)], tr=256)


def _proj_ln_loss(name, act, w2, h_in, g, b, layer, target):
    def body(rows, consts, outs, accs):
        y = ALPHA * rows[1][...] + _dot(rows[0][...], consts[0][...], NN)
        xh, r = _ln_stats(y)
        gain = consts[1][layer:layer + 1, :]
        err = xh * gain + consts[2][layer:layer + 1, :] - rows[2][...]
        accs[0][...] += jnp.sum(err * err, axis=0, keepdims=True)
        dy = _ln_back(err * (1.0 / D_MODEL), xh, r, gain, accs[1], accs[2])
        outs[0][...] = dy
        outs[1][...] = dy.astype(BF16)

    return _rowwise(name, body, [_full(act), _full(h_in), _full(target)], [w2, g, b], [(D_MODEL, F32), (D_MODEL, BF16)],
                    [((1, D_MODEL), F32)] * 3, tr=256)


def _dh_ln_back(name, da, w, dy_next, y, g, layer, proj=None, deps=()):
    def body(rows, consts, outs, accs):
        n = consts[0].shape[2]
        acc = ALPHA * rows[1][...]
        for d in range(N_DEV):
            acc = acc + _dot(rows[0][:, d * n:(d + 1) * n], consts[0][d], NT)
        xh, r = _ln_stats(rows[2][...])
        dy = _ln_back(acc, xh, r, consts[1][layer:layer + 1, :], accs[0], accs[1])
        outs[0][...] = dy
        outs[1][...] = dy.astype(BF16)
        if proj is not None:
            outs[2][...] = _dot(dy.astype(BF16), consts[2][...], NT).astype(BF16)

    out_rows = [(D_MODEL, F32), (D_MODEL, BF16)] + ([(proj.shape[0], BF16)] if proj is not None else [])
    return _rowwise(name, body, [_full(da), _full(dy_next), _full(y)], [w, g] + ([proj] if proj is not None else []), out_rows,
                    [((1, D_MODEL), F32)] * 2, tr=256, deps=deps)


def _relu2_epilogue(acc):
    a = jnp.maximum(acc, 0.0)
    return acc, a * a


def _mlp_up(tag, h_bf, w1):
    T = h_bf.shape[0]
    tm = min(TM, T)
    return _tiled(f"{tag}_ff1", (1, T // tm), [_rb(h_bf, tm), _res(w1)],
                  [_out(T, D_FF, BF16, tm, D_FF), _out(T, D_FF, BF16, tm, D_FF)],
                  _mmc_blocks(N_DEV, NN, lambda w, d: w[d], epilogue=_relu2_epilogue), direct=True)


def _mlp_bwd_w(tag, h_bf, a, act, dff_bf, w2, deps=()):
    T = h_bf.shape[0]
    tm = min(TM, T)
    da = _tiled(f"{tag}_dact", (1, T // tm), [_rb(dff_bf, tm), _res(w2), _rb(a, tm)], [_out(T, D_FF, BF16, tm, D_FF)],
                _mmc_blocks(N_DEV, NT, lambda w, d: w[d], epilogue=lambda acc, a_t: (acc * 2.0 * jnp.maximum(a_t.astype(F32), 0.0),)),
                direct=True, deps=deps)
    dw2 = _tiled(f"{tag}_dw2", (1, D_FF // TM), [_tl(act, TM), _res(dff_bf)],
                 [_out(D_FF, D_MODEL, F32, TM, D_MODEL)], _mmc(TN_)).reshape(N_DEV, D_FF // N_DEV, D_MODEL)
    dw1 = _tiled(f"{tag}_dw1", (N_DEV, 1), [_res(h_bf), _cw(da, TN)], [_out_dev(D_MODEL, TN, D_MODEL)], _mmc(TN_))
    return da, dw1, dw2


def _rope_tables(positions_col, invf_lane):
    def body(rows, consts, outs, accs):
        ang = rows[0][...].astype(F32) * consts[0][...]
        c, s = jnp.cos(ang), jnp.sin(ang)
        lane = lax.broadcasted_iota(jnp.int32, ang.shape, 1)
        outs[0][...] = jnp.where(lane < 64, 1.0, jnp.where(lane < 96, c, 0.0))
        outs[1][...] = jnp.where((lane >= 64) & (lane < 80), -s, 0.0)
        outs[2][...] = jnp.where((lane >= 80) & (lane < 96), s, 0.0)

    return _rowwise("rope_tables", body, [_full(positions_col)], [invf_lane], [(HEAD_W, F32)] * 3)


def _rope(x, c, s1, s2):
    return x * c + pltpu.roll(x, 112, 1) * s1 + pltpu.roll(x, 16, 1) * s2


def _rope_t(dx, c, s1, s2):
    return dx * c + pltpu.roll(dx * s1, 16, 1) + pltpu.roll(dx * s2, 112, 1)


def _rms(c):
    r = lax.rsqrt(jnp.mean(c * c, axis=-1, keepdims=True) + EPS)
    return c * r, r


def _rope_heads(x, c, s1, s2, fn):
    return jnp.concatenate([fn(x[:, h * HEAD_W:(h + 1) * HEAD_W], c, s1, s2) for h in range(HEADS)], axis=1)


def _mla_in(x, wm, ws, tabs, gq, gkv, deps=()):
    def body(rows, consts, outs, accs):
        xb = rows[0][...].astype(BF16)
        zm = _dot(xb, consts[0][...], NN)
        outs[0][...] = zm
        outs[1][...] = _dot(xb, consts[1][...], NN)
        outs[2][...] = (_rms(zm[:, 0:256])[0] * consts[2][...]).astype(BF16)
        outs[3][...] = (_rms(zm[:, 256:512])[0] * consts[3][...]).astype(BF16)
        outs[4][...] = _rope(zm[:, 512:640], rows[1][...], rows[2][...], rows[3][...])

    return _rowwise("l0_in", body, [_full(x)] + [_full(t) for t in tabs], [wm, ws, gq, gkv],
                    [(640, F32), (1024, F32), (256, BF16), (256, BF16), (HEAD_W, F32)], deps=deps)


def _mla_qkv(cqn, ckvn, kr_rot, tabs, wq, wk, wv):
    def body(rows, consts, outs, accs):
        c, s1, s2 = rows[3][...], rows[4][...], rows[5][...]
        outs[0][...] = _rope_heads(_dot(rows[0][...], consts[0][...], NN), c, s1, s2, _rope).astype(BF16)
        outs[1][...] = (_dot(rows[1][...], consts[1][...], NN) + jnp.concatenate([rows[2][...]] * HEADS, axis=1)).astype(BF16)
        outs[2][...] = _dot(rows[1][...], consts[2][...], NN).astype(BF16)

    rows = [_full(cqn), _full(ckvn), _full(kr_rot)] + [_full(t) for t in tabs]
    return _rowwise("l0_qkv", body, rows, [wq, wk, wv], [(HEADS * HEAD_W, BF16)] * 3)


def _mla_back(zm, cqn, ckvn, tabs, gq, gkv, wq, wk, wv, dq, dk, dv):
    def body(rows, consts, outs, accs):
        c, s1, s2 = rows[4][...], rows[5][...], rows[6][...]
        dk_t, dv_bf = rows[8][...], rows[9][...].astype(BF16)
        dq_bf = _rope_heads(rows[7][...], c, s1, s2, _rope_t).astype(BF16)
        dk_bf = dk_t.astype(BF16)
        accs[0][...] += _dot(rows[2][...], dq_bf, TN_)
        accs[1][...] += _dot(rows[3][...], dk_bf, TN_)
        accs[2][...] += _dot(rows[3][...], dv_bf, TN_)
        dlat = [_dot(dq_bf, consts[2][...], NT), _dot(dk_bf, consts[3][...], NT) + _dot(dv_bf, consts[4][...], NT)]
        for k in range(2):
            ch, r = _rms(rows[k][...])
            accs[3 + k][...] += jnp.sum(dlat[k] * ch, axis=0, keepdims=True)
            dc = dlat[k] * consts[k][...]
            outs[0][:, 256 * k:256 * (k + 1)] = (r * (dc - ch * jnp.mean(dc * ch, axis=-1, keepdims=True))).astype(BF16)
        dks = dk_t[:, 0:HEAD_W]
        for h in range(1, HEADS):
            dks = dks + dk_t[:, h * HEAD_W:(h + 1) * HEAD_W]
        lane = lax.broadcasted_iota(jnp.int32, dks.shape, 1)
        dks = jnp.where((lane >= 64) & (lane < 96), dks, 0.0)
        outs[0][:, 512:640] = _rope_t(dks, c, s1, s2).astype(BF16)

    rows = [(zm, 256, 0), (zm, 256, 1), _full(cqn), _full(ckvn)] + [_full(t) for t in tabs] + [_full(dq), _full(dk), _full(dv)]
    wide = HEADS * HEAD_W
    return _rowwise("l0_mla_back", body, rows, [gq, gkv, wq, wk, wv], [(640, BF16)],
                    [((MLA_LORA, wide), F32)] * 3 + [((1, MLA_LORA), F32)] * 2, tr=256)


def _in_back(x, dzm, dzs, dy, wm, ws):
    def body(rows, consts, outs, accs):
        dzm_t, dzs_t = rows[1][...], rows[2][...]
        outs[0][...] = _dot(dzm_t, consts[0][...], NT) + _dot(dzs_t, consts[1][...], NT) + ALPHA * rows[3][...]
        xb = rows[0][...].astype(BF16)
        accs[0][...] += _dot(xb, dzm_t, TN_)
        accs[1][...] += _dot(xb, dzs_t, TN_)

    return _rowwise("l0_in_back", body, [_full(x), _full(dzm), _full(dzs), _full(dy)], [wm, ws], [(D_MODEL, F32)],
                    [((D_MODEL, 640), F32), ((D_MODEL, 1024), F32)])


def _out_weight_grads(o_att, b_out, dy_bf):
    def body(rows, consts, outs, accs):
        d = rows[2][...]
        accs[0][...] += _dot(rows[0][...].astype(BF16), d, TN_)
        accs[1][...] += _dot(rows[1][...], d, TN_)

    return _rowwise("l0_dw_out", body, [_full(o_att), _full(b_out), _full(dy_bf)], [], [],
                    [((HEADS * HEAD_W, D_MODEL), F32), ((SGU_DIM, D_MODEL), F32)])


def _attn_block(T):
    return min(1024, T)


def _attn_fwd(q, k, v):
    T = q.shape[0]
    BQ = _attn_block(T)
    nq = T // BQ

    def kern(q_ref, k_ref, v_ref, o_ref, lse_ref):
        def step(i, j, carry, masked):
            m, l, acc = carry
            qb = q_ref[pl.ds(pl.multiple_of(i * BQ, BQ), BQ), :]
            kb = k_ref[pl.ds(pl.multiple_of(j * BQ, BQ), BQ), :]
            vb = v_ref[pl.ds(pl.multiple_of(j * BQ, BQ), BQ), :]
            s = _dot(qb, kb, NT) * MLA_SCALE
            if masked:
                row = lax.broadcasted_iota(jnp.int32, s.shape, 0)
                col = lax.broadcasted_iota(jnp.int32, s.shape, 1)
                s = jnp.where(col <= row, s, -1e30)
            m_new = jnp.maximum(m, jnp.max(s, axis=-1, keepdims=True))
            p = jnp.exp(s - m_new)
            a = jnp.exp(m - m_new)
            l = a * l + jnp.sum(p, axis=-1, keepdims=True)
            acc = a * acc + _dot(p.astype(BF16), vb, NN)
            return m_new, l, acc

        def qloop(i, _):
            init = (jnp.full((BQ, 1), -1e30, F32), jnp.zeros((BQ, 1), F32), jnp.zeros((BQ, HEAD_W), F32))
            carry = lax.fori_loop(0, i, lambda j, c: step(i, j, c, False), init)
            m, l, acc = step(i, i, carry, True)
            rows = pl.ds(pl.multiple_of(i * BQ, BQ), BQ)
            o_ref[rows, :] = acc / l
            lse_ref[0, rows, :] = m + jnp.log(l)
            return 0

        lax.fori_loop(0, nq, qloop, 0)

    head = pl.BlockSpec((T, HEAD_W), lambda h: (0, h))
    nbytes = 3 * _nbytes((T, HEAD_W), BF16) + _nbytes((T, HEAD_W), F32) + _nbytes((T, 128), F32)
    return pl.pallas_call(
        kern, name="attn_fwd", grid=(HEADS,), in_specs=[head, head, head],
        out_specs=[head, pl.BlockSpec((1, T, 1), lambda h: (h, 0, 0))],
        out_shape=[pltpu.HBM((T, HEADS * HEAD_W), F32), pltpu.HBM((HEADS, T, 1), F32)],
        compiler_params=pltpu.CompilerParams(dimension_semantics=("parallel",), vmem_limit_bytes=_vmem(nbytes)),
    )(_hbm(q), _hbm(k), _hbm(v))


def _attn_bwd(q, k, v, o, lse, dcat, deps=()):
    T = q.shape[0]
    BQ = _attn_block(T)
    nq = T // BQ
    deps = _deps(deps)

    def kern(q_ref, k_ref, v_ref, o_ref, lse_ref, do_ref, *rest):
        dq_ref, dk_ref, dv_ref, dd_ref = rest[len(deps):]
        dq_ref[...] = jnp.zeros(dq_ref.shape, F32)

        def dloop(i, _):
            rows = pl.ds(pl.multiple_of(i * BQ, BQ), BQ)
            dd_ref[rows, :] = jnp.sum(do_ref[rows, :].astype(F32) * o_ref[rows, :], axis=-1, keepdims=True)
            return 0

        lax.fori_loop(0, nq, dloop, 0)

        def step(j, i, carry, masked):
            dk_acc, dv_acc = carry
            rq = pl.ds(pl.multiple_of(i * BQ, BQ), BQ)
            rk = pl.ds(pl.multiple_of(j * BQ, BQ), BQ)
            qb, kb, vb, dob = q_ref[rq, :], k_ref[rk, :], v_ref[rk, :], do_ref[rq, :]
            s = _dot(qb, kb, NT) * MLA_SCALE
            p = jnp.exp(s - lse_ref[0, rq, :])
            if masked:
                row = lax.broadcasted_iota(jnp.int32, s.shape, 0)
                col = lax.broadcasted_iota(jnp.int32, s.shape, 1)
                p = jnp.where(col <= row, p, 0.0)
            dp = _dot(dob, vb, NT)
            ds = (p * (dp - dd_ref[rq, :]) * MLA_SCALE).astype(BF16)
            dv_acc = dv_acc + _dot(p.astype(BF16), dob, TN_)
            dk_acc = dk_acc + _dot(ds, qb, TN_)
            dq_ref[rq, :] += _dot(ds, kb, NN)
            return dk_acc, dv_acc

        def kloop(j, _):
            init = (jnp.zeros((BQ, HEAD_W), F32), jnp.zeros((BQ, HEAD_W), F32))
            carry = step(j, j, init, True)
            dk_acc, dv_acc = lax.fori_loop(j + 1, nq, lambda i, c: step(j, i, c, False), carry)
            rk = pl.ds(pl.multiple_of(j * BQ, BQ), BQ)
            dk_ref[rk, :] = dk_acc
            dv_ref[rk, :] = dv_acc
            return 0

        lax.fori_loop(0, nq, kloop, 0)

    head = pl.BlockSpec((T, HEAD_W), lambda h: (0, h))
    nbytes = 4 * _nbytes((T, HEAD_W), BF16) + 5 * _nbytes((T, HEAD_W), F32) + 2 * _nbytes((T, 128), F32)
    return pl.pallas_call(
        kern, name="attn_bwd", grid=(HEADS,),
        in_specs=[head, head, head, head, pl.BlockSpec((1, T, 1), lambda h: (h, 0, 0)), head] + [ANY_SPEC] * len(deps),
        out_specs=[head, head, head],
        out_shape=[pltpu.HBM((T, HEADS * HEAD_W), F32)] * 3,
        scratch_shapes=[pltpu.VMEM((T, 1), F32)],
        compiler_params=pltpu.CompilerParams(dimension_semantics=("parallel",), vmem_limit_bytes=_vmem(nbytes)),
    )(*[_hbm(a) for a in (q, k, v, o, lse, dcat)], *deps)


def _sgu_common(u, v, ln_g, ln_b):
    ua, tu = _gelu(u)
    va, tv = _gelu(v)
    vh, r = _ln_stats(va)
    return ua, tu, tv, vh, r, vh * ln_g + ln_b


def _tril_mask(n):
    return lax.broadcasted_iota(jnp.int32, (n, n), 1) <= lax.broadcasted_iota(jnp.int32, (n, n), 0)


def _sgu_fwd(zs, ln_g, ln_b, w, bias_full):
    def body(rows, consts, outs, accs):
        ua, _, _, _, _, vn = _sgu_common(rows[0][...], rows[1][...], consts[0][...], consts[1][...])
        vn = vn.astype(BF16)
        tri = _tril_mask(SGU_CHUNK)
        for g in range(SGU_G):
            wg = jnp.where(tri, consts[2][0, g], 0.0).astype(BF16)
            cols = slice(g * 128, (g + 1) * 128)
            for c in range(ua.shape[0] // SGU_CHUNK):
                rws = slice(c * SGU_CHUNK, (c + 1) * SGU_CHUNK)
                mixed = _dot(wg, vn[rws, cols], NN) + consts[3][:, cols]
                outs[0][rws, cols] = (ua[rws, cols] * mixed).astype(BF16)

    return _rowwise("sgu_fwd", body, [(zs, 512, 0), (zs, 512, 1)], [ln_g, ln_b, w, bias_full], [(SGU_DIM, BF16)])


def _sgu_bwd(zs, dcat, ln_g, ln_b, w, bias_full):
    def body(rows, consts, outs, accs):
        u, v = rows[0][...], rows[1][...]
        ua, tu, tv, vh, r, vn = _sgu_common(u, v, consts[0][...], consts[1][...])
        dout = rows[2][...].astype(F32)
        vn_bf = vn.astype(BF16)
        tri = _tril_mask(SGU_CHUNK)
        dmixed = (dout * ua)
        dmixed_bf = dmixed.astype(BF16)
        ones = jnp.ones((8, SGU_CHUNK), F32)
        dvn_cols, mixed_cols = [], []
        for g in range(SGU_G):
            wg = jnp.where(tri, consts[2][0, g], 0.0).astype(BF16)
            cols = slice(g * 128, (g + 1) * 128)
            dvn_rows, mixed_rows = [], []
            dw = jnp.zeros((SGU_CHUNK, SGU_CHUNK), F32)
            dmix_sum = jnp.zeros((SGU_CHUNK, 128), F32)
            for c in range(u.shape[0] // SGU_CHUNK):
                rws = slice(c * SGU_CHUNK, (c + 1) * SGU_CHUNK)
                mixed_rows.append(_dot(wg, vn_bf[rws, cols], NN) + consts[3][:, cols])
                dvn_rows.append(_dot(wg, dmixed_bf[rws, cols], TN_))
                dw = dw + _dot(dmixed_bf[rws, cols], vn_bf[rws, cols], NT)
                dmix_sum = dmix_sum + dmixed[rws, cols]
            accs[0][g] += jnp.where(tri, dw, 0.0)
            accs[3][g:g + 1, :] += _dot(ones, dmix_sum, NT, precision=HIGHEST)[0:1, :]
            dvn_cols.append(jnp.concatenate(dvn_rows, axis=0))
            mixed_cols.append(jnp.concatenate(mixed_rows, axis=0))
        dvn = jnp.concatenate(dvn_cols, axis=1)
        mixed = jnp.concatenate(mixed_cols, axis=1)
        accs[1][...] += jnp.sum(dvn * vh, axis=0, keepdims=True)
        accs[2][...] += jnp.sum(dvn, axis=0, keepdims=True)
        dvh = dvn * consts[0][...]
        dva = r * (dvh - jnp.mean(dvh, axis=-1, keepdims=True) - vh * jnp.mean(dvh * vh, axis=-1, keepdims=True))
        outs[0][:, 0:512] = (dout * mixed * _gelu_grad(u, tu)).astype(BF16)
        outs[0][:, 512:1024] = (dva * _gelu_grad(v, tv)).astype(BF16)

    return _rowwise("sgu_bwd", body, [(zs, 512, 0), (zs, 512, 1), (dcat, 512, 2)], [ln_g, ln_b, w, bias_full], [(1024, BF16)],
                    [((SGU_G, 128, 128), F32), ((1, SGU_DIM), F32), ((1, SGU_DIM), F32), ((SGU_G, 128), F32)], tr=256)


def _lower_bound(hg_lb):
    a0, a1 = hg_lb[0:1, :], hg_lb[1:2, :]
    m = jnp.maximum(a0, a1)
    e0, e1 = jnp.exp(a0 - m), jnp.exp(a1 - m)
    s0, s1 = e0 / (e0 + e1), e1 / (e0 + e1)
    return (s0 + s1) - s0, s0, s1


def _hg_gates(qr, fr, lb):
    C = qr.shape[0]
    sq = _sig(qr)
    qf = qr * sq
    sf = _sig(fr)
    gate = lb + (1.0 - lb) * sf
    kk = 1.0 - gate
    tri = _tril_mask(C)
    b = _dot(jnp.where(tri, 1.0, 0.0), jnp.log(gate), NN, precision=HIGHEST)
    bref = b[C // 2 - 1:C // 2, :]
    bl = b[C - 1:C, :]
    e_b = jnp.exp(b)
    e_q = jnp.exp(b - bref)
    e_k = jnp.exp(bref - b)
    e_lb = jnp.exp(bl - b)
    return dict(sq=sq, qf=qf, sf=sf, gate=gate, kk=kk, tri=tri, bl=bl, e_b=e_b, e_q=e_q, e_k=e_k, e_lb=e_lb)


def _hgrn_fwd(z1, hg_lb, gnorm):
    T = z1.shape[0]
    C = min(HG_CHUNK, T)
    nc = T // C

    def kern(q_ref, f_ref, i_ref, g_ref, lb_ref, gn_ref, o_ref, hg_ref, st_ref, s_scr):
        @pl.when(pl.program_id(0) == 0)
        def _():
            s_scr[...] = jnp.zeros(s_scr.shape, F32)

        lb_all, _, _ = _lower_bound(lb_ref[...])
        st_ref[0] = s_scr[...]
        for h in range(HEADS):
            cols = slice(h * HEAD_W, (h + 1) * HEAD_W)
            t = _hg_gates(q_ref[:, cols], f_ref[:, cols], lb_all[:, cols])
            v = i_ref[:, cols]
            v_bf = v.astype(BF16)
            st = s_scr[h]
            a = jnp.where(t["tri"], _dot((t["qf"] * t["e_q"]).astype(BF16), (t["kk"] * t["e_k"]).astype(BF16), NT), 0.0)
            o = _dot(a.astype(BF16), v_bf, NN) + _dot((t["qf"] * t["e_b"]).astype(BF16), st.astype(BF16), NT)
            s_scr[h] = st * jnp.exp(t["bl"]) + _dot(v_bf, (t["kk"] * t["e_lb"]).astype(BF16), TN_)
            o_ref[:, cols] = o
            gr = g_ref[:, cols]
            r = lax.rsqrt(jnp.mean(o * o, axis=-1, keepdims=True) + EPS)
            hg_ref[:, cols] = (o * r * gn_ref[:, cols] * (gr * _sig(gr))).astype(BF16)

    seg = lambda k: pl.BlockSpec((C, D_MODEL), functools.partial(lambda n, k: (n, k), k=k))
    row = pl.BlockSpec((C, D_MODEL), lambda n: (n, 0))
    nbytes = 6 * _nbytes((C, D_MODEL), F32) + 3 * _nbytes((HEADS, 128, 128), F32)
    return pl.pallas_call(
        kern, name="hgrn_fwd", grid=(nc,),
        in_specs=[seg(0), seg(1), seg(2), seg(3), pl.BlockSpec((2, D_MODEL), lambda n: (0, 0)),
                  pl.BlockSpec((1, D_MODEL), lambda n: (0, 0))],
        out_specs=[row, row, pl.BlockSpec((1, HEADS, 128, 128), lambda n: (n, 0, 0, 0))],
        out_shape=[pltpu.HBM((T, D_MODEL), F32), pltpu.HBM((T, D_MODEL), BF16),
                   pltpu.HBM((nc, HEADS, 128, 128), F32)],
        scratch_shapes=[pltpu.VMEM((HEADS, 128, 128), F32)],
        compiler_params=pltpu.CompilerParams(dimension_semantics=("arbitrary",), vmem_limit_bytes=_vmem(nbytes)),
    )(*[_hbm(a) for a in (z1, z1, z1, z1, hg_lb, gnorm)])


def _hgrn_bwd(z1, o_pre, dhg, states, hg_lb, gnorm):
    T = z1.shape[0]
    C = min(HG_CHUNK, T)
    nc = T // C

    def kern(q_ref, f_ref, i_ref, g_ref, o_ref, dhg_ref, st_ref, lb_ref, gn_ref, dz_ref, dlb_ref, dgn_ref, ds_scr, dlb_scr):
        n = pl.program_id(0)

        @pl.when(n == 0)
        def _():
            ds_scr[...] = jnp.zeros(ds_scr.shape, F32)
            dlb_scr[...] = jnp.zeros(dlb_scr.shape, F32)
            dgn_ref[...] = jnp.zeros(dgn_ref.shape, F32)

        lb_all, s0, s1 = _lower_bound(lb_ref[...])
        for h in range(HEADS):
            cols = slice(h * HEAD_W, (h + 1) * HEAD_W)
            lb = lb_all[:, cols]
            qr, fr = q_ref[:, cols], f_ref[:, cols]
            t = _hg_gates(qr, fr, lb)
            tri = t["tri"]
            v_bf = i_ref[:, cols].astype(BF16)
            st_bf = st_ref[0, h].astype(BF16)
            dst = ds_scr[h]
            dst_bf = dst.astype(BF16)
            o = o_ref[:, cols]
            gr = g_ref[:, cols]
            sg = _sig(gr)
            sil = gr * sg
            gn = gn_ref[:, cols]
            r = lax.rsqrt(jnp.mean(o * o, axis=-1, keepdims=True) + EPS)
            on = o * r
            dh = dhg_ref[:, cols].astype(F32)
            dgn_ref[:, cols] += jnp.sum(dh * on * sil, axis=0, keepdims=True)
            dg = dh * on * gn * (sg * (1.0 + gr * (1.0 - sg)))
            don = dh * gn * sil
            do_bf = (r * (don - on * jnp.mean(don * on, axis=-1, keepdims=True))).astype(BF16)
            qe = (t["qf"] * t["e_q"]).astype(BF16)
            ke = (t["kk"] * t["e_k"]).astype(BF16)
            qb = (t["qf"] * t["e_b"]).astype(BF16)
            kh_bf = (t["kk"] * t["e_lb"]).astype(BF16)
            a_bf = jnp.where(tri, _dot(qe, ke, NT), 0.0).astype(BF16)
            da_bf = jnp.where(tri, _dot(do_bf, v_bf, NT), 0.0).astype(BF16)
            dv = _dot(a_bf, do_bf, TN_) + _dot(kh_bf, dst_bf, NT)
            dqe = _dot(da_bf, ke, NN)
            dqb = _dot(do_bf, st_bf, NN)
            dke = _dot(da_bf, qe, TN_)
            dkh = _dot(v_bf, dst_bf, NN)
            dqf = dqe * t["e_q"] + dqb * t["e_b"]
            dkk = dke * t["e_k"] + dkh * t["e_lb"]
            kh_r = kh_bf.astype(F32)
            db = qe.astype(F32) * dqe - ke.astype(F32) * dke + qb.astype(F32) * dqb - kh_r * dkh
            e_bl = jnp.exp(t["bl"])
            dbl = jnp.sum(dkh * kh_r, axis=0, keepdims=True) + e_bl * jnp.sum(st_ref[0, h] * dst, axis=0, keepdims=True)
            dlg = _dot(jnp.where(tri, 1.0, 0.0), db, TN_, precision=HIGHEST) + dbl
            ds_scr[h] = dst * e_bl + _dot(do_bf, qb, TN_)
            dgate = dlg / t["gate"] - dkk
            sf = t["sf"]
            dlb_scr[:, cols] += jnp.sum(dgate * (1.0 - sf), axis=0, keepdims=True)
            df = dgate * (1.0 - lb) * sf * (1.0 - sf)
            dq = dqf * (t["sq"] * (1.0 + qr * (1.0 - t["sq"])))
            dz_ref[:, cols] = dq.astype(BF16)
            dz_ref[:, D_MODEL + h * HEAD_W:D_MODEL + (h + 1) * HEAD_W] = df.astype(BF16)
            dz_ref[:, 2 * D_MODEL + h * HEAD_W:2 * D_MODEL + (h + 1) * HEAD_W] = dv.astype(BF16)
            dz_ref[:, 3 * D_MODEL + h * HEAD_W:3 * D_MODEL + (h + 1) * HEAD_W] = dg.astype(BF16)

        @pl.when(n == nc - 1)
        def _():
            d = s0 * s1 * dlb_scr[...]
            dlb_ref[0:1, :] = -d
            dlb_ref[1:2, :] = d

    seg = lambda k: pl.BlockSpec((C, D_MODEL), functools.partial(lambda n, k: (nc - 1 - n, k), k=k))
    nbytes = 6 * _nbytes((C, D_MODEL), F32) + _nbytes((C, 4 * D_MODEL), BF16) + 3 * _nbytes((HEADS, 128, 128), F32)
    return pl.pallas_call(
        kern, name="hgrn_bwd", grid=(nc,),
        in_specs=[seg(0), seg(1), seg(2), seg(3), seg(0), seg(0),
                  pl.BlockSpec((1, HEADS, 128, 128), lambda n: (nc - 1 - n, 0, 0, 0)),
                  pl.BlockSpec((2, D_MODEL), lambda n: (0, 0)), pl.BlockSpec((1, D_MODEL), lambda n: (0, 0))],
        out_specs=[pl.BlockSpec((C, 4 * D_MODEL), lambda n: (nc - 1 - n, 0)),
                   pl.BlockSpec((2, D_MODEL), lambda n: (0, 0)), pl.BlockSpec((1, D_MODEL), lambda n: (0, 0))],
        out_shape=[pltpu.HBM((T, 4 * D_MODEL), BF16), pltpu.HBM((2, D_MODEL), F32),
                   pltpu.HBM((1, D_MODEL), F32)],
        scratch_shapes=[pltpu.VMEM((HEADS, 128, 128), F32), pltpu.VMEM((1, D_MODEL), F32)],
        compiler_params=pltpu.CompilerParams(dimension_semantics=("arbitrary",), vmem_limit_bytes=_vmem(nbytes)),
    )(*[_hbm(a) for a in (z1, z1, z1, z1, o_pre, dhg, states, hg_lb, gnorm)])


def _prep_weights(gw):
    w_in_e = gw["w_in_e"].transpose(1, 0, 2).reshape(D_MODEL, 1568)
    kr = jnp.pad(w_in_e[:, 512:544], ((0, 0), (64, 32)))
    wm = jnp.concatenate([w_in_e[:, 0:512], kr], axis=1)
    ws = w_in_e[:, 544:1568]
    w_qb = gw["w_qb"].transpose(1, 0, 2).reshape(MLA_LORA, HEADS, 96)
    wq = jnp.pad(w_qb, ((0, 0), (0, 0), (0, 32))).reshape(MLA_LORA, HEADS * HEAD_W)
    kvb = gw["w_kvb"].transpose(1, 0, 2).reshape(MLA_LORA, HEADS, 128)
    wk = jnp.pad(kvb[:, :, :64], ((0, 0), (0, 0), (0, 64))).reshape(MLA_LORA, HEADS * HEAD_W)
    wv = jnp.pad(kvb[:, :, 64:], ((0, 0), (0, 0), (0, 64))).reshape(MLA_LORA, HEADS * HEAD_W)
    w_out_e = gw["w_out_e"].reshape(D_MODEL, D_MODEL)
    woa = jnp.pad(w_out_e[:512].reshape(HEADS, 64, D_MODEL), ((0, 0), (0, 64), (0, 0))).reshape(HEADS * HEAD_W, D_MODEL)
    return dict(wm=wm, ws=ws, wq=wq, wk=wk, wv=wv, woa=woa, wob=w_out_e[512:])


def _unprep_grads(g):
    dwm, dws = g["wm"], g["ws"]
    d_in_e = jnp.concatenate([dwm[:, 0:512], dwm[:, 512 + 64:512 + 96], dws], axis=1)
    d_qb = g["wq"].reshape(MLA_LORA, HEADS, HEAD_W)[:, :, :96].reshape(MLA_LORA, HEADS * 96)
    dk = g["wk"].reshape(MLA_LORA, HEADS, HEAD_W)[:, :, :64]
    dv = g["wv"].reshape(MLA_LORA, HEADS, HEAD_W)[:, :, :64]
    d_kvb = jnp.concatenate([dk, dv], axis=2).reshape(MLA_LORA, HEADS * 128)
    d_oa = g["woa"].reshape(HEADS, HEAD_W, D_MODEL)[:, :64].reshape(HEADS * 64, D_MODEL)
    dev_major = lambda a: a.reshape(a.shape[0], N_DEV, a.shape[1] // N_DEV).transpose(1, 0, 2)
    return dict(w_in_e=dev_major(d_in_e), w_qb=dev_major(d_qb), w_kvb=dev_major(d_kvb),
                w_out_e=jnp.concatenate([d_oa, g["wob"]], axis=0).reshape(N_DEV, D_MODEL // N_DEV, D_MODEL))


def _local_step(x, positions, target, gw, sp, ex):
    w = _prep_weights(gw)
    T = x.shape[0]
    tm = min(TM, T)
    nt = T // tm
    half = MLA_ROPE // 2
    inv_freq = ROPE_BASE ** (-jnp.arange(half, dtype=F32) / half)
    invf_lane = jnp.concatenate([jnp.zeros((64,), F32), inv_freq, inv_freq, jnp.zeros((32,), F32)]).reshape(1, HEAD_W)
    tabs = _rope_tables(positions.reshape(T, 1), invf_lane)
    bias_full = jnp.repeat(sp["sgu_b"][0].T, 128, axis=1)
    sgu_w = sp["sgu_w"]
    gq, gkv = sp["mla_gq"], sp["mla_gkv"]
    ln1_g, ln1_b, ln2_g, ln2_b = sp["ln1_g"], sp["ln1_b"], sp["ln2_g"], sp["ln2_b"]
    zm, zs, cqn, ckvn, kr_rot = _mla_in(x, w["wm"], w["ws"], tabs, gq, gkv, deps=[ex.first_token])
    q, k, v = _mla_qkv(cqn, ckvn, kr_rot, tabs, w["wq"], w["wk"], w["wv"])
    o_att, lse = _attn_fwd(q, k, v)
    b_out = _sgu_fwd(zs, sp["sgu_ln_g"], sp["sgu_ln_b"], sgu_w, bias_full)
    y1, h1, h1_bf = _proj_ln("l0_out_ln1", [o_att, b_out], [w["woa"], w["wob"]], x, ln1_g, ln1_b, 0)
    big = ex.weights_ready(after=y1)
    w_ff1, w_in_o, w_out_o = big["w_ff1"], big["w_in_o"], big["w_out_o"].reshape(D_MODEL, D_MODEL)
    w_ff2 = [a.reshape(D_FF, D_MODEL) for a in big["w_ff2"]]
    a0, act0 = _mlp_up("l0", h1_bf, w_ff1[0])
    y2, h2, h2_bf = _proj_ln("l0_ff2_ln2", [act0], [w_ff2[0]], h1, ln2_g, ln2_b, 0)

    z1 = _tiled("l1_in", (1, nt), [_rb(h2_bf, tm), _res(w_in_o)], [_out(T, 4 * D_MODEL, F32, tm, 4 * D_MODEL)],
                _mmc_blocks(N_DEV, NN, lambda w, d: w[d]), direct=True)
    o_pre, hg, states = _hgrn_fwd(z1, sp["hg_lb"], sp["hg_gnorm"])
    y3, h3, h3_bf = _proj_ln("l1_out_ln1", [hg], [w_out_o], h2, ln1_g, ln1_b, 1)
    a1, act1 = _mlp_up("l1", h3_bf, w_ff1[1])

    gs, g0 = {}, {}
    dy4, dy4_bf, sq_err, gs["ln2_g1"], gs["ln2_b1"] = _proj_ln_loss("l1_ff2_loss", act1, w_ff2[1], h3, ln2_g, ln2_b, 1, target)
    da1, dw1_1, dw2_1 = _mlp_bwd_w("l1", h3_bf, a1, act1, dy4_bf, big["w_ff2"][1])
    dy3, dy3_bf, dhg, gs["ln1_g1"], gs["ln1_b1"] = _dh_ln_back("l1_dh_ln1", da1, w_ff1[1], dy4, y3, ln1_g, 1, proj=w_out_o)
    d_out_o = _tiled("l1_dwout", (2, D_MODEL // TM), [_tl(hg, TM), _cw(dy3_bf, TN)], [_out(D_MODEL, D_MODEL, F32, TM, TN)],
                     _mmc(TN_)).reshape(N_DEV, D_MODEL // N_DEV, D_MODEL)
    dz1, gs["hg_lb"], gs["hg_gnorm"] = _hgrn_bwd(z1, o_pre, dhg, states, sp["hg_lb"], sp["hg_gnorm"])
    d_in_o = _tiled("l1_dwin", (N_DEV, 1), [_res(h2_bf), _cw(dz1, TN)], [_out_dev(D_MODEL, TN, D_MODEL)], _mmc(TN_))
    token = ex.grads_start("l1", [dw1_1, dw2_1, d_in_o, d_out_o])

    dy2, dy2_bf, gs["ln2_g0"], gs["ln2_b0"] = _dh_ln_back("l1_dh_ln2", dz1, w_in_o, dy3, y2, ln2_g, 0, deps=[token])
    token = ex.grads_middle("l1", after=dy2)
    da0, dw1_0, dw2_0 = _mlp_bwd_w("l0", h1_bf, a0, act0, dy2_bf, big["w_ff2"][0], deps=[token])
    token = ex.grads_start("l0m", [dw1_0, dw2_0])
    wo_cat = jnp.concatenate([w["woa"], w["wob"]], axis=0)
    dy1, dy1_bf, dcat, gs["ln1_g0"], gs["ln1_b0"] = _dh_ln_back("l0_dh_ln1", da0, w_ff1[0], dy2, y1, ln1_g, 0, proj=wo_cat, deps=[token])
    ex.grads_end("l1", after=dy1)
    g0["woa"], g0["wob"] = _out_weight_grads(o_att, b_out, dy1_bf)
    token = ex.grads_middle("l0m", after=g0["wob"])
    dzs, gs["sgu_w"], gs["sgu_ln_g"], gs["sgu_ln_b"], gs["sgu_b"] = _sgu_bwd(zs, dcat, sp["sgu_ln_g"], sp["sgu_ln_b"], sgu_w, bias_full)
    dq, dk, dv = _attn_bwd(q, k, v, o_att, lse, dcat, deps=[token])
    ex.grads_end("l0m", after=dq)
    dzm, g0["wq"], g0["wk"], g0["wv"], gs["mla_gq"], gs["mla_gkv"] = _mla_back(zm, cqn, ckvn, tabs, gq, gkv, w["wq"], w["wk"], w["wv"],
                                                                                 dq, dk, dv)
    dx, g0["wm"], g0["ws"] = _in_back(x, dzm, dzs, dy1, w["wm"], w["ws"])

    return sq_err, dx, _unprep_grads(g0), gs


def _me():
    return lax.axis_index("x"), lax.axis_index("y"), lax.axis_index("c")


def _hbm_call(name, kern, operands, out_shape, n_sems, extra_scratch=()):
    any_spec = pl.BlockSpec(memory_space=pl.ANY)
    return pl.pallas_call(
        kern, name=name, out_shape=out_shape, in_specs=[any_spec] * len(operands), out_specs=[any_spec] * len(out_shape),
        scratch_shapes=[pltpu.SemaphoreType.DMA((n_sems,)), pltpu.SemaphoreType.DMA((n_sems,)), *extra_scratch],
    )(*[_hbm(a) for a in operands])


ANY_SPEC = pl.BlockSpec(memory_space=pl.ANY)
HBM_SPEC = pl.BlockSpec(memory_space=pltpu.HBM)
SEM_SPEC = pl.BlockSpec(memory_space=pltpu.SEMAPHORE)
EFFECT = pltpu.SideEffectType.DATAFLOW_SIDE_EFFECTING


def _split_start(name, srcs, lands, n_sems, make_copies, after=()):
    n, m, k = len(srcs), len(lands), len(after)

    def body(*refs):
        for cp in make_copies(refs[:n], refs[n:n + m], refs[n + m + k], refs[n + m + k + 1]):
            cp.start()
        refs[-1][...] = jnp.zeros(refs[-1].shape, F32)

    out_shape = (pltpu.SemaphoreType.DMA((n_sems,)), pltpu.SemaphoreType.DMA((n_sems,)),
                 *[pltpu.HBM(a.shape, a.dtype) for a in (*srcs, *lands)], jax.ShapeDtypeStruct((8, 128), F32))
    res = pl.pallas_call(
        body, name=name, out_shape=out_shape, in_specs=[HBM_SPEC] * (n + m) + [ANY_SPEC] * k,
        out_specs=(SEM_SPEC, SEM_SPEC, *[HBM_SPEC] * (n + m), pl.BlockSpec(memory_space=pltpu.VMEM)),
        input_output_aliases={i: 2 + i for i in range(n + m)},
        compiler_params=pltpu.CompilerParams(has_side_effects=EFFECT),
    )(*[_hbm(a) for a in (*srcs, *lands)], *after)
    return res[0], res[1], list(res[2:2 + n]), list(res[2 + n:2 + n + m]), res[-1]


def _split_wait(name, send_sems, recv_sems, srcs, lands, after, make_copies):
    n, m = len(srcs), len(lands)

    def body(*refs):
        for cp in make_copies(refs[:n], refs[n:n + m], refs[n + m], refs[n + m + 1]):
            cp.wait_send()
            cp.wait_recv()

    res = pl.pallas_call(
        body, name=name, out_shape=tuple(pltpu.HBM(a.shape, a.dtype) for a in (*srcs, *lands)),
        in_specs=[HBM_SPEC] * (n + m) + [SEM_SPEC, SEM_SPEC] + [ANY_SPEC] * len(after), out_specs=tuple([HBM_SPEC] * (n + m)),
        input_output_aliases={i: i for i in range(n + m)},
        compiler_params=pltpu.CompilerParams(has_side_effects=EFFECT),
    )(*srcs, *lands, send_sems, recv_sems, *after)
    return list(res[:n]), list(res[n:])


def _place_own(shards, dev):
    n = len(shards)

    def kern(dev_ref, *refs):
        for x_ref, o_ref in zip(refs[:n], refs[n:]):
            o_ref[...] = x_ref[...].astype(o_ref.dtype)

    blocks = [(None, *a.shape[1:]) for a, _, _ in shards]
    nbytes = sum(_nbytes(b, a.dtype) + _nbytes(b, dt) for b, (a, _, dt) in zip(blocks, shards))
    return pl.pallas_call(
        kern, name="weights_place_own", out_shape=[pltpu.HBM((N_DEV, *a.shape[1:]), dt) for a, _, dt in shards],
        grid_spec=pltpu.PrefetchScalarGridSpec(
            num_scalar_prefetch=1, grid=(1,),
            in_specs=[pl.BlockSpec(b, functools.partial(lambda i, dev, l: (l, 0, 0), l=l)) for b, (_, l, _) in zip(blocks, shards)],
            out_specs=[pl.BlockSpec(b, lambda i, dev: (dev[0], 0, 0)) for b in blocks]),
        compiler_params=pltpu.CompilerParams(dimension_semantics=("arbitrary",), vmem_limit_bytes=_vmem(nbytes)),
    )(dev, *[_hbm(a) for a, _, _ in shards])


def _ag_first_copies(src_refs, out_refs, send_sems, recv_sems):
    x, y, c = _me()
    targets = [(x, y, 1 - c), (1 - x, y, c), (x, 1 - y, c), (1 - x, 1 - y, c)]
    return [pltpu.make_async_remote_copy(
        src_ref=out_refs[op].at[4 * x + 2 * y + c], dst_ref=out_refs[op].at[4 * x + 2 * y + c], send_sem=send_sems.at[4 * op + k],
        recv_sem=recv_sems.at[4 * op + k], device_id=to, device_id_type=MESH)
        for op in range(len(out_refs)) for k, to in enumerate(targets)]


def _ag_second(gathered):
    n = len(gathered)

    def kern(*refs):
        in_refs, out_refs, (send_sems, recv_sems) = refs[:n], refs[n:2 * n], refs[2 * n:]
        x, y, c = _me()
        chips = [(1 - x, y), (x, 1 - y), (1 - x, 1 - y)]
        passed = [pltpu.make_async_remote_copy(
            src_ref=in_refs[op].at[4 * cx + 2 * cy + c], dst_ref=out_refs[op].at[4 * cx + 2 * cy + c],
            send_sem=send_sems.at[3 * op + j], recv_sem=recv_sems.at[3 * op + j], device_id=(x, y, 1 - c), device_id_type=MESH)
            for op in range(n) for j, (cx, cy) in enumerate(chips)]
        for cp in passed:
            cp.start()
        for cp in passed:
            cp.wait_send()
        for op in range(n):
            for j, (cx, cy) in enumerate(chips):
                slot = out_refs[op].at[4 * cx + 2 * cy + 1 - c]
                pltpu.make_async_remote_copy(src_ref=slot, dst_ref=slot, send_sem=send_sems.at[3 * op + j],
                                             recv_sem=recv_sems.at[3 * op + j], device_id=(x, y, c), device_id_type=MESH).wait_recv()

    return pl.pallas_call(
        kern, name="weights_all_gather_second", out_shape=[pltpu.HBM(g.shape, g.dtype) for g in gathered],
        in_specs=[ANY_SPEC] * n, out_specs=[ANY_SPEC] * n, input_output_aliases={i: i for i in range(n)},
        scratch_shapes=[pltpu.SemaphoreType.DMA((3 * n,)), pltpu.SemaphoreType.DMA((3 * n,))],
    )(*[_hbm(a) for a in gathered])


def _rs_sibling_copies(g_refs, out_refs, send_sems, recv_sems):
    x, y, c = _me()
    return [pltpu.make_async_remote_copy(
        src_ref=g_refs[op].at[k, 1 - c], dst_ref=out_refs[op].at[k], send_sem=send_sems.at[4 * op + k],
        recv_sem=recv_sems.at[4 * op + k], device_id=(x, y, 1 - c), device_id_type=MESH)
        for op in range(len(g_refs)) for k in range(4)]


def _rs_chip_copies(p_refs, out_refs, send_sems, recv_sems):
    x, y, c = _me()
    chips = [(1 - x, y), (x, 1 - y), (1 - x, 1 - y)]
    return [pltpu.make_async_remote_copy(
        src_ref=p_refs[op].at[2 * cx + cy], dst_ref=out_refs[op].at[j], send_sem=send_sems.at[3 * op + j],
        recv_sem=recv_sems.at[3 * op + j], device_id=(cx, cy, c), device_id_type=MESH)
        for op in range(len(p_refs)) for j, (cx, cy) in enumerate(chips)]


def _all_gather(placed):
    n = len(placed)

    def kern(*refs):
        in_refs, out_refs, (send_sems, recv_sems) = refs[:n], refs[n:2 * n], refs[2 * n:]
        x, y, c = _me()
        me, sibling = (x, y, c), (x, y, 1 - c)
        chips = [(1 - x, y), (x, 1 - y), (1 - x, 1 - y)]

        def copy(op, k, block, to, own=False):
            idx = 4 * block[0] + 2 * block[1] + block[2]
            return pltpu.make_async_remote_copy(
                src_ref=(in_refs if own else out_refs)[op].at[idx], dst_ref=out_refs[op].at[idx], send_sem=send_sems.at[7 * op + k],
                recv_sem=recv_sems.at[7 * op + k], device_id=to, device_id_type=MESH)

        first = []
        for op in range(n):
            first.append(copy(op, 0, me, sibling, own=True))
            first += [copy(op, 1 + j, me, (*chip, c), own=True) for j, chip in enumerate(chips)]
        for cp in first:
            cp.start()
        passed = []
        for j, chip in enumerate(chips):
            for op in range(n):
                copy(op, 1 + j, (*chip, c), me).wait_recv()
                passed.append(copy(op, 4 + j, (*chip, c), sibling))
                passed[-1].start()
        for op in range(n):
            copy(op, 0, sibling, me).wait_recv()
            for j, chip in enumerate(chips):
                copy(op, 4 + j, (*chip, 1 - c), me).wait_recv()
        for cp in first + passed:
            cp.wait_send()

    return pl.pallas_call(
        kern, name="weights_all_gather", out_shape=[pltpu.HBM(g.shape, g.dtype) for g in placed],
        in_specs=[ANY_SPEC] * n, out_specs=[ANY_SPEC] * n, input_output_aliases={i: i for i in range(n)},
        scratch_shapes=[pltpu.SemaphoreType.DMA((7 * n,)), pltpu.SemaphoreType.DMA((7 * n,))],
    )(*[_hbm(a) for a in placed])


def _rs_sibling(grads):
    n = len(grads)

    def kern(*refs):
        g_refs, out_refs, (send_sems, recv_sems) = refs[:n], refs[n:2 * n], refs[2 * n:]
        x, y, c = _me()
        copies = [pltpu.make_async_remote_copy(
            src_ref=g_refs[op].at[k, 1 - c], dst_ref=out_refs[op].at[k], send_sem=send_sems.at[4 * op + k],
            recv_sem=recv_sems.at[4 * op + k], device_id=(x, y, 1 - c), device_id_type=MESH) for op in range(n) for k in range(4)]
        for cp in copies:
            cp.start()
        for cp in copies:
            cp.wait()

    out_shape = [pltpu.HBM((4, *g.shape[2:]), g.dtype) for g in grads]
    return _hbm_call("grads_to_sibling", kern, grads, out_shape, 4 * n)


def _rs_chips(sums):
    n = len(sums)

    def kern(*refs):
        p_refs, out_refs, (send_sems, recv_sems) = refs[:n], refs[n:2 * n], refs[2 * n:]
        x, y, c = _me()
        chips = [(1 - x, y), (x, 1 - y), (1 - x, 1 - y)]
        copies = [pltpu.make_async_remote_copy(
            src_ref=p_refs[op].at[2 * cx + cy], dst_ref=out_refs[op].at[j], send_sem=send_sems.at[3 * op + j],
            recv_sem=recv_sems.at[3 * op + j], device_id=(cx, cy, c), device_id_type=MESH)
            for op in range(n) for j, (cx, cy) in enumerate(chips)]
        for cp in copies:
            cp.start()
        for cp in copies:
            cp.wait()

    out_shape = [pltpu.HBM((3, *p.shape[1:]), p.dtype) for p in sums]
    return _hbm_call("grads_between_chips", kern, sums, out_shape, 3 * n)


def _row_tile(r):
    return r if r <= 256 else 256


def _chip_sum(name, g, from_sibling, core):
    _, _, R, W = g.shape
    tr = _row_tile(R)

    def kern(core_ref, g_ref, s_ref, o_ref):
        o_ref[...] = (g_ref[...] + s_ref[...]).astype(BF16)

    return pl.pallas_call(
        kern, name=name, out_shape=pltpu.HBM((4, R, W), BF16),
        grid_spec=pltpu.PrefetchScalarGridSpec(
            num_scalar_prefetch=1, grid=(4, R // tr),
            in_specs=[pl.BlockSpec((None, None, tr, W), lambda k, i, core: (k, core[0], i, 0)),
                      pl.BlockSpec((None, tr, W), lambda k, i, core: (k, i, 0))],
            out_specs=pl.BlockSpec((None, tr, W), lambda k, i, core: (k, i, 0))),
        compiler_params=pltpu.CompilerParams(dimension_semantics=("parallel", "parallel"), vmem_limit_bytes=_vmem(3 * tr * W * 4)),
    )(core, _hbm(g), _hbm(from_sibling))


def _adamw(w, g, m, v):
    m = ADAM_B1 * m + (1.0 - ADAM_B1) * g
    v = ADAM_B2 * v + (1.0 - ADAM_B2) * (g * g)
    m_hat = m / (1.0 - ADAM_B1 ** ADAM_STEP)
    v_hat = v / (1.0 - ADAM_B2 ** ADAM_STEP)
    return -ADAM_LR * (m_hat / (jnp.sqrt(v_hat) + ADAM_EPS) + ADAM_WD * w), m, v


def _finish_sharded(name, layers, w, m, v, where):
    nl, R, W = w.shape
    tr = _row_tile(R)

    def kern(where_ref, *refs):
        w_ref, m_ref, v_ref, go_ref, d_ref, mo_ref, vo_ref = refs[3 * nl:]
        for l in range(nl):
            g_ref, s_ref, c_ref = refs[3 * l:3 * l + 3]
            grad = g_ref[...] + s_ref[...]
            for j in range(3):
                grad = grad + c_ref[j].astype(F32)
            go_ref[l] = grad
            d_ref[l], mo_ref[l], vo_ref[l] = _adamw(w_ref[l], grad, m_ref[l], v_ref[l])

    row = pl.BlockSpec((nl, tr, W), lambda i, wh: (0, i, 0))
    in_specs, args = [], []
    for g, s, c in layers:
        in_specs += [pl.BlockSpec((None, None, tr, W), lambda i, wh: (wh[0], wh[1], i, 0)),
                     pl.BlockSpec((None, tr, W), lambda i, wh: (wh[0], i, 0)),
                     pl.BlockSpec((3, tr, W), lambda i, wh: (0, i, 0))]
        args += [g, s, c]
    return pl.pallas_call(
        kern, name=name, out_shape=[pltpu.HBM((nl, R, W), F32)] * 4,
        grid_spec=pltpu.PrefetchScalarGridSpec(num_scalar_prefetch=1, grid=(R // tr,), in_specs=in_specs + [row, row, row],
                                               out_specs=[row, row, row, row]),
        compiler_params=pltpu.CompilerParams(dimension_semantics=("parallel",), vmem_limit_bytes=_vmem(nl * 11 * tr * W * 4)),
    )(where, *[_hbm(a) for a in (*args, w, m, v)])


SMALL_PLACE = (("mla_gq", 0, 0, 1, 256), ("mla_gkv", 0, 256, 1, 256), ("sgu_ln_g", 0, 512, 1, 512), ("sgu_ln_b", 1, 0, 1, 512),
               ("hg_lb", 2, 0, 2, 1024), ("ln1_g", 4, 0, 2, 1024), ("ln1_b", 6, 0, 2, 1024), ("sgu_b", 8, 0, 4, 128),
               ("ln2_g", 12, 0, 2, 1024), ("ln2_b", 14, 0, 2, 1024), ("hg_gnorm", 16, 0, 1, 1024))
SMALL_BUF_ROWS = 24


def _small_reduce_adamw(gs, given):
    pieces = [(gs["mla_gq"], 0, 0), (gs["mla_gkv"], 0, 256), (gs["sgu_ln_g"], 0, 512), (gs["sgu_ln_b"], 1, 0), (gs["hg_lb"], 2, 0),
              (gs["ln1_g0"], 4, 0), (gs["ln1_g1"], 5, 0), (gs["ln1_b0"], 6, 0), (gs["ln1_b1"], 7, 0), (gs["sgu_b"], 8, 0),
              (gs["ln2_g0"], 12, 0), (gs["ln2_g1"], 13, 0), (gs["ln2_b0"], 14, 0), (gs["ln2_b1"], 15, 0), (gs["hg_gnorm"], 16, 0)]
    names = [p[0] for p in SMALL_PLACE] + ["sgu_w"]
    n_p, n_names = len(pieces), len(names)
    wmv = [given[pre + name] for name in names for pre in ("", "m_", "v_")]

    def kern(*refs):
        piece_refs, gw_ref = refs[:n_p], refs[n_p]
        wmv_refs = refs[n_p + 1:n_p + 1 + 3 * n_names]
        out_refs = refs[n_p + 1 + 3 * n_names:n_p + 1 + 7 * n_names]
        buf_a, buf_b, send_sems, recv_sems = refs[n_p + 1 + 7 * n_names:]
        px, py, pc = _me()
        me = 4 * px + 2 * py + pc
        mine_a, mine_b = buf_a.at[me], buf_b.at[me]
        mine_a[...] = jnp.zeros(mine_a.shape, F32)
        for ref, (_, r, l0) in zip(piece_refs, pieces):
            mine_a[r:r + ref.shape[0], l0:l0 + ref.shape[1]] = ref[...]
        mine_b[...] = gw_ref[...]
        copies = []
        for r in range(1, N_DEV):
            peer = (px ^ (r >> 2), py ^ ((r >> 1) & 1), pc ^ (r & 1))
            for k, mine in enumerate((mine_a, mine_b)):
                copies.append(pltpu.make_async_remote_copy(
                    src_ref=mine, dst_ref=mine, send_sem=send_sems.at[2 * (r - 1) + k], recv_sem=recv_sems.at[2 * (r - 1) + k],
                    device_id=peer, device_id_type=MESH))
        for cp in copies:
            cp.start()
        for r in range(1, N_DEV):
            for k, buf in enumerate((buf_a, buf_b)):
                theirs = buf.at[me ^ r]
                pltpu.make_async_remote_copy(
                    src_ref=theirs, dst_ref=theirs, send_sem=send_sems.at[2 * (r - 1) + k], recv_sem=recv_sems.at[2 * (r - 1) + k],
                    device_id=(px, py, pc), device_id_type=MESH).wait_recv()
        for cp in copies:
            cp.wait_send()
        sum_a, sum_b = buf_a[0], buf_b[0]
        for d in range(1, N_DEV):
            sum_a, sum_b = sum_a + buf_a[d], sum_b + buf_b[d]

        def own_block(full):
            acc = full[:, 0:128]
            for b in range(1, N_DEV):
                acc = jnp.where(me == b, full[:, b * 128:(b + 1) * 128], acc)
            return acc

        for idx, name in enumerate(names):
            w_ref, m_ref, v_ref = wmv_refs[3 * idx:3 * idx + 3]
            if name == "sgu_w":
                grad = sum_b[None]
            else:
                _, r, l0, nr, nl = SMALL_PLACE[idx]
                grad = sum_a[r:r + nr, l0:l0 + nl]
                if name == "hg_gnorm":
                    grad = own_block(grad)
                if name == "sgu_b":
                    grad = grad[None]
            res = (grad, *_adamw(w_ref[...], grad, m_ref[...], v_ref[...]))
            for o_ref, val in zip(out_refs[4 * idx:4 * idx + 4], res):
                o_ref[...] = val

    vmem = pl.BlockSpec(memory_space=pltpu.VMEM)
    operands = [p[0] for p in pieces] + [gs["sgu_w"]] + wmv
    out_shape = [jax.ShapeDtypeStruct(given[name].shape, F32) for name in names for _ in range(4)]
    res = pl.pallas_call(
        kern, name="small_all_reduce_adamw", out_shape=out_shape, in_specs=[vmem] * len(operands), out_specs=[vmem] * len(out_shape),
        scratch_shapes=[pltpu.VMEM((N_DEV, SMALL_BUF_ROWS, D_MODEL), F32), pltpu.VMEM((N_DEV, SGU_G, 128, 128), F32),
                        pltpu.SemaphoreType.DMA((14,)), pltpu.SemaphoreType.DMA((14,))],
    )(*operands)
    return {name: res[4 * idx:4 * idx + 4] for idx, name in enumerate(names)}


class _Exchange:
    def __init__(self, given):
        self.given = given
        px, py, pc = _me()
        self.core = pc.reshape(1).astype(jnp.int32)
        self.dev = (4 * px + 2 * py + pc).reshape(1).astype(jnp.int32)
        self.where = jnp.stack([2 * px + py, pc]).astype(jnp.int32)
        self.state, self.layers = {}, {}

    def start_weights(self, lands, after):
        self.weights = _split_start("weights_first_start", [], lands, 4 * len(lands), _ag_first_copies, after=after)
        self.first_token = self.weights[4]

    def weights_ready(self, after):
        send_sems, recv_sems, shards, lands, _ = self.weights
        _, lands = _split_wait("weights_first_wait", send_sems, recv_sems, shards, lands, [after], _ag_first_copies)
        got = _ag_second(lands)
        return dict(w_in_o=got[0], w_out_o=got[1], w_ff1=[got[2], got[3]], w_ff2=[got[4], got[5]])

    def grads_start(self, tag, grads):
        blocks = [g.reshape(4, 2, *g.shape[1:]) for g in grads]
        lands = [lax.empty((4, *b.shape[2:]), F32) for b in blocks]
        self.state[tag] = _split_start(f"grads_{tag}_sibling_start", blocks, lands, 4 * len(blocks), _rs_sibling_copies)
        return self.state[tag][4]

    def grads_middle(self, tag, after):
        send_sems, recv_sems, blocks, lands, _ = self.state[tag]
        blocks, from_sibling = _split_wait(f"grads_{tag}_sibling_wait", send_sems, recv_sems, blocks, lands, [after], _rs_sibling_copies)
        sums = [_chip_sum(f"grads_{tag}_chip_sum_{k}", b, s, self.core) for k, (b, s) in enumerate(zip(blocks, from_sibling))]
        lands = [lax.empty((3, *p.shape[1:]), BF16) for p in sums]
        self.state[tag] = (blocks, from_sibling, _split_start(f"grads_{tag}_chips_start", sums, lands, 3 * len(sums), _rs_chip_copies))
        return self.state[tag][2][4]

    def grads_end(self, tag, after):
        blocks, from_sibling, (send_sems, recv_sems, sums, lands, _) = self.state[tag]
        _, from_chips = _split_wait(f"grads_{tag}_chips_wait", send_sems, recv_sems, sums, lands, [after], _rs_chip_copies)
        self.layers[tag] = list(zip(blocks, from_sibling, from_chips))


SHARDED = ("w_in_e", "w_qb", "w_kvb", "w_out_e", "w_in_o", "w_out_o", "w_ff1", "w_ff2")


def kernel(x, positions, w_in_e, mla_gq, mla_gkv, w_qb, w_kvb, sgu_ln_g, sgu_ln_b, sgu_w, sgu_b, w_out_e, w_in_o, hg_lb, hg_gnorm, w_out_o, ln1_g, ln1_b, w_ff1, w_ff2, ln2_g, ln2_b, loss_target, m_w_in_e, m_mla_gq, m_mla_gkv, m_w_qb, m_w_kvb, m_sgu_ln_g, m_sgu_ln_b, m_sgu_w, m_sgu_b, m_w_out_e, m_w_in_o, m_hg_lb, m_hg_gnorm, m_w_out_o, m_ln1_g, m_ln1_b, m_w_ff1, m_w_ff2, m_ln2_g, m_ln2_b, v_w_in_e, v_mla_gq, v_mla_gkv, v_w_qb, v_w_kvb, v_sgu_ln_g, v_sgu_ln_b, v_sgu_w, v_sgu_b, v_w_out_e, v_w_in_o, v_hg_lb, v_hg_gnorm, v_w_out_o, v_ln1_g, v_ln1_b, v_w_ff1, v_w_ff2, v_ln2_g, v_ln2_b):
    given = dict(locals())
    ex = _Exchange(given)

    names = ["w_in_e", "w_qb", "w_kvb", "w_out_e"]
    placed = _place_own([(given[n], 0, BF16) for n in names] + [(hg_gnorm.reshape(1, 1, D_MODEL // N_DEV), 0, F32)]
                        + [(w_in_o, 0, BF16), (w_out_o, 0, BF16), (w_ff1, 0, BF16), (w_ff1, 1, BF16), (w_ff2, 0, BF16), (w_ff2, 1, BF16)],
                        ex.dev)
    got = _all_gather(placed[:5])
    ex.start_weights(placed[5:], after=[got[0]])
    gw = dict(zip(names, got[:4]))
    small_names = ["mla_gq", "mla_gkv", "sgu_ln_g", "sgu_ln_b", "sgu_w", "sgu_b", "hg_lb", "ln1_g", "ln1_b", "ln2_g", "ln2_b"]
    sp = {n: given[n] for n in small_names}
    sp["hg_gnorm"] = got[4].reshape(1, D_MODEL)

    sq_err, dx, grads, gs = _local_step(x[0], positions[0], loss_target[0], gw, sp, ex)
    loss = lax.psum(0.5 * jnp.sum(sq_err) / D_MODEL, ("x", "y", "c"))

    blocks = [grads[n].reshape(4, 2, *grads[n].shape[1:]) for n in names]
    from_sibling = _rs_sibling(blocks)
    chip_sums = [_chip_sum(f"grads_l0_chip_sum_{k}", b, s, ex.core) for k, (b, s) in enumerate(zip(blocks, from_sibling))]
    from_chips = _rs_chips(chip_sums)
    per_weight = dict(zip(names, [[l] for l in zip(blocks, from_sibling, from_chips)]))
    l1, l0m = ex.layers["l1"], ex.layers["l0m"]
    per_weight.update(w_ff1=[l0m[0], l1[0]], w_ff2=[l0m[1], l1[1]], w_in_o=[l1[2]], w_out_o=[l1[3]])
    results = {n: _finish_sharded(f"finish_{n}", per_weight[n], given[n], given["m_" + n], given["v_" + n], ex.where) for n in SHARDED}

    results.update(_small_reduce_adamw(gs, given))

    order = ["w_in_e", "mla_gq", "mla_gkv", "w_qb", "w_kvb", "sgu_ln_g", "sgu_ln_b", "sgu_w", "sgu_b", "w_out_e", "w_in_o",
             "hg_lb", "hg_gnorm", "w_out_o", "ln1_g", "ln1_b", "w_ff1", "w_ff2", "ln2_g", "ln2_b"]
    return (loss, dx[None], *[results[name][kind] for kind in range(4) for name in order])
```

```python
import functools
import math

import jax
import jax.numpy as jnp
import numpy as np
from jax import lax
from jax.experimental import pallas as pl
from jax.experimental.pallas import tpu as pltpu

F32 = jnp.float32
BF16 = jnp.bfloat16
MESH = pl.DeviceIdType.MESH
HIGHEST = lax.Precision.HIGHEST

D_MODEL = 1024
D_FF = 4096
N_DEV = 8
HEADS = 8
HEAD_W = 128
MLA_NOPE = 64
MLA_ROPE = 32
MLA_V = 64
MLA_LORA = 256
MLA_SCALE = (MLA_NOPE + MLA_ROPE) ** -0.5
ROPE_BASE = 10000.0
SGU_DIM = 512
SGU_G = 4
SGU_CHUNK = 128
HG_CHUNK = 64
ALPHA = (2 * 2) ** 0.25
EPS = 1e-5
ADAM_LR, ADAM_B1, ADAM_B2, ADAM_EPS, ADAM_WD, ADAM_STEP = 0.001, 0.9, 0.999, 1e-08, 0.01, 10

VMEM_CAP_V7X = 56 * 2**20
VMEM_SLACK = 12 * 2**20
TM = 512
TN = 512


def _vmem(block_bytes):
    return int(min(VMEM_CAP_V7X, 2 * block_bytes + VMEM_SLACK))


def _hbm(a):
    return pltpu.with_memory_space_constraint(a, pltpu.HBM)


def _nbytes(shape, dtype):
    return int(np.prod([d for d in shape if d is not None])) * jnp.dtype(dtype).itemsize


def _sig(x):
    return 1.0 / (1.0 + jnp.exp(-x))


def _gelu(x):
    c = math.sqrt(2.0 / math.pi)
    t = jnp.tanh(c * (x + 0.044715 * x * x * x))
    return 0.5 * x * (1.0 + t), t


def _gelu_grad(x, t):
    c = math.sqrt(2.0 / math.pi)
    return 0.5 * (1.0 + t) + 0.5 * x * (1.0 - t * t) * c * (1.0 + 3 * 0.044715 * x * x)


def _dot(a, b, dims, precision=None):
    return lax.dot_general(a, b, (dims, ((), ())), preferred_element_type=F32, precision=precision)


NN = ((1,), (0,))
NT = ((1,), (1,))
TN_ = ((0,), (0,))


def _deps(deps):
    return [d for d in deps if d is not None]


def _tiled(name, grid, ins, outs, compute, direct=False, deps=()):
    n_in, deps = len(ins), _deps(deps)
    n_skip = n_in + len(deps)

    def kern(*refs):
        if direct:
            compute(refs[:n_in], refs[n_skip:])
            return
        for o_ref, r in zip(refs[n_skip:], compute(*refs[:n_in])):
            o_ref[...] = r.astype(o_ref.dtype).reshape(o_ref.shape)

    swap = lambda f: (lambda j, i: f(i, j))
    nbytes = sum(_nbytes(blk, a.dtype) for a, blk, _ in ins) + sum(_nbytes(blk, dt) + _nbytes(blk, F32) for _, dt, blk, _ in outs)
    res = pl.pallas_call(
        kern, name=name, grid=grid,
        in_specs=[pl.BlockSpec(blk, swap(f)) for _, blk, f in ins] + [ANY_SPEC] * len(deps),
        out_specs=[pl.BlockSpec(blk, swap(f)) for _, _, blk, f in outs],
        out_shape=[pltpu.HBM(shape, dt) for shape, dt, _, _ in outs],
        compiler_params=pltpu.CompilerParams(dimension_semantics=("parallel", "parallel"), vmem_limit_bytes=_vmem(nbytes)),
    )(*[_hbm(a) for a, _, _ in ins], *deps)
    return res if len(res) > 1 else res[0]


def _rb(a, tm, w=None, cb=0):
    return (a, (tm, a.shape[1] if w is None else w), lambda i, j: (i, cb))


def _rbj(a, tm, tn):
    return (a, (tm, tn), lambda i, j: (i, j))


def _cw(b, tn):
    return (b, (b.shape[0], tn), lambda i, j: (0, j))


def _rw(b, tn):
    return (b, (tn, b.shape[1]), lambda i, j: (j, 0))


def _tl(a, tm):
    return (a, (a.shape[0], tm), lambda i, j: (0, i))


def _gcw(g):
    return (g, (None, g.shape[1], g.shape[2]), lambda i, j: (j, 0, 0))


def _grw(g, tn):
    return (g, (N_DEV, tn, g.shape[2]), lambda i, j: (0, j, 0))


def _out(m, n, dtype, tm, tn):
    return ((m, n), dtype, (tm, tn), lambda i, j: (i, j))


def _out_dev(k, n, tm):
    return ((N_DEV, k, n), F32, (None, tm, n), lambda i, j: (j, i, 0))


def _mmc(dims, n_pairs=1, epilogue=None):
    def compute(*refs):
        acc = None
        for k in range(n_pairs):
            d = _dot(refs[2 * k][...].astype(BF16), refs[2 * k + 1][...].astype(BF16), dims)
            acc = d if acc is None else acc + d
        ext = [r[...] for r in refs[2 * n_pairs:]]
        return epilogue(acc, *ext) if epilogue is not None else (acc,)

    return compute


def _res(w):
    return (w, w.shape, functools.partial(lambda i, j, nd: (0,) * nd, nd=w.ndim))


def _mmc_blocks(nblk, dims, rhs_block, epilogue=None):
    def compute(in_refs, out_refs):
        a = in_refs[0][...].astype(BF16)
        for d in range(nblk):
            acc = _dot(a, rhs_block(in_refs[1], d).astype(BF16), dims)
            n = acc.shape[1]
            ext = [r[:, d * n:(d + 1) * n] for r in in_refs[2:]]
            res = epilogue(acc, *ext) if epilogue is not None else (acc,)
            for o_ref, r in zip(out_refs, res):
                o_ref[:, d * n:(d + 1) * n] = r.astype(o_ref.dtype)

    return compute


def _mmc_dev(epilogue=None):
    def compute(a_ref, b_ref, *ext_refs):
        n = b_ref.shape[2]
        acc = None
        for d in range(N_DEV):
            t = _dot(a_ref[:, d * n:(d + 1) * n].astype(BF16), b_ref[d].astype(BF16), NT)
            acc = t if acc is None else acc + t
        ext = [r[...] for r in ext_refs]
        return epilogue(acc, *ext) if epilogue is not None else (acc,)

    return compute


def _rowwise(name, body, rows, consts, out_rows, out_accs=(), tr=512, deps=()):
    T = rows[0][0].shape[0]
    tr = min(tr, T)
    deps = _deps(deps)
    nr, ncn, no, nd = len(rows), len(consts), len(out_rows), len(deps)

    def kern(*refs):
        accs = refs[nr + ncn + nd + no:]
        if accs:
            @pl.when(pl.program_id(0) == 0)
            def _():
                for a in accs:
                    a[...] = jnp.zeros(a.shape, a.dtype)
        body(refs[:nr], refs[nr:nr + ncn], refs[nr + ncn + nd:nr + ncn + nd + no], accs)

    in_specs = [pl.BlockSpec((tr, w), functools.partial(lambda i, cb: (i, cb), cb=cb)) for _, w, cb in rows]
    in_specs += [pl.BlockSpec(c.shape, functools.partial(lambda i, nd: (0,) * nd, nd=c.ndim)) for c in consts]
    in_specs += [ANY_SPEC] * nd
    out_specs = [pl.BlockSpec((tr, w), lambda i: (i, 0)) for w, _ in out_rows]
    out_specs += [pl.BlockSpec(s, functools.partial(lambda i, nd: (0,) * nd, nd=len(s))) for s, _ in out_accs]
    out_shape = [pltpu.HBM((T, w), dt) for w, dt in out_rows]
    out_shape += [pltpu.HBM(s, dt) for s, dt in out_accs]
    nbytes = sum(_nbytes((tr, w), a.dtype) for a, w, _ in rows) + sum(_nbytes(c.shape, c.dtype) for c in consts)
    nbytes += sum(_nbytes((tr, w), dt) for w, dt in out_rows) + sum(_nbytes(s, dt) for s, dt in out_accs)
    res = pl.pallas_call(
        kern, name=name, grid=(T // tr,), in_specs=in_specs, out_specs=out_specs, out_shape=out_shape,
        compiler_params=pltpu.CompilerParams(dimension_semantics=("arbitrary",), vmem_limit_bytes=_vmem(nbytes)),
    )(*[_hbm(a) for a, _, _ in rows], *[_hbm(c) for c in consts], *deps)
    return res if len(res) > 1 else res[0]


def _full(a):
    return (a, a.shape[1], 0)


def _ln_stats(y):
    mu = jnp.mean(y, axis=-1, keepdims=True)
    yc = y - mu
    r = lax.rsqrt(jnp.mean(yc * yc, axis=-1, keepdims=True) + EPS)
    return yc * r, r


def _ln_back(dh, xh, r, gain, dg_ref, db_ref):
    dg_ref[...] += jnp.sum(dh * xh, axis=0, keepdims=True)
    db_ref[...] += jnp.sum(dh, axis=0, keepdims=True)
    dx = dh * gain
    return r * (dx - jnp.mean(dx, axis=-1, keepdims=True) - xh * jnp.mean(dx * xh, axis=-1, keepdims=True))


def _proj_ln(name, acts, weights, h_in, g, b, layer):
    n = len(acts)

    def body(rows, consts, outs, accs):
        acc = None
        for k in range(n):
            d = _dot(rows[k][...].astype(BF16), consts[k][...], NN)
            acc = d if acc is None else acc + d
        y = ALPHA * rows[n][...] + acc
        xh, _ = _ln_stats(y)
        h = xh * consts[n][layer:layer + 1, :] + consts[n + 1][layer:layer + 1, :]
        outs[0][...] = y
        outs[1][...] = h
        outs[2][...] = h.astype(BF16)

    return _rowwise(name, body, [_full(a) for a in acts] + [_full(h_in)], [*weights, g, b],
                    [(D_MODEL, F32), (D_MODEL, F32), (D_MODEL, BF16)], tr=256)


def _proj_ln_loss(name, act, w2, h_in, g, b, layer, target):
    def body(rows, consts, outs, accs):
        y = ALPHA * rows[1][...] + _dot(rows[0][...], consts[0][...], NN)
        xh, r = _ln_stats(y)
        gain = consts[1][layer:layer + 1, :]
        err = xh * gain + consts[2][layer:layer + 1, :] - rows[2][...]
        accs[0][...] += jnp.sum(err * err, axis=0, keepdims=True)
        dy = _ln_back(err * (1.0 / D_MODEL), xh, r, gain, accs[1], accs[2])
        outs[0][...] = dy
        outs[1][...] = dy.astype(BF16)

    return _rowwise(name, body, [_full(act), _full(h_in), _full(target)], [w2, g, b], [(D_MODEL, F32), (D_MODEL, BF16)],
                    [((1, D_MODEL), F32)] * 3, tr=256)


def _dh_ln_back(name, da, w, dy_next, y, g, layer, proj=None, deps=()):
    def body(rows, consts, outs, accs):
        n = consts[0].shape[2]
        acc = ALPHA * rows[1][...]
        for d in range(N_DEV):
            acc = acc + _dot(rows[0][:, d * n:(d + 1) * n], consts[0][d], NT)
        xh, r = _ln_stats(rows[2][...])
        dy = _ln_back(acc, xh, r, consts[1][layer:layer + 1, :], accs[0], accs[1])
        outs[0][...] = dy
        outs[1][...] = dy.astype(BF16)
        if proj is not None:
            outs[2][...] = _dot(dy.astype(BF16), consts[2][...], NT).astype(BF16)

    out_rows = [(D_MODEL, F32), (D_MODEL, BF16)] + ([(proj.shape[0], BF16)] if proj is not None else [])
    return _rowwise(name, body, [_full(da), _full(dy_next), _full(y)], [w, g] + ([proj] if proj is not None else []), out_rows,
                    [((1, D_MODEL), F32)] * 2, tr=256, deps=deps)


def _relu2_epilogue(acc):
    a = jnp.maximum(acc, 0.0)
    return acc, a * a


def _mlp_up(tag, h_bf, w1):
    T = h_bf.shape[0]
    tm = min(TM, T)
    return _tiled(f"{tag}_ff1", (1, T // tm), [_rb(h_bf, tm), _res(w1)],
                  [_out(T, D_FF, BF16, tm, D_FF), _out(T, D_FF, BF16, tm, D_FF)],
                  _mmc_blocks(N_DEV, NN, lambda w, d: w[d], epilogue=_relu2_epilogue), direct=True)


def _mlp_bwd_w(tag, h_bf, a, act, dff_bf, w2, deps=()):
    T = h_bf.shape[0]
    tm = min(TM, T)
    da = _tiled(f"{tag}_dact", (1, T // tm), [_rb(dff_bf, tm), _res(w2), _rb(a, tm)], [_out(T, D_FF, BF16, tm, D_FF)],
                _mmc_blocks(N_DEV, NT, lambda w, d: w[d], epilogue=lambda acc, a_t: (acc * 2.0 * jnp.maximum(a_t.astype(F32), 0.0),)),
                direct=True, deps=deps)
    dw2 = _tiled(f"{tag}_dw2", (1, D_FF // TM), [_tl(act, TM), _res(dff_bf)],
                 [_out(D_FF, D_MODEL, F32, TM, D_MODEL)], _mmc(TN_)).reshape(N_DEV, D_FF // N_DEV, D_MODEL)
    dw1 = _tiled(f"{tag}_dw1", (N_DEV, 1), [_res(h_bf), _cw(da, TN)], [_out_dev(D_MODEL, TN, D_MODEL)], _mmc(TN_))
    return da, dw1, dw2


def _rope_tables(positions_col, invf_lane):
    def body(rows, consts, outs, accs):
        ang = rows[0][...].astype(F32) * consts[0][...]
        c, s = jnp.cos(ang), jnp.sin(ang)
        lane = lax.broadcasted_iota(jnp.int32, ang.shape, 1)
        outs[0][...] = jnp.where(lane < 64, 1.0, jnp.where(lane < 96, c, 0.0))
        outs[1][...] = jnp.where((lane >= 64) & (lane < 80), -s, 0.0)
        outs[2][...] = jnp.where((lane >= 80) & (lane < 96), s, 0.0)

    return _rowwise("rope_tables", body, [_full(positions_col)], [invf_lane], [(HEAD_W, F32)] * 3)


def _rope(x, c, s1, s2):
    return x * c + pltpu.roll(x, 112, 1) * s1 + pltpu.roll(x, 16, 1) * s2


def _rope_t(dx, c, s1, s2):
    return dx * c + pltpu.roll(dx * s1, 16, 1) + pltpu.roll(dx * s2, 112, 1)


def _rms(c):
    r = lax.rsqrt(jnp.mean(c * c, axis=-1, keepdims=True) + EPS)
    return c * r, r


def _rope_heads(x, c, s1, s2, fn):
    return jnp.concatenate([fn(x[:, h * HEAD_W:(h + 1) * HEAD_W], c, s1, s2) for h in range(HEADS)], axis=1)


def _mla_in(x, wm, ws, tabs, gq, gkv, deps=()):
    def body(rows, consts, outs, accs):
        xb = rows[0][...].astype(BF16)
        zm = _dot(xb, consts[0][...], NN)
        outs[0][...] = zm
        outs[1][...] = _dot(xb, consts[1][...], NN)
        outs[2][...] = (_rms(zm[:, 0:256])[0] * consts[2][...]).astype(BF16)
        outs[3][...] = (_rms(zm[:, 256:512])[0] * consts[3][...]).astype(BF16)
        outs[4][...] = _rope(zm[:, 512:640], rows[1][...], rows[2][...], rows[3][...])

    return _rowwise("l0_in", body, [_full(x)] + [_full(t) for t in tabs], [wm, ws, gq, gkv],
                    [(640, F32), (1024, F32), (256, BF16), (256, BF16), (HEAD_W, F32)], deps=deps)


def _mla_qkv(cqn, ckvn, kr_rot, tabs, wq, wk, wv):
    def body(rows, consts, outs, accs):
        c, s1, s2 = rows[3][...], rows[4][...], rows[5][...]
        outs[0][...] = _rope_heads(_dot(rows[0][...], consts[0][...], NN), c, s1, s2, _rope).astype(BF16)
        outs[1][...] = (_dot(rows[1][...], consts[1][...], NN) + jnp.concatenate([rows[2][...]] * HEADS, axis=1)).astype(BF16)
        outs[2][...] = _dot(rows[1][...], consts[2][...], NN).astype(BF16)

    rows = [_full(cqn), _full(ckvn), _full(kr_rot)] + [_full(t) for t in tabs]
    return _rowwise("l0_qkv", body, rows, [wq, wk, wv], [(HEADS * HEAD_W, BF16)] * 3)


def _mla_back(zm, cqn, ckvn, tabs, gq, gkv, wq, wk, wv, dq, dk, dv):
    def body(rows, consts, outs, accs):
        c, s1, s2 = rows[4][...], rows[5][...], rows[6][...]
        dk_t, dv_bf = rows[8][...], rows[9][...].astype(BF16)
        dq_bf = _rope_heads(rows[7][...], c, s1, s2, _rope_t).astype(BF16)
        dk_bf = dk_t.astype(BF16)
        accs[0][...] += _dot(rows[2][...], dq_bf, TN_)
        accs[1][...] += _dot(rows[3][...], dk_bf, TN_)
        accs[2][...] += _dot(rows[3][...], dv_bf, TN_)
        dlat = [_dot(dq_bf, consts[2][...], NT), _dot(dk_bf, consts[3][...], NT) + _dot(dv_bf, consts[4][...], NT)]
        for k in range(2):
            ch, r = _rms(rows[k][...])
            accs[3 + k][...] += jnp.sum(dlat[k] * ch, axis=0, keepdims=True)
            dc = dlat[k] * consts[k][...]
            outs[0][:, 256 * k:256 * (k + 1)] = (r * (dc - ch * jnp.mean(dc * ch, axis=-1, keepdims=True))).astype(BF16)
        dks = dk_t[:, 0:HEAD_W]
        for h in range(1, HEADS):
            dks = dks + dk_t[:, h * HEAD_W:(h + 1) * HEAD_W]
        lane = lax.broadcasted_iota(jnp.int32, dks.shape, 1)
        dks = jnp.where((lane >= 64) & (lane < 96), dks, 0.0)
        outs[0][:, 512:640] = _rope_t(dks, c, s1, s2).astype(BF16)

    rows = [(zm, 256, 0), (zm, 256, 1), _full(cqn), _full(ckvn)] + [_full(t) for t in tabs] + [_full(dq), _full(dk), _full(dv)]
    wide = HEADS * HEAD_W
    return _rowwise("l0_mla_back", body, rows, [gq, gkv, wq, wk, wv], [(640, BF16)],
                    [((MLA_LORA, wide), F32)] * 3 + [((1, MLA_LORA), F32)] * 2, tr=256)


def _in_back(x, dzm, dzs, dy, wm, ws):
    def body(rows, consts, outs, accs):
        dzm_t, dzs_t = rows[1][...], rows[2][...]
        outs[0][...] = _dot(dzm_t, consts[0][...], NT) + _dot(dzs_t, consts[1][...], NT) + ALPHA * rows[3][...]
        xb = rows[0][...].astype(BF16)
        accs[0][...] += _dot(xb, dzm_t, TN_)
        accs[1][...] += _dot(xb, dzs_t, TN_)

    return _rowwise("l0_in_back", body, [_full(x), _full(dzm), _full(dzs), _full(dy)], [wm, ws], [(D_MODEL, F32)],
                    [((D_MODEL, 640), F32), ((D_MODEL, 1024), F32)])


def _out_weight_grads(o_att, b_out, dy_bf):
    def body(rows, consts, outs, accs):
        d = rows[2][...]
        accs[0][...] += _dot(rows[0][...].astype(BF16), d, TN_)
        accs[1][...] += _dot(rows[1][...], d, TN_)

    return _rowwise("l0_dw_out", body, [_full(o_att), _full(b_out), _full(dy_bf)], [], [],
                    [((HEADS * HEAD_W, D_MODEL), F32), ((SGU_DIM, D_MODEL), F32)])


def _attn_block(T):
    return min(1024, T)


def _attn_fwd(q, k, v):
    T = q.shape[0]
    BQ = _attn_block(T)
    nq = T // BQ

    def kern(q_ref, k_ref, v_ref, o_ref, lse_ref):
        def step(i, j, carry, masked):
            m, l, acc = carry
            qb = q_ref[pl.ds(pl.multiple_of(i * BQ, BQ), BQ), :]
            kb = k_ref[pl.ds(pl.multiple_of(j * BQ, BQ), BQ), :]
            vb = v_ref[pl.ds(pl.multiple_of(j * BQ, BQ), BQ), :]
            s = _dot(qb, kb, NT) * MLA_SCALE
            if masked:
                row = lax.broadcasted_iota(jnp.int32, s.shape, 0)
                col = lax.broadcasted_iota(jnp.int32, s.shape, 1)
                s = jnp.where(col <= row, s, -1e30)
            m_new = jnp.maximum(m, jnp.max(s, axis=-1, keepdims=True))
            p = jnp.exp(s - m_new)
            a = jnp.exp(m - m_new)
            l = a * l + jnp.sum(p, axis=-1, keepdims=True)
            acc = a * acc + _dot(p.astype(BF16), vb, NN)
            return m_new, l, acc

        def qloop(i, _):
            init = (jnp.full((BQ, 1), -1e30, F32), jnp.zeros((BQ, 1), F32), jnp.zeros((BQ, HEAD_W), F32))
            carry = lax.fori_loop(0, i, lambda j, c: step(i, j, c, False), init)
            m, l, acc = step(i, i, carry, True)
            rows = pl.ds(pl.multiple_of(i * BQ, BQ), BQ)
            o_ref[rows, :] = acc / l
            lse_ref[0, rows, :] = m + jnp.log(l)
            return 0

        lax.fori_loop(0, nq, qloop, 0)

    head = pl.BlockSpec((T, HEAD_W), lambda h: (0, h))
    nbytes = 3 * _nbytes((T, HEAD_W), BF16) + _nbytes((T, HEAD_W), F32) + _nbytes((T, 128), F32)
    return pl.pallas_call(
        kern, name="attn_fwd", grid=(HEADS,), in_specs=[head, head, head],
        out_specs=[head, pl.BlockSpec((1, T, 1), lambda h: (h, 0, 0))],
        out_shape=[pltpu.HBM((T, HEADS * HEAD_W), F32), pltpu.HBM((HEADS, T, 1), F32)],
        compiler_params=pltpu.CompilerParams(dimension_semantics=("parallel",), vmem_limit_bytes=_vmem(nbytes)),
    )(_hbm(q), _hbm(k), _hbm(v))


def _attn_bwd(q, k, v, o, lse, dcat, deps=()):
    T = q.shape[0]
    BQ = _attn_block(T)
    nq = T // BQ
    deps = _deps(deps)

    def kern(q_ref, k_ref, v_ref, o_ref, lse_ref, do_ref, *rest):
        dq_ref, dk_ref, dv_ref, dd_ref = rest[len(deps):]
        dq_ref[...] = jnp.zeros(dq_ref.shape, F32)

        def dloop(i, _):
            rows = pl.ds(pl.multiple_of(i * BQ, BQ), BQ)
            dd_ref[rows, :] = jnp.sum(do_ref[rows, :].astype(F32) * o_ref[rows, :], axis=-1, keepdims=True)
            return 0

        lax.fori_loop(0, nq, dloop, 0)

        def step(j, i, carry, masked):
            dk_acc, dv_acc = carry
            rq = pl.ds(pl.multiple_of(i * BQ, BQ), BQ)
            rk = pl.ds(pl.multiple_of(j * BQ, BQ), BQ)
            qb, kb, vb, dob = q_ref[rq, :], k_ref[rk, :], v_ref[rk, :], do_ref[rq, :]
            s = _dot(qb, kb, NT) * MLA_SCALE
            p = jnp.exp(s - lse_ref[0, rq, :])
            if masked:
                row = lax.broadcasted_iota(jnp.int32, s.shape, 0)
                col = lax.broadcasted_iota(jnp.int32, s.shape, 1)
                p = jnp.where(col <= row, p, 0.0)
            dp = _dot(dob, vb, NT)
            ds = (p * (dp - dd_ref[rq, :]) * MLA_SCALE).astype(BF16)
            dv_acc = dv_acc + _dot(p.astype(BF16), dob, TN_)
            dk_acc = dk_acc + _dot(ds, qb, TN_)
            dq_ref[rq, :] += _dot(ds, kb, NN)
            return dk_acc, dv_acc

        def kloop(j, _):
            init = (jnp.zeros((BQ, HEAD_W), F32), jnp.zeros((BQ, HEAD_W), F32))
            carry = step(j, j, init, True)
            dk_acc, dv_acc = lax.fori_loop(j + 1, nq, lambda i, c: step(j, i, c, False), carry)
            rk = pl.ds(pl.multiple_of(j * BQ, BQ), BQ)
            dk_ref[rk, :] = dk_acc
            dv_ref[rk, :] = dv_acc
            return 0

        lax.fori_loop(0, nq, kloop, 0)

    head = pl.BlockSpec((T, HEAD_W), lambda h: (0, h))
    nbytes = 4 * _nbytes((T, HEAD_W), BF16) + 5 * _nbytes((T, HEAD_W), F32) + 2 * _nbytes((T, 128), F32)
    return pl.pallas_call(
        kern, name="attn_bwd", grid=(HEADS,),
        in_specs=[head, head, head, head, pl.BlockSpec((1, T, 1), lambda h: (h, 0, 0)), head] + [ANY_SPEC] * len(deps),
        out_specs=[head, head, head],
        out_shape=[pltpu.HBM((T, HEADS * HEAD_W), F32)] * 3,
        scratch_shapes=[pltpu.VMEM((T, 1), F32)],
        compiler_params=pltpu.CompilerParams(dimension_semantics=("parallel",), vmem_limit_bytes=_vmem(nbytes)),
    )(*[_hbm(a) for a in (q, k, v, o, lse, dcat)], *deps)


def _sgu_common(u, v, ln_g, ln_b):
    ua, tu = _gelu(u)
    va, tv = _gelu(v)
    vh, r = _ln_stats(va)
    return ua, tu, tv, vh, r, vh * ln_g + ln_b


def _tril_mask(n):
    return lax.broadcasted_iota(jnp.int32, (n, n), 1) <= lax.broadcasted_iota(jnp.int32, (n, n), 0)


def _sgu_fwd(zs, ln_g, ln_b, w, bias_full):
    def body(rows, consts, outs, accs):
        ua, _, _, _, _, vn = _sgu_common(rows[0][...], rows[1][...], consts[0][...], consts[1][...])
        vn = vn.astype(BF16)
        tri = _tril_mask(SGU_CHUNK)
        for g in range(SGU_G):
            wg = jnp.where(tri, consts[2][0, g], 0.0).astype(BF16)
            cols = slice(g * 128, (g + 1) * 128)
            for c in range(ua.shape[0] // SGU_CHUNK):
                rws = slice(c * SGU_CHUNK, (c + 1) * SGU_CHUNK)
                mixed = _dot(wg, vn[rws, cols], NN) + consts[3][:, cols]
                outs[0][rws, cols] = (ua[rws, cols] * mixed).astype(BF16)

    return _rowwise("sgu_fwd", body, [(zs, 512, 0), (zs, 512, 1)], [ln_g, ln_b, w, bias_full], [(SGU_DIM, BF16)])


def _sgu_bwd(zs, dcat, ln_g, ln_b, w, bias_full):
    def body(rows, consts, outs, accs):
        u, v = rows[0][...], rows[1][...]
        ua, tu, tv, vh, r, vn = _sgu_common(u, v, consts[0][...], consts[1][...])
        dout = rows[2][...].astype(F32)
        vn_bf = vn.astype(BF16)
        tri = _tril_mask(SGU_CHUNK)
        dmixed = (dout * ua)
        dmixed_bf = dmixed.astype(BF16)
        ones = jnp.ones((8, SGU_CHUNK), F32)
        dvn_cols, mixed_cols = [], []
        for g in range(SGU_G):
            wg = jnp.where(tri, consts[2][0, g], 0.0).astype(BF16)
            cols = slice(g * 128, (g + 1) * 128)
            dvn_rows, mixed_rows = [], []
            dw = jnp.zeros((SGU_CHUNK, SGU_CHUNK), F32)
            dmix_sum = jnp.zeros((SGU_CHUNK, 128), F32)
            for c in range(u.shape[0] // SGU_CHUNK):
                rws = slice(c * SGU_CHUNK, (c + 1) * SGU_CHUNK)
                mixed_rows.append(_dot(wg, vn_bf[rws, cols], NN) + consts[3][:, cols])
                dvn_rows.append(_dot(wg, dmixed_bf[rws, cols], TN_))
                dw = dw + _dot(dmixed_bf[rws, cols], vn_bf[rws, cols], NT)
                dmix_sum = dmix_sum + dmixed[rws, cols]
            accs[0][g] += jnp.where(tri, dw, 0.0)
            accs[3][g:g + 1, :] += _dot(ones, dmix_sum, NT, precision=HIGHEST)[0:1, :]
            dvn_cols.append(jnp.concatenate(dvn_rows, axis=0))
            mixed_cols.append(jnp.concatenate(mixed_rows, axis=0))
        dvn = jnp.concatenate(dvn_cols, axis=1)
        mixed = jnp.concatenate(mixed_cols, axis=1)
        accs[1][...] += jnp.sum(dvn * vh, axis=0, keepdims=True)
        accs[2][...] += jnp.sum(dvn, axis=0, keepdims=True)
        dvh = dvn * consts[0][...]
        dva = r * (dvh - jnp.mean(dvh, axis=-1, keepdims=True) - vh * jnp.mean(dvh * vh, axis=-1, keepdims=True))
        outs[0][:, 0:512] = (dout * mixed * _gelu_grad(u, tu)).astype(BF16)
        outs[0][:, 512:1024] = (dva * _gelu_grad(v, tv)).astype(BF16)

    return _rowwise("sgu_bwd", body, [(zs, 512, 0), (zs, 512, 1), (dcat, 512, 2)], [ln_g, ln_b, w, bias_full], [(1024, BF16)],
                    [((SGU_G, 128, 128), F32), ((1, SGU_DIM), F32), ((1, SGU_DIM), F32), ((SGU_G, 128), F32)], tr=256)


def _lower_bound(hg_lb):
    a0, a1 = hg_lb[0:1, :], hg_lb[1:2, :]
    m = jnp.maximum(a0, a1)
    e0, e1 = jnp.exp(a0 - m), jnp.exp(a1 - m)
    s0, s1 = e0 / (e0 + e1), e1 / (e0 + e1)
    return (s0 + s1) - s0, s0, s1


def _prefix_rows(x, reverse=False):
    n = x.shape[0]
    row = lax.broadcasted_iota(jnp.int32, x.shape, 0)
    s = 1
    while s < n:
        if reverse:
            x = x + jnp.where(row < n - s, pltpu.roll(x, n - s, 0), 0.0)
        else:
            x = x + jnp.where(row >= s, pltpu.roll(x, s, 0), 0.0)
        s *= 2
    return x


def _hg_gates(qr, fr, lb):
    C = qr.shape[0]
    sq = _sig(qr)
    qf = qr * sq
    sf = _sig(fr)
    gate = lb + (1.0 - lb) * sf
    kk = 1.0 - gate
    tri = _tril_mask(C)
    b = _prefix_rows(jnp.log(gate))
    bref = b[C // 2 - 1:C // 2, :]
    bl = b[C - 1:C, :]
    e_b = jnp.exp(b)
    e_q = jnp.exp(b - bref)
    e_k = jnp.exp(bref - b)
    e_lb = jnp.exp(bl - b)
    return dict(sq=sq, qf=qf, sf=sf, gate=gate, kk=kk, tri=tri, bl=bl, e_b=e_b, e_q=e_q, e_k=e_k, e_lb=e_lb)


def _hgrn_fwd(z1, hg_lb, gnorm):
    T = z1.shape[0]
    C = min(HG_CHUNK, T)
    nc = T // C

    def kern(q_ref, f_ref, i_ref, g_ref, lb_ref, gn_ref, o_ref, hg_ref, st_ref, s_scr):
        @pl.when(pl.program_id(0) == 0)
        def _():
            s_scr[...] = jnp.zeros(s_scr.shape, F32)

        lb_all, _, _ = _lower_bound(lb_ref[...])
        st_ref[0] = s_scr[...]
        for h in range(HEADS):
            cols = slice(h * HEAD_W, (h + 1) * HEAD_W)
            t = _hg_gates(q_ref[:, cols], f_ref[:, cols], lb_all[:, cols])
            v = i_ref[:, cols]
            v_bf = v.astype(BF16)
            st = s_scr[h]
            a = jnp.where(t["tri"], _dot((t["qf"] * t["e_q"]).astype(BF16), (t["kk"] * t["e_k"]).astype(BF16), NT), 0.0)
            o = _dot(a.astype(BF16), v_bf, NN) + _dot((t["qf"] * t["e_b"]).astype(BF16), st.astype(BF16), NT)
            s_scr[h] = st * jnp.exp(t["bl"]) + _dot(v_bf, (t["kk"] * t["e_lb"]).astype(BF16), TN_)
            o_ref[:, cols] = o
            gr = g_ref[:, cols]
            r = lax.rsqrt(jnp.mean(o * o, axis=-1, keepdims=True) + EPS)
            hg_ref[:, cols] = (o * r * gn_ref[:, cols] * (gr * _sig(gr))).astype(BF16)

    seg = lambda k: pl.BlockSpec((C, D_MODEL), functools.partial(lambda n, k: (n, k), k=k))
    row = pl.BlockSpec((C, D_MODEL), lambda n: (n, 0))
    nbytes = 6 * _nbytes((C, D_MODEL), F32) + 3 * _nbytes((HEADS, 128, 128), F32)
    return pl.pallas_call(
        kern, name="hgrn_fwd", grid=(nc,),
        in_specs=[seg(0), seg(1), seg(2), seg(3), pl.BlockSpec((2, D_MODEL), lambda n: (0, 0)),
                  pl.BlockSpec((1, D_MODEL), lambda n: (0, 0))],
        out_specs=[row, row, pl.BlockSpec((1, HEADS, 128, 128), lambda n: (n, 0, 0, 0))],
        out_shape=[pltpu.HBM((T, D_MODEL), F32), pltpu.HBM((T, D_MODEL), BF16),
                   pltpu.HBM((nc, HEADS, 128, 128), F32)],
        scratch_shapes=[pltpu.VMEM((HEADS, 128, 128), F32)],
        compiler_params=pltpu.CompilerParams(dimension_semantics=("arbitrary",), vmem_limit_bytes=_vmem(nbytes)),
    )(*[_hbm(a) for a in (z1, z1, z1, z1, hg_lb, gnorm)])


def _hgrn_bwd(z1, o_pre, dhg, states, hg_lb, gnorm):
    T = z1.shape[0]
    C = min(HG_CHUNK, T)
    nc = T // C

    def kern(q_ref, f_ref, i_ref, g_ref, o_ref, dhg_ref, st_ref, lb_ref, gn_ref, dz_ref, dlb_ref, dgn_ref, ds_scr, dlb_scr):
        n = pl.program_id(0)

        @pl.when(n == 0)
        def _():
            ds_scr[...] = jnp.zeros(ds_scr.shape, F32)
            dlb_scr[...] = jnp.zeros(dlb_scr.shape, F32)
            dgn_ref[...] = jnp.zeros(dgn_ref.shape, F32)

        lb_all, s0, s1 = _lower_bound(lb_ref[...])
        for h in range(HEADS):
            cols = slice(h * HEAD_W, (h + 1) * HEAD_W)
            lb = lb_all[:, cols]
            qr, fr = q_ref[:, cols], f_ref[:, cols]
            t = _hg_gates(qr, fr, lb)
            tri = t["tri"]
            v_bf = i_ref[:, cols].astype(BF16)
            st_bf = st_ref[0, h].astype(BF16)
            dst = ds_scr[h]
            dst_bf = dst.astype(BF16)
            o = o_ref[:, cols]
            gr = g_ref[:, cols]
            sg = _sig(gr)
            sil = gr * sg
            gn = gn_ref[:, cols]
            r = lax.rsqrt(jnp.mean(o * o, axis=-1, keepdims=True) + EPS)
            on = o * r
            dh = dhg_ref[:, cols].astype(F32)
            dgn_ref[:, cols] += jnp.sum(dh * on * sil, axis=0, keepdims=True)
            dg = dh * on * gn * (sg * (1.0 + gr * (1.0 - sg)))
            don = dh * gn * sil
            do_bf = (r * (don - on * jnp.mean(don * on, axis=-1, keepdims=True))).astype(BF16)
            qe = (t["qf"] * t["e_q"]).astype(BF16)
            ke = (t["kk"] * t["e_k"]).astype(BF16)
            qb = (t["qf"] * t["e_b"]).astype(BF16)
            kh_bf = (t["kk"] * t["e_lb"]).astype(BF16)
            a_bf = jnp.where(tri, _dot(qe, ke, NT), 0.0).astype(BF16)
            da_bf = jnp.where(tri, _dot(do_bf, v_bf, NT), 0.0).astype(BF16)
            dv = _dot(a_bf, do_bf, TN_) + _dot(kh_bf, dst_bf, NT)
            dqe = _dot(da_bf, ke, NN)
            dqb = _dot(do_bf, st_bf, NN)
            dke = _dot(da_bf, qe, TN_)
            dkh = _dot(v_bf, dst_bf, NN)
            dqf = dqe * t["e_q"] + dqb * t["e_b"]
            dkk = dke * t["e_k"] + dkh * t["e_lb"]
            kh_r = kh_bf.astype(F32)
            db = qe.astype(F32) * dqe - ke.astype(F32) * dke + qb.astype(F32) * dqb - kh_r * dkh
            e_bl = jnp.exp(t["bl"])
            dbl = jnp.sum(dkh * kh_r, axis=0, keepdims=True) + e_bl * jnp.sum(st_ref[0, h] * dst, axis=0, keepdims=True)
            dlg = _prefix_rows(db, reverse=True) + dbl
            ds_scr[h] = dst * e_bl + _dot(do_bf, qb, TN_)
            dgate = dlg / t["gate"] - dkk
            sf = t["sf"]
            dlb_scr[:, cols] += jnp.sum(dgate * (1.0 - sf), axis=0, keepdims=True)
            df = dgate * (1.0 - lb) * sf * (1.0 - sf)
            dq = dqf * (t["sq"] * (1.0 + qr * (1.0 - t["sq"])))
            dz_ref[:, cols] = dq.astype(BF16)
            dz_ref[:, D_MODEL + h * HEAD_W:D_MODEL + (h + 1) * HEAD_W] = df.astype(BF16)
            dz_ref[:, 2 * D_MODEL + h * HEAD_W:2 * D_MODEL + (h + 1) * HEAD_W] = dv.astype(BF16)
            dz_ref[:, 3 * D_MODEL + h * HEAD_W:3 * D_MODEL + (h + 1) * HEAD_W] = dg.astype(BF16)

        @pl.when(n == nc - 1)
        def _():
            d = s0 * s1 * dlb_scr[...]
            dlb_ref[0:1, :] = -d
            dlb_ref[1:2, :] = d

    seg = lambda k: pl.BlockSpec((C, D_MODEL), functools.partial(lambda n, k: (nc - 1 - n, k), k=k))
    nbytes = 6 * _nbytes((C, D_MODEL), F32) + _nbytes((C, 4 * D_MODEL), BF16) + 3 * _nbytes((HEADS, 128, 128), F32)
    return pl.pallas_call(
        kern, name="hgrn_bwd", grid=(nc,),
        in_specs=[seg(0), seg(1), seg(2), seg(3), seg(0), seg(0),
                  pl.BlockSpec((1, HEADS, 128, 128), lambda n: (nc - 1 - n, 0, 0, 0)),
                  pl.BlockSpec((2, D_MODEL), lambda n: (0, 0)), pl.BlockSpec((1, D_MODEL), lambda n: (0, 0))],
        out_specs=[pl.BlockSpec((C, 4 * D_MODEL), lambda n: (nc - 1 - n, 0)),
                   pl.BlockSpec((2, D_MODEL), lambda n: (0, 0)), pl.BlockSpec((1, D_MODEL), lambda n: (0, 0))],
        out_shape=[pltpu.HBM((T, 4 * D_MODEL), BF16), pltpu.HBM((2, D_MODEL), F32),
                   pltpu.HBM((1, D_MODEL), F32)],
        scratch_shapes=[pltpu.VMEM((HEADS, 128, 128), F32), pltpu.VMEM((1, D_MODEL), F32)],
        compiler_params=pltpu.CompilerParams(dimension_semantics=("arbitrary",), vmem_limit_bytes=_vmem(nbytes)),
    )(*[_hbm(a) for a in (z1, z1, z1, z1, o_pre, dhg, states, hg_lb, gnorm)])


def _prep_weights(gw):
    w_in_e = gw["w_in_e"].transpose(1, 0, 2).reshape(D_MODEL, 1568)
    kr = jnp.pad(w_in_e[:, 512:544], ((0, 0), (64, 32)))
    wm = jnp.concatenate([w_in_e[:, 0:512], kr], axis=1)
    ws = w_in_e[:, 544:1568]
    w_qb = gw["w_qb"].transpose(1, 0, 2).reshape(MLA_LORA, HEADS, 96)
    wq = jnp.pad(w_qb, ((0, 0), (0, 0), (0, 32))).reshape(MLA_LORA, HEADS * HEAD_W)
    kvb = gw["w_kvb"].transpose(1, 0, 2).reshape(MLA_LORA, HEADS, 128)
    wk = jnp.pad(kvb[:, :, :64], ((0, 0), (0, 0), (0, 64))).reshape(MLA_LORA, HEADS * HEAD_W)
    wv = jnp.pad(kvb[:, :, 64:], ((0, 0), (0, 0), (0, 64))).reshape(MLA_LORA, HEADS * HEAD_W)
    w_out_e = gw["w_out_e"].reshape(D_MODEL, D_MODEL)
    woa = jnp.pad(w_out_e[:512].reshape(HEADS, 64, D_MODEL), ((0, 0), (0, 64), (0, 0))).reshape(HEADS * HEAD_W, D_MODEL)
    return dict(wm=wm, ws=ws, wq=wq, wk=wk, wv=wv, woa=woa, wob=w_out_e[512:])


def _unprep_grads(g):
    dwm, dws = g["wm"], g["ws"]
    d_in_e = jnp.concatenate([dwm[:, 0:512], dwm[:, 512 + 64:512 + 96], dws], axis=1)
    d_qb = g["wq"].reshape(MLA_LORA, HEADS, HEAD_W)[:, :, :96].reshape(MLA_LORA, HEADS * 96)
    dk = g["wk"].reshape(MLA_LORA, HEADS, HEAD_W)[:, :, :64]
    dv = g["wv"].reshape(MLA_LORA, HEADS, HEAD_W)[:, :, :64]
    d_kvb = jnp.concatenate([dk, dv], axis=2).reshape(MLA_LORA, HEADS * 128)
    d_oa = g["woa"].reshape(HEADS, HEAD_W, D_MODEL)[:, :64].reshape(HEADS * 64, D_MODEL)
    dev_major = lambda a: a.reshape(a.shape[0], N_DEV, a.shape[1] // N_DEV).transpose(1, 0, 2)
    return dict(w_in_e=dev_major(d_in_e), w_qb=dev_major(d_qb), w_kvb=dev_major(d_kvb),
                w_out_e=jnp.concatenate([d_oa, g["wob"]], axis=0).reshape(N_DEV, D_MODEL // N_DEV, D_MODEL))


def _local_step(x, positions, target, gw, sp, ex):
    w = _prep_weights(gw)
    T = x.shape[0]
    tm = min(TM, T)
    nt = T // tm
    half = MLA_ROPE // 2
    inv_freq = ROPE_BASE ** (-jnp.arange(half, dtype=F32) / half)
    invf_lane = jnp.concatenate([jnp.zeros((64,), F32), inv_freq, inv_freq, jnp.zeros((32,), F32)]).reshape(1, HEAD_W)
    tabs = _rope_tables(positions.reshape(T, 1), invf_lane)
    bias_full = jnp.repeat(sp["sgu_b"][0].T, 128, axis=1)
    sgu_w = sp["sgu_w"]
    gq, gkv = sp["mla_gq"], sp["mla_gkv"]
    ln1_g, ln1_b, ln2_g, ln2_b = sp["ln1_g"], sp["ln1_b"], sp["ln2_g"], sp["ln2_b"]
    zm, zs, cqn, ckvn, kr_rot = _mla_in(x, w["wm"], w["ws"], tabs, gq, gkv, deps=[ex.first_token])
    q, k, v = _mla_qkv(cqn, ckvn, kr_rot, tabs, w["wq"], w["wk"], w["wv"])
    o_att, lse = _attn_fwd(q, k, v)
    b_out = _sgu_fwd(zs, sp["sgu_ln_g"], sp["sgu_ln_b"], sgu_w, bias_full)
    y1, h1, h1_bf = _proj_ln("l0_out_ln1", [o_att, b_out], [w["woa"], w["wob"]], x, ln1_g, ln1_b, 0)
    big = ex.weights_ready(after=y1)
    w_ff1, w_in_o, w_out_o = big["w_ff1"], big["w_in_o"], big["w_out_o"].reshape(D_MODEL, D_MODEL)
    w_ff2 = [a.reshape(D_FF, D_MODEL) for a in big["w_ff2"]]
    a0, act0 = _mlp_up("l0", h1_bf, w_ff1[0])
    y2, h2, h2_bf = _proj_ln("l0_ff2_ln2", [act0], [w_ff2[0]], h1, ln2_g, ln2_b, 0)

    z1 = _tiled("l1_in", (1, nt), [_rb(h2_bf, tm), _res(w_in_o)], [_out(T, 4 * D_MODEL, F32, tm, 4 * D_MODEL)],
                _mmc_blocks(N_DEV, NN, lambda w, d: w[d]), direct=True)
    o_pre, hg, states = _hgrn_fwd(z1, sp["hg_lb"], sp["hg_gnorm"])
    y3, h3, h3_bf = _proj_ln("l1_out_ln1", [hg], [w_out_o], h2, ln1_g, ln1_b, 1)
    a1, act1 = _mlp_up("l1", h3_bf, w_ff1[1])

    gs, g0 = {}, {}
    dy4, dy4_bf, sq_err, gs["ln2_g1"], gs["ln2_b1"] = _proj_ln_loss("l1_ff2_loss", act1, w_ff2[1], h3, ln2_g, ln2_b, 1, target)
    da1, dw1_1, dw2_1 = _mlp_bwd_w("l1", h3_bf, a1, act1, dy4_bf, big["w_ff2"][1])
    dy3, dy3_bf, dhg, gs["ln1_g1"], gs["ln1_b1"] = _dh_ln_back("l1_dh_ln1", da1, w_ff1[1], dy4, y3, ln1_g, 1, proj=w_out_o)
    d_out_o = _tiled("l1_dwout", (2, D_MODEL // TM), [_tl(hg, TM), _cw(dy3_bf, TN)], [_out(D_MODEL, D_MODEL, F32, TM, TN)],
                     _mmc(TN_)).reshape(N_DEV, D_MODEL // N_DEV, D_MODEL)
    dz1, gs["hg_lb"], gs["hg_gnorm"] = _hgrn_bwd(z1, o_pre, dhg, states, sp["hg_lb"], sp["hg_gnorm"])
    d_in_o = _tiled("l1_dwin", (N_DEV, 1), [_res(h2_bf), _cw(dz1, TN)], [_out_dev(D_MODEL, TN, D_MODEL)], _mmc(TN_))
    token = ex.grads_start("l1", [dw1_1, dw2_1, d_in_o, d_out_o])

    dy2, dy2_bf, gs["ln2_g0"], gs["ln2_b0"] = _dh_ln_back("l1_dh_ln2", dz1, w_in_o, dy3, y2, ln2_g, 0, deps=[token])
    token = ex.grads_middle("l1", after=dy2)
    da0, dw1_0, dw2_0 = _mlp_bwd_w("l0", h1_bf, a0, act0, dy2_bf, big["w_ff2"][0], deps=[token])
    token = ex.grads_start("l0m", [dw1_0, dw2_0])
    wo_cat = jnp.concatenate([w["woa"], w["wob"]], axis=0)
    dy1, dy1_bf, dcat, gs["ln1_g0"], gs["ln1_b0"] = _dh_ln_back("l0_dh_ln1", da0, w_ff1[0], dy2, y1, ln1_g, 0, proj=wo_cat, deps=[token])
    ex.grads_end("l1", after=dy1)
    g0["woa"], g0["wob"] = _out_weight_grads(o_att, b_out, dy1_bf)
    token = ex.grads_middle("l0m", after=g0["wob"])
    dzs, gs["sgu_w"], gs["sgu_ln_g"], gs["sgu_ln_b"], gs["sgu_b"] = _sgu_bwd(zs, dcat, sp["sgu_ln_g"], sp["sgu_ln_b"], sgu_w, bias_full)
    dq, dk, dv = _attn_bwd(q, k, v, o_att, lse, dcat, deps=[token])
    ex.grads_end("l0m", after=dq)
    dzm, g0["wq"], g0["wk"], g0["wv"], gs["mla_gq"], gs["mla_gkv"] = _mla_back(zm, cqn, ckvn, tabs, gq, gkv, w["wq"], w["wk"], w["wv"],
                                                                                 dq, dk, dv)
    dx, g0["wm"], g0["ws"] = _in_back(x, dzm, dzs, dy1, w["wm"], w["ws"])

    return sq_err, dx, _unprep_grads(g0), gs


def _me():
    return lax.axis_index("x"), lax.axis_index("y"), lax.axis_index("c")


def _hbm_call(name, kern, operands, out_shape, n_sems, extra_scratch=()):
    any_spec = pl.BlockSpec(memory_space=pl.ANY)
    return pl.pallas_call(
        kern, name=name, out_shape=out_shape, in_specs=[any_spec] * len(operands), out_specs=[any_spec] * len(out_shape),
        scratch_shapes=[pltpu.SemaphoreType.DMA((n_sems,)), pltpu.SemaphoreType.DMA((n_sems,)), *extra_scratch],
    )(*[_hbm(a) for a in operands])


ANY_SPEC = pl.BlockSpec(memory_space=pl.ANY)
HBM_SPEC = pl.BlockSpec(memory_space=pltpu.HBM)
SEM_SPEC = pl.BlockSpec(memory_space=pltpu.SEMAPHORE)
EFFECT = pltpu.SideEffectType.DATAFLOW_SIDE_EFFECTING


def _split_start(name, srcs, lands, n_sems, make_copies, after=()):
    n, m, k = len(srcs), len(lands), len(after)

    def body(*refs):
        for cp in make_copies(refs[:n], refs[n:n + m], refs[n + m + k], refs[n + m + k + 1]):
            cp.start()
        refs[-1][...] = jnp.zeros(refs[-1].shape, F32)

    out_shape = (pltpu.SemaphoreType.DMA((n_sems,)), pltpu.SemaphoreType.DMA((n_sems,)),
                 *[pltpu.HBM(a.shape, a.dtype) for a in (*srcs, *lands)], jax.ShapeDtypeStruct((8, 128), F32))
    res = pl.pallas_call(
        body, name=name, out_shape=out_shape, in_specs=[HBM_SPEC] * (n + m) + [ANY_SPEC] * k,
        out_specs=(SEM_SPEC, SEM_SPEC, *[HBM_SPEC] * (n + m), pl.BlockSpec(memory_space=pltpu.VMEM)),
        input_output_aliases={i: 2 + i for i in range(n + m)},
        compiler_params=pltpu.CompilerParams(has_side_effects=EFFECT),
    )(*[_hbm(a) for a in (*srcs, *lands)], *after)
    return res[0], res[1], list(res[2:2 + n]), list(res[2 + n:2 + n + m]), res[-1]


def _split_wait(name, send_sems, recv_sems, srcs, lands, after, make_copies):
    n, m = len(srcs), len(lands)

    def body(*refs):
        for cp in make_copies(refs[:n], refs[n:n + m], refs[n + m], refs[n + m + 1]):
            cp.wait_send()
            cp.wait_recv()

    res = pl.pallas_call(
        body, name=name, out_shape=tuple(pltpu.HBM(a.shape, a.dtype) for a in (*srcs, *lands)),
        in_specs=[HBM_SPEC] * (n + m) + [SEM_SPEC, SEM_SPEC] + [ANY_SPEC] * len(after), out_specs=tuple([HBM_SPEC] * (n + m)),
        input_output_aliases={i: i for i in range(n + m)},
        compiler_params=pltpu.CompilerParams(has_side_effects=EFFECT),
    )(*srcs, *lands, send_sems, recv_sems, *after)
    return list(res[:n]), list(res[n:])


def _place_own(shards, dev):
    n = len(shards)

    def kern(dev_ref, *refs):
        for x_ref, o_ref in zip(refs[:n], refs[n:]):
            o_ref[...] = x_ref[...].astype(o_ref.dtype)

    blocks = [(None, *a.shape[1:]) for a, _, _ in shards]
    nbytes = sum(_nbytes(b, a.dtype) + _nbytes(b, dt) for b, (a, _, dt) in zip(blocks, shards))
    return pl.pallas_call(
        kern, name="weights_place_own", out_shape=[pltpu.HBM((N_DEV, *a.shape[1:]), dt) for a, _, dt in shards],
        grid_spec=pltpu.PrefetchScalarGridSpec(
            num_scalar_prefetch=1, grid=(1,),
            in_specs=[pl.BlockSpec(b, functools.partial(lambda i, dev, l: (l, 0, 0), l=l)) for b, (_, l, _) in zip(blocks, shards)],
            out_specs=[pl.BlockSpec(b, lambda i, dev: (dev[0], 0, 0)) for b in blocks]),
        compiler_params=pltpu.CompilerParams(dimension_semantics=("arbitrary",), vmem_limit_bytes=_vmem(nbytes)),
    )(dev, *[_hbm(a) for a, _, _ in shards])


def _ag_first_copies(src_refs, out_refs, send_sems, recv_sems):
    x, y, c = _me()
    targets = [(x, y, 1 - c), (1 - x, y, c), (x, 1 - y, c), (1 - x, 1 - y, c)]
    return [pltpu.make_async_remote_copy(
        src_ref=out_refs[op].at[4 * x + 2 * y + c], dst_ref=out_refs[op].at[4 * x + 2 * y + c], send_sem=send_sems.at[4 * op + k],
        recv_sem=recv_sems.at[4 * op + k], device_id=to, device_id_type=MESH)
        for op in range(len(out_refs)) for k, to in enumerate(targets)]


def _ag_second(gathered):
    n = len(gathered)

    def kern(*refs):
        in_refs, out_refs, (send_sems, recv_sems) = refs[:n], refs[n:2 * n], refs[2 * n:]
        x, y, c = _me()
        chips = [(1 - x, y), (x, 1 - y), (1 - x, 1 - y)]
        passed = [pltpu.make_async_remote_copy(
            src_ref=in_refs[op].at[4 * cx + 2 * cy + c], dst_ref=out_refs[op].at[4 * cx + 2 * cy + c],
            send_sem=send_sems.at[3 * op + j], recv_sem=recv_sems.at[3 * op + j], device_id=(x, y, 1 - c), device_id_type=MESH)
            for op in range(n) for j, (cx, cy) in enumerate(chips)]
        for cp in passed:
            cp.start()
        for cp in passed:
            cp.wait_send()
        for op in range(n):
            for j, (cx, cy) in enumerate(chips):
                slot = out_refs[op].at[4 * cx + 2 * cy + 1 - c]
                pltpu.make_async_remote_copy(src_ref=slot, dst_ref=slot, send_sem=send_sems.at[3 * op + j],
                                             recv_sem=recv_sems.at[3 * op + j], device_id=(x, y, c), device_id_type=MESH).wait_recv()

    return pl.pallas_call(
        kern, name="weights_all_gather_second", out_shape=[pltpu.HBM(g.shape, g.dtype) for g in gathered],
        in_specs=[ANY_SPEC] * n, out_specs=[ANY_SPEC] * n, input_output_aliases={i: i for i in range(n)},
        scratch_shapes=[pltpu.SemaphoreType.DMA((3 * n,)), pltpu.SemaphoreType.DMA((3 * n,))],
    )(*[_hbm(a) for a in gathered])


def _rs_sibling_copies(g_refs, out_refs, send_sems, recv_sems):
    x, y, c = _me()
    return [pltpu.make_async_remote_copy(
        src_ref=g_refs[op].at[k, 1 - c], dst_ref=out_refs[op].at[k], send_sem=send_sems.at[4 * op + k],
        recv_sem=recv_sems.at[4 * op + k], device_id=(x, y, 1 - c), device_id_type=MESH)
        for op in range(len(g_refs)) for k in range(4)]


def _rs_chip_copies(p_refs, out_refs, send_sems, recv_sems):
    x, y, c = _me()
    chips = [(1 - x, y), (x, 1 - y), (1 - x, 1 - y)]
    return [pltpu.make_async_remote_copy(
        src_ref=p_refs[op].at[2 * cx + cy], dst_ref=out_refs[op].at[j], send_sem=send_sems.at[3 * op + j],
        recv_sem=recv_sems.at[3 * op + j], device_id=(cx, cy, c), device_id_type=MESH)
        for op in range(len(p_refs)) for j, (cx, cy) in enumerate(chips)]


def _all_gather(placed):
    n = len(placed)

    def kern(*refs):
        in_refs, out_refs, (send_sems, recv_sems) = refs[:n], refs[n:2 * n], refs[2 * n:]
        x, y, c = _me()
        me, sibling = (x, y, c), (x, y, 1 - c)
        chips = [(1 - x, y), (x, 1 - y), (1 - x, 1 - y)]

        def copy(op, k, block, to, own=False):
            idx = 4 * block[0] + 2 * block[1] + block[2]
            return pltpu.make_async_remote_copy(
                src_ref=(in_refs if own else out_refs)[op].at[idx], dst_ref=out_refs[op].at[idx], send_sem=send_sems.at[7 * op + k],
                recv_sem=recv_sems.at[7 * op + k], device_id=to, device_id_type=MESH)

        first = []
        for op in range(n):
            first.append(copy(op, 0, me, sibling, own=True))
            first += [copy(op, 1 + j, me, (*chip, c), own=True) for j, chip in enumerate(chips)]
        for cp in first:
            cp.start()
        passed = []
        for j, chip in enumerate(chips):
            for op in range(n):
                copy(op, 1 + j, (*chip, c), me).wait_recv()
                passed.append(copy(op, 4 + j, (*chip, c), sibling))
                passed[-1].start()
        for op in range(n):
            copy(op, 0, sibling, me).wait_recv()
            for j, chip in enumerate(chips):
                copy(op, 4 + j, (*chip, 1 - c), me).wait_recv()
        for cp in first + passed:
            cp.wait_send()

    return pl.pallas_call(
        kern, name="weights_all_gather", out_shape=[pltpu.HBM(g.shape, g.dtype) for g in placed],
        in_specs=[ANY_SPEC] * n, out_specs=[ANY_SPEC] * n, input_output_aliases={i: i for i in range(n)},
        scratch_shapes=[pltpu.SemaphoreType.DMA((7 * n,)), pltpu.SemaphoreType.DMA((7 * n,))],
    )(*[_hbm(a) for a in placed])


def _rs_sibling(grads):
    n = len(grads)

    def kern(*refs):
        g_refs, out_refs, (send_sems, recv_sems) = refs[:n], refs[n:2 * n], refs[2 * n:]
        x, y, c = _me()
        copies = [pltpu.make_async_remote_copy(
            src_ref=g_refs[op].at[k, 1 - c], dst_ref=out_refs[op].at[k], send_sem=send_sems.at[4 * op + k],
            recv_sem=recv_sems.at[4 * op + k], device_id=(x, y, 1 - c), device_id_type=MESH) for op in range(n) for k in range(4)]
        for cp in copies:
            cp.start()
        for cp in copies:
            cp.wait()

    out_shape = [pltpu.HBM((4, *g.shape[2:]), g.dtype) for g in grads]
    return _hbm_call("grads_to_sibling", kern, grads, out_shape, 4 * n)


def _rs_chips(sums):
    n = len(sums)

    def kern(*refs):
        p_refs, out_refs, (send_sems, recv_sems) = refs[:n], refs[n:2 * n], refs[2 * n:]
        x, y, c = _me()
        chips = [(1 - x, y), (x, 1 - y), (1 - x, 1 - y)]
        copies = [pltpu.make_async_remote_copy(
            src_ref=p_refs[op].at[2 * cx + cy], dst_ref=out_refs[op].at[j], send_sem=send_sems.at[3 * op + j],
            recv_sem=recv_sems.at[3 * op + j], device_id=(cx, cy, c), device_id_type=MESH)
            for op in range(n) for j, (cx, cy) in enumerate(chips)]
        for cp in copies:
            cp.start()
        for cp in copies:
            cp.wait()

    out_shape = [pltpu.HBM((3, *p.shape[1:]), p.dtype) for p in sums]
    return _hbm_call("grads_between_chips", kern, sums, out_shape, 3 * n)


def _row_tile(r):
    return r if r <= 256 else 256


def _chip_sum(name, g, from_sibling, core):
    _, _, R, W = g.shape
    tr = _row_tile(R)

    def kern(core_ref, g_ref, s_ref, o_ref):
        o_ref[...] = (g_ref[...] + s_ref[...]).astype(BF16)

    return pl.pallas_call(
        kern, name=name, out_shape=pltpu.HBM((4, R, W), BF16),
        grid_spec=pltpu.PrefetchScalarGridSpec(
            num_scalar_prefetch=1, grid=(4, R // tr),
            in_specs=[pl.BlockSpec((None, None, tr, W), lambda k, i, core: (k, core[0], i, 0)),
                      pl.BlockSpec((None, tr, W), lambda k, i, core: (k, i, 0))],
            out_specs=pl.BlockSpec((None, tr, W), lambda k, i, core: (k, i, 0))),
        compiler_params=pltpu.CompilerParams(dimension_semantics=("parallel", "parallel"), vmem_limit_bytes=_vmem(3 * tr * W * 4)),
    )(core, _hbm(g), _hbm(from_sibling))


def _adamw(w, g, m, v):
    m = ADAM_B1 * m + (1.0 - ADAM_B1) * g
    v = ADAM_B2 * v + (1.0 - ADAM_B2) * (g * g)
    m_hat = m / (1.0 - ADAM_B1 ** ADAM_STEP)
    v_hat = v / (1.0 - ADAM_B2 ** ADAM_STEP)
    return -ADAM_LR * (m_hat / (jnp.sqrt(v_hat) + ADAM_EPS) + ADAM_WD * w), m, v


def _finish_sharded(name, layers, w, m, v, where):
    nl, R, W = w.shape
    tr = _row_tile(R)

    def kern(where_ref, *refs):
        w_ref, m_ref, v_ref, go_ref, d_ref, mo_ref, vo_ref = refs[3 * nl:]
        for l in range(nl):
            g_ref, s_ref, c_ref = refs[3 * l:3 * l + 3]
            grad = g_ref[...] + s_ref[...]
            for j in range(3):
                grad = grad + c_ref[j].astype(F32)
            go_ref[l] = grad
            d_ref[l], mo_ref[l], vo_ref[l] = _adamw(w_ref[l], grad, m_ref[l], v_ref[l])

    row = pl.BlockSpec((nl, tr, W), lambda i, wh: (0, i, 0))
    in_specs, args = [], []
    for g, s, c in layers:
        in_specs += [pl.BlockSpec((None, None, tr, W), lambda i, wh: (wh[0], wh[1], i, 0)),
                     pl.BlockSpec((None, tr, W), lambda i, wh: (wh[0], i, 0)),
                     pl.BlockSpec((3, tr, W), lambda i, wh: (0, i, 0))]
        args += [g, s, c]
    return pl.pallas_call(
        kern, name=name, out_shape=[pltpu.HBM((nl, R, W), F32)] * 4,
        grid_spec=pltpu.PrefetchScalarGridSpec(num_scalar_prefetch=1, grid=(R // tr,), in_specs=in_specs + [row, row, row],
                                               out_specs=[row, row, row, row]),
        compiler_params=pltpu.CompilerParams(dimension_semantics=("parallel",), vmem_limit_bytes=_vmem(nl * 11 * tr * W * 4)),
    )(where, *[_hbm(a) for a in (*args, w, m, v)])


SMALL_PLACE = (("mla_gq", 0, 0, 1, 256), ("mla_gkv", 0, 256, 1, 256), ("sgu_ln_g", 0, 512, 1, 512), ("sgu_ln_b", 1, 0, 1, 512),
               ("hg_lb", 2, 0, 2, 1024), ("ln1_g", 4, 0, 2, 1024), ("ln1_b", 6, 0, 2, 1024), ("sgu_b", 8, 0, 4, 128),
               ("ln2_g", 12, 0, 2, 1024), ("ln2_b", 14, 0, 2, 1024), ("hg_gnorm", 16, 0, 1, 1024))
SMALL_BUF_ROWS = 24


def _small_reduce_adamw(gs, given):
    pieces = [(gs["mla_gq"], 0, 0), (gs["mla_gkv"], 0, 256), (gs["sgu_ln_g"], 0, 512), (gs["sgu_ln_b"], 1, 0), (gs["hg_lb"], 2, 0),
              (gs["ln1_g0"], 4, 0), (gs["ln1_g1"], 5, 0), (gs["ln1_b0"], 6, 0), (gs["ln1_b1"], 7, 0), (gs["sgu_b"], 8, 0),
              (gs["ln2_g0"], 12, 0), (gs["ln2_g1"], 13, 0), (gs["ln2_b0"], 14, 0), (gs["ln2_b1"], 15, 0), (gs["hg_gnorm"], 16, 0)]
    names = [p[0] for p in SMALL_PLACE] + ["sgu_w"]
    n_p, n_names = len(pieces), len(names)
    wmv = [given[pre + name] for name in names for pre in ("", "m_", "v_")]

    def kern(*refs):
        piece_refs, gw_ref = refs[:n_p], refs[n_p]
        wmv_refs = refs[n_p + 1:n_p + 1 + 3 * n_names]
        out_refs = refs[n_p + 1 + 3 * n_names:n_p + 1 + 7 * n_names]
        buf_a, buf_b, send_sems, recv_sems = refs[n_p + 1 + 7 * n_names:]
        px, py, pc = _me()
        me = 4 * px + 2 * py + pc
        mine_a, mine_b = buf_a.at[me], buf_b.at[me]
        mine_a[...] = jnp.zeros(mine_a.shape, F32)
        for ref, (_, r, l0) in zip(piece_refs, pieces):
            mine_a[r:r + ref.shape[0], l0:l0 + ref.shape[1]] = ref[...]
        mine_b[...] = gw_ref[...]
        copies = []
        for r in range(1, N_DEV):
            peer = (px ^ (r >> 2), py ^ ((r >> 1) & 1), pc ^ (r & 1))
            for k, mine in enumerate((mine_a, mine_b)):
                copies.append(pltpu.make_async_remote_copy(
                    src_ref=mine, dst_ref=mine, send_sem=send_sems.at[2 * (r - 1) + k], recv_sem=recv_sems.at[2 * (r - 1) + k],
                    device_id=peer, device_id_type=MESH))
        for cp in copies:
            cp.start()
        for r in range(1, N_DEV):
            for k, buf in enumerate((buf_a, buf_b)):
                theirs = buf.at[me ^ r]
                pltpu.make_async_remote_copy(
                    src_ref=theirs, dst_ref=theirs, send_sem=send_sems.at[2 * (r - 1) + k], recv_sem=recv_sems.at[2 * (r - 1) + k],
                    device_id=(px, py, pc), device_id_type=MESH).wait_recv()
        for cp in copies:
            cp.wait_send()
        sum_a, sum_b = buf_a[0], buf_b[0]
        for d in range(1, N_DEV):
            sum_a, sum_b = sum_a + buf_a[d], sum_b + buf_b[d]

        def own_block(full):
            acc = full[:, 0:128]
            for b in range(1, N_DEV):
                acc = jnp.where(me == b, full[:, b * 128:(b + 1) * 128], acc)
            return acc

        for idx, name in enumerate(names):
            w_ref, m_ref, v_ref = wmv_refs[3 * idx:3 * idx + 3]
            if name == "sgu_w":
                grad = sum_b[None]
            else:
                _, r, l0, nr, nl = SMALL_PLACE[idx]
                grad = sum_a[r:r + nr, l0:l0 + nl]
                if name == "hg_gnorm":
                    grad = own_block(grad)
                if name == "sgu_b":
                    grad = grad[None]
            res = (grad, *_adamw(w_ref[...], grad, m_ref[...], v_ref[...]))
            for o_ref, val in zip(out_refs[4 * idx:4 * idx + 4], res):
                o_ref[...] = val

    vmem = pl.BlockSpec(memory_space=pltpu.VMEM)
    operands = [p[0] for p in pieces] + [gs["sgu_w"]] + wmv
    out_shape = [jax.ShapeDtypeStruct(given[name].shape, F32) for name in names for _ in range(4)]
    res = pl.pallas_call(
        kern, name="small_all_reduce_adamw", out_shape=out_shape, in_specs=[vmem] * len(operands), out_specs=[vmem] * len(out_shape),
        scratch_shapes=[pltpu.VMEM((N_DEV, SMALL_BUF_ROWS, D_MODEL), F32), pltpu.VMEM((N_DEV, SGU_G, 128, 128), F32),
                        pltpu.SemaphoreType.DMA((14,)), pltpu.SemaphoreType.DMA((14,))],
    )(*operands)
    return {name: res[4 * idx:4 * idx + 4] for idx, name in enumerate(names)}


class _Exchange:
    def __init__(self, given):
        self.given = given
        px, py, pc = _me()
        self.core = pc.reshape(1).astype(jnp.int32)
        self.dev = (4 * px + 2 * py + pc).reshape(1).astype(jnp.int32)
        self.where = jnp.stack([2 * px + py, pc]).astype(jnp.int32)
        self.state, self.layers = {}, {}

    def start_weights(self, lands, after):
        self.weights = _split_start("weights_first_start", [], lands, 4 * len(lands), _ag_first_copies, after=after)
        self.first_token = self.weights[4]

    def weights_ready(self, after):
        send_sems, recv_sems, shards, lands, _ = self.weights
        _, lands = _split_wait("weights_first_wait", send_sems, recv_sems, shards, lands, [after], _ag_first_copies)
        got = _ag_second(lands)
        return dict(w_in_o=got[0], w_out_o=got[1], w_ff1=[got[2], got[3]], w_ff2=[got[4], got[5]])

    def grads_start(self, tag, grads):
        blocks = [g.reshape(4, 2, *g.shape[1:]) for g in grads]
        lands = [lax.empty((4, *b.shape[2:]), F32) for b in blocks]
        self.state[tag] = _split_start(f"grads_{tag}_sibling_start", blocks, lands, 4 * len(blocks), _rs_sibling_copies)
        return self.state[tag][4]

    def grads_middle(self, tag, after):
        send_sems, recv_sems, blocks, lands, _ = self.state[tag]
        blocks, from_sibling = _split_wait(f"grads_{tag}_sibling_wait", send_sems, recv_sems, blocks, lands, [after], _rs_sibling_copies)
        sums = [_chip_sum(f"grads_{tag}_chip_sum_{k}", b, s, self.core) for k, (b, s) in enumerate(zip(blocks, from_sibling))]
        lands = [lax.empty((3, *p.shape[1:]), BF16) for p in sums]
        self.state[tag] = (blocks, from_sibling, _split_start(f"grads_{tag}_chips_start", sums, lands, 3 * len(sums), _rs_chip_copies))
        return self.state[tag][2][4]

    def grads_end(self, tag, after):
        blocks, from_sibling, (send_sems, recv_sems, sums, lands, _) = self.state[tag]
        _, from_chips = _split_wait(f"grads_{tag}_chips_wait", send_sems, recv_sems, sums, lands, [after], _rs_chip_copies)
        self.layers[tag] = list(zip(blocks, from_sibling, from_chips))


SHARDED = ("w_in_e", "w_qb", "w_kvb", "w_out_e", "w_in_o", "w_out_o", "w_ff1", "w_ff2")


def kernel(x, positions, w_in_e, mla_gq, mla_gkv, w_qb, w_kvb, sgu_ln_g, sgu_ln_b, sgu_w, sgu_b, w_out_e, w_in_o, hg_lb, hg_gnorm, w_out_o, ln1_g, ln1_b, w_ff1, w_ff2, ln2_g, ln2_b, loss_target, m_w_in_e, m_mla_gq, m_mla_gkv, m_w_qb, m_w_kvb, m_sgu_ln_g, m_sgu_ln_b, m_sgu_w, m_sgu_b, m_w_out_e, m_w_in_o, m_hg_lb, m_hg_gnorm, m_w_out_o, m_ln1_g, m_ln1_b, m_w_ff1, m_w_ff2, m_ln2_g, m_ln2_b, v_w_in_e, v_mla_gq, v_mla_gkv, v_w_qb, v_w_kvb, v_sgu_ln_g, v_sgu_ln_b, v_sgu_w, v_sgu_b, v_w_out_e, v_w_in_o, v_hg_lb, v_hg_gnorm, v_w_out_o, v_ln1_g, v_ln1_b, v_w_ff1, v_w_ff2, v_ln2_g, v_ln2_b):
    given = dict(locals())
    ex = _Exchange(given)

    names = ["w_in_e", "w_qb", "w_kvb", "w_out_e"]
    placed = _place_own([(given[n], 0, BF16) for n in names] + [(hg_gnorm.reshape(1, 1, D_MODEL // N_DEV), 0, F32)]
                        + [(w_in_o, 0, BF16), (w_out_o, 0, BF16), (w_ff1, 0, BF16), (w_ff1, 1, BF16), (w_ff2, 0, BF16), (w_ff2, 1, BF16)],
                        ex.dev)
    got = _all_gather(placed[:5])
    ex.start_weights(placed[5:], after=[got[0]])
    gw = dict(zip(names, got[:4]))
    small_names = ["mla_gq", "mla_gkv", "sgu_ln_g", "sgu_ln_b", "sgu_w", "sgu_b", "hg_lb", "ln1_g", "ln1_b", "ln2_g", "ln2_b"]
    sp = {n: given[n] for n in small_names}
    sp["hg_gnorm"] = got[4].reshape(1, D_MODEL)

    sq_err, dx, grads, gs = _local_step(x[0], positions[0], loss_target[0], gw, sp, ex)
    loss = lax.psum(0.5 * jnp.sum(sq_err) / D_MODEL, ("x", "y", "c"))

    blocks = [grads[n].reshape(4, 2, *grads[n].shape[1:]) for n in names]
    from_sibling = _rs_sibling(blocks)
    chip_sums = [_chip_sum(f"grads_l0_chip_sum_{k}", b, s, ex.core) for k, (b, s) in enumerate(zip(blocks, from_sibling))]
    from_chips = _rs_chips(chip_sums)
    per_weight = dict(zip(names, [[l] for l in zip(blocks, from_sibling, from_chips)]))
    l1, l0m = ex.layers["l1"], ex.layers["l0m"]
    per_weight.update(w_ff1=[l0m[0], l1[0]], w_ff2=[l0m[1], l1[1]], w_in_o=[l1[2]], w_out_o=[l1[3]])
    results = {n: _finish_sharded(f"finish_{n}", per_weight[n], given[n], given["m_" + n], given["v_" + n], ex.where) for n in SHARDED}

    results.update(_small_reduce_adamw(gs, given))

    order = ["w_in_e", "mla_gq", "mla_gkv", "w_qb", "w_kvb", "sgu_ln_g", "sgu_ln_b", "sgu_w", "sgu_b", "w_out_e", "w_in_o",
             "hg_lb", "hg_gnorm", "w_out_o", "ln1_g", "ln1_b", "w_ff1", "w_ff2", "ln2_g", "ln2_b"]
    return (loss, dx[None], *[results[name][kind] for kind in range(4) for name in order])
```

```python
import functools
import math

import jax
import jax.numpy as jnp
import numpy as np
from jax import lax
from jax.experimental import pallas as pl
from jax.experimental.pallas import tpu as pltpu

F32 = jnp.float32
BF16 = jnp.bfloat16
MESH = pl.DeviceIdType.MESH
HIGHEST = lax.Precision.HIGHEST

D_MODEL = 1024
D_FF = 4096
N_DEV = 8
HEADS = 8
HEAD_W = 128
MLA_NOPE = 64
MLA_ROPE = 32
MLA_V = 64
MLA_LORA = 256
MLA_SCALE = (MLA_NOPE + MLA_ROPE) ** -0.5
ROPE_BASE = 10000.0
SGU_DIM = 512
SGU_G = 4
SGU_CHUNK = 128
HG_CHUNK = 64
ALPHA = (2 * 2) ** 0.25
EPS = 1e-5
ADAM_LR, ADAM_B1, ADAM_B2, ADAM_EPS, ADAM_WD, ADAM_STEP = 0.001, 0.9, 0.999, 1e-08, 0.01, 10

VMEM_CAP_V7X = 56 * 2**20
VMEM_SLACK = 12 * 2**20
TM = 512
TN = 512


def _vmem(block_bytes):
    return int(min(VMEM_CAP_V7X, 2 * block_bytes + VMEM_SLACK))


def _hbm(a):
    return pltpu.with_memory_space_constraint(a, pltpu.HBM)


def _nbytes(shape, dtype):
    return int(np.prod([d for d in shape if d is not None])) * jnp.dtype(dtype).itemsize


def _sig(x):
    return 1.0 / (1.0 + jnp.exp(-x))


def _gelu(x):
    c = math.sqrt(2.0 / math.pi)
    t = jnp.tanh(c * (x + 0.044715 * x * x * x))
    return 0.5 * x * (1.0 + t), t


def _gelu_grad(x, t):
    c = math.sqrt(2.0 / math.pi)
    return 0.5 * (1.0 + t) + 0.5 * x * (1.0 - t * t) * c * (1.0 + 3 * 0.044715 * x * x)


def _dot(a, b, dims, precision=None):
    return lax.dot_general(a, b, (dims, ((), ())), preferred_element_type=F32, precision=precision)


NN = ((1,), (0,))
NT = ((1,), (1,))
TN_ = ((0,), (0,))


def _deps(deps):
    return [d for d in deps if d is not None]


def _tiled(name, grid, ins, outs, compute, direct=False, deps=()):
    n_in, deps = len(ins), _deps(deps)
    n_skip = n_in + len(deps)

    def kern(*refs):
        if direct:
            compute(refs[:n_in], refs[n_skip:])
            return
        for o_ref, r in zip(refs[n_skip:], compute(*refs[:n_in])):
            o_ref[...] = r.astype(o_ref.dtype).reshape(o_ref.shape)

    swap = lambda f: (lambda j, i: f(i, j))
    nbytes = sum(_nbytes(blk, a.dtype) for a, blk, _ in ins) + sum(_nbytes(blk, dt) + _nbytes(blk, F32) for _, dt, blk, _ in outs)
    res = pl.pallas_call(
        kern, name=name, grid=grid,
        in_specs=[pl.BlockSpec(blk, swap(f), pipeline_mode=pl.Buffered(1) if tuple(blk) == tuple(a.shape) else None)
                  for a, blk, f in ins] + [ANY_SPEC] * len(deps),
        out_specs=[pl.BlockSpec(blk, swap(f)) for _, _, blk, f in outs],
        out_shape=[pltpu.HBM(shape, dt) for shape, dt, _, _ in outs],
        compiler_params=pltpu.CompilerParams(dimension_semantics=("parallel", "parallel"), vmem_limit_bytes=_vmem(nbytes)),
    )(*[_hbm(a) for a, _, _ in ins], *deps)
    return res if len(res) > 1 else res[0]


def _rb(a, tm, w=None, cb=0):
    return (a, (tm, a.shape[1] if w is None else w), lambda i, j: (i, cb))


def _rbj(a, tm, tn):
    return (a, (tm, tn), lambda i, j: (i, j))


def _cw(b, tn):
    return (b, (b.shape[0], tn), lambda i, j: (0, j))


def _rw(b, tn):
    return (b, (tn, b.shape[1]), lambda i, j: (j, 0))


def _tl(a, tm):
    return (a, (a.shape[0], tm), lambda i, j: (0, i))


def _gcw(g):
    return (g, (None, g.shape[1], g.shape[2]), lambda i, j: (j, 0, 0))


def _grw(g, tn):
    return (g, (N_DEV, tn, g.shape[2]), lambda i, j: (0, j, 0))


def _out(m, n, dtype, tm, tn):
    return ((m, n), dtype, (tm, tn), lambda i, j: (i, j))


def _out_dev(k, n, tm):
    return ((N_DEV, k, n), F32, (None, tm, n), lambda i, j: (j, i, 0))


def _mmc(dims, n_pairs=1, epilogue=None):
    def compute(*refs):
        acc = None
        for k in range(n_pairs):
            d = _dot(refs[2 * k][...].astype(BF16), refs[2 * k + 1][...].astype(BF16), dims)
            acc = d if acc is None else acc + d
        ext = [r[...] for r in refs[2 * n_pairs:]]
        return epilogue(acc, *ext) if epilogue is not None else (acc,)

    return compute


def _res(w):
    return (w, w.shape, functools.partial(lambda i, j, nd: (0,) * nd, nd=w.ndim))


def _mmc_blocks(nblk, dims, rhs_block, epilogue=None):
    def compute(in_refs, out_refs):
        a = in_refs[0][...].astype(BF16)
        for d in range(nblk):
            acc = _dot(a, rhs_block(in_refs[1], d).astype(BF16), dims)
            n = acc.shape[1]
            ext = [r[:, d * n:(d + 1) * n] for r in in_refs[2:]]
            res = epilogue(acc, *ext) if epilogue is not None else (acc,)
            for o_ref, r in zip(out_refs, res):
                o_ref[:, d * n:(d + 1) * n] = r.astype(o_ref.dtype)

    return compute


def _mmc_dev(epilogue=None):
    def compute(a_ref, b_ref, *ext_refs):
        n = b_ref.shape[2]
        acc = None
        for d in range(N_DEV):
            t = _dot(a_ref[:, d * n:(d + 1) * n].astype(BF16), b_ref[d].astype(BF16), NT)
            acc = t if acc is None else acc + t
        ext = [r[...] for r in ext_refs]
        return epilogue(acc, *ext) if epilogue is not None else (acc,)

    return compute


def _rowwise(name, body, rows, consts, out_rows, out_accs=(), tr=512, deps=()):
    T = rows[0][0].shape[0]
    tr = min(tr, T)
    deps = _deps(deps)
    nr, ncn, no, nd = len(rows), len(consts), len(out_rows), len(deps)

    def kern(*refs):
        accs = refs[nr + ncn + nd + no:]
        if accs:
            @pl.when(pl.program_id(0) == 0)
            def _():
                for a in accs:
                    a[...] = jnp.zeros(a.shape, a.dtype)
        body(refs[:nr], refs[nr:nr + ncn], refs[nr + ncn + nd:nr + ncn + nd + no], accs)

    in_specs = [pl.BlockSpec((tr, w), functools.partial(lambda i, cb: (i, cb), cb=cb)) for _, w, cb in rows]
    in_specs += [pl.BlockSpec(c.shape, functools.partial(lambda i, nd: (0,) * nd, nd=c.ndim), pipeline_mode=pl.Buffered(1))
                 for c in consts]
    in_specs += [ANY_SPEC] * nd
    out_specs = [pl.BlockSpec((tr, w), lambda i: (i, 0)) for w, _ in out_rows]
    out_specs += [pl.BlockSpec(s, functools.partial(lambda i, nd: (0,) * nd, nd=len(s))) for s, _ in out_accs]
    out_shape = [pltpu.HBM((T, w), dt) for w, dt in out_rows]
    out_shape += [pltpu.HBM(s, dt) for s, dt in out_accs]
    nbytes = sum(_nbytes((tr, w), a.dtype) for a, w, _ in rows) + sum(_nbytes(c.shape, c.dtype) for c in consts)
    nbytes += sum(_nbytes((tr, w), dt) for w, dt in out_rows) + sum(_nbytes(s, dt) for s, dt in out_accs)
    res = pl.pallas_call(
        kern, name=name, grid=(T // tr,), in_specs=in_specs, out_specs=out_specs, out_shape=out_shape,
        compiler_params=pltpu.CompilerParams(dimension_semantics=("arbitrary",), vmem_limit_bytes=_vmem(nbytes)),
    )(*[_hbm(a) for a, _, _ in rows], *[_hbm(c) for c in consts], *deps)
    return res if len(res) > 1 else res[0]


def _full(a):
    return (a, a.shape[1], 0)


def _ln_stats(y):
    mu = jnp.mean(y, axis=-1, keepdims=True)
    yc = y - mu
    r = lax.rsqrt(jnp.mean(yc * yc, axis=-1, keepdims=True) + EPS)
    return yc * r, r


def _ln_back(dh, xh, r, gain, dg_ref, db_ref):
    dg_ref[...] += jnp.sum(dh * xh, axis=0, keepdims=True)
    db_ref[...] += jnp.sum(dh, axis=0, keepdims=True)
    dx = dh * gain
    return r * (dx - jnp.mean(dx, axis=-1, keepdims=True) - xh * jnp.mean(dx * xh, axis=-1, keepdims=True))


def _proj_ln(name, acts, weights, h_in, g, b, layer, deps=()):
    n = len(acts)

    def body(rows, consts, outs, accs):
        acc = None
        for k in range(n):
            d = _dot(rows[k][...].astype(BF16), consts[k][...], NN)
            acc = d if acc is None else acc + d
        y = ALPHA * rows[n][...] + acc
        xh, _ = _ln_stats(y)
        h = xh * consts[n][layer:layer + 1, :] + consts[n + 1][layer:layer + 1, :]
        outs[0][...] = y
        outs[1][...] = h
        outs[2][...] = h.astype(BF16)

    return _rowwise(name, body, [_full(a) for a in acts] + [_full(h_in)], [*weights, g, b],
                    [(D_MODEL, F32), (D_MODEL, F32), (D_MODEL, BF16)], tr=TM, deps=deps)


def _proj_ln_loss(name, act, w2, h_in, g, b, layer, target):
    def body(rows, consts, outs, accs):
        y = ALPHA * rows[1][...] + _dot(rows[0][...], consts[0][...], NN)
        xh, r = _ln_stats(y)
        gain = consts[1][layer:layer + 1, :]
        err = xh * gain + consts[2][layer:layer + 1, :] - rows[2][...]
        accs[0][...] += jnp.sum(err * err, axis=0, keepdims=True)
        dy = _ln_back(err * (1.0 / D_MODEL), xh, r, gain, accs[1], accs[2])
        outs[0][...] = dy
        outs[1][...] = dy.astype(BF16)

    return _rowwise(name, body, [_full(act), _full(h_in), _full(target)], [w2, g, b], [(D_MODEL, F32), (D_MODEL, BF16)],
                    [((1, D_MODEL), F32)] * 3, tr=TM)


def _dh_ln_back(name, da, w, dy_next, y, g, layer, proj=None, deps=()):
    def body(rows, consts, outs, accs):
        n = consts[0].shape[2]
        acc = ALPHA * rows[1][...]
        for d in range(N_DEV):
            acc = acc + _dot(rows[0][:, d * n:(d + 1) * n], consts[0][d], NT)
        xh, r = _ln_stats(rows[2][...])
        dy = _ln_back(acc, xh, r, consts[1][layer:layer + 1, :], accs[0], accs[1])
        outs[0][...] = dy
        outs[1][...] = dy.astype(BF16)
        if proj is not None:
            outs[2][...] = _dot(dy.astype(BF16), consts[2][...], NT).astype(BF16)

    out_rows = [(D_MODEL, F32), (D_MODEL, BF16)] + ([(proj.shape[0], BF16)] if proj is not None else [])
    return _rowwise(name, body, [_full(da), _full(dy_next), _full(y)], [w, g] + ([proj] if proj is not None else []), out_rows,
                    [((1, D_MODEL), F32)] * 2, tr=TM, deps=deps)


def _relu2_epilogue(acc):
    a = jnp.maximum(acc, 0.0)
    return acc, a * a


def _mlp_up(tag, h_bf, w1):
    T = h_bf.shape[0]
    tm = min(TM, T)
    return _tiled(f"{tag}_ff1", (1, T // tm), [_rb(h_bf, tm), _res(w1)],
                  [_out(T, D_FF, BF16, tm, D_FF), _out(T, D_FF, BF16, tm, D_FF)],
                  _mmc_blocks(N_DEV, NN, lambda w, d: w[d], epilogue=_relu2_epilogue), direct=True)


def _mlp_bwd_w(tag, h_bf, a, act, dff_bf, w2, deps=()):
    T = h_bf.shape[0]
    tm = min(TM, T)
    da = _tiled(f"{tag}_dact", (1, T // tm), [_rb(dff_bf, tm), _res(w2), _rb(a, tm)], [_out(T, D_FF, BF16, tm, D_FF)],
                _mmc_blocks(N_DEV, NT, lambda w, d: w[d], epilogue=lambda acc, a_t: (acc * 2.0 * jnp.maximum(a_t.astype(F32), 0.0),)),
                direct=True, deps=deps)
    dw2 = _tiled(f"{tag}_dw2", (1, D_FF // TM), [_tl(act, TM), _res(dff_bf)],
                 [_out(D_FF, D_MODEL, F32, TM, D_MODEL)], _mmc(TN_)).reshape(N_DEV, D_FF // N_DEV, D_MODEL)
    dw1 = _tiled(f"{tag}_dw1", (N_DEV, 1), [_res(h_bf), _cw(da, TN)], [_out_dev(D_MODEL, TN, D_MODEL)], _mmc(TN_))
    return da, dw1, dw2


def _rope_tables(positions_col, invf_lane):
    def body(rows, consts, outs, accs):
        ang = rows[0][...].astype(F32) * consts[0][...]
        c, s = jnp.cos(ang), jnp.sin(ang)
        lane = lax.broadcasted_iota(jnp.int32, ang.shape, 1)
        outs[0][...] = jnp.where(lane < 64, 1.0, jnp.where(lane < 96, c, 0.0))
        outs[1][...] = jnp.where((lane >= 64) & (lane < 80), -s, 0.0)
        outs[2][...] = jnp.where((lane >= 80) & (lane < 96), s, 0.0)

    return _rowwise("rope_tables", body, [_full(positions_col)], [invf_lane], [(HEAD_W, F32)] * 3)


def _rope(x, c, s1, s2):
    return x * c + pltpu.roll(x, 112, 1) * s1 + pltpu.roll(x, 16, 1) * s2


def _rope_t(dx, c, s1, s2):
    return dx * c + pltpu.roll(dx * s1, 16, 1) + pltpu.roll(dx * s2, 112, 1)


def _rms(c):
    r = lax.rsqrt(jnp.mean(c * c, axis=-1, keepdims=True) + EPS)
    return c * r, r


def _rope_heads(x, c, s1, s2, fn):
    return jnp.concatenate([fn(x[:, h * HEAD_W:(h + 1) * HEAD_W], c, s1, s2) for h in range(HEADS)], axis=1)


def _mla_in(x, wm, ws, tabs, gq, gkv, deps=()):
    def body(rows, consts, outs, accs):
        xb = rows[0][...].astype(BF16)
        zm = _dot(xb, consts[0][...], NN)
        outs[0][...] = zm
        outs[1][...] = _dot(xb, consts[1][...], NN)
        outs[2][...] = (_rms(zm[:, 0:256])[0] * consts[2][...]).astype(BF16)
        outs[3][...] = (_rms(zm[:, 256:512])[0] * consts[3][...]).astype(BF16)
        outs[4][...] = _rope(zm[:, 512:640], rows[1][...], rows[2][...], rows[3][...])

    return _rowwise("l0_in", body, [_full(x)] + [_full(t) for t in tabs], [wm, ws, gq, gkv],
                    [(640, F32), (1024, F32), (256, BF16), (256, BF16), (HEAD_W, F32)], deps=deps)


def _mla_qkv(cqn, ckvn, kr_rot, tabs, wq, wk, wv):
    def body(rows, consts, outs, accs):
        c, s1, s2 = rows[3][...], rows[4][...], rows[5][...]
        outs[0][...] = _rope_heads(_dot(rows[0][...], consts[0][...], NN), c, s1, s2, _rope).astype(BF16)
        outs[1][...] = (_dot(rows[1][...], consts[1][...], NN) + jnp.concatenate([rows[2][...]] * HEADS, axis=1)).astype(BF16)
        outs[2][...] = _dot(rows[1][...], consts[2][...], NN).astype(BF16)

    rows = [_full(cqn), _full(ckvn), _full(kr_rot)] + [_full(t) for t in tabs]
    return _rowwise("l0_qkv", body, rows, [wq, wk, wv], [(HEADS * HEAD_W, BF16)] * 3)


def _mla_back(zm, cqn, ckvn, tabs, gq, gkv, wq, wk, wv, dq, dk, dv):
    def body(rows, consts, outs, accs):
        c, s1, s2 = rows[4][...], rows[5][...], rows[6][...]
        dk_t, dv_bf = rows[8][...], rows[9][...].astype(BF16)
        dq_bf = _rope_heads(rows[7][...], c, s1, s2, _rope_t).astype(BF16)
        dk_bf = dk_t.astype(BF16)
        accs[0][...] += _dot(rows[2][...], dq_bf, TN_)
        accs[1][...] += _dot(rows[3][...], dk_bf, TN_)
        accs[2][...] += _dot(rows[3][...], dv_bf, TN_)
        dlat = [_dot(dq_bf, consts[2][...], NT), _dot(dk_bf, consts[3][...], NT) + _dot(dv_bf, consts[4][...], NT)]
        for k in range(2):
            ch, r = _rms(rows[k][...])
            accs[3 + k][...] += jnp.sum(dlat[k] * ch, axis=0, keepdims=True)
            dc = dlat[k] * consts[k][...]
            outs[0][:, 256 * k:256 * (k + 1)] = (r * (dc - ch * jnp.mean(dc * ch, axis=-1, keepdims=True))).astype(BF16)
        dks = dk_t[:, 0:HEAD_W]
        for h in range(1, HEADS):
            dks = dks + dk_t[:, h * HEAD_W:(h + 1) * HEAD_W]
        lane = lax.broadcasted_iota(jnp.int32, dks.shape, 1)
        dks = jnp.where((lane >= 64) & (lane < 96), dks, 0.0)
        outs[0][:, 512:640] = _rope_t(dks, c, s1, s2).astype(BF16)

    rows = [(zm, 256, 0), (zm, 256, 1), _full(cqn), _full(ckvn)] + [_full(t) for t in tabs] + [_full(dq), _full(dk), _full(dv)]
    wide = HEADS * HEAD_W
    return _rowwise("l0_mla_back", body, rows, [gq, gkv, wq, wk, wv], [(640, BF16)],
                    [((MLA_LORA, wide), F32)] * 3 + [((1, MLA_LORA), F32)] * 2, tr=256)


def _in_back(x, dzm, dzs, dy, wm, ws):
    def body(rows, consts, outs, accs):
        dzm_t, dzs_t = rows[1][...], rows[2][...]
        outs[0][...] = _dot(dzm_t, consts[0][...], NT) + _dot(dzs_t, consts[1][...], NT) + ALPHA * rows[3][...]
        xb = rows[0][...].astype(BF16)
        accs[0][...] += _dot(xb, dzm_t, TN_)
        accs[1][...] += _dot(xb, dzs_t, TN_)

    return _rowwise("l0_in_back", body, [_full(x), _full(dzm), _full(dzs), _full(dy)], [wm, ws], [(D_MODEL, F32)],
                    [((D_MODEL, 640), F32), ((D_MODEL, 1024), F32)])


def _out_weight_grads(o_att, b_out, dy_bf):
    def body(rows, consts, outs, accs):
        d = rows[2][...]
        accs[0][...] += _dot(rows[0][...].astype(BF16), d, TN_)
        accs[1][...] += _dot(rows[1][...], d, TN_)

    return _rowwise("l0_dw_out", body, [_full(o_att), _full(b_out), _full(dy_bf)], [], [],
                    [((HEADS * HEAD_W, D_MODEL), F32), ((SGU_DIM, D_MODEL), F32)])


def _attn_block(T):
    return min(1024, T)


def _attn_fwd(q, k, v):
    T = q.shape[0]
    BQ = _attn_block(T)
    nq = T // BQ

    def kern(q_ref, k_ref, v_ref, o_ref, lse_ref):
        def step(i, j, carry, masked):
            m, l, acc = carry
            qb = q_ref[pl.ds(pl.multiple_of(i * BQ, BQ), BQ), :]
            kb = k_ref[pl.ds(pl.multiple_of(j * BQ, BQ), BQ), :]
            vb = v_ref[pl.ds(pl.multiple_of(j * BQ, BQ), BQ), :]
            s = _dot(qb, kb, NT) * MLA_SCALE
            if masked:
                row = lax.broadcasted_iota(jnp.int32, s.shape, 0)
                col = lax.broadcasted_iota(jnp.int32, s.shape, 1)
                s = jnp.where(col <= row, s, -1e30)
            m_new = jnp.maximum(m, jnp.max(s, axis=-1, keepdims=True))
            p = jnp.exp(s - m_new)
            a = jnp.exp(m - m_new)
            l = a * l + jnp.sum(p, axis=-1, keepdims=True)
            acc = a * acc + _dot(p.astype(BF16), vb, NN)
            return m_new, l, acc

        def qloop(i, _):
            init = (jnp.full((BQ, 1), -1e30, F32), jnp.zeros((BQ, 1), F32), jnp.zeros((BQ, HEAD_W), F32))
            carry = lax.fori_loop(0, i, lambda j, c: step(i, j, c, False), init)
            m, l, acc = step(i, i, carry, True)
            rows = pl.ds(pl.multiple_of(i * BQ, BQ), BQ)
            o_ref[rows, :] = acc / l
            lse_ref[0, rows, :] = m + jnp.log(l)
            return 0

        lax.fori_loop(0, nq, qloop, 0)

    head = pl.BlockSpec((T, HEAD_W), lambda h: (0, h))
    nbytes = 3 * _nbytes((T, HEAD_W), BF16) + _nbytes((T, HEAD_W), F32) + _nbytes((T, 128), F32)
    return pl.pallas_call(
        kern, name="attn_fwd", grid=(HEADS,), in_specs=[head, head, head],
        out_specs=[head, pl.BlockSpec((1, T, 1), lambda h: (h, 0, 0))],
        out_shape=[pltpu.HBM((T, HEADS * HEAD_W), F32), pltpu.HBM((HEADS, T, 1), F32)],
        compiler_params=pltpu.CompilerParams(dimension_semantics=("parallel",), vmem_limit_bytes=_vmem(nbytes)),
    )(_hbm(q), _hbm(k), _hbm(v))


def _attn_bwd(q, k, v, o, lse, dcat, deps=()):
    T = q.shape[0]
    BQ = _attn_block(T)
    nq = T // BQ
    deps = _deps(deps)

    def kern(q_ref, k_ref, v_ref, o_ref, lse_ref, do_ref, *rest):
        dq_ref, dk_ref, dv_ref, dd_ref = rest[len(deps):]
        dq_ref[...] = jnp.zeros(dq_ref.shape, F32)

        def dloop(i, _):
            rows = pl.ds(pl.multiple_of(i * BQ, BQ), BQ)
            dd_ref[rows, :] = jnp.sum(do_ref[rows, :].astype(F32) * o_ref[rows, :], axis=-1, keepdims=True)
            return 0

        lax.fori_loop(0, nq, dloop, 0)

        def step(j, i, carry, masked):
            dk_acc, dv_acc = carry
            rq = pl.ds(pl.multiple_of(i * BQ, BQ), BQ)
            rk = pl.ds(pl.multiple_of(j * BQ, BQ), BQ)
            qb, kb, vb, dob = q_ref[rq, :], k_ref[rk, :], v_ref[rk, :], do_ref[rq, :]
            s = _dot(qb, kb, NT) * MLA_SCALE
            p = jnp.exp(s - lse_ref[0, rq, :])
            if masked:
                row = lax.broadcasted_iota(jnp.int32, s.shape, 0)
                col = lax.broadcasted_iota(jnp.int32, s.shape, 1)
                p = jnp.where(col <= row, p, 0.0)
            dp = _dot(dob, vb, NT)
            ds = (p * (dp - dd_ref[rq, :]) * MLA_SCALE).astype(BF16)
            dv_acc = dv_acc + _dot(p.astype(BF16), dob, TN_)
            dk_acc = dk_acc + _dot(ds, qb, TN_)
            dq_ref[rq, :] += _dot(ds, kb, NN)
            return dk_acc, dv_acc

        def kloop(j, _):
            init = (jnp.zeros((BQ, HEAD_W), F32), jnp.zeros((BQ, HEAD_W), F32))
            carry = step(j, j, init, True)
            dk_acc, dv_acc = lax.fori_loop(j + 1, nq, lambda i, c: step(j, i, c, False), carry)
            rk = pl.ds(pl.multiple_of(j * BQ, BQ), BQ)
            dk_ref[rk, :] = dk_acc
            dv_ref[rk, :] = dv_acc
            return 0

        lax.fori_loop(0, nq, kloop, 0)

    head = pl.BlockSpec((T, HEAD_W), lambda h: (0, h))
    nbytes = 4 * _nbytes((T, HEAD_W), BF16) + 5 * _nbytes((T, HEAD_W), F32) + 2 * _nbytes((T, 128), F32)
    return pl.pallas_call(
        kern, name="attn_bwd", grid=(HEADS,),
        in_specs=[head, head, head, head, pl.BlockSpec((1, T, 1), lambda h: (h, 0, 0)), head] + [ANY_SPEC] * len(deps),
        out_specs=[head, head, head],
        out_shape=[pltpu.HBM((T, HEADS * HEAD_W), F32)] * 3,
        scratch_shapes=[pltpu.VMEM((T, 1), F32)],
        compiler_params=pltpu.CompilerParams(dimension_semantics=("parallel",), vmem_limit_bytes=_vmem(nbytes)),
    )(*[_hbm(a) for a in (q, k, v, o, lse, dcat)], *deps)


def _sgu_common(u, v, ln_g, ln_b):
    ua, tu = _gelu(u)
    va, tv = _gelu(v)
    vh, r = _ln_stats(va)
    return ua, tu, tv, vh, r, vh * ln_g + ln_b


def _tril_mask(n):
    return lax.broadcasted_iota(jnp.int32, (n, n), 1) <= lax.broadcasted_iota(jnp.int32, (n, n), 0)


def _sgu_fwd(zs, ln_g, ln_b, w, bias_full):
    def body(rows, consts, outs, accs):
        ua, _, _, _, _, vn = _sgu_common(rows[0][...], rows[1][...], consts[0][...], consts[1][...])
        vn = vn.astype(BF16)
        tri = _tril_mask(SGU_CHUNK)
        for g in range(SGU_G):
            wg = jnp.where(tri, consts[2][0, g], 0.0).astype(BF16)
            cols = slice(g * 128, (g + 1) * 128)
            for c in range(ua.shape[0] // SGU_CHUNK):
                rws = slice(c * SGU_CHUNK, (c + 1) * SGU_CHUNK)
                mixed = _dot(wg, vn[rws, cols], NN) + consts[3][:, cols]
                outs[0][rws, cols] = (ua[rws, cols] * mixed).astype(BF16)

    return _rowwise("sgu_fwd", body, [(zs, 512, 0), (zs, 512, 1)], [ln_g, ln_b, w, bias_full], [(SGU_DIM, BF16)])


def _sgu_bwd(zs, dcat, ln_g, ln_b, w, bias_full):
    def body(rows, consts, outs, accs):
        u, v = rows[0][...], rows[1][...]
        ua, tu, tv, vh, r, vn = _sgu_common(u, v, consts[0][...], consts[1][...])
        dout = rows[2][...].astype(F32)
        vn_bf = vn.astype(BF16)
        tri = _tril_mask(SGU_CHUNK)
        dmixed = (dout * ua)
        dmixed_bf = dmixed.astype(BF16)
        ones = jnp.ones((8, SGU_CHUNK), F32)
        dvn_cols, mixed_cols = [], []
        for g in range(SGU_G):
            wg = jnp.where(tri, consts[2][0, g], 0.0).astype(BF16)
            cols = slice(g * 128, (g + 1) * 128)
            dvn_rows, mixed_rows = [], []
            dw = jnp.zeros((SGU_CHUNK, SGU_CHUNK), F32)
            dmix_sum = jnp.zeros((SGU_CHUNK, 128), F32)
            for c in range(u.shape[0] // SGU_CHUNK):
                rws = slice(c * SGU_CHUNK, (c + 1) * SGU_CHUNK)
                mixed_rows.append(_dot(wg, vn_bf[rws, cols], NN) + consts[3][:, cols])
                dvn_rows.append(_dot(wg, dmixed_bf[rws, cols], TN_))
                dw = dw + _dot(dmixed_bf[rws, cols], vn_bf[rws, cols], NT)
                dmix_sum = dmix_sum + dmixed[rws, cols]
            accs[0][g] += jnp.where(tri, dw, 0.0)
            accs[3][g:g + 1, :] += _dot(ones, dmix_sum, NT, precision=HIGHEST)[0:1, :]
            dvn_cols.append(jnp.concatenate(dvn_rows, axis=0))
            mixed_cols.append(jnp.concatenate(mixed_rows, axis=0))
        dvn = jnp.concatenate(dvn_cols, axis=1)
        mixed = jnp.concatenate(mixed_cols, axis=1)
        accs[1][...] += jnp.sum(dvn * vh, axis=0, keepdims=True)
        accs[2][...] += jnp.sum(dvn, axis=0, keepdims=True)
        dvh = dvn * consts[0][...]
        dva = r * (dvh - jnp.mean(dvh, axis=-1, keepdims=True) - vh * jnp.mean(dvh * vh, axis=-1, keepdims=True))
        outs[0][:, 0:512] = (dout * mixed * _gelu_grad(u, tu)).astype(BF16)
        outs[0][:, 512:1024] = (dva * _gelu_grad(v, tv)).astype(BF16)

    return _rowwise("sgu_bwd", body, [(zs, 512, 0), (zs, 512, 1), (dcat, 512, 2)], [ln_g, ln_b, w, bias_full], [(1024, BF16)],
                    [((SGU_G, 128, 128), F32), ((1, SGU_DIM), F32), ((1, SGU_DIM), F32), ((SGU_G, 128), F32)], tr=256)


def _lower_bound(hg_lb):
    a0, a1 = hg_lb[0:1, :], hg_lb[1:2, :]
    m = jnp.maximum(a0, a1)
    e0, e1 = jnp.exp(a0 - m), jnp.exp(a1 - m)
    s0, s1 = e0 / (e0 + e1), e1 / (e0 + e1)
    return (s0 + s1) - s0, s0, s1


def _prefix_rows(x, reverse=False):
    n = x.shape[0]
    row = lax.broadcasted_iota(jnp.int32, x.shape, 0)
    s = 1
    while s < n:
        if reverse:
            x = x + jnp.where(row < n - s, pltpu.roll(x, n - s, 0), 0.0)
        else:
            x = x + jnp.where(row >= s, pltpu.roll(x, s, 0), 0.0)
        s *= 2
    return x


def _hg_gates(qr, fr, lb):
    C = qr.shape[0]
    sq = _sig(qr)
    qf = qr * sq
    sf = _sig(fr)
    gate = lb + (1.0 - lb) * sf
    kk = 1.0 - gate
    tri = _tril_mask(C)
    b = _prefix_rows(jnp.log(gate))
    bref = b[C // 2 - 1:C // 2, :]
    bl = b[C - 1:C, :]
    e_b = jnp.exp(b)
    e_q = jnp.exp(b - bref)
    e_k = jnp.exp(bref - b)
    e_lb = jnp.exp(bl - b)
    return dict(sq=sq, qf=qf, sf=sf, gate=gate, kk=kk, tri=tri, bl=bl, e_b=e_b, e_q=e_q, e_k=e_k, e_lb=e_lb)


def _hgrn_fwd(z1, hg_lb, gnorm):
    T = z1.shape[0]
    C = min(HG_CHUNK, T)
    nc = T // C

    def kern(q_ref, f_ref, i_ref, g_ref, lb_ref, gn_ref, o_ref, hg_ref, st_ref, s_scr):
        @pl.when(pl.program_id(0) == 0)
        def _():
            s_scr[...] = jnp.zeros(s_scr.shape, F32)

        lb_all, _, _ = _lower_bound(lb_ref[...])
        st_ref[0] = s_scr[...]
        for h in range(HEADS):
            cols = slice(h * HEAD_W, (h + 1) * HEAD_W)
            t = _hg_gates(q_ref[:, cols], f_ref[:, cols], lb_all[:, cols])
            v = i_ref[:, cols]
            v_bf = v.astype(BF16)
            st = s_scr[h]
            a = jnp.where(t["tri"], _dot((t["qf"] * t["e_q"]).astype(BF16), (t["kk"] * t["e_k"]).astype(BF16), NT), 0.0)
            o = _dot(a.astype(BF16), v_bf, NN) + _dot((t["qf"] * t["e_b"]).astype(BF16), st.astype(BF16), NT)
            s_scr[h] = st * jnp.exp(t["bl"]) + _dot(v_bf, (t["kk"] * t["e_lb"]).astype(BF16), TN_)
            o_ref[:, cols] = o
            gr = g_ref[:, cols]
            r = lax.rsqrt(jnp.mean(o * o, axis=-1, keepdims=True) + EPS)
            hg_ref[:, cols] = (o * r * gn_ref[:, cols] * (gr * _sig(gr))).astype(BF16)

    seg = lambda k: pl.BlockSpec((C, D_MODEL), functools.partial(lambda n, k: (n, k), k=k))
    row = pl.BlockSpec((C, D_MODEL), lambda n: (n, 0))
    nbytes = 6 * _nbytes((C, D_MODEL), F32) + 3 * _nbytes((HEADS, 128, 128), F32)
    return pl.pallas_call(
        kern, name="hgrn_fwd", grid=(nc,),
        in_specs=[seg(0), seg(1), seg(2), seg(3), pl.BlockSpec((2, D_MODEL), lambda n: (0, 0)),
                  pl.BlockSpec((1, D_MODEL), lambda n: (0, 0))],
        out_specs=[row, row, pl.BlockSpec((1, HEADS, 128, 128), lambda n: (n, 0, 0, 0))],
        out_shape=[pltpu.HBM((T, D_MODEL), F32), pltpu.HBM((T, D_MODEL), BF16),
                   pltpu.HBM((nc, HEADS, 128, 128), F32)],
        scratch_shapes=[pltpu.VMEM((HEADS, 128, 128), F32)],
        compiler_params=pltpu.CompilerParams(dimension_semantics=("arbitrary",), vmem_limit_bytes=_vmem(nbytes)),
    )(*[_hbm(a) for a in (z1, z1, z1, z1, hg_lb, gnorm)])


def _hgrn_bwd(z1, o_pre, dhg, states, hg_lb, gnorm):
    T = z1.shape[0]
    C = min(HG_CHUNK, T)
    nc = T // C

    def kern(q_ref, f_ref, i_ref, g_ref, o_ref, dhg_ref, st_ref, lb_ref, gn_ref, dz_ref, dlb_ref, dgn_ref, ds_scr, dlb_scr):
        n = pl.program_id(0)

        @pl.when(n == 0)
        def _():
            ds_scr[...] = jnp.zeros(ds_scr.shape, F32)
            dlb_scr[...] = jnp.zeros(dlb_scr.shape, F32)
            dgn_ref[...] = jnp.zeros(dgn_ref.shape, F32)

        lb_all, s0, s1 = _lower_bound(lb_ref[...])
        for h in range(HEADS):
            cols = slice(h * HEAD_W, (h + 1) * HEAD_W)
            lb = lb_all[:, cols]
            qr, fr = q_ref[:, cols], f_ref[:, cols]
            t = _hg_gates(qr, fr, lb)
            tri = t["tri"]
            v_bf = i_ref[:, cols].astype(BF16)
            st_bf = st_ref[0, h].astype(BF16)
            dst = ds_scr[h]
            dst_bf = dst.astype(BF16)
            o = o_ref[:, cols]
            gr = g_ref[:, cols]
            sg = _sig(gr)
            sil = gr * sg
            gn = gn_ref[:, cols]
            r = lax.rsqrt(jnp.mean(o * o, axis=-1, keepdims=True) + EPS)
            on = o * r
            dh = dhg_ref[:, cols].astype(F32)
            dgn_ref[:, cols] += jnp.sum(dh * on * sil, axis=0, keepdims=True)
            dg = dh * on * gn * (sg * (1.0 + gr * (1.0 - sg)))
            don = dh * gn * sil
            do_bf = (r * (don - on * jnp.mean(don * on, axis=-1, keepdims=True))).astype(BF16)
            qe = (t["qf"] * t["e_q"]).astype(BF16)
            ke = (t["kk"] * t["e_k"]).astype(BF16)
            qb = (t["qf"] * t["e_b"]).astype(BF16)
            kh_bf = (t["kk"] * t["e_lb"]).astype(BF16)
            a_bf = jnp.where(tri, _dot(qe, ke, NT), 0.0).astype(BF16)
            da_bf = jnp.where(tri, _dot(do_bf, v_bf, NT), 0.0).astype(BF16)
            dv = _dot(a_bf, do_bf, TN_) + _dot(kh_bf, dst_bf, NT)
            dqe = _dot(da_bf, ke, NN)
            dqb = _dot(do_bf, st_bf, NN)
            dke = _dot(da_bf, qe, TN_)
            dkh = _dot(v_bf, dst_bf, NN)
            dqf = dqe * t["e_q"] + dqb * t["e_b"]
            dkk = dke * t["e_k"] + dkh * t["e_lb"]
            kh_r = kh_bf.astype(F32)
            db = qe.astype(F32) * dqe - ke.astype(F32) * dke + qb.astype(F32) * dqb - kh_r * dkh
            e_bl = jnp.exp(t["bl"])
            dbl = jnp.sum(dkh * kh_r, axis=0, keepdims=True) + e_bl * jnp.sum(st_ref[0, h] * dst, axis=0, keepdims=True)
            dlg = _prefix_rows(db, reverse=True) + dbl
            ds_scr[h] = dst * e_bl + _dot(do_bf, qb, TN_)
            dgate = dlg / t["gate"] - dkk
            sf = t["sf"]
            dlb_scr[:, cols] += jnp.sum(dgate * (1.0 - sf), axis=0, keepdims=True)
            df = dgate * (1.0 - lb) * sf * (1.0 - sf)
            dq = dqf * (t["sq"] * (1.0 + qr * (1.0 - t["sq"])))
            dz_ref[:, cols] = dq.astype(BF16)
            dz_ref[:, D_MODEL + h * HEAD_W:D_MODEL + (h + 1) * HEAD_W] = df.astype(BF16)
            dz_ref[:, 2 * D_MODEL + h * HEAD_W:2 * D_MODEL + (h + 1) * HEAD_W] = dv.astype(BF16)
            dz_ref[:, 3 * D_MODEL + h * HEAD_W:3 * D_MODEL + (h + 1) * HEAD_W] = dg.astype(BF16)

        @pl.when(n == nc - 1)
        def _():
            d = s0 * s1 * dlb_scr[...]
            dlb_ref[0:1, :] = -d
            dlb_ref[1:2, :] = d

    seg = lambda k: pl.BlockSpec((C, D_MODEL), functools.partial(lambda n, k: (nc - 1 - n, k), k=k))
    nbytes = 6 * _nbytes((C, D_MODEL), F32) + _nbytes((C, 4 * D_MODEL), BF16) + 3 * _nbytes((HEADS, 128, 128), F32)
    return pl.pallas_call(
        kern, name="hgrn_bwd", grid=(nc,),
        in_specs=[seg(0), seg(1), seg(2), seg(3), seg(0), seg(0),
                  pl.BlockSpec((1, HEADS, 128, 128), lambda n: (nc - 1 - n, 0, 0, 0)),
                  pl.BlockSpec((2, D_MODEL), lambda n: (0, 0)), pl.BlockSpec((1, D_MODEL), lambda n: (0, 0))],
        out_specs=[pl.BlockSpec((C, 4 * D_MODEL), lambda n: (nc - 1 - n, 0)),
                   pl.BlockSpec((2, D_MODEL), lambda n: (0, 0)), pl.BlockSpec((1, D_MODEL), lambda n: (0, 0))],
        out_shape=[pltpu.HBM((T, 4 * D_MODEL), BF16), pltpu.HBM((2, D_MODEL), F32),
                   pltpu.HBM((1, D_MODEL), F32)],
        scratch_shapes=[pltpu.VMEM((HEADS, 128, 128), F32), pltpu.VMEM((1, D_MODEL), F32)],
        compiler_params=pltpu.CompilerParams(dimension_semantics=("arbitrary",), vmem_limit_bytes=_vmem(nbytes)),
    )(*[_hbm(a) for a in (z1, z1, z1, z1, o_pre, dhg, states, hg_lb, gnorm)])


def _prep_weights(gw):
    w_in_e = gw["w_in_e"].transpose(1, 0, 2).reshape(D_MODEL, 1568)
    kr = jnp.pad(w_in_e[:, 512:544], ((0, 0), (64, 32)))
    wm = jnp.concatenate([w_in_e[:, 0:512], kr], axis=1)
    ws = w_in_e[:, 544:1568]
    w_qb = gw["w_qb"].transpose(1, 0, 2).reshape(MLA_LORA, HEADS, 96)
    wq = jnp.pad(w_qb, ((0, 0), (0, 0), (0, 32))).reshape(MLA_LORA, HEADS * HEAD_W)
    kvb = gw["w_kvb"].transpose(1, 0, 2).reshape(MLA_LORA, HEADS, 128)
    wk = jnp.pad(kvb[:, :, :64], ((0, 0), (0, 0), (0, 64))).reshape(MLA_LORA, HEADS * HEAD_W)
    wv = jnp.pad(kvb[:, :, 64:], ((0, 0), (0, 0), (0, 64))).reshape(MLA_LORA, HEADS * HEAD_W)
    w_out_e = gw["w_out_e"].reshape(D_MODEL, D_MODEL)
    woa = jnp.pad(w_out_e[:512].reshape(HEADS, 64, D_MODEL), ((0, 0), (0, 64), (0, 0))).reshape(HEADS * HEAD_W, D_MODEL)
    return dict(wm=wm, ws=ws, wq=wq, wk=wk, wv=wv, woa=woa, wob=w_out_e[512:])


def _unprep_grads(g):
    dwm, dws = g["wm"], g["ws"]
    d_in_e = jnp.concatenate([dwm[:, 0:512], dwm[:, 512 + 64:512 + 96], dws], axis=1)
    d_qb = g["wq"].reshape(MLA_LORA, HEADS, HEAD_W)[:, :, :96].reshape(MLA_LORA, HEADS * 96)
    dk = g["wk"].reshape(MLA_LORA, HEADS, HEAD_W)[:, :, :64]
    dv = g["wv"].reshape(MLA_LORA, HEADS, HEAD_W)[:, :, :64]
    d_kvb = jnp.concatenate([dk, dv], axis=2).reshape(MLA_LORA, HEADS * 128)
    d_oa = g["woa"].reshape(HEADS, HEAD_W, D_MODEL)[:, :64].reshape(HEADS * 64, D_MODEL)
    dev_major = lambda a: a.reshape(a.shape[0], N_DEV, a.shape[1] // N_DEV).transpose(1, 0, 2)
    return dict(w_in_e=dev_major(d_in_e), w_qb=dev_major(d_qb), w_kvb=dev_major(d_kvb),
                w_out_e=jnp.concatenate([d_oa, g["wob"]], axis=0).reshape(N_DEV, D_MODEL // N_DEV, D_MODEL))


def _local_step(x, positions, target, gw, sp, ex):
    w = _prep_weights(gw)
    T = x.shape[0]
    tm = min(TM, T)
    nt = T // tm
    half = MLA_ROPE // 2
    inv_freq = ROPE_BASE ** (-jnp.arange(half, dtype=F32) / half)
    invf_lane = jnp.concatenate([jnp.zeros((64,), F32), inv_freq, inv_freq, jnp.zeros((32,), F32)]).reshape(1, HEAD_W)
    tabs = _rope_tables(positions.reshape(T, 1), invf_lane)
    bias_full = jnp.repeat(sp["sgu_b"][0].T, 128, axis=1)
    sgu_w = sp["sgu_w"]
    gq, gkv = sp["mla_gq"], sp["mla_gkv"]
    ln1_g, ln1_b, ln2_g, ln2_b = sp["ln1_g"], sp["ln1_b"], sp["ln2_g"], sp["ln2_b"]
    zm, zs, cqn, ckvn, kr_rot = _mla_in(x, w["wm"], w["ws"], tabs, gq, gkv, deps=[ex.first_token])
    q, k, v = _mla_qkv(cqn, ckvn, kr_rot, tabs, w["wq"], w["wk"], w["wv"])
    o_att, lse = _attn_fwd(q, k, v)
    b_out = _sgu_fwd(zs, sp["sgu_ln_g"], sp["sgu_ln_b"], sgu_w, bias_full)
    token = ex.weights_forward(after=[o_att, b_out])
    y1, h1, h1_bf = _proj_ln("l0_out_ln1", [o_att, b_out], [w["woa"], w["wob"]], x, ln1_g, ln1_b, 0, deps=[token])
    big = ex.weights_ready(after=[y1])
    w_ff1, w_in_o, w_out_o = big["w_ff1"], big["w_in_o"], big["w_out_o"].reshape(D_MODEL, D_MODEL)
    w_ff2 = [a.reshape(D_FF, D_MODEL) for a in big["w_ff2"]]
    a0, act0 = _mlp_up("l0", h1_bf, w_ff1[0])
    y2, h2, h2_bf = _proj_ln("l0_ff2_ln2", [act0], [w_ff2[0]], h1, ln2_g, ln2_b, 0)

    z1 = _tiled("l1_in", (1, nt), [_rb(h2_bf, tm), _res(w_in_o)], [_out(T, 4 * D_MODEL, F32, tm, 4 * D_MODEL)],
                _mmc_blocks(N_DEV, NN, lambda w, d: w[d]), direct=True)
    o_pre, hg, states = _hgrn_fwd(z1, sp["hg_lb"], sp["hg_gnorm"])
    y3, h3, h3_bf = _proj_ln("l1_out_ln1", [hg], [w_out_o], h2, ln1_g, ln1_b, 1)
    a1, act1 = _mlp_up("l1", h3_bf, w_ff1[1])

    gs, g0 = {}, {}
    dy4, dy4_bf, sq_err, gs["ln2_g1"], gs["ln2_b1"] = _proj_ln_loss("l1_ff2_loss", act1, w_ff2[1], h3, ln2_g, ln2_b, 1, target)
    da1, dw1_1, dw2_1 = _mlp_bwd_w("l1", h3_bf, a1, act1, dy4_bf, big["w_ff2"][1])
    dy3, dy3_bf, dhg, gs["ln1_g1"], gs["ln1_b1"] = _dh_ln_back("l1_dh_ln1", da1, w_ff1[1], dy4, y3, ln1_g, 1, proj=w_out_o)
    d_out_o = _tiled("l1_dwout", (2, D_MODEL // TM), [_tl(hg, TM), _cw(dy3_bf, TN)], [_out(D_MODEL, D_MODEL, F32, TM, TN)],
                     _mmc(TN_)).reshape(N_DEV, D_MODEL // N_DEV, D_MODEL)
    dz1, gs["hg_lb"], gs["hg_gnorm"] = _hgrn_bwd(z1, o_pre, dhg, states, sp["hg_lb"], sp["hg_gnorm"])
    d_in_o = _tiled("l1_dwin", (N_DEV, 1), [_res(h2_bf), _cw(dz1, TN)], [_out_dev(D_MODEL, TN, D_MODEL)], _mmc(TN_))
    token = ex.grads_start("l1", [dw1_1, dw2_1, d_in_o, d_out_o])

    dy2, dy2_bf, gs["ln2_g0"], gs["ln2_b0"] = _dh_ln_back("l1_dh_ln2", dz1, w_in_o, dy3, y2, ln2_g, 0, deps=[token])
    token = ex.grads_middle("l1", after=dy2)
    da0, dw1_0, dw2_0 = _mlp_bwd_w("l0", h1_bf, a0, act0, dy2_bf, big["w_ff2"][0], deps=[token])
    token = ex.grads_start("l0m", [dw1_0, dw2_0])
    wo_cat = jnp.concatenate([w["woa"], w["wob"]], axis=0)
    dy1, dy1_bf, dcat, gs["ln1_g0"], gs["ln1_b0"] = _dh_ln_back("l0_dh_ln1", da0, w_ff1[0], dy2, y1, ln1_g, 0, proj=wo_cat, deps=[token])
    ex.grads_end("l1", after=dy1)
    g0["woa"], g0["wob"] = _out_weight_grads(o_att, b_out, dy1_bf)
    token = ex.grads_middle("l0m", after=g0["wob"])
    dzs, gs["sgu_w"], gs["sgu_ln_g"], gs["sgu_ln_b"], gs["sgu_b"] = _sgu_bwd(zs, dcat, sp["sgu_ln_g"], sp["sgu_ln_b"], sgu_w, bias_full)
    dq, dk, dv = _attn_bwd(q, k, v, o_att, lse, dcat, deps=[token])
    ex.grads_end("l0m", after=dq)
    dzm, g0["wq"], g0["wk"], g0["wv"], gs["mla_gq"], gs["mla_gkv"] = _mla_back(zm, cqn, ckvn, tabs, gq, gkv, w["wq"], w["wk"], w["wv"],
                                                                                 dq, dk, dv)
    dx, g0["wm"], g0["ws"] = _in_back(x, dzm, dzs, dy1, w["wm"], w["ws"])

    return sq_err, dx, _unprep_grads(g0), gs


def _me():
    return lax.axis_index("x"), lax.axis_index("y"), lax.axis_index("c")


def _hbm_call(name, kern, operands, out_shape, n_sems, extra_scratch=()):
    any_spec = pl.BlockSpec(memory_space=pl.ANY)
    return pl.pallas_call(
        kern, name=name, out_shape=out_shape, in_specs=[any_spec] * len(operands), out_specs=[any_spec] * len(out_shape),
        scratch_shapes=[pltpu.SemaphoreType.DMA((n_sems,)), pltpu.SemaphoreType.DMA((n_sems,)), *extra_scratch],
    )(*[_hbm(a) for a in operands])


ANY_SPEC = pl.BlockSpec(memory_space=pl.ANY)
HBM_SPEC = pl.BlockSpec(memory_space=pltpu.HBM)
SEM_SPEC = pl.BlockSpec(memory_space=pltpu.SEMAPHORE)
EFFECT = pltpu.SideEffectType.DATAFLOW_SIDE_EFFECTING


def _split_start(name, srcs, lands, n_sems, make_copies, after=()):
    n, m, k = len(srcs), len(lands), len(after)

    def body(*refs):
        for cp in make_copies(refs[:n], refs[n:n + m], refs[n + m + k], refs[n + m + k + 1]):
            cp.start()
        refs[-1][...] = jnp.zeros(refs[-1].shape, F32)

    out_shape = (pltpu.SemaphoreType.DMA((n_sems,)), pltpu.SemaphoreType.DMA((n_sems,)),
                 *[pltpu.HBM(a.shape, a.dtype) for a in (*srcs, *lands)], jax.ShapeDtypeStruct((8, 128), F32))
    res = pl.pallas_call(
        body, name=name, out_shape=out_shape, in_specs=[HBM_SPEC] * (n + m) + [ANY_SPEC] * k,
        out_specs=(SEM_SPEC, SEM_SPEC, *[HBM_SPEC] * (n + m), pl.BlockSpec(memory_space=pltpu.VMEM)),
        input_output_aliases={i: 2 + i for i in range(n + m)},
        compiler_params=pltpu.CompilerParams(has_side_effects=EFFECT),
    )(*[_hbm(a) for a in (*srcs, *lands)], *after)
    return res[0], res[1], list(res[2:2 + n]), list(res[2 + n:2 + n + m]), res[-1]


def _split_wait(name, send_sems, recv_sems, srcs, lands, after, make_copies):
    n, m = len(srcs), len(lands)

    def body(*refs):
        for cp in make_copies(refs[:n], refs[n:n + m], refs[n + m], refs[n + m + 1]):
            cp.wait_send()
            cp.wait_recv()

    res = pl.pallas_call(
        body, name=name, out_shape=tuple(pltpu.HBM(a.shape, a.dtype) for a in (*srcs, *lands)),
        in_specs=[HBM_SPEC] * (n + m) + [SEM_SPEC, SEM_SPEC] + [ANY_SPEC] * len(after), out_specs=tuple([HBM_SPEC] * (n + m)),
        input_output_aliases={i: i for i in range(n + m)},
        compiler_params=pltpu.CompilerParams(has_side_effects=EFFECT),
    )(*srcs, *lands, send_sems, recv_sems, *after)
    return list(res[:n]), list(res[n:])


def _place_own(shards, dev):
    n = len(shards)

    def kern(dev_ref, *refs):
        for x_ref, o_ref in zip(refs[:n], refs[n:]):
            o_ref[...] = x_ref[...].astype(o_ref.dtype)

    blocks = [(None, *a.shape[1:]) for a, _, _ in shards]
    nbytes = sum(_nbytes(b, a.dtype) + _nbytes(b, dt) for b, (a, _, dt) in zip(blocks, shards))
    return pl.pallas_call(
        kern, name="weights_place_own", out_shape=[pltpu.HBM((N_DEV, *a.shape[1:]), dt) for a, _, dt in shards],
        grid_spec=pltpu.PrefetchScalarGridSpec(
            num_scalar_prefetch=1, grid=(1,),
            in_specs=[pl.BlockSpec(b, functools.partial(lambda i, dev, l: (l, 0, 0), l=l)) for b, (_, l, _) in zip(blocks, shards)],
            out_specs=[pl.BlockSpec(b, lambda i, dev: (dev[0], 0, 0)) for b in blocks]),
        compiler_params=pltpu.CompilerParams(dimension_semantics=("arbitrary",), vmem_limit_bytes=_vmem(nbytes)),
    )(dev, *[_hbm(a) for a, _, _ in shards])


def _ag_first_copies(src_refs, out_refs, send_sems, recv_sems):
    x, y, c = _me()
    targets = [(x, y, 1 - c), (1 - x, y, c), (x, 1 - y, c), (1 - x, 1 - y, c)]
    return [pltpu.make_async_remote_copy(
        src_ref=out_refs[op].at[4 * x + 2 * y + c], dst_ref=out_refs[op].at[4 * x + 2 * y + c], send_sem=send_sems.at[4 * op + k],
        recv_sem=recv_sems.at[4 * op + k], device_id=to, device_id_type=MESH)
        for op in range(len(out_refs)) for k, to in enumerate(targets)]


def _ag_second_copies(src_refs, out_refs, send_sems, recv_sems):
    x, y, c = _me()
    chips = [(1 - x, y), (x, 1 - y), (1 - x, 1 - y)]
    return [pltpu.make_async_remote_copy(
        src_ref=out_refs[op].at[4 * cx + 2 * cy + c], dst_ref=out_refs[op].at[4 * cx + 2 * cy + c],
        send_sem=send_sems.at[3 * op + j], recv_sem=recv_sems.at[3 * op + j], device_id=(x, y, 1 - c), device_id_type=MESH)
        for op in range(len(out_refs)) for j, (cx, cy) in enumerate(chips)]


def _rs_sibling_copies(g_refs, out_refs, send_sems, recv_sems):
    x, y, c = _me()
    return [pltpu.make_async_remote_copy(
        src_ref=g_refs[op].at[k, 1 - c], dst_ref=out_refs[op].at[k], send_sem=send_sems.at[4 * op + k],
        recv_sem=recv_sems.at[4 * op + k], device_id=(x, y, 1 - c), device_id_type=MESH)
        for op in range(len(g_refs)) for k in range(4)]


def _rs_chip_copies(p_refs, out_refs, send_sems, recv_sems):
    x, y, c = _me()
    chips = [(1 - x, y), (x, 1 - y), (1 - x, 1 - y)]
    return [pltpu.make_async_remote_copy(
        src_ref=p_refs[op].at[2 * cx + cy], dst_ref=out_refs[op].at[j], send_sem=send_sems.at[3 * op + j],
        recv_sem=recv_sems.at[3 * op + j], device_id=(cx, cy, c), device_id_type=MESH)
        for op in range(len(p_refs)) for j, (cx, cy) in enumerate(chips)]


def _all_gather(placed):
    n = len(placed)

    def kern(*refs):
        in_refs, out_refs, (send_sems, recv_sems) = refs[:n], refs[n:2 * n], refs[2 * n:]
        x, y, c = _me()
        me, sibling = (x, y, c), (x, y, 1 - c)
        chips = [(1 - x, y), (x, 1 - y), (1 - x, 1 - y)]

        def copy(op, k, block, to, own=False):
            idx = 4 * block[0] + 2 * block[1] + block[2]
            return pltpu.make_async_remote_copy(
                src_ref=(in_refs if own else out_refs)[op].at[idx], dst_ref=out_refs[op].at[idx], send_sem=send_sems.at[7 * op + k],
                recv_sem=recv_sems.at[7 * op + k], device_id=to, device_id_type=MESH)

        first = []
        for op in range(n):
            first.append(copy(op, 0, me, sibling, own=True))
            first += [copy(op, 1 + j, me, (*chip, c), own=True) for j, chip in enumerate(chips)]
        for cp in first:
            cp.start()
        passed = []
        for j, chip in enumerate(chips):
            for op in range(n):
                copy(op, 1 + j, (*chip, c), me).wait_recv()
                passed.append(copy(op, 4 + j, (*chip, c), sibling))
                passed[-1].start()
        for op in range(n):
            copy(op, 0, sibling, me).wait_recv()
            for j, chip in enumerate(chips):
                copy(op, 4 + j, (*chip, 1 - c), me).wait_recv()
        for cp in first + passed:
            cp.wait_send()

    return pl.pallas_call(
        kern, name="weights_all_gather", out_shape=[pltpu.HBM(g.shape, g.dtype) for g in placed],
        in_specs=[ANY_SPEC] * n, out_specs=[ANY_SPEC] * n, input_output_aliases={i: i for i in range(n)},
        scratch_shapes=[pltpu.SemaphoreType.DMA((7 * n,)), pltpu.SemaphoreType.DMA((7 * n,))],
    )(*[_hbm(a) for a in placed])


def _rs_sibling(grads):
    n = len(grads)

    def kern(*refs):
        g_refs, out_refs, (send_sems, recv_sems) = refs[:n], refs[n:2 * n], refs[2 * n:]
        x, y, c = _me()
        copies = [pltpu.make_async_remote_copy(
            src_ref=g_refs[op].at[k, 1 - c], dst_ref=out_refs[op].at[k], send_sem=send_sems.at[4 * op + k],
            recv_sem=recv_sems.at[4 * op + k], device_id=(x, y, 1 - c), device_id_type=MESH) for op in range(n) for k in range(4)]
        for cp in copies:
            cp.start()
        for cp in copies:
            cp.wait()

    out_shape = [pltpu.HBM((4, *g.shape[2:]), g.dtype) for g in grads]
    return _hbm_call("grads_to_sibling", kern, grads, out_shape, 4 * n)


def _rs_chips(sums):
    n = len(sums)

    def kern(*refs):
        p_refs, out_refs, (send_sems, recv_sems) = refs[:n], refs[n:2 * n], refs[2 * n:]
        x, y, c = _me()
        chips = [(1 - x, y), (x, 1 - y), (1 - x, 1 - y)]
        copies = [pltpu.make_async_remote_copy(
            src_ref=p_refs[op].at[2 * cx + cy], dst_ref=out_refs[op].at[j], send_sem=send_sems.at[3 * op + j],
            recv_sem=recv_sems.at[3 * op + j], device_id=(cx, cy, c), device_id_type=MESH)
            for op in range(n) for j, (cx, cy) in enumerate(chips)]
        for cp in copies:
            cp.start()
        for cp in copies:
            cp.wait()

    out_shape = [pltpu.HBM((3, *p.shape[1:]), p.dtype) for p in sums]
    return _hbm_call("grads_between_chips", kern, sums, out_shape, 3 * n)


def _row_tile(r, w, n_blocks):
    tr = r
    while tr > 8 and 2 * n_blocks * tr * w * 4 > 24 * 2**20:
        tr //= 2
    return tr


def _chip_sum(name, g, from_sibling, core):
    _, _, R, W = g.shape
    tr = _row_tile(R, W, 3)

    def kern(core_ref, g_ref, s_ref, o_ref):
        o_ref[...] = (g_ref[...] + s_ref[...]).astype(BF16)

    return pl.pallas_call(
        kern, name=name, out_shape=pltpu.HBM((4, R, W), BF16),
        grid_spec=pltpu.PrefetchScalarGridSpec(
            num_scalar_prefetch=1, grid=(4, R // tr),
            in_specs=[pl.BlockSpec((None, None, tr, W), lambda k, i, core: (k, core[0], i, 0)),
                      pl.BlockSpec((None, tr, W), lambda k, i, core: (k, i, 0))],
            out_specs=pl.BlockSpec((None, tr, W), lambda k, i, core: (k, i, 0))),
        compiler_params=pltpu.CompilerParams(dimension_semantics=("parallel", "parallel"), vmem_limit_bytes=_vmem(3 * tr * W * 4)),
    )(core, _hbm(g), _hbm(from_sibling))


def _adamw(w, g, m, v):
    m = ADAM_B1 * m + (1.0 - ADAM_B1) * g
    v = ADAM_B2 * v + (1.0 - ADAM_B2) * (g * g)
    m_hat = m / (1.0 - ADAM_B1 ** ADAM_STEP)
    v_hat = v / (1.0 - ADAM_B2 ** ADAM_STEP)
    return -ADAM_LR * (m_hat / (jnp.sqrt(v_hat) + ADAM_EPS) + ADAM_WD * w), m, v


def _finish_sharded(name, layers, w, m, v, where):
    nl, R, W = w.shape
    tr = _row_tile(R, W, 11 * nl)

    def kern(where_ref, *refs):
        w_ref, m_ref, v_ref, go_ref, d_ref, mo_ref, vo_ref = refs[3 * nl:]
        for l in range(nl):
            g_ref, s_ref, c_ref = refs[3 * l:3 * l + 3]
            grad = g_ref[...] + s_ref[...]
            for j in range(3):
                grad = grad + c_ref[j].astype(F32)
            go_ref[l] = grad
            d_ref[l], mo_ref[l], vo_ref[l] = _adamw(w_ref[l], grad, m_ref[l], v_ref[l])

    row = pl.BlockSpec((nl, tr, W), lambda i, wh: (0, i, 0))
    in_specs, args = [], []
    for g, s, c in layers:
        in_specs += [pl.BlockSpec((None, None, tr, W), lambda i, wh: (wh[0], wh[1], i, 0)),
                     pl.BlockSpec((None, tr, W), lambda i, wh: (wh[0], i, 0)),
                     pl.BlockSpec((3, tr, W), lambda i, wh: (0, i, 0))]
        args += [g, s, c]
    return pl.pallas_call(
        kern, name=name, out_shape=[pltpu.HBM((nl, R, W), F32)] * 4,
        grid_spec=pltpu.PrefetchScalarGridSpec(num_scalar_prefetch=1, grid=(R // tr,), in_specs=in_specs + [row, row, row],
                                               out_specs=[row, row, row, row]),
        compiler_params=pltpu.CompilerParams(dimension_semantics=("parallel",), vmem_limit_bytes=_vmem(nl * 11 * tr * W * 4)),
    )(where, *[_hbm(a) for a in (*args, w, m, v)])


SMALL_PLACE = (("mla_gq", 0, 0, 1, 256), ("mla_gkv", 0, 256, 1, 256), ("sgu_ln_g", 0, 512, 1, 512), ("sgu_ln_b", 1, 0, 1, 512),
               ("hg_lb", 2, 0, 2, 1024), ("ln1_g", 4, 0, 2, 1024), ("ln1_b", 6, 0, 2, 1024), ("sgu_b", 8, 0, 4, 128),
               ("ln2_g", 12, 0, 2, 1024), ("ln2_b", 14, 0, 2, 1024), ("hg_gnorm", 16, 0, 1, 1024))
SMALL_BUF_ROWS = 24


def _small_reduce_adamw(gs, given):
    pieces = [(gs["mla_gq"], 0, 0), (gs["mla_gkv"], 0, 256), (gs["sgu_ln_g"], 0, 512), (gs["sgu_ln_b"], 1, 0), (gs["hg_lb"], 2, 0),
              (gs["ln1_g0"], 4, 0), (gs["ln1_g1"], 5, 0), (gs["ln1_b0"], 6, 0), (gs["ln1_b1"], 7, 0), (gs["sgu_b"], 8, 0),
              (gs["ln2_g0"], 12, 0), (gs["ln2_g1"], 13, 0), (gs["ln2_b0"], 14, 0), (gs["ln2_b1"], 15, 0), (gs["hg_gnorm"], 16, 0)]
    names = [p[0] for p in SMALL_PLACE] + ["sgu_w"]
    n_p, n_names = len(pieces), len(names)
    wmv = [given[pre + name] for name in names for pre in ("", "m_", "v_")]
    vmem = pl.BlockSpec(memory_space=pltpu.VMEM)

    def reduce_kern(*refs):
        piece_refs, gw_ref, sum_a_ref, sum_b_ref = refs[:n_p], refs[n_p], refs[n_p + 1], refs[n_p + 2]
        buf_a, buf_b, send_sems, recv_sems = refs[n_p + 3:]
        px, py, pc = _me()
        me = 4 * px + 2 * py + pc
        mine_a, mine_b = buf_a.at[me], buf_b.at[me]
        mine_a[...] = jnp.zeros(mine_a.shape, F32)
        for ref, (_, r, l0) in zip(piece_refs, pieces):
            mine_a[r:r + ref.shape[0], l0:l0 + ref.shape[1]] = ref[...]
        mine_b[...] = gw_ref[...]
        copies = []
        for r in range(1, N_DEV):
            peer = (px ^ (r >> 2), py ^ ((r >> 1) & 1), pc ^ (r & 1))
            for k, mine in enumerate((mine_a, mine_b)):
                copies.append(pltpu.make_async_remote_copy(
                    src_ref=mine, dst_ref=mine, send_sem=send_sems.at[2 * (r - 1) + k], recv_sem=recv_sems.at[2 * (r - 1) + k],
                    device_id=peer, device_id_type=MESH))
        for cp in copies:
            cp.start()
        for r in range(1, N_DEV):
            for k, buf in enumerate((buf_a, buf_b)):
                theirs = buf.at[me ^ r]
                pltpu.make_async_remote_copy(
                    src_ref=theirs, dst_ref=theirs, send_sem=send_sems.at[2 * (r - 1) + k], recv_sem=recv_sems.at[2 * (r - 1) + k],
                    device_id=(px, py, pc), device_id_type=MESH).wait_recv()
        for cp in copies:
            cp.wait_send()
        sum_a, sum_b = buf_a[0], buf_b[0]
        for d in range(1, N_DEV):
            sum_a, sum_b = sum_a + buf_a[d], sum_b + buf_b[d]
        sum_a_ref[...] = sum_a
        sum_b_ref[...] = sum_b

    sum_a, sum_b = pl.pallas_call(
        reduce_kern, name="small_all_reduce", in_specs=[vmem] * (n_p + 1), out_specs=[vmem, vmem],
        out_shape=[jax.ShapeDtypeStruct((SMALL_BUF_ROWS, D_MODEL), F32), jax.ShapeDtypeStruct((SGU_G, 128, 128), F32)],
        scratch_shapes=[pltpu.VMEM((N_DEV, SMALL_BUF_ROWS, D_MODEL), F32), pltpu.VMEM((N_DEV, SGU_G, 128, 128), F32),
                        pltpu.SemaphoreType.DMA((14,)), pltpu.SemaphoreType.DMA((14,))],
    )(*[p[0] for p in pieces], gs["sgu_w"])

    def kern(*refs):
        sum_a, sum_b = refs[0][...], refs[1][...]
        wmv_refs, out_refs = refs[2:2 + 3 * n_names], refs[2 + 3 * n_names:]
        px, py, pc = _me()
        me = 4 * px + 2 * py + pc

        def own_block(full):
            acc = full[:, 0:128]
            for b in range(1, N_DEV):
                acc = jnp.where(me == b, full[:, b * 128:(b + 1) * 128], acc)
            return acc

        for idx, name in enumerate(names):
            w_ref, m_ref, v_ref = wmv_refs[3 * idx:3 * idx + 3]
            if name == "sgu_w":
                grad = sum_b[None]
            else:
                _, r, l0, nr, nl = SMALL_PLACE[idx]
                grad = sum_a[r:r + nr, l0:l0 + nl]
                if name == "hg_gnorm":
                    grad = own_block(grad)
                if name == "sgu_b":
                    grad = grad[None]
            res = (grad, *_adamw(w_ref[...], grad, m_ref[...], v_ref[...]))
            for o_ref, val in zip(out_refs[4 * idx:4 * idx + 4], res):
                o_ref[...] = val

    out_shape = [jax.ShapeDtypeStruct(given[name].shape, F32) for name in names for _ in range(4)]
    res = pl.pallas_call(
        kern, name="small_adamw", out_shape=out_shape, in_specs=[vmem] * (2 + len(wmv)), out_specs=[vmem] * len(out_shape),
    )(sum_a, sum_b, *wmv)
    return {name: res[4 * idx:4 * idx + 4] for idx, name in enumerate(names)}


class _Exchange:
    def __init__(self, given):
        self.given = given
        px, py, pc = _me()
        self.core = pc.reshape(1).astype(jnp.int32)
        self.dev = (4 * px + 2 * py + pc).reshape(1).astype(jnp.int32)
        self.where = jnp.stack([2 * px + py, pc]).astype(jnp.int32)
        self.state, self.layers = {}, {}

    def start_weights(self, lands, after):
        self.weights = _split_start("weights_first_start", [], lands, 4 * len(lands), _ag_first_copies, after=after)
        self.first_token = self.weights[4]

    def weights_forward(self, after):
        send_sems, recv_sems, shards, lands, _ = self.weights
        _, lands = _split_wait("weights_first_wait", send_sems, recv_sems, shards, lands, after, _ag_first_copies)
        self.weights = _split_start("weights_second_start", [], lands, 3 * len(lands), _ag_second_copies)
        return self.weights[4]

    def weights_ready(self, after):
        send_sems, recv_sems, shards, lands, _ = self.weights
        _, got = _split_wait("weights_second_wait", send_sems, recv_sems, shards, lands, after, _ag_second_copies)
        return dict(w_in_o=got[0], w_out_o=got[1], w_ff1=[got[2], got[3]], w_ff2=[got[4], got[5]])

    def grads_start(self, tag, grads):
        blocks = [g.reshape(4, 2, *g.shape[1:]) for g in grads]
        lands = [lax.empty((4, *b.shape[2:]), F32) for b in blocks]
        self.state[tag] = _split_start(f"grads_{tag}_sibling_start", blocks, lands, 4 * len(blocks), _rs_sibling_copies)
        return self.state[tag][4]

    def grads_middle(self, tag, after):
        send_sems, recv_sems, blocks, lands, _ = self.state[tag]
        blocks, from_sibling = _split_wait(f"grads_{tag}_sibling_wait", send_sems, recv_sems, blocks, lands, [after], _rs_sibling_copies)
        sums = [_chip_sum(f"grads_{tag}_chip_sum_{k}", b, s, self.core) for k, (b, s) in enumerate(zip(blocks, from_sibling))]
        lands = [lax.empty((3, *p.shape[1:]), BF16) for p in sums]
        self.state[tag] = (blocks, from_sibling, _split_start(f"grads_{tag}_chips_start", sums, lands, 3 * len(sums), _rs_chip_copies))
        return self.state[tag][2][4]

    def grads_end(self, tag, after):
        blocks, from_sibling, (send_sems, recv_sems, sums, lands, _) = self.state[tag]
        _, from_chips = _split_wait(f"grads_{tag}_chips_wait", send_sems, recv_sems, sums, lands, [after], _rs_chip_copies)
        self.layers[tag] = list(zip(blocks, from_sibling, from_chips))


SHARDED = ("w_in_e", "w_qb", "w_kvb", "w_out_e", "w_in_o", "w_out_o", "w_ff1", "w_ff2")


def kernel(x, positions, w_in_e, mla_gq, mla_gkv, w_qb, w_kvb, sgu_ln_g, sgu_ln_b, sgu_w, sgu_b, w_out_e, w_in_o, hg_lb, hg_gnorm, w_out_o, ln1_g, ln1_b, w_ff1, w_ff2, ln2_g, ln2_b, loss_target, m_w_in_e, m_mla_gq, m_mla_gkv, m_w_qb, m_w_kvb, m_sgu_ln_g, m_sgu_ln_b, m_sgu_w, m_sgu_b, m_w_out_e, m_w_in_o, m_hg_lb, m_hg_gnorm, m_w_out_o, m_ln1_g, m_ln1_b, m_w_ff1, m_w_ff2, m_ln2_g, m_ln2_b, v_w_in_e, v_mla_gq, v_mla_gkv, v_w_qb, v_w_kvb, v_sgu_ln_g, v_sgu_ln_b, v_sgu_w, v_sgu_b, v_w_out_e, v_w_in_o, v_hg_lb, v_hg_gnorm, v_w_out_o, v_ln1_g, v_ln1_b, v_w_ff1, v_w_ff2, v_ln2_g, v_ln2_b):
    given = dict(locals())
    ex = _Exchange(given)

    names = ["w_in_e", "w_qb", "w_kvb", "w_out_e"]
    placed = _place_own([(given[n], 0, BF16) for n in names] + [(hg_gnorm.reshape(1, 1, D_MODEL // N_DEV), 0, F32)]
                        + [(w_in_o, 0, BF16), (w_out_o, 0, BF16), (w_ff1, 0, BF16), (w_ff1, 1, BF16), (w_ff2, 0, BF16), (w_ff2, 1, BF16)],
                        ex.dev)
    got = _all_gather(placed[:5])
    ex.start_weights(placed[5:], after=[got[0]])
    gw = dict(zip(names, got[:4]))
    small_names = ["mla_gq", "mla_gkv", "sgu_ln_g", "sgu_ln_b", "sgu_w", "sgu_b", "hg_lb", "ln1_g", "ln1_b", "ln2_g", "ln2_b"]
    sp = {n: given[n] for n in small_names}
    sp["hg_gnorm"] = got[4].reshape(1, D_MODEL)

    sq_err, dx, grads, gs = _local_step(x[0], positions[0], loss_target[0], gw, sp, ex)
    loss = lax.psum(0.5 * jnp.sum(sq_err) / D_MODEL, ("x", "y", "c"))

    blocks = [grads[n].reshape(4, 2, *grads[n].shape[1:]) for n in names]
    from_sibling = _rs_sibling(blocks)
    chip_sums = [_chip_sum(f"grads_l0_chip_sum_{k}", b, s, ex.core) for k, (b, s) in enumerate(zip(blocks, from_sibling))]
    from_chips = _rs_chips(chip_sums)
    per_weight = dict(zip(names, [[l] for l in zip(blocks, from_sibling, from_chips)]))
    l1, l0m = ex.layers["l1"], ex.layers["l0m"]
    per_weight.update(w_ff1=[l0m[0], l1[0]], w_ff2=[l0m[1], l1[1]], w_in_o=[l1[2]], w_out_o=[l1[3]])
    results = {n: _finish_sharded(f"finish_{n}", per_weight[n], given[n], given["m_" + n], given["v_" + n], ex.where) for n in SHARDED}

    results.update(_small_reduce_adamw(gs, given))

    order = ["w_in_e", "mla_gq", "mla_gkv", "w_qb", "w_kvb", "sgu_ln_g", "sgu_ln_b", "sgu_w", "sgu_b", "w_out_e", "w_in_o",
             "hg_lb", "hg_gnorm", "w_out_o", "ln1_g", "ln1_b", "w_ff1", "w_ff2", "ln2_g", "ln2_b"]
    return (loss, dx[None], *[results[name][kind] for kind in range(4) for name in order])
```

```python
import functools
import math

import jax
import jax.numpy as jnp
import numpy as np
from jax import lax
from jax.experimental import pallas as pl
from jax.experimental.pallas import tpu as pltpu

F32 = jnp.float32
BF16 = jnp.bfloat16
MESH = pl.DeviceIdType.MESH
HIGHEST = lax.Precision.HIGHEST

D_MODEL = 1024
D_FF = 4096
N_DEV = 8
HEADS = 8
HEAD_W = 128
MLA_NOPE = 64
MLA_ROPE = 32
MLA_V = 64
MLA_LORA = 256
MLA_SCALE = (MLA_NOPE + MLA_ROPE) ** -0.5
ROPE_BASE = 10000.0
SGU_DIM = 512
SGU_G = 4
SGU_CHUNK = 128
HG_CHUNK = 64
HG_CHUNKS_PER_STEP = 4
ALPHA = (2 * 2) ** 0.25
EPS = 1e-5
ADAM_LR, ADAM_B1, ADAM_B2, ADAM_EPS, ADAM_WD, ADAM_STEP = 0.001, 0.9, 0.999, 1e-08, 0.01, 10

VMEM_CAP_V7X = 56 * 2**20
VMEM_SLACK = 12 * 2**20
TM = 512
TN = 512


def _vmem(block_bytes):
    return int(min(VMEM_CAP_V7X, 2 * block_bytes + VMEM_SLACK))


def _hbm(a):
    return pltpu.with_memory_space_constraint(a, pltpu.HBM)


def _nbytes(shape, dtype):
    return int(np.prod([d for d in shape if d is not None])) * jnp.dtype(dtype).itemsize


def _sig(x):
    return 1.0 / (1.0 + jnp.exp(-x))


def _gelu(x):
    c = math.sqrt(2.0 / math.pi)
    t = jnp.tanh(c * (x + 0.044715 * x * x * x))
    return 0.5 * x * (1.0 + t), t


def _gelu_grad(x, t):
    c = math.sqrt(2.0 / math.pi)
    return 0.5 * (1.0 + t) + 0.5 * x * (1.0 - t * t) * c * (1.0 + 3 * 0.044715 * x * x)


def _dot(a, b, dims, precision=None):
    return lax.dot_general(a, b, (dims, ((), ())), preferred_element_type=F32, precision=precision)


NN = ((1,), (0,))
NT = ((1,), (1,))
TN_ = ((0,), (0,))


def _deps(deps):
    return [d for d in deps if d is not None]


def _tiled(name, grid, ins, outs, compute, direct=False, deps=()):
    n_in, deps = len(ins), _deps(deps)
    n_skip = n_in + len(deps)

    def kern(*refs):
        if direct:
            compute(refs[:n_in], refs[n_skip:])
            return
        for o_ref, r in zip(refs[n_skip:], compute(*refs[:n_in])):
            o_ref[...] = r.astype(o_ref.dtype).reshape(o_ref.shape)

    swap = lambda f: (lambda j, i: f(i, j))
    nbytes = sum(_nbytes(blk, a.dtype) for a, blk, _ in ins) + sum(_nbytes(blk, dt) + _nbytes(blk, F32) for _, dt, blk, _ in outs)
    res = pl.pallas_call(
        kern, name=name, grid=grid,
        in_specs=[pl.BlockSpec(blk, swap(f), pipeline_mode=pl.Buffered(1) if tuple(blk) == tuple(a.shape) else None)
                  for a, blk, f in ins] + [ANY_SPEC] * len(deps),
        out_specs=[pl.BlockSpec(blk, swap(f)) for _, _, blk, f in outs],
        out_shape=[pltpu.HBM(shape, dt) for shape, dt, _, _ in outs],
        compiler_params=pltpu.CompilerParams(dimension_semantics=("parallel", "parallel"), vmem_limit_bytes=_vmem(nbytes)),
    )(*[_hbm(a) for a, _, _ in ins], *deps)
    return res if len(res) > 1 else res[0]


def _rb(a, tm, w=None, cb=0):
    return (a, (tm, a.shape[1] if w is None else w), lambda i, j: (i, cb))


def _rbj(a, tm, tn):
    return (a, (tm, tn), lambda i, j: (i, j))


def _cw(b, tn):
    return (b, (b.shape[0], tn), lambda i, j: (0, j))


def _rw(b, tn):
    return (b, (tn, b.shape[1]), lambda i, j: (j, 0))


def _tl(a, tm):
    return (a, (a.shape[0], tm), lambda i, j: (0, i))


def _gcw(g):
    return (g, (None, g.shape[1], g.shape[2]), lambda i, j: (j, 0, 0))


def _grw(g, tn):
    return (g, (N_DEV, tn, g.shape[2]), lambda i, j: (0, j, 0))


def _out(m, n, dtype, tm, tn):
    return ((m, n), dtype, (tm, tn), lambda i, j: (i, j))


def _out_dev(k, n, tm):
    return ((N_DEV, k, n), F32, (None, tm, n), lambda i, j: (j, i, 0))


def _mmc(dims, n_pairs=1, epilogue=None):
    def compute(*refs):
        acc = None
        for k in range(n_pairs):
            d = _dot(refs[2 * k][...].astype(BF16), refs[2 * k + 1][...].astype(BF16), dims)
            acc = d if acc is None else acc + d
        ext = [r[...] for r in refs[2 * n_pairs:]]
        return epilogue(acc, *ext) if epilogue is not None else (acc,)

    return compute


def _res(w):
    return (w, w.shape, functools.partial(lambda i, j, nd: (0,) * nd, nd=w.ndim))


def _mmc_blocks(nblk, dims, rhs_block, epilogue=None):
    def compute(in_refs, out_refs):
        a = in_refs[0][...].astype(BF16)
        for d in range(nblk):
            acc = _dot(a, rhs_block(in_refs[1], d).astype(BF16), dims)
            n = acc.shape[1]
            ext = [r[:, d * n:(d + 1) * n] for r in in_refs[2:]]
            res = epilogue(acc, *ext) if epilogue is not None else (acc,)
            for o_ref, r in zip(out_refs, res):
                o_ref[:, d * n:(d + 1) * n] = r.astype(o_ref.dtype)

    return compute


def _mmc_dev(epilogue=None):
    def compute(a_ref, b_ref, *ext_refs):
        n = b_ref.shape[2]
        acc = None
        for d in range(N_DEV):
            t = _dot(a_ref[:, d * n:(d + 1) * n].astype(BF16), b_ref[d].astype(BF16), NT)
            acc = t if acc is None else acc + t
        ext = [r[...] for r in ext_refs]
        return epilogue(acc, *ext) if epilogue is not None else (acc,)

    return compute


def _rowwise(name, body, rows, consts, out_rows, out_accs=(), tr=512, deps=()):
    T = rows[0][0].shape[0]
    tr = min(tr, T)
    deps = _deps(deps)
    nr, ncn, no, nd = len(rows), len(consts), len(out_rows), len(deps)

    def kern(*refs):
        accs = refs[nr + ncn + nd + no:]
        if accs:
            @pl.when(pl.program_id(0) == 0)
            def _():
                for a in accs:
                    a[...] = jnp.zeros(a.shape, a.dtype)
        body(refs[:nr], refs[nr:nr + ncn], refs[nr + ncn + nd:nr + ncn + nd + no], accs)

    in_specs = [pl.BlockSpec((tr, w), functools.partial(lambda i, cb: (i, cb), cb=cb)) for _, w, cb in rows]
    in_specs += [pl.BlockSpec(c.shape, functools.partial(lambda i, nd: (0,) * nd, nd=c.ndim), pipeline_mode=pl.Buffered(1))
                 for c in consts]
    in_specs += [ANY_SPEC] * nd
    out_specs = [pl.BlockSpec((tr, w), lambda i: (i, 0)) for w, _ in out_rows]
    out_specs += [pl.BlockSpec(s, functools.partial(lambda i, nd: (0,) * nd, nd=len(s))) for s, _ in out_accs]
    out_shape = [pltpu.HBM((T, w), dt) for w, dt in out_rows]
    out_shape += [pltpu.HBM(s, dt) for s, dt in out_accs]
    nbytes = sum(_nbytes((tr, w), a.dtype) for a, w, _ in rows) + sum(_nbytes(c.shape, c.dtype) for c in consts)
    nbytes += sum(_nbytes((tr, w), dt) for w, dt in out_rows) + sum(_nbytes(s, dt) for s, dt in out_accs)
    res = pl.pallas_call(
        kern, name=name, grid=(T // tr,), in_specs=in_specs, out_specs=out_specs, out_shape=out_shape,
        compiler_params=pltpu.CompilerParams(dimension_semantics=("arbitrary",), vmem_limit_bytes=_vmem(nbytes)),
    )(*[_hbm(a) for a, _, _ in rows], *[_hbm(c) for c in consts], *deps)
    return res if len(res) > 1 else res[0]


def _full(a):
    return (a, a.shape[1], 0)


def _ln_stats(y):
    mu = jnp.mean(y, axis=-1, keepdims=True)
    yc = y - mu
    r = lax.rsqrt(jnp.mean(yc * yc, axis=-1, keepdims=True) + EPS)
    return yc * r, r


def _ln_back(dh, xh, r, gain, dg_ref, db_ref):
    dg_ref[...] += jnp.sum(dh * xh, axis=0, keepdims=True)
    db_ref[...] += jnp.sum(dh, axis=0, keepdims=True)
    dx = dh * gain
    return r * (dx - jnp.mean(dx, axis=-1, keepdims=True) - xh * jnp.mean(dx * xh, axis=-1, keepdims=True))


def _proj_ln(name, acts, weights, h_in, g, b, layer, deps=()):
    n = len(acts)

    def body(rows, consts, outs, accs):
        acc = None
        for k in range(n):
            d = _dot(rows[k][...].astype(BF16), consts[k][...], NN)
            acc = d if acc is None else acc + d
        y = ALPHA * rows[n][...] + acc
        xh, _ = _ln_stats(y)
        h = xh * consts[n][layer:layer + 1, :] + consts[n + 1][layer:layer + 1, :]
        outs[0][...] = y
        outs[1][...] = h
        outs[2][...] = h.astype(BF16)

    return _rowwise(name, body, [_full(a) for a in acts] + [_full(h_in)], [*weights, g, b],
                    [(D_MODEL, F32), (D_MODEL, F32), (D_MODEL, BF16)], tr=TM, deps=deps)


def _proj_ln_loss(name, act, w2, h_in, g, b, layer, target):
    def body(rows, consts, outs, accs):
        y = ALPHA * rows[1][...] + _dot(rows[0][...], consts[0][...], NN)
        xh, r = _ln_stats(y)
        gain = consts[1][layer:layer + 1, :]
        err = xh * gain + consts[2][layer:layer + 1, :] - rows[2][...]
        accs[0][...] += jnp.sum(err * err, axis=0, keepdims=True)
        dy = _ln_back(err * (1.0 / D_MODEL), xh, r, gain, accs[1], accs[2])
        outs[0][...] = dy
        outs[1][...] = dy.astype(BF16)

    return _rowwise(name, body, [_full(act), _full(h_in), _full(target)], [w2, g, b], [(D_MODEL, F32), (D_MODEL, BF16)],
                    [((1, D_MODEL), F32)] * 3, tr=TM)


def _dh_ln_back(name, da, w, dy_next, y, g, layer, proj=(), deps=()):
    def body(rows, consts, outs, accs):
        n = consts[0].shape[2]
        acc = ALPHA * rows[1][...]
        for d in range(N_DEV):
            acc = acc + _dot(rows[0][:, d * n:(d + 1) * n], consts[0][d], NT)
        xh, r = _ln_stats(rows[2][...])
        dy = _ln_back(acc, xh, r, consts[1][layer:layer + 1, :], accs[0], accs[1])
        outs[0][...] = dy
        dy_bf = dy.astype(BF16)
        outs[1][...] = dy_bf
        off = 0
        for k, p in enumerate(proj):
            outs[2][:, off:off + p.shape[0]] = _dot(dy_bf, consts[2 + k][...], NT).astype(BF16)
            off += p.shape[0]

    out_rows = [(D_MODEL, F32), (D_MODEL, BF16)] + ([(sum(p.shape[0] for p in proj), BF16)] if proj else [])
    return _rowwise(name, body, [_full(da), _full(dy_next), _full(y)], [w, g, *proj], out_rows,
                    [((1, D_MODEL), F32)] * 2, tr=TM, deps=deps)


def _relu2_epilogue(acc):
    a = jnp.maximum(acc, 0.0)
    return acc, a * a


def _mlp_up(tag, h_bf, w1):
    T = h_bf.shape[0]
    tm = min(TM, T)
    return _tiled(f"{tag}_ff1", (1, T // tm), [_rb(h_bf, tm), _res(w1)],
                  [_out(T, D_FF, BF16, tm, D_FF), _out(T, D_FF, BF16, tm, D_FF)],
                  _mmc_blocks(N_DEV, NN, lambda w, d: w[d], epilogue=_relu2_epilogue), direct=True)


def _mlp_bwd_w(tag, h_bf, a, act, dff_bf, w2, deps=()):
    T = h_bf.shape[0]
    tm = min(TM, T)
    da = _tiled(f"{tag}_dact", (1, T // tm), [_rb(dff_bf, tm), _res(w2), _rb(a, tm)], [_out(T, D_FF, BF16, tm, D_FF)],
                _mmc_blocks(N_DEV, NT, lambda w, d: w[d], epilogue=lambda acc, a_t: (acc * 2.0 * jnp.maximum(a_t.astype(F32), 0.0),)),
                direct=True, deps=deps)
    dw2 = _tiled(f"{tag}_dw2", (1, D_FF // TM), [_tl(act, TM), _res(dff_bf)],
                 [_out(D_FF, D_MODEL, F32, TM, D_MODEL)], _mmc(TN_)).reshape(N_DEV, D_FF // N_DEV, D_MODEL)
    dw1 = _tiled(f"{tag}_dw1", (N_DEV, 1), [_res(h_bf), _cw(da, TN)], [_out_dev(D_MODEL, TN, D_MODEL)], _mmc(TN_))
    return da, dw1, dw2


def _rope_tables(positions_col, invf_lane):
    def body(rows, consts, outs, accs):
        ang = rows[0][...].astype(F32) * consts[0][...]
        c, s = jnp.cos(ang), jnp.sin(ang)
        lane = lax.broadcasted_iota(jnp.int32, ang.shape, 1)
        outs[0][...] = jnp.where(lane < 64, 1.0, jnp.where(lane < 96, c, 0.0))
        outs[1][...] = jnp.where((lane >= 64) & (lane < 80), -s, 0.0)
        outs[2][...] = jnp.where((lane >= 80) & (lane < 96), s, 0.0)

    return _rowwise("rope_tables", body, [_full(positions_col)], [invf_lane], [(HEAD_W, F32)] * 3)


def _rope(x, c, s1, s2):
    return x * c + pltpu.roll(x, 112, 1) * s1 + pltpu.roll(x, 16, 1) * s2


def _rope_t(dx, c, s1, s2):
    return dx * c + pltpu.roll(dx * s1, 16, 1) + pltpu.roll(dx * s2, 112, 1)


def _rms(c):
    r = lax.rsqrt(jnp.mean(c * c, axis=-1, keepdims=True) + EPS)
    return c * r, r


def _rope_heads(x, c, s1, s2, fn):
    return jnp.concatenate([fn(x[:, h * HEAD_W:(h + 1) * HEAD_W], c, s1, s2) for h in range(HEADS)], axis=1)


def _mla_in(x, wm, ws, tabs, gq, gkv, deps=()):
    def body(rows, consts, outs, accs):
        xb = rows[0][...].astype(BF16)
        zm = _dot(xb, consts[0][...], NN)
        outs[0][...] = zm
        outs[1][...] = _dot(xb, consts[1][...], NN)
        outs[2][...] = (_rms(zm[:, 0:256])[0] * consts[2][...]).astype(BF16)
        outs[3][...] = (_rms(zm[:, 256:512])[0] * consts[3][...]).astype(BF16)
        outs[4][...] = _rope(zm[:, 512:640], rows[1][...], rows[2][...], rows[3][...])

    return _rowwise("l0_in", body, [_full(x)] + [_full(t) for t in tabs], [wm, ws, gq, gkv],
                    [(640, F32), (1024, F32), (256, BF16), (256, BF16), (HEAD_W, F32)], deps=deps)


def _mla_qkv(cqn, ckvn, kr_rot, tabs, wq, wk, wv):
    def body(rows, consts, outs, accs):
        c, s1, s2 = rows[3][...], rows[4][...], rows[5][...]
        outs[0][...] = _rope_heads(_dot(rows[0][...], consts[0][...], NN), c, s1, s2, _rope).astype(BF16)
        outs[1][...] = (_dot(rows[1][...], consts[1][...], NN) + jnp.concatenate([rows[2][...]] * HEADS, axis=1)).astype(BF16)
        outs[2][...] = _dot(rows[1][...], consts[2][...], NN).astype(BF16)

    rows = [_full(cqn), _full(ckvn), _full(kr_rot)] + [_full(t) for t in tabs]
    return _rowwise("l0_qkv", body, rows, [wq, wk, wv], [(HEADS * HEAD_W, BF16)] * 3)


def _mla_back(zm, cqn, ckvn, tabs, gq, gkv, wq, wk, wv, dq, dk, dv):
    def body(rows, consts, outs, accs):
        c, s1, s2 = rows[4][...], rows[5][...], rows[6][...]
        dk_t, dv_bf = rows[8][...], rows[9][...].astype(BF16)
        dq_bf = _rope_heads(rows[7][...], c, s1, s2, _rope_t).astype(BF16)
        dk_bf = dk_t.astype(BF16)
        accs[0][...] += _dot(rows[2][...], dq_bf, TN_)
        accs[1][...] += _dot(rows[3][...], dk_bf, TN_)
        accs[2][...] += _dot(rows[3][...], dv_bf, TN_)
        dlat = [_dot(dq_bf, consts[2][...], NT), _dot(dk_bf, consts[3][...], NT) + _dot(dv_bf, consts[4][...], NT)]
        for k in range(2):
            ch, r = _rms(rows[k][...])
            accs[3 + k][...] += jnp.sum(dlat[k] * ch, axis=0, keepdims=True)
            dc = dlat[k] * consts[k][...]
            outs[0][:, 256 * k:256 * (k + 1)] = (r * (dc - ch * jnp.mean(dc * ch, axis=-1, keepdims=True))).astype(BF16)
        dks = dk_t[:, 0:HEAD_W]
        for h in range(1, HEADS):
            dks = dks + dk_t[:, h * HEAD_W:(h + 1) * HEAD_W]
        lane = lax.broadcasted_iota(jnp.int32, dks.shape, 1)
        dks = jnp.where((lane >= 64) & (lane < 96), dks, 0.0)
        outs[0][:, 512:640] = _rope_t(dks, c, s1, s2).astype(BF16)

    rows = [(zm, 256, 0), (zm, 256, 1), _full(cqn), _full(ckvn)] + [_full(t) for t in tabs] + [_full(dq), _full(dk), _full(dv)]
    wide = HEADS * HEAD_W
    return _rowwise("l0_mla_back", body, rows, [gq, gkv, wq, wk, wv], [(640, BF16)],
                    [((MLA_LORA, wide), F32)] * 3 + [((1, MLA_LORA), F32)] * 2, tr=256)


def _in_back(x, dzm, dzs, dy, wm, ws):
    def body(rows, consts, outs, accs):
        dzm_t, dzs_t = rows[1][...], rows[2][...]
        outs[0][...] = _dot(dzm_t, consts[0][...], NT) + _dot(dzs_t, consts[1][...], NT) + ALPHA * rows[3][...]
        xb = rows[0][...].astype(BF16)
        accs[0][...] += _dot(xb, dzm_t, TN_)
        accs[1][...] += _dot(xb, dzs_t, TN_)

    return _rowwise("l0_in_back", body, [_full(x), _full(dzm), _full(dzs), _full(dy)], [wm, ws], [(D_MODEL, F32)],
                    [((D_MODEL, 640), F32), ((D_MODEL, 1024), F32)])


def _out_weight_grads(o_att, b_out, dy_bf):
    def body(rows, consts, outs, accs):
        d = rows[2][...]
        accs[0][...] += _dot(rows[0][...].astype(BF16), d, TN_)
        accs[1][...] += _dot(rows[1][...], d, TN_)

    return _rowwise("l0_dw_out", body, [_full(o_att), _full(b_out), _full(dy_bf)], [], [],
                    [((HEADS * HEAD_W, D_MODEL), F32), ((SGU_DIM, D_MODEL), F32)])


def _attn_block(T):
    return min(1024, T)


def _attn_fwd(q, k, v):
    T = q.shape[0]
    BQ = _attn_block(T)
    nq = T // BQ

    def kern(q_ref, k_ref, v_ref, o_ref, lse_ref):
        def step(i, j, carry, masked):
            m, l, acc = carry
            qb = q_ref[pl.ds(pl.multiple_of(i * BQ, BQ), BQ), :]
            kb = k_ref[pl.ds(pl.multiple_of(j * BQ, BQ), BQ), :]
            vb = v_ref[pl.ds(pl.multiple_of(j * BQ, BQ), BQ), :]
            s = _dot(qb, kb, NT) * MLA_SCALE
            if masked:
                row = lax.broadcasted_iota(jnp.int32, s.shape, 0)
                col = lax.broadcasted_iota(jnp.int32, s.shape, 1)
                s = jnp.where(col <= row, s, -1e30)
            m_new = jnp.maximum(m, jnp.max(s, axis=-1, keepdims=True))
            p = jnp.exp(s - m_new)
            a = jnp.exp(m - m_new)
            l = a * l + jnp.sum(p, axis=-1, keepdims=True)
            acc = a * acc + _dot(p.astype(BF16), vb, NN)
            return m_new, l, acc

        def qloop(i, _):
            init = (jnp.full((BQ, 1), -1e30, F32), jnp.zeros((BQ, 1), F32), jnp.zeros((BQ, HEAD_W), F32))
            carry = lax.fori_loop(0, i, lambda j, c: step(i, j, c, False), init)
            m, l, acc = step(i, i, carry, True)
            rows = pl.ds(pl.multiple_of(i * BQ, BQ), BQ)
            o_ref[rows, :] = acc / l
            lse_ref[0, rows, :] = m + jnp.log(l)
            return 0

        lax.fori_loop(0, nq, qloop, 0)

    head = pl.BlockSpec((T, HEAD_W), lambda h: (0, h))
    nbytes = 3 * _nbytes((T, HEAD_W), BF16) + _nbytes((T, HEAD_W), F32) + _nbytes((T, 128), F32)
    return pl.pallas_call(
        kern, name="attn_fwd", grid=(HEADS,), in_specs=[head, head, head],
        out_specs=[head, pl.BlockSpec((1, T, 1), lambda h: (h, 0, 0))],
        out_shape=[pltpu.HBM((T, HEADS * HEAD_W), F32), pltpu.HBM((HEADS, T, 1), F32)],
        compiler_params=pltpu.CompilerParams(dimension_semantics=("parallel",), vmem_limit_bytes=_vmem(nbytes)),
    )(_hbm(q), _hbm(k), _hbm(v))


def _attn_bwd(q, k, v, o, lse, dcat, deps=()):
    T = q.shape[0]
    BQ = _attn_block(T)
    nq = T // BQ
    deps = _deps(deps)

    def kern(q_ref, k_ref, v_ref, o_ref, lse_ref, do_ref, *rest):
        dq_ref, dk_ref, dv_ref, dd_ref = rest[len(deps):]
        dq_ref[...] = jnp.zeros(dq_ref.shape, F32)

        def dloop(i, _):
            rows = pl.ds(pl.multiple_of(i * BQ, BQ), BQ)
            dd_ref[rows, :] = jnp.sum(do_ref[rows, :].astype(F32) * o_ref[rows, :], axis=-1, keepdims=True)
            return 0

        lax.fori_loop(0, nq, dloop, 0)

        def step(j, i, carry, masked):
            dk_acc, dv_acc = carry
            rq = pl.ds(pl.multiple_of(i * BQ, BQ), BQ)
            rk = pl.ds(pl.multiple_of(j * BQ, BQ), BQ)
            qb, kb, vb, dob = q_ref[rq, :], k_ref[rk, :], v_ref[rk, :], do_ref[rq, :]
            s = _dot(qb, kb, NT) * MLA_SCALE
            p = jnp.exp(s - lse_ref[0, rq, :])
            if masked:
                row = lax.broadcasted_iota(jnp.int32, s.shape, 0)
                col = lax.broadcasted_iota(jnp.int32, s.shape, 1)
                p = jnp.where(col <= row, p, 0.0)
            dp = _dot(dob, vb, NT)
            ds = (p * (dp - dd_ref[rq, :]) * MLA_SCALE).astype(BF16)
            dv_acc = dv_acc + _dot(p.astype(BF16), dob, TN_)
            dk_acc = dk_acc + _dot(ds, qb, TN_)
            dq_ref[rq, :] += _dot(ds, kb, NN)
            return dk_acc, dv_acc

        def kloop(j, _):
            init = (jnp.zeros((BQ, HEAD_W), F32), jnp.zeros((BQ, HEAD_W), F32))
            carry = step(j, j, init, True)
            dk_acc, dv_acc = lax.fori_loop(j + 1, nq, lambda i, c: step(j, i, c, False), carry)
            rk = pl.ds(pl.multiple_of(j * BQ, BQ), BQ)
            dk_ref[rk, :] = dk_acc
            dv_ref[rk, :] = dv_acc
            return 0

        lax.fori_loop(0, nq, kloop, 0)

    head = pl.BlockSpec((T, HEAD_W), lambda h: (0, h))
    nbytes = 4 * _nbytes((T, HEAD_W), BF16) + 5 * _nbytes((T, HEAD_W), F32) + 2 * _nbytes((T, 128), F32)
    return pl.pallas_call(
        kern, name="attn_bwd", grid=(HEADS,),
        in_specs=[head, head, head, head, pl.BlockSpec((1, T, 1), lambda h: (h, 0, 0)), head] + [ANY_SPEC] * len(deps),
        out_specs=[head, head, head],
        out_shape=[pltpu.HBM((T, HEADS * HEAD_W), F32)] * 3,
        scratch_shapes=[pltpu.VMEM((T, 1), F32)],
        compiler_params=pltpu.CompilerParams(dimension_semantics=("parallel",), vmem_limit_bytes=_vmem(nbytes)),
    )(*[_hbm(a) for a in (q, k, v, o, lse, dcat)], *deps)


def _sgu_common(u, v, ln_g, ln_b):
    ua, tu = _gelu(u)
    va, tv = _gelu(v)
    vh, r = _ln_stats(va)
    return ua, tu, tv, vh, r, vh * ln_g + ln_b


def _tril_mask(n):
    return lax.broadcasted_iota(jnp.int32, (n, n), 1) <= lax.broadcasted_iota(jnp.int32, (n, n), 0)


def _sgu_fwd(zs, ln_g, ln_b, w, bias_full):
    def body(rows, consts, outs, accs):
        ua, _, _, _, _, vn = _sgu_common(rows[0][...], rows[1][...], consts[0][...], consts[1][...])
        vn = vn.astype(BF16)
        tri = _tril_mask(SGU_CHUNK)
        for g in range(SGU_G):
            wg = jnp.where(tri, consts[2][0, g], 0.0).astype(BF16)
            cols = slice(g * 128, (g + 1) * 128)
            for c in range(ua.shape[0] // SGU_CHUNK):
                rws = slice(c * SGU_CHUNK, (c + 1) * SGU_CHUNK)
                mixed = _dot(wg, vn[rws, cols], NN) + consts[3][:, cols]
                outs[0][rws, cols] = (ua[rws, cols] * mixed).astype(BF16)

    return _rowwise("sgu_fwd", body, [(zs, 512, 0), (zs, 512, 1)], [ln_g, ln_b, w, bias_full], [(SGU_DIM, BF16)])


def _sgu_bwd(zs, dcat, ln_g, ln_b, w, bias_full):
    def body(rows, consts, outs, accs):
        u, v = rows[0][...], rows[1][...]
        ua, tu, tv, vh, r, vn = _sgu_common(u, v, consts[0][...], consts[1][...])
        dout = rows[2][...].astype(F32)
        vn_bf = vn.astype(BF16)
        tri = _tril_mask(SGU_CHUNK)
        dmixed = (dout * ua)
        dmixed_bf = dmixed.astype(BF16)
        ones = jnp.ones((8, SGU_CHUNK), F32)
        dvn_cols, mixed_cols = [], []
        for g in range(SGU_G):
            wg = jnp.where(tri, consts[2][0, g], 0.0).astype(BF16)
            cols = slice(g * 128, (g + 1) * 128)
            dvn_rows, mixed_rows = [], []
            dw = jnp.zeros((SGU_CHUNK, SGU_CHUNK), F32)
            dmix_sum = jnp.zeros((SGU_CHUNK, 128), F32)
            for c in range(u.shape[0] // SGU_CHUNK):
                rws = slice(c * SGU_CHUNK, (c + 1) * SGU_CHUNK)
                mixed_rows.append(_dot(wg, vn_bf[rws, cols], NN) + consts[3][:, cols])
                dvn_rows.append(_dot(wg, dmixed_bf[rws, cols], TN_))
                dw = dw + _dot(dmixed_bf[rws, cols], vn_bf[rws, cols], NT)
                dmix_sum = dmix_sum + dmixed[rws, cols]
            accs[0][g] += jnp.where(tri, dw, 0.0)
            accs[3][g:g + 1, :] += _dot(ones, dmix_sum, NT, precision=HIGHEST)[0:1, :]
            dvn_cols.append(jnp.concatenate(dvn_rows, axis=0))
            mixed_cols.append(jnp.concatenate(mixed_rows, axis=0))
        dvn = jnp.concatenate(dvn_cols, axis=1)
        mixed = jnp.concatenate(mixed_cols, axis=1)
        accs[1][...] += jnp.sum(dvn * vh, axis=0, keepdims=True)
        accs[2][...] += jnp.sum(dvn, axis=0, keepdims=True)
        dvh = dvn * consts[0][...]
        dva = r * (dvh - jnp.mean(dvh, axis=-1, keepdims=True) - vh * jnp.mean(dvh * vh, axis=-1, keepdims=True))
        outs[0][:, 0:512] = (dout * mixed * _gelu_grad(u, tu)).astype(BF16)
        outs[0][:, 512:1024] = (dva * _gelu_grad(v, tv)).astype(BF16)

    return _rowwise("sgu_bwd", body, [(zs, 512, 0), (zs, 512, 1), (dcat, 512, 2)], [ln_g, ln_b, w, bias_full], [(1024, BF16)],
                    [((SGU_G, 128, 128), F32), ((1, SGU_DIM), F32), ((1, SGU_DIM), F32), ((SGU_G, 128), F32)], tr=256)


def _lower_bound(hg_lb):
    a0, a1 = hg_lb[0:1, :], hg_lb[1:2, :]
    m = jnp.maximum(a0, a1)
    e0, e1 = jnp.exp(a0 - m), jnp.exp(a1 - m)
    s0, s1 = e0 / (e0 + e1), e1 / (e0 + e1)
    return (s0 + s1) - s0, s0, s1


def _prefix_rows(x, reverse=False):
    n = x.shape[0]
    row = lax.broadcasted_iota(jnp.int32, x.shape, 0)
    s = 1
    while s < n:
        if reverse:
            x = x + jnp.where(row < n - s, pltpu.roll(x, n - s, 0), 0.0)
        else:
            x = x + jnp.where(row >= s, pltpu.roll(x, s, 0), 0.0)
        s *= 2
    return x


def _hg_gates(qr, fr, lb):
    C = qr.shape[0]
    sq = _sig(qr)
    qf = qr * sq
    sf = _sig(fr)
    gate = lb + (1.0 - lb) * sf
    kk = 1.0 - gate
    tri = _tril_mask(C)
    b = _prefix_rows(jnp.log(gate))
    bref = b[C // 2 - 1:C // 2, :]
    bl = b[C - 1:C, :]
    e_b = jnp.exp(b)
    e_q = jnp.exp(b - bref)
    e_k = jnp.exp(bref - b)
    e_lb = jnp.exp(bl - b)
    return dict(sq=sq, qf=qf, sf=sf, gate=gate, kk=kk, tri=tri, bl=bl, e_b=e_b, e_q=e_q, e_k=e_k, e_lb=e_lb)


def _hgrn_fwd(z1, hg_lb, gnorm):
    T = z1.shape[0]
    C = min(HG_CHUNK, T)
    nc = T // C
    ns = HG_CHUNKS_PER_STEP if nc % HG_CHUNKS_PER_STEP == 0 else 1
    R = ns * C

    def kern(q_ref, f_ref, i_ref, g_ref, lb_ref, gn_ref, o_ref, hg_ref, st_ref, s_scr):
        @pl.when(pl.program_id(0) == 0)
        def _():
            s_scr[...] = jnp.zeros(s_scr.shape, F32)

        lb_all, _, _ = _lower_bound(lb_ref[...])
        for sub in range(ns):
            rows = slice(sub * C, (sub + 1) * C)
            st_ref[sub] = s_scr[...]
            for h in range(HEADS):
                cols = slice(h * HEAD_W, (h + 1) * HEAD_W)
                t = _hg_gates(q_ref[rows, cols], f_ref[rows, cols], lb_all[:, cols])
                v_bf = i_ref[rows, cols].astype(BF16)
                st = s_scr[h]
                a = jnp.where(t["tri"], _dot((t["qf"] * t["e_q"]).astype(BF16), (t["kk"] * t["e_k"]).astype(BF16), NT), 0.0)
                o = _dot(a.astype(BF16), v_bf, NN) + _dot((t["qf"] * t["e_b"]).astype(BF16), st.astype(BF16), NT)
                s_scr[h] = st * jnp.exp(t["bl"]) + _dot(v_bf, (t["kk"] * t["e_lb"]).astype(BF16), TN_)
                o_ref[rows, cols] = o
                gr = g_ref[rows, cols]
                r = lax.rsqrt(jnp.mean(o * o, axis=-1, keepdims=True) + EPS)
                hg_ref[rows, cols] = (o * r * gn_ref[:, cols] * (gr * _sig(gr))).astype(BF16)

    seg = lambda k: pl.BlockSpec((R, D_MODEL), functools.partial(lambda n, k: (n, k), k=k))
    row = pl.BlockSpec((R, D_MODEL), lambda n: (n, 0))
    nbytes = 6 * _nbytes((R, D_MODEL), F32) + (2 + ns) * _nbytes((HEADS, 128, 128), F32)
    return pl.pallas_call(
        kern, name="hgrn_fwd", grid=(nc // ns,),
        in_specs=[seg(0), seg(1), seg(2), seg(3), pl.BlockSpec((2, D_MODEL), lambda n: (0, 0)),
                  pl.BlockSpec((1, D_MODEL), lambda n: (0, 0))],
        out_specs=[row, row, pl.BlockSpec((ns, HEADS, 128, 128), lambda n: (n, 0, 0, 0))],
        out_shape=[pltpu.HBM((T, D_MODEL), F32), pltpu.HBM((T, D_MODEL), BF16),
                   pltpu.HBM((nc, HEADS, 128, 128), F32)],
        scratch_shapes=[pltpu.VMEM((HEADS, 128, 128), F32)],
        compiler_params=pltpu.CompilerParams(dimension_semantics=("arbitrary",), vmem_limit_bytes=_vmem(nbytes)),
    )(*[_hbm(a) for a in (z1, z1, z1, z1, hg_lb, gnorm)])


def _hgrn_bwd(z1, o_pre, dhg, states, hg_lb, gnorm):
    T = z1.shape[0]
    C = min(HG_CHUNK, T)
    nc = T // C
    ns = HG_CHUNKS_PER_STEP if nc % HG_CHUNKS_PER_STEP == 0 else 1
    R, steps = ns * C, nc // ns

    def kern(q_ref, f_ref, i_ref, g_ref, o_ref, dhg_ref, st_ref, lb_ref, gn_ref, dz_ref, dlb_ref, dgn_ref, ds_scr, dlb_scr):
        n = pl.program_id(0)

        @pl.when(n == 0)
        def _():
            ds_scr[...] = jnp.zeros(ds_scr.shape, F32)
            dlb_scr[...] = jnp.zeros(dlb_scr.shape, F32)
            dgn_ref[...] = jnp.zeros(dgn_ref.shape, F32)

        lb_all, s0, s1 = _lower_bound(lb_ref[...])
        for sub in reversed(range(ns)):
            rows = slice(sub * C, (sub + 1) * C)
            for h in range(HEADS):
                cols = slice(h * HEAD_W, (h + 1) * HEAD_W)
                lb = lb_all[:, cols]
                qr, fr = q_ref[rows, cols], f_ref[rows, cols]
                t = _hg_gates(qr, fr, lb)
                tri = t["tri"]
                v_bf = i_ref[rows, cols].astype(BF16)
                st_bf = st_ref[sub, h].astype(BF16)
                dst = ds_scr[h]
                dst_bf = dst.astype(BF16)
                o = o_ref[rows, cols]
                gr = g_ref[rows, cols]
                sg = _sig(gr)
                sil = gr * sg
                gn = gn_ref[:, cols]
                r = lax.rsqrt(jnp.mean(o * o, axis=-1, keepdims=True) + EPS)
                on = o * r
                dh = dhg_ref[rows, cols].astype(F32)
                dgn_ref[:, cols] += jnp.sum(dh * on * sil, axis=0, keepdims=True)
                dg = dh * on * gn * (sg * (1.0 + gr * (1.0 - sg)))
                don = dh * gn * sil
                do_bf = (r * (don - on * jnp.mean(don * on, axis=-1, keepdims=True))).astype(BF16)
                qe = (t["qf"] * t["e_q"]).astype(BF16)
                ke = (t["kk"] * t["e_k"]).astype(BF16)
                qb = (t["qf"] * t["e_b"]).astype(BF16)
                kh_bf = (t["kk"] * t["e_lb"]).astype(BF16)
                a_bf = jnp.where(tri, _dot(qe, ke, NT), 0.0).astype(BF16)
                da_bf = jnp.where(tri, _dot(do_bf, v_bf, NT), 0.0).astype(BF16)
                dv = _dot(a_bf, do_bf, TN_) + _dot(kh_bf, dst_bf, NT)
                dqe = _dot(da_bf, ke, NN)
                dqb = _dot(do_bf, st_bf, NN)
                dke = _dot(da_bf, qe, TN_)
                dkh = _dot(v_bf, dst_bf, NN)
                dqf = dqe * t["e_q"] + dqb * t["e_b"]
                dkk = dke * t["e_k"] + dkh * t["e_lb"]
                kh_r = kh_bf.astype(F32)
                db = qe.astype(F32) * dqe - ke.astype(F32) * dke + qb.astype(F32) * dqb - kh_r * dkh
                e_bl = jnp.exp(t["bl"])
                dbl = jnp.sum(dkh * kh_r, axis=0, keepdims=True) + e_bl * jnp.sum(st_ref[sub, h] * dst, axis=0, keepdims=True)
                dlg = _prefix_rows(db, reverse=True) + dbl
                ds_scr[h] = dst * e_bl + _dot(do_bf, qb, TN_)
                dgate = dlg / t["gate"] - dkk
                sf = t["sf"]
                dlb_scr[:, cols] += jnp.sum(dgate * (1.0 - sf), axis=0, keepdims=True)
                df = dgate * (1.0 - lb) * sf * (1.0 - sf)
                dq = dqf * (t["sq"] * (1.0 + qr * (1.0 - t["sq"])))
                dz_ref[rows, cols] = dq.astype(BF16)
                dz_ref[rows, D_MODEL + h * HEAD_W:D_MODEL + (h + 1) * HEAD_W] = df.astype(BF16)
                dz_ref[rows, 2 * D_MODEL + h * HEAD_W:2 * D_MODEL + (h + 1) * HEAD_W] = dv.astype(BF16)
                dz_ref[rows, 3 * D_MODEL + h * HEAD_W:3 * D_MODEL + (h + 1) * HEAD_W] = dg.astype(BF16)

        @pl.when(n == steps - 1)
        def _():
            d = s0 * s1 * dlb_scr[...]
            dlb_ref[0:1, :] = -d
            dlb_ref[1:2, :] = d

    seg = lambda k: pl.BlockSpec((R, D_MODEL), functools.partial(lambda n, k: (steps - 1 - n, k), k=k))
    nbytes = 6 * _nbytes((R, D_MODEL), F32) + _nbytes((R, 4 * D_MODEL), BF16) + (2 + ns) * _nbytes((HEADS, 128, 128), F32)
    return pl.pallas_call(
        kern, name="hgrn_bwd", grid=(steps,),
        in_specs=[seg(0), seg(1), seg(2), seg(3), seg(0), seg(0),
                  pl.BlockSpec((ns, HEADS, 128, 128), lambda n: (steps - 1 - n, 0, 0, 0)),
                  pl.BlockSpec((2, D_MODEL), lambda n: (0, 0)), pl.BlockSpec((1, D_MODEL), lambda n: (0, 0))],
        out_specs=[pl.BlockSpec((R, 4 * D_MODEL), lambda n: (steps - 1 - n, 0)),
                   pl.BlockSpec((2, D_MODEL), lambda n: (0, 0)), pl.BlockSpec((1, D_MODEL), lambda n: (0, 0))],
        out_shape=[pltpu.HBM((T, 4 * D_MODEL), BF16), pltpu.HBM((2, D_MODEL), F32),
                   pltpu.HBM((1, D_MODEL), F32)],
        scratch_shapes=[pltpu.VMEM((HEADS, 128, 128), F32), pltpu.VMEM((1, D_MODEL), F32)],
        compiler_params=pltpu.CompilerParams(dimension_semantics=("arbitrary",), vmem_limit_bytes=_vmem(nbytes)),
    )(*[_hbm(a) for a in (z1, z1, z1, z1, o_pre, dhg, states, hg_lb, gnorm)])


def _prep_weights(gw):
    w_in_e = gw["w_in_e"].transpose(1, 0, 2).reshape(D_MODEL, 1568)
    kr = jnp.pad(w_in_e[:, 512:544], ((0, 0), (64, 32)))
    wm = jnp.concatenate([w_in_e[:, 0:512], kr], axis=1)
    ws = w_in_e[:, 544:1568]
    w_qb = gw["w_qb"].transpose(1, 0, 2).reshape(MLA_LORA, HEADS, 96)
    wq = jnp.pad(w_qb, ((0, 0), (0, 0), (0, 32))).reshape(MLA_LORA, HEADS * HEAD_W)
    kvb = gw["w_kvb"].transpose(1, 0, 2).reshape(MLA_LORA, HEADS, 128)
    wk = jnp.pad(kvb[:, :, :64], ((0, 0), (0, 0), (0, 64))).reshape(MLA_LORA, HEADS * HEAD_W)
    wv = jnp.pad(kvb[:, :, 64:], ((0, 0), (0, 0), (0, 64))).reshape(MLA_LORA, HEADS * HEAD_W)
    w_out_e = gw["w_out_e"].reshape(D_MODEL, D_MODEL)
    woa = jnp.pad(w_out_e[:512].reshape(HEADS, 64, D_MODEL), ((0, 0), (0, 64), (0, 0))).reshape(HEADS * HEAD_W, D_MODEL)
    return dict(wm=wm, ws=ws, wq=wq, wk=wk, wv=wv, woa=woa, wob=w_out_e[512:])


def _unprep_grads(g):
    dwm, dws = g["wm"], g["ws"]
    d_in_e = jnp.concatenate([dwm[:, 0:512], dwm[:, 512 + 64:512 + 96], dws], axis=1)
    d_qb = g["wq"].reshape(MLA_LORA, HEADS, HEAD_W)[:, :, :96].reshape(MLA_LORA, HEADS * 96)
    dk = g["wk"].reshape(MLA_LORA, HEADS, HEAD_W)[:, :, :64]
    dv = g["wv"].reshape(MLA_LORA, HEADS, HEAD_W)[:, :, :64]
    d_kvb = jnp.concatenate([dk, dv], axis=2).reshape(MLA_LORA, HEADS * 128)
    d_oa = g["woa"].reshape(HEADS, HEAD_W, D_MODEL)[:, :64].reshape(HEADS * 64, D_MODEL)
    dev_major = lambda a: a.reshape(a.shape[0], N_DEV, a.shape[1] // N_DEV).transpose(1, 0, 2)
    return dict(w_in_e=dev_major(d_in_e), w_qb=dev_major(d_qb), w_kvb=dev_major(d_kvb),
                w_out_e=jnp.concatenate([d_oa, g["wob"]], axis=0).reshape(N_DEV, D_MODEL // N_DEV, D_MODEL))


def _local_step(x, positions, target, gw, sp, ex):
    w = _prep_weights(gw)
    T = x.shape[0]
    tm = min(TM, T)
    nt = T // tm
    half = MLA_ROPE // 2
    inv_freq = ROPE_BASE ** (-jnp.arange(half, dtype=F32) / half)
    invf_lane = jnp.concatenate([jnp.zeros((64,), F32), inv_freq, inv_freq, jnp.zeros((32,), F32)]).reshape(1, HEAD_W)
    tabs = _rope_tables(positions.reshape(T, 1), invf_lane)
    bias_full = jnp.repeat(sp["sgu_b"][0].T, 128, axis=1)
    sgu_w = sp["sgu_w"]
    gq, gkv = sp["mla_gq"], sp["mla_gkv"]
    ln1_g, ln1_b, ln2_g, ln2_b = sp["ln1_g"], sp["ln1_b"], sp["ln2_g"], sp["ln2_b"]
    zm, zs, cqn, ckvn, kr_rot = _mla_in(x, w["wm"], w["ws"], tabs, gq, gkv, deps=[ex.first_token])
    q, k, v = _mla_qkv(cqn, ckvn, kr_rot, tabs, w["wq"], w["wk"], w["wv"])
    o_att, lse = _attn_fwd(q, k, v)
    b_out = _sgu_fwd(zs, sp["sgu_ln_g"], sp["sgu_ln_b"], sgu_w, bias_full)
    token = ex.weights_forward(after=[o_att, b_out])
    y1, h1, h1_bf = _proj_ln("l0_out_ln1", [o_att, b_out], [w["woa"], w["wob"]], x, ln1_g, ln1_b, 0, deps=[token])
    big = ex.weights_ready(after=[y1])
    w_ff1, w_in_o, w_out_o = big["w_ff1"], big["w_in_o"], big["w_out_o"].reshape(D_MODEL, D_MODEL)
    w_ff2 = [a.reshape(D_FF, D_MODEL) for a in big["w_ff2"]]
    a0, act0 = _mlp_up("l0", h1_bf, w_ff1[0])
    y2, h2, h2_bf = _proj_ln("l0_ff2_ln2", [act0], [w_ff2[0]], h1, ln2_g, ln2_b, 0)

    z1 = _tiled("l1_in", (1, nt), [_rb(h2_bf, tm), _res(w_in_o)], [_out(T, 4 * D_MODEL, F32, tm, 4 * D_MODEL)],
                _mmc_blocks(N_DEV, NN, lambda w, d: w[d]), direct=True)
    o_pre, hg, states = _hgrn_fwd(z1, sp["hg_lb"], sp["hg_gnorm"])
    y3, h3, h3_bf = _proj_ln("l1_out_ln1", [hg], [w_out_o], h2, ln1_g, ln1_b, 1)
    a1, act1 = _mlp_up("l1", h3_bf, w_ff1[1])

    gs, g0 = {}, {}
    dy4, dy4_bf, sq_err, gs["ln2_g1"], gs["ln2_b1"] = _proj_ln_loss("l1_ff2_loss", act1, w_ff2[1], h3, ln2_g, ln2_b, 1, target)
    da1, dw1_1, dw2_1 = _mlp_bwd_w("l1", h3_bf, a1, act1, dy4_bf, big["w_ff2"][1])
    dy3, dy3_bf, dhg, gs["ln1_g1"], gs["ln1_b1"] = _dh_ln_back("l1_dh_ln1", da1, w_ff1[1], dy4, y3, ln1_g, 1, proj=[w_out_o])
    d_out_o = _tiled("l1_dwout", (2, D_MODEL // TM), [_tl(hg, TM), _cw(dy3_bf, TN)], [_out(D_MODEL, D_MODEL, F32, TM, TN)],
                     _mmc(TN_)).reshape(N_DEV, D_MODEL // N_DEV, D_MODEL)
    dz1, gs["hg_lb"], gs["hg_gnorm"] = _hgrn_bwd(z1, o_pre, dhg, states, sp["hg_lb"], sp["hg_gnorm"])
    d_in_o = _tiled("l1_dwin", (N_DEV, 1), [_res(h2_bf), _cw(dz1, TN)], [_out_dev(D_MODEL, TN, D_MODEL)], _mmc(TN_))
    token = ex.grads_start("l1", [dw1_1, dw2_1, d_in_o, d_out_o])

    dy2, dy2_bf, gs["ln2_g0"], gs["ln2_b0"] = _dh_ln_back("l1_dh_ln2", dz1, w_in_o, dy3, y2, ln2_g, 0, deps=[token])
    token = ex.grads_middle("l1", after=dy2)
    da0, dw1_0, dw2_0 = _mlp_bwd_w("l0", h1_bf, a0, act0, dy2_bf, big["w_ff2"][0], deps=[token])
    token = ex.grads_start("l0m", [dw1_0, dw2_0])
    dy1, dy1_bf, dcat, gs["ln1_g0"], gs["ln1_b0"] = _dh_ln_back("l0_dh_ln1", da0, w_ff1[0], dy2, y1, ln1_g, 0,
                                                                 proj=[w["woa"], w["wob"]], deps=[token])
    ex.grads_end("l1", after=dy1)
    g0["woa"], g0["wob"] = _out_weight_grads(o_att, b_out, dy1_bf)
    token = ex.grads_middle("l0m", after=g0["wob"])
    dzs, gs["sgu_w"], gs["sgu_ln_g"], gs["sgu_ln_b"], gs["sgu_b"] = _sgu_bwd(zs, dcat, sp["sgu_ln_g"], sp["sgu_ln_b"], sgu_w, bias_full)
    dq, dk, dv = _attn_bwd(q, k, v, o_att, lse, dcat, deps=[token])
    ex.grads_end("l0m", after=dq)
    dzm, g0["wq"], g0["wk"], g0["wv"], gs["mla_gq"], gs["mla_gkv"] = _mla_back(zm, cqn, ckvn, tabs, gq, gkv, w["wq"], w["wk"], w["wv"],
                                                                                 dq, dk, dv)
    dx, g0["wm"], g0["ws"] = _in_back(x, dzm, dzs, dy1, w["wm"], w["ws"])

    return sq_err, dx, _unprep_grads(g0), gs


def _me():
    return lax.axis_index("x"), lax.axis_index("y"), lax.axis_index("c")


def _hbm_call(name, kern, operands, out_shape, n_sems, extra_scratch=()):
    any_spec = pl.BlockSpec(memory_space=pl.ANY)
    return pl.pallas_call(
        kern, name=name, out_shape=out_shape, in_specs=[any_spec] * len(operands), out_specs=[any_spec] * len(out_shape),
        scratch_shapes=[pltpu.SemaphoreType.DMA((n_sems,)), pltpu.SemaphoreType.DMA((n_sems,)), *extra_scratch],
    )(*[_hbm(a) for a in operands])


ANY_SPEC = pl.BlockSpec(memory_space=pl.ANY)
HBM_SPEC = pl.BlockSpec(memory_space=pltpu.HBM)
SEM_SPEC = pl.BlockSpec(memory_space=pltpu.SEMAPHORE)
EFFECT = pltpu.SideEffectType.DATAFLOW_SIDE_EFFECTING


def _split_start(name, srcs, lands, n_sems, make_copies, after=()):
    n, m, k = len(srcs), len(lands), len(after)

    def body(*refs):
        for cp in make_copies(refs[:n], refs[n:n + m], refs[n + m + k], refs[n + m + k + 1]):
            cp.start()
        refs[-1][...] = jnp.zeros(refs[-1].shape, F32)

    out_shape = (pltpu.SemaphoreType.DMA((n_sems,)), pltpu.SemaphoreType.DMA((n_sems,)),
                 *[pltpu.HBM(a.shape, a.dtype) for a in (*srcs, *lands)], jax.ShapeDtypeStruct((8, 128), F32))
    res = pl.pallas_call(
        body, name=name, out_shape=out_shape, in_specs=[HBM_SPEC] * (n + m) + [ANY_SPEC] * k,
        out_specs=(SEM_SPEC, SEM_SPEC, *[HBM_SPEC] * (n + m), pl.BlockSpec(memory_space=pltpu.VMEM)),
        input_output_aliases={i: 2 + i for i in range(n + m)},
        compiler_params=pltpu.CompilerParams(has_side_effects=EFFECT),
    )(*[_hbm(a) for a in (*srcs, *lands)], *after)
    return res[0], res[1], list(res[2:2 + n]), list(res[2 + n:2 + n + m]), res[-1]


def _split_wait(name, send_sems, recv_sems, srcs, lands, after, make_copies):
    n, m = len(srcs), len(lands)

    def body(*refs):
        for cp in make_copies(refs[:n], refs[n:n + m], refs[n + m], refs[n + m + 1]):
            cp.wait_send()
            cp.wait_recv()

    res = pl.pallas_call(
        body, name=name, out_shape=tuple(pltpu.HBM(a.shape, a.dtype) for a in (*srcs, *lands)),
        in_specs=[HBM_SPEC] * (n + m) + [SEM_SPEC, SEM_SPEC] + [ANY_SPEC] * len(after), out_specs=tuple([HBM_SPEC] * (n + m)),
        input_output_aliases={i: i for i in range(n + m)},
        compiler_params=pltpu.CompilerParams(has_side_effects=EFFECT),
    )(*srcs, *lands, send_sems, recv_sems, *after)
    return list(res[:n]), list(res[n:])


def _place_own(shards, dev):
    n = len(shards)

    def kern(dev_ref, *refs):
        for x_ref, o_ref in zip(refs[:n], refs[n:]):
            o_ref[...] = x_ref[...].astype(o_ref.dtype)

    blocks = [(None, *a.shape[1:]) for a, _, _ in shards]
    nbytes = sum(_nbytes(b, a.dtype) + _nbytes(b, dt) for b, (a, _, dt) in zip(blocks, shards))
    return pl.pallas_call(
        kern, name="weights_place_own", out_shape=[pltpu.HBM((N_DEV, *a.shape[1:]), dt) for a, _, dt in shards],
        grid_spec=pltpu.PrefetchScalarGridSpec(
            num_scalar_prefetch=1, grid=(1,),
            in_specs=[pl.BlockSpec(b, functools.partial(lambda i, dev, l: (l, 0, 0), l=l)) for b, (_, l, _) in zip(blocks, shards)],
            out_specs=[pl.BlockSpec(b, lambda i, dev: (dev[0], 0, 0)) for b in blocks]),
        compiler_params=pltpu.CompilerParams(dimension_semantics=("arbitrary",), vmem_limit_bytes=_vmem(nbytes)),
    )(dev, *[_hbm(a) for a, _, _ in shards])


def _ag_first_copies(src_refs, out_refs, send_sems, recv_sems):
    x, y, c = _me()
    targets = [(x, y, 1 - c), (1 - x, y, c), (x, 1 - y, c), (1 - x, 1 - y, c)]
    return [pltpu.make_async_remote_copy(
        src_ref=out_refs[op].at[4 * x + 2 * y + c], dst_ref=out_refs[op].at[4 * x + 2 * y + c], send_sem=send_sems.at[4 * op + k],
        recv_sem=recv_sems.at[4 * op + k], device_id=to, device_id_type=MESH)
        for op in range(len(out_refs)) for k, to in enumerate(targets)]


def _ag_second_copies(src_refs, out_refs, send_sems, recv_sems):
    x, y, c = _me()
    chips = [(1 - x, y), (x, 1 - y), (1 - x, 1 - y)]
    return [pltpu.make_async_remote_copy(
        src_ref=out_refs[op].at[4 * cx + 2 * cy + c], dst_ref=out_refs[op].at[4 * cx + 2 * cy + c],
        send_sem=send_sems.at[3 * op + j], recv_sem=recv_sems.at[3 * op + j], device_id=(x, y, 1 - c), device_id_type=MESH)
        for op in range(len(out_refs)) for j, (cx, cy) in enumerate(chips)]


def _rs_sibling_copies(g_refs, out_refs, send_sems, recv_sems):
    x, y, c = _me()
    return [pltpu.make_async_remote_copy(
        src_ref=g_refs[op].at[k, 1 - c], dst_ref=out_refs[op].at[k], send_sem=send_sems.at[4 * op + k],
        recv_sem=recv_sems.at[4 * op + k], device_id=(x, y, 1 - c), device_id_type=MESH)
        for op in range(len(g_refs)) for k in range(4)]


def _rs_chip_copies(p_refs, out_refs, send_sems, recv_sems):
    x, y, c = _me()
    chips = [(1 - x, y), (x, 1 - y), (1 - x, 1 - y)]
    return [pltpu.make_async_remote_copy(
        src_ref=p_refs[op].at[2 * cx + cy], dst_ref=out_refs[op].at[j], send_sem=send_sems.at[3 * op + j],
        recv_sem=recv_sems.at[3 * op + j], device_id=(cx, cy, c), device_id_type=MESH)
        for op in range(len(p_refs)) for j, (cx, cy) in enumerate(chips)]


def _all_gather(placed):
    n = len(placed)

    def kern(*refs):
        in_refs, out_refs, (send_sems, recv_sems) = refs[:n], refs[n:2 * n], refs[2 * n:]
        x, y, c = _me()
        me, sibling = (x, y, c), (x, y, 1 - c)
        chips = [(1 - x, y), (x, 1 - y), (1 - x, 1 - y)]

        def copy(op, k, block, to, own=False):
            idx = 4 * block[0] + 2 * block[1] + block[2]
            return pltpu.make_async_remote_copy(
                src_ref=(in_refs if own else out_refs)[op].at[idx], dst_ref=out_refs[op].at[idx], send_sem=send_sems.at[7 * op + k],
                recv_sem=recv_sems.at[7 * op + k], device_id=to, device_id_type=MESH)

        first = []
        for op in range(n):
            first.append(copy(op, 0, me, sibling, own=True))
            first += [copy(op, 1 + j, me, (*chip, c), own=True) for j, chip in enumerate(chips)]
        for cp in first:
            cp.start()
        passed = []
        for j, chip in enumerate(chips):
            for op in range(n):
                copy(op, 1 + j, (*chip, c), me).wait_recv()
                passed.append(copy(op, 4 + j, (*chip, c), sibling))
                passed[-1].start()
        for op in range(n):
            copy(op, 0, sibling, me).wait_recv()
            for j, chip in enumerate(chips):
                copy(op, 4 + j, (*chip, 1 - c), me).wait_recv()
        for cp in first + passed:
            cp.wait_send()

    return pl.pallas_call(
        kern, name="weights_all_gather", out_shape=[pltpu.HBM(g.shape, g.dtype) for g in placed],
        in_specs=[ANY_SPEC] * n, out_specs=[ANY_SPEC] * n, input_output_aliases={i: i for i in range(n)},
        scratch_shapes=[pltpu.SemaphoreType.DMA((7 * n,)), pltpu.SemaphoreType.DMA((7 * n,))],
    )(*[_hbm(a) for a in placed])


def _rs_sibling(grads):
    n = len(grads)

    def kern(*refs):
        g_refs, out_refs, (send_sems, recv_sems) = refs[:n], refs[n:2 * n], refs[2 * n:]
        x, y, c = _me()
        copies = [pltpu.make_async_remote_copy(
            src_ref=g_refs[op].at[k, 1 - c], dst_ref=out_refs[op].at[k], send_sem=send_sems.at[4 * op + k],
            recv_sem=recv_sems.at[4 * op + k], device_id=(x, y, 1 - c), device_id_type=MESH) for op in range(n) for k in range(4)]
        for cp in copies:
            cp.start()
        for cp in copies:
            cp.wait()

    out_shape = [pltpu.HBM((4, *g.shape[2:]), g.dtype) for g in grads]
    return _hbm_call("grads_to_sibling", kern, grads, out_shape, 4 * n)


def _rs_chips(sums):
    n = len(sums)

    def kern(*refs):
        p_refs, out_refs, (send_sems, recv_sems) = refs[:n], refs[n:2 * n], refs[2 * n:]
        x, y, c = _me()
        chips = [(1 - x, y), (x, 1 - y), (1 - x, 1 - y)]
        copies = [pltpu.make_async_remote_copy(
            src_ref=p_refs[op].at[2 * cx + cy], dst_ref=out_refs[op].at[j], send_sem=send_sems.at[3 * op + j],
            recv_sem=recv_sems.at[3 * op + j], device_id=(cx, cy, c), device_id_type=MESH)
            for op in range(n) for j, (cx, cy) in enumerate(chips)]
        for cp in copies:
            cp.start()
        for cp in copies:
            cp.wait()

    out_shape = [pltpu.HBM((3, *p.shape[1:]), p.dtype) for p in sums]
    return _hbm_call("grads_between_chips", kern, sums, out_shape, 3 * n)


def _row_tile(r, w, n_blocks):
    tr = r
    while tr > 8 and 2 * n_blocks * tr * w * 4 > 24 * 2**20:
        tr //= 2
    return tr


def _chip_sum(name, g, from_sibling, core):
    _, _, R, W = g.shape
    tr = _row_tile(R, W, 3)

    def kern(core_ref, g_ref, s_ref, o_ref):
        o_ref[...] = (g_ref[...] + s_ref[...]).astype(BF16)

    return pl.pallas_call(
        kern, name=name, out_shape=pltpu.HBM((4, R, W), BF16),
        grid_spec=pltpu.PrefetchScalarGridSpec(
            num_scalar_prefetch=1, grid=(4, R // tr),
            in_specs=[pl.BlockSpec((None, None, tr, W), lambda k, i, core: (k, core[0], i, 0)),
                      pl.BlockSpec((None, tr, W), lambda k, i, core: (k, i, 0))],
            out_specs=pl.BlockSpec((None, tr, W), lambda k, i, core: (k, i, 0))),
        compiler_params=pltpu.CompilerParams(dimension_semantics=("parallel", "parallel"), vmem_limit_bytes=_vmem(3 * tr * W * 4)),
    )(core, _hbm(g), _hbm(from_sibling))


def _adamw(w, g, m, v):
    m = ADAM_B1 * m + (1.0 - ADAM_B1) * g
    v = ADAM_B2 * v + (1.0 - ADAM_B2) * (g * g)
    m_hat = m / (1.0 - ADAM_B1 ** ADAM_STEP)
    v_hat = v / (1.0 - ADAM_B2 ** ADAM_STEP)
    return -ADAM_LR * (m_hat / (jnp.sqrt(v_hat) + ADAM_EPS) + ADAM_WD * w), m, v


def _finish_sharded(name, layers, w, m, v, where):
    nl, R, W = w.shape
    tr = _row_tile(R, W, 11 * nl)

    def kern(where_ref, *refs):
        w_ref, m_ref, v_ref, go_ref, d_ref, mo_ref, vo_ref = refs[3 * nl:]
        for l in range(nl):
            g_ref, s_ref, c_ref = refs[3 * l:3 * l + 3]
            grad = g_ref[...] + s_ref[...]
            for j in range(3):
                grad = grad + c_ref[j].astype(F32)
            go_ref[l] = grad
            d_ref[l], mo_ref[l], vo_ref[l] = _adamw(w_ref[l], grad, m_ref[l], v_ref[l])

    row = pl.BlockSpec((nl, tr, W), lambda i, wh: (0, i, 0))
    in_specs, args = [], []
    for g, s, c in layers:
        in_specs += [pl.BlockSpec((None, None, tr, W), lambda i, wh: (wh[0], wh[1], i, 0)),
                     pl.BlockSpec((None, tr, W), lambda i, wh: (wh[0], i, 0)),
                     pl.BlockSpec((3, tr, W), lambda i, wh: (0, i, 0))]
        args += [g, s, c]
    return pl.pallas_call(
        kern, name=name, out_shape=[pltpu.HBM((nl, R, W), F32)] * 4,
        grid_spec=pltpu.PrefetchScalarGridSpec(num_scalar_prefetch=1, grid=(R // tr,), in_specs=in_specs + [row, row, row],
                                               out_specs=[row, row, row, row]),
        compiler_params=pltpu.CompilerParams(dimension_semantics=("parallel",), vmem_limit_bytes=_vmem(nl * 11 * tr * W * 4)),
    )(where, *[_hbm(a) for a in (*args, w, m, v)])


SMALL_PLACE = (("mla_gq", 0, 0, 1, 256), ("mla_gkv", 0, 256, 1, 256), ("sgu_ln_g", 0, 512, 1, 512), ("sgu_ln_b", 1, 0, 1, 512),
               ("hg_lb", 2, 0, 2, 1024), ("ln1_g", 4, 0, 2, 1024), ("ln1_b", 6, 0, 2, 1024), ("sgu_b", 8, 0, 4, 128),
               ("ln2_g", 12, 0, 2, 1024), ("ln2_b", 14, 0, 2, 1024), ("hg_gnorm", 16, 0, 1, 1024))
SMALL_BUF_ROWS = 24


def _small_reduce_adamw(gs, given):
    pieces = [(gs["mla_gq"], 0, 0), (gs["mla_gkv"], 0, 256), (gs["sgu_ln_g"], 0, 512), (gs["sgu_ln_b"], 1, 0), (gs["hg_lb"], 2, 0),
              (gs["ln1_g0"], 4, 0), (gs["ln1_g1"], 5, 0), (gs["ln1_b0"], 6, 0), (gs["ln1_b1"], 7, 0), (gs["sgu_b"], 8, 0),
              (gs["ln2_g0"], 12, 0), (gs["ln2_g1"], 13, 0), (gs["ln2_b0"], 14, 0), (gs["ln2_b1"], 15, 0), (gs["hg_gnorm"], 16, 0)]
    names = [p[0] for p in SMALL_PLACE] + ["sgu_w"]
    n_p, n_names = len(pieces), len(names)
    wmv = [given[pre + name] for name in names for pre in ("", "m_", "v_")]
    vmem = pl.BlockSpec(memory_space=pltpu.VMEM)

    def reduce_kern(*refs):
        piece_refs, gw_ref, sum_a_ref, sum_b_ref = refs[:n_p], refs[n_p], refs[n_p + 1], refs[n_p + 2]
        buf_a, buf_b, send_sems, recv_sems = refs[n_p + 3:]
        px, py, pc = _me()
        me = 4 * px + 2 * py + pc
        mine_a, mine_b = buf_a.at[me], buf_b.at[me]
        mine_a[...] = jnp.zeros(mine_a.shape, F32)
        for ref, (_, r, l0) in zip(piece_refs, pieces):
            mine_a[r:r + ref.shape[0], l0:l0 + ref.shape[1]] = ref[...]
        mine_b[...] = gw_ref[...]
        copies = []
        for r in range(1, N_DEV):
            peer = (px ^ (r >> 2), py ^ ((r >> 1) & 1), pc ^ (r & 1))
            for k, mine in enumerate((mine_a, mine_b)):
                copies.append(pltpu.make_async_remote_copy(
                    src_ref=mine, dst_ref=mine, send_sem=send_sems.at[2 * (r - 1) + k], recv_sem=recv_sems.at[2 * (r - 1) + k],
                    device_id=peer, device_id_type=MESH))
        for cp in copies:
            cp.start()
        for r in range(1, N_DEV):
            for k, buf in enumerate((buf_a, buf_b)):
                theirs = buf.at[me ^ r]
                pltpu.make_async_remote_copy(
                    src_ref=theirs, dst_ref=theirs, send_sem=send_sems.at[2 * (r - 1) + k], recv_sem=recv_sems.at[2 * (r - 1) + k],
                    device_id=(px, py, pc), device_id_type=MESH).wait_recv()
        for cp in copies:
            cp.wait_send()
        sum_a, sum_b = buf_a[0], buf_b[0]
        for d in range(1, N_DEV):
            sum_a, sum_b = sum_a + buf_a[d], sum_b + buf_b[d]
        sum_a_ref[...] = sum_a
        sum_b_ref[...] = sum_b

    sum_a, sum_b = pl.pallas_call(
        reduce_kern, name="small_all_reduce", in_specs=[vmem] * (n_p + 1), out_specs=[vmem, vmem],
        out_shape=[jax.ShapeDtypeStruct((SMALL_BUF_ROWS, D_MODEL), F32), jax.ShapeDtypeStruct((SGU_G, 128, 128), F32)],
        scratch_shapes=[pltpu.VMEM((N_DEV, SMALL_BUF_ROWS, D_MODEL), F32), pltpu.VMEM((N_DEV, SGU_G, 128, 128), F32),
                        pltpu.SemaphoreType.DMA((14,)), pltpu.SemaphoreType.DMA((14,))],
    )(*[p[0] for p in pieces], gs["sgu_w"])

    def kern(*refs):
        sum_a, sum_b = refs[0][...], refs[1][...]
        wmv_refs, out_refs = refs[2:2 + 3 * n_names], refs[2 + 3 * n_names:]
        px, py, pc = _me()
        me = 4 * px + 2 * py + pc

        def own_block(full):
            acc = full[:, 0:128]
            for b in range(1, N_DEV):
                acc = jnp.where(me == b, full[:, b * 128:(b + 1) * 128], acc)
            return acc

        for idx, name in enumerate(names):
            w_ref, m_ref, v_ref = wmv_refs[3 * idx:3 * idx + 3]
            if name == "sgu_w":
                grad = sum_b[None]
            else:
                _, r, l0, nr, nl = SMALL_PLACE[idx]
                grad = sum_a[r:r + nr, l0:l0 + nl]
                if name == "hg_gnorm":
                    grad = own_block(grad)
                if name == "sgu_b":
                    grad = grad[None]
            res = (grad, *_adamw(w_ref[...], grad, m_ref[...], v_ref[...]))
            for o_ref, val in zip(out_refs[4 * idx:4 * idx + 4], res):
                o_ref[...] = val

    out_shape = [jax.ShapeDtypeStruct(given[name].shape, F32) for name in names for _ in range(4)]
    res = pl.pallas_call(
        kern, name="small_adamw", out_shape=out_shape, in_specs=[vmem] * (2 + len(wmv)), out_specs=[vmem] * len(out_shape),
    )(sum_a, sum_b, *wmv)
    return {name: res[4 * idx:4 * idx + 4] for idx, name in enumerate(names)}


class _Exchange:
    def __init__(self, given):
        self.given = given
        px, py, pc = _me()
        self.core = pc.reshape(1).astype(jnp.int32)
        self.dev = (4 * px + 2 * py + pc).reshape(1).astype(jnp.int32)
        self.where = jnp.stack([2 * px + py, pc]).astype(jnp.int32)
        self.state, self.layers = {}, {}

    def start_weights(self, lands, after):
        self.weights = _split_start("weights_first_start", [], lands, 4 * len(lands), _ag_first_copies, after=after)
        self.first_token = self.weights[4]

    def weights_forward(self, after):
        send_sems, recv_sems, shards, lands, _ = self.weights
        _, lands = _split_wait("weights_first_wait", send_sems, recv_sems, shards, lands, after, _ag_first_copies)
        self.weights = _split_start("weights_second_start", [], lands, 3 * len(lands), _ag_second_copies)
        return self.weights[4]

    def weights_ready(self, after):
        send_sems, recv_sems, shards, lands, _ = self.weights
        _, got = _split_wait("weights_second_wait", send_sems, recv_sems, shards, lands, after, _ag_second_copies)
        return dict(w_in_o=got[0], w_out_o=got[1], w_ff1=[got[2], got[3]], w_ff2=[got[4], got[5]])

    def grads_start(self, tag, grads):
        blocks = [g.reshape(4, 2, *g.shape[1:]) for g in grads]
        lands = [lax.empty((4, *b.shape[2:]), F32) for b in blocks]
        self.state[tag] = _split_start(f"grads_{tag}_sibling_start", blocks, lands, 4 * len(blocks), _rs_sibling_copies)
        return self.state[tag][4]

    def grads_middle(self, tag, after):
        send_sems, recv_sems, blocks, lands, _ = self.state[tag]
        blocks, from_sibling = _split_wait(f"grads_{tag}_sibling_wait", send_sems, recv_sems, blocks, lands, [after], _rs_sibling_copies)
        sums = [_chip_sum(f"grads_{tag}_chip_sum_{k}", b, s, self.core) for k, (b, s) in enumerate(zip(blocks, from_sibling))]
        lands = [lax.empty((3, *p.shape[1:]), BF16) for p in sums]
        self.state[tag] = (blocks, from_sibling, _split_start(f"grads_{tag}_chips_start", sums, lands, 3 * len(sums), _rs_chip_copies))
        return self.state[tag][2][4]

    def grads_end(self, tag, after):
        blocks, from_sibling, (send_sems, recv_sems, sums, lands, _) = self.state[tag]
        _, from_chips = _split_wait(f"grads_{tag}_chips_wait", send_sems, recv_sems, sums, lands, [after], _rs_chip_copies)
        self.layers[tag] = list(zip(blocks, from_sibling, from_chips))


SHARDED = ("w_in_e", "w_qb", "w_kvb", "w_out_e", "w_in_o", "w_out_o", "w_ff1", "w_ff2")


def kernel(x, positions, w_in_e, mla_gq, mla_gkv, w_qb, w_kvb, sgu_ln_g, sgu_ln_b, sgu_w, sgu_b, w_out_e, w_in_o, hg_lb, hg_gnorm, w_out_o, ln1_g, ln1_b, w_ff1, w_ff2, ln2_g, ln2_b, loss_target, m_w_in_e, m_mla_gq, m_mla_gkv, m_w_qb, m_w_kvb, m_sgu_ln_g, m_sgu_ln_b, m_sgu_w, m_sgu_b, m_w_out_e, m_w_in_o, m_hg_lb, m_hg_gnorm, m_w_out_o, m_ln1_g, m_ln1_b, m_w_ff1, m_w_ff2, m_ln2_g, m_ln2_b, v_w_in_e, v_mla_gq, v_mla_gkv, v_w_qb, v_w_kvb, v_sgu_ln_g, v_sgu_ln_b, v_sgu_w, v_sgu_b, v_w_out_e, v_w_in_o, v_hg_lb, v_hg_gnorm, v_w_out_o, v_ln1_g, v_ln1_b, v_w_ff1, v_w_ff2, v_ln2_g, v_ln2_b):
    given = dict(locals())
    ex = _Exchange(given)

    names = ["w_in_e", "w_qb", "w_kvb", "w_out_e"]
    placed = _place_own([(given[n], 0, BF16) for n in names] + [(hg_gnorm.reshape(1, 1, D_MODEL // N_DEV), 0, F32)]
                        + [(w_in_o, 0, BF16), (w_out_o, 0, BF16), (w_ff1, 0, BF16), (w_ff1, 1, BF16), (w_ff2, 0, BF16), (w_ff2, 1, BF16)],
                        ex.dev)
    got = _all_gather(placed[:5])
    ex.start_weights(placed[5:], after=[got[0]])
    gw = dict(zip(names, got[:4]))
    small_names = ["mla_gq", "mla_gkv", "sgu_ln_g", "sgu_ln_b", "sgu_w", "sgu_b", "hg_lb", "ln1_g", "ln1_b", "ln2_g", "ln2_b"]
    sp = {n: given[n] for n in small_names}
    sp["hg_gnorm"] = got[4].reshape(1, D_MODEL)

    sq_err, dx, grads, gs = _local_step(x[0], positions[0], loss_target[0], gw, sp, ex)
    loss = lax.psum(0.5 * jnp.sum(sq_err) / D_MODEL, ("x", "y", "c"))

    blocks = [grads[n].reshape(4, 2, *grads[n].shape[1:]) for n in names]
    from_sibling = _rs_sibling(blocks)
    chip_sums = [_chip_sum(f"grads_l0_chip_sum_{k}", b, s, ex.core) for k, (b, s) in enumerate(zip(blocks, from_sibling))]
    from_chips = _rs_chips(chip_sums)
    per_weight = dict(zip(names, [[l] for l in zip(blocks, from_sibling, from_chips)]))
    l1, l0m = ex.layers["l1"], ex.layers["l0m"]
    per_weight.update(w_ff1=[l0m[0], l1[0]], w_ff2=[l0m[1], l1[1]], w_in_o=[l1[2]], w_out_o=[l1[3]])
    results = {n: _finish_sharded(f"finish_{n}", per_weight[n], given[n], given["m_" + n], given["v_" + n], ex.where) for n in SHARDED}

    results.update(_small_reduce_adamw(gs, given))

    order = ["w_in_e", "mla_gq", "mla_gkv", "w_qb", "w_kvb", "sgu_ln_g", "sgu_ln_b", "sgu_w", "sgu_b", "w_out_e", "w_in_o",
             "hg_lb", "hg_gnorm", "w_out_o", "ln1_g", "ln1_b", "w_ff1", "w_ff2", "ln2_g", "ln2_b"]
    return (loss, dx[None], *[results[name][kind] for kind in range(4) for name in order])
```

```python
import functools
import math

import jax
import jax.numpy as jnp
import numpy as np
from jax import lax
from jax.experimental import pallas as pl
from jax.experimental.pallas import tpu as pltpu

F32 = jnp.float32
BF16 = jnp.bfloat16
MESH = pl.DeviceIdType.MESH
HIGHEST = lax.Precision.HIGHEST

D_MODEL = 1024
D_FF = 4096
N_DEV = 8
HEADS = 8
HEAD_W = 128
MLA_NOPE = 64
MLA_ROPE = 32
MLA_V = 64
MLA_LORA = 256
MLA_SCALE = (MLA_NOPE + MLA_ROPE) ** -0.5
ROPE_BASE = 10000.0
SGU_DIM = 512
SGU_G = 4
SGU_CHUNK = 128
HG_CHUNK = 64
HG_CHUNKS_PER_STEP = 4
ALPHA = (2 * 2) ** 0.25
EPS = 1e-5
ADAM_LR, ADAM_B1, ADAM_B2, ADAM_EPS, ADAM_WD, ADAM_STEP = 0.001, 0.9, 0.999, 1e-08, 0.01, 10

VMEM_CAP_V7X = 56 * 2**20
VMEM_SLACK = 12 * 2**20
TM = 512
TN = 512


def _vmem(block_bytes):
    return int(min(VMEM_CAP_V7X, 2 * block_bytes + VMEM_SLACK))


def _hbm(a):
    return pltpu.with_memory_space_constraint(a, pltpu.HBM)


def _nbytes(shape, dtype):
    return int(np.prod([d for d in shape if d is not None])) * jnp.dtype(dtype).itemsize


def _sig(x):
    return 1.0 / (1.0 + jnp.exp(-x))


def _gelu(x):
    c = math.sqrt(2.0 / math.pi)
    t = jnp.tanh(c * (x + 0.044715 * x * x * x))
    return 0.5 * x * (1.0 + t), t


def _gelu_grad(x, t):
    c = math.sqrt(2.0 / math.pi)
    return 0.5 * (1.0 + t) + 0.5 * x * (1.0 - t * t) * c * (1.0 + 3 * 0.044715 * x * x)


def _dot(a, b, dims, precision=None):
    return lax.dot_general(a, b, (dims, ((), ())), preferred_element_type=F32, precision=precision)


NN = ((1,), (0,))
NT = ((1,), (1,))
TN_ = ((0,), (0,))


def _deps(deps):
    return [d for d in deps if d is not None]


def _tiled(name, grid, ins, outs, compute, direct=False, deps=()):
    n_in, deps = len(ins), _deps(deps)
    n_skip = n_in + len(deps)

    def kern(*refs):
        if direct:
            compute(refs[:n_in], refs[n_skip:])
            return
        for o_ref, r in zip(refs[n_skip:], compute(*refs[:n_in])):
            o_ref[...] = r.astype(o_ref.dtype).reshape(o_ref.shape)

    swap = lambda f: (lambda j, i: f(i, j))
    nbytes = sum(_nbytes(blk, a.dtype) for a, blk, _ in ins) + sum(_nbytes(blk, dt) + _nbytes(blk, F32) for _, dt, blk, _ in outs)
    res = pl.pallas_call(
        kern, name=name, grid=grid,
        in_specs=[pl.BlockSpec(blk, swap(f), pipeline_mode=pl.Buffered(1) if tuple(blk) == tuple(a.shape) else None)
                  for a, blk, f in ins] + [ANY_SPEC] * len(deps),
        out_specs=[pl.BlockSpec(blk, swap(f)) for _, _, blk, f in outs],
        out_shape=[pltpu.HBM(shape, dt) for shape, dt, _, _ in outs],
        compiler_params=pltpu.CompilerParams(dimension_semantics=("parallel", "parallel"), vmem_limit_bytes=_vmem(nbytes)),
    )(*[_hbm(a) for a, _, _ in ins], *deps)
    return res if len(res) > 1 else res[0]


def _rb(a, tm, w=None, cb=0):
    return (a, (tm, a.shape[1] if w is None else w), lambda i, j: (i, cb))


def _rbj(a, tm, tn):
    return (a, (tm, tn), lambda i, j: (i, j))


def _cw(b, tn):
    return (b, (b.shape[0], tn), lambda i, j: (0, j))


def _rw(b, tn):
    return (b, (tn, b.shape[1]), lambda i, j: (j, 0))


def _tl(a, tm):
    return (a, (a.shape[0], tm), lambda i, j: (0, i))


def _gcw(g):
    return (g, (None, g.shape[1], g.shape[2]), lambda i, j: (j, 0, 0))


def _grw(g, tn):
    return (g, (N_DEV, tn, g.shape[2]), lambda i, j: (0, j, 0))


def _out(m, n, dtype, tm, tn):
    return ((m, n), dtype, (tm, tn), lambda i, j: (i, j))


def _out_dev(k, n, tm):
    return ((N_DEV, k, n), F32, (None, tm, n), lambda i, j: (j, i, 0))


def _mmc(dims, n_pairs=1, epilogue=None):
    def compute(*refs):
        acc = None
        for k in range(n_pairs):
            d = _dot(refs[2 * k][...].astype(BF16), refs[2 * k + 1][...].astype(BF16), dims)
            acc = d if acc is None else acc + d
        ext = [r[...] for r in refs[2 * n_pairs:]]
        return epilogue(acc, *ext) if epilogue is not None else (acc,)

    return compute


def _res(w):
    return (w, w.shape, functools.partial(lambda i, j, nd: (0,) * nd, nd=w.ndim))


def _mmc_blocks(nblk, dims, rhs_block, epilogue=None):
    def compute(in_refs, out_refs):
        a = in_refs[0][...].astype(BF16)
        for d in range(nblk):
            acc = _dot(a, rhs_block(in_refs[1], d).astype(BF16), dims)
            n = acc.shape[1]
            ext = [r[:, d * n:(d + 1) * n] for r in in_refs[2:]]
            res = epilogue(acc, *ext) if epilogue is not None else (acc,)
            for o_ref, r in zip(out_refs, res):
                o_ref[:, d * n:(d + 1) * n] = r.astype(o_ref.dtype)

    return compute


def _mmc_dev(epilogue=None):
    def compute(a_ref, b_ref, *ext_refs):
        n = b_ref.shape[2]
        acc = None
        for d in range(N_DEV):
            t = _dot(a_ref[:, d * n:(d + 1) * n].astype(BF16), b_ref[d].astype(BF16), NT)
            acc = t if acc is None else acc + t
        ext = [r[...] for r in ext_refs]
        return epilogue(acc, *ext) if epilogue is not None else (acc,)

    return compute


def _rowwise(name, body, rows, consts, out_rows, out_accs=(), tr=512, deps=()):
    T = rows[0][0].shape[0]
    tr = min(tr, T)
    deps = _deps(deps)
    nr, ncn, no, nd = len(rows), len(consts), len(out_rows), len(deps)

    def kern(*refs):
        accs = refs[nr + ncn + nd + no:]
        if accs:
            @pl.when(pl.program_id(0) == 0)
            def _():
                for a in accs:
                    a[...] = jnp.zeros(a.shape, a.dtype)
        body(refs[:nr], refs[nr:nr + ncn], refs[nr + ncn + nd:nr + ncn + nd + no], accs)

    in_specs = [pl.BlockSpec((tr, w), functools.partial(lambda i, cb: (i, cb), cb=cb)) for _, w, cb in rows]
    in_specs += [pl.BlockSpec(c.shape, functools.partial(lambda i, nd: (0,) * nd, nd=c.ndim), pipeline_mode=pl.Buffered(1))
                 for c in consts]
    in_specs += [ANY_SPEC] * nd
    out_specs = [pl.BlockSpec((tr, w), lambda i: (i, 0)) for w, _ in out_rows]
    out_specs += [pl.BlockSpec(s, functools.partial(lambda i, nd: (0,) * nd, nd=len(s))) for s, _ in out_accs]
    out_shape = [pltpu.HBM((T, w), dt) for w, dt in out_rows]
    out_shape += [pltpu.HBM(s, dt) for s, dt in out_accs]
    nbytes = sum(_nbytes((tr, w), a.dtype) for a, w, _ in rows) + sum(_nbytes(c.shape, c.dtype) for c in consts)
    nbytes += sum(_nbytes((tr, w), dt) for w, dt in out_rows) + sum(_nbytes(s, dt) for s, dt in out_accs)
    res = pl.pallas_call(
        kern, name=name, grid=(T // tr,), in_specs=in_specs, out_specs=out_specs, out_shape=out_shape,
        compiler_params=pltpu.CompilerParams(dimension_semantics=("arbitrary",), vmem_limit_bytes=_vmem(nbytes)),
    )(*[_hbm(a) for a, _, _ in rows], *[_hbm(c) for c in consts], *deps)
    return res if len(res) > 1 else res[0]


def _full(a):
    return (a, a.shape[1], 0)


def _ln_stats(y):
    mu = jnp.mean(y, axis=-1, keepdims=True)
    yc = y - mu
    r = lax.rsqrt(jnp.mean(yc * yc, axis=-1, keepdims=True) + EPS)
    return yc * r, r


def _ln_back(dh, xh, r, gain, dg_ref, db_ref):
    dg_ref[...] += jnp.sum(dh * xh, axis=0, keepdims=True)
    db_ref[...] += jnp.sum(dh, axis=0, keepdims=True)
    dx = dh * gain
    return r * (dx - jnp.mean(dx, axis=-1, keepdims=True) - xh * jnp.mean(dx * xh, axis=-1, keepdims=True))


def _proj_ln(name, acts, weights, h_in, g, b, layer, deps=()):
    n = len(acts)

    def body(rows, consts, outs, accs):
        acc = None
        for k in range(n):
            d = _dot(rows[k][...].astype(BF16), consts[k][...], NN)
            acc = d if acc is None else acc + d
        y = ALPHA * rows[n][...] + acc
        xh, _ = _ln_stats(y)
        h = xh * consts[n][layer:layer + 1, :] + consts[n + 1][layer:layer + 1, :]
        outs[0][...] = y
        outs[1][...] = h
        outs[2][...] = h.astype(BF16)

    return _rowwise(name, body, [_full(a) for a in acts] + [_full(h_in)], [*weights, g, b],
                    [(D_MODEL, F32), (D_MODEL, F32), (D_MODEL, BF16)], tr=TM, deps=deps)


def _proj_ln_loss(name, act, w2, h_in, g, b, layer, target):
    def body(rows, consts, outs, accs):
        y = ALPHA * rows[1][...] + _dot(rows[0][...], consts[0][...], NN)
        xh, r = _ln_stats(y)
        gain = consts[1][layer:layer + 1, :]
        err = xh * gain + consts[2][layer:layer + 1, :] - rows[2][...]
        accs[0][...] += jnp.sum(err * err, axis=0, keepdims=True)
        dy = _ln_back(err * (1.0 / D_MODEL), xh, r, gain, accs[1], accs[2])
        outs[0][...] = dy
        outs[1][...] = dy.astype(BF16)

    return _rowwise(name, body, [_full(act), _full(h_in), _full(target)], [w2, g, b], [(D_MODEL, F32), (D_MODEL, BF16)],
                    [((1, D_MODEL), F32)] * 3, tr=TM)


def _dh_ln_back(name, da, w, dy_next, y, g, layer, proj=(), deps=()):
    def body(rows, consts, outs, accs):
        n = consts[0].shape[2]
        acc = ALPHA * rows[1][...]
        for d in range(N_DEV):
            acc = acc + _dot(rows[0][:, d * n:(d + 1) * n], consts[0][d], NT)
        xh, r = _ln_stats(rows[2][...])
        dy = _ln_back(acc, xh, r, consts[1][layer:layer + 1, :], accs[0], accs[1])
        outs[0][...] = dy
        dy_bf = dy.astype(BF16)
        outs[1][...] = dy_bf
        off = 0
        for k, p in enumerate(proj):
            outs[2][:, off:off + p.shape[0]] = _dot(dy_bf, consts[2 + k][...], NT).astype(BF16)
            off += p.shape[0]

    out_rows = [(D_MODEL, F32), (D_MODEL, BF16)] + ([(sum(p.shape[0] for p in proj), BF16)] if proj else [])
    return _rowwise(name, body, [_full(da), _full(dy_next), _full(y)], [w, g, *proj], out_rows,
                    [((1, D_MODEL), F32)] * 2, tr=TM, deps=deps)


def _relu2_epilogue(acc):
    a = jnp.maximum(acc, 0.0)
    return acc, a * a


def _mlp_up(tag, h_bf, w1):
    T = h_bf.shape[0]
    tm = min(TM, T)
    return _tiled(f"{tag}_ff1", (1, T // tm), [_rb(h_bf, tm), _res(w1)],
                  [_out(T, D_FF, BF16, tm, D_FF), _out(T, D_FF, BF16, tm, D_FF)],
                  _mmc_blocks(N_DEV, NN, lambda w, d: w[d], epilogue=_relu2_epilogue), direct=True)


def _mlp_bwd_w(tag, h_bf, a, act, dff_bf, w2, deps=()):
    T = h_bf.shape[0]
    tm = min(TM, T)
    da = _tiled(f"{tag}_dact", (1, T // tm), [_rb(dff_bf, tm), _res(w2), _rb(a, tm)], [_out(T, D_FF, BF16, tm, D_FF)],
                _mmc_blocks(N_DEV, NT, lambda w, d: w[d], epilogue=lambda acc, a_t: (acc * 2.0 * jnp.maximum(a_t.astype(F32), 0.0),)),
                direct=True, deps=deps)
    dw2 = _tiled(f"{tag}_dw2", (1, D_FF // TM), [_tl(act, TM), _res(dff_bf)],
                 [_out(D_FF, D_MODEL, F32, TM, D_MODEL)], _mmc(TN_)).reshape(N_DEV, D_FF // N_DEV, D_MODEL)
    dw1 = _tiled(f"{tag}_dw1", (N_DEV, 1), [_res(h_bf), _cw(da, TN)], [_out_dev(D_MODEL, TN, D_MODEL)], _mmc(TN_))
    return da, dw1, dw2


def _rope_tables(positions_col, invf_lane):
    def body(rows, consts, outs, accs):
        ang = rows[0][...].astype(F32) * consts[0][...]
        c, s = jnp.cos(ang), jnp.sin(ang)
        lane = lax.broadcasted_iota(jnp.int32, ang.shape, 1)
        outs[0][...] = jnp.where(lane < 64, 1.0, jnp.where(lane < 96, c, 0.0))
        outs[1][...] = jnp.where((lane >= 64) & (lane < 80), -s, 0.0)
        outs[2][...] = jnp.where((lane >= 80) & (lane < 96), s, 0.0)

    return _rowwise("rope_tables", body, [_full(positions_col)], [invf_lane], [(HEAD_W, F32)] * 3)


def _rope(x, c, s1, s2):
    return x * c + pltpu.roll(x, 112, 1) * s1 + pltpu.roll(x, 16, 1) * s2


def _rope_t(dx, c, s1, s2):
    return dx * c + pltpu.roll(dx * s1, 16, 1) + pltpu.roll(dx * s2, 112, 1)


def _rms(c):
    r = lax.rsqrt(jnp.mean(c * c, axis=-1, keepdims=True) + EPS)
    return c * r, r


def _rope_heads(x, c, s1, s2, fn):
    return jnp.concatenate([fn(x[:, h * HEAD_W:(h + 1) * HEAD_W], c, s1, s2) for h in range(HEADS)], axis=1)


def _mla_in(x, wm, ws, tabs, gq, gkv, deps=()):
    def body(rows, consts, outs, accs):
        xb = rows[0][...].astype(BF16)
        zm = _dot(xb, consts[0][...], NN)
        outs[0][...] = zm
        outs[1][...] = _dot(xb, consts[1][...], NN)
        outs[2][...] = (_rms(zm[:, 0:256])[0] * consts[2][...]).astype(BF16)
        outs[3][...] = (_rms(zm[:, 256:512])[0] * consts[3][...]).astype(BF16)
        outs[4][...] = _rope(zm[:, 512:640], rows[1][...], rows[2][...], rows[3][...])

    return _rowwise("l0_in", body, [_full(x)] + [_full(t) for t in tabs], [wm, ws, gq, gkv],
                    [(640, F32), (1024, F32), (256, BF16), (256, BF16), (HEAD_W, F32)], deps=deps)


def _mla_qkv(cqn, ckvn, kr_rot, tabs, wq, wk, wv):
    def body(rows, consts, outs, accs):
        c, s1, s2 = rows[3][...], rows[4][...], rows[5][...]
        outs[0][...] = _rope_heads(_dot(rows[0][...], consts[0][...], NN), c, s1, s2, _rope).astype(BF16)
        outs[1][...] = (_dot(rows[1][...], consts[1][...], NN) + jnp.concatenate([rows[2][...]] * HEADS, axis=1)).astype(BF16)
        outs[2][...] = _dot(rows[1][...], consts[2][...], NN).astype(BF16)

    rows = [_full(cqn), _full(ckvn), _full(kr_rot)] + [_full(t) for t in tabs]
    return _rowwise("l0_qkv", body, rows, [wq, wk, wv], [(HEADS * HEAD_W, BF16)] * 3)


def _mla_back(zm, cqn, ckvn, tabs, gq, gkv, wq, wk, wv, dq, dk, dv):
    def body(rows, consts, outs, accs):
        c, s1, s2 = rows[4][...], rows[5][...], rows[6][...]
        dk_t, dv_bf = rows[8][...], rows[9][...].astype(BF16)
        dq_bf = _rope_heads(rows[7][...], c, s1, s2, _rope_t).astype(BF16)
        dk_bf = dk_t.astype(BF16)
        accs[0][...] += _dot(rows[2][...], dq_bf, TN_)
        accs[1][...] += _dot(rows[3][...], dk_bf, TN_)
        accs[2][...] += _dot(rows[3][...], dv_bf, TN_)
        dlat = [_dot(dq_bf, consts[2][...], NT), _dot(dk_bf, consts[3][...], NT) + _dot(dv_bf, consts[4][...], NT)]
        for k in range(2):
            ch, r = _rms(rows[k][...])
            accs[3 + k][...] += jnp.sum(dlat[k] * ch, axis=0, keepdims=True)
            dc = dlat[k] * consts[k][...]
            outs[0][:, 256 * k:256 * (k + 1)] = (r * (dc - ch * jnp.mean(dc * ch, axis=-1, keepdims=True))).astype(BF16)
        dks = dk_t[:, 0:HEAD_W]
        for h in range(1, HEADS):
            dks = dks + dk_t[:, h * HEAD_W:(h + 1) * HEAD_W]
        lane = lax.broadcasted_iota(jnp.int32, dks.shape, 1)
        dks = jnp.where((lane >= 64) & (lane < 96), dks, 0.0)
        outs[0][:, 512:640] = _rope_t(dks, c, s1, s2).astype(BF16)

    rows = [(zm, 256, 0), (zm, 256, 1), _full(cqn), _full(ckvn)] + [_full(t) for t in tabs] + [_full(dq), _full(dk), _full(dv)]
    wide = HEADS * HEAD_W
    return _rowwise("l0_mla_back", body, rows, [gq, gkv, wq, wk, wv], [(640, BF16)],
                    [((MLA_LORA, wide), F32)] * 3 + [((1, MLA_LORA), F32)] * 2, tr=256)


def _in_back(x, dzm, dzs, dy, wm, ws, deps=()):
    def body(rows, consts, outs, accs):
        dzm_t, dzs_t = rows[1][...], rows[2][...]
        outs[0][...] = _dot(dzm_t, consts[0][...], NT) + _dot(dzs_t, consts[1][...], NT) + ALPHA * rows[3][...]
        xb = rows[0][...].astype(BF16)
        accs[0][...] += _dot(xb, dzm_t, TN_)
        accs[1][...] += _dot(xb, dzs_t, TN_)

    return _rowwise("l0_in_back", body, [_full(x), _full(dzm), _full(dzs), _full(dy)], [wm, ws], [(D_MODEL, F32)],
                    [((D_MODEL, 640), F32), ((D_MODEL, 1024), F32)], deps=deps)


def _out_weight_grads(o_att, b_out, dy_bf):
    def body(rows, consts, outs, accs):
        d = rows[2][...]
        accs[0][...] += _dot(rows[0][...].astype(BF16), d, TN_)
        accs[1][...] += _dot(rows[1][...], d, TN_)

    return _rowwise("l0_dw_out", body, [_full(o_att), _full(b_out), _full(dy_bf)], [], [],
                    [((HEADS * HEAD_W, D_MODEL), F32), ((SGU_DIM, D_MODEL), F32)])


def _attn_block(T):
    return min(1024, T)


def _attn_fwd(q, k, v):
    T = q.shape[0]
    BQ = _attn_block(T)
    nq = T // BQ

    def kern(q_ref, k_ref, v_ref, o_ref, lse_ref):
        def step(i, j, carry, masked):
            m, l, acc = carry
            qb = q_ref[pl.ds(pl.multiple_of(i * BQ, BQ), BQ), :]
            kb = k_ref[pl.ds(pl.multiple_of(j * BQ, BQ), BQ), :]
            vb = v_ref[pl.ds(pl.multiple_of(j * BQ, BQ), BQ), :]
            s = _dot(qb, kb, NT) * MLA_SCALE
            if masked:
                row = lax.broadcasted_iota(jnp.int32, s.shape, 0)
                col = lax.broadcasted_iota(jnp.int32, s.shape, 1)
                s = jnp.where(col <= row, s, -1e30)
            m_new = jnp.maximum(m, jnp.max(s, axis=-1, keepdims=True))
            p = jnp.exp(s - m_new)
            a = jnp.exp(m - m_new)
            l = a * l + jnp.sum(p, axis=-1, keepdims=True)
            acc = a * acc + _dot(p.astype(BF16), vb, NN)
            return m_new, l, acc

        def qloop(i, _):
            init = (jnp.full((BQ, 1), -1e30, F32), jnp.zeros((BQ, 1), F32), jnp.zeros((BQ, HEAD_W), F32))
            carry = lax.fori_loop(0, i, lambda j, c: step(i, j, c, False), init)
            m, l, acc = step(i, i, carry, True)
            rows = pl.ds(pl.multiple_of(i * BQ, BQ), BQ)
            o_ref[rows, :] = acc / l
            lse_ref[0, rows, :] = m + jnp.log(l)
            return 0

        lax.fori_loop(0, nq, qloop, 0)

    head = pl.BlockSpec((T, HEAD_W), lambda h: (0, h))
    nbytes = 3 * _nbytes((T, HEAD_W), BF16) + _nbytes((T, HEAD_W), F32) + _nbytes((T, 128), F32)
    return pl.pallas_call(
        kern, name="attn_fwd", grid=(HEADS,), in_specs=[head, head, head],
        out_specs=[head, pl.BlockSpec((1, T, 1), lambda h: (h, 0, 0))],
        out_shape=[pltpu.HBM((T, HEADS * HEAD_W), F32), pltpu.HBM((HEADS, T, 1), F32)],
        compiler_params=pltpu.CompilerParams(dimension_semantics=("parallel",), vmem_limit_bytes=_vmem(nbytes)),
    )(_hbm(q), _hbm(k), _hbm(v))


def _attn_bwd(q, k, v, o, lse, dcat, deps=()):
    T = q.shape[0]
    BQ = _attn_block(T)
    nq = T // BQ
    deps = _deps(deps)

    def kern(q_ref, k_ref, v_ref, o_ref, lse_ref, do_ref, *rest):
        dq_ref, dk_ref, dv_ref, dd_ref = rest[len(deps):]
        dq_ref[...] = jnp.zeros(dq_ref.shape, F32)

        def dloop(i, _):
            rows = pl.ds(pl.multiple_of(i * BQ, BQ), BQ)
            dd_ref[rows, :] = jnp.sum(do_ref[rows, :].astype(F32) * o_ref[rows, :], axis=-1, keepdims=True)
            return 0

        lax.fori_loop(0, nq, dloop, 0)

        def step(j, i, carry, masked):
            dk_acc, dv_acc = carry
            rq = pl.ds(pl.multiple_of(i * BQ, BQ), BQ)
            rk = pl.ds(pl.multiple_of(j * BQ, BQ), BQ)
            qb, kb, vb, dob = q_ref[rq, :], k_ref[rk, :], v_ref[rk, :], do_ref[rq, :]
            s = _dot(qb, kb, NT) * MLA_SCALE
            p = jnp.exp(s - lse_ref[0, rq, :])
            if masked:
                row = lax.broadcasted_iota(jnp.int32, s.shape, 0)
                col = lax.broadcasted_iota(jnp.int32, s.shape, 1)
                p = jnp.where(col <= row, p, 0.0)
            dp = _dot(dob, vb, NT)
            ds = (p * (dp - dd_ref[rq, :]) * MLA_SCALE).astype(BF16)
            dv_acc = dv_acc + _dot(p.astype(BF16), dob, TN_)
            dk_acc = dk_acc + _dot(ds, qb, TN_)
            dq_ref[rq, :] += _dot(ds, kb, NN)
            return dk_acc, dv_acc

        def kloop(j, _):
            init = (jnp.zeros((BQ, HEAD_W), F32), jnp.zeros((BQ, HEAD_W), F32))
            carry = step(j, j, init, True)
            dk_acc, dv_acc = lax.fori_loop(j + 1, nq, lambda i, c: step(j, i, c, False), carry)
            rk = pl.ds(pl.multiple_of(j * BQ, BQ), BQ)
            dk_ref[rk, :] = dk_acc
            dv_ref[rk, :] = dv_acc
            return 0

        lax.fori_loop(0, nq, kloop, 0)

    head = pl.BlockSpec((T, HEAD_W), lambda h: (0, h))
    nbytes = 4 * _nbytes((T, HEAD_W), BF16) + 5 * _nbytes((T, HEAD_W), F32) + 2 * _nbytes((T, 128), F32)
    return pl.pallas_call(
        kern, name="attn_bwd", grid=(HEADS,),
        in_specs=[head, head, head, head, pl.BlockSpec((1, T, 1), lambda h: (h, 0, 0)), head] + [ANY_SPEC] * len(deps),
        out_specs=[head, head, head],
        out_shape=[pltpu.HBM((T, HEADS * HEAD_W), F32)] * 3,
        scratch_shapes=[pltpu.VMEM((T, 1), F32)],
        compiler_params=pltpu.CompilerParams(dimension_semantics=("parallel",), vmem_limit_bytes=_vmem(nbytes)),
    )(*[_hbm(a) for a in (q, k, v, o, lse, dcat)], *deps)


def _sgu_common(u, v, ln_g, ln_b):
    ua, tu = _gelu(u)
    va, tv = _gelu(v)
    vh, r = _ln_stats(va)
    return ua, tu, tv, vh, r, vh * ln_g + ln_b


def _tril_mask(n):
    return lax.broadcasted_iota(jnp.int32, (n, n), 1) <= lax.broadcasted_iota(jnp.int32, (n, n), 0)


def _sgu_fwd(zs, ln_g, ln_b, w, bias_full):
    def body(rows, consts, outs, accs):
        ua, _, _, _, _, vn = _sgu_common(rows[0][...], rows[1][...], consts[0][...], consts[1][...])
        vn = vn.astype(BF16)
        tri = _tril_mask(SGU_CHUNK)
        for g in range(SGU_G):
            wg = jnp.where(tri, consts[2][0, g], 0.0).astype(BF16)
            cols = slice(g * 128, (g + 1) * 128)
            for c in range(ua.shape[0] // SGU_CHUNK):
                rws = slice(c * SGU_CHUNK, (c + 1) * SGU_CHUNK)
                mixed = _dot(wg, vn[rws, cols], NN) + consts[3][:, cols]
                outs[0][rws, cols] = (ua[rws, cols] * mixed).astype(BF16)

    return _rowwise("sgu_fwd", body, [(zs, 512, 0), (zs, 512, 1)], [ln_g, ln_b, w, bias_full], [(SGU_DIM, BF16)])


def _sgu_bwd(zs, dcat, ln_g, ln_b, w, bias_full):
    def body(rows, consts, outs, accs):
        u, v = rows[0][...], rows[1][...]
        ua, tu, tv, vh, r, vn = _sgu_common(u, v, consts[0][...], consts[1][...])
        dout = rows[2][...].astype(F32)
        vn_bf = vn.astype(BF16)
        tri = _tril_mask(SGU_CHUNK)
        dmixed = (dout * ua)
        dmixed_bf = dmixed.astype(BF16)
        ones = jnp.ones((8, SGU_CHUNK), F32)
        dvn_cols, mixed_cols = [], []
        for g in range(SGU_G):
            wg = jnp.where(tri, consts[2][0, g], 0.0).astype(BF16)
            cols = slice(g * 128, (g + 1) * 128)
            dvn_rows, mixed_rows = [], []
            dw = jnp.zeros((SGU_CHUNK, SGU_CHUNK), F32)
            dmix_sum = jnp.zeros((SGU_CHUNK, 128), F32)
            for c in range(u.shape[0] // SGU_CHUNK):
                rws = slice(c * SGU_CHUNK, (c + 1) * SGU_CHUNK)
                mixed_rows.append(_dot(wg, vn_bf[rws, cols], NN) + consts[3][:, cols])
                dvn_rows.append(_dot(wg, dmixed_bf[rws, cols], TN_))
                dw = dw + _dot(dmixed_bf[rws, cols], vn_bf[rws, cols], NT)
                dmix_sum = dmix_sum + dmixed[rws, cols]
            accs[0][g] += jnp.where(tri, dw, 0.0)
            accs[3][g:g + 1, :] += _dot(ones, dmix_sum, NT, precision=HIGHEST)[0:1, :]
            dvn_cols.append(jnp.concatenate(dvn_rows, axis=0))
            mixed_cols.append(jnp.concatenate(mixed_rows, axis=0))
        dvn = jnp.concatenate(dvn_cols, axis=1)
        mixed = jnp.concatenate(mixed_cols, axis=1)
        accs[1][...] += jnp.sum(dvn * vh, axis=0, keepdims=True)
        accs[2][...] += jnp.sum(dvn, axis=0, keepdims=True)
        dvh = dvn * consts[0][...]
        dva = r * (dvh - jnp.mean(dvh, axis=-1, keepdims=True) - vh * jnp.mean(dvh * vh, axis=-1, keepdims=True))
        outs[0][:, 0:512] = (dout * mixed * _gelu_grad(u, tu)).astype(BF16)
        outs[0][:, 512:1024] = (dva * _gelu_grad(v, tv)).astype(BF16)

    return _rowwise("sgu_bwd", body, [(zs, 512, 0), (zs, 512, 1), (dcat, 512, 2)], [ln_g, ln_b, w, bias_full], [(1024, BF16)],
                    [((SGU_G, 128, 128), F32), ((1, SGU_DIM), F32), ((1, SGU_DIM), F32), ((SGU_G, 128), F32)], tr=256)


def _lower_bound(hg_lb):
    a0, a1 = hg_lb[0:1, :], hg_lb[1:2, :]
    m = jnp.maximum(a0, a1)
    e0, e1 = jnp.exp(a0 - m), jnp.exp(a1 - m)
    s0, s1 = e0 / (e0 + e1), e1 / (e0 + e1)
    return (s0 + s1) - s0, s0, s1


def _prefix_rows(x, reverse=False):
    n = x.shape[0]
    row = lax.broadcasted_iota(jnp.int32, x.shape, 0)
    s = 1
    while s < n:
        if reverse:
            x = x + jnp.where(row < n - s, pltpu.roll(x, n - s, 0), 0.0)
        else:
            x = x + jnp.where(row >= s, pltpu.roll(x, s, 0), 0.0)
        s *= 2
    return x


def _hg_gates(qr, fr, lb):
    C = qr.shape[0]
    sq = _sig(qr)
    qf = qr * sq
    sf = _sig(fr)
    gate = lb + (1.0 - lb) * sf
    kk = 1.0 - gate
    tri = _tril_mask(C)
    b = _prefix_rows(jnp.log(gate))
    bref = b[C // 2 - 1:C // 2, :]
    bl = b[C - 1:C, :]
    e_b = jnp.exp(b)
    e_q = jnp.exp(b - bref)
    e_k = jnp.exp(bref - b)
    e_lb = jnp.exp(bl - b)
    return dict(sq=sq, qf=qf, sf=sf, gate=gate, kk=kk, tri=tri, bl=bl, e_b=e_b, e_q=e_q, e_k=e_k, e_lb=e_lb)


def _hgrn_fwd(z1, hg_lb, gnorm):
    T = z1.shape[0]
    C = min(HG_CHUNK, T)
    nc = T // C
    ns = HG_CHUNKS_PER_STEP if nc % HG_CHUNKS_PER_STEP == 0 else 1
    R = ns * C

    def kern(q_ref, f_ref, i_ref, g_ref, lb_ref, gn_ref, o_ref, hg_ref, st_ref, s_scr):
        @pl.when(pl.program_id(0) == 0)
        def _():
            s_scr[...] = jnp.zeros(s_scr.shape, F32)

        lb_all, _, _ = _lower_bound(lb_ref[...])
        for sub in range(ns):
            rows = slice(sub * C, (sub + 1) * C)
            st_ref[sub] = s_scr[...]
            for h in range(HEADS):
                cols = slice(h * HEAD_W, (h + 1) * HEAD_W)
                t = _hg_gates(q_ref[rows, cols], f_ref[rows, cols], lb_all[:, cols])
                v_bf = i_ref[rows, cols].astype(BF16)
                st = s_scr[h]
                a = jnp.where(t["tri"], _dot((t["qf"] * t["e_q"]).astype(BF16), (t["kk"] * t["e_k"]).astype(BF16), NT), 0.0)
                o = _dot(a.astype(BF16), v_bf, NN) + _dot((t["qf"] * t["e_b"]).astype(BF16), st.astype(BF16), NT)
                s_scr[h] = st * jnp.exp(t["bl"]) + _dot(v_bf, (t["kk"] * t["e_lb"]).astype(BF16), TN_)
                o_ref[rows, cols] = o
                gr = g_ref[rows, cols]
                r = lax.rsqrt(jnp.mean(o * o, axis=-1, keepdims=True) + EPS)
                hg_ref[rows, cols] = (o * r * gn_ref[:, cols] * (gr * _sig(gr))).astype(BF16)

    seg = lambda k: pl.BlockSpec((R, D_MODEL), functools.partial(lambda n, k: (n, k), k=k))
    row = pl.BlockSpec((R, D_MODEL), lambda n: (n, 0))
    nbytes = 6 * _nbytes((R, D_MODEL), F32) + (2 + ns) * _nbytes((HEADS, 128, 128), F32)
    return pl.pallas_call(
        kern, name="hgrn_fwd", grid=(nc // ns,),
        in_specs=[seg(0), seg(1), seg(2), seg(3), pl.BlockSpec((2, D_MODEL), lambda n: (0, 0)),
                  pl.BlockSpec((1, D_MODEL), lambda n: (0, 0))],
        out_specs=[row, row, pl.BlockSpec((ns, HEADS, 128, 128), lambda n: (n, 0, 0, 0))],
        out_shape=[pltpu.HBM((T, D_MODEL), F32), pltpu.HBM((T, D_MODEL), BF16),
                   pltpu.HBM((nc, HEADS, 128, 128), F32)],
        scratch_shapes=[pltpu.VMEM((HEADS, 128, 128), F32)],
        compiler_params=pltpu.CompilerParams(dimension_semantics=("arbitrary",), vmem_limit_bytes=_vmem(nbytes)),
    )(*[_hbm(a) for a in (z1, z1, z1, z1, hg_lb, gnorm)])


def _hgrn_bwd(z1, o_pre, dhg, states, hg_lb, gnorm):
    T = z1.shape[0]
    C = min(HG_CHUNK, T)
    nc = T // C
    ns = HG_CHUNKS_PER_STEP if nc % HG_CHUNKS_PER_STEP == 0 else 1
    R, steps = ns * C, nc // ns

    def kern(q_ref, f_ref, i_ref, g_ref, o_ref, dhg_ref, st_ref, lb_ref, gn_ref, dz_ref, dlb_ref, dgn_ref, ds_scr, dlb_scr):
        n = pl.program_id(0)

        @pl.when(n == 0)
        def _():
            ds_scr[...] = jnp.zeros(ds_scr.shape, F32)
            dlb_scr[...] = jnp.zeros(dlb_scr.shape, F32)
            dgn_ref[...] = jnp.zeros(dgn_ref.shape, F32)

        lb_all, s0, s1 = _lower_bound(lb_ref[...])
        for sub in reversed(range(ns)):
            rows = slice(sub * C, (sub + 1) * C)
            for h in range(HEADS):
                cols = slice(h * HEAD_W, (h + 1) * HEAD_W)
                lb = lb_all[:, cols]
                qr, fr = q_ref[rows, cols], f_ref[rows, cols]
                t = _hg_gates(qr, fr, lb)
                tri = t["tri"]
                v_bf = i_ref[rows, cols].astype(BF16)
                st_bf = st_ref[sub, h].astype(BF16)
                dst = ds_scr[h]
                dst_bf = dst.astype(BF16)
                o = o_ref[rows, cols]
                gr = g_ref[rows, cols]
                sg = _sig(gr)
                sil = gr * sg
                gn = gn_ref[:, cols]
                r = lax.rsqrt(jnp.mean(o * o, axis=-1, keepdims=True) + EPS)
                on = o * r
                dh = dhg_ref[rows, cols].astype(F32)
                dgn_ref[:, cols] += jnp.sum(dh * on * sil, axis=0, keepdims=True)
                dg = dh * on * gn * (sg * (1.0 + gr * (1.0 - sg)))
                don = dh * gn * sil
                do_bf = (r * (don - on * jnp.mean(don * on, axis=-1, keepdims=True))).astype(BF16)
                qe = (t["qf"] * t["e_q"]).astype(BF16)
                ke = (t["kk"] * t["e_k"]).astype(BF16)
                qb = (t["qf"] * t["e_b"]).astype(BF16)
                kh_bf = (t["kk"] * t["e_lb"]).astype(BF16)
                a_bf = jnp.where(tri, _dot(qe, ke, NT), 0.0).astype(BF16)
                da_bf = jnp.where(tri, _dot(do_bf, v_bf, NT), 0.0).astype(BF16)
                dv = _dot(a_bf, do_bf, TN_) + _dot(kh_bf, dst_bf, NT)
                dqe = _dot(da_bf, ke, NN)
                dqb = _dot(do_bf, st_bf, NN)
                dke = _dot(da_bf, qe, TN_)
                dkh = _dot(v_bf, dst_bf, NN)
                dqf = dqe * t["e_q"] + dqb * t["e_b"]
                dkk = dke * t["e_k"] + dkh * t["e_lb"]
                kh_r = kh_bf.astype(F32)
                db = qe.astype(F32) * dqe - ke.astype(F32) * dke + qb.astype(F32) * dqb - kh_r * dkh
                e_bl = jnp.exp(t["bl"])
                dbl = jnp.sum(dkh * kh_r, axis=0, keepdims=True) + e_bl * jnp.sum(st_ref[sub, h] * dst, axis=0, keepdims=True)
                dlg = _prefix_rows(db, reverse=True) + dbl
                ds_scr[h] = dst * e_bl + _dot(do_bf, qb, TN_)
                dgate = dlg / t["gate"] - dkk
                sf = t["sf"]
                dlb_scr[:, cols] += jnp.sum(dgate * (1.0 - sf), axis=0, keepdims=True)
                df = dgate * (1.0 - lb) * sf * (1.0 - sf)
                dq = dqf * (t["sq"] * (1.0 + qr * (1.0 - t["sq"])))
                dz_ref[rows, cols] = dq.astype(BF16)
                dz_ref[rows, D_MODEL + h * HEAD_W:D_MODEL + (h + 1) * HEAD_W] = df.astype(BF16)
                dz_ref[rows, 2 * D_MODEL + h * HEAD_W:2 * D_MODEL + (h + 1) * HEAD_W] = dv.astype(BF16)
                dz_ref[rows, 3 * D_MODEL + h * HEAD_W:3 * D_MODEL + (h + 1) * HEAD_W] = dg.astype(BF16)

        @pl.when(n == steps - 1)
        def _():
            d = s0 * s1 * dlb_scr[...]
            dlb_ref[0:1, :] = -d
            dlb_ref[1:2, :] = d

    seg = lambda k: pl.BlockSpec((R, D_MODEL), functools.partial(lambda n, k: (steps - 1 - n, k), k=k))
    nbytes = 6 * _nbytes((R, D_MODEL), F32) + _nbytes((R, 4 * D_MODEL), BF16) + (2 + ns) * _nbytes((HEADS, 128, 128), F32)
    return pl.pallas_call(
        kern, name="hgrn_bwd", grid=(steps,),
        in_specs=[seg(0), seg(1), seg(2), seg(3), seg(0), seg(0),
                  pl.BlockSpec((ns, HEADS, 128, 128), lambda n: (steps - 1 - n, 0, 0, 0)),
                  pl.BlockSpec((2, D_MODEL), lambda n: (0, 0)), pl.BlockSpec((1, D_MODEL), lambda n: (0, 0))],
        out_specs=[pl.BlockSpec((R, 4 * D_MODEL), lambda n: (steps - 1 - n, 0)),
                   pl.BlockSpec((2, D_MODEL), lambda n: (0, 0)), pl.BlockSpec((1, D_MODEL), lambda n: (0, 0))],
        out_shape=[pltpu.HBM((T, 4 * D_MODEL), BF16), pltpu.HBM((2, D_MODEL), F32),
                   pltpu.HBM((1, D_MODEL), F32)],
        scratch_shapes=[pltpu.VMEM((HEADS, 128, 128), F32), pltpu.VMEM((1, D_MODEL), F32)],
        compiler_params=pltpu.CompilerParams(dimension_semantics=("arbitrary",), vmem_limit_bytes=_vmem(nbytes)),
    )(*[_hbm(a) for a in (z1, z1, z1, z1, o_pre, dhg, states, hg_lb, gnorm)])


def _prep_weights(gw):
    w_in_e = gw["w_in_e"].transpose(1, 0, 2).reshape(D_MODEL, 1568)
    kr = jnp.pad(w_in_e[:, 512:544], ((0, 0), (64, 32)))
    wm = jnp.concatenate([w_in_e[:, 0:512], kr], axis=1)
    ws = w_in_e[:, 544:1568]
    w_qb = gw["w_qb"].transpose(1, 0, 2).reshape(MLA_LORA, HEADS, 96)
    wq = jnp.pad(w_qb, ((0, 0), (0, 0), (0, 32))).reshape(MLA_LORA, HEADS * HEAD_W)
    kvb = gw["w_kvb"].transpose(1, 0, 2).reshape(MLA_LORA, HEADS, 128)
    wk = jnp.pad(kvb[:, :, :64], ((0, 0), (0, 0), (0, 64))).reshape(MLA_LORA, HEADS * HEAD_W)
    wv = jnp.pad(kvb[:, :, 64:], ((0, 0), (0, 0), (0, 64))).reshape(MLA_LORA, HEADS * HEAD_W)
    w_out_e = gw["w_out_e"].reshape(D_MODEL, D_MODEL)
    woa = jnp.pad(w_out_e[:512].reshape(HEADS, 64, D_MODEL), ((0, 0), (0, 64), (0, 0))).reshape(HEADS * HEAD_W, D_MODEL)
    return dict(wm=wm, ws=ws, wq=wq, wk=wk, wv=wv, woa=woa, wob=w_out_e[512:])


def _unprep_grads(g):
    dwm, dws = g["wm"], g["ws"]
    d_in_e = jnp.concatenate([dwm[:, 0:512], dwm[:, 512 + 64:512 + 96], dws], axis=1)
    d_qb = g["wq"].reshape(MLA_LORA, HEADS, HEAD_W)[:, :, :96].reshape(MLA_LORA, HEADS * 96)
    dk = g["wk"].reshape(MLA_LORA, HEADS, HEAD_W)[:, :, :64]
    dv = g["wv"].reshape(MLA_LORA, HEADS, HEAD_W)[:, :, :64]
    d_kvb = jnp.concatenate([dk, dv], axis=2).reshape(MLA_LORA, HEADS * 128)
    d_oa = g["woa"].reshape(HEADS, HEAD_W, D_MODEL)[:, :64].reshape(HEADS * 64, D_MODEL)
    dev_major = lambda a: a.reshape(a.shape[0], N_DEV, a.shape[1] // N_DEV).transpose(1, 0, 2)
    return dict(w_in_e=dev_major(d_in_e), w_qb=dev_major(d_qb), w_kvb=dev_major(d_kvb),
                w_out_e=jnp.concatenate([d_oa, g["wob"]], axis=0).reshape(N_DEV, D_MODEL // N_DEV, D_MODEL))


def _local_step(x, positions, target, gw, sp, ex):
    w = _prep_weights(gw)
    T = x.shape[0]
    tm = min(TM, T)
    nt = T // tm
    half = MLA_ROPE // 2
    inv_freq = ROPE_BASE ** (-jnp.arange(half, dtype=F32) / half)
    invf_lane = jnp.concatenate([jnp.zeros((64,), F32), inv_freq, inv_freq, jnp.zeros((32,), F32)]).reshape(1, HEAD_W)
    tabs = _rope_tables(positions.reshape(T, 1), invf_lane)
    bias_full = jnp.repeat(sp["sgu_b"][0].T, 128, axis=1)
    sgu_w = sp["sgu_w"]
    gq, gkv = sp["mla_gq"], sp["mla_gkv"]
    ln1_g, ln1_b, ln2_g, ln2_b = sp["ln1_g"], sp["ln1_b"], sp["ln2_g"], sp["ln2_b"]
    zm, zs, cqn, ckvn, kr_rot = _mla_in(x, w["wm"], w["ws"], tabs, gq, gkv, deps=[ex.first_token])
    q, k, v = _mla_qkv(cqn, ckvn, kr_rot, tabs, w["wq"], w["wk"], w["wv"])
    o_att, lse = _attn_fwd(q, k, v)
    b_out = _sgu_fwd(zs, sp["sgu_ln_g"], sp["sgu_ln_b"], sgu_w, bias_full)
    token = ex.weights_forward(after=[o_att, b_out])
    y1, h1, h1_bf = _proj_ln("l0_out_ln1", [o_att, b_out], [w["woa"], w["wob"]], x, ln1_g, ln1_b, 0, deps=[token])
    big = ex.weights_ready(after=[y1])
    w_ff1, w_in_o, w_out_o = big["w_ff1"], big["w_in_o"], big["w_out_o"].reshape(D_MODEL, D_MODEL)
    w_ff2 = [a.reshape(D_FF, D_MODEL) for a in big["w_ff2"]]
    a0, act0 = _mlp_up("l0", h1_bf, w_ff1[0])
    y2, h2, h2_bf = _proj_ln("l0_ff2_ln2", [act0], [w_ff2[0]], h1, ln2_g, ln2_b, 0)

    z1 = _tiled("l1_in", (1, nt), [_rb(h2_bf, tm), _res(w_in_o)], [_out(T, 4 * D_MODEL, F32, tm, 4 * D_MODEL)],
                _mmc_blocks(N_DEV, NN, lambda w, d: w[d]), direct=True)
    o_pre, hg, states = _hgrn_fwd(z1, sp["hg_lb"], sp["hg_gnorm"])
    y3, h3, h3_bf = _proj_ln("l1_out_ln1", [hg], [w_out_o], h2, ln1_g, ln1_b, 1)
    a1, act1 = _mlp_up("l1", h3_bf, w_ff1[1])

    gs, g0 = {}, {}
    dy4, dy4_bf, sq_err, gs["ln2_g1"], gs["ln2_b1"] = _proj_ln_loss("l1_ff2_loss", act1, w_ff2[1], h3, ln2_g, ln2_b, 1, target)
    da1, dw1_1, dw2_1 = _mlp_bwd_w("l1", h3_bf, a1, act1, dy4_bf, big["w_ff2"][1])
    dy3, dy3_bf, dhg, gs["ln1_g1"], gs["ln1_b1"] = _dh_ln_back("l1_dh_ln1", da1, w_ff1[1], dy4, y3, ln1_g, 1, proj=[w_out_o])
    d_out_o = _tiled("l1_dwout", (2, D_MODEL // TM), [_tl(hg, TM), _cw(dy3_bf, TN)], [_out(D_MODEL, D_MODEL, F32, TM, TN)],
                     _mmc(TN_)).reshape(N_DEV, D_MODEL // N_DEV, D_MODEL)
    dz1, gs["hg_lb"], gs["hg_gnorm"] = _hgrn_bwd(z1, o_pre, dhg, states, sp["hg_lb"], sp["hg_gnorm"])
    d_in_o = _tiled("l1_dwin", (N_DEV, 1), [_res(h2_bf), _cw(dz1, TN)], [_out_dev(D_MODEL, TN, D_MODEL)], _mmc(TN_))
    token = ex.grads_start("l1", [dw1_1, dw2_1, d_in_o, d_out_o])

    dy2, dy2_bf, gs["ln2_g0"], gs["ln2_b0"] = _dh_ln_back("l1_dh_ln2", dz1, w_in_o, dy3, y2, ln2_g, 0, deps=[token])
    token = ex.grads_middle("l1", after=dy2)
    da0, dw1_0, dw2_0 = _mlp_bwd_w("l0", h1_bf, a0, act0, dy2_bf, big["w_ff2"][0], deps=[token])
    token = ex.grads_start("l0m", [dw1_0, dw2_0])
    dy1, dy1_bf, dcat, gs["ln1_g0"], gs["ln1_b0"] = _dh_ln_back("l0_dh_ln1", da0, w_ff1[0], dy2, y1, ln1_g, 0,
                                                                 proj=[w["woa"], w["wob"]], deps=[token])
    ex.grads_end("l1", after=dy1)
    g0["woa"], g0["wob"] = _out_weight_grads(o_att, b_out, dy1_bf)
    token = ex.grads_middle("l0m", after=g0["wob"])
    dzs, gs["sgu_w"], gs["sgu_ln_g"], gs["sgu_ln_b"], gs["sgu_b"] = _sgu_bwd(zs, dcat, sp["sgu_ln_g"], sp["sgu_ln_b"], sgu_w, bias_full)
    dq, dk, dv = _attn_bwd(q, k, v, o_att, lse, dcat, deps=[token])
    ex.grads_end("l0m", after=dq)
    dzm, g0["wq"], g0["wk"], g0["wv"], gs["mla_gq"], gs["mla_gkv"] = _mla_back(zm, cqn, ckvn, tabs, gq, gkv, w["wq"], w["wk"], w["wv"],
                                                                                 dq, dk, dv)
    token = ex.small_start(gs)
    dx, g0["wm"], g0["ws"] = _in_back(x, dzm, dzs, dy1, w["wm"], w["ws"], deps=[token])

    return sq_err, dx, _unprep_grads(g0), gs


def _me():
    return lax.axis_index("x"), lax.axis_index("y"), lax.axis_index("c")


ANY_SPEC = pl.BlockSpec(memory_space=pl.ANY)
HBM_SPEC = pl.BlockSpec(memory_space=pltpu.HBM)
SEM_SPEC = pl.BlockSpec(memory_space=pltpu.SEMAPHORE)
EFFECT = pltpu.SideEffectType.DATAFLOW_SIDE_EFFECTING


def _split_start(name, srcs, lands, n_sems, make_copies, after=()):
    n, m, k = len(srcs), len(lands), len(after)

    def body(*refs):
        for cp in make_copies(refs[:n], refs[n:n + m], refs[n + m + k], refs[n + m + k + 1]):
            cp.start()
        refs[-1][...] = jnp.zeros(refs[-1].shape, F32)

    out_shape = (pltpu.SemaphoreType.DMA((n_sems,)), pltpu.SemaphoreType.DMA((n_sems,)),
                 *[pltpu.HBM(a.shape, a.dtype) for a in (*srcs, *lands)], jax.ShapeDtypeStruct((8, 128), F32))
    res = pl.pallas_call(
        body, name=name, out_shape=out_shape, in_specs=[HBM_SPEC] * (n + m) + [ANY_SPEC] * k,
        out_specs=(SEM_SPEC, SEM_SPEC, *[HBM_SPEC] * (n + m), pl.BlockSpec(memory_space=pltpu.VMEM)),
        input_output_aliases={i: 2 + i for i in range(n + m)},
        compiler_params=pltpu.CompilerParams(has_side_effects=EFFECT),
    )(*[_hbm(a) for a in (*srcs, *lands)], *after)
    return res[0], res[1], list(res[2:2 + n]), list(res[2 + n:2 + n + m]), res[-1]


def _split_wait(name, send_sems, recv_sems, srcs, lands, after, make_copies):
    n, m = len(srcs), len(lands)

    def body(*refs):
        for cp in make_copies(refs[:n], refs[n:n + m], refs[n + m], refs[n + m + 1]):
            cp.wait_send()
            cp.wait_recv()

    res = pl.pallas_call(
        body, name=name, out_shape=tuple(pltpu.HBM(a.shape, a.dtype) for a in (*srcs, *lands)),
        in_specs=[HBM_SPEC] * (n + m) + [SEM_SPEC, SEM_SPEC] + [ANY_SPEC] * len(after), out_specs=tuple([HBM_SPEC] * (n + m)),
        input_output_aliases={i: i for i in range(n + m)},
        compiler_params=pltpu.CompilerParams(has_side_effects=EFFECT),
    )(*srcs, *lands, send_sems, recv_sems, *after)
    return list(res[:n]), list(res[n:])


def _place_own(shards, dev):
    n = len(shards)

    def kern(dev_ref, *refs):
        for x_ref, o_ref in zip(refs[:n], refs[n:]):
            o_ref[...] = x_ref[...].astype(o_ref.dtype)

    blocks = [(None, *a.shape[1:]) for a, _, _ in shards]
    nbytes = sum(_nbytes(b, a.dtype) + _nbytes(b, dt) for b, (a, _, dt) in zip(blocks, shards))
    return pl.pallas_call(
        kern, name="weights_place_own", out_shape=[pltpu.HBM((N_DEV, *a.shape[1:]), dt) for a, _, dt in shards],
        grid_spec=pltpu.PrefetchScalarGridSpec(
            num_scalar_prefetch=1, grid=(1,),
            in_specs=[pl.BlockSpec(b, functools.partial(lambda i, dev, l: (l, 0, 0), l=l)) for b, (_, l, _) in zip(blocks, shards)],
            out_specs=[pl.BlockSpec(b, lambda i, dev: (dev[0], 0, 0)) for b in blocks]),
        compiler_params=pltpu.CompilerParams(dimension_semantics=("arbitrary",), vmem_limit_bytes=_vmem(nbytes)),
    )(dev, *[_hbm(a) for a, _, _ in shards])


def _ag_first_copies(src_refs, out_refs, send_sems, recv_sems):
    x, y, c = _me()
    targets = [(x, y, 1 - c), (1 - x, y, c), (x, 1 - y, c), (1 - x, 1 - y, c)]
    return [pltpu.make_async_remote_copy(
        src_ref=out_refs[op].at[4 * x + 2 * y + c], dst_ref=out_refs[op].at[4 * x + 2 * y + c], send_sem=send_sems.at[4 * op + k],
        recv_sem=recv_sems.at[4 * op + k], device_id=to, device_id_type=MESH)
        for op in range(len(out_refs)) for k, to in enumerate(targets)]


def _ag_second_copies(src_refs, out_refs, send_sems, recv_sems):
    x, y, c = _me()
    chips = [(1 - x, y), (x, 1 - y), (1 - x, 1 - y)]
    return [pltpu.make_async_remote_copy(
        src_ref=out_refs[op].at[4 * cx + 2 * cy + c], dst_ref=out_refs[op].at[4 * cx + 2 * cy + c],
        send_sem=send_sems.at[3 * op + j], recv_sem=recv_sems.at[3 * op + j], device_id=(x, y, 1 - c), device_id_type=MESH)
        for op in range(len(out_refs)) for j, (cx, cy) in enumerate(chips)]


def _rs_sibling_copies(g_refs, out_refs, send_sems, recv_sems):
    x, y, c = _me()
    return [pltpu.make_async_remote_copy(
        src_ref=g_refs[op].at[k, 1 - c], dst_ref=out_refs[op].at[k], send_sem=send_sems.at[4 * op + k],
        recv_sem=recv_sems.at[4 * op + k], device_id=(x, y, 1 - c), device_id_type=MESH)
        for op in range(len(g_refs)) for k in range(4)]


def _rs_chip_copies(p_refs, out_refs, send_sems, recv_sems):
    x, y, c = _me()
    chips = [(1 - x, y), (x, 1 - y), (1 - x, 1 - y)]
    return [pltpu.make_async_remote_copy(
        src_ref=p_refs[op].at[2 * cx + cy], dst_ref=out_refs[op].at[j], send_sem=send_sems.at[3 * op + j],
        recv_sem=recv_sems.at[3 * op + j], device_id=(cx, cy, c), device_id_type=MESH)
        for op in range(len(p_refs)) for j, (cx, cy) in enumerate(chips)]


def _all_gather(placed):
    n = len(placed)

    def kern(*refs):
        in_refs, out_refs, (send_sems, recv_sems) = refs[:n], refs[n:2 * n], refs[2 * n:]
        x, y, c = _me()
        me, sibling = (x, y, c), (x, y, 1 - c)
        chips = [(1 - x, y), (x, 1 - y), (1 - x, 1 - y)]

        def copy(op, k, block, to, own=False):
            idx = 4 * block[0] + 2 * block[1] + block[2]
            return pltpu.make_async_remote_copy(
                src_ref=(in_refs if own else out_refs)[op].at[idx], dst_ref=out_refs[op].at[idx], send_sem=send_sems.at[7 * op + k],
                recv_sem=recv_sems.at[7 * op + k], device_id=to, device_id_type=MESH)

        first = []
        for op in range(n):
            first.append(copy(op, 0, me, sibling, own=True))
            first += [copy(op, 1 + j, me, (*chip, c), own=True) for j, chip in enumerate(chips)]
        for cp in first:
            cp.start()
        passed = []
        for j, chip in enumerate(chips):
            for op in range(n):
                copy(op, 1 + j, (*chip, c), me).wait_recv()
                passed.append(copy(op, 4 + j, (*chip, c), sibling))
                passed[-1].start()
        for op in range(n):
            copy(op, 0, sibling, me).wait_recv()
            for j, chip in enumerate(chips):
                copy(op, 4 + j, (*chip, 1 - c), me).wait_recv()
        for cp in first + passed:
            cp.wait_send()

    return pl.pallas_call(
        kern, name="weights_all_gather", out_shape=[pltpu.HBM(g.shape, g.dtype) for g in placed],
        in_specs=[ANY_SPEC] * n, out_specs=[ANY_SPEC] * n, input_output_aliases={i: i for i in range(n)},
        scratch_shapes=[pltpu.SemaphoreType.DMA((7 * n,)), pltpu.SemaphoreType.DMA((7 * n,))],
    )(*[_hbm(a) for a in placed])


def _row_tile(r, w, n_blocks):
    tr = r
    while tr > 8 and 2 * n_blocks * tr * w * 4 > 24 * 2**20:
        tr //= 2
    return tr


def _chip_sum(name, g, from_sibling, core):
    _, _, R, W = g.shape
    tr = _row_tile(R, W, 3)

    def kern(core_ref, g_ref, s_ref, o_ref):
        o_ref[...] = (g_ref[...] + s_ref[...]).astype(BF16)

    return pl.pallas_call(
        kern, name=name, out_shape=pltpu.HBM((4, R, W), BF16),
        grid_spec=pltpu.PrefetchScalarGridSpec(
            num_scalar_prefetch=1, grid=(4, R // tr),
            in_specs=[pl.BlockSpec((None, None, tr, W), lambda k, i, core: (k, core[0], i, 0)),
                      pl.BlockSpec((None, tr, W), lambda k, i, core: (k, i, 0))],
            out_specs=pl.BlockSpec((None, tr, W), lambda k, i, core: (k, i, 0))),
        compiler_params=pltpu.CompilerParams(dimension_semantics=("parallel", "parallel"), vmem_limit_bytes=_vmem(3 * tr * W * 4)),
    )(core, _hbm(g), _hbm(from_sibling))


def _adamw(w, g, m, v):
    m = ADAM_B1 * m + (1.0 - ADAM_B1) * g
    v = ADAM_B2 * v + (1.0 - ADAM_B2) * (g * g)
    m_hat = m / (1.0 - ADAM_B1 ** ADAM_STEP)
    v_hat = v / (1.0 - ADAM_B2 ** ADAM_STEP)
    return -ADAM_LR * (m_hat / (jnp.sqrt(v_hat) + ADAM_EPS) + ADAM_WD * w), m, v


def _finish_sharded(name, layers, w, m, v, where, deps=()):
    nl, R, W = w.shape
    tr = _row_tile(R, W, 11 * nl)
    deps = _deps(deps)

    def kern(where_ref, *refs):
        w_ref, m_ref, v_ref = refs[3 * nl:3 * nl + 3]
        go_ref, d_ref, mo_ref, vo_ref = refs[3 * nl + 3 + len(deps):]
        for l in range(nl):
            g_ref, s_ref, c_ref = refs[3 * l:3 * l + 3]
            grad = g_ref[...] + s_ref[...]
            for j in range(3):
                grad = grad + c_ref[j].astype(F32)
            go_ref[l] = grad
            d_ref[l], mo_ref[l], vo_ref[l] = _adamw(w_ref[l], grad, m_ref[l], v_ref[l])

    row = pl.BlockSpec((nl, tr, W), lambda i, wh: (0, i, 0))
    in_specs, args = [], []
    for g, s, c in layers:
        in_specs += [pl.BlockSpec((None, None, tr, W), lambda i, wh: (wh[0], wh[1], i, 0)),
                     pl.BlockSpec((None, tr, W), lambda i, wh: (wh[0], i, 0)),
                     pl.BlockSpec((3, tr, W), lambda i, wh: (0, i, 0))]
        args += [g, s, c]
    return pl.pallas_call(
        kern, name=name, out_shape=[pltpu.HBM((nl, R, W), F32)] * 4,
        grid_spec=pltpu.PrefetchScalarGridSpec(num_scalar_prefetch=1, grid=(R // tr,),
                                               in_specs=in_specs + [row, row, row] + [ANY_SPEC] * len(deps),
                                               out_specs=[row, row, row, row]),
        compiler_params=pltpu.CompilerParams(dimension_semantics=("parallel",), vmem_limit_bytes=_vmem(nl * 11 * tr * W * 4)),
    )(where, *[_hbm(a) for a in (*args, w, m, v)], *deps)


SMALL_PLACE = (("mla_gq", 0, 0, 1, 256), ("mla_gkv", 0, 256, 1, 256), ("sgu_ln_g", 0, 512, 1, 512), ("sgu_ln_b", 1, 0, 1, 512),
               ("hg_lb", 2, 0, 2, 1024), ("ln1_g", 4, 0, 2, 1024), ("ln1_b", 6, 0, 2, 1024), ("sgu_b", 8, 0, 4, 128),
               ("ln2_g", 12, 0, 2, 1024), ("ln2_b", 14, 0, 2, 1024), ("hg_gnorm", 16, 0, 1, 1024))
SMALL_BUF_ROWS = 24


def _small_pack(gs, dev):
    pieces = [(gs["mla_gq"], 0, 0), (gs["mla_gkv"], 0, 256), (gs["sgu_ln_g"], 0, 512), (gs["sgu_ln_b"], 1, 0), (gs["hg_lb"], 2, 0),
              (gs["ln1_g0"], 4, 0), (gs["ln1_g1"], 5, 0), (gs["ln1_b0"], 6, 0), (gs["ln1_b1"], 7, 0), (gs["sgu_b"], 8, 0),
              (gs["ln2_g0"], 12, 0), (gs["ln2_g1"], 13, 0), (gs["ln2_b0"], 14, 0), (gs["ln2_b1"], 15, 0), (gs["hg_gnorm"], 16, 0)]
    n_p = len(pieces)

    def kern(dev_ref, *refs):
        a_ref, b_ref = refs[n_p + 1], refs[n_p + 2]
        a_ref[...] = jnp.zeros(a_ref.shape, F32)
        for ref, (_, r, l0) in zip(refs[:n_p], pieces):
            a_ref[r:r + ref.shape[0], l0:l0 + ref.shape[1]] = ref[...]
        b_ref[...] = refs[n_p][...]

    whole = lambda a: pl.BlockSpec(a.shape, functools.partial(lambda i, dev, nd: (0,) * nd, nd=a.ndim))
    return pl.pallas_call(
        kern, name="small_grads_pack",
        out_shape=[pltpu.HBM((N_DEV, SMALL_BUF_ROWS, D_MODEL), F32), pltpu.HBM((N_DEV, SGU_G, 128, 128), F32)],
        grid_spec=pltpu.PrefetchScalarGridSpec(
            num_scalar_prefetch=1, grid=(1,), in_specs=[whole(p[0]) for p in pieces] + [whole(gs["sgu_w"])],
            out_specs=[pl.BlockSpec((None, SMALL_BUF_ROWS, D_MODEL), lambda i, dev: (dev[0], 0, 0)),
                       pl.BlockSpec((None, SGU_G, 128, 128), lambda i, dev: (dev[0], 0, 0, 0))]),
    )(dev, *[p[0] for p in pieces], gs["sgu_w"])


def _small_copies(src_refs, land_refs, send_sems, recv_sems):
    px, py, pc = _me()
    me = 4 * px + 2 * py + pc
    return [pltpu.make_async_remote_copy(
        src_ref=land_refs[k].at[me], dst_ref=land_refs[k].at[me], send_sem=send_sems.at[2 * (r - 1) + k],
        recv_sem=recv_sems.at[2 * (r - 1) + k], device_id=(px ^ (r >> 2), py ^ ((r >> 1) & 1), pc ^ (r & 1)), device_id_type=MESH)
        for r in range(1, N_DEV) for k in range(2)]


def _small_adamw(slots_a, slots_b, given):
    names = [p[0] for p in SMALL_PLACE] + ["sgu_w"]
    n_names = len(names)
    wmv = [given[pre + name] for name in names for pre in ("", "m_", "v_")]
    vmem = pl.BlockSpec(memory_space=pltpu.VMEM)

    def kern(*refs):
        sum_a, sum_b = refs[0][0], refs[1][0]
        for d in range(1, N_DEV):
            sum_a, sum_b = sum_a + refs[0][d], sum_b + refs[1][d]
        wmv_refs, out_refs = refs[2:2 + 3 * n_names], refs[2 + 3 * n_names:]
        px, py, pc = _me()
        me = 4 * px + 2 * py + pc

        def own_block(full):
            acc = full[:, 0:128]
            for b in range(1, N_DEV):
                acc = jnp.where(me == b, full[:, b * 128:(b + 1) * 128], acc)
            return acc

        for idx, name in enumerate(names):
            w_ref, m_ref, v_ref = wmv_refs[3 * idx:3 * idx + 3]
            if name == "sgu_w":
                grad = sum_b[None]
            else:
                _, r, l0, nr, nl = SMALL_PLACE[idx]
                grad = sum_a[r:r + nr, l0:l0 + nl]
                if name == "hg_gnorm":
                    grad = own_block(grad)
                if name == "sgu_b":
                    grad = grad[None]
            res = (grad, *_adamw(w_ref[...], grad, m_ref[...], v_ref[...]))
            for o_ref, val in zip(out_refs[4 * idx:4 * idx + 4], res):
                o_ref[...] = val

    out_shape = [jax.ShapeDtypeStruct(given[name].shape, F32) for name in names for _ in range(4)]
    res = pl.pallas_call(
        kern, name="small_adamw", out_shape=out_shape, in_specs=[vmem] * (2 + len(wmv)), out_specs=[vmem] * len(out_shape),
    )(slots_a, slots_b, *wmv)
    return {name: res[4 * idx:4 * idx + 4] for idx, name in enumerate(names)}


class _Exchange:
    def __init__(self, given):
        self.given = given
        px, py, pc = _me()
        self.core = pc.reshape(1).astype(jnp.int32)
        self.dev = (4 * px + 2 * py + pc).reshape(1).astype(jnp.int32)
        self.where = jnp.stack([2 * px + py, pc]).astype(jnp.int32)
        self.state, self.layers = {}, {}

    def start_weights(self, lands, after):
        self.weights = _split_start("weights_first_start", [], lands, 4 * len(lands), _ag_first_copies, after=after)
        self.first_token = self.weights[4]

    def weights_forward(self, after):
        send_sems, recv_sems, shards, lands, _ = self.weights
        _, lands = _split_wait("weights_first_wait", send_sems, recv_sems, shards, lands, after, _ag_first_copies)
        self.weights = _split_start("weights_second_start", [], lands, 3 * len(lands), _ag_second_copies)
        return self.weights[4]

    def weights_ready(self, after):
        send_sems, recv_sems, shards, lands, _ = self.weights
        _, got = _split_wait("weights_second_wait", send_sems, recv_sems, shards, lands, after, _ag_second_copies)
        return dict(w_in_o=got[0], w_out_o=got[1], w_ff1=[got[2], got[3]], w_ff2=[got[4], got[5]])

    def small_start(self, gs):
        self.small = _split_start("small_grads_start", [], _small_pack(gs, self.dev), 14, _small_copies)
        return self.small[4]

    def small_finish(self, after):
        send_sems, recv_sems, _, lands, _ = self.small
        _, lands = _split_wait("small_grads_wait", send_sems, recv_sems, [], lands, after, _small_copies)
        return _small_adamw(lands[0], lands[1], self.given)

    def grads_start(self, tag, grads):
        blocks = [g.reshape(4, 2, *g.shape[1:]) for g in grads]
        lands = [lax.empty((4, *b.shape[2:]), F32) for b in blocks]
        self.state[tag] = _split_start(f"grads_{tag}_sibling_start", blocks, lands, 4 * len(blocks), _rs_sibling_copies)
        return self.state[tag][4]

    def grads_middle(self, tag, after):
        send_sems, recv_sems, blocks, lands, _ = self.state[tag]
        blocks, from_sibling = _split_wait(f"grads_{tag}_sibling_wait", send_sems, recv_sems, blocks, lands, [after], _rs_sibling_copies)
        sums = [_chip_sum(f"grads_{tag}_chip_sum_{k}", b, s, self.core) for k, (b, s) in enumerate(zip(blocks, from_sibling))]
        lands = [lax.empty((3, *p.shape[1:]), BF16) for p in sums]
        self.state[tag] = (blocks, from_sibling, _split_start(f"grads_{tag}_chips_start", sums, lands, 3 * len(sums), _rs_chip_copies))
        return self.state[tag][2][4]

    def grads_end(self, tag, after):
        blocks, from_sibling, (send_sems, recv_sems, sums, lands, _) = self.state[tag]
        after = list(after) if isinstance(after, (list, tuple)) else [after]
        _, from_chips = _split_wait(f"grads_{tag}_chips_wait", send_sems, recv_sems, sums, lands, after, _rs_chip_copies)
        self.layers[tag] = list(zip(blocks, from_sibling, from_chips))


SHARDED = ("w_in_e", "w_qb", "w_kvb", "w_out_e", "w_in_o", "w_out_o", "w_ff1", "w_ff2")


def kernel(x, positions, w_in_e, mla_gq, mla_gkv, w_qb, w_kvb, sgu_ln_g, sgu_ln_b, sgu_w, sgu_b, w_out_e, w_in_o, hg_lb, hg_gnorm, w_out_o, ln1_g, ln1_b, w_ff1, w_ff2, ln2_g, ln2_b, loss_target, m_w_in_e, m_mla_gq, m_mla_gkv, m_w_qb, m_w_kvb, m_sgu_ln_g, m_sgu_ln_b, m_sgu_w, m_sgu_b, m_w_out_e, m_w_in_o, m_hg_lb, m_hg_gnorm, m_w_out_o, m_ln1_g, m_ln1_b, m_w_ff1, m_w_ff2, m_ln2_g, m_ln2_b, v_w_in_e, v_mla_gq, v_mla_gkv, v_w_qb, v_w_kvb, v_sgu_ln_g, v_sgu_ln_b, v_sgu_w, v_sgu_b, v_w_out_e, v_w_in_o, v_hg_lb, v_hg_gnorm, v_w_out_o, v_ln1_g, v_ln1_b, v_w_ff1, v_w_ff2, v_ln2_g, v_ln2_b):
    given = dict(locals())
    ex = _Exchange(given)

    names = ["w_in_e", "w_qb", "w_kvb", "w_out_e"]
    placed = _place_own([(given[n], 0, BF16) for n in names] + [(hg_gnorm.reshape(1, 1, D_MODEL // N_DEV), 0, F32)]
                        + [(w_in_o, 0, BF16), (w_out_o, 0, BF16), (w_ff1, 0, BF16), (w_ff1, 1, BF16), (w_ff2, 0, BF16), (w_ff2, 1, BF16)],
                        ex.dev)
    got = _all_gather(placed[:5])
    ex.start_weights(placed[5:], after=[got[0]])
    gw = dict(zip(names, got[:4]))
    small_names = ["mla_gq", "mla_gkv", "sgu_ln_g", "sgu_ln_b", "sgu_w", "sgu_b", "hg_lb", "ln1_g", "ln1_b", "ln2_g", "ln2_b"]
    sp = {n: given[n] for n in small_names}
    sp["hg_gnorm"] = got[4].reshape(1, D_MODEL)

    sq_err, dx, grads, gs = _local_step(x[0], positions[0], loss_target[0], gw, sp, ex)
    loss = lax.psum(0.5 * jnp.sum(sq_err) / D_MODEL, ("x", "y", "c"))

    def finish(n, layers, deps=()):
        return _finish_sharded(f"finish_{n}", layers, given[n], given["m_" + n], given["v_" + n], ex.where, deps=deps)

    l1, l0m = ex.layers["l1"], ex.layers["l0m"]
    results = {}
    token = ex.grads_start("l0s", [grads[n] for n in names])
    results["w_ff1"] = finish("w_ff1", [l0m[0], l1[0]], deps=[token])
    token = ex.grads_middle("l0s", after=results["w_ff1"][0])
    results["w_ff2"] = finish("w_ff2", [l0m[1], l1[1]], deps=[token])
    results["w_in_o"] = finish("w_in_o", [l1[2]], deps=[token])
    results["w_out_o"] = finish("w_out_o", [l1[3]], deps=[token])
    results.update(ex.small_finish(after=[results["w_in_o"][0]]))
    ex.grads_end("l0s", after=[results[n][0] for n in ("mla_gq", "w_ff2", "w_in_o", "w_out_o")])
    for n, layer in zip(names, ex.layers["l0s"]):
        results[n] = finish(n, [layer])

    order = ["w_in_e", "mla_gq", "mla_gkv", "w_qb", "w_kvb", "sgu_ln_g", "sgu_ln_b", "sgu_w", "sgu_b", "w_out_e", "w_in_o",
             "hg_lb", "hg_gnorm", "w_out_o", "ln1_g", "ln1_b", "w_ff1", "w_ff2", "ln2_g", "ln2_b"]
    return (loss, dx[None], *[results[name][kind] for kind in range(4) for name in order])
```

```python
import functools
import math

import jax
import jax.numpy as jnp
import numpy as np
from jax import lax
from jax.experimental import pallas as pl
from jax.experimental.pallas import tpu as pltpu

F32 = jnp.float32
BF16 = jnp.bfloat16
MESH = pl.DeviceIdType.MESH
HIGHEST = lax.Precision.HIGHEST

D_MODEL = 1024
D_FF = 4096
N_DEV = 8
HEADS = 8
HEAD_W = 128
MLA_NOPE = 64
MLA_ROPE = 32
MLA_V = 64
MLA_LORA = 256
MLA_SCALE = (MLA_NOPE + MLA_ROPE) ** -0.5
ROPE_BASE = 10000.0
SGU_DIM = 512
SGU_G = 4
SGU_CHUNK = 128
HG_CHUNK = 64
HG_CHUNKS_PER_STEP = 4
ALPHA = (2 * 2) ** 0.25
EPS = 1e-5
ADAM_LR, ADAM_B1, ADAM_B2, ADAM_EPS, ADAM_WD, ADAM_STEP = 0.001, 0.9, 0.999, 1e-08, 0.01, 10

VMEM_CAP_V7X = 56 * 2**20
VMEM_SLACK = 12 * 2**20
TM = 512
TN = 512


def _vmem(block_bytes):
    return int(min(VMEM_CAP_V7X, 2 * block_bytes + VMEM_SLACK))


def _hbm(a):
    return pltpu.with_memory_space_constraint(a, pltpu.HBM)


def _nbytes(shape, dtype):
    return int(np.prod([d for d in shape if d is not None])) * jnp.dtype(dtype).itemsize


def _sig(x):
    return 1.0 / (1.0 + jnp.exp(-x))


def _gelu(x):
    c = math.sqrt(2.0 / math.pi)
    t = jnp.tanh(c * (x + 0.044715 * x * x * x))
    return 0.5 * x * (1.0 + t), t


def _gelu_grad(x, t):
    c = math.sqrt(2.0 / math.pi)
    return 0.5 * (1.0 + t) + 0.5 * x * (1.0 - t * t) * c * (1.0 + 3 * 0.044715 * x * x)


def _dot(a, b, dims, precision=None):
    return lax.dot_general(a, b, (dims, ((), ())), preferred_element_type=F32, precision=precision)


NN = ((1,), (0,))
NT = ((1,), (1,))
TN_ = ((0,), (0,))


def _deps(deps):
    return [d for d in deps if d is not None]


def _tiled(name, grid, ins, outs, compute, direct=False, deps=()):
    n_in, deps = len(ins), _deps(deps)
    n_skip = n_in + len(deps)

    def kern(*refs):
        if direct:
            compute(refs[:n_in], refs[n_skip:])
            return
        for o_ref, r in zip(refs[n_skip:], compute(*refs[:n_in])):
            o_ref[...] = r.astype(o_ref.dtype).reshape(o_ref.shape)

    swap = lambda f: (lambda j, i: f(i, j))
    nbytes = sum(_nbytes(blk, a.dtype) for a, blk, _ in ins) + sum(_nbytes(blk, dt) + _nbytes(blk, F32) for _, dt, blk, _ in outs)
    res = pl.pallas_call(
        kern, name=name, grid=grid,
        in_specs=[pl.BlockSpec(blk, swap(f), pipeline_mode=pl.Buffered(1) if tuple(blk) == tuple(a.shape) else None)
                  for a, blk, f in ins] + [ANY_SPEC] * len(deps),
        out_specs=[pl.BlockSpec(blk, swap(f)) for _, _, blk, f in outs],
        out_shape=[pltpu.HBM(shape, dt) for shape, dt, _, _ in outs],
        compiler_params=pltpu.CompilerParams(dimension_semantics=("parallel", "parallel"), vmem_limit_bytes=_vmem(nbytes)),
    )(*[_hbm(a) for a, _, _ in ins], *deps)
    return res if len(res) > 1 else res[0]


def _rb(a, tm, w=None, cb=0):
    return (a, (tm, a.shape[1] if w is None else w), lambda i, j: (i, cb))


def _rbj(a, tm, tn):
    return (a, (tm, tn), lambda i, j: (i, j))


def _cw(b, tn):
    return (b, (b.shape[0], tn), lambda i, j: (0, j))


def _rw(b, tn):
    return (b, (tn, b.shape[1]), lambda i, j: (j, 0))


def _tl(a, tm):
    return (a, (a.shape[0], tm), lambda i, j: (0, i))


def _gcw(g):
    return (g, (None, g.shape[1], g.shape[2]), lambda i, j: (j, 0, 0))


def _grw(g, tn):
    return (g, (N_DEV, tn, g.shape[2]), lambda i, j: (0, j, 0))


def _out(m, n, dtype, tm, tn):
    return ((m, n), dtype, (tm, tn), lambda i, j: (i, j))


def _out_dev(k, n, tm):
    return ((N_DEV, k, n), F32, (None, tm, n), lambda i, j: (j, i, 0))


def _mmc(dims, n_pairs=1, epilogue=None):
    def compute(*refs):
        acc = None
        for k in range(n_pairs):
            d = _dot(refs[2 * k][...].astype(BF16), refs[2 * k + 1][...].astype(BF16), dims)
            acc = d if acc is None else acc + d
        ext = [r[...] for r in refs[2 * n_pairs:]]
        return epilogue(acc, *ext) if epilogue is not None else (acc,)

    return compute


def _res(w):
    return (w, w.shape, functools.partial(lambda i, j, nd: (0,) * nd, nd=w.ndim))


def _mmc_blocks(nblk, dims, rhs_block, epilogue=None):
    def compute(in_refs, out_refs):
        a = in_refs[0][...].astype(BF16)
        for d in range(nblk):
            acc = _dot(a, rhs_block(in_refs[1], d).astype(BF16), dims)
            n = acc.shape[1]
            ext = [r[:, d * n:(d + 1) * n] for r in in_refs[2:]]
            res = epilogue(acc, *ext) if epilogue is not None else (acc,)
            for o_ref, r in zip(out_refs, res):
                o_ref[:, d * n:(d + 1) * n] = r.astype(o_ref.dtype)

    return compute


def _mmc_dev(epilogue=None):
    def compute(a_ref, b_ref, *ext_refs):
        n = b_ref.shape[2]
        acc = None
        for d in range(N_DEV):
            t = _dot(a_ref[:, d * n:(d + 1) * n].astype(BF16), b_ref[d].astype(BF16), NT)
            acc = t if acc is None else acc + t
        ext = [r[...] for r in ext_refs]
        return epilogue(acc, *ext) if epilogue is not None else (acc,)

    return compute


def _rowwise(name, body, rows, consts, out_rows, out_accs=(), tr=512, deps=()):
    T = rows[0][0].shape[0]
    tr = min(tr, T)
    deps = _deps(deps)
    nr, ncn, no, nd = len(rows), len(consts), len(out_rows), len(deps)

    def kern(*refs):
        accs = refs[nr + ncn + nd + no:]
        if accs:
            @pl.when(pl.program_id(0) == 0)
            def _():
                for a in accs:
                    a[...] = jnp.zeros(a.shape, a.dtype)
        body(refs[:nr], refs[nr:nr + ncn], refs[nr + ncn + nd:nr + ncn + nd + no], accs)

    in_specs = [pl.BlockSpec((tr, w), functools.partial(lambda i, cb: (i, cb), cb=cb)) for _, w, cb in rows]
    in_specs += [pl.BlockSpec(c.shape, functools.partial(lambda i, nd: (0,) * nd, nd=c.ndim), pipeline_mode=pl.Buffered(1))
                 for c in consts]
    in_specs += [ANY_SPEC] * nd
    out_specs = [pl.BlockSpec((tr, w), lambda i: (i, 0)) for w, _ in out_rows]
    out_specs += [pl.BlockSpec(s, functools.partial(lambda i, nd: (0,) * nd, nd=len(s))) for s, _ in out_accs]
    out_shape = [pltpu.HBM((T, w), dt) for w, dt in out_rows]
    out_shape += [pltpu.HBM(s, dt) for s, dt in out_accs]
    nbytes = sum(_nbytes((tr, w), a.dtype) for a, w, _ in rows) + sum(_nbytes(c.shape, c.dtype) for c in consts)
    nbytes += sum(_nbytes((tr, w), dt) for w, dt in out_rows) + sum(_nbytes(s, dt) for s, dt in out_accs)
    res = pl.pallas_call(
        kern, name=name, grid=(T // tr,), in_specs=in_specs, out_specs=out_specs, out_shape=out_shape,
        compiler_params=pltpu.CompilerParams(dimension_semantics=("arbitrary",), vmem_limit_bytes=_vmem(nbytes)),
    )(*[_hbm(a) for a, _, _ in rows], *[_hbm(c) for c in consts], *deps)
    return res if len(res) > 1 else res[0]


def _full(a):
    return (a, a.shape[1], 0)


def _ln_stats(y):
    mu = jnp.mean(y, axis=-1, keepdims=True)
    yc = y - mu
    r = lax.rsqrt(jnp.mean(yc * yc, axis=-1, keepdims=True) + EPS)
    return yc * r, r


def _row_halves(n):
    return [slice(0, n // 2), slice(n // 2, n)] if n >= 256 else [slice(0, n)]


def _ln_back(dh, xh, r, gain, dg_ref, db_ref):
    dg_ref[...] += jnp.sum(dh * xh, axis=0, keepdims=True)
    db_ref[...] += jnp.sum(dh, axis=0, keepdims=True)
    dx = dh * gain
    return r * (dx - jnp.mean(dx, axis=-1, keepdims=True) - xh * jnp.mean(dx * xh, axis=-1, keepdims=True))


def _proj_ln(name, acts, weights, h_in, g, b, layer, deps=()):
    n = len(acts)

    def body(rows, consts, outs, accs):
        acc = None
        for k in range(n):
            d = _dot(rows[k][...].astype(BF16), consts[k][...], NN)
            acc = d if acc is None else acc + d
        y = ALPHA * rows[n][...] + acc
        xh, _ = _ln_stats(y)
        h = xh * consts[n][layer:layer + 1, :] + consts[n + 1][layer:layer + 1, :]
        outs[0][...] = y
        outs[1][...] = h
        outs[2][...] = h.astype(BF16)

    return _rowwise(name, body, [_full(a) for a in acts] + [_full(h_in)], [*weights, g, b],
                    [(D_MODEL, F32), (D_MODEL, F32), (D_MODEL, BF16)], tr=TM, deps=deps)


def _proj_ln_loss(name, act, w2, h_in, g, b, layer, target):
    def body(rows, consts, outs, accs):
        y = ALPHA * rows[1][...] + _dot(rows[0][...], consts[0][...], NN)
        xh, r = _ln_stats(y)
        gain = consts[1][layer:layer + 1, :]
        err = xh * gain + consts[2][layer:layer + 1, :] - rows[2][...]
        accs[0][...] += jnp.sum(err * err, axis=0, keepdims=True)
        dy = _ln_back(err * (1.0 / D_MODEL), xh, r, gain, accs[1], accs[2])
        outs[0][...] = dy
        outs[1][...] = dy.astype(BF16)

    return _rowwise(name, body, [_full(act), _full(h_in), _full(target)], [w2, g, b], [(D_MODEL, F32), (D_MODEL, BF16)],
                    [((1, D_MODEL), F32)] * 3, tr=TM)


def _dh_ln_back(name, da, w, dy_next, y, g, layer, proj=(), deps=()):
    def body(rows, consts, outs, accs):
        n = consts[0].shape[2]
        for sl in _row_halves(rows[0].shape[0]):
            acc = ALPHA * rows[1][sl, :]
            for d in range(N_DEV):
                acc = acc + _dot(rows[0][sl, d * n:(d + 1) * n], consts[0][d], NT)
            xh, r = _ln_stats(rows[2][sl, :])
            dy = _ln_back(acc, xh, r, consts[1][layer:layer + 1, :], accs[0], accs[1])
            outs[0][sl, :] = dy
            dy_bf = dy.astype(BF16)
            outs[1][sl, :] = dy_bf
            off = 0
            for k, p in enumerate(proj):
                outs[2][sl, off:off + p.shape[0]] = _dot(dy_bf, consts[2 + k][...], NT).astype(BF16)
                off += p.shape[0]

    out_rows = [(D_MODEL, F32), (D_MODEL, BF16)] + ([(sum(p.shape[0] for p in proj), BF16)] if proj else [])
    return _rowwise(name, body, [_full(da), _full(dy_next), _full(y)], [w, g, *proj], out_rows,
                    [((1, D_MODEL), F32)] * 2, tr=TM, deps=deps)


def _relu2_epilogue(acc):
    a = jnp.maximum(acc, 0.0)
    return acc, a * a


def _mlp_up(tag, h_bf, w1):
    T = h_bf.shape[0]
    tm = min(TM, T)
    return _tiled(f"{tag}_ff1", (1, T // tm), [_rb(h_bf, tm), _res(w1)],
                  [_out(T, D_FF, BF16, tm, D_FF), _out(T, D_FF, BF16, tm, D_FF)],
                  _mmc_blocks(N_DEV, NN, lambda w, d: w[d], epilogue=_relu2_epilogue), direct=True)


def _mlp_bwd_w(tag, h_bf, a, act, dff_bf, w2, deps=()):
    T = h_bf.shape[0]
    tm = min(TM, T)
    da = _tiled(f"{tag}_dact", (1, T // tm), [_rb(dff_bf, tm), _res(w2), _rb(a, tm)], [_out(T, D_FF, BF16, tm, D_FF)],
                _mmc_blocks(N_DEV, NT, lambda w, d: w[d], epilogue=lambda acc, a_t: (acc * 2.0 * jnp.maximum(a_t.astype(F32), 0.0),)),
                direct=True, deps=deps)
    dw2 = _tiled(f"{tag}_dw2", (1, D_FF // TM), [_tl(act, TM), _res(dff_bf)],
                 [_out(D_FF, D_MODEL, F32, TM, D_MODEL)], _mmc(TN_)).reshape(N_DEV, D_FF // N_DEV, D_MODEL)
    dw1 = _tiled(f"{tag}_dw1", (N_DEV, 1), [_res(h_bf), _cw(da, TN)], [_out_dev(D_MODEL, TN, D_MODEL)], _mmc(TN_))
    return da, dw1, dw2


def _rope_tables(positions_col, invf_lane):
    def body(rows, consts, outs, accs):
        ang = rows[0][...].astype(F32) * consts[0][...]
        c, s = jnp.cos(ang), jnp.sin(ang)
        lane = lax.broadcasted_iota(jnp.int32, ang.shape, 1)
        outs[0][...] = jnp.where(lane < 64, 1.0, jnp.where(lane < 96, c, 0.0))
        outs[1][...] = jnp.where((lane >= 64) & (lane < 80), -s, 0.0)
        outs[2][...] = jnp.where((lane >= 80) & (lane < 96), s, 0.0)

    return _rowwise("rope_tables", body, [_full(positions_col)], [invf_lane], [(HEAD_W, F32)] * 3)


def _rope(x, c, s1, s2):
    return x * c + pltpu.roll(x, 112, 1) * s1 + pltpu.roll(x, 16, 1) * s2


def _rope_t(dx, c, s1, s2):
    return dx * c + pltpu.roll(dx * s1, 16, 1) + pltpu.roll(dx * s2, 112, 1)


def _rms(c):
    r = lax.rsqrt(jnp.mean(c * c, axis=-1, keepdims=True) + EPS)
    return c * r, r


def _rope_heads(x, c, s1, s2, fn):
    return jnp.concatenate([fn(x[:, h * HEAD_W:(h + 1) * HEAD_W], c, s1, s2) for h in range(HEADS)], axis=1)


def _mla_in(x, wm, ws, tabs, gq, gkv, deps=()):
    def body(rows, consts, outs, accs):
        xb = rows[0][...].astype(BF16)
        zm = _dot(xb, consts[0][...], NN)
        outs[0][...] = zm
        outs[1][...] = _dot(xb, consts[1][...], NN)
        outs[2][...] = (_rms(zm[:, 0:256])[0] * consts[2][...]).astype(BF16)
        outs[3][...] = (_rms(zm[:, 256:512])[0] * consts[3][...]).astype(BF16)
        outs[4][...] = _rope(zm[:, 512:640], rows[1][...], rows[2][...], rows[3][...])

    return _rowwise("l0_in", body, [_full(x)] + [_full(t) for t in tabs], [wm, ws, gq, gkv],
                    [(640, F32), (1024, F32), (256, BF16), (256, BF16), (HEAD_W, F32)], deps=deps)


def _mla_qkv(cqn, ckvn, kr_rot, tabs, wq, wk, wv):
    def body(rows, consts, outs, accs):
        c, s1, s2 = rows[3][...], rows[4][...], rows[5][...]
        outs[0][...] = _rope_heads(_dot(rows[0][...], consts[0][...], NN), c, s1, s2, _rope).astype(BF16)
        outs[1][...] = (_dot(rows[1][...], consts[1][...], NN) + jnp.concatenate([rows[2][...]] * HEADS, axis=1)).astype(BF16)
        outs[2][...] = _dot(rows[1][...], consts[2][...], NN).astype(BF16)

    rows = [_full(cqn), _full(ckvn), _full(kr_rot)] + [_full(t) for t in tabs]
    return _rowwise("l0_qkv", body, rows, [wq, wk, wv], [(HEADS * HEAD_W, BF16)] * 3)


def _mla_back(zm, cqn, ckvn, tabs, gq, gkv, wq, wk, wv, dq, dk, dv):
    def body(rows, consts, outs, accs):
        c, s1, s2 = rows[4][...], rows[5][...], rows[6][...]
        dk_t, dv_bf = rows[8][...], rows[9][...].astype(BF16)
        dq_bf = _rope_heads(rows[7][...], c, s1, s2, _rope_t).astype(BF16)
        dk_bf = dk_t.astype(BF16)
        accs[0][...] += _dot(rows[2][...], dq_bf, TN_)
        accs[1][...] += _dot(rows[3][...], dk_bf, TN_)
        accs[2][...] += _dot(rows[3][...], dv_bf, TN_)
        dlat = [_dot(dq_bf, consts[2][...], NT), _dot(dk_bf, consts[3][...], NT) + _dot(dv_bf, consts[4][...], NT)]
        for k in range(2):
            ch, r = _rms(rows[k][...])
            accs[3 + k][...] += jnp.sum(dlat[k] * ch, axis=0, keepdims=True)
            dc = dlat[k] * consts[k][...]
            outs[0][:, 256 * k:256 * (k + 1)] = (r * (dc - ch * jnp.mean(dc * ch, axis=-1, keepdims=True))).astype(BF16)
        dks = dk_t[:, 0:HEAD_W]
        for h in range(1, HEADS):
            dks = dks + dk_t[:, h * HEAD_W:(h + 1) * HEAD_W]
        lane = lax.broadcasted_iota(jnp.int32, dks.shape, 1)
        dks = jnp.where((lane >= 64) & (lane < 96), dks, 0.0)
        outs[0][:, 512:640] = _rope_t(dks, c, s1, s2).astype(BF16)

    rows = [(zm, 256, 0), (zm, 256, 1), _full(cqn), _full(ckvn)] + [_full(t) for t in tabs] + [_full(dq), _full(dk), _full(dv)]
    wide = HEADS * HEAD_W
    return _rowwise("l0_mla_back", body, rows, [gq, gkv, wq, wk, wv], [(640, BF16)],
                    [((MLA_LORA, wide), F32)] * 3 + [((1, MLA_LORA), F32)] * 2, tr=256)


def _in_back(x, dzm, dzs, dy, wm, ws, deps=()):
    def body(rows, consts, outs, accs):
        dzm_t, dzs_t = rows[1][...], rows[2][...]
        outs[0][...] = _dot(dzm_t, consts[0][...], NT) + _dot(dzs_t, consts[1][...], NT) + ALPHA * rows[3][...]
        xb = rows[0][...].astype(BF16)
        accs[0][...] += _dot(xb, dzm_t, TN_)
        accs[1][...] += _dot(xb, dzs_t, TN_)

    return _rowwise("l0_in_back", body, [_full(x), _full(dzm), _full(dzs), _full(dy)], [wm, ws], [(D_MODEL, F32)],
                    [((D_MODEL, 640), F32), ((D_MODEL, 1024), F32)], deps=deps)


def _out_weight_grads(o_att, b_out, dy_bf):
    def body(rows, consts, outs, accs):
        d = rows[2][...]
        accs[0][...] += _dot(rows[0][...].astype(BF16), d, TN_)
        accs[1][...] += _dot(rows[1][...], d, TN_)

    return _rowwise("l0_dw_out", body, [_full(o_att), _full(b_out), _full(dy_bf)], [], [],
                    [((HEADS * HEAD_W, D_MODEL), F32), ((SGU_DIM, D_MODEL), F32)])


def _attn_block(T):
    return min(1024, T)


def _attn_fwd(q, k, v):
    T = q.shape[0]
    BQ = _attn_block(T)
    nq = T // BQ

    def kern(q_ref, k_ref, v_ref, o_ref, lse_ref):
        def step(i, j, carry, masked):
            m, l, acc = carry
            qb = q_ref[pl.ds(pl.multiple_of(i * BQ, BQ), BQ), :]
            kb = k_ref[pl.ds(pl.multiple_of(j * BQ, BQ), BQ), :]
            vb = v_ref[pl.ds(pl.multiple_of(j * BQ, BQ), BQ), :]
            s = _dot(qb, kb, NT) * MLA_SCALE
            if masked:
                row = lax.broadcasted_iota(jnp.int32, s.shape, 0)
                col = lax.broadcasted_iota(jnp.int32, s.shape, 1)
                s = jnp.where(col <= row, s, -1e30)
            m_new = jnp.maximum(m, jnp.max(s, axis=-1, keepdims=True))
            p = jnp.exp(s - m_new)
            a = jnp.exp(m - m_new)
            l = a * l + jnp.sum(p, axis=-1, keepdims=True)
            acc = a * acc + _dot(p.astype(BF16), vb, NN)
            return m_new, l, acc

        def qloop(i, _):
            init = (jnp.full((BQ, 1), -1e30, F32), jnp.zeros((BQ, 1), F32), jnp.zeros((BQ, HEAD_W), F32))
            carry = lax.fori_loop(0, i, lambda j, c: step(i, j, c, False), init)
            m, l, acc = step(i, i, carry, True)
            rows = pl.ds(pl.multiple_of(i * BQ, BQ), BQ)
            o_ref[rows, :] = acc / l
            lse_ref[0, rows, :] = m + jnp.log(l)
            return 0

        lax.fori_loop(0, nq, qloop, 0)

    head = pl.BlockSpec((T, HEAD_W), lambda h: (0, h))
    nbytes = 3 * _nbytes((T, HEAD_W), BF16) + _nbytes((T, HEAD_W), F32) + _nbytes((T, 128), F32)
    return pl.pallas_call(
        kern, name="attn_fwd", grid=(HEADS,), in_specs=[head, head, head],
        out_specs=[head, pl.BlockSpec((1, T, 1), lambda h: (h, 0, 0))],
        out_shape=[pltpu.HBM((T, HEADS * HEAD_W), F32), pltpu.HBM((HEADS, T, 1), F32)],
        compiler_params=pltpu.CompilerParams(dimension_semantics=("parallel",), vmem_limit_bytes=_vmem(nbytes)),
    )(_hbm(q), _hbm(k), _hbm(v))


def _attn_bwd(q, k, v, o, lse, dcat, deps=()):
    T = q.shape[0]
    BQ = _attn_block(T)
    nq = T // BQ
    deps = _deps(deps)

    def kern(q_ref, k_ref, v_ref, o_ref, lse_ref, do_ref, *rest):
        dq_ref, dk_ref, dv_ref, dd_ref = rest[len(deps):]
        dq_ref[...] = jnp.zeros(dq_ref.shape, F32)

        def dloop(i, _):
            rows = pl.ds(pl.multiple_of(i * BQ, BQ), BQ)
            dd_ref[rows, :] = jnp.sum(do_ref[rows, :].astype(F32) * o_ref[rows, :], axis=-1, keepdims=True)
            return 0

        lax.fori_loop(0, nq, dloop, 0)

        def step(j, i, carry, masked):
            dk_acc, dv_acc = carry
            rq = pl.ds(pl.multiple_of(i * BQ, BQ), BQ)
            rk = pl.ds(pl.multiple_of(j * BQ, BQ), BQ)
            qb, kb, vb, dob = q_ref[rq, :], k_ref[rk, :], v_ref[rk, :], do_ref[rq, :]
            s = _dot(qb, kb, NT) * MLA_SCALE
            p = jnp.exp(s - lse_ref[0, rq, :])
            if masked:
                row = lax.broadcasted_iota(jnp.int32, s.shape, 0)
                col = lax.broadcasted_iota(jnp.int32, s.shape, 1)
                p = jnp.where(col <= row, p, 0.0)
            dp = _dot(dob, vb, NT)
            ds = (p * (dp - dd_ref[rq, :]) * MLA_SCALE).astype(BF16)
            dv_acc = dv_acc + _dot(p.astype(BF16), dob, TN_)
            dk_acc = dk_acc + _dot(ds, qb, TN_)
            dq_ref[rq, :] += _dot(ds, kb, NN)
            return dk_acc, dv_acc

        def kloop(j, _):
            init = (jnp.zeros((BQ, HEAD_W), F32), jnp.zeros((BQ, HEAD_W), F32))
            carry = step(j, j, init, True)
            dk_acc, dv_acc = lax.fori_loop(j + 1, nq, lambda i, c: step(j, i, c, False), carry)
            rk = pl.ds(pl.multiple_of(j * BQ, BQ), BQ)
            dk_ref[rk, :] = dk_acc
            dv_ref[rk, :] = dv_acc
            return 0

        lax.fori_loop(0, nq, kloop, 0)

    head = pl.BlockSpec((T, HEAD_W), lambda h: (0, h))
    nbytes = 4 * _nbytes((T, HEAD_W), BF16) + 5 * _nbytes((T, HEAD_W), F32) + 2 * _nbytes((T, 128), F32)
    return pl.pallas_call(
        kern, name="attn_bwd", grid=(HEADS,),
        in_specs=[head, head, head, head, pl.BlockSpec((1, T, 1), lambda h: (h, 0, 0)), head] + [ANY_SPEC] * len(deps),
        out_specs=[head, head, head],
        out_shape=[pltpu.HBM((T, HEADS * HEAD_W), F32)] * 3,
        scratch_shapes=[pltpu.VMEM((T, 1), F32)],
        compiler_params=pltpu.CompilerParams(dimension_semantics=("parallel",), vmem_limit_bytes=_vmem(nbytes)),
    )(*[_hbm(a) for a in (q, k, v, o, lse, dcat)], *deps)


def _sgu_common(u, v, ln_g, ln_b):
    ua, tu = _gelu(u)
    va, tv = _gelu(v)
    vh, r = _ln_stats(va)
    return ua, tu, tv, vh, r, vh * ln_g + ln_b


def _tril_mask(n):
    return lax.broadcasted_iota(jnp.int32, (n, n), 1) <= lax.broadcasted_iota(jnp.int32, (n, n), 0)


def _sgu_fwd(zs, ln_g, ln_b, w, bias_full):
    def body(rows, consts, outs, accs):
        ua, _, _, _, _, vn = _sgu_common(rows[0][...], rows[1][...], consts[0][...], consts[1][...])
        vn = vn.astype(BF16)
        tri = _tril_mask(SGU_CHUNK)
        for g in range(SGU_G):
            wg = jnp.where(tri, consts[2][0, g], 0.0).astype(BF16)
            cols = slice(g * 128, (g + 1) * 128)
            for c in range(ua.shape[0] // SGU_CHUNK):
                rws = slice(c * SGU_CHUNK, (c + 1) * SGU_CHUNK)
                mixed = _dot(wg, vn[rws, cols], NN) + consts[3][:, cols]
                outs[0][rws, cols] = (ua[rws, cols] * mixed).astype(BF16)

    return _rowwise("sgu_fwd", body, [(zs, 512, 0), (zs, 512, 1)], [ln_g, ln_b, w, bias_full], [(SGU_DIM, BF16)])


def _sgu_bwd(zs, dcat, ln_g, ln_b, w, bias_full):
    def body(rows, consts, outs, accs):
        u, v = rows[0][...], rows[1][...]
        ua, tu, tv, vh, r, vn = _sgu_common(u, v, consts[0][...], consts[1][...])
        dout = rows[2][...].astype(F32)
        vn_bf = vn.astype(BF16)
        tri = _tril_mask(SGU_CHUNK)
        dmixed = (dout * ua)
        dmixed_bf = dmixed.astype(BF16)
        ones = jnp.ones((8, SGU_CHUNK), F32)
        dvn_cols, mixed_cols = [], []
        for g in range(SGU_G):
            wg = jnp.where(tri, consts[2][0, g], 0.0).astype(BF16)
            cols = slice(g * 128, (g + 1) * 128)
            dvn_rows, mixed_rows = [], []
            dw = jnp.zeros((SGU_CHUNK, SGU_CHUNK), F32)
            dmix_sum = jnp.zeros((SGU_CHUNK, 128), F32)
            for c in range(u.shape[0] // SGU_CHUNK):
                rws = slice(c * SGU_CHUNK, (c + 1) * SGU_CHUNK)
                mixed_rows.append(_dot(wg, vn_bf[rws, cols], NN) + consts[3][:, cols])
                dvn_rows.append(_dot(wg, dmixed_bf[rws, cols], TN_))
                dw = dw + _dot(dmixed_bf[rws, cols], vn_bf[rws, cols], NT)
                dmix_sum = dmix_sum + dmixed[rws, cols]
            accs[0][g] += jnp.where(tri, dw, 0.0)
            accs[3][g:g + 1, :] += _dot(ones, dmix_sum, NT, precision=HIGHEST)[0:1, :]
            dvn_cols.append(jnp.concatenate(dvn_rows, axis=0))
            mixed_cols.append(jnp.concatenate(mixed_rows, axis=0))
        dvn = jnp.concatenate(dvn_cols, axis=1)
        mixed = jnp.concatenate(mixed_cols, axis=1)
        accs[1][...] += jnp.sum(dvn * vh, axis=0, keepdims=True)
        accs[2][...] += jnp.sum(dvn, axis=0, keepdims=True)
        dvh = dvn * consts[0][...]
        dva = r * (dvh - jnp.mean(dvh, axis=-1, keepdims=True) - vh * jnp.mean(dvh * vh, axis=-1, keepdims=True))
        outs[0][:, 0:512] = (dout * mixed * _gelu_grad(u, tu)).astype(BF16)
        outs[0][:, 512:1024] = (dva * _gelu_grad(v, tv)).astype(BF16)

    return _rowwise("sgu_bwd", body, [(zs, 512, 0), (zs, 512, 1), (dcat, 512, 2)], [ln_g, ln_b, w, bias_full], [(1024, BF16)],
                    [((SGU_G, 128, 128), F32), ((1, SGU_DIM), F32), ((1, SGU_DIM), F32), ((SGU_G, 128), F32)], tr=256)


def _lower_bound(hg_lb):
    a0, a1 = hg_lb[0:1, :], hg_lb[1:2, :]
    m = jnp.maximum(a0, a1)
    e0, e1 = jnp.exp(a0 - m), jnp.exp(a1 - m)
    s0, s1 = e0 / (e0 + e1), e1 / (e0 + e1)
    return (s0 + s1) - s0, s0, s1


def _prefix_rows(x, reverse=False):
    n = x.shape[0]
    row = lax.broadcasted_iota(jnp.int32, x.shape, 0)
    s = 1
    while s < n:
        if reverse:
            x = x + jnp.where(row < n - s, pltpu.roll(x, n - s, 0), 0.0)
        else:
            x = x + jnp.where(row >= s, pltpu.roll(x, s, 0), 0.0)
        s *= 2
    return x


def _hg_gates(qr, fr, lb):
    C = qr.shape[0]
    sq = _sig(qr)
    qf = qr * sq
    sf = _sig(fr)
    gate = lb + (1.0 - lb) * sf
    kk = 1.0 - gate
    tri = _tril_mask(C)
    b = _prefix_rows(jnp.log(gate))
    bref = b[C // 2 - 1:C // 2, :]
    bl = b[C - 1:C, :]
    e_b = jnp.exp(b)
    e_q = jnp.exp(b - bref)
    e_k = jnp.exp(bref - b)
    e_lb = jnp.exp(bl - b)
    return dict(sq=sq, qf=qf, sf=sf, gate=gate, kk=kk, tri=tri, bl=bl, e_b=e_b, e_q=e_q, e_k=e_k, e_lb=e_lb)


def _hgrn_fwd(z1, hg_lb, gnorm):
    T = z1.shape[0]
    C = min(HG_CHUNK, T)
    nc = T // C
    ns = HG_CHUNKS_PER_STEP if nc % HG_CHUNKS_PER_STEP == 0 else 1
    R = ns * C

    def kern(q_ref, f_ref, i_ref, g_ref, lb_ref, gn_ref, o_ref, hg_ref, st_ref, s_scr):
        @pl.when(pl.program_id(0) == 0)
        def _():
            s_scr[...] = jnp.zeros(s_scr.shape, F32)

        lb_all, _, _ = _lower_bound(lb_ref[...])
        for sub in range(ns):
            rows = slice(sub * C, (sub + 1) * C)
            st_ref[sub] = s_scr[...]
            for h in range(HEADS):
                cols = slice(h * HEAD_W, (h + 1) * HEAD_W)
                t = _hg_gates(q_ref[rows, cols], f_ref[rows, cols], lb_all[:, cols])
                v_bf = i_ref[rows, cols].astype(BF16)
                st = s_scr[h]
                a = jnp.where(t["tri"], _dot((t["qf"] * t["e_q"]).astype(BF16), (t["kk"] * t["e_k"]).astype(BF16), NT), 0.0)
                o = _dot(a.astype(BF16), v_bf, NN) + _dot((t["qf"] * t["e_b"]).astype(BF16), st.astype(BF16), NT)
                s_scr[h] = st * jnp.exp(t["bl"]) + _dot(v_bf, (t["kk"] * t["e_lb"]).astype(BF16), TN_)
                o_ref[rows, cols] = o
                gr = g_ref[rows, cols]
                r = lax.rsqrt(jnp.mean(o * o, axis=-1, keepdims=True) + EPS)
                hg_ref[rows, cols] = (o * r * gn_ref[:, cols] * (gr * _sig(gr))).astype(BF16)

    seg = lambda k: pl.BlockSpec((R, D_MODEL), functools.partial(lambda n, k: (n, k), k=k))
    row = pl.BlockSpec((R, D_MODEL), lambda n: (n, 0))
    nbytes = 6 * _nbytes((R, D_MODEL), F32) + (2 + ns) * _nbytes((HEADS, 128, 128), F32)
    return pl.pallas_call(
        kern, name="hgrn_fwd", grid=(nc // ns,),
        in_specs=[seg(0), seg(1), seg(2), seg(3), pl.BlockSpec((2, D_MODEL), lambda n: (0, 0)),
                  pl.BlockSpec((1, D_MODEL), lambda n: (0, 0))],
        out_specs=[row, row, pl.BlockSpec((ns, HEADS, 128, 128), lambda n: (n, 0, 0, 0))],
        out_shape=[pltpu.HBM((T, D_MODEL), F32), pltpu.HBM((T, D_MODEL), BF16),
                   pltpu.HBM((nc, HEADS, 128, 128), F32)],
        scratch_shapes=[pltpu.VMEM((HEADS, 128, 128), F32)],
        compiler_params=pltpu.CompilerParams(dimension_semantics=("arbitrary",), vmem_limit_bytes=_vmem(nbytes)),
    )(*[_hbm(a) for a in (z1, z1, z1, z1, hg_lb, gnorm)])


def _hgrn_bwd(z1, o_pre, dhg, states, hg_lb, gnorm):
    T = z1.shape[0]
    C = min(HG_CHUNK, T)
    nc = T // C
    ns = HG_CHUNKS_PER_STEP if nc % HG_CHUNKS_PER_STEP == 0 else 1
    R, steps = ns * C, nc // ns

    def kern(q_ref, f_ref, i_ref, g_ref, o_ref, dhg_ref, st_ref, lb_ref, gn_ref, dz_ref, dlb_ref, dgn_ref, ds_scr, dlb_scr):
        n = pl.program_id(0)

        @pl.when(n == 0)
        def _():
            ds_scr[...] = jnp.zeros(ds_scr.shape, F32)
            dlb_scr[...] = jnp.zeros(dlb_scr.shape, F32)
            dgn_ref[...] = jnp.zeros(dgn_ref.shape, F32)

        lb_all, s0, s1 = _lower_bound(lb_ref[...])
        for sub in reversed(range(ns)):
            rows = slice(sub * C, (sub + 1) * C)
            for h in range(HEADS):
                cols = slice(h * HEAD_W, (h + 1) * HEAD_W)
                lb = lb_all[:, cols]
                qr, fr = q_ref[rows, cols], f_ref[rows, cols]
                t = _hg_gates(qr, fr, lb)
                tri = t["tri"]
                v_bf = i_ref[rows, cols].astype(BF16)
                st_bf = st_ref[sub, h].astype(BF16)
                dst = ds_scr[h]
                dst_bf = dst.astype(BF16)
                o = o_ref[rows, cols]
                gr = g_ref[rows, cols]
                sg = _sig(gr)
                sil = gr * sg
                gn = gn_ref[:, cols]
                r = lax.rsqrt(jnp.mean(o * o, axis=-1, keepdims=True) + EPS)
                on = o * r
                dh = dhg_ref[rows, cols].astype(F32)
                dgn_ref[:, cols] += jnp.sum(dh * on * sil, axis=0, keepdims=True)
                dg = dh * on * gn * (sg * (1.0 + gr * (1.0 - sg)))
                don = dh * gn * sil
                do_bf = (r * (don - on * jnp.mean(don * on, axis=-1, keepdims=True))).astype(BF16)
                qe = (t["qf"] * t["e_q"]).astype(BF16)
                ke = (t["kk"] * t["e_k"]).astype(BF16)
                qb = (t["qf"] * t["e_b"]).astype(BF16)
                kh_bf = (t["kk"] * t["e_lb"]).astype(BF16)
                a_bf = jnp.where(tri, _dot(qe, ke, NT), 0.0).astype(BF16)
                da_bf = jnp.where(tri, _dot(do_bf, v_bf, NT), 0.0).astype(BF16)
                dv = _dot(a_bf, do_bf, TN_) + _dot(kh_bf, dst_bf, NT)
                dqe = _dot(da_bf, ke, NN)
                dqb = _dot(do_bf, st_bf, NN)
                dke = _dot(da_bf, qe, TN_)
                dkh = _dot(v_bf, dst_bf, NN)
                dqf = dqe * t["e_q"] + dqb * t["e_b"]
                dkk = dke * t["e_k"] + dkh * t["e_lb"]
                kh_r = kh_bf.astype(F32)
                db = qe.astype(F32) * dqe - ke.astype(F32) * dke + qb.astype(F32) * dqb - kh_r * dkh
                e_bl = jnp.exp(t["bl"])
                dbl = jnp.sum(dkh * kh_r, axis=0, keepdims=True) + e_bl * jnp.sum(st_ref[sub, h] * dst, axis=0, keepdims=True)
                dlg = _prefix_rows(db, reverse=True) + dbl
                ds_scr[h] = dst * e_bl + _dot(do_bf, qb, TN_)
                dgate = dlg / t["gate"] - dkk
                sf = t["sf"]
                dlb_scr[:, cols] += jnp.sum(dgate * (1.0 - sf), axis=0, keepdims=True)
                df = dgate * (1.0 - lb) * sf * (1.0 - sf)
                dq = dqf * (t["sq"] * (1.0 + qr * (1.0 - t["sq"])))
                dz_ref[rows, cols] = dq.astype(BF16)
                dz_ref[rows, D_MODEL + h * HEAD_W:D_MODEL + (h + 1) * HEAD_W] = df.astype(BF16)
                dz_ref[rows, 2 * D_MODEL + h * HEAD_W:2 * D_MODEL + (h + 1) * HEAD_W] = dv.astype(BF16)
                dz_ref[rows, 3 * D_MODEL + h * HEAD_W:3 * D_MODEL + (h + 1) * HEAD_W] = dg.astype(BF16)

        @pl.when(n == steps - 1)
        def _():
            d = s0 * s1 * dlb_scr[...]
            dlb_ref[0:1, :] = -d
            dlb_ref[1:2, :] = d

    seg = lambda k: pl.BlockSpec((R, D_MODEL), functools.partial(lambda n, k: (steps - 1 - n, k), k=k))
    nbytes = 6 * _nbytes((R, D_MODEL), F32) + _nbytes((R, 4 * D_MODEL), BF16) + (2 + ns) * _nbytes((HEADS, 128, 128), F32)
    return pl.pallas_call(
        kern, name="hgrn_bwd", grid=(steps,),
        in_specs=[seg(0), seg(1), seg(2), seg(3), seg(0), seg(0),
                  pl.BlockSpec((ns, HEADS, 128, 128), lambda n: (steps - 1 - n, 0, 0, 0)),
                  pl.BlockSpec((2, D_MODEL), lambda n: (0, 0)), pl.BlockSpec((1, D_MODEL), lambda n: (0, 0))],
        out_specs=[pl.BlockSpec((R, 4 * D_MODEL), lambda n: (steps - 1 - n, 0)),
                   pl.BlockSpec((2, D_MODEL), lambda n: (0, 0)), pl.BlockSpec((1, D_MODEL), lambda n: (0, 0))],
        out_shape=[pltpu.HBM((T, 4 * D_MODEL), BF16), pltpu.HBM((2, D_MODEL), F32),
                   pltpu.HBM((1, D_MODEL), F32)],
        scratch_shapes=[pltpu.VMEM((HEADS, 128, 128), F32), pltpu.VMEM((1, D_MODEL), F32)],
        compiler_params=pltpu.CompilerParams(dimension_semantics=("arbitrary",), vmem_limit_bytes=_vmem(nbytes)),
    )(*[_hbm(a) for a in (z1, z1, z1, z1, o_pre, dhg, states, hg_lb, gnorm)])


def _prep_weights(gw):
    w_in_e = gw["w_in_e"].transpose(1, 0, 2).reshape(D_MODEL, 1568)
    kr = jnp.pad(w_in_e[:, 512:544], ((0, 0), (64, 32)))
    wm = jnp.concatenate([w_in_e[:, 0:512], kr], axis=1)
    ws = w_in_e[:, 544:1568]
    w_qb = gw["w_qb"].transpose(1, 0, 2).reshape(MLA_LORA, HEADS, 96)
    wq = jnp.pad(w_qb, ((0, 0), (0, 0), (0, 32))).reshape(MLA_LORA, HEADS * HEAD_W)
    kvb = gw["w_kvb"].transpose(1, 0, 2).reshape(MLA_LORA, HEADS, 128)
    wk = jnp.pad(kvb[:, :, :64], ((0, 0), (0, 0), (0, 64))).reshape(MLA_LORA, HEADS * HEAD_W)
    wv = jnp.pad(kvb[:, :, 64:], ((0, 0), (0, 0), (0, 64))).reshape(MLA_LORA, HEADS * HEAD_W)
    w_out_e = gw["w_out_e"].reshape(D_MODEL, D_MODEL)
    woa = jnp.pad(w_out_e[:512].reshape(HEADS, 64, D_MODEL), ((0, 0), (0, 64), (0, 0))).reshape(HEADS * HEAD_W, D_MODEL)
    return dict(wm=wm, ws=ws, wq=wq, wk=wk, wv=wv, woa=woa, wob=w_out_e[512:])


def _unprep_grads(g):
    dwm, dws = g["wm"], g["ws"]
    d_in_e = jnp.concatenate([dwm[:, 0:512], dwm[:, 512 + 64:512 + 96], dws], axis=1)
    d_qb = g["wq"].reshape(MLA_LORA, HEADS, HEAD_W)[:, :, :96].reshape(MLA_LORA, HEADS * 96)
    dk = g["wk"].reshape(MLA_LORA, HEADS, HEAD_W)[:, :, :64]
    dv = g["wv"].reshape(MLA_LORA, HEADS, HEAD_W)[:, :, :64]
    d_kvb = jnp.concatenate([dk, dv], axis=2).reshape(MLA_LORA, HEADS * 128)
    d_oa = g["woa"].reshape(HEADS, HEAD_W, D_MODEL)[:, :64].reshape(HEADS * 64, D_MODEL)
    dev_major = lambda a: a.reshape(a.shape[0], N_DEV, a.shape[1] // N_DEV).transpose(1, 0, 2)
    return dict(w_in_e=dev_major(d_in_e), w_qb=dev_major(d_qb), w_kvb=dev_major(d_kvb),
                w_out_e=jnp.concatenate([d_oa, g["wob"]], axis=0).reshape(N_DEV, D_MODEL // N_DEV, D_MODEL))


def _local_step(x, positions, target, gw, sp, ex):
    w = _prep_weights(gw)
    T = x.shape[0]
    tm = min(TM, T)
    nt = T // tm
    half = MLA_ROPE // 2
    inv_freq = ROPE_BASE ** (-jnp.arange(half, dtype=F32) / half)
    invf_lane = jnp.concatenate([jnp.zeros((64,), F32), inv_freq, inv_freq, jnp.zeros((32,), F32)]).reshape(1, HEAD_W)
    tabs = _rope_tables(positions.reshape(T, 1), invf_lane)
    bias_full = jnp.repeat(sp["sgu_b"][0].T, 128, axis=1)
    sgu_w = sp["sgu_w"]
    gq, gkv = sp["mla_gq"], sp["mla_gkv"]
    ln1_g, ln1_b, ln2_g, ln2_b = sp["ln1_g"], sp["ln1_b"], sp["ln2_g"], sp["ln2_b"]
    zm, zs, cqn, ckvn, kr_rot = _mla_in(x, w["wm"], w["ws"], tabs, gq, gkv, deps=[ex.first_token])
    q, k, v = _mla_qkv(cqn, ckvn, kr_rot, tabs, w["wq"], w["wk"], w["wv"])
    o_att, lse = _attn_fwd(q, k, v)
    b_out = _sgu_fwd(zs, sp["sgu_ln_g"], sp["sgu_ln_b"], sgu_w, bias_full)
    token = ex.weights_forward(after=[o_att, b_out])
    y1, h1, h1_bf = _proj_ln("l0_out_ln1", [o_att, b_out], [w["woa"], w["wob"]], x, ln1_g, ln1_b, 0, deps=[token])
    big = ex.weights_ready(after=[y1])
    w_ff1, w_in_o, w_out_o = big["w_ff1"], big["w_in_o"], big["w_out_o"].reshape(D_MODEL, D_MODEL)
    w_ff2 = [a.reshape(D_FF, D_MODEL) for a in big["w_ff2"]]
    a0, act0 = _mlp_up("l0", h1_bf, w_ff1[0])
    y2, h2, h2_bf = _proj_ln("l0_ff2_ln2", [act0], [w_ff2[0]], h1, ln2_g, ln2_b, 0)

    z1 = _tiled("l1_in", (1, nt), [_rb(h2_bf, tm), _res(w_in_o)], [_out(T, 4 * D_MODEL, F32, tm, 4 * D_MODEL)],
                _mmc_blocks(N_DEV, NN, lambda w, d: w[d]), direct=True)
    o_pre, hg, states = _hgrn_fwd(z1, sp["hg_lb"], sp["hg_gnorm"])
    y3, h3, h3_bf = _proj_ln("l1_out_ln1", [hg], [w_out_o], h2, ln1_g, ln1_b, 1)
    a1, act1 = _mlp_up("l1", h3_bf, w_ff1[1])

    gs, g0 = {}, {}
    dy4, dy4_bf, sq_err, gs["ln2_g1"], gs["ln2_b1"] = _proj_ln_loss("l1_ff2_loss", act1, w_ff2[1], h3, ln2_g, ln2_b, 1, target)
    gs["sq_err"] = sq_err
    da1, dw1_1, dw2_1 = _mlp_bwd_w("l1", h3_bf, a1, act1, dy4_bf, big["w_ff2"][1])
    dy3, dy3_bf, dhg, gs["ln1_g1"], gs["ln1_b1"] = _dh_ln_back("l1_dh_ln1", da1, w_ff1[1], dy4, y3, ln1_g, 1, proj=[w_out_o])
    d_out_o = _tiled("l1_dwout", (2, D_MODEL // TM), [_tl(hg, TM), _cw(dy3_bf, TN)], [_out(D_MODEL, D_MODEL, F32, TM, TN)],
                     _mmc(TN_)).reshape(N_DEV, D_MODEL // N_DEV, D_MODEL)
    dz1, gs["hg_lb"], gs["hg_gnorm"] = _hgrn_bwd(z1, o_pre, dhg, states, sp["hg_lb"], sp["hg_gnorm"])
    d_in_o = _tiled("l1_dwin", (N_DEV, 1), [_res(h2_bf), _cw(dz1, TN)], [_out_dev(D_MODEL, TN, D_MODEL)], _mmc(TN_))
    token = ex.grads_start("l1", [dw1_1, dw2_1, d_in_o, d_out_o])

    dy2, dy2_bf, gs["ln2_g0"], gs["ln2_b0"] = _dh_ln_back("l1_dh_ln2", dz1, w_in_o, dy3, y2, ln2_g, 0, deps=[token])
    token = ex.grads_middle("l1", after=dy2)
    da0, dw1_0, dw2_0 = _mlp_bwd_w("l0", h1_bf, a0, act0, dy2_bf, big["w_ff2"][0], deps=[token])
    token = ex.grads_start("l0m", [dw1_0, dw2_0])
    dy1, dy1_bf, dcat, gs["ln1_g0"], gs["ln1_b0"] = _dh_ln_back("l0_dh_ln1", da0, w_ff1[0], dy2, y1, ln1_g, 0,
                                                                 proj=[w["woa"], w["wob"]], deps=[token])
    ex.grads_end("l1", after=dy1)
    g0["woa"], g0["wob"] = _out_weight_grads(o_att, b_out, dy1_bf)
    token = ex.grads_middle("l0m", after=g0["wob"])
    dzs, gs["sgu_w"], gs["sgu_ln_g"], gs["sgu_ln_b"], gs["sgu_b"] = _sgu_bwd(zs, dcat, sp["sgu_ln_g"], sp["sgu_ln_b"], sgu_w, bias_full)
    dq, dk, dv = _attn_bwd(q, k, v, o_att, lse, dcat, deps=[token])
    ex.grads_end("l0m", after=dq)
    dzm, g0["wq"], g0["wk"], g0["wv"], gs["mla_gq"], gs["mla_gkv"] = _mla_back(zm, cqn, ckvn, tabs, gq, gkv, w["wq"], w["wk"], w["wv"],
                                                                                 dq, dk, dv)
    token = ex.small_start(gs)
    dx, g0["wm"], g0["ws"] = _in_back(x, dzm, dzs, dy1, w["wm"], w["ws"], deps=[token])

    return sq_err, dx, _unprep_grads(g0), gs


def _me():
    return lax.axis_index("x"), lax.axis_index("y"), lax.axis_index("c")


ANY_SPEC = pl.BlockSpec(memory_space=pl.ANY)
HBM_SPEC = pl.BlockSpec(memory_space=pltpu.HBM)
SEM_SPEC = pl.BlockSpec(memory_space=pltpu.SEMAPHORE)
EFFECT = pltpu.SideEffectType.DATAFLOW_SIDE_EFFECTING


def _split_start(name, srcs, lands, n_sems, make_copies, after=()):
    n, m, k = len(srcs), len(lands), len(after)

    def body(*refs):
        for cp in make_copies(refs[:n], refs[n:n + m], refs[n + m + k], refs[n + m + k + 1]):
            cp.start()
        refs[-1][...] = jnp.zeros(refs[-1].shape, F32)

    out_shape = (pltpu.SemaphoreType.DMA((n_sems,)), pltpu.SemaphoreType.DMA((n_sems,)),
                 *[pltpu.HBM(a.shape, a.dtype) for a in (*srcs, *lands)], jax.ShapeDtypeStruct((8, 128), F32))
    res = pl.pallas_call(
        body, name=name, out_shape=out_shape, in_specs=[HBM_SPEC] * (n + m) + [ANY_SPEC] * k,
        out_specs=(SEM_SPEC, SEM_SPEC, *[HBM_SPEC] * (n + m), pl.BlockSpec(memory_space=pltpu.VMEM)),
        input_output_aliases={i: 2 + i for i in range(n + m)},
        compiler_params=pltpu.CompilerParams(has_side_effects=EFFECT),
    )(*[_hbm(a) for a in (*srcs, *lands)], *after)
    return res[0], res[1], list(res[2:2 + n]), list(res[2 + n:2 + n + m]), res[-1]


def _split_wait(name, send_sems, recv_sems, srcs, lands, after, make_copies):
    n, m = len(srcs), len(lands)

    def body(*refs):
        for cp in make_copies(refs[:n], refs[n:n + m], refs[n + m], refs[n + m + 1]):
            cp.wait_send()
            cp.wait_recv()

    res = pl.pallas_call(
        body, name=name, out_shape=tuple(pltpu.HBM(a.shape, a.dtype) for a in (*srcs, *lands)),
        in_specs=[HBM_SPEC] * (n + m) + [SEM_SPEC, SEM_SPEC] + [ANY_SPEC] * len(after), out_specs=tuple([HBM_SPEC] * (n + m)),
        input_output_aliases={i: i for i in range(n + m)},
        compiler_params=pltpu.CompilerParams(has_side_effects=EFFECT),
    )(*srcs, *lands, send_sems, recv_sems, *after)
    return list(res[:n]), list(res[n:])


def _place_own(shards, dev):
    n = len(shards)

    def kern(dev_ref, *refs):
        for x_ref, o_ref in zip(refs[:n], refs[n:]):
            o_ref[...] = x_ref[...].astype(o_ref.dtype)

    blocks = [(None, *a.shape[1:]) for a, _, _ in shards]
    nbytes = sum(_nbytes(b, a.dtype) + _nbytes(b, dt) for b, (a, _, dt) in zip(blocks, shards))
    return pl.pallas_call(
        kern, name="weights_place_own", out_shape=[pltpu.HBM((N_DEV, *a.shape[1:]), dt) for a, _, dt in shards],
        grid_spec=pltpu.PrefetchScalarGridSpec(
            num_scalar_prefetch=1, grid=(1,),
            in_specs=[pl.BlockSpec(b, functools.partial(lambda i, dev, l: (l, 0, 0), l=l)) for b, (_, l, _) in zip(blocks, shards)],
            out_specs=[pl.BlockSpec(b, lambda i, dev: (dev[0], 0, 0)) for b in blocks]),
        compiler_params=pltpu.CompilerParams(dimension_semantics=("arbitrary",), vmem_limit_bytes=_vmem(nbytes)),
    )(dev, *[_hbm(a) for a, _, _ in shards])


def _ag_first_copies(src_refs, out_refs, send_sems, recv_sems):
    x, y, c = _me()
    targets = [(x, y, 1 - c), (1 - x, y, c), (x, 1 - y, c), (1 - x, 1 - y, c)]
    return [pltpu.make_async_remote_copy(
        src_ref=out_refs[op].at[4 * x + 2 * y + c], dst_ref=out_refs[op].at[4 * x + 2 * y + c], send_sem=send_sems.at[4 * op + k],
        recv_sem=recv_sems.at[4 * op + k], device_id=to, device_id_type=MESH)
        for op in range(len(out_refs)) for k, to in enumerate(targets)]


def _ag_second_copies(src_refs, out_refs, send_sems, recv_sems):
    x, y, c = _me()
    chips = [(1 - x, y), (x, 1 - y), (1 - x, 1 - y)]
    return [pltpu.make_async_remote_copy(
        src_ref=out_refs[op].at[4 * cx + 2 * cy + c], dst_ref=out_refs[op].at[4 * cx + 2 * cy + c],
        send_sem=send_sems.at[3 * op + j], recv_sem=recv_sems.at[3 * op + j], device_id=(x, y, 1 - c), device_id_type=MESH)
        for op in range(len(out_refs)) for j, (cx, cy) in enumerate(chips)]


def _rs_sibling_copies(g_refs, out_refs, send_sems, recv_sems):
    x, y, c = _me()
    return [pltpu.make_async_remote_copy(
        src_ref=g_refs[op].at[k, 1 - c], dst_ref=out_refs[op].at[k], send_sem=send_sems.at[4 * op + k],
        recv_sem=recv_sems.at[4 * op + k], device_id=(x, y, 1 - c), device_id_type=MESH)
        for op in range(len(g_refs)) for k in range(4)]


def _rs_chip_copies(p_refs, out_refs, send_sems, recv_sems):
    x, y, c = _me()
    chips = [(1 - x, y), (x, 1 - y), (1 - x, 1 - y)]
    return [pltpu.make_async_remote_copy(
        src_ref=p_refs[op].at[2 * cx + cy], dst_ref=out_refs[op].at[j], send_sem=send_sems.at[3 * op + j],
        recv_sem=recv_sems.at[3 * op + j], device_id=(cx, cy, c), device_id_type=MESH)
        for op in range(len(p_refs)) for j, (cx, cy) in enumerate(chips)]


def _all_gather(placed):
    n = len(placed)

    def kern(*refs):
        in_refs, out_refs, (send_sems, recv_sems) = refs[:n], refs[n:2 * n], refs[2 * n:]
        x, y, c = _me()
        me, sibling = (x, y, c), (x, y, 1 - c)
        chips = [(1 - x, y), (x, 1 - y), (1 - x, 1 - y)]

        def copy(op, k, block, to, own=False):
            idx = 4 * block[0] + 2 * block[1] + block[2]
            return pltpu.make_async_remote_copy(
                src_ref=(in_refs if own else out_refs)[op].at[idx], dst_ref=out_refs[op].at[idx], send_sem=send_sems.at[7 * op + k],
                recv_sem=recv_sems.at[7 * op + k], device_id=to, device_id_type=MESH)

        first = []
        for op in range(n):
            first.append(copy(op, 0, me, sibling, own=True))
            first += [copy(op, 1 + j, me, (*chip, c), own=True) for j, chip in enumerate(chips)]
        for cp in first:
            cp.start()
        passed = []
        for j, chip in enumerate(chips):
            for op in range(n):
                copy(op, 1 + j, (*chip, c), me).wait_recv()
                passed.append(copy(op, 4 + j, (*chip, c), sibling))
                passed[-1].start()
        for op in range(n):
            copy(op, 0, sibling, me).wait_recv()
            for j, chip in enumerate(chips):
                copy(op, 4 + j, (*chip, 1 - c), me).wait_recv()
        for cp in first + passed:
            cp.wait_send()

    return pl.pallas_call(
        kern, name="weights_all_gather", out_shape=[pltpu.HBM(g.shape, g.dtype) for g in placed],
        in_specs=[ANY_SPEC] * n, out_specs=[ANY_SPEC] * n, input_output_aliases={i: i for i in range(n)},
        scratch_shapes=[pltpu.SemaphoreType.DMA((7 * n,)), pltpu.SemaphoreType.DMA((7 * n,))],
    )(*[_hbm(a) for a in placed])


def _row_tile(r, w, n_blocks):
    tr = r
    while tr > 8 and 2 * n_blocks * tr * w * 4 > 24 * 2**20:
        tr //= 2
    return tr


def _chip_sum(name, g, from_sibling, core):
    _, _, R, W = g.shape
    tr = _row_tile(R, W, 3)

    def kern(core_ref, g_ref, s_ref, o_ref):
        o_ref[...] = (g_ref[...] + s_ref[...]).astype(BF16)

    return pl.pallas_call(
        kern, name=name, out_shape=pltpu.HBM((4, R, W), BF16),
        grid_spec=pltpu.PrefetchScalarGridSpec(
            num_scalar_prefetch=1, grid=(4, R // tr),
            in_specs=[pl.BlockSpec((None, None, tr, W), lambda k, i, core: (k, core[0], i, 0)),
                      pl.BlockSpec((None, tr, W), lambda k, i, core: (k, i, 0))],
            out_specs=pl.BlockSpec((None, tr, W), lambda k, i, core: (k, i, 0))),
        compiler_params=pltpu.CompilerParams(dimension_semantics=("parallel", "parallel"), vmem_limit_bytes=_vmem(3 * tr * W * 4)),
    )(core, _hbm(g), _hbm(from_sibling))


def _adamw(w, g, m, v):
    m = ADAM_B1 * m + (1.0 - ADAM_B1) * g
    v = ADAM_B2 * v + (1.0 - ADAM_B2) * (g * g)
    m_hat = m / (1.0 - ADAM_B1 ** ADAM_STEP)
    v_hat = v / (1.0 - ADAM_B2 ** ADAM_STEP)
    return -ADAM_LR * (m_hat / (jnp.sqrt(v_hat) + ADAM_EPS) + ADAM_WD * w), m, v


def _finish_sharded(name, layers, w, m, v, where, deps=()):
    nl, R, W = w.shape
    tr = _row_tile(R, W, 11 * nl)
    deps = _deps(deps)

    def kern(where_ref, *refs):
        w_ref, m_ref, v_ref = refs[3 * nl:3 * nl + 3]
        go_ref, d_ref, mo_ref, vo_ref = refs[3 * nl + 3 + len(deps):]
        for l in range(nl):
            g_ref, s_ref, c_ref = refs[3 * l:3 * l + 3]
            grad = g_ref[...] + s_ref[...]
            for j in range(3):
                grad = grad + c_ref[j].astype(F32)
            go_ref[l] = grad
            d_ref[l], mo_ref[l], vo_ref[l] = _adamw(w_ref[l], grad, m_ref[l], v_ref[l])

    row = pl.BlockSpec((nl, tr, W), lambda i, wh: (0, i, 0))
    in_specs, args = [], []
    for g, s, c in layers:
        in_specs += [pl.BlockSpec((None, None, tr, W), lambda i, wh: (wh[0], wh[1], i, 0)),
                     pl.BlockSpec((None, tr, W), lambda i, wh: (wh[0], i, 0)),
                     pl.BlockSpec((3, tr, W), lambda i, wh: (0, i, 0))]
        args += [g, s, c]
    return pl.pallas_call(
        kern, name=name, out_shape=[pltpu.HBM((nl, R, W), F32)] * 4,
        grid_spec=pltpu.PrefetchScalarGridSpec(num_scalar_prefetch=1, grid=(R // tr,),
                                               in_specs=in_specs + [row, row, row] + [ANY_SPEC] * len(deps),
                                               out_specs=[row, row, row, row]),
        compiler_params=pltpu.CompilerParams(dimension_semantics=("parallel",), vmem_limit_bytes=_vmem(nl * 11 * tr * W * 4)),
    )(where, *[_hbm(a) for a in (*args, w, m, v)], *deps)


SMALL_PLACE = (("mla_gq", 0, 0, 1, 256), ("mla_gkv", 0, 256, 1, 256), ("sgu_ln_g", 0, 512, 1, 512), ("sgu_ln_b", 1, 0, 1, 512),
               ("hg_lb", 2, 0, 2, 1024), ("ln1_g", 4, 0, 2, 1024), ("ln1_b", 6, 0, 2, 1024), ("sgu_b", 8, 0, 4, 128),
               ("ln2_g", 12, 0, 2, 1024), ("ln2_b", 14, 0, 2, 1024), ("hg_gnorm", 16, 0, 1, 1024))
SMALL_BUF_ROWS = 24
LOSS_ROW = 17


def _small_pack(gs, dev):
    pieces = [(gs["mla_gq"], 0, 0), (gs["mla_gkv"], 0, 256), (gs["sgu_ln_g"], 0, 512), (gs["sgu_ln_b"], 1, 0), (gs["hg_lb"], 2, 0),
              (gs["ln1_g0"], 4, 0), (gs["ln1_g1"], 5, 0), (gs["ln1_b0"], 6, 0), (gs["ln1_b1"], 7, 0), (gs["sgu_b"], 8, 0),
              (gs["ln2_g0"], 12, 0), (gs["ln2_g1"], 13, 0), (gs["ln2_b0"], 14, 0), (gs["ln2_b1"], 15, 0), (gs["hg_gnorm"], 16, 0),
              (gs["sq_err"], LOSS_ROW, 0)]
    n_p = len(pieces)

    def kern(dev_ref, *refs):
        a_ref, b_ref = refs[n_p + 1], refs[n_p + 2]
        a_ref[...] = jnp.zeros(a_ref.shape, F32)
        for ref, (_, r, l0) in zip(refs[:n_p], pieces):
            a_ref[r:r + ref.shape[0], l0:l0 + ref.shape[1]] = ref[...]
        b_ref[...] = refs[n_p][...]

    whole = lambda a: pl.BlockSpec(a.shape, functools.partial(lambda i, dev, nd: (0,) * nd, nd=a.ndim))
    return pl.pallas_call(
        kern, name="small_grads_pack",
        out_shape=[pltpu.HBM((N_DEV, SMALL_BUF_ROWS, D_MODEL), F32), pltpu.HBM((N_DEV, SGU_G, 128, 128), F32)],
        grid_spec=pltpu.PrefetchScalarGridSpec(
            num_scalar_prefetch=1, grid=(1,), in_specs=[whole(p[0]) for p in pieces] + [whole(gs["sgu_w"])],
            out_specs=[pl.BlockSpec((None, SMALL_BUF_ROWS, D_MODEL), lambda i, dev: (dev[0], 0, 0)),
                       pl.BlockSpec((None, SGU_G, 128, 128), lambda i, dev: (dev[0], 0, 0, 0))]),
    )(dev, *[p[0] for p in pieces], gs["sgu_w"])


def _small_copies(src_refs, land_refs, send_sems, recv_sems):
    px, py, pc = _me()
    me = 4 * px + 2 * py + pc
    return [pltpu.make_async_remote_copy(
        src_ref=land_refs[k].at[me], dst_ref=land_refs[k].at[me], send_sem=send_sems.at[2 * (r - 1) + k],
        recv_sem=recv_sems.at[2 * (r - 1) + k], device_id=(px ^ (r >> 2), py ^ ((r >> 1) & 1), pc ^ (r & 1)), device_id_type=MESH)
        for r in range(1, N_DEV) for k in range(2)]


def _small_adamw(slots_a, slots_b, given):
    names = [p[0] for p in SMALL_PLACE] + ["sgu_w"]
    n_names = len(names)
    wmv = [given[pre + name] for name in names for pre in ("", "m_", "v_")]
    vmem = pl.BlockSpec(memory_space=pltpu.VMEM)

    def kern(*refs):
        sum_a, sum_b = refs[0][0], refs[1][0]
        for d in range(1, N_DEV):
            sum_a, sum_b = sum_a + refs[0][d], sum_b + refs[1][d]
        wmv_refs, out_refs = refs[2:2 + 3 * n_names], refs[2 + 3 * n_names:]
        px, py, pc = _me()
        me = 4 * px + 2 * py + pc

        def own_block(full):
            acc = full[:, 0:128]
            for b in range(1, N_DEV):
                acc = jnp.where(me == b, full[:, b * 128:(b + 1) * 128], acc)
            return acc

        for idx, name in enumerate(names):
            w_ref, m_ref, v_ref = wmv_refs[3 * idx:3 * idx + 3]
            if name == "sgu_w":
                grad = sum_b[None]
            else:
                _, r, l0, nr, nl = SMALL_PLACE[idx]
                grad = sum_a[r:r + nr, l0:l0 + nl]
                if name == "hg_gnorm":
                    grad = own_block(grad)
                if name == "sgu_b":
                    grad = grad[None]
            res = (grad, *_adamw(w_ref[...], grad, m_ref[...], v_ref[...]))
            for o_ref, val in zip(out_refs[4 * idx:4 * idx + 4], res):
                o_ref[...] = val
        out_refs[4 * n_names][...] = (0.5 / D_MODEL) * jnp.sum(sum_a[LOSS_ROW:LOSS_ROW + 1, :], axis=1, keepdims=True)

    out_shape = [jax.ShapeDtypeStruct(given[name].shape, F32) for name in names for _ in range(4)]
    out_shape.append(jax.ShapeDtypeStruct((1, 1), F32))
    res = pl.pallas_call(
        kern, name="small_adamw", out_shape=out_shape, in_specs=[vmem] * (2 + len(wmv)), out_specs=[vmem] * len(out_shape),
    )(slots_a, slots_b, *wmv)
    out = {name: res[4 * idx:4 * idx + 4] for idx, name in enumerate(names)}
    out["loss"] = res[-1].reshape(())
    return out


class _Exchange:
    def __init__(self, given):
        self.given = given
        px, py, pc = _me()
        self.core = pc.reshape(1).astype(jnp.int32)
        self.dev = (4 * px + 2 * py + pc).reshape(1).astype(jnp.int32)
        self.where = jnp.stack([2 * px + py, pc]).astype(jnp.int32)
        self.state, self.layers = {}, {}

    def start_weights(self, lands, after):
        self.weights = _split_start("weights_first_start", [], lands, 4 * len(lands), _ag_first_copies, after=after)
        self.first_token = self.weights[4]

    def weights_forward(self, after):
        send_sems, recv_sems, shards, lands, _ = self.weights
        _, lands = _split_wait("weights_first_wait", send_sems, recv_sems, shards, lands, after, _ag_first_copies)
        self.weights = _split_start("weights_second_start", [], lands, 3 * len(lands), _ag_second_copies)
        return self.weights[4]

    def weights_ready(self, after):
        send_sems, recv_sems, shards, lands, _ = self.weights
        _, got = _split_wait("weights_second_wait", send_sems, recv_sems, shards, lands, after, _ag_second_copies)
        return dict(w_in_o=got[0], w_out_o=got[1], w_ff1=[got[2], got[3]], w_ff2=[got[4], got[5]])

    def small_start(self, gs):
        self.small = _split_start("small_grads_start", [], _small_pack(gs, self.dev), 14, _small_copies)
        return self.small[4]

    def small_finish(self, after):
        send_sems, recv_sems, _, lands, _ = self.small
        _, lands = _split_wait("small_grads_wait", send_sems, recv_sems, [], lands, after, _small_copies)
        return _small_adamw(lands[0], lands[1], self.given)

    def grads_start(self, tag, grads):
        blocks = [g.reshape(4, 2, *g.shape[1:]) for g in grads]
        lands = [lax.empty((4, *b.shape[2:]), F32) for b in blocks]
        self.state[tag] = _split_start(f"grads_{tag}_sibling_start", blocks, lands, 4 * len(blocks), _rs_sibling_copies)
        return self.state[tag][4]

    def grads_middle(self, tag, after):
        send_sems, recv_sems, blocks, lands, _ = self.state[tag]
        blocks, from_sibling = _split_wait(f"grads_{tag}_sibling_wait", send_sems, recv_sems, blocks, lands, [after], _rs_sibling_copies)
        sums = [_chip_sum(f"grads_{tag}_chip_sum_{k}", b, s, self.core) for k, (b, s) in enumerate(zip(blocks, from_sibling))]
        lands = [lax.empty((3, *p.shape[1:]), BF16) for p in sums]
        self.state[tag] = (blocks, from_sibling, _split_start(f"grads_{tag}_chips_start", sums, lands, 3 * len(sums), _rs_chip_copies))
        return self.state[tag][2][4]

    def grads_end(self, tag, after):
        blocks, from_sibling, (send_sems, recv_sems, sums, lands, _) = self.state[tag]
        after = list(after) if isinstance(after, (list, tuple)) else [after]
        _, from_chips = _split_wait(f"grads_{tag}_chips_wait", send_sems, recv_sems, sums, lands, after, _rs_chip_copies)
        self.layers[tag] = list(zip(blocks, from_sibling, from_chips))


SHARDED = ("w_in_e", "w_qb", "w_kvb", "w_out_e", "w_in_o", "w_out_o", "w_ff1", "w_ff2")


def kernel(x, positions, w_in_e, mla_gq, mla_gkv, w_qb, w_kvb, sgu_ln_g, sgu_ln_b, sgu_w, sgu_b, w_out_e, w_in_o, hg_lb, hg_gnorm, w_out_o, ln1_g, ln1_b, w_ff1, w_ff2, ln2_g, ln2_b, loss_target, m_w_in_e, m_mla_gq, m_mla_gkv, m_w_qb, m_w_kvb, m_sgu_ln_g, m_sgu_ln_b, m_sgu_w, m_sgu_b, m_w_out_e, m_w_in_o, m_hg_lb, m_hg_gnorm, m_w_out_o, m_ln1_g, m_ln1_b, m_w_ff1, m_w_ff2, m_ln2_g, m_ln2_b, v_w_in_e, v_mla_gq, v_mla_gkv, v_w_qb, v_w_kvb, v_sgu_ln_g, v_sgu_ln_b, v_sgu_w, v_sgu_b, v_w_out_e, v_w_in_o, v_hg_lb, v_hg_gnorm, v_w_out_o, v_ln1_g, v_ln1_b, v_w_ff1, v_w_ff2, v_ln2_g, v_ln2_b):
    given = dict(locals())
    ex = _Exchange(given)

    names = ["w_in_e", "w_qb", "w_kvb", "w_out_e"]
    placed = _place_own([(given[n], 0, BF16) for n in names] + [(hg_gnorm.reshape(1, 1, D_MODEL // N_DEV), 0, F32)]
                        + [(w_in_o, 0, BF16), (w_out_o, 0, BF16), (w_ff1, 0, BF16), (w_ff1, 1, BF16), (w_ff2, 0, BF16), (w_ff2, 1, BF16)],
                        ex.dev)
    got = _all_gather(placed[:5])
    ex.start_weights(placed[5:], after=[got[0]])
    gw = dict(zip(names, got[:4]))
    small_names = ["mla_gq", "mla_gkv", "sgu_ln_g", "sgu_ln_b", "sgu_w", "sgu_b", "hg_lb", "ln1_g", "ln1_b", "ln2_g", "ln2_b"]
    sp = {n: given[n] for n in small_names}
    sp["hg_gnorm"] = got[4].reshape(1, D_MODEL)

    _, dx, grads, gs = _local_step(x[0], positions[0], loss_target[0], gw, sp, ex)

    def finish(n, layers, deps=()):
        return _finish_sharded(f"finish_{n}", layers, given[n], given["m_" + n], given["v_" + n], ex.where, deps=deps)

    l1, l0m = ex.layers["l1"], ex.layers["l0m"]
    results = {}
    token = ex.grads_start("l0s", [grads[n] for n in names])
    results["w_ff1"] = finish("w_ff1", [l0m[0], l1[0]], deps=[token])
    token = ex.grads_middle("l0s", after=results["w_ff1"][0])
    results["w_ff2"] = finish("w_ff2", [l0m[1], l1[1]], deps=[token])
    results["w_in_o"] = finish("w_in_o", [l1[2]], deps=[token])
    results["w_out_o"] = finish("w_out_o", [l1[3]], deps=[token])
    results.update(ex.small_finish(after=[results["w_in_o"][0]]))
    ex.grads_end("l0s", after=[results[n][0] for n in ("mla_gq", "w_ff2", "w_in_o", "w_out_o")])
    for n, layer in zip(names, ex.layers["l0s"]):
        results[n] = finish(n, [layer])

    order = ["w_in_e", "mla_gq", "mla_gkv", "w_qb", "w_kvb", "sgu_ln_g", "sgu_ln_b", "sgu_w", "sgu_b", "w_out_e", "w_in_o",
             "hg_lb", "hg_gnorm", "w_out_o", "ln1_g", "ln1_b", "w_ff1", "w_ff2", "ln2_g", "ln2_b"]
    return (results["loss"], dx[None], *[results[name][kind] for kind in range(4) for name in order])
```

```python
import functools
import math

import jax
import jax.numpy as jnp
import numpy as np
from jax import lax
from jax.experimental import pallas as pl
from jax.experimental.pallas import tpu as pltpu

F32 = jnp.float32
BF16 = jnp.bfloat16
MESH = pl.DeviceIdType.MESH
HIGHEST = lax.Precision.HIGHEST

D_MODEL = 1024
D_FF = 4096
N_DEV = 8
HEADS = 8
HEAD_W = 128
MLA_NOPE = 64
MLA_ROPE = 32
MLA_V = 64
MLA_LORA = 256
MLA_SCALE = (MLA_NOPE + MLA_ROPE) ** -0.5
ROPE_BASE = 10000.0
SGU_DIM = 512
SGU_G = 4
SGU_CHUNK = 128
HG_CHUNK = 64
HG_CHUNKS_PER_STEP = 4
ALPHA = (2 * 2) ** 0.25
EPS = 1e-5
ADAM_LR, ADAM_B1, ADAM_B2, ADAM_EPS, ADAM_WD, ADAM_STEP = 0.001, 0.9, 0.999, 1e-08, 0.01, 10

VMEM_CAP_V7X = 56 * 2**20
VMEM_SLACK = 12 * 2**20
TM = 512
TN = 512


def _vmem(block_bytes):
    return int(min(VMEM_CAP_V7X, 2 * block_bytes + VMEM_SLACK))


def _hbm(a):
    return pltpu.with_memory_space_constraint(a, pltpu.HBM)


def _nbytes(shape, dtype):
    return int(np.prod([d for d in shape if d is not None])) * jnp.dtype(dtype).itemsize


def _sig(x):
    return 1.0 / (1.0 + jnp.exp(-x))


def _gelu(x):
    c = math.sqrt(2.0 / math.pi)
    t = jnp.tanh(c * (x + 0.044715 * x * x * x))
    return 0.5 * x * (1.0 + t), t


def _gelu_grad(x, t):
    c = math.sqrt(2.0 / math.pi)
    return 0.5 * (1.0 + t) + 0.5 * x * (1.0 - t * t) * c * (1.0 + 3 * 0.044715 * x * x)


def _dot(a, b, dims, precision=None):
    return lax.dot_general(a, b, (dims, ((), ())), preferred_element_type=F32, precision=precision)


NN = ((1,), (0,))
NT = ((1,), (1,))
TN_ = ((0,), (0,))


def _deps(deps):
    return [d for d in deps if d is not None]


def _tiled(name, grid, ins, outs, compute, direct=False, deps=()):
    n_in, deps = len(ins), _deps(deps)
    n_skip = n_in + len(deps)

    def kern(*refs):
        if direct:
            compute(refs[:n_in], refs[n_skip:])
            return
        for o_ref, r in zip(refs[n_skip:], compute(*refs[:n_in])):
            o_ref[...] = r.astype(o_ref.dtype).reshape(o_ref.shape)

    swap = lambda f: (lambda j, i: f(i, j))
    nbytes = sum(_nbytes(blk, a.dtype) for a, blk, _ in ins) + sum(_nbytes(blk, dt) + _nbytes(blk, F32) for _, dt, blk, _ in outs)
    res = pl.pallas_call(
        kern, name=name, grid=grid,
        in_specs=[pl.BlockSpec(blk, swap(f), pipeline_mode=pl.Buffered(1) if tuple(blk) == tuple(a.shape) else None)
                  for a, blk, f in ins] + [ANY_SPEC] * len(deps),
        out_specs=[pl.BlockSpec(blk, swap(f)) for _, _, blk, f in outs],
        out_shape=[pltpu.HBM(shape, dt) for shape, dt, _, _ in outs],
        compiler_params=pltpu.CompilerParams(dimension_semantics=("parallel", "parallel"), vmem_limit_bytes=_vmem(nbytes)),
    )(*[_hbm(a) for a, _, _ in ins], *deps)
    return res if len(res) > 1 else res[0]


def _rb(a, tm, w=None, cb=0):
    return (a, (tm, a.shape[1] if w is None else w), lambda i, j: (i, cb))


def _rbj(a, tm, tn):
    return (a, (tm, tn), lambda i, j: (i, j))


def _cw(b, tn):
    return (b, (b.shape[0], tn), lambda i, j: (0, j))


def _rw(b, tn):
    return (b, (tn, b.shape[1]), lambda i, j: (j, 0))


def _tl(a, tm):
    return (a, (a.shape[0], tm), lambda i, j: (0, i))


def _gcw(g):
    return (g, (None, g.shape[1], g.shape[2]), lambda i, j: (j, 0, 0))


def _grw(g, tn):
    return (g, (N_DEV, tn, g.shape[2]), lambda i, j: (0, j, 0))


def _out(m, n, dtype, tm, tn):
    return ((m, n), dtype, (tm, tn), lambda i, j: (i, j))


def _out_dev(k, n, tm):
    return ((N_DEV, k, n), F32, (None, tm, n), lambda i, j: (j, i, 0))


def _mmc(dims, n_pairs=1, epilogue=None):
    def compute(*refs):
        acc = None
        for k in range(n_pairs):
            d = _dot(refs[2 * k][...].astype(BF16), refs[2 * k + 1][...].astype(BF16), dims)
            acc = d if acc is None else acc + d
        ext = [r[...] for r in refs[2 * n_pairs:]]
        return epilogue(acc, *ext) if epilogue is not None else (acc,)

    return compute


def _res(w):
    return (w, w.shape, functools.partial(lambda i, j, nd: (0,) * nd, nd=w.ndim))


def _mmc_blocks(nblk, dims, rhs_block, epilogue=None):
    def compute(in_refs, out_refs):
        a = in_refs[0][...].astype(BF16)
        for d in range(nblk):
            acc = _dot(a, rhs_block(in_refs[1], d).astype(BF16), dims)
            n = acc.shape[1]
            ext = [r[:, d * n:(d + 1) * n] for r in in_refs[2:]]
            res = epilogue(acc, *ext) if epilogue is not None else (acc,)
            for o_ref, r in zip(out_refs, res):
                o_ref[:, d * n:(d + 1) * n] = r.astype(o_ref.dtype)

    return compute


def _mmc_dev(epilogue=None):
    def compute(a_ref, b_ref, *ext_refs):
        n = b_ref.shape[2]
        acc = None
        for d in range(N_DEV):
            t = _dot(a_ref[:, d * n:(d + 1) * n].astype(BF16), b_ref[d].astype(BF16), NT)
            acc = t if acc is None else acc + t
        ext = [r[...] for r in ext_refs]
        return epilogue(acc, *ext) if epilogue is not None else (acc,)

    return compute


def _rowwise(name, body, rows, consts, out_rows, out_accs=(), tr=512, deps=()):
    T = rows[0][0].shape[0]
    tr = min(tr, T)
    deps = _deps(deps)
    nr, ncn, no, nd = len(rows), len(consts), len(out_rows), len(deps)

    def kern(*refs):
        accs = refs[nr + ncn + nd + no:]
        if accs:
            @pl.when(pl.program_id(0) == 0)
            def _():
                for a in accs:
                    a[...] = jnp.zeros(a.shape, a.dtype)
        body(refs[:nr], refs[nr:nr + ncn], refs[nr + ncn + nd:nr + ncn + nd + no], accs)

    in_specs = [pl.BlockSpec((tr, w), functools.partial(lambda i, cb: (i, cb), cb=cb)) for _, w, cb in rows]
    in_specs += [pl.BlockSpec(c.shape, functools.partial(lambda i, nd: (0,) * nd, nd=c.ndim), pipeline_mode=pl.Buffered(1))
                 for c in consts]
    in_specs += [ANY_SPEC] * nd
    out_specs = [pl.BlockSpec((tr, w), lambda i: (i, 0)) for w, _ in out_rows]
    out_specs += [pl.BlockSpec(s, functools.partial(lambda i, nd: (0,) * nd, nd=len(s))) for s, _ in out_accs]
    out_shape = [pltpu.HBM((T, w), dt) for w, dt in out_rows]
    out_shape += [pltpu.HBM(s, dt) for s, dt in out_accs]
    nbytes = sum(_nbytes((tr, w), a.dtype) for a, w, _ in rows) + sum(_nbytes(c.shape, c.dtype) for c in consts)
    nbytes += sum(_nbytes((tr, w), dt) for w, dt in out_rows) + sum(_nbytes(s, dt) for s, dt in out_accs)
    res = pl.pallas_call(
        kern, name=name, grid=(T // tr,), in_specs=in_specs, out_specs=out_specs, out_shape=out_shape,
        compiler_params=pltpu.CompilerParams(dimension_semantics=("arbitrary",), vmem_limit_bytes=_vmem(nbytes)),
    )(*[_hbm(a) for a, _, _ in rows], *[_hbm(c) for c in consts], *deps)
    return res if len(res) > 1 else res[0]


def _full(a):
    return (a, a.shape[1], 0)


def _ln_stats(y):
    mu = jnp.mean(y, axis=-1, keepdims=True)
    yc = y - mu
    r = lax.rsqrt(jnp.mean(yc * yc, axis=-1, keepdims=True) + EPS)
    return yc * r, r


def _row_halves(n):
    return [slice(0, n // 2), slice(n // 2, n)] if n >= 256 else [slice(0, n)]


def _ln_back(dh, xh, r, gain, dg_ref, db_ref):
    dg_ref[...] += jnp.sum(dh * xh, axis=0, keepdims=True)
    db_ref[...] += jnp.sum(dh, axis=0, keepdims=True)
    dx = dh * gain
    return r * (dx - jnp.mean(dx, axis=-1, keepdims=True) - xh * jnp.mean(dx * xh, axis=-1, keepdims=True))


def _proj_ln(name, acts, weights, h_in, g, b, layer, deps=()):
    n = len(acts)

    def body(rows, consts, outs, accs):
        acc = None
        for k in range(n):
            d = _dot(rows[k][...].astype(BF16), consts[k][...], NN)
            acc = d if acc is None else acc + d
        y = ALPHA * rows[n][...] + acc
        xh, _ = _ln_stats(y)
        h = xh * consts[n][layer:layer + 1, :] + consts[n + 1][layer:layer + 1, :]
        outs[0][...] = y
        outs[1][...] = h
        outs[2][...] = h.astype(BF16)

    return _rowwise(name, body, [_full(a) for a in acts] + [_full(h_in)], [*weights, g, b],
                    [(D_MODEL, F32), (D_MODEL, F32), (D_MODEL, BF16)], tr=TM, deps=deps)


def _proj_ln_loss(name, act, w2, h_in, g, b, layer, target):
    def body(rows, consts, outs, accs):
        y = ALPHA * rows[1][...] + _dot(rows[0][...], consts[0][...], NN)
        xh, r = _ln_stats(y)
        gain = consts[1][layer:layer + 1, :]
        err = xh * gain + consts[2][layer:layer + 1, :] - rows[2][...]
        accs[0][...] += jnp.sum(err * err, axis=0, keepdims=True)
        dy = _ln_back(err * (1.0 / D_MODEL), xh, r, gain, accs[1], accs[2])
        outs[0][...] = dy
        outs[1][...] = dy.astype(BF16)

    return _rowwise(name, body, [_full(act), _full(h_in), _full(target)], [w2, g, b], [(D_MODEL, F32), (D_MODEL, BF16)],
                    [((1, D_MODEL), F32)] * 3, tr=TM)


def _dh_ln_back(name, da, w, dy_next, y, g, layer, proj=(), deps=()):
    def body(rows, consts, outs, accs):
        n = consts[0].shape[2]
        for sl in _row_halves(rows[0].shape[0]):
            acc = ALPHA * rows[1][sl, :]
            for d in range(N_DEV):
                acc = acc + _dot(rows[0][sl, d * n:(d + 1) * n], consts[0][d], NT)
            xh, r = _ln_stats(rows[2][sl, :])
            dy = _ln_back(acc, xh, r, consts[1][layer:layer + 1, :], accs[0], accs[1])
            outs[0][sl, :] = dy
            dy_bf = dy.astype(BF16)
            outs[1][sl, :] = dy_bf
            off = 0
            for k, p in enumerate(proj):
                outs[2][sl, off:off + p.shape[0]] = _dot(dy_bf, consts[2 + k][...], NT).astype(BF16)
                off += p.shape[0]

    out_rows = [(D_MODEL, F32), (D_MODEL, BF16)] + ([(sum(p.shape[0] for p in proj), BF16)] if proj else [])
    return _rowwise(name, body, [_full(da), _full(dy_next), _full(y)], [w, g, *proj], out_rows,
                    [((1, D_MODEL), F32)] * 2, tr=TM, deps=deps)


def _relu2_epilogue(acc):
    a = jnp.maximum(acc, 0.0)
    return acc, a * a


def _mlp_up(tag, h_bf, w1):
    T = h_bf.shape[0]
    tm = min(2 * TM, T)
    return _tiled(f"{tag}_ff1", (1, T // tm), [_rb(h_bf, tm), _res(w1)],
                  [_out(T, D_FF, BF16, tm, D_FF), _out(T, D_FF, BF16, tm, D_FF)],
                  _mmc_blocks(N_DEV, NN, lambda w, d: w[d], epilogue=_relu2_epilogue), direct=True)


def _mlp_bwd_w(tag, h_bf, a, act, dff_bf, w2, deps=()):
    T = h_bf.shape[0]
    tm = min(2 * TM, T)
    da = _tiled(f"{tag}_dact", (1, T // tm), [_rb(dff_bf, tm), _res(w2), _rb(a, tm)], [_out(T, D_FF, BF16, tm, D_FF)],
                _mmc_blocks(N_DEV, NT, lambda w, d: w[d], epilogue=lambda acc, a_t: (acc * 2.0 * jnp.maximum(a_t.astype(F32), 0.0),)),
                direct=True, deps=deps)
    dw2 = _tiled(f"{tag}_dw2", (1, D_FF // TM), [_tl(act, TM), _res(dff_bf)],
                 [_out(D_FF, D_MODEL, F32, TM, D_MODEL)], _mmc(TN_)).reshape(N_DEV, D_FF // N_DEV, D_MODEL)
    dw1 = _tiled(f"{tag}_dw1", (N_DEV, 1), [_res(h_bf), _cw(da, TN)], [_out_dev(D_MODEL, TN, D_MODEL)], _mmc(TN_))
    return da, dw1, dw2


def _rope_tables(positions_col, invf_lane):
    def body(rows, consts, outs, accs):
        ang = rows[0][...].astype(F32) * consts[0][...]
        c, s = jnp.cos(ang), jnp.sin(ang)
        lane = lax.broadcasted_iota(jnp.int32, ang.shape, 1)
        outs[0][...] = jnp.where(lane < 64, 1.0, jnp.where(lane < 96, c, 0.0))
        outs[1][...] = jnp.where((lane >= 64) & (lane < 80), -s, 0.0)
        outs[2][...] = jnp.where((lane >= 80) & (lane < 96), s, 0.0)

    return _rowwise("rope_tables", body, [_full(positions_col)], [invf_lane], [(HEAD_W, F32)] * 3)


def _rope(x, c, s1, s2):
    return x * c + pltpu.roll(x, 112, 1) * s1 + pltpu.roll(x, 16, 1) * s2


def _rope_t(dx, c, s1, s2):
    return dx * c + pltpu.roll(dx * s1, 16, 1) + pltpu.roll(dx * s2, 112, 1)


def _rms(c):
    r = lax.rsqrt(jnp.mean(c * c, axis=-1, keepdims=True) + EPS)
    return c * r, r


def _rope_heads(x, c, s1, s2, fn):
    return jnp.concatenate([fn(x[:, h * HEAD_W:(h + 1) * HEAD_W], c, s1, s2) for h in range(HEADS)], axis=1)


def _mla_in(x, wm, ws, tabs, gq, gkv, deps=()):
    def body(rows, consts, outs, accs):
        xb = rows[0][...].astype(BF16)
        zm = _dot(xb, consts[0][...], NN)
        outs[0][...] = zm
        outs[1][...] = _dot(xb, consts[1][...], NN)
        outs[2][...] = (_rms(zm[:, 0:256])[0] * consts[2][...]).astype(BF16)
        outs[3][...] = (_rms(zm[:, 256:512])[0] * consts[3][...]).astype(BF16)
        outs[4][...] = _rope(zm[:, 512:640], rows[1][...], rows[2][...], rows[3][...])

    return _rowwise("l0_in", body, [_full(x)] + [_full(t) for t in tabs], [wm, ws, gq, gkv],
                    [(640, F32), (1024, F32), (256, BF16), (256, BF16), (HEAD_W, F32)], deps=deps)


def _mla_qkv(cqn, ckvn, kr_rot, tabs, wq, wk, wv):
    def body(rows, consts, outs, accs):
        c, s1, s2 = rows[3][...], rows[4][...], rows[5][...]
        outs[0][...] = _rope_heads(_dot(rows[0][...], consts[0][...], NN), c, s1, s2, _rope).astype(BF16)
        outs[1][...] = (_dot(rows[1][...], consts[1][...], NN) + jnp.concatenate([rows[2][...]] * HEADS, axis=1)).astype(BF16)
        outs[2][...] = _dot(rows[1][...], consts[2][...], NN).astype(BF16)

    rows = [_full(cqn), _full(ckvn), _full(kr_rot)] + [_full(t) for t in tabs]
    return _rowwise("l0_qkv", body, rows, [wq, wk, wv], [(HEADS * HEAD_W, BF16)] * 3)


def _mla_back(zm, cqn, ckvn, tabs, gq, gkv, wq, wk, wv, dq, dk, dv):
    def body(rows, consts, outs, accs):
        c, s1, s2 = rows[4][...], rows[5][...], rows[6][...]
        dk_t, dv_bf = rows[8][...], rows[9][...].astype(BF16)
        dq_bf = _rope_heads(rows[7][...], c, s1, s2, _rope_t).astype(BF16)
        dk_bf = dk_t.astype(BF16)
        accs[0][...] += _dot(rows[2][...], dq_bf, TN_)
        accs[1][...] += _dot(rows[3][...], dk_bf, TN_)
        accs[2][...] += _dot(rows[3][...], dv_bf, TN_)
        dlat = [_dot(dq_bf, consts[2][...], NT), _dot(dk_bf, consts[3][...], NT) + _dot(dv_bf, consts[4][...], NT)]
        for k in range(2):
            ch, r = _rms(rows[k][...])
            accs[3 + k][...] += jnp.sum(dlat[k] * ch, axis=0, keepdims=True)
            dc = dlat[k] * consts[k][...]
            outs[0][:, 256 * k:256 * (k + 1)] = (r * (dc - ch * jnp.mean(dc * ch, axis=-1, keepdims=True))).astype(BF16)
        dks = dk_t[:, 0:HEAD_W]
        for h in range(1, HEADS):
            dks = dks + dk_t[:, h * HEAD_W:(h + 1) * HEAD_W]
        lane = lax.broadcasted_iota(jnp.int32, dks.shape, 1)
        dks = jnp.where((lane >= 64) & (lane < 96), dks, 0.0)
        outs[0][:, 512:640] = _rope_t(dks, c, s1, s2).astype(BF16)

    rows = [(zm, 256, 0), (zm, 256, 1), _full(cqn), _full(ckvn)] + [_full(t) for t in tabs] + [_full(dq), _full(dk), _full(dv)]
    wide = HEADS * HEAD_W
    return _rowwise("l0_mla_back", body, rows, [gq, gkv, wq, wk, wv], [(640, BF16)],
                    [((MLA_LORA, wide), F32)] * 3 + [((1, MLA_LORA), F32)] * 2, tr=256)


def _in_back(x, dzm, dzs, dy, wm, ws, deps=()):
    def body(rows, consts, outs, accs):
        dzm_t, dzs_t = rows[1][...], rows[2][...]
        outs[0][...] = _dot(dzm_t, consts[0][...], NT) + _dot(dzs_t, consts[1][...], NT) + ALPHA * rows[3][...]
        xb = rows[0][...].astype(BF16)
        accs[0][...] += _dot(xb, dzm_t, TN_)
        accs[1][...] += _dot(xb, dzs_t, TN_)

    return _rowwise("l0_in_back", body, [_full(x), _full(dzm), _full(dzs), _full(dy)], [wm, ws], [(D_MODEL, F32)],
                    [((D_MODEL, 640), F32), ((D_MODEL, 1024), F32)], deps=deps)


def _out_weight_grads(o_att, b_out, dy_bf):
    def body(rows, consts, outs, accs):
        d = rows[2][...]
        accs[0][...] += _dot(rows[0][...].astype(BF16), d, TN_)
        accs[1][...] += _dot(rows[1][...], d, TN_)

    return _rowwise("l0_dw_out", body, [_full(o_att), _full(b_out), _full(dy_bf)], [], [],
                    [((HEADS * HEAD_W, D_MODEL), F32), ((SGU_DIM, D_MODEL), F32)])


def _attn_block(T):
    return min(1024, T)


def _attn_fwd(q, k, v):
    T = q.shape[0]
    BQ = _attn_block(T)
    nq = T // BQ

    def kern(q_ref, k_ref, v_ref, o_ref, lse_ref):
        def step(i, j, carry, masked):
            m, l, acc = carry
            qb = q_ref[pl.ds(pl.multiple_of(i * BQ, BQ), BQ), :]
            kb = k_ref[pl.ds(pl.multiple_of(j * BQ, BQ), BQ), :]
            vb = v_ref[pl.ds(pl.multiple_of(j * BQ, BQ), BQ), :]
            s = _dot(qb, kb, NT) * MLA_SCALE
            if masked:
                row = lax.broadcasted_iota(jnp.int32, s.shape, 0)
                col = lax.broadcasted_iota(jnp.int32, s.shape, 1)
                s = jnp.where(col <= row, s, -1e30)
            m_new = jnp.maximum(m, jnp.max(s, axis=-1, keepdims=True))
            p = jnp.exp(s - m_new)
            a = jnp.exp(m - m_new)
            l = a * l + jnp.sum(p, axis=-1, keepdims=True)
            acc = a * acc + _dot(p.astype(BF16), vb, NN)
            return m_new, l, acc

        def qloop(i, _):
            init = (jnp.full((BQ, 1), -1e30, F32), jnp.zeros((BQ, 1), F32), jnp.zeros((BQ, HEAD_W), F32))
            carry = lax.fori_loop(0, i, lambda j, c: step(i, j, c, False), init)
            m, l, acc = step(i, i, carry, True)
            rows = pl.ds(pl.multiple_of(i * BQ, BQ), BQ)
            o_ref[rows, :] = acc / l
            lse_ref[0, rows, :] = m + jnp.log(l)
            return 0

        lax.fori_loop(0, nq, qloop, 0)

    head = pl.BlockSpec((T, HEAD_W), lambda h: (0, h))
    nbytes = 3 * _nbytes((T, HEAD_W), BF16) + _nbytes((T, HEAD_W), F32) + _nbytes((T, 128), F32)
    return pl.pallas_call(
        kern, name="attn_fwd", grid=(HEADS,), in_specs=[head, head, head],
        out_specs=[head, pl.BlockSpec((1, T, 1), lambda h: (h, 0, 0))],
        out_shape=[pltpu.HBM((T, HEADS * HEAD_W), F32), pltpu.HBM((HEADS, T, 1), F32)],
        compiler_params=pltpu.CompilerParams(dimension_semantics=("parallel",), vmem_limit_bytes=_vmem(nbytes)),
    )(_hbm(q), _hbm(k), _hbm(v))


def _attn_bwd(q, k, v, o, lse, dcat, deps=()):
    T = q.shape[0]
    BQ = _attn_block(T)
    nq = T // BQ
    deps = _deps(deps)

    def kern(q_ref, k_ref, v_ref, o_ref, lse_ref, do_ref, *rest):
        dq_ref, dk_ref, dv_ref, dd_ref = rest[len(deps):]
        dq_ref[...] = jnp.zeros(dq_ref.shape, F32)

        def dloop(i, _):
            rows = pl.ds(pl.multiple_of(i * BQ, BQ), BQ)
            dd_ref[rows, :] = jnp.sum(do_ref[rows, :].astype(F32) * o_ref[rows, :], axis=-1, keepdims=True)
            return 0

        lax.fori_loop(0, nq, dloop, 0)

        def step(j, i, carry, masked):
            dk_acc, dv_acc = carry
            rq = pl.ds(pl.multiple_of(i * BQ, BQ), BQ)
            rk = pl.ds(pl.multiple_of(j * BQ, BQ), BQ)
            qb, kb, vb, dob = q_ref[rq, :], k_ref[rk, :], v_ref[rk, :], do_ref[rq, :]
            s = _dot(qb, kb, NT) * MLA_SCALE
            p = jnp.exp(s - lse_ref[0, rq, :])
            if masked:
                row = lax.broadcasted_iota(jnp.int32, s.shape, 0)
                col = lax.broadcasted_iota(jnp.int32, s.shape, 1)
                p = jnp.where(col <= row, p, 0.0)
            dp = _dot(dob, vb, NT)
            ds = (p * (dp - dd_ref[rq, :]) * MLA_SCALE).astype(BF16)
            dv_acc = dv_acc + _dot(p.astype(BF16), dob, TN_)
            dk_acc = dk_acc + _dot(ds, qb, TN_)
            dq_ref[rq, :] += _dot(ds, kb, NN)
            return dk_acc, dv_acc

        def kloop(j, _):
            init = (jnp.zeros((BQ, HEAD_W), F32), jnp.zeros((BQ, HEAD_W), F32))
            carry = step(j, j, init, True)
            dk_acc, dv_acc = lax.fori_loop(j + 1, nq, lambda i, c: step(j, i, c, False), carry)
            rk = pl.ds(pl.multiple_of(j * BQ, BQ), BQ)
            dk_ref[rk, :] = dk_acc
            dv_ref[rk, :] = dv_acc
            return 0

        lax.fori_loop(0, nq, kloop, 0)

    head = pl.BlockSpec((T, HEAD_W), lambda h: (0, h))
    nbytes = 4 * _nbytes((T, HEAD_W), BF16) + 5 * _nbytes((T, HEAD_W), F32) + 2 * _nbytes((T, 128), F32)
    return pl.pallas_call(
        kern, name="attn_bwd", grid=(HEADS,),
        in_specs=[head, head, head, head, pl.BlockSpec((1, T, 1), lambda h: (h, 0, 0)), head] + [ANY_SPEC] * len(deps),
        out_specs=[head, head, head],
        out_shape=[pltpu.HBM((T, HEADS * HEAD_W), F32)] * 3,
        scratch_shapes=[pltpu.VMEM((T, 1), F32)],
        compiler_params=pltpu.CompilerParams(dimension_semantics=("parallel",), vmem_limit_bytes=_vmem(nbytes)),
    )(*[_hbm(a) for a in (q, k, v, o, lse, dcat)], *deps)


def _sgu_common(u, v, ln_g, ln_b):
    ua, tu = _gelu(u)
    va, tv = _gelu(v)
    vh, r = _ln_stats(va)
    return ua, tu, tv, vh, r, vh * ln_g + ln_b


def _tril_mask(n):
    return lax.broadcasted_iota(jnp.int32, (n, n), 1) <= lax.broadcasted_iota(jnp.int32, (n, n), 0)


def _sgu_fwd(zs, ln_g, ln_b, w, bias_full):
    def body(rows, consts, outs, accs):
        ua, _, _, _, _, vn = _sgu_common(rows[0][...], rows[1][...], consts[0][...], consts[1][...])
        vn = vn.astype(BF16)
        tri = _tril_mask(SGU_CHUNK)
        for g in range(SGU_G):
            wg = jnp.where(tri, consts[2][0, g], 0.0).astype(BF16)
            cols = slice(g * 128, (g + 1) * 128)
            for c in range(ua.shape[0] // SGU_CHUNK):
                rws = slice(c * SGU_CHUNK, (c + 1) * SGU_CHUNK)
                mixed = _dot(wg, vn[rws, cols], NN) + consts[3][:, cols]
                outs[0][rws, cols] = (ua[rws, cols] * mixed).astype(BF16)

    return _rowwise("sgu_fwd", body, [(zs, 512, 0), (zs, 512, 1)], [ln_g, ln_b, w, bias_full], [(SGU_DIM, BF16)])


def _sgu_bwd(zs, dcat, ln_g, ln_b, w, bias_full):
    def body(rows, consts, outs, accs):
        u, v = rows[0][...], rows[1][...]
        ua, tu, tv, vh, r, vn = _sgu_common(u, v, consts[0][...], consts[1][...])
        dout = rows[2][...].astype(F32)
        vn_bf = vn.astype(BF16)
        tri = _tril_mask(SGU_CHUNK)
        dmixed = (dout * ua)
        dmixed_bf = dmixed.astype(BF16)
        ones = jnp.ones((8, SGU_CHUNK), F32)
        dvn_cols, mixed_cols = [], []
        for g in range(SGU_G):
            wg = jnp.where(tri, consts[2][0, g], 0.0).astype(BF16)
            cols = slice(g * 128, (g + 1) * 128)
            dvn_rows, mixed_rows = [], []
            dw = jnp.zeros((SGU_CHUNK, SGU_CHUNK), F32)
            dmix_sum = jnp.zeros((SGU_CHUNK, 128), F32)
            for c in range(u.shape[0] // SGU_CHUNK):
                rws = slice(c * SGU_CHUNK, (c + 1) * SGU_CHUNK)
                mixed_rows.append(_dot(wg, vn_bf[rws, cols], NN) + consts[3][:, cols])
                dvn_rows.append(_dot(wg, dmixed_bf[rws, cols], TN_))
                dw = dw + _dot(dmixed_bf[rws, cols], vn_bf[rws, cols], NT)
                dmix_sum = dmix_sum + dmixed[rws, cols]
            accs[0][g] += jnp.where(tri, dw, 0.0)
            accs[3][g:g + 1, :] += _dot(ones, dmix_sum, NT, precision=HIGHEST)[0:1, :]
            dvn_cols.append(jnp.concatenate(dvn_rows, axis=0))
            mixed_cols.append(jnp.concatenate(mixed_rows, axis=0))
        dvn = jnp.concatenate(dvn_cols, axis=1)
        mixed = jnp.concatenate(mixed_cols, axis=1)
        accs[1][...] += jnp.sum(dvn * vh, axis=0, keepdims=True)
        accs[2][...] += jnp.sum(dvn, axis=0, keepdims=True)
        dvh = dvn * consts[0][...]
        dva = r * (dvh - jnp.mean(dvh, axis=-1, keepdims=True) - vh * jnp.mean(dvh * vh, axis=-1, keepdims=True))
        outs[0][:, 0:512] = (dout * mixed * _gelu_grad(u, tu)).astype(BF16)
        outs[0][:, 512:1024] = (dva * _gelu_grad(v, tv)).astype(BF16)

    return _rowwise("sgu_bwd", body, [(zs, 512, 0), (zs, 512, 1), (dcat, 512, 2)], [ln_g, ln_b, w, bias_full], [(1024, BF16)],
                    [((SGU_G, 128, 128), F32), ((1, SGU_DIM), F32), ((1, SGU_DIM), F32), ((SGU_G, 128), F32)], tr=256)


def _lower_bound(hg_lb):
    a0, a1 = hg_lb[0:1, :], hg_lb[1:2, :]
    m = jnp.maximum(a0, a1)
    e0, e1 = jnp.exp(a0 - m), jnp.exp(a1 - m)
    s0, s1 = e0 / (e0 + e1), e1 / (e0 + e1)
    return (s0 + s1) - s0, s0, s1


def _prefix_rows(x, reverse=False):
    n = x.shape[0]
    row = lax.broadcasted_iota(jnp.int32, x.shape, 0)
    s = 1
    while s < n:
        if reverse:
            x = x + jnp.where(row < n - s, pltpu.roll(x, n - s, 0), 0.0)
        else:
            x = x + jnp.where(row >= s, pltpu.roll(x, s, 0), 0.0)
        s *= 2
    return x


def _hg_gates(qr, fr, lb):
    C = qr.shape[0]
    sq = _sig(qr)
    qf = qr * sq
    sf = _sig(fr)
    gate = lb + (1.0 - lb) * sf
    kk = 1.0 - gate
    tri = _tril_mask(C)
    b = _prefix_rows(jnp.log(gate))
    bref = b[C // 2 - 1:C // 2, :]
    bl = b[C - 1:C, :]
    e_b = jnp.exp(b)
    e_q = jnp.exp(b - bref)
    e_k = jnp.exp(bref - b)
    e_lb = jnp.exp(bl - b)
    return dict(sq=sq, qf=qf, sf=sf, gate=gate, kk=kk, tri=tri, bl=bl, e_b=e_b, e_q=e_q, e_k=e_k, e_lb=e_lb)


def _hgrn_fwd(z1, hg_lb, gnorm):
    T = z1.shape[0]
    C = min(HG_CHUNK, T)
    nc = T // C
    ns = HG_CHUNKS_PER_STEP if nc % HG_CHUNKS_PER_STEP == 0 else 1
    R = ns * C

    def kern(q_ref, f_ref, i_ref, g_ref, lb_ref, gn_ref, o_ref, hg_ref, st_ref, s_scr):
        @pl.when(pl.program_id(0) == 0)
        def _():
            s_scr[...] = jnp.zeros(s_scr.shape, F32)

        lb_all, _, _ = _lower_bound(lb_ref[...])
        for sub in range(ns):
            rows = slice(sub * C, (sub + 1) * C)
            st_ref[sub] = s_scr[...]
            for h in range(HEADS):
                cols = slice(h * HEAD_W, (h + 1) * HEAD_W)
                t = _hg_gates(q_ref[rows, cols], f_ref[rows, cols], lb_all[:, cols])
                v_bf = i_ref[rows, cols].astype(BF16)
                st = s_scr[h]
                a = jnp.where(t["tri"], _dot((t["qf"] * t["e_q"]).astype(BF16), (t["kk"] * t["e_k"]).astype(BF16), NT), 0.0)
                o = _dot(a.astype(BF16), v_bf, NN) + _dot((t["qf"] * t["e_b"]).astype(BF16), st.astype(BF16), NT)
                s_scr[h] = st * jnp.exp(t["bl"]) + _dot(v_bf, (t["kk"] * t["e_lb"]).astype(BF16), TN_)
                o_ref[rows, cols] = o
                gr = g_ref[rows, cols]
                r = lax.rsqrt(jnp.mean(o * o, axis=-1, keepdims=True) + EPS)
                hg_ref[rows, cols] = (o * r * gn_ref[:, cols] * (gr * _sig(gr))).astype(BF16)

    seg = lambda k: pl.BlockSpec((R, D_MODEL), functools.partial(lambda n, k: (n, k), k=k))
    row = pl.BlockSpec((R, D_MODEL), lambda n: (n, 0))
    nbytes = 6 * _nbytes((R, D_MODEL), F32) + (2 + ns) * _nbytes((HEADS, 128, 128), F32)
    return pl.pallas_call(
        kern, name="hgrn_fwd", grid=(nc // ns,),
        in_specs=[seg(0), seg(1), seg(2), seg(3), pl.BlockSpec((2, D_MODEL), lambda n: (0, 0)),
                  pl.BlockSpec((1, D_MODEL), lambda n: (0, 0))],
        out_specs=[row, row, pl.BlockSpec((ns, HEADS, 128, 128), lambda n: (n, 0, 0, 0))],
        out_shape=[pltpu.HBM((T, D_MODEL), F32), pltpu.HBM((T, D_MODEL), BF16),
                   pltpu.HBM((nc, HEADS, 128, 128), F32)],
        scratch_shapes=[pltpu.VMEM((HEADS, 128, 128), F32)],
        compiler_params=pltpu.CompilerParams(dimension_semantics=("arbitrary",), vmem_limit_bytes=_vmem(nbytes)),
    )(*[_hbm(a) for a in (z1, z1, z1, z1, hg_lb, gnorm)])


def _hgrn_bwd(z1, o_pre, dhg, states, hg_lb, gnorm):
    T = z1.shape[0]
    C = min(HG_CHUNK, T)
    nc = T // C
    ns = HG_CHUNKS_PER_STEP if nc % HG_CHUNKS_PER_STEP == 0 else 1
    R, steps = ns * C, nc // ns

    def kern(q_ref, f_ref, i_ref, g_ref, o_ref, dhg_ref, st_ref, lb_ref, gn_ref, dz_ref, dlb_ref, dgn_ref, ds_scr, dlb_scr):
        n = pl.program_id(0)

        @pl.when(n == 0)
        def _():
            ds_scr[...] = jnp.zeros(ds_scr.shape, F32)
            dlb_scr[...] = jnp.zeros(dlb_scr.shape, F32)
            dgn_ref[...] = jnp.zeros(dgn_ref.shape, F32)

        lb_all, s0, s1 = _lower_bound(lb_ref[...])
        for sub in reversed(range(ns)):
            rows = slice(sub * C, (sub + 1) * C)
            for h in range(HEADS):
                cols = slice(h * HEAD_W, (h + 1) * HEAD_W)
                lb = lb_all[:, cols]
                qr, fr = q_ref[rows, cols], f_ref[rows, cols]
                t = _hg_gates(qr, fr, lb)
                tri = t["tri"]
                v_bf = i_ref[rows, cols].astype(BF16)
                st_bf = st_ref[sub, h].astype(BF16)
                dst = ds_scr[h]
                dst_bf = dst.astype(BF16)
                o = o_ref[rows, cols]
                gr = g_ref[rows, cols]
                sg = _sig(gr)
                sil = gr * sg
                gn = gn_ref[:, cols]
                r = lax.rsqrt(jnp.mean(o * o, axis=-1, keepdims=True) + EPS)
                on = o * r
                dh = dhg_ref[rows, cols].astype(F32)
                dgn_ref[:, cols] += jnp.sum(dh * on * sil, axis=0, keepdims=True)
                dg = dh * on * gn * (sg * (1.0 + gr * (1.0 - sg)))
                don = dh * gn * sil
                do_bf = (r * (don - on * jnp.mean(don * on, axis=-1, keepdims=True))).astype(BF16)
                qe = (t["qf"] * t["e_q"]).astype(BF16)
                ke = (t["kk"] * t["e_k"]).astype(BF16)
                qb = (t["qf"] * t["e_b"]).astype(BF16)
                kh_bf = (t["kk"] * t["e_lb"]).astype(BF16)
                a_bf = jnp.where(tri, _dot(qe, ke, NT), 0.0).astype(BF16)
                da_bf = jnp.where(tri, _dot(do_bf, v_bf, NT), 0.0).astype(BF16)
                dv = _dot(a_bf, do_bf, TN_) + _dot(kh_bf, dst_bf, NT)
                dqe = _dot(da_bf, ke, NN)
                dqb = _dot(do_bf, st_bf, NN)
                dke = _dot(da_bf, qe, TN_)
                dkh = _dot(v_bf, dst_bf, NN)
                dqf = dqe * t["e_q"] + dqb * t["e_b"]
                dkk = dke * t["e_k"] + dkh * t["e_lb"]
                kh_r = kh_bf.astype(F32)
                db = qe.astype(F32) * dqe - ke.astype(F32) * dke + qb.astype(F32) * dqb - kh_r * dkh
                e_bl = jnp.exp(t["bl"])
                dbl = jnp.sum(dkh * kh_r, axis=0, keepdims=True) + e_bl * jnp.sum(st_ref[sub, h] * dst, axis=0, keepdims=True)
                dlg = _prefix_rows(db, reverse=True) + dbl
                ds_scr[h] = dst * e_bl + _dot(do_bf, qb, TN_)
                dgate = dlg / t["gate"] - dkk
                sf = t["sf"]
                dlb_scr[:, cols] += jnp.sum(dgate * (1.0 - sf), axis=0, keepdims=True)
                df = dgate * (1.0 - lb) * sf * (1.0 - sf)
                dq = dqf * (t["sq"] * (1.0 + qr * (1.0 - t["sq"])))
                dz_ref[rows, cols] = dq.astype(BF16)
                dz_ref[rows, D_MODEL + h * HEAD_W:D_MODEL + (h + 1) * HEAD_W] = df.astype(BF16)
                dz_ref[rows, 2 * D_MODEL + h * HEAD_W:2 * D_MODEL + (h + 1) * HEAD_W] = dv.astype(BF16)
                dz_ref[rows, 3 * D_MODEL + h * HEAD_W:3 * D_MODEL + (h + 1) * HEAD_W] = dg.astype(BF16)

        @pl.when(n == steps - 1)
        def _():
            d = s0 * s1 * dlb_scr[...]
            dlb_ref[0:1, :] = -d
            dlb_ref[1:2, :] = d

    seg = lambda k: pl.BlockSpec((R, D_MODEL), functools.partial(lambda n, k: (steps - 1 - n, k), k=k))
    nbytes = 6 * _nbytes((R, D_MODEL), F32) + _nbytes((R, 4 * D_MODEL), BF16) + (2 + ns) * _nbytes((HEADS, 128, 128), F32)
    return pl.pallas_call(
        kern, name="hgrn_bwd", grid=(steps,),
        in_specs=[seg(0), seg(1), seg(2), seg(3), seg(0), seg(0),
                  pl.BlockSpec((ns, HEADS, 128, 128), lambda n: (steps - 1 - n, 0, 0, 0)),
                  pl.BlockSpec((2, D_MODEL), lambda n: (0, 0)), pl.BlockSpec((1, D_MODEL), lambda n: (0, 0))],
        out_specs=[pl.BlockSpec((R, 4 * D_MODEL), lambda n: (steps - 1 - n, 0)),
                   pl.BlockSpec((2, D_MODEL), lambda n: (0, 0)), pl.BlockSpec((1, D_MODEL), lambda n: (0, 0))],
        out_shape=[pltpu.HBM((T, 4 * D_MODEL), BF16), pltpu.HBM((2, D_MODEL), F32),
                   pltpu.HBM((1, D_MODEL), F32)],
        scratch_shapes=[pltpu.VMEM((HEADS, 128, 128), F32), pltpu.VMEM((1, D_MODEL), F32)],
        compiler_params=pltpu.CompilerParams(dimension_semantics=("arbitrary",), vmem_limit_bytes=_vmem(nbytes)),
    )(*[_hbm(a) for a in (z1, z1, z1, z1, o_pre, dhg, states, hg_lb, gnorm)])


def _prep_weights(gw):
    w_in_e = gw["w_in_e"].transpose(1, 0, 2).reshape(D_MODEL, 1568)
    kr = jnp.pad(w_in_e[:, 512:544], ((0, 0), (64, 32)))
    wm = jnp.concatenate([w_in_e[:, 0:512], kr], axis=1)
    ws = w_in_e[:, 544:1568]
    w_qb = gw["w_qb"].transpose(1, 0, 2).reshape(MLA_LORA, HEADS, 96)
    wq = jnp.pad(w_qb, ((0, 0), (0, 0), (0, 32))).reshape(MLA_LORA, HEADS * HEAD_W)
    kvb = gw["w_kvb"].transpose(1, 0, 2).reshape(MLA_LORA, HEADS, 128)
    wk = jnp.pad(kvb[:, :, :64], ((0, 0), (0, 0), (0, 64))).reshape(MLA_LORA, HEADS * HEAD_W)
    wv = jnp.pad(kvb[:, :, 64:], ((0, 0), (0, 0), (0, 64))).reshape(MLA_LORA, HEADS * HEAD_W)
    w_out_e = gw["w_out_e"].reshape(D_MODEL, D_MODEL)
    woa = jnp.pad(w_out_e[:512].reshape(HEADS, 64, D_MODEL), ((0, 0), (0, 64), (0, 0))).reshape(HEADS * HEAD_W, D_MODEL)
    return dict(wm=wm, ws=ws, wq=wq, wk=wk, wv=wv, woa=woa, wob=w_out_e[512:])


def _unprep_grads(g):
    dwm, dws = g["wm"], g["ws"]
    d_in_e = jnp.concatenate([dwm[:, 0:512], dwm[:, 512 + 64:512 + 96], dws], axis=1)
    d_qb = g["wq"].reshape(MLA_LORA, HEADS, HEAD_W)[:, :, :96].reshape(MLA_LORA, HEADS * 96)
    dk = g["wk"].reshape(MLA_LORA, HEADS, HEAD_W)[:, :, :64]
    dv = g["wv"].reshape(MLA_LORA, HEADS, HEAD_W)[:, :, :64]
    d_kvb = jnp.concatenate([dk, dv], axis=2).reshape(MLA_LORA, HEADS * 128)
    d_oa = g["woa"].reshape(HEADS, HEAD_W, D_MODEL)[:, :64].reshape(HEADS * 64, D_MODEL)
    dev_major = lambda a: a.reshape(a.shape[0], N_DEV, a.shape[1] // N_DEV).transpose(1, 0, 2)
    return dict(w_in_e=dev_major(d_in_e), w_qb=dev_major(d_qb), w_kvb=dev_major(d_kvb),
                w_out_e=jnp.concatenate([d_oa, g["wob"]], axis=0).reshape(N_DEV, D_MODEL // N_DEV, D_MODEL))


def _local_step(x, positions, target, gw, sp, ex):
    w = _prep_weights(gw)
    T = x.shape[0]
    tm = min(2 * TM, T)
    nt = T // tm
    half = MLA_ROPE // 2
    inv_freq = ROPE_BASE ** (-jnp.arange(half, dtype=F32) / half)
    invf_lane = jnp.concatenate([jnp.zeros((64,), F32), inv_freq, inv_freq, jnp.zeros((32,), F32)]).reshape(1, HEAD_W)
    tabs = _rope_tables(positions.reshape(T, 1), invf_lane)
    bias_full = jnp.repeat(sp["sgu_b"][0].T, 128, axis=1)
    sgu_w = sp["sgu_w"]
    gq, gkv = sp["mla_gq"], sp["mla_gkv"]
    ln1_g, ln1_b, ln2_g, ln2_b = sp["ln1_g"], sp["ln1_b"], sp["ln2_g"], sp["ln2_b"]
    zm, zs, cqn, ckvn, kr_rot = _mla_in(x, w["wm"], w["ws"], tabs, gq, gkv, deps=[ex.first_token])
    q, k, v = _mla_qkv(cqn, ckvn, kr_rot, tabs, w["wq"], w["wk"], w["wv"])
    o_att, lse = _attn_fwd(q, k, v)
    b_out = _sgu_fwd(zs, sp["sgu_ln_g"], sp["sgu_ln_b"], sgu_w, bias_full)
    token = ex.weights_forward(after=[o_att, b_out])
    y1, h1, h1_bf = _proj_ln("l0_out_ln1", [o_att, b_out], [w["woa"], w["wob"]], x, ln1_g, ln1_b, 0, deps=[token])
    big = ex.weights_ready(after=[y1])
    w_ff1, w_in_o, w_out_o = big["w_ff1"], big["w_in_o"], big["w_out_o"].reshape(D_MODEL, D_MODEL)
    w_ff2 = [a.reshape(D_FF, D_MODEL) for a in big["w_ff2"]]
    a0, act0 = _mlp_up("l0", h1_bf, w_ff1[0])
    y2, h2, h2_bf = _proj_ln("l0_ff2_ln2", [act0], [w_ff2[0]], h1, ln2_g, ln2_b, 0)

    z1 = _tiled("l1_in", (1, nt), [_rb(h2_bf, tm), _res(w_in_o)], [_out(T, 4 * D_MODEL, F32, tm, 4 * D_MODEL)],
                _mmc_blocks(N_DEV, NN, lambda w, d: w[d]), direct=True)
    o_pre, hg, states = _hgrn_fwd(z1, sp["hg_lb"], sp["hg_gnorm"])
    y3, h3, h3_bf = _proj_ln("l1_out_ln1", [hg], [w_out_o], h2, ln1_g, ln1_b, 1)
    a1, act1 = _mlp_up("l1", h3_bf, w_ff1[1])

    gs, g0 = {}, {}
    dy4, dy4_bf, sq_err, gs["ln2_g1"], gs["ln2_b1"] = _proj_ln_loss("l1_ff2_loss", act1, w_ff2[1], h3, ln2_g, ln2_b, 1, target)
    gs["sq_err"] = sq_err
    da1, dw1_1, dw2_1 = _mlp_bwd_w("l1", h3_bf, a1, act1, dy4_bf, big["w_ff2"][1])
    dy3, dy3_bf, dhg, gs["ln1_g1"], gs["ln1_b1"] = _dh_ln_back("l1_dh_ln1", da1, w_ff1[1], dy4, y3, ln1_g, 1, proj=[w_out_o])
    d_out_o = _tiled("l1_dwout", (2, D_MODEL // TM), [_tl(hg, TM), _cw(dy3_bf, TN)], [_out(D_MODEL, D_MODEL, F32, TM, TN)],
                     _mmc(TN_)).reshape(N_DEV, D_MODEL // N_DEV, D_MODEL)
    dz1, gs["hg_lb"], gs["hg_gnorm"] = _hgrn_bwd(z1, o_pre, dhg, states, sp["hg_lb"], sp["hg_gnorm"])
    d_in_o = _tiled("l1_dwin", (N_DEV, 1), [_res(h2_bf), _cw(dz1, TN)], [_out_dev(D_MODEL, TN, D_MODEL)], _mmc(TN_))
    token = ex.grads_start("l1", [dw1_1, dw2_1, d_in_o, d_out_o])

    dy2, dy2_bf, gs["ln2_g0"], gs["ln2_b0"] = _dh_ln_back("l1_dh_ln2", dz1, w_in_o, dy3, y2, ln2_g, 0, deps=[token])
    token = ex.grads_middle("l1", after=dy2)
    da0, dw1_0, dw2_0 = _mlp_bwd_w("l0", h1_bf, a0, act0, dy2_bf, big["w_ff2"][0], deps=[token])
    token = ex.grads_start("l0m", [dw1_0, dw2_0])
    dy1, dy1_bf, dcat, gs["ln1_g0"], gs["ln1_b0"] = _dh_ln_back("l0_dh_ln1", da0, w_ff1[0], dy2, y1, ln1_g, 0,
                                                                 proj=[w["woa"], w["wob"]], deps=[token])
    ex.grads_end("l1", after=dy1)
    g0["woa"], g0["wob"] = _out_weight_grads(o_att, b_out, dy1_bf)
    token = ex.grads_middle("l0m", after=g0["wob"])
    dzs, gs["sgu_w"], gs["sgu_ln_g"], gs["sgu_ln_b"], gs["sgu_b"] = _sgu_bwd(zs, dcat, sp["sgu_ln_g"], sp["sgu_ln_b"], sgu_w, bias_full)
    dq, dk, dv = _attn_bwd(q, k, v, o_att, lse, dcat, deps=[token])
    ex.grads_end("l0m", after=dq)
    dzm, g0["wq"], g0["wk"], g0["wv"], gs["mla_gq"], gs["mla_gkv"] = _mla_back(zm, cqn, ckvn, tabs, gq, gkv, w["wq"], w["wk"], w["wv"],
                                                                                 dq, dk, dv)
    token = ex.small_start(gs)
    dx, g0["wm"], g0["ws"] = _in_back(x, dzm, dzs, dy1, w["wm"], w["ws"], deps=[token])

    return sq_err, dx, _unprep_grads(g0), gs


def _me():
    return lax.axis_index("x"), lax.axis_index("y"), lax.axis_index("c")


ANY_SPEC = pl.BlockSpec(memory_space=pl.ANY)
HBM_SPEC = pl.BlockSpec(memory_space=pltpu.HBM)
SEM_SPEC = pl.BlockSpec(memory_space=pltpu.SEMAPHORE)
EFFECT = pltpu.SideEffectType.DATAFLOW_SIDE_EFFECTING


def _split_start(name, srcs, lands, n_sems, make_copies, after=()):
    n, m, k = len(srcs), len(lands), len(after)

    def body(*refs):
        for cp in make_copies(refs[:n], refs[n:n + m], refs[n + m + k], refs[n + m + k + 1]):
            cp.start()
        refs[-1][...] = jnp.zeros(refs[-1].shape, F32)

    out_shape = (pltpu.SemaphoreType.DMA((n_sems,)), pltpu.SemaphoreType.DMA((n_sems,)),
                 *[pltpu.HBM(a.shape, a.dtype) for a in (*srcs, *lands)], jax.ShapeDtypeStruct((8, 128), F32))
    res = pl.pallas_call(
        body, name=name, out_shape=out_shape, in_specs=[HBM_SPEC] * (n + m) + [ANY_SPEC] * k,
        out_specs=(SEM_SPEC, SEM_SPEC, *[HBM_SPEC] * (n + m), pl.BlockSpec(memory_space=pltpu.VMEM)),
        input_output_aliases={i: 2 + i for i in range(n + m)},
        compiler_params=pltpu.CompilerParams(has_side_effects=EFFECT),
    )(*[_hbm(a) for a in (*srcs, *lands)], *after)
    return res[0], res[1], list(res[2:2 + n]), list(res[2 + n:2 + n + m]), res[-1]


def _split_wait(name, send_sems, recv_sems, srcs, lands, after, make_copies):
    n, m = len(srcs), len(lands)

    def body(*refs):
        for cp in make_copies(refs[:n], refs[n:n + m], refs[n + m], refs[n + m + 1]):
            cp.wait_send()
            cp.wait_recv()

    res = pl.pallas_call(
        body, name=name, out_shape=tuple(pltpu.HBM(a.shape, a.dtype) for a in (*srcs, *lands)),
        in_specs=[HBM_SPEC] * (n + m) + [SEM_SPEC, SEM_SPEC] + [ANY_SPEC] * len(after), out_specs=tuple([HBM_SPEC] * (n + m)),
        input_output_aliases={i: i for i in range(n + m)},
        compiler_params=pltpu.CompilerParams(has_side_effects=EFFECT),
    )(*srcs, *lands, send_sems, recv_sems, *after)
    return list(res[:n]), list(res[n:])


def _place_own(shards, dev):
    n = len(shards)

    def kern(dev_ref, *refs):
        for x_ref, o_ref in zip(refs[:n], refs[n:]):
            o_ref[...] = x_ref[...].astype(o_ref.dtype)

    blocks = [(None, *a.shape[1:]) for a, _, _ in shards]
    nbytes = sum(_nbytes(b, a.dtype) + _nbytes(b, dt) for b, (a, _, dt) in zip(blocks, shards))
    return pl.pallas_call(
        kern, name="weights_place_own", out_shape=[pltpu.HBM((N_DEV, *a.shape[1:]), dt) for a, _, dt in shards],
        grid_spec=pltpu.PrefetchScalarGridSpec(
            num_scalar_prefetch=1, grid=(1,),
            in_specs=[pl.BlockSpec(b, functools.partial(lambda i, dev, l: (l, 0, 0), l=l)) for b, (_, l, _) in zip(blocks, shards)],
            out_specs=[pl.BlockSpec(b, lambda i, dev: (dev[0], 0, 0)) for b in blocks]),
        compiler_params=pltpu.CompilerParams(dimension_semantics=("arbitrary",), vmem_limit_bytes=_vmem(nbytes)),
    )(dev, *[_hbm(a) for a, _, _ in shards])


def _ag_first_copies(src_refs, out_refs, send_sems, recv_sems):
    x, y, c = _me()
    targets = [(x, y, 1 - c), (1 - x, y, c), (x, 1 - y, c), (1 - x, 1 - y, c)]
    return [pltpu.make_async_remote_copy(
        src_ref=out_refs[op].at[4 * x + 2 * y + c], dst_ref=out_refs[op].at[4 * x + 2 * y + c], send_sem=send_sems.at[4 * op + k],
        recv_sem=recv_sems.at[4 * op + k], device_id=to, device_id_type=MESH)
        for op in range(len(out_refs)) for k, to in enumerate(targets)]


def _ag_second_copies(src_refs, out_refs, send_sems, recv_sems):
    x, y, c = _me()
    chips = [(1 - x, y), (x, 1 - y), (1 - x, 1 - y)]
    return [pltpu.make_async_remote_copy(
        src_ref=out_refs[op].at[4 * cx + 2 * cy + c], dst_ref=out_refs[op].at[4 * cx + 2 * cy + c],
        send_sem=send_sems.at[3 * op + j], recv_sem=recv_sems.at[3 * op + j], device_id=(x, y, 1 - c), device_id_type=MESH)
        for op in range(len(out_refs)) for j, (cx, cy) in enumerate(chips)]


def _rs_sibling_copies(g_refs, out_refs, send_sems, recv_sems):
    x, y, c = _me()
    return [pltpu.make_async_remote_copy(
        src_ref=g_refs[op].at[k, 1 - c], dst_ref=out_refs[op].at[k], send_sem=send_sems.at[4 * op + k],
        recv_sem=recv_sems.at[4 * op + k], device_id=(x, y, 1 - c), device_id_type=MESH)
        for op in range(len(g_refs)) for k in range(4)]


def _rs_chip_copies(p_refs, out_refs, send_sems, recv_sems):
    x, y, c = _me()
    chips = [(1 - x, y), (x, 1 - y), (1 - x, 1 - y)]
    return [pltpu.make_async_remote_copy(
        src_ref=p_refs[op].at[2 * cx + cy], dst_ref=out_refs[op].at[j], send_sem=send_sems.at[3 * op + j],
        recv_sem=recv_sems.at[3 * op + j], device_id=(cx, cy, c), device_id_type=MESH)
        for op in range(len(p_refs)) for j, (cx, cy) in enumerate(chips)]


def _all_gather(placed):
    n = len(placed)

    def kern(*refs):
        in_refs, out_refs, (send_sems, recv_sems) = refs[:n], refs[n:2 * n], refs[2 * n:]
        x, y, c = _me()
        me, sibling = (x, y, c), (x, y, 1 - c)
        chips = [(1 - x, y), (x, 1 - y), (1 - x, 1 - y)]

        def copy(op, k, block, to, own=False):
            idx = 4 * block[0] + 2 * block[1] + block[2]
            return pltpu.make_async_remote_copy(
                src_ref=(in_refs if own else out_refs)[op].at[idx], dst_ref=out_refs[op].at[idx], send_sem=send_sems.at[7 * op + k],
                recv_sem=recv_sems.at[7 * op + k], device_id=to, device_id_type=MESH)

        first = []
        for op in range(n):
            first.append(copy(op, 0, me, sibling, own=True))
            first += [copy(op, 1 + j, me, (*chip, c), own=True) for j, chip in enumerate(chips)]
        for cp in first:
            cp.start()
        passed = []
        for j, chip in enumerate(chips):
            for op in range(n):
                copy(op, 1 + j, (*chip, c), me).wait_recv()
                passed.append(copy(op, 4 + j, (*chip, c), sibling))
                passed[-1].start()
        for op in range(n):
            copy(op, 0, sibling, me).wait_recv()
            for j, chip in enumerate(chips):
                copy(op, 4 + j, (*chip, 1 - c), me).wait_recv()
        for cp in first + passed:
            cp.wait_send()

    return pl.pallas_call(
        kern, name="weights_all_gather", out_shape=[pltpu.HBM(g.shape, g.dtype) for g in placed],
        in_specs=[ANY_SPEC] * n, out_specs=[ANY_SPEC] * n, input_output_aliases={i: i for i in range(n)},
        scratch_shapes=[pltpu.SemaphoreType.DMA((7 * n,)), pltpu.SemaphoreType.DMA((7 * n,))],
    )(*[_hbm(a) for a in placed])


def _row_tile(r, w, n_blocks):
    tr = r
    while tr > 8 and 2 * n_blocks * tr * w * 4 > 24 * 2**20:
        tr //= 2
    return tr


def _chip_sum(name, g, from_sibling, core):
    _, _, R, W = g.shape
    tr = _row_tile(R, W, 3)

    def kern(core_ref, g_ref, s_ref, o_ref):
        o_ref[...] = (g_ref[...] + s_ref[...]).astype(BF16)

    return pl.pallas_call(
        kern, name=name, out_shape=pltpu.HBM((4, R, W), BF16),
        grid_spec=pltpu.PrefetchScalarGridSpec(
            num_scalar_prefetch=1, grid=(4, R // tr),
            in_specs=[pl.BlockSpec((None, None, tr, W), lambda k, i, core: (k, core[0], i, 0)),
                      pl.BlockSpec((None, tr, W), lambda k, i, core: (k, i, 0))],
            out_specs=pl.BlockSpec((None, tr, W), lambda k, i, core: (k, i, 0))),
        compiler_params=pltpu.CompilerParams(dimension_semantics=("parallel", "parallel"), vmem_limit_bytes=_vmem(3 * tr * W * 4)),
    )(core, _hbm(g), _hbm(from_sibling))


def _adamw(w, g, m, v):
    m = ADAM_B1 * m + (1.0 - ADAM_B1) * g
    v = ADAM_B2 * v + (1.0 - ADAM_B2) * (g * g)
    m_hat = m / (1.0 - ADAM_B1 ** ADAM_STEP)
    v_hat = v / (1.0 - ADAM_B2 ** ADAM_STEP)
    return -ADAM_LR * (m_hat / (jnp.sqrt(v_hat) + ADAM_EPS) + ADAM_WD * w), m, v


def _finish_sharded(name, layers, w, m, v, where, deps=()):
    nl, R, W = w.shape
    tr = _row_tile(R, W, 11 * nl)
    deps = _deps(deps)

    def kern(where_ref, *refs):
        w_ref, m_ref, v_ref = refs[3 * nl:3 * nl + 3]
        go_ref, d_ref, mo_ref, vo_ref = refs[3 * nl + 3 + len(deps):]
        for l in range(nl):
            g_ref, s_ref, c_ref = refs[3 * l:3 * l + 3]
            grad = g_ref[...] + s_ref[...]
            for j in range(3):
                grad = grad + c_ref[j].astype(F32)
            go_ref[l] = grad
            d_ref[l], mo_ref[l], vo_ref[l] = _adamw(w_ref[l], grad, m_ref[l], v_ref[l])

    row = pl.BlockSpec((nl, tr, W), lambda i, wh: (0, i, 0))
    in_specs, args = [], []
    for g, s, c in layers:
        in_specs += [pl.BlockSpec((None, None, tr, W), lambda i, wh: (wh[0], wh[1], i, 0)),
                     pl.BlockSpec((None, tr, W), lambda i, wh: (wh[0], i, 0)),
                     pl.BlockSpec((3, tr, W), lambda i, wh: (0, i, 0))]
        args += [g, s, c]
    return pl.pallas_call(
        kern, name=name, out_shape=[pltpu.HBM((nl, R, W), F32)] * 4,
        grid_spec=pltpu.PrefetchScalarGridSpec(num_scalar_prefetch=1, grid=(R // tr,),
                                               in_specs=in_specs + [row, row, row] + [ANY_SPEC] * len(deps),
                                               out_specs=[row, row, row, row]),
        compiler_params=pltpu.CompilerParams(dimension_semantics=("parallel",), vmem_limit_bytes=_vmem(nl * 11 * tr * W * 4)),
    )(where, *[_hbm(a) for a in (*args, w, m, v)], *deps)


SMALL_PLACE = (("mla_gq", 0, 0, 1, 256), ("mla_gkv", 0, 256, 1, 256), ("sgu_ln_g", 0, 512, 1, 512), ("sgu_ln_b", 1, 0, 1, 512),
               ("hg_lb", 2, 0, 2, 1024), ("ln1_g", 4, 0, 2, 1024), ("ln1_b", 6, 0, 2, 1024), ("sgu_b", 8, 0, 4, 128),
               ("ln2_g", 12, 0, 2, 1024), ("ln2_b", 14, 0, 2, 1024), ("hg_gnorm", 16, 0, 1, 1024))
SMALL_BUF_ROWS = 24
LOSS_ROW = 17


def _small_pack(gs, dev):
    pieces = [(gs["mla_gq"], 0, 0), (gs["mla_gkv"], 0, 256), (gs["sgu_ln_g"], 0, 512), (gs["sgu_ln_b"], 1, 0), (gs["hg_lb"], 2, 0),
              (gs["ln1_g0"], 4, 0), (gs["ln1_g1"], 5, 0), (gs["ln1_b0"], 6, 0), (gs["ln1_b1"], 7, 0), (gs["sgu_b"], 8, 0),
              (gs["ln2_g0"], 12, 0), (gs["ln2_g1"], 13, 0), (gs["ln2_b0"], 14, 0), (gs["ln2_b1"], 15, 0), (gs["hg_gnorm"], 16, 0),
              (gs["sq_err"], LOSS_ROW, 0)]
    n_p = len(pieces)

    def kern(dev_ref, *refs):
        a_ref, b_ref = refs[n_p + 1], refs[n_p + 2]
        a_ref[...] = jnp.zeros(a_ref.shape, F32)
        for ref, (_, r, l0) in zip(refs[:n_p], pieces):
            a_ref[r:r + ref.shape[0], l0:l0 + ref.shape[1]] = ref[...]
        b_ref[...] = refs[n_p][...]

    whole = lambda a: pl.BlockSpec(a.shape, functools.partial(lambda i, dev, nd: (0,) * nd, nd=a.ndim))
    return pl.pallas_call(
        kern, name="small_grads_pack",
        out_shape=[pltpu.HBM((N_DEV, SMALL_BUF_ROWS, D_MODEL), F32), pltpu.HBM((N_DEV, SGU_G, 128, 128), F32)],
        grid_spec=pltpu.PrefetchScalarGridSpec(
            num_scalar_prefetch=1, grid=(1,), in_specs=[whole(p[0]) for p in pieces] + [whole(gs["sgu_w"])],
            out_specs=[pl.BlockSpec((None, SMALL_BUF_ROWS, D_MODEL), lambda i, dev: (dev[0], 0, 0)),
                       pl.BlockSpec((None, SGU_G, 128, 128), lambda i, dev: (dev[0], 0, 0, 0))]),
    )(dev, *[p[0] for p in pieces], gs["sgu_w"])


def _small_copies(src_refs, land_refs, send_sems, recv_sems):
    px, py, pc = _me()
    me = 4 * px + 2 * py + pc
    return [pltpu.make_async_remote_copy(
        src_ref=land_refs[k].at[me], dst_ref=land_refs[k].at[me], send_sem=send_sems.at[2 * (r - 1) + k],
        recv_sem=recv_sems.at[2 * (r - 1) + k], device_id=(px ^ (r >> 2), py ^ ((r >> 1) & 1), pc ^ (r & 1)), device_id_type=MESH)
        for r in range(1, N_DEV) for k in range(2)]


def _small_adamw(slots_a, slots_b, given):
    names = [p[0] for p in SMALL_PLACE] + ["sgu_w"]
    n_names = len(names)
    wmv = [given[pre + name] for name in names for pre in ("", "m_", "v_")]
    vmem = pl.BlockSpec(memory_space=pltpu.VMEM)

    def kern(*refs):
        sum_a, sum_b = refs[0][0], refs[1][0]
        for d in range(1, N_DEV):
            sum_a, sum_b = sum_a + refs[0][d], sum_b + refs[1][d]
        wmv_refs, out_refs = refs[2:2 + 3 * n_names], refs[2 + 3 * n_names:]
        px, py, pc = _me()
        me = 4 * px + 2 * py + pc

        def own_block(full):
            acc = full[:, 0:128]
            for b in range(1, N_DEV):
                acc = jnp.where(me == b, full[:, b * 128:(b + 1) * 128], acc)
            return acc

        for idx, name in enumerate(names):
            w_ref, m_ref, v_ref = wmv_refs[3 * idx:3 * idx + 3]
            if name == "sgu_w":
                grad = sum_b[None]
            else:
                _, r, l0, nr, nl = SMALL_PLACE[idx]
                grad = sum_a[r:r + nr, l0:l0 + nl]
                if name == "hg_gnorm":
                    grad = own_block(grad)
                if name == "sgu_b":
                    grad = grad[None]
            res = (grad, *_adamw(w_ref[...], grad, m_ref[...], v_ref[...]))
            for o_ref, val in zip(out_refs[4 * idx:4 * idx + 4], res):
                o_ref[...] = val
        out_refs[4 * n_names][...] = (0.5 / D_MODEL) * jnp.sum(sum_a[LOSS_ROW:LOSS_ROW + 1, :], axis=1, keepdims=True)

    out_shape = [jax.ShapeDtypeStruct(given[name].shape, F32) for name in names for _ in range(4)]
    out_shape.append(jax.ShapeDtypeStruct((1, 1), F32))
    res = pl.pallas_call(
        kern, name="small_adamw", out_shape=out_shape, in_specs=[vmem] * (2 + len(wmv)), out_specs=[vmem] * len(out_shape),
    )(slots_a, slots_b, *wmv)
    out = {name: res[4 * idx:4 * idx + 4] for idx, name in enumerate(names)}
    out["loss"] = res[-1].reshape(())
    return out


class _Exchange:
    def __init__(self, given):
        self.given = given
        px, py, pc = _me()
        self.core = pc.reshape(1).astype(jnp.int32)
        self.dev = (4 * px + 2 * py + pc).reshape(1).astype(jnp.int32)
        self.where = jnp.stack([2 * px + py, pc]).astype(jnp.int32)
        self.state, self.layers = {}, {}

    def start_weights(self, lands, after):
        self.weights = _split_start("weights_first_start", [], lands, 4 * len(lands), _ag_first_copies, after=after)
        self.first_token = self.weights[4]

    def weights_forward(self, after):
        send_sems, recv_sems, shards, lands, _ = self.weights
        _, lands = _split_wait("weights_first_wait", send_sems, recv_sems, shards, lands, after, _ag_first_copies)
        self.weights = _split_start("weights_second_start", [], lands, 3 * len(lands), _ag_second_copies)
        return self.weights[4]

    def weights_ready(self, after):
        send_sems, recv_sems, shards, lands, _ = self.weights
        _, got = _split_wait("weights_second_wait", send_sems, recv_sems, shards, lands, after, _ag_second_copies)
        return dict(w_in_o=got[0], w_out_o=got[1], w_ff1=[got[2], got[3]], w_ff2=[got[4], got[5]])

    def small_start(self, gs):
        self.small = _split_start("small_grads_start", [], _small_pack(gs, self.dev), 14, _small_copies)
        return self.small[4]

    def small_finish(self, after):
        send_sems, recv_sems, _, lands, _ = self.small
        _, lands = _split_wait("small_grads_wait", send_sems, recv_sems, [], lands, after, _small_copies)
        return _small_adamw(lands[0], lands[1], self.given)

    def grads_start(self, tag, grads):
        blocks = [g.reshape(4, 2, *g.shape[1:]) for g in grads]
        lands = [lax.empty((4, *b.shape[2:]), F32) for b in blocks]
        self.state[tag] = _split_start(f"grads_{tag}_sibling_start", blocks, lands, 4 * len(blocks), _rs_sibling_copies)
        return self.state[tag][4]

    def grads_middle(self, tag, after):
        send_sems, recv_sems, blocks, lands, _ = self.state[tag]
        blocks, from_sibling = _split_wait(f"grads_{tag}_sibling_wait", send_sems, recv_sems, blocks, lands, [after], _rs_sibling_copies)
        sums = [_chip_sum(f"grads_{tag}_chip_sum_{k}", b, s, self.core) for k, (b, s) in enumerate(zip(blocks, from_sibling))]
        lands = [lax.empty((3, *p.shape[1:]), BF16) for p in sums]
        self.state[tag] = (blocks, from_sibling, _split_start(f"grads_{tag}_chips_start", sums, lands, 3 * len(sums), _rs_chip_copies))
        return self.state[tag][2][4]

    def grads_end(self, tag, after):
        blocks, from_sibling, (send_sems, recv_sems, sums, lands, _) = self.state[tag]
        after = list(after) if isinstance(after, (list, tuple)) else [after]
        _, from_chips = _split_wait(f"grads_{tag}_chips_wait", send_sems, recv_sems, sums, lands, after, _rs_chip_copies)
        self.layers[tag] = list(zip(blocks, from_sibling, from_chips))


SHARDED = ("w_in_e", "w_qb", "w_kvb", "w_out_e", "w_in_o", "w_out_o", "w_ff1", "w_ff2")


def kernel(x, positions, w_in_e, mla_gq, mla_gkv, w_qb, w_kvb, sgu_ln_g, sgu_ln_b, sgu_w, sgu_b, w_out_e, w_in_o, hg_lb, hg_gnorm, w_out_o, ln1_g, ln1_b, w_ff1, w_ff2, ln2_g, ln2_b, loss_target, m_w_in_e, m_mla_gq, m_mla_gkv, m_w_qb, m_w_kvb, m_sgu_ln_g, m_sgu_ln_b, m_sgu_w, m_sgu_b, m_w_out_e, m_w_in_o, m_hg_lb, m_hg_gnorm, m_w_out_o, m_ln1_g, m_ln1_b, m_w_ff1, m_w_ff2, m_ln2_g, m_ln2_b, v_w_in_e, v_mla_gq, v_mla_gkv, v_w_qb, v_w_kvb, v_sgu_ln_g, v_sgu_ln_b, v_sgu_w, v_sgu_b, v_w_out_e, v_w_in_o, v_hg_lb, v_hg_gnorm, v_w_out_o, v_ln1_g, v_ln1_b, v_w_ff1, v_w_ff2, v_ln2_g, v_ln2_b):
    given = dict(locals())
    ex = _Exchange(given)

    names = ["w_in_e", "w_qb", "w_kvb", "w_out_e"]
    placed = _place_own([(given[n], 0, BF16) for n in names] + [(hg_gnorm.reshape(1, 1, D_MODEL // N_DEV), 0, F32)]
                        + [(w_in_o, 0, BF16), (w_out_o, 0, BF16), (w_ff1, 0, BF16), (w_ff1, 1, BF16), (w_ff2, 0, BF16), (w_ff2, 1, BF16)],
                        ex.dev)
    got = _all_gather(placed[:5])
    ex.start_weights(placed[5:], after=[got[0]])
    gw = dict(zip(names, got[:4]))
    small_names = ["mla_gq", "mla_gkv", "sgu_ln_g", "sgu_ln_b", "sgu_w", "sgu_b", "hg_lb", "ln1_g", "ln1_b", "ln2_g", "ln2_b"]
    sp = {n: given[n] for n in small_names}
    sp["hg_gnorm"] = got[4].reshape(1, D_MODEL)

    _, dx, grads, gs = _local_step(x[0], positions[0], loss_target[0], gw, sp, ex)

    def finish(n, layers, deps=()):
        return _finish_sharded(f"finish_{n}", layers, given[n], given["m_" + n], given["v_" + n], ex.where, deps=deps)

    l1, l0m = ex.layers["l1"], ex.layers["l0m"]
    results = {}
    token = ex.grads_start("l0s", [grads[n] for n in names])
    results["w_ff1"] = finish("w_ff1", [l0m[0], l1[0]], deps=[token])
    token = ex.grads_middle("l0s", after=results["w_ff1"][0])
    results["w_ff2"] = finish("w_ff2", [l0m[1], l1[1]], deps=[token])
    results["w_in_o"] = finish("w_in_o", [l1[2]], deps=[token])
    results["w_out_o"] = finish("w_out_o", [l1[3]], deps=[token])
    results.update(ex.small_finish(after=[results["w_in_o"][0]]))
    ex.grads_end("l0s", after=[results[n][0] for n in ("mla_gq", "w_ff2", "w_in_o", "w_out_o")])
    for n, layer in zip(names, ex.layers["l0s"]):
        results[n] = finish(n, [layer])

    order = ["w_in_e", "mla_gq", "mla_gkv", "w_qb", "w_kvb", "sgu_ln_g", "sgu_ln_b", "sgu_w", "sgu_b", "w_out_e", "w_in_o",
             "hg_lb", "hg_gnorm", "w_out_o", "ln1_g", "ln1_b", "w_ff1", "w_ff2", "ln2_g", "ln2_b"]
    return (results["loss"], dx[None], *[results[name][kind] for kind in range(4) for name in order])
```

```python
import functools
import math

import jax
import jax.numpy as jnp
import numpy as np
from jax import lax
from jax.experimental import pallas as pl
from jax.experimental.pallas import tpu as pltpu

F32 = jnp.float32
BF16 = jnp.bfloat16
MESH = pl.DeviceIdType.MESH
HIGHEST = lax.Precision.HIGHEST

D_MODEL = 1024
D_FF = 4096
N_DEV = 8
HEADS = 8
HEAD_W = 128
MLA_NOPE = 64
MLA_ROPE = 32
MLA_V = 64
MLA_LORA = 256
MLA_SCALE = (MLA_NOPE + MLA_ROPE) ** -0.5
ROPE_BASE = 10000.0
SGU_DIM = 512
SGU_G = 4
SGU_CHUNK = 128
HG_CHUNK = 64
HG_CHUNKS_PER_STEP = 4
ALPHA = (2 * 2) ** 0.25
EPS = 1e-5
ADAM_LR, ADAM_B1, ADAM_B2, ADAM_EPS, ADAM_WD, ADAM_STEP = 0.001, 0.9, 0.999, 1e-08, 0.01, 10

VMEM_CAP_V7X = 56 * 2**20
VMEM_SLACK = 12 * 2**20
TM = 512
TN = 512


def _vmem(block_bytes):
    return int(min(VMEM_CAP_V7X, 2 * block_bytes + VMEM_SLACK))


def _hbm(a):
    return pltpu.with_memory_space_constraint(a, pltpu.HBM)


def _nbytes(shape, dtype):
    return int(np.prod([d for d in shape if d is not None])) * jnp.dtype(dtype).itemsize


def _sig(x):
    return 1.0 / (1.0 + jnp.exp(-x))


def _gelu(x):
    c = math.sqrt(2.0 / math.pi)
    t = jnp.tanh(c * (x + 0.044715 * x * x * x))
    return 0.5 * x * (1.0 + t), t


def _gelu_grad(x, t):
    c = math.sqrt(2.0 / math.pi)
    return 0.5 * (1.0 + t) + 0.5 * x * (1.0 - t * t) * c * (1.0 + 3 * 0.044715 * x * x)


def _dot(a, b, dims, precision=None):
    return lax.dot_general(a, b, (dims, ((), ())), preferred_element_type=F32, precision=precision)


NN = ((1,), (0,))
NT = ((1,), (1,))
TN_ = ((0,), (0,))


def _deps(deps):
    return [d for d in deps if d is not None]


def _tiled(name, grid, ins, outs, compute, direct=False, deps=()):
    n_in, deps = len(ins), _deps(deps)
    n_skip = n_in + len(deps)

    def kern(*refs):
        if direct:
            compute(refs[:n_in], refs[n_skip:])
            return
        for o_ref, r in zip(refs[n_skip:], compute(*refs[:n_in])):
            o_ref[...] = r.astype(o_ref.dtype).reshape(o_ref.shape)

    swap = lambda f: (lambda j, i: f(i, j))
    nbytes = sum(_nbytes(blk, a.dtype) for a, blk, _ in ins) + sum(_nbytes(blk, dt) + _nbytes(blk, F32) for _, dt, blk, _ in outs)
    res = pl.pallas_call(
        kern, name=name, grid=grid,
        in_specs=[pl.BlockSpec(blk, swap(f), pipeline_mode=pl.Buffered(1) if tuple(blk) == tuple(a.shape) else None)
                  for a, blk, f in ins] + [ANY_SPEC] * len(deps),
        out_specs=[pl.BlockSpec(blk, swap(f)) for _, _, blk, f in outs],
        out_shape=[pltpu.HBM(shape, dt) for shape, dt, _, _ in outs],
        compiler_params=pltpu.CompilerParams(dimension_semantics=("parallel", "parallel"), vmem_limit_bytes=_vmem(nbytes)),
    )(*[_hbm(a) for a, _, _ in ins], *deps)
    return res if len(res) > 1 else res[0]


def _rb(a, tm, w=None, cb=0):
    return (a, (tm, a.shape[1] if w is None else w), lambda i, j: (i, cb))


def _cw(b, tn):
    return (b, (b.shape[0], tn), lambda i, j: (0, j))


def _tl(a, tm):
    return (a, (a.shape[0], tm), lambda i, j: (0, i))


def _out(m, n, dtype, tm, tn):
    return ((m, n), dtype, (tm, tn), lambda i, j: (i, j))


def _out_dev(k, n, tm, dtype=F32):
    return ((N_DEV, k, n), dtype, (None, tm, n), lambda i, j: (j, i, 0))


def _twice(acc):
    return acc, acc


def _mmc(dims, n_pairs=1, epilogue=None):
    def compute(*refs):
        acc = None
        for k in range(n_pairs):
            d = _dot(refs[2 * k][...].astype(BF16), refs[2 * k + 1][...].astype(BF16), dims)
            acc = d if acc is None else acc + d
        ext = [r[...] for r in refs[2 * n_pairs:]]
        return epilogue(acc, *ext) if epilogue is not None else (acc,)

    return compute


def _res(w):
    return (w, w.shape, functools.partial(lambda i, j, nd: (0,) * nd, nd=w.ndim))


def _mmc_blocks(nblk, dims, rhs_block, epilogue=None):
    def compute(in_refs, out_refs):
        a = in_refs[0][...].astype(BF16)
        for d in range(nblk):
            acc = _dot(a, rhs_block(in_refs[1], d).astype(BF16), dims)
            n = acc.shape[1]
            ext = [r[:, d * n:(d + 1) * n] for r in in_refs[2:]]
            res = epilogue(acc, *ext) if epilogue is not None else (acc,)
            for o_ref, r in zip(out_refs, res):
                o_ref[:, d * n:(d + 1) * n] = r.astype(o_ref.dtype)

    return compute


def _rowwise(name, body, rows, consts, out_rows, out_accs=(), tr=512, deps=()):
    T = rows[0][0].shape[0]
    tr = min(tr, T)
    deps = _deps(deps)
    nr, ncn, no, nd = len(rows), len(consts), len(out_rows), len(deps)

    def kern(*refs):
        accs = refs[nr + ncn + nd + no:]
        if accs:
            @pl.when(pl.program_id(0) == 0)
            def _():
                for a in accs:
                    a[...] = jnp.zeros(a.shape, a.dtype)
        body(refs[:nr], refs[nr:nr + ncn], refs[nr + ncn + nd:nr + ncn + nd + no], accs)

    in_specs = [pl.BlockSpec((tr, w), functools.partial(lambda i, cb: (i, cb), cb=cb)) for _, w, cb in rows]
    in_specs += [pl.BlockSpec(c.shape, functools.partial(lambda i, nd: (0,) * nd, nd=c.ndim), pipeline_mode=pl.Buffered(1))
                 for c in consts]
    in_specs += [ANY_SPEC] * nd
    out_specs = [pl.BlockSpec((tr, w), lambda i: (i, 0)) for w, _ in out_rows]
    out_specs += [pl.BlockSpec(s, functools.partial(lambda i, nd: (0,) * nd, nd=len(s))) for s, _ in out_accs]
    out_shape = [pltpu.HBM((T, w), dt) for w, dt in out_rows]
    out_shape += [pltpu.HBM(s, dt) for s, dt in out_accs]
    nbytes = sum(_nbytes((tr, w), a.dtype) for a, w, _ in rows) + sum(_nbytes(c.shape, c.dtype) for c in consts)
    nbytes += sum(_nbytes((tr, w), dt) for w, dt in out_rows) + sum(_nbytes(s, dt) for s, dt in out_accs)
    res = pl.pallas_call(
        kern, name=name, grid=(T // tr,), in_specs=in_specs, out_specs=out_specs, out_shape=out_shape,
        compiler_params=pltpu.CompilerParams(dimension_semantics=("arbitrary",), vmem_limit_bytes=_vmem(nbytes)),
    )(*[_hbm(a) for a, _, _ in rows], *[_hbm(c) for c in consts], *deps)
    return res if len(res) > 1 else res[0]


def _full(a):
    return (a, a.shape[1], 0)


def _ln_stats(y):
    mu = jnp.mean(y, axis=-1, keepdims=True)
    yc = y - mu
    r = lax.rsqrt(jnp.mean(yc * yc, axis=-1, keepdims=True) + EPS)
    return yc * r, r


def _row_halves(n):
    return [slice(0, n // 2), slice(n // 2, n)] if n >= 256 else [slice(0, n)]


def _ln_back(dh, xh, r, gain, dg_ref, db_ref):
    dg_ref[...] += jnp.sum(dh * xh, axis=0, keepdims=True)
    db_ref[...] += jnp.sum(dh, axis=0, keepdims=True)
    dx = dh * gain
    return r * (dx - jnp.mean(dx, axis=-1, keepdims=True) - xh * jnp.mean(dx * xh, axis=-1, keepdims=True))


def _proj_ln(name, acts, weights, h_in, g, b, layer, deps=()):
    n = len(acts)

    def body(rows, consts, outs, accs):
        acc = None
        for k in range(n):
            d = _dot(rows[k][...].astype(BF16), consts[k][...], NN)
            acc = d if acc is None else acc + d
        y = ALPHA * rows[n][...] + acc
        xh, _ = _ln_stats(y)
        h = xh * consts[n][layer:layer + 1, :] + consts[n + 1][layer:layer + 1, :]
        outs[0][...] = y
        outs[1][...] = h
        outs[2][...] = h.astype(BF16)

    return _rowwise(name, body, [_full(a) for a in acts] + [_full(h_in)], [*weights, g, b],
                    [(D_MODEL, F32), (D_MODEL, F32), (D_MODEL, BF16)], tr=TM, deps=deps)


def _proj_ln_loss(name, act, w2, h_in, g, b, layer, target):
    def body(rows, consts, outs, accs):
        y = ALPHA * rows[1][...] + _dot(rows[0][...], consts[0][...], NN)
        xh, r = _ln_stats(y)
        gain = consts[1][layer:layer + 1, :]
        err = xh * gain + consts[2][layer:layer + 1, :] - rows[2][...]
        accs[0][...] += jnp.sum(err * err, axis=0, keepdims=True)
        dy = _ln_back(err * (1.0 / D_MODEL), xh, r, gain, accs[1], accs[2])
        outs[0][...] = dy
        outs[1][...] = dy.astype(BF16)

    return _rowwise(name, body, [_full(act), _full(h_in), _full(target)], [w2, g, b], [(D_MODEL, F32), (D_MODEL, BF16)],
                    [((1, D_MODEL), F32)] * 3, tr=TM)


def _dh_ln_back(name, da, w, dy_next, y, g, layer, proj=(), deps=()):
    def body(rows, consts, outs, accs):
        n = consts[0].shape[2]
        for sl in _row_halves(rows[0].shape[0]):
            acc = ALPHA * rows[1][sl, :]
            for d in range(N_DEV):
                acc = acc + _dot(rows[0][sl, d * n:(d + 1) * n], consts[0][d], NT)
            xh, r = _ln_stats(rows[2][sl, :])
            dy = _ln_back(acc, xh, r, consts[1][layer:layer + 1, :], accs[0], accs[1])
            outs[0][sl, :] = dy
            dy_bf = dy.astype(BF16)
            outs[1][sl, :] = dy_bf
            off = 0
            for k, p in enumerate(proj):
                outs[2][sl, off:off + p.shape[0]] = _dot(dy_bf, consts[2 + k][...], NT).astype(BF16)
                off += p.shape[0]

    out_rows = [(D_MODEL, F32), (D_MODEL, BF16)] + ([(sum(p.shape[0] for p in proj), BF16)] if proj else [])
    return _rowwise(name, body, [_full(da), _full(dy_next), _full(y)], [w, g, *proj], out_rows,
                    [((1, D_MODEL), F32)] * 2, tr=TM, deps=deps)


def _relu2_epilogue(acc):
    a = jnp.maximum(acc, 0.0)
    return acc, a * a


def _mlp_up(tag, h_bf, w1):
    T = h_bf.shape[0]
    tm = min(TM, T)
    return _tiled(f"{tag}_ff1", (1, T // tm), [_rb(h_bf, tm), _res(w1)],
                  [_out(T, D_FF, BF16, tm, D_FF), _out(T, D_FF, BF16, tm, D_FF)],
                  _mmc_blocks(N_DEV, NN, lambda w, d: w[d], epilogue=_relu2_epilogue), direct=True)


def _mlp_bwd_w(tag, h_bf, a, act, dff_bf, w2, deps=()):
    T = h_bf.shape[0]
    tm = min(TM, T)
    da = _tiled(f"{tag}_dact", (1, T // tm), [_rb(dff_bf, tm), _res(w2), _rb(a, tm)], [_out(T, D_FF, BF16, tm, D_FF)],
                _mmc_blocks(N_DEV, NT, lambda w, d: w[d], epilogue=lambda acc, a_t: (acc * 2.0 * jnp.maximum(a_t.astype(F32), 0.0),)),
                direct=True, deps=deps)
    dw2 = _tiled(f"{tag}_dw2", (1, D_FF // TM), [_tl(act, TM), _res(dff_bf)],
                 [_out(D_FF, D_MODEL, F32, TM, D_MODEL), _out(D_FF, D_MODEL, BF16, TM, D_MODEL)], _mmc(TN_, epilogue=_twice))
    dw1 = _tiled(f"{tag}_dw1", (N_DEV, 1), [_res(h_bf), _cw(da, TN)],
                 [_out_dev(D_MODEL, TN, D_MODEL), _out_dev(D_MODEL, TN, D_MODEL, BF16)], _mmc(TN_, epilogue=_twice))
    return da, dw1, [a.reshape(N_DEV, D_FF // N_DEV, D_MODEL) for a in dw2]


def _rope_tables(positions_col, invf_lane):
    def body(rows, consts, outs, accs):
        ang = rows[0][...].astype(F32) * consts[0][...]
        c, s = jnp.cos(ang), jnp.sin(ang)
        lane = lax.broadcasted_iota(jnp.int32, ang.shape, 1)
        outs[0][...] = jnp.where(lane < 64, 1.0, jnp.where(lane < 96, c, 0.0))
        outs[1][...] = jnp.where((lane >= 64) & (lane < 80), -s, 0.0)
        outs[2][...] = jnp.where((lane >= 80) & (lane < 96), s, 0.0)

    return _rowwise("rope_tables", body, [_full(positions_col)], [invf_lane], [(HEAD_W, F32)] * 3)


def _rope(x, c, s1, s2):
    return x * c + pltpu.roll(x, 112, 1) * s1 + pltpu.roll(x, 16, 1) * s2


def _rope_t(dx, c, s1, s2):
    return dx * c + pltpu.roll(dx * s1, 16, 1) + pltpu.roll(dx * s2, 112, 1)


def _rms(c):
    r = lax.rsqrt(jnp.mean(c * c, axis=-1, keepdims=True) + EPS)
    return c * r, r


def _rope_heads(x, c, s1, s2, fn):
    return jnp.concatenate([fn(x[:, h * HEAD_W:(h + 1) * HEAD_W], c, s1, s2) for h in range(HEADS)], axis=1)


def _mla_in(x, wm, ws, tabs, gq, gkv, deps=()):
    def body(rows, consts, outs, accs):
        xb = rows[0][...].astype(BF16)
        zm = _dot(xb, consts[0][...], NN)
        outs[0][...] = zm
        outs[1][...] = _dot(xb, consts[1][...], NN)
        outs[2][...] = (_rms(zm[:, 0:256])[0] * consts[2][...]).astype(BF16)
        outs[3][...] = (_rms(zm[:, 256:512])[0] * consts[3][...]).astype(BF16)
        outs[4][...] = _rope(zm[:, 512:640], rows[1][...], rows[2][...], rows[3][...])

    return _rowwise("l0_in", body, [_full(x)] + [_full(t) for t in tabs], [wm, ws, gq, gkv],
                    [(640, F32), (1024, F32), (256, BF16), (256, BF16), (HEAD_W, F32)], deps=deps)


def _mla_qkv(cqn, ckvn, kr_rot, tabs, wq, wk, wv):
    def body(rows, consts, outs, accs):
        c, s1, s2 = rows[3][...], rows[4][...], rows[5][...]
        outs[0][...] = _rope_heads(_dot(rows[0][...], consts[0][...], NN), c, s1, s2, _rope).astype(BF16)
        outs[1][...] = (_dot(rows[1][...], consts[1][...], NN) + jnp.concatenate([rows[2][...]] * HEADS, axis=1)).astype(BF16)
        outs[2][...] = _dot(rows[1][...], consts[2][...], NN).astype(BF16)

    rows = [_full(cqn), _full(ckvn), _full(kr_rot)] + [_full(t) for t in tabs]
    return _rowwise("l0_qkv", body, rows, [wq, wk, wv], [(HEADS * HEAD_W, BF16)] * 3)


def _mla_back(zm, cqn, ckvn, tabs, gq, gkv, wq, wk, wv, dq, dk, dv):
    def body(rows, consts, outs, accs):
        c, s1, s2 = rows[4][...], rows[5][...], rows[6][...]
        dk_t, dv_bf = rows[8][...], rows[9][...].astype(BF16)
        dq_bf = _rope_heads(rows[7][...], c, s1, s2, _rope_t).astype(BF16)
        dk_bf = dk_t.astype(BF16)
        accs[0][...] += _dot(rows[2][...], dq_bf, TN_)
        accs[1][...] += _dot(rows[3][...], dk_bf, TN_)
        accs[2][...] += _dot(rows[3][...], dv_bf, TN_)
        dlat = [_dot(dq_bf, consts[2][...], NT), _dot(dk_bf, consts[3][...], NT) + _dot(dv_bf, consts[4][...], NT)]
        for k in range(2):
            ch, r = _rms(rows[k][...])
            accs[3 + k][...] += jnp.sum(dlat[k] * ch, axis=0, keepdims=True)
            dc = dlat[k] * consts[k][...]
            outs[0][:, 256 * k:256 * (k + 1)] = (r * (dc - ch * jnp.mean(dc * ch, axis=-1, keepdims=True))).astype(BF16)
        dks = dk_t[:, 0:HEAD_W]
        for h in range(1, HEADS):
            dks = dks + dk_t[:, h * HEAD_W:(h + 1) * HEAD_W]
        lane = lax.broadcasted_iota(jnp.int32, dks.shape, 1)
        dks = jnp.where((lane >= 64) & (lane < 96), dks, 0.0)
        outs[0][:, 512:640] = _rope_t(dks, c, s1, s2).astype(BF16)

    rows = [(zm, 256, 0), (zm, 256, 1), _full(cqn), _full(ckvn)] + [_full(t) for t in tabs] + [_full(dq), _full(dk), _full(dv)]
    wide = HEADS * HEAD_W
    return _rowwise("l0_mla_back", body, rows, [gq, gkv, wq, wk, wv], [(640, BF16)],
                    [((MLA_LORA, wide), F32)] * 3 + [((1, MLA_LORA), F32)] * 2, tr=256)


def _in_back(x, dzm, dzs, dy, wm, ws, deps=()):
    def body(rows, consts, outs, accs):
        dzm_t, dzs_t = rows[1][...], rows[2][...]
        outs[0][...] = _dot(dzm_t, consts[0][...], NT) + _dot(dzs_t, consts[1][...], NT) + ALPHA * rows[3][...]
        xb = rows[0][...].astype(BF16)
        accs[0][...] += _dot(xb, dzm_t, TN_)
        accs[1][...] += _dot(xb, dzs_t, TN_)

    return _rowwise("l0_in_back", body, [_full(x), _full(dzm), _full(dzs), _full(dy)], [wm, ws], [(D_MODEL, F32)],
                    [((D_MODEL, 640), F32), ((D_MODEL, 1024), F32)], deps=deps)


def _out_weight_grads(o_att, b_out, dy_bf):
    def body(rows, consts, outs, accs):
        d = rows[2][...]
        accs[0][...] += _dot(rows[0][...].astype(BF16), d, TN_)
        accs[1][...] += _dot(rows[1][...], d, TN_)

    return _rowwise("l0_dw_out", body, [_full(o_att), _full(b_out), _full(dy_bf)], [], [],
                    [((HEADS * HEAD_W, D_MODEL), F32), ((SGU_DIM, D_MODEL), F32)])


def _attn_block(T):
    return min(1024, T)


def _attn_fwd(q, k, v):
    T = q.shape[0]
    BQ = _attn_block(T)
    nq = T // BQ

    def kern(q_ref, k_ref, v_ref, o_ref, lse_ref):
        def step(i, j, carry, masked):
            m, l, acc = carry
            qb = q_ref[pl.ds(pl.multiple_of(i * BQ, BQ), BQ), :]
            kb = k_ref[pl.ds(pl.multiple_of(j * BQ, BQ), BQ), :]
            vb = v_ref[pl.ds(pl.multiple_of(j * BQ, BQ), BQ), :]
            s = _dot(qb, kb, NT) * MLA_SCALE
            if masked:
                row = lax.broadcasted_iota(jnp.int32, s.shape, 0)
                col = lax.broadcasted_iota(jnp.int32, s.shape, 1)
                s = jnp.where(col <= row, s, -1e30)
            m_new = jnp.maximum(m, jnp.max(s, axis=-1, keepdims=True))
            p = jnp.exp(s - m_new)
            a = jnp.exp(m - m_new)
            l = a * l + jnp.sum(p, axis=-1, keepdims=True)
            acc = a * acc + _dot(p.astype(BF16), vb, NN)
            return m_new, l, acc

        def qloop(i, _):
            init = (jnp.full((BQ, 1), -1e30, F32), jnp.zeros((BQ, 1), F32), jnp.zeros((BQ, HEAD_W), F32))
            carry = lax.fori_loop(0, i, lambda j, c: step(i, j, c, False), init)
            m, l, acc = step(i, i, carry, True)
            rows = pl.ds(pl.multiple_of(i * BQ, BQ), BQ)
            o_ref[rows, :] = acc / l
            lse_ref[0, rows, :] = m + jnp.log(l)
            return 0

        lax.fori_loop(0, nq, qloop, 0)

    head = pl.BlockSpec((T, HEAD_W), lambda h: (0, h))
    nbytes = 3 * _nbytes((T, HEAD_W), BF16) + _nbytes((T, HEAD_W), F32) + _nbytes((T, 128), F32)
    return pl.pallas_call(
        kern, name="attn_fwd", grid=(HEADS,), in_specs=[head, head, head],
        out_specs=[head, pl.BlockSpec((1, T, 1), lambda h: (h, 0, 0))],
        out_shape=[pltpu.HBM((T, HEADS * HEAD_W), F32), pltpu.HBM((HEADS, T, 1), F32)],
        compiler_params=pltpu.CompilerParams(dimension_semantics=("parallel",), vmem_limit_bytes=_vmem(nbytes)),
    )(_hbm(q), _hbm(k), _hbm(v))


def _attn_bwd(q, k, v, o, lse, dcat, deps=()):
    T = q.shape[0]
    BQ = _attn_block(T)
    nq = T // BQ
    deps = _deps(deps)

    def kern(q_ref, k_ref, v_ref, o_ref, lse_ref, do_ref, *rest):
        dq_ref, dk_ref, dv_ref, dd_ref = rest[len(deps):]
        dq_ref[...] = jnp.zeros(dq_ref.shape, F32)

        def dloop(i, _):
            rows = pl.ds(pl.multiple_of(i * BQ, BQ), BQ)
            dd_ref[rows, :] = jnp.sum(do_ref[rows, :].astype(F32) * o_ref[rows, :], axis=-1, keepdims=True)
            return 0

        lax.fori_loop(0, nq, dloop, 0)

        def step(j, i, carry, masked):
            dk_acc, dv_acc = carry
            rq = pl.ds(pl.multiple_of(i * BQ, BQ), BQ)
            rk = pl.ds(pl.multiple_of(j * BQ, BQ), BQ)
            qb, kb, vb, dob = q_ref[rq, :], k_ref[rk, :], v_ref[rk, :], do_ref[rq, :]
            s = _dot(qb, kb, NT) * MLA_SCALE
            p = jnp.exp(s - lse_ref[0, rq, :])
            if masked:
                row = lax.broadcasted_iota(jnp.int32, s.shape, 0)
                col = lax.broadcasted_iota(jnp.int32, s.shape, 1)
                p = jnp.where(col <= row, p, 0.0)
            dp = _dot(dob, vb, NT)
            ds = (p * (dp - dd_ref[rq, :]) * MLA_SCALE).astype(BF16)
            dv_acc = dv_acc + _dot(p.astype(BF16), dob, TN_)
            dk_acc = dk_acc + _dot(ds, qb, TN_)
            dq_ref[rq, :] += _dot(ds, kb, NN)
            return dk_acc, dv_acc

        def kloop(j, _):
            init = (jnp.zeros((BQ, HEAD_W), F32), jnp.zeros((BQ, HEAD_W), F32))
            carry = step(j, j, init, True)
            dk_acc, dv_acc = lax.fori_loop(j + 1, nq, lambda i, c: step(j, i, c, False), carry)
            rk = pl.ds(pl.multiple_of(j * BQ, BQ), BQ)
            dk_ref[rk, :] = dk_acc
            dv_ref[rk, :] = dv_acc
            return 0

        lax.fori_loop(0, nq, kloop, 0)

    head = pl.BlockSpec((T, HEAD_W), lambda h: (0, h))
    nbytes = 4 * _nbytes((T, HEAD_W), BF16) + 5 * _nbytes((T, HEAD_W), F32) + 2 * _nbytes((T, 128), F32)
    return pl.pallas_call(
        kern, name="attn_bwd", grid=(HEADS,),
        in_specs=[head, head, head, head, pl.BlockSpec((1, T, 1), lambda h: (h, 0, 0)), head] + [ANY_SPEC] * len(deps),
        out_specs=[head, head, head],
        out_shape=[pltpu.HBM((T, HEADS * HEAD_W), F32)] * 3,
        scratch_shapes=[pltpu.VMEM((T, 1), F32)],
        compiler_params=pltpu.CompilerParams(dimension_semantics=("parallel",), vmem_limit_bytes=_vmem(nbytes)),
    )(*[_hbm(a) for a in (q, k, v, o, lse, dcat)], *deps)


def _sgu_common(u, v, ln_g, ln_b):
    ua, tu = _gelu(u)
    va, tv = _gelu(v)
    vh, r = _ln_stats(va)
    return ua, tu, tv, vh, r, vh * ln_g + ln_b


def _tril_mask(n):
    return lax.broadcasted_iota(jnp.int32, (n, n), 1) <= lax.broadcasted_iota(jnp.int32, (n, n), 0)


def _sgu_fwd(zs, ln_g, ln_b, w, bias_full):
    def body(rows, consts, outs, accs):
        ua, _, _, _, _, vn = _sgu_common(rows[0][...], rows[1][...], consts[0][...], consts[1][...])
        vn = vn.astype(BF16)
        tri = _tril_mask(SGU_CHUNK)
        for g in range(SGU_G):
            wg = jnp.where(tri, consts[2][0, g], 0.0).astype(BF16)
            cols = slice(g * 128, (g + 1) * 128)
            for c in range(ua.shape[0] // SGU_CHUNK):
                rws = slice(c * SGU_CHUNK, (c + 1) * SGU_CHUNK)
                mixed = _dot(wg, vn[rws, cols], NN) + consts[3][:, cols]
                outs[0][rws, cols] = (ua[rws, cols] * mixed).astype(BF16)

    return _rowwise("sgu_fwd", body, [(zs, 512, 0), (zs, 512, 1)], [ln_g, ln_b, w, bias_full], [(SGU_DIM, BF16)])


def _sgu_bwd(zs, dcat, ln_g, ln_b, w, bias_full):
    def body(rows, consts, outs, accs):
        u, v = rows[0][...], rows[1][...]
        ua, tu, tv, vh, r, vn = _sgu_common(u, v, consts[0][...], consts[1][...])
        dout = rows[2][...].astype(F32)
        vn_bf = vn.astype(BF16)
        tri = _tril_mask(SGU_CHUNK)
        dmixed = (dout * ua)
        dmixed_bf = dmixed.astype(BF16)
        ones = jnp.ones((8, SGU_CHUNK), F32)
        dvn_cols, mixed_cols = [], []
        for g in range(SGU_G):
            wg = jnp.where(tri, consts[2][0, g], 0.0).astype(BF16)
            cols = slice(g * 128, (g + 1) * 128)
            dvn_rows, mixed_rows = [], []
            dw = jnp.zeros((SGU_CHUNK, SGU_CHUNK), F32)
            dmix_sum = jnp.zeros((SGU_CHUNK, 128), F32)
            for c in range(u.shape[0] // SGU_CHUNK):
                rws = slice(c * SGU_CHUNK, (c + 1) * SGU_CHUNK)
                mixed_rows.append(_dot(wg, vn_bf[rws, cols], NN) + consts[3][:, cols])
                dvn_rows.append(_dot(wg, dmixed_bf[rws, cols], TN_))
                dw = dw + _dot(dmixed_bf[rws, cols], vn_bf[rws, cols], NT)
                dmix_sum = dmix_sum + dmixed[rws, cols]
            accs[0][g] += jnp.where(tri, dw, 0.0)
            accs[3][g:g + 1, :] += _dot(ones, dmix_sum, NT, precision=HIGHEST)[0:1, :]
            dvn_cols.append(jnp.concatenate(dvn_rows, axis=0))
            mixed_cols.append(jnp.concatenate(mixed_rows, axis=0))
        dvn = jnp.concatenate(dvn_cols, axis=1)
        mixed = jnp.concatenate(mixed_cols, axis=1)
        accs[1][...] += jnp.sum(dvn * vh, axis=0, keepdims=True)
        accs[2][...] += jnp.sum(dvn, axis=0, keepdims=True)
        dvh = dvn * consts[0][...]
        dva = r * (dvh - jnp.mean(dvh, axis=-1, keepdims=True) - vh * jnp.mean(dvh * vh, axis=-1, keepdims=True))
        outs[0][:, 0:512] = (dout * mixed * _gelu_grad(u, tu)).astype(BF16)
        outs[0][:, 512:1024] = (dva * _gelu_grad(v, tv)).astype(BF16)

    return _rowwise("sgu_bwd", body, [(zs, 512, 0), (zs, 512, 1), (dcat, 512, 2)], [ln_g, ln_b, w, bias_full], [(1024, BF16)],
                    [((SGU_G, 128, 128), F32), ((1, SGU_DIM), F32), ((1, SGU_DIM), F32), ((SGU_G, 128), F32)], tr=256)


def _lower_bound(hg_lb):
    a0, a1 = hg_lb[0:1, :], hg_lb[1:2, :]
    m = jnp.maximum(a0, a1)
    e0, e1 = jnp.exp(a0 - m), jnp.exp(a1 - m)
    s0, s1 = e0 / (e0 + e1), e1 / (e0 + e1)
    return (s0 + s1) - s0, s0, s1


def _prefix_rows(x, reverse=False):
    n = x.shape[0]
    row = lax.broadcasted_iota(jnp.int32, x.shape, 0)
    s = 1
    while s < n:
        if reverse:
            x = x + jnp.where(row < n - s, pltpu.roll(x, n - s, 0), 0.0)
        else:
            x = x + jnp.where(row >= s, pltpu.roll(x, s, 0), 0.0)
        s *= 2
    return x


def _hg_gates(qr, fr, lb):
    C = qr.shape[0]
    sq = _sig(qr)
    qf = qr * sq
    sf = _sig(fr)
    gate = lb + (1.0 - lb) * sf
    kk = 1.0 - gate
    tri = _tril_mask(C)
    b = _prefix_rows(jnp.log(gate))
    bref = b[C // 2 - 1:C // 2, :]
    bl = b[C - 1:C, :]
    e_b = jnp.exp(b)
    e_q = jnp.exp(b - bref)
    e_k = jnp.exp(bref - b)
    e_lb = jnp.exp(bl - b)
    return dict(sq=sq, qf=qf, sf=sf, gate=gate, kk=kk, tri=tri, bl=bl, e_b=e_b, e_q=e_q, e_k=e_k, e_lb=e_lb)


def _hgrn_fwd(z1, hg_lb, gnorm):
    T = z1.shape[0]
    C = min(HG_CHUNK, T)
    nc = T // C
    ns = HG_CHUNKS_PER_STEP if nc % HG_CHUNKS_PER_STEP == 0 else 1
    R = ns * C

    def kern(q_ref, f_ref, i_ref, g_ref, lb_ref, gn_ref, o_ref, hg_ref, st_ref, s_scr):
        @pl.when(pl.program_id(0) == 0)
        def _():
            s_scr[...] = jnp.zeros(s_scr.shape, F32)

        lb_all, _, _ = _lower_bound(lb_ref[...])
        for sub in range(ns):
            rows = slice(sub * C, (sub + 1) * C)
            st_ref[sub] = s_scr[...]
            for h in range(HEADS):
                cols = slice(h * HEAD_W, (h + 1) * HEAD_W)
                t = _hg_gates(q_ref[rows, cols], f_ref[rows, cols], lb_all[:, cols])
                v_bf = i_ref[rows, cols].astype(BF16)
                st = s_scr[h]
                a = jnp.where(t["tri"], _dot((t["qf"] * t["e_q"]).astype(BF16), (t["kk"] * t["e_k"]).astype(BF16), NT), 0.0)
                o = _dot(a.astype(BF16), v_bf, NN) + _dot((t["qf"] * t["e_b"]).astype(BF16), st.astype(BF16), NT)
                s_scr[h] = st * jnp.exp(t["bl"]) + _dot(v_bf, (t["kk"] * t["e_lb"]).astype(BF16), TN_)
                o_ref[rows, cols] = o
                gr = g_ref[rows, cols]
                r = lax.rsqrt(jnp.mean(o * o, axis=-1, keepdims=True) + EPS)
                hg_ref[rows, cols] = (o * r * gn_ref[:, cols] * (gr * _sig(gr))).astype(BF16)

    seg = lambda k: pl.BlockSpec((R, D_MODEL), functools.partial(lambda n, k: (n, k), k=k))
    row = pl.BlockSpec((R, D_MODEL), lambda n: (n, 0))
    nbytes = 6 * _nbytes((R, D_MODEL), F32) + (2 + ns) * _nbytes((HEADS, 128, 128), F32)
    return pl.pallas_call(
        kern, name="hgrn_fwd", grid=(nc // ns,),
        in_specs=[seg(0), seg(1), seg(2), seg(3), pl.BlockSpec((2, D_MODEL), lambda n: (0, 0)),
                  pl.BlockSpec((1, D_MODEL), lambda n: (0, 0))],
        out_specs=[row, row, pl.BlockSpec((ns, HEADS, 128, 128), lambda n: (n, 0, 0, 0))],
        out_shape=[pltpu.HBM((T, D_MODEL), F32), pltpu.HBM((T, D_MODEL), BF16),
                   pltpu.HBM((nc, HEADS, 128, 128), F32)],
        scratch_shapes=[pltpu.VMEM((HEADS, 128, 128), F32)],
        compiler_params=pltpu.CompilerParams(dimension_semantics=("arbitrary",), vmem_limit_bytes=_vmem(nbytes)),
    )(*[_hbm(a) for a in (z1, z1, z1, z1, hg_lb, gnorm)])


def _hgrn_bwd(z1, o_pre, dhg, states, hg_lb, gnorm):
    T = z1.shape[0]
    C = min(HG_CHUNK, T)
    nc = T // C
    ns = HG_CHUNKS_PER_STEP if nc % HG_CHUNKS_PER_STEP == 0 else 1
    R, steps = ns * C, nc // ns

    def kern(q_ref, f_ref, i_ref, g_ref, o_ref, dhg_ref, st_ref, lb_ref, gn_ref, dz_ref, dlb_ref, dgn_ref, ds_scr, dlb_scr):
        n = pl.program_id(0)

        @pl.when(n == 0)
        def _():
            ds_scr[...] = jnp.zeros(ds_scr.shape, F32)
            dlb_scr[...] = jnp.zeros(dlb_scr.shape, F32)
            dgn_ref[...] = jnp.zeros(dgn_ref.shape, F32)

        lb_all, s0, s1 = _lower_bound(lb_ref[...])
        for sub in reversed(range(ns)):
            rows = slice(sub * C, (sub + 1) * C)
            for h in range(HEADS):
                cols = slice(h * HEAD_W, (h + 1) * HEAD_W)
                lb = lb_all[:, cols]
                qr, fr = q_ref[rows, cols], f_ref[rows, cols]
                t = _hg_gates(qr, fr, lb)
                tri = t["tri"]
                v_bf = i_ref[rows, cols].astype(BF16)
                st_bf = st_ref[sub, h].astype(BF16)
                dst = ds_scr[h]
                dst_bf = dst.astype(BF16)
                o = o_ref[rows, cols]
                gr = g_ref[rows, cols]
                sg = _sig(gr)
                sil = gr * sg
                gn = gn_ref[:, cols]
                r = lax.rsqrt(jnp.mean(o * o, axis=-1, keepdims=True) + EPS)
                on = o * r
                dh = dhg_ref[rows, cols].astype(F32)
                dgn_ref[:, cols] += jnp.sum(dh * on * sil, axis=0, keepdims=True)
                dg = dh * on * gn * (sg * (1.0 + gr * (1.0 - sg)))
                don = dh * gn * sil
                do_bf = (r * (don - on * jnp.mean(don * on, axis=-1, keepdims=True))).astype(BF16)
                qe = (t["qf"] * t["e_q"]).astype(BF16)
                ke = (t["kk"] * t["e_k"]).astype(BF16)
                qb = (t["qf"] * t["e_b"]).astype(BF16)
                kh_bf = (t["kk"] * t["e_lb"]).astype(BF16)
                a_bf = jnp.where(tri, _dot(qe, ke, NT), 0.0).astype(BF16)
                da_bf = jnp.where(tri, _dot(do_bf, v_bf, NT), 0.0).astype(BF16)
                dv = _dot(a_bf, do_bf, TN_) + _dot(kh_bf, dst_bf, NT)
                dqe = _dot(da_bf, ke, NN)
                dqb = _dot(do_bf, st_bf, NN)
                dke = _dot(da_bf, qe, TN_)
                dkh = _dot(v_bf, dst_bf, NN)
                dqf = dqe * t["e_q"] + dqb * t["e_b"]
                dkk = dke * t["e_k"] + dkh * t["e_lb"]
                kh_r = kh_bf.astype(F32)
                db = qe.astype(F32) * dqe - ke.astype(F32) * dke + qb.astype(F32) * dqb - kh_r * dkh
                e_bl = jnp.exp(t["bl"])
                dbl = jnp.sum(dkh * kh_r, axis=0, keepdims=True) + e_bl * jnp.sum(st_ref[sub, h] * dst, axis=0, keepdims=True)
                dlg = _prefix_rows(db, reverse=True) + dbl
                ds_scr[h] = dst * e_bl + _dot(do_bf, qb, TN_)
                dgate = dlg / t["gate"] - dkk
                sf = t["sf"]
                dlb_scr[:, cols] += jnp.sum(dgate * (1.0 - sf), axis=0, keepdims=True)
                df = dgate * (1.0 - lb) * sf * (1.0 - sf)
                dq = dqf * (t["sq"] * (1.0 + qr * (1.0 - t["sq"])))
                dz_ref[rows, cols] = dq.astype(BF16)
                dz_ref[rows, D_MODEL + h * HEAD_W:D_MODEL + (h + 1) * HEAD_W] = df.astype(BF16)
                dz_ref[rows, 2 * D_MODEL + h * HEAD_W:2 * D_MODEL + (h + 1) * HEAD_W] = dv.astype(BF16)
                dz_ref[rows, 3 * D_MODEL + h * HEAD_W:3 * D_MODEL + (h + 1) * HEAD_W] = dg.astype(BF16)

        @pl.when(n == steps - 1)
        def _():
            d = s0 * s1 * dlb_scr[...]
            dlb_ref[0:1, :] = -d
            dlb_ref[1:2, :] = d

    seg = lambda k: pl.BlockSpec((R, D_MODEL), functools.partial(lambda n, k: (steps - 1 - n, k), k=k))
    nbytes = 6 * _nbytes((R, D_MODEL), F32) + _nbytes((R, 4 * D_MODEL), BF16) + (2 + ns) * _nbytes((HEADS, 128, 128), F32)
    return pl.pallas_call(
        kern, name="hgrn_bwd", grid=(steps,),
        in_specs=[seg(0), seg(1), seg(2), seg(3), seg(0), seg(0),
                  pl.BlockSpec((ns, HEADS, 128, 128), lambda n: (steps - 1 - n, 0, 0, 0)),
                  pl.BlockSpec((2, D_MODEL), lambda n: (0, 0)), pl.BlockSpec((1, D_MODEL), lambda n: (0, 0))],
        out_specs=[pl.BlockSpec((R, 4 * D_MODEL), lambda n: (steps - 1 - n, 0)),
                   pl.BlockSpec((2, D_MODEL), lambda n: (0, 0)), pl.BlockSpec((1, D_MODEL), lambda n: (0, 0))],
        out_shape=[pltpu.HBM((T, 4 * D_MODEL), BF16), pltpu.HBM((2, D_MODEL), F32),
                   pltpu.HBM((1, D_MODEL), F32)],
        scratch_shapes=[pltpu.VMEM((HEADS, 128, 128), F32), pltpu.VMEM((1, D_MODEL), F32)],
        compiler_params=pltpu.CompilerParams(dimension_semantics=("arbitrary",), vmem_limit_bytes=_vmem(nbytes)),
    )(*[_hbm(a) for a in (z1, z1, z1, z1, o_pre, dhg, states, hg_lb, gnorm)])


def _prep_weights(gw):
    w_in_e = gw["w_in_e"].transpose(1, 0, 2).reshape(D_MODEL, 1568)
    kr = jnp.pad(w_in_e[:, 512:544], ((0, 0), (64, 32)))
    wm = jnp.concatenate([w_in_e[:, 0:512], kr], axis=1)
    ws = w_in_e[:, 544:1568]
    w_qb = gw["w_qb"].transpose(1, 0, 2).reshape(MLA_LORA, HEADS, 96)
    wq = jnp.pad(w_qb, ((0, 0), (0, 0), (0, 32))).reshape(MLA_LORA, HEADS * HEAD_W)
    kvb = gw["w_kvb"].transpose(1, 0, 2).reshape(MLA_LORA, HEADS, 128)
    wk = jnp.pad(kvb[:, :, :64], ((0, 0), (0, 0), (0, 64))).reshape(MLA_LORA, HEADS * HEAD_W)
    wv = jnp.pad(kvb[:, :, 64:], ((0, 0), (0, 0), (0, 64))).reshape(MLA_LORA, HEADS * HEAD_W)
    w_out_e = gw["w_out_e"].reshape(D_MODEL, D_MODEL)
    woa = jnp.pad(w_out_e[:512].reshape(HEADS, 64, D_MODEL), ((0, 0), (0, 64), (0, 0))).reshape(HEADS * HEAD_W, D_MODEL)
    return dict(wm=wm, ws=ws, wq=wq, wk=wk, wv=wv, woa=woa, wob=w_out_e[512:])


def _unprep_grads(g):
    dwm, dws = g["wm"], g["ws"]
    d_in_e = jnp.concatenate([dwm[:, 0:512], dwm[:, 512 + 64:512 + 96], dws], axis=1)
    d_qb = g["wq"].reshape(MLA_LORA, HEADS, HEAD_W)[:, :, :96].reshape(MLA_LORA, HEADS * 96)
    dk = g["wk"].reshape(MLA_LORA, HEADS, HEAD_W)[:, :, :64]
    dv = g["wv"].reshape(MLA_LORA, HEADS, HEAD_W)[:, :, :64]
    d_kvb = jnp.concatenate([dk, dv], axis=2).reshape(MLA_LORA, HEADS * 128)
    d_oa = g["woa"].reshape(HEADS, HEAD_W, D_MODEL)[:, :64].reshape(HEADS * 64, D_MODEL)
    dev_major = lambda a: a.reshape(a.shape[0], N_DEV, a.shape[1] // N_DEV).transpose(1, 0, 2)
    return dict(w_in_e=dev_major(d_in_e), w_qb=dev_major(d_qb), w_kvb=dev_major(d_kvb),
                w_out_e=jnp.concatenate([d_oa, g["wob"]], axis=0).reshape(N_DEV, D_MODEL // N_DEV, D_MODEL))


def _local_step(x, positions, target, gw, sp, ex):
    w = _prep_weights(gw)
    T = x.shape[0]
    tm = min(TM, T)
    nt = T // tm
    half = MLA_ROPE // 2
    inv_freq = ROPE_BASE ** (-jnp.arange(half, dtype=F32) / half)
    invf_lane = jnp.concatenate([jnp.zeros((64,), F32), inv_freq, inv_freq, jnp.zeros((32,), F32)]).reshape(1, HEAD_W)
    tabs = _rope_tables(positions.reshape(T, 1), invf_lane)
    bias_full = jnp.repeat(sp["sgu_b"][0].T, 128, axis=1)
    sgu_w = sp["sgu_w"]
    gq, gkv = sp["mla_gq"], sp["mla_gkv"]
    ln1_g, ln1_b, ln2_g, ln2_b = sp["ln1_g"], sp["ln1_b"], sp["ln2_g"], sp["ln2_b"]
    zm, zs, cqn, ckvn, kr_rot = _mla_in(x, w["wm"], w["ws"], tabs, gq, gkv, deps=[ex.first_token])
    q, k, v = _mla_qkv(cqn, ckvn, kr_rot, tabs, w["wq"], w["wk"], w["wv"])
    o_att, lse = _attn_fwd(q, k, v)
    b_out = _sgu_fwd(zs, sp["sgu_ln_g"], sp["sgu_ln_b"], sgu_w, bias_full)
    token = ex.weights_forward(after=[o_att, b_out])
    y1, h1, h1_bf = _proj_ln("l0_out_ln1", [o_att, b_out], [w["woa"], w["wob"]], x, ln1_g, ln1_b, 0, deps=[token])
    big = ex.weights_ready(after=[y1])
    w_ff1, w_in_o, w_out_o = big["w_ff1"], big["w_in_o"], big["w_out_o"].reshape(D_MODEL, D_MODEL)
    w_ff2 = [a.reshape(D_FF, D_MODEL) for a in big["w_ff2"]]
    a0, act0 = _mlp_up("l0", h1_bf, w_ff1[0])
    y2, h2, h2_bf = _proj_ln("l0_ff2_ln2", [act0], [w_ff2[0]], h1, ln2_g, ln2_b, 0)

    z1 = _tiled("l1_in", (1, nt), [_rb(h2_bf, tm), _res(w_in_o)], [_out(T, 4 * D_MODEL, F32, tm, 4 * D_MODEL)],
                _mmc_blocks(N_DEV, NN, lambda w, d: w[d]), direct=True)
    o_pre, hg, states = _hgrn_fwd(z1, sp["hg_lb"], sp["hg_gnorm"])
    y3, h3, h3_bf = _proj_ln("l1_out_ln1", [hg], [w_out_o], h2, ln1_g, ln1_b, 1)
    a1, act1 = _mlp_up("l1", h3_bf, w_ff1[1])

    gs, g0 = {}, {}
    dy4, dy4_bf, sq_err, gs["ln2_g1"], gs["ln2_b1"] = _proj_ln_loss("l1_ff2_loss", act1, w_ff2[1], h3, ln2_g, ln2_b, 1, target)
    gs["sq_err"] = sq_err
    da1, dw1_1, dw2_1 = _mlp_bwd_w("l1", h3_bf, a1, act1, dy4_bf, big["w_ff2"][1])
    dy3, dy3_bf, dhg, gs["ln1_g1"], gs["ln1_b1"] = _dh_ln_back("l1_dh_ln1", da1, w_ff1[1], dy4, y3, ln1_g, 1, proj=[w_out_o])
    d_out_o = _tiled("l1_dwout", (2, D_MODEL // TM), [_tl(hg, TM), _cw(dy3_bf, TN)],
                     [_out(D_MODEL, D_MODEL, F32, TM, TN), _out(D_MODEL, D_MODEL, BF16, TM, TN)], _mmc(TN_, epilogue=_twice))
    d_out_o = [a.reshape(N_DEV, D_MODEL // N_DEV, D_MODEL) for a in d_out_o]
    dz1, gs["hg_lb"], gs["hg_gnorm"] = _hgrn_bwd(z1, o_pre, dhg, states, sp["hg_lb"], sp["hg_gnorm"])
    d_in_o = _tiled("l1_dwin", (N_DEV, 1), [_res(h2_bf), _cw(dz1, TN)],
                    [_out_dev(D_MODEL, TN, D_MODEL), _out_dev(D_MODEL, TN, D_MODEL, BF16)], _mmc(TN_, epilogue=_twice))
    token = ex.direct_start("l1", [dw1_1, dw2_1, d_in_o, d_out_o])

    dy2, dy2_bf, gs["ln2_g0"], gs["ln2_b0"] = _dh_ln_back("l1_dh_ln2", dz1, w_in_o, dy3, y2, ln2_g, 0, deps=[token])
    da0, dw1_0, dw2_0 = _mlp_bwd_w("l0", h1_bf, a0, act0, dy2_bf, big["w_ff2"][0])
    token = ex.direct_start("l0m", [dw1_0, dw2_0])
    dy1, dy1_bf, dcat, gs["ln1_g0"], gs["ln1_b0"] = _dh_ln_back("l0_dh_ln1", da0, w_ff1[0], dy2, y1, ln1_g, 0,
                                                                 proj=[w["woa"], w["wob"]], deps=[token])
    g0["woa"], g0["wob"] = _out_weight_grads(o_att, b_out, dy1_bf)
    dzs, gs["sgu_w"], gs["sgu_ln_g"], gs["sgu_ln_b"], gs["sgu_b"] = _sgu_bwd(zs, dcat, sp["sgu_ln_g"], sp["sgu_ln_b"], sgu_w, bias_full)
    dq, dk, dv = _attn_bwd(q, k, v, o_att, lse, dcat)
    dzm, g0["wq"], g0["wk"], g0["wv"], gs["mla_gq"], gs["mla_gkv"] = _mla_back(zm, cqn, ckvn, tabs, gq, gkv, w["wq"], w["wk"], w["wv"],
                                                                                 dq, dk, dv)
    token = ex.small_start(gs)
    dx, g0["wm"], g0["ws"] = _in_back(x, dzm, dzs, dy1, w["wm"], w["ws"], deps=[token])

    return sq_err, dx, _unprep_grads(g0), gs


def _me():
    return lax.axis_index("x"), lax.axis_index("y"), lax.axis_index("c")


ANY_SPEC = pl.BlockSpec(memory_space=pl.ANY)
HBM_SPEC = pl.BlockSpec(memory_space=pltpu.HBM)
SEM_SPEC = pl.BlockSpec(memory_space=pltpu.SEMAPHORE)
EFFECT = pltpu.SideEffectType.DATAFLOW_SIDE_EFFECTING


def _split_start(name, srcs, lands, n_sems, make_copies, after=()):
    n, m, k = len(srcs), len(lands), len(after)

    def body(*refs):
        for cp in make_copies(refs[:n], refs[n:n + m], refs[n + m + k], refs[n + m + k + 1]):
            cp.start()
        refs[-1][...] = jnp.zeros(refs[-1].shape, F32)

    out_shape = (pltpu.SemaphoreType.DMA((n_sems,)), pltpu.SemaphoreType.DMA((n_sems,)),
                 *[pltpu.HBM(a.shape, a.dtype) for a in (*srcs, *lands)], jax.ShapeDtypeStruct((8, 128), F32))
    res = pl.pallas_call(
        body, name=name, out_shape=out_shape, in_specs=[HBM_SPEC] * (n + m) + [ANY_SPEC] * k,
        out_specs=(SEM_SPEC, SEM_SPEC, *[HBM_SPEC] * (n + m), pl.BlockSpec(memory_space=pltpu.VMEM)),
        input_output_aliases={i: 2 + i for i in range(n + m)},
        compiler_params=pltpu.CompilerParams(has_side_effects=EFFECT),
    )(*[_hbm(a) for a in (*srcs, *lands)], *after)
    return res[0], res[1], list(res[2:2 + n]), list(res[2 + n:2 + n + m]), res[-1]


def _split_wait(name, send_sems, recv_sems, srcs, lands, after, make_copies):
    n, m = len(srcs), len(lands)

    def body(*refs):
        for cp in make_copies(refs[:n], refs[n:n + m], refs[n + m], refs[n + m + 1]):
            cp.wait_send()
            cp.wait_recv()

    res = pl.pallas_call(
        body, name=name, out_shape=tuple(pltpu.HBM(a.shape, a.dtype) for a in (*srcs, *lands)),
        in_specs=[HBM_SPEC] * (n + m) + [SEM_SPEC, SEM_SPEC] + [ANY_SPEC] * len(after), out_specs=tuple([HBM_SPEC] * (n + m)),
        input_output_aliases={i: i for i in range(n + m)},
        compiler_params=pltpu.CompilerParams(has_side_effects=EFFECT),
    )(*srcs, *lands, send_sems, recv_sems, *after)
    return list(res[:n]), list(res[n:])


def _place_own(shards, dev):
    n = len(shards)

    def kern(dev_ref, *refs):
        for x_ref, o_ref in zip(refs[:n], refs[n:]):
            o_ref[...] = x_ref[...].astype(o_ref.dtype)

    blocks = [(None, *a.shape[1:]) for a, _, _ in shards]
    nbytes = sum(_nbytes(b, a.dtype) + _nbytes(b, dt) for b, (a, _, dt) in zip(blocks, shards))
    return pl.pallas_call(
        kern, name="weights_place_own", out_shape=[pltpu.HBM((N_DEV, *a.shape[1:]), dt) for a, _, dt in shards],
        grid_spec=pltpu.PrefetchScalarGridSpec(
            num_scalar_prefetch=1, grid=(1,),
            in_specs=[pl.BlockSpec(b, functools.partial(lambda i, dev, l: (l, 0, 0), l=l)) for b, (_, l, _) in zip(blocks, shards)],
            out_specs=[pl.BlockSpec(b, lambda i, dev: (dev[0], 0, 0)) for b in blocks]),
        compiler_params=pltpu.CompilerParams(dimension_semantics=("arbitrary",), vmem_limit_bytes=_vmem(nbytes)),
    )(dev, *[_hbm(a) for a, _, _ in shards])


def _ag_first_copies(src_refs, out_refs, send_sems, recv_sems):
    x, y, c = _me()
    targets = [(x, y, 1 - c), (1 - x, y, c), (x, 1 - y, c), (1 - x, 1 - y, c)]
    return [pltpu.make_async_remote_copy(
        src_ref=out_refs[op].at[4 * x + 2 * y + c], dst_ref=out_refs[op].at[4 * x + 2 * y + c], send_sem=send_sems.at[4 * op + k],
        recv_sem=recv_sems.at[4 * op + k], device_id=to, device_id_type=MESH)
        for op in range(len(out_refs)) for k, to in enumerate(targets)]


def _ag_second_copies(src_refs, out_refs, send_sems, recv_sems):
    x, y, c = _me()
    chips = [(1 - x, y), (x, 1 - y), (1 - x, 1 - y)]
    return [pltpu.make_async_remote_copy(
        src_ref=out_refs[op].at[4 * cx + 2 * cy + c], dst_ref=out_refs[op].at[4 * cx + 2 * cy + c],
        send_sem=send_sems.at[3 * op + j], recv_sem=recv_sems.at[3 * op + j], device_id=(x, y, 1 - c), device_id_type=MESH)
        for op in range(len(out_refs)) for j, (cx, cy) in enumerate(chips)]


def _rs_sibling_copies(g_refs, out_refs, send_sems, recv_sems):
    x, y, c = _me()
    return [pltpu.make_async_remote_copy(
        src_ref=g_refs[op].at[k, 1 - c], dst_ref=out_refs[op].at[k], send_sem=send_sems.at[4 * op + k],
        recv_sem=recv_sems.at[4 * op + k], device_id=(x, y, 1 - c), device_id_type=MESH)
        for op in range(len(g_refs)) for k in range(4)]


def _rs_direct_copies(g_refs, land_refs, send_sems, recv_sems):
    x, y, c = _me()
    n = len(g_refs) // 2
    chips = [(1 - x, y), (x, 1 - y), (1 - x, 1 - y)]
    copies = []
    for op in range(n):
        g32, g16, from_sib, from_others = g_refs[op], g_refs[n + op], land_refs[op], land_refs[n + op]
        copies.append(pltpu.make_async_remote_copy(
            src_ref=g32.at[2 * x + y, 1 - c], dst_ref=from_sib, send_sem=send_sems.at[7 * op], recv_sem=recv_sems.at[7 * op],
            device_id=(x, y, 1 - c), device_id_type=MESH))
        for j, (cx, cy) in enumerate(chips):
            for s, cc in enumerate((c, 1 - c)):
                copies.append(pltpu.make_async_remote_copy(
                    src_ref=g16.at[2 * cx + cy, cc], dst_ref=from_others.at[2 * j + s], send_sem=send_sems.at[7 * op + 1 + 2 * j + s],
                    recv_sem=recv_sems.at[7 * op + 1 + 2 * j + s], device_id=(cx, cy, cc), device_id_type=MESH))
    return copies


def _rs_chip_copies(p_refs, out_refs, send_sems, recv_sems):
    x, y, c = _me()
    chips = [(1 - x, y), (x, 1 - y), (1 - x, 1 - y)]
    return [pltpu.make_async_remote_copy(
        src_ref=p_refs[op].at[2 * cx + cy], dst_ref=out_refs[op].at[j], send_sem=send_sems.at[3 * op + j],
        recv_sem=recv_sems.at[3 * op + j], device_id=(cx, cy, c), device_id_type=MESH)
        for op in range(len(p_refs)) for j, (cx, cy) in enumerate(chips)]


def _all_gather(placed):
    n = len(placed)

    def kern(*refs):
        in_refs, out_refs, (send_sems, recv_sems) = refs[:n], refs[n:2 * n], refs[2 * n:]
        x, y, c = _me()
        me, sibling = (x, y, c), (x, y, 1 - c)
        chips = [(1 - x, y), (x, 1 - y), (1 - x, 1 - y)]

        def copy(op, k, block, to, own=False):
            idx = 4 * block[0] + 2 * block[1] + block[2]
            return pltpu.make_async_remote_copy(
                src_ref=(in_refs if own else out_refs)[op].at[idx], dst_ref=out_refs[op].at[idx], send_sem=send_sems.at[7 * op + k],
                recv_sem=recv_sems.at[7 * op + k], device_id=to, device_id_type=MESH)

        first = []
        for op in range(n):
            first.append(copy(op, 0, me, sibling, own=True))
            first += [copy(op, 1 + j, me, (*chip, c), own=True) for j, chip in enumerate(chips)]
        for cp in first:
            cp.start()
        passed = []
        for j, chip in enumerate(chips):
            for op in range(n):
                copy(op, 1 + j, (*chip, c), me).wait_recv()
                passed.append(copy(op, 4 + j, (*chip, c), sibling))
                passed[-1].start()
        for op in range(n):
            copy(op, 0, sibling, me).wait_recv()
            for j, chip in enumerate(chips):
                copy(op, 4 + j, (*chip, 1 - c), me).wait_recv()
        for cp in first + passed:
            cp.wait_send()

    return pl.pallas_call(
        kern, name="weights_all_gather", out_shape=[pltpu.HBM(g.shape, g.dtype) for g in placed],
        in_specs=[ANY_SPEC] * n, out_specs=[ANY_SPEC] * n, input_output_aliases={i: i for i in range(n)},
        scratch_shapes=[pltpu.SemaphoreType.DMA((7 * n,)), pltpu.SemaphoreType.DMA((7 * n,))],
    )(*[_hbm(a) for a in placed])


def _row_tile(r, w, n_blocks):
    tr = r
    while tr > 8 and 2 * n_blocks * tr * w * 4 > 24 * 2**20:
        tr //= 2
    return tr


def _chip_sum(name, g, from_sibling, core):
    _, _, R, W = g.shape
    tr = _row_tile(R, W, 3)

    def kern(core_ref, g_ref, s_ref, o_ref):
        o_ref[...] = (g_ref[...] + s_ref[...]).astype(BF16)

    return pl.pallas_call(
        kern, name=name, out_shape=pltpu.HBM((4, R, W), BF16),
        grid_spec=pltpu.PrefetchScalarGridSpec(
            num_scalar_prefetch=1, grid=(4, R // tr),
            in_specs=[pl.BlockSpec((None, None, tr, W), lambda k, i, core: (k, core[0], i, 0)),
                      pl.BlockSpec((None, tr, W), lambda k, i, core: (k, i, 0))],
            out_specs=pl.BlockSpec((None, tr, W), lambda k, i, core: (k, i, 0))),
        compiler_params=pltpu.CompilerParams(dimension_semantics=("parallel", "parallel"), vmem_limit_bytes=_vmem(3 * tr * W * 4)),
    )(core, _hbm(g), _hbm(from_sibling))


def _adamw(w, g, m, v):
    m = ADAM_B1 * m + (1.0 - ADAM_B1) * g
    v = ADAM_B2 * v + (1.0 - ADAM_B2) * (g * g)
    m_hat = m / (1.0 - ADAM_B1 ** ADAM_STEP)
    v_hat = v / (1.0 - ADAM_B2 ** ADAM_STEP)
    return -ADAM_LR * (m_hat / (jnp.sqrt(v_hat) + ADAM_EPS) + ADAM_WD * w), m, v


def _finish_sharded(name, layers, w, m, v, where, deps=()):
    nl, R, W = w.shape
    n_other = layers[0][2].shape[0]
    tr = _row_tile(R, W, (8 + n_other) * nl)
    deps = _deps(deps)

    def kern(where_ref, *refs):
        w_ref, m_ref, v_ref = refs[3 * nl:3 * nl + 3]
        go_ref, d_ref, mo_ref, vo_ref = refs[3 * nl + 3 + len(deps):]
        for l in range(nl):
            g_ref, s_ref, c_ref = refs[3 * l:3 * l + 3]
            grad = g_ref[...] + s_ref[...]
            for j in range(n_other):
                grad = grad + c_ref[j].astype(F32)
            go_ref[l] = grad
            d_ref[l], mo_ref[l], vo_ref[l] = _adamw(w_ref[l], grad, m_ref[l], v_ref[l])

    row = pl.BlockSpec((nl, tr, W), lambda i, wh: (0, i, 0))
    in_specs, args = [], []
    for g, s, c in layers:
        sib = (pl.BlockSpec((None, tr, W), lambda i, wh: (wh[0], i, 0)) if s.ndim == 3 else pl.BlockSpec((tr, W), lambda i, wh: (i, 0)))
        in_specs += [pl.BlockSpec((None, None, tr, W), lambda i, wh: (wh[0], wh[1], i, 0)), sib,
                     pl.BlockSpec((n_other, tr, W), lambda i, wh: (0, i, 0))]
        args += [g, s, c]
    return pl.pallas_call(
        kern, name=name, out_shape=[pltpu.HBM((nl, R, W), F32)] * 4,
        grid_spec=pltpu.PrefetchScalarGridSpec(num_scalar_prefetch=1, grid=(R // tr,),
                                               in_specs=in_specs + [row, row, row] + [ANY_SPEC] * len(deps),
                                               out_specs=[row, row, row, row]),
        compiler_params=pltpu.CompilerParams(dimension_semantics=("parallel",),
                                             vmem_limit_bytes=_vmem(nl * (8 + n_other) * tr * W * 4)),
    )(where, *[_hbm(a) for a in (*args, w, m, v)], *deps)


SMALL_PLACE = (("mla_gq", 0, 0, 1, 256), ("mla_gkv", 0, 256, 1, 256), ("sgu_ln_g", 0, 512, 1, 512), ("sgu_ln_b", 1, 0, 1, 512),
               ("hg_lb", 2, 0, 2, 1024), ("ln1_g", 4, 0, 2, 1024), ("ln1_b", 6, 0, 2, 1024), ("sgu_b", 8, 0, 4, 128),
               ("ln2_g", 12, 0, 2, 1024), ("ln2_b", 14, 0, 2, 1024), ("hg_gnorm", 16, 0, 1, 1024))
SMALL_BUF_ROWS = 24
LOSS_ROW = 17


def _small_pack(gs, dev):
    pieces = [(gs["mla_gq"], 0, 0), (gs["mla_gkv"], 0, 256), (gs["sgu_ln_g"], 0, 512), (gs["sgu_ln_b"], 1, 0), (gs["hg_lb"], 2, 0),
              (gs["ln1_g0"], 4, 0), (gs["ln1_g1"], 5, 0), (gs["ln1_b0"], 6, 0), (gs["ln1_b1"], 7, 0), (gs["sgu_b"], 8, 0),
              (gs["ln2_g0"], 12, 0), (gs["ln2_g1"], 13, 0), (gs["ln2_b0"], 14, 0), (gs["ln2_b1"], 15, 0), (gs["hg_gnorm"], 16, 0),
              (gs["sq_err"], LOSS_ROW, 0)]
    n_p = len(pieces)

    def kern(dev_ref, *refs):
        a_ref, b_ref = refs[n_p + 1], refs[n_p + 2]
        a_ref[...] = jnp.zeros(a_ref.shape, F32)
        for ref, (_, r, l0) in zip(refs[:n_p], pieces):
            a_ref[r:r + ref.shape[0], l0:l0 + ref.shape[1]] = ref[...]
        b_ref[...] = refs[n_p][...]

    whole = lambda a: pl.BlockSpec(a.shape, functools.partial(lambda i, dev, nd: (0,) * nd, nd=a.ndim))
    return pl.pallas_call(
        kern, name="small_grads_pack",
        out_shape=[pltpu.HBM((N_DEV, SMALL_BUF_ROWS, D_MODEL), F32), pltpu.HBM((N_DEV, SGU_G, 128, 128), F32)],
        grid_spec=pltpu.PrefetchScalarGridSpec(
            num_scalar_prefetch=1, grid=(1,), in_specs=[whole(p[0]) for p in pieces] + [whole(gs["sgu_w"])],
            out_specs=[pl.BlockSpec((None, SMALL_BUF_ROWS, D_MODEL), lambda i, dev: (dev[0], 0, 0)),
                       pl.BlockSpec((None, SGU_G, 128, 128), lambda i, dev: (dev[0], 0, 0, 0))]),
    )(dev, *[p[0] for p in pieces], gs["sgu_w"])


def _small_copies(src_refs, land_refs, send_sems, recv_sems):
    px, py, pc = _me()
    me = 4 * px + 2 * py + pc
    return [pltpu.make_async_remote_copy(
        src_ref=land_refs[k].at[me], dst_ref=land_refs[k].at[me], send_sem=send_sems.at[2 * (r - 1) + k],
        recv_sem=recv_sems.at[2 * (r - 1) + k], device_id=(px ^ (r >> 2), py ^ ((r >> 1) & 1), pc ^ (r & 1)), device_id_type=MESH)
        for r in range(1, N_DEV) for k in range(2)]


def _small_adamw(slots_a, slots_b, given):
    names = [p[0] for p in SMALL_PLACE] + ["sgu_w"]
    n_names = len(names)
    wmv = [given[pre + name] for name in names for pre in ("", "m_", "v_")]
    vmem = pl.BlockSpec(memory_space=pltpu.VMEM)

    def kern(*refs):
        sum_a, sum_b = refs[0][0], refs[1][0]
        for d in range(1, N_DEV):
            sum_a, sum_b = sum_a + refs[0][d], sum_b + refs[1][d]
        wmv_refs, out_refs = refs[2:2 + 3 * n_names], refs[2 + 3 * n_names:]
        px, py, pc = _me()
        me = 4 * px + 2 * py + pc

        def own_block(full):
            acc = full[:, 0:128]
            for b in range(1, N_DEV):
                acc = jnp.where(me == b, full[:, b * 128:(b + 1) * 128], acc)
            return acc

        for idx, name in enumerate(names):
            w_ref, m_ref, v_ref = wmv_refs[3 * idx:3 * idx + 3]
            if name == "sgu_w":
                grad = sum_b[None]
            else:
                _, r, l0, nr, nl = SMALL_PLACE[idx]
                grad = sum_a[r:r + nr, l0:l0 + nl]
                if name == "hg_gnorm":
                    grad = own_block(grad)
                if name == "sgu_b":
                    grad = grad[None]
            res = (grad, *_adamw(w_ref[...], grad, m_ref[...], v_ref[...]))
            for o_ref, val in zip(out_refs[4 * idx:4 * idx + 4], res):
                o_ref[...] = val
        out_refs[4 * n_names][...] = (0.5 / D_MODEL) * jnp.sum(sum_a[LOSS_ROW:LOSS_ROW + 1, :], axis=1, keepdims=True)

    out_shape = [jax.ShapeDtypeStruct(given[name].shape, F32) for name in names for _ in range(4)]
    out_shape.append(jax.ShapeDtypeStruct((1, 1), F32))
    res = pl.pallas_call(
        kern, name="small_adamw", out_shape=out_shape, in_specs=[vmem] * (2 + len(wmv)), out_specs=[vmem] * len(out_shape),
    )(slots_a, slots_b, *wmv)
    out = {name: res[4 * idx:4 * idx + 4] for idx, name in enumerate(names)}
    out["loss"] = res[-1].reshape(())
    return out


class _Exchange:
    def __init__(self, given):
        self.given = given
        px, py, pc = _me()
        self.core = pc.reshape(1).astype(jnp.int32)
        self.dev = (4 * px + 2 * py + pc).reshape(1).astype(jnp.int32)
        self.where = jnp.stack([2 * px + py, pc]).astype(jnp.int32)
        self.state, self.layers = {}, {}

    def start_weights(self, lands, after):
        self.weights = _split_start("weights_first_start", [], lands, 4 * len(lands), _ag_first_copies, after=after)
        self.first_token = self.weights[4]

    def weights_forward(self, after):
        send_sems, recv_sems, shards, lands, _ = self.weights
        _, lands = _split_wait("weights_first_wait", send_sems, recv_sems, shards, lands, after, _ag_first_copies)
        self.weights = _split_start("weights_second_start", [], lands, 3 * len(lands), _ag_second_copies)
        return self.weights[4]

    def weights_ready(self, after):
        send_sems, recv_sems, shards, lands, _ = self.weights
        _, got = _split_wait("weights_second_wait", send_sems, recv_sems, shards, lands, after, _ag_second_copies)
        return dict(w_in_o=got[0], w_out_o=got[1], w_ff1=[got[2], got[3]], w_ff2=[got[4], got[5]])

    def small_start(self, gs):
        self.small = _split_start("small_grads_start", [], _small_pack(gs, self.dev), 14, _small_copies)
        return self.small[4]

    def small_finish(self, after):
        send_sems, recv_sems, _, lands, _ = self.small
        _, lands = _split_wait("small_grads_wait", send_sems, recv_sems, [], lands, after, _small_copies)
        return _small_adamw(lands[0], lands[1], self.given)

    def direct_start(self, tag, grads):
        f32 = [g[0].reshape(4, 2, *g[0].shape[1:]) for g in grads]
        bf16 = [g[1].reshape(4, 2, *g[1].shape[1:]) for g in grads]
        lands = [lax.empty(b.shape[2:], F32) for b in f32] + [lax.empty((6, *b.shape[2:]), BF16) for b in f32]
        self.state[tag] = _split_start(f"grads_{tag}_start", f32 + bf16, lands, 7 * len(grads), _rs_direct_copies)
        return self.state[tag][4]

    def direct_end(self, tag, after):
        send_sems, recv_sems, srcs, lands, _ = self.state[tag]
        srcs, lands = _split_wait(f"grads_{tag}_wait", send_sems, recv_sems, srcs, lands, after, _rs_direct_copies)
        n = len(lands) // 2
        self.layers[tag] = list(zip(srcs[:n], lands[:n], lands[n:]))

    def grads_start(self, tag, grads):
        blocks = [g.reshape(4, 2, *g.shape[1:]) for g in grads]
        lands = [lax.empty((4, *b.shape[2:]), F32) for b in blocks]
        self.state[tag] = _split_start(f"grads_{tag}_sibling_start", blocks, lands, 4 * len(blocks), _rs_sibling_copies)
        return self.state[tag][4]

    def grads_middle(self, tag, after):
        send_sems, recv_sems, blocks, lands, _ = self.state[tag]
        blocks, from_sibling = _split_wait(f"grads_{tag}_sibling_wait", send_sems, recv_sems, blocks, lands, [after], _rs_sibling_copies)
        sums = [_chip_sum(f"grads_{tag}_chip_sum_{k}", b, s, self.core) for k, (b, s) in enumerate(zip(blocks, from_sibling))]
        lands = [lax.empty((3, *p.shape[1:]), BF16) for p in sums]
        self.state[tag] = (blocks, from_sibling, _split_start(f"grads_{tag}_chips_start", sums, lands, 3 * len(sums), _rs_chip_copies))
        return self.state[tag][2][4]

    def grads_end(self, tag, after):
        blocks, from_sibling, (send_sems, recv_sems, sums, lands, _) = self.state[tag]
        after = list(after) if isinstance(after, (list, tuple)) else [after]
        _, from_chips = _split_wait(f"grads_{tag}_chips_wait", send_sems, recv_sems, sums, lands, after, _rs_chip_copies)
        self.layers[tag] = list(zip(blocks, from_sibling, from_chips))


def kernel(x, positions, w_in_e, mla_gq, mla_gkv, w_qb, w_kvb, sgu_ln_g, sgu_ln_b, sgu_w, sgu_b, w_out_e, w_in_o, hg_lb, hg_gnorm, w_out_o, ln1_g, ln1_b, w_ff1, w_ff2, ln2_g, ln2_b, loss_target, m_w_in_e, m_mla_gq, m_mla_gkv, m_w_qb, m_w_kvb, m_sgu_ln_g, m_sgu_ln_b, m_sgu_w, m_sgu_b, m_w_out_e, m_w_in_o, m_hg_lb, m_hg_gnorm, m_w_out_o, m_ln1_g, m_ln1_b, m_w_ff1, m_w_ff2, m_ln2_g, m_ln2_b, v_w_in_e, v_mla_gq, v_mla_gkv, v_w_qb, v_w_kvb, v_sgu_ln_g, v_sgu_ln_b, v_sgu_w, v_sgu_b, v_w_out_e, v_w_in_o, v_hg_lb, v_hg_gnorm, v_w_out_o, v_ln1_g, v_ln1_b, v_w_ff1, v_w_ff2, v_ln2_g, v_ln2_b):
    given = dict(locals())
    ex = _Exchange(given)

    names = ["w_in_e", "w_qb", "w_kvb", "w_out_e"]
    placed = _place_own([(given[n], 0, BF16) for n in names] + [(hg_gnorm.reshape(1, 1, D_MODEL // N_DEV), 0, F32)]
                        + [(w_in_o, 0, BF16), (w_out_o, 0, BF16), (w_ff1, 0, BF16), (w_ff1, 1, BF16), (w_ff2, 0, BF16), (w_ff2, 1, BF16)],
                        ex.dev)
    got = _all_gather(placed[:5])
    ex.start_weights(placed[5:], after=[got[0]])
    gw = dict(zip(names, got[:4]))
    small_names = ["mla_gq", "mla_gkv", "sgu_ln_g", "sgu_ln_b", "sgu_w", "sgu_b", "hg_lb", "ln1_g", "ln1_b", "ln2_g", "ln2_b"]
    sp = {n: given[n] for n in small_names}
    sp["hg_gnorm"] = got[4].reshape(1, D_MODEL)

    _, dx, grads, gs = _local_step(x[0], positions[0], loss_target[0], gw, sp, ex)

    def finish(n, layers, deps=()):
        return _finish_sharded(f"finish_{n}", layers, given[n], given["m_" + n], given["v_" + n], ex.where, deps=deps)

    ex.direct_end("l1", after=[dx])
    ex.direct_end("l0m", after=[dx])
    l1, l0m = ex.layers["l1"], ex.layers["l0m"]
    results = {}
    token = ex.grads_start("l0s", [grads[n] for n in names])
    results["w_ff1"] = finish("w_ff1", [l0m[0], l1[0]], deps=[token])
    token = ex.grads_middle("l0s", after=results["w_ff1"][0])
    results["w_ff2"] = finish("w_ff2", [l0m[1], l1[1]], deps=[token])
    results["w_in_o"] = finish("w_in_o", [l1[2]], deps=[token])
    results["w_out_o"] = finish("w_out_o", [l1[3]], deps=[token])
    results.update(ex.small_finish(after=[results["w_in_o"][0]]))
    ex.grads_end("l0s", after=[results[n][0] for n in ("mla_gq", "w_ff2", "w_in_o", "w_out_o")])
    for n, layer in zip(names, ex.layers["l0s"]):
        results[n] = finish(n, [layer])

    order = ["w_in_e", "mla_gq", "mla_gkv", "w_qb", "w_kvb", "sgu_ln_g", "sgu_ln_b", "sgu_w", "sgu_b", "w_out_e", "w_in_o",
             "hg_lb", "hg_gnorm", "w_out_o", "ln1_g", "ln1_b", "w_ff1", "w_ff2", "ln2_g", "ln2_b"]
    return (results["loss"], dx[None], *[results[name][kind] for kind in range(4) for name in order])
```

```python
import functools
import math

import jax
import jax.numpy as jnp
import numpy as np
from jax import lax
from jax.experimental import pallas as pl
from jax.experimental.pallas import tpu as pltpu

F32 = jnp.float32
BF16 = jnp.bfloat16
MESH = pl.DeviceIdType.MESH
HIGHEST = lax.Precision.HIGHEST

D_MODEL = 1024
D_FF = 4096
N_DEV = 8
HEADS = 8
HEAD_W = 128
MLA_NOPE = 64
MLA_ROPE = 32
MLA_V = 64
MLA_LORA = 256
MLA_SCALE = (MLA_NOPE + MLA_ROPE) ** -0.5
ROPE_BASE = 10000.0
SGU_DIM = 512
SGU_G = 4
SGU_CHUNK = 128
HG_CHUNK = 64
HG_CHUNKS_PER_STEP = 8
ALPHA = (2 * 2) ** 0.25
EPS = 1e-5
ADAM_LR, ADAM_B1, ADAM_B2, ADAM_EPS, ADAM_WD, ADAM_STEP = 0.001, 0.9, 0.999, 1e-08, 0.01, 10

VMEM_CAP_V7X = 56 * 2**20
VMEM_SLACK = 12 * 2**20
TM = 512
TN = 512


def _vmem(block_bytes):
    return int(min(VMEM_CAP_V7X, 2 * block_bytes + VMEM_SLACK))


def _hbm(a):
    return pltpu.with_memory_space_constraint(a, pltpu.HBM)


def _nbytes(shape, dtype):
    return int(np.prod([d for d in shape if d is not None])) * jnp.dtype(dtype).itemsize


def _sig(x):
    return 1.0 / (1.0 + jnp.exp(-x))


def _gelu(x):
    c = math.sqrt(2.0 / math.pi)
    t = jnp.tanh(c * (x + 0.044715 * x * x * x))
    return 0.5 * x * (1.0 + t), t


def _gelu_grad(x, t):
    c = math.sqrt(2.0 / math.pi)
    return 0.5 * (1.0 + t) + 0.5 * x * (1.0 - t * t) * c * (1.0 + 3 * 0.044715 * x * x)


def _dot(a, b, dims, precision=None):
    return lax.dot_general(a, b, (dims, ((), ())), preferred_element_type=F32, precision=precision)


NN = ((1,), (0,))
NT = ((1,), (1,))
TN_ = ((0,), (0,))


def _deps(deps):
    return [d for d in deps if d is not None]


def _tiled(name, grid, ins, outs, compute, direct=False, deps=()):
    n_in, deps = len(ins), _deps(deps)
    n_skip = n_in + len(deps)

    def kern(*refs):
        if direct:
            compute(refs[:n_in], refs[n_skip:])
            return
        for o_ref, r in zip(refs[n_skip:], compute(*refs[:n_in])):
            o_ref[...] = r.astype(o_ref.dtype).reshape(o_ref.shape)

    swap = lambda f: (lambda j, i: f(i, j))
    nbytes = sum(_nbytes(blk, a.dtype) for a, blk, _ in ins) + sum(_nbytes(blk, dt) + _nbytes(blk, F32) for _, dt, blk, _ in outs)
    res = pl.pallas_call(
        kern, name=name, grid=grid,
        in_specs=[pl.BlockSpec(blk, swap(f), pipeline_mode=pl.Buffered(1) if tuple(blk) == tuple(a.shape) else None)
                  for a, blk, f in ins] + [ANY_SPEC] * len(deps),
        out_specs=[pl.BlockSpec(blk, swap(f)) for _, _, blk, f in outs],
        out_shape=[pltpu.HBM(shape, dt) for shape, dt, _, _ in outs],
        compiler_params=pltpu.CompilerParams(dimension_semantics=("parallel", "parallel"), vmem_limit_bytes=_vmem(nbytes)),
    )(*[_hbm(a) for a, _, _ in ins], *deps)
    return res if len(res) > 1 else res[0]


def _rb(a, tm, w=None, cb=0):
    return (a, (tm, a.shape[1] if w is None else w), lambda i, j: (i, cb))


def _cw(b, tn):
    return (b, (b.shape[0], tn), lambda i, j: (0, j))


def _tl(a, tm):
    return (a, (a.shape[0], tm), lambda i, j: (0, i))


def _out(m, n, dtype, tm, tn):
    return ((m, n), dtype, (tm, tn), lambda i, j: (i, j))


def _out_dev(k, n, tm, dtype=F32):
    return ((N_DEV, k, n), dtype, (None, tm, n), lambda i, j: (j, i, 0))


def _twice(acc):
    return acc, acc


def _mmc(dims, n_pairs=1, epilogue=None):
    def compute(*refs):
        acc = None
        for k in range(n_pairs):
            d = _dot(refs[2 * k][...].astype(BF16), refs[2 * k + 1][...].astype(BF16), dims)
            acc = d if acc is None else acc + d
        ext = [r[...] for r in refs[2 * n_pairs:]]
        return epilogue(acc, *ext) if epilogue is not None else (acc,)

    return compute


def _res(w):
    return (w, w.shape, functools.partial(lambda i, j, nd: (0,) * nd, nd=w.ndim))


def _mmc_blocks(nblk, dims, rhs_block, epilogue=None):
    def compute(in_refs, out_refs):
        a = in_refs[0][...].astype(BF16)
        for d in range(nblk):
            acc = _dot(a, rhs_block(in_refs[1], d).astype(BF16), dims)
            n = acc.shape[1]
            ext = [r[:, d * n:(d + 1) * n] for r in in_refs[2:]]
            res = epilogue(acc, *ext) if epilogue is not None else (acc,)
            for o_ref, r in zip(out_refs, res):
                o_ref[:, d * n:(d + 1) * n] = r.astype(o_ref.dtype)

    return compute


def _rowwise(name, body, rows, consts, out_rows, out_accs=(), tr=512, deps=()):
    T = rows[0][0].shape[0]
    tr = min(tr, T)
    deps = _deps(deps)
    nr, ncn, no, nd = len(rows), len(consts), len(out_rows), len(deps)

    def kern(*refs):
        accs = refs[nr + ncn + nd + no:]
        if accs:
            @pl.when(pl.program_id(0) == 0)
            def _():
                for a in accs:
                    a[...] = jnp.zeros(a.shape, a.dtype)
        body(refs[:nr], refs[nr:nr + ncn], refs[nr + ncn + nd:nr + ncn + nd + no], accs)

    in_specs = [pl.BlockSpec((tr, w), functools.partial(lambda i, cb: (i, cb), cb=cb)) for _, w, cb in rows]
    in_specs += [pl.BlockSpec(c.shape, functools.partial(lambda i, nd: (0,) * nd, nd=c.ndim), pipeline_mode=pl.Buffered(1))
                 for c in consts]
    in_specs += [ANY_SPEC] * nd
    out_specs = [pl.BlockSpec((tr, w), lambda i: (i, 0)) for w, _ in out_rows]
    out_specs += [pl.BlockSpec(s, functools.partial(lambda i, nd: (0,) * nd, nd=len(s))) for s, _ in out_accs]
    out_shape = [pltpu.HBM((T, w), dt) for w, dt in out_rows]
    out_shape += [pltpu.HBM(s, dt) for s, dt in out_accs]
    nbytes = sum(_nbytes((tr, w), a.dtype) for a, w, _ in rows) + sum(_nbytes(c.shape, c.dtype) for c in consts)
    nbytes += sum(_nbytes((tr, w), dt) for w, dt in out_rows) + sum(_nbytes(s, dt) for s, dt in out_accs)
    res = pl.pallas_call(
        kern, name=name, grid=(T // tr,), in_specs=in_specs, out_specs=out_specs, out_shape=out_shape,
        compiler_params=pltpu.CompilerParams(dimension_semantics=("arbitrary",), vmem_limit_bytes=_vmem(nbytes)),
    )(*[_hbm(a) for a, _, _ in rows], *[_hbm(c) for c in consts], *deps)
    return res if len(res) > 1 else res[0]


def _full(a):
    return (a, a.shape[1], 0)


def _ln_stats(y):
    mu = jnp.mean(y, axis=-1, keepdims=True)
    yc = y - mu
    r = lax.rsqrt(jnp.mean(yc * yc, axis=-1, keepdims=True) + EPS)
    return yc * r, r


def _row_halves(n):
    return [slice(0, n // 2), slice(n // 2, n)] if n >= 256 else [slice(0, n)]


def _ln_back(dh, xh, r, gain, dg_ref, db_ref):
    dg_ref[...] += jnp.sum(dh * xh, axis=0, keepdims=True)
    db_ref[...] += jnp.sum(dh, axis=0, keepdims=True)
    dx = dh * gain
    return r * (dx - jnp.mean(dx, axis=-1, keepdims=True) - xh * jnp.mean(dx * xh, axis=-1, keepdims=True))


def _residual(resid, prev):
    if prev is None:
        return resid
    g_ref, b_ref, layer = prev
    return _ln_stats(resid)[0] * g_ref[layer:layer + 1, :] + b_ref[layer:layer + 1, :]


def _proj_ln(name, acts, weights, resid, g, b, layer, prev=None, deps=()):
    n = len(acts)

    def body(rows, consts, outs, accs):
        acc = None
        for k in range(n):
            d = _dot(rows[k][...].astype(BF16), consts[k][...], NN)
            acc = d if acc is None else acc + d
        h_in = _residual(rows[n][...], None if prev is None else (consts[n + 2], consts[n + 3], prev[2]))
        y = ALPHA * h_in + acc
        xh, _ = _ln_stats(y)
        outs[0][...] = y
        outs[1][...] = (xh * consts[n][layer:layer + 1, :] + consts[n + 1][layer:layer + 1, :]).astype(BF16)

    consts = [*weights, g, b] + ([] if prev is None else [prev[0], prev[1]])
    return _rowwise(name, body, [_full(a) for a in acts] + [_full(resid)], consts, [(D_MODEL, F32), (D_MODEL, BF16)], tr=TM, deps=deps)


def _proj_ln_loss(name, act, w2, resid, g, b, layer, target, prev):
    def body(rows, consts, outs, accs):
        y = ALPHA * _residual(rows[1][...], (consts[3], consts[4], prev[2])) + _dot(rows[0][...], consts[0][...], NN)
        xh, r = _ln_stats(y)
        gain = consts[1][layer:layer + 1, :]
        err = xh * gain + consts[2][layer:layer + 1, :] - rows[2][...]
        accs[0][...] += jnp.sum(err * err, axis=0, keepdims=True)
        dy = _ln_back(err * (1.0 / D_MODEL), xh, r, gain, accs[1], accs[2])
        outs[0][...] = dy
        outs[1][...] = dy.astype(BF16)

    return _rowwise(name, body, [_full(act), _full(resid), _full(target)], [w2, g, b, prev[0], prev[1]],
                    [(D_MODEL, F32), (D_MODEL, BF16)], [((1, D_MODEL), F32)] * 3, tr=TM)


def _dh_ln_back(name, da, w, dy_next, y, g, layer, proj=(), deps=()):
    def body(rows, consts, outs, accs):
        n = consts[0].shape[2]
        for sl in _row_halves(rows[0].shape[0]):
            acc = ALPHA * rows[1][sl, :]
            for d in range(N_DEV):
                acc = acc + _dot(rows[0][sl, d * n:(d + 1) * n], consts[0][d], NT)
            xh, r = _ln_stats(rows[2][sl, :])
            dy = _ln_back(acc, xh, r, consts[1][layer:layer + 1, :], accs[0], accs[1])
            outs[0][sl, :] = dy
            dy_bf = dy.astype(BF16)
            outs[1][sl, :] = dy_bf
            off = 0
            for k, p in enumerate(proj):
                outs[2][sl, off:off + p.shape[0]] = _dot(dy_bf, consts[2 + k][...], NT).astype(BF16)
                off += p.shape[0]

    out_rows = [(D_MODEL, F32), (D_MODEL, BF16)] + ([(sum(p.shape[0] for p in proj), BF16)] if proj else [])
    return _rowwise(name, body, [_full(da), _full(dy_next), _full(y)], [w, g, *proj], out_rows,
                    [((1, D_MODEL), F32)] * 2, tr=TM, deps=deps)


def _relu2_epilogue(acc):
    a = jnp.maximum(acc, 0.0)
    return acc, a * a


def _mlp_up(tag, h_bf, w1):
    T = h_bf.shape[0]
    tm = min(TM, T)
    return _tiled(f"{tag}_ff1", (1, T // tm), [_rb(h_bf, tm), _res(w1)],
                  [_out(T, D_FF, BF16, tm, D_FF), _out(T, D_FF, BF16, tm, D_FF)],
                  _mmc_blocks(N_DEV, NN, lambda w, d: w[d], epilogue=_relu2_epilogue), direct=True)


def _mlp_bwd_w(tag, h_bf, a, act, dff_bf, w2, deps=()):
    T = h_bf.shape[0]
    tm = min(TM, T)
    da = _tiled(f"{tag}_dact", (1, T // tm), [_rb(dff_bf, tm), _res(w2), _rb(a, tm)], [_out(T, D_FF, BF16, tm, D_FF)],
                _mmc_blocks(N_DEV, NT, lambda w, d: w[d], epilogue=lambda acc, a_t: (acc * 2.0 * jnp.maximum(a_t.astype(F32), 0.0),)),
                direct=True, deps=deps)
    dw2 = _tiled(f"{tag}_dw2", (1, D_FF // TM), [_tl(act, TM), _res(dff_bf)],
                 [_out(D_FF, D_MODEL, F32, TM, D_MODEL), _out(D_FF, D_MODEL, BF16, TM, D_MODEL)], _mmc(TN_, epilogue=_twice))
    dw1 = _tiled(f"{tag}_dw1", (N_DEV, 1), [_res(h_bf), _cw(da, TN)],
                 [_out_dev(D_MODEL, TN, D_MODEL), _out_dev(D_MODEL, TN, D_MODEL, BF16)], _mmc(TN_, epilogue=_twice))
    return da, dw1, [a.reshape(N_DEV, D_FF // N_DEV, D_MODEL) for a in dw2]


def _rope_tables(positions_col, invf_lane):
    def body(rows, consts, outs, accs):
        ang = rows[0][...].astype(F32) * consts[0][...]
        c, s = jnp.cos(ang), jnp.sin(ang)
        lane = lax.broadcasted_iota(jnp.int32, ang.shape, 1)
        outs[0][...] = jnp.where(lane < 64, 1.0, jnp.where(lane < 96, c, 0.0))
        outs[1][...] = jnp.where((lane >= 64) & (lane < 80), -s, 0.0)
        outs[2][...] = jnp.where((lane >= 80) & (lane < 96), s, 0.0)

    return _rowwise("rope_tables", body, [_full(positions_col)], [invf_lane], [(HEAD_W, F32)] * 3)


def _rope(x, c, s1, s2):
    return x * c + pltpu.roll(x, 112, 1) * s1 + pltpu.roll(x, 16, 1) * s2


def _rope_t(dx, c, s1, s2):
    return dx * c + pltpu.roll(dx * s1, 16, 1) + pltpu.roll(dx * s2, 112, 1)


def _rms(c):
    r = lax.rsqrt(jnp.mean(c * c, axis=-1, keepdims=True) + EPS)
    return c * r, r


def _rope_heads(x, c, s1, s2, fn):
    return jnp.concatenate([fn(x[:, h * HEAD_W:(h + 1) * HEAD_W], c, s1, s2) for h in range(HEADS)], axis=1)


def _mla_in(x, wm, ws, tabs, gq, gkv, deps=()):
    def body(rows, consts, outs, accs):
        xb = rows[0][...].astype(BF16)
        zm = _dot(xb, consts[0][...], NN)
        outs[0][...] = zm
        outs[1][...] = _dot(xb, consts[1][...], NN)
        outs[2][...] = (_rms(zm[:, 0:256])[0] * consts[2][...]).astype(BF16)
        outs[3][...] = (_rms(zm[:, 256:512])[0] * consts[3][...]).astype(BF16)
        outs[4][...] = _rope(zm[:, 512:640], rows[1][...], rows[2][...], rows[3][...])

    return _rowwise("l0_in", body, [_full(x)] + [_full(t) for t in tabs], [wm, ws, gq, gkv],
                    [(640, F32), (1024, F32), (256, BF16), (256, BF16), (HEAD_W, F32)], deps=deps)


def _mla_qkv(cqn, ckvn, kr_rot, tabs, wq, wk, wv):
    def body(rows, consts, outs, accs):
        c, s1, s2 = rows[3][...], rows[4][...], rows[5][...]
        outs[0][...] = _rope_heads(_dot(rows[0][...], consts[0][...], NN), c, s1, s2, _rope).astype(BF16)
        outs[1][...] = (_dot(rows[1][...], consts[1][...], NN) + jnp.concatenate([rows[2][...]] * HEADS, axis=1)).astype(BF16)
        outs[2][...] = _dot(rows[1][...], consts[2][...], NN).astype(BF16)

    rows = [_full(cqn), _full(ckvn), _full(kr_rot)] + [_full(t) for t in tabs]
    return _rowwise("l0_qkv", body, rows, [wq, wk, wv], [(HEADS * HEAD_W, BF16)] * 3)


def _mla_back(zm, cqn, ckvn, tabs, gq, gkv, wq, wk, wv, dq, dk, dv):
    def body(rows, consts, outs, accs):
        c, s1, s2 = rows[4][...], rows[5][...], rows[6][...]
        dk_t, dv_bf = rows[8][...], rows[9][...].astype(BF16)
        dq_bf = _rope_heads(rows[7][...], c, s1, s2, _rope_t).astype(BF16)
        dk_bf = dk_t.astype(BF16)
        accs[0][...] += _dot(rows[2][...], dq_bf, TN_)
        accs[1][...] += _dot(rows[3][...], dk_bf, TN_)
        accs[2][...] += _dot(rows[3][...], dv_bf, TN_)
        dlat = [_dot(dq_bf, consts[2][...], NT), _dot(dk_bf, consts[3][...], NT) + _dot(dv_bf, consts[4][...], NT)]
        for k in range(2):
            ch, r = _rms(rows[k][...])
            accs[3 + k][...] += jnp.sum(dlat[k] * ch, axis=0, keepdims=True)
            dc = dlat[k] * consts[k][...]
            outs[0][:, 256 * k:256 * (k + 1)] = (r * (dc - ch * jnp.mean(dc * ch, axis=-1, keepdims=True))).astype(BF16)
        dks = dk_t[:, 0:HEAD_W]
        for h in range(1, HEADS):
            dks = dks + dk_t[:, h * HEAD_W:(h + 1) * HEAD_W]
        lane = lax.broadcasted_iota(jnp.int32, dks.shape, 1)
        dks = jnp.where((lane >= 64) & (lane < 96), dks, 0.0)
        outs[0][:, 512:640] = _rope_t(dks, c, s1, s2).astype(BF16)

    rows = [(zm, 256, 0), (zm, 256, 1), _full(cqn), _full(ckvn)] + [_full(t) for t in tabs] + [_full(dq), _full(dk), _full(dv)]
    wide = HEADS * HEAD_W
    return _rowwise("l0_mla_back", body, rows, [gq, gkv, wq, wk, wv], [(640, BF16)],
                    [((MLA_LORA, wide), F32)] * 3 + [((1, MLA_LORA), F32)] * 2, tr=256)


def _in_back(x, dzm, dzs, dy, wm, ws, deps=()):
    def body(rows, consts, outs, accs):
        dzm_t, dzs_t = rows[1][...], rows[2][...]
        outs[0][...] = _dot(dzm_t, consts[0][...], NT) + _dot(dzs_t, consts[1][...], NT) + ALPHA * rows[3][...]
        xb = rows[0][...].astype(BF16)
        accs[0][...] += _dot(xb, dzm_t, TN_)
        accs[1][...] += _dot(xb, dzs_t, TN_)

    return _rowwise("l0_in_back", body, [_full(x), _full(dzm), _full(dzs), _full(dy)], [wm, ws], [(D_MODEL, F32)],
                    [((D_MODEL, 640), F32), ((D_MODEL, 1024), F32)], deps=deps)


def _out_weight_grads(o_att, b_out, dy_bf):
    def body(rows, consts, outs, accs):
        d = rows[2][...]
        accs[0][...] += _dot(rows[0][...].astype(BF16), d, TN_)
        accs[1][...] += _dot(rows[1][...], d, TN_)

    return _rowwise("l0_dw_out", body, [_full(o_att), _full(b_out), _full(dy_bf)], [], [],
                    [((HEADS * HEAD_W, D_MODEL), F32), ((SGU_DIM, D_MODEL), F32)])


def _attn_block(T):
    return min(1024, T)


def _attn_fwd(q, k, v):
    T = q.shape[0]
    BQ = _attn_block(T)
    nq = T // BQ

    def kern(q_ref, k_ref, v_ref, o_ref, lse_ref):
        def step(i, j, carry, masked):
            m, l, acc = carry
            qb = q_ref[pl.ds(pl.multiple_of(i * BQ, BQ), BQ), :]
            kb = k_ref[pl.ds(pl.multiple_of(j * BQ, BQ), BQ), :]
            vb = v_ref[pl.ds(pl.multiple_of(j * BQ, BQ), BQ), :]
            s = _dot(qb, kb, NT) * MLA_SCALE
            if masked:
                row = lax.broadcasted_iota(jnp.int32, s.shape, 0)
                col = lax.broadcasted_iota(jnp.int32, s.shape, 1)
                s = jnp.where(col <= row, s, -1e30)
            m_new = jnp.maximum(m, jnp.max(s, axis=-1, keepdims=True))
            p = jnp.exp(s - m_new)
            a = jnp.exp(m - m_new)
            l = a * l + jnp.sum(p, axis=-1, keepdims=True)
            acc = a * acc + _dot(p.astype(BF16), vb, NN)
            return m_new, l, acc

        def qloop(i, _):
            init = (jnp.full((BQ, 1), -1e30, F32), jnp.zeros((BQ, 1), F32), jnp.zeros((BQ, HEAD_W), F32))
            carry = lax.fori_loop(0, i, lambda j, c: step(i, j, c, False), init)
            m, l, acc = step(i, i, carry, True)
            rows = pl.ds(pl.multiple_of(i * BQ, BQ), BQ)
            o_ref[rows, :] = (acc / l).astype(BF16)
            lse_ref[0, rows, :] = m + jnp.log(l)
            return 0

        lax.fori_loop(0, nq, qloop, 0)

    head = pl.BlockSpec((T, HEAD_W), lambda h: (0, h))
    nbytes = 3 * _nbytes((T, HEAD_W), BF16) + _nbytes((T, HEAD_W), F32) + _nbytes((T, 128), F32)
    return pl.pallas_call(
        kern, name="attn_fwd", grid=(HEADS,), in_specs=[head, head, head],
        out_specs=[head, pl.BlockSpec((1, T, 1), lambda h: (h, 0, 0))],
        out_shape=[pltpu.HBM((T, HEADS * HEAD_W), BF16), pltpu.HBM((HEADS, T, 1), F32)],
        compiler_params=pltpu.CompilerParams(dimension_semantics=("parallel",), vmem_limit_bytes=_vmem(nbytes)),
    )(_hbm(q), _hbm(k), _hbm(v))


def _attn_bwd(q, k, v, o, lse, dcat, deps=()):
    T = q.shape[0]
    BQ = _attn_block(T)
    nq = T // BQ
    deps = _deps(deps)

    def kern(q_ref, k_ref, v_ref, o_ref, lse_ref, do_ref, *rest):
        dq_ref, dk_ref, dv_ref, dd_ref = rest[len(deps):]
        dq_ref[...] = jnp.zeros(dq_ref.shape, F32)

        def dloop(i, _):
            rows = pl.ds(pl.multiple_of(i * BQ, BQ), BQ)
            dd_ref[rows, :] = jnp.sum(do_ref[rows, :].astype(F32) * o_ref[rows, :].astype(F32), axis=-1, keepdims=True)
            return 0

        lax.fori_loop(0, nq, dloop, 0)

        def step(j, i, carry, masked):
            dk_acc, dv_acc = carry
            rq = pl.ds(pl.multiple_of(i * BQ, BQ), BQ)
            rk = pl.ds(pl.multiple_of(j * BQ, BQ), BQ)
            qb, kb, vb, dob = q_ref[rq, :], k_ref[rk, :], v_ref[rk, :], do_ref[rq, :]
            s = _dot(qb, kb, NT) * MLA_SCALE
            p = jnp.exp(s - lse_ref[0, rq, :])
            if masked:
                row = lax.broadcasted_iota(jnp.int32, s.shape, 0)
                col = lax.broadcasted_iota(jnp.int32, s.shape, 1)
                p = jnp.where(col <= row, p, 0.0)
            dp = _dot(dob, vb, NT)
            ds = (p * (dp - dd_ref[rq, :]) * MLA_SCALE).astype(BF16)
            dv_acc = dv_acc + _dot(p.astype(BF16), dob, TN_)
            dk_acc = dk_acc + _dot(ds, qb, TN_)
            dq_ref[rq, :] += _dot(ds, kb, NN)
            return dk_acc, dv_acc

        def kloop(j, _):
            init = (jnp.zeros((BQ, HEAD_W), F32), jnp.zeros((BQ, HEAD_W), F32))
            carry = step(j, j, init, True)
            dk_acc, dv_acc = lax.fori_loop(j + 1, nq, lambda i, c: step(j, i, c, False), carry)
            rk = pl.ds(pl.multiple_of(j * BQ, BQ), BQ)
            dk_ref[rk, :] = dk_acc
            dv_ref[rk, :] = dv_acc.astype(BF16)
            return 0

        lax.fori_loop(0, nq, kloop, 0)

    head = pl.BlockSpec((T, HEAD_W), lambda h: (0, h))
    nbytes = 4 * _nbytes((T, HEAD_W), BF16) + 5 * _nbytes((T, HEAD_W), F32) + 2 * _nbytes((T, 128), F32)
    return pl.pallas_call(
        kern, name="attn_bwd", grid=(HEADS,),
        in_specs=[head, head, head, head, pl.BlockSpec((1, T, 1), lambda h: (h, 0, 0)), head] + [ANY_SPEC] * len(deps),
        out_specs=[head, head, head],
        out_shape=[pltpu.HBM((T, HEADS * HEAD_W), F32), pltpu.HBM((T, HEADS * HEAD_W), F32), pltpu.HBM((T, HEADS * HEAD_W), BF16)],
        scratch_shapes=[pltpu.VMEM((T, 1), F32)],
        compiler_params=pltpu.CompilerParams(dimension_semantics=("parallel",), vmem_limit_bytes=_vmem(nbytes)),
    )(*[_hbm(a) for a in (q, k, v, o, lse, dcat)], *deps)


def _sgu_common(u, v, ln_g, ln_b):
    ua, tu = _gelu(u)
    va, tv = _gelu(v)
    vh, r = _ln_stats(va)
    return ua, tu, tv, vh, r, vh * ln_g + ln_b


def _tril_mask(n):
    return lax.broadcasted_iota(jnp.int32, (n, n), 1) <= lax.broadcasted_iota(jnp.int32, (n, n), 0)


def _sgu_fwd(zs, ln_g, ln_b, w, bias_full):
    def body(rows, consts, outs, accs):
        ua, _, _, _, _, vn = _sgu_common(rows[0][...], rows[1][...], consts[0][...], consts[1][...])
        vn = vn.astype(BF16)
        tri = _tril_mask(SGU_CHUNK)
        for g in range(SGU_G):
            wg = jnp.where(tri, consts[2][0, g], 0.0).astype(BF16)
            cols = slice(g * 128, (g + 1) * 128)
            for c in range(ua.shape[0] // SGU_CHUNK):
                rws = slice(c * SGU_CHUNK, (c + 1) * SGU_CHUNK)
                mixed = _dot(wg, vn[rws, cols], NN) + consts[3][:, cols]
                outs[0][rws, cols] = (ua[rws, cols] * mixed).astype(BF16)

    return _rowwise("sgu_fwd", body, [(zs, 512, 0), (zs, 512, 1)], [ln_g, ln_b, w, bias_full], [(SGU_DIM, BF16)])


def _sgu_bwd(zs, dcat, ln_g, ln_b, w, bias_full):
    def body(rows, consts, outs, accs):
        u, v = rows[0][...], rows[1][...]
        ua, tu, tv, vh, r, vn = _sgu_common(u, v, consts[0][...], consts[1][...])
        dout = rows[2][...].astype(F32)
        vn_bf = vn.astype(BF16)
        tri = _tril_mask(SGU_CHUNK)
        dmixed = (dout * ua)
        dmixed_bf = dmixed.astype(BF16)
        ones = jnp.ones((8, SGU_CHUNK), F32)
        dvn_cols, mixed_cols = [], []
        for g in range(SGU_G):
            wg = jnp.where(tri, consts[2][0, g], 0.0).astype(BF16)
            cols = slice(g * 128, (g + 1) * 128)
            dvn_rows, mixed_rows = [], []
            dw = jnp.zeros((SGU_CHUNK, SGU_CHUNK), F32)
            dmix_sum = jnp.zeros((SGU_CHUNK, 128), F32)
            for c in range(u.shape[0] // SGU_CHUNK):
                rws = slice(c * SGU_CHUNK, (c + 1) * SGU_CHUNK)
                mixed_rows.append(_dot(wg, vn_bf[rws, cols], NN) + consts[3][:, cols])
                dvn_rows.append(_dot(wg, dmixed_bf[rws, cols], TN_))
                dw = dw + _dot(dmixed_bf[rws, cols], vn_bf[rws, cols], NT)
                dmix_sum = dmix_sum + dmixed[rws, cols]
            accs[0][g] += jnp.where(tri, dw, 0.0)
            accs[3][g:g + 1, :] += _dot(ones, dmix_sum, NT, precision=HIGHEST)[0:1, :]
            dvn_cols.append(jnp.concatenate(dvn_rows, axis=0))
            mixed_cols.append(jnp.concatenate(mixed_rows, axis=0))
        dvn = jnp.concatenate(dvn_cols, axis=1)
        mixed = jnp.concatenate(mixed_cols, axis=1)
        accs[1][...] += jnp.sum(dvn * vh, axis=0, keepdims=True)
        accs[2][...] += jnp.sum(dvn, axis=0, keepdims=True)
        dvh = dvn * consts[0][...]
        dva = r * (dvh - jnp.mean(dvh, axis=-1, keepdims=True) - vh * jnp.mean(dvh * vh, axis=-1, keepdims=True))
        outs[0][:, 0:512] = (dout * mixed * _gelu_grad(u, tu)).astype(BF16)
        outs[0][:, 512:1024] = (dva * _gelu_grad(v, tv)).astype(BF16)

    return _rowwise("sgu_bwd", body, [(zs, 512, 0), (zs, 512, 1), (dcat, 512, 2)], [ln_g, ln_b, w, bias_full], [(1024, BF16)],
                    [((SGU_G, 128, 128), F32), ((1, SGU_DIM), F32), ((1, SGU_DIM), F32), ((SGU_G, 128), F32)], tr=256)


def _lower_bound(hg_lb):
    a0, a1 = hg_lb[0:1, :], hg_lb[1:2, :]
    m = jnp.maximum(a0, a1)
    e0, e1 = jnp.exp(a0 - m), jnp.exp(a1 - m)
    s0, s1 = e0 / (e0 + e1), e1 / (e0 + e1)
    return (s0 + s1) - s0, s0, s1


def _prefix_rows(x, reverse=False):
    n = x.shape[0]
    row = lax.broadcasted_iota(jnp.int32, x.shape, 0)
    s = 1
    while s < n:
        if reverse:
            x = x + jnp.where(row < n - s, pltpu.roll(x, n - s, 0), 0.0)
        else:
            x = x + jnp.where(row >= s, pltpu.roll(x, s, 0), 0.0)
        s *= 2
    return x


def _hg_gates(qr, fr, lb):
    C = qr.shape[0]
    sq = _sig(qr)
    qf = qr * sq
    sf = _sig(fr)
    gate = lb + (1.0 - lb) * sf
    kk = 1.0 - gate
    tri = _tril_mask(C)
    b = _prefix_rows(jnp.log(gate))
    bref = b[C // 2 - 1:C // 2, :]
    bl = b[C - 1:C, :]
    e_b = jnp.exp(b)
    e_q = jnp.exp(b - bref)
    e_k = jnp.exp(bref - b)
    e_lb = jnp.exp(bl - b)
    return dict(sq=sq, qf=qf, sf=sf, gate=gate, kk=kk, tri=tri, bl=bl, e_b=e_b, e_q=e_q, e_k=e_k, e_lb=e_lb)


def _hgrn_fwd(z1, hg_lb, gnorm):
    T = z1.shape[0]
    C = min(HG_CHUNK, T)
    nc = T // C
    ns = HG_CHUNKS_PER_STEP if nc % HG_CHUNKS_PER_STEP == 0 else 1
    R = ns * C

    def kern(q_ref, f_ref, i_ref, g_ref, lb_ref, gn_ref, o_ref, hg_ref, st_ref, s_scr):
        @pl.when(pl.program_id(0) == 0)
        def _():
            s_scr[...] = jnp.zeros(s_scr.shape, F32)

        lb_all, _, _ = _lower_bound(lb_ref[...])
        for sub in range(ns):
            rows = slice(sub * C, (sub + 1) * C)
            st_ref[sub] = s_scr[...]
            for h in range(HEADS):
                cols = slice(h * HEAD_W, (h + 1) * HEAD_W)
                t = _hg_gates(q_ref[rows, cols], f_ref[rows, cols], lb_all[:, cols])
                v_bf = i_ref[rows, cols].astype(BF16)
                st = s_scr[h]
                a = jnp.where(t["tri"], _dot((t["qf"] * t["e_q"]).astype(BF16), (t["kk"] * t["e_k"]).astype(BF16), NT), 0.0)
                o = _dot(a.astype(BF16), v_bf, NN) + _dot((t["qf"] * t["e_b"]).astype(BF16), st.astype(BF16), NT)
                s_scr[h] = st * jnp.exp(t["bl"]) + _dot(v_bf, (t["kk"] * t["e_lb"]).astype(BF16), TN_)
                o_ref[rows, cols] = o
                gr = g_ref[rows, cols]
                r = lax.rsqrt(jnp.mean(o * o, axis=-1, keepdims=True) + EPS)
                hg_ref[rows, cols] = (o * r * gn_ref[:, cols] * (gr * _sig(gr))).astype(BF16)

    seg = lambda k: pl.BlockSpec((R, D_MODEL), functools.partial(lambda n, k: (n, k), k=k))
    row = pl.BlockSpec((R, D_MODEL), lambda n: (n, 0))
    nbytes = 6 * _nbytes((R, D_MODEL), F32) + (2 + ns) * _nbytes((HEADS, 128, 128), F32)
    return pl.pallas_call(
        kern, name="hgrn_fwd", grid=(nc // ns,),
        in_specs=[seg(0), seg(1), seg(2), seg(3), pl.BlockSpec((2, D_MODEL), lambda n: (0, 0)),
                  pl.BlockSpec((1, D_MODEL), lambda n: (0, 0))],
        out_specs=[row, row, pl.BlockSpec((ns, HEADS, 128, 128), lambda n: (n, 0, 0, 0))],
        out_shape=[pltpu.HBM((T, D_MODEL), F32), pltpu.HBM((T, D_MODEL), BF16),
                   pltpu.HBM((nc, HEADS, 128, 128), F32)],
        scratch_shapes=[pltpu.VMEM((HEADS, 128, 128), F32)],
        compiler_params=pltpu.CompilerParams(dimension_semantics=("arbitrary",), vmem_limit_bytes=_vmem(nbytes)),
    )(*[_hbm(a) for a in (z1, z1, z1, z1, hg_lb, gnorm)])


def _hgrn_bwd(z1, o_pre, dhg, states, hg_lb, gnorm):
    T = z1.shape[0]
    C = min(HG_CHUNK, T)
    nc = T // C
    ns = HG_CHUNKS_PER_STEP if nc % HG_CHUNKS_PER_STEP == 0 else 1
    R, steps = ns * C, nc // ns

    def kern(q_ref, f_ref, i_ref, g_ref, o_ref, dhg_ref, st_ref, lb_ref, gn_ref, dz_ref, dlb_ref, dgn_ref, ds_scr, dlb_scr):
        n = pl.program_id(0)

        @pl.when(n == 0)
        def _():
            ds_scr[...] = jnp.zeros(ds_scr.shape, F32)
            dlb_scr[...] = jnp.zeros(dlb_scr.shape, F32)
            dgn_ref[...] = jnp.zeros(dgn_ref.shape, F32)

        lb_all, s0, s1 = _lower_bound(lb_ref[...])
        for sub in reversed(range(ns)):
            rows = slice(sub * C, (sub + 1) * C)
            for h in range(HEADS):
                cols = slice(h * HEAD_W, (h + 1) * HEAD_W)
                lb = lb_all[:, cols]
                qr, fr = q_ref[rows, cols], f_ref[rows, cols]
                t = _hg_gates(qr, fr, lb)
                tri = t["tri"]
                v_bf = i_ref[rows, cols].astype(BF16)
                st_bf = st_ref[sub, h].astype(BF16)
                dst = ds_scr[h]
                dst_bf = dst.astype(BF16)
                o = o_ref[rows, cols]
                gr = g_ref[rows, cols]
                sg = _sig(gr)
                sil = gr * sg
                gn = gn_ref[:, cols]
                r = lax.rsqrt(jnp.mean(o * o, axis=-1, keepdims=True) + EPS)
                on = o * r
                dh = dhg_ref[rows, cols].astype(F32)
                dgn_ref[:, cols] += jnp.sum(dh * on * sil, axis=0, keepdims=True)
                dg = dh * on * gn * (sg * (1.0 + gr * (1.0 - sg)))
                don = dh * gn * sil
                do_bf = (r * (don - on * jnp.mean(don * on, axis=-1, keepdims=True))).astype(BF16)
                qe = (t["qf"] * t["e_q"]).astype(BF16)
                ke = (t["kk"] * t["e_k"]).astype(BF16)
                qb = (t["qf"] * t["e_b"]).astype(BF16)
                kh_bf = (t["kk"] * t["e_lb"]).astype(BF16)
                a_bf = jnp.where(tri, _dot(qe, ke, NT), 0.0).astype(BF16)
                da_bf = jnp.where(tri, _dot(do_bf, v_bf, NT), 0.0).astype(BF16)
                dv = _dot(a_bf, do_bf, TN_) + _dot(kh_bf, dst_bf, NT)
                dqe = _dot(da_bf, ke, NN)
                dqb = _dot(do_bf, st_bf, NN)
                dke = _dot(da_bf, qe, TN_)
                dkh = _dot(v_bf, dst_bf, NN)
                dqf = dqe * t["e_q"] + dqb * t["e_b"]
                dkk = dke * t["e_k"] + dkh * t["e_lb"]
                kh_r = kh_bf.astype(F32)
                db = qe.astype(F32) * dqe - ke.astype(F32) * dke + qb.astype(F32) * dqb - kh_r * dkh
                e_bl = jnp.exp(t["bl"])
                dbl = jnp.sum(dkh * kh_r, axis=0, keepdims=True) + e_bl * jnp.sum(st_ref[sub, h] * dst, axis=0, keepdims=True)
                dlg = _prefix_rows(db, reverse=True) + dbl
                ds_scr[h] = dst * e_bl + _dot(do_bf, qb, TN_)
                dgate = dlg / t["gate"] - dkk
                sf = t["sf"]
                dlb_scr[:, cols] += jnp.sum(dgate * (1.0 - sf), axis=0, keepdims=True)
                df = dgate * (1.0 - lb) * sf * (1.0 - sf)
                dq = dqf * (t["sq"] * (1.0 + qr * (1.0 - t["sq"])))
                dz_ref[rows, cols] = dq.astype(BF16)
                dz_ref[rows, D_MODEL + h * HEAD_W:D_MODEL + (h + 1) * HEAD_W] = df.astype(BF16)
                dz_ref[rows, 2 * D_MODEL + h * HEAD_W:2 * D_MODEL + (h + 1) * HEAD_W] = dv.astype(BF16)
                dz_ref[rows, 3 * D_MODEL + h * HEAD_W:3 * D_MODEL + (h + 1) * HEAD_W] = dg.astype(BF16)

        @pl.when(n == steps - 1)
        def _():
            d = s0 * s1 * dlb_scr[...]
            dlb_ref[0:1, :] = -d
            dlb_ref[1:2, :] = d

    seg = lambda k: pl.BlockSpec((R, D_MODEL), functools.partial(lambda n, k: (steps - 1 - n, k), k=k))
    nbytes = 6 * _nbytes((R, D_MODEL), F32) + _nbytes((R, 4 * D_MODEL), BF16) + (2 + ns) * _nbytes((HEADS, 128, 128), F32)
    return pl.pallas_call(
        kern, name="hgrn_bwd", grid=(steps,),
        in_specs=[seg(0), seg(1), seg(2), seg(3), seg(0), seg(0),
                  pl.BlockSpec((ns, HEADS, 128, 128), lambda n: (steps - 1 - n, 0, 0, 0)),
                  pl.BlockSpec((2, D_MODEL), lambda n: (0, 0)), pl.BlockSpec((1, D_MODEL), lambda n: (0, 0))],
        out_specs=[pl.BlockSpec((R, 4 * D_MODEL), lambda n: (steps - 1 - n, 0)),
                   pl.BlockSpec((2, D_MODEL), lambda n: (0, 0)), pl.BlockSpec((1, D_MODEL), lambda n: (0, 0))],
        out_shape=[pltpu.HBM((T, 4 * D_MODEL), BF16), pltpu.HBM((2, D_MODEL), F32),
                   pltpu.HBM((1, D_MODEL), F32)],
        scratch_shapes=[pltpu.VMEM((HEADS, 128, 128), F32), pltpu.VMEM((1, D_MODEL), F32)],
        compiler_params=pltpu.CompilerParams(dimension_semantics=("arbitrary",), vmem_limit_bytes=_vmem(nbytes)),
    )(*[_hbm(a) for a in (z1, z1, z1, z1, o_pre, dhg, states, hg_lb, gnorm)])


def _prep_weights(gw):
    w_in_e = gw["w_in_e"].transpose(1, 0, 2).reshape(D_MODEL, 1568)
    kr = jnp.pad(w_in_e[:, 512:544], ((0, 0), (64, 32)))
    wm = jnp.concatenate([w_in_e[:, 0:512], kr], axis=1)
    ws = w_in_e[:, 544:1568]
    w_qb = gw["w_qb"].transpose(1, 0, 2).reshape(MLA_LORA, HEADS, 96)
    wq = jnp.pad(w_qb, ((0, 0), (0, 0), (0, 32))).reshape(MLA_LORA, HEADS * HEAD_W)
    kvb = gw["w_kvb"].transpose(1, 0, 2).reshape(MLA_LORA, HEADS, 128)
    wk = jnp.pad(kvb[:, :, :64], ((0, 0), (0, 0), (0, 64))).reshape(MLA_LORA, HEADS * HEAD_W)
    wv = jnp.pad(kvb[:, :, 64:], ((0, 0), (0, 0), (0, 64))).reshape(MLA_LORA, HEADS * HEAD_W)
    w_out_e = gw["w_out_e"].reshape(D_MODEL, D_MODEL)
    woa = jnp.pad(w_out_e[:512].reshape(HEADS, 64, D_MODEL), ((0, 0), (0, 64), (0, 0))).reshape(HEADS * HEAD_W, D_MODEL)
    return dict(wm=wm, ws=ws, wq=wq, wk=wk, wv=wv, woa=woa, wob=w_out_e[512:])


def _unprep_grads(g):
    dwm, dws = g["wm"], g["ws"]
    d_in_e = jnp.concatenate([dwm[:, 0:512], dwm[:, 512 + 64:512 + 96], dws], axis=1)
    d_qb = g["wq"].reshape(MLA_LORA, HEADS, HEAD_W)[:, :, :96].reshape(MLA_LORA, HEADS * 96)
    dk = g["wk"].reshape(MLA_LORA, HEADS, HEAD_W)[:, :, :64]
    dv = g["wv"].reshape(MLA_LORA, HEADS, HEAD_W)[:, :, :64]
    d_kvb = jnp.concatenate([dk, dv], axis=2).reshape(MLA_LORA, HEADS * 128)
    d_oa = g["woa"].reshape(HEADS, HEAD_W, D_MODEL)[:, :64].reshape(HEADS * 64, D_MODEL)
    dev_major = lambda a: a.reshape(a.shape[0], N_DEV, a.shape[1] // N_DEV).transpose(1, 0, 2)
    return dict(w_in_e=dev_major(d_in_e), w_qb=dev_major(d_qb), w_kvb=dev_major(d_kvb),
                w_out_e=jnp.concatenate([d_oa, g["wob"]], axis=0).reshape(N_DEV, D_MODEL // N_DEV, D_MODEL))


def _local_step(x, positions, target, gw, sp, ex):
    w = _prep_weights(gw)
    T = x.shape[0]
    tm = min(TM, T)
    nt = T // tm
    half = MLA_ROPE // 2
    inv_freq = ROPE_BASE ** (-jnp.arange(half, dtype=F32) / half)
    invf_lane = jnp.concatenate([jnp.zeros((64,), F32), inv_freq, inv_freq, jnp.zeros((32,), F32)]).reshape(1, HEAD_W)
    tabs = _rope_tables(positions.reshape(T, 1), invf_lane)
    bias_full = jnp.repeat(sp["sgu_b"][0].T, 128, axis=1)
    sgu_w = sp["sgu_w"]
    gq, gkv = sp["mla_gq"], sp["mla_gkv"]
    ln1_g, ln1_b, ln2_g, ln2_b = sp["ln1_g"], sp["ln1_b"], sp["ln2_g"], sp["ln2_b"]
    zm, zs, cqn, ckvn, kr_rot = _mla_in(x, w["wm"], w["ws"], tabs, gq, gkv, deps=[ex.first_token])
    q, k, v = _mla_qkv(cqn, ckvn, kr_rot, tabs, w["wq"], w["wk"], w["wv"])
    o_att, lse = _attn_fwd(q, k, v)
    b_out = _sgu_fwd(zs, sp["sgu_ln_g"], sp["sgu_ln_b"], sgu_w, bias_full)
    token = ex.weights_forward(after=[o_att, b_out])
    y1, h1_bf = _proj_ln("l0_out_ln1", [o_att, b_out], [w["woa"], w["wob"]], x, ln1_g, ln1_b, 0, deps=[token])
    big = ex.weights_ready(after=[y1])
    w_ff1, w_in_o, w_out_o = big["w_ff1"], big["w_in_o"], big["w_out_o"].reshape(D_MODEL, D_MODEL)
    w_ff2 = [a.reshape(D_FF, D_MODEL) for a in big["w_ff2"]]
    a0, act0 = _mlp_up("l0", h1_bf, w_ff1[0])
    y2, h2_bf = _proj_ln("l0_ff2_ln2", [act0], [w_ff2[0]], y1, ln2_g, ln2_b, 0, prev=(ln1_g, ln1_b, 0))

    z1 = _tiled("l1_in", (1, nt), [_rb(h2_bf, tm), _res(w_in_o)], [_out(T, 4 * D_MODEL, F32, tm, 4 * D_MODEL)],
                _mmc_blocks(N_DEV, NN, lambda w, d: w[d]), direct=True)
    o_pre, hg, states = _hgrn_fwd(z1, sp["hg_lb"], sp["hg_gnorm"])
    y3, h3_bf = _proj_ln("l1_out_ln1", [hg], [w_out_o], y2, ln1_g, ln1_b, 1, prev=(ln2_g, ln2_b, 0))
    a1, act1 = _mlp_up("l1", h3_bf, w_ff1[1])

    gs, g0 = {}, {}
    dy4, dy4_bf, sq_err, gs["ln2_g1"], gs["ln2_b1"] = _proj_ln_loss("l1_ff2_loss", act1, w_ff2[1], y3, ln2_g, ln2_b, 1, target,
                                                                     prev=(ln1_g, ln1_b, 1))
    gs["sq_err"] = sq_err
    da1, dw1_1, dw2_1 = _mlp_bwd_w("l1", h3_bf, a1, act1, dy4_bf, big["w_ff2"][1])
    dy3, dy3_bf, dhg, gs["ln1_g1"], gs["ln1_b1"] = _dh_ln_back("l1_dh_ln1", da1, w_ff1[1], dy4, y3, ln1_g, 1, proj=[w_out_o])
    d_out_o = _tiled("l1_dwout", (2, D_MODEL // TM), [_tl(hg, TM), _cw(dy3_bf, TN)],
                     [_out(D_MODEL, D_MODEL, F32, TM, TN), _out(D_MODEL, D_MODEL, BF16, TM, TN)], _mmc(TN_, epilogue=_twice))
    d_out_o = [a.reshape(N_DEV, D_MODEL // N_DEV, D_MODEL) for a in d_out_o]
    dz1, gs["hg_lb"], gs["hg_gnorm"] = _hgrn_bwd(z1, o_pre, dhg, states, sp["hg_lb"], sp["hg_gnorm"])
    d_in_o = _tiled("l1_dwin", (N_DEV, 1), [_res(h2_bf), _cw(dz1, TN)],
                    [_out_dev(D_MODEL, TN, D_MODEL), _out_dev(D_MODEL, TN, D_MODEL, BF16)], _mmc(TN_, epilogue=_twice))
    token = ex.direct_start("l1", [dw1_1, dw2_1, d_in_o, d_out_o])

    dy2, dy2_bf, gs["ln2_g0"], gs["ln2_b0"] = _dh_ln_back("l1_dh_ln2", dz1, w_in_o, dy3, y2, ln2_g, 0, deps=[token])
    da0, dw1_0, dw2_0 = _mlp_bwd_w("l0", h1_bf, a0, act0, dy2_bf, big["w_ff2"][0])
    token = ex.direct_start("l0m", [dw1_0, dw2_0])
    dy1, dy1_bf, dcat, gs["ln1_g0"], gs["ln1_b0"] = _dh_ln_back("l0_dh_ln1", da0, w_ff1[0], dy2, y1, ln1_g, 0,
                                                                 proj=[w["woa"], w["wob"]], deps=[token])
    g0["woa"], g0["wob"] = _out_weight_grads(o_att, b_out, dy1_bf)
    dzs, gs["sgu_w"], gs["sgu_ln_g"], gs["sgu_ln_b"], gs["sgu_b"] = _sgu_bwd(zs, dcat, sp["sgu_ln_g"], sp["sgu_ln_b"], sgu_w, bias_full)
    dq, dk, dv = _attn_bwd(q, k, v, o_att, lse, dcat)
    dzm, g0["wq"], g0["wk"], g0["wv"], gs["mla_gq"], gs["mla_gkv"] = _mla_back(zm, cqn, ckvn, tabs, gq, gkv, w["wq"], w["wk"], w["wv"],
                                                                                 dq, dk, dv)
    token = ex.small_start(gs)
    dx, g0["wm"], g0["ws"] = _in_back(x, dzm, dzs, dy1, w["wm"], w["ws"], deps=[token])

    return sq_err, dx, _unprep_grads(g0), gs


def _me():
    return lax.axis_index("x"), lax.axis_index("y"), lax.axis_index("c")


ANY_SPEC = pl.BlockSpec(memory_space=pl.ANY)
HBM_SPEC = pl.BlockSpec(memory_space=pltpu.HBM)
SEM_SPEC = pl.BlockSpec(memory_space=pltpu.SEMAPHORE)
EFFECT = pltpu.SideEffectType.DATAFLOW_SIDE_EFFECTING


def _split_start(name, srcs, lands, n_sems, make_copies, after=()):
    n, m, k = len(srcs), len(lands), len(after)

    def body(*refs):
        for cp in make_copies(refs[:n], refs[n:n + m], refs[n + m + k], refs[n + m + k + 1]):
            cp.start()
        refs[-1][...] = jnp.zeros(refs[-1].shape, F32)

    out_shape = (pltpu.SemaphoreType.DMA((n_sems,)), pltpu.SemaphoreType.DMA((n_sems,)),
                 *[pltpu.HBM(a.shape, a.dtype) for a in (*srcs, *lands)], jax.ShapeDtypeStruct((8, 128), F32))
    res = pl.pallas_call(
        body, name=name, out_shape=out_shape, in_specs=[HBM_SPEC] * (n + m) + [ANY_SPEC] * k,
        out_specs=(SEM_SPEC, SEM_SPEC, *[HBM_SPEC] * (n + m), pl.BlockSpec(memory_space=pltpu.VMEM)),
        input_output_aliases={i: 2 + i for i in range(n + m)},
        compiler_params=pltpu.CompilerParams(has_side_effects=EFFECT),
    )(*[_hbm(a) for a in (*srcs, *lands)], *after)
    return res[0], res[1], list(res[2:2 + n]), list(res[2 + n:2 + n + m]), res[-1]


def _split_wait(name, send_sems, recv_sems, srcs, lands, after, make_copies):
    n, m = len(srcs), len(lands)

    def body(*refs):
        for cp in make_copies(refs[:n], refs[n:n + m], refs[n + m], refs[n + m + 1]):
            cp.wait_send()
            cp.wait_recv()

    res = pl.pallas_call(
        body, name=name, out_shape=tuple(pltpu.HBM(a.shape, a.dtype) for a in (*srcs, *lands)),
        in_specs=[HBM_SPEC] * (n + m) + [SEM_SPEC, SEM_SPEC] + [ANY_SPEC] * len(after), out_specs=tuple([HBM_SPEC] * (n + m)),
        input_output_aliases={i: i for i in range(n + m)},
        compiler_params=pltpu.CompilerParams(has_side_effects=EFFECT),
    )(*srcs, *lands, send_sems, recv_sems, *after)
    return list(res[:n]), list(res[n:])


def _place_own(shards, dev):
    n = len(shards)

    def kern(dev_ref, *refs):
        for x_ref, o_ref in zip(refs[:n], refs[n:]):
            o_ref[...] = x_ref[...].astype(o_ref.dtype)

    blocks = [(None, *a.shape[1:]) for a, _, _ in shards]
    nbytes = sum(_nbytes(b, a.dtype) + _nbytes(b, dt) for b, (a, _, dt) in zip(blocks, shards))
    return pl.pallas_call(
        kern, name="weights_place_own", out_shape=[pltpu.HBM((N_DEV, *a.shape[1:]), dt) for a, _, dt in shards],
        grid_spec=pltpu.PrefetchScalarGridSpec(
            num_scalar_prefetch=1, grid=(1,),
            in_specs=[pl.BlockSpec(b, functools.partial(lambda i, dev, l: (l, 0, 0), l=l)) for b, (_, l, _) in zip(blocks, shards)],
            out_specs=[pl.BlockSpec(b, lambda i, dev: (dev[0], 0, 0)) for b in blocks]),
        compiler_params=pltpu.CompilerParams(dimension_semantics=("arbitrary",), vmem_limit_bytes=_vmem(nbytes)),
    )(dev, *[_hbm(a) for a, _, _ in shards])


def _ag_first_copies(src_refs, out_refs, send_sems, recv_sems):
    x, y, c = _me()
    targets = [(x, y, 1 - c), (1 - x, y, c), (x, 1 - y, c), (1 - x, 1 - y, c)]
    return [pltpu.make_async_remote_copy(
        src_ref=out_refs[op].at[4 * x + 2 * y + c], dst_ref=out_refs[op].at[4 * x + 2 * y + c], send_sem=send_sems.at[4 * op + k],
        recv_sem=recv_sems.at[4 * op + k], device_id=to, device_id_type=MESH)
        for op in range(len(out_refs)) for k, to in enumerate(targets)]


def _ag_second_copies(src_refs, out_refs, send_sems, recv_sems):
    x, y, c = _me()
    chips = [(1 - x, y), (x, 1 - y), (1 - x, 1 - y)]
    return [pltpu.make_async_remote_copy(
        src_ref=out_refs[op].at[4 * cx + 2 * cy + c], dst_ref=out_refs[op].at[4 * cx + 2 * cy + c],
        send_sem=send_sems.at[3 * op + j], recv_sem=recv_sems.at[3 * op + j], device_id=(x, y, 1 - c), device_id_type=MESH)
        for op in range(len(out_refs)) for j, (cx, cy) in enumerate(chips)]


def _rs_sibling_copies(g_refs, out_refs, send_sems, recv_sems):
    x, y, c = _me()
    return [pltpu.make_async_remote_copy(
        src_ref=g_refs[op].at[k, 1 - c], dst_ref=out_refs[op].at[k], send_sem=send_sems.at[4 * op + k],
        recv_sem=recv_sems.at[4 * op + k], device_id=(x, y, 1 - c), device_id_type=MESH)
        for op in range(len(g_refs)) for k in range(4)]


def _rs_direct_copies(g_refs, land_refs, send_sems, recv_sems):
    x, y, c = _me()
    n = len(g_refs) // 2
    chips = [(1 - x, y), (x, 1 - y), (1 - x, 1 - y)]
    copies = []
    for op in range(n):
        g32, g16, from_sib, from_others = g_refs[op], g_refs[n + op], land_refs[op], land_refs[n + op]
        copies.append(pltpu.make_async_remote_copy(
            src_ref=g32.at[2 * x + y, 1 - c], dst_ref=from_sib, send_sem=send_sems.at[7 * op], recv_sem=recv_sems.at[7 * op],
            device_id=(x, y, 1 - c), device_id_type=MESH))
        for j, (cx, cy) in enumerate(chips):
            for s, cc in enumerate((c, 1 - c)):
                copies.append(pltpu.make_async_remote_copy(
                    src_ref=g16.at[2 * cx + cy, cc], dst_ref=from_others.at[2 * j + s], send_sem=send_sems.at[7 * op + 1 + 2 * j + s],
                    recv_sem=recv_sems.at[7 * op + 1 + 2 * j + s], device_id=(cx, cy, cc), device_id_type=MESH))
    return copies


def _rs_chip_copies(p_refs, out_refs, send_sems, recv_sems):
    x, y, c = _me()
    chips = [(1 - x, y), (x, 1 - y), (1 - x, 1 - y)]
    return [pltpu.make_async_remote_copy(
        src_ref=p_refs[op].at[2 * cx + cy], dst_ref=out_refs[op].at[j], send_sem=send_sems.at[3 * op + j],
        recv_sem=recv_sems.at[3 * op + j], device_id=(cx, cy, c), device_id_type=MESH)
        for op in range(len(p_refs)) for j, (cx, cy) in enumerate(chips)]


def _all_gather(placed):
    n = len(placed)

    def kern(*refs):
        in_refs, out_refs, (send_sems, recv_sems) = refs[:n], refs[n:2 * n], refs[2 * n:]
        x, y, c = _me()
        me, sibling = (x, y, c), (x, y, 1 - c)
        chips = [(1 - x, y), (x, 1 - y), (1 - x, 1 - y)]

        def copy(op, k, block, to, own=False):
            idx = 4 * block[0] + 2 * block[1] + block[2]
            return pltpu.make_async_remote_copy(
                src_ref=(in_refs if own else out_refs)[op].at[idx], dst_ref=out_refs[op].at[idx], send_sem=send_sems.at[7 * op + k],
                recv_sem=recv_sems.at[7 * op + k], device_id=to, device_id_type=MESH)

        first = []
        for op in range(n):
            first.append(copy(op, 0, me, sibling, own=True))
            first += [copy(op, 1 + j, me, (*chip, c), own=True) for j, chip in enumerate(chips)]
        for cp in first:
            cp.start()
        passed = []
        for j, chip in enumerate(chips):
            for op in range(n):
                copy(op, 1 + j, (*chip, c), me).wait_recv()
                passed.append(copy(op, 4 + j, (*chip, c), sibling))
                passed[-1].start()
        for op in range(n):
            copy(op, 0, sibling, me).wait_recv()
            for j, chip in enumerate(chips):
                copy(op, 4 + j, (*chip, 1 - c), me).wait_recv()
        for cp in first + passed:
            cp.wait_send()

    return pl.pallas_call(
        kern, name="weights_all_gather", out_shape=[pltpu.HBM(g.shape, g.dtype) for g in placed],
        in_specs=[ANY_SPEC] * n, out_specs=[ANY_SPEC] * n, input_output_aliases={i: i for i in range(n)},
        scratch_shapes=[pltpu.SemaphoreType.DMA((7 * n,)), pltpu.SemaphoreType.DMA((7 * n,))],
    )(*[_hbm(a) for a in placed])


def _row_tile(r, w, n_blocks):
    tr = r
    while tr > 8 and 2 * n_blocks * tr * w * 4 > 24 * 2**20:
        tr //= 2
    return tr


def _chip_sum(name, g, from_sibling, core):
    _, _, R, W = g.shape
    tr = _row_tile(R, W, 3)

    def kern(core_ref, g_ref, s_ref, o_ref):
        o_ref[...] = (g_ref[...] + s_ref[...]).astype(BF16)

    return pl.pallas_call(
        kern, name=name, out_shape=pltpu.HBM((4, R, W), BF16),
        grid_spec=pltpu.PrefetchScalarGridSpec(
            num_scalar_prefetch=1, grid=(4, R // tr),
            in_specs=[pl.BlockSpec((None, None, tr, W), lambda k, i, core: (k, core[0], i, 0)),
                      pl.BlockSpec((None, tr, W), lambda k, i, core: (k, i, 0))],
            out_specs=pl.BlockSpec((None, tr, W), lambda k, i, core: (k, i, 0))),
        compiler_params=pltpu.CompilerParams(dimension_semantics=("parallel", "parallel"), vmem_limit_bytes=_vmem(3 * tr * W * 4)),
    )(core, _hbm(g), _hbm(from_sibling))


def _adamw(w, g, m, v):
    m = ADAM_B1 * m + (1.0 - ADAM_B1) * g
    v = ADAM_B2 * v + (1.0 - ADAM_B2) * (g * g)
    m_hat = m / (1.0 - ADAM_B1 ** ADAM_STEP)
    v_hat = v / (1.0 - ADAM_B2 ** ADAM_STEP)
    return -ADAM_LR * (m_hat / (jnp.sqrt(v_hat) + ADAM_EPS) + ADAM_WD * w), m, v


def _finish_sharded(name, layers, w, m, v, where, deps=()):
    nl, R, W = w.shape
    n_other = layers[0][2].shape[0]
    tr = _row_tile(R, W, (8 + n_other) * nl)
    deps = _deps(deps)

    def kern(where_ref, *refs):
        w_ref, m_ref, v_ref = refs[3 * nl:3 * nl + 3]
        go_ref, d_ref, mo_ref, vo_ref = refs[3 * nl + 3 + len(deps):]
        for l in range(nl):
            g_ref, s_ref, c_ref = refs[3 * l:3 * l + 3]
            grad = g_ref[...] + s_ref[...]
            for j in range(n_other):
                grad = grad + c_ref[j].astype(F32)
            go_ref[l] = grad
            d_ref[l], mo_ref[l], vo_ref[l] = _adamw(w_ref[l], grad, m_ref[l], v_ref[l])

    row = pl.BlockSpec((nl, tr, W), lambda i, wh: (0, i, 0))
    in_specs, args = [], []
    for g, s, c in layers:
        sib = (pl.BlockSpec((None, tr, W), lambda i, wh: (wh[0], i, 0)) if s.ndim == 3 else pl.BlockSpec((tr, W), lambda i, wh: (i, 0)))
        in_specs += [pl.BlockSpec((None, None, tr, W), lambda i, wh: (wh[0], wh[1], i, 0)), sib,
                     pl.BlockSpec((n_other, tr, W), lambda i, wh: (0, i, 0))]
        args += [g, s, c]
    return pl.pallas_call(
        kern, name=name, out_shape=[pltpu.HBM((nl, R, W), F32)] * 4,
        grid_spec=pltpu.PrefetchScalarGridSpec(num_scalar_prefetch=1, grid=(R // tr,),
                                               in_specs=in_specs + [row, row, row] + [ANY_SPEC] * len(deps),
                                               out_specs=[row, row, row, row]),
        compiler_params=pltpu.CompilerParams(dimension_semantics=("parallel",),
                                             vmem_limit_bytes=_vmem(nl * (8 + n_other) * tr * W * 4)),
    )(where, *[_hbm(a) for a in (*args, w, m, v)], *deps)


SMALL_PLACE = (("mla_gq", 0, 0, 1, 256), ("mla_gkv", 0, 256, 1, 256), ("sgu_ln_g", 0, 512, 1, 512), ("sgu_ln_b", 1, 0, 1, 512),
               ("hg_lb", 2, 0, 2, 1024), ("ln1_g", 4, 0, 2, 1024), ("ln1_b", 6, 0, 2, 1024), ("sgu_b", 8, 0, 4, 128),
               ("ln2_g", 12, 0, 2, 1024), ("ln2_b", 14, 0, 2, 1024), ("hg_gnorm", 16, 0, 1, 1024))
SMALL_BUF_ROWS = 24
LOSS_ROW = 17


def _small_pack(gs, dev):
    pieces = [(gs["mla_gq"], 0, 0), (gs["mla_gkv"], 0, 256), (gs["sgu_ln_g"], 0, 512), (gs["sgu_ln_b"], 1, 0), (gs["hg_lb"], 2, 0),
              (gs["ln1_g0"], 4, 0), (gs["ln1_g1"], 5, 0), (gs["ln1_b0"], 6, 0), (gs["ln1_b1"], 7, 0), (gs["sgu_b"], 8, 0),
              (gs["ln2_g0"], 12, 0), (gs["ln2_g1"], 13, 0), (gs["ln2_b0"], 14, 0), (gs["ln2_b1"], 15, 0), (gs["hg_gnorm"], 16, 0),
              (gs["sq_err"], LOSS_ROW, 0)]
    n_p = len(pieces)

    def kern(dev_ref, *refs):
        a_ref, b_ref = refs[n_p + 1], refs[n_p + 2]
        a_ref[...] = jnp.zeros(a_ref.shape, F32)
        for ref, (_, r, l0) in zip(refs[:n_p], pieces):
            a_ref[r:r + ref.shape[0], l0:l0 + ref.shape[1]] = ref[...]
        b_ref[...] = refs[n_p][...]

    whole = lambda a: pl.BlockSpec(a.shape, functools.partial(lambda i, dev, nd: (0,) * nd, nd=a.ndim))
    return pl.pallas_call(
        kern, name="small_grads_pack",
        out_shape=[pltpu.HBM((N_DEV, SMALL_BUF_ROWS, D_MODEL), F32), pltpu.HBM((N_DEV, SGU_G, 128, 128), F32)],
        grid_spec=pltpu.PrefetchScalarGridSpec(
            num_scalar_prefetch=1, grid=(1,), in_specs=[whole(p[0]) for p in pieces] + [whole(gs["sgu_w"])],
            out_specs=[pl.BlockSpec((None, SMALL_BUF_ROWS, D_MODEL), lambda i, dev: (dev[0], 0, 0)),
                       pl.BlockSpec((None, SGU_G, 128, 128), lambda i, dev: (dev[0], 0, 0, 0))]),
    )(dev, *[p[0] for p in pieces], gs["sgu_w"])


def _small_copies(src_refs, land_refs, send_sems, recv_sems):
    px, py, pc = _me()
    me = 4 * px + 2 * py + pc
    return [pltpu.make_async_remote_copy(
        src_ref=land_refs[k].at[me], dst_ref=land_refs[k].at[me], send_sem=send_sems.at[2 * (r - 1) + k],
        recv_sem=recv_sems.at[2 * (r - 1) + k], device_id=(px ^ (r >> 2), py ^ ((r >> 1) & 1), pc ^ (r & 1)), device_id_type=MESH)
        for r in range(1, N_DEV) for k in range(2)]


def _small_adamw(slots_a, slots_b, given):
    names = [p[0] for p in SMALL_PLACE] + ["sgu_w"]
    n_names = len(names)
    wmv = [given[pre + name] for name in names for pre in ("", "m_", "v_")]
    vmem = pl.BlockSpec(memory_space=pltpu.VMEM)

    def kern(*refs):
        sum_a, sum_b = refs[0][0], refs[1][0]
        for d in range(1, N_DEV):
            sum_a, sum_b = sum_a + refs[0][d], sum_b + refs[1][d]
        wmv_refs, out_refs = refs[2:2 + 3 * n_names], refs[2 + 3 * n_names:]
        px, py, pc = _me()
        me = 4 * px + 2 * py + pc

        def own_block(full):
            acc = full[:, 0:128]
            for b in range(1, N_DEV):
                acc = jnp.where(me == b, full[:, b * 128:(b + 1) * 128], acc)
            return acc

        for idx, name in enumerate(names):
            w_ref, m_ref, v_ref = wmv_refs[3 * idx:3 * idx + 3]
            if name == "sgu_w":
                grad = sum_b[None]
            else:
                _, r, l0, nr, nl = SMALL_PLACE[idx]
                grad = sum_a[r:r + nr, l0:l0 + nl]
                if name == "hg_gnorm":
                    grad = own_block(grad)
                if name == "sgu_b":
                    grad = grad[None]
            res = (grad, *_adamw(w_ref[...], grad, m_ref[...], v_ref[...]))
            for o_ref, val in zip(out_refs[4 * idx:4 * idx + 4], res):
                o_ref[...] = val
        out_refs[4 * n_names][...] = (0.5 / D_MODEL) * jnp.sum(sum_a[LOSS_ROW:LOSS_ROW + 1, :], axis=1, keepdims=True)

    out_shape = [jax.ShapeDtypeStruct(given[name].shape, F32) for name in names for _ in range(4)]
    out_shape.append(jax.ShapeDtypeStruct((1, 1), F32))
    res = pl.pallas_call(
        kern, name="small_adamw", out_shape=out_shape, in_specs=[vmem] * (2 + len(wmv)), out_specs=[vmem] * len(out_shape),
    )(slots_a, slots_b, *wmv)
    out = {name: res[4 * idx:4 * idx + 4] for idx, name in enumerate(names)}
    out["loss"] = res[-1].reshape(())
    return out


class _Exchange:
    def __init__(self, given):
        self.given = given
        px, py, pc = _me()
        self.core = pc.reshape(1).astype(jnp.int32)
        self.dev = (4 * px + 2 * py + pc).reshape(1).astype(jnp.int32)
        self.where = jnp.stack([2 * px + py, pc]).astype(jnp.int32)
        self.state, self.layers = {}, {}

    def start_weights(self, lands, after):
        self.weights = _split_start("weights_first_start", [], lands, 4 * len(lands), _ag_first_copies, after=after)
        self.first_token = self.weights[4]

    def weights_forward(self, after):
        send_sems, recv_sems, shards, lands, _ = self.weights
        _, lands = _split_wait("weights_first_wait", send_sems, recv_sems, shards, lands, after, _ag_first_copies)
        self.weights = _split_start("weights_second_start", [], lands, 3 * len(lands), _ag_second_copies)
        return self.weights[4]

    def weights_ready(self, after):
        send_sems, recv_sems, shards, lands, _ = self.weights
        _, got = _split_wait("weights_second_wait", send_sems, recv_sems, shards, lands, after, _ag_second_copies)
        return dict(w_in_o=got[0], w_out_o=got[1], w_ff1=[got[2], got[3]], w_ff2=[got[4], got[5]])

    def small_start(self, gs):
        self.small = _split_start("small_grads_start", [], _small_pack(gs, self.dev), 14, _small_copies)
        return self.small[4]

    def small_finish(self, after):
        send_sems, recv_sems, _, lands, _ = self.small
        _, lands = _split_wait("small_grads_wait", send_sems, recv_sems, [], lands, after, _small_copies)
        return _small_adamw(lands[0], lands[1], self.given)

    def direct_start(self, tag, grads):
        f32 = [g[0].reshape(4, 2, *g[0].shape[1:]) for g in grads]
        bf16 = [g[1].reshape(4, 2, *g[1].shape[1:]) for g in grads]
        lands = [lax.empty(b.shape[2:], F32) for b in f32] + [lax.empty((6, *b.shape[2:]), BF16) for b in f32]
        self.state[tag] = _split_start(f"grads_{tag}_start", f32 + bf16, lands, 7 * len(grads), _rs_direct_copies)
        return self.state[tag][4]

    def direct_end(self, tag, after):
        send_sems, recv_sems, srcs, lands, _ = self.state[tag]
        srcs, lands = _split_wait(f"grads_{tag}_wait", send_sems, recv_sems, srcs, lands, after, _rs_direct_copies)
        n = len(lands) // 2
        self.layers[tag] = list(zip(srcs[:n], lands[:n], lands[n:]))

    def grads_start(self, tag, grads):
        blocks = [g.reshape(4, 2, *g.shape[1:]) for g in grads]
        lands = [lax.empty((4, *b.shape[2:]), F32) for b in blocks]
        self.state[tag] = _split_start(f"grads_{tag}_sibling_start", blocks, lands, 4 * len(blocks), _rs_sibling_copies)
        return self.state[tag][4]

    def grads_middle(self, tag, after):
        send_sems, recv_sems, blocks, lands, _ = self.state[tag]
        blocks, from_sibling = _split_wait(f"grads_{tag}_sibling_wait", send_sems, recv_sems, blocks, lands, [after], _rs_sibling_copies)
        sums = [_chip_sum(f"grads_{tag}_chip_sum_{k}", b, s, self.core) for k, (b, s) in enumerate(zip(blocks, from_sibling))]
        lands = [lax.empty((3, *p.shape[1:]), BF16) for p in sums]
        self.state[tag] = (blocks, from_sibling, _split_start(f"grads_{tag}_chips_start", sums, lands, 3 * len(sums), _rs_chip_copies))
        return self.state[tag][2][4]

    def grads_end(self, tag, after):
        blocks, from_sibling, (send_sems, recv_sems, sums, lands, _) = self.state[tag]
        after = list(after) if isinstance(after, (list, tuple)) else [after]
        _, from_chips = _split_wait(f"grads_{tag}_chips_wait", send_sems, recv_sems, sums, lands, after, _rs_chip_copies)
        self.layers[tag] = list(zip(blocks, from_sibling, from_chips))


def kernel(x, positions, w_in_e, mla_gq, mla_gkv, w_qb, w_kvb, sgu_ln_g, sgu_ln_b, sgu_w, sgu_b, w_out_e, w_in_o, hg_lb, hg_gnorm, w_out_o, ln1_g, ln1_b, w_ff1, w_ff2, ln2_g, ln2_b, loss_target, m_w_in_e, m_mla_gq, m_mla_gkv, m_w_qb, m_w_kvb, m_sgu_ln_g, m_sgu_ln_b, m_sgu_w, m_sgu_b, m_w_out_e, m_w_in_o, m_hg_lb, m_hg_gnorm, m_w_out_o, m_ln1_g, m_ln1_b, m_w_ff1, m_w_ff2, m_ln2_g, m_ln2_b, v_w_in_e, v_mla_gq, v_mla_gkv, v_w_qb, v_w_kvb, v_sgu_ln_g, v_sgu_ln_b, v_sgu_w, v_sgu_b, v_w_out_e, v_w_in_o, v_hg_lb, v_hg_gnorm, v_w_out_o, v_ln1_g, v_ln1_b, v_w_ff1, v_w_ff2, v_ln2_g, v_ln2_b):
    given = dict(locals())
    ex = _Exchange(given)

    names = ["w_in_e", "w_qb", "w_kvb", "w_out_e"]
    placed = _place_own([(given[n], 0, BF16) for n in names] + [(hg_gnorm.reshape(1, 1, D_MODEL // N_DEV), 0, F32)]
                        + [(w_in_o, 0, BF16), (w_out_o, 0, BF16), (w_ff1, 0, BF16), (w_ff1, 1, BF16), (w_ff2, 0, BF16), (w_ff2, 1, BF16)],
                        ex.dev)
    got = _all_gather(placed[:5])
    ex.start_weights(placed[5:], after=[got[0]])
    gw = dict(zip(names, got[:4]))
    small_names = ["mla_gq", "mla_gkv", "sgu_ln_g", "sgu_ln_b", "sgu_w", "sgu_b", "hg_lb", "ln1_g", "ln1_b", "ln2_g", "ln2_b"]
    sp = {n: given[n] for n in small_names}
    sp["hg_gnorm"] = got[4].reshape(1, D_MODEL)

    _, dx, grads, gs = _local_step(x[0], positions[0], loss_target[0], gw, sp, ex)

    def finish(n, layers, deps=()):
        return _finish_sharded(f"finish_{n}", layers, given[n], given["m_" + n], given["v_" + n], ex.where, deps=deps)

    ex.direct_end("l1", after=[dx])
    ex.direct_end("l0m", after=[dx])
    l1, l0m = ex.layers["l1"], ex.layers["l0m"]
    results = {}
    token = ex.grads_start("l0s", [grads[n] for n in names])
    results["w_ff1"] = finish("w_ff1", [l0m[0], l1[0]], deps=[token])
    token = ex.grads_middle("l0s", after=results["w_ff1"][0])
    results["w_ff2"] = finish("w_ff2", [l0m[1], l1[1]], deps=[token])
    results["w_in_o"] = finish("w_in_o", [l1[2]], deps=[token])
    results["w_out_o"] = finish("w_out_o", [l1[3]], deps=[token])
    results.update(ex.small_finish(after=[results["w_in_o"][0]]))
    ex.grads_end("l0s", after=[results[n][0] for n in ("mla_gq", "w_ff2", "w_in_o", "w_out_o")])
    for n, layer in zip(names, ex.layers["l0s"]):
        results[n] = finish(n, [layer])

    order = ["w_in_e", "mla_gq", "mla_gkv", "w_qb", "w_kvb", "sgu_ln_g", "sgu_ln_b", "sgu_w", "sgu_b", "w_out_e", "w_in_o",
             "hg_lb", "hg_gnorm", "w_out_o", "ln1_g", "ln1_b", "w_ff1", "w_ff2", "ln2_g", "ln2_b"]
    return (results["loss"], dx[None], *[results[name][kind] for kind in range(4) for name in order])
```

```python
import functools
import math

import jax
import jax.numpy as jnp
import numpy as np
from jax import lax
from jax.experimental import pallas as pl
from jax.experimental.pallas import tpu as pltpu

F32 = jnp.float32
BF16 = jnp.bfloat16
MESH = pl.DeviceIdType.MESH
HIGHEST = lax.Precision.HIGHEST

D_MODEL = 1024
D_FF = 4096
N_DEV = 8
HEADS = 8
HEAD_W = 128
MLA_NOPE = 64
MLA_ROPE = 32
MLA_V = 64
MLA_LORA = 256
MLA_SCALE = (MLA_NOPE + MLA_ROPE) ** -0.5
ROPE_BASE = 10000.0
SGU_DIM = 512
SGU_G = 4
SGU_CHUNK = 128
HG_CHUNK = 64
HG_CHUNKS_PER_STEP = 8
ALPHA = (2 * 2) ** 0.25
EPS = 1e-5
ADAM_LR, ADAM_B1, ADAM_B2, ADAM_EPS, ADAM_WD, ADAM_STEP = 0.001, 0.9, 0.999, 1e-08, 0.01, 10

VMEM_CAP_V7X = 56 * 2**20
VMEM_SLACK = 12 * 2**20
TM = 512
TN = 512


def _vmem(block_bytes):
    return int(min(VMEM_CAP_V7X, 2 * block_bytes + VMEM_SLACK))


def _hbm(a):
    return pltpu.with_memory_space_constraint(a, pltpu.HBM)


def _nbytes(shape, dtype):
    return int(np.prod([d for d in shape if d is not None])) * jnp.dtype(dtype).itemsize


def _sig(x):
    return 1.0 / (1.0 + jnp.exp(-x))


def _gelu(x):
    c = math.sqrt(2.0 / math.pi)
    t = jnp.tanh(c * (x + 0.044715 * x * x * x))
    return 0.5 * x * (1.0 + t), t


def _gelu_grad(x, t):
    c = math.sqrt(2.0 / math.pi)
    return 0.5 * (1.0 + t) + 0.5 * x * (1.0 - t * t) * c * (1.0 + 3 * 0.044715 * x * x)


def _dot(a, b, dims, precision=None):
    return lax.dot_general(a, b, (dims, ((), ())), preferred_element_type=F32, precision=precision)


NN = ((1,), (0,))
NT = ((1,), (1,))
TN_ = ((0,), (0,))


def _deps(deps):
    return [d for d in deps if d is not None]


def _tiled(name, grid, ins, outs, compute, direct=False, deps=()):
    n_in, deps = len(ins), _deps(deps)
    n_skip = n_in + len(deps)

    def kern(*refs):
        if direct:
            compute(refs[:n_in], refs[n_skip:])
            return
        for o_ref, r in zip(refs[n_skip:], compute(*refs[:n_in])):
            o_ref[...] = r.astype(o_ref.dtype).reshape(o_ref.shape)

    swap = lambda f: (lambda j, i: f(i, j))
    nbytes = sum(_nbytes(blk, a.dtype) for a, blk, _ in ins) + sum(_nbytes(blk, dt) + _nbytes(blk, F32) for _, dt, blk, _ in outs)
    res = pl.pallas_call(
        kern, name=name, grid=grid,
        in_specs=[pl.BlockSpec(blk, swap(f), pipeline_mode=pl.Buffered(1) if tuple(blk) == tuple(a.shape) else None)
                  for a, blk, f in ins] + [ANY_SPEC] * len(deps),
        out_specs=[pl.BlockSpec(blk, swap(f)) for _, _, blk, f in outs],
        out_shape=[pltpu.HBM(shape, dt) for shape, dt, _, _ in outs],
        compiler_params=pltpu.CompilerParams(dimension_semantics=("parallel", "parallel"), vmem_limit_bytes=_vmem(nbytes)),
    )(*[_hbm(a) for a, _, _ in ins], *deps)
    return res if len(res) > 1 else res[0]


def _rb(a, tm, w=None, cb=0):
    return (a, (tm, a.shape[1] if w is None else w), lambda i, j: (i, cb))


def _cw(b, tn):
    return (b, (b.shape[0], tn), lambda i, j: (0, j))


def _tl(a, tm):
    return (a, (a.shape[0], tm), lambda i, j: (0, i))


def _out(m, n, dtype, tm, tn):
    return ((m, n), dtype, (tm, tn), lambda i, j: (i, j))


def _out_dev(k, n, tm, dtype=F32):
    return ((N_DEV, k, n), dtype, (None, tm, n), lambda i, j: (j, i, 0))


def _twice(acc):
    return acc, acc


def _mmc(dims, n_pairs=1, epilogue=None):
    def compute(*refs):
        acc = None
        for k in range(n_pairs):
            d = _dot(refs[2 * k][...].astype(BF16), refs[2 * k + 1][...].astype(BF16), dims)
            acc = d if acc is None else acc + d
        ext = [r[...] for r in refs[2 * n_pairs:]]
        return epilogue(acc, *ext) if epilogue is not None else (acc,)

    return compute


def _res(w):
    return (w, w.shape, functools.partial(lambda i, j, nd: (0,) * nd, nd=w.ndim))


def _mmc_blocks(nblk, dims, rhs_block, epilogue=None):
    def compute(in_refs, out_refs):
        a = in_refs[0][...].astype(BF16)
        for d in range(nblk):
            acc = _dot(a, rhs_block(in_refs[1], d).astype(BF16), dims)
            n = acc.shape[1]
            ext = [r[:, d * n:(d + 1) * n] for r in in_refs[2:]]
            res = epilogue(acc, *ext) if epilogue is not None else (acc,)
            for o_ref, r in zip(out_refs, res):
                o_ref[:, d * n:(d + 1) * n] = r.astype(o_ref.dtype)

    return compute


def _rowwise(name, body, rows, consts, out_rows, out_accs=(), tr=512, deps=()):
    T = rows[0][0].shape[0]
    tr = min(tr, T)
    deps = _deps(deps)
    nr, ncn, no, nd = len(rows), len(consts), len(out_rows), len(deps)

    def kern(*refs):
        accs = refs[nr + ncn + nd + no:]
        if accs:
            @pl.when(pl.program_id(0) == 0)
            def _():
                for a in accs:
                    a[...] = jnp.zeros(a.shape, a.dtype)
        body(refs[:nr], refs[nr:nr + ncn], refs[nr + ncn + nd:nr + ncn + nd + no], accs)

    in_specs = [pl.BlockSpec((tr, w), functools.partial(lambda i, cb: (i, cb), cb=cb)) for _, w, cb in rows]
    in_specs += [pl.BlockSpec(c.shape, functools.partial(lambda i, nd: (0,) * nd, nd=c.ndim), pipeline_mode=pl.Buffered(1))
                 for c in consts]
    in_specs += [ANY_SPEC] * nd
    out_specs = [pl.BlockSpec((tr, w), lambda i: (i, 0)) for w, _ in out_rows]
    out_specs += [pl.BlockSpec(s, functools.partial(lambda i, nd: (0,) * nd, nd=len(s))) for s, _ in out_accs]
    out_shape = [pltpu.HBM((T, w), dt) for w, dt in out_rows]
    out_shape += [pltpu.HBM(s, dt) for s, dt in out_accs]
    nbytes = sum(_nbytes((tr, w), a.dtype) for a, w, _ in rows) + sum(_nbytes(c.shape, c.dtype) for c in consts)
    nbytes += sum(_nbytes((tr, w), dt) for w, dt in out_rows) + sum(_nbytes(s, dt) for s, dt in out_accs)
    res = pl.pallas_call(
        kern, name=name, grid=(T // tr,), in_specs=in_specs, out_specs=out_specs, out_shape=out_shape,
        compiler_params=pltpu.CompilerParams(dimension_semantics=("arbitrary",), vmem_limit_bytes=_vmem(nbytes)),
    )(*[_hbm(a) for a, _, _ in rows], *[_hbm(c) for c in consts], *deps)
    return res if len(res) > 1 else res[0]


def _full(a):
    return (a, a.shape[1], 0)


def _ln_stats(y):
    mu = jnp.mean(y, axis=-1, keepdims=True)
    yc = y - mu
    r = lax.rsqrt(jnp.mean(yc * yc, axis=-1, keepdims=True) + EPS)
    return yc * r, r


def _row_halves(n):
    return [slice(0, n // 2), slice(n // 2, n)] if n >= 256 else [slice(0, n)]


def _ln_back(dh, xh, r, gain, dg_ref, db_ref):
    dg_ref[...] += jnp.sum(dh * xh, axis=0, keepdims=True)
    db_ref[...] += jnp.sum(dh, axis=0, keepdims=True)
    dx = dh * gain
    return r * (dx - jnp.mean(dx, axis=-1, keepdims=True) - xh * jnp.mean(dx * xh, axis=-1, keepdims=True))


def _residual(resid, prev):
    if prev is None:
        return resid
    g_ref, b_ref, layer = prev
    return _ln_stats(resid)[0] * g_ref[layer:layer + 1, :] + b_ref[layer:layer + 1, :]


def _proj_ln(name, acts, weights, resid, g, b, layer, prev=None, deps=()):
    n = len(acts)

    def body(rows, consts, outs, accs):
        acc = None
        for k in range(n):
            d = _dot(rows[k][...].astype(BF16), consts[k][...], NN)
            acc = d if acc is None else acc + d
        h_in = _residual(rows[n][...], None if prev is None else (consts[n + 2], consts[n + 3], prev[2]))
        y = ALPHA * h_in + acc
        xh, _ = _ln_stats(y)
        outs[0][...] = y
        outs[1][...] = (xh * consts[n][layer:layer + 1, :] + consts[n + 1][layer:layer + 1, :]).astype(BF16)

    consts = [*weights, g, b] + ([] if prev is None else [prev[0], prev[1]])
    return _rowwise(name, body, [_full(a) for a in acts] + [_full(resid)], consts, [(D_MODEL, F32), (D_MODEL, BF16)], tr=TM, deps=deps)


def _proj_ln_loss(name, act, w2, resid, g, b, layer, target, prev):
    def body(rows, consts, outs, accs):
        y = ALPHA * _residual(rows[1][...], (consts[3], consts[4], prev[2])) + _dot(rows[0][...], consts[0][...], NN)
        xh, r = _ln_stats(y)
        gain = consts[1][layer:layer + 1, :]
        err = xh * gain + consts[2][layer:layer + 1, :] - rows[2][...]
        accs[0][...] += jnp.sum(err * err, axis=0, keepdims=True)
        dy = _ln_back(err * (1.0 / D_MODEL), xh, r, gain, accs[1], accs[2])
        outs[0][...] = dy
        outs[1][...] = dy.astype(BF16)

    return _rowwise(name, body, [_full(act), _full(resid), _full(target)], [w2, g, b, prev[0], prev[1]],
                    [(D_MODEL, F32), (D_MODEL, BF16)], [((1, D_MODEL), F32)] * 3, tr=TM)


def _dh_ln_back(name, da, w, dy_next, y, g, layer, proj=(), deps=()):
    def body(rows, consts, outs, accs):
        n = consts[0].shape[2]
        for sl in _row_halves(rows[0].shape[0]):
            acc = ALPHA * rows[1][sl, :]
            for d in range(N_DEV):
                acc = acc + _dot(rows[0][sl, d * n:(d + 1) * n], consts[0][d], NT)
            xh, r = _ln_stats(rows[2][sl, :])
            dy = _ln_back(acc, xh, r, consts[1][layer:layer + 1, :], accs[0], accs[1])
            outs[0][sl, :] = dy
            dy_bf = dy.astype(BF16)
            outs[1][sl, :] = dy_bf
            off = 0
            for k, p in enumerate(proj):
                outs[2][sl, off:off + p.shape[0]] = _dot(dy_bf, consts[2 + k][...], NT).astype(BF16)
                off += p.shape[0]

    out_rows = [(D_MODEL, F32), (D_MODEL, BF16)] + ([(sum(p.shape[0] for p in proj), BF16)] if proj else [])
    return _rowwise(name, body, [_full(da), _full(dy_next), _full(y)], [w, g, *proj], out_rows,
                    [((1, D_MODEL), F32)] * 2, tr=TM, deps=deps)


def _relu2_epilogue(acc):
    a = jnp.maximum(acc, 0.0)
    return acc, a * a


def _mlp_up(tag, h_bf, w1):
    T = h_bf.shape[0]
    tm = min(TM, T)
    return _tiled(f"{tag}_ff1", (1, T // tm), [_rb(h_bf, tm), _res(w1)],
                  [_out(T, D_FF, BF16, tm, D_FF), _out(T, D_FF, BF16, tm, D_FF)],
                  _mmc_blocks(N_DEV, NN, lambda w, d: w[d], epilogue=_relu2_epilogue), direct=True)


def _mlp_bwd_w(tag, h_bf, a, act, dff_bf, w2, deps=()):
    T = h_bf.shape[0]
    tm = min(TM, T)
    da = _tiled(f"{tag}_dact", (1, T // tm), [_rb(dff_bf, tm), _res(w2), _rb(a, tm)], [_out(T, D_FF, BF16, tm, D_FF)],
                _mmc_blocks(N_DEV, NT, lambda w, d: w[d], epilogue=lambda acc, a_t: (acc * 2.0 * jnp.maximum(a_t.astype(F32), 0.0),)),
                direct=True, deps=deps)
    dw2 = _tiled(f"{tag}_dw2", (1, D_FF // TM), [_tl(act, TM), _res(dff_bf)],
                 [_out(D_FF, D_MODEL, F32, TM, D_MODEL), _out(D_FF, D_MODEL, BF16, TM, D_MODEL)], _mmc(TN_, epilogue=_twice))
    dw1 = _tiled(f"{tag}_dw1", (N_DEV, 1), [_res(h_bf), _cw(da, TN)],
                 [_out_dev(D_MODEL, TN, D_MODEL), _out_dev(D_MODEL, TN, D_MODEL, BF16)], _mmc(TN_, epilogue=_twice))
    return da, dw1, [a.reshape(N_DEV, D_FF // N_DEV, D_MODEL) for a in dw2]


def _rope_tables(positions_col, invf_lane):
    def body(rows, consts, outs, accs):
        ang = rows[0][...].astype(F32) * consts[0][...]
        c, s = jnp.cos(ang), jnp.sin(ang)
        lane = lax.broadcasted_iota(jnp.int32, ang.shape, 1)
        outs[0][...] = jnp.where(lane < 64, 1.0, jnp.where(lane < 96, c, 0.0))
        outs[1][...] = jnp.where((lane >= 64) & (lane < 80), -s, 0.0)
        outs[2][...] = jnp.where((lane >= 80) & (lane < 96), s, 0.0)

    return _rowwise("rope_tables", body, [_full(positions_col)], [invf_lane], [(HEAD_W, F32)] * 3)


def _rope(x, c, s1, s2):
    return x * c + pltpu.roll(x, 112, 1) * s1 + pltpu.roll(x, 16, 1) * s2


def _rope_t(dx, c, s1, s2):
    return dx * c + pltpu.roll(dx * s1, 16, 1) + pltpu.roll(dx * s2, 112, 1)


def _rms(c):
    r = lax.rsqrt(jnp.mean(c * c, axis=-1, keepdims=True) + EPS)
    return c * r, r


def _rope_heads(x, c, s1, s2, fn):
    return jnp.concatenate([fn(x[:, h * HEAD_W:(h + 1) * HEAD_W], c, s1, s2) for h in range(HEADS)], axis=1)


def _mla_in(x, wm, ws, tabs, gq, gkv, deps=()):
    def body(rows, consts, outs, accs):
        xb = rows[0][...].astype(BF16)
        zm = _dot(xb, consts[0][...], NN)
        outs[0][...] = zm
        outs[1][...] = _dot(xb, consts[1][...], NN)
        outs[2][...] = (_rms(zm[:, 0:256])[0] * consts[2][...]).astype(BF16)
        outs[3][...] = (_rms(zm[:, 256:512])[0] * consts[3][...]).astype(BF16)
        outs[4][...] = _rope(zm[:, 512:640], rows[1][...], rows[2][...], rows[3][...])

    return _rowwise("l0_in", body, [_full(x)] + [_full(t) for t in tabs], [wm, ws, gq, gkv],
                    [(640, F32), (1024, F32), (256, BF16), (256, BF16), (HEAD_W, F32)], deps=deps)


def _mla_qkv(cqn, ckvn, kr_rot, tabs, wq, wk, wv):
    def body(rows, consts, outs, accs):
        c, s1, s2 = rows[3][...], rows[4][...], rows[5][...]
        outs[0][...] = _rope_heads(_dot(rows[0][...], consts[0][...], NN), c, s1, s2, _rope).astype(BF16)
        outs[1][...] = (_dot(rows[1][...], consts[1][...], NN) + jnp.concatenate([rows[2][...]] * HEADS, axis=1)).astype(BF16)
        outs[2][...] = _dot(rows[1][...], consts[2][...], NN).astype(BF16)

    rows = [_full(cqn), _full(ckvn), _full(kr_rot)] + [_full(t) for t in tabs]
    return _rowwise("l0_qkv", body, rows, [wq, wk, wv], [(HEADS * HEAD_W, BF16)] * 3)


def _mla_back(zm, cqn, ckvn, tabs, gq, gkv, wq, wk, wv, dq, dk, dv):
    def body(rows, consts, outs, accs):
        c, s1, s2 = rows[4][...], rows[5][...], rows[6][...]
        dk_t, dv_bf = rows[8][...], rows[9][...].astype(BF16)
        dq_bf = _rope_heads(rows[7][...], c, s1, s2, _rope_t).astype(BF16)
        dk_bf = dk_t.astype(BF16)
        accs[0][...] += _dot(rows[2][...], dq_bf, TN_)
        accs[1][...] += _dot(rows[3][...], dk_bf, TN_)
        accs[2][...] += _dot(rows[3][...], dv_bf, TN_)
        dlat = [_dot(dq_bf, consts[2][...], NT), _dot(dk_bf, consts[3][...], NT) + _dot(dv_bf, consts[4][...], NT)]
        for k in range(2):
            ch, r = _rms(rows[k][...])
            accs[3 + k][...] += jnp.sum(dlat[k] * ch, axis=0, keepdims=True)
            dc = dlat[k] * consts[k][...]
            outs[0][:, 256 * k:256 * (k + 1)] = (r * (dc - ch * jnp.mean(dc * ch, axis=-1, keepdims=True))).astype(BF16)
        dks = dk_t[:, 0:HEAD_W]
        for h in range(1, HEADS):
            dks = dks + dk_t[:, h * HEAD_W:(h + 1) * HEAD_W]
        lane = lax.broadcasted_iota(jnp.int32, dks.shape, 1)
        dks = jnp.where((lane >= 64) & (lane < 96), dks, 0.0)
        outs[0][:, 512:640] = _rope_t(dks, c, s1, s2).astype(BF16)

    rows = [(zm, 256, 0), (zm, 256, 1), _full(cqn), _full(ckvn)] + [_full(t) for t in tabs] + [_full(dq), _full(dk), _full(dv)]
    wide = HEADS * HEAD_W
    return _rowwise("l0_mla_back", body, rows, [gq, gkv, wq, wk, wv], [(640, BF16)],
                    [((MLA_LORA, wide), F32)] * 3 + [((1, MLA_LORA), F32)] * 2, tr=256)


def _in_back(x, dzm, dzs, dy, wm, ws, deps=()):
    def body(rows, consts, outs, accs):
        dzm_t, dzs_t = rows[1][...], rows[2][...]
        outs[0][...] = _dot(dzm_t, consts[0][...], NT) + _dot(dzs_t, consts[1][...], NT) + ALPHA * rows[3][...]
        xb = rows[0][...].astype(BF16)
        accs[0][...] += _dot(xb, dzm_t, TN_)
        accs[1][...] += _dot(xb, dzs_t, TN_)

    return _rowwise("l0_in_back", body, [_full(x), _full(dzm), _full(dzs), _full(dy)], [wm, ws], [(D_MODEL, F32)],
                    [((D_MODEL, 640), F32), ((D_MODEL, 1024), F32)], deps=deps)


def _out_weight_grads(o_att, b_out, dy_bf):
    def body(rows, consts, outs, accs):
        d = rows[2][...]
        accs[0][...] += _dot(rows[0][...].astype(BF16), d, TN_)
        accs[1][...] += _dot(rows[1][...], d, TN_)

    return _rowwise("l0_dw_out", body, [_full(o_att), _full(b_out), _full(dy_bf)], [], [],
                    [((HEADS * HEAD_W, D_MODEL), F32), ((SGU_DIM, D_MODEL), F32)])


def _attn_block(T):
    return min(1024, T)


def _attn_fwd(q, k, v):
    T = q.shape[0]
    BQ = _attn_block(T)
    nq = T // BQ

    def kern(q_ref, k_ref, v_ref, o_ref, lse_ref):
        def step(i, j, carry, masked):
            m, l, acc = carry
            qb = q_ref[pl.ds(pl.multiple_of(i * BQ, BQ), BQ), :]
            kb = k_ref[pl.ds(pl.multiple_of(j * BQ, BQ), BQ), :]
            vb = v_ref[pl.ds(pl.multiple_of(j * BQ, BQ), BQ), :]
            s = _dot(qb, kb, NT) * MLA_SCALE
            if masked:
                row = lax.broadcasted_iota(jnp.int32, s.shape, 0)
                col = lax.broadcasted_iota(jnp.int32, s.shape, 1)
                s = jnp.where(col <= row, s, -1e30)
            m_new = jnp.maximum(m, jnp.max(s, axis=-1, keepdims=True))
            p = jnp.exp(s - m_new)
            a = jnp.exp(m - m_new)
            l = a * l + jnp.sum(p, axis=-1, keepdims=True)
            acc = a * acc + _dot(p.astype(BF16), vb, NN)
            return m_new, l, acc

        def qloop(i, _):
            init = (jnp.full((BQ, 1), -1e30, F32), jnp.zeros((BQ, 1), F32), jnp.zeros((BQ, HEAD_W), F32))
            carry = lax.fori_loop(0, i, lambda j, c: step(i, j, c, False), init)
            m, l, acc = step(i, i, carry, True)
            rows = pl.ds(pl.multiple_of(i * BQ, BQ), BQ)
            o_ref[rows, :] = acc / l
            lse_ref[0, rows, :] = m + jnp.log(l)
            return 0

        lax.fori_loop(0, nq, qloop, 0)

    head = pl.BlockSpec((T, HEAD_W), lambda h: (0, h))
    nbytes = 3 * _nbytes((T, HEAD_W), BF16) + _nbytes((T, HEAD_W), F32) + _nbytes((T, 128), F32)
    return pl.pallas_call(
        kern, name="attn_fwd", grid=(HEADS,), in_specs=[head, head, head],
        out_specs=[head, pl.BlockSpec((1, T, 1), lambda h: (h, 0, 0))],
        out_shape=[pltpu.HBM((T, HEADS * HEAD_W), F32), pltpu.HBM((HEADS, T, 1), F32)],
        compiler_params=pltpu.CompilerParams(dimension_semantics=("parallel",), vmem_limit_bytes=_vmem(nbytes)),
    )(_hbm(q), _hbm(k), _hbm(v))


def _attn_bwd(q, k, v, o, lse, dcat, deps=()):
    T = q.shape[0]
    BQ = _attn_block(T)
    nq = T // BQ
    deps = _deps(deps)

    def kern(q_ref, k_ref, v_ref, o_ref, lse_ref, do_ref, *rest):
        dq_ref, dk_ref, dv_ref, dd_ref = rest[len(deps):]
        dq_ref[...] = jnp.zeros(dq_ref.shape, F32)

        def dloop(i, _):
            rows = pl.ds(pl.multiple_of(i * BQ, BQ), BQ)
            dd_ref[rows, :] = jnp.sum(do_ref[rows, :].astype(F32) * o_ref[rows, :], axis=-1, keepdims=True)
            return 0

        lax.fori_loop(0, nq, dloop, 0)

        def step(j, i, carry, masked):
            dk_acc, dv_acc = carry
            rq = pl.ds(pl.multiple_of(i * BQ, BQ), BQ)
            rk = pl.ds(pl.multiple_of(j * BQ, BQ), BQ)
            qb, kb, vb, dob = q_ref[rq, :], k_ref[rk, :], v_ref[rk, :], do_ref[rq, :]
            s = _dot(qb, kb, NT) * MLA_SCALE
            p = jnp.exp(s - lse_ref[0, rq, :])
            if masked:
                row = lax.broadcasted_iota(jnp.int32, s.shape, 0)
                col = lax.broadcasted_iota(jnp.int32, s.shape, 1)
                p = jnp.where(col <= row, p, 0.0)
            dp = _dot(dob, vb, NT)
            ds = (p * (dp - dd_ref[rq, :]) * MLA_SCALE).astype(BF16)
            dv_acc = dv_acc + _dot(p.astype(BF16), dob, TN_)
            dk_acc = dk_acc + _dot(ds, qb, TN_)
            dq_ref[rq, :] += _dot(ds, kb, NN)
            return dk_acc, dv_acc

        def kloop(j, _):
            init = (jnp.zeros((BQ, HEAD_W), F32), jnp.zeros((BQ, HEAD_W), F32))
            carry = step(j, j, init, True)
            dk_acc, dv_acc = lax.fori_loop(j + 1, nq, lambda i, c: step(j, i, c, False), carry)
            rk = pl.ds(pl.multiple_of(j * BQ, BQ), BQ)
            dk_ref[rk, :] = dk_acc
            dv_ref[rk, :] = dv_acc
            return 0

        lax.fori_loop(0, nq, kloop, 0)

    head = pl.BlockSpec((T, HEAD_W), lambda h: (0, h))
    nbytes = 4 * _nbytes((T, HEAD_W), BF16) + 5 * _nbytes((T, HEAD_W), F32) + 2 * _nbytes((T, 128), F32)
    return pl.pallas_call(
        kern, name="attn_bwd", grid=(HEADS,),
        in_specs=[head, head, head, head, pl.BlockSpec((1, T, 1), lambda h: (h, 0, 0)), head] + [ANY_SPEC] * len(deps),
        out_specs=[head, head, head],
        out_shape=[pltpu.HBM((T, HEADS * HEAD_W), F32)] * 3,
        scratch_shapes=[pltpu.VMEM((T, 1), F32)],
        compiler_params=pltpu.CompilerParams(dimension_semantics=("parallel",), vmem_limit_bytes=_vmem(nbytes)),
    )(*[_hbm(a) for a in (q, k, v, o, lse, dcat)], *deps)


def _sgu_common(u, v, ln_g, ln_b):
    ua, tu = _gelu(u)
    va, tv = _gelu(v)
    vh, r = _ln_stats(va)
    return ua, tu, tv, vh, r, vh * ln_g + ln_b


def _tril_mask(n):
    return lax.broadcasted_iota(jnp.int32, (n, n), 1) <= lax.broadcasted_iota(jnp.int32, (n, n), 0)


def _sgu_fwd(zs, ln_g, ln_b, w, bias_full):
    def body(rows, consts, outs, accs):
        ua, _, _, _, _, vn = _sgu_common(rows[0][...], rows[1][...], consts[0][...], consts[1][...])
        vn = vn.astype(BF16)
        tri = _tril_mask(SGU_CHUNK)
        for g in range(SGU_G):
            wg = jnp.where(tri, consts[2][0, g], 0.0).astype(BF16)
            cols = slice(g * 128, (g + 1) * 128)
            for c in range(ua.shape[0] // SGU_CHUNK):
                rws = slice(c * SGU_CHUNK, (c + 1) * SGU_CHUNK)
                mixed = _dot(wg, vn[rws, cols], NN) + consts[3][:, cols]
                outs[0][rws, cols] = (ua[rws, cols] * mixed).astype(BF16)

    return _rowwise("sgu_fwd", body, [(zs, 512, 0), (zs, 512, 1)], [ln_g, ln_b, w, bias_full], [(SGU_DIM, BF16)])


def _sgu_bwd(zs, dcat, ln_g, ln_b, w, bias_full):
    def body(rows, consts, outs, accs):
        u, v = rows[0][...], rows[1][...]
        ua, tu, tv, vh, r, vn = _sgu_common(u, v, consts[0][...], consts[1][...])
        dout = rows[2][...].astype(F32)
        vn_bf = vn.astype(BF16)
        tri = _tril_mask(SGU_CHUNK)
        dmixed = (dout * ua)
        dmixed_bf = dmixed.astype(BF16)
        ones = jnp.ones((8, SGU_CHUNK), F32)
        dvn_cols, mixed_cols = [], []
        for g in range(SGU_G):
            wg = jnp.where(tri, consts[2][0, g], 0.0).astype(BF16)
            cols = slice(g * 128, (g + 1) * 128)
            dvn_rows, mixed_rows = [], []
            dw = jnp.zeros((SGU_CHUNK, SGU_CHUNK), F32)
            dmix_sum = jnp.zeros((SGU_CHUNK, 128), F32)
            for c in range(u.shape[0] // SGU_CHUNK):
                rws = slice(c * SGU_CHUNK, (c + 1) * SGU_CHUNK)
                mixed_rows.append(_dot(wg, vn_bf[rws, cols], NN) + consts[3][:, cols])
                dvn_rows.append(_dot(wg, dmixed_bf[rws, cols], TN_))
                dw = dw + _dot(dmixed_bf[rws, cols], vn_bf[rws, cols], NT)
                dmix_sum = dmix_sum + dmixed[rws, cols]
            accs[0][g] += jnp.where(tri, dw, 0.0)
            accs[3][g:g + 1, :] += _dot(ones, dmix_sum, NT, precision=HIGHEST)[0:1, :]
            dvn_cols.append(jnp.concatenate(dvn_rows, axis=0))
            mixed_cols.append(jnp.concatenate(mixed_rows, axis=0))
        dvn = jnp.concatenate(dvn_cols, axis=1)
        mixed = jnp.concatenate(mixed_cols, axis=1)
        accs[1][...] += jnp.sum(dvn * vh, axis=0, keepdims=True)
        accs[2][...] += jnp.sum(dvn, axis=0, keepdims=True)
        dvh = dvn * consts[0][...]
        dva = r * (dvh - jnp.mean(dvh, axis=-1, keepdims=True) - vh * jnp.mean(dvh * vh, axis=-1, keepdims=True))
        outs[0][:, 0:512] = (dout * mixed * _gelu_grad(u, tu)).astype(BF16)
        outs[0][:, 512:1024] = (dva * _gelu_grad(v, tv)).astype(BF16)

    return _rowwise("sgu_bwd", body, [(zs, 512, 0), (zs, 512, 1), (dcat, 512, 2)], [ln_g, ln_b, w, bias_full], [(1024, BF16)],
                    [((SGU_G, 128, 128), F32), ((1, SGU_DIM), F32), ((1, SGU_DIM), F32), ((SGU_G, 128), F32)], tr=256)


def _lower_bound(hg_lb):
    a0, a1 = hg_lb[0:1, :], hg_lb[1:2, :]
    m = jnp.maximum(a0, a1)
    e0, e1 = jnp.exp(a0 - m), jnp.exp(a1 - m)
    s0, s1 = e0 / (e0 + e1), e1 / (e0 + e1)
    return (s0 + s1) - s0, s0, s1


def _prefix_rows(x, reverse=False):
    n = x.shape[0]
    row = lax.broadcasted_iota(jnp.int32, x.shape, 0)
    s = 1
    while s < n:
        if reverse:
            x = x + jnp.where(row < n - s, pltpu.roll(x, n - s, 0), 0.0)
        else:
            x = x + jnp.where(row >= s, pltpu.roll(x, s, 0), 0.0)
        s *= 2
    return x


def _hg_gates(qr, fr, lb):
    C = qr.shape[0]
    sq = _sig(qr)
    qf = qr * sq
    sf = _sig(fr)
    gate = lb + (1.0 - lb) * sf
    kk = 1.0 - gate
    tri = _tril_mask(C)
    b = _prefix_rows(jnp.log(gate))
    bref = b[C // 2 - 1:C // 2, :]
    bl = b[C - 1:C, :]
    e_b = jnp.exp(b)
    e_q = jnp.exp(b - bref)
    e_k = jnp.exp(bref - b)
    e_lb = jnp.exp(bl - b)
    return dict(sq=sq, qf=qf, sf=sf, gate=gate, kk=kk, tri=tri, bl=bl, e_b=e_b, e_q=e_q, e_k=e_k, e_lb=e_lb)


def _hgrn_fwd(z1, hg_lb, gnorm):
    T = z1.shape[0]
    C = min(HG_CHUNK, T)
    nc = T // C
    ns = HG_CHUNKS_PER_STEP if nc % HG_CHUNKS_PER_STEP == 0 else 1
    R = ns * C

    def kern(q_ref, f_ref, i_ref, g_ref, lb_ref, gn_ref, o_ref, hg_ref, st_ref, s_scr):
        @pl.when(pl.program_id(0) == 0)
        def _():
            s_scr[...] = jnp.zeros(s_scr.shape, F32)

        lb_all, _, _ = _lower_bound(lb_ref[...])
        for sub in range(ns):
            rows = slice(sub * C, (sub + 1) * C)
            st_ref[sub] = s_scr[...]
            for h in range(HEADS):
                cols = slice(h * HEAD_W, (h + 1) * HEAD_W)
                t = _hg_gates(q_ref[rows, cols], f_ref[rows, cols], lb_all[:, cols])
                v_bf = i_ref[rows, cols].astype(BF16)
                st = s_scr[h]
                a = jnp.where(t["tri"], _dot((t["qf"] * t["e_q"]).astype(BF16), (t["kk"] * t["e_k"]).astype(BF16), NT), 0.0)
                o = _dot(a.astype(BF16), v_bf, NN) + _dot((t["qf"] * t["e_b"]).astype(BF16), st.astype(BF16), NT)
                s_scr[h] = st * jnp.exp(t["bl"]) + _dot(v_bf, (t["kk"] * t["e_lb"]).astype(BF16), TN_)
                o_ref[rows, cols] = o
                gr = g_ref[rows, cols]
                r = lax.rsqrt(jnp.mean(o * o, axis=-1, keepdims=True) + EPS)
                hg_ref[rows, cols] = (o * r * gn_ref[:, cols] * (gr * _sig(gr))).astype(BF16)

    seg = lambda k: pl.BlockSpec((R, D_MODEL), functools.partial(lambda n, k: (n, k), k=k))
    row = pl.BlockSpec((R, D_MODEL), lambda n: (n, 0))
    nbytes = 6 * _nbytes((R, D_MODEL), F32) + (2 + ns) * _nbytes((HEADS, 128, 128), F32)
    return pl.pallas_call(
        kern, name="hgrn_fwd", grid=(nc // ns,),
        in_specs=[seg(0), seg(1), seg(2), seg(3), pl.BlockSpec((2, D_MODEL), lambda n: (0, 0)),
                  pl.BlockSpec((1, D_MODEL), lambda n: (0, 0))],
        out_specs=[row, row, pl.BlockSpec((ns, HEADS, 128, 128), lambda n: (n, 0, 0, 0))],
        out_shape=[pltpu.HBM((T, D_MODEL), F32), pltpu.HBM((T, D_MODEL), BF16),
                   pltpu.HBM((nc, HEADS, 128, 128), F32)],
        scratch_shapes=[pltpu.VMEM((HEADS, 128, 128), F32)],
        compiler_params=pltpu.CompilerParams(dimension_semantics=("arbitrary",), vmem_limit_bytes=_vmem(nbytes)),
    )(*[_hbm(a) for a in (z1, z1, z1, z1, hg_lb, gnorm)])


def _hgrn_bwd(z1, o_pre, dhg, states, hg_lb, gnorm):
    T = z1.shape[0]
    C = min(HG_CHUNK, T)
    nc = T // C
    ns = HG_CHUNKS_PER_STEP if nc % HG_CHUNKS_PER_STEP == 0 else 1
    R, steps = ns * C, nc // ns

    def kern(q_ref, f_ref, i_ref, g_ref, o_ref, dhg_ref, st_ref, lb_ref, gn_ref, dz_ref, dlb_ref, dgn_ref, ds_scr, dlb_scr):
        n = pl.program_id(0)

        @pl.when(n == 0)
        def _():
            ds_scr[...] = jnp.zeros(ds_scr.shape, F32)
            dlb_scr[...] = jnp.zeros(dlb_scr.shape, F32)
            dgn_ref[...] = jnp.zeros(dgn_ref.shape, F32)

        lb_all, s0, s1 = _lower_bound(lb_ref[...])
        for sub in reversed(range(ns)):
            rows = slice(sub * C, (sub + 1) * C)
            for h in range(HEADS):
                cols = slice(h * HEAD_W, (h + 1) * HEAD_W)
                lb = lb_all[:, cols]
                qr, fr = q_ref[rows, cols], f_ref[rows, cols]
                t = _hg_gates(qr, fr, lb)
                tri = t["tri"]
                v_bf = i_ref[rows, cols].astype(BF16)
                st_bf = st_ref[sub, h].astype(BF16)
                dst = ds_scr[h]
                dst_bf = dst.astype(BF16)
                o = o_ref[rows, cols]
                gr = g_ref[rows, cols]
                sg = _sig(gr)
                sil = gr * sg
                gn = gn_ref[:, cols]
                r = lax.rsqrt(jnp.mean(o * o, axis=-1, keepdims=True) + EPS)
                on = o * r
                dh = dhg_ref[rows, cols].astype(F32)
                dgn_ref[:, cols] += jnp.sum(dh * on * sil, axis=0, keepdims=True)
                dg = dh * on * gn * (sg * (1.0 + gr * (1.0 - sg)))
                don = dh * gn * sil
                do_bf = (r * (don - on * jnp.mean(don * on, axis=-1, keepdims=True))).astype(BF16)
                qe = (t["qf"] * t["e_q"]).astype(BF16)
                ke = (t["kk"] * t["e_k"]).astype(BF16)
                qb = (t["qf"] * t["e_b"]).astype(BF16)
                kh_bf = (t["kk"] * t["e_lb"]).astype(BF16)
                a_bf = jnp.where(tri, _dot(qe, ke, NT), 0.0).astype(BF16)
                da_bf = jnp.where(tri, _dot(do_bf, v_bf, NT), 0.0).astype(BF16)
                dv = _dot(a_bf, do_bf, TN_) + _dot(kh_bf, dst_bf, NT)
                dqe = _dot(da_bf, ke, NN)
                dqb = _dot(do_bf, st_bf, NN)
                dke = _dot(da_bf, qe, TN_)
                dkh = _dot(v_bf, dst_bf, NN)
                dqf = dqe * t["e_q"] + dqb * t["e_b"]
                dkk = dke * t["e_k"] + dkh * t["e_lb"]
                kh_r = kh_bf.astype(F32)
                db = qe.astype(F32) * dqe - ke.astype(F32) * dke + qb.astype(F32) * dqb - kh_r * dkh
                e_bl = jnp.exp(t["bl"])
                dbl = jnp.sum(dkh * kh_r, axis=0, keepdims=True) + e_bl * jnp.sum(st_ref[sub, h] * dst, axis=0, keepdims=True)
                dlg = _prefix_rows(db, reverse=True) + dbl
                ds_scr[h] = dst * e_bl + _dot(do_bf, qb, TN_)
                dgate = dlg / t["gate"] - dkk
                sf = t["sf"]
                dlb_scr[:, cols] += jnp.sum(dgate * (1.0 - sf), axis=0, keepdims=True)
                df = dgate * (1.0 - lb) * sf * (1.0 - sf)
                dq = dqf * (t["sq"] * (1.0 + qr * (1.0 - t["sq"])))
                dz_ref[rows, cols] = dq.astype(BF16)
                dz_ref[rows, D_MODEL + h * HEAD_W:D_MODEL + (h + 1) * HEAD_W] = df.astype(BF16)
                dz_ref[rows, 2 * D_MODEL + h * HEAD_W:2 * D_MODEL + (h + 1) * HEAD_W] = dv.astype(BF16)
                dz_ref[rows, 3 * D_MODEL + h * HEAD_W:3 * D_MODEL + (h + 1) * HEAD_W] = dg.astype(BF16)

        @pl.when(n == steps - 1)
        def _():
            d = s0 * s1 * dlb_scr[...]
            dlb_ref[0:1, :] = -d
            dlb_ref[1:2, :] = d

    seg = lambda k: pl.BlockSpec((R, D_MODEL), functools.partial(lambda n, k: (steps - 1 - n, k), k=k))
    nbytes = 6 * _nbytes((R, D_MODEL), F32) + _nbytes((R, 4 * D_MODEL), BF16) + (2 + ns) * _nbytes((HEADS, 128, 128), F32)
    return pl.pallas_call(
        kern, name="hgrn_bwd", grid=(steps,),
        in_specs=[seg(0), seg(1), seg(2), seg(3), seg(0), seg(0),
                  pl.BlockSpec((ns, HEADS, 128, 128), lambda n: (steps - 1 - n, 0, 0, 0)),
                  pl.BlockSpec((2, D_MODEL), lambda n: (0, 0)), pl.BlockSpec((1, D_MODEL), lambda n: (0, 0))],
        out_specs=[pl.BlockSpec((R, 4 * D_MODEL), lambda n: (steps - 1 - n, 0)),
                   pl.BlockSpec((2, D_MODEL), lambda n: (0, 0)), pl.BlockSpec((1, D_MODEL), lambda n: (0, 0))],
        out_shape=[pltpu.HBM((T, 4 * D_MODEL), BF16), pltpu.HBM((2, D_MODEL), F32),
                   pltpu.HBM((1, D_MODEL), F32)],
        scratch_shapes=[pltpu.VMEM((HEADS, 128, 128), F32), pltpu.VMEM((1, D_MODEL), F32)],
        compiler_params=pltpu.CompilerParams(dimension_semantics=("arbitrary",), vmem_limit_bytes=_vmem(nbytes)),
    )(*[_hbm(a) for a in (z1, z1, z1, z1, o_pre, dhg, states, hg_lb, gnorm)])


def _prep_weights(gw):
    w_in_e = gw["w_in_e"].transpose(1, 0, 2).reshape(D_MODEL, 1568)
    kr = jnp.pad(w_in_e[:, 512:544], ((0, 0), (64, 32)))
    wm = jnp.concatenate([w_in_e[:, 0:512], kr], axis=1)
    ws = w_in_e[:, 544:1568]
    w_qb = gw["w_qb"].transpose(1, 0, 2).reshape(MLA_LORA, HEADS, 96)
    wq = jnp.pad(w_qb, ((0, 0), (0, 0), (0, 32))).reshape(MLA_LORA, HEADS * HEAD_W)
    kvb = gw["w_kvb"].transpose(1, 0, 2).reshape(MLA_LORA, HEADS, 128)
    wk = jnp.pad(kvb[:, :, :64], ((0, 0), (0, 0), (0, 64))).reshape(MLA_LORA, HEADS * HEAD_W)
    wv = jnp.pad(kvb[:, :, 64:], ((0, 0), (0, 0), (0, 64))).reshape(MLA_LORA, HEADS * HEAD_W)
    w_out_e = gw["w_out_e"].reshape(D_MODEL, D_MODEL)
    woa = jnp.pad(w_out_e[:512].reshape(HEADS, 64, D_MODEL), ((0, 0), (0, 64), (0, 0))).reshape(HEADS * HEAD_W, D_MODEL)
    return dict(wm=wm, ws=ws, wq=wq, wk=wk, wv=wv, woa=woa, wob=w_out_e[512:])


def _unprep_grads(g):
    dwm, dws = g["wm"], g["ws"]
    d_in_e = jnp.concatenate([dwm[:, 0:512], dwm[:, 512 + 64:512 + 96], dws], axis=1)
    d_qb = g["wq"].reshape(MLA_LORA, HEADS, HEAD_W)[:, :, :96].reshape(MLA_LORA, HEADS * 96)
    dk = g["wk"].reshape(MLA_LORA, HEADS, HEAD_W)[:, :, :64]
    dv = g["wv"].reshape(MLA_LORA, HEADS, HEAD_W)[:, :, :64]
    d_kvb = jnp.concatenate([dk, dv], axis=2).reshape(MLA_LORA, HEADS * 128)
    d_oa = g["woa"].reshape(HEADS, HEAD_W, D_MODEL)[:, :64].reshape(HEADS * 64, D_MODEL)
    dev_major = lambda a: a.reshape(a.shape[0], N_DEV, a.shape[1] // N_DEV).transpose(1, 0, 2)
    return dict(w_in_e=dev_major(d_in_e), w_qb=dev_major(d_qb), w_kvb=dev_major(d_kvb),
                w_out_e=jnp.concatenate([d_oa, g["wob"]], axis=0).reshape(N_DEV, D_MODEL // N_DEV, D_MODEL))


def _local_step(x, positions, target, gw, sp, ex):
    w = _prep_weights(gw)
    T = x.shape[0]
    tm = min(TM, T)
    nt = T // tm
    half = MLA_ROPE // 2
    inv_freq = ROPE_BASE ** (-jnp.arange(half, dtype=F32) / half)
    invf_lane = jnp.concatenate([jnp.zeros((64,), F32), inv_freq, inv_freq, jnp.zeros((32,), F32)]).reshape(1, HEAD_W)
    tabs = _rope_tables(positions.reshape(T, 1), invf_lane)
    bias_full = jnp.repeat(sp["sgu_b"][0].T, 128, axis=1)
    sgu_w = sp["sgu_w"]
    gq, gkv = sp["mla_gq"], sp["mla_gkv"]
    ln1_g, ln1_b, ln2_g, ln2_b = sp["ln1_g"], sp["ln1_b"], sp["ln2_g"], sp["ln2_b"]
    zm, zs, cqn, ckvn, kr_rot = _mla_in(x, w["wm"], w["ws"], tabs, gq, gkv, deps=[ex.first_token])
    q, k, v = _mla_qkv(cqn, ckvn, kr_rot, tabs, w["wq"], w["wk"], w["wv"])
    o_att, lse = _attn_fwd(q, k, v)
    b_out = _sgu_fwd(zs, sp["sgu_ln_g"], sp["sgu_ln_b"], sgu_w, bias_full)
    token = ex.weights_forward(after=[o_att, b_out])
    y1, h1_bf = _proj_ln("l0_out_ln1", [o_att, b_out], [w["woa"], w["wob"]], x, ln1_g, ln1_b, 0, deps=[token])
    big = ex.weights_ready(after=[y1])
    w_ff1, w_in_o, w_out_o = big["w_ff1"], big["w_in_o"], big["w_out_o"].reshape(D_MODEL, D_MODEL)
    w_ff2 = [a.reshape(D_FF, D_MODEL) for a in big["w_ff2"]]
    a0, act0 = _mlp_up("l0", h1_bf, w_ff1[0])
    y2, h2_bf = _proj_ln("l0_ff2_ln2", [act0], [w_ff2[0]], y1, ln2_g, ln2_b, 0, prev=(ln1_g, ln1_b, 0))

    z1 = _tiled("l1_in", (1, nt), [_rb(h2_bf, tm), _res(w_in_o)], [_out(T, 4 * D_MODEL, F32, tm, 4 * D_MODEL)],
                _mmc_blocks(N_DEV, NN, lambda w, d: w[d]), direct=True)
    o_pre, hg, states = _hgrn_fwd(z1, sp["hg_lb"], sp["hg_gnorm"])
    y3, h3_bf = _proj_ln("l1_out_ln1", [hg], [w_out_o], y2, ln1_g, ln1_b, 1, prev=(ln2_g, ln2_b, 0))
    a1, act1 = _mlp_up("l1", h3_bf, w_ff1[1])

    gs, g0 = {}, {}
    dy4, dy4_bf, sq_err, gs["ln2_g1"], gs["ln2_b1"] = _proj_ln_loss("l1_ff2_loss", act1, w_ff2[1], y3, ln2_g, ln2_b, 1, target,
                                                                     prev=(ln1_g, ln1_b, 1))
    gs["sq_err"] = sq_err
    da1, dw1_1, dw2_1 = _mlp_bwd_w("l1", h3_bf, a1, act1, dy4_bf, big["w_ff2"][1])
    dy3, dy3_bf, dhg, gs["ln1_g1"], gs["ln1_b1"] = _dh_ln_back("l1_dh_ln1", da1, w_ff1[1], dy4, y3, ln1_g, 1, proj=[w_out_o])
    d_out_o = _tiled("l1_dwout", (2, D_MODEL // TM), [_tl(hg, TM), _cw(dy3_bf, TN)],
                     [_out(D_MODEL, D_MODEL, F32, TM, TN), _out(D_MODEL, D_MODEL, BF16, TM, TN)], _mmc(TN_, epilogue=_twice))
    d_out_o = [a.reshape(N_DEV, D_MODEL // N_DEV, D_MODEL) for a in d_out_o]
    dz1, gs["hg_lb"], gs["hg_gnorm"] = _hgrn_bwd(z1, o_pre, dhg, states, sp["hg_lb"], sp["hg_gnorm"])
    d_in_o = _tiled("l1_dwin", (N_DEV, 1), [_res(h2_bf), _cw(dz1, TN)],
                    [_out_dev(D_MODEL, TN, D_MODEL), _out_dev(D_MODEL, TN, D_MODEL, BF16)], _mmc(TN_, epilogue=_twice))
    token = ex.direct_start("l1", [dw1_1, dw2_1, d_in_o, d_out_o])

    dy2, dy2_bf, gs["ln2_g0"], gs["ln2_b0"] = _dh_ln_back("l1_dh_ln2", dz1, w_in_o, dy3, y2, ln2_g, 0, deps=[token])
    da0, dw1_0, dw2_0 = _mlp_bwd_w("l0", h1_bf, a0, act0, dy2_bf, big["w_ff2"][0])
    token = ex.direct_start("l0m", [dw1_0, dw2_0])
    dy1, dy1_bf, dcat, gs["ln1_g0"], gs["ln1_b0"] = _dh_ln_back("l0_dh_ln1", da0, w_ff1[0], dy2, y1, ln1_g, 0,
                                                                 proj=[w["woa"], w["wob"]], deps=[token])
    g0["woa"], g0["wob"] = _out_weight_grads(o_att, b_out, dy1_bf)
    dzs, gs["sgu_w"], gs["sgu_ln_g"], gs["sgu_ln_b"], gs["sgu_b"] = _sgu_bwd(zs, dcat, sp["sgu_ln_g"], sp["sgu_ln_b"], sgu_w, bias_full)
    dq, dk, dv = _attn_bwd(q, k, v, o_att, lse, dcat)
    dzm, g0["wq"], g0["wk"], g0["wv"], gs["mla_gq"], gs["mla_gkv"] = _mla_back(zm, cqn, ckvn, tabs, gq, gkv, w["wq"], w["wk"], w["wv"],
                                                                                 dq, dk, dv)
    token = ex.small_start(gs)
    dx, g0["wm"], g0["ws"] = _in_back(x, dzm, dzs, dy1, w["wm"], w["ws"], deps=[token])

    return sq_err, dx, _unprep_grads(g0), gs


def _me():
    return lax.axis_index("x"), lax.axis_index("y"), lax.axis_index("c")


ANY_SPEC = pl.BlockSpec(memory_space=pl.ANY)
HBM_SPEC = pl.BlockSpec(memory_space=pltpu.HBM)
SEM_SPEC = pl.BlockSpec(memory_space=pltpu.SEMAPHORE)
EFFECT = pltpu.SideEffectType.DATAFLOW_SIDE_EFFECTING


def _split_start(name, srcs, lands, n_sems, make_copies, after=()):
    n, m, k = len(srcs), len(lands), len(after)

    def body(*refs):
        for cp in make_copies(refs[:n], refs[n:n + m], refs[n + m + k], refs[n + m + k + 1]):
            cp.start()
        refs[-1][...] = jnp.zeros(refs[-1].shape, F32)

    out_shape = (pltpu.SemaphoreType.DMA((n_sems,)), pltpu.SemaphoreType.DMA((n_sems,)),
                 *[pltpu.HBM(a.shape, a.dtype) for a in (*srcs, *lands)], jax.ShapeDtypeStruct((8, 128), F32))
    res = pl.pallas_call(
        body, name=name, out_shape=out_shape, in_specs=[HBM_SPEC] * (n + m) + [ANY_SPEC] * k,
        out_specs=(SEM_SPEC, SEM_SPEC, *[HBM_SPEC] * (n + m), pl.BlockSpec(memory_space=pltpu.VMEM)),
        input_output_aliases={i: 2 + i for i in range(n + m)},
        compiler_params=pltpu.CompilerParams(has_side_effects=EFFECT),
    )(*[_hbm(a) for a in (*srcs, *lands)], *after)
    return res[0], res[1], list(res[2:2 + n]), list(res[2 + n:2 + n + m]), res[-1]


def _split_wait(name, send_sems, recv_sems, srcs, lands, after, make_copies):
    n, m = len(srcs), len(lands)

    def body(*refs):
        for cp in make_copies(refs[:n], refs[n:n + m], refs[n + m], refs[n + m + 1]):
            cp.wait_send()
            cp.wait_recv()

    res = pl.pallas_call(
        body, name=name, out_shape=tuple(pltpu.HBM(a.shape, a.dtype) for a in (*srcs, *lands)),
        in_specs=[HBM_SPEC] * (n + m) + [SEM_SPEC, SEM_SPEC] + [ANY_SPEC] * len(after), out_specs=tuple([HBM_SPEC] * (n + m)),
        input_output_aliases={i: i for i in range(n + m)},
        compiler_params=pltpu.CompilerParams(has_side_effects=EFFECT),
    )(*srcs, *lands, send_sems, recv_sems, *after)
    return list(res[:n]), list(res[n:])


def _place_own(shards, dev):
    n = len(shards)

    def kern(dev_ref, *refs):
        for x_ref, o_ref in zip(refs[:n], refs[n:]):
            o_ref[...] = x_ref[...].astype(o_ref.dtype)

    blocks = [(None, *a.shape[1:]) for a, _, _ in shards]
    nbytes = sum(_nbytes(b, a.dtype) + _nbytes(b, dt) for b, (a, _, dt) in zip(blocks, shards))
    return pl.pallas_call(
        kern, name="weights_place_own", out_shape=[pltpu.HBM((N_DEV, *a.shape[1:]), dt) for a, _, dt in shards],
        grid_spec=pltpu.PrefetchScalarGridSpec(
            num_scalar_prefetch=1, grid=(1,),
            in_specs=[pl.BlockSpec(b, functools.partial(lambda i, dev, l: (l, 0, 0), l=l)) for b, (_, l, _) in zip(blocks, shards)],
            out_specs=[pl.BlockSpec(b, lambda i, dev: (dev[0], 0, 0)) for b in blocks]),
        compiler_params=pltpu.CompilerParams(dimension_semantics=("arbitrary",), vmem_limit_bytes=_vmem(nbytes)),
    )(dev, *[_hbm(a) for a, _, _ in shards])


def _ag_first_copies(src_refs, out_refs, send_sems, recv_sems):
    x, y, c = _me()
    targets = [(x, y, 1 - c), (1 - x, y, c), (x, 1 - y, c), (1 - x, 1 - y, c)]
    return [pltpu.make_async_remote_copy(
        src_ref=out_refs[op].at[4 * x + 2 * y + c], dst_ref=out_refs[op].at[4 * x + 2 * y + c], send_sem=send_sems.at[4 * op + k],
        recv_sem=recv_sems.at[4 * op + k], device_id=to, device_id_type=MESH)
        for op in range(len(out_refs)) for k, to in enumerate(targets)]


def _ag_second_copies(src_refs, out_refs, send_sems, recv_sems):
    x, y, c = _me()
    chips = [(1 - x, y), (x, 1 - y), (1 - x, 1 - y)]
    return [pltpu.make_async_remote_copy(
        src_ref=out_refs[op].at[4 * cx + 2 * cy + c], dst_ref=out_refs[op].at[4 * cx + 2 * cy + c],
        send_sem=send_sems.at[3 * op + j], recv_sem=recv_sems.at[3 * op + j], device_id=(x, y, 1 - c), device_id_type=MESH)
        for op in range(len(out_refs)) for j, (cx, cy) in enumerate(chips)]


def _rs_sibling_copies(g_refs, out_refs, send_sems, recv_sems):
    x, y, c = _me()
    return [pltpu.make_async_remote_copy(
        src_ref=g_refs[op].at[k, 1 - c], dst_ref=out_refs[op].at[k], send_sem=send_sems.at[4 * op + k],
        recv_sem=recv_sems.at[4 * op + k], device_id=(x, y, 1 - c), device_id_type=MESH)
        for op in range(len(g_refs)) for k in range(4)]


def _rs_direct_copies(g_refs, land_refs, send_sems, recv_sems):
    x, y, c = _me()
    n = len(g_refs) // 2
    chips = [(1 - x, y), (x, 1 - y), (1 - x, 1 - y)]
    copies = []
    for op in range(n):
        g32, g16, from_sib, from_others = g_refs[op], g_refs[n + op], land_refs[op], land_refs[n + op]
        copies.append(pltpu.make_async_remote_copy(
            src_ref=g32.at[2 * x + y, 1 - c], dst_ref=from_sib, send_sem=send_sems.at[7 * op], recv_sem=recv_sems.at[7 * op],
            device_id=(x, y, 1 - c), device_id_type=MESH))
        for j, (cx, cy) in enumerate(chips):
            for s, cc in enumerate((c, 1 - c)):
                copies.append(pltpu.make_async_remote_copy(
                    src_ref=g16.at[2 * cx + cy, cc], dst_ref=from_others.at[2 * j + s], send_sem=send_sems.at[7 * op + 1 + 2 * j + s],
                    recv_sem=recv_sems.at[7 * op + 1 + 2 * j + s], device_id=(cx, cy, cc), device_id_type=MESH))
    return copies


def _rs_chip_copies(p_refs, out_refs, send_sems, recv_sems):
    x, y, c = _me()
    chips = [(1 - x, y), (x, 1 - y), (1 - x, 1 - y)]
    return [pltpu.make_async_remote_copy(
        src_ref=p_refs[op].at[2 * cx + cy], dst_ref=out_refs[op].at[j], send_sem=send_sems.at[3 * op + j],
        recv_sem=recv_sems.at[3 * op + j], device_id=(cx, cy, c), device_id_type=MESH)
        for op in range(len(p_refs)) for j, (cx, cy) in enumerate(chips)]


def _all_gather(placed):
    n = len(placed)

    def kern(*refs):
        in_refs, out_refs, (send_sems, recv_sems) = refs[:n], refs[n:2 * n], refs[2 * n:]
        x, y, c = _me()
        me, sibling = (x, y, c), (x, y, 1 - c)
        chips = [(1 - x, y), (x, 1 - y), (1 - x, 1 - y)]

        def copy(op, k, block, to, own=False):
            idx = 4 * block[0] + 2 * block[1] + block[2]
            return pltpu.make_async_remote_copy(
                src_ref=(in_refs if own else out_refs)[op].at[idx], dst_ref=out_refs[op].at[idx], send_sem=send_sems.at[7 * op + k],
                recv_sem=recv_sems.at[7 * op + k], device_id=to, device_id_type=MESH)

        first = []
        for op in range(n):
            first.append(copy(op, 0, me, sibling, own=True))
            first += [copy(op, 1 + j, me, (*chip, c), own=True) for j, chip in enumerate(chips)]
        for cp in first:
            cp.start()
        passed = []
        for j, chip in enumerate(chips):
            for op in range(n):
                copy(op, 1 + j, (*chip, c), me).wait_recv()
                passed.append(copy(op, 4 + j, (*chip, c), sibling))
                passed[-1].start()
        for op in range(n):
            copy(op, 0, sibling, me).wait_recv()
            for j, chip in enumerate(chips):
                copy(op, 4 + j, (*chip, 1 - c), me).wait_recv()
        for cp in first + passed:
            cp.wait_send()

    return pl.pallas_call(
        kern, name="weights_all_gather", out_shape=[pltpu.HBM(g.shape, g.dtype) for g in placed],
        in_specs=[ANY_SPEC] * n, out_specs=[ANY_SPEC] * n, input_output_aliases={i: i for i in range(n)},
        scratch_shapes=[pltpu.SemaphoreType.DMA((7 * n,)), pltpu.SemaphoreType.DMA((7 * n,))],
    )(*[_hbm(a) for a in placed])


def _row_tile(r, w, n_blocks):
    tr = r
    while tr > 8 and 2 * n_blocks * tr * w * 4 > 24 * 2**20:
        tr //= 2
    return tr


def _chip_sum(name, g, from_sibling, core):
    _, _, R, W = g.shape
    tr = _row_tile(R, W, 3)

    def kern(core_ref, g_ref, s_ref, o_ref):
        o_ref[...] = (g_ref[...] + s_ref[...]).astype(BF16)

    return pl.pallas_call(
        kern, name=name, out_shape=pltpu.HBM((4, R, W), BF16),
        grid_spec=pltpu.PrefetchScalarGridSpec(
            num_scalar_prefetch=1, grid=(4, R // tr),
            in_specs=[pl.BlockSpec((None, None, tr, W), lambda k, i, core: (k, core[0], i, 0)),
                      pl.BlockSpec((None, tr, W), lambda k, i, core: (k, i, 0))],
            out_specs=pl.BlockSpec((None, tr, W), lambda k, i, core: (k, i, 0))),
        compiler_params=pltpu.CompilerParams(dimension_semantics=("parallel", "parallel"), vmem_limit_bytes=_vmem(3 * tr * W * 4)),
    )(core, _hbm(g), _hbm(from_sibling))


def _adamw(w, g, m, v):
    m = ADAM_B1 * m + (1.0 - ADAM_B1) * g
    v = ADAM_B2 * v + (1.0 - ADAM_B2) * (g * g)
    m_hat = m / (1.0 - ADAM_B1 ** ADAM_STEP)
    v_hat = v / (1.0 - ADAM_B2 ** ADAM_STEP)
    return -ADAM_LR * (m_hat / (jnp.sqrt(v_hat) + ADAM_EPS) + ADAM_WD * w), m, v


def _finish_sharded(name, layers, w, m, v, where, deps=()):
    nl, R, W = w.shape
    n_other = layers[0][2].shape[0]
    tr = _row_tile(R, W, (8 + n_other) * nl)
    deps = _deps(deps)

    def kern(where_ref, *refs):
        w_ref, m_ref, v_ref = refs[3 * nl:3 * nl + 3]
        go_ref, d_ref, mo_ref, vo_ref = refs[3 * nl + 3 + len(deps):]
        for l in range(nl):
            g_ref, s_ref, c_ref = refs[3 * l:3 * l + 3]
            grad = g_ref[...] + s_ref[...]
            for j in range(n_other):
                grad = grad + c_ref[j].astype(F32)
            go_ref[l] = grad
            d_ref[l], mo_ref[l], vo_ref[l] = _adamw(w_ref[l], grad, m_ref[l], v_ref[l])

    row = pl.BlockSpec((nl, tr, W), lambda i, wh: (0, i, 0))
    in_specs, args = [], []
    for g, s, c in layers:
        sib = (pl.BlockSpec((None, tr, W), lambda i, wh: (wh[0], i, 0)) if s.ndim == 3 else pl.BlockSpec((tr, W), lambda i, wh: (i, 0)))
        in_specs += [pl.BlockSpec((None, None, tr, W), lambda i, wh: (wh[0], wh[1], i, 0)), sib,
                     pl.BlockSpec((n_other, tr, W), lambda i, wh: (0, i, 0))]
        args += [g, s, c]
    return pl.pallas_call(
        kern, name=name, out_shape=[pltpu.HBM((nl, R, W), F32)] * 4,
        grid_spec=pltpu.PrefetchScalarGridSpec(num_scalar_prefetch=1, grid=(R // tr,),
                                               in_specs=in_specs + [row, row, row] + [ANY_SPEC] * len(deps),
                                               out_specs=[row, row, row, row]),
        compiler_params=pltpu.CompilerParams(dimension_semantics=("parallel",),
                                             vmem_limit_bytes=_vmem(nl * (8 + n_other) * tr * W * 4)),
    )(where, *[_hbm(a) for a in (*args, w, m, v)], *deps)


SMALL_PLACE = (("mla_gq", 0, 0, 1, 256), ("mla_gkv", 0, 256, 1, 256), ("sgu_ln_g", 0, 512, 1, 512), ("sgu_ln_b", 1, 0, 1, 512),
               ("hg_lb", 2, 0, 2, 1024), ("ln1_g", 4, 0, 2, 1024), ("ln1_b", 6, 0, 2, 1024), ("sgu_b", 8, 0, 4, 128),
               ("ln2_g", 12, 0, 2, 1024), ("ln2_b", 14, 0, 2, 1024), ("hg_gnorm", 16, 0, 1, 1024))
SMALL_BUF_ROWS = 24
LOSS_ROW = 17


def _small_pack(gs, dev):
    pieces = [(gs["mla_gq"], 0, 0), (gs["mla_gkv"], 0, 256), (gs["sgu_ln_g"], 0, 512), (gs["sgu_ln_b"], 1, 0), (gs["hg_lb"], 2, 0),
              (gs["ln1_g0"], 4, 0), (gs["ln1_g1"], 5, 0), (gs["ln1_b0"], 6, 0), (gs["ln1_b1"], 7, 0), (gs["sgu_b"], 8, 0),
              (gs["ln2_g0"], 12, 0), (gs["ln2_g1"], 13, 0), (gs["ln2_b0"], 14, 0), (gs["ln2_b1"], 15, 0), (gs["hg_gnorm"], 16, 0),
              (gs["sq_err"], LOSS_ROW, 0)]
    n_p = len(pieces)

    def kern(dev_ref, *refs):
        a_ref, b_ref = refs[n_p + 1], refs[n_p + 2]
        a_ref[...] = jnp.zeros(a_ref.shape, F32)
        for ref, (_, r, l0) in zip(refs[:n_p], pieces):
            a_ref[r:r + ref.shape[0], l0:l0 + ref.shape[1]] = ref[...]
        b_ref[...] = refs[n_p][...]

    whole = lambda a: pl.BlockSpec(a.shape, functools.partial(lambda i, dev, nd: (0,) * nd, nd=a.ndim))
    return pl.pallas_call(
        kern, name="small_grads_pack",
        out_shape=[pltpu.HBM((N_DEV, SMALL_BUF_ROWS, D_MODEL), F32), pltpu.HBM((N_DEV, SGU_G, 128, 128), F32)],
        grid_spec=pltpu.PrefetchScalarGridSpec(
            num_scalar_prefetch=1, grid=(1,), in_specs=[whole(p[0]) for p in pieces] + [whole(gs["sgu_w"])],
            out_specs=[pl.BlockSpec((None, SMALL_BUF_ROWS, D_MODEL), lambda i, dev: (dev[0], 0, 0)),
                       pl.BlockSpec((None, SGU_G, 128, 128), lambda i, dev: (dev[0], 0, 0, 0))]),
    )(dev, *[p[0] for p in pieces], gs["sgu_w"])


def _small_copies(src_refs, land_refs, send_sems, recv_sems):
    px, py, pc = _me()
    me = 4 * px + 2 * py + pc
    return [pltpu.make_async_remote_copy(
        src_ref=land_refs[k].at[me], dst_ref=land_refs[k].at[me], send_sem=send_sems.at[2 * (r - 1) + k],
        recv_sem=recv_sems.at[2 * (r - 1) + k], device_id=(px ^ (r >> 2), py ^ ((r >> 1) & 1), pc ^ (r & 1)), device_id_type=MESH)
        for r in range(1, N_DEV) for k in range(2)]


def _small_adamw(slots_a, slots_b, given):
    names = [p[0] for p in SMALL_PLACE] + ["sgu_w"]
    n_names = len(names)
    wmv = [given[pre + name] for name in names for pre in ("", "m_", "v_")]
    vmem = pl.BlockSpec(memory_space=pltpu.VMEM)

    def kern(*refs):
        sum_a, sum_b = refs[0][0], refs[1][0]
        for d in range(1, N_DEV):
            sum_a, sum_b = sum_a + refs[0][d], sum_b + refs[1][d]
        wmv_refs, out_refs = refs[2:2 + 3 * n_names], refs[2 + 3 * n_names:]
        px, py, pc = _me()
        me = 4 * px + 2 * py + pc

        def own_block(full):
            acc = full[:, 0:128]
            for b in range(1, N_DEV):
                acc = jnp.where(me == b, full[:, b * 128:(b + 1) * 128], acc)
            return acc

        for idx, name in enumerate(names):
            w_ref, m_ref, v_ref = wmv_refs[3 * idx:3 * idx + 3]
            if name == "sgu_w":
                grad = sum_b[None]
            else:
                _, r, l0, nr, nl = SMALL_PLACE[idx]
                grad = sum_a[r:r + nr, l0:l0 + nl]
                if name == "hg_gnorm":
                    grad = own_block(grad)
                if name == "sgu_b":
                    grad = grad[None]
            res = (grad, *_adamw(w_ref[...], grad, m_ref[...], v_ref[...]))
            for o_ref, val in zip(out_refs[4 * idx:4 * idx + 4], res):
                o_ref[...] = val
        out_refs[4 * n_names][...] = (0.5 / D_MODEL) * jnp.sum(sum_a[LOSS_ROW:LOSS_ROW + 1, :], axis=1, keepdims=True)

    out_shape = [jax.ShapeDtypeStruct(given[name].shape, F32) for name in names for _ in range(4)]
    out_shape.append(jax.ShapeDtypeStruct((1, 1), F32))
    res = pl.pallas_call(
        kern, name="small_adamw", out_shape=out_shape, in_specs=[vmem] * (2 + len(wmv)), out_specs=[vmem] * len(out_shape),
    )(slots_a, slots_b, *wmv)
    out = {name: res[4 * idx:4 * idx + 4] for idx, name in enumerate(names)}
    out["loss"] = res[-1].reshape(())
    return out


class _Exchange:
    def __init__(self, given):
        self.given = given
        px, py, pc = _me()
        self.core = pc.reshape(1).astype(jnp.int32)
        self.dev = (4 * px + 2 * py + pc).reshape(1).astype(jnp.int32)
        self.where = jnp.stack([2 * px + py, pc]).astype(jnp.int32)
        self.state, self.layers = {}, {}

    def start_weights(self, lands, after):
        self.weights = _split_start("weights_first_start", [], lands, 4 * len(lands), _ag_first_copies, after=after)
        self.first_token = self.weights[4]

    def weights_forward(self, after):
        send_sems, recv_sems, shards, lands, _ = self.weights
        _, lands = _split_wait("weights_first_wait", send_sems, recv_sems, shards, lands, after, _ag_first_copies)
        self.weights = _split_start("weights_second_start", [], lands, 3 * len(lands), _ag_second_copies)
        return self.weights[4]

    def weights_ready(self, after):
        send_sems, recv_sems, shards, lands, _ = self.weights
        _, got = _split_wait("weights_second_wait", send_sems, recv_sems, shards, lands, after, _ag_second_copies)
        return dict(w_in_o=got[0], w_out_o=got[1], w_ff1=[got[2], got[3]], w_ff2=[got[4], got[5]])

    def small_start(self, gs):
        self.small = _split_start("small_grads_start", [], _small_pack(gs, self.dev), 14, _small_copies)
        return self.small[4]

    def small_finish(self, after):
        send_sems, recv_sems, _, lands, _ = self.small
        _, lands = _split_wait("small_grads_wait", send_sems, recv_sems, [], lands, after, _small_copies)
        return _small_adamw(lands[0], lands[1], self.given)

    def direct_start(self, tag, grads):
        f32 = [g[0].reshape(4, 2, *g[0].shape[1:]) for g in grads]
        bf16 = [g[1].reshape(4, 2, *g[1].shape[1:]) for g in grads]
        lands = [lax.empty(b.shape[2:], F32) for b in f32] + [lax.empty((6, *b.shape[2:]), BF16) for b in f32]
        self.state[tag] = _split_start(f"grads_{tag}_start", f32 + bf16, lands, 7 * len(grads), _rs_direct_copies)
        return self.state[tag][4]

    def direct_end(self, tag, after):
        send_sems, recv_sems, srcs, lands, _ = self.state[tag]
        srcs, lands = _split_wait(f"grads_{tag}_wait", send_sems, recv_sems, srcs, lands, after, _rs_direct_copies)
        n = len(lands) // 2
        self.layers[tag] = list(zip(srcs[:n], lands[:n], lands[n:]))

    def grads_start(self, tag, grads):
        blocks = [g.reshape(4, 2, *g.shape[1:]) for g in grads]
        lands = [lax.empty((4, *b.shape[2:]), F32) for b in blocks]
        self.state[tag] = _split_start(f"grads_{tag}_sibling_start", blocks, lands, 4 * len(blocks), _rs_sibling_copies)
        return self.state[tag][4]

    def grads_middle(self, tag, after):
        send_sems, recv_sems, blocks, lands, _ = self.state[tag]
        blocks, from_sibling = _split_wait(f"grads_{tag}_sibling_wait", send_sems, recv_sems, blocks, lands, [after], _rs_sibling_copies)
        sums = [_chip_sum(f"grads_{tag}_chip_sum_{k}", b, s, self.core) for k, (b, s) in enumerate(zip(blocks, from_sibling))]
        lands = [lax.empty((3, *p.shape[1:]), BF16) for p in sums]
        self.state[tag] = (blocks, from_sibling, _split_start(f"grads_{tag}_chips_start", sums, lands, 3 * len(sums), _rs_chip_copies))
        return self.state[tag][2][4]

    def grads_end(self, tag, after):
        blocks, from_sibling, (send_sems, recv_sems, sums, lands, _) = self.state[tag]
        after = list(after) if isinstance(after, (list, tuple)) else [after]
        _, from_chips = _split_wait(f"grads_{tag}_chips_wait", send_sems, recv_sems, sums, lands, after, _rs_chip_copies)
        self.layers[tag] = list(zip(blocks, from_sibling, from_chips))


def kernel(x, positions, w_in_e, mla_gq, mla_gkv, w_qb, w_kvb, sgu_ln_g, sgu_ln_b, sgu_w, sgu_b, w_out_e, w_in_o, hg_lb, hg_gnorm, w_out_o, ln1_g, ln1_b, w_ff1, w_ff2, ln2_g, ln2_b, loss_target, m_w_in_e, m_mla_gq, m_mla_gkv, m_w_qb, m_w_kvb, m_sgu_ln_g, m_sgu_ln_b, m_sgu_w, m_sgu_b, m_w_out_e, m_w_in_o, m_hg_lb, m_hg_gnorm, m_w_out_o, m_ln1_g, m_ln1_b, m_w_ff1, m_w_ff2, m_ln2_g, m_ln2_b, v_w_in_e, v_mla_gq, v_mla_gkv, v_w_qb, v_w_kvb, v_sgu_ln_g, v_sgu_ln_b, v_sgu_w, v_sgu_b, v_w_out_e, v_w_in_o, v_hg_lb, v_hg_gnorm, v_w_out_o, v_ln1_g, v_ln1_b, v_w_ff1, v_w_ff2, v_ln2_g, v_ln2_b):
    given = dict(locals())
    ex = _Exchange(given)

    names = ["w_in_e", "w_qb", "w_kvb", "w_out_e"]
    placed = _place_own([(given[n], 0, BF16) for n in names] + [(hg_gnorm.reshape(1, 1, D_MODEL // N_DEV), 0, F32)]
                        + [(w_in_o, 0, BF16), (w_out_o, 0, BF16), (w_ff1, 0, BF16), (w_ff1, 1, BF16), (w_ff2, 0, BF16), (w_ff2, 1, BF16)],
                        ex.dev)
    got = _all_gather(placed[:5])
    ex.start_weights(placed[5:], after=[got[0]])
    gw = dict(zip(names, got[:4]))
    small_names = ["mla_gq", "mla_gkv", "sgu_ln_g", "sgu_ln_b", "sgu_w", "sgu_b", "hg_lb", "ln1_g", "ln1_b", "ln2_g", "ln2_b"]
    sp = {n: given[n] for n in small_names}
    sp["hg_gnorm"] = got[4].reshape(1, D_MODEL)

    _, dx, grads, gs = _local_step(x[0], positions[0], loss_target[0], gw, sp, ex)

    def finish(n, layers, deps=()):
        return _finish_sharded(f"finish_{n}", layers, given[n], given["m_" + n], given["v_" + n], ex.where, deps=deps)

    ex.direct_end("l1", after=[dx])
    ex.direct_end("l0m", after=[dx])
    l1, l0m = ex.layers["l1"], ex.layers["l0m"]
    results = {}
    token = ex.grads_start("l0s", [grads[n] for n in names])
    results["w_ff1"] = finish("w_ff1", [l0m[0], l1[0]], deps=[token])
    token = ex.grads_middle("l0s", after=results["w_ff1"][0])
    results["w_ff2"] = finish("w_ff2", [l0m[1], l1[1]], deps=[token])
    results["w_in_o"] = finish("w_in_o", [l1[2]], deps=[token])
    results["w_out_o"] = finish("w_out_o", [l1[3]], deps=[token])
    results.update(ex.small_finish(after=[results["w_in_o"][0]]))
    ex.grads_end("l0s", after=[results[n][0] for n in ("mla_gq", "w_ff2", "w_in_o", "w_out_o")])
    for n, layer in zip(names, ex.layers["l0s"]):
        results[n] = finish(n, [layer])

    order = ["w_in_e", "mla_gq", "mla_gkv", "w_qb", "w_kvb", "sgu_ln_g", "sgu_ln_b", "sgu_w", "sgu_b", "w_out_e", "w_in_o",
             "hg_lb", "hg_gnorm", "w_out_o", "ln1_g", "ln1_b", "w_ff1", "w_ff2", "ln2_g", "ln2_b"]
    return (results["loss"], dx[None], *[results[name][kind] for kind in range(4) for name in order])
```

```python
import functools
import math

import jax
import jax.numpy as jnp
import numpy as np
from jax import lax
from jax.experimental import pallas as pl
from jax.experimental.pallas import tpu as pltpu

F32 = jnp.float32
BF16 = jnp.bfloat16
MESH = pl.DeviceIdType.MESH
HIGHEST = lax.Precision.HIGHEST

D_MODEL = 1024
D_FF = 4096
N_DEV = 8
HEADS = 8
HEAD_W = 128
MLA_NOPE = 64
MLA_ROPE = 32
MLA_V = 64
MLA_LORA = 256
MLA_SCALE = (MLA_NOPE + MLA_ROPE) ** -0.5
ROPE_BASE = 10000.0
SGU_DIM = 512
SGU_G = 4
SGU_CHUNK = 128
HG_CHUNK = 64
HG_CHUNKS_PER_STEP = 4
ALPHA = (2 * 2) ** 0.25
EPS = 1e-5
ADAM_LR, ADAM_B1, ADAM_B2, ADAM_EPS, ADAM_WD, ADAM_STEP = 0.001, 0.9, 0.999, 1e-08, 0.01, 10

VMEM_CAP_V7X = 56 * 2**20
VMEM_SLACK = 12 * 2**20
TM = 512
TN = 512


def _vmem(block_bytes):
    return int(min(VMEM_CAP_V7X, 2 * block_bytes + VMEM_SLACK))


def _hbm(a):
    return pltpu.with_memory_space_constraint(a, pltpu.HBM)


def _nbytes(shape, dtype):
    return int(np.prod([d for d in shape if d is not None])) * jnp.dtype(dtype).itemsize


def _sig(x):
    return 1.0 / (1.0 + jnp.exp(-x))


def _gelu(x):
    c = math.sqrt(2.0 / math.pi)
    t = jnp.tanh(c * (x + 0.044715 * x * x * x))
    return 0.5 * x * (1.0 + t), t


def _gelu_grad(x, t):
    c = math.sqrt(2.0 / math.pi)
    return 0.5 * (1.0 + t) + 0.5 * x * (1.0 - t * t) * c * (1.0 + 3 * 0.044715 * x * x)


def _dot(a, b, dims, precision=None):
    return lax.dot_general(a, b, (dims, ((), ())), preferred_element_type=F32, precision=precision)


NN = ((1,), (0,))
NT = ((1,), (1,))
TN_ = ((0,), (0,))


def _deps(deps):
    return [d for d in deps if d is not None]


def _tiled(name, grid, ins, outs, compute, direct=False, deps=()):
    n_in, deps = len(ins), _deps(deps)
    n_skip = n_in + len(deps)

    def kern(*refs):
        if direct:
            compute(refs[:n_in], refs[n_skip:])
            return
        for o_ref, r in zip(refs[n_skip:], compute(*refs[:n_in])):
            o_ref[...] = r.astype(o_ref.dtype).reshape(o_ref.shape)

    swap = lambda f: (lambda j, i: f(i, j))
    nbytes = sum(_nbytes(blk, a.dtype) for a, blk, _ in ins) + sum(_nbytes(blk, dt) + _nbytes(blk, F32) for _, dt, blk, _ in outs)
    res = pl.pallas_call(
        kern, name=name, grid=grid,
        in_specs=[pl.BlockSpec(blk, swap(f), pipeline_mode=pl.Buffered(1) if tuple(blk) == tuple(a.shape) else None)
                  for a, blk, f in ins] + [ANY_SPEC] * len(deps),
        out_specs=[pl.BlockSpec(blk, swap(f)) for _, _, blk, f in outs],
        out_shape=[pltpu.HBM(shape, dt) for shape, dt, _, _ in outs],
        compiler_params=pltpu.CompilerParams(dimension_semantics=("parallel", "parallel"), vmem_limit_bytes=_vmem(nbytes)),
    )(*[_hbm(a) for a, _, _ in ins], *deps)
    return res if len(res) > 1 else res[0]


def _rb(a, tm, w=None, cb=0):
    return (a, (tm, a.shape[1] if w is None else w), lambda i, j: (i, cb))


def _cw(b, tn):
    return (b, (b.shape[0], tn), lambda i, j: (0, j))


def _tl(a, tm):
    return (a, (a.shape[0], tm), lambda i, j: (0, i))


def _out(m, n, dtype, tm, tn):
    return ((m, n), dtype, (tm, tn), lambda i, j: (i, j))


def _out_dev(k, n, tm, dtype=F32):
    return ((N_DEV, k, n), dtype, (None, tm, n), lambda i, j: (j, i, 0))


def _twice(acc):
    return acc, acc


def _mmc(dims, n_pairs=1, epilogue=None):
    def compute(*refs):
        acc = None
        for k in range(n_pairs):
            d = _dot(refs[2 * k][...].astype(BF16), refs[2 * k + 1][...].astype(BF16), dims)
            acc = d if acc is None else acc + d
        ext = [r[...] for r in refs[2 * n_pairs:]]
        return epilogue(acc, *ext) if epilogue is not None else (acc,)

    return compute


def _res(w):
    return (w, w.shape, functools.partial(lambda i, j, nd: (0,) * nd, nd=w.ndim))


def _mmc_blocks(nblk, dims, rhs_block, epilogue=None):
    def compute(in_refs, out_refs):
        a = in_refs[0][...].astype(BF16)
        for d in range(nblk):
            acc = _dot(a, rhs_block(in_refs[1], d).astype(BF16), dims)
            n = acc.shape[1]
            ext = [r[:, d * n:(d + 1) * n] for r in in_refs[2:]]
            res = epilogue(acc, *ext) if epilogue is not None else (acc,)
            for o_ref, r in zip(out_refs, res):
                o_ref[:, d * n:(d + 1) * n] = r.astype(o_ref.dtype)

    return compute


def _rowwise(name, body, rows, consts, out_rows, out_accs=(), tr=512, deps=()):
    T = rows[0][0].shape[0]
    tr = min(tr, T)
    deps = _deps(deps)
    nr, ncn, no, nd = len(rows), len(consts), len(out_rows), len(deps)

    def kern(*refs):
        accs = refs[nr + ncn + nd + no:]
        if accs:
            @pl.when(pl.program_id(0) == 0)
            def _():
                for a in accs:
                    a[...] = jnp.zeros(a.shape, a.dtype)
        body(refs[:nr], refs[nr:nr + ncn], refs[nr + ncn + nd:nr + ncn + nd + no], accs)

    in_specs = [pl.BlockSpec((tr, w), functools.partial(lambda i, cb: (i, cb), cb=cb)) for _, w, cb in rows]
    in_specs += [pl.BlockSpec(c.shape, functools.partial(lambda i, nd: (0,) * nd, nd=c.ndim), pipeline_mode=pl.Buffered(1))
                 for c in consts]
    in_specs += [ANY_SPEC] * nd
    out_specs = [pl.BlockSpec((tr, w), lambda i: (i, 0)) for w, _ in out_rows]
    out_specs += [pl.BlockSpec(s, functools.partial(lambda i, nd: (0,) * nd, nd=len(s))) for s, _ in out_accs]
    out_shape = [pltpu.HBM((T, w), dt) for w, dt in out_rows]
    out_shape += [pltpu.HBM(s, dt) for s, dt in out_accs]
    nbytes = sum(_nbytes((tr, w), a.dtype) for a, w, _ in rows) + sum(_nbytes(c.shape, c.dtype) for c in consts)
    nbytes += sum(_nbytes((tr, w), dt) for w, dt in out_rows) + sum(_nbytes(s, dt) for s, dt in out_accs)
    res = pl.pallas_call(
        kern, name=name, grid=(T // tr,), in_specs=in_specs, out_specs=out_specs, out_shape=out_shape,
        compiler_params=pltpu.CompilerParams(dimension_semantics=("arbitrary",), vmem_limit_bytes=_vmem(nbytes)),
    )(*[_hbm(a) for a, _, _ in rows], *[_hbm(c) for c in consts], *deps)
    return res if len(res) > 1 else res[0]


def _full(a):
    return (a, a.shape[1], 0)


def _ln_stats(y):
    mu = jnp.mean(y, axis=-1, keepdims=True)
    yc = y - mu
    r = lax.rsqrt(jnp.mean(yc * yc, axis=-1, keepdims=True) + EPS)
    return yc * r, r


def _row_halves(n):
    return [slice(0, n // 2), slice(n // 2, n)] if n >= 256 else [slice(0, n)]


def _ln_back(dh, xh, r, gain, dg_ref, db_ref):
    dg_ref[...] += jnp.sum(dh * xh, axis=0, keepdims=True)
    db_ref[...] += jnp.sum(dh, axis=0, keepdims=True)
    dx = dh * gain
    return r * (dx - jnp.mean(dx, axis=-1, keepdims=True) - xh * jnp.mean(dx * xh, axis=-1, keepdims=True))


def _proj_ln(name, acts, weights, h_in, g, b, layer, deps=()):
    n = len(acts)

    def body(rows, consts, outs, accs):
        acc = None
        for k in range(n):
            d = _dot(rows[k][...].astype(BF16), consts[k][...], NN)
            acc = d if acc is None else acc + d
        y = ALPHA * rows[n][...] + acc
        xh, _ = _ln_stats(y)
        h = xh * consts[n][layer:layer + 1, :] + consts[n + 1][layer:layer + 1, :]
        outs[0][...] = y
        outs[1][...] = h
        outs[2][...] = h.astype(BF16)

    return _rowwise(name, body, [_full(a) for a in acts] + [_full(h_in)], [*weights, g, b],
                    [(D_MODEL, F32), (D_MODEL, F32), (D_MODEL, BF16)], tr=TM, deps=deps)


def _proj_ln_loss(name, act, w2, h_in, g, b, layer, target):
    def body(rows, consts, outs, accs):
        y = ALPHA * rows[1][...] + _dot(rows[0][...], consts[0][...], NN)
        xh, r = _ln_stats(y)
        gain = consts[1][layer:layer + 1, :]
        err = xh * gain + consts[2][layer:layer + 1, :] - rows[2][...]
        accs[0][...] += jnp.sum(err * err, axis=0, keepdims=True)
        dy = _ln_back(err * (1.0 / D_MODEL), xh, r, gain, accs[1], accs[2])
        outs[0][...] = dy
        outs[1][...] = dy.astype(BF16)

    return _rowwise(name, body, [_full(act), _full(h_in), _full(target)], [w2, g, b], [(D_MODEL, F32), (D_MODEL, BF16)],
                    [((1, D_MODEL), F32)] * 3, tr=TM)


def _dh_ln_back(name, da, w, dy_next, y, g, layer, proj=(), deps=()):
    def body(rows, consts, outs, accs):
        n = consts[0].shape[2]
        for sl in _row_halves(rows[0].shape[0]):
            acc = ALPHA * rows[1][sl, :]
            for d in range(N_DEV):
                acc = acc + _dot(rows[0][sl, d * n:(d + 1) * n], consts[0][d], NT)
            xh, r = _ln_stats(rows[2][sl, :])
            dy = _ln_back(acc, xh, r, consts[1][layer:layer + 1, :], accs[0], accs[1])
            outs[0][sl, :] = dy
            dy_bf = dy.astype(BF16)
            outs[1][sl, :] = dy_bf
            off = 0
            for k, p in enumerate(proj):
                outs[2][sl, off:off + p.shape[0]] = _dot(dy_bf, consts[2 + k][...], NT).astype(BF16)
                off += p.shape[0]

    out_rows = [(D_MODEL, F32), (D_MODEL, BF16)] + ([(sum(p.shape[0] for p in proj), BF16)] if proj else [])
    return _rowwise(name, body, [_full(da), _full(dy_next), _full(y)], [w, g, *proj], out_rows,
                    [((1, D_MODEL), F32)] * 2, tr=TM, deps=deps)


def _relu2_epilogue(acc):
    a = jnp.maximum(acc, 0.0)
    return acc, a * a


def _mlp_up(tag, h_bf, w1):
    T = h_bf.shape[0]
    tm = min(TM, T)
    return _tiled(f"{tag}_ff1", (1, T // tm), [_rb(h_bf, tm), _res(w1)],
                  [_out(T, D_FF, BF16, tm, D_FF), _out(T, D_FF, BF16, tm, D_FF)],
                  _mmc_blocks(N_DEV, NN, lambda w, d: w[d], epilogue=_relu2_epilogue), direct=True)


def _mlp_bwd_w(tag, h_bf, a, act, dff_bf, w2, deps=()):
    T = h_bf.shape[0]
    tm = min(TM, T)
    da = _tiled(f"{tag}_dact", (1, T // tm), [_rb(dff_bf, tm), _res(w2), _rb(a, tm)], [_out(T, D_FF, BF16, tm, D_FF)],
                _mmc_blocks(N_DEV, NT, lambda w, d: w[d], epilogue=lambda acc, a_t: (acc * 2.0 * jnp.maximum(a_t.astype(F32), 0.0),)),
                direct=True, deps=deps)
    dw2 = _tiled(f"{tag}_dw2", (1, D_FF // TM), [_tl(act, TM), _res(dff_bf)],
                 [_out(D_FF, D_MODEL, F32, TM, D_MODEL), _out(D_FF, D_MODEL, BF16, TM, D_MODEL)], _mmc(TN_, epilogue=_twice))
    dw1 = _tiled(f"{tag}_dw1", (N_DEV, 1), [_res(h_bf), _cw(da, TN)],
                 [_out_dev(D_MODEL, TN, D_MODEL), _out_dev(D_MODEL, TN, D_MODEL, BF16)], _mmc(TN_, epilogue=_twice))
    return da, dw1, [a.reshape(N_DEV, D_FF // N_DEV, D_MODEL) for a in dw2]


def _rope_tables(positions_col, invf_lane):
    def body(rows, consts, outs, accs):
        ang = rows[0][...].astype(F32) * consts[0][...]
        c, s = jnp.cos(ang), jnp.sin(ang)
        lane = lax.broadcasted_iota(jnp.int32, ang.shape, 1)
        outs[0][...] = jnp.where(lane < 64, 1.0, jnp.where(lane < 96, c, 0.0))
        outs[1][...] = jnp.where((lane >= 64) & (lane < 80), -s, 0.0)
        outs[2][...] = jnp.where((lane >= 80) & (lane < 96), s, 0.0)

    return _rowwise("rope_tables", body, [_full(positions_col)], [invf_lane], [(HEAD_W, F32)] * 3)


def _rope(x, c, s1, s2):
    return x * c + pltpu.roll(x, 112, 1) * s1 + pltpu.roll(x, 16, 1) * s2


def _rope_t(dx, c, s1, s2):
    return dx * c + pltpu.roll(dx * s1, 16, 1) + pltpu.roll(dx * s2, 112, 1)


def _rms(c):
    r = lax.rsqrt(jnp.mean(c * c, axis=-1, keepdims=True) + EPS)
    return c * r, r


def _rope_heads(x, c, s1, s2, fn):
    return jnp.concatenate([fn(x[:, h * HEAD_W:(h + 1) * HEAD_W], c, s1, s2) for h in range(HEADS)], axis=1)


def _mla_in(x, wm, ws, tabs, gq, gkv, deps=()):
    def body(rows, consts, outs, accs):
        xb = rows[0][...].astype(BF16)
        zm = _dot(xb, consts[0][...], NN)
        outs[0][...] = zm
        outs[1][...] = _dot(xb, consts[1][...], NN)
        outs[2][...] = (_rms(zm[:, 0:256])[0] * consts[2][...]).astype(BF16)
        outs[3][...] = (_rms(zm[:, 256:512])[0] * consts[3][...]).astype(BF16)
        outs[4][...] = _rope(zm[:, 512:640], rows[1][...], rows[2][...], rows[3][...])

    return _rowwise("l0_in", body, [_full(x)] + [_full(t) for t in tabs], [wm, ws, gq, gkv],
                    [(640, F32), (1024, F32), (256, BF16), (256, BF16), (HEAD_W, F32)], deps=deps)


def _mla_qkv(cqn, ckvn, kr_rot, tabs, wq, wk, wv):
    def body(rows, consts, outs, accs):
        c, s1, s2 = rows[3][...], rows[4][...], rows[5][...]
        outs[0][...] = _rope_heads(_dot(rows[0][...], consts[0][...], NN), c, s1, s2, _rope).astype(BF16)
        outs[1][...] = (_dot(rows[1][...], consts[1][...], NN) + jnp.concatenate([rows[2][...]] * HEADS, axis=1)).astype(BF16)
        outs[2][...] = _dot(rows[1][...], consts[2][...], NN).astype(BF16)

    rows = [_full(cqn), _full(ckvn), _full(kr_rot)] + [_full(t) for t in tabs]
    return _rowwise("l0_qkv", body, rows, [wq, wk, wv], [(HEADS * HEAD_W, BF16)] * 3)


def _mla_back(zm, cqn, ckvn, tabs, gq, gkv, wq, wk, wv, dq, dk, dv):
    def body(rows, consts, outs, accs):
        c, s1, s2 = rows[4][...], rows[5][...], rows[6][...]
        dk_t, dv_bf = rows[8][...], rows[9][...].astype(BF16)
        dq_bf = _rope_heads(rows[7][...], c, s1, s2, _rope_t).astype(BF16)
        dk_bf = dk_t.astype(BF16)
        accs[0][...] += _dot(rows[2][...], dq_bf, TN_)
        accs[1][...] += _dot(rows[3][...], dk_bf, TN_)
        accs[2][...] += _dot(rows[3][...], dv_bf, TN_)
        dlat = [_dot(dq_bf, consts[2][...], NT), _dot(dk_bf, consts[3][...], NT) + _dot(dv_bf, consts[4][...], NT)]
        for k in range(2):
            ch, r = _rms(rows[k][...])
            accs[3 + k][...] += jnp.sum(dlat[k] * ch, axis=0, keepdims=True)
            dc = dlat[k] * consts[k][...]
            outs[0][:, 256 * k:256 * (k + 1)] = (r * (dc - ch * jnp.mean(dc * ch, axis=-1, keepdims=True))).astype(BF16)
        dks = dk_t[:, 0:HEAD_W]
        for h in range(1, HEADS):
            dks = dks + dk_t[:, h * HEAD_W:(h + 1) * HEAD_W]
        lane = lax.broadcasted_iota(jnp.int32, dks.shape, 1)
        dks = jnp.where((lane >= 64) & (lane < 96), dks, 0.0)
        outs[0][:, 512:640] = _rope_t(dks, c, s1, s2).astype(BF16)

    rows = [(zm, 256, 0), (zm, 256, 1), _full(cqn), _full(ckvn)] + [_full(t) for t in tabs] + [_full(dq), _full(dk), _full(dv)]
    wide = HEADS * HEAD_W
    return _rowwise("l0_mla_back", body, rows, [gq, gkv, wq, wk, wv], [(640, BF16)],
                    [((MLA_LORA, wide), F32)] * 3 + [((1, MLA_LORA), F32)] * 2, tr=256)


def _in_back(x, dzm, dzs, dy, wm, ws, deps=()):
    def body(rows, consts, outs, accs):
        dzm_t, dzs_t = rows[1][...], rows[2][...]
        outs[0][...] = _dot(dzm_t, consts[0][...], NT) + _dot(dzs_t, consts[1][...], NT) + ALPHA * rows[3][...]
        xb = rows[0][...].astype(BF16)
        accs[0][...] += _dot(xb, dzm_t, TN_)
        accs[1][...] += _dot(xb, dzs_t, TN_)

    return _rowwise("l0_in_back", body, [_full(x), _full(dzm), _full(dzs), _full(dy)], [wm, ws], [(D_MODEL, F32)],
                    [((D_MODEL, 640), F32), ((D_MODEL, 1024), F32)], deps=deps)


def _out_weight_grads(o_att, b_out, dy_bf):
    def body(rows, consts, outs, accs):
        d = rows[2][...]
        accs[0][...] += _dot(rows[0][...].astype(BF16), d, TN_)
        accs[1][...] += _dot(rows[1][...], d, TN_)

    return _rowwise("l0_dw_out", body, [_full(o_att), _full(b_out), _full(dy_bf)], [], [],
                    [((HEADS * HEAD_W, D_MODEL), F32), ((SGU_DIM, D_MODEL), F32)])


def _attn_block(T):
    return min(1024, T)


def _attn_fwd(q, k, v):
    T = q.shape[0]
    BQ = _attn_block(T)
    nq = T // BQ

    def kern(q_ref, k_ref, v_ref, o_ref, lse_ref):
        def step(i, j, carry, masked):
            m, l, acc = carry
            qb = q_ref[pl.ds(pl.multiple_of(i * BQ, BQ), BQ), :]
            kb = k_ref[pl.ds(pl.multiple_of(j * BQ, BQ), BQ), :]
            vb = v_ref[pl.ds(pl.multiple_of(j * BQ, BQ), BQ), :]
            s = _dot(qb, kb, NT) * MLA_SCALE
            if masked:
                row = lax.broadcasted_iota(jnp.int32, s.shape, 0)
                col = lax.broadcasted_iota(jnp.int32, s.shape, 1)
                s = jnp.where(col <= row, s, -1e30)
            m_new = jnp.maximum(m, jnp.max(s, axis=-1, keepdims=True))
            p = jnp.exp(s - m_new)
            a = jnp.exp(m - m_new)
            l = a * l + jnp.sum(p, axis=-1, keepdims=True)
            acc = a * acc + _dot(p.astype(BF16), vb, NN)
            return m_new, l, acc

        def qloop(i, _):
            init = (jnp.full((BQ, 1), -1e30, F32), jnp.zeros((BQ, 1), F32), jnp.zeros((BQ, HEAD_W), F32))
            carry = lax.fori_loop(0, i, lambda j, c: step(i, j, c, False), init)
            m, l, acc = step(i, i, carry, True)
            rows = pl.ds(pl.multiple_of(i * BQ, BQ), BQ)
            o_ref[rows, :] = acc / l
            lse_ref[0, rows, :] = m + jnp.log(l)
            return 0

        lax.fori_loop(0, nq, qloop, 0)

    head = pl.BlockSpec((T, HEAD_W), lambda h: (0, h))
    nbytes = 3 * _nbytes((T, HEAD_W), BF16) + _nbytes((T, HEAD_W), F32) + _nbytes((T, 128), F32)
    return pl.pallas_call(
        kern, name="attn_fwd", grid=(HEADS,), in_specs=[head, head, head],
        out_specs=[head, pl.BlockSpec((1, T, 1), lambda h: (h, 0, 0))],
        out_shape=[pltpu.HBM((T, HEADS * HEAD_W), F32), pltpu.HBM((HEADS, T, 1), F32)],
        compiler_params=pltpu.CompilerParams(dimension_semantics=("parallel",), vmem_limit_bytes=_vmem(nbytes)),
    )(_hbm(q), _hbm(k), _hbm(v))


def _attn_bwd(q, k, v, o, lse, dcat, deps=()):
    T = q.shape[0]
    BQ = _attn_block(T)
    nq = T // BQ
    deps = _deps(deps)

    def kern(q_ref, k_ref, v_ref, o_ref, lse_ref, do_ref, *rest):
        dq_ref, dk_ref, dv_ref, dd_ref = rest[len(deps):]
        dq_ref[...] = jnp.zeros(dq_ref.shape, F32)

        def dloop(i, _):
            rows = pl.ds(pl.multiple_of(i * BQ, BQ), BQ)
            dd_ref[rows, :] = jnp.sum(do_ref[rows, :].astype(F32) * o_ref[rows, :], axis=-1, keepdims=True)
            return 0

        lax.fori_loop(0, nq, dloop, 0)

        def tile(q0, k0, n, carry, masked):
            dk_acc, dv_acc = carry
            rq = pl.ds(pl.multiple_of(q0, n), n)
            rk = pl.ds(pl.multiple_of(k0, n), n)
            qb, kb, vb, dob = q_ref[rq, :], k_ref[rk, :], v_ref[rk, :], do_ref[rq, :]
            s = _dot(qb, kb, NT) * MLA_SCALE
            p = jnp.exp(s - lse_ref[0, rq, :])
            if masked:
                row = lax.broadcasted_iota(jnp.int32, s.shape, 0)
                col = lax.broadcasted_iota(jnp.int32, s.shape, 1)
                p = jnp.where(col <= row, p, 0.0)
            dp = _dot(dob, vb, NT)
            ds = (p * (dp - dd_ref[rq, :]) * MLA_SCALE).astype(BF16)
            dv_acc = dv_acc + _dot(p.astype(BF16), dob, TN_)
            dk_acc = dk_acc + _dot(ds, qb, TN_)
            dq_ref[rq, :] += _dot(ds, kb, NN)
            return dk_acc, dv_acc

        def kloop(j, _):
            base, half = j * BQ, BQ // 2
            zero = (jnp.zeros((half, HEAD_W), F32), jnp.zeros((half, HEAD_W), F32))
            early = tile(base + half, base, half, tile(base, base, half, zero, True), False)
            late = tile(base + half, base + half, half, zero, True)
            carry = tuple(jnp.concatenate([a, b], axis=0) for a, b in zip(early, late))
            dk_acc, dv_acc = lax.fori_loop(j + 1, nq, lambda i, c: tile(i * BQ, base, BQ, c, False), carry)
            rk = pl.ds(pl.multiple_of(j * BQ, BQ), BQ)
            dk_ref[rk, :] = dk_acc
            dv_ref[rk, :] = dv_acc
            return 0

        lax.fori_loop(0, nq, kloop, 0)

    head = pl.BlockSpec((T, HEAD_W), lambda h: (0, h))
    nbytes = 4 * _nbytes((T, HEAD_W), BF16) + 5 * _nbytes((T, HEAD_W), F32) + 2 * _nbytes((T, 128), F32)
    return pl.pallas_call(
        kern, name="attn_bwd", grid=(HEADS,),
        in_specs=[head, head, head, head, pl.BlockSpec((1, T, 1), lambda h: (h, 0, 0)), head] + [ANY_SPEC] * len(deps),
        out_specs=[head, head, head],
        out_shape=[pltpu.HBM((T, HEADS * HEAD_W), F32)] * 3,
        scratch_shapes=[pltpu.VMEM((T, 1), F32)],
        compiler_params=pltpu.CompilerParams(dimension_semantics=("parallel",), vmem_limit_bytes=_vmem(nbytes)),
    )(*[_hbm(a) for a in (q, k, v, o, lse, dcat)], *deps)


def _sgu_common(u, v, ln_g, ln_b):
    ua, tu = _gelu(u)
    va, tv = _gelu(v)
    vh, r = _ln_stats(va)
    return ua, tu, tv, vh, r, vh * ln_g + ln_b


def _tril_mask(n):
    return lax.broadcasted_iota(jnp.int32, (n, n), 1) <= lax.broadcasted_iota(jnp.int32, (n, n), 0)


def _sgu_fwd(zs, ln_g, ln_b, w, bias_full):
    def body(rows, consts, outs, accs):
        ua, _, _, _, _, vn = _sgu_common(rows[0][...], rows[1][...], consts[0][...], consts[1][...])
        vn = vn.astype(BF16)
        tri = _tril_mask(SGU_CHUNK)
        for g in range(SGU_G):
            wg = jnp.where(tri, consts[2][0, g], 0.0).astype(BF16)
            cols = slice(g * 128, (g + 1) * 128)
            for c in range(ua.shape[0] // SGU_CHUNK):
                rws = slice(c * SGU_CHUNK, (c + 1) * SGU_CHUNK)
                mixed = _dot(wg, vn[rws, cols], NN) + consts[3][:, cols]
                outs[0][rws, cols] = (ua[rws, cols] * mixed).astype(BF16)

    return _rowwise("sgu_fwd", body, [(zs, 512, 0), (zs, 512, 1)], [ln_g, ln_b, w, bias_full], [(SGU_DIM, BF16)])


def _sgu_bwd(zs, dcat, ln_g, ln_b, w, bias_full):
    def body(rows, consts, outs, accs):
        u, v = rows[0][...], rows[1][...]
        ua, tu, tv, vh, r, vn = _sgu_common(u, v, consts[0][...], consts[1][...])
        dout = rows[2][...].astype(F32)
        vn_bf = vn.astype(BF16)
        tri = _tril_mask(SGU_CHUNK)
        dmixed = (dout * ua)
        dmixed_bf = dmixed.astype(BF16)
        ones = jnp.ones((8, SGU_CHUNK), F32)
        dvn_cols, mixed_cols = [], []
        for g in range(SGU_G):
            wg = jnp.where(tri, consts[2][0, g], 0.0).astype(BF16)
            cols = slice(g * 128, (g + 1) * 128)
            dvn_rows, mixed_rows = [], []
            dw = jnp.zeros((SGU_CHUNK, SGU_CHUNK), F32)
            dmix_sum = jnp.zeros((SGU_CHUNK, 128), F32)
            for c in range(u.shape[0] // SGU_CHUNK):
                rws = slice(c * SGU_CHUNK, (c + 1) * SGU_CHUNK)
                mixed_rows.append(_dot(wg, vn_bf[rws, cols], NN) + consts[3][:, cols])
                dvn_rows.append(_dot(wg, dmixed_bf[rws, cols], TN_))
                dw = dw + _dot(dmixed_bf[rws, cols], vn_bf[rws, cols], NT)
                dmix_sum = dmix_sum + dmixed[rws, cols]
            accs[0][g] += jnp.where(tri, dw, 0.0)
            accs[3][g:g + 1, :] += _dot(ones, dmix_sum, NT, precision=HIGHEST)[0:1, :]
            dvn_cols.append(jnp.concatenate(dvn_rows, axis=0))
            mixed_cols.append(jnp.concatenate(mixed_rows, axis=0))
        dvn = jnp.concatenate(dvn_cols, axis=1)
        mixed = jnp.concatenate(mixed_cols, axis=1)
        accs[1][...] += jnp.sum(dvn * vh, axis=0, keepdims=True)
        accs[2][...] += jnp.sum(dvn, axis=0, keepdims=True)
        dvh = dvn * consts[0][...]
        dva = r * (dvh - jnp.mean(dvh, axis=-1, keepdims=True) - vh * jnp.mean(dvh * vh, axis=-1, keepdims=True))
        outs[0][:, 0:512] = (dout * mixed * _gelu_grad(u, tu)).astype(BF16)
        outs[0][:, 512:1024] = (dva * _gelu_grad(v, tv)).astype(BF16)

    return _rowwise("sgu_bwd", body, [(zs, 512, 0), (zs, 512, 1), (dcat, 512, 2)], [ln_g, ln_b, w, bias_full], [(1024, BF16)],
                    [((SGU_G, 128, 128), F32), ((1, SGU_DIM), F32), ((1, SGU_DIM), F32), ((SGU_G, 128), F32)], tr=256)


def _lower_bound(hg_lb):
    a0, a1 = hg_lb[0:1, :], hg_lb[1:2, :]
    m = jnp.maximum(a0, a1)
    e0, e1 = jnp.exp(a0 - m), jnp.exp(a1 - m)
    s0, s1 = e0 / (e0 + e1), e1 / (e0 + e1)
    return (s0 + s1) - s0, s0, s1


def _prefix_rows(x, reverse=False):
    n = x.shape[0]
    row = lax.broadcasted_iota(jnp.int32, x.shape, 0)
    s = 1
    while s < n:
        if reverse:
            x = x + jnp.where(row < n - s, pltpu.roll(x, n - s, 0), 0.0)
        else:
            x = x + jnp.where(row >= s, pltpu.roll(x, s, 0), 0.0)
        s *= 2
    return x


def _hg_gates(qr, fr, lb):
    C = qr.shape[0]
    sq = _sig(qr)
    qf = qr * sq
    sf = _sig(fr)
    gate = lb + (1.0 - lb) * sf
    kk = 1.0 - gate
    tri = _tril_mask(C)
    b = _prefix_rows(jnp.log(gate))
    bref = b[C // 2 - 1:C // 2, :]
    bl = b[C - 1:C, :]
    e_b = jnp.exp(b)
    e_q = jnp.exp(b - bref)
    e_k = jnp.exp(bref - b)
    e_lb = jnp.exp(bl - b)
    return dict(sq=sq, qf=qf, sf=sf, gate=gate, kk=kk, tri=tri, bl=bl, e_b=e_b, e_q=e_q, e_k=e_k, e_lb=e_lb)


def _hgrn_fwd(z1, hg_lb, gnorm):
    T = z1.shape[0]
    C = min(HG_CHUNK, T)
    nc = T // C
    ns = HG_CHUNKS_PER_STEP if nc % HG_CHUNKS_PER_STEP == 0 else 1
    R = ns * C

    def kern(q_ref, f_ref, i_ref, g_ref, lb_ref, gn_ref, o_ref, hg_ref, st_ref, s_scr):
        @pl.when(pl.program_id(0) == 0)
        def _():
            s_scr[...] = jnp.zeros(s_scr.shape, F32)

        lb_all, _, _ = _lower_bound(lb_ref[...])
        for sub in range(ns):
            rows = slice(sub * C, (sub + 1) * C)
            st_ref[sub] = s_scr[...]
            for h in range(HEADS):
                cols = slice(h * HEAD_W, (h + 1) * HEAD_W)
                t = _hg_gates(q_ref[rows, cols], f_ref[rows, cols], lb_all[:, cols])
                v_bf = i_ref[rows, cols].astype(BF16)
                st = s_scr[h]
                a = jnp.where(t["tri"], _dot((t["qf"] * t["e_q"]).astype(BF16), (t["kk"] * t["e_k"]).astype(BF16), NT), 0.0)
                o = _dot(a.astype(BF16), v_bf, NN) + _dot((t["qf"] * t["e_b"]).astype(BF16), st.astype(BF16), NT)
                s_scr[h] = st * jnp.exp(t["bl"]) + _dot(v_bf, (t["kk"] * t["e_lb"]).astype(BF16), TN_)
                o_ref[rows, cols] = o
                gr = g_ref[rows, cols]
                r = lax.rsqrt(jnp.mean(o * o, axis=-1, keepdims=True) + EPS)
                hg_ref[rows, cols] = (o * r * gn_ref[:, cols] * (gr * _sig(gr))).astype(BF16)

    seg = lambda k: pl.BlockSpec((R, D_MODEL), functools.partial(lambda n, k: (n, k), k=k))
    row = pl.BlockSpec((R, D_MODEL), lambda n: (n, 0))
    nbytes = 6 * _nbytes((R, D_MODEL), F32) + (2 + ns) * _nbytes((HEADS, 128, 128), F32)
    return pl.pallas_call(
        kern, name="hgrn_fwd", grid=(nc // ns,),
        in_specs=[seg(0), seg(1), seg(2), seg(3), pl.BlockSpec((2, D_MODEL), lambda n: (0, 0)),
                  pl.BlockSpec((1, D_MODEL), lambda n: (0, 0))],
        out_specs=[row, row, pl.BlockSpec((ns, HEADS, 128, 128), lambda n: (n, 0, 0, 0))],
        out_shape=[pltpu.HBM((T, D_MODEL), F32), pltpu.HBM((T, D_MODEL), BF16),
                   pltpu.HBM((nc, HEADS, 128, 128), F32)],
        scratch_shapes=[pltpu.VMEM((HEADS, 128, 128), F32)],
        compiler_params=pltpu.CompilerParams(dimension_semantics=("arbitrary",), vmem_limit_bytes=_vmem(nbytes)),
    )(*[_hbm(a) for a in (z1, z1, z1, z1, hg_lb, gnorm)])


def _hgrn_bwd(z1, o_pre, dhg, states, hg_lb, gnorm):
    T = z1.shape[0]
    C = min(HG_CHUNK, T)
    nc = T // C
    ns = HG_CHUNKS_PER_STEP if nc % HG_CHUNKS_PER_STEP == 0 else 1
    R, steps = ns * C, nc // ns

    def kern(q_ref, f_ref, i_ref, g_ref, o_ref, dhg_ref, st_ref, lb_ref, gn_ref, dz_ref, dlb_ref, dgn_ref, ds_scr, dlb_scr):
        n = pl.program_id(0)

        @pl.when(n == 0)
        def _():
            ds_scr[...] = jnp.zeros(ds_scr.shape, F32)
            dlb_scr[...] = jnp.zeros(dlb_scr.shape, F32)
            dgn_ref[...] = jnp.zeros(dgn_ref.shape, F32)

        lb_all, s0, s1 = _lower_bound(lb_ref[...])
        for sub in reversed(range(ns)):
            rows = slice(sub * C, (sub + 1) * C)
            for h in range(HEADS):
                cols = slice(h * HEAD_W, (h + 1) * HEAD_W)
                lb = lb_all[:, cols]
                qr, fr = q_ref[rows, cols], f_ref[rows, cols]
                t = _hg_gates(qr, fr, lb)
                tri = t["tri"]
                v_bf = i_ref[rows, cols].astype(BF16)
                st_bf = st_ref[sub, h].astype(BF16)
                dst = ds_scr[h]
                dst_bf = dst.astype(BF16)
                o = o_ref[rows, cols]
                gr = g_ref[rows, cols]
                sg = _sig(gr)
                sil = gr * sg
                gn = gn_ref[:, cols]
                r = lax.rsqrt(jnp.mean(o * o, axis=-1, keepdims=True) + EPS)
                on = o * r
                dh = dhg_ref[rows, cols].astype(F32)
                dgn_ref[:, cols] += jnp.sum(dh * on * sil, axis=0, keepdims=True)
                dg = dh * on * gn * (sg * (1.0 + gr * (1.0 - sg)))
                don = dh * gn * sil
                do_bf = (r * (don - on * jnp.mean(don * on, axis=-1, keepdims=True))).astype(BF16)
                qe = (t["qf"] * t["e_q"]).astype(BF16)
                ke = (t["kk"] * t["e_k"]).astype(BF16)
                qb = (t["qf"] * t["e_b"]).astype(BF16)
                kh_bf = (t["kk"] * t["e_lb"]).astype(BF16)
                a_bf = jnp.where(tri, _dot(qe, ke, NT), 0.0).astype(BF16)
                da_bf = jnp.where(tri, _dot(do_bf, v_bf, NT), 0.0).astype(BF16)
                dv = _dot(a_bf, do_bf, TN_) + _dot(kh_bf, dst_bf, NT)
                dqe = _dot(da_bf, ke, NN)
                dqb = _dot(do_bf, st_bf, NN)
                dke = _dot(da_bf, qe, TN_)
                dkh = _dot(v_bf, dst_bf, NN)
                dqf = dqe * t["e_q"] + dqb * t["e_b"]
                dkk = dke * t["e_k"] + dkh * t["e_lb"]
                kh_r = kh_bf.astype(F32)
                db = qe.astype(F32) * dqe - ke.astype(F32) * dke + qb.astype(F32) * dqb - kh_r * dkh
                e_bl = jnp.exp(t["bl"])
                dbl = jnp.sum(dkh * kh_r, axis=0, keepdims=True) + e_bl * jnp.sum(st_ref[sub, h] * dst, axis=0, keepdims=True)
                dlg = _prefix_rows(db, reverse=True) + dbl
                ds_scr[h] = dst * e_bl + _dot(do_bf, qb, TN_)
                dgate = dlg / t["gate"] - dkk
                sf = t["sf"]
                dlb_scr[:, cols] += jnp.sum(dgate * (1.0 - sf), axis=0, keepdims=True)
                df = dgate * (1.0 - lb) * sf * (1.0 - sf)
                dq = dqf * (t["sq"] * (1.0 + qr * (1.0 - t["sq"])))
                dz_ref[rows, cols] = dq.astype(BF16)
                dz_ref[rows, D_MODEL + h * HEAD_W:D_MODEL + (h + 1) * HEAD_W] = df.astype(BF16)
                dz_ref[rows, 2 * D_MODEL + h * HEAD_W:2 * D_MODEL + (h + 1) * HEAD_W] = dv.astype(BF16)
                dz_ref[rows, 3 * D_MODEL + h * HEAD_W:3 * D_MODEL + (h + 1) * HEAD_W] = dg.astype(BF16)

        @pl.when(n == steps - 1)
        def _():
            d = s0 * s1 * dlb_scr[...]
            dlb_ref[0:1, :] = -d
            dlb_ref[1:2, :] = d

    seg = lambda k: pl.BlockSpec((R, D_MODEL), functools.partial(lambda n, k: (steps - 1 - n, k), k=k))
    nbytes = 6 * _nbytes((R, D_MODEL), F32) + _nbytes((R, 4 * D_MODEL), BF16) + (2 + ns) * _nbytes((HEADS, 128, 128), F32)
    return pl.pallas_call(
        kern, name="hgrn_bwd", grid=(steps,),
        in_specs=[seg(0), seg(1), seg(2), seg(3), seg(0), seg(0),
                  pl.BlockSpec((ns, HEADS, 128, 128), lambda n: (steps - 1 - n, 0, 0, 0)),
                  pl.BlockSpec((2, D_MODEL), lambda n: (0, 0)), pl.BlockSpec((1, D_MODEL), lambda n: (0, 0))],
        out_specs=[pl.BlockSpec((R, 4 * D_MODEL), lambda n: (steps - 1 - n, 0)),
                   pl.BlockSpec((2, D_MODEL), lambda n: (0, 0)), pl.BlockSpec((1, D_MODEL), lambda n: (0, 0))],
        out_shape=[pltpu.HBM((T, 4 * D_MODEL), BF16), pltpu.HBM((2, D_MODEL), F32),
                   pltpu.HBM((1, D_MODEL), F32)],
        scratch_shapes=[pltpu.VMEM((HEADS, 128, 128), F32), pltpu.VMEM((1, D_MODEL), F32)],
        compiler_params=pltpu.CompilerParams(dimension_semantics=("arbitrary",), vmem_limit_bytes=_vmem(nbytes)),
    )(*[_hbm(a) for a in (z1, z1, z1, z1, o_pre, dhg, states, hg_lb, gnorm)])


def _prep_weights(gw):
    w_in_e = gw["w_in_e"].transpose(1, 0, 2).reshape(D_MODEL, 1568)
    kr = jnp.pad(w_in_e[:, 512:544], ((0, 0), (64, 32)))
    wm = jnp.concatenate([w_in_e[:, 0:512], kr], axis=1)
    ws = w_in_e[:, 544:1568]
    w_qb = gw["w_qb"].transpose(1, 0, 2).reshape(MLA_LORA, HEADS, 96)
    wq = jnp.pad(w_qb, ((0, 0), (0, 0), (0, 32))).reshape(MLA_LORA, HEADS * HEAD_W)
    kvb = gw["w_kvb"].transpose(1, 0, 2).reshape(MLA_LORA, HEADS, 128)
    wk = jnp.pad(kvb[:, :, :64], ((0, 0), (0, 0), (0, 64))).reshape(MLA_LORA, HEADS * HEAD_W)
    wv = jnp.pad(kvb[:, :, 64:], ((0, 0), (0, 0), (0, 64))).reshape(MLA_LORA, HEADS * HEAD_W)
    w_out_e = gw["w_out_e"].reshape(D_MODEL, D_MODEL)
    woa = jnp.pad(w_out_e[:512].reshape(HEADS, 64, D_MODEL), ((0, 0), (0, 64), (0, 0))).reshape(HEADS * HEAD_W, D_MODEL)
    return dict(wm=wm, ws=ws, wq=wq, wk=wk, wv=wv, woa=woa, wob=w_out_e[512:])


def _unprep_grads(g):
    dwm, dws = g["wm"], g["ws"]
    d_in_e = jnp.concatenate([dwm[:, 0:512], dwm[:, 512 + 64:512 + 96], dws], axis=1)
    d_qb = g["wq"].reshape(MLA_LORA, HEADS, HEAD_W)[:, :, :96].reshape(MLA_LORA, HEADS * 96)
    dk = g["wk"].reshape(MLA_LORA, HEADS, HEAD_W)[:, :, :64]
    dv = g["wv"].reshape(MLA_LORA, HEADS, HEAD_W)[:, :, :64]
    d_kvb = jnp.concatenate([dk, dv], axis=2).reshape(MLA_LORA, HEADS * 128)
    d_oa = g["woa"].reshape(HEADS, HEAD_W, D_MODEL)[:, :64].reshape(HEADS * 64, D_MODEL)
    dev_major = lambda a: a.reshape(a.shape[0], N_DEV, a.shape[1] // N_DEV).transpose(1, 0, 2)
    return dict(w_in_e=dev_major(d_in_e), w_qb=dev_major(d_qb), w_kvb=dev_major(d_kvb),
                w_out_e=jnp.concatenate([d_oa, g["wob"]], axis=0).reshape(N_DEV, D_MODEL // N_DEV, D_MODEL))


def _local_step(x, positions, target, gw, sp, ex):
    w = _prep_weights(gw)
    T = x.shape[0]
    tm = min(TM, T)
    nt = T // tm
    half = MLA_ROPE // 2
    inv_freq = ROPE_BASE ** (-jnp.arange(half, dtype=F32) / half)
    invf_lane = jnp.concatenate([jnp.zeros((64,), F32), inv_freq, inv_freq, jnp.zeros((32,), F32)]).reshape(1, HEAD_W)
    tabs = _rope_tables(positions.reshape(T, 1), invf_lane)
    bias_full = jnp.repeat(sp["sgu_b"][0].T, 128, axis=1)
    sgu_w = sp["sgu_w"]
    gq, gkv = sp["mla_gq"], sp["mla_gkv"]
    ln1_g, ln1_b, ln2_g, ln2_b = sp["ln1_g"], sp["ln1_b"], sp["ln2_g"], sp["ln2_b"]
    zm, zs, cqn, ckvn, kr_rot = _mla_in(x, w["wm"], w["ws"], tabs, gq, gkv, deps=[ex.first_token])
    q, k, v = _mla_qkv(cqn, ckvn, kr_rot, tabs, w["wq"], w["wk"], w["wv"])
    o_att, lse = _attn_fwd(q, k, v)
    b_out = _sgu_fwd(zs, sp["sgu_ln_g"], sp["sgu_ln_b"], sgu_w, bias_full)
    token = ex.weights_forward(after=[o_att, b_out])
    y1, h1, h1_bf = _proj_ln("l0_out_ln1", [o_att, b_out], [w["woa"], w["wob"]], x, ln1_g, ln1_b, 0, deps=[token])
    big = ex.weights_ready(after=[y1])
    w_ff1, w_in_o, w_out_o = big["w_ff1"], big["w_in_o"], big["w_out_o"].reshape(D_MODEL, D_MODEL)
    w_ff2 = [a.reshape(D_FF, D_MODEL) for a in big["w_ff2"]]
    a0, act0 = _mlp_up("l0", h1_bf, w_ff1[0])
    y2, h2, h2_bf = _proj_ln("l0_ff2_ln2", [act0], [w_ff2[0]], h1, ln2_g, ln2_b, 0)

    z1 = _tiled("l1_in", (1, nt), [_rb(h2_bf, tm), _res(w_in_o)], [_out(T, 4 * D_MODEL, F32, tm, 4 * D_MODEL)],
                _mmc_blocks(N_DEV, NN, lambda w, d: w[d]), direct=True)
    o_pre, hg, states = _hgrn_fwd(z1, sp["hg_lb"], sp["hg_gnorm"])
    y3, h3, h3_bf = _proj_ln("l1_out_ln1", [hg], [w_out_o], h2, ln1_g, ln1_b, 1)
    a1, act1 = _mlp_up("l1", h3_bf, w_ff1[1])

    gs, g0 = {}, {}
    dy4, dy4_bf, sq_err, gs["ln2_g1"], gs["ln2_b1"] = _proj_ln_loss("l1_ff2_loss", act1, w_ff2[1], h3, ln2_g, ln2_b, 1, target)
    gs["sq_err"] = sq_err
    da1, dw1_1, dw2_1 = _mlp_bwd_w("l1", h3_bf, a1, act1, dy4_bf, big["w_ff2"][1])
    dy3, dy3_bf, dhg, gs["ln1_g1"], gs["ln1_b1"] = _dh_ln_back("l1_dh_ln1", da1, w_ff1[1], dy4, y3, ln1_g, 1, proj=[w_out_o])
    d_out_o = _tiled("l1_dwout", (2, D_MODEL // TM), [_tl(hg, TM), _cw(dy3_bf, TN)],
                     [_out(D_MODEL, D_MODEL, F32, TM, TN), _out(D_MODEL, D_MODEL, BF16, TM, TN)], _mmc(TN_, epilogue=_twice))
    d_out_o = [a.reshape(N_DEV, D_MODEL // N_DEV, D_MODEL) for a in d_out_o]
    dz1, gs["hg_lb"], gs["hg_gnorm"] = _hgrn_bwd(z1, o_pre, dhg, states, sp["hg_lb"], sp["hg_gnorm"])
    d_in_o = _tiled("l1_dwin", (N_DEV, 1), [_res(h2_bf), _cw(dz1, TN)],
                    [_out_dev(D_MODEL, TN, D_MODEL), _out_dev(D_MODEL, TN, D_MODEL, BF16)], _mmc(TN_, epilogue=_twice))
    token = ex.direct_start("l1", [dw1_1, dw2_1, d_in_o, d_out_o])

    dy2, dy2_bf, gs["ln2_g0"], gs["ln2_b0"] = _dh_ln_back("l1_dh_ln2", dz1, w_in_o, dy3, y2, ln2_g, 0, deps=[token])
    da0, dw1_0, dw2_0 = _mlp_bwd_w("l0", h1_bf, a0, act0, dy2_bf, big["w_ff2"][0])
    token = ex.direct_start("l0m", [dw1_0, dw2_0])
    dy1, dy1_bf, dcat, gs["ln1_g0"], gs["ln1_b0"] = _dh_ln_back("l0_dh_ln1", da0, w_ff1[0], dy2, y1, ln1_g, 0,
                                                                 proj=[w["woa"], w["wob"]], deps=[token])
    g0["woa"], g0["wob"] = _out_weight_grads(o_att, b_out, dy1_bf)
    dzs, gs["sgu_w"], gs["sgu_ln_g"], gs["sgu_ln_b"], gs["sgu_b"] = _sgu_bwd(zs, dcat, sp["sgu_ln_g"], sp["sgu_ln_b"], sgu_w, bias_full)
    dq, dk, dv = _attn_bwd(q, k, v, o_att, lse, dcat)
    dzm, g0["wq"], g0["wk"], g0["wv"], gs["mla_gq"], gs["mla_gkv"] = _mla_back(zm, cqn, ckvn, tabs, gq, gkv, w["wq"], w["wk"], w["wv"],
                                                                                 dq, dk, dv)
    token = ex.small_start(gs)
    dx, g0["wm"], g0["ws"] = _in_back(x, dzm, dzs, dy1, w["wm"], w["ws"], deps=[token])

    return sq_err, dx, _unprep_grads(g0), gs


def _me():
    return lax.axis_index("x"), lax.axis_index("y"), lax.axis_index("c")


ANY_SPEC = pl.BlockSpec(memory_space=pl.ANY)
HBM_SPEC = pl.BlockSpec(memory_space=pltpu.HBM)
SEM_SPEC = pl.BlockSpec(memory_space=pltpu.SEMAPHORE)
EFFECT = pltpu.SideEffectType.DATAFLOW_SIDE_EFFECTING


def _split_start(name, srcs, lands, n_sems, make_copies, after=()):
    n, m, k = len(srcs), len(lands), len(after)

    def body(*refs):
        for cp in make_copies(refs[:n], refs[n:n + m], refs[n + m + k], refs[n + m + k + 1]):
            cp.start()
        refs[-1][...] = jnp.zeros(refs[-1].shape, F32)

    out_shape = (pltpu.SemaphoreType.DMA((n_sems,)), pltpu.SemaphoreType.DMA((n_sems,)),
                 *[pltpu.HBM(a.shape, a.dtype) for a in (*srcs, *lands)], jax.ShapeDtypeStruct((8, 128), F32))
    res = pl.pallas_call(
        body, name=name, out_shape=out_shape, in_specs=[HBM_SPEC] * (n + m) + [ANY_SPEC] * k,
        out_specs=(SEM_SPEC, SEM_SPEC, *[HBM_SPEC] * (n + m), pl.BlockSpec(memory_space=pltpu.VMEM)),
        input_output_aliases={i: 2 + i for i in range(n + m)},
        compiler_params=pltpu.CompilerParams(has_side_effects=EFFECT),
    )(*[_hbm(a) for a in (*srcs, *lands)], *after)
    return res[0], res[1], list(res[2:2 + n]), list(res[2 + n:2 + n + m]), res[-1]


def _split_wait(name, send_sems, recv_sems, srcs, lands, after, make_copies):
    n, m = len(srcs), len(lands)

    def body(*refs):
        for cp in make_copies(refs[:n], refs[n:n + m], refs[n + m], refs[n + m + 1]):
            cp.wait_send()
            cp.wait_recv()

    res = pl.pallas_call(
        body, name=name, out_shape=tuple(pltpu.HBM(a.shape, a.dtype) for a in (*srcs, *lands)),
        in_specs=[HBM_SPEC] * (n + m) + [SEM_SPEC, SEM_SPEC] + [ANY_SPEC] * len(after), out_specs=tuple([HBM_SPEC] * (n + m)),
        input_output_aliases={i: i for i in range(n + m)},
        compiler_params=pltpu.CompilerParams(has_side_effects=EFFECT),
    )(*srcs, *lands, send_sems, recv_sems, *after)
    return list(res[:n]), list(res[n:])


def _place_own(shards, dev):
    n = len(shards)

    def kern(dev_ref, *refs):
        for x_ref, o_ref in zip(refs[:n], refs[n:]):
            o_ref[...] = x_ref[...].astype(o_ref.dtype)

    blocks = [(None, *a.shape[1:]) for a, _, _ in shards]
    nbytes = sum(_nbytes(b, a.dtype) + _nbytes(b, dt) for b, (a, _, dt) in zip(blocks, shards))
    return pl.pallas_call(
        kern, name="weights_place_own", out_shape=[pltpu.HBM((N_DEV, *a.shape[1:]), dt) for a, _, dt in shards],
        grid_spec=pltpu.PrefetchScalarGridSpec(
            num_scalar_prefetch=1, grid=(1,),
            in_specs=[pl.BlockSpec(b, functools.partial(lambda i, dev, l: (l, 0, 0), l=l)) for b, (_, l, _) in zip(blocks, shards)],
            out_specs=[pl.BlockSpec(b, lambda i, dev: (dev[0], 0, 0)) for b in blocks]),
        compiler_params=pltpu.CompilerParams(dimension_semantics=("arbitrary",), vmem_limit_bytes=_vmem(nbytes)),
    )(dev, *[_hbm(a) for a, _, _ in shards])


def _ag_first_copies(src_refs, out_refs, send_sems, recv_sems):
    x, y, c = _me()
    targets = [(x, y, 1 - c), (1 - x, y, c), (x, 1 - y, c), (1 - x, 1 - y, c)]
    return [pltpu.make_async_remote_copy(
        src_ref=out_refs[op].at[4 * x + 2 * y + c], dst_ref=out_refs[op].at[4 * x + 2 * y + c], send_sem=send_sems.at[4 * op + k],
        recv_sem=recv_sems.at[4 * op + k], device_id=to, device_id_type=MESH)
        for op in range(len(out_refs)) for k, to in enumerate(targets)]


def _ag_second_copies(src_refs, out_refs, send_sems, recv_sems):
    x, y, c = _me()
    chips = [(1 - x, y), (x, 1 - y), (1 - x, 1 - y)]
    return [pltpu.make_async_remote_copy(
        src_ref=out_refs[op].at[4 * cx + 2 * cy + c], dst_ref=out_refs[op].at[4 * cx + 2 * cy + c],
        send_sem=send_sems.at[3 * op + j], recv_sem=recv_sems.at[3 * op + j], device_id=(x, y, 1 - c), device_id_type=MESH)
        for op in range(len(out_refs)) for j, (cx, cy) in enumerate(chips)]


def _rs_sibling_copies(g_refs, out_refs, send_sems, recv_sems):
    x, y, c = _me()
    return [pltpu.make_async_remote_copy(
        src_ref=g_refs[op].at[k, 1 - c], dst_ref=out_refs[op].at[k], send_sem=send_sems.at[4 * op + k],
        recv_sem=recv_sems.at[4 * op + k], device_id=(x, y, 1 - c), device_id_type=MESH)
        for op in range(len(g_refs)) for k in range(4)]


def _rs_direct_copies(g_refs, land_refs, send_sems, recv_sems):
    x, y, c = _me()
    n = len(g_refs) // 2
    chips = [(1 - x, y), (x, 1 - y), (1 - x, 1 - y)]
    copies = []
    for op in range(n):
        g32, g16, from_sib, from_others = g_refs[op], g_refs[n + op], land_refs[op], land_refs[n + op]
        copies.append(pltpu.make_async_remote_copy(
            src_ref=g32.at[2 * x + y, 1 - c], dst_ref=from_sib, send_sem=send_sems.at[7 * op], recv_sem=recv_sems.at[7 * op],
            device_id=(x, y, 1 - c), device_id_type=MESH))
        for j, (cx, cy) in enumerate(chips):
            for s, cc in enumerate((c, 1 - c)):
                copies.append(pltpu.make_async_remote_copy(
                    src_ref=g16.at[2 * cx + cy, cc], dst_ref=from_others.at[2 * j + s], send_sem=send_sems.at[7 * op + 1 + 2 * j + s],
                    recv_sem=recv_sems.at[7 * op + 1 + 2 * j + s], device_id=(cx, cy, cc), device_id_type=MESH))
    return copies


def _rs_chip_copies(p_refs, out_refs, send_sems, recv_sems):
    x, y, c = _me()
    chips = [(1 - x, y), (x, 1 - y), (1 - x, 1 - y)]
    return [pltpu.make_async_remote_copy(
        src_ref=p_refs[op].at[2 * cx + cy], dst_ref=out_refs[op].at[j], send_sem=send_sems.at[3 * op + j],
        recv_sem=recv_sems.at[3 * op + j], device_id=(cx, cy, c), device_id_type=MESH)
        for op in range(len(p_refs)) for j, (cx, cy) in enumerate(chips)]


def _all_gather(placed):
    n = len(placed)

    def kern(*refs):
        in_refs, out_refs, (send_sems, recv_sems) = refs[:n], refs[n:2 * n], refs[2 * n:]
        x, y, c = _me()
        me, sibling = (x, y, c), (x, y, 1 - c)
        chips = [(1 - x, y), (x, 1 - y), (1 - x, 1 - y)]

        def copy(op, k, block, to, own=False):
            idx = 4 * block[0] + 2 * block[1] + block[2]
            return pltpu.make_async_remote_copy(
                src_ref=(in_refs if own else out_refs)[op].at[idx], dst_ref=out_refs[op].at[idx], send_sem=send_sems.at[7 * op + k],
                recv_sem=recv_sems.at[7 * op + k], device_id=to, device_id_type=MESH)

        first = []
        for op in range(n):
            first.append(copy(op, 0, me, sibling, own=True))
            first += [copy(op, 1 + j, me, (*chip, c), own=True) for j, chip in enumerate(chips)]
        for cp in first:
            cp.start()
        passed = []
        for j, chip in enumerate(chips):
            for op in range(n):
                copy(op, 1 + j, (*chip, c), me).wait_recv()
                passed.append(copy(op, 4 + j, (*chip, c), sibling))
                passed[-1].start()
        for op in range(n):
            copy(op, 0, sibling, me).wait_recv()
            for j, chip in enumerate(chips):
                copy(op, 4 + j, (*chip, 1 - c), me).wait_recv()
        for cp in first + passed:
            cp.wait_send()

    return pl.pallas_call(
        kern, name="weights_all_gather", out_shape=[pltpu.HBM(g.shape, g.dtype) for g in placed],
        in_specs=[ANY_SPEC] * n, out_specs=[ANY_SPEC] * n, input_output_aliases={i: i for i in range(n)},
        scratch_shapes=[pltpu.SemaphoreType.DMA((7 * n,)), pltpu.SemaphoreType.DMA((7 * n,))],
    )(*[_hbm(a) for a in placed])


def _row_tile(r, w, n_blocks):
    tr = r
    while tr > 8 and 2 * n_blocks * tr * w * 4 > 24 * 2**20:
        tr //= 2
    return tr


def _chip_sum(name, g, from_sibling, core):
    _, _, R, W = g.shape
    tr = _row_tile(R, W, 3)

    def kern(core_ref, g_ref, s_ref, o_ref):
        o_ref[...] = (g_ref[...] + s_ref[...]).astype(BF16)

    return pl.pallas_call(
        kern, name=name, out_shape=pltpu.HBM((4, R, W), BF16),
        grid_spec=pltpu.PrefetchScalarGridSpec(
            num_scalar_prefetch=1, grid=(4, R // tr),
            in_specs=[pl.BlockSpec((None, None, tr, W), lambda k, i, core: (k, core[0], i, 0)),
                      pl.BlockSpec((None, tr, W), lambda k, i, core: (k, i, 0))],
            out_specs=pl.BlockSpec((None, tr, W), lambda k, i, core: (k, i, 0))),
        compiler_params=pltpu.CompilerParams(dimension_semantics=("parallel", "parallel"), vmem_limit_bytes=_vmem(3 * tr * W * 4)),
    )(core, _hbm(g), _hbm(from_sibling))


def _adamw(w, g, m, v):
    m = ADAM_B1 * m + (1.0 - ADAM_B1) * g
    v = ADAM_B2 * v + (1.0 - ADAM_B2) * (g * g)
    m_hat = m / (1.0 - ADAM_B1 ** ADAM_STEP)
    v_hat = v / (1.0 - ADAM_B2 ** ADAM_STEP)
    return -ADAM_LR * (m_hat / (jnp.sqrt(v_hat) + ADAM_EPS) + ADAM_WD * w), m, v


def _finish_sharded(name, layers, w, m, v, where, deps=()):
    nl, R, W = w.shape
    n_other = layers[0][2].shape[0]
    tr = _row_tile(R, W, (8 + n_other) * nl)
    deps = _deps(deps)

    def kern(where_ref, *refs):
        w_ref, m_ref, v_ref = refs[3 * nl:3 * nl + 3]
        go_ref, d_ref, mo_ref, vo_ref = refs[3 * nl + 3 + len(deps):]
        for l in range(nl):
            g_ref, s_ref, c_ref = refs[3 * l:3 * l + 3]
            grad = g_ref[...] + s_ref[...]
            for j in range(n_other):
                grad = grad + c_ref[j].astype(F32)
            go_ref[l] = grad
            d_ref[l], mo_ref[l], vo_ref[l] = _adamw(w_ref[l], grad, m_ref[l], v_ref[l])

    row = pl.BlockSpec((nl, tr, W), lambda i, wh: (0, i, 0))
    in_specs, args = [], []
    for g, s, c in layers:
        sib = (pl.BlockSpec((None, tr, W), lambda i, wh: (wh[0], i, 0)) if s.ndim == 3 else pl.BlockSpec((tr, W), lambda i, wh: (i, 0)))
        in_specs += [pl.BlockSpec((None, None, tr, W), lambda i, wh: (wh[0], wh[1], i, 0)), sib,
                     pl.BlockSpec((n_other, tr, W), lambda i, wh: (0, i, 0))]
        args += [g, s, c]
    return pl.pallas_call(
        kern, name=name, out_shape=[pltpu.HBM((nl, R, W), F32)] * 4,
        grid_spec=pltpu.PrefetchScalarGridSpec(num_scalar_prefetch=1, grid=(R // tr,),
                                               in_specs=in_specs + [row, row, row] + [ANY_SPEC] * len(deps),
                                               out_specs=[row, row, row, row]),
        compiler_params=pltpu.CompilerParams(dimension_semantics=("parallel",),
                                             vmem_limit_bytes=_vmem(nl * (8 + n_other) * tr * W * 4)),
    )(where, *[_hbm(a) for a in (*args, w, m, v)], *deps)


SMALL_PLACE = (("mla_gq", 0, 0, 1, 256), ("mla_gkv", 0, 256, 1, 256), ("sgu_ln_g", 0, 512, 1, 512), ("sgu_ln_b", 1, 0, 1, 512),
               ("hg_lb", 2, 0, 2, 1024), ("ln1_g", 4, 0, 2, 1024), ("ln1_b", 6, 0, 2, 1024), ("sgu_b", 8, 0, 4, 128),
               ("ln2_g", 12, 0, 2, 1024), ("ln2_b", 14, 0, 2, 1024), ("hg_gnorm", 16, 0, 1, 1024))
SMALL_BUF_ROWS = 24
LOSS_ROW = 17


def _small_pack(gs, dev):
    pieces = [(gs["mla_gq"], 0, 0), (gs["mla_gkv"], 0, 256), (gs["sgu_ln_g"], 0, 512), (gs["sgu_ln_b"], 1, 0), (gs["hg_lb"], 2, 0),
              (gs["ln1_g0"], 4, 0), (gs["ln1_g1"], 5, 0), (gs["ln1_b0"], 6, 0), (gs["ln1_b1"], 7, 0), (gs["sgu_b"], 8, 0),
              (gs["ln2_g0"], 12, 0), (gs["ln2_g1"], 13, 0), (gs["ln2_b0"], 14, 0), (gs["ln2_b1"], 15, 0), (gs["hg_gnorm"], 16, 0),
              (gs["sq_err"], LOSS_ROW, 0)]
    n_p = len(pieces)

    def kern(dev_ref, *refs):
        a_ref, b_ref = refs[n_p + 1], refs[n_p + 2]
        a_ref[...] = jnp.zeros(a_ref.shape, F32)
        for ref, (_, r, l0) in zip(refs[:n_p], pieces):
            a_ref[r:r + ref.shape[0], l0:l0 + ref.shape[1]] = ref[...]
        b_ref[...] = refs[n_p][...]

    whole = lambda a: pl.BlockSpec(a.shape, functools.partial(lambda i, dev, nd: (0,) * nd, nd=a.ndim))
    return pl.pallas_call(
        kern, name="small_grads_pack",
        out_shape=[pltpu.HBM((N_DEV, SMALL_BUF_ROWS, D_MODEL), F32), pltpu.HBM((N_DEV, SGU_G, 128, 128), F32)],
        grid_spec=pltpu.PrefetchScalarGridSpec(
            num_scalar_prefetch=1, grid=(1,), in_specs=[whole(p[0]) for p in pieces] + [whole(gs["sgu_w"])],
            out_specs=[pl.BlockSpec((None, SMALL_BUF_ROWS, D_MODEL), lambda i, dev: (dev[0], 0, 0)),
                       pl.BlockSpec((None, SGU_G, 128, 128), lambda i, dev: (dev[0], 0, 0, 0))]),
    )(dev, *[p[0] for p in pieces], gs["sgu_w"])


def _small_copies(src_refs, land_refs, send_sems, recv_sems):
    px, py, pc = _me()
    me = 4 * px + 2 * py + pc
    return [pltpu.make_async_remote_copy(
        src_ref=land_refs[k].at[me], dst_ref=land_refs[k].at[me], send_sem=send_sems.at[2 * (r - 1) + k],
        recv_sem=recv_sems.at[2 * (r - 1) + k], device_id=(px ^ (r >> 2), py ^ ((r >> 1) & 1), pc ^ (r & 1)), device_id_type=MESH)
        for r in range(1, N_DEV) for k in range(2)]


def _small_adamw(slots_a, slots_b, given):
    names = [p[0] for p in SMALL_PLACE] + ["sgu_w"]
    n_names = len(names)
    wmv = [given[pre + name] for name in names for pre in ("", "m_", "v_")]
    vmem = pl.BlockSpec(memory_space=pltpu.VMEM)

    def kern(*refs):
        sum_a, sum_b = refs[0][0], refs[1][0]
        for d in range(1, N_DEV):
            sum_a, sum_b = sum_a + refs[0][d], sum_b + refs[1][d]
        wmv_refs, out_refs = refs[2:2 + 3 * n_names], refs[2 + 3 * n_names:]
        px, py, pc = _me()
        me = 4 * px + 2 * py + pc

        def own_block(full):
            acc = full[:, 0:128]
            for b in range(1, N_DEV):
                acc = jnp.where(me == b, full[:, b * 128:(b + 1) * 128], acc)
            return acc

        for idx, name in enumerate(names):
            w_ref, m_ref, v_ref = wmv_refs[3 * idx:3 * idx + 3]
            if name == "sgu_w":
                grad = sum_b[None]
            else:
                _, r, l0, nr, nl = SMALL_PLACE[idx]
                grad = sum_a[r:r + nr, l0:l0 + nl]
                if name == "hg_gnorm":
                    grad = own_block(grad)
                if name == "sgu_b":
                    grad = grad[None]
            res = (grad, *_adamw(w_ref[...], grad, m_ref[...], v_ref[...]))
            for o_ref, val in zip(out_refs[4 * idx:4 * idx + 4], res):
                o_ref[...] = val
        out_refs[4 * n_names][...] = (0.5 / D_MODEL) * jnp.sum(sum_a[LOSS_ROW:LOSS_ROW + 1, :], axis=1, keepdims=True)

    out_shape = [jax.ShapeDtypeStruct(given[name].shape, F32) for name in names for _ in range(4)]
    out_shape.append(jax.ShapeDtypeStruct((1, 1), F32))
    res = pl.pallas_call(
        kern, name="small_adamw", out_shape=out_shape, in_specs=[vmem] * (2 + len(wmv)), out_specs=[vmem] * len(out_shape),
    )(slots_a, slots_b, *wmv)
    out = {name: res[4 * idx:4 * idx + 4] for idx, name in enumerate(names)}
    out["loss"] = res[-1].reshape(())
    return out


class _Exchange:
    def __init__(self, given):
        self.given = given
        px, py, pc = _me()
        self.core = pc.reshape(1).astype(jnp.int32)
        self.dev = (4 * px + 2 * py + pc).reshape(1).astype(jnp.int32)
        self.where = jnp.stack([2 * px + py, pc]).astype(jnp.int32)
        self.state, self.layers = {}, {}

    def start_weights(self, lands, after):
        self.weights = _split_start("weights_first_start", [], lands, 4 * len(lands), _ag_first_copies, after=after)
        self.first_token = self.weights[4]

    def weights_forward(self, after):
        send_sems, recv_sems, shards, lands, _ = self.weights
        _, lands = _split_wait("weights_first_wait", send_sems, recv_sems, shards, lands, after, _ag_first_copies)
        self.weights = _split_start("weights_second_start", [], lands, 3 * len(lands), _ag_second_copies)
        return self.weights[4]

    def weights_ready(self, after):
        send_sems, recv_sems, shards, lands, _ = self.weights
        _, got = _split_wait("weights_second_wait", send_sems, recv_sems, shards, lands, after, _ag_second_copies)
        return dict(w_in_o=got[0], w_out_o=got[1], w_ff1=[got[2], got[3]], w_ff2=[got[4], got[5]])

    def small_start(self, gs):
        self.small = _split_start("small_grads_start", [], _small_pack(gs, self.dev), 14, _small_copies)
        return self.small[4]

    def small_finish(self, after):
        send_sems, recv_sems, _, lands, _ = self.small
        _, lands = _split_wait("small_grads_wait", send_sems, recv_sems, [], lands, after, _small_copies)
        return _small_adamw(lands[0], lands[1], self.given)

    def direct_start(self, tag, grads):
        f32 = [g[0].reshape(4, 2, *g[0].shape[1:]) for g in grads]
        bf16 = [g[1].reshape(4, 2, *g[1].shape[1:]) for g in grads]
        lands = [lax.empty(b.shape[2:], F32) for b in f32] + [lax.empty((6, *b.shape[2:]), BF16) for b in f32]
        self.state[tag] = _split_start(f"grads_{tag}_start", f32 + bf16, lands, 7 * len(grads), _rs_direct_copies)
        return self.state[tag][4]

    def direct_end(self, tag, after):
        send_sems, recv_sems, srcs, lands, _ = self.state[tag]
        srcs, lands = _split_wait(f"grads_{tag}_wait", send_sems, recv_sems, srcs, lands, after, _rs_direct_copies)
        n = len(lands) // 2
        self.layers[tag] = list(zip(srcs[:n], lands[:n], lands[n:]))

    def grads_start(self, tag, grads):
        blocks = [g.reshape(4, 2, *g.shape[1:]) for g in grads]
        lands = [lax.empty((4, *b.shape[2:]), F32) for b in blocks]
        self.state[tag] = _split_start(f"grads_{tag}_sibling_start", blocks, lands, 4 * len(blocks), _rs_sibling_copies)
        return self.state[tag][4]

    def grads_middle(self, tag, after):
        send_sems, recv_sems, blocks, lands, _ = self.state[tag]
        blocks, from_sibling = _split_wait(f"grads_{tag}_sibling_wait", send_sems, recv_sems, blocks, lands, [after], _rs_sibling_copies)
        sums = [_chip_sum(f"grads_{tag}_chip_sum_{k}", b, s, self.core) for k, (b, s) in enumerate(zip(blocks, from_sibling))]
        lands = [lax.empty((3, *p.shape[1:]), BF16) for p in sums]
        self.state[tag] = (blocks, from_sibling, _split_start(f"grads_{tag}_chips_start", sums, lands, 3 * len(sums), _rs_chip_copies))
        return self.state[tag][2][4]

    def grads_end(self, tag, after):
        blocks, from_sibling, (send_sems, recv_sems, sums, lands, _) = self.state[tag]
        after = list(after) if isinstance(after, (list, tuple)) else [after]
        _, from_chips = _split_wait(f"grads_{tag}_chips_wait", send_sems, recv_sems, sums, lands, after, _rs_chip_copies)
        self.layers[tag] = list(zip(blocks, from_sibling, from_chips))


def kernel(x, positions, w_in_e, mla_gq, mla_gkv, w_qb, w_kvb, sgu_ln_g, sgu_ln_b, sgu_w, sgu_b, w_out_e, w_in_o, hg_lb, hg_gnorm, w_out_o, ln1_g, ln1_b, w_ff1, w_ff2, ln2_g, ln2_b, loss_target, m_w_in_e, m_mla_gq, m_mla_gkv, m_w_qb, m_w_kvb, m_sgu_ln_g, m_sgu_ln_b, m_sgu_w, m_sgu_b, m_w_out_e, m_w_in_o, m_hg_lb, m_hg_gnorm, m_w_out_o, m_ln1_g, m_ln1_b, m_w_ff1, m_w_ff2, m_ln2_g, m_ln2_b, v_w_in_e, v_mla_gq, v_mla_gkv, v_w_qb, v_w_kvb, v_sgu_ln_g, v_sgu_ln_b, v_sgu_w, v_sgu_b, v_w_out_e, v_w_in_o, v_hg_lb, v_hg_gnorm, v_w_out_o, v_ln1_g, v_ln1_b, v_w_ff1, v_w_ff2, v_ln2_g, v_ln2_b):
    given = dict(locals())
    ex = _Exchange(given)

    names = ["w_in_e", "w_qb", "w_kvb", "w_out_e"]
    placed = _place_own([(given[n], 0, BF16) for n in names] + [(hg_gnorm.reshape(1, 1, D_MODEL // N_DEV), 0, F32)]
                        + [(w_in_o, 0, BF16), (w_out_o, 0, BF16), (w_ff1, 0, BF16), (w_ff1, 1, BF16), (w_ff2, 0, BF16), (w_ff2, 1, BF16)],
                        ex.dev)
    got = _all_gather(placed[:5])
    ex.start_weights(placed[5:], after=[got[0]])
    gw = dict(zip(names, got[:4]))
    small_names = ["mla_gq", "mla_gkv", "sgu_ln_g", "sgu_ln_b", "sgu_w", "sgu_b", "hg_lb", "ln1_g", "ln1_b", "ln2_g", "ln2_b"]
    sp = {n: given[n] for n in small_names}
    sp["hg_gnorm"] = got[4].reshape(1, D_MODEL)

    _, dx, grads, gs = _local_step(x[0], positions[0], loss_target[0], gw, sp, ex)

    def finish(n, layers, deps=()):
        return _finish_sharded(f"finish_{n}", layers, given[n], given["m_" + n], given["v_" + n], ex.where, deps=deps)

    ex.direct_end("l1", after=[dx])
    ex.direct_end("l0m", after=[dx])
    l1, l0m = ex.layers["l1"], ex.layers["l0m"]
    results = {}
    token = ex.grads_start("l0s", [grads[n] for n in names])
    results["w_ff1"] = finish("w_ff1", [l0m[0], l1[0]], deps=[token])
    token = ex.grads_middle("l0s", after=results["w_ff1"][0])
    results["w_ff2"] = finish("w_ff2", [l0m[1], l1[1]], deps=[token])
    results["w_in_o"] = finish("w_in_o", [l1[2]], deps=[token])
    results["w_out_o"] = finish("w_out_o", [l1[3]], deps=[token])
    results.update(ex.small_finish(after=[results["w_in_o"][0]]))
    ex.grads_end("l0s", after=[results[n][0] for n in ("mla_gq", "w_ff2", "w_in_o", "w_out_o")])
    for n, layer in zip(names, ex.layers["l0s"]):
        results[n] = finish(n, [layer])

    order = ["w_in_e", "mla_gq", "mla_gkv", "w_qb", "w_kvb", "sgu_ln_g", "sgu_ln_b", "sgu_w", "sgu_b", "w_out_e", "w_in_o",
             "hg_lb", "hg_gnorm", "w_out_o", "ln1_g", "ln1_b", "w_ff1", "w_ff2", "ln2_g", "ln2_b"]
    return (results["loss"], dx[None], *[results[name][kind] for kind in range(4) for name in order])
```

```python
import functools
import math

import jax
import jax.numpy as jnp
import numpy as np
from jax import lax
from jax.experimental import pallas as pl
from jax.experimental.pallas import tpu as pltpu

F32 = jnp.float32
BF16 = jnp.bfloat16
MESH = pl.DeviceIdType.MESH
HIGHEST = lax.Precision.HIGHEST

D_MODEL = 1024
D_FF = 4096
N_DEV = 8
HEADS = 8
HEAD_W = 128
MLA_NOPE = 64
MLA_ROPE = 32
MLA_V = 64
MLA_LORA = 256
MLA_SCALE = (MLA_NOPE + MLA_ROPE) ** -0.5
ROPE_BASE = 10000.0
SGU_DIM = 512
SGU_G = 4
SGU_CHUNK = 128
HG_CHUNK = 64
HG_CHUNKS_PER_STEP = 4
ALPHA = (2 * 2) ** 0.25
EPS = 1e-5
ADAM_LR, ADAM_B1, ADAM_B2, ADAM_EPS, ADAM_WD, ADAM_STEP = 0.001, 0.9, 0.999, 1e-08, 0.01, 10

VMEM_CAP_V7X = 56 * 2**20
VMEM_SLACK = 12 * 2**20
TM = 512
TN = 512


def _vmem(block_bytes):
    return int(min(VMEM_CAP_V7X, 2 * block_bytes + VMEM_SLACK))


def _hbm(a):
    return pltpu.with_memory_space_constraint(a, pltpu.HBM)


def _nbytes(shape, dtype):
    return int(np.prod([d for d in shape if d is not None])) * jnp.dtype(dtype).itemsize


def _sig(x):
    return 1.0 / (1.0 + jnp.exp(-x))


def _gelu(x):
    c = math.sqrt(2.0 / math.pi)
    t = jnp.tanh(c * (x + 0.044715 * x * x * x))
    return 0.5 * x * (1.0 + t), t


def _gelu_grad(x, t):
    c = math.sqrt(2.0 / math.pi)
    return 0.5 * (1.0 + t) + 0.5 * x * (1.0 - t * t) * c * (1.0 + 3 * 0.044715 * x * x)


def _dot(a, b, dims, precision=None):
    return lax.dot_general(a, b, (dims, ((), ())), preferred_element_type=F32, precision=precision)


NN = ((1,), (0,))
NT = ((1,), (1,))
TN_ = ((0,), (0,))


def _deps(deps):
    return [d for d in deps if d is not None]


def _tiled(name, grid, ins, outs, compute, direct=False, deps=()):
    n_in, deps = len(ins), _deps(deps)
    n_skip = n_in + len(deps)

    def kern(*refs):
        if direct:
            compute(refs[:n_in], refs[n_skip:])
            return
        for o_ref, r in zip(refs[n_skip:], compute(*refs[:n_in])):
            o_ref[...] = r.astype(o_ref.dtype).reshape(o_ref.shape)

    swap = lambda f: (lambda j, i: f(i, j))
    nbytes = sum(_nbytes(blk, a.dtype) for a, blk, _ in ins) + sum(_nbytes(blk, dt) + _nbytes(blk, F32) for _, dt, blk, _ in outs)
    res = pl.pallas_call(
        kern, name=name, grid=grid,
        in_specs=[pl.BlockSpec(blk, swap(f), pipeline_mode=pl.Buffered(1) if tuple(blk) == tuple(a.shape) else None)
                  for a, blk, f in ins] + [ANY_SPEC] * len(deps),
        out_specs=[pl.BlockSpec(blk, swap(f)) for _, _, blk, f in outs],
        out_shape=[pltpu.HBM(shape, dt) for shape, dt, _, _ in outs],
        compiler_params=pltpu.CompilerParams(dimension_semantics=("parallel", "parallel"), vmem_limit_bytes=_vmem(nbytes)),
    )(*[_hbm(a) for a, _, _ in ins], *deps)
    return res if len(res) > 1 else res[0]


def _rb(a, tm, w=None, cb=0):
    return (a, (tm, a.shape[1] if w is None else w), lambda i, j: (i, cb))


def _cw(b, tn):
    return (b, (b.shape[0], tn), lambda i, j: (0, j))


def _tl(a, tm):
    return (a, (a.shape[0], tm), lambda i, j: (0, i))


def _out(m, n, dtype, tm, tn):
    return ((m, n), dtype, (tm, tn), lambda i, j: (i, j))


def _out_dev(k, n, tm, dtype=F32):
    return ((N_DEV, k, n), dtype, (None, tm, n), lambda i, j: (j, i, 0))


def _twice(acc):
    return acc, acc


def _mmc(dims, n_pairs=1, epilogue=None):
    def compute(*refs):
        acc = None
        for k in range(n_pairs):
            d = _dot(refs[2 * k][...].astype(BF16), refs[2 * k + 1][...].astype(BF16), dims)
            acc = d if acc is None else acc + d
        ext = [r[...] for r in refs[2 * n_pairs:]]
        return epilogue(acc, *ext) if epilogue is not None else (acc,)

    return compute


def _res(w):
    return (w, w.shape, functools.partial(lambda i, j, nd: (0,) * nd, nd=w.ndim))


def _mmc_blocks(nblk, dims, rhs_block, epilogue=None):
    def compute(in_refs, out_refs):
        a = in_refs[0][...].astype(BF16)
        for d in range(nblk):
            acc = _dot(a, rhs_block(in_refs[1], d).astype(BF16), dims)
            n = acc.shape[1]
            ext = [r[:, d * n:(d + 1) * n] for r in in_refs[2:]]
            res = epilogue(acc, *ext) if epilogue is not None else (acc,)
            for o_ref, r in zip(out_refs, res):
                o_ref[:, d * n:(d + 1) * n] = r.astype(o_ref.dtype)

    return compute


def _rowwise(name, body, rows, consts, out_rows, out_accs=(), tr=512, deps=()):
    T = rows[0][0].shape[0]
    tr = min(tr, T)
    deps = _deps(deps)
    nr, ncn, no, nd = len(rows), len(consts), len(out_rows), len(deps)

    def kern(*refs):
        accs = refs[nr + ncn + nd + no:]
        if accs:
            @pl.when(pl.program_id(0) == 0)
            def _():
                for a in accs:
                    a[...] = jnp.zeros(a.shape, a.dtype)
        body(refs[:nr], refs[nr:nr + ncn], refs[nr + ncn + nd:nr + ncn + nd + no], accs)

    in_specs = [pl.BlockSpec((tr, w), functools.partial(lambda i, cb: (i, cb), cb=cb)) for _, w, cb in rows]
    in_specs += [pl.BlockSpec(c.shape, functools.partial(lambda i, nd: (0,) * nd, nd=c.ndim), pipeline_mode=pl.Buffered(1))
                 for c in consts]
    in_specs += [ANY_SPEC] * nd
    out_specs = [pl.BlockSpec((tr, w), lambda i: (i, 0)) for w, _ in out_rows]
    out_specs += [pl.BlockSpec(s, functools.partial(lambda i, nd: (0,) * nd, nd=len(s))) for s, _ in out_accs]
    out_shape = [pltpu.HBM((T, w), dt) for w, dt in out_rows]
    out_shape += [pltpu.HBM(s, dt) for s, dt in out_accs]
    nbytes = sum(_nbytes((tr, w), a.dtype) for a, w, _ in rows) + sum(_nbytes(c.shape, c.dtype) for c in consts)
    nbytes += sum(_nbytes((tr, w), dt) for w, dt in out_rows) + sum(_nbytes(s, dt) for s, dt in out_accs)
    res = pl.pallas_call(
        kern, name=name, grid=(T // tr,), in_specs=in_specs, out_specs=out_specs, out_shape=out_shape,
        compiler_params=pltpu.CompilerParams(dimension_semantics=("arbitrary",), vmem_limit_bytes=_vmem(nbytes)),
    )(*[_hbm(a) for a, _, _ in rows], *[_hbm(c) for c in consts], *deps)
    return res if len(res) > 1 else res[0]


def _full(a):
    return (a, a.shape[1], 0)


def _ln_stats(y):
    mu = jnp.mean(y, axis=-1, keepdims=True)
    yc = y - mu
    r = lax.rsqrt(jnp.mean(yc * yc, axis=-1, keepdims=True) + EPS)
    return yc * r, r


def _row_halves(n):
    return [slice(0, n // 2), slice(n // 2, n)] if n >= 256 else [slice(0, n)]


def _ln_back(dh, xh, r, gain, dg_ref, db_ref):
    dg_ref[...] += jnp.sum(dh * xh, axis=0, keepdims=True)
    db_ref[...] += jnp.sum(dh, axis=0, keepdims=True)
    dx = dh * gain
    return r * (dx - jnp.mean(dx, axis=-1, keepdims=True) - xh * jnp.mean(dx * xh, axis=-1, keepdims=True))


def _proj_ln(name, acts, weights, h_in, g, b, layer, deps=()):
    n = len(acts)

    def body(rows, consts, outs, accs):
        acc = None
        for k in range(n):
            d = _dot(rows[k][...].astype(BF16), consts[k][...], NN)
            acc = d if acc is None else acc + d
        y = ALPHA * rows[n][...] + acc
        xh, _ = _ln_stats(y)
        h = xh * consts[n][layer:layer + 1, :] + consts[n + 1][layer:layer + 1, :]
        outs[0][...] = y
        outs[1][...] = h
        outs[2][...] = h.astype(BF16)

    return _rowwise(name, body, [_full(a) for a in acts] + [_full(h_in)], [*weights, g, b],
                    [(D_MODEL, F32), (D_MODEL, F32), (D_MODEL, BF16)], tr=TM, deps=deps)


def _proj_ln_loss(name, act, w2, h_in, g, b, layer, target):
    def body(rows, consts, outs, accs):
        y = ALPHA * rows[1][...] + _dot(rows[0][...], consts[0][...], NN)
        xh, r = _ln_stats(y)
        gain = consts[1][layer:layer + 1, :]
        err = xh * gain + consts[2][layer:layer + 1, :] - rows[2][...]
        accs[0][...] += jnp.sum(err * err, axis=0, keepdims=True)
        dy = _ln_back(err * (1.0 / D_MODEL), xh, r, gain, accs[1], accs[2])
        outs[0][...] = dy
        outs[1][...] = dy.astype(BF16)

    return _rowwise(name, body, [_full(act), _full(h_in), _full(target)], [w2, g, b], [(D_MODEL, F32), (D_MODEL, BF16)],
                    [((1, D_MODEL), F32)] * 3, tr=TM)


def _dh_ln_back(name, da, w, dy_next, y, g, layer, proj=(), deps=()):
    def body(rows, consts, outs, accs):
        n = consts[0].shape[2]
        for sl in _row_halves(rows[0].shape[0]):
            acc = ALPHA * rows[1][sl, :]
            for d in range(N_DEV):
                acc = acc + _dot(rows[0][sl, d * n:(d + 1) * n], consts[0][d], NT)
            xh, r = _ln_stats(rows[2][sl, :])
            dy = _ln_back(acc, xh, r, consts[1][layer:layer + 1, :], accs[0], accs[1])
            outs[0][sl, :] = dy
            dy_bf = dy.astype(BF16)
            outs[1][sl, :] = dy_bf
            off = 0
            for k, p in enumerate(proj):
                outs[2][sl, off:off + p.shape[0]] = _dot(dy_bf, consts[2 + k][...], NT).astype(BF16)
                off += p.shape[0]

    out_rows = [(D_MODEL, F32), (D_MODEL, BF16)] + ([(sum(p.shape[0] for p in proj), BF16)] if proj else [])
    return _rowwise(name, body, [_full(da), _full(dy_next), _full(y)], [w, g, *proj], out_rows,
                    [((1, D_MODEL), F32)] * 2, tr=TM, deps=deps)


def _relu2_epilogue(acc):
    a = jnp.maximum(acc, 0.0)
    return acc, a * a


def _mlp_up(tag, h_bf, w1):
    T = h_bf.shape[0]
    tm = min(TM, T)
    return _tiled(f"{tag}_ff1", (1, T // tm), [_rb(h_bf, tm), _res(w1)],
                  [_out(T, D_FF, BF16, tm, D_FF), _out(T, D_FF, BF16, tm, D_FF)],
                  _mmc_blocks(N_DEV, NN, lambda w, d: w[d], epilogue=_relu2_epilogue), direct=True)


def _mlp_bwd_w(tag, h_bf, a, act, dff_bf, w2, deps=()):
    T = h_bf.shape[0]
    tm = min(TM, T)
    da = _tiled(f"{tag}_dact", (1, T // tm), [_rb(dff_bf, tm), _res(w2), _rb(a, tm)], [_out(T, D_FF, BF16, tm, D_FF)],
                _mmc_blocks(N_DEV, NT, lambda w, d: w[d], epilogue=lambda acc, a_t: (acc * 2.0 * jnp.maximum(a_t.astype(F32), 0.0),)),
                direct=True, deps=deps)
    dw2 = _tiled(f"{tag}_dw2", (1, D_FF // TM), [_tl(act, TM), _res(dff_bf)],
                 [_out(D_FF, D_MODEL, F32, TM, D_MODEL), _out(D_FF, D_MODEL, BF16, TM, D_MODEL)], _mmc(TN_, epilogue=_twice))
    dw1 = _tiled(f"{tag}_dw1", (N_DEV, 1), [_res(h_bf), _cw(da, TN)],
                 [_out_dev(D_MODEL, TN, D_MODEL), _out_dev(D_MODEL, TN, D_MODEL, BF16)], _mmc(TN_, epilogue=_twice))
    return da, dw1, [a.reshape(N_DEV, D_FF // N_DEV, D_MODEL) for a in dw2]


def _rope_tables(positions_col, invf_lane):
    def body(rows, consts, outs, accs):
        ang = rows[0][...].astype(F32) * consts[0][...]
        c, s = jnp.cos(ang), jnp.sin(ang)
        lane = lax.broadcasted_iota(jnp.int32, ang.shape, 1)
        outs[0][...] = jnp.where(lane < 64, 1.0, jnp.where(lane < 96, c, 0.0))
        outs[1][...] = jnp.where((lane >= 64) & (lane < 80), -s, 0.0)
        outs[2][...] = jnp.where((lane >= 80) & (lane < 96), s, 0.0)

    return _rowwise("rope_tables", body, [_full(positions_col)], [invf_lane], [(HEAD_W, F32)] * 3)


def _rope(x, c, s1, s2):
    return x * c + pltpu.roll(x, 112, 1) * s1 + pltpu.roll(x, 16, 1) * s2


def _rope_t(dx, c, s1, s2):
    return dx * c + pltpu.roll(dx * s1, 16, 1) + pltpu.roll(dx * s2, 112, 1)


def _rms(c):
    r = lax.rsqrt(jnp.mean(c * c, axis=-1, keepdims=True) + EPS)
    return c * r, r


def _rope_heads(x, c, s1, s2, fn):
    return jnp.concatenate([fn(x[:, h * HEAD_W:(h + 1) * HEAD_W], c, s1, s2) for h in range(HEADS)], axis=1)


def _mla_in(x, wm, ws, tabs, gq, gkv, deps=()):
    def body(rows, consts, outs, accs):
        xb = rows[0][...].astype(BF16)
        zm = _dot(xb, consts[0][...], NN)
        outs[0][...] = zm
        outs[1][...] = _dot(xb, consts[1][...], NN)
        outs[2][...] = (_rms(zm[:, 0:256])[0] * consts[2][...]).astype(BF16)
        outs[3][...] = (_rms(zm[:, 256:512])[0] * consts[3][...]).astype(BF16)
        outs[4][...] = _rope(zm[:, 512:640], rows[1][...], rows[2][...], rows[3][...])

    return _rowwise("l0_in", body, [_full(x)] + [_full(t) for t in tabs], [wm, ws, gq, gkv],
                    [(640, F32), (1024, F32), (256, BF16), (256, BF16), (HEAD_W, F32)], deps=deps)


def _mla_qkv(cqn, ckvn, kr_rot, tabs, wq, wk, wv):
    def body(rows, consts, outs, accs):
        c, s1, s2 = rows[3][...], rows[4][...], rows[5][...]
        outs[0][...] = _rope_heads(_dot(rows[0][...], consts[0][...], NN), c, s1, s2, _rope).astype(BF16)
        outs[1][...] = (_dot(rows[1][...], consts[1][...], NN) + jnp.concatenate([rows[2][...]] * HEADS, axis=1)).astype(BF16)
        outs[2][...] = _dot(rows[1][...], consts[2][...], NN).astype(BF16)

    rows = [_full(cqn), _full(ckvn), _full(kr_rot)] + [_full(t) for t in tabs]
    return _rowwise("l0_qkv", body, rows, [wq, wk, wv], [(HEADS * HEAD_W, BF16)] * 3)


def _mla_back(zm, cqn, ckvn, tabs, gq, gkv, wq, wk, wv, dq, dk, dv):
    def body(rows, consts, outs, accs):
        c, s1, s2 = rows[4][...], rows[5][...], rows[6][...]
        dk_t, dv_bf = rows[8][...], rows[9][...].astype(BF16)
        dq_bf = _rope_heads(rows[7][...], c, s1, s2, _rope_t).astype(BF16)
        dk_bf = dk_t.astype(BF16)
        accs[0][...] += _dot(rows[2][...], dq_bf, TN_)
        accs[1][...] += _dot(rows[3][...], dk_bf, TN_)
        accs[2][...] += _dot(rows[3][...], dv_bf, TN_)
        dlat = [_dot(dq_bf, consts[2][...], NT), _dot(dk_bf, consts[3][...], NT) + _dot(dv_bf, consts[4][...], NT)]
        for k in range(2):
            ch, r = _rms(rows[k][...])
            accs[3 + k][...] += jnp.sum(dlat[k] * ch, axis=0, keepdims=True)
            dc = dlat[k] * consts[k][...]
            outs[0][:, 256 * k:256 * (k + 1)] = (r * (dc - ch * jnp.mean(dc * ch, axis=-1, keepdims=True))).astype(BF16)
        dks = dk_t[:, 0:HEAD_W]
        for h in range(1, HEADS):
            dks = dks + dk_t[:, h * HEAD_W:(h + 1) * HEAD_W]
        lane = lax.broadcasted_iota(jnp.int32, dks.shape, 1)
        dks = jnp.where((lane >= 64) & (lane < 96), dks, 0.0)
        outs[0][:, 512:640] = _rope_t(dks, c, s1, s2).astype(BF16)

    rows = [(zm, 256, 0), (zm, 256, 1), _full(cqn), _full(ckvn)] + [_full(t) for t in tabs] + [_full(dq), _full(dk), _full(dv)]
    wide = HEADS * HEAD_W
    return _rowwise("l0_mla_back", body, rows, [gq, gkv, wq, wk, wv], [(640, BF16)],
                    [((MLA_LORA, wide), F32)] * 3 + [((1, MLA_LORA), F32)] * 2, tr=256)


def _in_back(x, dzm, dzs, dy, wm, ws, deps=()):
    def body(rows, consts, outs, accs):
        dzm_t, dzs_t = rows[1][...], rows[2][...]
        outs[0][...] = _dot(dzm_t, consts[0][...], NT) + _dot(dzs_t, consts[1][...], NT) + ALPHA * rows[3][...]
        xb = rows[0][...].astype(BF16)
        accs[0][...] += _dot(xb, dzm_t, TN_)
        accs[1][...] += _dot(xb, dzs_t, TN_)

    return _rowwise("l0_in_back", body, [_full(x), _full(dzm), _full(dzs), _full(dy)], [wm, ws], [(D_MODEL, F32)],
                    [((D_MODEL, 640), F32), ((D_MODEL, 1024), F32)], deps=deps)


def _out_weight_grads(o_att, b_out, dy_bf):
    def body(rows, consts, outs, accs):
        d = rows[2][...]
        accs[0][...] += _dot(rows[0][...].astype(BF16), d, TN_)
        accs[1][...] += _dot(rows[1][...], d, TN_)

    return _rowwise("l0_dw_out", body, [_full(o_att), _full(b_out), _full(dy_bf)], [], [],
                    [((HEADS * HEAD_W, D_MODEL), F32), ((SGU_DIM, D_MODEL), F32)])


def _attn_block(T):
    return min(1024, T)


def _attn_fwd(q, k, v):
    T = q.shape[0]
    BQ = _attn_block(T)
    nq = T // BQ

    def kern(q_ref, k_ref, v_ref, o_ref, lse_ref):
        def step(i, j, carry, masked):
            m, l, acc = carry
            qb = q_ref[pl.ds(pl.multiple_of(i * BQ, BQ), BQ), :]
            kb = k_ref[pl.ds(pl.multiple_of(j * BQ, BQ), BQ), :]
            vb = v_ref[pl.ds(pl.multiple_of(j * BQ, BQ), BQ), :]
            s = _dot(qb, kb, NT) * MLA_SCALE
            if masked:
                row = lax.broadcasted_iota(jnp.int32, s.shape, 0)
                col = lax.broadcasted_iota(jnp.int32, s.shape, 1)
                s = jnp.where(col <= row, s, -1e30)
            m_new = jnp.maximum(m, jnp.max(s, axis=-1, keepdims=True))
            p = jnp.exp(s - m_new)
            a = jnp.exp(m - m_new)
            l = a * l + jnp.sum(p, axis=-1, keepdims=True)
            acc = a * acc + _dot(p.astype(BF16), vb, NN)
            return m_new, l, acc

        def qloop(i, _):
            init = (jnp.full((BQ, 1), -1e30, F32), jnp.zeros((BQ, 1), F32), jnp.zeros((BQ, HEAD_W), F32))
            carry = lax.fori_loop(0, i, lambda j, c: step(i, j, c, False), init)
            m, l, acc = step(i, i, carry, True)
            rows = pl.ds(pl.multiple_of(i * BQ, BQ), BQ)
            o_ref[rows, :] = acc / l
            lse_ref[0, rows, :] = m + jnp.log(l)
            return 0

        lax.fori_loop(0, nq, qloop, 0)

    head = pl.BlockSpec((T, HEAD_W), lambda h: (0, h))
    nbytes = 3 * _nbytes((T, HEAD_W), BF16) + _nbytes((T, HEAD_W), F32) + _nbytes((T, 128), F32)
    return pl.pallas_call(
        kern, name="attn_fwd", grid=(HEADS,), in_specs=[head, head, head],
        out_specs=[head, pl.BlockSpec((1, T, 1), lambda h: (h, 0, 0))],
        out_shape=[pltpu.HBM((T, HEADS * HEAD_W), F32), pltpu.HBM((HEADS, T, 1), F32)],
        compiler_params=pltpu.CompilerParams(dimension_semantics=("parallel",), vmem_limit_bytes=_vmem(nbytes)),
    )(_hbm(q), _hbm(k), _hbm(v))


def _attn_bwd(q, k, v, o, lse, dcat, deps=()):
    T = q.shape[0]
    BQ = _attn_block(T)
    nq = T // BQ
    deps = _deps(deps)

    def kern(q_ref, k_ref, v_ref, o_ref, lse_ref, do_ref, *rest):
        dq_ref, dk_ref, dv_ref, dd_ref = rest[len(deps):]
        dq_ref[...] = jnp.zeros(dq_ref.shape, F32)

        def dloop(i, _):
            rows = pl.ds(pl.multiple_of(i * BQ, BQ), BQ)
            dd_ref[rows, :] = jnp.sum(do_ref[rows, :].astype(F32) * o_ref[rows, :], axis=-1, keepdims=True)
            return 0

        lax.fori_loop(0, nq, dloop, 0)

        def tile(q0, k0, n, carry, masked):
            dk_acc, dv_acc = carry
            rq = pl.ds(pl.multiple_of(q0, n), n)
            rk = pl.ds(pl.multiple_of(k0, n), n)
            qb, kb, vb, dob = q_ref[rq, :], k_ref[rk, :], v_ref[rk, :], do_ref[rq, :]
            s = _dot(qb, kb, NT) * MLA_SCALE
            p = jnp.exp(s - lse_ref[0, rq, :])
            if masked:
                row = lax.broadcasted_iota(jnp.int32, s.shape, 0)
                col = lax.broadcasted_iota(jnp.int32, s.shape, 1)
                p = jnp.where(col <= row, p, 0.0)
            dp = _dot(dob, vb, NT)
            ds = (p * (dp - dd_ref[rq, :]) * MLA_SCALE).astype(BF16)
            dv_acc = dv_acc + _dot(p.astype(BF16), dob, TN_)
            dk_acc = dk_acc + _dot(ds, qb, TN_)
            dq_ref[rq, :] += _dot(ds, kb, NN)
            return dk_acc, dv_acc

        def kloop(j, _):
            base, half = j * BQ, BQ // 2
            zero = (jnp.zeros((half, HEAD_W), F32), jnp.zeros((half, HEAD_W), F32))
            early = tile(base + half, base, half, tile(base, base, half, zero, True), False)
            late = tile(base + half, base + half, half, zero, True)
            carry = tuple(jnp.concatenate([a, b], axis=0) for a, b in zip(early, late))
            dk_acc, dv_acc = lax.fori_loop(j + 1, nq, lambda i, c: tile(i * BQ, base, BQ, c, False), carry)
            rk = pl.ds(pl.multiple_of(j * BQ, BQ), BQ)
            dk_ref[rk, :] = dk_acc
            dv_ref[rk, :] = dv_acc
            return 0

        lax.fori_loop(0, nq, kloop, 0)

    head = pl.BlockSpec((T, HEAD_W), lambda h: (0, h))
    nbytes = 4 * _nbytes((T, HEAD_W), BF16) + 5 * _nbytes((T, HEAD_W), F32) + 2 * _nbytes((T, 128), F32)
    return pl.pallas_call(
        kern, name="attn_bwd", grid=(HEADS,),
        in_specs=[head, head, head, head, pl.BlockSpec((1, T, 1), lambda h: (h, 0, 0)), head] + [ANY_SPEC] * len(deps),
        out_specs=[head, head, head],
        out_shape=[pltpu.HBM((T, HEADS * HEAD_W), F32)] * 3,
        scratch_shapes=[pltpu.VMEM((T, 1), F32)],
        compiler_params=pltpu.CompilerParams(dimension_semantics=("parallel",), vmem_limit_bytes=_vmem(nbytes)),
    )(*[_hbm(a) for a in (q, k, v, o, lse, dcat)], *deps)


def _sgu_common(u, v, ln_g, ln_b):
    ua, tu = _gelu(u)
    va, tv = _gelu(v)
    vh, r = _ln_stats(va)
    return ua, tu, tv, vh, r, vh * ln_g + ln_b


def _tril_mask(n):
    return lax.broadcasted_iota(jnp.int32, (n, n), 1) <= lax.broadcasted_iota(jnp.int32, (n, n), 0)


def _sgu_fwd(zs, ln_g, ln_b, w, bias_full):
    def body(rows, consts, outs, accs):
        ua, _, _, _, _, vn = _sgu_common(rows[0][...], rows[1][...], consts[0][...], consts[1][...])
        vn = vn.astype(BF16)
        tri = _tril_mask(SGU_CHUNK)
        for g in range(SGU_G):
            wg = jnp.where(tri, consts[2][0, g], 0.0).astype(BF16)
            cols = slice(g * 128, (g + 1) * 128)
            for c in range(ua.shape[0] // SGU_CHUNK):
                rws = slice(c * SGU_CHUNK, (c + 1) * SGU_CHUNK)
                mixed = _dot(wg, vn[rws, cols], NN) + consts[3][:, cols]
                outs[0][rws, cols] = (ua[rws, cols] * mixed).astype(BF16)

    return _rowwise("sgu_fwd", body, [(zs, 512, 0), (zs, 512, 1)], [ln_g, ln_b, w, bias_full], [(SGU_DIM, BF16)])


def _sgu_bwd(zs, dcat, ln_g, ln_b, w, bias_full):
    def body(rows, consts, outs, accs):
        u, v = rows[0][...], rows[1][...]
        ua, tu, tv, vh, r, vn = _sgu_common(u, v, consts[0][...], consts[1][...])
        dout = rows[2][...].astype(F32)
        vn_bf = vn.astype(BF16)
        tri = _tril_mask(SGU_CHUNK)
        dmixed = (dout * ua)
        dmixed_bf = dmixed.astype(BF16)
        ones = jnp.ones((8, SGU_CHUNK), F32)
        dvn_cols, mixed_cols = [], []
        for g in range(SGU_G):
            wg = jnp.where(tri, consts[2][0, g], 0.0).astype(BF16)
            cols = slice(g * 128, (g + 1) * 128)
            dvn_rows, mixed_rows = [], []
            dw = jnp.zeros((SGU_CHUNK, SGU_CHUNK), F32)
            dmix_sum = jnp.zeros((SGU_CHUNK, 128), F32)
            for c in range(u.shape[0] // SGU_CHUNK):
                rws = slice(c * SGU_CHUNK, (c + 1) * SGU_CHUNK)
                mixed_rows.append(_dot(wg, vn_bf[rws, cols], NN) + consts[3][:, cols])
                dvn_rows.append(_dot(wg, dmixed_bf[rws, cols], TN_))
                dw = dw + _dot(dmixed_bf[rws, cols], vn_bf[rws, cols], NT)
                dmix_sum = dmix_sum + dmixed[rws, cols]
            accs[0][g] += jnp.where(tri, dw, 0.0)
            accs[3][g:g + 1, :] += _dot(ones, dmix_sum, NT, precision=HIGHEST)[0:1, :]
            dvn_cols.append(jnp.concatenate(dvn_rows, axis=0))
            mixed_cols.append(jnp.concatenate(mixed_rows, axis=0))
        dvn = jnp.concatenate(dvn_cols, axis=1)
        mixed = jnp.concatenate(mixed_cols, axis=1)
        accs[1][...] += jnp.sum(dvn * vh, axis=0, keepdims=True)
        accs[2][...] += jnp.sum(dvn, axis=0, keepdims=True)
        dvh = dvn * consts[0][...]
        dva = r * (dvh - jnp.mean(dvh, axis=-1, keepdims=True) - vh * jnp.mean(dvh * vh, axis=-1, keepdims=True))
        outs[0][:, 0:512] = (dout * mixed * _gelu_grad(u, tu)).astype(BF16)
        outs[0][:, 512:1024] = (dva * _gelu_grad(v, tv)).astype(BF16)

    return _rowwise("sgu_bwd", body, [(zs, 512, 0), (zs, 512, 1), (dcat, 512, 2)], [ln_g, ln_b, w, bias_full], [(1024, BF16)],
                    [((SGU_G, 128, 128), F32), ((1, SGU_DIM), F32), ((1, SGU_DIM), F32), ((SGU_G, 128), F32)], tr=256)


def _lower_bound(hg_lb):
    a0, a1 = hg_lb[0:1, :], hg_lb[1:2, :]
    m = jnp.maximum(a0, a1)
    e0, e1 = jnp.exp(a0 - m), jnp.exp(a1 - m)
    s0, s1 = e0 / (e0 + e1), e1 / (e0 + e1)
    return (s0 + s1) - s0, s0, s1


def _prefix_rows(x, reverse=False):
    n = x.shape[0]
    row = lax.broadcasted_iota(jnp.int32, x.shape, 0)
    s = 1
    while s < n:
        if reverse:
            x = x + jnp.where(row < n - s, pltpu.roll(x, n - s, 0), 0.0)
        else:
            x = x + jnp.where(row >= s, pltpu.roll(x, s, 0), 0.0)
        s *= 2
    return x


def _hg_gates(qr, fr, lb):
    C = qr.shape[0]
    sq = _sig(qr)
    qf = qr * sq
    sf = _sig(fr)
    gate = lb + (1.0 - lb) * sf
    kk = 1.0 - gate
    tri = _tril_mask(C)
    b = _prefix_rows(jnp.log(gate))
    bref = b[C // 2 - 1:C // 2, :]
    bl = b[C - 1:C, :]
    e_b = jnp.exp(b)
    e_q = jnp.exp(b - bref)
    e_k = jnp.exp(bref - b)
    e_lb = jnp.exp(bl - b)
    return dict(sq=sq, qf=qf, sf=sf, gate=gate, kk=kk, tri=tri, bl=bl, e_b=e_b, e_q=e_q, e_k=e_k, e_lb=e_lb)


def _hgrn_fwd(z1, hg_lb, gnorm):
    T = z1.shape[0]
    C = min(HG_CHUNK, T)
    nc = T // C
    ns = HG_CHUNKS_PER_STEP if nc % HG_CHUNKS_PER_STEP == 0 else 1
    R = ns * C

    def kern(q_ref, f_ref, i_ref, g_ref, lb_ref, gn_ref, o_ref, hg_ref, st_ref, s_scr):
        @pl.when(pl.program_id(0) == 0)
        def _():
            s_scr[...] = jnp.zeros(s_scr.shape, F32)

        lb_all, _, _ = _lower_bound(lb_ref[...])
        for sub in range(ns):
            rows = slice(sub * C, (sub + 1) * C)
            st_ref[sub] = s_scr[...]
            for h in range(HEADS):
                cols = slice(h * HEAD_W, (h + 1) * HEAD_W)
                t = _hg_gates(q_ref[rows, cols], f_ref[rows, cols], lb_all[:, cols])
                v_bf = i_ref[rows, cols].astype(BF16)
                st = s_scr[h]
                a = jnp.where(t["tri"], _dot((t["qf"] * t["e_q"]).astype(BF16), (t["kk"] * t["e_k"]).astype(BF16), NT), 0.0)
                o = _dot(a.astype(BF16), v_bf, NN) + _dot((t["qf"] * t["e_b"]).astype(BF16), st.astype(BF16), NT)
                s_scr[h] = st * jnp.exp(t["bl"]) + _dot(v_bf, (t["kk"] * t["e_lb"]).astype(BF16), TN_)
                o_ref[rows, cols] = o
                gr = g_ref[rows, cols]
                r = lax.rsqrt(jnp.mean(o * o, axis=-1, keepdims=True) + EPS)
                hg_ref[rows, cols] = (o * r * gn_ref[:, cols] * (gr * _sig(gr))).astype(BF16)

    seg = lambda k: pl.BlockSpec((R, D_MODEL), functools.partial(lambda n, k: (n, k), k=k))
    row = pl.BlockSpec((R, D_MODEL), lambda n: (n, 0))
    nbytes = 6 * _nbytes((R, D_MODEL), F32) + (2 + ns) * _nbytes((HEADS, 128, 128), F32)
    return pl.pallas_call(
        kern, name="hgrn_fwd", grid=(nc // ns,),
        in_specs=[seg(0), seg(1), seg(2), seg(3), pl.BlockSpec((2, D_MODEL), lambda n: (0, 0)),
                  pl.BlockSpec((1, D_MODEL), lambda n: (0, 0))],
        out_specs=[row, row, pl.BlockSpec((ns, HEADS, 128, 128), lambda n: (n, 0, 0, 0))],
        out_shape=[pltpu.HBM((T, D_MODEL), F32), pltpu.HBM((T, D_MODEL), BF16),
                   pltpu.HBM((nc, HEADS, 128, 128), F32)],
        scratch_shapes=[pltpu.VMEM((HEADS, 128, 128), F32)],
        compiler_params=pltpu.CompilerParams(dimension_semantics=("arbitrary",), vmem_limit_bytes=_vmem(nbytes)),
    )(*[_hbm(a) for a in (z1, z1, z1, z1, hg_lb, gnorm)])


def _hgrn_bwd(z1, o_pre, dhg, states, hg_lb, gnorm):
    T = z1.shape[0]
    C = min(HG_CHUNK, T)
    nc = T // C
    ns = HG_CHUNKS_PER_STEP if nc % HG_CHUNKS_PER_STEP == 0 else 1
    R, steps = ns * C, nc // ns

    def kern(q_ref, f_ref, i_ref, g_ref, o_ref, dhg_ref, st_ref, lb_ref, gn_ref, dz_ref, dlb_ref, dgn_ref, ds_scr, dlb_scr):
        n = pl.program_id(0)

        @pl.when(n == 0)
        def _():
            ds_scr[...] = jnp.zeros(ds_scr.shape, F32)
            dlb_scr[...] = jnp.zeros(dlb_scr.shape, F32)
            dgn_ref[...] = jnp.zeros(dgn_ref.shape, F32)

        lb_all, s0, s1 = _lower_bound(lb_ref[...])
        for sub in reversed(range(ns)):
            rows = slice(sub * C, (sub + 1) * C)
            for h in range(HEADS):
                cols = slice(h * HEAD_W, (h + 1) * HEAD_W)
                lb = lb_all[:, cols]
                qr, fr = q_ref[rows, cols], f_ref[rows, cols]
                t = _hg_gates(qr, fr, lb)
                tri = t["tri"]
                v_bf = i_ref[rows, cols].astype(BF16)
                st_bf = st_ref[sub, h].astype(BF16)
                dst = ds_scr[h]
                dst_bf = dst.astype(BF16)
                o = o_ref[rows, cols]
                gr = g_ref[rows, cols]
                sg = _sig(gr)
                sil = gr * sg
                gn = gn_ref[:, cols]
                r = lax.rsqrt(jnp.mean(o * o, axis=-1, keepdims=True) + EPS)
                on = o * r
                dh = dhg_ref[rows, cols].astype(F32)
                dgn_ref[:, cols] += jnp.sum(dh * on * sil, axis=0, keepdims=True)
                dg = dh * on * gn * (sg * (1.0 + gr * (1.0 - sg)))
                don = dh * gn * sil
                do_bf = (r * (don - on * jnp.mean(don * on, axis=-1, keepdims=True))).astype(BF16)
                qe = (t["qf"] * t["e_q"]).astype(BF16)
                ke = (t["kk"] * t["e_k"]).astype(BF16)
                qb = (t["qf"] * t["e_b"]).astype(BF16)
                kh_bf = (t["kk"] * t["e_lb"]).astype(BF16)
                a_bf = jnp.where(tri, _dot(qe, ke, NT), 0.0).astype(BF16)
                da_bf = jnp.where(tri, _dot(do_bf, v_bf, NT), 0.0).astype(BF16)
                dv = _dot(a_bf, do_bf, TN_) + _dot(kh_bf, dst_bf, NT)
                dqe = _dot(da_bf, ke, NN)
                dqb = _dot(do_bf, st_bf, NN)
                dke = _dot(da_bf, qe, TN_)
                dkh = _dot(v_bf, dst_bf, NN)
                dqf = dqe * t["e_q"] + dqb * t["e_b"]
                dkk = dke * t["e_k"] + dkh * t["e_lb"]
                kh_r = kh_bf.astype(F32)
                db = qe.astype(F32) * dqe - ke.astype(F32) * dke + qb.astype(F32) * dqb - kh_r * dkh
                e_bl = jnp.exp(t["bl"])
                dbl = jnp.sum(dkh * kh_r, axis=0, keepdims=True) + e_bl * jnp.sum(st_ref[sub, h] * dst, axis=0, keepdims=True)
                dlg = _prefix_rows(db, reverse=True) + dbl
                ds_scr[h] = dst * e_bl + _dot(do_bf, qb, TN_)
                dgate = dlg / t["gate"] - dkk
                sf = t["sf"]
                dlb_scr[:, cols] += jnp.sum(dgate * (1.0 - sf), axis=0, keepdims=True)
                df = dgate * (1.0 - lb) * sf * (1.0 - sf)
                dq = dqf * (t["sq"] * (1.0 + qr * (1.0 - t["sq"])))
                dz_ref[rows, cols] = dq.astype(BF16)
                dz_ref[rows, D_MODEL + h * HEAD_W:D_MODEL + (h + 1) * HEAD_W] = df.astype(BF16)
                dz_ref[rows, 2 * D_MODEL + h * HEAD_W:2 * D_MODEL + (h + 1) * HEAD_W] = dv.astype(BF16)
                dz_ref[rows, 3 * D_MODEL + h * HEAD_W:3 * D_MODEL + (h + 1) * HEAD_W] = dg.astype(BF16)

        @pl.when(n == steps - 1)
        def _():
            d = s0 * s1 * dlb_scr[...]
            dlb_ref[0:1, :] = -d
            dlb_ref[1:2, :] = d

    seg = lambda k: pl.BlockSpec((R, D_MODEL), functools.partial(lambda n, k: (steps - 1 - n, k), k=k))
    nbytes = 6 * _nbytes((R, D_MODEL), F32) + _nbytes((R, 4 * D_MODEL), BF16) + (2 + ns) * _nbytes((HEADS, 128, 128), F32)
    return pl.pallas_call(
        kern, name="hgrn_bwd", grid=(steps,),
        in_specs=[seg(0), seg(1), seg(2), seg(3), seg(0), seg(0),
                  pl.BlockSpec((ns, HEADS, 128, 128), lambda n: (steps - 1 - n, 0, 0, 0)),
                  pl.BlockSpec((2, D_MODEL), lambda n: (0, 0)), pl.BlockSpec((1, D_MODEL), lambda n: (0, 0))],
        out_specs=[pl.BlockSpec((R, 4 * D_MODEL), lambda n: (steps - 1 - n, 0)),
                   pl.BlockSpec((2, D_MODEL), lambda n: (0, 0)), pl.BlockSpec((1, D_MODEL), lambda n: (0, 0))],
        out_shape=[pltpu.HBM((T, 4 * D_MODEL), BF16), pltpu.HBM((2, D_MODEL), F32),
                   pltpu.HBM((1, D_MODEL), F32)],
        scratch_shapes=[pltpu.VMEM((HEADS, 128, 128), F32), pltpu.VMEM((1, D_MODEL), F32)],
        compiler_params=pltpu.CompilerParams(dimension_semantics=("arbitrary",), vmem_limit_bytes=_vmem(nbytes)),
    )(*[_hbm(a) for a in (z1, z1, z1, z1, o_pre, dhg, states, hg_lb, gnorm)])


def _prep_weights(gw):
    w_in_e = gw["w_in_e"].transpose(1, 0, 2).reshape(D_MODEL, 1568)
    kr = jnp.pad(w_in_e[:, 512:544], ((0, 0), (64, 32)))
    wm = jnp.concatenate([w_in_e[:, 0:512], kr], axis=1)
    ws = w_in_e[:, 544:1568]
    w_qb = gw["w_qb"].transpose(1, 0, 2).reshape(MLA_LORA, HEADS, 96)
    wq = jnp.pad(w_qb, ((0, 0), (0, 0), (0, 32))).reshape(MLA_LORA, HEADS * HEAD_W)
    kvb = gw["w_kvb"].transpose(1, 0, 2).reshape(MLA_LORA, HEADS, 128)
    wk = jnp.pad(kvb[:, :, :64], ((0, 0), (0, 0), (0, 64))).reshape(MLA_LORA, HEADS * HEAD_W)
    wv = jnp.pad(kvb[:, :, 64:], ((0, 0), (0, 0), (0, 64))).reshape(MLA_LORA, HEADS * HEAD_W)
    return dict(wm=wm, ws=ws, wq=wq, wk=wk, wv=wv)


def _prep_out_weight(gathered):
    w_out_e = gathered.reshape(D_MODEL, D_MODEL)
    woa = jnp.pad(w_out_e[:512].reshape(HEADS, 64, D_MODEL), ((0, 0), (0, 64), (0, 0))).reshape(HEADS * HEAD_W, D_MODEL)
    return woa, w_out_e[512:]


def _unprep_grads(g):
    dwm, dws = g["wm"], g["ws"]
    d_in_e = jnp.concatenate([dwm[:, 0:512], dwm[:, 512 + 64:512 + 96], dws], axis=1)
    d_qb = g["wq"].reshape(MLA_LORA, HEADS, HEAD_W)[:, :, :96].reshape(MLA_LORA, HEADS * 96)
    dk = g["wk"].reshape(MLA_LORA, HEADS, HEAD_W)[:, :, :64]
    dv = g["wv"].reshape(MLA_LORA, HEADS, HEAD_W)[:, :, :64]
    d_kvb = jnp.concatenate([dk, dv], axis=2).reshape(MLA_LORA, HEADS * 128)
    d_oa = g["woa"].reshape(HEADS, HEAD_W, D_MODEL)[:, :64].reshape(HEADS * 64, D_MODEL)
    dev_major = lambda a: a.reshape(a.shape[0], N_DEV, a.shape[1] // N_DEV).transpose(1, 0, 2)
    return dict(w_in_e=dev_major(d_in_e), w_qb=dev_major(d_qb), w_kvb=dev_major(d_kvb),
                w_out_e=jnp.concatenate([d_oa, g["wob"]], axis=0).reshape(N_DEV, D_MODEL // N_DEV, D_MODEL))


def _local_step(x, positions, target, gw, sp, ex):
    w = _prep_weights(gw)
    T = x.shape[0]
    tm = min(TM, T)
    nt = T // tm
    half = MLA_ROPE // 2
    inv_freq = ROPE_BASE ** (-jnp.arange(half, dtype=F32) / half)
    invf_lane = jnp.concatenate([jnp.zeros((64,), F32), inv_freq, inv_freq, jnp.zeros((32,), F32)]).reshape(1, HEAD_W)
    tabs = _rope_tables(positions.reshape(T, 1), invf_lane)
    bias_full = jnp.repeat(sp["sgu_b"][0].T, 128, axis=1)
    sgu_w = sp["sgu_w"]
    gq, gkv = sp["mla_gq"], sp["mla_gkv"]
    ln1_g, ln1_b, ln2_g, ln2_b = sp["ln1_g"], sp["ln1_b"], sp["ln2_g"], sp["ln2_b"]
    zm, zs, cqn, ckvn, kr_rot = _mla_in(x, w["wm"], w["ws"], tabs, gq, gkv, deps=[ex.first_token])
    q, k, v = _mla_qkv(cqn, ckvn, kr_rot, tabs, w["wq"], w["wk"], w["wv"])
    o_att, lse = _attn_fwd(q, k, v)
    b_out = _sgu_fwd(zs, sp["sgu_ln_g"], sp["sgu_ln_b"], sgu_w, bias_full)
    token, (w_out_e, gnorm) = ex.weights_forward(after=[o_att, b_out])
    w["woa"], w["wob"] = _prep_out_weight(w_out_e)
    gnorm = gnorm.reshape(1, D_MODEL)
    y1, h1, h1_bf = _proj_ln("l0_out_ln1", [o_att, b_out], [w["woa"], w["wob"]], x, ln1_g, ln1_b, 0, deps=[token])
    big = ex.weights_ready(after=[y1])
    w_ff1, w_in_o, w_out_o = big["w_ff1"], big["w_in_o"], big["w_out_o"].reshape(D_MODEL, D_MODEL)
    w_ff2 = [a.reshape(D_FF, D_MODEL) for a in big["w_ff2"]]
    a0, act0 = _mlp_up("l0", h1_bf, w_ff1[0])
    y2, h2, h2_bf = _proj_ln("l0_ff2_ln2", [act0], [w_ff2[0]], h1, ln2_g, ln2_b, 0)

    z1 = _tiled("l1_in", (1, nt), [_rb(h2_bf, tm), _res(w_in_o)], [_out(T, 4 * D_MODEL, F32, tm, 4 * D_MODEL)],
                _mmc_blocks(N_DEV, NN, lambda w, d: w[d]), direct=True)
    o_pre, hg, states = _hgrn_fwd(z1, sp["hg_lb"], gnorm)
    y3, h3, h3_bf = _proj_ln("l1_out_ln1", [hg], [w_out_o], h2, ln1_g, ln1_b, 1)
    a1, act1 = _mlp_up("l1", h3_bf, w_ff1[1])

    gs, g0 = {}, {}
    dy4, dy4_bf, sq_err, gs["ln2_g1"], gs["ln2_b1"] = _proj_ln_loss("l1_ff2_loss", act1, w_ff2[1], h3, ln2_g, ln2_b, 1, target)
    gs["sq_err"] = sq_err
    da1, dw1_1, dw2_1 = _mlp_bwd_w("l1", h3_bf, a1, act1, dy4_bf, big["w_ff2"][1])
    dy3, dy3_bf, dhg, gs["ln1_g1"], gs["ln1_b1"] = _dh_ln_back("l1_dh_ln1", da1, w_ff1[1], dy4, y3, ln1_g, 1, proj=[w_out_o])
    d_out_o = _tiled("l1_dwout", (2, D_MODEL // TM), [_tl(hg, TM), _cw(dy3_bf, TN)],
                     [_out(D_MODEL, D_MODEL, F32, TM, TN), _out(D_MODEL, D_MODEL, BF16, TM, TN)], _mmc(TN_, epilogue=_twice))
    d_out_o = [a.reshape(N_DEV, D_MODEL // N_DEV, D_MODEL) for a in d_out_o]
    dz1, gs["hg_lb"], gs["hg_gnorm"] = _hgrn_bwd(z1, o_pre, dhg, states, sp["hg_lb"], gnorm)
    d_in_o = _tiled("l1_dwin", (N_DEV, 1), [_res(h2_bf), _cw(dz1, TN)],
                    [_out_dev(D_MODEL, TN, D_MODEL), _out_dev(D_MODEL, TN, D_MODEL, BF16)], _mmc(TN_, epilogue=_twice))
    token = ex.direct_start("l1", [dw1_1, dw2_1, d_in_o, d_out_o])

    dy2, dy2_bf, gs["ln2_g0"], gs["ln2_b0"] = _dh_ln_back("l1_dh_ln2", dz1, w_in_o, dy3, y2, ln2_g, 0, deps=[token])
    da0, dw1_0, dw2_0 = _mlp_bwd_w("l0", h1_bf, a0, act0, dy2_bf, big["w_ff2"][0])
    token = ex.direct_start("l0m", [dw1_0, dw2_0])
    dy1, dy1_bf, dcat, gs["ln1_g0"], gs["ln1_b0"] = _dh_ln_back("l0_dh_ln1", da0, w_ff1[0], dy2, y1, ln1_g, 0,
                                                                 proj=[w["woa"], w["wob"]], deps=[token])
    g0["woa"], g0["wob"] = _out_weight_grads(o_att, b_out, dy1_bf)
    dzs, gs["sgu_w"], gs["sgu_ln_g"], gs["sgu_ln_b"], gs["sgu_b"] = _sgu_bwd(zs, dcat, sp["sgu_ln_g"], sp["sgu_ln_b"], sgu_w, bias_full)
    dq, dk, dv = _attn_bwd(q, k, v, o_att, lse, dcat)
    dzm, g0["wq"], g0["wk"], g0["wv"], gs["mla_gq"], gs["mla_gkv"] = _mla_back(zm, cqn, ckvn, tabs, gq, gkv, w["wq"], w["wk"], w["wv"],
                                                                                 dq, dk, dv)
    token = ex.small_start(gs)
    dx, g0["wm"], g0["ws"] = _in_back(x, dzm, dzs, dy1, w["wm"], w["ws"], deps=[token])

    return sq_err, dx, _unprep_grads(g0), gs


def _me():
    return lax.axis_index("x"), lax.axis_index("y"), lax.axis_index("c")


ANY_SPEC = pl.BlockSpec(memory_space=pl.ANY)
HBM_SPEC = pl.BlockSpec(memory_space=pltpu.HBM)
SEM_SPEC = pl.BlockSpec(memory_space=pltpu.SEMAPHORE)
EFFECT = pltpu.SideEffectType.DATAFLOW_SIDE_EFFECTING


def _split_start(name, srcs, lands, n_sems, make_copies, after=()):
    n, m, k = len(srcs), len(lands), len(after)

    def body(*refs):
        for cp in make_copies(refs[:n], refs[n:n + m], refs[n + m + k], refs[n + m + k + 1]):
            cp.start()
        refs[-1][...] = jnp.zeros(refs[-1].shape, F32)

    out_shape = (pltpu.SemaphoreType.DMA((n_sems,)), pltpu.SemaphoreType.DMA((n_sems,)),
                 *[pltpu.HBM(a.shape, a.dtype) for a in (*srcs, *lands)], jax.ShapeDtypeStruct((8, 128), F32))
    res = pl.pallas_call(
        body, name=name, out_shape=out_shape, in_specs=[HBM_SPEC] * (n + m) + [ANY_SPEC] * k,
        out_specs=(SEM_SPEC, SEM_SPEC, *[HBM_SPEC] * (n + m), pl.BlockSpec(memory_space=pltpu.VMEM)),
        input_output_aliases={i: 2 + i for i in range(n + m)},
        compiler_params=pltpu.CompilerParams(has_side_effects=EFFECT),
    )(*[_hbm(a) for a in (*srcs, *lands)], *after)
    return res[0], res[1], list(res[2:2 + n]), list(res[2 + n:2 + n + m]), res[-1]


def _split_wait(name, send_sems, recv_sems, srcs, lands, after, make_copies):
    n, m = len(srcs), len(lands)

    def body(*refs):
        for cp in make_copies(refs[:n], refs[n:n + m], refs[n + m], refs[n + m + 1]):
            cp.wait_send()
            cp.wait_recv()

    res = pl.pallas_call(
        body, name=name, out_shape=tuple(pltpu.HBM(a.shape, a.dtype) for a in (*srcs, *lands)),
        in_specs=[HBM_SPEC] * (n + m) + [SEM_SPEC, SEM_SPEC] + [ANY_SPEC] * len(after), out_specs=tuple([HBM_SPEC] * (n + m)),
        input_output_aliases={i: i for i in range(n + m)},
        compiler_params=pltpu.CompilerParams(has_side_effects=EFFECT),
    )(*srcs, *lands, send_sems, recv_sems, *after)
    return list(res[:n]), list(res[n:])


def _place_own(shards, dev):
    n = len(shards)

    def kern(dev_ref, *refs):
        for x_ref, o_ref in zip(refs[:n], refs[n:]):
            o_ref[...] = x_ref[...].astype(o_ref.dtype)

    blocks = [(None, *a.shape[1:]) for a, _, _ in shards]
    nbytes = sum(_nbytes(b, a.dtype) + _nbytes(b, dt) for b, (a, _, dt) in zip(blocks, shards))
    return pl.pallas_call(
        kern, name="weights_place_own", out_shape=[pltpu.HBM((N_DEV, *a.shape[1:]), dt) for a, _, dt in shards],
        grid_spec=pltpu.PrefetchScalarGridSpec(
            num_scalar_prefetch=1, grid=(1,),
            in_specs=[pl.BlockSpec(b, functools.partial(lambda i, dev, l: (l, 0, 0), l=l)) for b, (_, l, _) in zip(blocks, shards)],
            out_specs=[pl.BlockSpec(b, lambda i, dev: (dev[0], 0, 0)) for b in blocks]),
        compiler_params=pltpu.CompilerParams(dimension_semantics=("arbitrary",), vmem_limit_bytes=_vmem(nbytes)),
    )(dev, *[_hbm(a) for a, _, _ in shards])


def _ag_first_copies(src_refs, out_refs, send_sems, recv_sems, n_direct=0):
    x, y, c = _me()
    me = 4 * x + 2 * y + c
    copies = []
    for op in range(n_direct):
        for r in range(1, N_DEV):
            copies.append(pltpu.make_async_remote_copy(
                src_ref=out_refs[op].at[me], dst_ref=out_refs[op].at[me], send_sem=send_sems.at[7 * op + r - 1],
                recv_sem=recv_sems.at[7 * op + r - 1], device_id=(x ^ (r >> 2), y ^ ((r >> 1) & 1), c ^ (r & 1)), device_id_type=MESH))
    targets = [(x, y, 1 - c), (1 - x, y, c), (x, 1 - y, c), (1 - x, 1 - y, c)]
    for op in range(n_direct, len(out_refs)):
        for k, to in enumerate(targets):
            sem = 7 * n_direct + 4 * (op - n_direct) + k
            copies.append(pltpu.make_async_remote_copy(
                src_ref=out_refs[op].at[me], dst_ref=out_refs[op].at[me], send_sem=send_sems.at[sem], recv_sem=recv_sems.at[sem],
                device_id=to, device_id_type=MESH))
    return copies


def _ag_second_copies(src_refs, out_refs, send_sems, recv_sems):
    x, y, c = _me()
    chips = [(1 - x, y), (x, 1 - y), (1 - x, 1 - y)]
    return [pltpu.make_async_remote_copy(
        src_ref=out_refs[op].at[4 * cx + 2 * cy + c], dst_ref=out_refs[op].at[4 * cx + 2 * cy + c],
        send_sem=send_sems.at[3 * op + j], recv_sem=recv_sems.at[3 * op + j], device_id=(x, y, 1 - c), device_id_type=MESH)
        for op in range(len(out_refs)) for j, (cx, cy) in enumerate(chips)]


def _rs_sibling_copies(g_refs, out_refs, send_sems, recv_sems):
    x, y, c = _me()
    return [pltpu.make_async_remote_copy(
        src_ref=g_refs[op].at[k, 1 - c], dst_ref=out_refs[op].at[k], send_sem=send_sems.at[4 * op + k],
        recv_sem=recv_sems.at[4 * op + k], device_id=(x, y, 1 - c), device_id_type=MESH)
        for op in range(len(g_refs)) for k in range(4)]


def _rs_direct_copies(g_refs, land_refs, send_sems, recv_sems):
    x, y, c = _me()
    n = len(g_refs) // 2
    chips = [(1 - x, y), (x, 1 - y), (1 - x, 1 - y)]
    copies = []
    for op in range(n):
        g32, g16, from_sib, from_others = g_refs[op], g_refs[n + op], land_refs[op], land_refs[n + op]
        copies.append(pltpu.make_async_remote_copy(
            src_ref=g32.at[2 * x + y, 1 - c], dst_ref=from_sib, send_sem=send_sems.at[7 * op], recv_sem=recv_sems.at[7 * op],
            device_id=(x, y, 1 - c), device_id_type=MESH))
        for j, (cx, cy) in enumerate(chips):
            for s, cc in enumerate((c, 1 - c)):
                copies.append(pltpu.make_async_remote_copy(
                    src_ref=g16.at[2 * cx + cy, cc], dst_ref=from_others.at[2 * j + s], send_sem=send_sems.at[7 * op + 1 + 2 * j + s],
                    recv_sem=recv_sems.at[7 * op + 1 + 2 * j + s], device_id=(cx, cy, cc), device_id_type=MESH))
    return copies


def _rs_chip_copies(p_refs, out_refs, send_sems, recv_sems):
    x, y, c = _me()
    chips = [(1 - x, y), (x, 1 - y), (1 - x, 1 - y)]
    return [pltpu.make_async_remote_copy(
        src_ref=p_refs[op].at[2 * cx + cy], dst_ref=out_refs[op].at[j], send_sem=send_sems.at[3 * op + j],
        recv_sem=recv_sems.at[3 * op + j], device_id=(cx, cy, c), device_id_type=MESH)
        for op in range(len(p_refs)) for j, (cx, cy) in enumerate(chips)]


def _all_gather(placed):
    n = len(placed)

    def kern(*refs):
        in_refs, out_refs, (send_sems, recv_sems) = refs[:n], refs[n:2 * n], refs[2 * n:]
        x, y, c = _me()
        me, sibling = (x, y, c), (x, y, 1 - c)
        chips = [(1 - x, y), (x, 1 - y), (1 - x, 1 - y)]

        def copy(op, k, block, to, own=False):
            idx = 4 * block[0] + 2 * block[1] + block[2]
            return pltpu.make_async_remote_copy(
                src_ref=(in_refs if own else out_refs)[op].at[idx], dst_ref=out_refs[op].at[idx], send_sem=send_sems.at[7 * op + k],
                recv_sem=recv_sems.at[7 * op + k], device_id=to, device_id_type=MESH)

        first = []
        for op in range(n):
            first.append(copy(op, 0, me, sibling, own=True))
            first += [copy(op, 1 + j, me, (*chip, c), own=True) for j, chip in enumerate(chips)]
        for cp in first:
            cp.start()
        passed = []
        for j, chip in enumerate(chips):
            for op in range(n):
                copy(op, 1 + j, (*chip, c), me).wait_recv()
                passed.append(copy(op, 4 + j, (*chip, c), sibling))
                passed[-1].start()
        for op in range(n):
            copy(op, 0, sibling, me).wait_recv()
            for j, chip in enumerate(chips):
                copy(op, 4 + j, (*chip, 1 - c), me).wait_recv()
        for cp in first + passed:
            cp.wait_send()

    return pl.pallas_call(
        kern, name="weights_all_gather", out_shape=[pltpu.HBM(g.shape, g.dtype) for g in placed],
        in_specs=[ANY_SPEC] * n, out_specs=[ANY_SPEC] * n, input_output_aliases={i: i for i in range(n)},
        scratch_shapes=[pltpu.SemaphoreType.DMA((7 * n,)), pltpu.SemaphoreType.DMA((7 * n,))],
    )(*[_hbm(a) for a in placed])


def _row_tile(r, w, n_blocks):
    tr = r
    while tr > 8 and 2 * n_blocks * tr * w * 4 > 24 * 2**20:
        tr //= 2
    return tr


def _chip_sum(name, g, from_sibling, core):
    _, _, R, W = g.shape
    tr = _row_tile(R, W, 3)

    def kern(core_ref, g_ref, s_ref, o_ref):
        o_ref[...] = (g_ref[...] + s_ref[...]).astype(BF16)

    return pl.pallas_call(
        kern, name=name, out_shape=pltpu.HBM((4, R, W), BF16),
        grid_spec=pltpu.PrefetchScalarGridSpec(
            num_scalar_prefetch=1, grid=(4, R // tr),
            in_specs=[pl.BlockSpec((None, None, tr, W), lambda k, i, core: (k, core[0], i, 0)),
                      pl.BlockSpec((None, tr, W), lambda k, i, core: (k, i, 0))],
            out_specs=pl.BlockSpec((None, tr, W), lambda k, i, core: (k, i, 0))),
        compiler_params=pltpu.CompilerParams(dimension_semantics=("parallel", "parallel"), vmem_limit_bytes=_vmem(3 * tr * W * 4)),
    )(core, _hbm(g), _hbm(from_sibling))


def _adamw(w, g, m, v):
    m = ADAM_B1 * m + (1.0 - ADAM_B1) * g
    v = ADAM_B2 * v + (1.0 - ADAM_B2) * (g * g)
    m_hat = m / (1.0 - ADAM_B1 ** ADAM_STEP)
    v_hat = v / (1.0 - ADAM_B2 ** ADAM_STEP)
    return -ADAM_LR * (m_hat / (jnp.sqrt(v_hat) + ADAM_EPS) + ADAM_WD * w), m, v


def _finish_sharded(name, layers, w, m, v, where, deps=()):
    nl, R, W = w.shape
    n_other = layers[0][2].shape[0]
    tr = _row_tile(R, W, (8 + n_other) * nl)
    deps = _deps(deps)

    def kern(where_ref, *refs):
        w_ref, m_ref, v_ref = refs[3 * nl:3 * nl + 3]
        go_ref, d_ref, mo_ref, vo_ref = refs[3 * nl + 3 + len(deps):]
        for l in range(nl):
            g_ref, s_ref, c_ref = refs[3 * l:3 * l + 3]
            grad = g_ref[...] + s_ref[...]
            for j in range(n_other):
                grad = grad + c_ref[j].astype(F32)
            go_ref[l] = grad
            d_ref[l], mo_ref[l], vo_ref[l] = _adamw(w_ref[l], grad, m_ref[l], v_ref[l])

    row = pl.BlockSpec((nl, tr, W), lambda i, wh: (0, i, 0))
    in_specs, args = [], []
    for g, s, c in layers:
        sib = (pl.BlockSpec((None, tr, W), lambda i, wh: (wh[0], i, 0)) if s.ndim == 3 else pl.BlockSpec((tr, W), lambda i, wh: (i, 0)))
        in_specs += [pl.BlockSpec((None, None, tr, W), lambda i, wh: (wh[0], wh[1], i, 0)), sib,
                     pl.BlockSpec((n_other, tr, W), lambda i, wh: (0, i, 0))]
        args += [g, s, c]
    return pl.pallas_call(
        kern, name=name, out_shape=[pltpu.HBM((nl, R, W), F32)] * 4,
        grid_spec=pltpu.PrefetchScalarGridSpec(num_scalar_prefetch=1, grid=(R // tr,),
                                               in_specs=in_specs + [row, row, row] + [ANY_SPEC] * len(deps),
                                               out_specs=[row, row, row, row]),
        compiler_params=pltpu.CompilerParams(dimension_semantics=("parallel",),
                                             vmem_limit_bytes=_vmem(nl * (8 + n_other) * tr * W * 4)),
    )(where, *[_hbm(a) for a in (*args, w, m, v)], *deps)


SMALL_PLACE = (("mla_gq", 0, 0, 1, 256), ("mla_gkv", 0, 256, 1, 256), ("sgu_ln_g", 0, 512, 1, 512), ("sgu_ln_b", 1, 0, 1, 512),
               ("hg_lb", 2, 0, 2, 1024), ("ln1_g", 4, 0, 2, 1024), ("ln1_b", 6, 0, 2, 1024), ("sgu_b", 8, 0, 4, 128),
               ("ln2_g", 12, 0, 2, 1024), ("ln2_b", 14, 0, 2, 1024), ("hg_gnorm", 16, 0, 1, 1024))
SMALL_BUF_ROWS = 24
LOSS_ROW = 17


def _small_pack(gs, dev):
    pieces = [(gs["mla_gq"], 0, 0), (gs["mla_gkv"], 0, 256), (gs["sgu_ln_g"], 0, 512), (gs["sgu_ln_b"], 1, 0), (gs["hg_lb"], 2, 0),
              (gs["ln1_g0"], 4, 0), (gs["ln1_g1"], 5, 0), (gs["ln1_b0"], 6, 0), (gs["ln1_b1"], 7, 0), (gs["sgu_b"], 8, 0),
              (gs["ln2_g0"], 12, 0), (gs["ln2_g1"], 13, 0), (gs["ln2_b0"], 14, 0), (gs["ln2_b1"], 15, 0), (gs["hg_gnorm"], 16, 0),
              (gs["sq_err"], LOSS_ROW, 0)]
    n_p = len(pieces)

    def kern(dev_ref, *refs):
        a_ref, b_ref = refs[n_p + 1], refs[n_p + 2]
        a_ref[...] = jnp.zeros(a_ref.shape, F32)
        for ref, (_, r, l0) in zip(refs[:n_p], pieces):
            a_ref[r:r + ref.shape[0], l0:l0 + ref.shape[1]] = ref[...]
        b_ref[...] = refs[n_p][...]

    whole = lambda a: pl.BlockSpec(a.shape, functools.partial(lambda i, dev, nd: (0,) * nd, nd=a.ndim))
    return pl.pallas_call(
        kern, name="small_grads_pack",
        out_shape=[pltpu.HBM((N_DEV, SMALL_BUF_ROWS, D_MODEL), F32), pltpu.HBM((N_DEV, SGU_G, 128, 128), F32)],
        grid_spec=pltpu.PrefetchScalarGridSpec(
            num_scalar_prefetch=1, grid=(1,), in_specs=[whole(p[0]) for p in pieces] + [whole(gs["sgu_w"])],
            out_specs=[pl.BlockSpec((None, SMALL_BUF_ROWS, D_MODEL), lambda i, dev: (dev[0], 0, 0)),
                       pl.BlockSpec((None, SGU_G, 128, 128), lambda i, dev: (dev[0], 0, 0, 0))]),
    )(dev, *[p[0] for p in pieces], gs["sgu_w"])


def _small_copies(src_refs, land_refs, send_sems, recv_sems):
    px, py, pc = _me()
    me = 4 * px + 2 * py + pc
    return [pltpu.make_async_remote_copy(
        src_ref=land_refs[k].at[me], dst_ref=land_refs[k].at[me], send_sem=send_sems.at[2 * (r - 1) + k],
        recv_sem=recv_sems.at[2 * (r - 1) + k], device_id=(px ^ (r >> 2), py ^ ((r >> 1) & 1), pc ^ (r & 1)), device_id_type=MESH)
        for r in range(1, N_DEV) for k in range(2)]


def _small_adamw(slots_a, slots_b, given):
    names = [p[0] for p in SMALL_PLACE] + ["sgu_w"]
    n_names = len(names)
    wmv = [given[pre + name] for name in names for pre in ("", "m_", "v_")]
    vmem = pl.BlockSpec(memory_space=pltpu.VMEM)

    def kern(*refs):
        sum_a, sum_b = refs[0][0], refs[1][0]
        for d in range(1, N_DEV):
            sum_a, sum_b = sum_a + refs[0][d], sum_b + refs[1][d]
        wmv_refs, out_refs = refs[2:2 + 3 * n_names], refs[2 + 3 * n_names:]
        px, py, pc = _me()
        me = 4 * px + 2 * py + pc

        def own_block(full):
            acc = full[:, 0:128]
            for b in range(1, N_DEV):
                acc = jnp.where(me == b, full[:, b * 128:(b + 1) * 128], acc)
            return acc

        for idx, name in enumerate(names):
            w_ref, m_ref, v_ref = wmv_refs[3 * idx:3 * idx + 3]
            if name == "sgu_w":
                grad = sum_b[None]
            else:
                _, r, l0, nr, nl = SMALL_PLACE[idx]
                grad = sum_a[r:r + nr, l0:l0 + nl]
                if name == "hg_gnorm":
                    grad = own_block(grad)
                if name == "sgu_b":
                    grad = grad[None]
            res = (grad, *_adamw(w_ref[...], grad, m_ref[...], v_ref[...]))
            for o_ref, val in zip(out_refs[4 * idx:4 * idx + 4], res):
                o_ref[...] = val
        out_refs[4 * n_names][...] = (0.5 / D_MODEL) * jnp.sum(sum_a[LOSS_ROW:LOSS_ROW + 1, :], axis=1, keepdims=True)

    out_shape = [jax.ShapeDtypeStruct(given[name].shape, F32) for name in names for _ in range(4)]
    out_shape.append(jax.ShapeDtypeStruct((1, 1), F32))
    res = pl.pallas_call(
        kern, name="small_adamw", out_shape=out_shape, in_specs=[vmem] * (2 + len(wmv)), out_specs=[vmem] * len(out_shape),
    )(slots_a, slots_b, *wmv)
    out = {name: res[4 * idx:4 * idx + 4] for idx, name in enumerate(names)}
    out["loss"] = res[-1].reshape(())
    return out


class _Exchange:
    def __init__(self, given):
        self.given = given
        px, py, pc = _me()
        self.core = pc.reshape(1).astype(jnp.int32)
        self.dev = (4 * px + 2 * py + pc).reshape(1).astype(jnp.int32)
        self.where = jnp.stack([2 * px + py, pc]).astype(jnp.int32)
        self.state, self.layers = {}, {}

    def start_weights(self, direct, lands, after):
        self.first_copies = functools.partial(_ag_first_copies, n_direct=len(direct))
        self.n_direct = len(direct)
        self.weights = _split_start("weights_first_start", [], [*direct, *lands], 7 * len(direct) + 4 * len(lands),
                                    self.first_copies, after=after)
        self.first_token = self.weights[4]

    def weights_forward(self, after):
        send_sems, recv_sems, shards, lands, _ = self.weights
        _, lands = _split_wait("weights_first_wait", send_sems, recv_sems, shards, lands, after, self.first_copies)
        direct, lands = lands[:self.n_direct], lands[self.n_direct:]
        self.weights = _split_start("weights_second_start", [], lands, 3 * len(lands), _ag_second_copies)
        return self.weights[4], direct

    def weights_ready(self, after):
        send_sems, recv_sems, shards, lands, _ = self.weights
        _, got = _split_wait("weights_second_wait", send_sems, recv_sems, shards, lands, after, _ag_second_copies)
        return dict(w_in_o=got[0], w_out_o=got[1], w_ff1=[got[2], got[3]], w_ff2=[got[4], got[5]])

    def small_start(self, gs):
        self.small = _split_start("small_grads_start", [], _small_pack(gs, self.dev), 14, _small_copies)
        return self.small[4]

    def small_finish(self, after):
        send_sems, recv_sems, _, lands, _ = self.small
        _, lands = _split_wait("small_grads_wait", send_sems, recv_sems, [], lands, after, _small_copies)
        return _small_adamw(lands[0], lands[1], self.given)

    def direct_start(self, tag, grads):
        f32 = [g[0].reshape(4, 2, *g[0].shape[1:]) for g in grads]
        bf16 = [g[1].reshape(4, 2, *g[1].shape[1:]) for g in grads]
        lands = [lax.empty(b.shape[2:], F32) for b in f32] + [lax.empty((6, *b.shape[2:]), BF16) for b in f32]
        self.state[tag] = _split_start(f"grads_{tag}_start", f32 + bf16, lands, 7 * len(grads), _rs_direct_copies)
        return self.state[tag][4]

    def direct_end(self, tag, after):
        send_sems, recv_sems, srcs, lands, _ = self.state[tag]
        srcs, lands = _split_wait(f"grads_{tag}_wait", send_sems, recv_sems, srcs, lands, after, _rs_direct_copies)
        n = len(lands) // 2
        self.layers[tag] = list(zip(srcs[:n], lands[:n], lands[n:]))

    def grads_start(self, tag, grads):
        blocks = [g.reshape(4, 2, *g.shape[1:]) for g in grads]
        lands = [lax.empty((4, *b.shape[2:]), F32) for b in blocks]
        self.state[tag] = _split_start(f"grads_{tag}_sibling_start", blocks, lands, 4 * len(blocks), _rs_sibling_copies)
        return self.state[tag][4]

    def grads_middle(self, tag, after):
        send_sems, recv_sems, blocks, lands, _ = self.state[tag]
        blocks, from_sibling = _split_wait(f"grads_{tag}_sibling_wait", send_sems, recv_sems, blocks, lands, [after], _rs_sibling_copies)
        sums = [_chip_sum(f"grads_{tag}_chip_sum_{k}", b, s, self.core) for k, (b, s) in enumerate(zip(blocks, from_sibling))]
        lands = [lax.empty((3, *p.shape[1:]), BF16) for p in sums]
        self.state[tag] = (blocks, from_sibling, _split_start(f"grads_{tag}_chips_start", sums, lands, 3 * len(sums), _rs_chip_copies))
        return self.state[tag][2][4]

    def grads_end(self, tag, after):
        blocks, from_sibling, (send_sems, recv_sems, sums, lands, _) = self.state[tag]
        after = list(after) if isinstance(after, (list, tuple)) else [after]
        _, from_chips = _split_wait(f"grads_{tag}_chips_wait", send_sems, recv_sems, sums, lands, after, _rs_chip_copies)
        self.layers[tag] = list(zip(blocks, from_sibling, from_chips))


def kernel(x, positions, w_in_e, mla_gq, mla_gkv, w_qb, w_kvb, sgu_ln_g, sgu_ln_b, sgu_w, sgu_b, w_out_e, w_in_o, hg_lb, hg_gnorm, w_out_o, ln1_g, ln1_b, w_ff1, w_ff2, ln2_g, ln2_b, loss_target, m_w_in_e, m_mla_gq, m_mla_gkv, m_w_qb, m_w_kvb, m_sgu_ln_g, m_sgu_ln_b, m_sgu_w, m_sgu_b, m_w_out_e, m_w_in_o, m_hg_lb, m_hg_gnorm, m_w_out_o, m_ln1_g, m_ln1_b, m_w_ff1, m_w_ff2, m_ln2_g, m_ln2_b, v_w_in_e, v_mla_gq, v_mla_gkv, v_w_qb, v_w_kvb, v_sgu_ln_g, v_sgu_ln_b, v_sgu_w, v_sgu_b, v_w_out_e, v_w_in_o, v_hg_lb, v_hg_gnorm, v_w_out_o, v_ln1_g, v_ln1_b, v_w_ff1, v_w_ff2, v_ln2_g, v_ln2_b):
    given = dict(locals())
    ex = _Exchange(given)

    names = ["w_in_e", "w_qb", "w_kvb", "w_out_e"]
    placed = _place_own([(given[n], 0, BF16) for n in names] + [(hg_gnorm.reshape(1, 1, D_MODEL // N_DEV), 0, F32)]
                        + [(w_in_o, 0, BF16), (w_out_o, 0, BF16), (w_ff1, 0, BF16), (w_ff1, 1, BF16), (w_ff2, 0, BF16), (w_ff2, 1, BF16)],
                        ex.dev)
    got = _all_gather(placed[:3])
    ex.start_weights(placed[3:5], placed[5:], after=[got[0]])
    gw = dict(zip(names[:3], got))
    small_names = ["mla_gq", "mla_gkv", "sgu_ln_g", "sgu_ln_b", "sgu_w", "sgu_b", "hg_lb", "ln1_g", "ln1_b", "ln2_g", "ln2_b"]
    sp = {n: given[n] for n in small_names}

    _, dx, grads, gs = _local_step(x[0], positions[0], loss_target[0], gw, sp, ex)

    def finish(n, layers, deps=()):
        return _finish_sharded(f"finish_{n}", layers, given[n], given["m_" + n], given["v_" + n], ex.where, deps=deps)

    ex.direct_end("l1", after=[dx])
    ex.direct_end("l0m", after=[dx])
    l1, l0m = ex.layers["l1"], ex.layers["l0m"]
    results = {}
    token = ex.grads_start("l0s", [grads[n] for n in names])
    results["w_ff1"] = finish("w_ff1", [l0m[0], l1[0]], deps=[token])
    token = ex.grads_middle("l0s", after=results["w_ff1"][0])
    results["w_ff2"] = finish("w_ff2", [l0m[1], l1[1]], deps=[token])
    results["w_in_o"] = finish("w_in_o", [l1[2]], deps=[token])
    results["w_out_o"] = finish("w_out_o", [l1[3]], deps=[token])
    results.update(ex.small_finish(after=[results["w_in_o"][0]]))
    ex.grads_end("l0s", after=[results[n][0] for n in ("mla_gq", "w_ff2", "w_in_o", "w_out_o")])
    for n, layer in zip(names, ex.layers["l0s"]):
        results[n] = finish(n, [layer])

    order = ["w_in_e", "mla_gq", "mla_gkv", "w_qb", "w_kvb", "sgu_ln_g", "sgu_ln_b", "sgu_w", "sgu_b", "w_out_e", "w_in_o",
             "hg_lb", "hg_gnorm", "w_out_o", "ln1_g", "ln1_b", "w_ff1", "w_ff2", "ln2_g", "ln2_b"]
    return (results["loss"], dx[None], *[results[name][kind] for kind in range(4) for name in order])
```

```python
import functools
import math

import jax
import jax.numpy as jnp
import numpy as np
from jax import lax
from jax.experimental import pallas as pl
from jax.experimental.pallas import tpu as pltpu

F32 = jnp.float32
BF16 = jnp.bfloat16
MESH = pl.DeviceIdType.MESH
HIGHEST = lax.Precision.HIGHEST

D_MODEL = 1024
D_FF = 4096
N_DEV = 8
HEADS = 8
HEAD_W = 128
MLA_NOPE = 64
MLA_ROPE = 32
MLA_V = 64
MLA_LORA = 256
MLA_SCALE = (MLA_NOPE + MLA_ROPE) ** -0.5
ROPE_BASE = 10000.0
SGU_DIM = 512
SGU_G = 4
SGU_CHUNK = 128
HG_CHUNK = 64
HG_CHUNKS_PER_STEP = 4
ALPHA = (2 * 2) ** 0.25
EPS = 1e-5
ADAM_LR, ADAM_B1, ADAM_B2, ADAM_EPS, ADAM_WD, ADAM_STEP = 0.001, 0.9, 0.999, 1e-08, 0.01, 10

VMEM_CAP_V7X = 56 * 2**20
VMEM_SLACK = 12 * 2**20
TM = 512
TN = 512


def _vmem(block_bytes):
    return int(min(VMEM_CAP_V7X, 2 * block_bytes + VMEM_SLACK))


def _hbm(a):
    return pltpu.with_memory_space_constraint(a, pltpu.HBM)


def _nbytes(shape, dtype):
    return int(np.prod([d for d in shape if d is not None])) * jnp.dtype(dtype).itemsize


def _sig(x):
    return 1.0 / (1.0 + jnp.exp(-x))


def _gelu(x):
    c = math.sqrt(2.0 / math.pi)
    t = jnp.tanh(c * (x + 0.044715 * x * x * x))
    return 0.5 * x * (1.0 + t), t


def _gelu_grad(x, t):
    c = math.sqrt(2.0 / math.pi)
    return 0.5 * (1.0 + t) + 0.5 * x * (1.0 - t * t) * c * (1.0 + 3 * 0.044715 * x * x)


def _dot(a, b, dims, precision=None):
    return lax.dot_general(a, b, (dims, ((), ())), preferred_element_type=F32, precision=precision)


NN = ((1,), (0,))
NT = ((1,), (1,))
TN_ = ((0,), (0,))


def _deps(deps):
    return [d for d in deps if d is not None]


def _tiled(name, grid, ins, outs, compute, direct=False, deps=()):
    n_in, deps = len(ins), _deps(deps)
    n_skip = n_in + len(deps)

    def kern(*refs):
        if direct:
            compute(refs[:n_in], refs[n_skip:])
            return
        for o_ref, r in zip(refs[n_skip:], compute(*refs[:n_in])):
            o_ref[...] = r.astype(o_ref.dtype).reshape(o_ref.shape)

    swap = lambda f: (lambda j, i: f(i, j))
    nbytes = sum(_nbytes(blk, a.dtype) for a, blk, _ in ins) + sum(_nbytes(blk, dt) + _nbytes(blk, F32) for _, dt, blk, _ in outs)
    res = pl.pallas_call(
        kern, name=name, grid=grid,
        in_specs=[pl.BlockSpec(blk, swap(f), pipeline_mode=pl.Buffered(1) if tuple(blk) == tuple(a.shape) else None)
                  for a, blk, f in ins] + [ANY_SPEC] * len(deps),
        out_specs=[pl.BlockSpec(blk, swap(f)) for _, _, blk, f in outs],
        out_shape=[pltpu.HBM(shape, dt) for shape, dt, _, _ in outs],
        compiler_params=pltpu.CompilerParams(dimension_semantics=("parallel", "parallel"), vmem_limit_bytes=_vmem(nbytes)),
    )(*[_hbm(a) for a, _, _ in ins], *deps)
    return res if len(res) > 1 else res[0]


def _rb(a, tm, w=None, cb=0):
    return (a, (tm, a.shape[1] if w is None else w), lambda i, j: (i, cb))


def _cw(b, tn):
    return (b, (b.shape[0], tn), lambda i, j: (0, j))


def _tl(a, tm):
    return (a, (a.shape[0], tm), lambda i, j: (0, i))


def _out(m, n, dtype, tm, tn):
    return ((m, n), dtype, (tm, tn), lambda i, j: (i, j))


def _out_dev(k, n, tm, dtype=F32):
    return ((N_DEV, k, n), dtype, (None, tm, n), lambda i, j: (j, i, 0))


def _twice(acc):
    return acc, acc


def _mmc(dims, n_pairs=1, epilogue=None):
    def compute(*refs):
        acc = None
        for k in range(n_pairs):
            d = _dot(refs[2 * k][...].astype(BF16), refs[2 * k + 1][...].astype(BF16), dims)
            acc = d if acc is None else acc + d
        ext = [r[...] for r in refs[2 * n_pairs:]]
        return epilogue(acc, *ext) if epilogue is not None else (acc,)

    return compute


def _res(w):
    return (w, w.shape, functools.partial(lambda i, j, nd: (0,) * nd, nd=w.ndim))


def _mmc_blocks(nblk, dims, rhs_block, epilogue=None):
    def compute(in_refs, out_refs):
        a = in_refs[0][...].astype(BF16)
        for d in range(nblk):
            acc = _dot(a, rhs_block(in_refs[1], d).astype(BF16), dims)
            n = acc.shape[1]
            ext = [r[:, d * n:(d + 1) * n] for r in in_refs[2:]]
            res = epilogue(acc, *ext) if epilogue is not None else (acc,)
            for o_ref, r in zip(out_refs, res):
                o_ref[:, d * n:(d + 1) * n] = r.astype(o_ref.dtype)

    return compute


def _rowwise(name, body, rows, consts, out_rows, out_accs=(), tr=512, deps=()):
    T = rows[0][0].shape[0]
    tr = min(tr, T)
    deps = _deps(deps)
    nr, ncn, no, nd = len(rows), len(consts), len(out_rows), len(deps)

    def kern(*refs):
        accs = refs[nr + ncn + nd + no:]
        if accs:
            @pl.when(pl.program_id(0) == 0)
            def _():
                for a in accs:
                    a[...] = jnp.zeros(a.shape, a.dtype)
        body(refs[:nr], refs[nr:nr + ncn], refs[nr + ncn + nd:nr + ncn + nd + no], accs)

    in_specs = [pl.BlockSpec((tr, w), functools.partial(lambda i, cb: (i, cb), cb=cb)) for _, w, cb in rows]
    in_specs += [pl.BlockSpec(c.shape, functools.partial(lambda i, nd: (0,) * nd, nd=c.ndim), pipeline_mode=pl.Buffered(1))
                 for c in consts]
    in_specs += [ANY_SPEC] * nd
    out_specs = [pl.BlockSpec((tr, w), lambda i: (i, 0)) for w, _ in out_rows]
    out_specs += [pl.BlockSpec(s, functools.partial(lambda i, nd: (0,) * nd, nd=len(s))) for s, _ in out_accs]
    out_shape = [pltpu.HBM((T, w), dt) for w, dt in out_rows]
    out_shape += [pltpu.HBM(s, dt) for s, dt in out_accs]
    nbytes = sum(_nbytes((tr, w), a.dtype) for a, w, _ in rows) + sum(_nbytes(c.shape, c.dtype) for c in consts)
    nbytes += sum(_nbytes((tr, w), dt) for w, dt in out_rows) + sum(_nbytes(s, dt) for s, dt in out_accs)
    res = pl.pallas_call(
        kern, name=name, grid=(T // tr,), in_specs=in_specs, out_specs=out_specs, out_shape=out_shape,
        compiler_params=pltpu.CompilerParams(dimension_semantics=("arbitrary",), vmem_limit_bytes=_vmem(nbytes)),
    )(*[_hbm(a) for a, _, _ in rows], *[_hbm(c) for c in consts], *deps)
    return res if len(res) > 1 else res[0]


def _full(a):
    return (a, a.shape[1], 0)


def _ln_stats(y):
    mu = jnp.mean(y, axis=-1, keepdims=True)
    yc = y - mu
    r = lax.rsqrt(jnp.mean(yc * yc, axis=-1, keepdims=True) + EPS)
    return yc * r, r


def _row_halves(n):
    return [slice(0, n // 2), slice(n // 2, n)] if n >= 256 else [slice(0, n)]


def _ln_back(dh, xh, r, gain, dg_ref, db_ref):
    dg_ref[...] += jnp.sum(dh * xh, axis=0, keepdims=True)
    db_ref[...] += jnp.sum(dh, axis=0, keepdims=True)
    dx = dh * gain
    return r * (dx - jnp.mean(dx, axis=-1, keepdims=True) - xh * jnp.mean(dx * xh, axis=-1, keepdims=True))


def _proj_ln(name, acts, weights, h_in, g, b, layer, deps=()):
    n = len(acts)

    def body(rows, consts, outs, accs):
        acc = None
        for k in range(n):
            d = _dot(rows[k][...].astype(BF16), consts[k][...], NN)
            acc = d if acc is None else acc + d
        y = ALPHA * rows[n][...] + acc
        xh, _ = _ln_stats(y)
        h = xh * consts[n][layer:layer + 1, :] + consts[n + 1][layer:layer + 1, :]
        outs[0][...] = y
        outs[1][...] = h
        outs[2][...] = h.astype(BF16)

    return _rowwise(name, body, [_full(a) for a in acts] + [_full(h_in)], [*weights, g, b],
                    [(D_MODEL, F32), (D_MODEL, F32), (D_MODEL, BF16)], tr=TM, deps=deps)


def _proj_ln_loss(name, act, w2, h_in, g, b, layer, target):
    def body(rows, consts, outs, accs):
        y = ALPHA * rows[1][...] + _dot(rows[0][...], consts[0][...], NN)
        xh, r = _ln_stats(y)
        gain = consts[1][layer:layer + 1, :]
        err = xh * gain + consts[2][layer:layer + 1, :] - rows[2][...]
        accs[0][...] += jnp.sum(err * err, axis=0, keepdims=True)
        dy = _ln_back(err * (1.0 / D_MODEL), xh, r, gain, accs[1], accs[2])
        outs[0][...] = dy
        outs[1][...] = dy.astype(BF16)

    return _rowwise(name, body, [_full(act), _full(h_in), _full(target)], [w2, g, b], [(D_MODEL, F32), (D_MODEL, BF16)],
                    [((1, D_MODEL), F32)] * 3, tr=TM)


def _dh_ln_back(name, da, w, dy_next, y, g, layer, proj=(), deps=()):
    def body(rows, consts, outs, accs):
        n = consts[0].shape[2]
        for sl in _row_halves(rows[0].shape[0]):
            acc = ALPHA * rows[1][sl, :]
            for d in range(N_DEV):
                acc = acc + _dot(rows[0][sl, d * n:(d + 1) * n], consts[0][d], NT)
            xh, r = _ln_stats(rows[2][sl, :])
            dy = _ln_back(acc, xh, r, consts[1][layer:layer + 1, :], accs[0], accs[1])
            outs[0][sl, :] = dy
            dy_bf = dy.astype(BF16)
            outs[1][sl, :] = dy_bf
            off = 0
            for k, p in enumerate(proj):
                outs[2][sl, off:off + p.shape[0]] = _dot(dy_bf, consts[2 + k][...], NT).astype(BF16)
                off += p.shape[0]

    out_rows = [(D_MODEL, F32), (D_MODEL, BF16)] + ([(sum(p.shape[0] for p in proj), BF16)] if proj else [])
    return _rowwise(name, body, [_full(da), _full(dy_next), _full(y)], [w, g, *proj], out_rows,
                    [((1, D_MODEL), F32)] * 2, tr=TM, deps=deps)


def _relu2_epilogue(acc):
    a = jnp.maximum(acc, 0.0)
    return acc, a * a


def _mlp_up(tag, h_bf, w1):
    T = h_bf.shape[0]
    tm = min(TM, T)
    return _tiled(f"{tag}_ff1", (1, T // tm), [_rb(h_bf, tm), _res(w1)],
                  [_out(T, D_FF, BF16, tm, D_FF), _out(T, D_FF, BF16, tm, D_FF)],
                  _mmc_blocks(N_DEV, NN, lambda w, d: w[d], epilogue=_relu2_epilogue), direct=True)


def _mlp_bwd_w(tag, h_bf, a, act, dff_bf, w2, deps=()):
    T = h_bf.shape[0]
    tm = min(TM, T)
    da = _tiled(f"{tag}_dact", (1, T // tm), [_rb(dff_bf, tm), _res(w2), _rb(a, tm)], [_out(T, D_FF, BF16, tm, D_FF)],
                _mmc_blocks(N_DEV, NT, lambda w, d: w[d], epilogue=lambda acc, a_t: (acc * 2.0 * jnp.maximum(a_t.astype(F32), 0.0),)),
                direct=True, deps=deps)
    dw2 = _tiled(f"{tag}_dw2", (1, D_FF // TM), [_tl(act, TM), _res(dff_bf)],
                 [_out(D_FF, D_MODEL, F32, TM, D_MODEL), _out(D_FF, D_MODEL, BF16, TM, D_MODEL)], _mmc(TN_, epilogue=_twice))
    dw1 = _tiled(f"{tag}_dw1", (N_DEV, 1), [_res(h_bf), _cw(da, TN)],
                 [_out_dev(D_MODEL, TN, D_MODEL), _out_dev(D_MODEL, TN, D_MODEL, BF16)], _mmc(TN_, epilogue=_twice))
    return da, dw1, [a.reshape(N_DEV, D_FF // N_DEV, D_MODEL) for a in dw2]


def _rope_tables(positions_col, invf_lane):
    def body(rows, consts, outs, accs):
        ang = rows[0][...].astype(F32) * consts[0][...]
        c, s = jnp.cos(ang), jnp.sin(ang)
        lane = lax.broadcasted_iota(jnp.int32, ang.shape, 1)
        outs[0][...] = jnp.where(lane < 64, 1.0, jnp.where(lane < 96, c, 0.0))
        outs[1][...] = jnp.where((lane >= 64) & (lane < 80), -s, 0.0)
        outs[2][...] = jnp.where((lane >= 80) & (lane < 96), s, 0.0)

    return _rowwise("rope_tables", body, [_full(positions_col)], [invf_lane], [(HEAD_W, F32)] * 3)


def _rope(x, c, s1, s2):
    return x * c + pltpu.roll(x, 112, 1) * s1 + pltpu.roll(x, 16, 1) * s2


def _rope_t(dx, c, s1, s2):
    return dx * c + pltpu.roll(dx * s1, 16, 1) + pltpu.roll(dx * s2, 112, 1)


def _rms(c):
    r = lax.rsqrt(jnp.mean(c * c, axis=-1, keepdims=True) + EPS)
    return c * r, r


def _rope_heads(x, c, s1, s2, fn):
    return jnp.concatenate([fn(x[:, h * HEAD_W:(h + 1) * HEAD_W], c, s1, s2) for h in range(HEADS)], axis=1)


def _mla_in(x, wm, ws, tabs, gq, gkv, deps=()):
    def body(rows, consts, outs, accs):
        xb = rows[0][...].astype(BF16)
        zm = _dot(xb, consts[0][...], NN)
        outs[0][...] = zm
        outs[1][...] = _dot(xb, consts[1][...], NN)
        outs[2][...] = (_rms(zm[:, 0:256])[0] * consts[2][...]).astype(BF16)
        outs[3][...] = (_rms(zm[:, 256:512])[0] * consts[3][...]).astype(BF16)
        outs[4][...] = _rope(zm[:, 512:640], rows[1][...], rows[2][...], rows[3][...])

    return _rowwise("l0_in", body, [_full(x)] + [_full(t) for t in tabs], [wm, ws, gq, gkv],
                    [(640, F32), (1024, F32), (256, BF16), (256, BF16), (HEAD_W, F32)], deps=deps)


def _mla_qkv(cqn, ckvn, kr_rot, tabs, wq, wk, wv):
    def body(rows, consts, outs, accs):
        c, s1, s2 = rows[3][...], rows[4][...], rows[5][...]
        outs[0][...] = _rope_heads(_dot(rows[0][...], consts[0][...], NN), c, s1, s2, _rope).astype(BF16)
        outs[1][...] = (_dot(rows[1][...], consts[1][...], NN) + jnp.concatenate([rows[2][...]] * HEADS, axis=1)).astype(BF16)
        outs[2][...] = _dot(rows[1][...], consts[2][...], NN).astype(BF16)

    rows = [_full(cqn), _full(ckvn), _full(kr_rot)] + [_full(t) for t in tabs]
    return _rowwise("l0_qkv", body, rows, [wq, wk, wv], [(HEADS * HEAD_W, BF16)] * 3)


def _mla_back(zm, cqn, ckvn, tabs, gq, gkv, wq, wk, wv, dq, dk, dv):
    def body(rows, consts, outs, accs):
        c, s1, s2 = rows[4][...], rows[5][...], rows[6][...]
        dk_t, dv_bf = rows[8][...], rows[9][...].astype(BF16)
        dq_bf = _rope_heads(rows[7][...], c, s1, s2, _rope_t).astype(BF16)
        dk_bf = dk_t.astype(BF16)
        accs[0][...] += _dot(rows[2][...], dq_bf, TN_)
        accs[1][...] += _dot(rows[3][...], dk_bf, TN_)
        accs[2][...] += _dot(rows[3][...], dv_bf, TN_)
        dlat = [_dot(dq_bf, consts[2][...], NT), _dot(dk_bf, consts[3][...], NT) + _dot(dv_bf, consts[4][...], NT)]
        for k in range(2):
            ch, r = _rms(rows[k][...])
            accs[3 + k][...] += jnp.sum(dlat[k] * ch, axis=0, keepdims=True)
            dc = dlat[k] * consts[k][...]
            outs[0][:, 256 * k:256 * (k + 1)] = (r * (dc - ch * jnp.mean(dc * ch, axis=-1, keepdims=True))).astype(BF16)
        dks = dk_t[:, 0:HEAD_W]
        for h in range(1, HEADS):
            dks = dks + dk_t[:, h * HEAD_W:(h + 1) * HEAD_W]
        lane = lax.broadcasted_iota(jnp.int32, dks.shape, 1)
        dks = jnp.where((lane >= 64) & (lane < 96), dks, 0.0)
        outs[0][:, 512:640] = _rope_t(dks, c, s1, s2).astype(BF16)

    rows = [(zm, 256, 0), (zm, 256, 1), _full(cqn), _full(ckvn)] + [_full(t) for t in tabs] + [_full(dq), _full(dk), _full(dv)]
    wide = HEADS * HEAD_W
    return _rowwise("l0_mla_back", body, rows, [gq, gkv, wq, wk, wv], [(640, BF16)],
                    [((MLA_LORA, wide), F32)] * 3 + [((1, MLA_LORA), F32)] * 2, tr=256)


def _in_back(x, dzm, dzs, dy, wm, ws, deps=()):
    def body(rows, consts, outs, accs):
        dzm_t, dzs_t = rows[1][...], rows[2][...]
        outs[0][...] = _dot(dzm_t, consts[0][...], NT) + _dot(dzs_t, consts[1][...], NT) + ALPHA * rows[3][...]
        xb = rows[0][...].astype(BF16)
        accs[0][...] += _dot(xb, dzm_t, TN_)
        accs[1][...] += _dot(xb, dzs_t, TN_)

    return _rowwise("l0_in_back", body, [_full(x), _full(dzm), _full(dzs), _full(dy)], [wm, ws], [(D_MODEL, F32)],
                    [((D_MODEL, 640), F32), ((D_MODEL, 1024), F32)], deps=deps)


def _out_weight_grads(o_att, b_out, dy_bf):
    def body(rows, consts, outs, accs):
        d = rows[2][...]
        accs[0][...] += _dot(rows[0][...].astype(BF16), d, TN_)
        accs[1][...] += _dot(rows[1][...], d, TN_)

    return _rowwise("l0_dw_out", body, [_full(o_att), _full(b_out), _full(dy_bf)], [], [],
                    [((HEADS * HEAD_W, D_MODEL), F32), ((SGU_DIM, D_MODEL), F32)])


def _attn_block(T):
    return min(1024, T)


def _attn_fwd(q, k, v):
    T = q.shape[0]
    BQ = _attn_block(T)
    nq = T // BQ

    def kern(q_ref, k_ref, v_ref, o_ref, lse_ref):
        def tile(q0, k0, n, carry, masked):
            m, l, acc = carry
            qb = q_ref[pl.ds(pl.multiple_of(q0, n), n), :]
            kb = k_ref[pl.ds(pl.multiple_of(k0, n), n), :]
            vb = v_ref[pl.ds(pl.multiple_of(k0, n), n), :]
            s = _dot(qb, kb, NT) * MLA_SCALE
            if masked:
                row = lax.broadcasted_iota(jnp.int32, s.shape, 0)
                col = lax.broadcasted_iota(jnp.int32, s.shape, 1)
                s = jnp.where(col <= row, s, -1e30)
            m_new = jnp.maximum(m, jnp.max(s, axis=-1, keepdims=True))
            p = jnp.exp(s - m_new)
            a = jnp.exp(m - m_new)
            l = a * l + jnp.sum(p, axis=-1, keepdims=True)
            acc = a * acc + _dot(p.astype(BF16), vb, NN)
            return m_new, l, acc

        def qloop(i, _):
            init = (jnp.full((BQ, 1), -1e30, F32), jnp.zeros((BQ, 1), F32), jnp.zeros((BQ, HEAD_W), F32))
            base, half = i * BQ, BQ // 2
            carry = lax.fori_loop(0, i, lambda j, c: tile(base, j * BQ, BQ, c, False), init)
            early = tile(base, base, half, tuple(a[:half] for a in carry), True)
            late = tile(base + half, base, half, tuple(a[half:] for a in carry), False)
            late = tile(base + half, base + half, half, late, True)
            m, l, acc = (jnp.concatenate([a, b], axis=0) for a, b in zip(early, late))
            rows = pl.ds(pl.multiple_of(base, BQ), BQ)
            o_ref[rows, :] = acc / l
            lse_ref[0, rows, :] = m + jnp.log(l)
            return 0

        lax.fori_loop(0, nq, qloop, 0)

    head = pl.BlockSpec((T, HEAD_W), lambda h: (0, h))
    nbytes = 3 * _nbytes((T, HEAD_W), BF16) + _nbytes((T, HEAD_W), F32) + _nbytes((T, 128), F32)
    return pl.pallas_call(
        kern, name="attn_fwd", grid=(HEADS,), in_specs=[head, head, head],
        out_specs=[head, pl.BlockSpec((1, T, 1), lambda h: (h, 0, 0))],
        out_shape=[pltpu.HBM((T, HEADS * HEAD_W), F32), pltpu.HBM((HEADS, T, 1), F32)],
        compiler_params=pltpu.CompilerParams(dimension_semantics=("parallel",), vmem_limit_bytes=_vmem(nbytes)),
    )(_hbm(q), _hbm(k), _hbm(v))


def _attn_bwd(q, k, v, o, lse, dcat, deps=()):
    T = q.shape[0]
    BQ = _attn_block(T)
    nq = T // BQ
    deps = _deps(deps)

    def kern(q_ref, k_ref, v_ref, o_ref, lse_ref, do_ref, *rest):
        dq_ref, dk_ref, dv_ref, dd_ref = rest[len(deps):]
        dq_ref[...] = jnp.zeros(dq_ref.shape, F32)

        def dloop(i, _):
            rows = pl.ds(pl.multiple_of(i * BQ, BQ), BQ)
            dd_ref[rows, :] = jnp.sum(do_ref[rows, :].astype(F32) * o_ref[rows, :], axis=-1, keepdims=True)
            return 0

        lax.fori_loop(0, nq, dloop, 0)

        def tile(q0, k0, n, carry, masked):
            dk_acc, dv_acc = carry
            rq = pl.ds(pl.multiple_of(q0, n), n)
            rk = pl.ds(pl.multiple_of(k0, n), n)
            qb, kb, vb, dob = q_ref[rq, :], k_ref[rk, :], v_ref[rk, :], do_ref[rq, :]
            s = _dot(qb, kb, NT) * MLA_SCALE
            p = jnp.exp(s - lse_ref[0, rq, :])
            if masked:
                row = lax.broadcasted_iota(jnp.int32, s.shape, 0)
                col = lax.broadcasted_iota(jnp.int32, s.shape, 1)
                p = jnp.where(col <= row, p, 0.0)
            dp = _dot(dob, vb, NT)
            ds = (p * (dp - dd_ref[rq, :]) * MLA_SCALE).astype(BF16)
            dv_acc = dv_acc + _dot(p.astype(BF16), dob, TN_)
            dk_acc = dk_acc + _dot(ds, qb, TN_)
            dq_ref[rq, :] += _dot(ds, kb, NN)
            return dk_acc, dv_acc

        def kloop(j, _):
            base, half = j * BQ, BQ // 2
            zero = (jnp.zeros((half, HEAD_W), F32), jnp.zeros((half, HEAD_W), F32))
            early = tile(base + half, base, half, tile(base, base, half, zero, True), False)
            late = tile(base + half, base + half, half, zero, True)
            carry = tuple(jnp.concatenate([a, b], axis=0) for a, b in zip(early, late))
            dk_acc, dv_acc = lax.fori_loop(j + 1, nq, lambda i, c: tile(i * BQ, base, BQ, c, False), carry)
            rk = pl.ds(pl.multiple_of(j * BQ, BQ), BQ)
            dk_ref[rk, :] = dk_acc
            dv_ref[rk, :] = dv_acc
            return 0

        lax.fori_loop(0, nq, kloop, 0)

    head = pl.BlockSpec((T, HEAD_W), lambda h: (0, h))
    nbytes = 4 * _nbytes((T, HEAD_W), BF16) + 5 * _nbytes((T, HEAD_W), F32) + 2 * _nbytes((T, 128), F32)
    return pl.pallas_call(
        kern, name="attn_bwd", grid=(HEADS,),
        in_specs=[head, head, head, head, pl.BlockSpec((1, T, 1), lambda h: (h, 0, 0)), head] + [ANY_SPEC] * len(deps),
        out_specs=[head, head, head],
        out_shape=[pltpu.HBM((T, HEADS * HEAD_W), F32)] * 3,
        scratch_shapes=[pltpu.VMEM((T, 1), F32)],
        compiler_params=pltpu.CompilerParams(dimension_semantics=("parallel",), vmem_limit_bytes=_vmem(nbytes)),
    )(*[_hbm(a) for a in (q, k, v, o, lse, dcat)], *deps)


def _sgu_common(u, v, ln_g, ln_b):
    ua, tu = _gelu(u)
    va, tv = _gelu(v)
    vh, r = _ln_stats(va)
    return ua, tu, tv, vh, r, vh * ln_g + ln_b


def _tril_mask(n):
    return lax.broadcasted_iota(jnp.int32, (n, n), 1) <= lax.broadcasted_iota(jnp.int32, (n, n), 0)


def _sgu_fwd(zs, ln_g, ln_b, w, bias_full):
    def body(rows, consts, outs, accs):
        ua, _, _, _, _, vn = _sgu_common(rows[0][...], rows[1][...], consts[0][...], consts[1][...])
        vn = vn.astype(BF16)
        tri = _tril_mask(SGU_CHUNK)
        for g in range(SGU_G):
            wg = jnp.where(tri, consts[2][0, g], 0.0).astype(BF16)
            cols = slice(g * 128, (g + 1) * 128)
            for c in range(ua.shape[0] // SGU_CHUNK):
                rws = slice(c * SGU_CHUNK, (c + 1) * SGU_CHUNK)
                mixed = _dot(wg, vn[rws, cols], NN) + consts[3][:, cols]
                outs[0][rws, cols] = (ua[rws, cols] * mixed).astype(BF16)

    return _rowwise("sgu_fwd", body, [(zs, 512, 0), (zs, 512, 1)], [ln_g, ln_b, w, bias_full], [(SGU_DIM, BF16)])


def _sgu_bwd(zs, dcat, ln_g, ln_b, w, bias_full):
    def body(rows, consts, outs, accs):
        u, v = rows[0][...], rows[1][...]
        ua, tu, tv, vh, r, vn = _sgu_common(u, v, consts[0][...], consts[1][...])
        dout = rows[2][...].astype(F32)
        vn_bf = vn.astype(BF16)
        tri = _tril_mask(SGU_CHUNK)
        dmixed = (dout * ua)
        dmixed_bf = dmixed.astype(BF16)
        ones = jnp.ones((8, SGU_CHUNK), F32)
        dvn_cols, mixed_cols = [], []
        for g in range(SGU_G):
            wg = jnp.where(tri, consts[2][0, g], 0.0).astype(BF16)
            cols = slice(g * 128, (g + 1) * 128)
            dvn_rows, mixed_rows = [], []
            dw = jnp.zeros((SGU_CHUNK, SGU_CHUNK), F32)
            dmix_sum = jnp.zeros((SGU_CHUNK, 128), F32)
            for c in range(u.shape[0] // SGU_CHUNK):
                rws = slice(c * SGU_CHUNK, (c + 1) * SGU_CHUNK)
                mixed_rows.append(_dot(wg, vn_bf[rws, cols], NN) + consts[3][:, cols])
                dvn_rows.append(_dot(wg, dmixed_bf[rws, cols], TN_))
                dw = dw + _dot(dmixed_bf[rws, cols], vn_bf[rws, cols], NT)
                dmix_sum = dmix_sum + dmixed[rws, cols]
            accs[0][g] += jnp.where(tri, dw, 0.0)
            accs[3][g:g + 1, :] += _dot(ones, dmix_sum, NT, precision=HIGHEST)[0:1, :]
            dvn_cols.append(jnp.concatenate(dvn_rows, axis=0))
            mixed_cols.append(jnp.concatenate(mixed_rows, axis=0))
        dvn = jnp.concatenate(dvn_cols, axis=1)
        mixed = jnp.concatenate(mixed_cols, axis=1)
        accs[1][...] += jnp.sum(dvn * vh, axis=0, keepdims=True)
        accs[2][...] += jnp.sum(dvn, axis=0, keepdims=True)
        dvh = dvn * consts[0][...]
        dva = r * (dvh - jnp.mean(dvh, axis=-1, keepdims=True) - vh * jnp.mean(dvh * vh, axis=-1, keepdims=True))
        outs[0][:, 0:512] = (dout * mixed * _gelu_grad(u, tu)).astype(BF16)
        outs[0][:, 512:1024] = (dva * _gelu_grad(v, tv)).astype(BF16)

    return _rowwise("sgu_bwd", body, [(zs, 512, 0), (zs, 512, 1), (dcat, 512, 2)], [ln_g, ln_b, w, bias_full], [(1024, BF16)],
                    [((SGU_G, 128, 128), F32), ((1, SGU_DIM), F32), ((1, SGU_DIM), F32), ((SGU_G, 128), F32)], tr=256)


def _lower_bound(hg_lb):
    a0, a1 = hg_lb[0:1, :], hg_lb[1:2, :]
    m = jnp.maximum(a0, a1)
    e0, e1 = jnp.exp(a0 - m), jnp.exp(a1 - m)
    s0, s1 = e0 / (e0 + e1), e1 / (e0 + e1)
    return (s0 + s1) - s0, s0, s1


def _prefix_rows(x, reverse=False):
    n = x.shape[0]
    row = lax.broadcasted_iota(jnp.int32, x.shape, 0)
    s = 1
    while s < n:
        if reverse:
            x = x + jnp.where(row < n - s, pltpu.roll(x, n - s, 0), 0.0)
        else:
            x = x + jnp.where(row >= s, pltpu.roll(x, s, 0), 0.0)
        s *= 2
    return x


def _hg_gates(qr, fr, lb):
    C = qr.shape[0]
    sq = _sig(qr)
    qf = qr * sq
    sf = _sig(fr)
    gate = lb + (1.0 - lb) * sf
    kk = 1.0 - gate
    tri = _tril_mask(C)
    b = _prefix_rows(jnp.log(gate))
    bref = b[C // 2 - 1:C // 2, :]
    bl = b[C - 1:C, :]
    e_b = jnp.exp(b)
    e_q = jnp.exp(b - bref)
    e_k = jnp.exp(bref - b)
    e_lb = jnp.exp(bl - b)
    return dict(sq=sq, qf=qf, sf=sf, gate=gate, kk=kk, tri=tri, bl=bl, e_b=e_b, e_q=e_q, e_k=e_k, e_lb=e_lb)


def _hgrn_fwd(z1, hg_lb, gnorm):
    T = z1.shape[0]
    C = min(HG_CHUNK, T)
    nc = T // C
    ns = HG_CHUNKS_PER_STEP if nc % HG_CHUNKS_PER_STEP == 0 else 1
    R = ns * C

    def kern(q_ref, f_ref, i_ref, g_ref, lb_ref, gn_ref, o_ref, hg_ref, st_ref, s_scr):
        @pl.when(pl.program_id(0) == 0)
        def _():
            s_scr[...] = jnp.zeros(s_scr.shape, F32)

        lb_all, _, _ = _lower_bound(lb_ref[...])
        for sub in range(ns):
            rows = slice(sub * C, (sub + 1) * C)
            st_ref[sub] = s_scr[...]
            for h in range(HEADS):
                cols = slice(h * HEAD_W, (h + 1) * HEAD_W)
                t = _hg_gates(q_ref[rows, cols], f_ref[rows, cols], lb_all[:, cols])
                v_bf = i_ref[rows, cols].astype(BF16)
                st = s_scr[h]
                a = jnp.where(t["tri"], _dot((t["qf"] * t["e_q"]).astype(BF16), (t["kk"] * t["e_k"]).astype(BF16), NT), 0.0)
                o = _dot(a.astype(BF16), v_bf, NN) + _dot((t["qf"] * t["e_b"]).astype(BF16), st.astype(BF16), NT)
                s_scr[h] = st * jnp.exp(t["bl"]) + _dot(v_bf, (t["kk"] * t["e_lb"]).astype(BF16), TN_)
                o_ref[rows, cols] = o
                gr = g_ref[rows, cols]
                r = lax.rsqrt(jnp.mean(o * o, axis=-1, keepdims=True) + EPS)
                hg_ref[rows, cols] = (o * r * gn_ref[:, cols] * (gr * _sig(gr))).astype(BF16)

    seg = lambda k: pl.BlockSpec((R, D_MODEL), functools.partial(lambda n, k: (n, k), k=k))
    row = pl.BlockSpec((R, D_MODEL), lambda n: (n, 0))
    nbytes = 6 * _nbytes((R, D_MODEL), F32) + (2 + ns) * _nbytes((HEADS, 128, 128), F32)
    return pl.pallas_call(
        kern, name="hgrn_fwd", grid=(nc // ns,),
        in_specs=[seg(0), seg(1), seg(2), seg(3), pl.BlockSpec((2, D_MODEL), lambda n: (0, 0)),
                  pl.BlockSpec((1, D_MODEL), lambda n: (0, 0))],
        out_specs=[row, row, pl.BlockSpec((ns, HEADS, 128, 128), lambda n: (n, 0, 0, 0))],
        out_shape=[pltpu.HBM((T, D_MODEL), F32), pltpu.HBM((T, D_MODEL), BF16),
                   pltpu.HBM((nc, HEADS, 128, 128), F32)],
        scratch_shapes=[pltpu.VMEM((HEADS, 128, 128), F32)],
        compiler_params=pltpu.CompilerParams(dimension_semantics=("arbitrary",), vmem_limit_bytes=_vmem(nbytes)),
    )(*[_hbm(a) for a in (z1, z1, z1, z1, hg_lb, gnorm)])


def _hgrn_bwd(z1, o_pre, dhg, states, hg_lb, gnorm):
    T = z1.shape[0]
    C = min(HG_CHUNK, T)
    nc = T // C
    ns = HG_CHUNKS_PER_STEP if nc % HG_CHUNKS_PER_STEP == 0 else 1
    R, steps = ns * C, nc // ns

    def kern(q_ref, f_ref, i_ref, g_ref, o_ref, dhg_ref, st_ref, lb_ref, gn_ref, dz_ref, dlb_ref, dgn_ref, ds_scr, dlb_scr):
        n = pl.program_id(0)

        @pl.when(n == 0)
        def _():
            ds_scr[...] = jnp.zeros(ds_scr.shape, F32)
            dlb_scr[...] = jnp.zeros(dlb_scr.shape, F32)
            dgn_ref[...] = jnp.zeros(dgn_ref.shape, F32)

        lb_all, s0, s1 = _lower_bound(lb_ref[...])
        for sub in reversed(range(ns)):
            rows = slice(sub * C, (sub + 1) * C)
            for h in range(HEADS):
                cols = slice(h * HEAD_W, (h + 1) * HEAD_W)
                lb = lb_all[:, cols]
                qr, fr = q_ref[rows, cols], f_ref[rows, cols]
                t = _hg_gates(qr, fr, lb)
                tri = t["tri"]
                v_bf = i_ref[rows, cols].astype(BF16)
                st_bf = st_ref[sub, h].astype(BF16)
                dst = ds_scr[h]
                dst_bf = dst.astype(BF16)
                o = o_ref[rows, cols]
                gr = g_ref[rows, cols]
                sg = _sig(gr)
                sil = gr * sg
                gn = gn_ref[:, cols]
                r = lax.rsqrt(jnp.mean(o * o, axis=-1, keepdims=True) + EPS)
                on = o * r
                dh = dhg_ref[rows, cols].astype(F32)
                dgn_ref[:, cols] += jnp.sum(dh * on * sil, axis=0, keepdims=True)
                dg = dh * on * gn * (sg * (1.0 + gr * (1.0 - sg)))
                don = dh * gn * sil
                do_bf = (r * (don - on * jnp.mean(don * on, axis=-1, keepdims=True))).astype(BF16)
                qe = (t["qf"] * t["e_q"]).astype(BF16)
                ke = (t["kk"] * t["e_k"]).astype(BF16)
                qb = (t["qf"] * t["e_b"]).astype(BF16)
                kh_bf = (t["kk"] * t["e_lb"]).astype(BF16)
                a_bf = jnp.where(tri, _dot(qe, ke, NT), 0.0).astype(BF16)
                da_bf = jnp.where(tri, _dot(do_bf, v_bf, NT), 0.0).astype(BF16)
                dv = _dot(a_bf, do_bf, TN_) + _dot(kh_bf, dst_bf, NT)
                dqe = _dot(da_bf, ke, NN)
                dqb = _dot(do_bf, st_bf, NN)
                dke = _dot(da_bf, qe, TN_)
                dkh = _dot(v_bf, dst_bf, NN)
                dqf = dqe * t["e_q"] + dqb * t["e_b"]
                dkk = dke * t["e_k"] + dkh * t["e_lb"]
                kh_r = kh_bf.astype(F32)
                db = qe.astype(F32) * dqe - ke.astype(F32) * dke + qb.astype(F32) * dqb - kh_r * dkh
                e_bl = jnp.exp(t["bl"])
                dbl = jnp.sum(dkh * kh_r, axis=0, keepdims=True) + e_bl * jnp.sum(st_ref[sub, h] * dst, axis=0, keepdims=True)
                dlg = _prefix_rows(db, reverse=True) + dbl
                ds_scr[h] = dst * e_bl + _dot(do_bf, qb, TN_)
                dgate = dlg / t["gate"] - dkk
                sf = t["sf"]
                dlb_scr[:, cols] += jnp.sum(dgate * (1.0 - sf), axis=0, keepdims=True)
                df = dgate * (1.0 - lb) * sf * (1.0 - sf)
                dq = dqf * (t["sq"] * (1.0 + qr * (1.0 - t["sq"])))
                dz_ref[rows, cols] = dq.astype(BF16)
                dz_ref[rows, D_MODEL + h * HEAD_W:D_MODEL + (h + 1) * HEAD_W] = df.astype(BF16)
                dz_ref[rows, 2 * D_MODEL + h * HEAD_W:2 * D_MODEL + (h + 1) * HEAD_W] = dv.astype(BF16)
                dz_ref[rows, 3 * D_MODEL + h * HEAD_W:3 * D_MODEL + (h + 1) * HEAD_W] = dg.astype(BF16)

        @pl.when(n == steps - 1)
        def _():
            d = s0 * s1 * dlb_scr[...]
            dlb_ref[0:1, :] = -d
            dlb_ref[1:2, :] = d

    seg = lambda k: pl.BlockSpec((R, D_MODEL), functools.partial(lambda n, k: (steps - 1 - n, k), k=k))
    nbytes = 6 * _nbytes((R, D_MODEL), F32) + _nbytes((R, 4 * D_MODEL), BF16) + (2 + ns) * _nbytes((HEADS, 128, 128), F32)
    return pl.pallas_call(
        kern, name="hgrn_bwd", grid=(steps,),
        in_specs=[seg(0), seg(1), seg(2), seg(3), seg(0), seg(0),
                  pl.BlockSpec((ns, HEADS, 128, 128), lambda n: (steps - 1 - n, 0, 0, 0)),
                  pl.BlockSpec((2, D_MODEL), lambda n: (0, 0)), pl.BlockSpec((1, D_MODEL), lambda n: (0, 0))],
        out_specs=[pl.BlockSpec((R, 4 * D_MODEL), lambda n: (steps - 1 - n, 0)),
                   pl.BlockSpec((2, D_MODEL), lambda n: (0, 0)), pl.BlockSpec((1, D_MODEL), lambda n: (0, 0))],
        out_shape=[pltpu.HBM((T, 4 * D_MODEL), BF16), pltpu.HBM((2, D_MODEL), F32),
                   pltpu.HBM((1, D_MODEL), F32)],
        scratch_shapes=[pltpu.VMEM((HEADS, 128, 128), F32), pltpu.VMEM((1, D_MODEL), F32)],
        compiler_params=pltpu.CompilerParams(dimension_semantics=("arbitrary",), vmem_limit_bytes=_vmem(nbytes)),
    )(*[_hbm(a) for a in (z1, z1, z1, z1, o_pre, dhg, states, hg_lb, gnorm)])


def _prep_weights(gw):
    w_in_e = gw["w_in_e"].transpose(1, 0, 2).reshape(D_MODEL, 1568)
    kr = jnp.pad(w_in_e[:, 512:544], ((0, 0), (64, 32)))
    wm = jnp.concatenate([w_in_e[:, 0:512], kr], axis=1)
    ws = w_in_e[:, 544:1568]
    w_qb = gw["w_qb"].transpose(1, 0, 2).reshape(MLA_LORA, HEADS, 96)
    wq = jnp.pad(w_qb, ((0, 0), (0, 0), (0, 32))).reshape(MLA_LORA, HEADS * HEAD_W)
    kvb = gw["w_kvb"].transpose(1, 0, 2).reshape(MLA_LORA, HEADS, 128)
    wk = jnp.pad(kvb[:, :, :64], ((0, 0), (0, 0), (0, 64))).reshape(MLA_LORA, HEADS * HEAD_W)
    wv = jnp.pad(kvb[:, :, 64:], ((0, 0), (0, 0), (0, 64))).reshape(MLA_LORA, HEADS * HEAD_W)
    w_out_e = gw["w_out_e"].reshape(D_MODEL, D_MODEL)
    woa = jnp.pad(w_out_e[:512].reshape(HEADS, 64, D_MODEL), ((0, 0), (0, 64), (0, 0))).reshape(HEADS * HEAD_W, D_MODEL)
    return dict(wm=wm, ws=ws, wq=wq, wk=wk, wv=wv, woa=woa, wob=w_out_e[512:])


def _unprep_grads(g):
    dwm, dws = g["wm"], g["ws"]
    d_in_e = jnp.concatenate([dwm[:, 0:512], dwm[:, 512 + 64:512 + 96], dws], axis=1)
    d_qb = g["wq"].reshape(MLA_LORA, HEADS, HEAD_W)[:, :, :96].reshape(MLA_LORA, HEADS * 96)
    dk = g["wk"].reshape(MLA_LORA, HEADS, HEAD_W)[:, :, :64]
    dv = g["wv"].reshape(MLA_LORA, HEADS, HEAD_W)[:, :, :64]
    d_kvb = jnp.concatenate([dk, dv], axis=2).reshape(MLA_LORA, HEADS * 128)
    d_oa = g["woa"].reshape(HEADS, HEAD_W, D_MODEL)[:, :64].reshape(HEADS * 64, D_MODEL)
    dev_major = lambda a: a.reshape(a.shape[0], N_DEV, a.shape[1] // N_DEV).transpose(1, 0, 2)
    return dict(w_in_e=dev_major(d_in_e), w_qb=dev_major(d_qb), w_kvb=dev_major(d_kvb),
                w_out_e=jnp.concatenate([d_oa, g["wob"]], axis=0).reshape(N_DEV, D_MODEL // N_DEV, D_MODEL))


def _local_step(x, positions, target, gw, sp, ex):
    w = _prep_weights(gw)
    T = x.shape[0]
    tm = min(TM, T)
    nt = T // tm
    half = MLA_ROPE // 2
    inv_freq = ROPE_BASE ** (-jnp.arange(half, dtype=F32) / half)
    invf_lane = jnp.concatenate([jnp.zeros((64,), F32), inv_freq, inv_freq, jnp.zeros((32,), F32)]).reshape(1, HEAD_W)
    tabs = _rope_tables(positions.reshape(T, 1), invf_lane)
    bias_full = jnp.repeat(sp["sgu_b"][0].T, 128, axis=1)
    sgu_w = sp["sgu_w"]
    gq, gkv = sp["mla_gq"], sp["mla_gkv"]
    ln1_g, ln1_b, ln2_g, ln2_b = sp["ln1_g"], sp["ln1_b"], sp["ln2_g"], sp["ln2_b"]
    zm, zs, cqn, ckvn, kr_rot = _mla_in(x, w["wm"], w["ws"], tabs, gq, gkv, deps=[ex.first_token])
    q, k, v = _mla_qkv(cqn, ckvn, kr_rot, tabs, w["wq"], w["wk"], w["wv"])
    o_att, lse = _attn_fwd(q, k, v)
    b_out = _sgu_fwd(zs, sp["sgu_ln_g"], sp["sgu_ln_b"], sgu_w, bias_full)
    token = ex.weights_forward(after=[o_att, b_out])
    y1, h1, h1_bf = _proj_ln("l0_out_ln1", [o_att, b_out], [w["woa"], w["wob"]], x, ln1_g, ln1_b, 0, deps=[token])
    big = ex.weights_ready(after=[y1])
    w_ff1, w_in_o, w_out_o = big["w_ff1"], big["w_in_o"], big["w_out_o"].reshape(D_MODEL, D_MODEL)
    w_ff2 = [a.reshape(D_FF, D_MODEL) for a in big["w_ff2"]]
    a0, act0 = _mlp_up("l0", h1_bf, w_ff1[0])
    y2, h2, h2_bf = _proj_ln("l0_ff2_ln2", [act0], [w_ff2[0]], h1, ln2_g, ln2_b, 0)

    z1 = _tiled("l1_in", (1, nt), [_rb(h2_bf, tm), _res(w_in_o)], [_out(T, 4 * D_MODEL, F32, tm, 4 * D_MODEL)],
                _mmc_blocks(N_DEV, NN, lambda w, d: w[d]), direct=True)
    o_pre, hg, states = _hgrn_fwd(z1, sp["hg_lb"], sp["hg_gnorm"])
    y3, h3, h3_bf = _proj_ln("l1_out_ln1", [hg], [w_out_o], h2, ln1_g, ln1_b, 1)
    a1, act1 = _mlp_up("l1", h3_bf, w_ff1[1])

    gs, g0 = {}, {}
    dy4, dy4_bf, sq_err, gs["ln2_g1"], gs["ln2_b1"] = _proj_ln_loss("l1_ff2_loss", act1, w_ff2[1], h3, ln2_g, ln2_b, 1, target)
    gs["sq_err"] = sq_err
    da1, dw1_1, dw2_1 = _mlp_bwd_w("l1", h3_bf, a1, act1, dy4_bf, big["w_ff2"][1])
    dy3, dy3_bf, dhg, gs["ln1_g1"], gs["ln1_b1"] = _dh_ln_back("l1_dh_ln1", da1, w_ff1[1], dy4, y3, ln1_g, 1, proj=[w_out_o])
    d_out_o = _tiled("l1_dwout", (2, D_MODEL // TM), [_tl(hg, TM), _cw(dy3_bf, TN)],
                     [_out(D_MODEL, D_MODEL, F32, TM, TN), _out(D_MODEL, D_MODEL, BF16, TM, TN)], _mmc(TN_, epilogue=_twice))
    d_out_o = [a.reshape(N_DEV, D_MODEL // N_DEV, D_MODEL) for a in d_out_o]
    dz1, gs["hg_lb"], gs["hg_gnorm"] = _hgrn_bwd(z1, o_pre, dhg, states, sp["hg_lb"], sp["hg_gnorm"])
    d_in_o = _tiled("l1_dwin", (N_DEV, 1), [_res(h2_bf), _cw(dz1, TN)],
                    [_out_dev(D_MODEL, TN, D_MODEL), _out_dev(D_MODEL, TN, D_MODEL, BF16)], _mmc(TN_, epilogue=_twice))
    token = ex.direct_start("l1", [dw1_1, dw2_1, d_in_o, d_out_o])

    dy2, dy2_bf, gs["ln2_g0"], gs["ln2_b0"] = _dh_ln_back("l1_dh_ln2", dz1, w_in_o, dy3, y2, ln2_g, 0, deps=[token])
    da0, dw1_0, dw2_0 = _mlp_bwd_w("l0", h1_bf, a0, act0, dy2_bf, big["w_ff2"][0])
    token = ex.direct_start("l0m", [dw1_0, dw2_0])
    dy1, dy1_bf, dcat, gs["ln1_g0"], gs["ln1_b0"] = _dh_ln_back("l0_dh_ln1", da0, w_ff1[0], dy2, y1, ln1_g, 0,
                                                                 proj=[w["woa"], w["wob"]], deps=[token])
    g0["woa"], g0["wob"] = _out_weight_grads(o_att, b_out, dy1_bf)
    dzs, gs["sgu_w"], gs["sgu_ln_g"], gs["sgu_ln_b"], gs["sgu_b"] = _sgu_bwd(zs, dcat, sp["sgu_ln_g"], sp["sgu_ln_b"], sgu_w, bias_full)
    dq, dk, dv = _attn_bwd(q, k, v, o_att, lse, dcat)
    dzm, g0["wq"], g0["wk"], g0["wv"], gs["mla_gq"], gs["mla_gkv"] = _mla_back(zm, cqn, ckvn, tabs, gq, gkv, w["wq"], w["wk"], w["wv"],
                                                                                 dq, dk, dv)
    token = ex.small_start(gs)
    dx, g0["wm"], g0["ws"] = _in_back(x, dzm, dzs, dy1, w["wm"], w["ws"], deps=[token])

    return sq_err, dx, _unprep_grads(g0), gs


def _me():
    return lax.axis_index("x"), lax.axis_index("y"), lax.axis_index("c")


ANY_SPEC = pl.BlockSpec(memory_space=pl.ANY)
HBM_SPEC = pl.BlockSpec(memory_space=pltpu.HBM)
SEM_SPEC = pl.BlockSpec(memory_space=pltpu.SEMAPHORE)
EFFECT = pltpu.SideEffectType.DATAFLOW_SIDE_EFFECTING


def _split_start(name, srcs, lands, n_sems, make_copies, after=()):
    n, m, k = len(srcs), len(lands), len(after)

    def body(*refs):
        for cp in make_copies(refs[:n], refs[n:n + m], refs[n + m + k], refs[n + m + k + 1]):
            cp.start()
        refs[-1][...] = jnp.zeros(refs[-1].shape, F32)

    out_shape = (pltpu.SemaphoreType.DMA((n_sems,)), pltpu.SemaphoreType.DMA((n_sems,)),
                 *[pltpu.HBM(a.shape, a.dtype) for a in (*srcs, *lands)], jax.ShapeDtypeStruct((8, 128), F32))
    res = pl.pallas_call(
        body, name=name, out_shape=out_shape, in_specs=[HBM_SPEC] * (n + m) + [ANY_SPEC] * k,
        out_specs=(SEM_SPEC, SEM_SPEC, *[HBM_SPEC] * (n + m), pl.BlockSpec(memory_space=pltpu.VMEM)),
        input_output_aliases={i: 2 + i for i in range(n + m)},
        compiler_params=pltpu.CompilerParams(has_side_effects=EFFECT),
    )(*[_hbm(a) for a in (*srcs, *lands)], *after)
    return res[0], res[1], list(res[2:2 + n]), list(res[2 + n:2 + n + m]), res[-1]


def _split_wait(name, send_sems, recv_sems, srcs, lands, after, make_copies):
    n, m = len(srcs), len(lands)

    def body(*refs):
        for cp in make_copies(refs[:n], refs[n:n + m], refs[n + m], refs[n + m + 1]):
            cp.wait_send()
            cp.wait_recv()

    res = pl.pallas_call(
        body, name=name, out_shape=tuple(pltpu.HBM(a.shape, a.dtype) for a in (*srcs, *lands)),
        in_specs=[HBM_SPEC] * (n + m) + [SEM_SPEC, SEM_SPEC] + [ANY_SPEC] * len(after), out_specs=tuple([HBM_SPEC] * (n + m)),
        input_output_aliases={i: i for i in range(n + m)},
        compiler_params=pltpu.CompilerParams(has_side_effects=EFFECT),
    )(*srcs, *lands, send_sems, recv_sems, *after)
    return list(res[:n]), list(res[n:])


def _place_own(shards, dev):
    n = len(shards)

    def kern(dev_ref, *refs):
        for x_ref, o_ref in zip(refs[:n], refs[n:]):
            o_ref[...] = x_ref[...].astype(o_ref.dtype)

    blocks = [(None, *a.shape[1:]) for a, _, _ in shards]
    nbytes = sum(_nbytes(b, a.dtype) + _nbytes(b, dt) for b, (a, _, dt) in zip(blocks, shards))
    return pl.pallas_call(
        kern, name="weights_place_own", out_shape=[pltpu.HBM((N_DEV, *a.shape[1:]), dt) for a, _, dt in shards],
        grid_spec=pltpu.PrefetchScalarGridSpec(
            num_scalar_prefetch=1, grid=(1,),
            in_specs=[pl.BlockSpec(b, functools.partial(lambda i, dev, l: (l, 0, 0), l=l)) for b, (_, l, _) in zip(blocks, shards)],
            out_specs=[pl.BlockSpec(b, lambda i, dev: (dev[0], 0, 0)) for b in blocks]),
        compiler_params=pltpu.CompilerParams(dimension_semantics=("arbitrary",), vmem_limit_bytes=_vmem(nbytes)),
    )(dev, *[_hbm(a) for a, _, _ in shards])


def _ag_first_copies(src_refs, out_refs, send_sems, recv_sems):
    x, y, c = _me()
    targets = [(x, y, 1 - c), (1 - x, y, c), (x, 1 - y, c), (1 - x, 1 - y, c)]
    return [pltpu.make_async_remote_copy(
        src_ref=out_refs[op].at[4 * x + 2 * y + c], dst_ref=out_refs[op].at[4 * x + 2 * y + c], send_sem=send_sems.at[4 * op + k],
        recv_sem=recv_sems.at[4 * op + k], device_id=to, device_id_type=MESH)
        for op in range(len(out_refs)) for k, to in enumerate(targets)]


def _ag_second_copies(src_refs, out_refs, send_sems, recv_sems):
    x, y, c = _me()
    chips = [(1 - x, y), (x, 1 - y), (1 - x, 1 - y)]
    return [pltpu.make_async_remote_copy(
        src_ref=out_refs[op].at[4 * cx + 2 * cy + c], dst_ref=out_refs[op].at[4 * cx + 2 * cy + c],
        send_sem=send_sems.at[3 * op + j], recv_sem=recv_sems.at[3 * op + j], device_id=(x, y, 1 - c), device_id_type=MESH)
        for op in range(len(out_refs)) for j, (cx, cy) in enumerate(chips)]


def _rs_sibling_copies(g_refs, out_refs, send_sems, recv_sems):
    x, y, c = _me()
    return [pltpu.make_async_remote_copy(
        src_ref=g_refs[op].at[k, 1 - c], dst_ref=out_refs[op].at[k], send_sem=send_sems.at[4 * op + k],
        recv_sem=recv_sems.at[4 * op + k], device_id=(x, y, 1 - c), device_id_type=MESH)
        for op in range(len(g_refs)) for k in range(4)]


def _rs_direct_copies(g_refs, land_refs, send_sems, recv_sems):
    x, y, c = _me()
    n = len(g_refs) // 2
    chips = [(1 - x, y), (x, 1 - y), (1 - x, 1 - y)]
    copies = []
    for op in range(n):
        g32, g16, from_sib, from_others = g_refs[op], g_refs[n + op], land_refs[op], land_refs[n + op]
        copies.append(pltpu.make_async_remote_copy(
            src_ref=g32.at[2 * x + y, 1 - c], dst_ref=from_sib, send_sem=send_sems.at[7 * op], recv_sem=recv_sems.at[7 * op],
            device_id=(x, y, 1 - c), device_id_type=MESH))
        for j, (cx, cy) in enumerate(chips):
            for s, cc in enumerate((c, 1 - c)):
                copies.append(pltpu.make_async_remote_copy(
                    src_ref=g16.at[2 * cx + cy, cc], dst_ref=from_others.at[2 * j + s], send_sem=send_sems.at[7 * op + 1 + 2 * j + s],
                    recv_sem=recv_sems.at[7 * op + 1 + 2 * j + s], device_id=(cx, cy, cc), device_id_type=MESH))
    return copies


def _rs_chip_copies(p_refs, out_refs, send_sems, recv_sems):
    x, y, c = _me()
    chips = [(1 - x, y), (x, 1 - y), (1 - x, 1 - y)]
    return [pltpu.make_async_remote_copy(
        src_ref=p_refs[op].at[2 * cx + cy], dst_ref=out_refs[op].at[j], send_sem=send_sems.at[3 * op + j],
        recv_sem=recv_sems.at[3 * op + j], device_id=(cx, cy, c), device_id_type=MESH)
        for op in range(len(p_refs)) for j, (cx, cy) in enumerate(chips)]


def _all_gather(placed):
    n = len(placed)

    def kern(*refs):
        in_refs, out_refs, (send_sems, recv_sems) = refs[:n], refs[n:2 * n], refs[2 * n:]
        x, y, c = _me()
        me, sibling = (x, y, c), (x, y, 1 - c)
        chips = [(1 - x, y), (x, 1 - y), (1 - x, 1 - y)]

        def copy(op, k, block, to, own=False):
            idx = 4 * block[0] + 2 * block[1] + block[2]
            return pltpu.make_async_remote_copy(
                src_ref=(in_refs if own else out_refs)[op].at[idx], dst_ref=out_refs[op].at[idx], send_sem=send_sems.at[7 * op + k],
                recv_sem=recv_sems.at[7 * op + k], device_id=to, device_id_type=MESH)

        first = []
        for op in range(n):
            first.append(copy(op, 0, me, sibling, own=True))
            first += [copy(op, 1 + j, me, (*chip, c), own=True) for j, chip in enumerate(chips)]
        for cp in first:
            cp.start()
        passed = []
        for j, chip in enumerate(chips):
            for op in range(n):
                copy(op, 1 + j, (*chip, c), me).wait_recv()
                passed.append(copy(op, 4 + j, (*chip, c), sibling))
                passed[-1].start()
        for op in range(n):
            copy(op, 0, sibling, me).wait_recv()
            for j, chip in enumerate(chips):
                copy(op, 4 + j, (*chip, 1 - c), me).wait_recv()
        for cp in first + passed:
            cp.wait_send()

    return pl.pallas_call(
        kern, name="weights_all_gather", out_shape=[pltpu.HBM(g.shape, g.dtype) for g in placed],
        in_specs=[ANY_SPEC] * n, out_specs=[ANY_SPEC] * n, input_output_aliases={i: i for i in range(n)},
        scratch_shapes=[pltpu.SemaphoreType.DMA((7 * n,)), pltpu.SemaphoreType.DMA((7 * n,))],
    )(*[_hbm(a) for a in placed])


def _row_tile(r, w, n_blocks):
    tr = r
    while tr > 8 and 2 * n_blocks * tr * w * 4 > 24 * 2**20:
        tr //= 2
    return tr


def _chip_sum(name, g, from_sibling, core):
    _, _, R, W = g.shape
    tr = _row_tile(R, W, 3)

    def kern(core_ref, g_ref, s_ref, o_ref):
        o_ref[...] = (g_ref[...] + s_ref[...]).astype(BF16)

    return pl.pallas_call(
        kern, name=name, out_shape=pltpu.HBM((4, R, W), BF16),
        grid_spec=pltpu.PrefetchScalarGridSpec(
            num_scalar_prefetch=1, grid=(4, R // tr),
            in_specs=[pl.BlockSpec((None, None, tr, W), lambda k, i, core: (k, core[0], i, 0)),
                      pl.BlockSpec((None, tr, W), lambda k, i, core: (k, i, 0))],
            out_specs=pl.BlockSpec((None, tr, W), lambda k, i, core: (k, i, 0))),
        compiler_params=pltpu.CompilerParams(dimension_semantics=("parallel", "parallel"), vmem_limit_bytes=_vmem(3 * tr * W * 4)),
    )(core, _hbm(g), _hbm(from_sibling))


def _adamw(w, g, m, v):
    m = ADAM_B1 * m + (1.0 - ADAM_B1) * g
    v = ADAM_B2 * v + (1.0 - ADAM_B2) * (g * g)
    m_hat = m / (1.0 - ADAM_B1 ** ADAM_STEP)
    v_hat = v / (1.0 - ADAM_B2 ** ADAM_STEP)
    return -ADAM_LR * (m_hat / (jnp.sqrt(v_hat) + ADAM_EPS) + ADAM_WD * w), m, v


def _finish_sharded(name, layers, w, m, v, where, deps=()):
    nl, R, W = w.shape
    n_other = layers[0][2].shape[0]
    tr = _row_tile(R, W, (8 + n_other) * nl)
    deps = _deps(deps)

    def kern(where_ref, *refs):
        w_ref, m_ref, v_ref = refs[3 * nl:3 * nl + 3]
        go_ref, d_ref, mo_ref, vo_ref = refs[3 * nl + 3 + len(deps):]
        for l in range(nl):
            g_ref, s_ref, c_ref = refs[3 * l:3 * l + 3]
            grad = g_ref[...] + s_ref[...]
            for j in range(n_other):
                grad = grad + c_ref[j].astype(F32)
            go_ref[l] = grad
            d_ref[l], mo_ref[l], vo_ref[l] = _adamw(w_ref[l], grad, m_ref[l], v_ref[l])

    row = pl.BlockSpec((nl, tr, W), lambda i, wh: (0, i, 0))
    in_specs, args = [], []
    for g, s, c in layers:
        sib = (pl.BlockSpec((None, tr, W), lambda i, wh: (wh[0], i, 0)) if s.ndim == 3 else pl.BlockSpec((tr, W), lambda i, wh: (i, 0)))
        in_specs += [pl.BlockSpec((None, None, tr, W), lambda i, wh: (wh[0], wh[1], i, 0)), sib,
                     pl.BlockSpec((n_other, tr, W), lambda i, wh: (0, i, 0))]
        args += [g, s, c]
    return pl.pallas_call(
        kern, name=name, out_shape=[pltpu.HBM((nl, R, W), F32)] * 4,
        grid_spec=pltpu.PrefetchScalarGridSpec(num_scalar_prefetch=1, grid=(R // tr,),
                                               in_specs=in_specs + [row, row, row] + [ANY_SPEC] * len(deps),
                                               out_specs=[row, row, row, row]),
        compiler_params=pltpu.CompilerParams(dimension_semantics=("parallel",),
                                             vmem_limit_bytes=_vmem(nl * (8 + n_other) * tr * W * 4)),
    )(where, *[_hbm(a) for a in (*args, w, m, v)], *deps)


SMALL_PLACE = (("mla_gq", 0, 0, 1, 256), ("mla_gkv", 0, 256, 1, 256), ("sgu_ln_g", 0, 512, 1, 512), ("sgu_ln_b", 1, 0, 1, 512),
               ("hg_lb", 2, 0, 2, 1024), ("ln1_g", 4, 0, 2, 1024), ("ln1_b", 6, 0, 2, 1024), ("sgu_b", 8, 0, 4, 128),
               ("ln2_g", 12, 0, 2, 1024), ("ln2_b", 14, 0, 2, 1024), ("hg_gnorm", 16, 0, 1, 1024))
SMALL_BUF_ROWS = 24
LOSS_ROW = 17


def _small_pack(gs, dev):
    pieces = [(gs["mla_gq"], 0, 0), (gs["mla_gkv"], 0, 256), (gs["sgu_ln_g"], 0, 512), (gs["sgu_ln_b"], 1, 0), (gs["hg_lb"], 2, 0),
              (gs["ln1_g0"], 4, 0), (gs["ln1_g1"], 5, 0), (gs["ln1_b0"], 6, 0), (gs["ln1_b1"], 7, 0), (gs["sgu_b"], 8, 0),
              (gs["ln2_g0"], 12, 0), (gs["ln2_g1"], 13, 0), (gs["ln2_b0"], 14, 0), (gs["ln2_b1"], 15, 0), (gs["hg_gnorm"], 16, 0),
              (gs["sq_err"], LOSS_ROW, 0)]
    n_p = len(pieces)

    def kern(dev_ref, *refs):
        a_ref, b_ref = refs[n_p + 1], refs[n_p + 2]
        a_ref[...] = jnp.zeros(a_ref.shape, F32)
        for ref, (_, r, l0) in zip(refs[:n_p], pieces):
            a_ref[r:r + ref.shape[0], l0:l0 + ref.shape[1]] = ref[...]
        b_ref[...] = refs[n_p][...]

    whole = lambda a: pl.BlockSpec(a.shape, functools.partial(lambda i, dev, nd: (0,) * nd, nd=a.ndim))
    return pl.pallas_call(
        kern, name="small_grads_pack",
        out_shape=[pltpu.HBM((N_DEV, SMALL_BUF_ROWS, D_MODEL), F32), pltpu.HBM((N_DEV, SGU_G, 128, 128), F32)],
        grid_spec=pltpu.PrefetchScalarGridSpec(
            num_scalar_prefetch=1, grid=(1,), in_specs=[whole(p[0]) for p in pieces] + [whole(gs["sgu_w"])],
            out_specs=[pl.BlockSpec((None, SMALL_BUF_ROWS, D_MODEL), lambda i, dev: (dev[0], 0, 0)),
                       pl.BlockSpec((None, SGU_G, 128, 128), lambda i, dev: (dev[0], 0, 0, 0))]),
    )(dev, *[p[0] for p in pieces], gs["sgu_w"])


def _small_copies(src_refs, land_refs, send_sems, recv_sems):
    px, py, pc = _me()
    me = 4 * px + 2 * py + pc
    return [pltpu.make_async_remote_copy(
        src_ref=land_refs[k].at[me], dst_ref=land_refs[k].at[me], send_sem=send_sems.at[2 * (r - 1) + k],
        recv_sem=recv_sems.at[2 * (r - 1) + k], device_id=(px ^ (r >> 2), py ^ ((r >> 1) & 1), pc ^ (r & 1)), device_id_type=MESH)
        for r in range(1, N_DEV) for k in range(2)]


def _small_adamw(slots_a, slots_b, given):
    names = [p[0] for p in SMALL_PLACE] + ["sgu_w"]
    n_names = len(names)
    wmv = [given[pre + name] for name in names for pre in ("", "m_", "v_")]
    vmem = pl.BlockSpec(memory_space=pltpu.VMEM)

    def kern(*refs):
        sum_a, sum_b = refs[0][0], refs[1][0]
        for d in range(1, N_DEV):
            sum_a, sum_b = sum_a + refs[0][d], sum_b + refs[1][d]
        wmv_refs, out_refs = refs[2:2 + 3 * n_names], refs[2 + 3 * n_names:]
        px, py, pc = _me()
        me = 4 * px + 2 * py + pc

        def own_block(full):
            acc = full[:, 0:128]
            for b in range(1, N_DEV):
                acc = jnp.where(me == b, full[:, b * 128:(b + 1) * 128], acc)
            return acc

        for idx, name in enumerate(names):
            w_ref, m_ref, v_ref = wmv_refs[3 * idx:3 * idx + 3]
            if name == "sgu_w":
                grad = sum_b[None]
            else:
                _, r, l0, nr, nl = SMALL_PLACE[idx]
                grad = sum_a[r:r + nr, l0:l0 + nl]
                if name == "hg_gnorm":
                    grad = own_block(grad)
                if name == "sgu_b":
                    grad = grad[None]
            res = (grad, *_adamw(w_ref[...], grad, m_ref[...], v_ref[...]))
            for o_ref, val in zip(out_refs[4 * idx:4 * idx + 4], res):
                o_ref[...] = val
        out_refs[4 * n_names][...] = (0.5 / D_MODEL) * jnp.sum(sum_a[LOSS_ROW:LOSS_ROW + 1, :], axis=1, keepdims=True)

    out_shape = [jax.ShapeDtypeStruct(given[name].shape, F32) for name in names for _ in range(4)]
    out_shape.append(jax.ShapeDtypeStruct((1, 1), F32))
    res = pl.pallas_call(
        kern, name="small_adamw", out_shape=out_shape, in_specs=[vmem] * (2 + len(wmv)), out_specs=[vmem] * len(out_shape),
    )(slots_a, slots_b, *wmv)
    out = {name: res[4 * idx:4 * idx + 4] for idx, name in enumerate(names)}
    out["loss"] = res[-1].reshape(())
    return out


class _Exchange:
    def __init__(self, given):
        self.given = given
        px, py, pc = _me()
        self.core = pc.reshape(1).astype(jnp.int32)
        self.dev = (4 * px + 2 * py + pc).reshape(1).astype(jnp.int32)
        self.where = jnp.stack([2 * px + py, pc]).astype(jnp.int32)
        self.state, self.layers = {}, {}

    def start_weights(self, lands, after):
        self.weights = _split_start("weights_first_start", [], lands, 4 * len(lands), _ag_first_copies, after=after)
        self.first_token = self.weights[4]

    def weights_forward(self, after):
        send_sems, recv_sems, shards, lands, _ = self.weights
        _, lands = _split_wait("weights_first_wait", send_sems, recv_sems, shards, lands, after, _ag_first_copies)
        self.weights = _split_start("weights_second_start", [], lands, 3 * len(lands), _ag_second_copies)
        return self.weights[4]

    def weights_ready(self, after):
        send_sems, recv_sems, shards, lands, _ = self.weights
        _, got = _split_wait("weights_second_wait", send_sems, recv_sems, shards, lands, after, _ag_second_copies)
        return dict(w_in_o=got[0], w_out_o=got[1], w_ff1=[got[2], got[3]], w_ff2=[got[4], got[5]])

    def small_start(self, gs):
        self.small = _split_start("small_grads_start", [], _small_pack(gs, self.dev), 14, _small_copies)
        return self.small[4]

    def small_finish(self, after):
        send_sems, recv_sems, _, lands, _ = self.small
        _, lands = _split_wait("small_grads_wait", send_sems, recv_sems, [], lands, after, _small_copies)
        return _small_adamw(lands[0], lands[1], self.given)

    def direct_start(self, tag, grads):
        f32 = [g[0].reshape(4, 2, *g[0].shape[1:]) for g in grads]
        bf16 = [g[1].reshape(4, 2, *g[1].shape[1:]) for g in grads]
        lands = [lax.empty(b.shape[2:], F32) for b in f32] + [lax.empty((6, *b.shape[2:]), BF16) for b in f32]
        self.state[tag] = _split_start(f"grads_{tag}_start", f32 + bf16, lands, 7 * len(grads), _rs_direct_copies)
        return self.state[tag][4]

    def direct_end(self, tag, after):
        send_sems, recv_sems, srcs, lands, _ = self.state[tag]
        srcs, lands = _split_wait(f"grads_{tag}_wait", send_sems, recv_sems, srcs, lands, after, _rs_direct_copies)
        n = len(lands) // 2
        self.layers[tag] = list(zip(srcs[:n], lands[:n], lands[n:]))

    def grads_start(self, tag, grads):
        blocks = [g.reshape(4, 2, *g.shape[1:]) for g in grads]
        lands = [lax.empty((4, *b.shape[2:]), F32) for b in blocks]
        self.state[tag] = _split_start(f"grads_{tag}_sibling_start", blocks, lands, 4 * len(blocks), _rs_sibling_copies)
        return self.state[tag][4]

    def grads_middle(self, tag, after):
        send_sems, recv_sems, blocks, lands, _ = self.state[tag]
        blocks, from_sibling = _split_wait(f"grads_{tag}_sibling_wait", send_sems, recv_sems, blocks, lands, [after], _rs_sibling_copies)
        sums = [_chip_sum(f"grads_{tag}_chip_sum_{k}", b, s, self.core) for k, (b, s) in enumerate(zip(blocks, from_sibling))]
        lands = [lax.empty((3, *p.shape[1:]), BF16) for p in sums]
        self.state[tag] = (blocks, from_sibling, _split_start(f"grads_{tag}_chips_start", sums, lands, 3 * len(sums), _rs_chip_copies))
        return self.state[tag][2][4]

    def grads_end(self, tag, after):
        blocks, from_sibling, (send_sems, recv_sems, sums, lands, _) = self.state[tag]
        after = list(after) if isinstance(after, (list, tuple)) else [after]
        _, from_chips = _split_wait(f"grads_{tag}_chips_wait", send_sems, recv_sems, sums, lands, after, _rs_chip_copies)
        self.layers[tag] = list(zip(blocks, from_sibling, from_chips))


def kernel(x, positions, w_in_e, mla_gq, mla_gkv, w_qb, w_kvb, sgu_ln_g, sgu_ln_b, sgu_w, sgu_b, w_out_e, w_in_o, hg_lb, hg_gnorm, w_out_o, ln1_g, ln1_b, w_ff1, w_ff2, ln2_g, ln2_b, loss_target, m_w_in_e, m_mla_gq, m_mla_gkv, m_w_qb, m_w_kvb, m_sgu_ln_g, m_sgu_ln_b, m_sgu_w, m_sgu_b, m_w_out_e, m_w_in_o, m_hg_lb, m_hg_gnorm, m_w_out_o, m_ln1_g, m_ln1_b, m_w_ff1, m_w_ff2, m_ln2_g, m_ln2_b, v_w_in_e, v_mla_gq, v_mla_gkv, v_w_qb, v_w_kvb, v_sgu_ln_g, v_sgu_ln_b, v_sgu_w, v_sgu_b, v_w_out_e, v_w_in_o, v_hg_lb, v_hg_gnorm, v_w_out_o, v_ln1_g, v_ln1_b, v_w_ff1, v_w_ff2, v_ln2_g, v_ln2_b):
    given = dict(locals())
    ex = _Exchange(given)

    names = ["w_in_e", "w_qb", "w_kvb", "w_out_e"]
    placed = _place_own([(given[n], 0, BF16) for n in names] + [(hg_gnorm.reshape(1, 1, D_MODEL // N_DEV), 0, F32)]
                        + [(w_in_o, 0, BF16), (w_out_o, 0, BF16), (w_ff1, 0, BF16), (w_ff1, 1, BF16), (w_ff2, 0, BF16), (w_ff2, 1, BF16)],
                        ex.dev)
    got = _all_gather(placed[:5])
    ex.start_weights(placed[5:], after=[got[0]])
    gw = dict(zip(names, got[:4]))
    small_names = ["mla_gq", "mla_gkv", "sgu_ln_g", "sgu_ln_b", "sgu_w", "sgu_b", "hg_lb", "ln1_g", "ln1_b", "ln2_g", "ln2_b"]
    sp = {n: given[n] for n in small_names}
    sp["hg_gnorm"] = got[4].reshape(1, D_MODEL)

    _, dx, grads, gs = _local_step(x[0], positions[0], loss_target[0], gw, sp, ex)

    def finish(n, layers, deps=()):
        return _finish_sharded(f"finish_{n}", layers, given[n], given["m_" + n], given["v_" + n], ex.where, deps=deps)

    ex.direct_end("l1", after=[dx])
    ex.direct_end("l0m", after=[dx])
    l1, l0m = ex.layers["l1"], ex.layers["l0m"]
    results = {}
    token = ex.grads_start("l0s", [grads[n] for n in names])
    results["w_ff1"] = finish("w_ff1", [l0m[0], l1[0]], deps=[token])
    token = ex.grads_middle("l0s", after=results["w_ff1"][0])
    results["w_ff2"] = finish("w_ff2", [l0m[1], l1[1]], deps=[token])
    results["w_in_o"] = finish("w_in_o", [l1[2]], deps=[token])
    results["w_out_o"] = finish("w_out_o", [l1[3]], deps=[token])
    results.update(ex.small_finish(after=[results["w_in_o"][0]]))
    ex.grads_end("l0s", after=[results[n][0] for n in ("mla_gq", "w_ff2", "w_in_o", "w_out_o")])
    for n, layer in zip(names, ex.layers["l0s"]):
        results[n] = finish(n, [layer])

    order = ["w_in_e", "mla_gq", "mla_gkv", "w_qb", "w_kvb", "sgu_ln_g", "sgu_ln_b", "sgu_w", "sgu_b", "w_out_e", "w_in_o",
             "hg_lb", "hg_gnorm", "w_out_o", "ln1_g", "ln1_b", "w_ff1", "w_ff2", "ln2_g", "ln2_b"]
    return (results["loss"], dx[None], *[results[name][kind] for kind in range(4) for name in order])
```

```python
import functools
import math

import jax
import jax.numpy as jnp
import numpy as np
from jax import lax
from jax.experimental import pallas as pl
from jax.experimental.pallas import tpu as pltpu

F32 = jnp.float32
BF16 = jnp.bfloat16
MESH = pl.DeviceIdType.MESH
HIGHEST = lax.Precision.HIGHEST

D_MODEL = 1024
D_FF = 4096
N_DEV = 8
HEADS = 8
HEAD_W = 128
MLA_NOPE = 64
MLA_ROPE = 32
MLA_V = 64
MLA_LORA = 256
MLA_SCALE = (MLA_NOPE + MLA_ROPE) ** -0.5
ROPE_BASE = 10000.0
SGU_DIM = 512
SGU_G = 4
SGU_CHUNK = 128
HG_CHUNK = 64
HG_CHUNKS_PER_STEP = 4
ALPHA = (2 * 2) ** 0.25
EPS = 1e-5
ADAM_LR, ADAM_B1, ADAM_B2, ADAM_EPS, ADAM_WD, ADAM_STEP = 0.001, 0.9, 0.999, 1e-08, 0.01, 10

VMEM_CAP_V7X = 56 * 2**20
VMEM_SLACK = 12 * 2**20
TM = 512
TN = 512


def _vmem(block_bytes):
    return int(min(VMEM_CAP_V7X, 2 * block_bytes + VMEM_SLACK))


def _hbm(a):
    return pltpu.with_memory_space_constraint(a, pltpu.HBM)


def _nbytes(shape, dtype):
    return int(np.prod([d for d in shape if d is not None])) * jnp.dtype(dtype).itemsize


def _sig(x):
    return 1.0 / (1.0 + jnp.exp(-x))


def _gelu(x):
    c = math.sqrt(2.0 / math.pi)
    t = jnp.tanh(c * (x + 0.044715 * x * x * x))
    return 0.5 * x * (1.0 + t), t


def _gelu_grad(x, t):
    c = math.sqrt(2.0 / math.pi)
    return 0.5 * (1.0 + t) + 0.5 * x * (1.0 - t * t) * c * (1.0 + 3 * 0.044715 * x * x)


def _dot(a, b, dims, precision=None):
    return lax.dot_general(a, b, (dims, ((), ())), preferred_element_type=F32, precision=precision)


NN = ((1,), (0,))
NT = ((1,), (1,))
TN_ = ((0,), (0,))


def _deps(deps):
    return [d for d in deps if d is not None]


class _Staged:
    def __init__(self, src, dst, sems, first):
        self.src, self.dst, self.sems, self.first, self.shape, self.seen = src, dst, sems, first, src.shape, set()

        @pl.when(first)
        def _():
            for d in range(self.shape[0]):
                self.copy(d).start()

    def copy(self, d):
        return pltpu.make_async_copy(self.src.at[d], self.dst.at[d], self.sems.at[d])

    def __getitem__(self, d):
        if d not in self.seen:
            self.seen.add(d)
            pl.when(self.first)(lambda: self.copy(d).wait())
        return self.dst[d]


def _staged_scratch(arrays):
    return [s for a in arrays for s in (pltpu.VMEM(a.shape, a.dtype), pltpu.SemaphoreType.DMA((a.shape[0],)))]


def _tiled(name, grid, ins, outs, compute, direct=False, deps=()):
    n_in, deps = len(ins), _deps(deps)
    n_skip = n_in + len(deps)
    staged = [k for k, (_, blk, _) in enumerate(ins) if blk is None]

    def kern(*refs):
        in_refs, scratch = list(refs[:n_in]), refs[n_skip + len(outs):]
        first = jnp.logical_and(pl.program_id(0) == 0, pl.program_id(1) == 0)
        for s, k in enumerate(staged):
            in_refs[k] = _Staged(in_refs[k], scratch[2 * s], scratch[2 * s + 1], first)
        if direct:
            compute(in_refs, refs[n_skip:n_skip + len(outs)])
            return
        for o_ref, r in zip(refs[n_skip:], compute(*in_refs)):
            o_ref[...] = r.astype(o_ref.dtype).reshape(o_ref.shape)

    swap = lambda f: (lambda j, i: f(i, j))
    nbytes = sum(_nbytes(a.shape if blk is None else blk, a.dtype) for a, blk, _ in ins)
    nbytes += sum(_nbytes(blk, dt) + _nbytes(blk, F32) for _, dt, blk, _ in outs)
    res = pl.pallas_call(
        kern, name=name, grid=grid,
        in_specs=[ANY_SPEC if blk is None else
                  pl.BlockSpec(blk, swap(f), pipeline_mode=pl.Buffered(1) if tuple(blk) == tuple(a.shape) else None)
                  for a, blk, f in ins] + [ANY_SPEC] * len(deps),
        out_specs=[pl.BlockSpec(blk, swap(f)) for _, _, blk, f in outs],
        out_shape=[pltpu.HBM(shape, dt) for shape, dt, _, _ in outs],
        scratch_shapes=_staged_scratch([ins[k][0] for k in staged]),
        compiler_params=pltpu.CompilerParams(dimension_semantics=("arbitrary",) * 2 if staged else ("parallel",) * 2,
                                             vmem_limit_bytes=_vmem(nbytes)),
    )(*[_hbm(a) for a, _, _ in ins], *deps)
    return res if len(res) > 1 else res[0]


def _rb(a, tm, w=None, cb=0):
    return (a, (tm, a.shape[1] if w is None else w), lambda i, j: (i, cb))


def _cw(b, tn):
    return (b, (b.shape[0], tn), lambda i, j: (0, j))


def _tl(a, tm):
    return (a, (a.shape[0], tm), lambda i, j: (0, i))


def _out(m, n, dtype, tm, tn):
    return ((m, n), dtype, (tm, tn), lambda i, j: (i, j))


def _out_dev(k, n, tm, dtype=F32):
    return ((N_DEV, k, n), dtype, (None, tm, n), lambda i, j: (j, i, 0))


def _twice(acc):
    return acc, acc


def _mmc(dims, n_pairs=1, epilogue=None):
    def compute(*refs):
        acc = None
        for k in range(n_pairs):
            d = _dot(refs[2 * k][...].astype(BF16), refs[2 * k + 1][...].astype(BF16), dims)
            acc = d if acc is None else acc + d
        ext = [r[...] for r in refs[2 * n_pairs:]]
        return epilogue(acc, *ext) if epilogue is not None else (acc,)

    return compute


def _res(w):
    return (w, w.shape, functools.partial(lambda i, j, nd: (0,) * nd, nd=w.ndim))


def _stg(w):
    return (w, None, None)


def _mmc_blocks(nblk, dims, rhs_block, epilogue=None):
    def compute(in_refs, out_refs):
        a = in_refs[0][...].astype(BF16)
        for d in range(nblk):
            acc = _dot(a, rhs_block(in_refs[1], d).astype(BF16), dims)
            n = acc.shape[1]
            ext = [r[:, d * n:(d + 1) * n] for r in in_refs[2:]]
            res = epilogue(acc, *ext) if epilogue is not None else (acc,)
            for o_ref, r in zip(out_refs, res):
                o_ref[:, d * n:(d + 1) * n] = r.astype(o_ref.dtype)

    return compute


def _rowwise(name, body, rows, consts, out_rows, out_accs=(), tr=512, deps=(), staged=0):
    T = rows[0][0].shape[0]
    tr = min(tr, T)
    deps = _deps(deps)
    nr, ncn, no, nd, na = len(rows), len(consts), len(out_rows), len(deps), len(out_accs)

    def kern(*refs):
        accs = refs[nr + ncn + nd + no:nr + ncn + nd + no + na]
        scratch = refs[nr + ncn + nd + no + na:]
        first = pl.program_id(0) == 0
        cs = [_Staged(refs[nr + k], scratch[2 * k], scratch[2 * k + 1], first) for k in range(staged)] + list(refs[nr + staged:nr + ncn])
        if accs:
            @pl.when(first)
            def _():
                for a in accs:
                    a[...] = jnp.zeros(a.shape, a.dtype)
        body(refs[:nr], cs, refs[nr + ncn + nd:nr + ncn + nd + no], accs)

    in_specs = [pl.BlockSpec((tr, w), functools.partial(lambda i, cb: (i, cb), cb=cb)) for _, w, cb in rows]
    in_specs += [ANY_SPEC] * staged
    in_specs += [pl.BlockSpec(c.shape, functools.partial(lambda i, nd: (0,) * nd, nd=c.ndim), pipeline_mode=pl.Buffered(1))
                 for c in consts[staged:]]
    in_specs += [ANY_SPEC] * nd
    out_specs = [pl.BlockSpec((tr, w), lambda i: (i, 0)) for w, _ in out_rows]
    out_specs += [pl.BlockSpec(s, functools.partial(lambda i, nd: (0,) * nd, nd=len(s))) for s, _ in out_accs]
    out_shape = [pltpu.HBM((T, w), dt) for w, dt in out_rows]
    out_shape += [pltpu.HBM(s, dt) for s, dt in out_accs]
    nbytes = sum(_nbytes((tr, w), a.dtype) for a, w, _ in rows) + sum(_nbytes(c.shape, c.dtype) for c in consts)
    nbytes += sum(_nbytes((tr, w), dt) for w, dt in out_rows) + sum(_nbytes(s, dt) for s, dt in out_accs)
    res = pl.pallas_call(
        kern, name=name, grid=(T // tr,), in_specs=in_specs, out_specs=out_specs, out_shape=out_shape,
        scratch_shapes=_staged_scratch(consts[:staged]), compiler_params=pltpu.CompilerParams(dimension_semantics=("arbitrary",), vmem_limit_bytes=_vmem(nbytes)),
    )(*[_hbm(a) for a, _, _ in rows], *[_hbm(c) for c in consts], *deps)
    return res if len(res) > 1 else res[0]


def _full(a):
    return (a, a.shape[1], 0)


def _ln_stats(y):
    mu = jnp.mean(y, axis=-1, keepdims=True)
    yc = y - mu
    r = lax.rsqrt(jnp.mean(yc * yc, axis=-1, keepdims=True) + EPS)
    return yc * r, r


def _row_halves(n):
    return [slice(0, n // 2), slice(n // 2, n)] if n >= 256 else [slice(0, n)]


def _ln_back(dh, xh, r, gain, dg_ref, db_ref):
    dg_ref[...] += jnp.sum(dh * xh, axis=0, keepdims=True)
    db_ref[...] += jnp.sum(dh, axis=0, keepdims=True)
    dx = dh * gain
    return r * (dx - jnp.mean(dx, axis=-1, keepdims=True) - xh * jnp.mean(dx * xh, axis=-1, keepdims=True))


def _proj_ln(name, acts, weights, h_in, g, b, layer, deps=()):
    n = len(acts)

    def body(rows, consts, outs, accs):
        acc = None
        for k in range(n):
            d = _dot(rows[k][...].astype(BF16), consts[k][...], NN)
            acc = d if acc is None else acc + d
        y = ALPHA * rows[n][...] + acc
        xh, _ = _ln_stats(y)
        h = xh * consts[n][layer:layer + 1, :] + consts[n + 1][layer:layer + 1, :]
        outs[0][...] = y
        outs[1][...] = h
        outs[2][...] = h.astype(BF16)

    return _rowwise(name, body, [_full(a) for a in acts] + [_full(h_in)], [*weights, g, b],
                    [(D_MODEL, F32), (D_MODEL, F32), (D_MODEL, BF16)], tr=TM, deps=deps)


def _proj_ln_loss(name, act, w2, h_in, g, b, layer, target):
    def body(rows, consts, outs, accs):
        y = ALPHA * rows[1][...] + _dot(rows[0][...], consts[0][...], NN)
        xh, r = _ln_stats(y)
        gain = consts[1][layer:layer + 1, :]
        err = xh * gain + consts[2][layer:layer + 1, :] - rows[2][...]
        accs[0][...] += jnp.sum(err * err, axis=0, keepdims=True)
        dy = _ln_back(err * (1.0 / D_MODEL), xh, r, gain, accs[1], accs[2])
        outs[0][...] = dy
        outs[1][...] = dy.astype(BF16)

    return _rowwise(name, body, [_full(act), _full(h_in), _full(target)], [w2, g, b], [(D_MODEL, F32), (D_MODEL, BF16)],
                    [((1, D_MODEL), F32)] * 3, tr=TM)


def _dh_ln_back(name, da, w, dy_next, y, g, layer, proj=(), deps=()):
    def body(rows, consts, outs, accs):
        n = consts[0].shape[2]
        for sl in _row_halves(rows[0].shape[0]):
            acc = ALPHA * rows[1][sl, :]
            for d in range(N_DEV):
                acc = acc + _dot(rows[0][sl, d * n:(d + 1) * n], consts[0][d], NT)
            xh, r = _ln_stats(rows[2][sl, :])
            dy = _ln_back(acc, xh, r, consts[1][layer:layer + 1, :], accs[0], accs[1])
            outs[0][sl, :] = dy
            dy_bf = dy.astype(BF16)
            outs[1][sl, :] = dy_bf
            off = 0
            for k, p in enumerate(proj):
                outs[2][sl, off:off + p.shape[0]] = _dot(dy_bf, consts[2 + k][...], NT).astype(BF16)
                off += p.shape[0]

    out_rows = [(D_MODEL, F32), (D_MODEL, BF16)] + ([(sum(p.shape[0] for p in proj), BF16)] if proj else [])
    return _rowwise(name, body, [_full(da), _full(dy_next), _full(y)], [w, g, *proj], out_rows,
                    [((1, D_MODEL), F32)] * 2, tr=TM, deps=deps, staged=1)


def _relu2_epilogue(acc):
    a = jnp.maximum(acc, 0.0)
    return acc, a * a


def _mlp_up(tag, h_bf, w1):
    T = h_bf.shape[0]
    tm = min(TM, T)
    return _tiled(f"{tag}_ff1", (1, T // tm), [_rb(h_bf, tm), _stg(w1)],
                  [_out(T, D_FF, BF16, tm, D_FF), _out(T, D_FF, BF16, tm, D_FF)],
                  _mmc_blocks(N_DEV, NN, lambda w, d: w[d], epilogue=_relu2_epilogue), direct=True)


def _mlp_bwd_w(tag, h_bf, a, act, dff_bf, w2, deps=()):
    T = h_bf.shape[0]
    tm = min(TM, T)
    da = _tiled(f"{tag}_dact", (1, T // tm), [_rb(dff_bf, tm), _stg(w2), _rb(a, tm)], [_out(T, D_FF, BF16, tm, D_FF)],
                _mmc_blocks(N_DEV, NT, lambda w, d: w[d], epilogue=lambda acc, a_t: (acc * 2.0 * jnp.maximum(a_t.astype(F32), 0.0),)),
                direct=True, deps=deps)
    dw2 = _tiled(f"{tag}_dw2", (1, D_FF // TM), [_tl(act, TM), _res(dff_bf)],
                 [_out(D_FF, D_MODEL, F32, TM, D_MODEL), _out(D_FF, D_MODEL, BF16, TM, D_MODEL)], _mmc(TN_, epilogue=_twice))
    dw1 = _tiled(f"{tag}_dw1", (N_DEV, 1), [_res(h_bf), _cw(da, TN)],
                 [_out_dev(D_MODEL, TN, D_MODEL), _out_dev(D_MODEL, TN, D_MODEL, BF16)], _mmc(TN_, epilogue=_twice))
    return da, dw1, [a.reshape(N_DEV, D_FF // N_DEV, D_MODEL) for a in dw2]


def _rope_tables(positions_col, invf_lane):
    def body(rows, consts, outs, accs):
        ang = rows[0][...].astype(F32) * consts[0][...]
        c, s = jnp.cos(ang), jnp.sin(ang)
        lane = lax.broadcasted_iota(jnp.int32, ang.shape, 1)
        outs[0][...] = jnp.where(lane < 64, 1.0, jnp.where(lane < 96, c, 0.0))
        outs[1][...] = jnp.where((lane >= 64) & (lane < 80), -s, 0.0)
        outs[2][...] = jnp.where((lane >= 80) & (lane < 96), s, 0.0)

    return _rowwise("rope_tables", body, [_full(positions_col)], [invf_lane], [(HEAD_W, F32)] * 3)


def _rope(x, c, s1, s2):
    return x * c + pltpu.roll(x, 112, 1) * s1 + pltpu.roll(x, 16, 1) * s2


def _rope_t(dx, c, s1, s2):
    return dx * c + pltpu.roll(dx * s1, 16, 1) + pltpu.roll(dx * s2, 112, 1)


def _rms(c):
    r = lax.rsqrt(jnp.mean(c * c, axis=-1, keepdims=True) + EPS)
    return c * r, r


def _rope_heads(x, c, s1, s2, fn):
    return jnp.concatenate([fn(x[:, h * HEAD_W:(h + 1) * HEAD_W], c, s1, s2) for h in range(HEADS)], axis=1)


def _mla_in(x, wm, ws, tabs, gq, gkv, deps=()):
    def body(rows, consts, outs, accs):
        xb = rows[0][...].astype(BF16)
        zm = _dot(xb, consts[0][...], NN)
        outs[0][...] = zm
        outs[1][...] = _dot(xb, consts[1][...], NN)
        outs[2][...] = (_rms(zm[:, 0:256])[0] * consts[2][...]).astype(BF16)
        outs[3][...] = (_rms(zm[:, 256:512])[0] * consts[3][...]).astype(BF16)
        outs[4][...] = _rope(zm[:, 512:640], rows[1][...], rows[2][...], rows[3][...])

    return _rowwise("l0_in", body, [_full(x)] + [_full(t) for t in tabs], [wm, ws, gq, gkv],
                    [(640, F32), (1024, F32), (256, BF16), (256, BF16), (HEAD_W, F32)], deps=deps)


def _mla_qkv(cqn, ckvn, kr_rot, tabs, wq, wk, wv):
    def body(rows, consts, outs, accs):
        c, s1, s2 = rows[3][...], rows[4][...], rows[5][...]
        outs[0][...] = _rope_heads(_dot(rows[0][...], consts[0][...], NN), c, s1, s2, _rope).astype(BF16)
        outs[1][...] = (_dot(rows[1][...], consts[1][...], NN) + jnp.concatenate([rows[2][...]] * HEADS, axis=1)).astype(BF16)
        outs[2][...] = _dot(rows[1][...], consts[2][...], NN).astype(BF16)

    rows = [_full(cqn), _full(ckvn), _full(kr_rot)] + [_full(t) for t in tabs]
    return _rowwise("l0_qkv", body, rows, [wq, wk, wv], [(HEADS * HEAD_W, BF16)] * 3)


def _mla_back(zm, cqn, ckvn, tabs, gq, gkv, wq, wk, wv, dq, dk, dv):
    def body(rows, consts, outs, accs):
        c, s1, s2 = rows[4][...], rows[5][...], rows[6][...]
        dk_t, dv_bf = rows[8][...], rows[9][...].astype(BF16)
        dq_bf = _rope_heads(rows[7][...], c, s1, s2, _rope_t).astype(BF16)
        dk_bf = dk_t.astype(BF16)
        accs[0][...] += _dot(rows[2][...], dq_bf, TN_)
        accs[1][...] += _dot(rows[3][...], dk_bf, TN_)
        accs[2][...] += _dot(rows[3][...], dv_bf, TN_)
        dlat = [_dot(dq_bf, consts[2][...], NT), _dot(dk_bf, consts[3][...], NT) + _dot(dv_bf, consts[4][...], NT)]
        for k in range(2):
            ch, r = _rms(rows[k][...])
            accs[3 + k][...] += jnp.sum(dlat[k] * ch, axis=0, keepdims=True)
            dc = dlat[k] * consts[k][...]
            outs[0][:, 256 * k:256 * (k + 1)] = (r * (dc - ch * jnp.mean(dc * ch, axis=-1, keepdims=True))).astype(BF16)
        dks = dk_t[:, 0:HEAD_W]
        for h in range(1, HEADS):
            dks = dks + dk_t[:, h * HEAD_W:(h + 1) * HEAD_W]
        lane = lax.broadcasted_iota(jnp.int32, dks.shape, 1)
        dks = jnp.where((lane >= 64) & (lane < 96), dks, 0.0)
        outs[0][:, 512:640] = _rope_t(dks, c, s1, s2).astype(BF16)

    rows = [(zm, 256, 0), (zm, 256, 1), _full(cqn), _full(ckvn)] + [_full(t) for t in tabs] + [_full(dq), _full(dk), _full(dv)]
    wide = HEADS * HEAD_W
    return _rowwise("l0_mla_back", body, rows, [gq, gkv, wq, wk, wv], [(640, BF16)],
                    [((MLA_LORA, wide), F32)] * 3 + [((1, MLA_LORA), F32)] * 2, tr=256)


def _in_back(x, dzm, dzs, dy, wm, ws, deps=()):
    def body(rows, consts, outs, accs):
        dzm_t, dzs_t = rows[1][...], rows[2][...]
        outs[0][...] = _dot(dzm_t, consts[0][...], NT) + _dot(dzs_t, consts[1][...], NT) + ALPHA * rows[3][...]
        xb = rows[0][...].astype(BF16)
        accs[0][...] += _dot(xb, dzm_t, TN_)
        accs[1][...] += _dot(xb, dzs_t, TN_)

    return _rowwise("l0_in_back", body, [_full(x), _full(dzm), _full(dzs), _full(dy)], [wm, ws], [(D_MODEL, F32)],
                    [((D_MODEL, 640), F32), ((D_MODEL, 1024), F32)], deps=deps)


def _out_weight_grads(o_att, b_out, dy_bf):
    def body(rows, consts, outs, accs):
        d = rows[2][...]
        accs[0][...] += _dot(rows[0][...].astype(BF16), d, TN_)
        accs[1][...] += _dot(rows[1][...], d, TN_)

    return _rowwise("l0_dw_out", body, [_full(o_att), _full(b_out), _full(dy_bf)], [], [],
                    [((HEADS * HEAD_W, D_MODEL), F32), ((SGU_DIM, D_MODEL), F32)])


def _attn_block(T):
    return min(1024, T)


def _attn_fwd(q, k, v):
    T = q.shape[0]
    BQ = _attn_block(T)
    nq = T // BQ

    def kern(q_ref, k_ref, v_ref, o_ref, lse_ref):
        def step(i, j, carry, masked):
            m, l, acc = carry
            qb = q_ref[pl.ds(pl.multiple_of(i * BQ, BQ), BQ), :]
            kb = k_ref[pl.ds(pl.multiple_of(j * BQ, BQ), BQ), :]
            vb = v_ref[pl.ds(pl.multiple_of(j * BQ, BQ), BQ), :]
            s = _dot(qb, kb, NT) * MLA_SCALE
            if masked:
                row = lax.broadcasted_iota(jnp.int32, s.shape, 0)
                col = lax.broadcasted_iota(jnp.int32, s.shape, 1)
                s = jnp.where(col <= row, s, -1e30)
            m_new = jnp.maximum(m, jnp.max(s, axis=-1, keepdims=True))
            p = jnp.exp(s - m_new)
            a = jnp.exp(m - m_new)
            l = a * l + jnp.sum(p, axis=-1, keepdims=True)
            acc = a * acc + _dot(p.astype(BF16), vb, NN)
            return m_new, l, acc

        def qloop(i, _):
            init = (jnp.full((BQ, 1), -1e30, F32), jnp.zeros((BQ, 1), F32), jnp.zeros((BQ, HEAD_W), F32))
            carry = lax.fori_loop(0, i, lambda j, c: step(i, j, c, False), init)
            m, l, acc = step(i, i, carry, True)
            rows = pl.ds(pl.multiple_of(i * BQ, BQ), BQ)
            o_ref[rows, :] = acc / l
            lse_ref[0, rows, :] = m + jnp.log(l)
            return 0

        lax.fori_loop(0, nq, qloop, 0)

    head = pl.BlockSpec((T, HEAD_W), lambda h: (0, h))
    nbytes = 3 * _nbytes((T, HEAD_W), BF16) + _nbytes((T, HEAD_W), F32) + _nbytes((T, 128), F32)
    return pl.pallas_call(
        kern, name="attn_fwd", grid=(HEADS,), in_specs=[head, head, head],
        out_specs=[head, pl.BlockSpec((1, T, 1), lambda h: (h, 0, 0))],
        out_shape=[pltpu.HBM((T, HEADS * HEAD_W), F32), pltpu.HBM((HEADS, T, 1), F32)],
        compiler_params=pltpu.CompilerParams(dimension_semantics=("parallel",), vmem_limit_bytes=_vmem(nbytes)),
    )(_hbm(q), _hbm(k), _hbm(v))


def _attn_bwd(q, k, v, o, lse, dcat, deps=()):
    T = q.shape[0]
    BQ = _attn_block(T)
    nq = T // BQ
    deps = _deps(deps)

    def kern(q_ref, k_ref, v_ref, o_ref, lse_ref, do_ref, *rest):
        dq_ref, dk_ref, dv_ref, dd_ref = rest[len(deps):]
        dq_ref[...] = jnp.zeros(dq_ref.shape, F32)

        def dloop(i, _):
            rows = pl.ds(pl.multiple_of(i * BQ, BQ), BQ)
            dd_ref[rows, :] = jnp.sum(do_ref[rows, :].astype(F32) * o_ref[rows, :], axis=-1, keepdims=True)
            return 0

        lax.fori_loop(0, nq, dloop, 0)

        def tile(q0, k0, n, carry, masked):
            dk_acc, dv_acc = carry
            rq = pl.ds(pl.multiple_of(q0, n), n)
            rk = pl.ds(pl.multiple_of(k0, n), n)
            qb, kb, vb, dob = q_ref[rq, :], k_ref[rk, :], v_ref[rk, :], do_ref[rq, :]
            s = _dot(qb, kb, NT) * MLA_SCALE
            p = jnp.exp(s - lse_ref[0, rq, :])
            if masked:
                row = lax.broadcasted_iota(jnp.int32, s.shape, 0)
                col = lax.broadcasted_iota(jnp.int32, s.shape, 1)
                p = jnp.where(col <= row, p, 0.0)
            dp = _dot(dob, vb, NT)
            ds = (p * (dp - dd_ref[rq, :]) * MLA_SCALE).astype(BF16)
            dv_acc = dv_acc + _dot(p.astype(BF16), dob, TN_)
            dk_acc = dk_acc + _dot(ds, qb, TN_)
            dq_ref[rq, :] += _dot(ds, kb, NN)
            return dk_acc, dv_acc

        def kloop(j, _):
            base, half = j * BQ, BQ // 2
            zero = (jnp.zeros((half, HEAD_W), F32), jnp.zeros((half, HEAD_W), F32))
            early = tile(base + half, base, half, tile(base, base, half, zero, True), False)
            late = tile(base + half, base + half, half, zero, True)
            carry = tuple(jnp.concatenate([a, b], axis=0) for a, b in zip(early, late))
            dk_acc, dv_acc = lax.fori_loop(j + 1, nq, lambda i, c: tile(i * BQ, base, BQ, c, False), carry)
            rk = pl.ds(pl.multiple_of(j * BQ, BQ), BQ)
            dk_ref[rk, :] = dk_acc
            dv_ref[rk, :] = dv_acc
            return 0

        lax.fori_loop(0, nq, kloop, 0)

    head = pl.BlockSpec((T, HEAD_W), lambda h: (0, h))
    nbytes = 4 * _nbytes((T, HEAD_W), BF16) + 5 * _nbytes((T, HEAD_W), F32) + 2 * _nbytes((T, 128), F32)
    return pl.pallas_call(
        kern, name="attn_bwd", grid=(HEADS,),
        in_specs=[head, head, head, head, pl.BlockSpec((1, T, 1), lambda h: (h, 0, 0)), head] + [ANY_SPEC] * len(deps),
        out_specs=[head, head, head],
        out_shape=[pltpu.HBM((T, HEADS * HEAD_W), F32)] * 3,
        scratch_shapes=[pltpu.VMEM((T, 1), F32)],
        compiler_params=pltpu.CompilerParams(dimension_semantics=("parallel",), vmem_limit_bytes=_vmem(nbytes)),
    )(*[_hbm(a) for a in (q, k, v, o, lse, dcat)], *deps)


def _sgu_common(u, v, ln_g, ln_b):
    ua, tu = _gelu(u)
    va, tv = _gelu(v)
    vh, r = _ln_stats(va)
    return ua, tu, tv, vh, r, vh * ln_g + ln_b


def _tril_mask(n):
    return lax.broadcasted_iota(jnp.int32, (n, n), 1) <= lax.broadcasted_iota(jnp.int32, (n, n), 0)


def _sgu_fwd(zs, ln_g, ln_b, w, bias_full):
    def body(rows, consts, outs, accs):
        ua, _, _, _, _, vn = _sgu_common(rows[0][...], rows[1][...], consts[0][...], consts[1][...])
        vn = vn.astype(BF16)
        tri = _tril_mask(SGU_CHUNK)
        for g in range(SGU_G):
            wg = jnp.where(tri, consts[2][0, g], 0.0).astype(BF16)
            cols = slice(g * 128, (g + 1) * 128)
            for c in range(ua.shape[0] // SGU_CHUNK):
                rws = slice(c * SGU_CHUNK, (c + 1) * SGU_CHUNK)
                mixed = _dot(wg, vn[rws, cols], NN) + consts[3][:, cols]
                outs[0][rws, cols] = (ua[rws, cols] * mixed).astype(BF16)

    return _rowwise("sgu_fwd", body, [(zs, 512, 0), (zs, 512, 1)], [ln_g, ln_b, w, bias_full], [(SGU_DIM, BF16)])


def _sgu_bwd(zs, dcat, ln_g, ln_b, w, bias_full):
    def body(rows, consts, outs, accs):
        u, v = rows[0][...], rows[1][...]
        ua, tu, tv, vh, r, vn = _sgu_common(u, v, consts[0][...], consts[1][...])
        dout = rows[2][...].astype(F32)
        vn_bf = vn.astype(BF16)
        tri = _tril_mask(SGU_CHUNK)
        dmixed = (dout * ua)
        dmixed_bf = dmixed.astype(BF16)
        ones = jnp.ones((8, SGU_CHUNK), F32)
        dvn_cols, mixed_cols = [], []
        for g in range(SGU_G):
            wg = jnp.where(tri, consts[2][0, g], 0.0).astype(BF16)
            cols = slice(g * 128, (g + 1) * 128)
            dvn_rows, mixed_rows = [], []
            dw = jnp.zeros((SGU_CHUNK, SGU_CHUNK), F32)
            dmix_sum = jnp.zeros((SGU_CHUNK, 128), F32)
            for c in range(u.shape[0] // SGU_CHUNK):
                rws = slice(c * SGU_CHUNK, (c + 1) * SGU_CHUNK)
                mixed_rows.append(_dot(wg, vn_bf[rws, cols], NN) + consts[3][:, cols])
                dvn_rows.append(_dot(wg, dmixed_bf[rws, cols], TN_))
                dw = dw + _dot(dmixed_bf[rws, cols], vn_bf[rws, cols], NT)
                dmix_sum = dmix_sum + dmixed[rws, cols]
            accs[0][g] += jnp.where(tri, dw, 0.0)
            accs[3][g:g + 1, :] += _dot(ones, dmix_sum, NT, precision=HIGHEST)[0:1, :]
            dvn_cols.append(jnp.concatenate(dvn_rows, axis=0))
            mixed_cols.append(jnp.concatenate(mixed_rows, axis=0))
        dvn = jnp.concatenate(dvn_cols, axis=1)
        mixed = jnp.concatenate(mixed_cols, axis=1)
        accs[1][...] += jnp.sum(dvn * vh, axis=0, keepdims=True)
        accs[2][...] += jnp.sum(dvn, axis=0, keepdims=True)
        dvh = dvn * consts[0][...]
        dva = r * (dvh - jnp.mean(dvh, axis=-1, keepdims=True) - vh * jnp.mean(dvh * vh, axis=-1, keepdims=True))
        outs[0][:, 0:512] = (dout * mixed * _gelu_grad(u, tu)).astype(BF16)
        outs[0][:, 512:1024] = (dva * _gelu_grad(v, tv)).astype(BF16)

    return _rowwise("sgu_bwd", body, [(zs, 512, 0), (zs, 512, 1), (dcat, 512, 2)], [ln_g, ln_b, w, bias_full], [(1024, BF16)],
                    [((SGU_G, 128, 128), F32), ((1, SGU_DIM), F32), ((1, SGU_DIM), F32), ((SGU_G, 128), F32)], tr=256)


def _lower_bound(hg_lb):
    a0, a1 = hg_lb[0:1, :], hg_lb[1:2, :]
    m = jnp.maximum(a0, a1)
    e0, e1 = jnp.exp(a0 - m), jnp.exp(a1 - m)
    s0, s1 = e0 / (e0 + e1), e1 / (e0 + e1)
    return (s0 + s1) - s0, s0, s1


def _prefix_rows(x, reverse=False):
    n = x.shape[0]
    row = lax.broadcasted_iota(jnp.int32, x.shape, 0)
    s = 1
    while s < n:
        if reverse:
            x = x + jnp.where(row < n - s, pltpu.roll(x, n - s, 0), 0.0)
        else:
            x = x + jnp.where(row >= s, pltpu.roll(x, s, 0), 0.0)
        s *= 2
    return x


def _hg_gates(qr, fr, lb):
    C = qr.shape[0]
    sq = _sig(qr)
    qf = qr * sq
    sf = _sig(fr)
    gate = lb + (1.0 - lb) * sf
    kk = 1.0 - gate
    tri = _tril_mask(C)
    b = _prefix_rows(jnp.log(gate))
    bref = b[C // 2 - 1:C // 2, :]
    bl = b[C - 1:C, :]
    e_b = jnp.exp(b)
    e_q = jnp.exp(b - bref)
    e_k = jnp.exp(bref - b)
    e_lb = jnp.exp(bl - b)
    return dict(sq=sq, qf=qf, sf=sf, gate=gate, kk=kk, tri=tri, bl=bl, e_b=e_b, e_q=e_q, e_k=e_k, e_lb=e_lb)


def _hgrn_fwd(z1, hg_lb, gnorm):
    T = z1.shape[0]
    C = min(HG_CHUNK, T)
    nc = T // C
    ns = HG_CHUNKS_PER_STEP if nc % HG_CHUNKS_PER_STEP == 0 else 1
    R = ns * C

    def kern(q_ref, f_ref, i_ref, g_ref, lb_ref, gn_ref, o_ref, hg_ref, st_ref, s_scr):
        @pl.when(pl.program_id(0) == 0)
        def _():
            s_scr[...] = jnp.zeros(s_scr.shape, F32)

        lb_all, _, _ = _lower_bound(lb_ref[...])
        for sub in range(ns):
            rows = slice(sub * C, (sub + 1) * C)
            st_ref[sub] = s_scr[...]
            for h in range(HEADS):
                cols = slice(h * HEAD_W, (h + 1) * HEAD_W)
                t = _hg_gates(q_ref[rows, cols], f_ref[rows, cols], lb_all[:, cols])
                v_bf = i_ref[rows, cols].astype(BF16)
                st = s_scr[h]
                a = jnp.where(t["tri"], _dot((t["qf"] * t["e_q"]).astype(BF16), (t["kk"] * t["e_k"]).astype(BF16), NT), 0.0)
                o = _dot(a.astype(BF16), v_bf, NN) + _dot((t["qf"] * t["e_b"]).astype(BF16), st.astype(BF16), NT)
                s_scr[h] = st * jnp.exp(t["bl"]) + _dot(v_bf, (t["kk"] * t["e_lb"]).astype(BF16), TN_)
                o_ref[rows, cols] = o
                gr = g_ref[rows, cols]
                r = lax.rsqrt(jnp.mean(o * o, axis=-1, keepdims=True) + EPS)
                hg_ref[rows, cols] = (o * r * gn_ref[:, cols] * (gr * _sig(gr))).astype(BF16)

    seg = lambda k: pl.BlockSpec((R, D_MODEL), functools.partial(lambda n, k: (n, k), k=k))
    row = pl.BlockSpec((R, D_MODEL), lambda n: (n, 0))
    nbytes = 6 * _nbytes((R, D_MODEL), F32) + (2 + ns) * _nbytes((HEADS, 128, 128), F32)
    return pl.pallas_call(
        kern, name="hgrn_fwd", grid=(nc // ns,),
        in_specs=[seg(0), seg(1), seg(2), seg(3), pl.BlockSpec((2, D_MODEL), lambda n: (0, 0)),
                  pl.BlockSpec((1, D_MODEL), lambda n: (0, 0))],
        out_specs=[row, row, pl.BlockSpec((ns, HEADS, 128, 128), lambda n: (n, 0, 0, 0))],
        out_shape=[pltpu.HBM((T, D_MODEL), F32), pltpu.HBM((T, D_MODEL), BF16),
                   pltpu.HBM((nc, HEADS, 128, 128), F32)],
        scratch_shapes=[pltpu.VMEM((HEADS, 128, 128), F32)],
        compiler_params=pltpu.CompilerParams(dimension_semantics=("arbitrary",), vmem_limit_bytes=_vmem(nbytes)),
    )(*[_hbm(a) for a in (z1, z1, z1, z1, hg_lb, gnorm)])


def _hgrn_bwd(z1, o_pre, dhg, states, hg_lb, gnorm):
    T = z1.shape[0]
    C = min(HG_CHUNK, T)
    nc = T // C
    ns = HG_CHUNKS_PER_STEP if nc % HG_CHUNKS_PER_STEP == 0 else 1
    R, steps = ns * C, nc // ns

    def kern(q_ref, f_ref, i_ref, g_ref, o_ref, dhg_ref, st_ref, lb_ref, gn_ref, dz_ref, dlb_ref, dgn_ref, ds_scr, dlb_scr):
        n = pl.program_id(0)

        @pl.when(n == 0)
        def _():
            ds_scr[...] = jnp.zeros(ds_scr.shape, F32)
            dlb_scr[...] = jnp.zeros(dlb_scr.shape, F32)
            dgn_ref[...] = jnp.zeros(dgn_ref.shape, F32)

        lb_all, s0, s1 = _lower_bound(lb_ref[...])
        for sub in reversed(range(ns)):
            rows = slice(sub * C, (sub + 1) * C)
            for h in range(HEADS):
                cols = slice(h * HEAD_W, (h + 1) * HEAD_W)
                lb = lb_all[:, cols]
                qr, fr = q_ref[rows, cols], f_ref[rows, cols]
                t = _hg_gates(qr, fr, lb)
                tri = t["tri"]
                v_bf = i_ref[rows, cols].astype(BF16)
                st_bf = st_ref[sub, h].astype(BF16)
                dst = ds_scr[h]
                dst_bf = dst.astype(BF16)
                o = o_ref[rows, cols]
                gr = g_ref[rows, cols]
                sg = _sig(gr)
                sil = gr * sg
                gn = gn_ref[:, cols]
                r = lax.rsqrt(jnp.mean(o * o, axis=-1, keepdims=True) + EPS)
                on = o * r
                dh = dhg_ref[rows, cols].astype(F32)
                dgn_ref[:, cols] += jnp.sum(dh * on * sil, axis=0, keepdims=True)
                dg = dh * on * gn * (sg * (1.0 + gr * (1.0 - sg)))
                don = dh * gn * sil
                do_bf = (r * (don - on * jnp.mean(don * on, axis=-1, keepdims=True))).astype(BF16)
                qe = (t["qf"] * t["e_q"]).astype(BF16)
                ke = (t["kk"] * t["e_k"]).astype(BF16)
                qb = (t["qf"] * t["e_b"]).astype(BF16)
                kh_bf = (t["kk"] * t["e_lb"]).astype(BF16)
                a_bf = jnp.where(tri, _dot(qe, ke, NT), 0.0).astype(BF16)
                da_bf = jnp.where(tri, _dot(do_bf, v_bf, NT), 0.0).astype(BF16)
                dv = _dot(a_bf, do_bf, TN_) + _dot(kh_bf, dst_bf, NT)
                dqe = _dot(da_bf, ke, NN)
                dqb = _dot(do_bf, st_bf, NN)
                dke = _dot(da_bf, qe, TN_)
                dkh = _dot(v_bf, dst_bf, NN)
                dqf = dqe * t["e_q"] + dqb * t["e_b"]
                dkk = dke * t["e_k"] + dkh * t["e_lb"]
                kh_r = kh_bf.astype(F32)
                db = qe.astype(F32) * dqe - ke.astype(F32) * dke + qb.astype(F32) * dqb - kh_r * dkh
                e_bl = jnp.exp(t["bl"])
                dbl = jnp.sum(dkh * kh_r, axis=0, keepdims=True) + e_bl * jnp.sum(st_ref[sub, h] * dst, axis=0, keepdims=True)
                dlg = _prefix_rows(db, reverse=True) + dbl
                ds_scr[h] = dst * e_bl + _dot(do_bf, qb, TN_)
                dgate = dlg / t["gate"] - dkk
                sf = t["sf"]
                dlb_scr[:, cols] += jnp.sum(dgate * (1.0 - sf), axis=0, keepdims=True)
                df = dgate * (1.0 - lb) * sf * (1.0 - sf)
                dq = dqf * (t["sq"] * (1.0 + qr * (1.0 - t["sq"])))
                dz_ref[rows, cols] = dq.astype(BF16)
                dz_ref[rows, D_MODEL + h * HEAD_W:D_MODEL + (h + 1) * HEAD_W] = df.astype(BF16)
                dz_ref[rows, 2 * D_MODEL + h * HEAD_W:2 * D_MODEL + (h + 1) * HEAD_W] = dv.astype(BF16)
                dz_ref[rows, 3 * D_MODEL + h * HEAD_W:3 * D_MODEL + (h + 1) * HEAD_W] = dg.astype(BF16)

        @pl.when(n == steps - 1)
        def _():
            d = s0 * s1 * dlb_scr[...]
            dlb_ref[0:1, :] = -d
            dlb_ref[1:2, :] = d

    seg = lambda k: pl.BlockSpec((R, D_MODEL), functools.partial(lambda n, k: (steps - 1 - n, k), k=k))
    nbytes = 6 * _nbytes((R, D_MODEL), F32) + _nbytes((R, 4 * D_MODEL), BF16) + (2 + ns) * _nbytes((HEADS, 128, 128), F32)
    return pl.pallas_call(
        kern, name="hgrn_bwd", grid=(steps,),
        in_specs=[seg(0), seg(1), seg(2), seg(3), seg(0), seg(0),
                  pl.BlockSpec((ns, HEADS, 128, 128), lambda n: (steps - 1 - n, 0, 0, 0)),
                  pl.BlockSpec((2, D_MODEL), lambda n: (0, 0)), pl.BlockSpec((1, D_MODEL), lambda n: (0, 0))],
        out_specs=[pl.BlockSpec((R, 4 * D_MODEL), lambda n: (steps - 1 - n, 0)),
                   pl.BlockSpec((2, D_MODEL), lambda n: (0, 0)), pl.BlockSpec((1, D_MODEL), lambda n: (0, 0))],
        out_shape=[pltpu.HBM((T, 4 * D_MODEL), BF16), pltpu.HBM((2, D_MODEL), F32),
                   pltpu.HBM((1, D_MODEL), F32)],
        scratch_shapes=[pltpu.VMEM((HEADS, 128, 128), F32), pltpu.VMEM((1, D_MODEL), F32)],
        compiler_params=pltpu.CompilerParams(dimension_semantics=("arbitrary",), vmem_limit_bytes=_vmem(nbytes)),
    )(*[_hbm(a) for a in (z1, z1, z1, z1, o_pre, dhg, states, hg_lb, gnorm)])


def _prep_weights(gw):
    w_in_e = gw["w_in_e"].transpose(1, 0, 2).reshape(D_MODEL, 1568)
    kr = jnp.pad(w_in_e[:, 512:544], ((0, 0), (64, 32)))
    wm = jnp.concatenate([w_in_e[:, 0:512], kr], axis=1)
    ws = w_in_e[:, 544:1568]
    w_qb = gw["w_qb"].transpose(1, 0, 2).reshape(MLA_LORA, HEADS, 96)
    wq = jnp.pad(w_qb, ((0, 0), (0, 0), (0, 32))).reshape(MLA_LORA, HEADS * HEAD_W)
    kvb = gw["w_kvb"].transpose(1, 0, 2).reshape(MLA_LORA, HEADS, 128)
    wk = jnp.pad(kvb[:, :, :64], ((0, 0), (0, 0), (0, 64))).reshape(MLA_LORA, HEADS * HEAD_W)
    wv = jnp.pad(kvb[:, :, 64:], ((0, 0), (0, 0), (0, 64))).reshape(MLA_LORA, HEADS * HEAD_W)
    w_out_e = gw["w_out_e"].reshape(D_MODEL, D_MODEL)
    woa = jnp.pad(w_out_e[:512].reshape(HEADS, 64, D_MODEL), ((0, 0), (0, 64), (0, 0))).reshape(HEADS * HEAD_W, D_MODEL)
    return dict(wm=wm, ws=ws, wq=wq, wk=wk, wv=wv, woa=woa, wob=w_out_e[512:])


def _unprep_grads(g):
    dwm, dws = g["wm"], g["ws"]
    d_in_e = jnp.concatenate([dwm[:, 0:512], dwm[:, 512 + 64:512 + 96], dws], axis=1)
    d_qb = g["wq"].reshape(MLA_LORA, HEADS, HEAD_W)[:, :, :96].reshape(MLA_LORA, HEADS * 96)
    dk = g["wk"].reshape(MLA_LORA, HEADS, HEAD_W)[:, :, :64]
    dv = g["wv"].reshape(MLA_LORA, HEADS, HEAD_W)[:, :, :64]
    d_kvb = jnp.concatenate([dk, dv], axis=2).reshape(MLA_LORA, HEADS * 128)
    d_oa = g["woa"].reshape(HEADS, HEAD_W, D_MODEL)[:, :64].reshape(HEADS * 64, D_MODEL)
    dev_major = lambda a: a.reshape(a.shape[0], N_DEV, a.shape[1] // N_DEV).transpose(1, 0, 2)
    return dict(w_in_e=dev_major(d_in_e), w_qb=dev_major(d_qb), w_kvb=dev_major(d_kvb),
                w_out_e=jnp.concatenate([d_oa, g["wob"]], axis=0).reshape(N_DEV, D_MODEL // N_DEV, D_MODEL))


def _local_step(x, positions, target, gw, sp, ex):
    w = _prep_weights(gw)
    T = x.shape[0]
    tm = min(TM, T)
    nt = T // tm
    half = MLA_ROPE // 2
    inv_freq = ROPE_BASE ** (-jnp.arange(half, dtype=F32) / half)
    invf_lane = jnp.concatenate([jnp.zeros((64,), F32), inv_freq, inv_freq, jnp.zeros((32,), F32)]).reshape(1, HEAD_W)
    tabs = _rope_tables(positions.reshape(T, 1), invf_lane)
    bias_full = jnp.repeat(sp["sgu_b"][0].T, 128, axis=1)
    sgu_w = sp["sgu_w"]
    gq, gkv = sp["mla_gq"], sp["mla_gkv"]
    ln1_g, ln1_b, ln2_g, ln2_b = sp["ln1_g"], sp["ln1_b"], sp["ln2_g"], sp["ln2_b"]
    zm, zs, cqn, ckvn, kr_rot = _mla_in(x, w["wm"], w["ws"], tabs, gq, gkv, deps=[ex.first_token])
    q, k, v = _mla_qkv(cqn, ckvn, kr_rot, tabs, w["wq"], w["wk"], w["wv"])
    o_att, lse = _attn_fwd(q, k, v)
    b_out = _sgu_fwd(zs, sp["sgu_ln_g"], sp["sgu_ln_b"], sgu_w, bias_full)
    token = ex.weights_forward(after=[o_att, b_out])
    y1, h1, h1_bf = _proj_ln("l0_out_ln1", [o_att, b_out], [w["woa"], w["wob"]], x, ln1_g, ln1_b, 0, deps=[token])
    big = ex.weights_ready(after=[y1])
    w_ff1, w_in_o, w_out_o = big["w_ff1"], big["w_in_o"], big["w_out_o"].reshape(D_MODEL, D_MODEL)
    w_ff2 = [a.reshape(D_FF, D_MODEL) for a in big["w_ff2"]]
    a0, act0 = _mlp_up("l0", h1_bf, w_ff1[0])
    y2, h2, h2_bf = _proj_ln("l0_ff2_ln2", [act0], [w_ff2[0]], h1, ln2_g, ln2_b, 0)

    z1 = _tiled("l1_in", (1, nt), [_rb(h2_bf, tm), _stg(w_in_o)], [_out(T, 4 * D_MODEL, F32, tm, 4 * D_MODEL)],
                _mmc_blocks(N_DEV, NN, lambda w, d: w[d]), direct=True)
    o_pre, hg, states = _hgrn_fwd(z1, sp["hg_lb"], sp["hg_gnorm"])
    y3, h3, h3_bf = _proj_ln("l1_out_ln1", [hg], [w_out_o], h2, ln1_g, ln1_b, 1)
    a1, act1 = _mlp_up("l1", h3_bf, w_ff1[1])

    gs, g0 = {}, {}
    dy4, dy4_bf, sq_err, gs["ln2_g1"], gs["ln2_b1"] = _proj_ln_loss("l1_ff2_loss", act1, w_ff2[1], h3, ln2_g, ln2_b, 1, target)
    gs["sq_err"] = sq_err
    da1, dw1_1, dw2_1 = _mlp_bwd_w("l1", h3_bf, a1, act1, dy4_bf, big["w_ff2"][1])
    dy3, dy3_bf, dhg, gs["ln1_g1"], gs["ln1_b1"] = _dh_ln_back("l1_dh_ln1", da1, w_ff1[1], dy4, y3, ln1_g, 1, proj=[w_out_o])
    d_out_o = _tiled("l1_dwout", (2, D_MODEL // TM), [_tl(hg, TM), _cw(dy3_bf, TN)],
                     [_out(D_MODEL, D_MODEL, F32, TM, TN), _out(D_MODEL, D_MODEL, BF16, TM, TN)], _mmc(TN_, epilogue=_twice))
    d_out_o = [a.reshape(N_DEV, D_MODEL // N_DEV, D_MODEL) for a in d_out_o]
    dz1, gs["hg_lb"], gs["hg_gnorm"] = _hgrn_bwd(z1, o_pre, dhg, states, sp["hg_lb"], sp["hg_gnorm"])
    d_in_o = _tiled("l1_dwin", (N_DEV, 1), [_res(h2_bf), _cw(dz1, TN)],
                    [_out_dev(D_MODEL, TN, D_MODEL), _out_dev(D_MODEL, TN, D_MODEL, BF16)], _mmc(TN_, epilogue=_twice))
    token = ex.direct_start("l1", [dw1_1, dw2_1, d_in_o, d_out_o])

    dy2, dy2_bf, gs["ln2_g0"], gs["ln2_b0"] = _dh_ln_back("l1_dh_ln2", dz1, w_in_o, dy3, y2, ln2_g, 0, deps=[token])
    da0, dw1_0, dw2_0 = _mlp_bwd_w("l0", h1_bf, a0, act0, dy2_bf, big["w_ff2"][0])
    token = ex.direct_start("l0m", [dw1_0, dw2_0])
    dy1, dy1_bf, dcat, gs["ln1_g0"], gs["ln1_b0"] = _dh_ln_back("l0_dh_ln1", da0, w_ff1[0], dy2, y1, ln1_g, 0,
                                                                 proj=[w["woa"], w["wob"]], deps=[token])
    g0["woa"], g0["wob"] = _out_weight_grads(o_att, b_out, dy1_bf)
    dzs, gs["sgu_w"], gs["sgu_ln_g"], gs["sgu_ln_b"], gs["sgu_b"] = _sgu_bwd(zs, dcat, sp["sgu_ln_g"], sp["sgu_ln_b"], sgu_w, bias_full)
    dq, dk, dv = _attn_bwd(q, k, v, o_att, lse, dcat)
    dzm, g0["wq"], g0["wk"], g0["wv"], gs["mla_gq"], gs["mla_gkv"] = _mla_back(zm, cqn, ckvn, tabs, gq, gkv, w["wq"], w["wk"], w["wv"],
                                                                                 dq, dk, dv)
    token = ex.small_start(gs)
    dx, g0["wm"], g0["ws"] = _in_back(x, dzm, dzs, dy1, w["wm"], w["ws"], deps=[token])

    return sq_err, dx, _unprep_grads(g0), gs


def _me():
    return lax.axis_index("x"), lax.axis_index("y"), lax.axis_index("c")


ANY_SPEC = pl.BlockSpec(memory_space=pl.ANY)
HBM_SPEC = pl.BlockSpec(memory_space=pltpu.HBM)
SEM_SPEC = pl.BlockSpec(memory_space=pltpu.SEMAPHORE)
EFFECT = pltpu.SideEffectType.DATAFLOW_SIDE_EFFECTING


def _split_start(name, srcs, lands, n_sems, make_copies, after=()):
    n, m, k = len(srcs), len(lands), len(after)

    def body(*refs):
        for cp in make_copies(refs[:n], refs[n:n + m], refs[n + m + k], refs[n + m + k + 1]):
            cp.start()
        refs[-1][...] = jnp.zeros(refs[-1].shape, F32)

    out_shape = (pltpu.SemaphoreType.DMA((n_sems,)), pltpu.SemaphoreType.DMA((n_sems,)),
                 *[pltpu.HBM(a.shape, a.dtype) for a in (*srcs, *lands)], jax.ShapeDtypeStruct((8, 128), F32))
    res = pl.pallas_call(
        body, name=name, out_shape=out_shape, in_specs=[HBM_SPEC] * (n + m) + [ANY_SPEC] * k,
        out_specs=(SEM_SPEC, SEM_SPEC, *[HBM_SPEC] * (n + m), pl.BlockSpec(memory_space=pltpu.VMEM)),
        input_output_aliases={i: 2 + i for i in range(n + m)},
        compiler_params=pltpu.CompilerParams(has_side_effects=EFFECT),
    )(*[_hbm(a) for a in (*srcs, *lands)], *after)
    return res[0], res[1], list(res[2:2 + n]), list(res[2 + n:2 + n + m]), res[-1]


def _split_wait(name, send_sems, recv_sems, srcs, lands, after, make_copies):
    n, m = len(srcs), len(lands)

    def body(*refs):
        for cp in make_copies(refs[:n], refs[n:n + m], refs[n + m], refs[n + m + 1]):
            cp.wait_send()
            cp.wait_recv()

    res = pl.pallas_call(
        body, name=name, out_shape=tuple(pltpu.HBM(a.shape, a.dtype) for a in (*srcs, *lands)),
        in_specs=[HBM_SPEC] * (n + m) + [SEM_SPEC, SEM_SPEC] + [ANY_SPEC] * len(after), out_specs=tuple([HBM_SPEC] * (n + m)),
        input_output_aliases={i: i for i in range(n + m)},
        compiler_params=pltpu.CompilerParams(has_side_effects=EFFECT),
    )(*srcs, *lands, send_sems, recv_sems, *after)
    return list(res[:n]), list(res[n:])


def _place_own(shards, dev):
    n = len(shards)

    def kern(dev_ref, *refs):
        for x_ref, o_ref in zip(refs[:n], refs[n:]):
            o_ref[...] = x_ref[...].astype(o_ref.dtype)

    blocks = [(None, *a.shape[1:]) for a, _, _ in shards]
    nbytes = sum(_nbytes(b, a.dtype) + _nbytes(b, dt) for b, (a, _, dt) in zip(blocks, shards))
    return pl.pallas_call(
        kern, name="weights_place_own", out_shape=[pltpu.HBM((N_DEV, *a.shape[1:]), dt) for a, _, dt in shards],
        grid_spec=pltpu.PrefetchScalarGridSpec(
            num_scalar_prefetch=1, grid=(1,),
            in_specs=[pl.BlockSpec(b, functools.partial(lambda i, dev, l: (l, 0, 0), l=l)) for b, (_, l, _) in zip(blocks, shards)],
            out_specs=[pl.BlockSpec(b, lambda i, dev: (dev[0], 0, 0)) for b in blocks]),
        compiler_params=pltpu.CompilerParams(dimension_semantics=("arbitrary",), vmem_limit_bytes=_vmem(nbytes)),
    )(dev, *[_hbm(a) for a, _, _ in shards])


def _ag_first_copies(src_refs, out_refs, send_sems, recv_sems):
    x, y, c = _me()
    targets = [(x, y, 1 - c), (1 - x, y, c), (x, 1 - y, c), (1 - x, 1 - y, c)]
    return [pltpu.make_async_remote_copy(
        src_ref=out_refs[op].at[4 * x + 2 * y + c], dst_ref=out_refs[op].at[4 * x + 2 * y + c], send_sem=send_sems.at[4 * op + k],
        recv_sem=recv_sems.at[4 * op + k], device_id=to, device_id_type=MESH)
        for op in range(len(out_refs)) for k, to in enumerate(targets)]


def _ag_second_copies(src_refs, out_refs, send_sems, recv_sems):
    x, y, c = _me()
    chips = [(1 - x, y), (x, 1 - y), (1 - x, 1 - y)]
    return [pltpu.make_async_remote_copy(
        src_ref=out_refs[op].at[4 * cx + 2 * cy + c], dst_ref=out_refs[op].at[4 * cx + 2 * cy + c],
        send_sem=send_sems.at[3 * op + j], recv_sem=recv_sems.at[3 * op + j], device_id=(x, y, 1 - c), device_id_type=MESH)
        for op in range(len(out_refs)) for j, (cx, cy) in enumerate(chips)]


def _rs_sibling_copies(g_refs, out_refs, send_sems, recv_sems):
    x, y, c = _me()
    return [pltpu.make_async_remote_copy(
        src_ref=g_refs[op].at[k, 1 - c], dst_ref=out_refs[op].at[k], send_sem=send_sems.at[4 * op + k],
        recv_sem=recv_sems.at[4 * op + k], device_id=(x, y, 1 - c), device_id_type=MESH)
        for op in range(len(g_refs)) for k in range(4)]


def _rs_direct_copies(g_refs, land_refs, send_sems, recv_sems):
    x, y, c = _me()
    n = len(g_refs) // 2
    chips = [(1 - x, y), (x, 1 - y), (1 - x, 1 - y)]
    copies = []
    for op in range(n):
        g32, g16, from_sib, from_others = g_refs[op], g_refs[n + op], land_refs[op], land_refs[n + op]
        copies.append(pltpu.make_async_remote_copy(
            src_ref=g32.at[2 * x + y, 1 - c], dst_ref=from_sib, send_sem=send_sems.at[7 * op], recv_sem=recv_sems.at[7 * op],
            device_id=(x, y, 1 - c), device_id_type=MESH))
        for j, (cx, cy) in enumerate(chips):
            for s, cc in enumerate((c, 1 - c)):
                copies.append(pltpu.make_async_remote_copy(
                    src_ref=g16.at[2 * cx + cy, cc], dst_ref=from_others.at[2 * j + s], send_sem=send_sems.at[7 * op + 1 + 2 * j + s],
                    recv_sem=recv_sems.at[7 * op + 1 + 2 * j + s], device_id=(cx, cy, cc), device_id_type=MESH))
    return copies


def _rs_chip_copies(p_refs, out_refs, send_sems, recv_sems):
    x, y, c = _me()
    chips = [(1 - x, y), (x, 1 - y), (1 - x, 1 - y)]
    return [pltpu.make_async_remote_copy(
        src_ref=p_refs[op].at[2 * cx + cy], dst_ref=out_refs[op].at[j], send_sem=send_sems.at[3 * op + j],
        recv_sem=recv_sems.at[3 * op + j], device_id=(cx, cy, c), device_id_type=MESH)
        for op in range(len(p_refs)) for j, (cx, cy) in enumerate(chips)]


def _all_gather(placed):
    n = len(placed)

    def kern(*refs):
        in_refs, out_refs, (send_sems, recv_sems) = refs[:n], refs[n:2 * n], refs[2 * n:]
        x, y, c = _me()
        me, sibling = (x, y, c), (x, y, 1 - c)
        chips = [(1 - x, y), (x, 1 - y), (1 - x, 1 - y)]

        def copy(op, k, block, to, own=False):
            idx = 4 * block[0] + 2 * block[1] + block[2]
            return pltpu.make_async_remote_copy(
                src_ref=(in_refs if own else out_refs)[op].at[idx], dst_ref=out_refs[op].at[idx], send_sem=send_sems.at[7 * op + k],
                recv_sem=recv_sems.at[7 * op + k], device_id=to, device_id_type=MESH)

        first = []
        for op in range(n):
            first.append(copy(op, 0, me, sibling, own=True))
            first += [copy(op, 1 + j, me, (*chip, c), own=True) for j, chip in enumerate(chips)]
        for cp in first:
            cp.start()
        passed = []
        for j, chip in enumerate(chips):
            for op in range(n):
                copy(op, 1 + j, (*chip, c), me).wait_recv()
                passed.append(copy(op, 4 + j, (*chip, c), sibling))
                passed[-1].start()
        for op in range(n):
            copy(op, 0, sibling, me).wait_recv()
            for j, chip in enumerate(chips):
                copy(op, 4 + j, (*chip, 1 - c), me).wait_recv()
        for cp in first + passed:
            cp.wait_send()

    return pl.pallas_call(
        kern, name="weights_all_gather", out_shape=[pltpu.HBM(g.shape, g.dtype) for g in placed],
        in_specs=[ANY_SPEC] * n, out_specs=[ANY_SPEC] * n, input_output_aliases={i: i for i in range(n)},
        scratch_shapes=[pltpu.SemaphoreType.DMA((7 * n,)), pltpu.SemaphoreType.DMA((7 * n,))],
    )(*[_hbm(a) for a in placed])


def _row_tile(r, w, n_blocks):
    tr = r
    while tr > 8 and 2 * n_blocks * tr * w * 4 > 24 * 2**20:
        tr //= 2
    return tr


def _chip_sum(name, g, from_sibling, core):
    _, _, R, W = g.shape
    tr = _row_tile(R, W, 3)

    def kern(core_ref, g_ref, s_ref, o_ref):
        o_ref[...] = (g_ref[...] + s_ref[...]).astype(BF16)

    return pl.pallas_call(
        kern, name=name, out_shape=pltpu.HBM((4, R, W), BF16),
        grid_spec=pltpu.PrefetchScalarGridSpec(
            num_scalar_prefetch=1, grid=(4, R // tr),
            in_specs=[pl.BlockSpec((None, None, tr, W), lambda k, i, core: (k, core[0], i, 0)),
                      pl.BlockSpec((None, tr, W), lambda k, i, core: (k, i, 0))],
            out_specs=pl.BlockSpec((None, tr, W), lambda k, i, core: (k, i, 0))),
        compiler_params=pltpu.CompilerParams(dimension_semantics=("parallel", "parallel"), vmem_limit_bytes=_vmem(3 * tr * W * 4)),
    )(core, _hbm(g), _hbm(from_sibling))


def _adamw(w, g, m, v):
    m = ADAM_B1 * m + (1.0 - ADAM_B1) * g
    v = ADAM_B2 * v + (1.0 - ADAM_B2) * (g * g)
    m_hat = m / (1.0 - ADAM_B1 ** ADAM_STEP)
    v_hat = v / (1.0 - ADAM_B2 ** ADAM_STEP)
    return -ADAM_LR * (m_hat / (jnp.sqrt(v_hat) + ADAM_EPS) + ADAM_WD * w), m, v


def _finish_sharded(name, layers, w, m, v, where, deps=()):
    nl, R, W = w.shape
    n_other = layers[0][2].shape[0]
    tr = _row_tile(R, W, (8 + n_other) * nl)
    deps = _deps(deps)

    def kern(where_ref, *refs):
        w_ref, m_ref, v_ref = refs[3 * nl:3 * nl + 3]
        go_ref, d_ref, mo_ref, vo_ref = refs[3 * nl + 3 + len(deps):]
        for l in range(nl):
            g_ref, s_ref, c_ref = refs[3 * l:3 * l + 3]
            grad = g_ref[...] + s_ref[...]
            for j in range(n_other):
                grad = grad + c_ref[j].astype(F32)
            go_ref[l] = grad
            d_ref[l], mo_ref[l], vo_ref[l] = _adamw(w_ref[l], grad, m_ref[l], v_ref[l])

    row = pl.BlockSpec((nl, tr, W), lambda i, wh: (0, i, 0))
    in_specs, args = [], []
    for g, s, c in layers:
        sib = (pl.BlockSpec((None, tr, W), lambda i, wh: (wh[0], i, 0)) if s.ndim == 3 else pl.BlockSpec((tr, W), lambda i, wh: (i, 0)))
        in_specs += [pl.BlockSpec((None, None, tr, W), lambda i, wh: (wh[0], wh[1], i, 0)), sib,
                     pl.BlockSpec((n_other, tr, W), lambda i, wh: (0, i, 0))]
        args += [g, s, c]
    return pl.pallas_call(
        kern, name=name, out_shape=[pltpu.HBM((nl, R, W), F32)] * 4,
        grid_spec=pltpu.PrefetchScalarGridSpec(num_scalar_prefetch=1, grid=(R // tr,),
                                               in_specs=in_specs + [row, row, row] + [ANY_SPEC] * len(deps),
                                               out_specs=[row, row, row, row]),
        compiler_params=pltpu.CompilerParams(dimension_semantics=("parallel",),
                                             vmem_limit_bytes=_vmem(nl * (8 + n_other) * tr * W * 4)),
    )(where, *[_hbm(a) for a in (*args, w, m, v)], *deps)


SMALL_PLACE = (("mla_gq", 0, 0, 1, 256), ("mla_gkv", 0, 256, 1, 256), ("sgu_ln_g", 0, 512, 1, 512), ("sgu_ln_b", 1, 0, 1, 512),
               ("hg_lb", 2, 0, 2, 1024), ("ln1_g", 4, 0, 2, 1024), ("ln1_b", 6, 0, 2, 1024), ("sgu_b", 8, 0, 4, 128),
               ("ln2_g", 12, 0, 2, 1024), ("ln2_b", 14, 0, 2, 1024), ("hg_gnorm", 16, 0, 1, 1024))
SMALL_BUF_ROWS = 24
LOSS_ROW = 17


def _small_pack(gs, dev):
    pieces = [(gs["mla_gq"], 0, 0), (gs["mla_gkv"], 0, 256), (gs["sgu_ln_g"], 0, 512), (gs["sgu_ln_b"], 1, 0), (gs["hg_lb"], 2, 0),
              (gs["ln1_g0"], 4, 0), (gs["ln1_g1"], 5, 0), (gs["ln1_b0"], 6, 0), (gs["ln1_b1"], 7, 0), (gs["sgu_b"], 8, 0),
              (gs["ln2_g0"], 12, 0), (gs["ln2_g1"], 13, 0), (gs["ln2_b0"], 14, 0), (gs["ln2_b1"], 15, 0), (gs["hg_gnorm"], 16, 0),
              (gs["sq_err"], LOSS_ROW, 0)]
    n_p = len(pieces)

    def kern(dev_ref, *refs):
        a_ref, b_ref = refs[n_p + 1], refs[n_p + 2]
        a_ref[...] = jnp.zeros(a_ref.shape, F32)
        for ref, (_, r, l0) in zip(refs[:n_p], pieces):
            a_ref[r:r + ref.shape[0], l0:l0 + ref.shape[1]] = ref[...]
        b_ref[...] = refs[n_p][...]

    whole = lambda a: pl.BlockSpec(a.shape, functools.partial(lambda i, dev, nd: (0,) * nd, nd=a.ndim))
    return pl.pallas_call(
        kern, name="small_grads_pack",
        out_shape=[pltpu.HBM((N_DEV, SMALL_BUF_ROWS, D_MODEL), F32), pltpu.HBM((N_DEV, SGU_G, 128, 128), F32)],
        grid_spec=pltpu.PrefetchScalarGridSpec(
            num_scalar_prefetch=1, grid=(1,), in_specs=[whole(p[0]) for p in pieces] + [whole(gs["sgu_w"])],
            out_specs=[pl.BlockSpec((None, SMALL_BUF_ROWS, D_MODEL), lambda i, dev: (dev[0], 0, 0)),
                       pl.BlockSpec((None, SGU_G, 128, 128), lambda i, dev: (dev[0], 0, 0, 0))]),
    )(dev, *[p[0] for p in pieces], gs["sgu_w"])


def _small_copies(src_refs, land_refs, send_sems, recv_sems):
    px, py, pc = _me()
    me = 4 * px + 2 * py + pc
    return [pltpu.make_async_remote_copy(
        src_ref=land_refs[k].at[me], dst_ref=land_refs[k].at[me], send_sem=send_sems.at[2 * (r - 1) + k],
        recv_sem=recv_sems.at[2 * (r - 1) + k], device_id=(px ^ (r >> 2), py ^ ((r >> 1) & 1), pc ^ (r & 1)), device_id_type=MESH)
        for r in range(1, N_DEV) for k in range(2)]


def _small_adamw(slots_a, slots_b, given):
    names = [p[0] for p in SMALL_PLACE] + ["sgu_w"]
    n_names = len(names)
    wmv = [given[pre + name] for name in names for pre in ("", "m_", "v_")]
    vmem = pl.BlockSpec(memory_space=pltpu.VMEM)

    def kern(*refs):
        sum_a, sum_b = refs[0][0], refs[1][0]
        for d in range(1, N_DEV):
            sum_a, sum_b = sum_a + refs[0][d], sum_b + refs[1][d]
        wmv_refs, out_refs = refs[2:2 + 3 * n_names], refs[2 + 3 * n_names:]
        px, py, pc = _me()
        me = 4 * px + 2 * py + pc

        def own_block(full):
            acc = full[:, 0:128]
            for b in range(1, N_DEV):
                acc = jnp.where(me == b, full[:, b * 128:(b + 1) * 128], acc)
            return acc

        for idx, name in enumerate(names):
            w_ref, m_ref, v_ref = wmv_refs[3 * idx:3 * idx + 3]
            if name == "sgu_w":
                grad = sum_b[None]
            else:
                _, r, l0, nr, nl = SMALL_PLACE[idx]
                grad = sum_a[r:r + nr, l0:l0 + nl]
                if name == "hg_gnorm":
                    grad = own_block(grad)
                if name == "sgu_b":
                    grad = grad[None]
            res = (grad, *_adamw(w_ref[...], grad, m_ref[...], v_ref[...]))
            for o_ref, val in zip(out_refs[4 * idx:4 * idx + 4], res):
                o_ref[...] = val
        out_refs[4 * n_names][...] = (0.5 / D_MODEL) * jnp.sum(sum_a[LOSS_ROW:LOSS_ROW + 1, :], axis=1, keepdims=True)

    out_shape = [jax.ShapeDtypeStruct(given[name].shape, F32) for name in names for _ in range(4)]
    out_shape.append(jax.ShapeDtypeStruct((1, 1), F32))
    res = pl.pallas_call(
        kern, name="small_adamw", out_shape=out_shape, in_specs=[vmem] * (2 + len(wmv)), out_specs=[vmem] * len(out_shape),
    )(slots_a, slots_b, *wmv)
    out = {name: res[4 * idx:4 * idx + 4] for idx, name in enumerate(names)}
    out["loss"] = res[-1].reshape(())
    return out


class _Exchange:
    def __init__(self, given):
        self.given = given
        px, py, pc = _me()
        self.core = pc.reshape(1).astype(jnp.int32)
        self.dev = (4 * px + 2 * py + pc).reshape(1).astype(jnp.int32)
        self.where = jnp.stack([2 * px + py, pc]).astype(jnp.int32)
        self.state, self.layers = {}, {}

    def start_weights(self, lands, after):
        self.weights = _split_start("weights_first_start", [], lands, 4 * len(lands), _ag_first_copies, after=after)
        self.first_token = self.weights[4]

    def weights_forward(self, after):
        send_sems, recv_sems, shards, lands, _ = self.weights
        _, lands = _split_wait("weights_first_wait", send_sems, recv_sems, shards, lands, after, _ag_first_copies)
        self.weights = _split_start("weights_second_start", [], lands, 3 * len(lands), _ag_second_copies)
        return self.weights[4]

    def weights_ready(self, after):
        send_sems, recv_sems, shards, lands, _ = self.weights
        _, got = _split_wait("weights_second_wait", send_sems, recv_sems, shards, lands, after, _ag_second_copies)
        return dict(w_in_o=got[0], w_out_o=got[1], w_ff1=[got[2], got[3]], w_ff2=[got[4], got[5]])

    def small_start(self, gs):
        self.small = _split_start("small_grads_start", [], _small_pack(gs, self.dev), 14, _small_copies)
        return self.small[4]

    def small_finish(self, after):
        send_sems, recv_sems, _, lands, _ = self.small
        _, lands = _split_wait("small_grads_wait", send_sems, recv_sems, [], lands, after, _small_copies)
        return _small_adamw(lands[0], lands[1], self.given)

    def direct_start(self, tag, grads):
        f32 = [g[0].reshape(4, 2, *g[0].shape[1:]) for g in grads]
        bf16 = [g[1].reshape(4, 2, *g[1].shape[1:]) for g in grads]
        lands = [lax.empty(b.shape[2:], F32) for b in f32] + [lax.empty((6, *b.shape[2:]), BF16) for b in f32]
        self.state[tag] = _split_start(f"grads_{tag}_start", f32 + bf16, lands, 7 * len(grads), _rs_direct_copies)
        return self.state[tag][4]

    def direct_end(self, tag, after):
        send_sems, recv_sems, srcs, lands, _ = self.state[tag]
        srcs, lands = _split_wait(f"grads_{tag}_wait", send_sems, recv_sems, srcs, lands, after, _rs_direct_copies)
        n = len(lands) // 2
        self.layers[tag] = list(zip(srcs[:n], lands[:n], lands[n:]))

    def grads_start(self, tag, grads):
        blocks = [g.reshape(4, 2, *g.shape[1:]) for g in grads]
        lands = [lax.empty((4, *b.shape[2:]), F32) for b in blocks]
        self.state[tag] = _split_start(f"grads_{tag}_sibling_start", blocks, lands, 4 * len(blocks), _rs_sibling_copies)
        return self.state[tag][4]

    def grads_middle(self, tag, after):
        send_sems, recv_sems, blocks, lands, _ = self.state[tag]
        blocks, from_sibling = _split_wait(f"grads_{tag}_sibling_wait", send_sems, recv_sems, blocks, lands, [after], _rs_sibling_copies)
        sums = [_chip_sum(f"grads_{tag}_chip_sum_{k}", b, s, self.core) for k, (b, s) in enumerate(zip(blocks, from_sibling))]
        lands = [lax.empty((3, *p.shape[1:]), BF16) for p in sums]
        self.state[tag] = (blocks, from_sibling, _split_start(f"grads_{tag}_chips_start", sums, lands, 3 * len(sums), _rs_chip_copies))
        return self.state[tag][2][4]

    def grads_end(self, tag, after):
        blocks, from_sibling, (send_sems, recv_sems, sums, lands, _) = self.state[tag]
        after = list(after) if isinstance(after, (list, tuple)) else [after]
        _, from_chips = _split_wait(f"grads_{tag}_chips_wait", send_sems, recv_sems, sums, lands, after, _rs_chip_copies)
        self.layers[tag] = list(zip(blocks, from_sibling, from_chips))


def kernel(x, positions, w_in_e, mla_gq, mla_gkv, w_qb, w_kvb, sgu_ln_g, sgu_ln_b, sgu_w, sgu_b, w_out_e, w_in_o, hg_lb, hg_gnorm, w_out_o, ln1_g, ln1_b, w_ff1, w_ff2, ln2_g, ln2_b, loss_target, m_w_in_e, m_mla_gq, m_mla_gkv, m_w_qb, m_w_kvb, m_sgu_ln_g, m_sgu_ln_b, m_sgu_w, m_sgu_b, m_w_out_e, m_w_in_o, m_hg_lb, m_hg_gnorm, m_w_out_o, m_ln1_g, m_ln1_b, m_w_ff1, m_w_ff2, m_ln2_g, m_ln2_b, v_w_in_e, v_mla_gq, v_mla_gkv, v_w_qb, v_w_kvb, v_sgu_ln_g, v_sgu_ln_b, v_sgu_w, v_sgu_b, v_w_out_e, v_w_in_o, v_hg_lb, v_hg_gnorm, v_w_out_o, v_ln1_g, v_ln1_b, v_w_ff1, v_w_ff2, v_ln2_g, v_ln2_b):
    given = dict(locals())
    ex = _Exchange(given)

    names = ["w_in_e", "w_qb", "w_kvb", "w_out_e"]
    placed = _place_own([(given[n], 0, BF16) for n in names] + [(hg_gnorm.reshape(1, 1, D_MODEL // N_DEV), 0, F32)]
                        + [(w_in_o, 0, BF16), (w_out_o, 0, BF16), (w_ff1, 0, BF16), (w_ff1, 1, BF16), (w_ff2, 0, BF16), (w_ff2, 1, BF16)],
                        ex.dev)
    got = _all_gather(placed[:5])
    ex.start_weights(placed[5:], after=[got[0]])
    gw = dict(zip(names, got[:4]))
    small_names = ["mla_gq", "mla_gkv", "sgu_ln_g", "sgu_ln_b", "sgu_w", "sgu_b", "hg_lb", "ln1_g", "ln1_b", "ln2_g", "ln2_b"]
    sp = {n: given[n] for n in small_names}
    sp["hg_gnorm"] = got[4].reshape(1, D_MODEL)

    _, dx, grads, gs = _local_step(x[0], positions[0], loss_target[0], gw, sp, ex)

    def finish(n, layers, deps=()):
        return _finish_sharded(f"finish_{n}", layers, given[n], given["m_" + n], given["v_" + n], ex.where, deps=deps)

    ex.direct_end("l1", after=[dx])
    ex.direct_end("l0m", after=[dx])
    l1, l0m = ex.layers["l1"], ex.layers["l0m"]
    results = {}
    token = ex.grads_start("l0s", [grads[n] for n in names])
    results["w_ff1"] = finish("w_ff1", [l0m[0], l1[0]], deps=[token])
    token = ex.grads_middle("l0s", after=results["w_ff1"][0])
    results["w_ff2"] = finish("w_ff2", [l0m[1], l1[1]], deps=[token])
    results["w_in_o"] = finish("w_in_o", [l1[2]], deps=[token])
    results["w_out_o"] = finish("w_out_o", [l1[3]], deps=[token])
    results.update(ex.small_finish(after=[results["w_in_o"][0]]))
    ex.grads_end("l0s", after=[results[n][0] for n in ("mla_gq", "w_ff2", "w_in_o", "w_out_o")])
    for n, layer in zip(names, ex.layers["l0s"]):
        results[n] = finish(n, [layer])

    order = ["w_in_e", "mla_gq", "mla_gkv", "w_qb", "w_kvb", "sgu_ln_g", "sgu_ln_b", "sgu_w", "sgu_b", "w_out_e", "w_in_o",
             "hg_lb", "hg_gnorm", "w_out_o", "ln1_g", "ln1_b", "w_ff1", "w_ff2", "ln2_g", "ln2_b"]
    return (results["loss"], dx[None], *[results[name][kind] for kind in range(4) for name in order])
```

```python
import functools
import math

import jax
import jax.numpy as jnp
import numpy as np
from jax import lax
from jax.experimental import pallas as pl
from jax.experimental.pallas import tpu as pltpu

F32 = jnp.float32
BF16 = jnp.bfloat16
MESH = pl.DeviceIdType.MESH
HIGHEST = lax.Precision.HIGHEST

D_MODEL = 1024
D_FF = 4096
N_DEV = 8
HEADS = 8
HEAD_W = 128
MLA_NOPE = 64
MLA_ROPE = 32
MLA_V = 64
MLA_LORA = 256
MLA_SCALE = (MLA_NOPE + MLA_ROPE) ** -0.5
ROPE_BASE = 10000.0
SGU_DIM = 512
SGU_G = 4
SGU_CHUNK = 128
HG_CHUNK = 64
HG_CHUNKS_PER_STEP = 4
ALPHA = (2 * 2) ** 0.25
EPS = 1e-5
ADAM_LR, ADAM_B1, ADAM_B2, ADAM_EPS, ADAM_WD, ADAM_STEP = 0.001, 0.9, 0.999, 1e-08, 0.01, 10

VMEM_CAP_V7X = 56 * 2**20
VMEM_SLACK = 12 * 2**20
TM = 512
TN = 512


def _vmem(block_bytes):
    return int(min(VMEM_CAP_V7X, 2 * block_bytes + VMEM_SLACK))


def _hbm(a):
    return pltpu.with_memory_space_constraint(a, pltpu.HBM)


def _nbytes(shape, dtype):
    return int(np.prod([d for d in shape if d is not None])) * jnp.dtype(dtype).itemsize


def _sig(x):
    return 1.0 / (1.0 + jnp.exp(-x))


def _gelu(x):
    c = math.sqrt(2.0 / math.pi)
    t = jnp.tanh(c * (x + 0.044715 * x * x * x))
    return 0.5 * x * (1.0 + t), t


def _gelu_grad(x, t):
    c = math.sqrt(2.0 / math.pi)
    return 0.5 * (1.0 + t) + 0.5 * x * (1.0 - t * t) * c * (1.0 + 3 * 0.044715 * x * x)


def _dot(a, b, dims, precision=None):
    return lax.dot_general(a, b, (dims, ((), ())), preferred_element_type=F32, precision=precision)


NN = ((1,), (0,))
NT = ((1,), (1,))
TN_ = ((0,), (0,))


def _deps(deps):
    return [d for d in deps if d is not None]


def _tiled(name, grid, ins, outs, compute, direct=False, deps=()):
    n_in, deps = len(ins), _deps(deps)
    n_skip = n_in + len(deps)

    def kern(*refs):
        if direct:
            compute(refs[:n_in], refs[n_skip:])
            return
        for o_ref, r in zip(refs[n_skip:], compute(*refs[:n_in])):
            o_ref[...] = r.astype(o_ref.dtype).reshape(o_ref.shape)

    swap = lambda f: (lambda j, i: f(i, j))
    nbytes = sum(_nbytes(blk, a.dtype) for a, blk, _ in ins) + sum(_nbytes(blk, dt) + _nbytes(blk, F32) for _, dt, blk, _ in outs)
    res = pl.pallas_call(
        kern, name=name, grid=grid,
        in_specs=[pl.BlockSpec(blk, swap(f), pipeline_mode=pl.Buffered(1) if tuple(blk) == tuple(a.shape) else None)
                  for a, blk, f in ins] + [ANY_SPEC] * len(deps),
        out_specs=[pl.BlockSpec(blk, swap(f)) for _, _, blk, f in outs],
        out_shape=[pltpu.HBM(shape, dt) for shape, dt, _, _ in outs],
        compiler_params=pltpu.CompilerParams(dimension_semantics=("parallel", "parallel"), vmem_limit_bytes=_vmem(nbytes)),
    )(*[_hbm(a) for a, _, _ in ins], *deps)
    return res if len(res) > 1 else res[0]


def _rb(a, tm, w=None, cb=0):
    return (a, (tm, a.shape[1] if w is None else w), lambda i, j: (i, cb))


def _cw(b, tn):
    return (b, (b.shape[0], tn), lambda i, j: (0, j))


def _tl(a, tm):
    return (a, (a.shape[0], tm), lambda i, j: (0, i))


def _out(m, n, dtype, tm, tn):
    return ((m, n), dtype, (tm, tn), lambda i, j: (i, j))


def _out_dev(k, n, tm, dtype=F32):
    return ((N_DEV, k, n), dtype, (None, tm, n), lambda i, j: (j, i, 0))


def _twice(acc):
    return acc, acc


def _mmc(dims, n_pairs=1, epilogue=None):
    def compute(*refs):
        acc = None
        for k in range(n_pairs):
            d = _dot(refs[2 * k][...].astype(BF16), refs[2 * k + 1][...].astype(BF16), dims)
            acc = d if acc is None else acc + d
        ext = [r[...] for r in refs[2 * n_pairs:]]
        return epilogue(acc, *ext) if epilogue is not None else (acc,)

    return compute


def _res(w):
    return (w, w.shape, functools.partial(lambda i, j, nd: (0,) * nd, nd=w.ndim))


def _mmc_blocks(nblk, dims, rhs_block, epilogue=None):
    def compute(in_refs, out_refs):
        a = in_refs[0][...].astype(BF16)
        for d in range(nblk):
            acc = _dot(a, rhs_block(in_refs[1], d).astype(BF16), dims)
            n = acc.shape[1]
            ext = [r[:, d * n:(d + 1) * n] for r in in_refs[2:]]
            res = epilogue(acc, *ext) if epilogue is not None else (acc,)
            for o_ref, r in zip(out_refs, res):
                o_ref[:, d * n:(d + 1) * n] = r.astype(o_ref.dtype)

    return compute


def _rowwise(name, body, rows, consts, out_rows, out_accs=(), tr=512, deps=()):
    T = rows[0][0].shape[0]
    tr = min(tr, T)
    deps = _deps(deps)
    nr, ncn, no, nd = len(rows), len(consts), len(out_rows), len(deps)

    def kern(*refs):
        accs = refs[nr + ncn + nd + no:]
        if accs:
            @pl.when(pl.program_id(0) == 0)
            def _():
                for a in accs:
                    a[...] = jnp.zeros(a.shape, a.dtype)
        body(refs[:nr], refs[nr:nr + ncn], refs[nr + ncn + nd:nr + ncn + nd + no], accs)

    in_specs = [pl.BlockSpec((tr, w), functools.partial(lambda i, cb: (i, cb), cb=cb)) for _, w, cb in rows]
    in_specs += [pl.BlockSpec(c.shape, functools.partial(lambda i, nd: (0,) * nd, nd=c.ndim), pipeline_mode=pl.Buffered(1))
                 for c in consts]
    in_specs += [ANY_SPEC] * nd
    out_specs = [pl.BlockSpec((tr, w), lambda i: (i, 0)) for w, _ in out_rows]
    out_specs += [pl.BlockSpec(s, functools.partial(lambda i, nd: (0,) * nd, nd=len(s))) for s, _ in out_accs]
    out_shape = [pltpu.HBM((T, w), dt) for w, dt in out_rows]
    out_shape += [pltpu.HBM(s, dt) for s, dt in out_accs]
    nbytes = sum(_nbytes((tr, w), a.dtype) for a, w, _ in rows) + sum(_nbytes(c.shape, c.dtype) for c in consts)
    nbytes += sum(_nbytes((tr, w), dt) for w, dt in out_rows) + sum(_nbytes(s, dt) for s, dt in out_accs)
    res = pl.pallas_call(
        kern, name=name, grid=(T // tr,), in_specs=in_specs, out_specs=out_specs, out_shape=out_shape,
        compiler_params=pltpu.CompilerParams(dimension_semantics=("arbitrary",), vmem_limit_bytes=_vmem(nbytes)),
    )(*[_hbm(a) for a, _, _ in rows], *[_hbm(c) for c in consts], *deps)
    return res if len(res) > 1 else res[0]


def _full(a):
    return (a, a.shape[1], 0)


def _ln_stats(y):
    mu = jnp.mean(y, axis=-1, keepdims=True)
    yc = y - mu
    r = lax.rsqrt(jnp.mean(yc * yc, axis=-1, keepdims=True) + EPS)
    return yc * r, r


def _row_halves(n):
    return [slice(0, n // 2), slice(n // 2, n)] if n >= 256 else [slice(0, n)]


def _ln_back(dh, xh, r, gain, dg_ref, db_ref):
    dg_ref[...] += jnp.sum(dh * xh, axis=0, keepdims=True)
    db_ref[...] += jnp.sum(dh, axis=0, keepdims=True)
    dx = dh * gain
    return r * (dx - jnp.mean(dx, axis=-1, keepdims=True) - xh * jnp.mean(dx * xh, axis=-1, keepdims=True))


def _proj_ln(name, acts, weights, h_in, g, b, layer, deps=()):
    n = len(acts)

    def body(rows, consts, outs, accs):
        acc = None
        for k in range(n):
            d = _dot(rows[k][...].astype(BF16), consts[k][...], NN)
            acc = d if acc is None else acc + d
        y = ALPHA * rows[n][...] + acc
        xh, _ = _ln_stats(y)
        h = xh * consts[n][layer:layer + 1, :] + consts[n + 1][layer:layer + 1, :]
        outs[0][...] = y
        outs[1][...] = h
        outs[2][...] = h.astype(BF16)

    return _rowwise(name, body, [_full(a) for a in acts] + [_full(h_in)], [*weights, g, b],
                    [(D_MODEL, F32), (D_MODEL, F32), (D_MODEL, BF16)], tr=TM, deps=deps)


def _proj_ln_loss(name, act, w2, h_in, g, b, layer, target):
    def body(rows, consts, outs, accs):
        y = ALPHA * rows[1][...] + _dot(rows[0][...], consts[0][...], NN)
        xh, r = _ln_stats(y)
        gain = consts[1][layer:layer + 1, :]
        err = xh * gain + consts[2][layer:layer + 1, :] - rows[2][...]
        accs[0][...] += jnp.sum(err * err, axis=0, keepdims=True)
        dy = _ln_back(err * (1.0 / D_MODEL), xh, r, gain, accs[1], accs[2])
        outs[0][...] = dy
        outs[1][...] = dy.astype(BF16)

    return _rowwise(name, body, [_full(act), _full(h_in), _full(target)], [w2, g, b], [(D_MODEL, F32), (D_MODEL, BF16)],
                    [((1, D_MODEL), F32)] * 3, tr=TM)


def _dh_ln_back(name, da, w, dy_next, y, g, layer, proj=(), deps=()):
    def body(rows, consts, outs, accs):
        n = consts[0].shape[2]
        for sl in _row_halves(rows[0].shape[0]):
            acc = ALPHA * rows[1][sl, :]
            for d in range(N_DEV):
                acc = acc + _dot(rows[0][sl, d * n:(d + 1) * n], consts[0][d], NT)
            xh, r = _ln_stats(rows[2][sl, :])
            dy = _ln_back(acc, xh, r, consts[1][layer:layer + 1, :], accs[0], accs[1])
            outs[0][sl, :] = dy
            dy_bf = dy.astype(BF16)
            outs[1][sl, :] = dy_bf
            off = 0
            for k, p in enumerate(proj):
                outs[2][sl, off:off + p.shape[0]] = _dot(dy_bf, consts[2 + k][...], NT).astype(BF16)
                off += p.shape[0]

    out_rows = [(D_MODEL, F32), (D_MODEL, BF16)] + ([(sum(p.shape[0] for p in proj), BF16)] if proj else [])
    return _rowwise(name, body, [_full(da), _full(dy_next), _full(y)], [w, g, *proj], out_rows,
                    [((1, D_MODEL), F32)] * 2, tr=TM, deps=deps)


def _relu2_epilogue(acc):
    a = jnp.maximum(acc, 0.0)
    return acc, a * a


def _mlp_up(tag, h_bf, w1):
    T = h_bf.shape[0]
    tm = min(TM, T)
    return _tiled(f"{tag}_ff1", (1, T // tm), [_rb(h_bf, tm), _res(w1)],
                  [_out(T, D_FF, BF16, tm, D_FF), _out(T, D_FF, BF16, tm, D_FF)],
                  _mmc_blocks(N_DEV, NN, lambda w, d: w[d], epilogue=_relu2_epilogue), direct=True)


def _mlp_bwd_w(tag, h_bf, a, act, dff_bf, w2, deps=()):
    T = h_bf.shape[0]
    tm = min(TM, T)
    da = _tiled(f"{tag}_dact", (1, T // tm), [_rb(dff_bf, tm), _res(w2), _rb(a, tm)], [_out(T, D_FF, BF16, tm, D_FF)],
                _mmc_blocks(N_DEV, NT, lambda w, d: w[d], epilogue=lambda acc, a_t: (acc * 2.0 * jnp.maximum(a_t.astype(F32), 0.0),)),
                direct=True, deps=deps)
    dw2 = _tiled(f"{tag}_dw2", (1, D_FF // TM), [_tl(act, TM), _res(dff_bf)],
                 [_out(D_FF, D_MODEL, F32, TM, D_MODEL), _out(D_FF, D_MODEL, BF16, TM, D_MODEL)], _mmc(TN_, epilogue=_twice))
    dw1 = _tiled(f"{tag}_dw1", (N_DEV, 1), [_res(h_bf), _cw(da, TN)],
                 [_out_dev(D_MODEL, TN, D_MODEL), _out_dev(D_MODEL, TN, D_MODEL, BF16)], _mmc(TN_, epilogue=_twice))
    return da, dw1, [a.reshape(N_DEV, D_FF // N_DEV, D_MODEL) for a in dw2]


def _rope_tables(positions_col, invf_lane):
    def body(rows, consts, outs, accs):
        ang = rows[0][...].astype(F32) * consts[0][...]
        c, s = jnp.cos(ang), jnp.sin(ang)
        lane = lax.broadcasted_iota(jnp.int32, ang.shape, 1)
        outs[0][...] = jnp.where(lane < 64, 1.0, jnp.where(lane < 96, c, 0.0))
        outs[1][...] = jnp.where((lane >= 64) & (lane < 80), -s, 0.0)
        outs[2][...] = jnp.where((lane >= 80) & (lane < 96), s, 0.0)

    return _rowwise("rope_tables", body, [_full(positions_col)], [invf_lane], [(HEAD_W, F32)] * 3)


def _rope(x, c, s1, s2):
    return x * c + pltpu.roll(x, 112, 1) * s1 + pltpu.roll(x, 16, 1) * s2


def _rope_t(dx, c, s1, s2):
    return dx * c + pltpu.roll(dx * s1, 16, 1) + pltpu.roll(dx * s2, 112, 1)


def _rms(c):
    r = lax.rsqrt(jnp.mean(c * c, axis=-1, keepdims=True) + EPS)
    return c * r, r


def _rope_heads(x, c, s1, s2, fn):
    return jnp.concatenate([fn(x[:, h * HEAD_W:(h + 1) * HEAD_W], c, s1, s2) for h in range(HEADS)], axis=1)


def _mla_in(x, wm, ws, tabs, gq, gkv, deps=()):
    def body(rows, consts, outs, accs):
        xb = rows[0][...].astype(BF16)
        zm = _dot(xb, consts[0][...], NT)
        outs[0][...] = zm
        outs[1][...] = _dot(xb, consts[1][...], NT)
        outs[2][...] = (_rms(zm[:, 0:256])[0] * consts[2][...]).astype(BF16)
        outs[3][...] = (_rms(zm[:, 256:512])[0] * consts[3][...]).astype(BF16)
        outs[4][...] = _rope(zm[:, 512:640], rows[1][...], rows[2][...], rows[3][...])

    return _rowwise("l0_in", body, [_full(x)] + [_full(t) for t in tabs], [wm, ws, gq, gkv],
                    [(640, F32), (1024, F32), (256, BF16), (256, BF16), (HEAD_W, F32)], deps=deps)


def _mla_qkv(cqn, ckvn, kr_rot, tabs, wq, wk, wv):
    def body(rows, consts, outs, accs):
        c, s1, s2 = rows[3][...], rows[4][...], rows[5][...]
        outs[0][...] = _rope_heads(_dot(rows[0][...], consts[0][...], NN), c, s1, s2, _rope).astype(BF16)
        outs[1][...] = (_dot(rows[1][...], consts[1][...], NN) + jnp.concatenate([rows[2][...]] * HEADS, axis=1)).astype(BF16)
        outs[2][...] = _dot(rows[1][...], consts[2][...], NN).astype(BF16)

    rows = [_full(cqn), _full(ckvn), _full(kr_rot)] + [_full(t) for t in tabs]
    return _rowwise("l0_qkv", body, rows, [wq, wk, wv], [(HEADS * HEAD_W, BF16)] * 3)


def _mla_back(zm, cqn, ckvn, tabs, gq, gkv, wq, wk, wv, dq, dk, dv):
    def body(rows, consts, outs, accs):
        c, s1, s2 = rows[4][...], rows[5][...], rows[6][...]
        dk_t, dv_bf = rows[8][...], rows[9][...].astype(BF16)
        dq_bf = _rope_heads(rows[7][...], c, s1, s2, _rope_t).astype(BF16)
        dk_bf = dk_t.astype(BF16)
        accs[0][...] += _dot(rows[2][...], dq_bf, TN_)
        accs[1][...] += _dot(rows[3][...], dk_bf, TN_)
        accs[2][...] += _dot(rows[3][...], dv_bf, TN_)
        dlat = [_dot(dq_bf, consts[2][...], NT), _dot(dk_bf, consts[3][...], NT) + _dot(dv_bf, consts[4][...], NT)]
        for k in range(2):
            ch, r = _rms(rows[k][...])
            accs[3 + k][...] += jnp.sum(dlat[k] * ch, axis=0, keepdims=True)
            dc = dlat[k] * consts[k][...]
            outs[0][:, 256 * k:256 * (k + 1)] = (r * (dc - ch * jnp.mean(dc * ch, axis=-1, keepdims=True))).astype(BF16)
        dks = dk_t[:, 0:HEAD_W]
        for h in range(1, HEADS):
            dks = dks + dk_t[:, h * HEAD_W:(h + 1) * HEAD_W]
        lane = lax.broadcasted_iota(jnp.int32, dks.shape, 1)
        dks = jnp.where((lane >= 64) & (lane < 96), dks, 0.0)
        outs[0][:, 512:640] = _rope_t(dks, c, s1, s2).astype(BF16)

    rows = [(zm, 256, 0), (zm, 256, 1), _full(cqn), _full(ckvn)] + [_full(t) for t in tabs] + [_full(dq), _full(dk), _full(dv)]
    wide = HEADS * HEAD_W
    return _rowwise("l0_mla_back", body, rows, [gq, gkv, wq, wk, wv], [(640, BF16)],
                    [((MLA_LORA, wide), F32)] * 3 + [((1, MLA_LORA), F32)] * 2, tr=256)


def _in_back(x, dzm, dzs, dy, wm, ws, deps=()):
    def body(rows, consts, outs, accs):
        dzm_t, dzs_t = rows[1][...], rows[2][...]
        outs[0][...] = _dot(dzm_t, consts[0][...], NN) + _dot(dzs_t, consts[1][...], NN) + ALPHA * rows[3][...]
        xb = rows[0][...].astype(BF16)
        accs[0][...] += _dot(dzm_t, xb, TN_)
        accs[1][...] += _dot(dzs_t, xb, TN_)

    return _rowwise("l0_in_back", body, [_full(x), _full(dzm), _full(dzs), _full(dy)], [wm, ws], [(D_MODEL, F32)],
                    [((640, D_MODEL), F32), ((1024, D_MODEL), F32)], deps=deps)


def _out_weight_grads(o_att, b_out, dy_bf):
    def body(rows, consts, outs, accs):
        d = rows[2][...]
        accs[0][...] += _dot(rows[0][...].astype(BF16), d, TN_)
        accs[1][...] += _dot(rows[1][...], d, TN_)

    return _rowwise("l0_dw_out", body, [_full(o_att), _full(b_out), _full(dy_bf)], [], [],
                    [((HEADS * HEAD_W, D_MODEL), F32), ((SGU_DIM, D_MODEL), F32)])


def _attn_block(T):
    return min(1024, T)


def _attn_fwd(q, k, v):
    T = q.shape[0]
    BQ = _attn_block(T)
    nq = T // BQ

    def kern(q_ref, k_ref, v_ref, o_ref, lse_ref):
        def step(i, j, carry, masked):
            m, l, acc = carry
            qb = q_ref[pl.ds(pl.multiple_of(i * BQ, BQ), BQ), :]
            kb = k_ref[pl.ds(pl.multiple_of(j * BQ, BQ), BQ), :]
            vb = v_ref[pl.ds(pl.multiple_of(j * BQ, BQ), BQ), :]
            s = _dot(qb, kb, NT) * MLA_SCALE
            if masked:
                row = lax.broadcasted_iota(jnp.int32, s.shape, 0)
                col = lax.broadcasted_iota(jnp.int32, s.shape, 1)
                s = jnp.where(col <= row, s, -1e30)
            m_new = jnp.maximum(m, jnp.max(s, axis=-1, keepdims=True))
            p = jnp.exp(s - m_new)
            a = jnp.exp(m - m_new)
            l = a * l + jnp.sum(p, axis=-1, keepdims=True)
            acc = a * acc + _dot(p.astype(BF16), vb, NN)
            return m_new, l, acc

        def qloop(i, _):
            init = (jnp.full((BQ, 1), -1e30, F32), jnp.zeros((BQ, 1), F32), jnp.zeros((BQ, HEAD_W), F32))
            carry = lax.fori_loop(0, i, lambda j, c: step(i, j, c, False), init)
            m, l, acc = step(i, i, carry, True)
            rows = pl.ds(pl.multiple_of(i * BQ, BQ), BQ)
            o_ref[rows, :] = acc / l
            lse_ref[0, rows, :] = m + jnp.log(l)
            return 0

        lax.fori_loop(0, nq, qloop, 0)

    head = pl.BlockSpec((T, HEAD_W), lambda h: (0, h))
    nbytes = 3 * _nbytes((T, HEAD_W), BF16) + _nbytes((T, HEAD_W), F32) + _nbytes((T, 128), F32)
    return pl.pallas_call(
        kern, name="attn_fwd", grid=(HEADS,), in_specs=[head, head, head],
        out_specs=[head, pl.BlockSpec((1, T, 1), lambda h: (h, 0, 0))],
        out_shape=[pltpu.HBM((T, HEADS * HEAD_W), F32), pltpu.HBM((HEADS, T, 1), F32)],
        compiler_params=pltpu.CompilerParams(dimension_semantics=("parallel",), vmem_limit_bytes=_vmem(nbytes)),
    )(_hbm(q), _hbm(k), _hbm(v))


def _attn_bwd(q, k, v, o, lse, dcat, deps=()):
    T = q.shape[0]
    BQ = _attn_block(T)
    nq = T // BQ
    deps = _deps(deps)

    def kern(q_ref, k_ref, v_ref, o_ref, lse_ref, do_ref, *rest):
        dq_ref, dk_ref, dv_ref, dd_ref = rest[len(deps):]
        dq_ref[...] = jnp.zeros(dq_ref.shape, F32)

        def dloop(i, _):
            rows = pl.ds(pl.multiple_of(i * BQ, BQ), BQ)
            dd_ref[rows, :] = jnp.sum(do_ref[rows, :].astype(F32) * o_ref[rows, :], axis=-1, keepdims=True)
            return 0

        lax.fori_loop(0, nq, dloop, 0)

        def tile(q0, k0, n, carry, masked):
            dk_acc, dv_acc = carry
            rq = pl.ds(pl.multiple_of(q0, n), n)
            rk = pl.ds(pl.multiple_of(k0, n), n)
            qb, kb, vb, dob = q_ref[rq, :], k_ref[rk, :], v_ref[rk, :], do_ref[rq, :]
            s = _dot(qb, kb, NT) * MLA_SCALE
            p = jnp.exp(s - lse_ref[0, rq, :])
            if masked:
                row = lax.broadcasted_iota(jnp.int32, s.shape, 0)
                col = lax.broadcasted_iota(jnp.int32, s.shape, 1)
                p = jnp.where(col <= row, p, 0.0)
            dp = _dot(dob, vb, NT)
            ds = (p * (dp - dd_ref[rq, :]) * MLA_SCALE).astype(BF16)
            dv_acc = dv_acc + _dot(p.astype(BF16), dob, TN_)
            dk_acc = dk_acc + _dot(ds, qb, TN_)
            dq_ref[rq, :] += _dot(ds, kb, NN)
            return dk_acc, dv_acc

        def kloop(j, _):
            base, half = j * BQ, BQ // 2
            zero = (jnp.zeros((half, HEAD_W), F32), jnp.zeros((half, HEAD_W), F32))
            early = tile(base + half, base, half, tile(base, base, half, zero, True), False)
            late = tile(base + half, base + half, half, zero, True)
            carry = tuple(jnp.concatenate([a, b], axis=0) for a, b in zip(early, late))
            dk_acc, dv_acc = lax.fori_loop(j + 1, nq, lambda i, c: tile(i * BQ, base, BQ, c, False), carry)
            rk = pl.ds(pl.multiple_of(j * BQ, BQ), BQ)
            dk_ref[rk, :] = dk_acc
            dv_ref[rk, :] = dv_acc
            return 0

        lax.fori_loop(0, nq, kloop, 0)

    head = pl.BlockSpec((T, HEAD_W), lambda h: (0, h))
    nbytes = 4 * _nbytes((T, HEAD_W), BF16) + 5 * _nbytes((T, HEAD_W), F32) + 2 * _nbytes((T, 128), F32)
    return pl.pallas_call(
        kern, name="attn_bwd", grid=(HEADS,),
        in_specs=[head, head, head, head, pl.BlockSpec((1, T, 1), lambda h: (h, 0, 0)), head] + [ANY_SPEC] * len(deps),
        out_specs=[head, head, head],
        out_shape=[pltpu.HBM((T, HEADS * HEAD_W), F32)] * 3,
        scratch_shapes=[pltpu.VMEM((T, 1), F32)],
        compiler_params=pltpu.CompilerParams(dimension_semantics=("parallel",), vmem_limit_bytes=_vmem(nbytes)),
    )(*[_hbm(a) for a in (q, k, v, o, lse, dcat)], *deps)


def _sgu_common(u, v, ln_g, ln_b):
    ua, tu = _gelu(u)
    va, tv = _gelu(v)
    vh, r = _ln_stats(va)
    return ua, tu, tv, vh, r, vh * ln_g + ln_b


def _tril_mask(n):
    return lax.broadcasted_iota(jnp.int32, (n, n), 1) <= lax.broadcasted_iota(jnp.int32, (n, n), 0)


def _sgu_fwd(zs, ln_g, ln_b, w, bias_full):
    def body(rows, consts, outs, accs):
        ua, _, _, _, _, vn = _sgu_common(rows[0][...], rows[1][...], consts[0][...], consts[1][...])
        vn = vn.astype(BF16)
        tri = _tril_mask(SGU_CHUNK)
        for g in range(SGU_G):
            wg = jnp.where(tri, consts[2][0, g], 0.0).astype(BF16)
            cols = slice(g * 128, (g + 1) * 128)
            for c in range(ua.shape[0] // SGU_CHUNK):
                rws = slice(c * SGU_CHUNK, (c + 1) * SGU_CHUNK)
                mixed = _dot(wg, vn[rws, cols], NN) + consts[3][:, cols]
                outs[0][rws, cols] = (ua[rws, cols] * mixed).astype(BF16)

    return _rowwise("sgu_fwd", body, [(zs, 512, 0), (zs, 512, 1)], [ln_g, ln_b, w, bias_full], [(SGU_DIM, BF16)])


def _sgu_bwd(zs, dcat, ln_g, ln_b, w, bias_full):
    def body(rows, consts, outs, accs):
        u, v = rows[0][...], rows[1][...]
        ua, tu, tv, vh, r, vn = _sgu_common(u, v, consts[0][...], consts[1][...])
        dout = rows[2][...].astype(F32)
        vn_bf = vn.astype(BF16)
        tri = _tril_mask(SGU_CHUNK)
        dmixed = (dout * ua)
        dmixed_bf = dmixed.astype(BF16)
        ones = jnp.ones((8, SGU_CHUNK), F32)
        dvn_cols, mixed_cols = [], []
        for g in range(SGU_G):
            wg = jnp.where(tri, consts[2][0, g], 0.0).astype(BF16)
            cols = slice(g * 128, (g + 1) * 128)
            dvn_rows, mixed_rows = [], []
            dw = jnp.zeros((SGU_CHUNK, SGU_CHUNK), F32)
            dmix_sum = jnp.zeros((SGU_CHUNK, 128), F32)
            for c in range(u.shape[0] // SGU_CHUNK):
                rws = slice(c * SGU_CHUNK, (c + 1) * SGU_CHUNK)
                mixed_rows.append(_dot(wg, vn_bf[rws, cols], NN) + consts[3][:, cols])
                dvn_rows.append(_dot(wg, dmixed_bf[rws, cols], TN_))
                dw = dw + _dot(dmixed_bf[rws, cols], vn_bf[rws, cols], NT)
                dmix_sum = dmix_sum + dmixed[rws, cols]
            accs[0][g] += jnp.where(tri, dw, 0.0)
            accs[3][g:g + 1, :] += _dot(ones, dmix_sum, NT, precision=HIGHEST)[0:1, :]
            dvn_cols.append(jnp.concatenate(dvn_rows, axis=0))
            mixed_cols.append(jnp.concatenate(mixed_rows, axis=0))
        dvn = jnp.concatenate(dvn_cols, axis=1)
        mixed = jnp.concatenate(mixed_cols, axis=1)
        accs[1][...] += jnp.sum(dvn * vh, axis=0, keepdims=True)
        accs[2][...] += jnp.sum(dvn, axis=0, keepdims=True)
        dvh = dvn * consts[0][...]
        dva = r * (dvh - jnp.mean(dvh, axis=-1, keepdims=True) - vh * jnp.mean(dvh * vh, axis=-1, keepdims=True))
        outs[0][:, 0:512] = (dout * mixed * _gelu_grad(u, tu)).astype(BF16)
        outs[0][:, 512:1024] = (dva * _gelu_grad(v, tv)).astype(BF16)

    return _rowwise("sgu_bwd", body, [(zs, 512, 0), (zs, 512, 1), (dcat, 512, 2)], [ln_g, ln_b, w, bias_full], [(1024, BF16)],
                    [((SGU_G, 128, 128), F32), ((1, SGU_DIM), F32), ((1, SGU_DIM), F32), ((SGU_G, 128), F32)], tr=256)


def _lower_bound(hg_lb):
    a0, a1 = hg_lb[0:1, :], hg_lb[1:2, :]
    m = jnp.maximum(a0, a1)
    e0, e1 = jnp.exp(a0 - m), jnp.exp(a1 - m)
    s0, s1 = e0 / (e0 + e1), e1 / (e0 + e1)
    return (s0 + s1) - s0, s0, s1


def _prefix_rows(x, reverse=False):
    n = x.shape[0]
    row = lax.broadcasted_iota(jnp.int32, x.shape, 0)
    s = 1
    while s < n:
        if reverse:
            x = x + jnp.where(row < n - s, pltpu.roll(x, n - s, 0), 0.0)
        else:
            x = x + jnp.where(row >= s, pltpu.roll(x, s, 0), 0.0)
        s *= 2
    return x


def _hg_gates(qr, fr, lb):
    C = qr.shape[0]
    sq = _sig(qr)
    qf = qr * sq
    sf = _sig(fr)
    gate = lb + (1.0 - lb) * sf
    kk = 1.0 - gate
    tri = _tril_mask(C)
    b = _prefix_rows(jnp.log(gate))
    bref = b[C // 2 - 1:C // 2, :]
    bl = b[C - 1:C, :]
    e_b = jnp.exp(b)
    e_q = jnp.exp(b - bref)
    e_k = jnp.exp(bref - b)
    e_lb = jnp.exp(bl - b)
    return dict(sq=sq, qf=qf, sf=sf, gate=gate, kk=kk, tri=tri, bl=bl, e_b=e_b, e_q=e_q, e_k=e_k, e_lb=e_lb)


def _hgrn_fwd(z1, hg_lb, gnorm):
    T = z1.shape[0]
    C = min(HG_CHUNK, T)
    nc = T // C
    ns = HG_CHUNKS_PER_STEP if nc % HG_CHUNKS_PER_STEP == 0 else 1
    R = ns * C

    def kern(q_ref, f_ref, i_ref, g_ref, lb_ref, gn_ref, o_ref, hg_ref, st_ref, s_scr):
        @pl.when(pl.program_id(0) == 0)
        def _():
            s_scr[...] = jnp.zeros(s_scr.shape, F32)

        lb_all, _, _ = _lower_bound(lb_ref[...])
        for sub in range(ns):
            rows = slice(sub * C, (sub + 1) * C)
            st_ref[sub] = s_scr[...]
            for h in range(HEADS):
                cols = slice(h * HEAD_W, (h + 1) * HEAD_W)
                t = _hg_gates(q_ref[rows, cols], f_ref[rows, cols], lb_all[:, cols])
                v_bf = i_ref[rows, cols].astype(BF16)
                st = s_scr[h]
                a = jnp.where(t["tri"], _dot((t["qf"] * t["e_q"]).astype(BF16), (t["kk"] * t["e_k"]).astype(BF16), NT), 0.0)
                o = _dot(a.astype(BF16), v_bf, NN) + _dot((t["qf"] * t["e_b"]).astype(BF16), st.astype(BF16), NT)
                s_scr[h] = st * jnp.exp(t["bl"]) + _dot(v_bf, (t["kk"] * t["e_lb"]).astype(BF16), TN_)
                o_ref[rows, cols] = o
                gr = g_ref[rows, cols]
                r = lax.rsqrt(jnp.mean(o * o, axis=-1, keepdims=True) + EPS)
                hg_ref[rows, cols] = (o * r * gn_ref[:, cols] * (gr * _sig(gr))).astype(BF16)

    seg = lambda k: pl.BlockSpec((R, D_MODEL), functools.partial(lambda n, k: (n, k), k=k))
    row = pl.BlockSpec((R, D_MODEL), lambda n: (n, 0))
    nbytes = 6 * _nbytes((R, D_MODEL), F32) + (2 + ns) * _nbytes((HEADS, 128, 128), F32)
    return pl.pallas_call(
        kern, name="hgrn_fwd", grid=(nc // ns,),
        in_specs=[seg(0), seg(1), seg(2), seg(3), pl.BlockSpec((2, D_MODEL), lambda n: (0, 0)),
                  pl.BlockSpec((1, D_MODEL), lambda n: (0, 0))],
        out_specs=[row, row, pl.BlockSpec((ns, HEADS, 128, 128), lambda n: (n, 0, 0, 0))],
        out_shape=[pltpu.HBM((T, D_MODEL), F32), pltpu.HBM((T, D_MODEL), BF16),
                   pltpu.HBM((nc, HEADS, 128, 128), F32)],
        scratch_shapes=[pltpu.VMEM((HEADS, 128, 128), F32)],
        compiler_params=pltpu.CompilerParams(dimension_semantics=("arbitrary",), vmem_limit_bytes=_vmem(nbytes)),
    )(*[_hbm(a) for a in (z1, z1, z1, z1, hg_lb, gnorm)])


def _hgrn_bwd(z1, o_pre, dhg, states, hg_lb, gnorm):
    T = z1.shape[0]
    C = min(HG_CHUNK, T)
    nc = T // C
    ns = HG_CHUNKS_PER_STEP if nc % HG_CHUNKS_PER_STEP == 0 else 1
    R, steps = ns * C, nc // ns

    def kern(q_ref, f_ref, i_ref, g_ref, o_ref, dhg_ref, st_ref, lb_ref, gn_ref, dz_ref, dlb_ref, dgn_ref, ds_scr, dlb_scr):
        n = pl.program_id(0)

        @pl.when(n == 0)
        def _():
            ds_scr[...] = jnp.zeros(ds_scr.shape, F32)
            dlb_scr[...] = jnp.zeros(dlb_scr.shape, F32)
            dgn_ref[...] = jnp.zeros(dgn_ref.shape, F32)

        lb_all, s0, s1 = _lower_bound(lb_ref[...])
        for sub in reversed(range(ns)):
            rows = slice(sub * C, (sub + 1) * C)
            for h in range(HEADS):
                cols = slice(h * HEAD_W, (h + 1) * HEAD_W)
                lb = lb_all[:, cols]
                qr, fr = q_ref[rows, cols], f_ref[rows, cols]
                t = _hg_gates(qr, fr, lb)
                tri = t["tri"]
                v_bf = i_ref[rows, cols].astype(BF16)
                st_bf = st_ref[sub, h].astype(BF16)
                dst = ds_scr[h]
                dst_bf = dst.astype(BF16)
                o = o_ref[rows, cols]
                gr = g_ref[rows, cols]
                sg = _sig(gr)
                sil = gr * sg
                gn = gn_ref[:, cols]
                r = lax.rsqrt(jnp.mean(o * o, axis=-1, keepdims=True) + EPS)
                on = o * r
                dh = dhg_ref[rows, cols].astype(F32)
                dgn_ref[:, cols] += jnp.sum(dh * on * sil, axis=0, keepdims=True)
                dg = dh * on * gn * (sg * (1.0 + gr * (1.0 - sg)))
                don = dh * gn * sil
                do_bf = (r * (don - on * jnp.mean(don * on, axis=-1, keepdims=True))).astype(BF16)
                qe = (t["qf"] * t["e_q"]).astype(BF16)
                ke = (t["kk"] * t["e_k"]).astype(BF16)
                qb = (t["qf"] * t["e_b"]).astype(BF16)
                kh_bf = (t["kk"] * t["e_lb"]).astype(BF16)
                a_bf = jnp.where(tri, _dot(qe, ke, NT), 0.0).astype(BF16)
                da_bf = jnp.where(tri, _dot(do_bf, v_bf, NT), 0.0).astype(BF16)
                dv = _dot(a_bf, do_bf, TN_) + _dot(kh_bf, dst_bf, NT)
                dqe = _dot(da_bf, ke, NN)
                dqb = _dot(do_bf, st_bf, NN)
                dke = _dot(da_bf, qe, TN_)
                dkh = _dot(v_bf, dst_bf, NN)
                dqf = dqe * t["e_q"] + dqb * t["e_b"]
                dkk = dke * t["e_k"] + dkh * t["e_lb"]
                kh_r = kh_bf.astype(F32)
                db = qe.astype(F32) * dqe - ke.astype(F32) * dke + qb.astype(F32) * dqb - kh_r * dkh
                e_bl = jnp.exp(t["bl"])
                dbl = jnp.sum(dkh * kh_r, axis=0, keepdims=True) + e_bl * jnp.sum(st_ref[sub, h] * dst, axis=0, keepdims=True)
                dlg = _prefix_rows(db, reverse=True) + dbl
                ds_scr[h] = dst * e_bl + _dot(do_bf, qb, TN_)
                dgate = dlg / t["gate"] - dkk
                sf = t["sf"]
                dlb_scr[:, cols] += jnp.sum(dgate * (1.0 - sf), axis=0, keepdims=True)
                df = dgate * (1.0 - lb) * sf * (1.0 - sf)
                dq = dqf * (t["sq"] * (1.0 + qr * (1.0 - t["sq"])))
                dz_ref[rows, cols] = dq.astype(BF16)
                dz_ref[rows, D_MODEL + h * HEAD_W:D_MODEL + (h + 1) * HEAD_W] = df.astype(BF16)
                dz_ref[rows, 2 * D_MODEL + h * HEAD_W:2 * D_MODEL + (h + 1) * HEAD_W] = dv.astype(BF16)
                dz_ref[rows, 3 * D_MODEL + h * HEAD_W:3 * D_MODEL + (h + 1) * HEAD_W] = dg.astype(BF16)

        @pl.when(n == steps - 1)
        def _():
            d = s0 * s1 * dlb_scr[...]
            dlb_ref[0:1, :] = -d
            dlb_ref[1:2, :] = d

    seg = lambda k: pl.BlockSpec((R, D_MODEL), functools.partial(lambda n, k: (steps - 1 - n, k), k=k))
    nbytes = 6 * _nbytes((R, D_MODEL), F32) + _nbytes((R, 4 * D_MODEL), BF16) + (2 + ns) * _nbytes((HEADS, 128, 128), F32)
    return pl.pallas_call(
        kern, name="hgrn_bwd", grid=(steps,),
        in_specs=[seg(0), seg(1), seg(2), seg(3), seg(0), seg(0),
                  pl.BlockSpec((ns, HEADS, 128, 128), lambda n: (steps - 1 - n, 0, 0, 0)),
                  pl.BlockSpec((2, D_MODEL), lambda n: (0, 0)), pl.BlockSpec((1, D_MODEL), lambda n: (0, 0))],
        out_specs=[pl.BlockSpec((R, 4 * D_MODEL), lambda n: (steps - 1 - n, 0)),
                   pl.BlockSpec((2, D_MODEL), lambda n: (0, 0)), pl.BlockSpec((1, D_MODEL), lambda n: (0, 0))],
        out_shape=[pltpu.HBM((T, 4 * D_MODEL), BF16), pltpu.HBM((2, D_MODEL), F32),
                   pltpu.HBM((1, D_MODEL), F32)],
        scratch_shapes=[pltpu.VMEM((HEADS, 128, 128), F32), pltpu.VMEM((1, D_MODEL), F32)],
        compiler_params=pltpu.CompilerParams(dimension_semantics=("arbitrary",), vmem_limit_bytes=_vmem(nbytes)),
    )(*[_hbm(a) for a in (z1, z1, z1, z1, o_pre, dhg, states, hg_lb, gnorm)])


def _prep_weights(gw):
    w_in_e = gw["w_in_e"].reshape(1568, D_MODEL)
    kr = jnp.pad(w_in_e[512:544], ((64, 32), (0, 0)))
    wm = jnp.concatenate([w_in_e[0:512], kr], axis=0)
    ws = w_in_e[544:1568]
    w_qb = gw["w_qb"].transpose(1, 0, 2).reshape(MLA_LORA, HEADS, 96)
    wq = jnp.pad(w_qb, ((0, 0), (0, 0), (0, 32))).reshape(MLA_LORA, HEADS * HEAD_W)
    kvb = gw["w_kvb"].transpose(1, 0, 2).reshape(MLA_LORA, HEADS, 128)
    wk = jnp.pad(kvb[:, :, :64], ((0, 0), (0, 0), (0, 64))).reshape(MLA_LORA, HEADS * HEAD_W)
    wv = jnp.pad(kvb[:, :, 64:], ((0, 0), (0, 0), (0, 64))).reshape(MLA_LORA, HEADS * HEAD_W)
    w_out_e = gw["w_out_e"].reshape(D_MODEL, D_MODEL)
    woa = jnp.pad(w_out_e[:512].reshape(HEADS, 64, D_MODEL), ((0, 0), (0, 64), (0, 0))).reshape(HEADS * HEAD_W, D_MODEL)
    return dict(wm=wm, ws=ws, wq=wq, wk=wk, wv=wv, woa=woa, wob=w_out_e[512:])


def _unprep_grads(g):
    dwm, dws = g["wm"], g["ws"]
    d_in_e = jnp.concatenate([dwm[0:512], dwm[512 + 64:512 + 96], dws], axis=0).reshape(N_DEV, 1568 // N_DEV, D_MODEL)
    d_qb = g["wq"].reshape(MLA_LORA, HEADS, HEAD_W)[:, :, :96].reshape(MLA_LORA, HEADS * 96)
    dk = g["wk"].reshape(MLA_LORA, HEADS, HEAD_W)[:, :, :64]
    dv = g["wv"].reshape(MLA_LORA, HEADS, HEAD_W)[:, :, :64]
    d_kvb = jnp.concatenate([dk, dv], axis=2).reshape(MLA_LORA, HEADS * 128)
    d_oa = g["woa"].reshape(HEADS, HEAD_W, D_MODEL)[:, :64].reshape(HEADS * 64, D_MODEL)
    dev_major = lambda a: a.reshape(a.shape[0], N_DEV, a.shape[1] // N_DEV).transpose(1, 0, 2)
    return dict(w_in_e=d_in_e, w_qb=dev_major(d_qb), w_kvb=dev_major(d_kvb),
                w_out_e=jnp.concatenate([d_oa, g["wob"]], axis=0).reshape(N_DEV, D_MODEL // N_DEV, D_MODEL))


def _local_step(x, positions, target, gw, sp, ex):
    w = _prep_weights(gw)
    T = x.shape[0]
    tm = min(TM, T)
    nt = T // tm
    half = MLA_ROPE // 2
    inv_freq = ROPE_BASE ** (-jnp.arange(half, dtype=F32) / half)
    invf_lane = jnp.concatenate([jnp.zeros((64,), F32), inv_freq, inv_freq, jnp.zeros((32,), F32)]).reshape(1, HEAD_W)
    tabs = _rope_tables(positions.reshape(T, 1), invf_lane)
    bias_full = jnp.repeat(sp["sgu_b"][0].T, 128, axis=1)
    sgu_w = sp["sgu_w"]
    gq, gkv = sp["mla_gq"], sp["mla_gkv"]
    ln1_g, ln1_b, ln2_g, ln2_b = sp["ln1_g"], sp["ln1_b"], sp["ln2_g"], sp["ln2_b"]
    zm, zs, cqn, ckvn, kr_rot = _mla_in(x, w["wm"], w["ws"], tabs, gq, gkv, deps=[ex.first_token])
    q, k, v = _mla_qkv(cqn, ckvn, kr_rot, tabs, w["wq"], w["wk"], w["wv"])
    o_att, lse = _attn_fwd(q, k, v)
    b_out = _sgu_fwd(zs, sp["sgu_ln_g"], sp["sgu_ln_b"], sgu_w, bias_full)
    token = ex.weights_forward(after=[o_att, b_out])
    y1, h1, h1_bf = _proj_ln("l0_out_ln1", [o_att, b_out], [w["woa"], w["wob"]], x, ln1_g, ln1_b, 0, deps=[token])
    big = ex.weights_ready(after=[y1])
    w_ff1, w_in_o, w_out_o = big["w_ff1"], big["w_in_o"], big["w_out_o"].reshape(D_MODEL, D_MODEL)
    w_ff2 = [a.reshape(D_FF, D_MODEL) for a in big["w_ff2"]]
    a0, act0 = _mlp_up("l0", h1_bf, w_ff1[0])
    y2, h2, h2_bf = _proj_ln("l0_ff2_ln2", [act0], [w_ff2[0]], h1, ln2_g, ln2_b, 0)

    z1 = _tiled("l1_in", (1, nt), [_rb(h2_bf, tm), _res(w_in_o)], [_out(T, 4 * D_MODEL, F32, tm, 4 * D_MODEL)],
                _mmc_blocks(N_DEV, NN, lambda w, d: w[d]), direct=True)
    o_pre, hg, states = _hgrn_fwd(z1, sp["hg_lb"], sp["hg_gnorm"])
    y3, h3, h3_bf = _proj_ln("l1_out_ln1", [hg], [w_out_o], h2, ln1_g, ln1_b, 1)
    a1, act1 = _mlp_up("l1", h3_bf, w_ff1[1])

    gs, g0 = {}, {}
    dy4, dy4_bf, sq_err, gs["ln2_g1"], gs["ln2_b1"] = _proj_ln_loss("l1_ff2_loss", act1, w_ff2[1], h3, ln2_g, ln2_b, 1, target)
    gs["sq_err"] = sq_err
    da1, dw1_1, dw2_1 = _mlp_bwd_w("l1", h3_bf, a1, act1, dy4_bf, big["w_ff2"][1])
    dy3, dy3_bf, dhg, gs["ln1_g1"], gs["ln1_b1"] = _dh_ln_back("l1_dh_ln1", da1, w_ff1[1], dy4, y3, ln1_g, 1, proj=[w_out_o])
    d_out_o = _tiled("l1_dwout", (2, D_MODEL // TM), [_tl(hg, TM), _cw(dy3_bf, TN)],
                     [_out(D_MODEL, D_MODEL, F32, TM, TN), _out(D_MODEL, D_MODEL, BF16, TM, TN)], _mmc(TN_, epilogue=_twice))
    d_out_o = [a.reshape(N_DEV, D_MODEL // N_DEV, D_MODEL) for a in d_out_o]
    dz1, gs["hg_lb"], gs["hg_gnorm"] = _hgrn_bwd(z1, o_pre, dhg, states, sp["hg_lb"], sp["hg_gnorm"])
    d_in_o = _tiled("l1_dwin", (N_DEV, 1), [_res(h2_bf), _cw(dz1, TN)],
                    [_out_dev(D_MODEL, TN, D_MODEL), _out_dev(D_MODEL, TN, D_MODEL, BF16)], _mmc(TN_, epilogue=_twice))
    token = ex.direct_start("l1", [dw1_1, dw2_1, d_in_o, d_out_o])

    dy2, dy2_bf, gs["ln2_g0"], gs["ln2_b0"] = _dh_ln_back("l1_dh_ln2", dz1, w_in_o, dy3, y2, ln2_g, 0, deps=[token])
    da0, dw1_0, dw2_0 = _mlp_bwd_w("l0", h1_bf, a0, act0, dy2_bf, big["w_ff2"][0])
    token = ex.direct_start("l0m", [dw1_0, dw2_0])
    dy1, dy1_bf, dcat, gs["ln1_g0"], gs["ln1_b0"] = _dh_ln_back("l0_dh_ln1", da0, w_ff1[0], dy2, y1, ln1_g, 0,
                                                                 proj=[w["woa"], w["wob"]], deps=[token])
    g0["woa"], g0["wob"] = _out_weight_grads(o_att, b_out, dy1_bf)
    dzs, gs["sgu_w"], gs["sgu_ln_g"], gs["sgu_ln_b"], gs["sgu_b"] = _sgu_bwd(zs, dcat, sp["sgu_ln_g"], sp["sgu_ln_b"], sgu_w, bias_full)
    dq, dk, dv = _attn_bwd(q, k, v, o_att, lse, dcat)
    dzm, g0["wq"], g0["wk"], g0["wv"], gs["mla_gq"], gs["mla_gkv"] = _mla_back(zm, cqn, ckvn, tabs, gq, gkv, w["wq"], w["wk"], w["wv"],
                                                                                 dq, dk, dv)
    token = ex.small_start(gs)
    dx, g0["wm"], g0["ws"] = _in_back(x, dzm, dzs, dy1, w["wm"], w["ws"], deps=[token])

    return sq_err, dx, _unprep_grads(g0), gs


def _me():
    return lax.axis_index("x"), lax.axis_index("y"), lax.axis_index("c")


ANY_SPEC = pl.BlockSpec(memory_space=pl.ANY)
HBM_SPEC = pl.BlockSpec(memory_space=pltpu.HBM)
SEM_SPEC = pl.BlockSpec(memory_space=pltpu.SEMAPHORE)
EFFECT = pltpu.SideEffectType.DATAFLOW_SIDE_EFFECTING


def _split_start(name, srcs, lands, n_sems, make_copies, after=()):
    n, m, k = len(srcs), len(lands), len(after)

    def body(*refs):
        for cp in make_copies(refs[:n], refs[n:n + m], refs[n + m + k], refs[n + m + k + 1]):
            cp.start()
        refs[-1][...] = jnp.zeros(refs[-1].shape, F32)

    out_shape = (pltpu.SemaphoreType.DMA((n_sems,)), pltpu.SemaphoreType.DMA((n_sems,)),
                 *[pltpu.HBM(a.shape, a.dtype) for a in (*srcs, *lands)], jax.ShapeDtypeStruct((8, 128), F32))
    res = pl.pallas_call(
        body, name=name, out_shape=out_shape, in_specs=[HBM_SPEC] * (n + m) + [ANY_SPEC] * k,
        out_specs=(SEM_SPEC, SEM_SPEC, *[HBM_SPEC] * (n + m), pl.BlockSpec(memory_space=pltpu.VMEM)),
        input_output_aliases={i: 2 + i for i in range(n + m)},
        compiler_params=pltpu.CompilerParams(has_side_effects=EFFECT),
    )(*[_hbm(a) for a in (*srcs, *lands)], *after)
    return res[0], res[1], list(res[2:2 + n]), list(res[2 + n:2 + n + m]), res[-1]


def _split_wait(name, send_sems, recv_sems, srcs, lands, after, make_copies):
    n, m = len(srcs), len(lands)

    def body(*refs):
        for cp in make_copies(refs[:n], refs[n:n + m], refs[n + m], refs[n + m + 1]):
            cp.wait_send()
            cp.wait_recv()

    res = pl.pallas_call(
        body, name=name, out_shape=tuple(pltpu.HBM(a.shape, a.dtype) for a in (*srcs, *lands)),
        in_specs=[HBM_SPEC] * (n + m) + [SEM_SPEC, SEM_SPEC] + [ANY_SPEC] * len(after), out_specs=tuple([HBM_SPEC] * (n + m)),
        input_output_aliases={i: i for i in range(n + m)},
        compiler_params=pltpu.CompilerParams(has_side_effects=EFFECT),
    )(*srcs, *lands, send_sems, recv_sems, *after)
    return list(res[:n]), list(res[n:])


def _place_own(shards, dev):
    n = len(shards)

    def kern(dev_ref, *refs):
        for x_ref, o_ref in zip(refs[:n], refs[n:]):
            o_ref[...] = x_ref[...].astype(o_ref.dtype)

    blocks = [(None, *a.shape[1:]) for a, _, _ in shards]
    nbytes = sum(_nbytes(b, a.dtype) + _nbytes(b, dt) for b, (a, _, dt) in zip(blocks, shards))
    return pl.pallas_call(
        kern, name="weights_place_own", out_shape=[pltpu.HBM((N_DEV, *a.shape[1:]), dt) for a, _, dt in shards],
        grid_spec=pltpu.PrefetchScalarGridSpec(
            num_scalar_prefetch=1, grid=(1,),
            in_specs=[pl.BlockSpec(b, functools.partial(lambda i, dev, l: (l, 0, 0), l=l)) for b, (_, l, _) in zip(blocks, shards)],
            out_specs=[pl.BlockSpec(b, lambda i, dev: (dev[0], 0, 0)) for b in blocks]),
        compiler_params=pltpu.CompilerParams(dimension_semantics=("arbitrary",), vmem_limit_bytes=_vmem(nbytes)),
    )(dev, *[_hbm(a) for a, _, _ in shards])


def _ag_first_copies(src_refs, out_refs, send_sems, recv_sems):
    x, y, c = _me()
    targets = [(x, y, 1 - c), (1 - x, y, c), (x, 1 - y, c), (1 - x, 1 - y, c)]
    return [pltpu.make_async_remote_copy(
        src_ref=out_refs[op].at[4 * x + 2 * y + c], dst_ref=out_refs[op].at[4 * x + 2 * y + c], send_sem=send_sems.at[4 * op + k],
        recv_sem=recv_sems.at[4 * op + k], device_id=to, device_id_type=MESH)
        for op in range(len(out_refs)) for k, to in enumerate(targets)]


def _ag_second_copies(src_refs, out_refs, send_sems, recv_sems):
    x, y, c = _me()
    chips = [(1 - x, y), (x, 1 - y), (1 - x, 1 - y)]
    return [pltpu.make_async_remote_copy(
        src_ref=out_refs[op].at[4 * cx + 2 * cy + c], dst_ref=out_refs[op].at[4 * cx + 2 * cy + c],
        send_sem=send_sems.at[3 * op + j], recv_sem=recv_sems.at[3 * op + j], device_id=(x, y, 1 - c), device_id_type=MESH)
        for op in range(len(out_refs)) for j, (cx, cy) in enumerate(chips)]


def _rs_sibling_copies(g_refs, out_refs, send_sems, recv_sems):
    x, y, c = _me()
    return [pltpu.make_async_remote_copy(
        src_ref=g_refs[op].at[k, 1 - c], dst_ref=out_refs[op].at[k], send_sem=send_sems.at[4 * op + k],
        recv_sem=recv_sems.at[4 * op + k], device_id=(x, y, 1 - c), device_id_type=MESH)
        for op in range(len(g_refs)) for k in range(4)]


def _rs_direct_copies(g_refs, land_refs, send_sems, recv_sems):
    x, y, c = _me()
    n = len(g_refs) // 2
    chips = [(1 - x, y), (x, 1 - y), (1 - x, 1 - y)]
    copies = []
    for op in range(n):
        g32, g16, from_sib, from_others = g_refs[op], g_refs[n + op], land_refs[op], land_refs[n + op]
        copies.append(pltpu.make_async_remote_copy(
            src_ref=g32.at[2 * x + y, 1 - c], dst_ref=from_sib, send_sem=send_sems.at[7 * op], recv_sem=recv_sems.at[7 * op],
            device_id=(x, y, 1 - c), device_id_type=MESH))
        for j, (cx, cy) in enumerate(chips):
            for s, cc in enumerate((c, 1 - c)):
                copies.append(pltpu.make_async_remote_copy(
                    src_ref=g16.at[2 * cx + cy, cc], dst_ref=from_others.at[2 * j + s], send_sem=send_sems.at[7 * op + 1 + 2 * j + s],
                    recv_sem=recv_sems.at[7 * op + 1 + 2 * j + s], device_id=(cx, cy, cc), device_id_type=MESH))
    return copies


def _rs_chip_copies(p_refs, out_refs, send_sems, recv_sems):
    x, y, c = _me()
    chips = [(1 - x, y), (x, 1 - y), (1 - x, 1 - y)]
    return [pltpu.make_async_remote_copy(
        src_ref=p_refs[op].at[2 * cx + cy], dst_ref=out_refs[op].at[j], send_sem=send_sems.at[3 * op + j],
        recv_sem=recv_sems.at[3 * op + j], device_id=(cx, cy, c), device_id_type=MESH)
        for op in range(len(p_refs)) for j, (cx, cy) in enumerate(chips)]


def _all_gather(placed):
    n = len(placed)

    def kern(*refs):
        in_refs, out_refs, (send_sems, recv_sems) = refs[:n], refs[n:2 * n], refs[2 * n:]
        x, y, c = _me()
        me, sibling = (x, y, c), (x, y, 1 - c)
        chips = [(1 - x, y), (x, 1 - y), (1 - x, 1 - y)]

        def copy(op, k, block, to, own=False):
            idx = 4 * block[0] + 2 * block[1] + block[2]
            return pltpu.make_async_remote_copy(
                src_ref=(in_refs if own else out_refs)[op].at[idx], dst_ref=out_refs[op].at[idx], send_sem=send_sems.at[7 * op + k],
                recv_sem=recv_sems.at[7 * op + k], device_id=to, device_id_type=MESH)

        first = []
        for op in range(n):
            first.append(copy(op, 0, me, sibling, own=True))
            first += [copy(op, 1 + j, me, (*chip, c), own=True) for j, chip in enumerate(chips)]
        for cp in first:
            cp.start()
        passed = []
        for j, chip in enumerate(chips):
            for op in range(n):
                copy(op, 1 + j, (*chip, c), me).wait_recv()
                passed.append(copy(op, 4 + j, (*chip, c), sibling))
                passed[-1].start()
        for op in range(n):
            copy(op, 0, sibling, me).wait_recv()
            for j, chip in enumerate(chips):
                copy(op, 4 + j, (*chip, 1 - c), me).wait_recv()
        for cp in first + passed:
            cp.wait_send()

    return pl.pallas_call(
        kern, name="weights_all_gather", out_shape=[pltpu.HBM(g.shape, g.dtype) for g in placed],
        in_specs=[ANY_SPEC] * n, out_specs=[ANY_SPEC] * n, input_output_aliases={i: i for i in range(n)},
        scratch_shapes=[pltpu.SemaphoreType.DMA((7 * n,)), pltpu.SemaphoreType.DMA((7 * n,))],
    )(*[_hbm(a) for a in placed])


def _row_tile(r, w, n_blocks):
    tr = r
    while tr > 8 and 2 * n_blocks * tr * w * 4 > 24 * 2**20:
        tr //= 2
    return tr


def _chip_sum(name, g, from_sibling, core):
    _, _, R, W = g.shape
    tr = _row_tile(R, W, 3)

    def kern(core_ref, g_ref, s_ref, o_ref):
        o_ref[...] = (g_ref[...] + s_ref[...]).astype(BF16)

    return pl.pallas_call(
        kern, name=name, out_shape=pltpu.HBM((4, R, W), BF16),
        grid_spec=pltpu.PrefetchScalarGridSpec(
            num_scalar_prefetch=1, grid=(4, R // tr),
            in_specs=[pl.BlockSpec((None, None, tr, W), lambda k, i, core: (k, core[0], i, 0)),
                      pl.BlockSpec((None, tr, W), lambda k, i, core: (k, i, 0))],
            out_specs=pl.BlockSpec((None, tr, W), lambda k, i, core: (k, i, 0))),
        compiler_params=pltpu.CompilerParams(dimension_semantics=("parallel", "parallel"), vmem_limit_bytes=_vmem(3 * tr * W * 4)),
    )(core, _hbm(g), _hbm(from_sibling))


def _adamw(w, g, m, v):
    m = ADAM_B1 * m + (1.0 - ADAM_B1) * g
    v = ADAM_B2 * v + (1.0 - ADAM_B2) * (g * g)
    m_hat = m / (1.0 - ADAM_B1 ** ADAM_STEP)
    v_hat = v / (1.0 - ADAM_B2 ** ADAM_STEP)
    return -ADAM_LR * (m_hat / (jnp.sqrt(v_hat) + ADAM_EPS) + ADAM_WD * w), m, v


def _finish_sharded(name, layers, w, m, v, where, deps=()):
    nl, R, W = w.shape
    n_other = layers[0][2].shape[0]
    tr = _row_tile(R, W, (8 + n_other) * nl)
    deps = _deps(deps)

    def kern(where_ref, *refs):
        w_ref, m_ref, v_ref = refs[3 * nl:3 * nl + 3]
        go_ref, d_ref, mo_ref, vo_ref = refs[3 * nl + 3 + len(deps):]
        for l in range(nl):
            g_ref, s_ref, c_ref = refs[3 * l:3 * l + 3]
            grad = g_ref[...] + s_ref[...]
            for j in range(n_other):
                grad = grad + c_ref[j].astype(F32)
            go_ref[l] = grad
            d_ref[l], mo_ref[l], vo_ref[l] = _adamw(w_ref[l], grad, m_ref[l], v_ref[l])

    row = pl.BlockSpec((nl, tr, W), lambda i, wh: (0, i, 0))
    in_specs, args = [], []
    for g, s, c in layers:
        sib = (pl.BlockSpec((None, tr, W), lambda i, wh: (wh[0], i, 0)) if s.ndim == 3 else pl.BlockSpec((tr, W), lambda i, wh: (i, 0)))
        in_specs += [pl.BlockSpec((None, None, tr, W), lambda i, wh: (wh[0], wh[1], i, 0)), sib,
                     pl.BlockSpec((n_other, tr, W), lambda i, wh: (0, i, 0))]
        args += [g, s, c]
    return pl.pallas_call(
        kern, name=name, out_shape=[pltpu.HBM((nl, R, W), F32)] * 4,
        grid_spec=pltpu.PrefetchScalarGridSpec(num_scalar_prefetch=1, grid=(R // tr,),
                                               in_specs=in_specs + [row, row, row] + [ANY_SPEC] * len(deps),
                                               out_specs=[row, row, row, row]),
        compiler_params=pltpu.CompilerParams(dimension_semantics=("parallel",),
                                             vmem_limit_bytes=_vmem(nl * (8 + n_other) * tr * W * 4)),
    )(where, *[_hbm(a) for a in (*args, w, m, v)], *deps)


SMALL_PLACE = (("mla_gq", 0, 0, 1, 256), ("mla_gkv", 0, 256, 1, 256), ("sgu_ln_g", 0, 512, 1, 512), ("sgu_ln_b", 1, 0, 1, 512),
               ("hg_lb", 2, 0, 2, 1024), ("ln1_g", 4, 0, 2, 1024), ("ln1_b", 6, 0, 2, 1024), ("sgu_b", 8, 0, 4, 128),
               ("ln2_g", 12, 0, 2, 1024), ("ln2_b", 14, 0, 2, 1024), ("hg_gnorm", 16, 0, 1, 1024))
SMALL_BUF_ROWS = 24
LOSS_ROW = 17


def _small_pack(gs, dev):
    pieces = [(gs["mla_gq"], 0, 0), (gs["mla_gkv"], 0, 256), (gs["sgu_ln_g"], 0, 512), (gs["sgu_ln_b"], 1, 0), (gs["hg_lb"], 2, 0),
              (gs["ln1_g0"], 4, 0), (gs["ln1_g1"], 5, 0), (gs["ln1_b0"], 6, 0), (gs["ln1_b1"], 7, 0), (gs["sgu_b"], 8, 0),
              (gs["ln2_g0"], 12, 0), (gs["ln2_g1"], 13, 0), (gs["ln2_b0"], 14, 0), (gs["ln2_b1"], 15, 0), (gs["hg_gnorm"], 16, 0),
              (gs["sq_err"], LOSS_ROW, 0)]
    n_p = len(pieces)

    def kern(dev_ref, *refs):
        a_ref, b_ref = refs[n_p + 1], refs[n_p + 2]
        a_ref[...] = jnp.zeros(a_ref.shape, F32)
        for ref, (_, r, l0) in zip(refs[:n_p], pieces):
            a_ref[r:r + ref.shape[0], l0:l0 + ref.shape[1]] = ref[...]
        b_ref[...] = refs[n_p][...]

    whole = lambda a: pl.BlockSpec(a.shape, functools.partial(lambda i, dev, nd: (0,) * nd, nd=a.ndim))
    return pl.pallas_call(
        kern, name="small_grads_pack",
        out_shape=[pltpu.HBM((N_DEV, SMALL_BUF_ROWS, D_MODEL), F32), pltpu.HBM((N_DEV, SGU_G, 128, 128), F32)],
        grid_spec=pltpu.PrefetchScalarGridSpec(
            num_scalar_prefetch=1, grid=(1,), in_specs=[whole(p[0]) for p in pieces] + [whole(gs["sgu_w"])],
            out_specs=[pl.BlockSpec((None, SMALL_BUF_ROWS, D_MODEL), lambda i, dev: (dev[0], 0, 0)),
                       pl.BlockSpec((None, SGU_G, 128, 128), lambda i, dev: (dev[0], 0, 0, 0))]),
    )(dev, *[p[0] for p in pieces], gs["sgu_w"])


def _small_copies(src_refs, land_refs, send_sems, recv_sems):
    px, py, pc = _me()
    me = 4 * px + 2 * py + pc
    return [pltpu.make_async_remote_copy(
        src_ref=land_refs[k].at[me], dst_ref=land_refs[k].at[me], send_sem=send_sems.at[2 * (r - 1) + k],
        recv_sem=recv_sems.at[2 * (r - 1) + k], device_id=(px ^ (r >> 2), py ^ ((r >> 1) & 1), pc ^ (r & 1)), device_id_type=MESH)
        for r in range(1, N_DEV) for k in range(2)]


def _small_adamw(slots_a, slots_b, given):
    names = [p[0] for p in SMALL_PLACE] + ["sgu_w"]
    n_names = len(names)
    wmv = [given[pre + name] for name in names for pre in ("", "m_", "v_")]
    vmem = pl.BlockSpec(memory_space=pltpu.VMEM)

    def kern(*refs):
        sum_a, sum_b = refs[0][0], refs[1][0]
        for d in range(1, N_DEV):
            sum_a, sum_b = sum_a + refs[0][d], sum_b + refs[1][d]
        wmv_refs, out_refs = refs[2:2 + 3 * n_names], refs[2 + 3 * n_names:]
        px, py, pc = _me()
        me = 4 * px + 2 * py + pc

        def own_block(full):
            acc = full[:, 0:128]
            for b in range(1, N_DEV):
                acc = jnp.where(me == b, full[:, b * 128:(b + 1) * 128], acc)
            return acc

        for idx, name in enumerate(names):
            w_ref, m_ref, v_ref = wmv_refs[3 * idx:3 * idx + 3]
            if name == "sgu_w":
                grad = sum_b[None]
            else:
                _, r, l0, nr, nl = SMALL_PLACE[idx]
                grad = sum_a[r:r + nr, l0:l0 + nl]
                if name == "hg_gnorm":
                    grad = own_block(grad)
                if name == "sgu_b":
                    grad = grad[None]
            res = (grad, *_adamw(w_ref[...], grad, m_ref[...], v_ref[...]))
            for o_ref, val in zip(out_refs[4 * idx:4 * idx + 4], res):
                o_ref[...] = val
        out_refs[4 * n_names][...] = (0.5 / D_MODEL) * jnp.sum(sum_a[LOSS_ROW:LOSS_ROW + 1, :], axis=1, keepdims=True)

    out_shape = [jax.ShapeDtypeStruct(given[name].shape, F32) for name in names for _ in range(4)]
    out_shape.append(jax.ShapeDtypeStruct((1, 1), F32))
    res = pl.pallas_call(
        kern, name="small_adamw", out_shape=out_shape, in_specs=[vmem] * (2 + len(wmv)), out_specs=[vmem] * len(out_shape),
    )(slots_a, slots_b, *wmv)
    out = {name: res[4 * idx:4 * idx + 4] for idx, name in enumerate(names)}
    out["loss"] = res[-1].reshape(())
    return out


class _Exchange:
    def __init__(self, given):
        self.given = given
        px, py, pc = _me()
        self.core = pc.reshape(1).astype(jnp.int32)
        self.dev = (4 * px + 2 * py + pc).reshape(1).astype(jnp.int32)
        self.where = jnp.stack([2 * px + py, pc]).astype(jnp.int32)
        self.state, self.layers = {}, {}

    def start_weights(self, lands, after):
        self.weights = _split_start("weights_first_start", [], lands, 4 * len(lands), _ag_first_copies, after=after)
        self.first_token = self.weights[4]

    def weights_forward(self, after):
        send_sems, recv_sems, shards, lands, _ = self.weights
        _, lands = _split_wait("weights_first_wait", send_sems, recv_sems, shards, lands, after, _ag_first_copies)
        self.weights = _split_start("weights_second_start", [], lands, 3 * len(lands), _ag_second_copies)
        return self.weights[4]

    def weights_ready(self, after):
        send_sems, recv_sems, shards, lands, _ = self.weights
        _, got = _split_wait("weights_second_wait", send_sems, recv_sems, shards, lands, after, _ag_second_copies)
        return dict(w_in_o=got[0], w_out_o=got[1], w_ff1=[got[2], got[3]], w_ff2=[got[4], got[5]])

    def small_start(self, gs):
        self.small = _split_start("small_grads_start", [], _small_pack(gs, self.dev), 14, _small_copies)
        return self.small[4]

    def small_finish(self, after):
        send_sems, recv_sems, _, lands, _ = self.small
        _, lands = _split_wait("small_grads_wait", send_sems, recv_sems, [], lands, after, _small_copies)
        return _small_adamw(lands[0], lands[1], self.given)

    def direct_start(self, tag, grads):
        f32 = [g[0].reshape(4, 2, *g[0].shape[1:]) for g in grads]
        bf16 = [g[1].reshape(4, 2, *g[1].shape[1:]) for g in grads]
        lands = [lax.empty(b.shape[2:], F32) for b in f32] + [lax.empty((6, *b.shape[2:]), BF16) for b in f32]
        self.state[tag] = _split_start(f"grads_{tag}_start", f32 + bf16, lands, 7 * len(grads), _rs_direct_copies)
        return self.state[tag][4]

    def direct_end(self, tag, after):
        send_sems, recv_sems, srcs, lands, _ = self.state[tag]
        srcs, lands = _split_wait(f"grads_{tag}_wait", send_sems, recv_sems, srcs, lands, after, _rs_direct_copies)
        n = len(lands) // 2
        self.layers[tag] = list(zip(srcs[:n], lands[:n], lands[n:]))

    def grads_start(self, tag, grads):
        blocks = [g.reshape(4, 2, *g.shape[1:]) for g in grads]
        lands = [lax.empty((4, *b.shape[2:]), F32) for b in blocks]
        self.state[tag] = _split_start(f"grads_{tag}_sibling_start", blocks, lands, 4 * len(blocks), _rs_sibling_copies)
        return self.state[tag][4]

    def grads_middle(self, tag, after):
        send_sems, recv_sems, blocks, lands, _ = self.state[tag]
        blocks, from_sibling = _split_wait(f"grads_{tag}_sibling_wait", send_sems, recv_sems, blocks, lands, [after], _rs_sibling_copies)
        sums = [_chip_sum(f"grads_{tag}_chip_sum_{k}", b, s, self.core) for k, (b, s) in enumerate(zip(blocks, from_sibling))]
        lands = [lax.empty((3, *p.shape[1:]), BF16) for p in sums]
        self.state[tag] = (blocks, from_sibling, _split_start(f"grads_{tag}_chips_start", sums, lands, 3 * len(sums), _rs_chip_copies))
        return self.state[tag][2][4]

    def grads_end(self, tag, after):
        blocks, from_sibling, (send_sems, recv_sems, sums, lands, _) = self.state[tag]
        after = list(after) if isinstance(after, (list, tuple)) else [after]
        _, from_chips = _split_wait(f"grads_{tag}_chips_wait", send_sems, recv_sems, sums, lands, after, _rs_chip_copies)
        self.layers[tag] = list(zip(blocks, from_sibling, from_chips))


def kernel(x, positions, w_in_e, mla_gq, mla_gkv, w_qb, w_kvb, sgu_ln_g, sgu_ln_b, sgu_w, sgu_b, w_out_e, w_in_o, hg_lb, hg_gnorm, w_out_o, ln1_g, ln1_b, w_ff1, w_ff2, ln2_g, ln2_b, loss_target, m_w_in_e, m_mla_gq, m_mla_gkv, m_w_qb, m_w_kvb, m_sgu_ln_g, m_sgu_ln_b, m_sgu_w, m_sgu_b, m_w_out_e, m_w_in_o, m_hg_lb, m_hg_gnorm, m_w_out_o, m_ln1_g, m_ln1_b, m_w_ff1, m_w_ff2, m_ln2_g, m_ln2_b, v_w_in_e, v_mla_gq, v_mla_gkv, v_w_qb, v_w_kvb, v_sgu_ln_g, v_sgu_ln_b, v_sgu_w, v_sgu_b, v_w_out_e, v_w_in_o, v_hg_lb, v_hg_gnorm, v_w_out_o, v_ln1_g, v_ln1_b, v_w_ff1, v_w_ff2, v_ln2_g, v_ln2_b):
    given = dict(locals())
    for n in ("w_in_e", "m_w_in_e", "v_w_in_e"):
        given[n] = jnp.swapaxes(given[n], 1, 2)
    ex = _Exchange(given)

    names = ["w_in_e", "w_qb", "w_kvb", "w_out_e"]
    placed = _place_own([(given[n], 0, BF16) for n in names] + [(hg_gnorm.reshape(1, 1, D_MODEL // N_DEV), 0, F32)]
                        + [(w_in_o, 0, BF16), (w_out_o, 0, BF16), (w_ff1, 0, BF16), (w_ff1, 1, BF16), (w_ff2, 0, BF16), (w_ff2, 1, BF16)],
                        ex.dev)
    got = _all_gather(placed[:5])
    ex.start_weights(placed[5:], after=[got[0]])
    gw = dict(zip(names, got[:4]))
    small_names = ["mla_gq", "mla_gkv", "sgu_ln_g", "sgu_ln_b", "sgu_w", "sgu_b", "hg_lb", "ln1_g", "ln1_b", "ln2_g", "ln2_b"]
    sp = {n: given[n] for n in small_names}
    sp["hg_gnorm"] = got[4].reshape(1, D_MODEL)

    _, dx, grads, gs = _local_step(x[0], positions[0], loss_target[0], gw, sp, ex)

    def finish(n, layers, deps=()):
        return _finish_sharded(f"finish_{n}", layers, given[n], given["m_" + n], given["v_" + n], ex.where, deps=deps)

    ex.direct_end("l1", after=[dx])
    ex.direct_end("l0m", after=[dx])
    l1, l0m = ex.layers["l1"], ex.layers["l0m"]
    results = {}
    token = ex.grads_start("l0s", [grads[n] for n in names])
    results["w_ff1"] = finish("w_ff1", [l0m[0], l1[0]], deps=[token])
    token = ex.grads_middle("l0s", after=results["w_ff1"][0])
    results["w_ff2"] = finish("w_ff2", [l0m[1], l1[1]], deps=[token])
    results["w_in_o"] = finish("w_in_o", [l1[2]], deps=[token])
    results["w_out_o"] = finish("w_out_o", [l1[3]], deps=[token])
    results.update(ex.small_finish(after=[results["w_in_o"][0]]))
    ex.grads_end("l0s", after=[results[n][0] for n in ("mla_gq", "w_ff2", "w_in_o", "w_out_o")])
    for n, layer in zip(names, ex.layers["l0s"]):
        results[n] = finish(n, [layer])
    results["w_in_e"] = [jnp.swapaxes(a, 1, 2) for a in results["w_in_e"]]

    order = ["w_in_e", "mla_gq", "mla_gkv", "w_qb", "w_kvb", "sgu_ln_g", "sgu_ln_b", "sgu_w", "sgu_b", "w_out_e", "w_in_o",
             "hg_lb", "hg_gnorm", "w_out_o", "ln1_g", "ln1_b", "w_ff1", "w_ff2", "ln2_g", "ln2_b"]
    return (results["loss"], dx[None], *[results[name][kind] for kind in range(4) for name in order])
```

```python
import functools
import math

import jax
import jax.numpy as jnp
import numpy as np
from jax import lax
from jax.experimental import pallas as pl
from jax.experimental.pallas import tpu as pltpu

F32 = jnp.float32
BF16 = jnp.bfloat16
MESH = pl.DeviceIdType.MESH
HIGHEST = lax.Precision.HIGHEST

D_MODEL = 1024
D_FF = 4096
N_DEV = 8
HEADS = 8
HEAD_W = 128
MLA_NOPE = 64
MLA_ROPE = 32
MLA_V = 64
MLA_LORA = 256
MLA_SCALE = (MLA_NOPE + MLA_ROPE) ** -0.5
ROPE_BASE = 10000.0
SGU_DIM = 512
SGU_G = 4
SGU_CHUNK = 128
HG_CHUNK = 64
HG_CHUNKS_PER_STEP = 4
ALPHA = (2 * 2) ** 0.25
EPS = 1e-5
ADAM_LR, ADAM_B1, ADAM_B2, ADAM_EPS, ADAM_WD, ADAM_STEP = 0.001, 0.9, 0.999, 1e-08, 0.01, 10

VMEM_CAP_V7X = 56 * 2**20
VMEM_SLACK = 12 * 2**20
TM = 512
TN = 512


def _vmem(block_bytes):
    return int(min(VMEM_CAP_V7X, 2 * block_bytes + VMEM_SLACK))


def _hbm(a):
    return pltpu.with_memory_space_constraint(a, pltpu.HBM)


def _nbytes(shape, dtype):
    return int(np.prod([d for d in shape if d is not None])) * jnp.dtype(dtype).itemsize


def _sig(x):
    return 1.0 / (1.0 + jnp.exp(-x))


def _gelu(x):
    c = math.sqrt(2.0 / math.pi)
    t = jnp.tanh(c * (x + 0.044715 * x * x * x))
    return 0.5 * x * (1.0 + t), t


def _gelu_grad(x, t):
    c = math.sqrt(2.0 / math.pi)
    return 0.5 * (1.0 + t) + 0.5 * x * (1.0 - t * t) * c * (1.0 + 3 * 0.044715 * x * x)


def _dot(a, b, dims, precision=None):
    return lax.dot_general(a, b, (dims, ((), ())), preferred_element_type=F32, precision=precision)


NN = ((1,), (0,))
NT = ((1,), (1,))
TN_ = ((0,), (0,))


def _deps(deps):
    return [d for d in deps if d is not None]


def _tiled(name, grid, ins, outs, compute, direct=False, deps=()):
    n_in, deps = len(ins), _deps(deps)
    n_skip = n_in + len(deps)

    def kern(*refs):
        if direct:
            compute(refs[:n_in], refs[n_skip:])
            return
        for o_ref, r in zip(refs[n_skip:], compute(*refs[:n_in])):
            o_ref[...] = r.astype(o_ref.dtype).reshape(o_ref.shape)

    swap = lambda f: (lambda j, i: f(i, j))
    nbytes = sum(_nbytes(blk, a.dtype) for a, blk, _ in ins) + sum(_nbytes(blk, dt) + _nbytes(blk, F32) for _, dt, blk, _ in outs)
    res = pl.pallas_call(
        kern, name=name, grid=grid,
        in_specs=[pl.BlockSpec(blk, swap(f), pipeline_mode=pl.Buffered(1) if tuple(blk) == tuple(a.shape) else None)
                  for a, blk, f in ins] + [ANY_SPEC] * len(deps),
        out_specs=[pl.BlockSpec(blk, swap(f)) for _, _, blk, f in outs],
        out_shape=[pltpu.HBM(shape, dt) for shape, dt, _, _ in outs],
        compiler_params=pltpu.CompilerParams(dimension_semantics=("parallel", "parallel"), vmem_limit_bytes=_vmem(nbytes)),
    )(*[_hbm(a) for a, _, _ in ins], *deps)
    return res if len(res) > 1 else res[0]


def _rb(a, tm, w=None, cb=0):
    return (a, (tm, a.shape[1] if w is None else w), lambda i, j: (i, cb))


def _cw(b, tn):
    return (b, (b.shape[0], tn), lambda i, j: (0, j))


def _tl(a, tm):
    return (a, (a.shape[0], tm), lambda i, j: (0, i))


def _out(m, n, dtype, tm, tn):
    return ((m, n), dtype, (tm, tn), lambda i, j: (i, j))


def _out_dev(k, n, tm, dtype=F32):
    return ((N_DEV, k, n), dtype, (None, tm, n), lambda i, j: (j, i, 0))


def _twice(acc):
    return acc, acc


def _mmc(dims, n_pairs=1, epilogue=None):
    def compute(*refs):
        acc = None
        for k in range(n_pairs):
            d = _dot(refs[2 * k][...].astype(BF16), refs[2 * k + 1][...].astype(BF16), dims)
            acc = d if acc is None else acc + d
        ext = [r[...] for r in refs[2 * n_pairs:]]
        return epilogue(acc, *ext) if epilogue is not None else (acc,)

    return compute


def _res(w):
    return (w, w.shape, functools.partial(lambda i, j, nd: (0,) * nd, nd=w.ndim))


def _mmc_blocks(nblk, dims, rhs_block, epilogue=None):
    def compute(in_refs, out_refs):
        a = in_refs[0][...].astype(BF16)
        for d in range(nblk):
            acc = _dot(a, rhs_block(in_refs[1], d).astype(BF16), dims)
            n = acc.shape[1]
            ext = [r[:, d * n:(d + 1) * n] for r in in_refs[2:]]
            res = epilogue(acc, *ext) if epilogue is not None else (acc,)
            for o_ref, r in zip(out_refs, res):
                o_ref[:, d * n:(d + 1) * n] = r.astype(o_ref.dtype)

    return compute


def _rowwise(name, body, rows, consts, out_rows, out_accs=(), tr=512, deps=()):
    T = rows[0][0].shape[0]
    tr = min(tr, T)
    deps = _deps(deps)
    nr, ncn, no, nd = len(rows), len(consts), len(out_rows), len(deps)

    def kern(*refs):
        accs = refs[nr + ncn + nd + no:]
        if accs:
            @pl.when(pl.program_id(0) == 0)
            def _():
                for a in accs:
                    a[...] = jnp.zeros(a.shape, a.dtype)
        body(refs[:nr], refs[nr:nr + ncn], refs[nr + ncn + nd:nr + ncn + nd + no], accs)

    in_specs = [pl.BlockSpec((tr, w), functools.partial(lambda i, cb: (i, cb), cb=cb)) for _, w, cb in rows]
    in_specs += [pl.BlockSpec(c.shape, functools.partial(lambda i, nd: (0,) * nd, nd=c.ndim), pipeline_mode=pl.Buffered(1))
                 for c in consts]
    in_specs += [ANY_SPEC] * nd
    out_specs = [pl.BlockSpec((tr, w), lambda i: (i, 0)) for w, _ in out_rows]
    out_specs += [pl.BlockSpec(s, functools.partial(lambda i, nd: (0,) * nd, nd=len(s))) for s, _ in out_accs]
    out_shape = [pltpu.HBM((T, w), dt) for w, dt in out_rows]
    out_shape += [pltpu.HBM(s, dt) for s, dt in out_accs]
    nbytes = sum(_nbytes((tr, w), a.dtype) for a, w, _ in rows) + sum(_nbytes(c.shape, c.dtype) for c in consts)
    nbytes += sum(_nbytes((tr, w), dt) for w, dt in out_rows) + sum(_nbytes(s, dt) for s, dt in out_accs)
    res = pl.pallas_call(
        kern, name=name, grid=(T // tr,), in_specs=in_specs, out_specs=out_specs, out_shape=out_shape,
        compiler_params=pltpu.CompilerParams(dimension_semantics=("arbitrary",), vmem_limit_bytes=_vmem(nbytes)),
    )(*[_hbm(a) for a, _, _ in rows], *[_hbm(c) for c in consts], *deps)
    return res if len(res) > 1 else res[0]


def _full(a):
    return (a, a.shape[1], 0)


def _ln_stats(y):
    mu = jnp.mean(y, axis=-1, keepdims=True)
    yc = y - mu
    r = lax.rsqrt(jnp.mean(yc * yc, axis=-1, keepdims=True) + EPS)
    return yc * r, r


def _row_halves(n):
    return [slice(0, n // 2), slice(n // 2, n)] if n >= 256 else [slice(0, n)]


def _ln_back(dh, xh, r, gain, dg_ref, db_ref):
    dg_ref[...] += jnp.sum(dh * xh, axis=0, keepdims=True)
    db_ref[...] += jnp.sum(dh, axis=0, keepdims=True)
    dx = dh * gain
    return r * (dx - jnp.mean(dx, axis=-1, keepdims=True) - xh * jnp.mean(dx * xh, axis=-1, keepdims=True))


def _proj_ln(name, acts, weights, h_in, g, b, layer, deps=()):
    n = len(acts)

    def body(rows, consts, outs, accs):
        acc = None
        for k in range(n):
            d = _dot(rows[k][...].astype(BF16), consts[k][...], NN)
            acc = d if acc is None else acc + d
        y = ALPHA * rows[n][...] + acc
        xh, _ = _ln_stats(y)
        h = xh * consts[n][layer:layer + 1, :] + consts[n + 1][layer:layer + 1, :]
        outs[0][...] = y
        outs[1][...] = h
        outs[2][...] = h.astype(BF16)

    return _rowwise(name, body, [_full(a) for a in acts] + [_full(h_in)], [*weights, g, b],
                    [(D_MODEL, F32), (D_MODEL, F32), (D_MODEL, BF16)], tr=TM, deps=deps)


def _proj_ln_loss(name, act, w2, h_in, g, b, layer, target):
    def body(rows, consts, outs, accs):
        y = ALPHA * rows[1][...] + _dot(rows[0][...], consts[0][...], NN)
        xh, r = _ln_stats(y)
        gain = consts[1][layer:layer + 1, :]
        err = xh * gain + consts[2][layer:layer + 1, :] - rows[2][...]
        accs[0][...] += jnp.sum(err * err, axis=0, keepdims=True)
        dy = _ln_back(err * (1.0 / D_MODEL), xh, r, gain, accs[1], accs[2])
        outs[0][...] = dy
        outs[1][...] = dy.astype(BF16)

    return _rowwise(name, body, [_full(act), _full(h_in), _full(target)], [w2, g, b], [(D_MODEL, F32), (D_MODEL, BF16)],
                    [((1, D_MODEL), F32)] * 3, tr=TM)


def _dh_ln_back(name, da, w, dy_next, y, g, layer, proj=(), deps=()):
    def body(rows, consts, outs, accs):
        n = consts[0].shape[2]
        for sl in _row_halves(rows[0].shape[0]):
            acc = ALPHA * rows[1][sl, :]
            for d in range(N_DEV):
                acc = acc + _dot(rows[0][sl, d * n:(d + 1) * n], consts[0][d], NT)
            xh, r = _ln_stats(rows[2][sl, :])
            dy = _ln_back(acc, xh, r, consts[1][layer:layer + 1, :], accs[0], accs[1])
            outs[0][sl, :] = dy
            dy_bf = dy.astype(BF16)
            outs[1][sl, :] = dy_bf
            off = 0
            for k, p in enumerate(proj):
                outs[2][sl, off:off + p.shape[0]] = _dot(dy_bf, consts[2 + k][...], NT).astype(BF16)
                off += p.shape[0]

    out_rows = [(D_MODEL, F32), (D_MODEL, BF16)] + ([(sum(p.shape[0] for p in proj), BF16)] if proj else [])
    return _rowwise(name, body, [_full(da), _full(dy_next), _full(y)], [w, g, *proj], out_rows,
                    [((1, D_MODEL), F32)] * 2, tr=TM, deps=deps)


def _relu2_epilogue(acc):
    a = jnp.maximum(acc, 0.0)
    return acc, a * a


def _mlp_up(tag, h_bf, w1):
    T = h_bf.shape[0]
    tm = min(TM, T)
    return _tiled(f"{tag}_ff1", (1, T // tm), [_rb(h_bf, tm), _res(w1)],
                  [_out(T, D_FF, BF16, tm, D_FF), _out(T, D_FF, BF16, tm, D_FF)],
                  _mmc_blocks(N_DEV, NN, lambda w, d: w[d], epilogue=_relu2_epilogue), direct=True)


def _mlp_bwd_w(tag, h_bf, a, act, dff_bf, w2, deps=()):
    T = h_bf.shape[0]
    tm = min(TM, T)
    da = _tiled(f"{tag}_dact", (1, T // tm), [_rb(dff_bf, tm), _res(w2), _rb(a, tm)], [_out(T, D_FF, BF16, tm, D_FF)],
                _mmc_blocks(N_DEV, NT, lambda w, d: w[d], epilogue=lambda acc, a_t: (acc * 2.0 * jnp.maximum(a_t.astype(F32), 0.0),)),
                direct=True, deps=deps)
    dw2 = _tiled(f"{tag}_dw2", (1, D_FF // TM), [_tl(act, TM), _res(dff_bf)],
                 [_out(D_FF, D_MODEL, F32, TM, D_MODEL), _out(D_FF, D_MODEL, BF16, TM, D_MODEL)], _mmc(TN_, epilogue=_twice))
    dw1 = _tiled(f"{tag}_dw1", (N_DEV, 1), [_res(h_bf), _cw(da, TN)],
                 [_out_dev(D_MODEL, TN, D_MODEL), _out_dev(D_MODEL, TN, D_MODEL, BF16)], _mmc(TN_, epilogue=_twice))
    return da, dw1, [a.reshape(N_DEV, D_FF // N_DEV, D_MODEL) for a in dw2]


def _rope_tables(positions_col, invf_lane):
    def body(rows, consts, outs, accs):
        ang = rows[0][...].astype(F32) * consts[0][...]
        c, s = jnp.cos(ang), jnp.sin(ang)
        lane = lax.broadcasted_iota(jnp.int32, ang.shape, 1)
        outs[0][...] = jnp.where(lane < 64, 1.0, jnp.where(lane < 96, c, 0.0))
        outs[1][...] = jnp.where((lane >= 64) & (lane < 80), -s, 0.0)
        outs[2][...] = jnp.where((lane >= 80) & (lane < 96), s, 0.0)

    return _rowwise("rope_tables", body, [_full(positions_col)], [invf_lane], [(HEAD_W, F32)] * 3)


def _rope(x, c, s1, s2):
    return x * c + pltpu.roll(x, 112, 1) * s1 + pltpu.roll(x, 16, 1) * s2


def _rope_t(dx, c, s1, s2):
    return dx * c + pltpu.roll(dx * s1, 16, 1) + pltpu.roll(dx * s2, 112, 1)


def _rms(c):
    r = lax.rsqrt(jnp.mean(c * c, axis=-1, keepdims=True) + EPS)
    return c * r, r


def _rope_heads(x, c, s1, s2, fn):
    return jnp.concatenate([fn(x[:, h * HEAD_W:(h + 1) * HEAD_W], c, s1, s2) for h in range(HEADS)], axis=1)


def _mla_in(x, wm, ws, tabs, gq, gkv, deps=()):
    def body(rows, consts, outs, accs):
        xb = rows[0][...].astype(BF16)
        zm = _dot(xb, consts[0][...], NT)
        outs[0][...] = zm
        outs[1][...] = _dot(xb, consts[1][...], NT)
        outs[2][...] = (_rms(zm[:, 0:256])[0] * consts[2][...]).astype(BF16)
        outs[3][...] = (_rms(zm[:, 256:512])[0] * consts[3][...]).astype(BF16)
        outs[4][...] = _rope(zm[:, 512:640], rows[1][...], rows[2][...], rows[3][...])

    return _rowwise("l0_in", body, [_full(x)] + [_full(t) for t in tabs], [wm, ws, gq, gkv],
                    [(640, F32), (1024, F32), (256, BF16), (256, BF16), (HEAD_W, F32)], deps=deps)


def _mla_qkv(cqn, ckvn, kr_rot, tabs, wq, wk, wv):
    def body(rows, consts, outs, accs):
        c, s1, s2 = rows[3][...], rows[4][...], rows[5][...]
        outs[0][...] = _rope_heads(_dot(rows[0][...], consts[0][...], NT), c, s1, s2, _rope).astype(BF16)
        outs[1][...] = (_dot(rows[1][...], consts[1][...], NN) + jnp.concatenate([rows[2][...]] * HEADS, axis=1)).astype(BF16)
        outs[2][...] = _dot(rows[1][...], consts[2][...], NN).astype(BF16)

    rows = [_full(cqn), _full(ckvn), _full(kr_rot)] + [_full(t) for t in tabs]
    return _rowwise("l0_qkv", body, rows, [wq, wk, wv], [(HEADS * HEAD_W, BF16)] * 3)


def _mla_back(zm, cqn, ckvn, tabs, gq, gkv, wq, wk, wv, dq, dk, dv):
    def body(rows, consts, outs, accs):
        c, s1, s2 = rows[4][...], rows[5][...], rows[6][...]
        dk_t, dv_bf = rows[8][...], rows[9][...].astype(BF16)
        dq_bf = _rope_heads(rows[7][...], c, s1, s2, _rope_t).astype(BF16)
        dk_bf = dk_t.astype(BF16)
        accs[0][...] += _dot(dq_bf, rows[2][...], TN_)
        accs[1][...] += _dot(rows[3][...], dk_bf, TN_)
        accs[2][...] += _dot(rows[3][...], dv_bf, TN_)
        dlat = [_dot(dq_bf, consts[2][...], NN), _dot(dk_bf, consts[3][...], NT) + _dot(dv_bf, consts[4][...], NT)]
        for k in range(2):
            ch, r = _rms(rows[k][...])
            accs[3 + k][...] += jnp.sum(dlat[k] * ch, axis=0, keepdims=True)
            dc = dlat[k] * consts[k][...]
            outs[0][:, 256 * k:256 * (k + 1)] = (r * (dc - ch * jnp.mean(dc * ch, axis=-1, keepdims=True))).astype(BF16)
        dks = dk_t[:, 0:HEAD_W]
        for h in range(1, HEADS):
            dks = dks + dk_t[:, h * HEAD_W:(h + 1) * HEAD_W]
        lane = lax.broadcasted_iota(jnp.int32, dks.shape, 1)
        dks = jnp.where((lane >= 64) & (lane < 96), dks, 0.0)
        outs[0][:, 512:640] = _rope_t(dks, c, s1, s2).astype(BF16)

    rows = [(zm, 256, 0), (zm, 256, 1), _full(cqn), _full(ckvn)] + [_full(t) for t in tabs] + [_full(dq), _full(dk), _full(dv)]
    wide = HEADS * HEAD_W
    return _rowwise("l0_mla_back", body, rows, [gq, gkv, wq, wk, wv], [(640, BF16)],
                    [((wide, MLA_LORA), F32)] + [((MLA_LORA, wide), F32)] * 2 + [((1, MLA_LORA), F32)] * 2, tr=256)


def _in_back(x, dzm, dzs, dy, wm, ws, deps=()):
    def body(rows, consts, outs, accs):
        dzm_t, dzs_t = rows[1][...], rows[2][...]
        outs[0][...] = _dot(dzm_t, consts[0][...], NN) + _dot(dzs_t, consts[1][...], NN) + ALPHA * rows[3][...]
        xb = rows[0][...].astype(BF16)
        accs[0][...] += _dot(dzm_t, xb, TN_)
        accs[1][...] += _dot(dzs_t, xb, TN_)

    return _rowwise("l0_in_back", body, [_full(x), _full(dzm), _full(dzs), _full(dy)], [wm, ws], [(D_MODEL, F32)],
                    [((640, D_MODEL), F32), ((1024, D_MODEL), F32)], deps=deps)


def _out_weight_grads(o_att, b_out, dy_bf):
    def body(rows, consts, outs, accs):
        d = rows[2][...]
        accs[0][...] += _dot(rows[0][...].astype(BF16), d, TN_)
        accs[1][...] += _dot(rows[1][...], d, TN_)

    return _rowwise("l0_dw_out", body, [_full(o_att), _full(b_out), _full(dy_bf)], [], [],
                    [((HEADS * HEAD_W, D_MODEL), F32), ((SGU_DIM, D_MODEL), F32)])


def _attn_block(T):
    return min(1024, T)


def _attn_fwd(q, k, v):
    T = q.shape[0]
    BQ = _attn_block(T)
    nq = T // BQ

    def kern(q_ref, k_ref, v_ref, o_ref, lse_ref):
        def step(i, j, carry, masked):
            m, l, acc = carry
            qb = q_ref[pl.ds(pl.multiple_of(i * BQ, BQ), BQ), :]
            kb = k_ref[pl.ds(pl.multiple_of(j * BQ, BQ), BQ), :]
            vb = v_ref[pl.ds(pl.multiple_of(j * BQ, BQ), BQ), :]
            s = _dot(qb, kb, NT) * MLA_SCALE
            if masked:
                row = lax.broadcasted_iota(jnp.int32, s.shape, 0)
                col = lax.broadcasted_iota(jnp.int32, s.shape, 1)
                s = jnp.where(col <= row, s, -1e30)
            m_new = jnp.maximum(m, jnp.max(s, axis=-1, keepdims=True))
            p = jnp.exp(s - m_new)
            a = jnp.exp(m - m_new)
            l = a * l + jnp.sum(p, axis=-1, keepdims=True)
            acc = a * acc + _dot(p.astype(BF16), vb, NN)
            return m_new, l, acc

        def qloop(i, _):
            init = (jnp.full((BQ, 1), -1e30, F32), jnp.zeros((BQ, 1), F32), jnp.zeros((BQ, HEAD_W), F32))
            carry = lax.fori_loop(0, i, lambda j, c: step(i, j, c, False), init)
            m, l, acc = step(i, i, carry, True)
            rows = pl.ds(pl.multiple_of(i * BQ, BQ), BQ)
            o_ref[rows, :] = acc / l
            lse_ref[0, rows, :] = m + jnp.log(l)
            return 0

        lax.fori_loop(0, nq, qloop, 0)

    head = pl.BlockSpec((T, HEAD_W), lambda h: (0, h))
    nbytes = 3 * _nbytes((T, HEAD_W), BF16) + _nbytes((T, HEAD_W), F32) + _nbytes((T, 128), F32)
    return pl.pallas_call(
        kern, name="attn_fwd", grid=(HEADS,), in_specs=[head, head, head],
        out_specs=[head, pl.BlockSpec((1, T, 1), lambda h: (h, 0, 0))],
        out_shape=[pltpu.HBM((T, HEADS * HEAD_W), F32), pltpu.HBM((HEADS, T, 1), F32)],
        compiler_params=pltpu.CompilerParams(dimension_semantics=("parallel",), vmem_limit_bytes=_vmem(nbytes)),
    )(_hbm(q), _hbm(k), _hbm(v))


def _attn_bwd(q, k, v, o, lse, dcat, deps=()):
    T = q.shape[0]
    BQ = _attn_block(T)
    nq = T // BQ
    deps = _deps(deps)

    def kern(q_ref, k_ref, v_ref, o_ref, lse_ref, do_ref, *rest):
        dq_ref, dk_ref, dv_ref, dd_ref = rest[len(deps):]
        dq_ref[...] = jnp.zeros(dq_ref.shape, F32)

        def dloop(i, _):
            rows = pl.ds(pl.multiple_of(i * BQ, BQ), BQ)
            dd_ref[rows, :] = jnp.sum(do_ref[rows, :].astype(F32) * o_ref[rows, :], axis=-1, keepdims=True)
            return 0

        lax.fori_loop(0, nq, dloop, 0)

        def tile(q0, k0, n, carry, masked):
            dk_acc, dv_acc = carry
            rq = pl.ds(pl.multiple_of(q0, n), n)
            rk = pl.ds(pl.multiple_of(k0, n), n)
            qb, kb, vb, dob = q_ref[rq, :], k_ref[rk, :], v_ref[rk, :], do_ref[rq, :]
            s = _dot(qb, kb, NT) * MLA_SCALE
            p = jnp.exp(s - lse_ref[0, rq, :])
            if masked:
                row = lax.broadcasted_iota(jnp.int32, s.shape, 0)
                col = lax.broadcasted_iota(jnp.int32, s.shape, 1)
                p = jnp.where(col <= row, p, 0.0)
            dp = _dot(dob, vb, NT)
            ds = (p * (dp - dd_ref[rq, :]) * MLA_SCALE).astype(BF16)
            dv_acc = dv_acc + _dot(p.astype(BF16), dob, TN_)
            dk_acc = dk_acc + _dot(ds, qb, TN_)
            dq_ref[rq, :] += _dot(ds, kb, NN)
            return dk_acc, dv_acc

        def kloop(j, _):
            base, half = j * BQ, BQ // 2
            zero = (jnp.zeros((half, HEAD_W), F32), jnp.zeros((half, HEAD_W), F32))
            early = tile(base + half, base, half, tile(base, base, half, zero, True), False)
            late = tile(base + half, base + half, half, zero, True)
            carry = tuple(jnp.concatenate([a, b], axis=0) for a, b in zip(early, late))
            dk_acc, dv_acc = lax.fori_loop(j + 1, nq, lambda i, c: tile(i * BQ, base, BQ, c, False), carry)
            rk = pl.ds(pl.multiple_of(j * BQ, BQ), BQ)
            dk_ref[rk, :] = dk_acc
            dv_ref[rk, :] = dv_acc
            return 0

        lax.fori_loop(0, nq, kloop, 0)

    head = pl.BlockSpec((T, HEAD_W), lambda h: (0, h))
    nbytes = 4 * _nbytes((T, HEAD_W), BF16) + 5 * _nbytes((T, HEAD_W), F32) + 2 * _nbytes((T, 128), F32)
    return pl.pallas_call(
        kern, name="attn_bwd", grid=(HEADS,),
        in_specs=[head, head, head, head, pl.BlockSpec((1, T, 1), lambda h: (h, 0, 0)), head] + [ANY_SPEC] * len(deps),
        out_specs=[head, head, head],
        out_shape=[pltpu.HBM((T, HEADS * HEAD_W), F32)] * 3,
        scratch_shapes=[pltpu.VMEM((T, 1), F32)],
        compiler_params=pltpu.CompilerParams(dimension_semantics=("parallel",), vmem_limit_bytes=_vmem(nbytes)),
    )(*[_hbm(a) for a in (q, k, v, o, lse, dcat)], *deps)


def _sgu_common(u, v, ln_g, ln_b):
    ua, tu = _gelu(u)
    va, tv = _gelu(v)
    vh, r = _ln_stats(va)
    return ua, tu, tv, vh, r, vh * ln_g + ln_b


def _tril_mask(n):
    return lax.broadcasted_iota(jnp.int32, (n, n), 1) <= lax.broadcasted_iota(jnp.int32, (n, n), 0)


def _sgu_fwd(zs, ln_g, ln_b, w, bias_full):
    def body(rows, consts, outs, accs):
        ua, _, _, _, _, vn = _sgu_common(rows[0][...], rows[1][...], consts[0][...], consts[1][...])
        vn = vn.astype(BF16)
        tri = _tril_mask(SGU_CHUNK)
        for g in range(SGU_G):
            wg = jnp.where(tri, consts[2][0, g], 0.0).astype(BF16)
            cols = slice(g * 128, (g + 1) * 128)
            for c in range(ua.shape[0] // SGU_CHUNK):
                rws = slice(c * SGU_CHUNK, (c + 1) * SGU_CHUNK)
                mixed = _dot(wg, vn[rws, cols], NN) + consts[3][:, cols]
                outs[0][rws, cols] = (ua[rws, cols] * mixed).astype(BF16)

    return _rowwise("sgu_fwd", body, [(zs, 512, 0), (zs, 512, 1)], [ln_g, ln_b, w, bias_full], [(SGU_DIM, BF16)])


def _sgu_bwd(zs, dcat, ln_g, ln_b, w, bias_full):
    def body(rows, consts, outs, accs):
        u, v = rows[0][...], rows[1][...]
        ua, tu, tv, vh, r, vn = _sgu_common(u, v, consts[0][...], consts[1][...])
        dout = rows[2][...].astype(F32)
        vn_bf = vn.astype(BF16)
        tri = _tril_mask(SGU_CHUNK)
        dmixed = (dout * ua)
        dmixed_bf = dmixed.astype(BF16)
        ones = jnp.ones((8, SGU_CHUNK), F32)
        dvn_cols, mixed_cols = [], []
        for g in range(SGU_G):
            wg = jnp.where(tri, consts[2][0, g], 0.0).astype(BF16)
            cols = slice(g * 128, (g + 1) * 128)
            dvn_rows, mixed_rows = [], []
            dw = jnp.zeros((SGU_CHUNK, SGU_CHUNK), F32)
            dmix_sum = jnp.zeros((SGU_CHUNK, 128), F32)
            for c in range(u.shape[0] // SGU_CHUNK):
                rws = slice(c * SGU_CHUNK, (c + 1) * SGU_CHUNK)
                mixed_rows.append(_dot(wg, vn_bf[rws, cols], NN) + consts[3][:, cols])
                dvn_rows.append(_dot(wg, dmixed_bf[rws, cols], TN_))
                dw = dw + _dot(dmixed_bf[rws, cols], vn_bf[rws, cols], NT)
                dmix_sum = dmix_sum + dmixed[rws, cols]
            accs[0][g] += jnp.where(tri, dw, 0.0)
            accs[3][g:g + 1, :] += _dot(ones, dmix_sum, NT, precision=HIGHEST)[0:1, :]
            dvn_cols.append(jnp.concatenate(dvn_rows, axis=0))
            mixed_cols.append(jnp.concatenate(mixed_rows, axis=0))
        dvn = jnp.concatenate(dvn_cols, axis=1)
        mixed = jnp.concatenate(mixed_cols, axis=1)
        accs[1][...] += jnp.sum(dvn * vh, axis=0, keepdims=True)
        accs[2][...] += jnp.sum(dvn, axis=0, keepdims=True)
        dvh = dvn * consts[0][...]
        dva = r * (dvh - jnp.mean(dvh, axis=-1, keepdims=True) - vh * jnp.mean(dvh * vh, axis=-1, keepdims=True))
        outs[0][:, 0:512] = (dout * mixed * _gelu_grad(u, tu)).astype(BF16)
        outs[0][:, 512:1024] = (dva * _gelu_grad(v, tv)).astype(BF16)

    return _rowwise("sgu_bwd", body, [(zs, 512, 0), (zs, 512, 1), (dcat, 512, 2)], [ln_g, ln_b, w, bias_full], [(1024, BF16)],
                    [((SGU_G, 128, 128), F32), ((1, SGU_DIM), F32), ((1, SGU_DIM), F32), ((SGU_G, 128), F32)], tr=256)


def _lower_bound(hg_lb):
    a0, a1 = hg_lb[0:1, :], hg_lb[1:2, :]
    m = jnp.maximum(a0, a1)
    e0, e1 = jnp.exp(a0 - m), jnp.exp(a1 - m)
    s0, s1 = e0 / (e0 + e1), e1 / (e0 + e1)
    return (s0 + s1) - s0, s0, s1


def _prefix_rows(x, reverse=False):
    n = x.shape[0]
    row = lax.broadcasted_iota(jnp.int32, x.shape, 0)
    s = 1
    while s < n:
        if reverse:
            x = x + jnp.where(row < n - s, pltpu.roll(x, n - s, 0), 0.0)
        else:
            x = x + jnp.where(row >= s, pltpu.roll(x, s, 0), 0.0)
        s *= 2
    return x


def _hg_gates(qr, fr, lb):
    C = qr.shape[0]
    sq = _sig(qr)
    qf = qr * sq
    sf = _sig(fr)
    gate = lb + (1.0 - lb) * sf
    kk = 1.0 - gate
    tri = _tril_mask(C)
    b = _prefix_rows(jnp.log(gate))
    bref = b[C // 2 - 1:C // 2, :]
    bl = b[C - 1:C, :]
    e_b = jnp.exp(b)
    e_q = jnp.exp(b - bref)
    e_k = jnp.exp(bref - b)
    e_lb = jnp.exp(bl - b)
    return dict(sq=sq, qf=qf, sf=sf, gate=gate, kk=kk, tri=tri, bl=bl, e_b=e_b, e_q=e_q, e_k=e_k, e_lb=e_lb)


def _hgrn_fwd(z1, hg_lb, gnorm):
    T = z1.shape[0]
    C = min(HG_CHUNK, T)
    nc = T // C
    ns = HG_CHUNKS_PER_STEP if nc % HG_CHUNKS_PER_STEP == 0 else 1
    R = ns * C

    def kern(q_ref, f_ref, i_ref, g_ref, lb_ref, gn_ref, o_ref, hg_ref, st_ref, s_scr):
        @pl.when(pl.program_id(0) == 0)
        def _():
            s_scr[...] = jnp.zeros(s_scr.shape, F32)

        lb_all, _, _ = _lower_bound(lb_ref[...])
        for sub in range(ns):
            rows = slice(sub * C, (sub + 1) * C)
            st_ref[sub] = s_scr[...]
            for h in range(HEADS):
                cols = slice(h * HEAD_W, (h + 1) * HEAD_W)
                t = _hg_gates(q_ref[rows, cols], f_ref[rows, cols], lb_all[:, cols])
                v_bf = i_ref[rows, cols].astype(BF16)
                st = s_scr[h]
                a = jnp.where(t["tri"], _dot((t["qf"] * t["e_q"]).astype(BF16), (t["kk"] * t["e_k"]).astype(BF16), NT), 0.0)
                o = _dot(a.astype(BF16), v_bf, NN) + _dot((t["qf"] * t["e_b"]).astype(BF16), st.astype(BF16), NT)
                s_scr[h] = st * jnp.exp(t["bl"]) + _dot(v_bf, (t["kk"] * t["e_lb"]).astype(BF16), TN_)
                o_ref[rows, cols] = o
                gr = g_ref[rows, cols]
                r = lax.rsqrt(jnp.mean(o * o, axis=-1, keepdims=True) + EPS)
                hg_ref[rows, cols] = (o * r * gn_ref[:, cols] * (gr * _sig(gr))).astype(BF16)

    seg = lambda k: pl.BlockSpec((R, D_MODEL), functools.partial(lambda n, k: (n, k), k=k))
    row = pl.BlockSpec((R, D_MODEL), lambda n: (n, 0))
    nbytes = 6 * _nbytes((R, D_MODEL), F32) + (2 + ns) * _nbytes((HEADS, 128, 128), F32)
    return pl.pallas_call(
        kern, name="hgrn_fwd", grid=(nc // ns,),
        in_specs=[seg(0), seg(1), seg(2), seg(3), pl.BlockSpec((2, D_MODEL), lambda n: (0, 0)),
                  pl.BlockSpec((1, D_MODEL), lambda n: (0, 0))],
        out_specs=[row, row, pl.BlockSpec((ns, HEADS, 128, 128), lambda n: (n, 0, 0, 0))],
        out_shape=[pltpu.HBM((T, D_MODEL), F32), pltpu.HBM((T, D_MODEL), BF16),
                   pltpu.HBM((nc, HEADS, 128, 128), F32)],
        scratch_shapes=[pltpu.VMEM((HEADS, 128, 128), F32)],
        compiler_params=pltpu.CompilerParams(dimension_semantics=("arbitrary",), vmem_limit_bytes=_vmem(nbytes)),
    )(*[_hbm(a) for a in (z1, z1, z1, z1, hg_lb, gnorm)])


def _hgrn_bwd(z1, o_pre, dhg, states, hg_lb, gnorm):
    T = z1.shape[0]
    C = min(HG_CHUNK, T)
    nc = T // C
    ns = HG_CHUNKS_PER_STEP if nc % HG_CHUNKS_PER_STEP == 0 else 1
    R, steps = ns * C, nc // ns

    def kern(q_ref, f_ref, i_ref, g_ref, o_ref, dhg_ref, st_ref, lb_ref, gn_ref, dz_ref, dlb_ref, dgn_ref, ds_scr, dlb_scr):
        n = pl.program_id(0)

        @pl.when(n == 0)
        def _():
            ds_scr[...] = jnp.zeros(ds_scr.shape, F32)
            dlb_scr[...] = jnp.zeros(dlb_scr.shape, F32)
            dgn_ref[...] = jnp.zeros(dgn_ref.shape, F32)

        lb_all, s0, s1 = _lower_bound(lb_ref[...])
        for sub in reversed(range(ns)):
            rows = slice(sub * C, (sub + 1) * C)
            for h in range(HEADS):
                cols = slice(h * HEAD_W, (h + 1) * HEAD_W)
                lb = lb_all[:, cols]
                qr, fr = q_ref[rows, cols], f_ref[rows, cols]
                t = _hg_gates(qr, fr, lb)
                tri = t["tri"]
                v_bf = i_ref[rows, cols].astype(BF16)
                st_bf = st_ref[sub, h].astype(BF16)
                dst = ds_scr[h]
                dst_bf = dst.astype(BF16)
                o = o_ref[rows, cols]
                gr = g_ref[rows, cols]
                sg = _sig(gr)
                sil = gr * sg
                gn = gn_ref[:, cols]
                r = lax.rsqrt(jnp.mean(o * o, axis=-1, keepdims=True) + EPS)
                on = o * r
                dh = dhg_ref[rows, cols].astype(F32)
                dgn_ref[:, cols] += jnp.sum(dh * on * sil, axis=0, keepdims=True)
                dg = dh * on * gn * (sg * (1.0 + gr * (1.0 - sg)))
                don = dh * gn * sil
                do_bf = (r * (don - on * jnp.mean(don * on, axis=-1, keepdims=True))).astype(BF16)
                qe = (t["qf"] * t["e_q"]).astype(BF16)
                ke = (t["kk"] * t["e_k"]).astype(BF16)
                qb = (t["qf"] * t["e_b"]).astype(BF16)
                kh_bf = (t["kk"] * t["e_lb"]).astype(BF16)
                a_bf = jnp.where(tri, _dot(qe, ke, NT), 0.0).astype(BF16)
                da_bf = jnp.where(tri, _dot(do_bf, v_bf, NT), 0.0).astype(BF16)
                dv = _dot(a_bf, do_bf, TN_) + _dot(kh_bf, dst_bf, NT)
                dqe = _dot(da_bf, ke, NN)
                dqb = _dot(do_bf, st_bf, NN)
                dke = _dot(da_bf, qe, TN_)
                dkh = _dot(v_bf, dst_bf, NN)
                dqf = dqe * t["e_q"] + dqb * t["e_b"]
                dkk = dke * t["e_k"] + dkh * t["e_lb"]
                kh_r = kh_bf.astype(F32)
                db = qe.astype(F32) * dqe - ke.astype(F32) * dke + qb.astype(F32) * dqb - kh_r * dkh
                e_bl = jnp.exp(t["bl"])
                dbl = jnp.sum(dkh * kh_r, axis=0, keepdims=True) + e_bl * jnp.sum(st_ref[sub, h] * dst, axis=0, keepdims=True)
                dlg = _prefix_rows(db, reverse=True) + dbl
                ds_scr[h] = dst * e_bl + _dot(do_bf, qb, TN_)
                dgate = dlg / t["gate"] - dkk
                sf = t["sf"]
                dlb_scr[:, cols] += jnp.sum(dgate * (1.0 - sf), axis=0, keepdims=True)
                df = dgate * (1.0 - lb) * sf * (1.0 - sf)
                dq = dqf * (t["sq"] * (1.0 + qr * (1.0 - t["sq"])))
                dz_ref[rows, cols] = dq.astype(BF16)
                dz_ref[rows, D_MODEL + h * HEAD_W:D_MODEL + (h + 1) * HEAD_W] = df.astype(BF16)
                dz_ref[rows, 2 * D_MODEL + h * HEAD_W:2 * D_MODEL + (h + 1) * HEAD_W] = dv.astype(BF16)
                dz_ref[rows, 3 * D_MODEL + h * HEAD_W:3 * D_MODEL + (h + 1) * HEAD_W] = dg.astype(BF16)

        @pl.when(n == steps - 1)
        def _():
            d = s0 * s1 * dlb_scr[...]
            dlb_ref[0:1, :] = -d
            dlb_ref[1:2, :] = d

    seg = lambda k: pl.BlockSpec((R, D_MODEL), functools.partial(lambda n, k: (steps - 1 - n, k), k=k))
    nbytes = 6 * _nbytes((R, D_MODEL), F32) + _nbytes((R, 4 * D_MODEL), BF16) + (2 + ns) * _nbytes((HEADS, 128, 128), F32)
    return pl.pallas_call(
        kern, name="hgrn_bwd", grid=(steps,),
        in_specs=[seg(0), seg(1), seg(2), seg(3), seg(0), seg(0),
                  pl.BlockSpec((ns, HEADS, 128, 128), lambda n: (steps - 1 - n, 0, 0, 0)),
                  pl.BlockSpec((2, D_MODEL), lambda n: (0, 0)), pl.BlockSpec((1, D_MODEL), lambda n: (0, 0))],
        out_specs=[pl.BlockSpec((R, 4 * D_MODEL), lambda n: (steps - 1 - n, 0)),
                   pl.BlockSpec((2, D_MODEL), lambda n: (0, 0)), pl.BlockSpec((1, D_MODEL), lambda n: (0, 0))],
        out_shape=[pltpu.HBM((T, 4 * D_MODEL), BF16), pltpu.HBM((2, D_MODEL), F32),
                   pltpu.HBM((1, D_MODEL), F32)],
        scratch_shapes=[pltpu.VMEM((HEADS, 128, 128), F32), pltpu.VMEM((1, D_MODEL), F32)],
        compiler_params=pltpu.CompilerParams(dimension_semantics=("arbitrary",), vmem_limit_bytes=_vmem(nbytes)),
    )(*[_hbm(a) for a in (z1, z1, z1, z1, o_pre, dhg, states, hg_lb, gnorm)])


def _prep_weights(gw):
    w_in_e = gw["w_in_e"].reshape(1568, D_MODEL)
    kr = jnp.pad(w_in_e[512:544], ((64, 32), (0, 0)))
    wm = jnp.concatenate([w_in_e[0:512], kr], axis=0)
    ws = w_in_e[544:1568]
    wq = jnp.pad(gw["w_qb"], ((0, 0), (0, 32), (0, 0))).reshape(HEADS * HEAD_W, MLA_LORA)
    kvb = gw["w_kvb"].transpose(1, 0, 2).reshape(MLA_LORA, HEADS, 128)
    wk = jnp.pad(kvb[:, :, :64], ((0, 0), (0, 0), (0, 64))).reshape(MLA_LORA, HEADS * HEAD_W)
    wv = jnp.pad(kvb[:, :, 64:], ((0, 0), (0, 0), (0, 64))).reshape(MLA_LORA, HEADS * HEAD_W)
    w_out_e = gw["w_out_e"].reshape(D_MODEL, D_MODEL)
    woa = jnp.pad(w_out_e[:512].reshape(HEADS, 64, D_MODEL), ((0, 0), (0, 64), (0, 0))).reshape(HEADS * HEAD_W, D_MODEL)
    return dict(wm=wm, ws=ws, wq=wq, wk=wk, wv=wv, woa=woa, wob=w_out_e[512:])


def _unprep_grads(g):
    dwm, dws = g["wm"], g["ws"]
    d_in_e = jnp.concatenate([dwm[0:512], dwm[512 + 64:512 + 96], dws], axis=0).reshape(N_DEV, 1568 // N_DEV, D_MODEL)
    d_qb = g["wq"].reshape(HEADS, HEAD_W, MLA_LORA)[:, :96]
    dk = g["wk"].reshape(MLA_LORA, HEADS, HEAD_W)[:, :, :64]
    dv = g["wv"].reshape(MLA_LORA, HEADS, HEAD_W)[:, :, :64]
    d_kvb = jnp.concatenate([dk, dv], axis=2).reshape(MLA_LORA, HEADS * 128)
    d_oa = g["woa"].reshape(HEADS, HEAD_W, D_MODEL)[:, :64].reshape(HEADS * 64, D_MODEL)
    dev_major = lambda a: a.reshape(a.shape[0], N_DEV, a.shape[1] // N_DEV).transpose(1, 0, 2)
    return dict(w_in_e=d_in_e, w_qb=d_qb, w_kvb=dev_major(d_kvb),
                w_out_e=jnp.concatenate([d_oa, g["wob"]], axis=0).reshape(N_DEV, D_MODEL // N_DEV, D_MODEL))


def _local_step(x, positions, target, gw, sp, ex):
    w = _prep_weights(gw)
    T = x.shape[0]
    tm = min(TM, T)
    nt = T // tm
    half = MLA_ROPE // 2
    inv_freq = ROPE_BASE ** (-jnp.arange(half, dtype=F32) / half)
    invf_lane = jnp.concatenate([jnp.zeros((64,), F32), inv_freq, inv_freq, jnp.zeros((32,), F32)]).reshape(1, HEAD_W)
    tabs = _rope_tables(positions.reshape(T, 1), invf_lane)
    bias_full = jnp.repeat(sp["sgu_b"][0].T, 128, axis=1)
    sgu_w = sp["sgu_w"]
    gq, gkv = sp["mla_gq"], sp["mla_gkv"]
    ln1_g, ln1_b, ln2_g, ln2_b = sp["ln1_g"], sp["ln1_b"], sp["ln2_g"], sp["ln2_b"]
    zm, zs, cqn, ckvn, kr_rot = _mla_in(x, w["wm"], w["ws"], tabs, gq, gkv, deps=[ex.first_token])
    q, k, v = _mla_qkv(cqn, ckvn, kr_rot, tabs, w["wq"], w["wk"], w["wv"])
    o_att, lse = _attn_fwd(q, k, v)
    b_out = _sgu_fwd(zs, sp["sgu_ln_g"], sp["sgu_ln_b"], sgu_w, bias_full)
    token = ex.weights_forward(after=[o_att, b_out])
    y1, h1, h1_bf = _proj_ln("l0_out_ln1", [o_att, b_out], [w["woa"], w["wob"]], x, ln1_g, ln1_b, 0, deps=[token])
    big = ex.weights_ready(after=[y1])
    w_ff1, w_in_o, w_out_o = big["w_ff1"], big["w_in_o"], big["w_out_o"].reshape(D_MODEL, D_MODEL)
    w_ff2 = [a.reshape(D_FF, D_MODEL) for a in big["w_ff2"]]
    a0, act0 = _mlp_up("l0", h1_bf, w_ff1[0])
    y2, h2, h2_bf = _proj_ln("l0_ff2_ln2", [act0], [w_ff2[0]], h1, ln2_g, ln2_b, 0)

    z1 = _tiled("l1_in", (1, nt), [_rb(h2_bf, tm), _res(w_in_o)], [_out(T, 4 * D_MODEL, F32, tm, 4 * D_MODEL)],
                _mmc_blocks(N_DEV, NN, lambda w, d: w[d]), direct=True)
    o_pre, hg, states = _hgrn_fwd(z1, sp["hg_lb"], sp["hg_gnorm"])
    y3, h3, h3_bf = _proj_ln("l1_out_ln1", [hg], [w_out_o], h2, ln1_g, ln1_b, 1)
    a1, act1 = _mlp_up("l1", h3_bf, w_ff1[1])

    gs, g0 = {}, {}
    dy4, dy4_bf, sq_err, gs["ln2_g1"], gs["ln2_b1"] = _proj_ln_loss("l1_ff2_loss", act1, w_ff2[1], h3, ln2_g, ln2_b, 1, target)
    gs["sq_err"] = sq_err
    da1, dw1_1, dw2_1 = _mlp_bwd_w("l1", h3_bf, a1, act1, dy4_bf, big["w_ff2"][1])
    dy3, dy3_bf, dhg, gs["ln1_g1"], gs["ln1_b1"] = _dh_ln_back("l1_dh_ln1", da1, w_ff1[1], dy4, y3, ln1_g, 1, proj=[w_out_o])
    d_out_o = _tiled("l1_dwout", (2, D_MODEL // TM), [_tl(hg, TM), _cw(dy3_bf, TN)],
                     [_out(D_MODEL, D_MODEL, F32, TM, TN), _out(D_MODEL, D_MODEL, BF16, TM, TN)], _mmc(TN_, epilogue=_twice))
    d_out_o = [a.reshape(N_DEV, D_MODEL // N_DEV, D_MODEL) for a in d_out_o]
    dz1, gs["hg_lb"], gs["hg_gnorm"] = _hgrn_bwd(z1, o_pre, dhg, states, sp["hg_lb"], sp["hg_gnorm"])
    d_in_o = _tiled("l1_dwin", (N_DEV, 1), [_res(h2_bf), _cw(dz1, TN)],
                    [_out_dev(D_MODEL, TN, D_MODEL), _out_dev(D_MODEL, TN, D_MODEL, BF16)], _mmc(TN_, epilogue=_twice))
    token = ex.direct_start("l1", [dw1_1, dw2_1, d_in_o, d_out_o])

    dy2, dy2_bf, gs["ln2_g0"], gs["ln2_b0"] = _dh_ln_back("l1_dh_ln2", dz1, w_in_o, dy3, y2, ln2_g, 0, deps=[token])
    da0, dw1_0, dw2_0 = _mlp_bwd_w("l0", h1_bf, a0, act0, dy2_bf, big["w_ff2"][0])
    token = ex.direct_start("l0m", [dw1_0, dw2_0])
    dy1, dy1_bf, dcat, gs["ln1_g0"], gs["ln1_b0"] = _dh_ln_back("l0_dh_ln1", da0, w_ff1[0], dy2, y1, ln1_g, 0,
                                                                 proj=[w["woa"], w["wob"]], deps=[token])
    g0["woa"], g0["wob"] = _out_weight_grads(o_att, b_out, dy1_bf)
    dzs, gs["sgu_w"], gs["sgu_ln_g"], gs["sgu_ln_b"], gs["sgu_b"] = _sgu_bwd(zs, dcat, sp["sgu_ln_g"], sp["sgu_ln_b"], sgu_w, bias_full)
    dq, dk, dv = _attn_bwd(q, k, v, o_att, lse, dcat)
    dzm, g0["wq"], g0["wk"], g0["wv"], gs["mla_gq"], gs["mla_gkv"] = _mla_back(zm, cqn, ckvn, tabs, gq, gkv, w["wq"], w["wk"], w["wv"],
                                                                                 dq, dk, dv)
    token = ex.small_start(gs)
    dx, g0["wm"], g0["ws"] = _in_back(x, dzm, dzs, dy1, w["wm"], w["ws"], deps=[token])

    return sq_err, dx, _unprep_grads(g0), gs


def _me():
    return lax.axis_index("x"), lax.axis_index("y"), lax.axis_index("c")


ANY_SPEC = pl.BlockSpec(memory_space=pl.ANY)
HBM_SPEC = pl.BlockSpec(memory_space=pltpu.HBM)
SEM_SPEC = pl.BlockSpec(memory_space=pltpu.SEMAPHORE)
EFFECT = pltpu.SideEffectType.DATAFLOW_SIDE_EFFECTING


def _split_start(name, srcs, lands, n_sems, make_copies, after=()):
    n, m, k = len(srcs), len(lands), len(after)

    def body(*refs):
        for cp in make_copies(refs[:n], refs[n:n + m], refs[n + m + k], refs[n + m + k + 1]):
            cp.start()
        refs[-1][...] = jnp.zeros(refs[-1].shape, F32)

    out_shape = (pltpu.SemaphoreType.DMA((n_sems,)), pltpu.SemaphoreType.DMA((n_sems,)),
                 *[pltpu.HBM(a.shape, a.dtype) for a in (*srcs, *lands)], jax.ShapeDtypeStruct((8, 128), F32))
    res = pl.pallas_call(
        body, name=name, out_shape=out_shape, in_specs=[HBM_SPEC] * (n + m) + [ANY_SPEC] * k,
        out_specs=(SEM_SPEC, SEM_SPEC, *[HBM_SPEC] * (n + m), pl.BlockSpec(memory_space=pltpu.VMEM)),
        input_output_aliases={i: 2 + i for i in range(n + m)},
        compiler_params=pltpu.CompilerParams(has_side_effects=EFFECT),
    )(*[_hbm(a) for a in (*srcs, *lands)], *after)
    return res[0], res[1], list(res[2:2 + n]), list(res[2 + n:2 + n + m]), res[-1]


def _split_wait(name, send_sems, recv_sems, srcs, lands, after, make_copies):
    n, m = len(srcs), len(lands)

    def body(*refs):
        for cp in make_copies(refs[:n], refs[n:n + m], refs[n + m], refs[n + m + 1]):
            cp.wait_send()
            cp.wait_recv()

    res = pl.pallas_call(
        body, name=name, out_shape=tuple(pltpu.HBM(a.shape, a.dtype) for a in (*srcs, *lands)),
        in_specs=[HBM_SPEC] * (n + m) + [SEM_SPEC, SEM_SPEC] + [ANY_SPEC] * len(after), out_specs=tuple([HBM_SPEC] * (n + m)),
        input_output_aliases={i: i for i in range(n + m)},
        compiler_params=pltpu.CompilerParams(has_side_effects=EFFECT),
    )(*srcs, *lands, send_sems, recv_sems, *after)
    return list(res[:n]), list(res[n:])


def _place_own(shards, dev):
    n = len(shards)

    def kern(dev_ref, *refs):
        for x_ref, o_ref in zip(refs[:n], refs[n:]):
            o_ref[...] = x_ref[...].astype(o_ref.dtype)

    blocks = [(None, *a.shape[1:]) for a, _, _ in shards]
    nbytes = sum(_nbytes(b, a.dtype) + _nbytes(b, dt) for b, (a, _, dt) in zip(blocks, shards))
    return pl.pallas_call(
        kern, name="weights_place_own", out_shape=[pltpu.HBM((N_DEV, *a.shape[1:]), dt) for a, _, dt in shards],
        grid_spec=pltpu.PrefetchScalarGridSpec(
            num_scalar_prefetch=1, grid=(1,),
            in_specs=[pl.BlockSpec(b, functools.partial(lambda i, dev, l: (l, 0, 0), l=l)) for b, (_, l, _) in zip(blocks, shards)],
            out_specs=[pl.BlockSpec(b, lambda i, dev: (dev[0], 0, 0)) for b in blocks]),
        compiler_params=pltpu.CompilerParams(dimension_semantics=("arbitrary",), vmem_limit_bytes=_vmem(nbytes)),
    )(dev, *[_hbm(a) for a, _, _ in shards])


def _ag_first_copies(src_refs, out_refs, send_sems, recv_sems):
    x, y, c = _me()
    targets = [(x, y, 1 - c), (1 - x, y, c), (x, 1 - y, c), (1 - x, 1 - y, c)]
    return [pltpu.make_async_remote_copy(
        src_ref=out_refs[op].at[4 * x + 2 * y + c], dst_ref=out_refs[op].at[4 * x + 2 * y + c], send_sem=send_sems.at[4 * op + k],
        recv_sem=recv_sems.at[4 * op + k], device_id=to, device_id_type=MESH)
        for op in range(len(out_refs)) for k, to in enumerate(targets)]


def _ag_second_copies(src_refs, out_refs, send_sems, recv_sems):
    x, y, c = _me()
    chips = [(1 - x, y), (x, 1 - y), (1 - x, 1 - y)]
    return [pltpu.make_async_remote_copy(
        src_ref=out_refs[op].at[4 * cx + 2 * cy + c], dst_ref=out_refs[op].at[4 * cx + 2 * cy + c],
        send_sem=send_sems.at[3 * op + j], recv_sem=recv_sems.at[3 * op + j], device_id=(x, y, 1 - c), device_id_type=MESH)
        for op in range(len(out_refs)) for j, (cx, cy) in enumerate(chips)]


def _rs_sibling_copies(g_refs, out_refs, send_sems, recv_sems):
    x, y, c = _me()
    return [pltpu.make_async_remote_copy(
        src_ref=g_refs[op].at[k, 1 - c], dst_ref=out_refs[op].at[k], send_sem=send_sems.at[4 * op + k],
        recv_sem=recv_sems.at[4 * op + k], device_id=(x, y, 1 - c), device_id_type=MESH)
        for op in range(len(g_refs)) for k in range(4)]


def _rs_direct_copies(g_refs, land_refs, send_sems, recv_sems):
    x, y, c = _me()
    n = len(g_refs) // 2
    chips = [(1 - x, y), (x, 1 - y), (1 - x, 1 - y)]
    copies = []
    for op in range(n):
        g32, g16, from_sib, from_others = g_refs[op], g_refs[n + op], land_refs[op], land_refs[n + op]
        copies.append(pltpu.make_async_remote_copy(
            src_ref=g32.at[2 * x + y, 1 - c], dst_ref=from_sib, send_sem=send_sems.at[7 * op], recv_sem=recv_sems.at[7 * op],
            device_id=(x, y, 1 - c), device_id_type=MESH))
        for j, (cx, cy) in enumerate(chips):
            for s, cc in enumerate((c, 1 - c)):
                copies.append(pltpu.make_async_remote_copy(
                    src_ref=g16.at[2 * cx + cy, cc], dst_ref=from_others.at[2 * j + s], send_sem=send_sems.at[7 * op + 1 + 2 * j + s],
                    recv_sem=recv_sems.at[7 * op + 1 + 2 * j + s], device_id=(cx, cy, cc), device_id_type=MESH))
    return copies


def _rs_chip_copies(p_refs, out_refs, send_sems, recv_sems):
    x, y, c = _me()
    chips = [(1 - x, y), (x, 1 - y), (1 - x, 1 - y)]
    return [pltpu.make_async_remote_copy(
        src_ref=p_refs[op].at[2 * cx + cy], dst_ref=out_refs[op].at[j], send_sem=send_sems.at[3 * op + j],
        recv_sem=recv_sems.at[3 * op + j], device_id=(cx, cy, c), device_id_type=MESH)
        for op in range(len(p_refs)) for j, (cx, cy) in enumerate(chips)]


def _all_gather(placed):
    n = len(placed)

    def kern(*refs):
        in_refs, out_refs, (send_sems, recv_sems) = refs[:n], refs[n:2 * n], refs[2 * n:]
        x, y, c = _me()
        me, sibling = (x, y, c), (x, y, 1 - c)
        chips = [(1 - x, y), (x, 1 - y), (1 - x, 1 - y)]

        def copy(op, k, block, to, own=False):
            idx = 4 * block[0] + 2 * block[1] + block[2]
            return pltpu.make_async_remote_copy(
                src_ref=(in_refs if own else out_refs)[op].at[idx], dst_ref=out_refs[op].at[idx], send_sem=send_sems.at[7 * op + k],
                recv_sem=recv_sems.at[7 * op + k], device_id=to, device_id_type=MESH)

        first = []
        for op in range(n):
            first.append(copy(op, 0, me, sibling, own=True))
            first += [copy(op, 1 + j, me, (*chip, c), own=True) for j, chip in enumerate(chips)]
        for cp in first:
            cp.start()
        passed = []
        for j, chip in enumerate(chips):
            for op in range(n):
                copy(op, 1 + j, (*chip, c), me).wait_recv()
                passed.append(copy(op, 4 + j, (*chip, c), sibling))
                passed[-1].start()
        for op in range(n):
            copy(op, 0, sibling, me).wait_recv()
            for j, chip in enumerate(chips):
                copy(op, 4 + j, (*chip, 1 - c), me).wait_recv()
        for cp in first + passed:
            cp.wait_send()

    return pl.pallas_call(
        kern, name="weights_all_gather", out_shape=[pltpu.HBM(g.shape, g.dtype) for g in placed],
        in_specs=[ANY_SPEC] * n, out_specs=[ANY_SPEC] * n, input_output_aliases={i: i for i in range(n)},
        scratch_shapes=[pltpu.SemaphoreType.DMA((7 * n,)), pltpu.SemaphoreType.DMA((7 * n,))],
    )(*[_hbm(a) for a in placed])


def _row_tile(r, w, n_blocks):
    tr = r
    while tr > 8 and 2 * n_blocks * tr * w * 4 > 24 * 2**20:
        tr //= 2
    return tr


def _chip_sum(name, g, from_sibling, core):
    _, _, R, W = g.shape
    tr = _row_tile(R, W, 3)

    def kern(core_ref, g_ref, s_ref, o_ref):
        o_ref[...] = (g_ref[...] + s_ref[...]).astype(BF16)

    return pl.pallas_call(
        kern, name=name, out_shape=pltpu.HBM((4, R, W), BF16),
        grid_spec=pltpu.PrefetchScalarGridSpec(
            num_scalar_prefetch=1, grid=(4, R // tr),
            in_specs=[pl.BlockSpec((None, None, tr, W), lambda k, i, core: (k, core[0], i, 0)),
                      pl.BlockSpec((None, tr, W), lambda k, i, core: (k, i, 0))],
            out_specs=pl.BlockSpec((None, tr, W), lambda k, i, core: (k, i, 0))),
        compiler_params=pltpu.CompilerParams(dimension_semantics=("parallel", "parallel"), vmem_limit_bytes=_vmem(3 * tr * W * 4)),
    )(core, _hbm(g), _hbm(from_sibling))


def _adamw(w, g, m, v):
    m = ADAM_B1 * m + (1.0 - ADAM_B1) * g
    v = ADAM_B2 * v + (1.0 - ADAM_B2) * (g * g)
    m_hat = m / (1.0 - ADAM_B1 ** ADAM_STEP)
    v_hat = v / (1.0 - ADAM_B2 ** ADAM_STEP)
    return -ADAM_LR * (m_hat / (jnp.sqrt(v_hat) + ADAM_EPS) + ADAM_WD * w), m, v


def _finish_sharded(name, layers, w, m, v, where, deps=()):
    nl, R, W = w.shape
    n_other = layers[0][2].shape[0]
    tr = _row_tile(R, W, (8 + n_other) * nl)
    deps = _deps(deps)

    def kern(where_ref, *refs):
        w_ref, m_ref, v_ref = refs[3 * nl:3 * nl + 3]
        go_ref, d_ref, mo_ref, vo_ref = refs[3 * nl + 3 + len(deps):]
        for l in range(nl):
            g_ref, s_ref, c_ref = refs[3 * l:3 * l + 3]
            grad = g_ref[...] + s_ref[...]
            for j in range(n_other):
                grad = grad + c_ref[j].astype(F32)
            go_ref[l] = grad
            d_ref[l], mo_ref[l], vo_ref[l] = _adamw(w_ref[l], grad, m_ref[l], v_ref[l])

    row = pl.BlockSpec((nl, tr, W), lambda i, wh: (0, i, 0))
    in_specs, args = [], []
    for g, s, c in layers:
        sib = (pl.BlockSpec((None, tr, W), lambda i, wh: (wh[0], i, 0)) if s.ndim == 3 else pl.BlockSpec((tr, W), lambda i, wh: (i, 0)))
        in_specs += [pl.BlockSpec((None, None, tr, W), lambda i, wh: (wh[0], wh[1], i, 0)), sib,
                     pl.BlockSpec((n_other, tr, W), lambda i, wh: (0, i, 0))]
        args += [g, s, c]
    return pl.pallas_call(
        kern, name=name, out_shape=[pltpu.HBM((nl, R, W), F32)] * 4,
        grid_spec=pltpu.PrefetchScalarGridSpec(num_scalar_prefetch=1, grid=(R // tr,),
                                               in_specs=in_specs + [row, row, row] + [ANY_SPEC] * len(deps),
                                               out_specs=[row, row, row, row]),
        compiler_params=pltpu.CompilerParams(dimension_semantics=("parallel",),
                                             vmem_limit_bytes=_vmem(nl * (8 + n_other) * tr * W * 4)),
    )(where, *[_hbm(a) for a in (*args, w, m, v)], *deps)


SMALL_PLACE = (("mla_gq", 0, 0, 1, 256), ("mla_gkv", 0, 256, 1, 256), ("sgu_ln_g", 0, 512, 1, 512), ("sgu_ln_b", 1, 0, 1, 512),
               ("hg_lb", 2, 0, 2, 1024), ("ln1_g", 4, 0, 2, 1024), ("ln1_b", 6, 0, 2, 1024), ("sgu_b", 8, 0, 4, 128),
               ("ln2_g", 12, 0, 2, 1024), ("ln2_b", 14, 0, 2, 1024), ("hg_gnorm", 16, 0, 1, 1024))
SMALL_BUF_ROWS = 24
LOSS_ROW = 17


def _small_pack(gs, dev):
    pieces = [(gs["mla_gq"], 0, 0), (gs["mla_gkv"], 0, 256), (gs["sgu_ln_g"], 0, 512), (gs["sgu_ln_b"], 1, 0), (gs["hg_lb"], 2, 0),
              (gs["ln1_g0"], 4, 0), (gs["ln1_g1"], 5, 0), (gs["ln1_b0"], 6, 0), (gs["ln1_b1"], 7, 0), (gs["sgu_b"], 8, 0),
              (gs["ln2_g0"], 12, 0), (gs["ln2_g1"], 13, 0), (gs["ln2_b0"], 14, 0), (gs["ln2_b1"], 15, 0), (gs["hg_gnorm"], 16, 0),
              (gs["sq_err"], LOSS_ROW, 0)]
    n_p = len(pieces)

    def kern(dev_ref, *refs):
        a_ref, b_ref = refs[n_p + 1], refs[n_p + 2]
        a_ref[...] = jnp.zeros(a_ref.shape, F32)
        for ref, (_, r, l0) in zip(refs[:n_p], pieces):
            a_ref[r:r + ref.shape[0], l0:l0 + ref.shape[1]] = ref[...]
        b_ref[...] = refs[n_p][...]

    whole = lambda a: pl.BlockSpec(a.shape, functools.partial(lambda i, dev, nd: (0,) * nd, nd=a.ndim))
    return pl.pallas_call(
        kern, name="small_grads_pack",
        out_shape=[pltpu.HBM((N_DEV, SMALL_BUF_ROWS, D_MODEL), F32), pltpu.HBM((N_DEV, SGU_G, 128, 128), F32)],
        grid_spec=pltpu.PrefetchScalarGridSpec(
            num_scalar_prefetch=1, grid=(1,), in_specs=[whole(p[0]) for p in pieces] + [whole(gs["sgu_w"])],
            out_specs=[pl.BlockSpec((None, SMALL_BUF_ROWS, D_MODEL), lambda i, dev: (dev[0], 0, 0)),
                       pl.BlockSpec((None, SGU_G, 128, 128), lambda i, dev: (dev[0], 0, 0, 0))]),
    )(dev, *[p[0] for p in pieces], gs["sgu_w"])


def _small_copies(src_refs, land_refs, send_sems, recv_sems):
    px, py, pc = _me()
    me = 4 * px + 2 * py + pc
    return [pltpu.make_async_remote_copy(
        src_ref=land_refs[k].at[me], dst_ref=land_refs[k].at[me], send_sem=send_sems.at[2 * (r - 1) + k],
        recv_sem=recv_sems.at[2 * (r - 1) + k], device_id=(px ^ (r >> 2), py ^ ((r >> 1) & 1), pc ^ (r & 1)), device_id_type=MESH)
        for r in range(1, N_DEV) for k in range(2)]


def _small_adamw(slots_a, slots_b, given):
    names = [p[0] for p in SMALL_PLACE] + ["sgu_w"]
    n_names = len(names)
    wmv = [given[pre + name] for name in names for pre in ("", "m_", "v_")]
    vmem = pl.BlockSpec(memory_space=pltpu.VMEM)

    def kern(*refs):
        sum_a, sum_b = refs[0][0], refs[1][0]
        for d in range(1, N_DEV):
            sum_a, sum_b = sum_a + refs[0][d], sum_b + refs[1][d]
        wmv_refs, out_refs = refs[2:2 + 3 * n_names], refs[2 + 3 * n_names:]
        px, py, pc = _me()
        me = 4 * px + 2 * py + pc

        def own_block(full):
            acc = full[:, 0:128]
            for b in range(1, N_DEV):
                acc = jnp.where(me == b, full[:, b * 128:(b + 1) * 128], acc)
            return acc

        for idx, name in enumerate(names):
            w_ref, m_ref, v_ref = wmv_refs[3 * idx:3 * idx + 3]
            if name == "sgu_w":
                grad = sum_b[None]
            else:
                _, r, l0, nr, nl = SMALL_PLACE[idx]
                grad = sum_a[r:r + nr, l0:l0 + nl]
                if name == "hg_gnorm":
                    grad = own_block(grad)
                if name == "sgu_b":
                    grad = grad[None]
            res = (grad, *_adamw(w_ref[...], grad, m_ref[...], v_ref[...]))
            for o_ref, val in zip(out_refs[4 * idx:4 * idx + 4], res):
                o_ref[...] = val
        out_refs[4 * n_names][...] = (0.5 / D_MODEL) * jnp.sum(sum_a[LOSS_ROW:LOSS_ROW + 1, :], axis=1, keepdims=True)

    out_shape = [jax.ShapeDtypeStruct(given[name].shape, F32) for name in names for _ in range(4)]
    out_shape.append(jax.ShapeDtypeStruct((1, 1), F32))
    res = pl.pallas_call(
        kern, name="small_adamw", out_shape=out_shape, in_specs=[vmem] * (2 + len(wmv)), out_specs=[vmem] * len(out_shape),
    )(slots_a, slots_b, *wmv)
    out = {name: res[4 * idx:4 * idx + 4] for idx, name in enumerate(names)}
    out["loss"] = res[-1].reshape(())
    return out


class _Exchange:
    def __init__(self, given):
        self.given = given
        px, py, pc = _me()
        self.core = pc.reshape(1).astype(jnp.int32)
        self.dev = (4 * px + 2 * py + pc).reshape(1).astype(jnp.int32)
        self.where = jnp.stack([2 * px + py, pc]).astype(jnp.int32)
        self.state, self.layers = {}, {}

    def start_weights(self, lands, after):
        self.weights = _split_start("weights_first_start", [], lands, 4 * len(lands), _ag_first_copies, after=after)
        self.first_token = self.weights[4]

    def weights_forward(self, after):
        send_sems, recv_sems, shards, lands, _ = self.weights
        _, lands = _split_wait("weights_first_wait", send_sems, recv_sems, shards, lands, after, _ag_first_copies)
        self.weights = _split_start("weights_second_start", [], lands, 3 * len(lands), _ag_second_copies)
        return self.weights[4]

    def weights_ready(self, after):
        send_sems, recv_sems, shards, lands, _ = self.weights
        _, got = _split_wait("weights_second_wait", send_sems, recv_sems, shards, lands, after, _ag_second_copies)
        return dict(w_in_o=got[0], w_out_o=got[1], w_ff1=[got[2], got[3]], w_ff2=[got[4], got[5]])

    def small_start(self, gs):
        self.small = _split_start("small_grads_start", [], _small_pack(gs, self.dev), 14, _small_copies)
        return self.small[4]

    def small_finish(self, after):
        send_sems, recv_sems, _, lands, _ = self.small
        _, lands = _split_wait("small_grads_wait", send_sems, recv_sems, [], lands, after, _small_copies)
        return _small_adamw(lands[0], lands[1], self.given)

    def direct_start(self, tag, grads):
        f32 = [g[0].reshape(4, 2, *g[0].shape[1:]) for g in grads]
        bf16 = [g[1].reshape(4, 2, *g[1].shape[1:]) for g in grads]
        lands = [lax.empty(b.shape[2:], F32) for b in f32] + [lax.empty((6, *b.shape[2:]), BF16) for b in f32]
        self.state[tag] = _split_start(f"grads_{tag}_start", f32 + bf16, lands, 7 * len(grads), _rs_direct_copies)
        return self.state[tag][4]

    def direct_end(self, tag, after):
        send_sems, recv_sems, srcs, lands, _ = self.state[tag]
        srcs, lands = _split_wait(f"grads_{tag}_wait", send_sems, recv_sems, srcs, lands, after, _rs_direct_copies)
        n = len(lands) // 2
        self.layers[tag] = list(zip(srcs[:n], lands[:n], lands[n:]))

    def grads_start(self, tag, grads):
        blocks = [g.reshape(4, 2, *g.shape[1:]) for g in grads]
        lands = [lax.empty((4, *b.shape[2:]), F32) for b in blocks]
        self.state[tag] = _split_start(f"grads_{tag}_sibling_start", blocks, lands, 4 * len(blocks), _rs_sibling_copies)
        return self.state[tag][4]

    def grads_middle(self, tag, after):
        send_sems, recv_sems, blocks, lands, _ = self.state[tag]
        blocks, from_sibling = _split_wait(f"grads_{tag}_sibling_wait", send_sems, recv_sems, blocks, lands, [after], _rs_sibling_copies)
        sums = [_chip_sum(f"grads_{tag}_chip_sum_{k}", b, s, self.core) for k, (b, s) in enumerate(zip(blocks, from_sibling))]
        lands = [lax.empty((3, *p.shape[1:]), BF16) for p in sums]
        self.state[tag] = (blocks, from_sibling, _split_start(f"grads_{tag}_chips_start", sums, lands, 3 * len(sums), _rs_chip_copies))
        return self.state[tag][2][4]

    def grads_end(self, tag, after):
        blocks, from_sibling, (send_sems, recv_sems, sums, lands, _) = self.state[tag]
        after = list(after) if isinstance(after, (list, tuple)) else [after]
        _, from_chips = _split_wait(f"grads_{tag}_chips_wait", send_sems, recv_sems, sums, lands, after, _rs_chip_copies)
        self.layers[tag] = list(zip(blocks, from_sibling, from_chips))


def kernel(x, positions, w_in_e, mla_gq, mla_gkv, w_qb, w_kvb, sgu_ln_g, sgu_ln_b, sgu_w, sgu_b, w_out_e, w_in_o, hg_lb, hg_gnorm, w_out_o, ln1_g, ln1_b, w_ff1, w_ff2, ln2_g, ln2_b, loss_target, m_w_in_e, m_mla_gq, m_mla_gkv, m_w_qb, m_w_kvb, m_sgu_ln_g, m_sgu_ln_b, m_sgu_w, m_sgu_b, m_w_out_e, m_w_in_o, m_hg_lb, m_hg_gnorm, m_w_out_o, m_ln1_g, m_ln1_b, m_w_ff1, m_w_ff2, m_ln2_g, m_ln2_b, v_w_in_e, v_mla_gq, v_mla_gkv, v_w_qb, v_w_kvb, v_sgu_ln_g, v_sgu_ln_b, v_sgu_w, v_sgu_b, v_w_out_e, v_w_in_o, v_hg_lb, v_hg_gnorm, v_w_out_o, v_ln1_g, v_ln1_b, v_w_ff1, v_w_ff2, v_ln2_g, v_ln2_b):
    given = dict(locals())
    transposed = ("w_in_e", "w_qb")
    for n in transposed:
        for pre in ("", "m_", "v_"):
            given[pre + n] = jnp.swapaxes(given[pre + n], 1, 2)
    ex = _Exchange(given)

    names = ["w_in_e", "w_qb", "w_kvb", "w_out_e"]
    placed = _place_own([(given[n], 0, BF16) for n in names] + [(hg_gnorm.reshape(1, 1, D_MODEL // N_DEV), 0, F32)]
                        + [(w_in_o, 0, BF16), (w_out_o, 0, BF16), (w_ff1, 0, BF16), (w_ff1, 1, BF16), (w_ff2, 0, BF16), (w_ff2, 1, BF16)],
                        ex.dev)
    got = _all_gather(placed[:5])
    ex.start_weights(placed[5:], after=[got[0]])
    gw = dict(zip(names, got[:4]))
    small_names = ["mla_gq", "mla_gkv", "sgu_ln_g", "sgu_ln_b", "sgu_w", "sgu_b", "hg_lb", "ln1_g", "ln1_b", "ln2_g", "ln2_b"]
    sp = {n: given[n] for n in small_names}
    sp["hg_gnorm"] = got[4].reshape(1, D_MODEL)

    _, dx, grads, gs = _local_step(x[0], positions[0], loss_target[0], gw, sp, ex)

    def finish(n, layers, deps=()):
        return _finish_sharded(f"finish_{n}", layers, given[n], given["m_" + n], given["v_" + n], ex.where, deps=deps)

    ex.direct_end("l1", after=[dx])
    ex.direct_end("l0m", after=[dx])
    l1, l0m = ex.layers["l1"], ex.layers["l0m"]
    results = {}
    token = ex.grads_start("l0s", [grads[n] for n in names])
    results["w_ff1"] = finish("w_ff1", [l0m[0], l1[0]], deps=[token])
    token = ex.grads_middle("l0s", after=results["w_ff1"][0])
    results["w_ff2"] = finish("w_ff2", [l0m[1], l1[1]], deps=[token])
    results["w_in_o"] = finish("w_in_o", [l1[2]], deps=[token])
    results["w_out_o"] = finish("w_out_o", [l1[3]], deps=[token])
    results.update(ex.small_finish(after=[results["w_in_o"][0]]))
    ex.grads_end("l0s", after=[results[n][0] for n in ("mla_gq", "w_ff2", "w_in_o", "w_out_o")])
    for n, layer in zip(names, ex.layers["l0s"]):
        results[n] = finish(n, [layer])
    for n in transposed:
        results[n] = [jnp.swapaxes(a, 1, 2) for a in results[n]]

    order = ["w_in_e", "mla_gq", "mla_gkv", "w_qb", "w_kvb", "sgu_ln_g", "sgu_ln_b", "sgu_w", "sgu_b", "w_out_e", "w_in_o",
             "hg_lb", "hg_gnorm", "w_out_o", "ln1_g", "ln1_b", "w_ff1", "w_ff2", "ln2_g", "ln2_b"]
    return (results["loss"], dx[None], *[results[name][kind] for kind in range(4) for name in order])
```

```python
import functools
import math

import jax
import jax.numpy as jnp
import numpy as np
from jax import lax
from jax.experimental import pallas as pl
from jax.experimental.pallas import tpu as pltpu

F32 = jnp.float32
BF16 = jnp.bfloat16
MESH = pl.DeviceIdType.MESH
HIGHEST = lax.Precision.HIGHEST

D_MODEL = 1024
D_FF = 4096
N_DEV = 8
HEADS = 8
HEAD_W = 128
MLA_NOPE = 64
MLA_ROPE = 32
MLA_V = 64
MLA_LORA = 256
MLA_SCALE = (MLA_NOPE + MLA_ROPE) ** -0.5
ROPE_BASE = 10000.0
SGU_DIM = 512
SGU_G = 4
SGU_CHUNK = 128
HG_CHUNK = 64
HG_CHUNKS_PER_STEP = 4
ALPHA = (2 * 2) ** 0.25
EPS = 1e-5
ADAM_LR, ADAM_B1, ADAM_B2, ADAM_EPS, ADAM_WD, ADAM_STEP = 0.001, 0.9, 0.999, 1e-08, 0.01, 10

VMEM_CAP_V7X = 56 * 2**20
VMEM_SLACK = 12 * 2**20
TM = 512
TN = 512


def _vmem(block_bytes):
    return int(min(VMEM_CAP_V7X, 2 * block_bytes + VMEM_SLACK))


def _hbm(a):
    return pltpu.with_memory_space_constraint(a, pltpu.HBM)


def _nbytes(shape, dtype):
    return int(np.prod([d for d in shape if d is not None])) * jnp.dtype(dtype).itemsize


def _sig(x):
    return 1.0 / (1.0 + jnp.exp(-x))


def _gelu(x):
    c = math.sqrt(2.0 / math.pi)
    t = jnp.tanh(c * (x + 0.044715 * x * x * x))
    return 0.5 * x * (1.0 + t), t


def _gelu_grad(x, t):
    c = math.sqrt(2.0 / math.pi)
    return 0.5 * (1.0 + t) + 0.5 * x * (1.0 - t * t) * c * (1.0 + 3 * 0.044715 * x * x)


def _dot(a, b, dims, precision=None):
    return lax.dot_general(a, b, (dims, ((), ())), preferred_element_type=F32, precision=precision)


NN = ((1,), (0,))
NT = ((1,), (1,))
TN_ = ((0,), (0,))


def _deps(deps):
    return [d for d in deps if d is not None]


def _tiled(name, grid, ins, outs, compute, direct=False, deps=()):
    n_in, deps = len(ins), _deps(deps)
    n_skip = n_in + len(deps)

    def kern(*refs):
        if direct:
            compute(refs[:n_in], refs[n_skip:])
            return
        for o_ref, r in zip(refs[n_skip:], compute(*refs[:n_in])):
            o_ref[...] = r.astype(o_ref.dtype).reshape(o_ref.shape)

    swap = lambda f: (lambda j, i: f(i, j))
    nbytes = sum(_nbytes(blk, a.dtype) for a, blk, _ in ins) + sum(_nbytes(blk, dt) + _nbytes(blk, F32) for _, dt, blk, _ in outs)
    res = pl.pallas_call(
        kern, name=name, grid=grid,
        in_specs=[pl.BlockSpec(blk, swap(f), pipeline_mode=pl.Buffered(1) if tuple(blk) == tuple(a.shape) else None)
                  for a, blk, f in ins] + [ANY_SPEC] * len(deps),
        out_specs=[pl.BlockSpec(blk, swap(f)) for _, _, blk, f in outs],
        out_shape=[pltpu.HBM(shape, dt) for shape, dt, _, _ in outs],
        compiler_params=pltpu.CompilerParams(dimension_semantics=("parallel", "parallel"), vmem_limit_bytes=_vmem(nbytes)),
    )(*[_hbm(a) for a, _, _ in ins], *deps)
    return res if len(res) > 1 else res[0]


def _rb(a, tm, w=None, cb=0):
    return (a, (tm, a.shape[1] if w is None else w), lambda i, j: (i, cb))


def _cw(b, tn):
    return (b, (b.shape[0], tn), lambda i, j: (0, j))


def _tl(a, tm):
    return (a, (a.shape[0], tm), lambda i, j: (0, i))


def _out(m, n, dtype, tm, tn):
    return ((m, n), dtype, (tm, tn), lambda i, j: (i, j))


def _out_dev(k, n, tm, dtype=F32):
    return ((N_DEV, k, n), dtype, (None, tm, n), lambda i, j: (j, i, 0))


def _twice(acc):
    return acc, acc


def _mmc(dims, n_pairs=1, epilogue=None):
    def compute(*refs):
        acc = None
        for k in range(n_pairs):
            d = _dot(refs[2 * k][...].astype(BF16), refs[2 * k + 1][...].astype(BF16), dims)
            acc = d if acc is None else acc + d
        ext = [r[...] for r in refs[2 * n_pairs:]]
        return epilogue(acc, *ext) if epilogue is not None else (acc,)

    return compute


def _res(w):
    return (w, w.shape, functools.partial(lambda i, j, nd: (0,) * nd, nd=w.ndim))


def _mmc_blocks(nblk, dims, rhs_block, epilogue=None):
    def compute(in_refs, out_refs):
        a = in_refs[0][...].astype(BF16)
        for d in range(nblk):
            acc = _dot(a, rhs_block(in_refs[1], d).astype(BF16), dims)
            n = acc.shape[1]
            ext = [r[:, d * n:(d + 1) * n] for r in in_refs[2:]]
            res = epilogue(acc, *ext) if epilogue is not None else (acc,)
            for o_ref, r in zip(out_refs, res):
                o_ref[:, d * n:(d + 1) * n] = r.astype(o_ref.dtype)

    return compute


def _rowwise(name, body, rows, consts, out_rows, out_accs=(), tr=512, deps=()):
    T = rows[0][0].shape[0]
    tr = min(tr, T)
    deps = _deps(deps)
    nr, ncn, no, nd = len(rows), len(consts), len(out_rows), len(deps)

    def kern(*refs):
        accs = refs[nr + ncn + nd + no:]
        if accs:
            @pl.when(pl.program_id(0) == 0)
            def _():
                for a in accs:
                    a[...] = jnp.zeros(a.shape, a.dtype)
        body(refs[:nr], refs[nr:nr + ncn], refs[nr + ncn + nd:nr + ncn + nd + no], accs)

    in_specs = [pl.BlockSpec((tr, w), functools.partial(lambda i, cb: (i, cb), cb=cb)) for _, w, cb in rows]
    in_specs += [pl.BlockSpec(c.shape, functools.partial(lambda i, nd: (0,) * nd, nd=c.ndim), pipeline_mode=pl.Buffered(1))
                 for c in consts]
    in_specs += [ANY_SPEC] * nd
    out_specs = [pl.BlockSpec((tr, w), lambda i: (i, 0)) for w, _ in out_rows]
    out_specs += [pl.BlockSpec(s, functools.partial(lambda i, nd: (0,) * nd, nd=len(s))) for s, _ in out_accs]
    out_shape = [pltpu.HBM((T, w), dt) for w, dt in out_rows]
    out_shape += [pltpu.HBM(s, dt) for s, dt in out_accs]
    nbytes = sum(_nbytes((tr, w), a.dtype) for a, w, _ in rows) + sum(_nbytes(c.shape, c.dtype) for c in consts)
    nbytes += sum(_nbytes((tr, w), dt) for w, dt in out_rows) + sum(_nbytes(s, dt) for s, dt in out_accs)
    res = pl.pallas_call(
        kern, name=name, grid=(T // tr,), in_specs=in_specs, out_specs=out_specs, out_shape=out_shape,
        compiler_params=pltpu.CompilerParams(dimension_semantics=("arbitrary",), vmem_limit_bytes=_vmem(nbytes)),
    )(*[_hbm(a) for a, _, _ in rows], *[_hbm(c) for c in consts], *deps)
    return res if len(res) > 1 else res[0]


def _full(a):
    return (a, a.shape[1], 0)


def _ln_stats(y):
    mu = jnp.mean(y, axis=-1, keepdims=True)
    yc = y - mu
    r = lax.rsqrt(jnp.mean(yc * yc, axis=-1, keepdims=True) + EPS)
    return yc * r, r


def _row_halves(n):
    return [slice(0, n // 2), slice(n // 2, n)] if n >= 256 else [slice(0, n)]


def _ln_back(dh, xh, r, gain, dg_ref, db_ref):
    dg_ref[...] += jnp.sum(dh * xh, axis=0, keepdims=True)
    db_ref[...] += jnp.sum(dh, axis=0, keepdims=True)
    dx = dh * gain
    return r * (dx - jnp.mean(dx, axis=-1, keepdims=True) - xh * jnp.mean(dx * xh, axis=-1, keepdims=True))


def _proj_ln(name, acts, weights, h_in, g, b, layer, deps=()):
    n = len(acts)

    def body(rows, consts, outs, accs):
        acc = None
        for k in range(n):
            d = _dot(rows[k][...].astype(BF16), consts[k][...], NN)
            acc = d if acc is None else acc + d
        y = ALPHA * rows[n][...] + acc
        xh, _ = _ln_stats(y)
        h = xh * consts[n][layer:layer + 1, :] + consts[n + 1][layer:layer + 1, :]
        outs[0][...] = y
        outs[1][...] = h
        outs[2][...] = h.astype(BF16)

    return _rowwise(name, body, [_full(a) for a in acts] + [_full(h_in)], [*weights, g, b],
                    [(D_MODEL, F32), (D_MODEL, F32), (D_MODEL, BF16)], tr=TM, deps=deps)


def _proj_ln_loss(name, act, w2, h_in, g, b, layer, target):
    def body(rows, consts, outs, accs):
        y = ALPHA * rows[1][...] + _dot(rows[0][...], consts[0][...], NN)
        xh, r = _ln_stats(y)
        gain = consts[1][layer:layer + 1, :]
        err = xh * gain + consts[2][layer:layer + 1, :] - rows[2][...]
        accs[0][...] += jnp.sum(err * err, axis=0, keepdims=True)
        dy = _ln_back(err * (1.0 / D_MODEL), xh, r, gain, accs[1], accs[2])
        outs[0][...] = dy
        outs[1][...] = dy.astype(BF16)

    return _rowwise(name, body, [_full(act), _full(h_in), _full(target)], [w2, g, b], [(D_MODEL, F32), (D_MODEL, BF16)],
                    [((1, D_MODEL), F32)] * 3, tr=TM)


def _dh_ln_back(name, da, w, dy_next, y, g, layer, proj=(), deps=()):
    def body(rows, consts, outs, accs):
        n = consts[0].shape[2]
        for sl in _row_halves(rows[0].shape[0]):
            acc = ALPHA * rows[1][sl, :]
            for d in range(N_DEV):
                acc = acc + _dot(rows[0][sl, d * n:(d + 1) * n], consts[0][d], NT)
            xh, r = _ln_stats(rows[2][sl, :])
            dy = _ln_back(acc, xh, r, consts[1][layer:layer + 1, :], accs[0], accs[1])
            outs[0][sl, :] = dy
            dy_bf = dy.astype(BF16)
            outs[1][sl, :] = dy_bf
            off = 0
            for k, p in enumerate(proj):
                outs[2][sl, off:off + p.shape[0]] = _dot(dy_bf, consts[2 + k][...], NT).astype(BF16)
                off += p.shape[0]

    out_rows = [(D_MODEL, F32), (D_MODEL, BF16)] + ([(sum(p.shape[0] for p in proj), BF16)] if proj else [])
    return _rowwise(name, body, [_full(da), _full(dy_next), _full(y)], [w, g, *proj], out_rows,
                    [((1, D_MODEL), F32)] * 2, tr=TM, deps=deps)


def _relu2_epilogue(acc):
    a = jnp.maximum(acc, 0.0)
    return acc, a * a


def _mlp_up(tag, h_bf, w1):
    T = h_bf.shape[0]
    tm = min(TM, T)
    return _tiled(f"{tag}_ff1", (1, T // tm), [_rb(h_bf, tm), _res(w1)],
                  [_out(T, D_FF, BF16, tm, D_FF), _out(T, D_FF, BF16, tm, D_FF)],
                  _mmc_blocks(N_DEV, NN, lambda w, d: w[d], epilogue=_relu2_epilogue), direct=True)


def _mlp_bwd_w(tag, h_bf, a, act, dff_bf, w2, deps=()):
    T = h_bf.shape[0]
    tm = min(TM, T)
    da = _tiled(f"{tag}_dact", (1, T // tm), [_rb(dff_bf, tm), _res(w2), _rb(a, tm)], [_out(T, D_FF, BF16, tm, D_FF)],
                _mmc_blocks(N_DEV, NT, lambda w, d: w[d], epilogue=lambda acc, a_t: (acc * 2.0 * jnp.maximum(a_t.astype(F32), 0.0),)),
                direct=True, deps=deps)
    dw2 = _tiled(f"{tag}_dw2", (1, D_FF // TM), [_tl(act, TM), _res(dff_bf)],
                 [_out(D_FF, D_MODEL, F32, TM, D_MODEL), _out(D_FF, D_MODEL, BF16, TM, D_MODEL)], _mmc(TN_, epilogue=_twice))
    dw1 = _tiled(f"{tag}_dw1", (N_DEV, 1), [_res(h_bf), _cw(da, TN)],
                 [_out_dev(D_MODEL, TN, D_MODEL), _out_dev(D_MODEL, TN, D_MODEL, BF16)], _mmc(TN_, epilogue=_twice))
    return da, dw1, [a.reshape(N_DEV, D_FF // N_DEV, D_MODEL) for a in dw2]


def _rope_tables(positions_col, invf_lane):
    def body(rows, consts, outs, accs):
        ang = rows[0][...].astype(F32) * consts[0][...]
        c, s = jnp.cos(ang), jnp.sin(ang)
        lane = lax.broadcasted_iota(jnp.int32, ang.shape, 1)
        outs[0][...] = jnp.where(lane < 64, 1.0, jnp.where(lane < 96, c, 0.0))
        outs[1][...] = jnp.where((lane >= 64) & (lane < 80), -s, 0.0)
        outs[2][...] = jnp.where((lane >= 80) & (lane < 96), s, 0.0)

    return _rowwise("rope_tables", body, [_full(positions_col)], [invf_lane], [(HEAD_W, F32)] * 3)


def _rope(x, c, s1, s2):
    return x * c + pltpu.roll(x, 112, 1) * s1 + pltpu.roll(x, 16, 1) * s2


def _rope_t(dx, c, s1, s2):
    return dx * c + pltpu.roll(dx * s1, 16, 1) + pltpu.roll(dx * s2, 112, 1)


def _rms(c):
    r = lax.rsqrt(jnp.mean(c * c, axis=-1, keepdims=True) + EPS)
    return c * r, r


def _rope_heads(x, c, s1, s2, fn):
    return jnp.concatenate([fn(x[:, h * HEAD_W:(h + 1) * HEAD_W], c, s1, s2) for h in range(HEADS)], axis=1)


def _rope_key_rows(w_ref):
    return jnp.concatenate([jnp.zeros((64, D_MODEL), BF16), w_ref[512:544, :], jnp.zeros((32, D_MODEL), BF16)], axis=0)


def _mla_in(x, w, tabs, gq, gkv, deps=()):
    def body(rows, consts, outs, accs):
        xb = rows[0][...].astype(BF16)
        w_ref = consts[0]
        zc = _dot(xb, w_ref[0:512, :], NT)
        zr = _dot(xb, _rope_key_rows(w_ref), NT)
        outs[0][:, 0:512] = zc
        outs[0][:, 512:640] = zr
        outs[1][...] = _dot(xb, w_ref[544:1568, :], NT)
        outs[2][...] = (_rms(zc[:, 0:256])[0] * consts[1][...]).astype(BF16)
        outs[3][...] = (_rms(zc[:, 256:512])[0] * consts[2][...]).astype(BF16)
        outs[4][...] = _rope(zr, rows[1][...], rows[2][...], rows[3][...])

    return _rowwise("l0_in", body, [_full(x)] + [_full(t) for t in tabs], [w, gq, gkv],
                    [(640, F32), (1024, F32), (256, BF16), (256, BF16), (HEAD_W, F32)], deps=deps)


def _mla_qkv(cqn, ckvn, kr_rot, tabs, wq, wk, wv):
    def body(rows, consts, outs, accs):
        c, s1, s2 = rows[3][...], rows[4][...], rows[5][...]
        outs[0][...] = _rope_heads(_dot(rows[0][...], consts[0][...], NN), c, s1, s2, _rope).astype(BF16)
        outs[1][...] = (_dot(rows[1][...], consts[1][...], NN) + jnp.concatenate([rows[2][...]] * HEADS, axis=1)).astype(BF16)
        outs[2][...] = _dot(rows[1][...], consts[2][...], NN).astype(BF16)

    rows = [_full(cqn), _full(ckvn), _full(kr_rot)] + [_full(t) for t in tabs]
    return _rowwise("l0_qkv", body, rows, [wq, wk, wv], [(HEADS * HEAD_W, BF16)] * 3)


def _mla_back(zm, cqn, ckvn, tabs, gq, gkv, wq, wk, wv, dq, dk, dv):
    def body(rows, consts, outs, accs):
        c, s1, s2 = rows[4][...], rows[5][...], rows[6][...]
        dk_t, dv_bf = rows[8][...], rows[9][...].astype(BF16)
        dq_bf = _rope_heads(rows[7][...], c, s1, s2, _rope_t).astype(BF16)
        dk_bf = dk_t.astype(BF16)
        accs[0][...] += _dot(rows[2][...], dq_bf, TN_)
        accs[1][...] += _dot(rows[3][...], dk_bf, TN_)
        accs[2][...] += _dot(rows[3][...], dv_bf, TN_)
        dlat = [_dot(dq_bf, consts[2][...], NT), _dot(dk_bf, consts[3][...], NT) + _dot(dv_bf, consts[4][...], NT)]
        for k in range(2):
            ch, r = _rms(rows[k][...])
            accs[3 + k][...] += jnp.sum(dlat[k] * ch, axis=0, keepdims=True)
            dc = dlat[k] * consts[k][...]
            outs[0][:, 256 * k:256 * (k + 1)] = (r * (dc - ch * jnp.mean(dc * ch, axis=-1, keepdims=True))).astype(BF16)
        dks = dk_t[:, 0:HEAD_W]
        for h in range(1, HEADS):
            dks = dks + dk_t[:, h * HEAD_W:(h + 1) * HEAD_W]
        lane = lax.broadcasted_iota(jnp.int32, dks.shape, 1)
        dks = jnp.where((lane >= 64) & (lane < 96), dks, 0.0)
        outs[0][:, 512:640] = _rope_t(dks, c, s1, s2).astype(BF16)

    rows = [(zm, 256, 0), (zm, 256, 1), _full(cqn), _full(ckvn)] + [_full(t) for t in tabs] + [_full(dq), _full(dk), _full(dv)]
    wide = HEADS * HEAD_W
    return _rowwise("l0_mla_back", body, rows, [gq, gkv, wq, wk, wv], [(640, BF16)],
                    [((MLA_LORA, wide), F32)] * 3 + [((1, MLA_LORA), F32)] * 2, tr=256)


def _in_back(x, dzm, dzs, dy, w, deps=()):
    def body(rows, consts, outs, accs):
        w_ref = consts[0]
        dzm_t, dzs_t = rows[1][...], rows[2][...]
        outs[0][...] = (_dot(dzm_t[:, 0:512], w_ref[0:512, :], NN) + _dot(dzm_t[:, 512:640], _rope_key_rows(w_ref), NN)
                        + _dot(dzs_t, w_ref[544:1568, :], NN) + ALPHA * rows[3][...])
        xb = rows[0][...].astype(BF16)
        gm = _dot(dzm_t, xb, TN_)
        accs[0][0:512, :] += gm[0:512]
        accs[0][512:544, :] += gm[512 + 64:512 + 96]
        accs[0][544:1568, :] += _dot(dzs_t, xb, TN_)

    return _rowwise("l0_in_back", body, [_full(x), _full(dzm), _full(dzs), _full(dy)], [w], [(D_MODEL, F32)],
                    [((1568, D_MODEL), F32)], deps=deps)


def _out_weight_grads(o_att, b_out, dy_bf):
    def body(rows, consts, outs, accs):
        d = rows[2][...]
        accs[0][...] += _dot(rows[0][...].astype(BF16), d, TN_)
        accs[1][...] += _dot(rows[1][...], d, TN_)

    return _rowwise("l0_dw_out", body, [_full(o_att), _full(b_out), _full(dy_bf)], [], [],
                    [((HEADS * HEAD_W, D_MODEL), F32), ((SGU_DIM, D_MODEL), F32)])


def _attn_block(T):
    return min(1024, T)


def _attn_fwd(q, k, v):
    T = q.shape[0]
    BQ = _attn_block(T)
    nq = T // BQ

    def kern(q_ref, k_ref, v_ref, o_ref, lse_ref):
        def step(i, j, carry, masked):
            m, l, acc = carry
            qb = q_ref[pl.ds(pl.multiple_of(i * BQ, BQ), BQ), :]
            kb = k_ref[pl.ds(pl.multiple_of(j * BQ, BQ), BQ), :]
            vb = v_ref[pl.ds(pl.multiple_of(j * BQ, BQ), BQ), :]
            s = _dot(qb, kb, NT) * MLA_SCALE
            if masked:
                row = lax.broadcasted_iota(jnp.int32, s.shape, 0)
                col = lax.broadcasted_iota(jnp.int32, s.shape, 1)
                s = jnp.where(col <= row, s, -1e30)
            m_new = jnp.maximum(m, jnp.max(s, axis=-1, keepdims=True))
            p = jnp.exp(s - m_new)
            a = jnp.exp(m - m_new)
            l = a * l + jnp.sum(p, axis=-1, keepdims=True)
            acc = a * acc + _dot(p.astype(BF16), vb, NN)
            return m_new, l, acc

        def qloop(i, _):
            init = (jnp.full((BQ, 1), -1e30, F32), jnp.zeros((BQ, 1), F32), jnp.zeros((BQ, HEAD_W), F32))
            carry = lax.fori_loop(0, i, lambda j, c: step(i, j, c, False), init)
            m, l, acc = step(i, i, carry, True)
            rows = pl.ds(pl.multiple_of(i * BQ, BQ), BQ)
            o_ref[rows, :] = acc / l
            lse_ref[0, rows, :] = m + jnp.log(l)
            return 0

        lax.fori_loop(0, nq, qloop, 0)

    head = pl.BlockSpec((T, HEAD_W), lambda h: (0, h))
    nbytes = 3 * _nbytes((T, HEAD_W), BF16) + _nbytes((T, HEAD_W), F32) + _nbytes((T, 128), F32)
    return pl.pallas_call(
        kern, name="attn_fwd", grid=(HEADS,), in_specs=[head, head, head],
        out_specs=[head, pl.BlockSpec((1, T, 1), lambda h: (h, 0, 0))],
        out_shape=[pltpu.HBM((T, HEADS * HEAD_W), F32), pltpu.HBM((HEADS, T, 1), F32)],
        compiler_params=pltpu.CompilerParams(dimension_semantics=("parallel",), vmem_limit_bytes=_vmem(nbytes)),
    )(_hbm(q), _hbm(k), _hbm(v))


def _attn_bwd(q, k, v, o, lse, dcat, deps=()):
    T = q.shape[0]
    BQ = _attn_block(T)
    nq = T // BQ
    deps = _deps(deps)

    def kern(q_ref, k_ref, v_ref, o_ref, lse_ref, do_ref, *rest):
        dq_ref, dk_ref, dv_ref, dd_ref = rest[len(deps):]
        dq_ref[...] = jnp.zeros(dq_ref.shape, F32)

        def dloop(i, _):
            rows = pl.ds(pl.multiple_of(i * BQ, BQ), BQ)
            dd_ref[rows, :] = jnp.sum(do_ref[rows, :].astype(F32) * o_ref[rows, :], axis=-1, keepdims=True)
            return 0

        lax.fori_loop(0, nq, dloop, 0)

        def tile(q0, k0, n, carry, masked):
            dk_acc, dv_acc = carry
            rq = pl.ds(pl.multiple_of(q0, n), n)
            rk = pl.ds(pl.multiple_of(k0, n), n)
            qb, kb, vb, dob = q_ref[rq, :], k_ref[rk, :], v_ref[rk, :], do_ref[rq, :]
            s = _dot(qb, kb, NT) * MLA_SCALE
            p = jnp.exp(s - lse_ref[0, rq, :])
            if masked:
                row = lax.broadcasted_iota(jnp.int32, s.shape, 0)
                col = lax.broadcasted_iota(jnp.int32, s.shape, 1)
                p = jnp.where(col <= row, p, 0.0)
            dp = _dot(dob, vb, NT)
            ds = (p * (dp - dd_ref[rq, :]) * MLA_SCALE).astype(BF16)
            dv_acc = dv_acc + _dot(p.astype(BF16), dob, TN_)
            dk_acc = dk_acc + _dot(ds, qb, TN_)
            dq_ref[rq, :] += _dot(ds, kb, NN)
            return dk_acc, dv_acc

        def kloop(j, _):
            base, half = j * BQ, BQ // 2
            zero = (jnp.zeros((half, HEAD_W), F32), jnp.zeros((half, HEAD_W), F32))
            early = tile(base + half, base, half, tile(base, base, half, zero, True), False)
            late = tile(base + half, base + half, half, zero, True)
            carry = tuple(jnp.concatenate([a, b], axis=0) for a, b in zip(early, late))
            dk_acc, dv_acc = lax.fori_loop(j + 1, nq, lambda i, c: tile(i * BQ, base, BQ, c, False), carry)
            rk = pl.ds(pl.multiple_of(j * BQ, BQ), BQ)
            dk_ref[rk, :] = dk_acc
            dv_ref[rk, :] = dv_acc
            return 0

        lax.fori_loop(0, nq, kloop, 0)

    head = pl.BlockSpec((T, HEAD_W), lambda h: (0, h))
    nbytes = 4 * _nbytes((T, HEAD_W), BF16) + 5 * _nbytes((T, HEAD_W), F32) + 2 * _nbytes((T, 128), F32)
    return pl.pallas_call(
        kern, name="attn_bwd", grid=(HEADS,),
        in_specs=[head, head, head, head, pl.BlockSpec((1, T, 1), lambda h: (h, 0, 0)), head] + [ANY_SPEC] * len(deps),
        out_specs=[head, head, head],
        out_shape=[pltpu.HBM((T, HEADS * HEAD_W), F32)] * 3,
        scratch_shapes=[pltpu.VMEM((T, 1), F32)],
        compiler_params=pltpu.CompilerParams(dimension_semantics=("parallel",), vmem_limit_bytes=_vmem(nbytes)),
    )(*[_hbm(a) for a in (q, k, v, o, lse, dcat)], *deps)


def _sgu_common(u, v, ln_g, ln_b):
    ua, tu = _gelu(u)
    va, tv = _gelu(v)
    vh, r = _ln_stats(va)
    return ua, tu, tv, vh, r, vh * ln_g + ln_b


def _tril_mask(n):
    return lax.broadcasted_iota(jnp.int32, (n, n), 1) <= lax.broadcasted_iota(jnp.int32, (n, n), 0)


def _sgu_fwd(zs, ln_g, ln_b, w, bias_full):
    def body(rows, consts, outs, accs):
        ua, _, _, _, _, vn = _sgu_common(rows[0][...], rows[1][...], consts[0][...], consts[1][...])
        vn = vn.astype(BF16)
        tri = _tril_mask(SGU_CHUNK)
        for g in range(SGU_G):
            wg = jnp.where(tri, consts[2][0, g], 0.0).astype(BF16)
            cols = slice(g * 128, (g + 1) * 128)
            for c in range(ua.shape[0] // SGU_CHUNK):
                rws = slice(c * SGU_CHUNK, (c + 1) * SGU_CHUNK)
                mixed = _dot(wg, vn[rws, cols], NN) + consts[3][:, cols]
                outs[0][rws, cols] = (ua[rws, cols] * mixed).astype(BF16)

    return _rowwise("sgu_fwd", body, [(zs, 512, 0), (zs, 512, 1)], [ln_g, ln_b, w, bias_full], [(SGU_DIM, BF16)])


def _sgu_bwd(zs, dcat, ln_g, ln_b, w, bias_full):
    def body(rows, consts, outs, accs):
        u, v = rows[0][...], rows[1][...]
        ua, tu, tv, vh, r, vn = _sgu_common(u, v, consts[0][...], consts[1][...])
        dout = rows[2][...].astype(F32)
        vn_bf = vn.astype(BF16)
        tri = _tril_mask(SGU_CHUNK)
        dmixed = (dout * ua)
        dmixed_bf = dmixed.astype(BF16)
        ones = jnp.ones((8, SGU_CHUNK), F32)
        dvn_cols, mixed_cols = [], []
        for g in range(SGU_G):
            wg = jnp.where(tri, consts[2][0, g], 0.0).astype(BF16)
            cols = slice(g * 128, (g + 1) * 128)
            dvn_rows, mixed_rows = [], []
            dw = jnp.zeros((SGU_CHUNK, SGU_CHUNK), F32)
            dmix_sum = jnp.zeros((SGU_CHUNK, 128), F32)
            for c in range(u.shape[0] // SGU_CHUNK):
                rws = slice(c * SGU_CHUNK, (c + 1) * SGU_CHUNK)
                mixed_rows.append(_dot(wg, vn_bf[rws, cols], NN) + consts[3][:, cols])
                dvn_rows.append(_dot(wg, dmixed_bf[rws, cols], TN_))
                dw = dw + _dot(dmixed_bf[rws, cols], vn_bf[rws, cols], NT)
                dmix_sum = dmix_sum + dmixed[rws, cols]
            accs[0][g] += jnp.where(tri, dw, 0.0)
            accs[3][g:g + 1, :] += _dot(ones, dmix_sum, NT, precision=HIGHEST)[0:1, :]
            dvn_cols.append(jnp.concatenate(dvn_rows, axis=0))
            mixed_cols.append(jnp.concatenate(mixed_rows, axis=0))
        dvn = jnp.concatenate(dvn_cols, axis=1)
        mixed = jnp.concatenate(mixed_cols, axis=1)
        accs[1][...] += jnp.sum(dvn * vh, axis=0, keepdims=True)
        accs[2][...] += jnp.sum(dvn, axis=0, keepdims=True)
        dvh = dvn * consts[0][...]
        dva = r * (dvh - jnp.mean(dvh, axis=-1, keepdims=True) - vh * jnp.mean(dvh * vh, axis=-1, keepdims=True))
        outs[0][:, 0:512] = (dout * mixed * _gelu_grad(u, tu)).astype(BF16)
        outs[0][:, 512:1024] = (dva * _gelu_grad(v, tv)).astype(BF16)

    return _rowwise("sgu_bwd", body, [(zs, 512, 0), (zs, 512, 1), (dcat, 512, 2)], [ln_g, ln_b, w, bias_full], [(1024, BF16)],
                    [((SGU_G, 128, 128), F32), ((1, SGU_DIM), F32), ((1, SGU_DIM), F32), ((SGU_G, 128), F32)], tr=256)


def _lower_bound(hg_lb):
    a0, a1 = hg_lb[0:1, :], hg_lb[1:2, :]
    m = jnp.maximum(a0, a1)
    e0, e1 = jnp.exp(a0 - m), jnp.exp(a1 - m)
    s0, s1 = e0 / (e0 + e1), e1 / (e0 + e1)
    return (s0 + s1) - s0, s0, s1


def _prefix_rows(x, reverse=False):
    n = x.shape[0]
    row = lax.broadcasted_iota(jnp.int32, x.shape, 0)
    s = 1
    while s < n:
        if reverse:
            x = x + jnp.where(row < n - s, pltpu.roll(x, n - s, 0), 0.0)
        else:
            x = x + jnp.where(row >= s, pltpu.roll(x, s, 0), 0.0)
        s *= 2
    return x


def _hg_gates(qr, fr, lb):
    C = qr.shape[0]
    sq = _sig(qr)
    qf = qr * sq
    sf = _sig(fr)
    gate = lb + (1.0 - lb) * sf
    kk = 1.0 - gate
    tri = _tril_mask(C)
    b = _prefix_rows(jnp.log(gate))
    bref = b[C // 2 - 1:C // 2, :]
    bl = b[C - 1:C, :]
    e_b = jnp.exp(b)
    e_q = jnp.exp(b - bref)
    e_k = jnp.exp(bref - b)
    e_lb = jnp.exp(bl - b)
    return dict(sq=sq, qf=qf, sf=sf, gate=gate, kk=kk, tri=tri, bl=bl, e_b=e_b, e_q=e_q, e_k=e_k, e_lb=e_lb)


def _hgrn_fwd(z1, hg_lb, gnorm):
    T = z1.shape[0]
    C = min(HG_CHUNK, T)
    nc = T // C
    ns = HG_CHUNKS_PER_STEP if nc % HG_CHUNKS_PER_STEP == 0 else 1
    R = ns * C

    def kern(q_ref, f_ref, i_ref, g_ref, lb_ref, gn_ref, o_ref, hg_ref, st_ref, s_scr):
        @pl.when(pl.program_id(0) == 0)
        def _():
            s_scr[...] = jnp.zeros(s_scr.shape, F32)

        lb_all, _, _ = _lower_bound(lb_ref[...])
        for sub in range(ns):
            rows = slice(sub * C, (sub + 1) * C)
            st_ref[sub] = s_scr[...]
            for h in range(HEADS):
                cols = slice(h * HEAD_W, (h + 1) * HEAD_W)
                t = _hg_gates(q_ref[rows, cols], f_ref[rows, cols], lb_all[:, cols])
                v_bf = i_ref[rows, cols].astype(BF16)
                st = s_scr[h]
                a = jnp.where(t["tri"], _dot((t["qf"] * t["e_q"]).astype(BF16), (t["kk"] * t["e_k"]).astype(BF16), NT), 0.0)
                o = _dot(a.astype(BF16), v_bf, NN) + _dot((t["qf"] * t["e_b"]).astype(BF16), st.astype(BF16), NT)
                s_scr[h] = st * jnp.exp(t["bl"]) + _dot(v_bf, (t["kk"] * t["e_lb"]).astype(BF16), TN_)
                o_ref[rows, cols] = o
                gr = g_ref[rows, cols]
                r = lax.rsqrt(jnp.mean(o * o, axis=-1, keepdims=True) + EPS)
                hg_ref[rows, cols] = (o * r * gn_ref[:, cols] * (gr * _sig(gr))).astype(BF16)

    seg = lambda k: pl.BlockSpec((R, D_MODEL), functools.partial(lambda n, k: (n, k), k=k))
    row = pl.BlockSpec((R, D_MODEL), lambda n: (n, 0))
    nbytes = 6 * _nbytes((R, D_MODEL), F32) + (2 + ns) * _nbytes((HEADS, 128, 128), F32)
    return pl.pallas_call(
        kern, name="hgrn_fwd", grid=(nc // ns,),
        in_specs=[seg(0), seg(1), seg(2), seg(3), pl.BlockSpec((2, D_MODEL), lambda n: (0, 0)),
                  pl.BlockSpec((1, D_MODEL), lambda n: (0, 0))],
        out_specs=[row, row, pl.BlockSpec((ns, HEADS, 128, 128), lambda n: (n, 0, 0, 0))],
        out_shape=[pltpu.HBM((T, D_MODEL), F32), pltpu.HBM((T, D_MODEL), BF16),
                   pltpu.HBM((nc, HEADS, 128, 128), F32)],
        scratch_shapes=[pltpu.VMEM((HEADS, 128, 128), F32)],
        compiler_params=pltpu.CompilerParams(dimension_semantics=("arbitrary",), vmem_limit_bytes=_vmem(nbytes)),
    )(*[_hbm(a) for a in (z1, z1, z1, z1, hg_lb, gnorm)])


def _hgrn_bwd(z1, o_pre, dhg, states, hg_lb, gnorm):
    T = z1.shape[0]
    C = min(HG_CHUNK, T)
    nc = T // C
    ns = HG_CHUNKS_PER_STEP if nc % HG_CHUNKS_PER_STEP == 0 else 1
    R, steps = ns * C, nc // ns

    def kern(q_ref, f_ref, i_ref, g_ref, o_ref, dhg_ref, st_ref, lb_ref, gn_ref, dz_ref, dlb_ref, dgn_ref, ds_scr, dlb_scr):
        n = pl.program_id(0)

        @pl.when(n == 0)
        def _():
            ds_scr[...] = jnp.zeros(ds_scr.shape, F32)
            dlb_scr[...] = jnp.zeros(dlb_scr.shape, F32)
            dgn_ref[...] = jnp.zeros(dgn_ref.shape, F32)

        lb_all, s0, s1 = _lower_bound(lb_ref[...])
        for sub in reversed(range(ns)):
            rows = slice(sub * C, (sub + 1) * C)
            for h in range(HEADS):
                cols = slice(h * HEAD_W, (h + 1) * HEAD_W)
                lb = lb_all[:, cols]
                qr, fr = q_ref[rows, cols], f_ref[rows, cols]
                t = _hg_gates(qr, fr, lb)
                tri = t["tri"]
                v_bf = i_ref[rows, cols].astype(BF16)
                st_bf = st_ref[sub, h].astype(BF16)
                dst = ds_scr[h]
                dst_bf = dst.astype(BF16)
                o = o_ref[rows, cols]
                gr = g_ref[rows, cols]
                sg = _sig(gr)
                sil = gr * sg
                gn = gn_ref[:, cols]
                r = lax.rsqrt(jnp.mean(o * o, axis=-1, keepdims=True) + EPS)
                on = o * r
                dh = dhg_ref[rows, cols].astype(F32)
                dgn_ref[:, cols] += jnp.sum(dh * on * sil, axis=0, keepdims=True)
                dg = dh * on * gn * (sg * (1.0 + gr * (1.0 - sg)))
                don = dh * gn * sil
                do_bf = (r * (don - on * jnp.mean(don * on, axis=-1, keepdims=True))).astype(BF16)
                qe = (t["qf"] * t["e_q"]).astype(BF16)
                ke = (t["kk"] * t["e_k"]).astype(BF16)
                qb = (t["qf"] * t["e_b"]).astype(BF16)
                kh_bf = (t["kk"] * t["e_lb"]).astype(BF16)
                a_bf = jnp.where(tri, _dot(qe, ke, NT), 0.0).astype(BF16)
                da_bf = jnp.where(tri, _dot(do_bf, v_bf, NT), 0.0).astype(BF16)
                dv = _dot(a_bf, do_bf, TN_) + _dot(kh_bf, dst_bf, NT)
                dqe = _dot(da_bf, ke, NN)
                dqb = _dot(do_bf, st_bf, NN)
                dke = _dot(da_bf, qe, TN_)
                dkh = _dot(v_bf, dst_bf, NN)
                dqf = dqe * t["e_q"] + dqb * t["e_b"]
                dkk = dke * t["e_k"] + dkh * t["e_lb"]
                kh_r = kh_bf.astype(F32)
                db = qe.astype(F32) * dqe - ke.astype(F32) * dke + qb.astype(F32) * dqb - kh_r * dkh
                e_bl = jnp.exp(t["bl"])
                dbl = jnp.sum(dkh * kh_r, axis=0, keepdims=True) + e_bl * jnp.sum(st_ref[sub, h] * dst, axis=0, keepdims=True)
                dlg = _prefix_rows(db, reverse=True) + dbl
                ds_scr[h] = dst * e_bl + _dot(do_bf, qb, TN_)
                dgate = dlg / t["gate"] - dkk
                sf = t["sf"]
                dlb_scr[:, cols] += jnp.sum(dgate * (1.0 - sf), axis=0, keepdims=True)
                df = dgate * (1.0 - lb) * sf * (1.0 - sf)
                dq = dqf * (t["sq"] * (1.0 + qr * (1.0 - t["sq"])))
                dz_ref[rows, cols] = dq.astype(BF16)
                dz_ref[rows, D_MODEL + h * HEAD_W:D_MODEL + (h + 1) * HEAD_W] = df.astype(BF16)
                dz_ref[rows, 2 * D_MODEL + h * HEAD_W:2 * D_MODEL + (h + 1) * HEAD_W] = dv.astype(BF16)
                dz_ref[rows, 3 * D_MODEL + h * HEAD_W:3 * D_MODEL + (h + 1) * HEAD_W] = dg.astype(BF16)

        @pl.when(n == steps - 1)
        def _():
            d = s0 * s1 * dlb_scr[...]
            dlb_ref[0:1, :] = -d
            dlb_ref[1:2, :] = d

    seg = lambda k: pl.BlockSpec((R, D_MODEL), functools.partial(lambda n, k: (steps - 1 - n, k), k=k))
    nbytes = 6 * _nbytes((R, D_MODEL), F32) + _nbytes((R, 4 * D_MODEL), BF16) + (2 + ns) * _nbytes((HEADS, 128, 128), F32)
    return pl.pallas_call(
        kern, name="hgrn_bwd", grid=(steps,),
        in_specs=[seg(0), seg(1), seg(2), seg(3), seg(0), seg(0),
                  pl.BlockSpec((ns, HEADS, 128, 128), lambda n: (steps - 1 - n, 0, 0, 0)),
                  pl.BlockSpec((2, D_MODEL), lambda n: (0, 0)), pl.BlockSpec((1, D_MODEL), lambda n: (0, 0))],
        out_specs=[pl.BlockSpec((R, 4 * D_MODEL), lambda n: (steps - 1 - n, 0)),
                   pl.BlockSpec((2, D_MODEL), lambda n: (0, 0)), pl.BlockSpec((1, D_MODEL), lambda n: (0, 0))],
        out_shape=[pltpu.HBM((T, 4 * D_MODEL), BF16), pltpu.HBM((2, D_MODEL), F32),
                   pltpu.HBM((1, D_MODEL), F32)],
        scratch_shapes=[pltpu.VMEM((HEADS, 128, 128), F32), pltpu.VMEM((1, D_MODEL), F32)],
        compiler_params=pltpu.CompilerParams(dimension_semantics=("arbitrary",), vmem_limit_bytes=_vmem(nbytes)),
    )(*[_hbm(a) for a in (z1, z1, z1, z1, o_pre, dhg, states, hg_lb, gnorm)])


def _prep_weights(gw):
    w_in = gw["w_in_e"].reshape(1568, D_MODEL)
    w_qb = gw["w_qb"].transpose(1, 0, 2).reshape(MLA_LORA, HEADS, 96)
    wq = jnp.pad(w_qb, ((0, 0), (0, 0), (0, 32))).reshape(MLA_LORA, HEADS * HEAD_W)
    kvb = gw["w_kvb"].transpose(1, 0, 2).reshape(MLA_LORA, HEADS, 128)
    wk = jnp.pad(kvb[:, :, :64], ((0, 0), (0, 0), (0, 64))).reshape(MLA_LORA, HEADS * HEAD_W)
    wv = jnp.pad(kvb[:, :, 64:], ((0, 0), (0, 0), (0, 64))).reshape(MLA_LORA, HEADS * HEAD_W)
    w_out_e = gw["w_out_e"].reshape(D_MODEL, D_MODEL)
    woa = jnp.pad(w_out_e[:512].reshape(HEADS, 64, D_MODEL), ((0, 0), (0, 64), (0, 0))).reshape(HEADS * HEAD_W, D_MODEL)
    return dict(w_in=w_in, wq=wq, wk=wk, wv=wv, woa=woa, wob=w_out_e[512:])


def _unprep_grads(g):
    d_in_e = g["w_in"].reshape(N_DEV, 1568 // N_DEV, D_MODEL)
    d_qb = g["wq"].reshape(MLA_LORA, HEADS, HEAD_W)[:, :, :96].reshape(MLA_LORA, HEADS * 96)
    dk = g["wk"].reshape(MLA_LORA, HEADS, HEAD_W)[:, :, :64]
    dv = g["wv"].reshape(MLA_LORA, HEADS, HEAD_W)[:, :, :64]
    d_kvb = jnp.concatenate([dk, dv], axis=2).reshape(MLA_LORA, HEADS * 128)
    d_oa = g["woa"].reshape(HEADS, HEAD_W, D_MODEL)[:, :64].reshape(HEADS * 64, D_MODEL)
    dev_major = lambda a: a.reshape(a.shape[0], N_DEV, a.shape[1] // N_DEV).transpose(1, 0, 2)
    return dict(w_in_e=d_in_e, w_qb=dev_major(d_qb), w_kvb=dev_major(d_kvb),
                w_out_e=jnp.concatenate([d_oa, g["wob"]], axis=0).reshape(N_DEV, D_MODEL // N_DEV, D_MODEL))


def _local_step(x, positions, target, gw, sp, ex):
    w = _prep_weights(gw)
    T = x.shape[0]
    tm = min(TM, T)
    nt = T // tm
    half = MLA_ROPE // 2
    inv_freq = ROPE_BASE ** (-jnp.arange(half, dtype=F32) / half)
    invf_lane = jnp.concatenate([jnp.zeros((64,), F32), inv_freq, inv_freq, jnp.zeros((32,), F32)]).reshape(1, HEAD_W)
    tabs = _rope_tables(positions.reshape(T, 1), invf_lane)
    bias_full = jnp.repeat(sp["sgu_b"][0].T, 128, axis=1)
    sgu_w = sp["sgu_w"]
    gq, gkv = sp["mla_gq"], sp["mla_gkv"]
    ln1_g, ln1_b, ln2_g, ln2_b = sp["ln1_g"], sp["ln1_b"], sp["ln2_g"], sp["ln2_b"]
    zm, zs, cqn, ckvn, kr_rot = _mla_in(x, w["w_in"], tabs, gq, gkv, deps=[ex.first_token])
    q, k, v = _mla_qkv(cqn, ckvn, kr_rot, tabs, w["wq"], w["wk"], w["wv"])
    o_att, lse = _attn_fwd(q, k, v)
    b_out = _sgu_fwd(zs, sp["sgu_ln_g"], sp["sgu_ln_b"], sgu_w, bias_full)
    token = ex.weights_forward(after=[o_att, b_out])
    y1, h1, h1_bf = _proj_ln("l0_out_ln1", [o_att, b_out], [w["woa"], w["wob"]], x, ln1_g, ln1_b, 0, deps=[token])
    big = ex.weights_ready(after=[y1])
    w_ff1, w_in_o, w_out_o = big["w_ff1"], big["w_in_o"], big["w_out_o"].reshape(D_MODEL, D_MODEL)
    w_ff2 = [a.reshape(D_FF, D_MODEL) for a in big["w_ff2"]]
    a0, act0 = _mlp_up("l0", h1_bf, w_ff1[0])
    y2, h2, h2_bf = _proj_ln("l0_ff2_ln2", [act0], [w_ff2[0]], h1, ln2_g, ln2_b, 0)

    z1 = _tiled("l1_in", (1, nt), [_rb(h2_bf, tm), _res(w_in_o)], [_out(T, 4 * D_MODEL, F32, tm, 4 * D_MODEL)],
                _mmc_blocks(N_DEV, NN, lambda w, d: w[d]), direct=True)
    o_pre, hg, states = _hgrn_fwd(z1, sp["hg_lb"], sp["hg_gnorm"])
    y3, h3, h3_bf = _proj_ln("l1_out_ln1", [hg], [w_out_o], h2, ln1_g, ln1_b, 1)
    a1, act1 = _mlp_up("l1", h3_bf, w_ff1[1])

    gs, g0 = {}, {}
    dy4, dy4_bf, sq_err, gs["ln2_g1"], gs["ln2_b1"] = _proj_ln_loss("l1_ff2_loss", act1, w_ff2[1], h3, ln2_g, ln2_b, 1, target)
    gs["sq_err"] = sq_err
    da1, dw1_1, dw2_1 = _mlp_bwd_w("l1", h3_bf, a1, act1, dy4_bf, big["w_ff2"][1])
    dy3, dy3_bf, dhg, gs["ln1_g1"], gs["ln1_b1"] = _dh_ln_back("l1_dh_ln1", da1, w_ff1[1], dy4, y3, ln1_g, 1, proj=[w_out_o])
    d_out_o = _tiled("l1_dwout", (2, D_MODEL // TM), [_tl(hg, TM), _cw(dy3_bf, TN)],
                     [_out(D_MODEL, D_MODEL, F32, TM, TN), _out(D_MODEL, D_MODEL, BF16, TM, TN)], _mmc(TN_, epilogue=_twice))
    d_out_o = [a.reshape(N_DEV, D_MODEL // N_DEV, D_MODEL) for a in d_out_o]
    dz1, gs["hg_lb"], gs["hg_gnorm"] = _hgrn_bwd(z1, o_pre, dhg, states, sp["hg_lb"], sp["hg_gnorm"])
    d_in_o = _tiled("l1_dwin", (N_DEV, 1), [_res(h2_bf), _cw(dz1, TN)],
                    [_out_dev(D_MODEL, TN, D_MODEL), _out_dev(D_MODEL, TN, D_MODEL, BF16)], _mmc(TN_, epilogue=_twice))
    token = ex.direct_start("l1", [dw1_1, dw2_1, d_in_o, d_out_o])

    dy2, dy2_bf, gs["ln2_g0"], gs["ln2_b0"] = _dh_ln_back("l1_dh_ln2", dz1, w_in_o, dy3, y2, ln2_g, 0, deps=[token])
    da0, dw1_0, dw2_0 = _mlp_bwd_w("l0", h1_bf, a0, act0, dy2_bf, big["w_ff2"][0])
    token = ex.direct_start("l0m", [dw1_0, dw2_0])
    dy1, dy1_bf, dcat, gs["ln1_g0"], gs["ln1_b0"] = _dh_ln_back("l0_dh_ln1", da0, w_ff1[0], dy2, y1, ln1_g, 0,
                                                                 proj=[w["woa"], w["wob"]], deps=[token])
    g0["woa"], g0["wob"] = _out_weight_grads(o_att, b_out, dy1_bf)
    dzs, gs["sgu_w"], gs["sgu_ln_g"], gs["sgu_ln_b"], gs["sgu_b"] = _sgu_bwd(zs, dcat, sp["sgu_ln_g"], sp["sgu_ln_b"], sgu_w, bias_full)
    dq, dk, dv = _attn_bwd(q, k, v, o_att, lse, dcat)
    dzm, g0["wq"], g0["wk"], g0["wv"], gs["mla_gq"], gs["mla_gkv"] = _mla_back(zm, cqn, ckvn, tabs, gq, gkv, w["wq"], w["wk"], w["wv"],
                                                                                 dq, dk, dv)
    token = ex.small_start(gs)
    dx, g0["w_in"] = _in_back(x, dzm, dzs, dy1, w["w_in"], deps=[token])

    return sq_err, dx, _unprep_grads(g0), gs


def _me():
    return lax.axis_index("x"), lax.axis_index("y"), lax.axis_index("c")


ANY_SPEC = pl.BlockSpec(memory_space=pl.ANY)
HBM_SPEC = pl.BlockSpec(memory_space=pltpu.HBM)
SEM_SPEC = pl.BlockSpec(memory_space=pltpu.SEMAPHORE)
EFFECT = pltpu.SideEffectType.DATAFLOW_SIDE_EFFECTING


def _split_start(name, srcs, lands, n_sems, make_copies, after=()):
    n, m, k = len(srcs), len(lands), len(after)

    def body(*refs):
        for cp in make_copies(refs[:n], refs[n:n + m], refs[n + m + k], refs[n + m + k + 1]):
            cp.start()
        refs[-1][...] = jnp.zeros(refs[-1].shape, F32)

    out_shape = (pltpu.SemaphoreType.DMA((n_sems,)), pltpu.SemaphoreType.DMA((n_sems,)),
                 *[pltpu.HBM(a.shape, a.dtype) for a in (*srcs, *lands)], jax.ShapeDtypeStruct((8, 128), F32))
    res = pl.pallas_call(
        body, name=name, out_shape=out_shape, in_specs=[HBM_SPEC] * (n + m) + [ANY_SPEC] * k,
        out_specs=(SEM_SPEC, SEM_SPEC, *[HBM_SPEC] * (n + m), pl.BlockSpec(memory_space=pltpu.VMEM)),
        input_output_aliases={i: 2 + i for i in range(n + m)},
        compiler_params=pltpu.CompilerParams(has_side_effects=EFFECT),
    )(*[_hbm(a) for a in (*srcs, *lands)], *after)
    return res[0], res[1], list(res[2:2 + n]), list(res[2 + n:2 + n + m]), res[-1]


def _split_wait(name, send_sems, recv_sems, srcs, lands, after, make_copies):
    n, m = len(srcs), len(lands)

    def body(*refs):
        for cp in make_copies(refs[:n], refs[n:n + m], refs[n + m], refs[n + m + 1]):
            cp.wait_send()
            cp.wait_recv()

    res = pl.pallas_call(
        body, name=name, out_shape=tuple(pltpu.HBM(a.shape, a.dtype) for a in (*srcs, *lands)),
        in_specs=[HBM_SPEC] * (n + m) + [SEM_SPEC, SEM_SPEC] + [ANY_SPEC] * len(after), out_specs=tuple([HBM_SPEC] * (n + m)),
        input_output_aliases={i: i for i in range(n + m)},
        compiler_params=pltpu.CompilerParams(has_side_effects=EFFECT),
    )(*srcs, *lands, send_sems, recv_sems, *after)
    return list(res[:n]), list(res[n:])


def _place_own(shards, dev):
    n = len(shards)

    def kern(dev_ref, *refs):
        for x_ref, o_ref in zip(refs[:n], refs[n:]):
            o_ref[...] = x_ref[...].astype(o_ref.dtype)

    blocks = [(None, *a.shape[1:]) for a, _, _ in shards]
    nbytes = sum(_nbytes(b, a.dtype) + _nbytes(b, dt) for b, (a, _, dt) in zip(blocks, shards))
    return pl.pallas_call(
        kern, name="weights_place_own", out_shape=[pltpu.HBM((N_DEV, *a.shape[1:]), dt) for a, _, dt in shards],
        grid_spec=pltpu.PrefetchScalarGridSpec(
            num_scalar_prefetch=1, grid=(1,),
            in_specs=[pl.BlockSpec(b, functools.partial(lambda i, dev, l: (l, 0, 0), l=l)) for b, (_, l, _) in zip(blocks, shards)],
            out_specs=[pl.BlockSpec(b, lambda i, dev: (dev[0], 0, 0)) for b in blocks]),
        compiler_params=pltpu.CompilerParams(dimension_semantics=("arbitrary",), vmem_limit_bytes=_vmem(nbytes)),
    )(dev, *[_hbm(a) for a, _, _ in shards])


def _ag_first_copies(src_refs, out_refs, send_sems, recv_sems):
    x, y, c = _me()
    targets = [(x, y, 1 - c), (1 - x, y, c), (x, 1 - y, c), (1 - x, 1 - y, c)]
    return [pltpu.make_async_remote_copy(
        src_ref=out_refs[op].at[4 * x + 2 * y + c], dst_ref=out_refs[op].at[4 * x + 2 * y + c], send_sem=send_sems.at[4 * op + k],
        recv_sem=recv_sems.at[4 * op + k], device_id=to, device_id_type=MESH)
        for op in range(len(out_refs)) for k, to in enumerate(targets)]


def _ag_second_copies(src_refs, out_refs, send_sems, recv_sems):
    x, y, c = _me()
    chips = [(1 - x, y), (x, 1 - y), (1 - x, 1 - y)]
    return [pltpu.make_async_remote_copy(
        src_ref=out_refs[op].at[4 * cx + 2 * cy + c], dst_ref=out_refs[op].at[4 * cx + 2 * cy + c],
        send_sem=send_sems.at[3 * op + j], recv_sem=recv_sems.at[3 * op + j], device_id=(x, y, 1 - c), device_id_type=MESH)
        for op in range(len(out_refs)) for j, (cx, cy) in enumerate(chips)]


def _rs_sibling_copies(g_refs, out_refs, send_sems, recv_sems):
    x, y, c = _me()
    return [pltpu.make_async_remote_copy(
        src_ref=g_refs[op].at[k, 1 - c], dst_ref=out_refs[op].at[k], send_sem=send_sems.at[4 * op + k],
        recv_sem=recv_sems.at[4 * op + k], device_id=(x, y, 1 - c), device_id_type=MESH)
        for op in range(len(g_refs)) for k in range(4)]


def _rs_direct_copies(g_refs, land_refs, send_sems, recv_sems):
    x, y, c = _me()
    n = len(g_refs) // 2
    chips = [(1 - x, y), (x, 1 - y), (1 - x, 1 - y)]
    copies = []
    for op in range(n):
        g32, g16, from_sib, from_others = g_refs[op], g_refs[n + op], land_refs[op], land_refs[n + op]
        copies.append(pltpu.make_async_remote_copy(
            src_ref=g32.at[2 * x + y, 1 - c], dst_ref=from_sib, send_sem=send_sems.at[7 * op], recv_sem=recv_sems.at[7 * op],
            device_id=(x, y, 1 - c), device_id_type=MESH))
        for j, (cx, cy) in enumerate(chips):
            for s, cc in enumerate((c, 1 - c)):
                copies.append(pltpu.make_async_remote_copy(
                    src_ref=g16.at[2 * cx + cy, cc], dst_ref=from_others.at[2 * j + s], send_sem=send_sems.at[7 * op + 1 + 2 * j + s],
                    recv_sem=recv_sems.at[7 * op + 1 + 2 * j + s], device_id=(cx, cy, cc), device_id_type=MESH))
    return copies


def _rs_chip_copies(p_refs, out_refs, send_sems, recv_sems):
    x, y, c = _me()
    chips = [(1 - x, y), (x, 1 - y), (1 - x, 1 - y)]
    return [pltpu.make_async_remote_copy(
        src_ref=p_refs[op].at[2 * cx + cy], dst_ref=out_refs[op].at[j], send_sem=send_sems.at[3 * op + j],
        recv_sem=recv_sems.at[3 * op + j], device_id=(cx, cy, c), device_id_type=MESH)
        for op in range(len(p_refs)) for j, (cx, cy) in enumerate(chips)]


def _all_gather(placed):
    n = len(placed)

    def kern(*refs):
        in_refs, out_refs, (send_sems, recv_sems) = refs[:n], refs[n:2 * n], refs[2 * n:]
        x, y, c = _me()
        me, sibling = (x, y, c), (x, y, 1 - c)
        chips = [(1 - x, y), (x, 1 - y), (1 - x, 1 - y)]

        def copy(op, k, block, to, own=False):
            idx = 4 * block[0] + 2 * block[1] + block[2]
            return pltpu.make_async_remote_copy(
                src_ref=(in_refs if own else out_refs)[op].at[idx], dst_ref=out_refs[op].at[idx], send_sem=send_sems.at[7 * op + k],
                recv_sem=recv_sems.at[7 * op + k], device_id=to, device_id_type=MESH)

        first = []
        for op in range(n):
            first.append(copy(op, 0, me, sibling, own=True))
            first += [copy(op, 1 + j, me, (*chip, c), own=True) for j, chip in enumerate(chips)]
        for cp in first:
            cp.start()
        passed = []
        for j, chip in enumerate(chips):
            for op in range(n):
                copy(op, 1 + j, (*chip, c), me).wait_recv()
                passed.append(copy(op, 4 + j, (*chip, c), sibling))
                passed[-1].start()
        for op in range(n):
            copy(op, 0, sibling, me).wait_recv()
            for j, chip in enumerate(chips):
                copy(op, 4 + j, (*chip, 1 - c), me).wait_recv()
        for cp in first + passed:
            cp.wait_send()

    return pl.pallas_call(
        kern, name="weights_all_gather", out_shape=[pltpu.HBM(g.shape, g.dtype) for g in placed],
        in_specs=[ANY_SPEC] * n, out_specs=[ANY_SPEC] * n, input_output_aliases={i: i for i in range(n)},
        scratch_shapes=[pltpu.SemaphoreType.DMA((7 * n,)), pltpu.SemaphoreType.DMA((7 * n,))],
    )(*[_hbm(a) for a in placed])


def _row_tile(r, w, n_blocks):
    tr = r
    while tr > 8 and 2 * n_blocks * tr * w * 4 > 24 * 2**20:
        tr //= 2
    return tr


def _chip_sum(name, g, from_sibling, core):
    _, _, R, W = g.shape
    tr = _row_tile(R, W, 3)

    def kern(core_ref, g_ref, s_ref, o_ref):
        o_ref[...] = (g_ref[...] + s_ref[...]).astype(BF16)

    return pl.pallas_call(
        kern, name=name, out_shape=pltpu.HBM((4, R, W), BF16),
        grid_spec=pltpu.PrefetchScalarGridSpec(
            num_scalar_prefetch=1, grid=(4, R // tr),
            in_specs=[pl.BlockSpec((None, None, tr, W), lambda k, i, core: (k, core[0], i, 0)),
                      pl.BlockSpec((None, tr, W), lambda k, i, core: (k, i, 0))],
            out_specs=pl.BlockSpec((None, tr, W), lambda k, i, core: (k, i, 0))),
        compiler_params=pltpu.CompilerParams(dimension_semantics=("parallel", "parallel"), vmem_limit_bytes=_vmem(3 * tr * W * 4)),
    )(core, _hbm(g), _hbm(from_sibling))


def _adamw(w, g, m, v):
    m = ADAM_B1 * m + (1.0 - ADAM_B1) * g
    v = ADAM_B2 * v + (1.0 - ADAM_B2) * (g * g)
    m_hat = m / (1.0 - ADAM_B1 ** ADAM_STEP)
    v_hat = v / (1.0 - ADAM_B2 ** ADAM_STEP)
    return -ADAM_LR * (m_hat / (jnp.sqrt(v_hat) + ADAM_EPS) + ADAM_WD * w), m, v


def _finish_sharded(name, layers, w, m, v, where, deps=()):
    nl, R, W = w.shape
    n_other = layers[0][2].shape[0]
    tr = _row_tile(R, W, (8 + n_other) * nl)
    deps = _deps(deps)

    def kern(where_ref, *refs):
        w_ref, m_ref, v_ref = refs[3 * nl:3 * nl + 3]
        go_ref, d_ref, mo_ref, vo_ref = refs[3 * nl + 3 + len(deps):]
        for l in range(nl):
            g_ref, s_ref, c_ref = refs[3 * l:3 * l + 3]
            grad = g_ref[...] + s_ref[...]
            for j in range(n_other):
                grad = grad + c_ref[j].astype(F32)
            go_ref[l] = grad
            d_ref[l], mo_ref[l], vo_ref[l] = _adamw(w_ref[l], grad, m_ref[l], v_ref[l])

    row = pl.BlockSpec((nl, tr, W), lambda i, wh: (0, i, 0))
    in_specs, args = [], []
    for g, s, c in layers:
        sib = (pl.BlockSpec((None, tr, W), lambda i, wh: (wh[0], i, 0)) if s.ndim == 3 else pl.BlockSpec((tr, W), lambda i, wh: (i, 0)))
        in_specs += [pl.BlockSpec((None, None, tr, W), lambda i, wh: (wh[0], wh[1], i, 0)), sib,
                     pl.BlockSpec((n_other, tr, W), lambda i, wh: (0, i, 0))]
        args += [g, s, c]
    return pl.pallas_call(
        kern, name=name, out_shape=[pltpu.HBM((nl, R, W), F32)] * 4,
        grid_spec=pltpu.PrefetchScalarGridSpec(num_scalar_prefetch=1, grid=(R // tr,),
                                               in_specs=in_specs + [row, row, row] + [ANY_SPEC] * len(deps),
                                               out_specs=[row, row, row, row]),
        compiler_params=pltpu.CompilerParams(dimension_semantics=("parallel",),
                                             vmem_limit_bytes=_vmem(nl * (8 + n_other) * tr * W * 4)),
    )(where, *[_hbm(a) for a in (*args, w, m, v)], *deps)


SMALL_PLACE = (("mla_gq", 0, 0, 1, 256), ("mla_gkv", 0, 256, 1, 256), ("sgu_ln_g", 0, 512, 1, 512), ("sgu_ln_b", 1, 0, 1, 512),
               ("hg_lb", 2, 0, 2, 1024), ("ln1_g", 4, 0, 2, 1024), ("ln1_b", 6, 0, 2, 1024), ("sgu_b", 8, 0, 4, 128),
               ("ln2_g", 12, 0, 2, 1024), ("ln2_b", 14, 0, 2, 1024), ("hg_gnorm", 16, 0, 1, 1024))
SMALL_BUF_ROWS = 24
LOSS_ROW = 17


def _small_pack(gs, dev):
    pieces = [(gs["mla_gq"], 0, 0), (gs["mla_gkv"], 0, 256), (gs["sgu_ln_g"], 0, 512), (gs["sgu_ln_b"], 1, 0), (gs["hg_lb"], 2, 0),
              (gs["ln1_g0"], 4, 0), (gs["ln1_g1"], 5, 0), (gs["ln1_b0"], 6, 0), (gs["ln1_b1"], 7, 0), (gs["sgu_b"], 8, 0),
              (gs["ln2_g0"], 12, 0), (gs["ln2_g1"], 13, 0), (gs["ln2_b0"], 14, 0), (gs["ln2_b1"], 15, 0), (gs["hg_gnorm"], 16, 0),
              (gs["sq_err"], LOSS_ROW, 0)]
    n_p = len(pieces)

    def kern(dev_ref, *refs):
        a_ref, b_ref = refs[n_p + 1], refs[n_p + 2]
        a_ref[...] = jnp.zeros(a_ref.shape, F32)
        for ref, (_, r, l0) in zip(refs[:n_p], pieces):
            a_ref[r:r + ref.shape[0], l0:l0 + ref.shape[1]] = ref[...]
        b_ref[...] = refs[n_p][...]

    whole = lambda a: pl.BlockSpec(a.shape, functools.partial(lambda i, dev, nd: (0,) * nd, nd=a.ndim))
    return pl.pallas_call(
        kern, name="small_grads_pack",
        out_shape=[pltpu.HBM((N_DEV, SMALL_BUF_ROWS, D_MODEL), F32), pltpu.HBM((N_DEV, SGU_G, 128, 128), F32)],
        grid_spec=pltpu.PrefetchScalarGridSpec(
            num_scalar_prefetch=1, grid=(1,), in_specs=[whole(p[0]) for p in pieces] + [whole(gs["sgu_w"])],
            out_specs=[pl.BlockSpec((None, SMALL_BUF_ROWS, D_MODEL), lambda i, dev: (dev[0], 0, 0)),
                       pl.BlockSpec((None, SGU_G, 128, 128), lambda i, dev: (dev[0], 0, 0, 0))]),
    )(dev, *[p[0] for p in pieces], gs["sgu_w"])


def _small_copies(src_refs, land_refs, send_sems, recv_sems):
    px, py, pc = _me()
    me = 4 * px + 2 * py + pc
    return [pltpu.make_async_remote_copy(
        src_ref=land_refs[k].at[me], dst_ref=land_refs[k].at[me], send_sem=send_sems.at[2 * (r - 1) + k],
        recv_sem=recv_sems.at[2 * (r - 1) + k], device_id=(px ^ (r >> 2), py ^ ((r >> 1) & 1), pc ^ (r & 1)), device_id_type=MESH)
        for r in range(1, N_DEV) for k in range(2)]


def _small_adamw(slots_a, slots_b, given):
    names = [p[0] for p in SMALL_PLACE] + ["sgu_w"]
    n_names = len(names)
    wmv = [given[pre + name] for name in names for pre in ("", "m_", "v_")]
    vmem = pl.BlockSpec(memory_space=pltpu.VMEM)

    def kern(*refs):
        sum_a, sum_b = refs[0][0], refs[1][0]
        for d in range(1, N_DEV):
            sum_a, sum_b = sum_a + refs[0][d], sum_b + refs[1][d]
        wmv_refs, out_refs = refs[2:2 + 3 * n_names], refs[2 + 3 * n_names:]
        px, py, pc = _me()
        me = 4 * px + 2 * py + pc

        def own_block(full):
            acc = full[:, 0:128]
            for b in range(1, N_DEV):
                acc = jnp.where(me == b, full[:, b * 128:(b + 1) * 128], acc)
            return acc

        for idx, name in enumerate(names):
            w_ref, m_ref, v_ref = wmv_refs[3 * idx:3 * idx + 3]
            if name == "sgu_w":
                grad = sum_b[None]
            else:
                _, r, l0, nr, nl = SMALL_PLACE[idx]
                grad = sum_a[r:r + nr, l0:l0 + nl]
                if name == "hg_gnorm":
                    grad = own_block(grad)
                if name == "sgu_b":
                    grad = grad[None]
            res = (grad, *_adamw(w_ref[...], grad, m_ref[...], v_ref[...]))
            for o_ref, val in zip(out_refs[4 * idx:4 * idx + 4], res):
                o_ref[...] = val
        out_refs[4 * n_names][...] = (0.5 / D_MODEL) * jnp.sum(sum_a[LOSS_ROW:LOSS_ROW + 1, :], axis=1, keepdims=True)

    out_shape = [jax.ShapeDtypeStruct(given[name].shape, F32) for name in names for _ in range(4)]
    out_shape.append(jax.ShapeDtypeStruct((1, 1), F32))
    res = pl.pallas_call(
        kern, name="small_adamw", out_shape=out_shape, in_specs=[vmem] * (2 + len(wmv)), out_specs=[vmem] * len(out_shape),
    )(slots_a, slots_b, *wmv)
    out = {name: res[4 * idx:4 * idx + 4] for idx, name in enumerate(names)}
    out["loss"] = res[-1].reshape(())
    return out


class _Exchange:
    def __init__(self, given):
        self.given = given
        px, py, pc = _me()
        self.core = pc.reshape(1).astype(jnp.int32)
        self.dev = (4 * px + 2 * py + pc).reshape(1).astype(jnp.int32)
        self.where = jnp.stack([2 * px + py, pc]).astype(jnp.int32)
        self.state, self.layers = {}, {}

    def start_weights(self, lands, after):
        self.weights = _split_start("weights_first_start", [], lands, 4 * len(lands), _ag_first_copies, after=after)
        self.first_token = self.weights[4]

    def weights_forward(self, after):
        send_sems, recv_sems, shards, lands, _ = self.weights
        _, lands = _split_wait("weights_first_wait", send_sems, recv_sems, shards, lands, after, _ag_first_copies)
        self.weights = _split_start("weights_second_start", [], lands, 3 * len(lands), _ag_second_copies)
        return self.weights[4]

    def weights_ready(self, after):
        send_sems, recv_sems, shards, lands, _ = self.weights
        _, got = _split_wait("weights_second_wait", send_sems, recv_sems, shards, lands, after, _ag_second_copies)
        return dict(w_in_o=got[0], w_out_o=got[1], w_ff1=[got[2], got[3]], w_ff2=[got[4], got[5]])

    def small_start(self, gs):
        self.small = _split_start("small_grads_start", [], _small_pack(gs, self.dev), 14, _small_copies)
        return self.small[4]

    def small_finish(self, after):
        send_sems, recv_sems, _, lands, _ = self.small
        _, lands = _split_wait("small_grads_wait", send_sems, recv_sems, [], lands, after, _small_copies)
        return _small_adamw(lands[0], lands[1], self.given)

    def direct_start(self, tag, grads):
        f32 = [g[0].reshape(4, 2, *g[0].shape[1:]) for g in grads]
        bf16 = [g[1].reshape(4, 2, *g[1].shape[1:]) for g in grads]
        lands = [lax.empty(b.shape[2:], F32) for b in f32] + [lax.empty((6, *b.shape[2:]), BF16) for b in f32]
        self.state[tag] = _split_start(f"grads_{tag}_start", f32 + bf16, lands, 7 * len(grads), _rs_direct_copies)
        return self.state[tag][4]

    def direct_end(self, tag, after):
        send_sems, recv_sems, srcs, lands, _ = self.state[tag]
        srcs, lands = _split_wait(f"grads_{tag}_wait", send_sems, recv_sems, srcs, lands, after, _rs_direct_copies)
        n = len(lands) // 2
        self.layers[tag] = list(zip(srcs[:n], lands[:n], lands[n:]))

    def grads_start(self, tag, grads):
        blocks = [g.reshape(4, 2, *g.shape[1:]) for g in grads]
        lands = [lax.empty((4, *b.shape[2:]), F32) for b in blocks]
        self.state[tag] = _split_start(f"grads_{tag}_sibling_start", blocks, lands, 4 * len(blocks), _rs_sibling_copies)
        return self.state[tag][4]

    def grads_middle(self, tag, after):
        send_sems, recv_sems, blocks, lands, _ = self.state[tag]
        blocks, from_sibling = _split_wait(f"grads_{tag}_sibling_wait", send_sems, recv_sems, blocks, lands, [after], _rs_sibling_copies)
        sums = [_chip_sum(f"grads_{tag}_chip_sum_{k}", b, s, self.core) for k, (b, s) in enumerate(zip(blocks, from_sibling))]
        lands = [lax.empty((3, *p.shape[1:]), BF16) for p in sums]
        self.state[tag] = (blocks, from_sibling, _split_start(f"grads_{tag}_chips_start", sums, lands, 3 * len(sums), _rs_chip_copies))
        return self.state[tag][2][4]

    def grads_end(self, tag, after):
        blocks, from_sibling, (send_sems, recv_sems, sums, lands, _) = self.state[tag]
        after = list(after) if isinstance(after, (list, tuple)) else [after]
        _, from_chips = _split_wait(f"grads_{tag}_chips_wait", send_sems, recv_sems, sums, lands, after, _rs_chip_copies)
        self.layers[tag] = list(zip(blocks, from_sibling, from_chips))


def kernel(x, positions, w_in_e, mla_gq, mla_gkv, w_qb, w_kvb, sgu_ln_g, sgu_ln_b, sgu_w, sgu_b, w_out_e, w_in_o, hg_lb, hg_gnorm, w_out_o, ln1_g, ln1_b, w_ff1, w_ff2, ln2_g, ln2_b, loss_target, m_w_in_e, m_mla_gq, m_mla_gkv, m_w_qb, m_w_kvb, m_sgu_ln_g, m_sgu_ln_b, m_sgu_w, m_sgu_b, m_w_out_e, m_w_in_o, m_hg_lb, m_hg_gnorm, m_w_out_o, m_ln1_g, m_ln1_b, m_w_ff1, m_w_ff2, m_ln2_g, m_ln2_b, v_w_in_e, v_mla_gq, v_mla_gkv, v_w_qb, v_w_kvb, v_sgu_ln_g, v_sgu_ln_b, v_sgu_w, v_sgu_b, v_w_out_e, v_w_in_o, v_hg_lb, v_hg_gnorm, v_w_out_o, v_ln1_g, v_ln1_b, v_w_ff1, v_w_ff2, v_ln2_g, v_ln2_b):
    given = dict(locals())
    for n in ("w_in_e", "m_w_in_e", "v_w_in_e"):
        given[n] = jnp.swapaxes(given[n], 1, 2)
    ex = _Exchange(given)

    names = ["w_in_e", "w_qb", "w_kvb", "w_out_e"]
    placed = _place_own([(given[n], 0, BF16) for n in names] + [(hg_gnorm.reshape(1, 1, D_MODEL // N_DEV), 0, F32)]
                        + [(w_in_o, 0, BF16), (w_out_o, 0, BF16), (w_ff1, 0, BF16), (w_ff1, 1, BF16), (w_ff2, 0, BF16), (w_ff2, 1, BF16)],
                        ex.dev)
    got = _all_gather(placed[:5])
    ex.start_weights(placed[5:], after=[got[0]])
    gw = dict(zip(names, got[:4]))
    small_names = ["mla_gq", "mla_gkv", "sgu_ln_g", "sgu_ln_b", "sgu_w", "sgu_b", "hg_lb", "ln1_g", "ln1_b", "ln2_g", "ln2_b"]
    sp = {n: given[n] for n in small_names}
    sp["hg_gnorm"] = got[4].reshape(1, D_MODEL)

    _, dx, grads, gs = _local_step(x[0], positions[0], loss_target[0], gw, sp, ex)

    def finish(n, layers, deps=()):
        return _finish_sharded(f"finish_{n}", layers, given[n], given["m_" + n], given["v_" + n], ex.where, deps=deps)

    ex.direct_end("l1", after=[dx])
    ex.direct_end("l0m", after=[dx])
    l1, l0m = ex.layers["l1"], ex.layers["l0m"]
    results = {}
    token = ex.grads_start("l0s", [grads[n] for n in names])
    results["w_ff1"] = finish("w_ff1", [l0m[0], l1[0]], deps=[token])
    token = ex.grads_middle("l0s", after=results["w_ff1"][0])
    results["w_ff2"] = finish("w_ff2", [l0m[1], l1[1]], deps=[token])
    results["w_in_o"] = finish("w_in_o", [l1[2]], deps=[token])
    results["w_out_o"] = finish("w_out_o", [l1[3]], deps=[token])
    results.update(ex.small_finish(after=[results["w_in_o"][0]]))
    ex.grads_end("l0s", after=[results[n][0] for n in ("mla_gq", "w_ff2", "w_in_o", "w_out_o")])
    for n, layer in zip(names, ex.layers["l0s"]):
        results[n] = finish(n, [layer])
    results["w_in_e"] = [jnp.swapaxes(a, 1, 2) for a in results["w_in_e"]]

    order = ["w_in_e", "mla_gq", "mla_gkv", "w_qb", "w_kvb", "sgu_ln_g", "sgu_ln_b", "sgu_w", "sgu_b", "w_out_e", "w_in_o",
             "hg_lb", "hg_gnorm", "w_out_o", "ln1_g", "ln1_b", "w_ff1", "w_ff2", "ln2_g", "ln2_b"]
    return (results["loss"], dx[None], *[results[name][kind] for kind in range(4) for name in order])
```

```python
import functools
import math

import jax
import jax.numpy as jnp
import numpy as np
from jax import lax
from jax.experimental import pallas as pl
from jax.experimental.pallas import tpu as pltpu

F32 = jnp.float32
BF16 = jnp.bfloat16
MESH = pl.DeviceIdType.MESH
HIGHEST = lax.Precision.HIGHEST

D_MODEL = 1024
D_FF = 4096
N_DEV = 8
HEADS = 8
HEAD_W = 128
MLA_NOPE = 64
MLA_ROPE = 32
MLA_V = 64
MLA_LORA = 256
MLA_SCALE = (MLA_NOPE + MLA_ROPE) ** -0.5
ROPE_BASE = 10000.0
SGU_DIM = 512
SGU_G = 4
SGU_CHUNK = 128
HG_CHUNK = 64
HG_CHUNKS_PER_STEP = 4
ALPHA = (2 * 2) ** 0.25
EPS = 1e-5
ADAM_LR, ADAM_B1, ADAM_B2, ADAM_EPS, ADAM_WD, ADAM_STEP = 0.001, 0.9, 0.999, 1e-08, 0.01, 10

VMEM_CAP_V7X = 56 * 2**20
VMEM_SLACK = 12 * 2**20
TM = 512
TN = 512


def _vmem(block_bytes):
    return int(min(VMEM_CAP_V7X, 2 * block_bytes + VMEM_SLACK))


def _hbm(a):
    return pltpu.with_memory_space_constraint(a, pltpu.HBM)


def _nbytes(shape, dtype):
    return int(np.prod([d for d in shape if d is not None])) * jnp.dtype(dtype).itemsize


def _sig(x):
    return 1.0 / (1.0 + jnp.exp(-x))


def _gelu(x):
    c = math.sqrt(2.0 / math.pi)
    t = jnp.tanh(c * (x + 0.044715 * x * x * x))
    return 0.5 * x * (1.0 + t), t


def _gelu_grad(x, t):
    c = math.sqrt(2.0 / math.pi)
    return 0.5 * (1.0 + t) + 0.5 * x * (1.0 - t * t) * c * (1.0 + 3 * 0.044715 * x * x)


def _dot(a, b, dims, precision=None):
    return lax.dot_general(a, b, (dims, ((), ())), preferred_element_type=F32, precision=precision)


NN = ((1,), (0,))
NT = ((1,), (1,))
TN_ = ((0,), (0,))


def _deps(deps):
    return [d for d in deps if d is not None]


def _tiled(name, grid, ins, outs, compute, direct=False, deps=()):
    n_in, deps = len(ins), _deps(deps)
    n_skip = n_in + len(deps)

    def kern(*refs):
        if direct:
            compute(refs[:n_in], refs[n_skip:])
            return
        for o_ref, r in zip(refs[n_skip:], compute(*refs[:n_in])):
            o_ref[...] = r.astype(o_ref.dtype).reshape(o_ref.shape)

    swap = lambda f: (lambda j, i: f(i, j))
    nbytes = sum(_nbytes(blk, a.dtype) for a, blk, _ in ins) + sum(_nbytes(blk, dt) + _nbytes(blk, F32) for _, dt, blk, _ in outs)
    res = pl.pallas_call(
        kern, name=name, grid=grid,
        in_specs=[pl.BlockSpec(blk, swap(f), pipeline_mode=pl.Buffered(1) if tuple(blk) == tuple(a.shape) else None)
                  for a, blk, f in ins] + [ANY_SPEC] * len(deps),
        out_specs=[pl.BlockSpec(blk, swap(f)) for _, _, blk, f in outs],
        out_shape=[pltpu.HBM(shape, dt) for shape, dt, _, _ in outs],
        compiler_params=pltpu.CompilerParams(dimension_semantics=("parallel", "parallel"), vmem_limit_bytes=_vmem(nbytes)),
    )(*[_hbm(a) for a, _, _ in ins], *deps)
    return res if len(res) > 1 else res[0]


def _rb(a, tm, w=None, cb=0):
    return (a, (tm, a.shape[1] if w is None else w), lambda i, j: (i, cb))


def _cw(b, tn):
    return (b, (b.shape[0], tn), lambda i, j: (0, j))


def _tl(a, tm):
    return (a, (a.shape[0], tm), lambda i, j: (0, i))


def _out(m, n, dtype, tm, tn):
    return ((m, n), dtype, (tm, tn), lambda i, j: (i, j))


def _out_dev(k, n, tm, dtype=F32):
    return ((N_DEV, k, n), dtype, (None, tm, n), lambda i, j: (j, i, 0))


def _twice(acc):
    return acc, acc


def _mmc(dims, n_pairs=1, epilogue=None):
    def compute(*refs):
        acc = None
        for k in range(n_pairs):
            d = _dot(refs[2 * k][...].astype(BF16), refs[2 * k + 1][...].astype(BF16), dims)
            acc = d if acc is None else acc + d
        ext = [r[...] for r in refs[2 * n_pairs:]]
        return epilogue(acc, *ext) if epilogue is not None else (acc,)

    return compute


def _res(w):
    return (w, w.shape, functools.partial(lambda i, j, nd: (0,) * nd, nd=w.ndim))


def _mmc_blocks(nblk, dims, rhs_block, epilogue=None):
    def compute(in_refs, out_refs):
        a = in_refs[0][...].astype(BF16)
        for d in range(nblk):
            acc = _dot(a, rhs_block(in_refs[1], d).astype(BF16), dims)
            n = acc.shape[1]
            ext = [r[:, d * n:(d + 1) * n] for r in in_refs[2:]]
            res = epilogue(acc, *ext) if epilogue is not None else (acc,)
            for o_ref, r in zip(out_refs, res):
                o_ref[:, d * n:(d + 1) * n] = r.astype(o_ref.dtype)

    return compute


def _rowwise(name, body, rows, consts, out_rows, out_accs=(), tr=512, deps=()):
    T = rows[0][0].shape[0]
    tr = min(tr, T)
    deps = _deps(deps)
    nr, ncn, no, nd = len(rows), len(consts), len(out_rows), len(deps)

    def kern(*refs):
        accs = refs[nr + ncn + nd + no:]
        if accs:
            @pl.when(pl.program_id(0) == 0)
            def _():
                for a in accs:
                    a[...] = jnp.zeros(a.shape, a.dtype)
        body(refs[:nr], refs[nr:nr + ncn], refs[nr + ncn + nd:nr + ncn + nd + no], accs)

    in_specs = [pl.BlockSpec((tr, w), functools.partial(lambda i, cb: (i, cb), cb=cb)) for _, w, cb in rows]
    in_specs += [pl.BlockSpec(c.shape, functools.partial(lambda i, nd: (0,) * nd, nd=c.ndim), pipeline_mode=pl.Buffered(1))
                 for c in consts]
    in_specs += [ANY_SPEC] * nd
    out_specs = [pl.BlockSpec((tr, w), lambda i: (i, 0)) for w, _ in out_rows]
    out_specs += [pl.BlockSpec(s, functools.partial(lambda i, nd: (0,) * nd, nd=len(s))) for s, _ in out_accs]
    out_shape = [pltpu.HBM((T, w), dt) for w, dt in out_rows]
    out_shape += [pltpu.HBM(s, dt) for s, dt in out_accs]
    nbytes = sum(_nbytes((tr, w), a.dtype) for a, w, _ in rows) + sum(_nbytes(c.shape, c.dtype) for c in consts)
    nbytes += sum(_nbytes((tr, w), dt) for w, dt in out_rows) + sum(_nbytes(s, dt) for s, dt in out_accs)
    res = pl.pallas_call(
        kern, name=name, grid=(T // tr,), in_specs=in_specs, out_specs=out_specs, out_shape=out_shape,
        compiler_params=pltpu.CompilerParams(dimension_semantics=("arbitrary",), vmem_limit_bytes=_vmem(nbytes)),
    )(*[_hbm(a) for a, _, _ in rows], *[_hbm(c) for c in consts], *deps)
    return res if len(res) > 1 else res[0]


def _full(a):
    return (a, a.shape[1], 0)


def _ln_stats(y):
    mu = jnp.mean(y, axis=-1, keepdims=True)
    yc = y - mu
    r = lax.rsqrt(jnp.mean(yc * yc, axis=-1, keepdims=True) + EPS)
    return yc * r, r


def _row_halves(n):
    return [slice(0, n // 2), slice(n // 2, n)] if n >= 256 else [slice(0, n)]


def _ln_back(dh, xh, r, gain, dg_ref, db_ref):
    dg_ref[...] += jnp.sum(dh * xh, axis=0, keepdims=True)
    db_ref[...] += jnp.sum(dh, axis=0, keepdims=True)
    dx = dh * gain
    return r * (dx - jnp.mean(dx, axis=-1, keepdims=True) - xh * jnp.mean(dx * xh, axis=-1, keepdims=True))


def _proj_ln(name, acts, weights, h_in, g, b, layer, deps=()):
    n = len(acts)

    def body(rows, consts, outs, accs):
        acc = None
        for k in range(n):
            d = _dot(rows[k][...].astype(BF16), consts[k][...], NN)
            acc = d if acc is None else acc + d
        y = ALPHA * rows[n][...] + acc
        xh, _ = _ln_stats(y)
        h = xh * consts[n][layer:layer + 1, :] + consts[n + 1][layer:layer + 1, :]
        outs[0][...] = y
        outs[1][...] = h
        outs[2][...] = h.astype(BF16)

    return _rowwise(name, body, [_full(a) for a in acts] + [_full(h_in)], [*weights, g, b],
                    [(D_MODEL, F32), (D_MODEL, F32), (D_MODEL, BF16)], tr=TM, deps=deps)


def _proj_ln_loss(name, act, w2, h_in, g, b, layer, target):
    def body(rows, consts, outs, accs):
        y = ALPHA * rows[1][...] + _dot(rows[0][...], consts[0][...], NN)
        xh, r = _ln_stats(y)
        gain = consts[1][layer:layer + 1, :]
        err = xh * gain + consts[2][layer:layer + 1, :] - rows[2][...]
        accs[0][...] += jnp.sum(err * err, axis=0, keepdims=True)
        dy = _ln_back(err * (1.0 / D_MODEL), xh, r, gain, accs[1], accs[2])
        outs[0][...] = dy
        outs[1][...] = dy.astype(BF16)

    return _rowwise(name, body, [_full(act), _full(h_in), _full(target)], [w2, g, b], [(D_MODEL, F32), (D_MODEL, BF16)],
                    [((1, D_MODEL), F32)] * 3, tr=TM)


def _dh_ln_back(name, da, w, dy_next, y, g, layer, proj=(), deps=()):
    def body(rows, consts, outs, accs):
        n = consts[0].shape[2]
        for sl in _row_halves(rows[0].shape[0]):
            acc = ALPHA * rows[1][sl, :]
            for d in range(N_DEV):
                acc = acc + _dot(rows[0][sl, d * n:(d + 1) * n], consts[0][d], NT)
            xh, r = _ln_stats(rows[2][sl, :])
            dy = _ln_back(acc, xh, r, consts[1][layer:layer + 1, :], accs[0], accs[1])
            outs[0][sl, :] = dy
            dy_bf = dy.astype(BF16)
            outs[1][sl, :] = dy_bf
            off = 0
            for k, p in enumerate(proj):
                outs[2][sl, off:off + p.shape[0]] = _dot(dy_bf, consts[2 + k][...], NT).astype(BF16)
                off += p.shape[0]

    out_rows = [(D_MODEL, F32), (D_MODEL, BF16)] + ([(sum(p.shape[0] for p in proj), BF16)] if proj else [])
    return _rowwise(name, body, [_full(da), _full(dy_next), _full(y)], [w, g, *proj], out_rows,
                    [((1, D_MODEL), F32)] * 2, tr=TM, deps=deps)


def _relu2_epilogue(acc):
    a = jnp.maximum(acc, 0.0)
    return acc, a * a


def _mlp_up(tag, h_bf, w1):
    T = h_bf.shape[0]
    tm = min(TM, T)
    return _tiled(f"{tag}_ff1", (1, T // tm), [_rb(h_bf, tm), _res(w1)],
                  [_out(T, D_FF, BF16, tm, D_FF), _out(T, D_FF, BF16, tm, D_FF)],
                  _mmc_blocks(N_DEV, NN, lambda w, d: w[d], epilogue=_relu2_epilogue), direct=True)


def _mlp_bwd_w(tag, h_bf, a, act, dff_bf, w2, deps=()):
    T = h_bf.shape[0]
    tm = min(TM, T)
    da = _tiled(f"{tag}_dact", (1, T // tm), [_rb(dff_bf, tm), _res(w2), _rb(a, tm)], [_out(T, D_FF, BF16, tm, D_FF)],
                _mmc_blocks(N_DEV, NT, lambda w, d: w[d], epilogue=lambda acc, a_t: (acc * 2.0 * jnp.maximum(a_t.astype(F32), 0.0),)),
                direct=True, deps=deps)
    dw2 = _tiled(f"{tag}_dw2", (1, D_FF // TM), [_tl(act, TM), _res(dff_bf)],
                 [_out(D_FF, D_MODEL, F32, TM, D_MODEL), _out(D_FF, D_MODEL, BF16, TM, D_MODEL)], _mmc(TN_, epilogue=_twice))
    dw1 = _tiled(f"{tag}_dw1", (N_DEV, 1), [_res(h_bf), _cw(da, TN)],
                 [_out_dev(D_MODEL, TN, D_MODEL), _out_dev(D_MODEL, TN, D_MODEL, BF16)], _mmc(TN_, epilogue=_twice))
    return da, dw1, [a.reshape(N_DEV, D_FF // N_DEV, D_MODEL) for a in dw2]


def _rope_tables(positions_col, inv_freq):
    T, half = positions_col.shape[0], MLA_ROPE // 2
    groups = HEAD_W // half
    n = T // groups

    def spread(a, g, first_lane):
        shift = (first_lane - half * g) % HEAD_W
        return pltpu.roll(a, shift, 1) if shift else a

    def body(rows, consts, outs, accs):
        lane = lax.broadcasted_iota(jnp.int32, (n, HEAD_W), 1)
        pos = jnp.zeros((n, HEAD_W), F32)
        for g in range(groups):
            pos = jnp.where(lane // half == g, rows[0][g * n:(g + 1) * n, :].astype(F32), pos)
        ang = pos * consts[0][...]
        c, s = jnp.cos(ang), jnp.sin(ang)
        for g in range(groups):
            r = slice(g * n, (g + 1) * n)
            outs[0][r, :] = jnp.where(lane < 64, 1.0, jnp.where(lane < 80, spread(c, g, 64), jnp.where(lane < 96, spread(c, g, 80), 0.0)))
            outs[1][r, :] = jnp.where((lane >= 64) & (lane < 80), -spread(s, g, 64), 0.0)
            outs[2][r, :] = jnp.where((lane >= 80) & (lane < 96), spread(s, g, 80), 0.0)

    return _rowwise("rope_tables", body, [_full(positions_col)], [jnp.tile(inv_freq, groups).reshape(1, HEAD_W)],
                    [(HEAD_W, F32)] * 3, tr=T)


def _rope(x, c, s1, s2):
    return x * c + pltpu.roll(x, 112, 1) * s1 + pltpu.roll(x, 16, 1) * s2


def _rope_t(dx, c, s1, s2):
    return dx * c + pltpu.roll(dx * s1, 16, 1) + pltpu.roll(dx * s2, 112, 1)


def _rms(c):
    r = lax.rsqrt(jnp.mean(c * c, axis=-1, keepdims=True) + EPS)
    return c * r, r


def _rope_heads(x, c, s1, s2, fn):
    return jnp.concatenate([fn(x[:, h * HEAD_W:(h + 1) * HEAD_W], c, s1, s2) for h in range(HEADS)], axis=1)


def _mla_in(x, wm, ws, tabs, gq, gkv, deps=()):
    def body(rows, consts, outs, accs):
        xb = rows[0][...].astype(BF16)
        zm = _dot(xb, consts[0][...], NT)
        outs[0][...] = zm
        outs[1][...] = _dot(xb, consts[1][...], NT)
        outs[2][...] = (_rms(zm[:, 0:256])[0] * consts[2][...]).astype(BF16)
        outs[3][...] = (_rms(zm[:, 256:512])[0] * consts[3][...]).astype(BF16)
        outs[4][...] = _rope(zm[:, 512:640], rows[1][...], rows[2][...], rows[3][...])

    return _rowwise("l0_in", body, [_full(x)] + [_full(t) for t in tabs], [wm, ws, gq, gkv],
                    [(640, F32), (1024, F32), (256, BF16), (256, BF16), (HEAD_W, F32)], deps=deps)


def _mla_qkv(cqn, ckvn, kr_rot, tabs, wq, wk, wv):
    def body(rows, consts, outs, accs):
        c, s1, s2 = rows[3][...], rows[4][...], rows[5][...]
        outs[0][...] = _rope_heads(_dot(rows[0][...], consts[0][...], NN), c, s1, s2, _rope).astype(BF16)
        outs[1][...] = (_dot(rows[1][...], consts[1][...], NN) + jnp.concatenate([rows[2][...]] * HEADS, axis=1)).astype(BF16)
        outs[2][...] = _dot(rows[1][...], consts[2][...], NN).astype(BF16)

    rows = [_full(cqn), _full(ckvn), _full(kr_rot)] + [_full(t) for t in tabs]
    return _rowwise("l0_qkv", body, rows, [wq, wk, wv], [(HEADS * HEAD_W, BF16)] * 3)


def _mla_back(zm, cqn, ckvn, tabs, gq, gkv, wq, wk, wv, dq, dk, dv):
    def body(rows, consts, outs, accs):
        c, s1, s2 = rows[4][...], rows[5][...], rows[6][...]
        dk_t, dv_bf = rows[8][...], rows[9][...].astype(BF16)
        dq_bf = _rope_heads(rows[7][...], c, s1, s2, _rope_t).astype(BF16)
        dk_bf = dk_t.astype(BF16)
        accs[0][...] += _dot(rows[2][...], dq_bf, TN_)
        accs[1][...] += _dot(rows[3][...], dk_bf, TN_)
        accs[2][...] += _dot(rows[3][...], dv_bf, TN_)
        dlat = [_dot(dq_bf, consts[2][...], NT), _dot(dk_bf, consts[3][...], NT) + _dot(dv_bf, consts[4][...], NT)]
        for k in range(2):
            ch, r = _rms(rows[k][...])
            accs[3 + k][...] += jnp.sum(dlat[k] * ch, axis=0, keepdims=True)
            dc = dlat[k] * consts[k][...]
            outs[0][:, 256 * k:256 * (k + 1)] = (r * (dc - ch * jnp.mean(dc * ch, axis=-1, keepdims=True))).astype(BF16)
        dks = dk_t[:, 0:HEAD_W]
        for h in range(1, HEADS):
            dks = dks + dk_t[:, h * HEAD_W:(h + 1) * HEAD_W]
        lane = lax.broadcasted_iota(jnp.int32, dks.shape, 1)
        dks = jnp.where((lane >= 64) & (lane < 96), dks, 0.0)
        outs[0][:, 512:640] = _rope_t(dks, c, s1, s2).astype(BF16)

    rows = [(zm, 256, 0), (zm, 256, 1), _full(cqn), _full(ckvn)] + [_full(t) for t in tabs] + [_full(dq), _full(dk), _full(dv)]
    wide = HEADS * HEAD_W
    return _rowwise("l0_mla_back", body, rows, [gq, gkv, wq, wk, wv], [(640, BF16)],
                    [((MLA_LORA, wide), F32)] * 3 + [((1, MLA_LORA), F32)] * 2, tr=256)


def _in_back(x, dzm, dzs, dy, wm, ws, deps=()):
    def body(rows, consts, outs, accs):
        dzm_t, dzs_t = rows[1][...], rows[2][...]
        outs[0][...] = _dot(dzm_t, consts[0][...], NN) + _dot(dzs_t, consts[1][...], NN) + ALPHA * rows[3][...]
        xb = rows[0][...].astype(BF16)
        accs[0][...] += _dot(dzm_t, xb, TN_)
        accs[1][...] += _dot(dzs_t, xb, TN_)

    return _rowwise("l0_in_back", body, [_full(x), _full(dzm), _full(dzs), _full(dy)], [wm, ws], [(D_MODEL, F32)],
                    [((640, D_MODEL), F32), ((1024, D_MODEL), F32)], deps=deps)


def _out_weight_grads(o_att, b_out, dy_bf):
    def body(rows, consts, outs, accs):
        d = rows[2][...]
        accs[0][...] += _dot(rows[0][...].astype(BF16), d, TN_)
        accs[1][...] += _dot(rows[1][...], d, TN_)

    return _rowwise("l0_dw_out", body, [_full(o_att), _full(b_out), _full(dy_bf)], [], [],
                    [((HEADS * HEAD_W, D_MODEL), F32), ((SGU_DIM, D_MODEL), F32)])


def _attn_block(T):
    return min(1024, T)


def _attn_fwd(q, k, v):
    T = q.shape[0]
    BQ = _attn_block(T)
    nq = T // BQ

    def kern(q_ref, k_ref, v_ref, o_ref, lse_ref):
        def step(i, j, carry, masked):
            m, l, acc = carry
            qb = q_ref[pl.ds(pl.multiple_of(i * BQ, BQ), BQ), :]
            kb = k_ref[pl.ds(pl.multiple_of(j * BQ, BQ), BQ), :]
            vb = v_ref[pl.ds(pl.multiple_of(j * BQ, BQ), BQ), :]
            s = _dot(qb, kb, NT) * MLA_SCALE
            if masked:
                row = lax.broadcasted_iota(jnp.int32, s.shape, 0)
                col = lax.broadcasted_iota(jnp.int32, s.shape, 1)
                s = jnp.where(col <= row, s, -1e30)
            m_new = jnp.maximum(m, jnp.max(s, axis=-1, keepdims=True))
            p = jnp.exp(s - m_new)
            a = jnp.exp(m - m_new)
            l = a * l + jnp.sum(p, axis=-1, keepdims=True)
            acc = a * acc + _dot(p.astype(BF16), vb, NN)
            return m_new, l, acc

        def qloop(i, _):
            init = (jnp.full((BQ, 1), -1e30, F32), jnp.zeros((BQ, 1), F32), jnp.zeros((BQ, HEAD_W), F32))
            carry = lax.fori_loop(0, i, lambda j, c: step(i, j, c, False), init)
            m, l, acc = step(i, i, carry, True)
            rows = pl.ds(pl.multiple_of(i * BQ, BQ), BQ)
            o_ref[rows, :] = acc / l
            lse_ref[0, rows, :] = m + jnp.log(l)
            return 0

        lax.fori_loop(0, nq, qloop, 0)

    head = pl.BlockSpec((T, HEAD_W), lambda h: (0, h))
    nbytes = 3 * _nbytes((T, HEAD_W), BF16) + _nbytes((T, HEAD_W), F32) + _nbytes((T, 128), F32)
    return pl.pallas_call(
        kern, name="attn_fwd", grid=(HEADS,), in_specs=[head, head, head],
        out_specs=[head, pl.BlockSpec((1, T, 1), lambda h: (h, 0, 0))],
        out_shape=[pltpu.HBM((T, HEADS * HEAD_W), F32), pltpu.HBM((HEADS, T, 1), F32)],
        compiler_params=pltpu.CompilerParams(dimension_semantics=("parallel",), vmem_limit_bytes=_vmem(nbytes)),
    )(_hbm(q), _hbm(k), _hbm(v))


def _attn_bwd(q, k, v, o, lse, dcat, deps=()):
    T = q.shape[0]
    BQ = _attn_block(T)
    nq = T // BQ
    deps = _deps(deps)

    def kern(q_ref, k_ref, v_ref, o_ref, lse_ref, do_ref, *rest):
        dq_ref, dk_ref, dv_ref, dd_ref = rest[len(deps):]
        dq_ref[...] = jnp.zeros(dq_ref.shape, F32)

        def dloop(i, _):
            rows = pl.ds(pl.multiple_of(i * BQ, BQ), BQ)
            dd_ref[rows, :] = jnp.sum(do_ref[rows, :].astype(F32) * o_ref[rows, :], axis=-1, keepdims=True)
            return 0

        lax.fori_loop(0, nq, dloop, 0)

        def tile(q0, k0, n, carry, masked):
            dk_acc, dv_acc = carry
            rq = pl.ds(pl.multiple_of(q0, n), n)
            rk = pl.ds(pl.multiple_of(k0, n), n)
            qb, kb, vb, dob = q_ref[rq, :], k_ref[rk, :], v_ref[rk, :], do_ref[rq, :]
            s = _dot(qb, kb, NT) * MLA_SCALE
            p = jnp.exp(s - lse_ref[0, rq, :])
            if masked:
                row = lax.broadcasted_iota(jnp.int32, s.shape, 0)
                col = lax.broadcasted_iota(jnp.int32, s.shape, 1)
                p = jnp.where(col <= row, p, 0.0)
            dp = _dot(dob, vb, NT)
            ds = (p * (dp - dd_ref[rq, :]) * MLA_SCALE).astype(BF16)
            dv_acc = dv_acc + _dot(p.astype(BF16), dob, TN_)
            dk_acc = dk_acc + _dot(ds, qb, TN_)
            dq_ref[rq, :] += _dot(ds, kb, NN)
            return dk_acc, dv_acc

        def kloop(j, _):
            base, half = j * BQ, BQ // 2
            zero = (jnp.zeros((half, HEAD_W), F32), jnp.zeros((half, HEAD_W), F32))
            early = tile(base + half, base, half, tile(base, base, half, zero, True), False)
            late = tile(base + half, base + half, half, zero, True)
            carry = tuple(jnp.concatenate([a, b], axis=0) for a, b in zip(early, late))
            dk_acc, dv_acc = lax.fori_loop(j + 1, nq, lambda i, c: tile(i * BQ, base, BQ, c, False), carry)
            rk = pl.ds(pl.multiple_of(j * BQ, BQ), BQ)
            dk_ref[rk, :] = dk_acc
            dv_ref[rk, :] = dv_acc
            return 0

        lax.fori_loop(0, nq, kloop, 0)

    head = pl.BlockSpec((T, HEAD_W), lambda h: (0, h))
    nbytes = 4 * _nbytes((T, HEAD_W), BF16) + 5 * _nbytes((T, HEAD_W), F32) + 2 * _nbytes((T, 128), F32)
    return pl.pallas_call(
        kern, name="attn_bwd", grid=(HEADS,),
        in_specs=[head, head, head, head, pl.BlockSpec((1, T, 1), lambda h: (h, 0, 0)), head] + [ANY_SPEC] * len(deps),
        out_specs=[head, head, head],
        out_shape=[pltpu.HBM((T, HEADS * HEAD_W), F32)] * 3,
        scratch_shapes=[pltpu.VMEM((T, 1), F32)],
        compiler_params=pltpu.CompilerParams(dimension_semantics=("parallel",), vmem_limit_bytes=_vmem(nbytes)),
    )(*[_hbm(a) for a in (q, k, v, o, lse, dcat)], *deps)


def _sgu_common(u, v, ln_g, ln_b):
    ua, tu = _gelu(u)
    va, tv = _gelu(v)
    vh, r = _ln_stats(va)
    return ua, tu, tv, vh, r, vh * ln_g + ln_b


def _tril_mask(n):
    return lax.broadcasted_iota(jnp.int32, (n, n), 1) <= lax.broadcasted_iota(jnp.int32, (n, n), 0)


def _sgu_fwd(zs, ln_g, ln_b, w, bias_full):
    def body(rows, consts, outs, accs):
        ua, _, _, _, _, vn = _sgu_common(rows[0][...], rows[1][...], consts[0][...], consts[1][...])
        vn = vn.astype(BF16)
        tri = _tril_mask(SGU_CHUNK)
        for g in range(SGU_G):
            wg = jnp.where(tri, consts[2][0, g], 0.0).astype(BF16)
            cols = slice(g * 128, (g + 1) * 128)
            for c in range(ua.shape[0] // SGU_CHUNK):
                rws = slice(c * SGU_CHUNK, (c + 1) * SGU_CHUNK)
                mixed = _dot(wg, vn[rws, cols], NN) + consts[3][:, cols]
                outs[0][rws, cols] = (ua[rws, cols] * mixed).astype(BF16)

    return _rowwise("sgu_fwd", body, [(zs, 512, 0), (zs, 512, 1)], [ln_g, ln_b, w, bias_full], [(SGU_DIM, BF16)])


def _sgu_bwd(zs, dcat, ln_g, ln_b, w, bias_full):
    def body(rows, consts, outs, accs):
        u, v = rows[0][...], rows[1][...]
        ua, tu, tv, vh, r, vn = _sgu_common(u, v, consts[0][...], consts[1][...])
        dout = rows[2][...].astype(F32)
        vn_bf = vn.astype(BF16)
        tri = _tril_mask(SGU_CHUNK)
        dmixed = (dout * ua)
        dmixed_bf = dmixed.astype(BF16)
        ones = jnp.ones((8, SGU_CHUNK), F32)
        dvn_cols, mixed_cols = [], []
        for g in range(SGU_G):
            wg = jnp.where(tri, consts[2][0, g], 0.0).astype(BF16)
            cols = slice(g * 128, (g + 1) * 128)
            dvn_rows, mixed_rows = [], []
            dw = jnp.zeros((SGU_CHUNK, SGU_CHUNK), F32)
            dmix_sum = jnp.zeros((SGU_CHUNK, 128), F32)
            for c in range(u.shape[0] // SGU_CHUNK):
                rws = slice(c * SGU_CHUNK, (c + 1) * SGU_CHUNK)
                mixed_rows.append(_dot(wg, vn_bf[rws, cols], NN) + consts[3][:, cols])
                dvn_rows.append(_dot(wg, dmixed_bf[rws, cols], TN_))
                dw = dw + _dot(dmixed_bf[rws, cols], vn_bf[rws, cols], NT)
                dmix_sum = dmix_sum + dmixed[rws, cols]
            accs[0][g] += jnp.where(tri, dw, 0.0)
            accs[3][g:g + 1, :] += _dot(ones, dmix_sum, NT, precision=HIGHEST)[0:1, :]
            dvn_cols.append(jnp.concatenate(dvn_rows, axis=0))
            mixed_cols.append(jnp.concatenate(mixed_rows, axis=0))
        dvn = jnp.concatenate(dvn_cols, axis=1)
        mixed = jnp.concatenate(mixed_cols, axis=1)
        accs[1][...] += jnp.sum(dvn * vh, axis=0, keepdims=True)
        accs[2][...] += jnp.sum(dvn, axis=0, keepdims=True)
        dvh = dvn * consts[0][...]
        dva = r * (dvh - jnp.mean(dvh, axis=-1, keepdims=True) - vh * jnp.mean(dvh * vh, axis=-1, keepdims=True))
        outs[0][:, 0:512] = (dout * mixed * _gelu_grad(u, tu)).astype(BF16)
        outs[0][:, 512:1024] = (dva * _gelu_grad(v, tv)).astype(BF16)

    return _rowwise("sgu_bwd", body, [(zs, 512, 0), (zs, 512, 1), (dcat, 512, 2)], [ln_g, ln_b, w, bias_full], [(1024, BF16)],
                    [((SGU_G, 128, 128), F32), ((1, SGU_DIM), F32), ((1, SGU_DIM), F32), ((SGU_G, 128), F32)], tr=256)


def _lower_bound(hg_lb):
    a0, a1 = hg_lb[0:1, :], hg_lb[1:2, :]
    m = jnp.maximum(a0, a1)
    e0, e1 = jnp.exp(a0 - m), jnp.exp(a1 - m)
    s0, s1 = e0 / (e0 + e1), e1 / (e0 + e1)
    return (s0 + s1) - s0, s0, s1


def _prefix_rows(x, reverse=False):
    n = x.shape[0]
    row = lax.broadcasted_iota(jnp.int32, x.shape, 0)
    s = 1
    while s < n:
        if reverse:
            x = x + jnp.where(row < n - s, pltpu.roll(x, n - s, 0), 0.0)
        else:
            x = x + jnp.where(row >= s, pltpu.roll(x, s, 0), 0.0)
        s *= 2
    return x


def _hg_gates(qr, fr, lb):
    C = qr.shape[0]
    sq = _sig(qr)
    qf = qr * sq
    sf = _sig(fr)
    gate = lb + (1.0 - lb) * sf
    kk = 1.0 - gate
    tri = _tril_mask(C)
    b = _prefix_rows(jnp.log(gate))
    bref = b[C // 2 - 1:C // 2, :]
    bl = b[C - 1:C, :]
    e_b = jnp.exp(b)
    e_q = jnp.exp(b - bref)
    e_k = jnp.exp(bref - b)
    e_lb = jnp.exp(bl - b)
    return dict(sq=sq, qf=qf, sf=sf, gate=gate, kk=kk, tri=tri, bl=bl, e_b=e_b, e_q=e_q, e_k=e_k, e_lb=e_lb)


def _hgrn_fwd(z1, hg_lb, gnorm):
    T = z1.shape[0]
    C = min(HG_CHUNK, T)
    nc = T // C
    ns = HG_CHUNKS_PER_STEP if nc % HG_CHUNKS_PER_STEP == 0 else 1
    R = ns * C

    def kern(q_ref, f_ref, i_ref, g_ref, lb_ref, gn_ref, o_ref, hg_ref, st_ref, s_scr):
        @pl.when(pl.program_id(0) == 0)
        def _():
            s_scr[...] = jnp.zeros(s_scr.shape, F32)

        lb_all, _, _ = _lower_bound(lb_ref[...])
        for sub in range(ns):
            rows = slice(sub * C, (sub + 1) * C)
            st_ref[sub] = s_scr[...]
            for h in range(HEADS):
                cols = slice(h * HEAD_W, (h + 1) * HEAD_W)
                t = _hg_gates(q_ref[rows, cols], f_ref[rows, cols], lb_all[:, cols])
                v_bf = i_ref[rows, cols].astype(BF16)
                st = s_scr[h]
                a = jnp.where(t["tri"], _dot((t["qf"] * t["e_q"]).astype(BF16), (t["kk"] * t["e_k"]).astype(BF16), NT), 0.0)
                o = _dot(a.astype(BF16), v_bf, NN) + _dot((t["qf"] * t["e_b"]).astype(BF16), st.astype(BF16), NT)
                s_scr[h] = st * jnp.exp(t["bl"]) + _dot(v_bf, (t["kk"] * t["e_lb"]).astype(BF16), TN_)
                o_ref[rows, cols] = o
                gr = g_ref[rows, cols]
                r = lax.rsqrt(jnp.mean(o * o, axis=-1, keepdims=True) + EPS)
                hg_ref[rows, cols] = (o * r * gn_ref[:, cols] * (gr * _sig(gr))).astype(BF16)

    seg = lambda k: pl.BlockSpec((R, D_MODEL), functools.partial(lambda n, k: (n, k), k=k))
    row = pl.BlockSpec((R, D_MODEL), lambda n: (n, 0))
    nbytes = 6 * _nbytes((R, D_MODEL), F32) + (2 + ns) * _nbytes((HEADS, 128, 128), F32)
    return pl.pallas_call(
        kern, name="hgrn_fwd", grid=(nc // ns,),
        in_specs=[seg(0), seg(1), seg(2), seg(3), pl.BlockSpec((2, D_MODEL), lambda n: (0, 0)),
                  pl.BlockSpec((1, D_MODEL), lambda n: (0, 0))],
        out_specs=[row, row, pl.BlockSpec((ns, HEADS, 128, 128), lambda n: (n, 0, 0, 0))],
        out_shape=[pltpu.HBM((T, D_MODEL), F32), pltpu.HBM((T, D_MODEL), BF16),
                   pltpu.HBM((nc, HEADS, 128, 128), F32)],
        scratch_shapes=[pltpu.VMEM((HEADS, 128, 128), F32)],
        compiler_params=pltpu.CompilerParams(dimension_semantics=("arbitrary",), vmem_limit_bytes=_vmem(nbytes)),
    )(*[_hbm(a) for a in (z1, z1, z1, z1, hg_lb, gnorm)])


def _hgrn_bwd(z1, o_pre, dhg, states, hg_lb, gnorm):
    T = z1.shape[0]
    C = min(HG_CHUNK, T)
    nc = T // C
    ns = HG_CHUNKS_PER_STEP if nc % HG_CHUNKS_PER_STEP == 0 else 1
    R, steps = ns * C, nc // ns

    def kern(q_ref, f_ref, i_ref, g_ref, o_ref, dhg_ref, st_ref, lb_ref, gn_ref, dz_ref, dlb_ref, dgn_ref, ds_scr, dlb_scr):
        n = pl.program_id(0)

        @pl.when(n == 0)
        def _():
            ds_scr[...] = jnp.zeros(ds_scr.shape, F32)
            dlb_scr[...] = jnp.zeros(dlb_scr.shape, F32)
            dgn_ref[...] = jnp.zeros(dgn_ref.shape, F32)

        lb_all, s0, s1 = _lower_bound(lb_ref[...])
        for sub in reversed(range(ns)):
            rows = slice(sub * C, (sub + 1) * C)
            for h in range(HEADS):
                cols = slice(h * HEAD_W, (h + 1) * HEAD_W)
                lb = lb_all[:, cols]
                qr, fr = q_ref[rows, cols], f_ref[rows, cols]
                t = _hg_gates(qr, fr, lb)
                tri = t["tri"]
                v_bf = i_ref[rows, cols].astype(BF16)
                st_bf = st_ref[sub, h].astype(BF16)
                dst = ds_scr[h]
                dst_bf = dst.astype(BF16)
                o = o_ref[rows, cols]
                gr = g_ref[rows, cols]
                sg = _sig(gr)
                sil = gr * sg
                gn = gn_ref[:, cols]
                r = lax.rsqrt(jnp.mean(o * o, axis=-1, keepdims=True) + EPS)
                on = o * r
                dh = dhg_ref[rows, cols].astype(F32)
                dgn_ref[:, cols] += jnp.sum(dh * on * sil, axis=0, keepdims=True)
                dg = dh * on * gn * (sg * (1.0 + gr * (1.0 - sg)))
                don = dh * gn * sil
                do_bf = (r * (don - on * jnp.mean(don * on, axis=-1, keepdims=True))).astype(BF16)
                qe = (t["qf"] * t["e_q"]).astype(BF16)
                ke = (t["kk"] * t["e_k"]).astype(BF16)
                qb = (t["qf"] * t["e_b"]).astype(BF16)
                kh_bf = (t["kk"] * t["e_lb"]).astype(BF16)
                a_bf = jnp.where(tri, _dot(qe, ke, NT), 0.0).astype(BF16)
                da_bf = jnp.where(tri, _dot(do_bf, v_bf, NT), 0.0).astype(BF16)
                dv = _dot(a_bf, do_bf, TN_) + _dot(kh_bf, dst_bf, NT)
                dqe = _dot(da_bf, ke, NN)
                dqb = _dot(do_bf, st_bf, NN)
                dke = _dot(da_bf, qe, TN_)
                dkh = _dot(v_bf, dst_bf, NN)
                dqf = dqe * t["e_q"] + dqb * t["e_b"]
                dkk = dke * t["e_k"] + dkh * t["e_lb"]
                kh_r = kh_bf.astype(F32)
                db = qe.astype(F32) * dqe - ke.astype(F32) * dke + qb.astype(F32) * dqb - kh_r * dkh
                e_bl = jnp.exp(t["bl"])
                dbl = jnp.sum(dkh * kh_r, axis=0, keepdims=True) + e_bl * jnp.sum(st_ref[sub, h] * dst, axis=0, keepdims=True)
                dlg = _prefix_rows(db, reverse=True) + dbl
                ds_scr[h] = dst * e_bl + _dot(do_bf, qb, TN_)
                dgate = dlg / t["gate"] - dkk
                sf = t["sf"]
                dlb_scr[:, cols] += jnp.sum(dgate * (1.0 - sf), axis=0, keepdims=True)
                df = dgate * (1.0 - lb) * sf * (1.0 - sf)
                dq = dqf * (t["sq"] * (1.0 + qr * (1.0 - t["sq"])))
                dz_ref[rows, cols] = dq.astype(BF16)
                dz_ref[rows, D_MODEL + h * HEAD_W:D_MODEL + (h + 1) * HEAD_W] = df.astype(BF16)
                dz_ref[rows, 2 * D_MODEL + h * HEAD_W:2 * D_MODEL + (h + 1) * HEAD_W] = dv.astype(BF16)
                dz_ref[rows, 3 * D_MODEL + h * HEAD_W:3 * D_MODEL + (h + 1) * HEAD_W] = dg.astype(BF16)

        @pl.when(n == steps - 1)
        def _():
            d = s0 * s1 * dlb_scr[...]
            dlb_ref[0:1, :] = -d
            dlb_ref[1:2, :] = d

    seg = lambda k: pl.BlockSpec((R, D_MODEL), functools.partial(lambda n, k: (steps - 1 - n, k), k=k))
    nbytes = 6 * _nbytes((R, D_MODEL), F32) + _nbytes((R, 4 * D_MODEL), BF16) + (2 + ns) * _nbytes((HEADS, 128, 128), F32)
    return pl.pallas_call(
        kern, name="hgrn_bwd", grid=(steps,),
        in_specs=[seg(0), seg(1), seg(2), seg(3), seg(0), seg(0),
                  pl.BlockSpec((ns, HEADS, 128, 128), lambda n: (steps - 1 - n, 0, 0, 0)),
                  pl.BlockSpec((2, D_MODEL), lambda n: (0, 0)), pl.BlockSpec((1, D_MODEL), lambda n: (0, 0))],
        out_specs=[pl.BlockSpec((R, 4 * D_MODEL), lambda n: (steps - 1 - n, 0)),
                   pl.BlockSpec((2, D_MODEL), lambda n: (0, 0)), pl.BlockSpec((1, D_MODEL), lambda n: (0, 0))],
        out_shape=[pltpu.HBM((T, 4 * D_MODEL), BF16), pltpu.HBM((2, D_MODEL), F32),
                   pltpu.HBM((1, D_MODEL), F32)],
        scratch_shapes=[pltpu.VMEM((HEADS, 128, 128), F32), pltpu.VMEM((1, D_MODEL), F32)],
        compiler_params=pltpu.CompilerParams(dimension_semantics=("arbitrary",), vmem_limit_bytes=_vmem(nbytes)),
    )(*[_hbm(a) for a in (z1, z1, z1, z1, o_pre, dhg, states, hg_lb, gnorm)])


def _prep_weights(gw):
    w_in_e = gw["w_in_e"].reshape(1568, D_MODEL)
    kr = jnp.pad(w_in_e[512:544], ((64, 32), (0, 0)))
    wm = jnp.concatenate([w_in_e[0:512], kr], axis=0)
    ws = w_in_e[544:1568]
    w_qb = gw["w_qb"].transpose(1, 0, 2).reshape(MLA_LORA, HEADS, 96)
    wq = jnp.pad(w_qb, ((0, 0), (0, 0), (0, 32))).reshape(MLA_LORA, HEADS * HEAD_W)
    kvb = gw["w_kvb"].transpose(1, 0, 2).reshape(MLA_LORA, HEADS, 128)
    wk = jnp.pad(kvb[:, :, :64], ((0, 0), (0, 0), (0, 64))).reshape(MLA_LORA, HEADS * HEAD_W)
    wv = jnp.pad(kvb[:, :, 64:], ((0, 0), (0, 0), (0, 64))).reshape(MLA_LORA, HEADS * HEAD_W)
    w_out_e = gw["w_out_e"].reshape(D_MODEL, D_MODEL)
    woa = jnp.pad(w_out_e[:512].reshape(HEADS, 64, D_MODEL), ((0, 0), (0, 64), (0, 0))).reshape(HEADS * HEAD_W, D_MODEL)
    return dict(wm=wm, ws=ws, wq=wq, wk=wk, wv=wv, woa=woa, wob=w_out_e[512:])


def _unprep_grads(g):
    dwm, dws = g["wm"], g["ws"]
    d_in_e = jnp.concatenate([dwm[0:512], dwm[512 + 64:512 + 96], dws], axis=0).reshape(N_DEV, 1568 // N_DEV, D_MODEL)
    d_qb = g["wq"].reshape(MLA_LORA, HEADS, HEAD_W)[:, :, :96].reshape(MLA_LORA, HEADS * 96)
    dk = g["wk"].reshape(MLA_LORA, HEADS, HEAD_W)[:, :, :64]
    dv = g["wv"].reshape(MLA_LORA, HEADS, HEAD_W)[:, :, :64]
    d_kvb = jnp.concatenate([dk, dv], axis=2).reshape(MLA_LORA, HEADS * 128)
    d_oa = g["woa"].reshape(HEADS, HEAD_W, D_MODEL)[:, :64].reshape(HEADS * 64, D_MODEL)
    dev_major = lambda a: a.reshape(a.shape[0], N_DEV, a.shape[1] // N_DEV).transpose(1, 0, 2)
    return dict(w_in_e=d_in_e, w_qb=dev_major(d_qb), w_kvb=dev_major(d_kvb),
                w_out_e=jnp.concatenate([d_oa, g["wob"]], axis=0).reshape(N_DEV, D_MODEL // N_DEV, D_MODEL))


def _local_step(x, positions, target, gw, sp, ex):
    w = _prep_weights(gw)
    T = x.shape[0]
    tm = min(TM, T)
    nt = T // tm
    half = MLA_ROPE // 2
    inv_freq = ROPE_BASE ** (-jnp.arange(half, dtype=F32) / half)
    tabs = _rope_tables(positions.reshape(T, 1), inv_freq)
    bias_full = jnp.repeat(sp["sgu_b"][0].T, 128, axis=1)
    sgu_w = sp["sgu_w"]
    gq, gkv = sp["mla_gq"], sp["mla_gkv"]
    ln1_g, ln1_b, ln2_g, ln2_b = sp["ln1_g"], sp["ln1_b"], sp["ln2_g"], sp["ln2_b"]
    zm, zs, cqn, ckvn, kr_rot = _mla_in(x, w["wm"], w["ws"], tabs, gq, gkv, deps=[ex.first_token])
    q, k, v = _mla_qkv(cqn, ckvn, kr_rot, tabs, w["wq"], w["wk"], w["wv"])
    o_att, lse = _attn_fwd(q, k, v)
    b_out = _sgu_fwd(zs, sp["sgu_ln_g"], sp["sgu_ln_b"], sgu_w, bias_full)
    token = ex.weights_forward(after=[o_att, b_out])
    y1, h1, h1_bf = _proj_ln("l0_out_ln1", [o_att, b_out], [w["woa"], w["wob"]], x, ln1_g, ln1_b, 0, deps=[token])
    big = ex.weights_ready(after=[y1])
    w_ff1, w_in_o, w_out_o = big["w_ff1"], big["w_in_o"], big["w_out_o"].reshape(D_MODEL, D_MODEL)
    w_ff2 = [a.reshape(D_FF, D_MODEL) for a in big["w_ff2"]]
    a0, act0 = _mlp_up("l0", h1_bf, w_ff1[0])
    y2, h2, h2_bf = _proj_ln("l0_ff2_ln2", [act0], [w_ff2[0]], h1, ln2_g, ln2_b, 0)

    z1 = _tiled("l1_in", (1, nt), [_rb(h2_bf, tm), _res(w_in_o)], [_out(T, 4 * D_MODEL, F32, tm, 4 * D_MODEL)],
                _mmc_blocks(N_DEV, NN, lambda w, d: w[d]), direct=True)
    o_pre, hg, states = _hgrn_fwd(z1, sp["hg_lb"], sp["hg_gnorm"])
    y3, h3, h3_bf = _proj_ln("l1_out_ln1", [hg], [w_out_o], h2, ln1_g, ln1_b, 1)
    a1, act1 = _mlp_up("l1", h3_bf, w_ff1[1])

    gs, g0 = {}, {}
    dy4, dy4_bf, sq_err, gs["ln2_g1"], gs["ln2_b1"] = _proj_ln_loss("l1_ff2_loss", act1, w_ff2[1], h3, ln2_g, ln2_b, 1, target)
    gs["sq_err"] = sq_err
    da1, dw1_1, dw2_1 = _mlp_bwd_w("l1", h3_bf, a1, act1, dy4_bf, big["w_ff2"][1])
    dy3, dy3_bf, dhg, gs["ln1_g1"], gs["ln1_b1"] = _dh_ln_back("l1_dh_ln1", da1, w_ff1[1], dy4, y3, ln1_g, 1, proj=[w_out_o])
    d_out_o = _tiled("l1_dwout", (2, D_MODEL // TM), [_tl(hg, TM), _cw(dy3_bf, TN)],
                     [_out(D_MODEL, D_MODEL, F32, TM, TN), _out(D_MODEL, D_MODEL, BF16, TM, TN)], _mmc(TN_, epilogue=_twice))
    d_out_o = [a.reshape(N_DEV, D_MODEL // N_DEV, D_MODEL) for a in d_out_o]
    dz1, gs["hg_lb"], gs["hg_gnorm"] = _hgrn_bwd(z1, o_pre, dhg, states, sp["hg_lb"], sp["hg_gnorm"])
    d_in_o = _tiled("l1_dwin", (N_DEV, 1), [_res(h2_bf), _cw(dz1, TN)],
                    [_out_dev(D_MODEL, TN, D_MODEL), _out_dev(D_MODEL, TN, D_MODEL, BF16)], _mmc(TN_, epilogue=_twice))
    token = ex.direct_start("l1", [dw1_1, dw2_1, d_in_o, d_out_o])

    dy2, dy2_bf, gs["ln2_g0"], gs["ln2_b0"] = _dh_ln_back("l1_dh_ln2", dz1, w_in_o, dy3, y2, ln2_g, 0, deps=[token])
    da0, dw1_0, dw2_0 = _mlp_bwd_w("l0", h1_bf, a0, act0, dy2_bf, big["w_ff2"][0])
    token = ex.direct_start("l0m", [dw1_0, dw2_0])
    dy1, dy1_bf, dcat, gs["ln1_g0"], gs["ln1_b0"] = _dh_ln_back("l0_dh_ln1", da0, w_ff1[0], dy2, y1, ln1_g, 0,
                                                                 proj=[w["woa"], w["wob"]], deps=[token])
    g0["woa"], g0["wob"] = _out_weight_grads(o_att, b_out, dy1_bf)
    dzs, gs["sgu_w"], gs["sgu_ln_g"], gs["sgu_ln_b"], gs["sgu_b"] = _sgu_bwd(zs, dcat, sp["sgu_ln_g"], sp["sgu_ln_b"], sgu_w, bias_full)
    dq, dk, dv = _attn_bwd(q, k, v, o_att, lse, dcat)
    dzm, g0["wq"], g0["wk"], g0["wv"], gs["mla_gq"], gs["mla_gkv"] = _mla_back(zm, cqn, ckvn, tabs, gq, gkv, w["wq"], w["wk"], w["wv"],
                                                                                 dq, dk, dv)
    token = ex.small_start(gs)
    dx, g0["wm"], g0["ws"] = _in_back(x, dzm, dzs, dy1, w["wm"], w["ws"], deps=[token])

    return sq_err, dx, _unprep_grads(g0), gs


def _me():
    return lax.axis_index("x"), lax.axis_index("y"), lax.axis_index("c")


ANY_SPEC = pl.BlockSpec(memory_space=pl.ANY)
HBM_SPEC = pl.BlockSpec(memory_space=pltpu.HBM)
SEM_SPEC = pl.BlockSpec(memory_space=pltpu.SEMAPHORE)
EFFECT = pltpu.SideEffectType.DATAFLOW_SIDE_EFFECTING


def _split_start(name, srcs, lands, n_sems, make_copies, after=()):
    n, m, k = len(srcs), len(lands), len(after)

    def body(*refs):
        for cp in make_copies(refs[:n], refs[n:n + m], refs[n + m + k], refs[n + m + k + 1]):
            cp.start()
        refs[-1][...] = jnp.zeros(refs[-1].shape, F32)

    out_shape = (pltpu.SemaphoreType.DMA((n_sems,)), pltpu.SemaphoreType.DMA((n_sems,)),
                 *[pltpu.HBM(a.shape, a.dtype) for a in (*srcs, *lands)], jax.ShapeDtypeStruct((8, 128), F32))
    res = pl.pallas_call(
        body, name=name, out_shape=out_shape, in_specs=[HBM_SPEC] * (n + m) + [ANY_SPEC] * k,
        out_specs=(SEM_SPEC, SEM_SPEC, *[HBM_SPEC] * (n + m), pl.BlockSpec(memory_space=pltpu.VMEM)),
        input_output_aliases={i: 2 + i for i in range(n + m)},
        compiler_params=pltpu.CompilerParams(has_side_effects=EFFECT),
    )(*[_hbm(a) for a in (*srcs, *lands)], *after)
    return res[0], res[1], list(res[2:2 + n]), list(res[2 + n:2 + n + m]), res[-1]


def _split_wait(name, send_sems, recv_sems, srcs, lands, after, make_copies):
    n, m = len(srcs), len(lands)

    def body(*refs):
        for cp in make_copies(refs[:n], refs[n:n + m], refs[n + m], refs[n + m + 1]):
            cp.wait_send()
            cp.wait_recv()

    res = pl.pallas_call(
        body, name=name, out_shape=tuple(pltpu.HBM(a.shape, a.dtype) for a in (*srcs, *lands)),
        in_specs=[HBM_SPEC] * (n + m) + [SEM_SPEC, SEM_SPEC] + [ANY_SPEC] * len(after), out_specs=tuple([HBM_SPEC] * (n + m)),
        input_output_aliases={i: i for i in range(n + m)},
        compiler_params=pltpu.CompilerParams(has_side_effects=EFFECT),
    )(*srcs, *lands, send_sems, recv_sems, *after)
    return list(res[:n]), list(res[n:])


def _place_own(shards, dev):
    n = len(shards)

    def kern(dev_ref, *refs):
        for x_ref, o_ref in zip(refs[:n], refs[n:]):
            o_ref[...] = x_ref[...].astype(o_ref.dtype)

    blocks = [(None, *a.shape[1:]) for a, _, _ in shards]
    nbytes = sum(_nbytes(b, a.dtype) + _nbytes(b, dt) for b, (a, _, dt) in zip(blocks, shards))
    return pl.pallas_call(
        kern, name="weights_place_own", out_shape=[pltpu.HBM((N_DEV, *a.shape[1:]), dt) for a, _, dt in shards],
        grid_spec=pltpu.PrefetchScalarGridSpec(
            num_scalar_prefetch=1, grid=(1,),
            in_specs=[pl.BlockSpec(b, functools.partial(lambda i, dev, l: (l, 0, 0), l=l)) for b, (_, l, _) in zip(blocks, shards)],
            out_specs=[pl.BlockSpec(b, lambda i, dev: (dev[0], 0, 0)) for b in blocks]),
        compiler_params=pltpu.CompilerParams(dimension_semantics=("arbitrary",), vmem_limit_bytes=_vmem(nbytes)),
    )(dev, *[_hbm(a) for a, _, _ in shards])


def _ag_first_copies(src_refs, out_refs, send_sems, recv_sems):
    x, y, c = _me()
    targets = [(x, y, 1 - c), (1 - x, y, c), (x, 1 - y, c), (1 - x, 1 - y, c)]
    return [pltpu.make_async_remote_copy(
        src_ref=out_refs[op].at[4 * x + 2 * y + c], dst_ref=out_refs[op].at[4 * x + 2 * y + c], send_sem=send_sems.at[4 * op + k],
        recv_sem=recv_sems.at[4 * op + k], device_id=to, device_id_type=MESH)
        for op in range(len(out_refs)) for k, to in enumerate(targets)]


def _ag_second_copies(src_refs, out_refs, send_sems, recv_sems):
    x, y, c = _me()
    chips = [(1 - x, y), (x, 1 - y), (1 - x, 1 - y)]
    return [pltpu.make_async_remote_copy(
        src_ref=out_refs[op].at[4 * cx + 2 * cy + c], dst_ref=out_refs[op].at[4 * cx + 2 * cy + c],
        send_sem=send_sems.at[3 * op + j], recv_sem=recv_sems.at[3 * op + j], device_id=(x, y, 1 - c), device_id_type=MESH)
        for op in range(len(out_refs)) for j, (cx, cy) in enumerate(chips)]


def _rs_sibling_copies(g_refs, out_refs, send_sems, recv_sems):
    x, y, c = _me()
    return [pltpu.make_async_remote_copy(
        src_ref=g_refs[op].at[k, 1 - c], dst_ref=out_refs[op].at[k], send_sem=send_sems.at[4 * op + k],
        recv_sem=recv_sems.at[4 * op + k], device_id=(x, y, 1 - c), device_id_type=MESH)
        for op in range(len(g_refs)) for k in range(4)]


def _rs_direct_copies(g_refs, land_refs, send_sems, recv_sems):
    x, y, c = _me()
    n = len(g_refs) // 2
    chips = [(1 - x, y), (x, 1 - y), (1 - x, 1 - y)]
    copies = []
    for op in range(n):
        g32, g16, from_sib, from_others = g_refs[op], g_refs[n + op], land_refs[op], land_refs[n + op]
        copies.append(pltpu.make_async_remote_copy(
            src_ref=g32.at[2 * x + y, 1 - c], dst_ref=from_sib, send_sem=send_sems.at[7 * op], recv_sem=recv_sems.at[7 * op],
            device_id=(x, y, 1 - c), device_id_type=MESH))
        for j, (cx, cy) in enumerate(chips):
            for s, cc in enumerate((c, 1 - c)):
                copies.append(pltpu.make_async_remote_copy(
                    src_ref=g16.at[2 * cx + cy, cc], dst_ref=from_others.at[2 * j + s], send_sem=send_sems.at[7 * op + 1 + 2 * j + s],
                    recv_sem=recv_sems.at[7 * op + 1 + 2 * j + s], device_id=(cx, cy, cc), device_id_type=MESH))
    return copies


def _rs_chip_copies(p_refs, out_refs, send_sems, recv_sems):
    x, y, c = _me()
    chips = [(1 - x, y), (x, 1 - y), (1 - x, 1 - y)]
    return [pltpu.make_async_remote_copy(
        src_ref=p_refs[op].at[2 * cx + cy], dst_ref=out_refs[op].at[j], send_sem=send_sems.at[3 * op + j],
        recv_sem=recv_sems.at[3 * op + j], device_id=(cx, cy, c), device_id_type=MESH)
        for op in range(len(p_refs)) for j, (cx, cy) in enumerate(chips)]


def _all_gather(placed):
    n = len(placed)

    def kern(*refs):
        in_refs, out_refs, (send_sems, recv_sems) = refs[:n], refs[n:2 * n], refs[2 * n:]
        x, y, c = _me()
        me, sibling = (x, y, c), (x, y, 1 - c)
        chips = [(1 - x, y), (x, 1 - y), (1 - x, 1 - y)]

        def copy(op, k, block, to, own=False):
            idx = 4 * block[0] + 2 * block[1] + block[2]
            return pltpu.make_async_remote_copy(
                src_ref=(in_refs if own else out_refs)[op].at[idx], dst_ref=out_refs[op].at[idx], send_sem=send_sems.at[7 * op + k],
                recv_sem=recv_sems.at[7 * op + k], device_id=to, device_id_type=MESH)

        first = []
        for op in range(n):
            first.append(copy(op, 0, me, sibling, own=True))
            first += [copy(op, 1 + j, me, (*chip, c), own=True) for j, chip in enumerate(chips)]
        for cp in first:
            cp.start()
        passed = []
        for j, chip in enumerate(chips):
            for op in range(n):
                copy(op, 1 + j, (*chip, c), me).wait_recv()
                passed.append(copy(op, 4 + j, (*chip, c), sibling))
                passed[-1].start()
        for op in range(n):
            copy(op, 0, sibling, me).wait_recv()
            for j, chip in enumerate(chips):
                copy(op, 4 + j, (*chip, 1 - c), me).wait_recv()
        for cp in first + passed:
            cp.wait_send()

    return pl.pallas_call(
        kern, name="weights_all_gather", out_shape=[pltpu.HBM(g.shape, g.dtype) for g in placed],
        in_specs=[ANY_SPEC] * n, out_specs=[ANY_SPEC] * n, input_output_aliases={i: i for i in range(n)},
        scratch_shapes=[pltpu.SemaphoreType.DMA((7 * n,)), pltpu.SemaphoreType.DMA((7 * n,))],
    )(*[_hbm(a) for a in placed])


def _row_tile(r, w, n_blocks):
    tr = r
    while tr > 8 and 2 * n_blocks * tr * w * 4 > 24 * 2**20:
        tr //= 2
    return tr


def _chip_sum(name, g, from_sibling, core):
    _, _, R, W = g.shape
    tr = _row_tile(R, W, 3)

    def kern(core_ref, g_ref, s_ref, o_ref):
        o_ref[...] = (g_ref[...] + s_ref[...]).astype(BF16)

    return pl.pallas_call(
        kern, name=name, out_shape=pltpu.HBM((4, R, W), BF16),
        grid_spec=pltpu.PrefetchScalarGridSpec(
            num_scalar_prefetch=1, grid=(4, R // tr),
            in_specs=[pl.BlockSpec((None, None, tr, W), lambda k, i, core: (k, core[0], i, 0)),
                      pl.BlockSpec((None, tr, W), lambda k, i, core: (k, i, 0))],
            out_specs=pl.BlockSpec((None, tr, W), lambda k, i, core: (k, i, 0))),
        compiler_params=pltpu.CompilerParams(dimension_semantics=("parallel", "parallel"), vmem_limit_bytes=_vmem(3 * tr * W * 4)),
    )(core, _hbm(g), _hbm(from_sibling))


def _adamw(w, g, m, v):
    m = ADAM_B1 * m + (1.0 - ADAM_B1) * g
    v = ADAM_B2 * v + (1.0 - ADAM_B2) * (g * g)
    m_hat = m / (1.0 - ADAM_B1 ** ADAM_STEP)
    v_hat = v / (1.0 - ADAM_B2 ** ADAM_STEP)
    return -ADAM_LR * (m_hat / (jnp.sqrt(v_hat) + ADAM_EPS) + ADAM_WD * w), m, v


def _finish_sharded(name, layers, w, m, v, where, deps=()):
    nl, R, W = w.shape
    n_other = layers[0][2].shape[0]
    tr = _row_tile(R, W, (8 + n_other) * nl)
    deps = _deps(deps)

    def kern(where_ref, *refs):
        w_ref, m_ref, v_ref = refs[3 * nl:3 * nl + 3]
        go_ref, d_ref, mo_ref, vo_ref = refs[3 * nl + 3 + len(deps):]
        for l in range(nl):
            g_ref, s_ref, c_ref = refs[3 * l:3 * l + 3]
            grad = g_ref[...] + s_ref[...]
            for j in range(n_other):
                grad = grad + c_ref[j].astype(F32)
            go_ref[l] = grad
            d_ref[l], mo_ref[l], vo_ref[l] = _adamw(w_ref[l], grad, m_ref[l], v_ref[l])

    row = pl.BlockSpec((nl, tr, W), lambda i, wh: (0, i, 0))
    in_specs, args = [], []
    for g, s, c in layers:
        sib = (pl.BlockSpec((None, tr, W), lambda i, wh: (wh[0], i, 0)) if s.ndim == 3 else pl.BlockSpec((tr, W), lambda i, wh: (i, 0)))
        in_specs += [pl.BlockSpec((None, None, tr, W), lambda i, wh: (wh[0], wh[1], i, 0)), sib,
                     pl.BlockSpec((n_other, tr, W), lambda i, wh: (0, i, 0))]
        args += [g, s, c]
    return pl.pallas_call(
        kern, name=name, out_shape=[pltpu.HBM((nl, R, W), F32)] * 4,
        grid_spec=pltpu.PrefetchScalarGridSpec(num_scalar_prefetch=1, grid=(R // tr,),
                                               in_specs=in_specs + [row, row, row] + [ANY_SPEC] * len(deps),
                                               out_specs=[row, row, row, row]),
        compiler_params=pltpu.CompilerParams(dimension_semantics=("parallel",),
                                             vmem_limit_bytes=_vmem(nl * (8 + n_other) * tr * W * 4)),
    )(where, *[_hbm(a) for a in (*args, w, m, v)], *deps)


SMALL_PLACE = (("mla_gq", 0, 0, 1, 256), ("mla_gkv", 0, 256, 1, 256), ("sgu_ln_g", 0, 512, 1, 512), ("sgu_ln_b", 1, 0, 1, 512),
               ("hg_lb", 2, 0, 2, 1024), ("ln1_g", 4, 0, 2, 1024), ("ln1_b", 6, 0, 2, 1024), ("sgu_b", 8, 0, 4, 128),
               ("ln2_g", 12, 0, 2, 1024), ("ln2_b", 14, 0, 2, 1024), ("hg_gnorm", 16, 0, 1, 1024))
SMALL_BUF_ROWS = 24
LOSS_ROW = 17


def _small_pack(gs, dev):
    pieces = [(gs["mla_gq"], 0, 0), (gs["mla_gkv"], 0, 256), (gs["sgu_ln_g"], 0, 512), (gs["sgu_ln_b"], 1, 0), (gs["hg_lb"], 2, 0),
              (gs["ln1_g0"], 4, 0), (gs["ln1_g1"], 5, 0), (gs["ln1_b0"], 6, 0), (gs["ln1_b1"], 7, 0), (gs["sgu_b"], 8, 0),
              (gs["ln2_g0"], 12, 0), (gs["ln2_g1"], 13, 0), (gs["ln2_b0"], 14, 0), (gs["ln2_b1"], 15, 0), (gs["hg_gnorm"], 16, 0),
              (gs["sq_err"], LOSS_ROW, 0)]
    n_p = len(pieces)

    def kern(dev_ref, *refs):
        a_ref, b_ref = refs[n_p + 1], refs[n_p + 2]
        a_ref[...] = jnp.zeros(a_ref.shape, F32)
        for ref, (_, r, l0) in zip(refs[:n_p], pieces):
            a_ref[r:r + ref.shape[0], l0:l0 + ref.shape[1]] = ref[...]
        b_ref[...] = refs[n_p][...]

    whole = lambda a: pl.BlockSpec(a.shape, functools.partial(lambda i, dev, nd: (0,) * nd, nd=a.ndim))
    return pl.pallas_call(
        kern, name="small_grads_pack",
        out_shape=[pltpu.HBM((N_DEV, SMALL_BUF_ROWS, D_MODEL), F32), pltpu.HBM((N_DEV, SGU_G, 128, 128), F32)],
        grid_spec=pltpu.PrefetchScalarGridSpec(
            num_scalar_prefetch=1, grid=(1,), in_specs=[whole(p[0]) for p in pieces] + [whole(gs["sgu_w"])],
            out_specs=[pl.BlockSpec((None, SMALL_BUF_ROWS, D_MODEL), lambda i, dev: (dev[0], 0, 0)),
                       pl.BlockSpec((None, SGU_G, 128, 128), lambda i, dev: (dev[0], 0, 0, 0))]),
    )(dev, *[p[0] for p in pieces], gs["sgu_w"])


def _small_copies(src_refs, land_refs, send_sems, recv_sems):
    px, py, pc = _me()
    me = 4 * px + 2 * py + pc
    return [pltpu.make_async_remote_copy(
        src_ref=land_refs[k].at[me], dst_ref=land_refs[k].at[me], send_sem=send_sems.at[2 * (r - 1) + k],
        recv_sem=recv_sems.at[2 * (r - 1) + k], device_id=(px ^ (r >> 2), py ^ ((r >> 1) & 1), pc ^ (r & 1)), device_id_type=MESH)
        for r in range(1, N_DEV) for k in range(2)]


def _small_adamw(slots_a, slots_b, given):
    names = [p[0] for p in SMALL_PLACE] + ["sgu_w"]
    n_names = len(names)
    wmv = [given[pre + name] for name in names for pre in ("", "m_", "v_")]
    vmem = pl.BlockSpec(memory_space=pltpu.VMEM)

    def kern(*refs):
        sum_a, sum_b = refs[0][0], refs[1][0]
        for d in range(1, N_DEV):
            sum_a, sum_b = sum_a + refs[0][d], sum_b + refs[1][d]
        wmv_refs, out_refs = refs[2:2 + 3 * n_names], refs[2 + 3 * n_names:]
        px, py, pc = _me()
        me = 4 * px + 2 * py + pc

        def own_block(full):
            acc = full[:, 0:128]
            for b in range(1, N_DEV):
                acc = jnp.where(me == b, full[:, b * 128:(b + 1) * 128], acc)
            return acc

        for idx, name in enumerate(names):
            w_ref, m_ref, v_ref = wmv_refs[3 * idx:3 * idx + 3]
            if name == "sgu_w":
                grad = sum_b[None]
            else:
                _, r, l0, nr, nl = SMALL_PLACE[idx]
                grad = sum_a[r:r + nr, l0:l0 + nl]
                if name == "hg_gnorm":
                    grad = own_block(grad)
                if name == "sgu_b":
                    grad = grad[None]
            res = (grad, *_adamw(w_ref[...], grad, m_ref[...], v_ref[...]))
            for o_ref, val in zip(out_refs[4 * idx:4 * idx + 4], res):
                o_ref[...] = val
        out_refs[4 * n_names][...] = (0.5 / D_MODEL) * jnp.sum(sum_a[LOSS_ROW:LOSS_ROW + 1, :], axis=1, keepdims=True)

    out_shape = [jax.ShapeDtypeStruct(given[name].shape, F32) for name in names for _ in range(4)]
    out_shape.append(jax.ShapeDtypeStruct((1, 1), F32))
    res = pl.pallas_call(
        kern, name="small_adamw", out_shape=out_shape, in_specs=[vmem] * (2 + len(wmv)), out_specs=[vmem] * len(out_shape),
    )(slots_a, slots_b, *wmv)
    out = {name: res[4 * idx:4 * idx + 4] for idx, name in enumerate(names)}
    out["loss"] = res[-1].reshape(())
    return out


class _Exchange:
    def __init__(self, given):
        self.given = given
        px, py, pc = _me()
        self.core = pc.reshape(1).astype(jnp.int32)
        self.dev = (4 * px + 2 * py + pc).reshape(1).astype(jnp.int32)
        self.where = jnp.stack([2 * px + py, pc]).astype(jnp.int32)
        self.state, self.layers = {}, {}

    def start_weights(self, lands, after):
        self.weights = _split_start("weights_first_start", [], lands, 4 * len(lands), _ag_first_copies, after=after)
        self.first_token = self.weights[4]

    def weights_forward(self, after):
        send_sems, recv_sems, shards, lands, _ = self.weights
        _, lands = _split_wait("weights_first_wait", send_sems, recv_sems, shards, lands, after, _ag_first_copies)
        self.weights = _split_start("weights_second_start", [], lands, 3 * len(lands), _ag_second_copies)
        return self.weights[4]

    def weights_ready(self, after):
        send_sems, recv_sems, shards, lands, _ = self.weights
        _, got = _split_wait("weights_second_wait", send_sems, recv_sems, shards, lands, after, _ag_second_copies)
        return dict(w_in_o=got[0], w_out_o=got[1], w_ff1=[got[2], got[3]], w_ff2=[got[4], got[5]])

    def small_start(self, gs):
        self.small = _split_start("small_grads_start", [], _small_pack(gs, self.dev), 14, _small_copies)
        return self.small[4]

    def small_finish(self, after):
        send_sems, recv_sems, _, lands, _ = self.small
        _, lands = _split_wait("small_grads_wait", send_sems, recv_sems, [], lands, after, _small_copies)
        return _small_adamw(lands[0], lands[1], self.given)

    def direct_start(self, tag, grads):
        f32 = [g[0].reshape(4, 2, *g[0].shape[1:]) for g in grads]
        bf16 = [g[1].reshape(4, 2, *g[1].shape[1:]) for g in grads]
        lands = [lax.empty(b.shape[2:], F32) for b in f32] + [lax.empty((6, *b.shape[2:]), BF16) for b in f32]
        self.state[tag] = _split_start(f"grads_{tag}_start", f32 + bf16, lands, 7 * len(grads), _rs_direct_copies)
        return self.state[tag][4]

    def direct_end(self, tag, after):
        send_sems, recv_sems, srcs, lands, _ = self.state[tag]
        srcs, lands = _split_wait(f"grads_{tag}_wait", send_sems, recv_sems, srcs, lands, after, _rs_direct_copies)
        n = len(lands) // 2
        self.layers[tag] = list(zip(srcs[:n], lands[:n], lands[n:]))

    def grads_start(self, tag, grads):
        blocks = [g.reshape(4, 2, *g.shape[1:]) for g in grads]
        lands = [lax.empty((4, *b.shape[2:]), F32) for b in blocks]
        self.state[tag] = _split_start(f"grads_{tag}_sibling_start", blocks, lands, 4 * len(blocks), _rs_sibling_copies)
        return self.state[tag][4]

    def grads_middle(self, tag, after):
        send_sems, recv_sems, blocks, lands, _ = self.state[tag]
        blocks, from_sibling = _split_wait(f"grads_{tag}_sibling_wait", send_sems, recv_sems, blocks, lands, [after], _rs_sibling_copies)
        sums = [_chip_sum(f"grads_{tag}_chip_sum_{k}", b, s, self.core) for k, (b, s) in enumerate(zip(blocks, from_sibling))]
        lands = [lax.empty((3, *p.shape[1:]), BF16) for p in sums]
        self.state[tag] = (blocks, from_sibling, _split_start(f"grads_{tag}_chips_start", sums, lands, 3 * len(sums), _rs_chip_copies))
        return self.state[tag][2][4]

    def grads_end(self, tag, after):
        blocks, from_sibling, (send_sems, recv_sems, sums, lands, _) = self.state[tag]
        after = list(after) if isinstance(after, (list, tuple)) else [after]
        _, from_chips = _split_wait(f"grads_{tag}_chips_wait", send_sems, recv_sems, sums, lands, after, _rs_chip_copies)
        self.layers[tag] = list(zip(blocks, from_sibling, from_chips))


def kernel(x, positions, w_in_e, mla_gq, mla_gkv, w_qb, w_kvb, sgu_ln_g, sgu_ln_b, sgu_w, sgu_b, w_out_e, w_in_o, hg_lb, hg_gnorm, w_out_o, ln1_g, ln1_b, w_ff1, w_ff2, ln2_g, ln2_b, loss_target, m_w_in_e, m_mla_gq, m_mla_gkv, m_w_qb, m_w_kvb, m_sgu_ln_g, m_sgu_ln_b, m_sgu_w, m_sgu_b, m_w_out_e, m_w_in_o, m_hg_lb, m_hg_gnorm, m_w_out_o, m_ln1_g, m_ln1_b, m_w_ff1, m_w_ff2, m_ln2_g, m_ln2_b, v_w_in_e, v_mla_gq, v_mla_gkv, v_w_qb, v_w_kvb, v_sgu_ln_g, v_sgu_ln_b, v_sgu_w, v_sgu_b, v_w_out_e, v_w_in_o, v_hg_lb, v_hg_gnorm, v_w_out_o, v_ln1_g, v_ln1_b, v_w_ff1, v_w_ff2, v_ln2_g, v_ln2_b):
    given = dict(locals())
    for n in ("w_in_e", "m_w_in_e", "v_w_in_e"):
        given[n] = jnp.swapaxes(given[n], 1, 2)
    ex = _Exchange(given)

    names = ["w_in_e", "w_qb", "w_kvb", "w_out_e"]
    placed = _place_own([(given[n], 0, BF16) for n in names] + [(hg_gnorm.reshape(1, 1, D_MODEL // N_DEV), 0, F32)]
                        + [(w_in_o, 0, BF16), (w_out_o, 0, BF16), (w_ff1, 0, BF16), (w_ff1, 1, BF16), (w_ff2, 0, BF16), (w_ff2, 1, BF16)],
                        ex.dev)
    got = _all_gather(placed[:5])
    ex.start_weights(placed[5:], after=[got[0]])
    gw = dict(zip(names, got[:4]))
    small_names = ["mla_gq", "mla_gkv", "sgu_ln_g", "sgu_ln_b", "sgu_w", "sgu_b", "hg_lb", "ln1_g", "ln1_b", "ln2_g", "ln2_b"]
    sp = {n: given[n] for n in small_names}
    sp["hg_gnorm"] = got[4].reshape(1, D_MODEL)

    _, dx, grads, gs = _local_step(x[0], positions[0], loss_target[0], gw, sp, ex)

    def finish(n, layers, deps=()):
        return _finish_sharded(f"finish_{n}", layers, given[n], given["m_" + n], given["v_" + n], ex.where, deps=deps)

    ex.direct_end("l1", after=[dx])
    ex.direct_end("l0m", after=[dx])
    l1, l0m = ex.layers["l1"], ex.layers["l0m"]
    results = {}
    token = ex.grads_start("l0s", [grads[n] for n in names])
    results["w_ff1"] = finish("w_ff1", [l0m[0], l1[0]], deps=[token])
    token = ex.grads_middle("l0s", after=results["w_ff1"][0])
    results["w_ff2"] = finish("w_ff2", [l0m[1], l1[1]], deps=[token])
    results["w_in_o"] = finish("w_in_o", [l1[2]], deps=[token])
    results["w_out_o"] = finish("w_out_o", [l1[3]], deps=[token])
    results.update(ex.small_finish(after=[results["w_in_o"][0]]))
    ex.grads_end("l0s", after=[results[n][0] for n in ("mla_gq", "w_ff2", "w_in_o", "w_out_o")])
    for n, layer in zip(names, ex.layers["l0s"]):
        results[n] = finish(n, [layer])
    results["w_in_e"] = [jnp.swapaxes(a, 1, 2) for a in results["w_in_e"]]

    order = ["w_in_e", "mla_gq", "mla_gkv", "w_qb", "w_kvb", "sgu_ln_g", "sgu_ln_b", "sgu_w", "sgu_b", "w_out_e", "w_in_o",
             "hg_lb", "hg_gnorm", "w_out_o", "ln1_g", "ln1_b", "w_ff1", "w_ff2", "ln2_g", "ln2_b"]
    return (results["loss"], dx[None], *[results[name][kind] for kind in range(4) for name in order])
```

```python
import functools
import math

import jax
import jax.numpy as jnp
import numpy as np
from jax import lax
from jax.experimental import pallas as pl
from jax.experimental.pallas import tpu as pltpu

F32 = jnp.float32
BF16 = jnp.bfloat16
MESH = pl.DeviceIdType.MESH
HIGHEST = lax.Precision.HIGHEST

D_MODEL = 1024
D_FF = 4096
N_DEV = 8
HEADS = 8
HEAD_W = 128
MLA_NOPE = 64
MLA_ROPE = 32
MLA_V = 64
MLA_LORA = 256
MLA_SCALE = (MLA_NOPE + MLA_ROPE) ** -0.5
ROPE_BASE = 10000.0
SGU_DIM = 512
SGU_G = 4
SGU_CHUNK = 128
HG_CHUNK = 64
HG_CHUNKS_PER_STEP = 4
ALPHA = (2 * 2) ** 0.25
EPS = 1e-5
ADAM_LR, ADAM_B1, ADAM_B2, ADAM_EPS, ADAM_WD, ADAM_STEP = 0.001, 0.9, 0.999, 1e-08, 0.01, 10

VMEM_CAP_V7X = 56 * 2**20
VMEM_SLACK = 12 * 2**20
TM = 512
TN = 512


def _vmem(block_bytes):
    return int(min(VMEM_CAP_V7X, 2 * block_bytes + VMEM_SLACK))


def _hbm(a):
    return pltpu.with_memory_space_constraint(a, pltpu.HBM)


def _nbytes(shape, dtype):
    return int(np.prod([d for d in shape if d is not None])) * jnp.dtype(dtype).itemsize


def _sig(x):
    return 1.0 / (1.0 + jnp.exp(-x))


def _gelu(x):
    c = math.sqrt(2.0 / math.pi)
    t = jnp.tanh(c * (x + 0.044715 * x * x * x))
    return 0.5 * x * (1.0 + t), t


def _gelu_grad(x, t):
    c = math.sqrt(2.0 / math.pi)
    return 0.5 * (1.0 + t) + 0.5 * x * (1.0 - t * t) * c * (1.0 + 3 * 0.044715 * x * x)


def _dot(a, b, dims, precision=None):
    return lax.dot_general(a, b, (dims, ((), ())), preferred_element_type=F32, precision=precision)


NN = ((1,), (0,))
NT = ((1,), (1,))
TN_ = ((0,), (0,))


def _deps(deps):
    return [d for d in deps if d is not None]


def _tiled(name, grid, ins, outs, compute, direct=False, deps=()):
    n_in, deps = len(ins), _deps(deps)
    n_skip = n_in + len(deps)

    def kern(*refs):
        if direct:
            compute(refs[:n_in], refs[n_skip:])
            return
        for o_ref, r in zip(refs[n_skip:], compute(*refs[:n_in])):
            o_ref[...] = r.astype(o_ref.dtype).reshape(o_ref.shape)

    swap = lambda f: (lambda j, i: f(i, j))
    nbytes = sum(_nbytes(blk, a.dtype) for a, blk, _ in ins) + sum(_nbytes(blk, dt) + _nbytes(blk, F32) for _, dt, blk, _ in outs)
    res = pl.pallas_call(
        kern, name=name, grid=grid,
        in_specs=[pl.BlockSpec(blk, swap(f), pipeline_mode=pl.Buffered(1) if tuple(blk) == tuple(a.shape) else None)
                  for a, blk, f in ins] + [ANY_SPEC] * len(deps),
        out_specs=[pl.BlockSpec(blk, swap(f)) for _, _, blk, f in outs],
        out_shape=[pltpu.HBM(shape, dt) for shape, dt, _, _ in outs],
        compiler_params=pltpu.CompilerParams(dimension_semantics=("parallel", "parallel"), vmem_limit_bytes=_vmem(nbytes)),
    )(*[_hbm(a) for a, _, _ in ins], *deps)
    return res if len(res) > 1 else res[0]


def _rb(a, tm, w=None, cb=0):
    return (a, (tm, a.shape[1] if w is None else w), lambda i, j: (i, cb))


def _cw(b, tn):
    return (b, (b.shape[0], tn), lambda i, j: (0, j))


def _tl(a, tm):
    return (a, (a.shape[0], tm), lambda i, j: (0, i))


def _out(m, n, dtype, tm, tn):
    return ((m, n), dtype, (tm, tn), lambda i, j: (i, j))


def _out_dev(k, n, tm, dtype=F32):
    return ((N_DEV, k, n), dtype, (None, tm, n), lambda i, j: (j, i, 0))


def _twice(acc):
    return acc, acc


def _mmc(dims, n_pairs=1, epilogue=None):
    def compute(*refs):
        acc = None
        for k in range(n_pairs):
            d = _dot(refs[2 * k][...].astype(BF16), refs[2 * k + 1][...].astype(BF16), dims)
            acc = d if acc is None else acc + d
        ext = [r[...] for r in refs[2 * n_pairs:]]
        return epilogue(acc, *ext) if epilogue is not None else (acc,)

    return compute


def _res(w):
    return (w, w.shape, functools.partial(lambda i, j, nd: (0,) * nd, nd=w.ndim))


def _mmc_blocks(nblk, dims, rhs_block, epilogue=None):
    def compute(in_refs, out_refs):
        a = in_refs[0][...].astype(BF16)
        for d in range(nblk):
            acc = _dot(a, rhs_block(in_refs[1], d).astype(BF16), dims)
            n = acc.shape[1]
            ext = [r[:, d * n:(d + 1) * n] for r in in_refs[2:]]
            res = epilogue(acc, *ext) if epilogue is not None else (acc,)
            for o_ref, r in zip(out_refs, res):
                o_ref[:, d * n:(d + 1) * n] = r.astype(o_ref.dtype)

    return compute


def _rowwise(name, body, rows, consts, out_rows, out_accs=(), tr=512, deps=()):
    T = rows[0][0].shape[0]
    tr = min(tr, T)
    deps = _deps(deps)
    nr, ncn, no, nd = len(rows), len(consts), len(out_rows), len(deps)

    def kern(*refs):
        accs = refs[nr + ncn + nd + no:]
        if accs:
            @pl.when(pl.program_id(0) == 0)
            def _():
                for a in accs:
                    a[...] = jnp.zeros(a.shape, a.dtype)
        body(refs[:nr], refs[nr:nr + ncn], refs[nr + ncn + nd:nr + ncn + nd + no], accs)

    in_specs = [pl.BlockSpec((tr, w), functools.partial(lambda i, cb: (i, cb), cb=cb)) for _, w, cb in rows]
    in_specs += [pl.BlockSpec(c.shape, functools.partial(lambda i, nd: (0,) * nd, nd=c.ndim), pipeline_mode=pl.Buffered(1))
                 for c in consts]
    in_specs += [ANY_SPEC] * nd
    out_specs = [pl.BlockSpec((tr, w), lambda i: (i, 0)) for w, _ in out_rows]
    out_specs += [pl.BlockSpec(s, functools.partial(lambda i, nd: (0,) * nd, nd=len(s))) for s, _ in out_accs]
    out_shape = [pltpu.HBM((T, w), dt) for w, dt in out_rows]
    out_shape += [pltpu.HBM(s, dt) for s, dt in out_accs]
    nbytes = sum(_nbytes((tr, w), a.dtype) for a, w, _ in rows) + sum(_nbytes(c.shape, c.dtype) for c in consts)
    nbytes += sum(_nbytes((tr, w), dt) for w, dt in out_rows) + sum(_nbytes(s, dt) for s, dt in out_accs)
    res = pl.pallas_call(
        kern, name=name, grid=(T // tr,), in_specs=in_specs, out_specs=out_specs, out_shape=out_shape,
        compiler_params=pltpu.CompilerParams(dimension_semantics=("arbitrary",), vmem_limit_bytes=_vmem(nbytes)),
    )(*[_hbm(a) for a, _, _ in rows], *[_hbm(c) for c in consts], *deps)
    return res if len(res) > 1 else res[0]


def _full(a):
    return (a, a.shape[1], 0)


def _ln_stats(y):
    mu = jnp.mean(y, axis=-1, keepdims=True)
    yc = y - mu
    r = lax.rsqrt(jnp.mean(yc * yc, axis=-1, keepdims=True) + EPS)
    return yc * r, r


def _row_halves(n):
    return [slice(0, n // 2), slice(n // 2, n)] if n >= 256 else [slice(0, n)]


def _ln_back(dh, xh, r, gain, dg_ref, db_ref):
    dg_ref[...] += jnp.sum(dh * xh, axis=0, keepdims=True)
    db_ref[...] += jnp.sum(dh, axis=0, keepdims=True)
    dx = dh * gain
    return r * (dx - jnp.mean(dx, axis=-1, keepdims=True) - xh * jnp.mean(dx * xh, axis=-1, keepdims=True))


def _proj_ln(name, acts, weights, h_in, g, b, layer, deps=()):
    n = len(acts)

    def body(rows, consts, outs, accs):
        acc = None
        for k in range(n):
            d = _dot(rows[k][...].astype(BF16), consts[k][...], NN)
            acc = d if acc is None else acc + d
        y = ALPHA * rows[n][...] + acc
        xh, _ = _ln_stats(y)
        h = xh * consts[n][layer:layer + 1, :] + consts[n + 1][layer:layer + 1, :]
        outs[0][...] = y
        outs[1][...] = h
        outs[2][...] = h.astype(BF16)

    return _rowwise(name, body, [_full(a) for a in acts] + [_full(h_in)], [*weights, g, b],
                    [(D_MODEL, F32), (D_MODEL, F32), (D_MODEL, BF16)], tr=TM, deps=deps)


def _proj_ln_loss(name, act, w2, h_in, g, b, layer, target):
    def body(rows, consts, outs, accs):
        y = ALPHA * rows[1][...] + _dot(rows[0][...], consts[0][...], NN)
        xh, r = _ln_stats(y)
        gain = consts[1][layer:layer + 1, :]
        err = xh * gain + consts[2][layer:layer + 1, :] - rows[2][...]
        accs[0][...] += jnp.sum(err * err, axis=0, keepdims=True)
        dy = _ln_back(err * (1.0 / D_MODEL), xh, r, gain, accs[1], accs[2])
        outs[0][...] = dy
        outs[1][...] = dy.astype(BF16)

    return _rowwise(name, body, [_full(act), _full(h_in), _full(target)], [w2, g, b], [(D_MODEL, F32), (D_MODEL, BF16)],
                    [((1, D_MODEL), F32)] * 3, tr=TM)


def _dh_ln_back(name, da, w, dy_next, y, g, layer, proj=(), deps=()):
    def body(rows, consts, outs, accs):
        n = consts[0].shape[2]
        for sl in _row_halves(rows[0].shape[0]):
            acc = ALPHA * rows[1][sl, :]
            for d in range(N_DEV):
                acc = acc + _dot(rows[0][sl, d * n:(d + 1) * n], consts[0][d], NT)
            xh, r = _ln_stats(rows[2][sl, :])
            dy = _ln_back(acc, xh, r, consts[1][layer:layer + 1, :], accs[0], accs[1])
            outs[0][sl, :] = dy
            dy_bf = dy.astype(BF16)
            outs[1][sl, :] = dy_bf
            off = 0
            for k, p in enumerate(proj):
                outs[2][sl, off:off + p.shape[0]] = _dot(dy_bf, consts[2 + k][...], NT).astype(BF16)
                off += p.shape[0]

    out_rows = [(D_MODEL, F32), (D_MODEL, BF16)] + ([(sum(p.shape[0] for p in proj), BF16)] if proj else [])
    return _rowwise(name, body, [_full(da), _full(dy_next), _full(y)], [w, g, *proj], out_rows,
                    [((1, D_MODEL), F32)] * 2, tr=TM, deps=deps)


def _relu2_epilogue(acc):
    a = jnp.maximum(acc, 0.0)
    return acc, a * a


def _mlp_up(tag, h_bf, w1):
    T = h_bf.shape[0]
    tm = min(TM, T)
    return _tiled(f"{tag}_ff1", (1, T // tm), [_rb(h_bf, tm), _res(w1)],
                  [_out(T, D_FF, BF16, tm, D_FF), _out(T, D_FF, BF16, tm, D_FF)],
                  _mmc_blocks(N_DEV, NN, lambda w, d: w[d], epilogue=_relu2_epilogue), direct=True)


def _mlp_bwd_w(tag, h_bf, a, act, dff_bf, w2, deps=()):
    T = h_bf.shape[0]
    tm = min(TM, T)
    da = _tiled(f"{tag}_dact", (1, T // tm), [_rb(dff_bf, tm), _res(w2), _rb(a, tm)], [_out(T, D_FF, BF16, tm, D_FF)],
                _mmc_blocks(N_DEV, NT, lambda w, d: w[d], epilogue=lambda acc, a_t: (acc * 2.0 * jnp.maximum(a_t.astype(F32), 0.0),)),
                direct=True, deps=deps)
    dw2 = _tiled(f"{tag}_dw2", (1, D_FF // TM), [_tl(act, TM), _res(dff_bf)],
                 [_out(D_FF, D_MODEL, F32, TM, D_MODEL), _out(D_FF, D_MODEL, BF16, TM, D_MODEL)], _mmc(TN_, epilogue=_twice))
    dw1 = _tiled(f"{tag}_dw1", (N_DEV, 1), [_res(h_bf), _cw(da, TN)],
                 [_out_dev(D_MODEL, TN, D_MODEL), _out_dev(D_MODEL, TN, D_MODEL, BF16)], _mmc(TN_, epilogue=_twice))
    return da, dw1, [a.reshape(N_DEV, D_FF // N_DEV, D_MODEL) for a in dw2]


def _rope_tables(positions_col, inv_freq):
    T, half = positions_col.shape[0], MLA_ROPE // 2
    groups = HEAD_W // half
    n = T // groups

    def spread(a, g, first_lane):
        shift = (first_lane - half * g) % HEAD_W
        return pltpu.roll(a, shift, 1) if shift else a

    def body(rows, consts, outs, accs):
        lane = lax.broadcasted_iota(jnp.int32, (n, HEAD_W), 1)
        pos = jnp.zeros((n, HEAD_W), F32)
        for g in range(groups):
            pos = jnp.where(lane // half == g, rows[0][g * n:(g + 1) * n, :].astype(F32), pos)
        ang = pos * consts[0][...]
        c, s = jnp.cos(ang), jnp.sin(ang)
        for g in range(groups):
            r = slice(g * n, (g + 1) * n)
            outs[0][r, :] = jnp.where(lane < 64, 1.0, jnp.where(lane < 80, spread(c, g, 64), jnp.where(lane < 96, spread(c, g, 80), 0.0)))
            outs[1][r, :] = jnp.where((lane >= 64) & (lane < 80), -spread(s, g, 64), 0.0)
            outs[2][r, :] = jnp.where((lane >= 80) & (lane < 96), spread(s, g, 80), 0.0)

    return _rowwise("rope_tables", body, [_full(positions_col)], [jnp.tile(inv_freq, groups).reshape(1, HEAD_W)],
                    [(HEAD_W, F32)] * 3, tr=T)


def _rope(x, c, s1, s2):
    return x * c + pltpu.roll(x, 112, 1) * s1 + pltpu.roll(x, 16, 1) * s2


def _rope_t(dx, c, s1, s2):
    return dx * c + pltpu.roll(dx * s1, 16, 1) + pltpu.roll(dx * s2, 112, 1)


def _rms(c):
    r = lax.rsqrt(jnp.mean(c * c, axis=-1, keepdims=True) + EPS)
    return c * r, r


def _rope_heads(x, c, s1, s2, fn):
    return jnp.concatenate([fn(x[:, h * HEAD_W:(h + 1) * HEAD_W], c, s1, s2) for h in range(HEADS)], axis=1)


def _rope_key_rows(w_ref):
    return jnp.concatenate([jnp.zeros((64, D_MODEL), BF16), w_ref[512:544, :], jnp.zeros((32, D_MODEL), BF16)], axis=0)


def _mla_in(x, w, tabs, gq, gkv, deps=()):
    def body(rows, consts, outs, accs):
        xb = rows[0][...].astype(BF16)
        w_ref = consts[0]
        zc = _dot(xb, w_ref[0:512, :], NT)
        zr = _dot(xb, _rope_key_rows(w_ref), NT)
        outs[0][:, 0:512] = zc
        outs[0][:, 512:640] = zr
        outs[1][...] = _dot(xb, w_ref[544:1568, :], NT)
        outs[2][...] = (_rms(zc[:, 0:256])[0] * consts[1][...]).astype(BF16)
        outs[3][...] = (_rms(zc[:, 256:512])[0] * consts[2][...]).astype(BF16)
        outs[4][...] = _rope(zr, rows[1][...], rows[2][...], rows[3][...])

    return _rowwise("l0_in", body, [_full(x)] + [_full(t) for t in tabs], [w, gq, gkv],
                    [(640, F32), (1024, F32), (256, BF16), (256, BF16), (HEAD_W, F32)], deps=deps)


def _mla_qkv(cqn, ckvn, kr_rot, tabs, wq, wk, wv):
    def body(rows, consts, outs, accs):
        c, s1, s2 = rows[3][...], rows[4][...], rows[5][...]
        outs[0][...] = _rope_heads(_dot(rows[0][...], consts[0][...], NN), c, s1, s2, _rope).astype(BF16)
        outs[1][...] = (_dot(rows[1][...], consts[1][...], NN) + jnp.concatenate([rows[2][...]] * HEADS, axis=1)).astype(BF16)
        outs[2][...] = _dot(rows[1][...], consts[2][...], NN).astype(BF16)

    rows = [_full(cqn), _full(ckvn), _full(kr_rot)] + [_full(t) for t in tabs]
    return _rowwise("l0_qkv", body, rows, [wq, wk, wv], [(HEADS * HEAD_W, BF16)] * 3)


def _mla_back(zm, cqn, ckvn, tabs, gq, gkv, wq, wk, wv, dq, dk, dv):
    def body(rows, consts, outs, accs):
        c, s1, s2 = rows[4][...], rows[5][...], rows[6][...]
        dk_t, dv_bf = rows[8][...], rows[9][...].astype(BF16)
        dq_bf = _rope_heads(rows[7][...], c, s1, s2, _rope_t).astype(BF16)
        dk_bf = dk_t.astype(BF16)
        accs[0][...] += _dot(rows[2][...], dq_bf, TN_)
        accs[1][...] += _dot(rows[3][...], dk_bf, TN_)
        accs[2][...] += _dot(rows[3][...], dv_bf, TN_)
        dlat = [_dot(dq_bf, consts[2][...], NT), _dot(dk_bf, consts[3][...], NT) + _dot(dv_bf, consts[4][...], NT)]
        for k in range(2):
            ch, r = _rms(rows[k][...])
            accs[3 + k][...] += jnp.sum(dlat[k] * ch, axis=0, keepdims=True)
            dc = dlat[k] * consts[k][...]
            outs[0][:, 256 * k:256 * (k + 1)] = (r * (dc - ch * jnp.mean(dc * ch, axis=-1, keepdims=True))).astype(BF16)
        dks = dk_t[:, 0:HEAD_W]
        for h in range(1, HEADS):
            dks = dks + dk_t[:, h * HEAD_W:(h + 1) * HEAD_W]
        lane = lax.broadcasted_iota(jnp.int32, dks.shape, 1)
        dks = jnp.where((lane >= 64) & (lane < 96), dks, 0.0)
        outs[0][:, 512:640] = _rope_t(dks, c, s1, s2).astype(BF16)

    rows = [(zm, 256, 0), (zm, 256, 1), _full(cqn), _full(ckvn)] + [_full(t) for t in tabs] + [_full(dq), _full(dk), _full(dv)]
    wide = HEADS * HEAD_W
    return _rowwise("l0_mla_back", body, rows, [gq, gkv, wq, wk, wv], [(640, BF16)],
                    [((MLA_LORA, wide), F32)] * 3 + [((1, MLA_LORA), F32)] * 2, tr=256)


def _in_back(x, dzm, dzs, dy, w, deps=()):
    def body(rows, consts, outs, accs):
        w_ref = consts[0]
        dzm_t, dzs_t = rows[1][...], rows[2][...]
        outs[0][...] = (_dot(dzm_t[:, 0:512], w_ref[0:512, :], NN) + _dot(dzm_t[:, 512:640], _rope_key_rows(w_ref), NN)
                        + _dot(dzs_t, w_ref[544:1568, :], NN) + ALPHA * rows[3][...])
        xb = rows[0][...].astype(BF16)
        gm = _dot(dzm_t, xb, TN_)
        accs[0][0:512, :] += gm[0:512]
        accs[0][512:544, :] += gm[512 + 64:512 + 96]
        accs[0][544:1568, :] += _dot(dzs_t, xb, TN_)

    return _rowwise("l0_in_back", body, [_full(x), _full(dzm), _full(dzs), _full(dy)], [w], [(D_MODEL, F32)],
                    [((1568, D_MODEL), F32)], deps=deps)


def _out_weight_grads(o_att, b_out, dy_bf):
    def body(rows, consts, outs, accs):
        d = rows[2][...]
        accs[0][...] += _dot(rows[0][...].astype(BF16), d, TN_)
        accs[1][...] += _dot(rows[1][...], d, TN_)

    return _rowwise("l0_dw_out", body, [_full(o_att), _full(b_out), _full(dy_bf)], [], [],
                    [((HEADS * HEAD_W, D_MODEL), F32), ((SGU_DIM, D_MODEL), F32)])


def _attn_block(T):
    return min(1024, T)


def _attn_fwd(q, k, v):
    T = q.shape[0]
    BQ = _attn_block(T)
    nq = T // BQ

    def kern(q_ref, k_ref, v_ref, o_ref, lse_ref):
        def step(i, j, carry, masked):
            m, l, acc = carry
            qb = q_ref[pl.ds(pl.multiple_of(i * BQ, BQ), BQ), :]
            kb = k_ref[pl.ds(pl.multiple_of(j * BQ, BQ), BQ), :]
            vb = v_ref[pl.ds(pl.multiple_of(j * BQ, BQ), BQ), :]
            s = _dot(qb, kb, NT) * MLA_SCALE
            if masked:
                row = lax.broadcasted_iota(jnp.int32, s.shape, 0)
                col = lax.broadcasted_iota(jnp.int32, s.shape, 1)
                s = jnp.where(col <= row, s, -1e30)
            m_new = jnp.maximum(m, jnp.max(s, axis=-1, keepdims=True))
            p = jnp.exp(s - m_new)
            a = jnp.exp(m - m_new)
            l = a * l + jnp.sum(p, axis=-1, keepdims=True)
            acc = a * acc + _dot(p.astype(BF16), vb, NN)
            return m_new, l, acc

        def qloop(i, _):
            init = (jnp.full((BQ, 1), -1e30, F32), jnp.zeros((BQ, 1), F32), jnp.zeros((BQ, HEAD_W), F32))
            carry = lax.fori_loop(0, i, lambda j, c: step(i, j, c, False), init)
            m, l, acc = step(i, i, carry, True)
            rows = pl.ds(pl.multiple_of(i * BQ, BQ), BQ)
            o_ref[rows, :] = acc / l
            lse_ref[0, rows, :] = m + jnp.log(l)
            return 0

        lax.fori_loop(0, nq, qloop, 0)

    head = pl.BlockSpec((T, HEAD_W), lambda h: (0, h))
    nbytes = 3 * _nbytes((T, HEAD_W), BF16) + _nbytes((T, HEAD_W), F32) + _nbytes((T, 128), F32)
    return pl.pallas_call(
        kern, name="attn_fwd", grid=(HEADS,), in_specs=[head, head, head],
        out_specs=[head, pl.BlockSpec((1, T, 1), lambda h: (h, 0, 0))],
        out_shape=[pltpu.HBM((T, HEADS * HEAD_W), F32), pltpu.HBM((HEADS, T, 1), F32)],
        compiler_params=pltpu.CompilerParams(dimension_semantics=("parallel",), vmem_limit_bytes=_vmem(nbytes)),
    )(_hbm(q), _hbm(k), _hbm(v))


def _attn_bwd(q, k, v, o, lse, dcat, deps=()):
    T = q.shape[0]
    BQ = _attn_block(T)
    nq = T // BQ
    deps = _deps(deps)

    def kern(q_ref, k_ref, v_ref, o_ref, lse_ref, do_ref, *rest):
        dq_ref, dk_ref, dv_ref, dd_ref = rest[len(deps):]
        dq_ref[...] = jnp.zeros(dq_ref.shape, F32)

        def dloop(i, _):
            rows = pl.ds(pl.multiple_of(i * BQ, BQ), BQ)
            dd_ref[rows, :] = jnp.sum(do_ref[rows, :].astype(F32) * o_ref[rows, :], axis=-1, keepdims=True)
            return 0

        lax.fori_loop(0, nq, dloop, 0)

        def tile(q0, k0, n, carry, masked):
            dk_acc, dv_acc = carry
            rq = pl.ds(pl.multiple_of(q0, n), n)
            rk = pl.ds(pl.multiple_of(k0, n), n)
            qb, kb, vb, dob = q_ref[rq, :], k_ref[rk, :], v_ref[rk, :], do_ref[rq, :]
            s = _dot(qb, kb, NT) * MLA_SCALE
            p = jnp.exp(s - lse_ref[0, rq, :])
            if masked:
                row = lax.broadcasted_iota(jnp.int32, s.shape, 0)
                col = lax.broadcasted_iota(jnp.int32, s.shape, 1)
                p = jnp.where(col <= row, p, 0.0)
            dp = _dot(dob, vb, NT)
            ds = (p * (dp - dd_ref[rq, :]) * MLA_SCALE).astype(BF16)
            dv_acc = dv_acc + _dot(p.astype(BF16), dob, TN_)
            dk_acc = dk_acc + _dot(ds, qb, TN_)
            dq_ref[rq, :] += _dot(ds, kb, NN)
            return dk_acc, dv_acc

        def kloop(j, _):
            base, half = j * BQ, BQ // 2
            zero = (jnp.zeros((half, HEAD_W), F32), jnp.zeros((half, HEAD_W), F32))
            early = tile(base + half, base, half, tile(base, base, half, zero, True), False)
            late = tile(base + half, base + half, half, zero, True)
            carry = tuple(jnp.concatenate([a, b], axis=0) for a, b in zip(early, late))
            dk_acc, dv_acc = lax.fori_loop(j + 1, nq, lambda i, c: tile(i * BQ, base, BQ, c, False), carry)
            rk = pl.ds(pl.multiple_of(j * BQ, BQ), BQ)
            dk_ref[rk, :] = dk_acc
            dv_ref[rk, :] = dv_acc
            return 0

        lax.fori_loop(0, nq, kloop, 0)

    head = pl.BlockSpec((T, HEAD_W), lambda h: (0, h))
    nbytes = 4 * _nbytes((T, HEAD_W), BF16) + 5 * _nbytes((T, HEAD_W), F32) + 2 * _nbytes((T, 128), F32)
    return pl.pallas_call(
        kern, name="attn_bwd", grid=(HEADS,),
        in_specs=[head, head, head, head, pl.BlockSpec((1, T, 1), lambda h: (h, 0, 0)), head] + [ANY_SPEC] * len(deps),
        out_specs=[head, head, head],
        out_shape=[pltpu.HBM((T, HEADS * HEAD_W), F32)] * 3,
        scratch_shapes=[pltpu.VMEM((T, 1), F32)],
        compiler_params=pltpu.CompilerParams(dimension_semantics=("parallel",), vmem_limit_bytes=_vmem(nbytes)),
    )(*[_hbm(a) for a in (q, k, v, o, lse, dcat)], *deps)


def _sgu_common(u, v, ln_g, ln_b):
    ua, tu = _gelu(u)
    va, tv = _gelu(v)
    vh, r = _ln_stats(va)
    return ua, tu, tv, vh, r, vh * ln_g + ln_b


def _tril_mask(n):
    return lax.broadcasted_iota(jnp.int32, (n, n), 1) <= lax.broadcasted_iota(jnp.int32, (n, n), 0)


def _sgu_fwd(zs, ln_g, ln_b, w, bias_full):
    def body(rows, consts, outs, accs):
        ua, _, _, _, _, vn = _sgu_common(rows[0][...], rows[1][...], consts[0][...], consts[1][...])
        vn = vn.astype(BF16)
        tri = _tril_mask(SGU_CHUNK)
        for g in range(SGU_G):
            wg = jnp.where(tri, consts[2][0, g], 0.0).astype(BF16)
            cols = slice(g * 128, (g + 1) * 128)
            for c in range(ua.shape[0] // SGU_CHUNK):
                rws = slice(c * SGU_CHUNK, (c + 1) * SGU_CHUNK)
                mixed = _dot(wg, vn[rws, cols], NN) + consts[3][:, cols]
                outs[0][rws, cols] = (ua[rws, cols] * mixed).astype(BF16)

    return _rowwise("sgu_fwd", body, [(zs, 512, 0), (zs, 512, 1)], [ln_g, ln_b, w, bias_full], [(SGU_DIM, BF16)])


def _sgu_bwd(zs, dcat, ln_g, ln_b, w, bias_full):
    def body(rows, consts, outs, accs):
        u, v = rows[0][...], rows[1][...]
        ua, tu, tv, vh, r, vn = _sgu_common(u, v, consts[0][...], consts[1][...])
        dout = rows[2][...].astype(F32)
        vn_bf = vn.astype(BF16)
        tri = _tril_mask(SGU_CHUNK)
        dmixed = (dout * ua)
        dmixed_bf = dmixed.astype(BF16)
        ones = jnp.ones((8, SGU_CHUNK), F32)
        dvn_cols, mixed_cols = [], []
        for g in range(SGU_G):
            wg = jnp.where(tri, consts[2][0, g], 0.0).astype(BF16)
            cols = slice(g * 128, (g + 1) * 128)
            dvn_rows, mixed_rows = [], []
            dw = jnp.zeros((SGU_CHUNK, SGU_CHUNK), F32)
            dmix_sum = jnp.zeros((SGU_CHUNK, 128), F32)
            for c in range(u.shape[0] // SGU_CHUNK):
                rws = slice(c * SGU_CHUNK, (c + 1) * SGU_CHUNK)
                mixed_rows.append(_dot(wg, vn_bf[rws, cols], NN) + consts[3][:, cols])
                dvn_rows.append(_dot(wg, dmixed_bf[rws, cols], TN_))
                dw = dw + _dot(dmixed_bf[rws, cols], vn_bf[rws, cols], NT)
                dmix_sum = dmix_sum + dmixed[rws, cols]
            accs[0][g] += jnp.where(tri, dw, 0.0)
            accs[3][g:g + 1, :] += _dot(ones, dmix_sum, NT, precision=HIGHEST)[0:1, :]
            dvn_cols.append(jnp.concatenate(dvn_rows, axis=0))
            mixed_cols.append(jnp.concatenate(mixed_rows, axis=0))
        dvn = jnp.concatenate(dvn_cols, axis=1)
        mixed = jnp.concatenate(mixed_cols, axis=1)
        accs[1][...] += jnp.sum(dvn * vh, axis=0, keepdims=True)
        accs[2][...] += jnp.sum(dvn, axis=0, keepdims=True)
        dvh = dvn * consts[0][...]
        dva = r * (dvh - jnp.mean(dvh, axis=-1, keepdims=True) - vh * jnp.mean(dvh * vh, axis=-1, keepdims=True))
        outs[0][:, 0:512] = (dout * mixed * _gelu_grad(u, tu)).astype(BF16)
        outs[0][:, 512:1024] = (dva * _gelu_grad(v, tv)).astype(BF16)

    return _rowwise("sgu_bwd", body, [(zs, 512, 0), (zs, 512, 1), (dcat, 512, 2)], [ln_g, ln_b, w, bias_full], [(1024, BF16)],
                    [((SGU_G, 128, 128), F32), ((1, SGU_DIM), F32), ((1, SGU_DIM), F32), ((SGU_G, 128), F32)], tr=256)


def _lower_bound(hg_lb):
    a0, a1 = hg_lb[0:1, :], hg_lb[1:2, :]
    m = jnp.maximum(a0, a1)
    e0, e1 = jnp.exp(a0 - m), jnp.exp(a1 - m)
    s0, s1 = e0 / (e0 + e1), e1 / (e0 + e1)
    return (s0 + s1) - s0, s0, s1


def _prefix_rows(x, reverse=False):
    n = x.shape[0]
    row = lax.broadcasted_iota(jnp.int32, x.shape, 0)
    s = 1
    while s < n:
        if reverse:
            x = x + jnp.where(row < n - s, pltpu.roll(x, n - s, 0), 0.0)
        else:
            x = x + jnp.where(row >= s, pltpu.roll(x, s, 0), 0.0)
        s *= 2
    return x


def _hg_gates(qr, fr, lb):
    C = qr.shape[0]
    sq = _sig(qr)
    qf = qr * sq
    sf = _sig(fr)
    gate = lb + (1.0 - lb) * sf
    kk = 1.0 - gate
    tri = _tril_mask(C)
    b = _prefix_rows(jnp.log(gate))
    bref = b[C // 2 - 1:C // 2, :]
    bl = b[C - 1:C, :]
    e_b = jnp.exp(b)
    e_q = jnp.exp(b - bref)
    e_k = jnp.exp(bref - b)
    e_lb = jnp.exp(bl - b)
    return dict(sq=sq, qf=qf, sf=sf, gate=gate, kk=kk, tri=tri, bl=bl, e_b=e_b, e_q=e_q, e_k=e_k, e_lb=e_lb)


def _hgrn_fwd(z1, hg_lb, gnorm):
    T = z1.shape[0]
    C = min(HG_CHUNK, T)
    nc = T // C
    ns = HG_CHUNKS_PER_STEP if nc % HG_CHUNKS_PER_STEP == 0 else 1
    R = ns * C

    def kern(q_ref, f_ref, i_ref, g_ref, lb_ref, gn_ref, o_ref, hg_ref, st_ref, s_scr):
        @pl.when(pl.program_id(0) == 0)
        def _():
            s_scr[...] = jnp.zeros(s_scr.shape, F32)

        lb_all, _, _ = _lower_bound(lb_ref[...])
        for sub in range(ns):
            rows = slice(sub * C, (sub + 1) * C)
            st_ref[sub] = s_scr[...]
            for h in range(HEADS):
                cols = slice(h * HEAD_W, (h + 1) * HEAD_W)
                t = _hg_gates(q_ref[rows, cols], f_ref[rows, cols], lb_all[:, cols])
                v_bf = i_ref[rows, cols].astype(BF16)
                st = s_scr[h]
                a = jnp.where(t["tri"], _dot((t["qf"] * t["e_q"]).astype(BF16), (t["kk"] * t["e_k"]).astype(BF16), NT), 0.0)
                o = _dot(a.astype(BF16), v_bf, NN) + _dot((t["qf"] * t["e_b"]).astype(BF16), st.astype(BF16), NT)
                s_scr[h] = st * jnp.exp(t["bl"]) + _dot(v_bf, (t["kk"] * t["e_lb"]).astype(BF16), TN_)
                o_ref[rows, cols] = o
                gr = g_ref[rows, cols]
                r = lax.rsqrt(jnp.mean(o * o, axis=-1, keepdims=True) + EPS)
                hg_ref[rows, cols] = (o * r * gn_ref[:, cols] * (gr * _sig(gr))).astype(BF16)

    seg = lambda k: pl.BlockSpec((R, D_MODEL), functools.partial(lambda n, k: (n, k), k=k))
    row = pl.BlockSpec((R, D_MODEL), lambda n: (n, 0))
    nbytes = 6 * _nbytes((R, D_MODEL), F32) + (2 + ns) * _nbytes((HEADS, 128, 128), F32)
    return pl.pallas_call(
        kern, name="hgrn_fwd", grid=(nc // ns,),
        in_specs=[seg(0), seg(1), seg(2), seg(3), pl.BlockSpec((2, D_MODEL), lambda n: (0, 0)),
                  pl.BlockSpec((1, D_MODEL), lambda n: (0, 0))],
        out_specs=[row, row, pl.BlockSpec((ns, HEADS, 128, 128), lambda n: (n, 0, 0, 0))],
        out_shape=[pltpu.HBM((T, D_MODEL), F32), pltpu.HBM((T, D_MODEL), BF16),
                   pltpu.HBM((nc, HEADS, 128, 128), F32)],
        scratch_shapes=[pltpu.VMEM((HEADS, 128, 128), F32)],
        compiler_params=pltpu.CompilerParams(dimension_semantics=("arbitrary",), vmem_limit_bytes=_vmem(nbytes)),
    )(*[_hbm(a) for a in (z1, z1, z1, z1, hg_lb, gnorm)])


def _hgrn_bwd(z1, o_pre, dhg, states, hg_lb, gnorm):
    T = z1.shape[0]
    C = min(HG_CHUNK, T)
    nc = T // C
    ns = HG_CHUNKS_PER_STEP if nc % HG_CHUNKS_PER_STEP == 0 else 1
    R, steps = ns * C, nc // ns

    def kern(q_ref, f_ref, i_ref, g_ref, o_ref, dhg_ref, st_ref, lb_ref, gn_ref, dz_ref, dlb_ref, dgn_ref, ds_scr, dlb_scr):
        n = pl.program_id(0)

        @pl.when(n == 0)
        def _():
            ds_scr[...] = jnp.zeros(ds_scr.shape, F32)
            dlb_scr[...] = jnp.zeros(dlb_scr.shape, F32)
            dgn_ref[...] = jnp.zeros(dgn_ref.shape, F32)

        lb_all, s0, s1 = _lower_bound(lb_ref[...])
        for sub in reversed(range(ns)):
            rows = slice(sub * C, (sub + 1) * C)
            for h in range(HEADS):
                cols = slice(h * HEAD_W, (h + 1) * HEAD_W)
                lb = lb_all[:, cols]
                qr, fr = q_ref[rows, cols], f_ref[rows, cols]
                t = _hg_gates(qr, fr, lb)
                tri = t["tri"]
                v_bf = i_ref[rows, cols].astype(BF16)
                st_bf = st_ref[sub, h].astype(BF16)
                dst = ds_scr[h]
                dst_bf = dst.astype(BF16)
                o = o_ref[rows, cols]
                gr = g_ref[rows, cols]
                sg = _sig(gr)
                sil = gr * sg
                gn = gn_ref[:, cols]
                r = lax.rsqrt(jnp.mean(o * o, axis=-1, keepdims=True) + EPS)
                on = o * r
                dh = dhg_ref[rows, cols].astype(F32)
                dgn_ref[:, cols] += jnp.sum(dh * on * sil, axis=0, keepdims=True)
                dg = dh * on * gn * (sg * (1.0 + gr * (1.0 - sg)))
                don = dh * gn * sil
                do_bf = (r * (don - on * jnp.mean(don * on, axis=-1, keepdims=True))).astype(BF16)
                qe = (t["qf"] * t["e_q"]).astype(BF16)
                ke = (t["kk"] * t["e_k"]).astype(BF16)
                qb = (t["qf"] * t["e_b"]).astype(BF16)
                kh_bf = (t["kk"] * t["e_lb"]).astype(BF16)
                a_bf = jnp.where(tri, _dot(qe, ke, NT), 0.0).astype(BF16)
                da_bf = jnp.where(tri, _dot(do_bf, v_bf, NT), 0.0).astype(BF16)
                dv = _dot(a_bf, do_bf, TN_) + _dot(kh_bf, dst_bf, NT)
                dqe = _dot(da_bf, ke, NN)
                dqb = _dot(do_bf, st_bf, NN)
                dke = _dot(da_bf, qe, TN_)
                dkh = _dot(v_bf, dst_bf, NN)
                dqf = dqe * t["e_q"] + dqb * t["e_b"]
                dkk = dke * t["e_k"] + dkh * t["e_lb"]
                kh_r = kh_bf.astype(F32)
                db = qe.astype(F32) * dqe - ke.astype(F32) * dke + qb.astype(F32) * dqb - kh_r * dkh
                e_bl = jnp.exp(t["bl"])
                dbl = jnp.sum(dkh * kh_r, axis=0, keepdims=True) + e_bl * jnp.sum(st_ref[sub, h] * dst, axis=0, keepdims=True)
                dlg = _prefix_rows(db, reverse=True) + dbl
                ds_scr[h] = dst * e_bl + _dot(do_bf, qb, TN_)
                dgate = dlg / t["gate"] - dkk
                sf = t["sf"]
                dlb_scr[:, cols] += jnp.sum(dgate * (1.0 - sf), axis=0, keepdims=True)
                df = dgate * (1.0 - lb) * sf * (1.0 - sf)
                dq = dqf * (t["sq"] * (1.0 + qr * (1.0 - t["sq"])))
                dz_ref[rows, cols] = dq.astype(BF16)
                dz_ref[rows, D_MODEL + h * HEAD_W:D_MODEL + (h + 1) * HEAD_W] = df.astype(BF16)
                dz_ref[rows, 2 * D_MODEL + h * HEAD_W:2 * D_MODEL + (h + 1) * HEAD_W] = dv.astype(BF16)
                dz_ref[rows, 3 * D_MODEL + h * HEAD_W:3 * D_MODEL + (h + 1) * HEAD_W] = dg.astype(BF16)

        @pl.when(n == steps - 1)
        def _():
            d = s0 * s1 * dlb_scr[...]
            dlb_ref[0:1, :] = -d
            dlb_ref[1:2, :] = d

    seg = lambda k: pl.BlockSpec((R, D_MODEL), functools.partial(lambda n, k: (steps - 1 - n, k), k=k))
    nbytes = 6 * _nbytes((R, D_MODEL), F32) + _nbytes((R, 4 * D_MODEL), BF16) + (2 + ns) * _nbytes((HEADS, 128, 128), F32)
    return pl.pallas_call(
        kern, name="hgrn_bwd", grid=(steps,),
        in_specs=[seg(0), seg(1), seg(2), seg(3), seg(0), seg(0),
                  pl.BlockSpec((ns, HEADS, 128, 128), lambda n: (steps - 1 - n, 0, 0, 0)),
                  pl.BlockSpec((2, D_MODEL), lambda n: (0, 0)), pl.BlockSpec((1, D_MODEL), lambda n: (0, 0))],
        out_specs=[pl.BlockSpec((R, 4 * D_MODEL), lambda n: (steps - 1 - n, 0)),
                   pl.BlockSpec((2, D_MODEL), lambda n: (0, 0)), pl.BlockSpec((1, D_MODEL), lambda n: (0, 0))],
        out_shape=[pltpu.HBM((T, 4 * D_MODEL), BF16), pltpu.HBM((2, D_MODEL), F32),
                   pltpu.HBM((1, D_MODEL), F32)],
        scratch_shapes=[pltpu.VMEM((HEADS, 128, 128), F32), pltpu.VMEM((1, D_MODEL), F32)],
        compiler_params=pltpu.CompilerParams(dimension_semantics=("arbitrary",), vmem_limit_bytes=_vmem(nbytes)),
    )(*[_hbm(a) for a in (z1, z1, z1, z1, o_pre, dhg, states, hg_lb, gnorm)])


def _prep_weights(gw):
    w_in = gw["w_in_e"].reshape(1568, D_MODEL)
    w_qb = gw["w_qb"].transpose(1, 0, 2).reshape(MLA_LORA, HEADS, 96)
    wq = jnp.pad(w_qb, ((0, 0), (0, 0), (0, 32))).reshape(MLA_LORA, HEADS * HEAD_W)
    kvb = gw["w_kvb"].transpose(1, 0, 2).reshape(MLA_LORA, HEADS, 128)
    wk = jnp.pad(kvb[:, :, :64], ((0, 0), (0, 0), (0, 64))).reshape(MLA_LORA, HEADS * HEAD_W)
    wv = jnp.pad(kvb[:, :, 64:], ((0, 0), (0, 0), (0, 64))).reshape(MLA_LORA, HEADS * HEAD_W)
    w_out_e = gw["w_out_e"].reshape(D_MODEL, D_MODEL)
    woa = jnp.pad(w_out_e[:512].reshape(HEADS, 64, D_MODEL), ((0, 0), (0, 64), (0, 0))).reshape(HEADS * HEAD_W, D_MODEL)
    return dict(w_in=w_in, wq=wq, wk=wk, wv=wv, woa=woa, wob=w_out_e[512:])


def _unprep_grads(g):
    d_in_e = g["w_in"].reshape(N_DEV, 1568 // N_DEV, D_MODEL)
    d_qb = g["wq"].reshape(MLA_LORA, HEADS, HEAD_W)[:, :, :96].reshape(MLA_LORA, HEADS * 96)
    dk = g["wk"].reshape(MLA_LORA, HEADS, HEAD_W)[:, :, :64]
    dv = g["wv"].reshape(MLA_LORA, HEADS, HEAD_W)[:, :, :64]
    d_kvb = jnp.concatenate([dk, dv], axis=2).reshape(MLA_LORA, HEADS * 128)
    d_oa = g["woa"].reshape(HEADS, HEAD_W, D_MODEL)[:, :64].reshape(HEADS * 64, D_MODEL)
    dev_major = lambda a: a.reshape(a.shape[0], N_DEV, a.shape[1] // N_DEV).transpose(1, 0, 2)
    return dict(w_in_e=d_in_e, w_qb=dev_major(d_qb), w_kvb=dev_major(d_kvb),
                w_out_e=jnp.concatenate([d_oa, g["wob"]], axis=0).reshape(N_DEV, D_MODEL // N_DEV, D_MODEL))


def _local_step(x, positions, target, gw, sp, ex):
    w = _prep_weights(gw)
    T = x.shape[0]
    tm = min(TM, T)
    nt = T // tm
    half = MLA_ROPE // 2
    inv_freq = ROPE_BASE ** (-jnp.arange(half, dtype=F32) / half)
    tabs = _rope_tables(positions.reshape(T, 1), inv_freq)
    bias_full = jnp.repeat(sp["sgu_b"][0].T, 128, axis=1)
    sgu_w = sp["sgu_w"]
    gq, gkv = sp["mla_gq"], sp["mla_gkv"]
    ln1_g, ln1_b, ln2_g, ln2_b = sp["ln1_g"], sp["ln1_b"], sp["ln2_g"], sp["ln2_b"]
    zm, zs, cqn, ckvn, kr_rot = _mla_in(x, w["w_in"], tabs, gq, gkv, deps=[ex.first_token])
    q, k, v = _mla_qkv(cqn, ckvn, kr_rot, tabs, w["wq"], w["wk"], w["wv"])
    o_att, lse = _attn_fwd(q, k, v)
    b_out = _sgu_fwd(zs, sp["sgu_ln_g"], sp["sgu_ln_b"], sgu_w, bias_full)
    token = ex.weights_forward(after=[o_att, b_out])
    y1, h1, h1_bf = _proj_ln("l0_out_ln1", [o_att, b_out], [w["woa"], w["wob"]], x, ln1_g, ln1_b, 0, deps=[token])
    big = ex.weights_ready(after=[y1])
    w_ff1, w_in_o, w_out_o = big["w_ff1"], big["w_in_o"], big["w_out_o"].reshape(D_MODEL, D_MODEL)
    w_ff2 = [a.reshape(D_FF, D_MODEL) for a in big["w_ff2"]]
    a0, act0 = _mlp_up("l0", h1_bf, w_ff1[0])
    y2, h2, h2_bf = _proj_ln("l0_ff2_ln2", [act0], [w_ff2[0]], h1, ln2_g, ln2_b, 0)

    z1 = _tiled("l1_in", (1, nt), [_rb(h2_bf, tm), _res(w_in_o)], [_out(T, 4 * D_MODEL, F32, tm, 4 * D_MODEL)],
                _mmc_blocks(N_DEV, NN, lambda w, d: w[d]), direct=True)
    o_pre, hg, states = _hgrn_fwd(z1, sp["hg_lb"], sp["hg_gnorm"])
    y3, h3, h3_bf = _proj_ln("l1_out_ln1", [hg], [w_out_o], h2, ln1_g, ln1_b, 1)
    a1, act1 = _mlp_up("l1", h3_bf, w_ff1[1])

    gs, g0 = {}, {}
    dy4, dy4_bf, sq_err, gs["ln2_g1"], gs["ln2_b1"] = _proj_ln_loss("l1_ff2_loss", act1, w_ff2[1], h3, ln2_g, ln2_b, 1, target)
    gs["sq_err"] = sq_err
    da1, dw1_1, dw2_1 = _mlp_bwd_w("l1", h3_bf, a1, act1, dy4_bf, big["w_ff2"][1])
    dy3, dy3_bf, dhg, gs["ln1_g1"], gs["ln1_b1"] = _dh_ln_back("l1_dh_ln1", da1, w_ff1[1], dy4, y3, ln1_g, 1, proj=[w_out_o])
    d_out_o = _tiled("l1_dwout", (2, D_MODEL // TM), [_tl(hg, TM), _cw(dy3_bf, TN)],
                     [_out(D_MODEL, D_MODEL, F32, TM, TN), _out(D_MODEL, D_MODEL, BF16, TM, TN)], _mmc(TN_, epilogue=_twice))
    d_out_o = [a.reshape(N_DEV, D_MODEL // N_DEV, D_MODEL) for a in d_out_o]
    dz1, gs["hg_lb"], gs["hg_gnorm"] = _hgrn_bwd(z1, o_pre, dhg, states, sp["hg_lb"], sp["hg_gnorm"])
    d_in_o = _tiled("l1_dwin", (N_DEV, 1), [_res(h2_bf), _cw(dz1, TN)],
                    [_out_dev(D_MODEL, TN, D_MODEL), _out_dev(D_MODEL, TN, D_MODEL, BF16)], _mmc(TN_, epilogue=_twice))
    token = ex.direct_start("l1", [dw1_1, dw2_1, d_in_o, d_out_o])

    dy2, dy2_bf, gs["ln2_g0"], gs["ln2_b0"] = _dh_ln_back("l1_dh_ln2", dz1, w_in_o, dy3, y2, ln2_g, 0, deps=[token])
    da0, dw1_0, dw2_0 = _mlp_bwd_w("l0", h1_bf, a0, act0, dy2_bf, big["w_ff2"][0])
    token = ex.direct_start("l0m", [dw1_0, dw2_0])
    dy1, dy1_bf, dcat, gs["ln1_g0"], gs["ln1_b0"] = _dh_ln_back("l0_dh_ln1", da0, w_ff1[0], dy2, y1, ln1_g, 0,
                                                                 proj=[w["woa"], w["wob"]], deps=[token])
    g0["woa"], g0["wob"] = _out_weight_grads(o_att, b_out, dy1_bf)
    dzs, gs["sgu_w"], gs["sgu_ln_g"], gs["sgu_ln_b"], gs["sgu_b"] = _sgu_bwd(zs, dcat, sp["sgu_ln_g"], sp["sgu_ln_b"], sgu_w, bias_full)
    dq, dk, dv = _attn_bwd(q, k, v, o_att, lse, dcat)
    dzm, g0["wq"], g0["wk"], g0["wv"], gs["mla_gq"], gs["mla_gkv"] = _mla_back(zm, cqn, ckvn, tabs, gq, gkv, w["wq"], w["wk"], w["wv"],
                                                                                 dq, dk, dv)
    token = ex.small_start(gs)
    dx, g0["w_in"] = _in_back(x, dzm, dzs, dy1, w["w_in"], deps=[token])

    return sq_err, dx, _unprep_grads(g0), gs


def _me():
    return lax.axis_index("x"), lax.axis_index("y"), lax.axis_index("c")


ANY_SPEC = pl.BlockSpec(memory_space=pl.ANY)
HBM_SPEC = pl.BlockSpec(memory_space=pltpu.HBM)
SEM_SPEC = pl.BlockSpec(memory_space=pltpu.SEMAPHORE)
EFFECT = pltpu.SideEffectType.DATAFLOW_SIDE_EFFECTING


def _split_start(name, srcs, lands, n_sems, make_copies, after=()):
    n, m, k = len(srcs), len(lands), len(after)

    def body(*refs):
        for cp in make_copies(refs[:n], refs[n:n + m], refs[n + m + k], refs[n + m + k + 1]):
            cp.start()
        refs[-1][...] = jnp.zeros(refs[-1].shape, F32)

    out_shape = (pltpu.SemaphoreType.DMA((n_sems,)), pltpu.SemaphoreType.DMA((n_sems,)),
                 *[pltpu.HBM(a.shape, a.dtype) for a in (*srcs, *lands)], jax.ShapeDtypeStruct((8, 128), F32))
    res = pl.pallas_call(
        body, name=name, out_shape=out_shape, in_specs=[HBM_SPEC] * (n + m) + [ANY_SPEC] * k,
        out_specs=(SEM_SPEC, SEM_SPEC, *[HBM_SPEC] * (n + m), pl.BlockSpec(memory_space=pltpu.VMEM)),
        input_output_aliases={i: 2 + i for i in range(n + m)},
        compiler_params=pltpu.CompilerParams(has_side_effects=EFFECT),
    )(*[_hbm(a) for a in (*srcs, *lands)], *after)
    return res[0], res[1], list(res[2:2 + n]), list(res[2 + n:2 + n + m]), res[-1]


def _split_wait(name, send_sems, recv_sems, srcs, lands, after, make_copies):
    n, m = len(srcs), len(lands)

    def body(*refs):
        for cp in make_copies(refs[:n], refs[n:n + m], refs[n + m], refs[n + m + 1]):
            cp.wait_send()
            cp.wait_recv()

    res = pl.pallas_call(
        body, name=name, out_shape=tuple(pltpu.HBM(a.shape, a.dtype) for a in (*srcs, *lands)),
        in_specs=[HBM_SPEC] * (n + m) + [SEM_SPEC, SEM_SPEC] + [ANY_SPEC] * len(after), out_specs=tuple([HBM_SPEC] * (n + m)),
        input_output_aliases={i: i for i in range(n + m)},
        compiler_params=pltpu.CompilerParams(has_side_effects=EFFECT),
    )(*srcs, *lands, send_sems, recv_sems, *after)
    return list(res[:n]), list(res[n:])


def _place_own(shards, dev):
    n = len(shards)

    def kern(dev_ref, *refs):
        for x_ref, o_ref in zip(refs[:n], refs[n:]):
            o_ref[...] = x_ref[...].astype(o_ref.dtype)

    blocks = [(None, *a.shape[1:]) for a, _, _ in shards]
    nbytes = sum(_nbytes(b, a.dtype) + _nbytes(b, dt) for b, (a, _, dt) in zip(blocks, shards))
    return pl.pallas_call(
        kern, name="weights_place_own", out_shape=[pltpu.HBM((N_DEV, *a.shape[1:]), dt) for a, _, dt in shards],
        grid_spec=pltpu.PrefetchScalarGridSpec(
            num_scalar_prefetch=1, grid=(1,),
            in_specs=[pl.BlockSpec(b, functools.partial(lambda i, dev, l: (l, 0, 0), l=l)) for b, (_, l, _) in zip(blocks, shards)],
            out_specs=[pl.BlockSpec(b, lambda i, dev: (dev[0], 0, 0)) for b in blocks]),
        compiler_params=pltpu.CompilerParams(dimension_semantics=("arbitrary",), vmem_limit_bytes=_vmem(nbytes)),
    )(dev, *[_hbm(a) for a, _, _ in shards])


def _ag_first_copies(src_refs, out_refs, send_sems, recv_sems):
    x, y, c = _me()
    targets = [(x, y, 1 - c), (1 - x, y, c), (x, 1 - y, c), (1 - x, 1 - y, c)]
    return [pltpu.make_async_remote_copy(
        src_ref=out_refs[op].at[4 * x + 2 * y + c], dst_ref=out_refs[op].at[4 * x + 2 * y + c], send_sem=send_sems.at[4 * op + k],
        recv_sem=recv_sems.at[4 * op + k], device_id=to, device_id_type=MESH)
        for op in range(len(out_refs)) for k, to in enumerate(targets)]


def _ag_second_copies(src_refs, out_refs, send_sems, recv_sems):
    x, y, c = _me()
    chips = [(1 - x, y), (x, 1 - y), (1 - x, 1 - y)]
    return [pltpu.make_async_remote_copy(
        src_ref=out_refs[op].at[4 * cx + 2 * cy + c], dst_ref=out_refs[op].at[4 * cx + 2 * cy + c],
        send_sem=send_sems.at[3 * op + j], recv_sem=recv_sems.at[3 * op + j], device_id=(x, y, 1 - c), device_id_type=MESH)
        for op in range(len(out_refs)) for j, (cx, cy) in enumerate(chips)]


def _rs_sibling_copies(g_refs, out_refs, send_sems, recv_sems):
    x, y, c = _me()
    return [pltpu.make_async_remote_copy(
        src_ref=g_refs[op].at[k, 1 - c], dst_ref=out_refs[op].at[k], send_sem=send_sems.at[4 * op + k],
        recv_sem=recv_sems.at[4 * op + k], device_id=(x, y, 1 - c), device_id_type=MESH)
        for op in range(len(g_refs)) for k in range(4)]


def _rs_direct_copies(g_refs, land_refs, send_sems, recv_sems):
    x, y, c = _me()
    n = len(g_refs) // 2
    chips = [(1 - x, y), (x, 1 - y), (1 - x, 1 - y)]
    copies = []
    for op in range(n):
        g32, g16, from_sib, from_others = g_refs[op], g_refs[n + op], land_refs[op], land_refs[n + op]
        copies.append(pltpu.make_async_remote_copy(
            src_ref=g32.at[2 * x + y, 1 - c], dst_ref=from_sib, send_sem=send_sems.at[7 * op], recv_sem=recv_sems.at[7 * op],
            device_id=(x, y, 1 - c), device_id_type=MESH))
        for j, (cx, cy) in enumerate(chips):
            for s, cc in enumerate((c, 1 - c)):
                copies.append(pltpu.make_async_remote_copy(
                    src_ref=g16.at[2 * cx + cy, cc], dst_ref=from_others.at[2 * j + s], send_sem=send_sems.at[7 * op + 1 + 2 * j + s],
                    recv_sem=recv_sems.at[7 * op + 1 + 2 * j + s], device_id=(cx, cy, cc), device_id_type=MESH))
    return copies


def _rs_chip_copies(p_refs, out_refs, send_sems, recv_sems):
    x, y, c = _me()
    chips = [(1 - x, y), (x, 1 - y), (1 - x, 1 - y)]
    return [pltpu.make_async_remote_copy(
        src_ref=p_refs[op].at[2 * cx + cy], dst_ref=out_refs[op].at[j], send_sem=send_sems.at[3 * op + j],
        recv_sem=recv_sems.at[3 * op + j], device_id=(cx, cy, c), device_id_type=MESH)
        for op in range(len(p_refs)) for j, (cx, cy) in enumerate(chips)]


def _all_gather(placed):
    n = len(placed)

    def kern(*refs):
        in_refs, out_refs, (send_sems, recv_sems) = refs[:n], refs[n:2 * n], refs[2 * n:]
        x, y, c = _me()
        me, sibling = (x, y, c), (x, y, 1 - c)
        chips = [(1 - x, y), (x, 1 - y), (1 - x, 1 - y)]

        def copy(op, k, block, to, own=False):
            idx = 4 * block[0] + 2 * block[1] + block[2]
            return pltpu.make_async_remote_copy(
                src_ref=(in_refs if own else out_refs)[op].at[idx], dst_ref=out_refs[op].at[idx], send_sem=send_sems.at[7 * op + k],
                recv_sem=recv_sems.at[7 * op + k], device_id=to, device_id_type=MESH)

        first = []
        for op in range(n):
            first.append(copy(op, 0, me, sibling, own=True))
            first += [copy(op, 1 + j, me, (*chip, c), own=True) for j, chip in enumerate(chips)]
        for cp in first:
            cp.start()
        passed = []
        for j, chip in enumerate(chips):
            for op in range(n):
                copy(op, 1 + j, (*chip, c), me).wait_recv()
                passed.append(copy(op, 4 + j, (*chip, c), sibling))
                passed[-1].start()
        for op in range(n):
            copy(op, 0, sibling, me).wait_recv()
            for j, chip in enumerate(chips):
                copy(op, 4 + j, (*chip, 1 - c), me).wait_recv()
        for cp in first + passed:
            cp.wait_send()

    return pl.pallas_call(
        kern, name="weights_all_gather", out_shape=[pltpu.HBM(g.shape, g.dtype) for g in placed],
        in_specs=[ANY_SPEC] * n, out_specs=[ANY_SPEC] * n, input_output_aliases={i: i for i in range(n)},
        scratch_shapes=[pltpu.SemaphoreType.DMA((7 * n,)), pltpu.SemaphoreType.DMA((7 * n,))],
    )(*[_hbm(a) for a in placed])


def _row_tile(r, w, n_blocks):
    tr = r
    while tr > 8 and 2 * n_blocks * tr * w * 4 > 24 * 2**20:
        tr //= 2
    return tr


def _chip_sum(name, g, from_sibling, core):
    _, _, R, W = g.shape
    tr = _row_tile(R, W, 3)

    def kern(core_ref, g_ref, s_ref, o_ref):
        o_ref[...] = (g_ref[...] + s_ref[...]).astype(BF16)

    return pl.pallas_call(
        kern, name=name, out_shape=pltpu.HBM((4, R, W), BF16),
        grid_spec=pltpu.PrefetchScalarGridSpec(
            num_scalar_prefetch=1, grid=(4, R // tr),
            in_specs=[pl.BlockSpec((None, None, tr, W), lambda k, i, core: (k, core[0], i, 0)),
                      pl.BlockSpec((None, tr, W), lambda k, i, core: (k, i, 0))],
            out_specs=pl.BlockSpec((None, tr, W), lambda k, i, core: (k, i, 0))),
        compiler_params=pltpu.CompilerParams(dimension_semantics=("parallel", "parallel"), vmem_limit_bytes=_vmem(3 * tr * W * 4)),
    )(core, _hbm(g), _hbm(from_sibling))


def _adamw(w, g, m, v):
    m = ADAM_B1 * m + (1.0 - ADAM_B1) * g
    v = ADAM_B2 * v + (1.0 - ADAM_B2) * (g * g)
    m_hat = m / (1.0 - ADAM_B1 ** ADAM_STEP)
    v_hat = v / (1.0 - ADAM_B2 ** ADAM_STEP)
    return -ADAM_LR * (m_hat / (jnp.sqrt(v_hat) + ADAM_EPS) + ADAM_WD * w), m, v


def _finish_sharded(name, layers, w, m, v, where, deps=()):
    nl, R, W = w.shape
    n_other = layers[0][2].shape[0]
    tr = _row_tile(R, W, (8 + n_other) * nl)
    deps = _deps(deps)

    def kern(where_ref, *refs):
        w_ref, m_ref, v_ref = refs[3 * nl:3 * nl + 3]
        go_ref, d_ref, mo_ref, vo_ref = refs[3 * nl + 3 + len(deps):]
        for l in range(nl):
            g_ref, s_ref, c_ref = refs[3 * l:3 * l + 3]
            grad = g_ref[...] + s_ref[...]
            for j in range(n_other):
                grad = grad + c_ref[j].astype(F32)
            go_ref[l] = grad
            d_ref[l], mo_ref[l], vo_ref[l] = _adamw(w_ref[l], grad, m_ref[l], v_ref[l])

    row = pl.BlockSpec((nl, tr, W), lambda i, wh: (0, i, 0))
    in_specs, args = [], []
    for g, s, c in layers:
        sib = (pl.BlockSpec((None, tr, W), lambda i, wh: (wh[0], i, 0)) if s.ndim == 3 else pl.BlockSpec((tr, W), lambda i, wh: (i, 0)))
        in_specs += [pl.BlockSpec((None, None, tr, W), lambda i, wh: (wh[0], wh[1], i, 0)), sib,
                     pl.BlockSpec((n_other, tr, W), lambda i, wh: (0, i, 0))]
        args += [g, s, c]
    return pl.pallas_call(
        kern, name=name, out_shape=[pltpu.HBM((nl, R, W), F32)] * 4,
        grid_spec=pltpu.PrefetchScalarGridSpec(num_scalar_prefetch=1, grid=(R // tr,),
                                               in_specs=in_specs + [row, row, row] + [ANY_SPEC] * len(deps),
                                               out_specs=[row, row, row, row]),
        compiler_params=pltpu.CompilerParams(dimension_semantics=("parallel",),
                                             vmem_limit_bytes=_vmem(nl * (8 + n_other) * tr * W * 4)),
    )(where, *[_hbm(a) for a in (*args, w, m, v)], *deps)


SMALL_PLACE = (("mla_gq", 0, 0, 1, 256), ("mla_gkv", 0, 256, 1, 256), ("sgu_ln_g", 0, 512, 1, 512), ("sgu_ln_b", 1, 0, 1, 512),
               ("hg_lb", 2, 0, 2, 1024), ("ln1_g", 4, 0, 2, 1024), ("ln1_b", 6, 0, 2, 1024), ("sgu_b", 8, 0, 4, 128),
               ("ln2_g", 12, 0, 2, 1024), ("ln2_b", 14, 0, 2, 1024), ("hg_gnorm", 16, 0, 1, 1024))
SMALL_BUF_ROWS = 24
LOSS_ROW = 17


def _small_pack(gs, dev):
    pieces = [(gs["mla_gq"], 0, 0), (gs["mla_gkv"], 0, 256), (gs["sgu_ln_g"], 0, 512), (gs["sgu_ln_b"], 1, 0), (gs["hg_lb"], 2, 0),
              (gs["ln1_g0"], 4, 0), (gs["ln1_g1"], 5, 0), (gs["ln1_b0"], 6, 0), (gs["ln1_b1"], 7, 0), (gs["sgu_b"], 8, 0),
              (gs["ln2_g0"], 12, 0), (gs["ln2_g1"], 13, 0), (gs["ln2_b0"], 14, 0), (gs["ln2_b1"], 15, 0), (gs["hg_gnorm"], 16, 0),
              (gs["sq_err"], LOSS_ROW, 0)]
    n_p = len(pieces)

    def kern(dev_ref, *refs):
        a_ref, b_ref = refs[n_p + 1], refs[n_p + 2]
        a_ref[...] = jnp.zeros(a_ref.shape, F32)
        for ref, (_, r, l0) in zip(refs[:n_p], pieces):
            a_ref[r:r + ref.shape[0], l0:l0 + ref.shape[1]] = ref[...]
        b_ref[...] = refs[n_p][...]

    whole = lambda a: pl.BlockSpec(a.shape, functools.partial(lambda i, dev, nd: (0,) * nd, nd=a.ndim))
    return pl.pallas_call(
        kern, name="small_grads_pack",
        out_shape=[pltpu.HBM((N_DEV, SMALL_BUF_ROWS, D_MODEL), F32), pltpu.HBM((N_DEV, SGU_G, 128, 128), F32)],
        grid_spec=pltpu.PrefetchScalarGridSpec(
            num_scalar_prefetch=1, grid=(1,), in_specs=[whole(p[0]) for p in pieces] + [whole(gs["sgu_w"])],
            out_specs=[pl.BlockSpec((None, SMALL_BUF_ROWS, D_MODEL), lambda i, dev: (dev[0], 0, 0)),
                       pl.BlockSpec((None, SGU_G, 128, 128), lambda i, dev: (dev[0], 0, 0, 0))]),
    )(dev, *[p[0] for p in pieces], gs["sgu_w"])


def _small_copies(src_refs, land_refs, send_sems, recv_sems):
    px, py, pc = _me()
    me = 4 * px + 2 * py + pc
    return [pltpu.make_async_remote_copy(
        src_ref=land_refs[k].at[me], dst_ref=land_refs[k].at[me], send_sem=send_sems.at[2 * (r - 1) + k],
        recv_sem=recv_sems.at[2 * (r - 1) + k], device_id=(px ^ (r >> 2), py ^ ((r >> 1) & 1), pc ^ (r & 1)), device_id_type=MESH)
        for r in range(1, N_DEV) for k in range(2)]


def _small_adamw(slots_a, slots_b, given):
    names = [p[0] for p in SMALL_PLACE] + ["sgu_w"]
    n_names = len(names)
    wmv = [given[pre + name] for name in names for pre in ("", "m_", "v_")]
    vmem = pl.BlockSpec(memory_space=pltpu.VMEM)

    def kern(*refs):
        sum_a, sum_b = refs[0][0], refs[1][0]
        for d in range(1, N_DEV):
            sum_a, sum_b = sum_a + refs[0][d], sum_b + refs[1][d]
        wmv_refs, out_refs = refs[2:2 + 3 * n_names], refs[2 + 3 * n_names:]
        px, py, pc = _me()
        me = 4 * px + 2 * py + pc

        def own_block(full):
            acc = full[:, 0:128]
            for b in range(1, N_DEV):
                acc = jnp.where(me == b, full[:, b * 128:(b + 1) * 128], acc)
            return acc

        for idx, name in enumerate(names):
            w_ref, m_ref, v_ref = wmv_refs[3 * idx:3 * idx + 3]
            if name == "sgu_w":
                grad = sum_b[None]
            else:
                _, r, l0, nr, nl = SMALL_PLACE[idx]
                grad = sum_a[r:r + nr, l0:l0 + nl]
                if name == "hg_gnorm":
                    grad = own_block(grad)
                if name == "sgu_b":
                    grad = grad[None]
            res = (grad, *_adamw(w_ref[...], grad, m_ref[...], v_ref[...]))
            for o_ref, val in zip(out_refs[4 * idx:4 * idx + 4], res):
                o_ref[...] = val
        out_refs[4 * n_names][...] = (0.5 / D_MODEL) * jnp.sum(sum_a[LOSS_ROW:LOSS_ROW + 1, :], axis=1, keepdims=True)

    out_shape = [jax.ShapeDtypeStruct(given[name].shape, F32) for name in names for _ in range(4)]
    out_shape.append(jax.ShapeDtypeStruct((1, 1), F32))
    res = pl.pallas_call(
        kern, name="small_adamw", out_shape=out_shape, in_specs=[vmem] * (2 + len(wmv)), out_specs=[vmem] * len(out_shape),
    )(slots_a, slots_b, *wmv)
    out = {name: res[4 * idx:4 * idx + 4] for idx, name in enumerate(names)}
    out["loss"] = res[-1].reshape(())
    return out


class _Exchange:
    def __init__(self, given):
        self.given = given
        px, py, pc = _me()
        self.core = pc.reshape(1).astype(jnp.int32)
        self.dev = (4 * px + 2 * py + pc).reshape(1).astype(jnp.int32)
        self.where = jnp.stack([2 * px + py, pc]).astype(jnp.int32)
        self.state, self.layers = {}, {}

    def start_weights(self, lands, after):
        self.weights = _split_start("weights_first_start", [], lands, 4 * len(lands), _ag_first_copies, after=after)
        self.first_token = self.weights[4]

    def weights_forward(self, after):
        send_sems, recv_sems, shards, lands, _ = self.weights
        _, lands = _split_wait("weights_first_wait", send_sems, recv_sems, shards, lands, after, _ag_first_copies)
        self.weights = _split_start("weights_second_start", [], lands, 3 * len(lands), _ag_second_copies)
        return self.weights[4]

    def weights_ready(self, after):
        send_sems, recv_sems, shards, lands, _ = self.weights
        _, got = _split_wait("weights_second_wait", send_sems, recv_sems, shards, lands, after, _ag_second_copies)
        return dict(w_in_o=got[0], w_out_o=got[1], w_ff1=[got[2], got[3]], w_ff2=[got[4], got[5]])

    def small_start(self, gs):
        self.small = _split_start("small_grads_start", [], _small_pack(gs, self.dev), 14, _small_copies)
        return self.small[4]

    def small_finish(self, after):
        send_sems, recv_sems, _, lands, _ = self.small
        _, lands = _split_wait("small_grads_wait", send_sems, recv_sems, [], lands, after, _small_copies)
        return _small_adamw(lands[0], lands[1], self.given)

    def direct_start(self, tag, grads):
        f32 = [g[0].reshape(4, 2, *g[0].shape[1:]) for g in grads]
        bf16 = [g[1].reshape(4, 2, *g[1].shape[1:]) for g in grads]
        lands = [lax.empty(b.shape[2:], F32) for b in f32] + [lax.empty((6, *b.shape[2:]), BF16) for b in f32]
        self.state[tag] = _split_start(f"grads_{tag}_start", f32 + bf16, lands, 7 * len(grads), _rs_direct_copies)
        return self.state[tag][4]

    def direct_end(self, tag, after):
        send_sems, recv_sems, srcs, lands, _ = self.state[tag]
        srcs, lands = _split_wait(f"grads_{tag}_wait", send_sems, recv_sems, srcs, lands, after, _rs_direct_copies)
        n = len(lands) // 2
        self.layers[tag] = list(zip(srcs[:n], lands[:n], lands[n:]))

    def grads_start(self, tag, grads):
        blocks = [g.reshape(4, 2, *g.shape[1:]) for g in grads]
        lands = [lax.empty((4, *b.shape[2:]), F32) for b in blocks]
        self.state[tag] = _split_start(f"grads_{tag}_sibling_start", blocks, lands, 4 * len(blocks), _rs_sibling_copies)
        return self.state[tag][4]

    def grads_middle(self, tag, after):
        send_sems, recv_sems, blocks, lands, _ = self.state[tag]
        blocks, from_sibling = _split_wait(f"grads_{tag}_sibling_wait", send_sems, recv_sems, blocks, lands, [after], _rs_sibling_copies)
        sums = [_chip_sum(f"grads_{tag}_chip_sum_{k}", b, s, self.core) for k, (b, s) in enumerate(zip(blocks, from_sibling))]
        lands = [lax.empty((3, *p.shape[1:]), BF16) for p in sums]
        self.state[tag] = (blocks, from_sibling, _split_start(f"grads_{tag}_chips_start", sums, lands, 3 * len(sums), _rs_chip_copies))
        return self.state[tag][2][4]

    def grads_end(self, tag, after):
        blocks, from_sibling, (send_sems, recv_sems, sums, lands, _) = self.state[tag]
        after = list(after) if isinstance(after, (list, tuple)) else [after]
        _, from_chips = _split_wait(f"grads_{tag}_chips_wait", send_sems, recv_sems, sums, lands, after, _rs_chip_copies)
        self.layers[tag] = list(zip(blocks, from_sibling, from_chips))


def kernel(x, positions, w_in_e, mla_gq, mla_gkv, w_qb, w_kvb, sgu_ln_g, sgu_ln_b, sgu_w, sgu_b, w_out_e, w_in_o, hg_lb, hg_gnorm, w_out_o, ln1_g, ln1_b, w_ff1, w_ff2, ln2_g, ln2_b, loss_target, m_w_in_e, m_mla_gq, m_mla_gkv, m_w_qb, m_w_kvb, m_sgu_ln_g, m_sgu_ln_b, m_sgu_w, m_sgu_b, m_w_out_e, m_w_in_o, m_hg_lb, m_hg_gnorm, m_w_out_o, m_ln1_g, m_ln1_b, m_w_ff1, m_w_ff2, m_ln2_g, m_ln2_b, v_w_in_e, v_mla_gq, v_mla_gkv, v_w_qb, v_w_kvb, v_sgu_ln_g, v_sgu_ln_b, v_sgu_w, v_sgu_b, v_w_out_e, v_w_in_o, v_hg_lb, v_hg_gnorm, v_w_out_o, v_ln1_g, v_ln1_b, v_w_ff1, v_w_ff2, v_ln2_g, v_ln2_b):
    given = dict(locals())
    for n in ("w_in_e", "m_w_in_e", "v_w_in_e"):
        given[n] = jnp.swapaxes(given[n], 1, 2)
    ex = _Exchange(given)

    names = ["w_in_e", "w_qb", "w_kvb", "w_out_e"]
    placed = _place_own([(given[n], 0, BF16) for n in names] + [(hg_gnorm.reshape(1, 1, D_MODEL // N_DEV), 0, F32)]
                        + [(w_in_o, 0, BF16), (w_out_o, 0, BF16), (w_ff1, 0, BF16), (w_ff1, 1, BF16), (w_ff2, 0, BF16), (w_ff2, 1, BF16)],
                        ex.dev)
    got = _all_gather(placed[:5])
    ex.start_weights(placed[5:], after=[got[0]])
    gw = dict(zip(names, got[:4]))
    small_names = ["mla_gq", "mla_gkv", "sgu_ln_g", "sgu_ln_b", "sgu_w", "sgu_b", "hg_lb", "ln1_g", "ln1_b", "ln2_g", "ln2_b"]
    sp = {n: given[n] for n in small_names}
    sp["hg_gnorm"] = got[4].reshape(1, D_MODEL)

    _, dx, grads, gs = _local_step(x[0], positions[0], loss_target[0], gw, sp, ex)

    def finish(n, layers, deps=()):
        return _finish_sharded(f"finish_{n}", layers, given[n], given["m_" + n], given["v_" + n], ex.where, deps=deps)

    ex.direct_end("l1", after=[dx])
    ex.direct_end("l0m", after=[dx])
    l1, l0m = ex.layers["l1"], ex.layers["l0m"]
    results = {}
    token = ex.grads_start("l0s", [grads[n] for n in names])
    results["w_ff1"] = finish("w_ff1", [l0m[0], l1[0]], deps=[token])
    token = ex.grads_middle("l0s", after=results["w_ff1"][0])
    results["w_ff2"] = finish("w_ff2", [l0m[1], l1[1]], deps=[token])
    results["w_in_o"] = finish("w_in_o", [l1[2]], deps=[token])
    results["w_out_o"] = finish("w_out_o", [l1[3]], deps=[token])
    results.update(ex.small_finish(after=[results["w_in_o"][0]]))
    ex.grads_end("l0s", after=[results[n][0] for n in ("mla_gq", "w_ff2", "w_in_o", "w_out_o")])
    for n, layer in zip(names, ex.layers["l0s"]):
        results[n] = finish(n, [layer])
    results["w_in_e"] = [jnp.swapaxes(a, 1, 2) for a in results["w_in_e"]]

    order = ["w_in_e", "mla_gq", "mla_gkv", "w_qb", "w_kvb", "sgu_ln_g", "sgu_ln_b", "sgu_w", "sgu_b", "w_out_e", "w_in_o",
             "hg_lb", "hg_gnorm", "w_out_o", "ln1_g", "ln1_b", "w_ff1", "w_ff2", "ln2_g", "ln2_b"]
    return (results["loss"], dx[None], *[results[name][kind] for kind in range(4) for name in order])
```

```python
import functools
import math

import jax
import jax.numpy as jnp
import numpy as np
from jax import lax
from jax.experimental import pallas as pl
from jax.experimental.pallas import tpu as pltpu

F32 = jnp.float32
BF16 = jnp.bfloat16
MESH = pl.DeviceIdType.MESH
HIGHEST = lax.Precision.HIGHEST

D_MODEL = 1024
D_FF = 4096
N_DEV = 8
HEADS = 8
HEAD_W = 128
MLA_NOPE = 64
MLA_ROPE = 32
MLA_V = 64
MLA_LORA = 256
MLA_SCALE = (MLA_NOPE + MLA_ROPE) ** -0.5
ROPE_BASE = 10000.0
SGU_DIM = 512
SGU_G = 4
SGU_CHUNK = 128
HG_CHUNK = 64
HG_CHUNKS_PER_STEP = 4
ALPHA = (2 * 2) ** 0.25
EPS = 1e-5
ADAM_LR, ADAM_B1, ADAM_B2, ADAM_EPS, ADAM_WD, ADAM_STEP = 0.001, 0.9, 0.999, 1e-08, 0.01, 10

VMEM_CAP_V7X = 56 * 2**20
VMEM_SLACK = 12 * 2**20
TM = 512
TN = 512


def _vmem(block_bytes):
    return int(min(VMEM_CAP_V7X, 2 * block_bytes + VMEM_SLACK))


def _hbm(a):
    return pltpu.with_memory_space_constraint(a, pltpu.HBM)


def _nbytes(shape, dtype):
    return int(np.prod([d for d in shape if d is not None])) * jnp.dtype(dtype).itemsize


def _sig(x):
    return 1.0 / (1.0 + jnp.exp(-x))


def _gelu(x):
    c = math.sqrt(2.0 / math.pi)
    t = jnp.tanh(c * (x + 0.044715 * x * x * x))
    return 0.5 * x * (1.0 + t), t


def _gelu_grad(x, t):
    c = math.sqrt(2.0 / math.pi)
    return 0.5 * (1.0 + t) + 0.5 * x * (1.0 - t * t) * c * (1.0 + 3 * 0.044715 * x * x)


def _dot(a, b, dims, precision=None):
    return lax.dot_general(a, b, (dims, ((), ())), preferred_element_type=F32, precision=precision)


NN = ((1,), (0,))
NT = ((1,), (1,))
TN_ = ((0,), (0,))


def _deps(deps):
    return [d for d in deps if d is not None]


RING = 3


def _tiled(name, grid, ins, outs, compute, direct=False, deps=(), ring=False):
    n_in, deps = len(ins), _deps(deps)
    n_skip = n_in + len(deps)
    n_steps = grid[1]
    ring = ring and grid[0] == 1 and n_steps >= RING - 1

    def kern(*refs):
        in_refs = list(refs[:n_in])
        if ring:
            src, (buf, sems), tm = in_refs[0], refs[n_skip + len(outs):], ins[0][1][0]
            s = pl.program_id(1)

            def copy(t):
                row = t * tm if isinstance(t, int) else pl.multiple_of(t * tm, tm)
                return pltpu.make_async_copy(src.at[pl.ds(row, tm), :], buf.at[t % RING], sems.at[t % RING])

            @pl.when(s == 0)
            def _():
                for t in range(RING - 1):
                    copy(t).start()

            @pl.when(s + RING - 1 < n_steps)
            def _():
                copy(s + RING - 1).start()

            copy(s).wait()
            in_refs[0] = buf.at[s % RING]
        if direct:
            compute(in_refs, refs[n_skip:n_skip + len(outs)])
            return
        for o_ref, r in zip(refs[n_skip:], compute(*in_refs)):
            o_ref[...] = r.astype(o_ref.dtype).reshape(o_ref.shape)

    swap = lambda f: (lambda j, i: f(i, j))
    nbytes = sum(_nbytes(blk, a.dtype) for a, blk, _ in ins) + sum(_nbytes(blk, dt) + _nbytes(blk, F32) for _, dt, blk, _ in outs)
    in_specs = [pl.BlockSpec(blk, swap(f), pipeline_mode=pl.Buffered(1) if tuple(blk) == tuple(a.shape) else None)
                for a, blk, f in ins]
    scratch = []
    if ring:
        in_specs[0] = ANY_SPEC
        scratch = [pltpu.VMEM((RING, *ins[0][1]), ins[0][0].dtype), pltpu.SemaphoreType.DMA((RING,))]
    res = pl.pallas_call(
        kern, name=name, grid=grid,
        in_specs=in_specs + [ANY_SPEC] * len(deps),
        out_specs=[pl.BlockSpec(blk, swap(f)) for _, _, blk, f in outs],
        out_shape=[pltpu.HBM(shape, dt) for shape, dt, _, _ in outs],
        scratch_shapes=scratch,
        compiler_params=pltpu.CompilerParams(dimension_semantics=("arbitrary",) * 2 if ring else ("parallel",) * 2,
                                             vmem_limit_bytes=_vmem(nbytes)),
    )(*[_hbm(a) for a, _, _ in ins], *deps)
    return res if len(res) > 1 else res[0]


def _rb(a, tm, w=None, cb=0):
    return (a, (tm, a.shape[1] if w is None else w), lambda i, j: (i, cb))


def _cw(b, tn):
    return (b, (b.shape[0], tn), lambda i, j: (0, j))


def _tl(a, tm):
    return (a, (a.shape[0], tm), lambda i, j: (0, i))


def _out(m, n, dtype, tm, tn):
    return ((m, n), dtype, (tm, tn), lambda i, j: (i, j))


def _out_dev(k, n, tm, dtype=F32):
    return ((N_DEV, k, n), dtype, (None, tm, n), lambda i, j: (j, i, 0))


def _twice(acc):
    return acc, acc


def _mmc(dims, n_pairs=1, epilogue=None):
    def compute(*refs):
        acc = None
        for k in range(n_pairs):
            d = _dot(refs[2 * k][...].astype(BF16), refs[2 * k + 1][...].astype(BF16), dims)
            acc = d if acc is None else acc + d
        ext = [r[...] for r in refs[2 * n_pairs:]]
        return epilogue(acc, *ext) if epilogue is not None else (acc,)

    return compute


def _res(w):
    return (w, w.shape, functools.partial(lambda i, j, nd: (0,) * nd, nd=w.ndim))


def _mmc_blocks(nblk, dims, rhs_block, epilogue=None):
    def compute(in_refs, out_refs):
        a = in_refs[0][...].astype(BF16)
        for d in range(nblk):
            acc = _dot(a, rhs_block(in_refs[1], d).astype(BF16), dims)
            n = acc.shape[1]
            ext = [r[:, d * n:(d + 1) * n] for r in in_refs[2:]]
            res = epilogue(acc, *ext) if epilogue is not None else (acc,)
            for o_ref, r in zip(out_refs, res):
                o_ref[:, d * n:(d + 1) * n] = r.astype(o_ref.dtype)

    return compute


def _rowwise(name, body, rows, consts, out_rows, out_accs=(), tr=512, deps=()):
    T = rows[0][0].shape[0]
    tr = min(tr, T)
    deps = _deps(deps)
    nr, ncn, no, nd = len(rows), len(consts), len(out_rows), len(deps)

    def kern(*refs):
        accs = refs[nr + ncn + nd + no:]
        if accs:
            @pl.when(pl.program_id(0) == 0)
            def _():
                for a in accs:
                    a[...] = jnp.zeros(a.shape, a.dtype)
        body(refs[:nr], refs[nr:nr + ncn], refs[nr + ncn + nd:nr + ncn + nd + no], accs)

    in_specs = [pl.BlockSpec((tr, w), functools.partial(lambda i, cb: (i, cb), cb=cb)) for _, w, cb in rows]
    in_specs += [pl.BlockSpec(c.shape, functools.partial(lambda i, nd: (0,) * nd, nd=c.ndim), pipeline_mode=pl.Buffered(1))
                 for c in consts]
    in_specs += [ANY_SPEC] * nd
    out_specs = [pl.BlockSpec((tr, w), lambda i: (i, 0)) for w, _ in out_rows]
    out_specs += [pl.BlockSpec(s, functools.partial(lambda i, nd: (0,) * nd, nd=len(s))) for s, _ in out_accs]
    out_shape = [pltpu.HBM((T, w), dt) for w, dt in out_rows]
    out_shape += [pltpu.HBM(s, dt) for s, dt in out_accs]
    nbytes = sum(_nbytes((tr, w), a.dtype) for a, w, _ in rows) + sum(_nbytes(c.shape, c.dtype) for c in consts)
    nbytes += sum(_nbytes((tr, w), dt) for w, dt in out_rows) + sum(_nbytes(s, dt) for s, dt in out_accs)
    res = pl.pallas_call(
        kern, name=name, grid=(T // tr,), in_specs=in_specs, out_specs=out_specs, out_shape=out_shape,
        compiler_params=pltpu.CompilerParams(dimension_semantics=("arbitrary",), vmem_limit_bytes=_vmem(nbytes)),
    )(*[_hbm(a) for a, _, _ in rows], *[_hbm(c) for c in consts], *deps)
    return res if len(res) > 1 else res[0]


def _full(a):
    return (a, a.shape[1], 0)


def _ln_stats(y):
    mu = jnp.mean(y, axis=-1, keepdims=True)
    yc = y - mu
    r = lax.rsqrt(jnp.mean(yc * yc, axis=-1, keepdims=True) + EPS)
    return yc * r, r


def _row_halves(n):
    return [slice(0, n // 2), slice(n // 2, n)] if n >= 256 else [slice(0, n)]


def _ln_back(dh, xh, r, gain, dg_ref, db_ref):
    dg_ref[...] += jnp.sum(dh * xh, axis=0, keepdims=True)
    db_ref[...] += jnp.sum(dh, axis=0, keepdims=True)
    dx = dh * gain
    return r * (dx - jnp.mean(dx, axis=-1, keepdims=True) - xh * jnp.mean(dx * xh, axis=-1, keepdims=True))


def _proj_ln(name, acts, weights, h_in, g, b, layer, deps=()):
    n = len(acts)

    def body(rows, consts, outs, accs):
        acc = None
        for k in range(n):
            d = _dot(rows[k][...].astype(BF16), consts[k][...], NN)
            acc = d if acc is None else acc + d
        y = ALPHA * rows[n][...] + acc
        xh, _ = _ln_stats(y)
        h = xh * consts[n][layer:layer + 1, :] + consts[n + 1][layer:layer + 1, :]
        outs[0][...] = y
        outs[1][...] = h
        outs[2][...] = h.astype(BF16)

    return _rowwise(name, body, [_full(a) for a in acts] + [_full(h_in)], [*weights, g, b],
                    [(D_MODEL, F32), (D_MODEL, F32), (D_MODEL, BF16)], tr=TM, deps=deps)


def _proj_ln_loss(name, act, w2, h_in, g, b, layer, target):
    def body(rows, consts, outs, accs):
        y = ALPHA * rows[1][...] + _dot(rows[0][...], consts[0][...], NN)
        xh, r = _ln_stats(y)
        gain = consts[1][layer:layer + 1, :]
        err = xh * gain + consts[2][layer:layer + 1, :] - rows[2][...]
        accs[0][...] += jnp.sum(err * err, axis=0, keepdims=True)
        dy = _ln_back(err * (1.0 / D_MODEL), xh, r, gain, accs[1], accs[2])
        outs[0][...] = dy
        outs[1][...] = dy.astype(BF16)

    return _rowwise(name, body, [_full(act), _full(h_in), _full(target)], [w2, g, b], [(D_MODEL, F32), (D_MODEL, BF16)],
                    [((1, D_MODEL), F32)] * 3, tr=TM)


def _dh_ln_back(name, da, w, dy_next, y, g, layer, proj=(), deps=()):
    def body(rows, consts, outs, accs):
        n = consts[0].shape[2]
        for sl in _row_halves(rows[0].shape[0]):
            acc = ALPHA * rows[1][sl, :]
            for d in range(N_DEV):
                acc = acc + _dot(rows[0][sl, d * n:(d + 1) * n], consts[0][d], NT)
            xh, r = _ln_stats(rows[2][sl, :])
            dy = _ln_back(acc, xh, r, consts[1][layer:layer + 1, :], accs[0], accs[1])
            outs[0][sl, :] = dy
            dy_bf = dy.astype(BF16)
            outs[1][sl, :] = dy_bf
            off = 0
            for k, p in enumerate(proj):
                outs[2][sl, off:off + p.shape[0]] = _dot(dy_bf, consts[2 + k][...], NT).astype(BF16)
                off += p.shape[0]

    out_rows = [(D_MODEL, F32), (D_MODEL, BF16)] + ([(sum(p.shape[0] for p in proj), BF16)] if proj else [])
    return _rowwise(name, body, [_full(da), _full(dy_next), _full(y)], [w, g, *proj], out_rows,
                    [((1, D_MODEL), F32)] * 2, tr=TM, deps=deps)


def _relu2_epilogue(acc):
    a = jnp.maximum(acc, 0.0)
    return acc, a * a


def _mlp_up(tag, h_bf, w1):
    T = h_bf.shape[0]
    tm = min(TM, T)
    return _tiled(f"{tag}_ff1", (1, T // tm), [_rb(h_bf, tm), _res(w1)],
                  [_out(T, D_FF, BF16, tm, D_FF), _out(T, D_FF, BF16, tm, D_FF)],
                  _mmc_blocks(N_DEV, NN, lambda w, d: w[d], epilogue=_relu2_epilogue), direct=True, ring=True)


def _mlp_bwd_w(tag, h_bf, a, act, dff_bf, w2, deps=()):
    T = h_bf.shape[0]
    tm = min(TM, T)
    da = _tiled(f"{tag}_dact", (1, T // tm), [_rb(dff_bf, tm), _res(w2), _rb(a, tm)], [_out(T, D_FF, BF16, tm, D_FF)],
                _mmc_blocks(N_DEV, NT, lambda w, d: w[d], epilogue=lambda acc, a_t: (acc * 2.0 * jnp.maximum(a_t.astype(F32), 0.0),)),
                direct=True, deps=deps, ring=True)
    dw2 = _tiled(f"{tag}_dw2", (1, D_FF // TM), [_tl(act, TM), _res(dff_bf)],
                 [_out(D_FF, D_MODEL, F32, TM, D_MODEL), _out(D_FF, D_MODEL, BF16, TM, D_MODEL)], _mmc(TN_, epilogue=_twice))
    dw1 = _tiled(f"{tag}_dw1", (N_DEV, 1), [_res(h_bf), _cw(da, TN)],
                 [_out_dev(D_MODEL, TN, D_MODEL), _out_dev(D_MODEL, TN, D_MODEL, BF16)], _mmc(TN_, epilogue=_twice))
    return da, dw1, [a.reshape(N_DEV, D_FF // N_DEV, D_MODEL) for a in dw2]


def _rope_tables(positions_col, inv_freq):
    T, half = positions_col.shape[0], MLA_ROPE // 2
    groups = HEAD_W // half
    n = T // groups

    def spread(a, g, first_lane):
        shift = (first_lane - half * g) % HEAD_W
        return pltpu.roll(a, shift, 1) if shift else a

    def body(rows, consts, outs, accs):
        lane = lax.broadcasted_iota(jnp.int32, (n, HEAD_W), 1)
        pos = jnp.zeros((n, HEAD_W), F32)
        for g in range(groups):
            pos = jnp.where(lane // half == g, rows[0][g * n:(g + 1) * n, :].astype(F32), pos)
        ang = pos * consts[0][...]
        c, s = jnp.cos(ang), jnp.sin(ang)
        for g in range(groups):
            r = slice(g * n, (g + 1) * n)
            outs[0][r, :] = jnp.where(lane < 64, 1.0, jnp.where(lane < 80, spread(c, g, 64), jnp.where(lane < 96, spread(c, g, 80), 0.0)))
            outs[1][r, :] = jnp.where((lane >= 64) & (lane < 80), -spread(s, g, 64), 0.0)
            outs[2][r, :] = jnp.where((lane >= 80) & (lane < 96), spread(s, g, 80), 0.0)

    return _rowwise("rope_tables", body, [_full(positions_col)], [jnp.tile(inv_freq, groups).reshape(1, HEAD_W)],
                    [(HEAD_W, F32)] * 3, tr=T)


def _rope(x, c, s1, s2):
    return x * c + pltpu.roll(x, 112, 1) * s1 + pltpu.roll(x, 16, 1) * s2


def _rope_t(dx, c, s1, s2):
    return dx * c + pltpu.roll(dx * s1, 16, 1) + pltpu.roll(dx * s2, 112, 1)


def _rms(c):
    r = lax.rsqrt(jnp.mean(c * c, axis=-1, keepdims=True) + EPS)
    return c * r, r


def _rope_heads(x, c, s1, s2, fn):
    return jnp.concatenate([fn(x[:, h * HEAD_W:(h + 1) * HEAD_W], c, s1, s2) for h in range(HEADS)], axis=1)


def _rope_key_rows(w_ref):
    return jnp.concatenate([jnp.zeros((64, D_MODEL), BF16), w_ref[512:544, :], jnp.zeros((32, D_MODEL), BF16)], axis=0)


def _mla_in(x, w, tabs, gq, gkv, deps=()):
    def body(rows, consts, outs, accs):
        xb = rows[0][...].astype(BF16)
        w_ref = consts[0]
        zc = _dot(xb, w_ref[0:512, :], NT)
        zr = _dot(xb, _rope_key_rows(w_ref), NT)
        outs[0][:, 0:512] = zc
        outs[0][:, 512:640] = zr
        outs[1][...] = _dot(xb, w_ref[544:1568, :], NT)
        outs[2][...] = (_rms(zc[:, 0:256])[0] * consts[1][...]).astype(BF16)
        outs[3][...] = (_rms(zc[:, 256:512])[0] * consts[2][...]).astype(BF16)
        outs[4][...] = _rope(zr, rows[1][...], rows[2][...], rows[3][...])

    return _rowwise("l0_in", body, [_full(x)] + [_full(t) for t in tabs], [w, gq, gkv],
                    [(640, F32), (1024, F32), (256, BF16), (256, BF16), (HEAD_W, F32)], deps=deps)


def _mla_qkv(cqn, ckvn, kr_rot, tabs, wq, wk, wv):
    def body(rows, consts, outs, accs):
        c, s1, s2 = rows[3][...], rows[4][...], rows[5][...]
        outs[0][...] = _rope_heads(_dot(rows[0][...], consts[0][...], NN), c, s1, s2, _rope).astype(BF16)
        outs[1][...] = (_dot(rows[1][...], consts[1][...], NN) + jnp.concatenate([rows[2][...]] * HEADS, axis=1)).astype(BF16)
        outs[2][...] = _dot(rows[1][...], consts[2][...], NN).astype(BF16)

    rows = [_full(cqn), _full(ckvn), _full(kr_rot)] + [_full(t) for t in tabs]
    return _rowwise("l0_qkv", body, rows, [wq, wk, wv], [(HEADS * HEAD_W, BF16)] * 3)


def _mla_back(zm, cqn, ckvn, tabs, gq, gkv, wq, wk, wv, dq, dk, dv):
    def body(rows, consts, outs, accs):
        c, s1, s2 = rows[4][...], rows[5][...], rows[6][...]
        dk_t, dv_bf = rows[8][...], rows[9][...].astype(BF16)
        dq_bf = _rope_heads(rows[7][...], c, s1, s2, _rope_t).astype(BF16)
        dk_bf = dk_t.astype(BF16)
        accs[0][...] += _dot(rows[2][...], dq_bf, TN_)
        accs[1][...] += _dot(rows[3][...], dk_bf, TN_)
        accs[2][...] += _dot(rows[3][...], dv_bf, TN_)
        dlat = [_dot(dq_bf, consts[2][...], NT), _dot(dk_bf, consts[3][...], NT) + _dot(dv_bf, consts[4][...], NT)]
        for k in range(2):
            ch, r = _rms(rows[k][...])
            accs[3 + k][...] += jnp.sum(dlat[k] * ch, axis=0, keepdims=True)
            dc = dlat[k] * consts[k][...]
            outs[0][:, 256 * k:256 * (k + 1)] = (r * (dc - ch * jnp.mean(dc * ch, axis=-1, keepdims=True))).astype(BF16)
        dks = dk_t[:, 0:HEAD_W]
        for h in range(1, HEADS):
            dks = dks + dk_t[:, h * HEAD_W:(h + 1) * HEAD_W]
        lane = lax.broadcasted_iota(jnp.int32, dks.shape, 1)
        dks = jnp.where((lane >= 64) & (lane < 96), dks, 0.0)
        outs[0][:, 512:640] = _rope_t(dks, c, s1, s2).astype(BF16)

    rows = [(zm, 256, 0), (zm, 256, 1), _full(cqn), _full(ckvn)] + [_full(t) for t in tabs] + [_full(dq), _full(dk), _full(dv)]
    wide = HEADS * HEAD_W
    return _rowwise("l0_mla_back", body, rows, [gq, gkv, wq, wk, wv], [(640, BF16)],
                    [((MLA_LORA, wide), F32)] * 3 + [((1, MLA_LORA), F32)] * 2, tr=256)


def _in_back(x, dzm, dzs, dy, w, deps=()):
    def body(rows, consts, outs, accs):
        w_ref = consts[0]
        dzm_t, dzs_t = rows[1][...], rows[2][...]
        outs[0][...] = (_dot(dzm_t[:, 0:512], w_ref[0:512, :], NN) + _dot(dzm_t[:, 512:640], _rope_key_rows(w_ref), NN)
                        + _dot(dzs_t, w_ref[544:1568, :], NN) + ALPHA * rows[3][...])
        xb = rows[0][...].astype(BF16)
        gm = _dot(dzm_t, xb, TN_)
        accs[0][0:512, :] += gm[0:512]
        accs[0][512:544, :] += gm[512 + 64:512 + 96]
        accs[0][544:1568, :] += _dot(dzs_t, xb, TN_)

    return _rowwise("l0_in_back", body, [_full(x), _full(dzm), _full(dzs), _full(dy)], [w], [(D_MODEL, F32)],
                    [((1568, D_MODEL), F32)], deps=deps)


def _out_weight_grads(o_att, b_out, dy_bf):
    def body(rows, consts, outs, accs):
        d = rows[2][...]
        accs[0][...] += _dot(rows[0][...].astype(BF16), d, TN_)
        accs[1][...] += _dot(rows[1][...], d, TN_)

    return _rowwise("l0_dw_out", body, [_full(o_att), _full(b_out), _full(dy_bf)], [], [],
                    [((HEADS * HEAD_W, D_MODEL), F32), ((SGU_DIM, D_MODEL), F32)])


def _attn_block(T):
    return min(1024, T)


def _attn_fwd(q, k, v):
    T = q.shape[0]
    BQ = _attn_block(T)
    nq = T // BQ

    def kern(q_ref, k_ref, v_ref, o_ref, lse_ref):
        def step(i, j, carry, masked):
            m, l, acc = carry
            qb = q_ref[pl.ds(pl.multiple_of(i * BQ, BQ), BQ), :]
            kb = k_ref[pl.ds(pl.multiple_of(j * BQ, BQ), BQ), :]
            vb = v_ref[pl.ds(pl.multiple_of(j * BQ, BQ), BQ), :]
            s = _dot(qb, kb, NT) * MLA_SCALE
            if masked:
                row = lax.broadcasted_iota(jnp.int32, s.shape, 0)
                col = lax.broadcasted_iota(jnp.int32, s.shape, 1)
                s = jnp.where(col <= row, s, -1e30)
            m_new = jnp.maximum(m, jnp.max(s, axis=-1, keepdims=True))
            p = jnp.exp(s - m_new)
            a = jnp.exp(m - m_new)
            l = a * l + jnp.sum(p, axis=-1, keepdims=True)
            acc = a * acc + _dot(p.astype(BF16), vb, NN)
            return m_new, l, acc

        def qloop(i, _):
            init = (jnp.full((BQ, 1), -1e30, F32), jnp.zeros((BQ, 1), F32), jnp.zeros((BQ, HEAD_W), F32))
            carry = lax.fori_loop(0, i, lambda j, c: step(i, j, c, False), init)
            m, l, acc = step(i, i, carry, True)
            rows = pl.ds(pl.multiple_of(i * BQ, BQ), BQ)
            o_ref[rows, :] = acc / l
            lse_ref[0, rows, :] = m + jnp.log(l)
            return 0

        lax.fori_loop(0, nq, qloop, 0)

    head = pl.BlockSpec((T, HEAD_W), lambda h: (0, h))
    nbytes = 3 * _nbytes((T, HEAD_W), BF16) + _nbytes((T, HEAD_W), F32) + _nbytes((T, 128), F32)
    return pl.pallas_call(
        kern, name="attn_fwd", grid=(HEADS,), in_specs=[head, head, head],
        out_specs=[head, pl.BlockSpec((1, T, 1), lambda h: (h, 0, 0))],
        out_shape=[pltpu.HBM((T, HEADS * HEAD_W), F32), pltpu.HBM((HEADS, T, 1), F32)],
        compiler_params=pltpu.CompilerParams(dimension_semantics=("parallel",), vmem_limit_bytes=_vmem(nbytes)),
    )(_hbm(q), _hbm(k), _hbm(v))


def _attn_bwd(q, k, v, o, lse, dcat, deps=()):
    T = q.shape[0]
    BQ = _attn_block(T)
    nq = T // BQ
    deps = _deps(deps)

    def kern(q_ref, k_ref, v_ref, o_ref, lse_ref, do_ref, *rest):
        dq_ref, dk_ref, dv_ref, dd_ref = rest[len(deps):]
        dq_ref[...] = jnp.zeros(dq_ref.shape, F32)

        def dloop(i, _):
            rows = pl.ds(pl.multiple_of(i * BQ, BQ), BQ)
            dd_ref[rows, :] = jnp.sum(do_ref[rows, :].astype(F32) * o_ref[rows, :], axis=-1, keepdims=True)
            return 0

        lax.fori_loop(0, nq, dloop, 0)

        def tile(q0, k0, n, carry, masked):
            dk_acc, dv_acc = carry
            rq = pl.ds(pl.multiple_of(q0, n), n)
            rk = pl.ds(pl.multiple_of(k0, n), n)
            qb, kb, vb, dob = q_ref[rq, :], k_ref[rk, :], v_ref[rk, :], do_ref[rq, :]
            s = _dot(qb, kb, NT) * MLA_SCALE
            p = jnp.exp(s - lse_ref[0, rq, :])
            if masked:
                row = lax.broadcasted_iota(jnp.int32, s.shape, 0)
                col = lax.broadcasted_iota(jnp.int32, s.shape, 1)
                p = jnp.where(col <= row, p, 0.0)
            dp = _dot(dob, vb, NT)
            ds = (p * (dp - dd_ref[rq, :]) * MLA_SCALE).astype(BF16)
            dv_acc = dv_acc + _dot(p.astype(BF16), dob, TN_)
            dk_acc = dk_acc + _dot(ds, qb, TN_)
            dq_ref[rq, :] += _dot(ds, kb, NN)
            return dk_acc, dv_acc

        def kloop(j, _):
            base, half = j * BQ, BQ // 2
            zero = (jnp.zeros((half, HEAD_W), F32), jnp.zeros((half, HEAD_W), F32))
            early = tile(base + half, base, half, tile(base, base, half, zero, True), False)
            late = tile(base + half, base + half, half, zero, True)
            carry = tuple(jnp.concatenate([a, b], axis=0) for a, b in zip(early, late))
            dk_acc, dv_acc = lax.fori_loop(j + 1, nq, lambda i, c: tile(i * BQ, base, BQ, c, False), carry)
            rk = pl.ds(pl.multiple_of(j * BQ, BQ), BQ)
            dk_ref[rk, :] = dk_acc
            dv_ref[rk, :] = dv_acc
            return 0

        lax.fori_loop(0, nq, kloop, 0)

    head = pl.BlockSpec((T, HEAD_W), lambda h: (0, h))
    nbytes = 4 * _nbytes((T, HEAD_W), BF16) + 5 * _nbytes((T, HEAD_W), F32) + 2 * _nbytes((T, 128), F32)
    return pl.pallas_call(
        kern, name="attn_bwd", grid=(HEADS,),
        in_specs=[head, head, head, head, pl.BlockSpec((1, T, 1), lambda h: (h, 0, 0)), head] + [ANY_SPEC] * len(deps),
        out_specs=[head, head, head],
        out_shape=[pltpu.HBM((T, HEADS * HEAD_W), F32)] * 3,
        scratch_shapes=[pltpu.VMEM((T, 1), F32)],
        compiler_params=pltpu.CompilerParams(dimension_semantics=("parallel",), vmem_limit_bytes=_vmem(nbytes)),
    )(*[_hbm(a) for a in (q, k, v, o, lse, dcat)], *deps)


def _sgu_common(u, v, ln_g, ln_b):
    ua, tu = _gelu(u)
    va, tv = _gelu(v)
    vh, r = _ln_stats(va)
    return ua, tu, tv, vh, r, vh * ln_g + ln_b


def _tril_mask(n):
    return lax.broadcasted_iota(jnp.int32, (n, n), 1) <= lax.broadcasted_iota(jnp.int32, (n, n), 0)


def _sgu_fwd(zs, ln_g, ln_b, w, bias_full):
    def body(rows, consts, outs, accs):
        ua, _, _, _, _, vn = _sgu_common(rows[0][...], rows[1][...], consts[0][...], consts[1][...])
        vn = vn.astype(BF16)
        tri = _tril_mask(SGU_CHUNK)
        for g in range(SGU_G):
            wg = jnp.where(tri, consts[2][0, g], 0.0).astype(BF16)
            cols = slice(g * 128, (g + 1) * 128)
            for c in range(ua.shape[0] // SGU_CHUNK):
                rws = slice(c * SGU_CHUNK, (c + 1) * SGU_CHUNK)
                mixed = _dot(wg, vn[rws, cols], NN) + consts[3][:, cols]
                outs[0][rws, cols] = (ua[rws, cols] * mixed).astype(BF16)

    return _rowwise("sgu_fwd", body, [(zs, 512, 0), (zs, 512, 1)], [ln_g, ln_b, w, bias_full], [(SGU_DIM, BF16)])


def _sgu_bwd(zs, dcat, ln_g, ln_b, w, bias_full):
    def body(rows, consts, outs, accs):
        u, v = rows[0][...], rows[1][...]
        ua, tu, tv, vh, r, vn = _sgu_common(u, v, consts[0][...], consts[1][...])
        dout = rows[2][...].astype(F32)
        vn_bf = vn.astype(BF16)
        tri = _tril_mask(SGU_CHUNK)
        dmixed = (dout * ua)
        dmixed_bf = dmixed.astype(BF16)
        ones = jnp.ones((8, SGU_CHUNK), F32)
        dvn_cols, mixed_cols = [], []
        for g in range(SGU_G):
            wg = jnp.where(tri, consts[2][0, g], 0.0).astype(BF16)
            cols = slice(g * 128, (g + 1) * 128)
            dvn_rows, mixed_rows = [], []
            dw = jnp.zeros((SGU_CHUNK, SGU_CHUNK), F32)
            dmix_sum = jnp.zeros((SGU_CHUNK, 128), F32)
            for c in range(u.shape[0] // SGU_CHUNK):
                rws = slice(c * SGU_CHUNK, (c + 1) * SGU_CHUNK)
                mixed_rows.append(_dot(wg, vn_bf[rws, cols], NN) + consts[3][:, cols])
                dvn_rows.append(_dot(wg, dmixed_bf[rws, cols], TN_))
                dw = dw + _dot(dmixed_bf[rws, cols], vn_bf[rws, cols], NT)
                dmix_sum = dmix_sum + dmixed[rws, cols]
            accs[0][g] += jnp.where(tri, dw, 0.0)
            accs[3][g:g + 1, :] += _dot(ones, dmix_sum, NT, precision=HIGHEST)[0:1, :]
            dvn_cols.append(jnp.concatenate(dvn_rows, axis=0))
            mixed_cols.append(jnp.concatenate(mixed_rows, axis=0))
        dvn = jnp.concatenate(dvn_cols, axis=1)
        mixed = jnp.concatenate(mixed_cols, axis=1)
        accs[1][...] += jnp.sum(dvn * vh, axis=0, keepdims=True)
        accs[2][...] += jnp.sum(dvn, axis=0, keepdims=True)
        dvh = dvn * consts[0][...]
        dva = r * (dvh - jnp.mean(dvh, axis=-1, keepdims=True) - vh * jnp.mean(dvh * vh, axis=-1, keepdims=True))
        outs[0][:, 0:512] = (dout * mixed * _gelu_grad(u, tu)).astype(BF16)
        outs[0][:, 512:1024] = (dva * _gelu_grad(v, tv)).astype(BF16)

    return _rowwise("sgu_bwd", body, [(zs, 512, 0), (zs, 512, 1), (dcat, 512, 2)], [ln_g, ln_b, w, bias_full], [(1024, BF16)],
                    [((SGU_G, 128, 128), F32), ((1, SGU_DIM), F32), ((1, SGU_DIM), F32), ((SGU_G, 128), F32)], tr=256)


def _lower_bound(hg_lb):
    a0, a1 = hg_lb[0:1, :], hg_lb[1:2, :]
    m = jnp.maximum(a0, a1)
    e0, e1 = jnp.exp(a0 - m), jnp.exp(a1 - m)
    s0, s1 = e0 / (e0 + e1), e1 / (e0 + e1)
    return (s0 + s1) - s0, s0, s1


def _prefix_rows(x, reverse=False):
    n = x.shape[0]
    row = lax.broadcasted_iota(jnp.int32, x.shape, 0)
    s = 1
    while s < n:
        if reverse:
            x = x + jnp.where(row < n - s, pltpu.roll(x, n - s, 0), 0.0)
        else:
            x = x + jnp.where(row >= s, pltpu.roll(x, s, 0), 0.0)
        s *= 2
    return x


def _hg_gates(qr, fr, lb):
    C = qr.shape[0]
    sq = _sig(qr)
    qf = qr * sq
    sf = _sig(fr)
    gate = lb + (1.0 - lb) * sf
    kk = 1.0 - gate
    tri = _tril_mask(C)
    b = _prefix_rows(jnp.log(gate))
    bref = b[C // 2 - 1:C // 2, :]
    bl = b[C - 1:C, :]
    e_b = jnp.exp(b)
    e_q = jnp.exp(b - bref)
    e_k = jnp.exp(bref - b)
    e_lb = jnp.exp(bl - b)
    return dict(sq=sq, qf=qf, sf=sf, gate=gate, kk=kk, tri=tri, bl=bl, e_b=e_b, e_q=e_q, e_k=e_k, e_lb=e_lb)


def _hgrn_fwd(z1, hg_lb, gnorm):
    T = z1.shape[0]
    C = min(HG_CHUNK, T)
    nc = T // C
    ns = HG_CHUNKS_PER_STEP if nc % HG_CHUNKS_PER_STEP == 0 else 1
    R = ns * C

    def kern(q_ref, f_ref, i_ref, g_ref, lb_ref, gn_ref, o_ref, hg_ref, st_ref, s_scr):
        @pl.when(pl.program_id(0) == 0)
        def _():
            s_scr[...] = jnp.zeros(s_scr.shape, F32)

        lb_all, _, _ = _lower_bound(lb_ref[...])
        for sub in range(ns):
            rows = slice(sub * C, (sub + 1) * C)
            st_ref[sub] = s_scr[...]
            for h in range(HEADS):
                cols = slice(h * HEAD_W, (h + 1) * HEAD_W)
                t = _hg_gates(q_ref[rows, cols], f_ref[rows, cols], lb_all[:, cols])
                v_bf = i_ref[rows, cols].astype(BF16)
                st = s_scr[h]
                a = jnp.where(t["tri"], _dot((t["qf"] * t["e_q"]).astype(BF16), (t["kk"] * t["e_k"]).astype(BF16), NT), 0.0)
                o = _dot(a.astype(BF16), v_bf, NN) + _dot((t["qf"] * t["e_b"]).astype(BF16), st.astype(BF16), NT)
                s_scr[h] = st * jnp.exp(t["bl"]) + _dot(v_bf, (t["kk"] * t["e_lb"]).astype(BF16), TN_)
                o_ref[rows, cols] = o
                gr = g_ref[rows, cols]
                r = lax.rsqrt(jnp.mean(o * o, axis=-1, keepdims=True) + EPS)
                hg_ref[rows, cols] = (o * r * gn_ref[:, cols] * (gr * _sig(gr))).astype(BF16)

    seg = lambda k: pl.BlockSpec((R, D_MODEL), functools.partial(lambda n, k: (n, k), k=k))
    row = pl.BlockSpec((R, D_MODEL), lambda n: (n, 0))
    nbytes = 6 * _nbytes((R, D_MODEL), F32) + (2 + ns) * _nbytes((HEADS, 128, 128), F32)
    return pl.pallas_call(
        kern, name="hgrn_fwd", grid=(nc // ns,),
        in_specs=[seg(0), seg(1), seg(2), seg(3), pl.BlockSpec((2, D_MODEL), lambda n: (0, 0)),
                  pl.BlockSpec((1, D_MODEL), lambda n: (0, 0))],
        out_specs=[row, row, pl.BlockSpec((ns, HEADS, 128, 128), lambda n: (n, 0, 0, 0))],
        out_shape=[pltpu.HBM((T, D_MODEL), F32), pltpu.HBM((T, D_MODEL), BF16),
                   pltpu.HBM((nc, HEADS, 128, 128), F32)],
        scratch_shapes=[pltpu.VMEM((HEADS, 128, 128), F32)],
        compiler_params=pltpu.CompilerParams(dimension_semantics=("arbitrary",), vmem_limit_bytes=_vmem(nbytes)),
    )(*[_hbm(a) for a in (z1, z1, z1, z1, hg_lb, gnorm)])


def _hgrn_bwd(z1, o_pre, dhg, states, hg_lb, gnorm):
    T = z1.shape[0]
    C = min(HG_CHUNK, T)
    nc = T // C
    ns = HG_CHUNKS_PER_STEP if nc % HG_CHUNKS_PER_STEP == 0 else 1
    R, steps = ns * C, nc // ns

    def kern(q_ref, f_ref, i_ref, g_ref, o_ref, dhg_ref, st_ref, lb_ref, gn_ref, dz_ref, dlb_ref, dgn_ref, ds_scr, dlb_scr):
        n = pl.program_id(0)

        @pl.when(n == 0)
        def _():
            ds_scr[...] = jnp.zeros(ds_scr.shape, F32)
            dlb_scr[...] = jnp.zeros(dlb_scr.shape, F32)
            dgn_ref[...] = jnp.zeros(dgn_ref.shape, F32)

        lb_all, s0, s1 = _lower_bound(lb_ref[...])
        for sub in reversed(range(ns)):
            rows = slice(sub * C, (sub + 1) * C)
            for h in range(HEADS):
                cols = slice(h * HEAD_W, (h + 1) * HEAD_W)
                lb = lb_all[:, cols]
                qr, fr = q_ref[rows, cols], f_ref[rows, cols]
                t = _hg_gates(qr, fr, lb)
                tri = t["tri"]
                v_bf = i_ref[rows, cols].astype(BF16)
                st_bf = st_ref[sub, h].astype(BF16)
                dst = ds_scr[h]
                dst_bf = dst.astype(BF16)
                o = o_ref[rows, cols]
                gr = g_ref[rows, cols]
                sg = _sig(gr)
                sil = gr * sg
                gn = gn_ref[:, cols]
                r = lax.rsqrt(jnp.mean(o * o, axis=-1, keepdims=True) + EPS)
                on = o * r
                dh = dhg_ref[rows, cols].astype(F32)
                dgn_ref[:, cols] += jnp.sum(dh * on * sil, axis=0, keepdims=True)
                dg = dh * on * gn * (sg * (1.0 + gr * (1.0 - sg)))
                don = dh * gn * sil
                do_bf = (r * (don - on * jnp.mean(don * on, axis=-1, keepdims=True))).astype(BF16)
                qe = (t["qf"] * t["e_q"]).astype(BF16)
                ke = (t["kk"] * t["e_k"]).astype(BF16)
                qb = (t["qf"] * t["e_b"]).astype(BF16)
                kh_bf = (t["kk"] * t["e_lb"]).astype(BF16)
                a_bf = jnp.where(tri, _dot(qe, ke, NT), 0.0).astype(BF16)
                da_bf = jnp.where(tri, _dot(do_bf, v_bf, NT), 0.0).astype(BF16)
                dv = _dot(a_bf, do_bf, TN_) + _dot(kh_bf, dst_bf, NT)
                dqe = _dot(da_bf, ke, NN)
                dqb = _dot(do_bf, st_bf, NN)
                dke = _dot(da_bf, qe, TN_)
                dkh = _dot(v_bf, dst_bf, NN)
                dqf = dqe * t["e_q"] + dqb * t["e_b"]
                dkk = dke * t["e_k"] + dkh * t["e_lb"]
                kh_r = kh_bf.astype(F32)
                db = qe.astype(F32) * dqe - ke.astype(F32) * dke + qb.astype(F32) * dqb - kh_r * dkh
                e_bl = jnp.exp(t["bl"])
                dbl = jnp.sum(dkh * kh_r, axis=0, keepdims=True) + e_bl * jnp.sum(st_ref[sub, h] * dst, axis=0, keepdims=True)
                dlg = _prefix_rows(db, reverse=True) + dbl
                ds_scr[h] = dst * e_bl + _dot(do_bf, qb, TN_)
                dgate = dlg / t["gate"] - dkk
                sf = t["sf"]
                dlb_scr[:, cols] += jnp.sum(dgate * (1.0 - sf), axis=0, keepdims=True)
                df = dgate * (1.0 - lb) * sf * (1.0 - sf)
                dq = dqf * (t["sq"] * (1.0 + qr * (1.0 - t["sq"])))
                dz_ref[rows, cols] = dq.astype(BF16)
                dz_ref[rows, D_MODEL + h * HEAD_W:D_MODEL + (h + 1) * HEAD_W] = df.astype(BF16)
                dz_ref[rows, 2 * D_MODEL + h * HEAD_W:2 * D_MODEL + (h + 1) * HEAD_W] = dv.astype(BF16)
                dz_ref[rows, 3 * D_MODEL + h * HEAD_W:3 * D_MODEL + (h + 1) * HEAD_W] = dg.astype(BF16)

        @pl.when(n == steps - 1)
        def _():
            d = s0 * s1 * dlb_scr[...]
            dlb_ref[0:1, :] = -d
            dlb_ref[1:2, :] = d

    seg = lambda k: pl.BlockSpec((R, D_MODEL), functools.partial(lambda n, k: (steps - 1 - n, k), k=k))
    nbytes = 6 * _nbytes((R, D_MODEL), F32) + _nbytes((R, 4 * D_MODEL), BF16) + (2 + ns) * _nbytes((HEADS, 128, 128), F32)
    return pl.pallas_call(
        kern, name="hgrn_bwd", grid=(steps,),
        in_specs=[seg(0), seg(1), seg(2), seg(3), seg(0), seg(0),
                  pl.BlockSpec((ns, HEADS, 128, 128), lambda n: (steps - 1 - n, 0, 0, 0)),
                  pl.BlockSpec((2, D_MODEL), lambda n: (0, 0)), pl.BlockSpec((1, D_MODEL), lambda n: (0, 0))],
        out_specs=[pl.BlockSpec((R, 4 * D_MODEL), lambda n: (steps - 1 - n, 0)),
                   pl.BlockSpec((2, D_MODEL), lambda n: (0, 0)), pl.BlockSpec((1, D_MODEL), lambda n: (0, 0))],
        out_shape=[pltpu.HBM((T, 4 * D_MODEL), BF16), pltpu.HBM((2, D_MODEL), F32),
                   pltpu.HBM((1, D_MODEL), F32)],
        scratch_shapes=[pltpu.VMEM((HEADS, 128, 128), F32), pltpu.VMEM((1, D_MODEL), F32)],
        compiler_params=pltpu.CompilerParams(dimension_semantics=("arbitrary",), vmem_limit_bytes=_vmem(nbytes)),
    )(*[_hbm(a) for a in (z1, z1, z1, z1, o_pre, dhg, states, hg_lb, gnorm)])


def _prep_weights(gw):
    w_in = gw["w_in_e"].reshape(1568, D_MODEL)
    w_qb = gw["w_qb"].transpose(1, 0, 2).reshape(MLA_LORA, HEADS, 96)
    wq = jnp.pad(w_qb, ((0, 0), (0, 0), (0, 32))).reshape(MLA_LORA, HEADS * HEAD_W)
    kvb = gw["w_kvb"].transpose(1, 0, 2).reshape(MLA_LORA, HEADS, 128)
    wk = jnp.pad(kvb[:, :, :64], ((0, 0), (0, 0), (0, 64))).reshape(MLA_LORA, HEADS * HEAD_W)
    wv = jnp.pad(kvb[:, :, 64:], ((0, 0), (0, 0), (0, 64))).reshape(MLA_LORA, HEADS * HEAD_W)
    w_out_e = gw["w_out_e"].reshape(D_MODEL, D_MODEL)
    woa = jnp.pad(w_out_e[:512].reshape(HEADS, 64, D_MODEL), ((0, 0), (0, 64), (0, 0))).reshape(HEADS * HEAD_W, D_MODEL)
    return dict(w_in=w_in, wq=wq, wk=wk, wv=wv, woa=woa, wob=w_out_e[512:])


def _unprep_grads(g):
    d_in_e = g["w_in"].reshape(N_DEV, 1568 // N_DEV, D_MODEL)
    d_qb = g["wq"].reshape(MLA_LORA, HEADS, HEAD_W)[:, :, :96].reshape(MLA_LORA, HEADS * 96)
    dk = g["wk"].reshape(MLA_LORA, HEADS, HEAD_W)[:, :, :64]
    dv = g["wv"].reshape(MLA_LORA, HEADS, HEAD_W)[:, :, :64]
    d_kvb = jnp.concatenate([dk, dv], axis=2).reshape(MLA_LORA, HEADS * 128)
    d_oa = g["woa"].reshape(HEADS, HEAD_W, D_MODEL)[:, :64].reshape(HEADS * 64, D_MODEL)
    dev_major = lambda a: a.reshape(a.shape[0], N_DEV, a.shape[1] // N_DEV).transpose(1, 0, 2)
    return dict(w_in_e=d_in_e, w_qb=dev_major(d_qb), w_kvb=dev_major(d_kvb),
                w_out_e=jnp.concatenate([d_oa, g["wob"]], axis=0).reshape(N_DEV, D_MODEL // N_DEV, D_MODEL))


def _local_step(x, positions, target, gw, sp, ex):
    w = _prep_weights(gw)
    T = x.shape[0]
    tm = min(TM, T)
    nt = T // tm
    half = MLA_ROPE // 2
    inv_freq = ROPE_BASE ** (-jnp.arange(half, dtype=F32) / half)
    tabs = _rope_tables(positions.reshape(T, 1), inv_freq)
    bias_full = jnp.repeat(sp["sgu_b"][0].T, 128, axis=1)
    sgu_w = sp["sgu_w"]
    gq, gkv = sp["mla_gq"], sp["mla_gkv"]
    ln1_g, ln1_b, ln2_g, ln2_b = sp["ln1_g"], sp["ln1_b"], sp["ln2_g"], sp["ln2_b"]
    zm, zs, cqn, ckvn, kr_rot = _mla_in(x, w["w_in"], tabs, gq, gkv, deps=[ex.first_token])
    q, k, v = _mla_qkv(cqn, ckvn, kr_rot, tabs, w["wq"], w["wk"], w["wv"])
    o_att, lse = _attn_fwd(q, k, v)
    b_out = _sgu_fwd(zs, sp["sgu_ln_g"], sp["sgu_ln_b"], sgu_w, bias_full)
    token = ex.weights_forward(after=[o_att, b_out])
    y1, h1, h1_bf = _proj_ln("l0_out_ln1", [o_att, b_out], [w["woa"], w["wob"]], x, ln1_g, ln1_b, 0, deps=[token])
    big = ex.weights_ready(after=[y1])
    w_ff1, w_in_o, w_out_o = big["w_ff1"], big["w_in_o"], big["w_out_o"].reshape(D_MODEL, D_MODEL)
    w_ff2 = [a.reshape(D_FF, D_MODEL) for a in big["w_ff2"]]
    a0, act0 = _mlp_up("l0", h1_bf, w_ff1[0])
    y2, h2, h2_bf = _proj_ln("l0_ff2_ln2", [act0], [w_ff2[0]], h1, ln2_g, ln2_b, 0)

    z1 = _tiled("l1_in", (1, nt), [_rb(h2_bf, tm), _res(w_in_o)], [_out(T, 4 * D_MODEL, F32, tm, 4 * D_MODEL)],
                _mmc_blocks(N_DEV, NN, lambda w, d: w[d]), direct=True, ring=True)
    o_pre, hg, states = _hgrn_fwd(z1, sp["hg_lb"], sp["hg_gnorm"])
    y3, h3, h3_bf = _proj_ln("l1_out_ln1", [hg], [w_out_o], h2, ln1_g, ln1_b, 1)
    a1, act1 = _mlp_up("l1", h3_bf, w_ff1[1])

    gs, g0 = {}, {}
    dy4, dy4_bf, sq_err, gs["ln2_g1"], gs["ln2_b1"] = _proj_ln_loss("l1_ff2_loss", act1, w_ff2[1], h3, ln2_g, ln2_b, 1, target)
    gs["sq_err"] = sq_err
    da1, dw1_1, dw2_1 = _mlp_bwd_w("l1", h3_bf, a1, act1, dy4_bf, big["w_ff2"][1])
    dy3, dy3_bf, dhg, gs["ln1_g1"], gs["ln1_b1"] = _dh_ln_back("l1_dh_ln1", da1, w_ff1[1], dy4, y3, ln1_g, 1, proj=[w_out_o])
    d_out_o = _tiled("l1_dwout", (2, D_MODEL // TM), [_tl(hg, TM), _cw(dy3_bf, TN)],
                     [_out(D_MODEL, D_MODEL, F32, TM, TN), _out(D_MODEL, D_MODEL, BF16, TM, TN)], _mmc(TN_, epilogue=_twice))
    d_out_o = [a.reshape(N_DEV, D_MODEL // N_DEV, D_MODEL) for a in d_out_o]
    dz1, gs["hg_lb"], gs["hg_gnorm"] = _hgrn_bwd(z1, o_pre, dhg, states, sp["hg_lb"], sp["hg_gnorm"])
    d_in_o = _tiled("l1_dwin", (N_DEV, 1), [_res(h2_bf), _cw(dz1, TN)],
                    [_out_dev(D_MODEL, TN, D_MODEL), _out_dev(D_MODEL, TN, D_MODEL, BF16)], _mmc(TN_, epilogue=_twice))
    token = ex.direct_start("l1", [dw1_1, dw2_1, d_in_o, d_out_o])

    dy2, dy2_bf, gs["ln2_g0"], gs["ln2_b0"] = _dh_ln_back("l1_dh_ln2", dz1, w_in_o, dy3, y2, ln2_g, 0, deps=[token])
    da0, dw1_0, dw2_0 = _mlp_bwd_w("l0", h1_bf, a0, act0, dy2_bf, big["w_ff2"][0])
    token = ex.direct_start("l0m", [dw1_0, dw2_0])
    dy1, dy1_bf, dcat, gs["ln1_g0"], gs["ln1_b0"] = _dh_ln_back("l0_dh_ln1", da0, w_ff1[0], dy2, y1, ln1_g, 0,
                                                                 proj=[w["woa"], w["wob"]], deps=[token])
    g0["woa"], g0["wob"] = _out_weight_grads(o_att, b_out, dy1_bf)
    dzs, gs["sgu_w"], gs["sgu_ln_g"], gs["sgu_ln_b"], gs["sgu_b"] = _sgu_bwd(zs, dcat, sp["sgu_ln_g"], sp["sgu_ln_b"], sgu_w, bias_full)
    dq, dk, dv = _attn_bwd(q, k, v, o_att, lse, dcat)
    dzm, g0["wq"], g0["wk"], g0["wv"], gs["mla_gq"], gs["mla_gkv"] = _mla_back(zm, cqn, ckvn, tabs, gq, gkv, w["wq"], w["wk"], w["wv"],
                                                                                 dq, dk, dv)
    token = ex.small_start(gs)
    dx, g0["w_in"] = _in_back(x, dzm, dzs, dy1, w["w_in"], deps=[token])

    return sq_err, dx, _unprep_grads(g0), gs


def _me():
    return lax.axis_index("x"), lax.axis_index("y"), lax.axis_index("c")


ANY_SPEC = pl.BlockSpec(memory_space=pl.ANY)
HBM_SPEC = pl.BlockSpec(memory_space=pltpu.HBM)
SEM_SPEC = pl.BlockSpec(memory_space=pltpu.SEMAPHORE)
EFFECT = pltpu.SideEffectType.DATAFLOW_SIDE_EFFECTING


def _split_start(name, srcs, lands, n_sems, make_copies, after=()):
    n, m, k = len(srcs), len(lands), len(after)

    def body(*refs):
        for cp in make_copies(refs[:n], refs[n:n + m], refs[n + m + k], refs[n + m + k + 1]):
            cp.start()
        refs[-1][...] = jnp.zeros(refs[-1].shape, F32)

    out_shape = (pltpu.SemaphoreType.DMA((n_sems,)), pltpu.SemaphoreType.DMA((n_sems,)),
                 *[pltpu.HBM(a.shape, a.dtype) for a in (*srcs, *lands)], jax.ShapeDtypeStruct((8, 128), F32))
    res = pl.pallas_call(
        body, name=name, out_shape=out_shape, in_specs=[HBM_SPEC] * (n + m) + [ANY_SPEC] * k,
        out_specs=(SEM_SPEC, SEM_SPEC, *[HBM_SPEC] * (n + m), pl.BlockSpec(memory_space=pltpu.VMEM)),
        input_output_aliases={i: 2 + i for i in range(n + m)},
        compiler_params=pltpu.CompilerParams(has_side_effects=EFFECT),
    )(*[_hbm(a) for a in (*srcs, *lands)], *after)
    return res[0], res[1], list(res[2:2 + n]), list(res[2 + n:2 + n + m]), res[-1]


def _split_wait(name, send_sems, recv_sems, srcs, lands, after, make_copies):
    n, m = len(srcs), len(lands)

    def body(*refs):
        for cp in make_copies(refs[:n], refs[n:n + m], refs[n + m], refs[n + m + 1]):
            cp.wait_send()
            cp.wait_recv()

    res = pl.pallas_call(
        body, name=name, out_shape=tuple(pltpu.HBM(a.shape, a.dtype) for a in (*srcs, *lands)),
        in_specs=[HBM_SPEC] * (n + m) + [SEM_SPEC, SEM_SPEC] + [ANY_SPEC] * len(after), out_specs=tuple([HBM_SPEC] * (n + m)),
        input_output_aliases={i: i for i in range(n + m)},
        compiler_params=pltpu.CompilerParams(has_side_effects=EFFECT),
    )(*srcs, *lands, send_sems, recv_sems, *after)
    return list(res[:n]), list(res[n:])


def _place_own(shards, dev):
    n = len(shards)

    def kern(dev_ref, *refs):
        for x_ref, o_ref in zip(refs[:n], refs[n:]):
            o_ref[...] = x_ref[...].astype(o_ref.dtype)

    blocks = [(None, *a.shape[1:]) for a, _, _ in shards]
    nbytes = sum(_nbytes(b, a.dtype) + _nbytes(b, dt) for b, (a, _, dt) in zip(blocks, shards))
    return pl.pallas_call(
        kern, name="weights_place_own", out_shape=[pltpu.HBM((N_DEV, *a.shape[1:]), dt) for a, _, dt in shards],
        grid_spec=pltpu.PrefetchScalarGridSpec(
            num_scalar_prefetch=1, grid=(1,),
            in_specs=[pl.BlockSpec(b, functools.partial(lambda i, dev, l: (l, 0, 0), l=l)) for b, (_, l, _) in zip(blocks, shards)],
            out_specs=[pl.BlockSpec(b, lambda i, dev: (dev[0], 0, 0)) for b in blocks]),
        compiler_params=pltpu.CompilerParams(dimension_semantics=("arbitrary",), vmem_limit_bytes=_vmem(nbytes)),
    )(dev, *[_hbm(a) for a, _, _ in shards])


def _ag_first_copies(src_refs, out_refs, send_sems, recv_sems):
    x, y, c = _me()
    targets = [(x, y, 1 - c), (1 - x, y, c), (x, 1 - y, c), (1 - x, 1 - y, c)]
    return [pltpu.make_async_remote_copy(
        src_ref=out_refs[op].at[4 * x + 2 * y + c], dst_ref=out_refs[op].at[4 * x + 2 * y + c], send_sem=send_sems.at[4 * op + k],
        recv_sem=recv_sems.at[4 * op + k], device_id=to, device_id_type=MESH)
        for op in range(len(out_refs)) for k, to in enumerate(targets)]


def _ag_second_copies(src_refs, out_refs, send_sems, recv_sems):
    x, y, c = _me()
    chips = [(1 - x, y), (x, 1 - y), (1 - x, 1 - y)]
    return [pltpu.make_async_remote_copy(
        src_ref=out_refs[op].at[4 * cx + 2 * cy + c], dst_ref=out_refs[op].at[4 * cx + 2 * cy + c],
        send_sem=send_sems.at[3 * op + j], recv_sem=recv_sems.at[3 * op + j], device_id=(x, y, 1 - c), device_id_type=MESH)
        for op in range(len(out_refs)) for j, (cx, cy) in enumerate(chips)]


def _rs_sibling_copies(g_refs, out_refs, send_sems, recv_sems):
    x, y, c = _me()
    return [pltpu.make_async_remote_copy(
        src_ref=g_refs[op].at[k, 1 - c], dst_ref=out_refs[op].at[k], send_sem=send_sems.at[4 * op + k],
        recv_sem=recv_sems.at[4 * op + k], device_id=(x, y, 1 - c), device_id_type=MESH)
        for op in range(len(g_refs)) for k in range(4)]


def _rs_direct_copies(g_refs, land_refs, send_sems, recv_sems):
    x, y, c = _me()
    n = len(g_refs) // 2
    chips = [(1 - x, y), (x, 1 - y), (1 - x, 1 - y)]
    copies = []
    for op in range(n):
        g32, g16, from_sib, from_others = g_refs[op], g_refs[n + op], land_refs[op], land_refs[n + op]
        copies.append(pltpu.make_async_remote_copy(
            src_ref=g32.at[2 * x + y, 1 - c], dst_ref=from_sib, send_sem=send_sems.at[7 * op], recv_sem=recv_sems.at[7 * op],
            device_id=(x, y, 1 - c), device_id_type=MESH))
        for j, (cx, cy) in enumerate(chips):
            for s, cc in enumerate((c, 1 - c)):
                copies.append(pltpu.make_async_remote_copy(
                    src_ref=g16.at[2 * cx + cy, cc], dst_ref=from_others.at[2 * j + s], send_sem=send_sems.at[7 * op + 1 + 2 * j + s],
                    recv_sem=recv_sems.at[7 * op + 1 + 2 * j + s], device_id=(cx, cy, cc), device_id_type=MESH))
    return copies


def _rs_chip_copies(p_refs, out_refs, send_sems, recv_sems):
    x, y, c = _me()
    chips = [(1 - x, y), (x, 1 - y), (1 - x, 1 - y)]
    return [pltpu.make_async_remote_copy(
        src_ref=p_refs[op].at[2 * cx + cy], dst_ref=out_refs[op].at[j], send_sem=send_sems.at[3 * op + j],
        recv_sem=recv_sems.at[3 * op + j], device_id=(cx, cy, c), device_id_type=MESH)
        for op in range(len(p_refs)) for j, (cx, cy) in enumerate(chips)]


def _all_gather(placed):
    n = len(placed)

    def kern(*refs):
        in_refs, out_refs, (send_sems, recv_sems) = refs[:n], refs[n:2 * n], refs[2 * n:]
        x, y, c = _me()
        me, sibling = (x, y, c), (x, y, 1 - c)
        chips = [(1 - x, y), (x, 1 - y), (1 - x, 1 - y)]

        def copy(op, k, block, to, own=False):
            idx = 4 * block[0] + 2 * block[1] + block[2]
            return pltpu.make_async_remote_copy(
                src_ref=(in_refs if own else out_refs)[op].at[idx], dst_ref=out_refs[op].at[idx], send_sem=send_sems.at[7 * op + k],
                recv_sem=recv_sems.at[7 * op + k], device_id=to, device_id_type=MESH)

        first = []
        for op in range(n):
            first.append(copy(op, 0, me, sibling, own=True))
            first += [copy(op, 1 + j, me, (*chip, c), own=True) for j, chip in enumerate(chips)]
        for cp in first:
            cp.start()
        passed = []
        for j, chip in enumerate(chips):
            for op in range(n):
                copy(op, 1 + j, (*chip, c), me).wait_recv()
                passed.append(copy(op, 4 + j, (*chip, c), sibling))
                passed[-1].start()
        for op in range(n):
            copy(op, 0, sibling, me).wait_recv()
            for j, chip in enumerate(chips):
                copy(op, 4 + j, (*chip, 1 - c), me).wait_recv()
        for cp in first + passed:
            cp.wait_send()

    return pl.pallas_call(
        kern, name="weights_all_gather", out_shape=[pltpu.HBM(g.shape, g.dtype) for g in placed],
        in_specs=[ANY_SPEC] * n, out_specs=[ANY_SPEC] * n, input_output_aliases={i: i for i in range(n)},
        scratch_shapes=[pltpu.SemaphoreType.DMA((7 * n,)), pltpu.SemaphoreType.DMA((7 * n,))],
    )(*[_hbm(a) for a in placed])


def _row_tile(r, w, n_blocks):
    tr = r
    while tr > 8 and 2 * n_blocks * tr * w * 4 > 24 * 2**20:
        tr //= 2
    return tr


def _chip_sum(name, g, from_sibling, core):
    _, _, R, W = g.shape
    tr = _row_tile(R, W, 3)

    def kern(core_ref, g_ref, s_ref, o_ref):
        o_ref[...] = (g_ref[...] + s_ref[...]).astype(BF16)

    return pl.pallas_call(
        kern, name=name, out_shape=pltpu.HBM((4, R, W), BF16),
        grid_spec=pltpu.PrefetchScalarGridSpec(
            num_scalar_prefetch=1, grid=(4, R // tr),
            in_specs=[pl.BlockSpec((None, None, tr, W), lambda k, i, core: (k, core[0], i, 0)),
                      pl.BlockSpec((None, tr, W), lambda k, i, core: (k, i, 0))],
            out_specs=pl.BlockSpec((None, tr, W), lambda k, i, core: (k, i, 0))),
        compiler_params=pltpu.CompilerParams(dimension_semantics=("parallel", "parallel"), vmem_limit_bytes=_vmem(3 * tr * W * 4)),
    )(core, _hbm(g), _hbm(from_sibling))


def _adamw(w, g, m, v):
    m = ADAM_B1 * m + (1.0 - ADAM_B1) * g
    v = ADAM_B2 * v + (1.0 - ADAM_B2) * (g * g)
    m_hat = m / (1.0 - ADAM_B1 ** ADAM_STEP)
    v_hat = v / (1.0 - ADAM_B2 ** ADAM_STEP)
    return -ADAM_LR * (m_hat / (jnp.sqrt(v_hat) + ADAM_EPS) + ADAM_WD * w), m, v


def _finish_sharded(name, layers, w, m, v, where, deps=()):
    nl, R, W = w.shape
    n_other = layers[0][2].shape[0]
    tr = _row_tile(R, W, (8 + n_other) * nl)
    deps = _deps(deps)

    def kern(where_ref, *refs):
        w_ref, m_ref, v_ref = refs[3 * nl:3 * nl + 3]
        go_ref, d_ref, mo_ref, vo_ref = refs[3 * nl + 3 + len(deps):]
        for l in range(nl):
            g_ref, s_ref, c_ref = refs[3 * l:3 * l + 3]
            grad = g_ref[...] + s_ref[...]
            for j in range(n_other):
                grad = grad + c_ref[j].astype(F32)
            go_ref[l] = grad
            d_ref[l], mo_ref[l], vo_ref[l] = _adamw(w_ref[l], grad, m_ref[l], v_ref[l])

    row = pl.BlockSpec((nl, tr, W), lambda i, wh: (0, i, 0))
    in_specs, args = [], []
    for g, s, c in layers:
        sib = (pl.BlockSpec((None, tr, W), lambda i, wh: (wh[0], i, 0)) if s.ndim == 3 else pl.BlockSpec((tr, W), lambda i, wh: (i, 0)))
        in_specs += [pl.BlockSpec((None, None, tr, W), lambda i, wh: (wh[0], wh[1], i, 0)), sib,
                     pl.BlockSpec((n_other, tr, W), lambda i, wh: (0, i, 0))]
        args += [g, s, c]
    return pl.pallas_call(
        kern, name=name, out_shape=[pltpu.HBM((nl, R, W), F32)] * 4,
        grid_spec=pltpu.PrefetchScalarGridSpec(num_scalar_prefetch=1, grid=(R // tr,),
                                               in_specs=in_specs + [row, row, row] + [ANY_SPEC] * len(deps),
                                               out_specs=[row, row, row, row]),
        compiler_params=pltpu.CompilerParams(dimension_semantics=("parallel",),
                                             vmem_limit_bytes=_vmem(nl * (8 + n_other) * tr * W * 4)),
    )(where, *[_hbm(a) for a in (*args, w, m, v)], *deps)


SMALL_PLACE = (("mla_gq", 0, 0, 1, 256), ("mla_gkv", 0, 256, 1, 256), ("sgu_ln_g", 0, 512, 1, 512), ("sgu_ln_b", 1, 0, 1, 512),
               ("hg_lb", 2, 0, 2, 1024), ("ln1_g", 4, 0, 2, 1024), ("ln1_b", 6, 0, 2, 1024), ("sgu_b", 8, 0, 4, 128),
               ("ln2_g", 12, 0, 2, 1024), ("ln2_b", 14, 0, 2, 1024), ("hg_gnorm", 16, 0, 1, 1024))
SMALL_BUF_ROWS = 24
LOSS_ROW = 17


def _small_pack(gs, dev):
    pieces = [(gs["mla_gq"], 0, 0), (gs["mla_gkv"], 0, 256), (gs["sgu_ln_g"], 0, 512), (gs["sgu_ln_b"], 1, 0), (gs["hg_lb"], 2, 0),
              (gs["ln1_g0"], 4, 0), (gs["ln1_g1"], 5, 0), (gs["ln1_b0"], 6, 0), (gs["ln1_b1"], 7, 0), (gs["sgu_b"], 8, 0),
              (gs["ln2_g0"], 12, 0), (gs["ln2_g1"], 13, 0), (gs["ln2_b0"], 14, 0), (gs["ln2_b1"], 15, 0), (gs["hg_gnorm"], 16, 0),
              (gs["sq_err"], LOSS_ROW, 0)]
    n_p = len(pieces)

    def kern(dev_ref, *refs):
        a_ref, b_ref = refs[n_p + 1], refs[n_p + 2]
        a_ref[...] = jnp.zeros(a_ref.shape, F32)
        for ref, (_, r, l0) in zip(refs[:n_p], pieces):
            a_ref[r:r + ref.shape[0], l0:l0 + ref.shape[1]] = ref[...]
        b_ref[...] = refs[n_p][...]

    whole = lambda a: pl.BlockSpec(a.shape, functools.partial(lambda i, dev, nd: (0,) * nd, nd=a.ndim))
    return pl.pallas_call(
        kern, name="small_grads_pack",
        out_shape=[pltpu.HBM((N_DEV, SMALL_BUF_ROWS, D_MODEL), F32), pltpu.HBM((N_DEV, SGU_G, 128, 128), F32)],
        grid_spec=pltpu.PrefetchScalarGridSpec(
            num_scalar_prefetch=1, grid=(1,), in_specs=[whole(p[0]) for p in pieces] + [whole(gs["sgu_w"])],
            out_specs=[pl.BlockSpec((None, SMALL_BUF_ROWS, D_MODEL), lambda i, dev: (dev[0], 0, 0)),
                       pl.BlockSpec((None, SGU_G, 128, 128), lambda i, dev: (dev[0], 0, 0, 0))]),
    )(dev, *[p[0] for p in pieces], gs["sgu_w"])


def _small_copies(src_refs, land_refs, send_sems, recv_sems):
    px, py, pc = _me()
    me = 4 * px + 2 * py + pc
    return [pltpu.make_async_remote_copy(
        src_ref=land_refs[k].at[me], dst_ref=land_refs[k].at[me], send_sem=send_sems.at[2 * (r - 1) + k],
        recv_sem=recv_sems.at[2 * (r - 1) + k], device_id=(px ^ (r >> 2), py ^ ((r >> 1) & 1), pc ^ (r & 1)), device_id_type=MESH)
        for r in range(1, N_DEV) for k in range(2)]


def _small_adamw(slots_a, slots_b, given):
    names = [p[0] for p in SMALL_PLACE] + ["sgu_w"]
    n_names = len(names)
    wmv = [given[pre + name] for name in names for pre in ("", "m_", "v_")]
    vmem = pl.BlockSpec(memory_space=pltpu.VMEM)

    def kern(*refs):
        sum_a, sum_b = refs[0][0], refs[1][0]
        for d in range(1, N_DEV):
            sum_a, sum_b = sum_a + refs[0][d], sum_b + refs[1][d]
        wmv_refs, out_refs = refs[2:2 + 3 * n_names], refs[2 + 3 * n_names:]
        px, py, pc = _me()
        me = 4 * px + 2 * py + pc

        def own_block(full):
            acc = full[:, 0:128]
            for b in range(1, N_DEV):
                acc = jnp.where(me == b, full[:, b * 128:(b + 1) * 128], acc)
            return acc

        for idx, name in enumerate(names):
            w_ref, m_ref, v_ref = wmv_refs[3 * idx:3 * idx + 3]
            if name == "sgu_w":
                grad = sum_b[None]
            else:
                _, r, l0, nr, nl = SMALL_PLACE[idx]
                grad = sum_a[r:r + nr, l0:l0 + nl]
                if name == "hg_gnorm":
                    grad = own_block(grad)
                if name == "sgu_b":
                    grad = grad[None]
            res = (grad, *_adamw(w_ref[...], grad, m_ref[...], v_ref[...]))
            for o_ref, val in zip(out_refs[4 * idx:4 * idx + 4], res):
                o_ref[...] = val
        out_refs[4 * n_names][...] = (0.5 / D_MODEL) * jnp.sum(sum_a[LOSS_ROW:LOSS_ROW + 1, :], axis=1, keepdims=True)

    out_shape = [jax.ShapeDtypeStruct(given[name].shape, F32) for name in names for _ in range(4)]
    out_shape.append(jax.ShapeDtypeStruct((1, 1), F32))
    res = pl.pallas_call(
        kern, name="small_adamw", out_shape=out_shape, in_specs=[vmem] * (2 + len(wmv)), out_specs=[vmem] * len(out_shape),
    )(slots_a, slots_b, *wmv)
    out = {name: res[4 * idx:4 * idx + 4] for idx, name in enumerate(names)}
    out["loss"] = res[-1].reshape(())
    return out


class _Exchange:
    def __init__(self, given):
        self.given = given
        px, py, pc = _me()
        self.core = pc.reshape(1).astype(jnp.int32)
        self.dev = (4 * px + 2 * py + pc).reshape(1).astype(jnp.int32)
        self.where = jnp.stack([2 * px + py, pc]).astype(jnp.int32)
        self.state, self.layers = {}, {}

    def start_weights(self, lands, after):
        self.weights = _split_start("weights_first_start", [], lands, 4 * len(lands), _ag_first_copies, after=after)
        self.first_token = self.weights[4]

    def weights_forward(self, after):
        send_sems, recv_sems, shards, lands, _ = self.weights
        _, lands = _split_wait("weights_first_wait", send_sems, recv_sems, shards, lands, after, _ag_first_copies)
        self.weights = _split_start("weights_second_start", [], lands, 3 * len(lands), _ag_second_copies)
        return self.weights[4]

    def weights_ready(self, after):
        send_sems, recv_sems, shards, lands, _ = self.weights
        _, got = _split_wait("weights_second_wait", send_sems, recv_sems, shards, lands, after, _ag_second_copies)
        return dict(w_in_o=got[0], w_out_o=got[1], w_ff1=[got[2], got[3]], w_ff2=[got[4], got[5]])

    def small_start(self, gs):
        self.small = _split_start("small_grads_start", [], _small_pack(gs, self.dev), 14, _small_copies)
        return self.small[4]

    def small_finish(self, after):
        send_sems, recv_sems, _, lands, _ = self.small
        _, lands = _split_wait("small_grads_wait", send_sems, recv_sems, [], lands, after, _small_copies)
        return _small_adamw(lands[0], lands[1], self.given)

    def direct_start(self, tag, grads):
        f32 = [g[0].reshape(4, 2, *g[0].shape[1:]) for g in grads]
        bf16 = [g[1].reshape(4, 2, *g[1].shape[1:]) for g in grads]
        lands = [lax.empty(b.shape[2:], F32) for b in f32] + [lax.empty((6, *b.shape[2:]), BF16) for b in f32]
        self.state[tag] = _split_start(f"grads_{tag}_start", f32 + bf16, lands, 7 * len(grads), _rs_direct_copies)
        return self.state[tag][4]

    def direct_end(self, tag, after):
        send_sems, recv_sems, srcs, lands, _ = self.state[tag]
        srcs, lands = _split_wait(f"grads_{tag}_wait", send_sems, recv_sems, srcs, lands, after, _rs_direct_copies)
        n = len(lands) // 2
        self.layers[tag] = list(zip(srcs[:n], lands[:n], lands[n:]))

    def grads_start(self, tag, grads):
        blocks = [g.reshape(4, 2, *g.shape[1:]) for g in grads]
        lands = [lax.empty((4, *b.shape[2:]), F32) for b in blocks]
        self.state[tag] = _split_start(f"grads_{tag}_sibling_start", blocks, lands, 4 * len(blocks), _rs_sibling_copies)
        return self.state[tag][4]

    def grads_middle(self, tag, after):
        send_sems, recv_sems, blocks, lands, _ = self.state[tag]
        blocks, from_sibling = _split_wait(f"grads_{tag}_sibling_wait", send_sems, recv_sems, blocks, lands, [after], _rs_sibling_copies)
        sums = [_chip_sum(f"grads_{tag}_chip_sum_{k}", b, s, self.core) for k, (b, s) in enumerate(zip(blocks, from_sibling))]
        lands = [lax.empty((3, *p.shape[1:]), BF16) for p in sums]
        self.state[tag] = (blocks, from_sibling, _split_start(f"grads_{tag}_chips_start", sums, lands, 3 * len(sums), _rs_chip_copies))
        return self.state[tag][2][4]

    def grads_end(self, tag, after):
        blocks, from_sibling, (send_sems, recv_sems, sums, lands, _) = self.state[tag]
        after = list(after) if isinstance(after, (list, tuple)) else [after]
        _, from_chips = _split_wait(f"grads_{tag}_chips_wait", send_sems, recv_sems, sums, lands, after, _rs_chip_copies)
        self.layers[tag] = list(zip(blocks, from_sibling, from_chips))


def kernel(x, positions, w_in_e, mla_gq, mla_gkv, w_qb, w_kvb, sgu_ln_g, sgu_ln_b, sgu_w, sgu_b, w_out_e, w_in_o, hg_lb, hg_gnorm, w_out_o, ln1_g, ln1_b, w_ff1, w_ff2, ln2_g, ln2_b, loss_target, m_w_in_e, m_mla_gq, m_mla_gkv, m_w_qb, m_w_kvb, m_sgu_ln_g, m_sgu_ln_b, m_sgu_w, m_sgu_b, m_w_out_e, m_w_in_o, m_hg_lb, m_hg_gnorm, m_w_out_o, m_ln1_g, m_ln1_b, m_w_ff1, m_w_ff2, m_ln2_g, m_ln2_b, v_w_in_e, v_mla_gq, v_mla_gkv, v_w_qb, v_w_kvb, v_sgu_ln_g, v_sgu_ln_b, v_sgu_w, v_sgu_b, v_w_out_e, v_w_in_o, v_hg_lb, v_hg_gnorm, v_w_out_o, v_ln1_g, v_ln1_b, v_w_ff1, v_w_ff2, v_ln2_g, v_ln2_b):
    given = dict(locals())
    for n in ("w_in_e", "m_w_in_e", "v_w_in_e"):
        given[n] = jnp.swapaxes(given[n], 1, 2)
    ex = _Exchange(given)

    names = ["w_in_e", "w_qb", "w_kvb", "w_out_e"]
    placed = _place_own([(given[n], 0, BF16) for n in names] + [(hg_gnorm.reshape(1, 1, D_MODEL // N_DEV), 0, F32)]
                        + [(w_in_o, 0, BF16), (w_out_o, 0, BF16), (w_ff1, 0, BF16), (w_ff1, 1, BF16), (w_ff2, 0, BF16), (w_ff2, 1, BF16)],
                        ex.dev)
    got = _all_gather(placed[:5])
    ex.start_weights(placed[5:], after=[got[0]])
    gw = dict(zip(names, got[:4]))
    small_names = ["mla_gq", "mla_gkv", "sgu_ln_g", "sgu_ln_b", "sgu_w", "sgu_b", "hg_lb", "ln1_g", "ln1_b", "ln2_g", "ln2_b"]
    sp = {n: given[n] for n in small_names}
    sp["hg_gnorm"] = got[4].reshape(1, D_MODEL)

    _, dx, grads, gs = _local_step(x[0], positions[0], loss_target[0], gw, sp, ex)

    def finish(n, layers, deps=()):
        return _finish_sharded(f"finish_{n}", layers, given[n], given["m_" + n], given["v_" + n], ex.where, deps=deps)

    ex.direct_end("l1", after=[dx])
    ex.direct_end("l0m", after=[dx])
    l1, l0m = ex.layers["l1"], ex.layers["l0m"]
    results = {}
    token = ex.grads_start("l0s", [grads[n] for n in names])
    results["w_ff1"] = finish("w_ff1", [l0m[0], l1[0]], deps=[token])
    token = ex.grads_middle("l0s", after=results["w_ff1"][0])
    results["w_ff2"] = finish("w_ff2", [l0m[1], l1[1]], deps=[token])
    results["w_in_o"] = finish("w_in_o", [l1[2]], deps=[token])
    results["w_out_o"] = finish("w_out_o", [l1[3]], deps=[token])
    results.update(ex.small_finish(after=[results["w_in_o"][0]]))
    ex.grads_end("l0s", after=[results[n][0] for n in ("mla_gq", "w_ff2", "w_in_o", "w_out_o")])
    for n, layer in zip(names, ex.layers["l0s"]):
        results[n] = finish(n, [layer])
    results["w_in_e"] = [jnp.swapaxes(a, 1, 2) for a in results["w_in_e"]]

    order = ["w_in_e", "mla_gq", "mla_gkv", "w_qb", "w_kvb", "sgu_ln_g", "sgu_ln_b", "sgu_w", "sgu_b", "w_out_e", "w_in_o",
             "hg_lb", "hg_gnorm", "w_out_o", "ln1_g", "ln1_b", "w_ff1", "w_ff2", "ln2_g", "ln2_b"]
    return (results["loss"], dx[None], *[results[name][kind] for kind in range(4) for name in order])
```
